```python
import jax, jax.numpy as jnp
from jax import lax
import numpy as np

D_MODEL = 2048
BATCH = 8
SEQ = 8192
DEPTH = 1

ATTN_WIDTH = D_MODEL // 2
ATTN_HEAD_DIM = 128
ATTN_HEADS = ATTN_WIDTH // ATTN_HEAD_DIM
DILATED_PATTERNS = ((128, 1), (512, 4), (2048, 16))
ROPE_THETA = 500000.0
ROPE_DIM = ATTN_HEAD_DIM // 4

GLA_VALUE_WIDTH = D_MODEL - ATTN_WIDTH
GLA_KEY_WIDTH = GLA_VALUE_WIDTH // 2
GLA_HEADS = 4
GLA_DK = GLA_KEY_WIDTH // GLA_HEADS
GLA_DV = GLA_VALUE_WIDTH // GLA_HEADS
GLA_GATE_RANK = 16
GLA_GATE_NORMALIZER = 16.0
GLA_CHUNK = 64

IN_SPLITS = (ATTN_WIDTH, ATTN_WIDTH, ATTN_WIDTH,
             GLA_KEY_WIDTH, GLA_KEY_WIDTH, GLA_VALUE_WIDTH, GLA_VALUE_WIDTH,
             GLA_GATE_RANK, GLA_GATE_RANK)
IN_WIDTH = sum(IN_SPLITS)

D_FF = 5632
CONV_WIDTH = 3
EPS = 1e-6

kernel_name = "hymba_gla_dilated_attn_convglu_encoder"


def rmsnorm(x, g):
    xf = x.astype(jnp.float32)
    y = xf * lax.rsqrt(jnp.mean(xf * xf, axis=-1, keepdims=True) + EPS)
    return (y * g.astype(jnp.float32)).astype(x.dtype)


def apply_partial_rope(t):
    S = t.shape[1]
    pos = jnp.arange(S, dtype=jnp.float32)
    inv_freq = ROPE_THETA ** (-jnp.arange(0, ROPE_DIM, 2, dtype=jnp.float32) / ROPE_DIM)
    ang = pos[:, None] * inv_freq[None, :]
    cos = jnp.cos(ang)[None, :, None, :]
    sin = jnp.sin(ang)[None, :, None, :]
    tr = t[..., :ROPE_DIM].astype(jnp.float32)
    x1, x2 = tr[..., :ROPE_DIM // 2], tr[..., ROPE_DIM // 2:]
    rot = jnp.concatenate([x1 * cos - x2 * sin, x2 * cos + x1 * sin], axis=-1)
    return jnp.concatenate([rot.astype(t.dtype), t[..., ROPE_DIM:]], axis=-1)


def to_residues(t, d):
    B, S = t.shape[:2]
    rest = t.shape[2:]
    t = t.reshape((B, S // d, d) + rest)
    return jnp.swapaxes(t, 1, 2).reshape((B * d, S // d) + rest)


def from_residues(t, d, B):
    N, L = t.shape[:2]
    rest = t.shape[2:]
    t = t.reshape((B, d, L) + rest)
    return jnp.swapaxes(t, 1, 2).reshape((B, L * d) + rest)


def banded_attention(q, k, v, n_side):
    N, L, H, Dh = q.shape
    blk = n_side
    nb = -(-L // blk)
    Lp = nb * blk
    pad = Lp - L
    qb = jnp.pad(q, ((0, 0), (0, pad), (0, 0), (0, 0))).reshape(N, nb, blk, H, Dh)
    kp = jnp.pad(k, ((0, 0), (blk, pad + blk), (0, 0), (0, 0)))
    vp = jnp.pad(v, ((0, 0), (blk, pad + blk), (0, 0), (0, 0)))

    def key_blocks(t):
        return jnp.concatenate(
            [t[:, i * blk:i * blk + Lp].reshape(N, nb, blk, H, Dh) for i in range(3)], axis=2)

    kb, vb = key_blocks(kp), key_blocks(vp)
    s = jnp.einsum('nbqhd,nbkhd->nbhqk', qb, kb).astype(jnp.float32) * (Dh ** -0.5)
    qpos = jnp.arange(nb)[:, None] * blk + jnp.arange(blk)[None, :]
    kpos = jnp.arange(nb)[:, None] * blk - blk + jnp.arange(3 * blk)[None, :]
    qp_, kp_ = qpos[:, :, None], kpos[:, None, :]
    mask = (jnp.abs(kp_ - qp_) <= n_side) & (kp_ >= 0) & ((kp_ < L) | (kp_ == qp_))
    s = jnp.where(mask[None, :, None], s, -jnp.inf)
    m = jnp.max(s, axis=-1, keepdims=True)
    p = jnp.exp(s - m)
    den = jnp.sum(p, axis=-1, keepdims=True)
    lse = (m + jnp.log(den))[..., 0]
    p = p / den
    o = jnp.einsum('nbhqk,nbkhd->nbqhd', p.astype(vb.dtype), vb).reshape(N, Lp, H, Dh)[:, :L]
    lse = jnp.swapaxes(lse, 2, 3).reshape(N, Lp, H)[:, :L]
    return o, lse


def dilated_attention(q, k, v):
    B = q.shape[0]
    outs, lses = [], []
    for window, d in DILATED_PATTERNS:
        n_side = (window // 2) // d
        o, lse = banded_attention(to_residues(q, d), to_residues(k, d), to_residues(v, d), n_side)
        outs.append(from_residues(o, d, B).astype(jnp.float32))
        lses.append(from_residues(lse, d, B))
    w = jax.nn.softmax(jnp.stack(lses), axis=0)
    o = jnp.einsum('pbsh,pbshd->bshd', w, jnp.stack(outs))
    return o.astype(q.dtype)


def gla_chunked(q, k, v, g):
    B, S, H, K = q.shape
    V = v.shape[-1]
    C = GLA_CHUNK
    n = S // C
    q, k, v, g = (t.reshape(B, n, C, H, t.shape[-1]) for t in (q, k, v, g))
    b = jnp.cumsum(g, axis=2)
    b_ref = b[:, :, C // 2 - 1:C // 2]
    b_last = b[:, :, -1]
    a = jnp.einsum('bnihk,bnjhk->bnhij', q * jnp.exp(b - b_ref), k * jnp.exp(b_ref - b))
    a = jnp.where(jnp.tril(jnp.ones((C, C), dtype=bool)), a, 0.0)
    o_intra = jnp.einsum('bnhij,bnjhv->bnihv', a, v)
    q_in = q * jnp.exp(b)
    k_st = k * jnp.exp(b_last[:, :, None] - b)
    dec = jnp.exp(b_last)

    def step(state, xs):
        qc, kc, vc, dc = xs
        o = jnp.einsum('bihk,bhkv->bihv', qc, state)
        state = dc[..., None] * state + jnp.einsum('bjhk,bjhv->bhkv', kc, vc)
        return state, o

    xs = tuple(jnp.moveaxis(t, 1, 0) for t in (q_in, k_st, v, dec))
    _, o_inter = lax.scan(step, jnp.zeros((B, H, K, V), jnp.float32), xs)
    o = o_intra + jnp.moveaxis(o_inter, 0, 1)
    return o.reshape(B, S, H, V)


def hybrid_mixer(h, w_in, gf_up, gf_b, gb_up, gb_b, gla_norm_g, attn_norm_g, w_out):
    B, S, _ = h.shape
    f32 = jnp.float32
    proj = h @ w_in
    split_at = np.cumsum(IN_SPLITS)[:-1].tolist()
    aq, ak, av, gq, gk, gv, gr, zf, zb = jnp.split(proj, split_at, axis=-1)

    def heads(t, nh):
        return t.reshape(B, S, nh, -1)

    aq = apply_partial_rope(heads(aq, ATTN_HEADS))
    ak = apply_partial_rope(heads(ak, ATTN_HEADS))
    ao = dilated_attention(aq, ak, heads(av, ATTN_HEADS)).reshape(B, S, ATTN_WIDTH)
    ao = rmsnorm(ao, attn_norm_g)

    q = heads(gq, GLA_HEADS).astype(f32) * (GLA_DK ** -0.5)
    k = heads(gk, GLA_HEADS).astype(f32)
    v = heads(gv, GLA_HEADS).astype(f32)
    log_gf = jax.nn.log_sigmoid((zf @ gf_up + gf_b).astype(f32)) / GLA_GATE_NORMALIZER
    log_gb = jax.nn.log_sigmoid((zb @ gb_up + gb_b).astype(f32)) / GLA_GATE_NORMALIZER
    flip = lambda t: jnp.flip(t, axis=1)
    o_f = gla_chunked(q, k, v, heads(log_gf, GLA_HEADS))
    o_b = flip(gla_chunked(flip(q), flip(k), flip(v), flip(heads(log_gb, GLA_HEADS))))
    go = rmsnorm(o_f + o_b, gla_norm_g).astype(h.dtype)
    go = (go * jax.nn.silu(heads(gr, GLA_HEADS))).reshape(B, S, GLA_VALUE_WIDTH)

    return jnp.concatenate([ao, go], axis=-1) @ w_out


def conv_glu_ffn(h, w_gate, w_up, conv_w, conv_b, w_down):
    gate = h @ w_gate
    gate = lax.conv_general_dilated(
        gate, conv_w[:, None, :].astype(gate.dtype), window_strides=(1,),
        padding=((CONV_WIDTH // 2, CONV_WIDTH // 2),),
        dimension_numbers=('NWC', 'WIO', 'NWC'), feature_group_count=D_FF) + conv_b
    return (jax.nn.silu(gate) * (h @ w_up)) @ w_down


def _fwd_setup_inputs(seed: int = 0) -> dict:
    key = jax.random.key(seed)
    ks = jax.random.split(key, 20)
    nrm = lambda k, shape, scale: jax.random.normal(k, shape, jnp.float32) * scale
    gain = lambda k, shape: 1.0 + 0.02 * jax.random.normal(k, shape, jnp.float32)
    L = DEPTH
    return {
        "x": jax.random.normal(ks[0], (BATCH, SEQ, D_MODEL), jnp.float32),
        "norm1_g": gain(ks[1], (L, D_MODEL)),
        "w_in": nrm(ks[2], (L, D_MODEL, IN_WIDTH), D_MODEL ** -0.5),
        "gf_up": nrm(ks[3], (L, GLA_GATE_RANK, GLA_KEY_WIDTH), GLA_GATE_RANK ** -0.5),
        "gf_b": nrm(ks[4], (L, GLA_KEY_WIDTH), 0.1),
        "gb_up": nrm(ks[5], (L, GLA_GATE_RANK, GLA_KEY_WIDTH), GLA_GATE_RANK ** -0.5),
        "gb_b": nrm(ks[6], (L, GLA_KEY_WIDTH), 0.1),
        "gla_norm_g": gain(ks[7], (L, GLA_DV)),
        "attn_norm_g": gain(ks[8], (L, ATTN_WIDTH)),
        "w_out": nrm(ks[9], (L, D_MODEL, D_MODEL), D_MODEL ** -0.5),
        "norm2_g": gain(ks[10], (L, D_MODEL)),
        "w_gate": nrm(ks[11], (L, D_MODEL, D_FF), D_MODEL ** -0.5),
        "w_up": nrm(ks[12], (L, D_MODEL, D_FF), D_MODEL ** -0.5),
        "conv_w": nrm(ks[13], (L, CONV_WIDTH, D_FF), CONV_WIDTH ** -0.5),
        "conv_b": nrm(ks[14], (L, D_FF), 0.02),
        "w_down": nrm(ks[15], (L, D_FF, D_MODEL), D_FF ** -0.5),
        "final_norm_g": gain(ks[16], (D_MODEL,)),
    }


def _fwd_reference(x, norm1_g, w_in, gf_up, gf_b, gb_up, gb_b, gla_norm_g, attn_norm_g, w_out,
              norm2_g, w_gate, w_up, conv_w, conv_b, w_down, final_norm_g):
    h = x
    for l in range(DEPTH):
        h = h + hybrid_mixer(rmsnorm(h, norm1_g[l]), w_in[l], gf_up[l], gf_b[l], gb_up[l], gb_b[l],
                             gla_norm_g[l], attn_norm_g[l], w_out[l])
        h = h + conv_glu_ffn(rmsnorm(h, norm2_g[l]), w_gate[l], w_up[l], conv_w[l], conv_b[l], w_down[l])
    return rmsnorm(h, final_norm_g)


import jax as _jax
import jax.numpy as _jnp

TWIN_FORMAT = 'train_step'
FWD_PARAMS = ['x', 'norm1_g', 'w_in', 'gf_up', 'gf_b', 'gb_up', 'gb_b', 'gla_norm_g', 'attn_norm_g', 'w_out', 'norm2_g', 'w_gate', 'w_up', 'conv_w', 'conv_b', 'w_down', 'final_norm_g']
TWIN_WEIGHTS = ['norm1_g', 'w_in', 'gf_up', 'gf_b', 'gb_up', 'gb_b', 'gla_norm_g', 'attn_norm_g', 'w_out', 'norm2_g', 'w_gate', 'w_up', 'conv_w', 'conv_b', 'w_down', 'final_norm_g']
TWIN_DIFF_INPUT = 'x'
TWIN_INPUTS = ['x', 'norm1_g', 'w_in', 'gf_up', 'gf_b', 'gb_up', 'gb_b', 'gla_norm_g', 'attn_norm_g', 'w_out', 'norm2_g', 'w_gate', 'w_up', 'conv_w', 'conv_b', 'w_down', 'final_norm_g', 'loss_target', 'm_norm1_g', 'm_w_in', 'm_gf_up', 'm_gf_b', 'm_gb_up', 'm_gb_b', 'm_gla_norm_g', 'm_attn_norm_g', 'm_w_out', 'm_norm2_g', 'm_w_gate', 'm_w_up', 'm_conv_w', 'm_conv_b', 'm_w_down', 'm_final_norm_g', 'v_norm1_g', 'v_w_in', 'v_gf_up', 'v_gf_b', 'v_gb_up', 'v_gb_b', 'v_gla_norm_g', 'v_attn_norm_g', 'v_w_out', 'v_norm2_g', 'v_w_gate', 'v_w_up', 'v_conv_w', 'v_conv_b', 'v_w_down', 'v_final_norm_g']
TWIN_OUTPUTS = ['loss', 'grad_x', 'grad_norm1_g', 'grad_w_in', 'grad_gf_up', 'grad_gf_b', 'grad_gb_up', 'grad_gb_b', 'grad_gla_norm_g', 'grad_attn_norm_g', 'grad_w_out', 'grad_norm2_g', 'grad_w_gate', 'grad_w_up', 'grad_conv_w', 'grad_conv_b', 'grad_w_down', 'grad_final_norm_g', 'delta_norm1_g', 'delta_w_in', 'delta_gf_up', 'delta_gf_b', 'delta_gb_up', 'delta_gb_b', 'delta_gla_norm_g', 'delta_attn_norm_g', 'delta_w_out', 'delta_norm2_g', 'delta_w_gate', 'delta_w_up', 'delta_conv_w', 'delta_conv_b', 'delta_w_down', 'delta_final_norm_g', 'new_m_norm1_g', 'new_m_w_in', 'new_m_gf_up', 'new_m_gf_b', 'new_m_gb_up', 'new_m_gb_b', 'new_m_gla_norm_g', 'new_m_attn_norm_g', 'new_m_w_out', 'new_m_norm2_g', 'new_m_w_gate', 'new_m_w_up', 'new_m_conv_w', 'new_m_conv_b', 'new_m_w_down', 'new_m_final_norm_g', 'new_v_norm1_g', 'new_v_w_in', 'new_v_gf_up', 'new_v_gf_b', 'new_v_gb_up', 'new_v_gb_b', 'new_v_gla_norm_g', 'new_v_attn_norm_g', 'new_v_w_out', 'new_v_norm2_g', 'new_v_w_gate', 'new_v_w_up', 'new_v_conv_w', 'new_v_conv_b', 'new_v_w_down', 'new_v_final_norm_g']
TWIN_LEAF_KINDS = {'loss': 'loss', 'grad_x': 'grad_x', 'grad_norm1_g': 'grad_w', 'grad_w_in': 'grad_w', 'grad_gf_up': 'grad_w', 'grad_gf_b': 'grad_w', 'grad_gb_up': 'grad_w', 'grad_gb_b': 'grad_w', 'grad_gla_norm_g': 'grad_w', 'grad_attn_norm_g': 'grad_w', 'grad_w_out': 'grad_w', 'grad_norm2_g': 'grad_w', 'grad_w_gate': 'grad_w', 'grad_w_up': 'grad_w', 'grad_conv_w': 'grad_w', 'grad_conv_b': 'grad_w', 'grad_w_down': 'grad_w', 'grad_final_norm_g': 'grad_w', 'delta_norm1_g': 'delta_w', 'delta_w_in': 'delta_w', 'delta_gf_up': 'delta_w', 'delta_gf_b': 'delta_w', 'delta_gb_up': 'delta_w', 'delta_gb_b': 'delta_w', 'delta_gla_norm_g': 'delta_w', 'delta_attn_norm_g': 'delta_w', 'delta_w_out': 'delta_w', 'delta_norm2_g': 'delta_w', 'delta_w_gate': 'delta_w', 'delta_w_up': 'delta_w', 'delta_conv_w': 'delta_w', 'delta_conv_b': 'delta_w', 'delta_w_down': 'delta_w', 'delta_final_norm_g': 'delta_w', 'new_m_norm1_g': 'new_m', 'new_m_w_in': 'new_m', 'new_m_gf_up': 'new_m', 'new_m_gf_b': 'new_m', 'new_m_gb_up': 'new_m', 'new_m_gb_b': 'new_m', 'new_m_gla_norm_g': 'new_m', 'new_m_attn_norm_g': 'new_m', 'new_m_w_out': 'new_m', 'new_m_norm2_g': 'new_m', 'new_m_w_gate': 'new_m', 'new_m_w_up': 'new_m', 'new_m_conv_w': 'new_m', 'new_m_conv_b': 'new_m', 'new_m_w_down': 'new_m', 'new_m_final_norm_g': 'new_m', 'new_v_norm1_g': 'new_v', 'new_v_w_in': 'new_v', 'new_v_gf_up': 'new_v', 'new_v_gf_b': 'new_v', 'new_v_gb_up': 'new_v', 'new_v_gb_b': 'new_v', 'new_v_gla_norm_g': 'new_v', 'new_v_attn_norm_g': 'new_v', 'new_v_w_out': 'new_v', 'new_v_norm2_g': 'new_v', 'new_v_w_gate': 'new_v', 'new_v_w_up': 'new_v', 'new_v_conv_w': 'new_v', 'new_v_conv_b': 'new_v', 'new_v_w_down': 'new_v', 'new_v_final_norm_g': 'new_v'}


def _forward(args):
    return _fwd_reference(*[args[k] for k in FWD_PARAMS])


def _output_shape():
    def fwd():
        inp = _fwd_setup_inputs(0)
        return _fwd_reference(*[inp[k] for k in FWD_PARAMS])
    out = _jax.eval_shape(fwd)
    return out.shape, out.dtype

N_MICROBATCH = 1
ADAM_LR = 0.001
ADAM_B1 = 0.9
ADAM_B2 = 0.999
ADAM_EPS = 1e-08
ADAM_WD = 0.01
ADAM_STEP = 10
PER_EXAMPLE_BATCH_AXIS = {'x': 0, 'loss_target': 0}
SHARED_INPUTS = []
_WEIGHT_DTYPES = {'norm1_g': _jnp.float32, 'w_in': _jnp.float32, 'gf_up': _jnp.float32, 'gf_b': _jnp.float32, 'gb_up': _jnp.float32, 'gb_b': _jnp.float32, 'gla_norm_g': _jnp.float32, 'attn_norm_g': _jnp.float32, 'w_out': _jnp.float32, 'norm2_g': _jnp.float32, 'w_gate': _jnp.float32, 'w_up': _jnp.float32, 'conv_w': _jnp.float32, 'conv_b': _jnp.float32, 'w_down': _jnp.float32, 'final_norm_g': _jnp.float32}
MOMENT_SCALE = {'norm1_g': 1.611328e-01, 'w_in': 9.139200e-02, 'gf_up': 7.111936e-03, 'gf_b': 2.956191e-02, 'gb_up': 7.296341e-03, 'gb_b': 2.862860e-02, 'gla_norm_g': 1.505805e-01, 'attn_norm_g': 1.125418e-01, 'w_out': 8.761607e-02, 'norm2_g': 8.396250e-02, 'w_gate': 3.380017e-02, 'w_up': 3.270016e-02, 'conv_w': 3.340223e-02, 'conv_b': 3.160114e-02, 'w_down': 5.436648e-02, 'final_norm_g': 3.194635e+01}


def _to_microbatches(a, axis):
    t = _jnp.moveaxis(a, axis, 0)
    t = t.reshape((N_MICROBATCH, t.shape[0] // N_MICROBATCH) + t.shape[1:])
    return _jnp.moveaxis(t, 1, axis + 1)


def setup_inputs(seed: int = 0) -> dict:
    inp = _fwd_setup_inputs(seed)
    key = _jax.random.fold_in(_jax.random.key(seed), 7919)
    shape, _ = _output_shape()
    out = dict(inp)
    out["loss_target"] = _jax.random.normal(_jax.random.fold_in(key, 0), shape, _jnp.float32)
    for i, name in enumerate(TWIN_WEIGHTS):
        w = inp[name].astype(_jnp.float32)
        if MOMENT_SCALE is None:
            s = _jnp.sqrt(_jnp.mean(_jnp.square(w)) + 1e-30)
        else:
            s = MOMENT_SCALE[name]
        km, kv = _jax.random.split(_jax.random.fold_in(key, i + 1))
        out[name] = w
        out["m_" + name] = s * _jax.random.normal(km, w.shape, _jnp.float32)
        out["v_" + name] = (s * s) * _jax.random.uniform(kv, w.shape, _jnp.float32, 0.5, 1.5)
    if N_MICROBATCH > 1:
        for name, axis in PER_EXAMPLE_BATCH_AXIS.items():
            out[name] = _to_microbatches(out[name], axis)
    return {'x': out['x'], 'norm1_g': out['norm1_g'], 'w_in': out['w_in'], 'gf_up': out['gf_up'], 'gf_b': out['gf_b'], 'gb_up': out['gb_up'], 'gb_b': out['gb_b'], 'gla_norm_g': out['gla_norm_g'], 'attn_norm_g': out['attn_norm_g'], 'w_out': out['w_out'], 'norm2_g': out['norm2_g'], 'w_gate': out['w_gate'], 'w_up': out['w_up'], 'conv_w': out['conv_w'], 'conv_b': out['conv_b'], 'w_down': out['w_down'], 'final_norm_g': out['final_norm_g'], 'loss_target': out['loss_target'], 'm_norm1_g': out['m_norm1_g'], 'm_w_in': out['m_w_in'], 'm_gf_up': out['m_gf_up'], 'm_gf_b': out['m_gf_b'], 'm_gb_up': out['m_gb_up'], 'm_gb_b': out['m_gb_b'], 'm_gla_norm_g': out['m_gla_norm_g'], 'm_attn_norm_g': out['m_attn_norm_g'], 'm_w_out': out['m_w_out'], 'm_norm2_g': out['m_norm2_g'], 'm_w_gate': out['m_w_gate'], 'm_w_up': out['m_w_up'], 'm_conv_w': out['m_conv_w'], 'm_conv_b': out['m_conv_b'], 'm_w_down': out['m_w_down'], 'm_final_norm_g': out['m_final_norm_g'], 'v_norm1_g': out['v_norm1_g'], 'v_w_in': out['v_w_in'], 'v_gf_up': out['v_gf_up'], 'v_gf_b': out['v_gf_b'], 'v_gb_up': out['v_gb_up'], 'v_gb_b': out['v_gb_b'], 'v_gla_norm_g': out['v_gla_norm_g'], 'v_attn_norm_g': out['v_attn_norm_g'], 'v_w_out': out['v_w_out'], 'v_norm2_g': out['v_norm2_g'], 'v_w_gate': out['v_w_gate'], 'v_w_up': out['v_w_up'], 'v_conv_w': out['v_conv_w'], 'v_conv_b': out['v_conv_b'], 'v_w_down': out['v_w_down'], 'v_final_norm_g': out['v_final_norm_g']}


def _loss(weights, diff, rest, loss_target):
    with _jax.named_scope("forward"):
        args = {**rest, TWIN_DIFF_INPUT: diff, **{k: w.astype(_WEIGHT_DTYPES[k]) for k, w in weights.items()}}
        y = _forward(args)
    with _jax.named_scope("loss_head"):
        err = _jnp.square(y.astype(_jnp.float32) - loss_target)
        return 0.5 * _jnp.sum(_jnp.mean(err, axis=-1)) if err.ndim else 0.5 * err


def _adamw(w, g, m, v):
    m = ADAM_B1 * m + (1.0 - ADAM_B1) * g
    v = ADAM_B2 * v + (1.0 - ADAM_B2) * _jnp.square(g)
    m_hat = m / (1.0 - ADAM_B1 ** ADAM_STEP)
    v_hat = v / (1.0 - ADAM_B2 ** ADAM_STEP)
    delta = -ADAM_LR * (m_hat / (_jnp.sqrt(v_hat) + ADAM_EPS) + ADAM_WD * w)
    return delta, m, v


def reference(x, norm1_g, w_in, gf_up, gf_b, gb_up, gb_b, gla_norm_g, attn_norm_g, w_out, norm2_g, w_gate, w_up, conv_w, conv_b, w_down, final_norm_g, loss_target, m_norm1_g, m_w_in, m_gf_up, m_gf_b, m_gb_up, m_gb_b, m_gla_norm_g, m_attn_norm_g, m_w_out, m_norm2_g, m_w_gate, m_w_up, m_conv_w, m_conv_b, m_w_down, m_final_norm_g, v_norm1_g, v_w_in, v_gf_up, v_gf_b, v_gb_up, v_gb_b, v_gla_norm_g, v_attn_norm_g, v_w_out, v_norm2_g, v_w_gate, v_w_up, v_conv_w, v_conv_b, v_w_down, v_final_norm_g):
    given = dict(x=x, norm1_g=norm1_g, w_in=w_in, gf_up=gf_up, gf_b=gf_b, gb_up=gb_up, gb_b=gb_b, gla_norm_g=gla_norm_g, attn_norm_g=attn_norm_g, w_out=w_out, norm2_g=norm2_g, w_gate=w_gate, w_up=w_up, conv_w=conv_w, conv_b=conv_b, w_down=w_down, final_norm_g=final_norm_g, loss_target=loss_target, m_norm1_g=m_norm1_g, m_w_in=m_w_in, m_gf_up=m_gf_up, m_gf_b=m_gf_b, m_gb_up=m_gb_up, m_gb_b=m_gb_b, m_gla_norm_g=m_gla_norm_g, m_attn_norm_g=m_attn_norm_g, m_w_out=m_w_out, m_norm2_g=m_norm2_g, m_w_gate=m_w_gate, m_w_up=m_w_up, m_conv_w=m_conv_w, m_conv_b=m_conv_b, m_w_down=m_w_down, m_final_norm_g=m_final_norm_g, v_norm1_g=v_norm1_g, v_w_in=v_w_in, v_gf_up=v_gf_up, v_gf_b=v_gf_b, v_gb_up=v_gb_up, v_gb_b=v_gb_b, v_gla_norm_g=v_gla_norm_g, v_attn_norm_g=v_attn_norm_g, v_w_out=v_w_out, v_norm2_g=v_norm2_g, v_w_gate=v_w_gate, v_w_up=v_w_up, v_conv_w=v_conv_w, v_conv_b=v_conv_b, v_w_down=v_w_down, v_final_norm_g=v_final_norm_g)
    weights = {n: given[n] for n in TWIN_WEIGHTS}
    shared = {n: given[n] for n in SHARED_INPUTS}
    per_example = {n: given[n] for n in ['x']}
    grad_fn = _jax.value_and_grad(_loss, argnums=(0, 1))

    def one_microbatch(ex, loss_target):
        ex = dict(ex)
        diff = ex.pop(TWIN_DIFF_INPUT)
        return grad_fn(weights, diff, {**shared, **ex}, loss_target)

    if N_MICROBATCH == 1:
        loss, (grad_w, grad_x) = one_microbatch(per_example, given["loss_target"])
    else:
        def body(carry, xs):
            loss_sum, grad_sum = carry
            l_k, (gw_k, gx_k) = one_microbatch(xs[0], xs[1])
            with _jax.named_scope("update"):
                return (loss_sum + l_k, _jax.tree.map(_jnp.add, grad_sum, gw_k)), gx_k

        init = (_jnp.zeros((), _jnp.float32), _jax.tree.map(_jnp.zeros_like, weights))
        (loss, grad_w), grad_x = _jax.lax.scan(body, init, (per_example, given["loss_target"]))
    with _jax.named_scope("update"):
        delta_w, new_m, new_v = {}, {}, {}
        for n in TWIN_WEIGHTS:
            delta_w[n], new_m[n], new_v[n] = _adamw(weights[n], grad_w[n], given["m_" + n], given["v_" + n])
    return (loss, grad_x, *[grad_w[n] for n in TWIN_WEIGHTS], *[delta_w[n] for n in TWIN_WEIGHTS],
            *[new_m[n] for n in TWIN_WEIGHTS], *[new_v[n] for n in TWIN_WEIGHTS])
```

```python
import functools
import math

import jax
import jax.numpy as jnp
from jax import lax
from jax.experimental import pallas as pl
from jax.experimental.pallas import tpu as pltpu

F32 = jnp.float32
BF16 = jnp.bfloat16

D_MODEL = 2048
ATTN_W = 1024
HEAD = 128
N_HEADS = 8
N_SIDE = 64
DILATIONS = (1, 4, 16)
ROPE_THETA = 500000.0
ROPE_DIM = 32
GLA_K = 512
GLA_V = 1024
GLA_HEADS = 4
GLA_DK = 128
GLA_DV = 256
GATE_RANK = 16
GATE_NORM = 16.0
CHUNK = 64
IN_MAIN = 6144
IN_W = 6176
Z_PAD = 128
D_FF = 5632
EPS = 1e-6
N_CHIPS = 4
N_DEV = 8
LANES = 128

ADAM_LR = 0.001
ADAM_B1 = 0.9
ADAM_B2 = 0.999
ADAM_EPS = 1e-08
ADAM_WD = 0.01
ADAM_STEP = 10

NEG = -1e30
MESH = pl.DeviceIdType.MESH
ANY = pl.BlockSpec(memory_space=pl.ANY)

NN = ((1,), (0,))
NT = ((1,), (1,))
TN = ((0,), (0,))


def _dot(a, b, dims=NN):
    return lax.dot_general(a, b, (dims, ((), ())), preferred_element_type=F32)


def _sigmoid(x):
    return 1.0 / (1.0 + jnp.exp(-x))


def _matmul(name, pairs, grid, out_shape, out_spec, nk, res=None):
    n_in = 2 * len(pairs) + (res is not None)
    dims = [p[4] for p in pairs]

    def body(*refs):
        ins, o_ref = refs[:n_in], refs[n_in]

        def partial_sum():
            tot = None
            for p, dn in enumerate(dims):
                a, b = ins[2 * p][...], ins[2 * p + 1][...]
                t = _dot(a.astype(BF16), b.astype(BF16), dn)
                tot = t if tot is None else tot + t
            return tot

        if nk == 1:
            t = partial_sum()
            if res is not None:
                t = t + ins[-1][...]
            o_ref[...] = t.astype(o_ref.dtype)
        else:
            acc_ref = refs[n_in + 1]
            k = pl.program_id(2)

            @pl.when(k == 0)
            def _():
                if res is not None:
                    acc_ref[...] = ins[-1][...]
                else:
                    acc_ref[...] = jnp.zeros_like(acc_ref)

            acc_ref[...] += partial_sum()

            @pl.when(k == nk - 1)
            def _():
                o_ref[...] = acc_ref[...].astype(o_ref.dtype)

    operands, in_specs = [], []
    for a, a_spec, b, b_spec, _ in pairs:
        operands += [a, b]
        in_specs += [a_spec, b_spec]
    if res is not None:
        operands.append(res[0])
        in_specs.append(res[1])
    acc_shape = tuple(s for s in out_spec.block_shape if s is not None)
    scratch = [pltpu.VMEM(acc_shape, F32)] if nk > 1 else []
    return pl.pallas_call(
        body, name=name, grid=grid, in_specs=in_specs, out_specs=out_spec, out_shape=out_shape, scratch_shapes=scratch,
    )(*operands)


def _mm_nn(name, a, b, tm, tn, out_dtype, res=None):
    M, K = a.shape
    N = b.shape[1]
    pairs = [(a, pl.BlockSpec((tm, K), lambda j, i: (i, 0)), b, pl.BlockSpec((K, tn), lambda j, i: (0, j)), NN)]
    r = None if res is None else (res, pl.BlockSpec((tm, tn), lambda j, i: (i, j)))
    return _matmul(name, pairs, (N // tn, M // tm), jax.ShapeDtypeStruct((M, N), out_dtype),
                   pl.BlockSpec((tm, tn), lambda j, i: (i, j)), 1, r)


def _mm_nn_sharded(name, a, b4, tm, out_dtype):
    M, K = a.shape
    n = b4.shape[2]
    pairs = [(a, pl.BlockSpec((tm, K), lambda j, i: (i, 0)), b4, pl.BlockSpec((None, K, n), lambda j, i: (j, 0, 0)), NN)]
    return _matmul(name, pairs, (N_CHIPS, M // tm), jax.ShapeDtypeStruct((M, N_CHIPS * n), out_dtype),
                   pl.BlockSpec((tm, n), lambda j, i: (i, j)), 1)


def _mm_nt(name, a, b, tm, tn, out_dtype, res=None):
    M, K = a.shape
    N = b.shape[0]
    pairs = [(a, pl.BlockSpec((tm, K), lambda j, i: (i, 0)), b, pl.BlockSpec((tn, K), lambda j, i: (j, 0)), NT)]
    r = None if res is None else (res, pl.BlockSpec((tm, tn), lambda j, i: (i, j)))
    return _matmul(name, pairs, (N // tn, M // tm), jax.ShapeDtypeStruct((M, N), out_dtype),
                   pl.BlockSpec((tm, tn), lambda j, i: (i, j)), 1, r)


def _mm_tn(name, a, g, tka, tn, tmm, out_dtype, out3=None):
    M, Ka = a.shape
    N = g.shape[1]
    pairs = [(a, pl.BlockSpec((tmm, tka), lambda i, j, k: (k, i)), g, pl.BlockSpec((tmm, tn), lambda i, j, k: (k, j)), TN)]
    if out3 is None:
        shape, spec = (Ka, N), pl.BlockSpec((tka, tn), lambda i, j, k: (i, j))
    else:
        shape, spec = (N // out3, Ka, out3), pl.BlockSpec((None, tka, tn), lambda i, j, k: (j, i, 0))
    return _matmul(name, pairs, (Ka // tka, N // tn, M // tmm), jax.ShapeDtypeStruct(shape, out_dtype), spec, M // tmm)


def _rms_fwd(name, x, g, tm=512):
    S, D = x.shape

    def body(x_ref, g_ref, o_ref):
        xv = x_ref[...]
        r = lax.rsqrt(jnp.mean(xv * xv, axis=-1, keepdims=True) + EPS)
        o_ref[...] = (xv * r * g_ref[...]).astype(o_ref.dtype)

    return pl.pallas_call(
        body, name=name, grid=(S // tm,),
        in_specs=[pl.BlockSpec((tm, D), lambda i: (i, 0)), pl.BlockSpec((1, D), lambda i: (0, 0))],
        out_specs=pl.BlockSpec((tm, D), lambda i: (i, 0)), out_shape=jax.ShapeDtypeStruct((S, D), BF16),
    )(x, g)


def _rms_bwd(name, x, g, dn, dres, tm=512):
    S, D = x.shape

    def body(x_ref, g_ref, dn_ref, dres_ref, dx_ref, dxb_ref, dg_ref):
        i = pl.program_id(0)

        @pl.when(i == 0)
        def _():
            dg_ref[...] = jnp.zeros_like(dg_ref)

        xv = x_ref[...]
        r = lax.rsqrt(jnp.mean(xv * xv, axis=-1, keepdims=True) + EPS)
        xhat = xv * r
        dnv = dn_ref[...].astype(F32)
        dg_ref[...] += jnp.sum(dnv * xhat, axis=0, keepdims=True)
        t = dnv * g_ref[...]
        dx = r * (t - xhat * jnp.mean(t * xhat, axis=-1, keepdims=True)) + dres_ref[...]
        dx_ref[...] = dx
        dxb_ref[...] = dx.astype(BF16)

    row = pl.BlockSpec((tm, D), lambda i: (i, 0))
    vec = pl.BlockSpec((1, D), lambda i: (0, 0))
    return pl.pallas_call(
        body, name=name, grid=(S // tm,), in_specs=[row, vec, row, row], out_specs=[row, row, vec],
        out_shape=[jax.ShapeDtypeStruct((S, D), F32), jax.ShapeDtypeStruct((S, D), BF16), jax.ShapeDtypeStruct((1, D), F32)],
    )(x, g, dn, dres)


def _final_loss(h2, g, target, tm=512):
    S, D = h2.shape

    def body(x_ref, g_ref, t_ref, loss_ref, dg_ref, dx_ref, dxb_ref):
        i = pl.program_id(0)

        @pl.when(i == 0)
        def _():
            loss_ref[...] = jnp.zeros_like(loss_ref)
            dg_ref[...] = jnp.zeros_like(dg_ref)

        xv = x_ref[...]
        r = lax.rsqrt(jnp.mean(xv * xv, axis=-1, keepdims=True) + EPS)
        xhat = xv * r
        gv = g_ref[...]
        diff = xhat * gv - t_ref[...]
        per_tok = jnp.mean(diff * diff, axis=-1, keepdims=True)
        loss_ref[...] += 0.5 * jnp.sum(per_tok, axis=0, keepdims=True)
        dy = diff * (1.0 / D)
        dg_ref[...] += jnp.sum(dy * xhat, axis=0, keepdims=True)
        t = dy * gv
        dx = r * (t - xhat * jnp.mean(t * xhat, axis=-1, keepdims=True))
        dx_ref[...] = dx
        dxb_ref[...] = dx.astype(BF16)

    row = pl.BlockSpec((tm, D), lambda i: (i, 0))
    vec = pl.BlockSpec((1, D), lambda i: (0, 0))
    return pl.pallas_call(
        body, name="final_loss", grid=(S // tm,), in_specs=[row, vec, row],
        out_specs=[pl.BlockSpec((1, LANES), lambda i: (0, 0)), vec, row, row],
        out_shape=[jax.ShapeDtypeStruct((1, LANES), F32), jax.ShapeDtypeStruct((1, D), F32),
                   jax.ShapeDtypeStruct((S, D), F32), jax.ShapeDtypeStruct((S, D), BF16)],
    )(h2, g, target)


def _rope_tables(S):
    pos = jnp.arange(S, dtype=F32)
    inv_freq = ROPE_THETA ** (-jnp.arange(0, ROPE_DIM, 2, dtype=F32) / ROPE_DIM)
    ang = pos[:, None] * inv_freq[None, :]
    cos, sin = jnp.cos(ang), jnp.sin(ang)
    half = ROPE_DIM // 2
    rest = HEAD - ROPE_DIM
    z_h, z_r = jnp.zeros((S, half), F32), jnp.zeros((S, rest), F32)
    tab_c = jnp.concatenate([cos, cos, jnp.ones((S, rest), F32)], axis=1)
    tab_up = jnp.concatenate([z_h, sin, z_r], axis=1)
    tab_dn = jnp.concatenate([-sin, z_h, z_r], axis=1)
    return tab_c, tab_up, tab_dn


def _rope_head(t, c, up, dn):
    half = ROPE_DIM // 2
    return t * c + pltpu.roll(t, half, axis=1) * up + pltpu.roll(t, HEAD - half, axis=1) * dn


def _rope_fwd(proj, tabs, tm=512):
    S = proj.shape[0]
    W = 2 * ATTN_W

    def body(p_ref, c_ref, up_ref, dn_ref, o_ref):
        c, up, dn = c_ref[...], up_ref[...], dn_ref[...]
        for h in range(W // HEAD):
            sl = slice(h * HEAD, (h + 1) * HEAD)
            o_ref[:, sl] = _rope_head(p_ref[:, sl].astype(F32), c, up, dn).astype(BF16)

    tab = pl.BlockSpec((tm, HEAD), lambda i: (i, 0))
    return pl.pallas_call(
        body, name="rope_fwd", grid=(S // tm,), in_specs=[pl.BlockSpec((tm, W), lambda i: (i, 0)), tab, tab, tab],
        out_specs=pl.BlockSpec((tm, W), lambda i: (i, 0)), out_shape=jax.ShapeDtypeStruct((S, W), BF16),
    )(proj, *tabs)


def _attn_grad_merge(dqs, dks, dvs, tabs, tm=256):
    S = dqs[0].shape[0]

    def body(*refs):
        q_refs, k_refs, v_refs = refs[0:3], refs[3:6], refs[6:9]
        c, up, dn = refs[9][...], refs[10][...], refs[11][...]
        o_ref = refs[12]
        for h in range(N_HEADS):
            sl = slice(h * HEAD, (h + 1) * HEAD)
            for part, rs in ((0, q_refs), (1, k_refs)):
                t = rs[0][:, sl] + rs[1][:, sl] + rs[2][:, sl]
                osl = slice(part * ATTN_W + h * HEAD, part * ATTN_W + (h + 1) * HEAD)
                o_ref[:, osl] = _rope_head(t, c, -up, -dn).astype(BF16)
        o_ref[:, 2 * ATTN_W:] = (v_refs[0][...] + v_refs[1][...] + v_refs[2][...]).astype(BF16)

    blk = pl.BlockSpec((tm, ATTN_W), lambda i: (i, 0))
    tab = pl.BlockSpec((tm, HEAD), lambda i: (i, 0))
    return pl.pallas_call(
        body, name="attn_grad_merge", grid=(S // tm,), in_specs=[blk] * 9 + [tab] * 3,
        out_specs=pl.BlockSpec((tm, 3 * ATTN_W), lambda i: (i, 0)), out_shape=jax.ShapeDtypeStruct((S, 3 * ATTN_W), BF16),
    )(*dqs, *dks, *dvs, *tabs)


def _band_specs(T, L, width, ncol, col):
    ratio = T // N_SIDE
    nhb = L // N_SIDE
    cur = pl.BlockSpec((T, width), lambda r, i: (i, r * ncol + col))
    prev = pl.BlockSpec((N_SIDE, width), lambda r, i: (jnp.maximum(i * ratio - 1, 0), r * ncol + col))
    nxt = pl.BlockSpec((N_SIDE, width), lambda r, i: (jnp.minimum((i + 1) * ratio, nhb - 1), r * ncol + col))
    return cur, prev, nxt


def _band_mask(T, L):
    i = pl.program_id(1)
    row = lax.broadcasted_iota(jnp.int32, (T, T + 2 * N_SIDE), 0)
    col = lax.broadcasted_iota(jnp.int32, (T, T + 2 * N_SIDE), 1)
    pos = i * T - N_SIDE + col
    return (col >= row) & (col <= row + 2 * N_SIDE) & (pos >= 0) & (pos < L)


def _window(p_ref, c_ref, n_ref, sl=None):
    if sl is None:
        return jnp.concatenate([p_ref[...], c_ref[...], n_ref[...]], axis=0)
    return jnp.concatenate([p_ref[:, sl], c_ref[:, sl], n_ref[:, sl]], axis=0)


def _attn_dims(S, d):
    L = S // d
    T = min(128, L)
    return L, T, (d, L // T)


def _attn_fwd(qk, proj, d):
    S = qk.shape[0]
    L, T, grid = _attn_dims(S, d)
    scale = HEAD ** -0.5
    qk_v = qk.reshape(L, d * 2 * ATTN_W)
    pj_v = proj.reshape(L, d * IN_MAIN)
    ncol_p = IN_MAIN // ATTN_W

    def body(q_ref, kp, kc, kn, vp, vc, vn, o_ref, lse_ref):
        mask = _band_mask(T, L)
        lane = lax.broadcasted_iota(jnp.int32, (T, LANES), 1)
        lse_tile = jnp.zeros((T, LANES), F32)
        for h in range(N_HEADS):
            sl = slice(h * HEAD, (h + 1) * HEAD)
            kw, vw = _window(kp, kc, kn, sl), _window(vp, vc, vn, sl)
            s = jnp.where(mask, _dot(q_ref[:, sl], kw, NT) * scale, NEG)
            m = jnp.max(s, axis=1, keepdims=True)
            p = jnp.exp(s - m)
            l = jnp.sum(p, axis=1, keepdims=True)
            o_ref[:, sl] = _dot(p.astype(BF16), vw) / l
            lse_tile = jnp.where(lane == h, m + jnp.log(l), lse_tile)
        lse_ref[...] = lse_tile

    q_cur, _, _ = _band_specs(T, L, ATTN_W, 2, 0)
    k_specs = _band_specs(T, L, ATTN_W, 2, 1)
    v_specs = _band_specs(T, L, ATTN_W, ncol_p, 2)
    o, lse = pl.pallas_call(
        body, name=f"attn_fwd_d{d}", grid=grid,
        in_specs=[q_cur, k_specs[1], k_specs[0], k_specs[2], v_specs[1], v_specs[0], v_specs[2]],
        out_specs=[pl.BlockSpec((T, ATTN_W), lambda r, i: (i, r)), pl.BlockSpec((T, LANES), lambda r, i: (i, r))],
        out_shape=[jax.ShapeDtypeStruct((L, d * ATTN_W), F32), jax.ShapeDtypeStruct((L, d * LANES), F32)],
    )(qk_v, qk_v, qk_v, qk_v, pj_v, pj_v, pj_v)
    return o.reshape(S, ATTN_W), lse.reshape(S, LANES)


def _attn_combine(outs, lses, g, tm=256):
    S = outs[0].shape[0]

    def body(o1, o2, o3, l1, l2, l3, g_ref, ao_ref, o_ref, lse_ref):
        lane = lax.broadcasted_iota(jnp.int32, (tm, LANES), 1)
        lse_tile = jnp.zeros((tm, LANES), F32)
        a = [l1[...], l2[...], l3[...]]
        ssq = jnp.zeros((tm, 1), F32)
        for h in range(N_HEADS):
            sl = slice(h * HEAD, (h + 1) * HEAD)
            a1, a2, a3 = (t[:, h:h + 1] for t in a)
            mx = jnp.maximum(jnp.maximum(a1, a2), a3)
            e1, e2, e3 = jnp.exp(a1 - mx), jnp.exp(a2 - mx), jnp.exp(a3 - mx)
            den = e1 + e2 + e3
            oh = (e1 * o1[:, sl] + e2 * o2[:, sl] + e3 * o3[:, sl]) / den
            o_ref[:, sl] = oh
            ssq = ssq + jnp.sum(oh * oh, axis=1, keepdims=True)
            lse_tile = jnp.where(lane == h, mx + jnp.log(den), lse_tile)
        lse_ref[...] = lse_tile
        r = lax.rsqrt(ssq * (1.0 / ATTN_W) + EPS)
        ao_ref[...] = (o_ref[...] * r * g_ref[...]).astype(BF16)

    blk = pl.BlockSpec((tm, ATTN_W), lambda i: (i, 0))
    ls = pl.BlockSpec((tm, LANES), lambda i: (i, 0))
    return pl.pallas_call(
        body, name="attn_combine", grid=(S // tm,),
        in_specs=[blk, blk, blk, ls, ls, ls, pl.BlockSpec((1, ATTN_W), lambda i: (0, 0))], out_specs=[blk, blk, ls],
        out_shape=[jax.ShapeDtypeStruct((S, ATTN_W), BF16), jax.ShapeDtypeStruct((S, ATTN_W), F32),
                   jax.ShapeDtypeStruct((S, LANES), F32)],
    )(*outs, *lses, g)


def _attn_norm_bwd(o, g, dao, tm=256):
    S = o.shape[0]

    def body(o_ref, g_ref, dao_ref, do_ref, dl_ref, dg_ref):
        i = pl.program_id(0)

        @pl.when(i == 0)
        def _():
            dg_ref[...] = jnp.zeros_like(dg_ref)

        ov = o_ref[...]
        r = lax.rsqrt(jnp.mean(ov * ov, axis=-1, keepdims=True) + EPS)
        ohat = ov * r
        dn = dao_ref[...].astype(F32)
        dg_ref[...] += jnp.sum(dn * ohat, axis=0, keepdims=True)
        t = dn * g_ref[...]
        do = r * (t - ohat * jnp.mean(t * ohat, axis=-1, keepdims=True))
        do_ref[...] = do.astype(BF16)
        prod = do * ov
        lane = lax.broadcasted_iota(jnp.int32, (tm, LANES), 1)
        tile = jnp.zeros((tm, LANES), F32)
        for h in range(N_HEADS):
            tile = jnp.where(lane == h, jnp.sum(prod[:, h * HEAD:(h + 1) * HEAD], axis=1, keepdims=True), tile)
        dl_ref[...] = tile

    blk = pl.BlockSpec((tm, ATTN_W), lambda i: (i, 0))
    vec = pl.BlockSpec((1, ATTN_W), lambda i: (0, 0))
    return pl.pallas_call(
        body, name="attn_norm_bwd", grid=(S // tm,),
        in_specs=[blk, vec, pl.BlockSpec((tm, ATTN_W), lambda i: (i, 0))],
        out_specs=[blk, pl.BlockSpec((tm, LANES), lambda i: (i, 0)), vec],
        out_shape=[jax.ShapeDtypeStruct((S, ATTN_W), BF16), jax.ShapeDtypeStruct((S, LANES), F32),
                   jax.ShapeDtypeStruct((1, ATTN_W), F32)],
    )(o, g, dao)


def _attn_bwd_dq(qk, proj, do, lse, delta, d):
    S = qk.shape[0]
    L, T, grid = _attn_dims(S, d)
    scale = HEAD ** -0.5
    qk_v = qk.reshape(L, d * 2 * ATTN_W)
    pj_v = proj.reshape(L, d * IN_MAIN)
    ncol_p = IN_MAIN // ATTN_W

    def body(q_ref, kp, kc, kn, vp, vc, vn, do_ref, lse_ref, dl_ref, dq_ref):
        mask = _band_mask(T, L)
        lse_t, dl_t = lse_ref[...], dl_ref[...]
        for h in range(N_HEADS):
            sl = slice(h * HEAD, (h + 1) * HEAD)
            kw, vw = _window(kp, kc, kn, sl), _window(vp, vc, vn, sl)
            s = _dot(q_ref[:, sl], kw, NT) * scale
            p = jnp.where(mask, jnp.exp(s - lse_t[:, h:h + 1]), 0.0)
            dp = _dot(do_ref[:, sl], vw, NT)
            ds = (p * (dp - dl_t[:, h:h + 1]) * scale).astype(BF16)
            dq_ref[:, sl] = _dot(ds, kw)

    q_cur, _, _ = _band_specs(T, L, ATTN_W, 2, 0)
    k_specs = _band_specs(T, L, ATTN_W, 2, 1)
    v_specs = _band_specs(T, L, ATTN_W, ncol_p, 2)
    cur = pl.BlockSpec((T, ATTN_W), lambda r, i: (i, r))
    stat = pl.BlockSpec((T, LANES), lambda r, i: (i, r))
    dq = pl.pallas_call(
        body, name=f"attn_bwd_dq_d{d}", grid=grid,
        in_specs=[q_cur, k_specs[1], k_specs[0], k_specs[2], v_specs[1], v_specs[0], v_specs[2], cur, stat, stat],
        out_specs=cur, out_shape=jax.ShapeDtypeStruct((L, d * ATTN_W), F32),
    )(qk_v, qk_v, qk_v, qk_v, pj_v, pj_v, pj_v, do.reshape(L, d * ATTN_W), lse.reshape(L, d * LANES),
      delta.reshape(L, d * LANES))
    return dq.reshape(S, ATTN_W)


def _attn_bwd_dkv(qk, proj, do, lse, delta, d):
    S = qk.shape[0]
    L, T, grid = _attn_dims(S, d)
    scale = HEAD ** -0.5
    qk_v = qk.reshape(L, d * 2 * ATTN_W)
    pj_v = proj.reshape(L, d * IN_MAIN)
    ncol_p = IN_MAIN // ATTN_W

    def body(k_ref, v_ref, qp, qc, qn, dop, doc, don, lp, lc, ln, dlp, dlc, dln, dk_ref, dv_ref):
        mask = _band_mask(T, L)
        lse_t = _window(lp, lc, ln).T
        dl_t = _window(dlp, dlc, dln).T
        for h in range(N_HEADS):
            sl = slice(h * HEAD, (h + 1) * HEAD)
            qw, dow = _window(qp, qc, qn, sl), _window(dop, doc, don, sl)
            st = _dot(k_ref[:, sl], qw, NT) * scale
            pt = jnp.where(mask, jnp.exp(st - lse_t[h:h + 1, :]), 0.0)
            dv_ref[:, sl] = _dot(pt.astype(BF16), dow)
            dpt = _dot(v_ref[:, sl], dow, NT)
            dst = (pt * (dpt - dl_t[h:h + 1, :]) * scale).astype(BF16)
            dk_ref[:, sl] = _dot(dst, qw)

    q_specs = _band_specs(T, L, ATTN_W, 2, 0)
    k_cur, _, _ = _band_specs(T, L, ATTN_W, 2, 1)
    v_cur, _, _ = _band_specs(T, L, ATTN_W, ncol_p, 2)
    do_specs = _band_specs(T, L, ATTN_W, 1, 0)
    st_specs = _band_specs(T, L, LANES, 1, 0)
    cur = pl.BlockSpec((T, ATTN_W), lambda r, i: (i, r))
    do_v, lse_v, dl_v = do.reshape(L, d * ATTN_W), lse.reshape(L, d * LANES), delta.reshape(L, d * LANES)
    dk, dv = pl.pallas_call(
        body, name=f"attn_bwd_dkv_d{d}", grid=grid,
        in_specs=[k_cur, v_cur, q_specs[1], q_specs[0], q_specs[2], do_specs[1], do_specs[0], do_specs[2],
                  st_specs[1], st_specs[0], st_specs[2], st_specs[1], st_specs[0], st_specs[2]],
        out_specs=[cur, cur],
        out_shape=[jax.ShapeDtypeStruct((L, d * ATTN_W), F32), jax.ShapeDtypeStruct((L, d * ATTN_W), F32)],
    )(qk_v, pj_v, qk_v, qk_v, qk_v, do_v, do_v, do_v, lse_v, lse_v, lse_v, dl_v, dl_v, dl_v)
    return dk.reshape(S, ATTN_W), dv.reshape(S, ATTN_W)


def _cumsum_rows(x, reverse):
    n = x.shape[0]
    row = lax.broadcasted_iota(jnp.int32, x.shape, 0)
    s = 1
    while s < n:
        if reverse:
            x = x + jnp.where(row < n - s, pltpu.roll(x, n - s, axis=0), 0.0)
        else:
            x = x + jnp.where(row >= s, pltpu.roll(x, s, axis=0), 0.0)
        s *= 2
    return x


def _gla_chunk_terms(q_ref, k_ref, v_ref, g_ref, h, reverse):
    ksl = slice(h * GLA_DK, (h + 1) * GLA_DK)
    q = q_ref[:, ksl].astype(F32) * (GLA_DK ** -0.5)
    k = k_ref[:, ksl].astype(F32)
    v = v_ref[:, h * GLA_DV:(h + 1) * GLA_DV]
    b = _cumsum_rows(g_ref[:, ksl], reverse)
    r_ref = CHUNK // 2 if reverse else CHUNK // 2 - 1
    r_last = 0 if reverse else CHUNK - 1
    b_ref, b_last = b[r_ref:r_ref + 1, :], b[r_last:r_last + 1, :]
    ii = lax.broadcasted_iota(jnp.int32, (CHUNK, CHUNK), 0)
    jj = lax.broadcasted_iota(jnp.int32, (CHUNK, CHUNK), 1)
    causal = (jj >= ii) if reverse else (jj <= ii)
    e_q, e_k = jnp.exp(b - b_ref), jnp.exp(b_ref - b)
    e_in, e_st = jnp.exp(b), jnp.exp(b_last - b)
    return dict(q=q, k=k, v=v, b=b, causal=causal, e_q=e_q, e_k=e_k, e_in=e_in, e_st=e_st, dec=jnp.exp(b_last),
                qe=q * e_q, ke=k * e_k, q_in=q * e_in, k_st=k * e_st, r_ref=r_ref, r_last=r_last)


def _gla_specs(n, order):
    q = pl.BlockSpec((CHUNK, GLA_K), lambda c: (order(c), 3 * ATTN_W // GLA_K))
    k = pl.BlockSpec((CHUNK, GLA_K), lambda c: (order(c), 3 * ATTN_W // GLA_K + 1))
    v = pl.BlockSpec((CHUNK, GLA_V), lambda c: (order(c), (3 * ATTN_W + 2 * GLA_K) // GLA_V))
    return q, k, v


def _gla_fwd(proj, gates, reverse, o_prev=None):
    S = proj.shape[0]
    n = S // CHUNK
    order = (lambda c: n - 1 - c) if reverse else (lambda c: c)
    gcol = 1 if reverse else 0

    def body(*refs):
        if o_prev is None:
            q_ref, k_ref, v_ref, g_ref, o_ref, st_ref, state = refs
        else:
            q_ref, k_ref, v_ref, g_ref, op_ref, o_ref, st_ref, state = refs
        c = pl.program_id(0)

        @pl.when(c == 0)
        def _():
            state[...] = jnp.zeros_like(state)

        for h in range(GLA_HEADS):
            t = _gla_chunk_terms(q_ref, k_ref, v_ref, g_ref, h, reverse)
            a = jnp.where(t["causal"], _dot(t["qe"].astype(BF16), t["ke"].astype(BF16), NT), 0.0)
            o = _dot(a.astype(BF16), t["v"])
            st = state[h]
            st_b = st.astype(BF16)
            st_ref[0, h] = st_b
            o = o + _dot(t["q_in"].astype(BF16), st_b, NT)
            state[h] = st * t["dec"] + _dot(t["v"], t["k_st"].astype(BF16), TN)
            vsl = slice(h * GLA_DV, (h + 1) * GLA_DV)
            if o_prev is not None:
                o = o + op_ref[:, vsl]
            o_ref[:, vsl] = o

    q_spec, k_spec, v_spec = _gla_specs(n, order)
    o_spec = pl.BlockSpec((CHUNK, GLA_V), lambda c: (order(c), 0))
    in_specs = [q_spec, k_spec, v_spec, pl.BlockSpec((CHUNK, GLA_K), lambda c: (order(c), gcol))]
    operands = [proj, proj, proj, gates]
    if o_prev is not None:
        in_specs.append(o_spec)
        operands.append(o_prev)
    return pl.pallas_call(
        body, name="gla_fwd_rev" if reverse else "gla_fwd", grid=(n,), in_specs=in_specs,
        out_specs=[o_spec, pl.BlockSpec((1, GLA_HEADS, GLA_DV, GLA_DK), lambda c: (order(c), 0, 0, 0))],
        out_shape=[jax.ShapeDtypeStruct((S, GLA_V), F32), jax.ShapeDtypeStruct((n, GLA_HEADS, GLA_DV, GLA_DK), BF16)],
        scratch_shapes=[pltpu.VMEM((GLA_HEADS, GLA_DV, GLA_DK), F32)],
    )(*operands)


def _gla_bwd(proj, gates, states, do, reverse, prev=None):
    S = proj.shape[0]
    n = S // CHUNK
    order = (lambda c: c) if reverse else (lambda c: n - 1 - c)
    gcol = 1 if reverse else 0
    out_dt = F32 if prev is None else BF16

    def body(*refs):
        if prev is None:
            q_ref, k_ref, v_ref, g_ref, st_ref, do_ref, dq_ref, dk_ref, dv_ref, dg_ref, dstate = refs
        else:
            q_ref, k_ref, v_ref, g_ref, st_ref, do_ref, pq, pk, pv, dq_ref, dk_ref, dv_ref, dg_ref, dstate = refs
        c = pl.program_id(0)

        @pl.when(c == 0)
        def _():
            dstate[...] = jnp.zeros_like(dstate)

        row = lax.broadcasted_iota(jnp.int32, (CHUNK, GLA_DK), 0)
        for h in range(GLA_HEADS):
            t = _gla_chunk_terms(q_ref, k_ref, v_ref, g_ref, h, reverse)
            ksl = slice(h * GLA_DK, (h + 1) * GLA_DK)
            vsl = slice(h * GLA_DV, (h + 1) * GLA_DV)
            v = t["v"]
            dob = do_ref[:, vsl].astype(BF16)
            st_b = st_ref[0, h]
            dst = dstate[h]
            dst_b = dst.astype(BF16)
            qe_b, ke_b = t["qe"].astype(BF16), t["ke"].astype(BF16)
            q_in_b, k_st_b = t["q_in"].astype(BF16), t["k_st"].astype(BF16)
            a = jnp.where(t["causal"], _dot(qe_b, ke_b, NT), 0.0)
            da = jnp.where(t["causal"], _dot(dob, v, NT), 0.0).astype(BF16)
            dv = _dot(a.astype(BF16), dob, TN) + _dot(k_st_b, dst_b, NT)
            dqe = _dot(da, ke_b)
            dke = _dot(da, qe_b, TN)
            dq_in = _dot(dob, st_b)
            dk_st = _dot(v, dst_b)
            ddec = jnp.sum(dst * st_b.astype(F32), axis=0, keepdims=True)
            dstate[h] = _dot(dob, q_in_b, TN) + dst * t["dec"]
            dq = (dqe * t["e_q"] + dq_in * t["e_in"]) * (GLA_DK ** -0.5)
            dk = dke * t["e_k"] + dk_st * t["e_st"]
            w_q, w_k = dqe * t["qe"], dke * t["ke"]
            w_st = dk_st * t["k_st"]
            db = w_q - w_k + dq_in * t["q_in"] - w_st
            db_ref = jnp.sum(w_k - w_q, axis=0, keepdims=True)
            db_last = jnp.sum(w_st, axis=0, keepdims=True) + ddec * t["dec"]
            db = db + jnp.where(row == t["r_ref"], db_ref, 0.0) + jnp.where(row == t["r_last"], db_last, 0.0)
            dg_ref[:, ksl] = _cumsum_rows(db, not reverse)
            if prev is not None:
                dq, dk, dv = dq + pq[:, ksl], dk + pk[:, ksl], dv + pv[:, vsl]
            dq_ref[:, ksl] = dq.astype(out_dt)
            dk_ref[:, ksl] = dk.astype(out_dt)
            dv_ref[:, vsl] = dv.astype(out_dt)

    q_spec, k_spec, v_spec = _gla_specs(n, order)
    kk = pl.BlockSpec((CHUNK, GLA_K), lambda c: (order(c), 0))
    vv = pl.BlockSpec((CHUNK, GLA_V), lambda c: (order(c), 0))
    in_specs = [q_spec, k_spec, v_spec, pl.BlockSpec((CHUNK, GLA_K), lambda c: (order(c), gcol)),
                pl.BlockSpec((1, GLA_HEADS, GLA_DV, GLA_DK), lambda c: (order(c), 0, 0, 0)), vv]
    operands = [proj, proj, proj, gates, states, do]
    if prev is not None:
        in_specs += [kk, kk, vv]
        operands += list(prev)
    return pl.pallas_call(
        body, name="gla_bwd_rev" if reverse else "gla_bwd", grid=(n,), in_specs=in_specs, out_specs=[kk, kk, vv, kk],
        out_shape=[jax.ShapeDtypeStruct((S, GLA_K), out_dt), jax.ShapeDtypeStruct((S, GLA_K), out_dt),
                   jax.ShapeDtypeStruct((S, GLA_V), out_dt), jax.ShapeDtypeStruct((S, GLA_K), F32)],
        scratch_shapes=[pltpu.VMEM((GLA_HEADS, GLA_DV, GLA_DK), F32)],
    )(*operands)


def _gates_fwd(z, wg, bias, tm=512):
    S = z.shape[0]
    W = 2 * GLA_K

    def body(z_ref, w_ref, b_ref, o_ref):
        zg = _dot(z_ref[...], w_ref[...]) + b_ref[...]
        o_ref[...] = (jnp.minimum(zg, 0.0) - jnp.log(1.0 + jnp.exp(-jnp.abs(zg)))) * (1.0 / GATE_NORM)

    return pl.pallas_call(
        body, name="gates_fwd", grid=(S // tm,),
        in_specs=[pl.BlockSpec((tm, Z_PAD), lambda i: (i, 0)), pl.BlockSpec((Z_PAD, W), lambda i: (0, 0)),
                  pl.BlockSpec((1, W), lambda i: (0, 0))],
        out_specs=pl.BlockSpec((tm, W), lambda i: (i, 0)), out_shape=jax.ShapeDtypeStruct((S, W), F32),
    )(z, wg, bias)


def _gates_bwd(z, wg, bias, dg_f, dg_b, tm=512):
    S = z.shape[0]
    W = 2 * GLA_K

    def body(z_ref, w_ref, b_ref, dgf_ref, dgb_ref, dz_ref, dw_ref, db_ref):
        i = pl.program_id(0)

        @pl.when(i == 0)
        def _():
            dw_ref[...] = jnp.zeros_like(dw_ref)
            db_ref[...] = jnp.zeros_like(db_ref)

        zv = z_ref[...]
        zg = _dot(zv, w_ref[...]) + b_ref[...]
        dg = jnp.concatenate([dgf_ref[...], dgb_ref[...]], axis=1)
        dzg = dg * (1.0 / GATE_NORM) * _sigmoid(-zg)
        db_ref[...] += jnp.sum(dzg, axis=0, keepdims=True)
        dzg_b = dzg.astype(BF16)
        dw_ref[...] += _dot(zv, dzg_b, TN)
        dz_ref[...] = _dot(dzg_b, w_ref[...], NT).astype(BF16)

    half = pl.BlockSpec((tm, GLA_K), lambda i: (i, 0))
    return pl.pallas_call(
        body, name="gates_bwd", grid=(S // tm,),
        in_specs=[pl.BlockSpec((tm, Z_PAD), lambda i: (i, 0)), pl.BlockSpec((Z_PAD, W), lambda i: (0, 0)),
                  pl.BlockSpec((1, W), lambda i: (0, 0)), half, half],
        out_specs=[pl.BlockSpec((tm, Z_PAD), lambda i: (i, 0)), pl.BlockSpec((Z_PAD, W), lambda i: (0, 0)),
                   pl.BlockSpec((1, W), lambda i: (0, 0))],
        out_shape=[jax.ShapeDtypeStruct((S, Z_PAD), BF16), jax.ShapeDtypeStruct((Z_PAD, W), F32),
                   jax.ShapeDtypeStruct((1, W), F32)],
    )(z, wg, bias, dg_f, dg_b)


def _gla_out_fwd(o, proj, g, tm=512):
    S = o.shape[0]

    def body(o_ref, gr_ref, g_ref, out_ref):
        gn = g_ref[...]
        for h in range(GLA_HEADS):
            sl = slice(h * GLA_DV, (h + 1) * GLA_DV)
            ov = o_ref[:, sl]
            r = lax.rsqrt(jnp.mean(ov * ov, axis=-1, keepdims=True) + EPS)
            gr = gr_ref[:, sl].astype(F32)
            out_ref[:, sl] = (ov * r * gn * (gr * _sigmoid(gr))).astype(BF16)

    blk = pl.BlockSpec((tm, GLA_V), lambda i: (i, 0))
    return pl.pallas_call(
        body, name="gla_out_fwd", grid=(S // tm,),
        in_specs=[blk, pl.BlockSpec((tm, GLA_V), lambda i: (i, (3 * ATTN_W + 2 * GLA_K + GLA_V) // GLA_V)),
                  pl.BlockSpec((1, GLA_DV), lambda i: (0, 0))],
        out_specs=blk, out_shape=jax.ShapeDtypeStruct((S, GLA_V), BF16),
    )(o, proj, g)


def _gla_out_bwd(o, proj, g, dcat, tm=512):
    S = o.shape[0]

    def body(o_ref, gr_ref, g_ref, dgo_ref, do_ref, dgr_ref, dg_ref):
        i = pl.program_id(0)

        @pl.when(i == 0)
        def _():
            dg_ref[...] = jnp.zeros_like(dg_ref)

        gn = g_ref[...]
        dg_acc = jnp.zeros((1, GLA_DV), F32)
        for h in range(GLA_HEADS):
            sl = slice(h * GLA_DV, (h + 1) * GLA_DV)
            ov = o_ref[:, sl]
            r = lax.rsqrt(jnp.mean(ov * ov, axis=-1, keepdims=True) + EPS)
            yhat = ov * r
            gr = gr_ref[:, sl].astype(F32)
            sg = _sigmoid(gr)
            dgo = dgo_ref[:, sl].astype(F32)
            dgr_ref[:, sl] = (dgo * (yhat * gn) * (sg * (1.0 + gr * (1.0 - sg)))).astype(BF16)
            dy = dgo * (gr * sg)
            dg_acc = dg_acc + jnp.sum(dy * yhat, axis=0, keepdims=True)
            t = dy * gn
            do_ref[:, sl] = r * (t - yhat * jnp.mean(t * yhat, axis=-1, keepdims=True))
        dg_ref[...] += dg_acc

    blk = pl.BlockSpec((tm, GLA_V), lambda i: (i, 0))
    vec = pl.BlockSpec((1, GLA_DV), lambda i: (0, 0))
    return pl.pallas_call(
        body, name="gla_out_bwd", grid=(S // tm,),
        in_specs=[blk, pl.BlockSpec((tm, GLA_V), lambda i: (i, (3 * ATTN_W + 2 * GLA_K + GLA_V) // GLA_V)), vec,
                  pl.BlockSpec((tm, GLA_V), lambda i: (i, 1))],
        out_specs=[blk, blk, vec],
        out_shape=[jax.ShapeDtypeStruct((S, GLA_V), F32), jax.ShapeDtypeStruct((S, GLA_V), BF16),
                   jax.ShapeDtypeStruct((1, GLA_DV), F32)],
    )(o, proj, g, dcat)


HALO = 16


def _halo_specs(tm, tn, S):
    cur = pl.BlockSpec((tm, tn), lambda j, i: (i, j))
    prev = pl.BlockSpec((HALO, tn), lambda j, i: (jnp.maximum(i * (tm // HALO) - 1, 0), j))
    nxt = pl.BlockSpec((HALO, tn), lambda j, i: (jnp.minimum((i + 1) * (tm // HALO), S // HALO - 1), j))
    return cur, prev, nxt


def _shifted(x, p_ref, n_ref, n_blocks):
    i = pl.program_id(1)
    tm = x.shape[0]
    row = lax.broadcasted_iota(jnp.int32, x.shape, 0)
    before = p_ref[HALO - 1:HALO, :].astype(F32) * (i > 0).astype(F32)
    after = n_ref[0:1, :].astype(F32) * (i < n_blocks - 1).astype(F32)
    x_m1 = jnp.where(row == 0, before, pltpu.roll(x, 1, axis=0))
    x_p1 = jnp.where(row == tm - 1, after, pltpu.roll(x, tm - 1, axis=0))
    return x_m1, x_p1


def _glu_fwd(gp, up, cw, cb, tm=512, tn=1408):
    S = gp.shape[0]
    nb = S // tm

    def body(c_ref, p_ref, n_ref, up_ref, w_ref, b_ref, o_ref):
        x = c_ref[...].astype(F32)
        x_m1, x_p1 = _shifted(x, p_ref, n_ref, nb)
        w = w_ref[...]
        gate = w[0:1, :] * x_m1 + w[1:2, :] * x + w[2:3, :] * x_p1 + b_ref[...]
        o_ref[...] = (gate * _sigmoid(gate) * up_ref[...].astype(F32)).astype(BF16)

    cur, prev, nxt = _halo_specs(tm, tn, S)
    return pl.pallas_call(
        body, name="glu_fwd", grid=(D_FF // tn, nb),
        in_specs=[cur, prev, nxt, cur, pl.BlockSpec((3, tn), lambda j, i: (0, j)), pl.BlockSpec((1, tn), lambda j, i: (0, j))],
        out_specs=cur, out_shape=jax.ShapeDtypeStruct((S, D_FF), BF16),
    )(gp, gp, gp, up, cw, cb)


def _glu_bwd(gp, up, dact, cw, cb, tm=512, tn=1408):
    S = gp.shape[0]
    nb = S // tm

    def body(c_ref, p_ref, n_ref, up_ref, da_ref, w_ref, b_ref, dup_ref, dgate_ref, dw_ref, db_ref):
        @pl.when(pl.program_id(1) == 0)
        def _():
            dw_ref[...] = jnp.zeros_like(dw_ref)
            db_ref[...] = jnp.zeros_like(db_ref)

        x = c_ref[...].astype(F32)
        x_m1, x_p1 = _shifted(x, p_ref, n_ref, nb)
        w = w_ref[...]
        gate = w[0:1, :] * x_m1 + w[1:2, :] * x + w[2:3, :] * x_p1 + b_ref[...]
        sg = _sigmoid(gate)
        da = da_ref[...].astype(F32)
        dup_ref[...] = (da * (gate * sg)).astype(BF16)
        dgate = da * up_ref[...].astype(F32) * (sg * (1.0 + gate * (1.0 - sg)))
        dgate_ref[...] = dgate.astype(BF16)
        db_ref[...] += jnp.sum(dgate, axis=0, keepdims=True)
        dw_ref[...] += jnp.concatenate(
            [jnp.sum(dgate * x_m1, axis=0, keepdims=True), jnp.sum(dgate * x, axis=0, keepdims=True),
             jnp.sum(dgate * x_p1, axis=0, keepdims=True)], axis=0)

    cur, prev, nxt = _halo_specs(tm, tn, S)
    w_spec = pl.BlockSpec((3, tn), lambda j, i: (0, j))
    b_spec = pl.BlockSpec((1, tn), lambda j, i: (0, j))
    return pl.pallas_call(
        body, name="glu_bwd", grid=(D_FF // tn, nb), in_specs=[cur, prev, nxt, cur, cur, w_spec, b_spec],
        out_specs=[cur, cur, w_spec, b_spec],
        out_shape=[jax.ShapeDtypeStruct((S, D_FF), BF16), jax.ShapeDtypeStruct((S, D_FF), BF16),
                   jax.ShapeDtypeStruct((3, D_FF), F32), jax.ShapeDtypeStruct((1, D_FF), F32)],
    )(gp, gp, gp, up, dact, cw, cb)


def _conv_bwd_input(dgate, cw, tm=512, tn=1408):
    S = dgate.shape[0]
    nb = S // tm

    def body(c_ref, p_ref, n_ref, w_ref, o_ref):
        x = c_ref[...].astype(F32)
        x_m1, x_p1 = _shifted(x, p_ref, n_ref, nb)
        w = w_ref[...]
        o_ref[...] = (w[0:1, :] * x_p1 + w[1:2, :] * x + w[2:3, :] * x_m1).astype(BF16)

    cur, prev, nxt = _halo_specs(tm, tn, S)
    return pl.pallas_call(
        body, name="conv_bwd_input", grid=(D_FF // tn, nb),
        in_specs=[cur, prev, nxt, pl.BlockSpec((3, tn), lambda j, i: (0, j))], out_specs=cur,
        out_shape=jax.ShapeDtypeStruct((S, D_FF), BF16),
    )(dgate, dgate, dgate, cw)


def _local_step(x, target, norm1_g, w_in_main, w_in_z, wg, gate_bias, gla_norm_g, attn_norm_g, w_out, norm2_g,
                w_gate4, w_up4, conv_w, conv_b, w_down, final_norm_g):
    S = x.shape[0]
    tabs = _rope_tables(S)

    n1 = _rms_fwd("rms1_fwd", x, norm1_g)
    proj = _mm_nn("in_proj", n1, w_in_main, 1024, 1536, BF16)
    z = _mm_nn("in_proj_z", n1, w_in_z, 1024, Z_PAD, BF16)
    qk = _rope_fwd(proj, tabs)
    branch = [_attn_fwd(qk, proj, d) for d in DILATIONS]
    ao, o_attn, lse = _attn_combine([b[0] for b in branch], [b[1] for b in branch], attn_norm_g)
    gates = _gates_fwd(z, wg, gate_bias)
    o_f, st_f = _gla_fwd(proj, gates, False)
    o_gla, st_b = _gla_fwd(proj, gates, True, o_prev=o_f)
    go = _gla_out_fwd(o_gla, proj, gla_norm_g)
    cat = jnp.concatenate([ao, go], axis=1)
    h1 = _mm_nn("out_proj", cat, w_out, 1024, 1024, F32, res=x)
    n2 = _rms_fwd("rms2_fwd", h1, norm2_g)
    gp = _mm_nn_sharded("ffn_gate", n2, w_gate4, 1024, BF16)
    up = _mm_nn_sharded("ffn_up", n2, w_up4, 1024, BF16)
    act = _glu_fwd(gp, up, conv_w, conv_b)
    tk = D_FF // N_CHIPS
    h2 = _matmul(
        "ffn_down",
        [(act, pl.BlockSpec((1024, tk), lambda i, j, k: (i, k)), w_down, pl.BlockSpec((tk, 1024), lambda i, j, k: (k, j)), NN)],
        (S // 1024, D_MODEL // 1024, N_CHIPS), jax.ShapeDtypeStruct((S, D_MODEL), F32),
        pl.BlockSpec((1024, 1024), lambda i, j, k: (i, j)), N_CHIPS,
        res=(h1, pl.BlockSpec((1024, 1024), lambda i, j, k: (i, j))))
    loss_row, d_final_g, dh2, dh2_b = _final_loss(h2, final_norm_g.reshape(1, D_MODEL), target)

    dact = _mm_nt("ffn_down_bwd", dh2_b, w_down, 1024, tk, BF16)
    dup, dgate, d_conv_w, d_conv_b = _glu_bwd(gp, up, dact, conv_w, conv_b)
    dgp = _conv_bwd_input(dgate, conv_w)
    d_w_down = _mm_tn("ffn_down_wgrad", act, dh2_b, tk, D_MODEL, 1024, BF16)
    d_w_gate4 = _mm_tn("ffn_gate_wgrad", n2, dgp, D_MODEL, tk, 1024, BF16, out3=tk)
    d_w_up4 = _mm_tn("ffn_up_wgrad", n2, dup, D_MODEL, tk, 1024, BF16, out3=tk)
    dn2 = _matmul(
        "ffn_in_bwd",
        [(dgp, pl.BlockSpec((1024, tk), lambda i, j, k: (i, k)), w_gate4, pl.BlockSpec((None, 1024, tk), lambda i, j, k: (k, j, 0)), NT),
         (dup, pl.BlockSpec((1024, tk), lambda i, j, k: (i, k)), w_up4, pl.BlockSpec((None, 1024, tk), lambda i, j, k: (k, j, 0)), NT)],
        (S // 1024, D_MODEL // 1024, N_CHIPS), jax.ShapeDtypeStruct((S, D_MODEL), F32),
        pl.BlockSpec((1024, 1024), lambda i, j, k: (i, j)), N_CHIPS)
    dh1, dh1_b, d_norm2_g = _rms_bwd("rms2_bwd", h1, norm2_g, dn2, dh2)

    d_w_out = _mm_tn("out_proj_wgrad", cat, dh1_b, D_MODEL, 1024, 1024, BF16)
    dcat = _mm_nt("out_proj_bwd", dh1_b, w_out, 1024, 1024, BF16)
    do_attn, delta, d_attn_norm_g = _attn_norm_bwd(o_attn, attn_norm_g, dcat)
    dqs, dks, dvs = [], [], []
    for d in DILATIONS:
        dqs.append(_attn_bwd_dq(qk, proj, do_attn, lse, delta, d))
        dk, dv = _attn_bwd_dkv(qk, proj, do_attn, lse, delta, d)
        dks.append(dk)
        dvs.append(dv)
    d_attn = _attn_grad_merge(dqs, dks, dvs, tabs)
    do_gla, dgr, d_gla_norm_g = _gla_out_bwd(o_gla, proj, gla_norm_g, dcat)
    dq_f, dk_f, dv_f, dg_f = _gla_bwd(proj, gates, st_f, do_gla, False)
    dgq, dgk, dgv, dg_b = _gla_bwd(proj, gates, st_b, do_gla, True, prev=(dq_f, dk_f, dv_f))
    dz, d_wg, d_gate_bias = _gates_bwd(z, wg, gate_bias, dg_f, dg_b)
    dproj = jnp.concatenate([d_attn, dgq, dgk, dgv, dgr], axis=1)
    d_w_in_main = _mm_tn("in_proj_wgrad", n1, dproj, D_MODEL, 1536, 1024, BF16)
    d_w_in_z = _mm_tn("in_proj_z_wgrad", n1, dz, D_MODEL, Z_PAD, 1024, BF16)
    tkm = IN_MAIN // 4
    dn1 = _matmul(
        "in_proj_bwd",
        [(dproj, pl.BlockSpec((1024, tkm), lambda i, j, k: (i, k)), w_in_main, pl.BlockSpec((1024, tkm), lambda i, j, k: (j, k)), NT)],
        (S // 1024, D_MODEL // 1024, 4), jax.ShapeDtypeStruct((S, D_MODEL), F32),
        pl.BlockSpec((1024, 1024), lambda i, j, k: (i, j)), 4)
    dn1 = _mm_nt("in_proj_z_bwd", dz, w_in_z, 1024, 1024, F32, res=dn1)
    grad_x, _, d_norm1_g = _rms_bwd("rms1_bwd", x, norm1_g, dn1, dh1)

    big = dict(w_in_main=d_w_in_main, w_in_z=d_w_in_z, w_out=d_w_out, w_gate4=d_w_gate4, w_up4=d_w_up4, w_down=d_w_down)
    small = dict(loss=loss_row, norm1_g=d_norm1_g, wg=d_wg, gate_bias=d_gate_bias, gla_norm_g=d_gla_norm_g,
                 attn_norm_g=d_attn_norm_g, norm2_g=d_norm2_g, conv_w=d_conv_w, conv_b=d_conv_b, final_norm_g=d_final_g)
    return grad_x, big, small


def _position():
    return lax.axis_index("x"), lax.axis_index("y"), lax.axis_index("c")


def _other_chips(x, y):
    return [(1 - x, y), (x, 1 - y), (1 - x, 1 - y)]


def _gather_chips(name, shards):
    n = len(shards)

    def body(*refs):
        ins, outs = refs[:n], refs[n:2 * n]
        send, recv, loc = refs[2 * n:]
        x, y, c = _position()
        me = 2 * x + y
        chips = _other_chips(x, y)
        started = []
        for w in range(n):
            own = pltpu.make_async_copy(ins[w], outs[w].at[me], loc.at[w])
            own.start()
            started.append(own)
        sends = []
        for w in range(n):
            for j, (px, py) in enumerate(chips):
                cp = pltpu.make_async_remote_copy(ins[w], outs[w].at[me], send.at[3 * w + j], recv.at[3 * w + j],
                                                  device_id=(px, py, c), device_id_type=MESH)
                cp.start()
                sends.append(cp)
        for w in range(n):
            for j, (px, py) in enumerate(chips):
                pltpu.make_async_remote_copy(ins[w], outs[w].at[2 * px + py], send.at[3 * w + j], recv.at[3 * w + j],
                                             device_id=(px, py, c), device_id_type=MESH).wait_recv()
        for cp in sends:
            cp.wait_send()
        for own in started:
            own.wait()

    return pl.pallas_call(
        body, name=name, in_specs=[ANY] * n, out_specs=[ANY] * n,
        out_shape=[jax.ShapeDtypeStruct((N_CHIPS,) + s.shape, s.dtype) for s in shards],
        scratch_shapes=[pltpu.SemaphoreType.DMA((3 * n,)), pltpu.SemaphoreType.DMA((3 * n,)), pltpu.SemaphoreType.DMA((n,))],
    )(*shards)


def _sibling_exchange(name, arrs):
    n = len(arrs)

    def body(*refs):
        ins, outs = refs[:n], refs[n:2 * n]
        send, recv = refs[2 * n:]
        x, y, c = _position()
        copies = [pltpu.make_async_remote_copy(ins[w], outs[w], send.at[w], recv.at[w], device_id=(x, y, 1 - c),
                                               device_id_type=MESH) for w in range(n)]
        for cp in copies:
            cp.start()
        for cp in copies:
            cp.wait()

    return pl.pallas_call(
        body, name=name, in_specs=[ANY] * n, out_specs=[ANY] * n,
        out_shape=[jax.ShapeDtypeStruct(a.shape, a.dtype) for a in arrs],
        scratch_shapes=[pltpu.SemaphoreType.DMA((n,)), pltpu.SemaphoreType.DMA((n,))],
    )(*arrs)


def _scatter_chips(name, parts):
    n = len(parts)

    def body(*refs):
        ins, outs = refs[:n], refs[n:2 * n]
        send, recv, loc = refs[2 * n:]
        x, y, c = _position()
        me = 2 * x + y
        chips = _other_chips(x, y)
        started = []
        for w in range(n):
            own = pltpu.make_async_copy(ins[w].at[me], outs[w].at[me], loc.at[w])
            own.start()
            started.append(own)
        sends = []
        for w in range(n):
            for j, (px, py) in enumerate(chips):
                cp = pltpu.make_async_remote_copy(ins[w].at[2 * px + py], outs[w].at[me], send.at[3 * w + j],
                                                  recv.at[3 * w + j], device_id=(px, py, c), device_id_type=MESH)
                cp.start()
                sends.append(cp)
        for w in range(n):
            for j, (px, py) in enumerate(chips):
                pltpu.make_async_remote_copy(ins[w].at[me], outs[w].at[2 * px + py], send.at[3 * w + j], recv.at[3 * w + j],
                                             device_id=(px, py, c), device_id_type=MESH).wait_recv()
        for cp in sends:
            cp.wait_send()
        for own in started:
            own.wait()

    return pl.pallas_call(
        body, name=name, in_specs=[ANY] * n, out_specs=[ANY] * n,
        out_shape=[jax.ShapeDtypeStruct(p.shape, p.dtype) for p in parts],
        scratch_shapes=[pltpu.SemaphoreType.DMA((3 * n,)), pltpu.SemaphoreType.DMA((3 * n,)), pltpu.SemaphoreType.DMA((n,))],
    )(*parts)


def _allreduce_rows(buf):
    R = buf.shape[0]

    def body(in_ref, out_ref, land, send, recv):
        x, y, c = _position()
        me = 4 * x + 2 * y + c
        land[pl.ds(me, 1)] = in_ref[...][None]
        peers = []
        for mask in range(1, N_DEV):
            px = 1 - x if mask & 4 else x
            py = 1 - y if mask & 2 else y
            pc = 1 - c if mask & 1 else c
            peers.append((px, py, pc))
        sends = []
        for k, peer in enumerate(peers):
            cp = pltpu.make_async_remote_copy(in_ref, land.at[me], send.at[k], recv.at[k], device_id=peer, device_id_type=MESH)
            cp.start()
            sends.append(cp)
        for k, (px, py, pc) in enumerate(peers):
            pltpu.make_async_remote_copy(in_ref, land.at[4 * px + 2 * py + pc], send.at[k], recv.at[k],
                                         device_id=(px, py, pc), device_id_type=MESH).wait_recv()
        for cp in sends:
            cp.wait_send()
        tot = land[0]
        for i in range(1, N_DEV):
            tot = tot + land[i]
        out_ref[...] = tot

    vm = pl.BlockSpec(memory_space=pltpu.VMEM)
    return pl.pallas_call(
        body, name="allreduce_small", in_specs=[vm], out_specs=vm, out_shape=jax.ShapeDtypeStruct((R, LANES), F32),
        scratch_shapes=[pltpu.VMEM((N_DEV, R, LANES), F32), pltpu.SemaphoreType.DMA((N_DEV - 1,)),
                        pltpu.SemaphoreType.DMA((N_DEV - 1,))],
    )(buf)


def _pair_sum(name, a, b, tr=512):
    shape = a.shape
    rows, cols = shape[0] * shape[1], shape[2]

    def body(a_ref, b_ref, o_ref):
        o_ref[...] = (a_ref[...].astype(F32) + b_ref[...].astype(F32)).astype(BF16)

    blk = pl.BlockSpec((tr, cols), lambda i: (i, 0))
    out = pl.pallas_call(
        body, name=name, grid=(rows // tr,), in_specs=[blk, blk], out_specs=blk,
        out_shape=jax.ShapeDtypeStruct((rows, cols), BF16),
    )(a.reshape(rows, cols), b.reshape(rows, cols))
    return out.reshape(shape)


def _adamw_math(w, m, v, g):
    m2 = ADAM_B1 * m + (1.0 - ADAM_B1) * g
    v2 = ADAM_B2 * v + (1.0 - ADAM_B2) * (g * g)
    m_hat = m2 / (1.0 - ADAM_B1 ** ADAM_STEP)
    v_hat = v2 / (1.0 - ADAM_B2 ** ADAM_STEP)
    delta = -ADAM_LR * (m_hat / (jnp.sqrt(v_hat) + ADAM_EPS) + ADAM_WD * w)
    return delta, m2, v2


def _adamw(name, w, m, v, g):
    r, c = w.shape
    stacked = g.ndim == 3
    tr = r if r <= 256 else (256 if r % 256 == 0 else 128)

    def body(w_ref, m_ref, v_ref, g_ref, go_ref, d_ref, m2_ref, v2_ref):
        if stacked:
            gv = g_ref[0].astype(F32)
            for i in range(1, N_CHIPS):
                gv = gv + g_ref[i].astype(F32)
        else:
            gv = g_ref[...]
        delta, m2, v2 = _adamw_math(w_ref[...], m_ref[...], v_ref[...], gv)
        go_ref[...] = gv
        d_ref[...] = delta
        m2_ref[...] = m2
        v2_ref[...] = v2

    blk = pl.BlockSpec((tr, c), lambda i: (i, 0))
    g_spec = pl.BlockSpec((N_CHIPS, tr, c), lambda i: (0, i, 0)) if stacked else blk
    out = jax.ShapeDtypeStruct((r, c), F32)
    return pl.pallas_call(
        body, name=name, grid=(r // tr,), in_specs=[blk, blk, blk, g_spec], out_specs=[blk] * 4, out_shape=[out] * 4,
    )(w, m, v, g)


def _pack_rows(pieces):
    flat = jnp.concatenate([p.reshape(-1) for p in pieces])
    rows = flat.shape[0] // LANES
    pad = (-rows) % 8
    return jnp.pad(flat.reshape(rows, LANES), ((0, pad), (0, 0)))


def _unpack_rows(buf, shapes):
    flat = buf.reshape(-1)
    out, at = [], 0
    for s in shapes:
        size = math.prod(s)
        out.append(flat[at:at + size].reshape(s))
        at += size
    return out


SMALL_NAMES = ("norm1_g", "gf_up", "gf_b", "gb_up", "gb_b", "gla_norm_g", "attn_norm_g", "norm2_g", "conv_w", "conv_b",
               "final_norm_g")
BIG_NAMES = ("w_in", "w_out", "w_gate", "w_up", "w_down")
WEIGHT_ORDER = ("norm1_g", "w_in", "gf_up", "gf_b", "gb_up", "gb_b", "gla_norm_g", "attn_norm_g", "w_out", "norm2_g",
                "w_gate", "w_up", "conv_w", "conv_b", "w_down", "final_norm_g")


def kernel(x, norm1_g, w_in, gf_up, gf_b, gb_up, gb_b, gla_norm_g, attn_norm_g, w_out, norm2_g, w_gate, w_up, conv_w, conv_b, w_down, final_norm_g, loss_target, m_norm1_g, m_w_in, m_gf_up, m_gf_b, m_gb_up, m_gb_b, m_gla_norm_g, m_attn_norm_g, m_w_out, m_norm2_g, m_w_gate, m_w_up, m_conv_w, m_conv_b, m_w_down, m_final_norm_g, v_norm1_g, v_w_in, v_gf_up, v_gf_b, v_gb_up, v_gb_b, v_gla_norm_g, v_attn_norm_g, v_w_out, v_norm2_g, v_w_gate, v_w_up, v_conv_w, v_conv_b, v_w_down, v_final_norm_g):
    w = dict(norm1_g=norm1_g, w_in=w_in, gf_up=gf_up, gf_b=gf_b, gb_up=gb_up, gb_b=gb_b, gla_norm_g=gla_norm_g,
             attn_norm_g=attn_norm_g, w_out=w_out, norm2_g=norm2_g, w_gate=w_gate, w_up=w_up, conv_w=conv_w, conv_b=conv_b,
             w_down=w_down, final_norm_g=final_norm_g)
    m = dict(norm1_g=m_norm1_g, w_in=m_w_in, gf_up=m_gf_up, gf_b=m_gf_b, gb_up=m_gb_up, gb_b=m_gb_b, gla_norm_g=m_gla_norm_g,
             attn_norm_g=m_attn_norm_g, w_out=m_w_out, norm2_g=m_norm2_g, w_gate=m_w_gate, w_up=m_w_up, conv_w=m_conv_w,
             conv_b=m_conv_b, w_down=m_w_down, final_norm_g=m_final_norm_g)
    v = dict(norm1_g=v_norm1_g, w_in=v_w_in, gf_up=v_gf_up, gf_b=v_gf_b, gb_up=v_gb_up, gb_b=v_gb_b, gla_norm_g=v_gla_norm_g,
             attn_norm_g=v_attn_norm_g, w_out=v_w_out, norm2_g=v_norm2_g, w_gate=v_w_gate, w_up=v_w_up, conv_w=v_conv_w,
             conv_b=v_conv_b, w_down=v_w_down, final_norm_g=v_final_norm_g)
    S = x.shape[1]
    chip = 2 * lax.axis_index("x") + lax.axis_index("y")
    n_in = IN_W // N_CHIPS
    n_ff = D_FF // N_CHIPS
    n_gk = GLA_K // N_CHIPS

    gathered = _gather_chips("gather_weights", [w[k][0].astype(BF16) for k in BIG_NAMES])
    w_in4, w_out4, w_gate4, w_up4, w_down4 = gathered
    w_in_full = jnp.transpose(w_in4, (1, 0, 2)).reshape(D_MODEL, IN_W)
    w_in_main = w_in_full[:, :IN_MAIN]
    w_in_z = jnp.pad(w_in_full[:, IN_MAIN:], ((0, 0), (0, Z_PAD - (IN_W - IN_MAIN))))
    small_shard = _pack_rows([gf_up[0], gb_up[0], conv_w[0]])
    (small4,) = _gather_chips("gather_small", [small_shard])
    rows_up = GATE_RANK * n_gk // LANES
    rows_cw = 3 * n_ff // LANES
    gf_full = jnp.transpose(small4[:, 0:rows_up].reshape(N_CHIPS, GATE_RANK, n_gk), (1, 0, 2)).reshape(GATE_RANK, GLA_K)
    gb_full = jnp.transpose(small4[:, rows_up:2 * rows_up].reshape(N_CHIPS, GATE_RANK, n_gk), (1, 0, 2)).reshape(GATE_RANK, GLA_K)
    cw_full = jnp.transpose(small4[:, 2 * rows_up:2 * rows_up + rows_cw].reshape(N_CHIPS, 3, n_ff), (1, 0, 2)).reshape(3, D_FF)
    wg = jnp.zeros((Z_PAD, 2 * GLA_K), F32)
    wg = wg.at[0:GATE_RANK, 0:GLA_K].set(gf_full).at[GATE_RANK:2 * GATE_RANK, GLA_K:].set(gb_full).astype(BF16)
    gate_bias = jnp.concatenate([gf_b, gb_b], axis=1)

    grad_x, big, small = _local_step(
        x[0], loss_target[0], norm1_g, w_in_main, w_in_z, wg, gate_bias, gla_norm_g, attn_norm_g,
        w_out4.reshape(D_MODEL, D_MODEL), norm2_g, w_gate4, w_up4, cw_full, conv_b, w_down4.reshape(D_FF, D_MODEL), final_norm_g)

    d_w_in = jnp.concatenate([big["w_in_main"], big["w_in_z"][:, :IN_W - IN_MAIN]], axis=1)
    mine = [jnp.transpose(d_w_in.reshape(D_MODEL, N_CHIPS, n_in), (1, 0, 2)),
            big["w_out"].reshape(N_CHIPS, D_MODEL // N_CHIPS, D_MODEL), big["w_gate4"], big["w_up4"],
            big["w_down"].reshape(N_CHIPS, n_ff, D_MODEL)]
    theirs = _sibling_exchange("grads_to_sibling", mine)
    chip_sums = [_pair_sum(f"pair_sum_{k}", a, b) for k, a, b in zip(BIG_NAMES, mine, theirs)]
    contributions = _scatter_chips("grads_to_chips", chip_sums)
    out = {}
    for k, parts in zip(BIG_NAMES, contributions):
        res = _adamw(f"adamw_{k}", w[k][0], m[k][0], v[k][0], parts)
        out[k] = [r[None] for r in res]

    d_gf_up = small["wg"][0:GATE_RANK, 0:GLA_K]
    d_gb_up = small["wg"][GATE_RANK:2 * GATE_RANK, GLA_K:]
    pieces = [small["loss"], small["norm1_g"], d_gf_up, small["gate_bias"][:, :GLA_K], d_gb_up, small["gate_bias"][:, GLA_K:],
              small["gla_norm_g"], small["attn_norm_g"], small["norm2_g"], small["conv_w"], small["conv_b"], small["final_norm_g"]]
    total = _allreduce_rows(_pack_rows(pieces))
    summed = _unpack_rows(total, [p.shape for p in pieces])
    loss = summed[0][0, 0]
    g_small = dict(zip(SMALL_NAMES, summed[1:]))
    g_small["gf_up"] = lax.dynamic_slice_in_dim(g_small["gf_up"], chip * n_gk, n_gk, axis=1)
    g_small["gb_up"] = lax.dynamic_slice_in_dim(g_small["gb_up"], chip * n_gk, n_gk, axis=1)
    g_small["conv_w"] = lax.dynamic_slice_in_dim(g_small["conv_w"], chip * n_ff, n_ff, axis=1)
    packed = [_pack_rows([t[k] for k in SMALL_NAMES]) for t in (w, m, v, g_small)]
    res = _adamw("adamw_small", *packed)
    shapes = [w[k].shape for k in SMALL_NAMES]
    for k, vals in zip(SMALL_NAMES, zip(*[_unpack_rows(r, shapes) for r in res])):
        out[k] = list(vals)

    grads, deltas, new_m, new_v = ([out[k][i] for k in WEIGHT_ORDER] for i in range(4))
    return (loss, grad_x[None], *grads, *deltas, *new_m, *new_v)
```

```python
import functools
import math

import jax
import jax.numpy as jnp
from jax import lax
from jax.experimental import pallas as pl
from jax.experimental.pallas import tpu as pltpu
from jax.experimental.pallas import tpu_sc as plsc

F32 = jnp.float32
BF16 = jnp.bfloat16

D_MODEL = 2048
ATTN_W = 1024
HEAD = 128
N_HEADS = 8
N_SIDE = 64
DILATIONS = (1, 4, 16)
ROPE_THETA = 500000.0
ROPE_DIM = 32
GLA_K = 512
GLA_V = 1024
GLA_HEADS = 4
GLA_DK = 128
GLA_DV = 256
GATE_RANK = 16
GATE_NORM = 16.0
CHUNK = 64
IN_MAIN = 6144
IN_W = 6176
Z_PAD = 128
D_FF = 5632
EPS = 1e-6
N_CHIPS = 4
N_DEV = 8
LANES = 128

ADAM_LR = 0.001
ADAM_B1 = 0.9
ADAM_B2 = 0.999
ADAM_EPS = 1e-08
ADAM_WD = 0.01
ADAM_STEP = 10

NEG = -1e30
MESH = pl.DeviceIdType.MESH
ANY = pl.BlockSpec(memory_space=pl.ANY)

NN = ((1,), (0,))
NT = ((1,), (1,))
TN = ((0,), (0,))


def _dot(a, b, dims=NN):
    return lax.dot_general(a, b, (dims, ((), ())), preferred_element_type=F32)


def _sigmoid(x):
    return 1.0 / (1.0 + jnp.exp(-x))


def _matmul(name, pairs, grid, out_shape, out_spec, nk, res=None):
    n_in = 2 * len(pairs) + (res is not None)
    dims = [p[4] for p in pairs]

    def body(*refs):
        ins, o_ref = refs[:n_in], refs[n_in]

        def partial_sum():
            tot = None
            for p, dn in enumerate(dims):
                a, b = ins[2 * p][...], ins[2 * p + 1][...]
                t = _dot(a.astype(BF16), b.astype(BF16), dn)
                tot = t if tot is None else tot + t
            return tot

        if nk == 1:
            t = partial_sum()
            if res is not None:
                t = t + ins[-1][...]
            o_ref[...] = t.astype(o_ref.dtype)
        else:
            acc_ref = refs[n_in + 1]
            k = pl.program_id(2)

            @pl.when(k == 0)
            def _():
                if res is not None:
                    acc_ref[...] = ins[-1][...]
                else:
                    acc_ref[...] = jnp.zeros_like(acc_ref)

            acc_ref[...] += partial_sum()

            @pl.when(k == nk - 1)
            def _():
                o_ref[...] = acc_ref[...].astype(o_ref.dtype)

    operands, in_specs = [], []
    for a, a_spec, b, b_spec, _ in pairs:
        operands += [a, b]
        in_specs += [a_spec, b_spec]
    if res is not None:
        operands.append(res[0])
        in_specs.append(res[1])
    acc_shape = tuple(s for s in out_spec.block_shape if s is not None)
    scratch = [pltpu.VMEM(acc_shape, F32)] if nk > 1 else []
    return pl.pallas_call(
        body, name=name, grid=grid, in_specs=in_specs, out_specs=out_spec, out_shape=out_shape, scratch_shapes=scratch,
    )(*operands)


def _mm_nn(name, a, b, tm, tn, out_dtype, res=None):
    M, K = a.shape
    N = b.shape[1]
    pairs = [(a, pl.BlockSpec((tm, K), lambda j, i: (i, 0)), b, pl.BlockSpec((K, tn), lambda j, i: (0, j)), NN)]
    r = None if res is None else (res, pl.BlockSpec((tm, tn), lambda j, i: (i, j)))
    return _matmul(name, pairs, (N // tn, M // tm), jax.ShapeDtypeStruct((M, N), out_dtype),
                   pl.BlockSpec((tm, tn), lambda j, i: (i, j)), 1, r)


def _mm_nn_sharded(name, a, b4, tm, out_dtype):
    M, K = a.shape
    n = b4.shape[2]
    pairs = [(a, pl.BlockSpec((tm, K), lambda j, i: (i, 0)), b4, pl.BlockSpec((None, K, n), lambda j, i: (j, 0, 0)), NN)]
    return _matmul(name, pairs, (N_CHIPS, M // tm), jax.ShapeDtypeStruct((M, N_CHIPS * n), out_dtype),
                   pl.BlockSpec((tm, n), lambda j, i: (i, j)), 1)


def _mm_nt(name, a, b, tm, tn, out_dtype, res=None):
    M, K = a.shape
    N = b.shape[0]
    pairs = [(a, pl.BlockSpec((tm, K), lambda j, i: (i, 0)), b, pl.BlockSpec((tn, K), lambda j, i: (j, 0)), NT)]
    r = None if res is None else (res, pl.BlockSpec((tm, tn), lambda j, i: (i, j)))
    return _matmul(name, pairs, (N // tn, M // tm), jax.ShapeDtypeStruct((M, N), out_dtype),
                   pl.BlockSpec((tm, tn), lambda j, i: (i, j)), 1, r)


def _mm_tn(name, a, g, tka, tn, tmm, out_dtype, out3=None):
    M, Ka = a.shape
    N = g.shape[1]
    pairs = [(a, pl.BlockSpec((tmm, tka), lambda i, j, k: (k, i)), g, pl.BlockSpec((tmm, tn), lambda i, j, k: (k, j)), TN)]
    if out3 is None:
        shape, spec = (Ka, N), pl.BlockSpec((tka, tn), lambda i, j, k: (i, j))
    else:
        shape, spec = (N // out3, Ka, out3), pl.BlockSpec((None, tka, tn), lambda i, j, k: (j, i, 0))
    return _matmul(name, pairs, (Ka // tka, N // tn, M // tmm), jax.ShapeDtypeStruct(shape, out_dtype), spec, M // tmm)


def _rms_fwd(name, x, g, tm=512):
    S, D = x.shape

    def body(x_ref, g_ref, o_ref):
        xv = x_ref[...]
        r = lax.rsqrt(jnp.mean(xv * xv, axis=-1, keepdims=True) + EPS)
        o_ref[...] = (xv * r * g_ref[...]).astype(o_ref.dtype)

    return pl.pallas_call(
        body, name=name, grid=(S // tm,),
        in_specs=[pl.BlockSpec((tm, D), lambda i: (i, 0)), pl.BlockSpec((1, D), lambda i: (0, 0))],
        out_specs=pl.BlockSpec((tm, D), lambda i: (i, 0)), out_shape=jax.ShapeDtypeStruct((S, D), BF16),
    )(x, g)


def _rms_bwd(name, x, g, dn, dres, tm=512):
    S, D = x.shape

    def body(x_ref, g_ref, dn_ref, dres_ref, dx_ref, dxb_ref, dg_ref):
        i = pl.program_id(0)

        @pl.when(i == 0)
        def _():
            dg_ref[...] = jnp.zeros_like(dg_ref)

        xv = x_ref[...]
        r = lax.rsqrt(jnp.mean(xv * xv, axis=-1, keepdims=True) + EPS)
        xhat = xv * r
        dnv = dn_ref[...].astype(F32)
        dg_ref[...] += jnp.sum(dnv * xhat, axis=0, keepdims=True)
        t = dnv * g_ref[...]
        dx = r * (t - xhat * jnp.mean(t * xhat, axis=-1, keepdims=True)) + dres_ref[...]
        dx_ref[...] = dx
        dxb_ref[...] = dx.astype(BF16)

    row = pl.BlockSpec((tm, D), lambda i: (i, 0))
    vec = pl.BlockSpec((1, D), lambda i: (0, 0))
    return pl.pallas_call(
        body, name=name, grid=(S // tm,), in_specs=[row, vec, row, row], out_specs=[row, row, vec],
        out_shape=[jax.ShapeDtypeStruct((S, D), F32), jax.ShapeDtypeStruct((S, D), BF16), jax.ShapeDtypeStruct((1, D), F32)],
    )(x, g, dn, dres)


def _final_loss(h2, g, target, tm=512):
    S, D = h2.shape

    def body(x_ref, g_ref, t_ref, loss_ref, dg_ref, dx_ref, dxb_ref):
        i = pl.program_id(0)

        @pl.when(i == 0)
        def _():
            loss_ref[...] = jnp.zeros_like(loss_ref)
            dg_ref[...] = jnp.zeros_like(dg_ref)

        xv = x_ref[...]
        r = lax.rsqrt(jnp.mean(xv * xv, axis=-1, keepdims=True) + EPS)
        xhat = xv * r
        gv = g_ref[...]
        diff = xhat * gv - t_ref[...]
        per_tok = jnp.mean(diff * diff, axis=-1, keepdims=True)
        loss_ref[...] += 0.5 * jnp.sum(per_tok, axis=0, keepdims=True)
        dy = diff * (1.0 / D)
        dg_ref[...] += jnp.sum(dy * xhat, axis=0, keepdims=True)
        t = dy * gv
        dx = r * (t - xhat * jnp.mean(t * xhat, axis=-1, keepdims=True))
        dx_ref[...] = dx
        dxb_ref[...] = dx.astype(BF16)

    row = pl.BlockSpec((tm, D), lambda i: (i, 0))
    vec = pl.BlockSpec((1, D), lambda i: (0, 0))
    return pl.pallas_call(
        body, name="final_loss", grid=(S // tm,), in_specs=[row, vec, row],
        out_specs=[pl.BlockSpec((1, LANES), lambda i: (0, 0)), vec, row, row],
        out_shape=[jax.ShapeDtypeStruct((1, LANES), F32), jax.ShapeDtypeStruct((1, D), F32),
                   jax.ShapeDtypeStruct((S, D), F32), jax.ShapeDtypeStruct((S, D), BF16)],
    )(h2, g, target)


def _rope_tables(S):
    pos = jnp.arange(S, dtype=F32)
    inv_freq = ROPE_THETA ** (-jnp.arange(0, ROPE_DIM, 2, dtype=F32) / ROPE_DIM)
    ang = pos[:, None] * inv_freq[None, :]
    cos, sin = jnp.cos(ang), jnp.sin(ang)
    half = ROPE_DIM // 2
    rest = HEAD - ROPE_DIM
    z_h, z_r = jnp.zeros((S, half), F32), jnp.zeros((S, rest), F32)
    tab_c = jnp.concatenate([cos, cos, jnp.ones((S, rest), F32)], axis=1)
    tab_up = jnp.concatenate([z_h, sin, z_r], axis=1)
    tab_dn = jnp.concatenate([-sin, z_h, z_r], axis=1)
    return tab_c, tab_up, tab_dn


def _rope_head(t, c, up, dn):
    half = ROPE_DIM // 2
    return t * c + pltpu.roll(t, half, axis=1) * up + pltpu.roll(t, HEAD - half, axis=1) * dn


def _rope_fwd(proj, tabs, tm=512):
    S = proj.shape[0]
    W = 2 * ATTN_W

    def body(p_ref, c_ref, up_ref, dn_ref, o_ref):
        c, up, dn = c_ref[...], up_ref[...], dn_ref[...]
        for h in range(W // HEAD):
            sl = slice(h * HEAD, (h + 1) * HEAD)
            o_ref[:, sl] = _rope_head(p_ref[:, sl].astype(F32), c, up, dn).astype(BF16)

    tab = pl.BlockSpec((tm, HEAD), lambda i: (i, 0))
    return pl.pallas_call(
        body, name="rope_fwd", grid=(S // tm,), in_specs=[pl.BlockSpec((tm, W), lambda i: (i, 0)), tab, tab, tab],
        out_specs=pl.BlockSpec((tm, W), lambda i: (i, 0)), out_shape=jax.ShapeDtypeStruct((S, W), BF16),
    )(proj, *tabs)


def _attn_grad_merge(dqs, dks, dvs, tabs, tm=256):
    S = dqs[0].shape[0]

    def body(*refs):
        q_refs, k_refs, v_refs = refs[0:3], refs[3:6], refs[6:9]
        c, up, dn = refs[9][...], refs[10][...], refs[11][...]
        o_ref = refs[12]
        for h in range(N_HEADS):
            sl = slice(h * HEAD, (h + 1) * HEAD)
            for part, rs in ((0, q_refs), (1, k_refs)):
                t = rs[0][:, sl].astype(F32) + rs[1][:, sl].astype(F32) + rs[2][:, sl].astype(F32)
                osl = slice(part * ATTN_W + h * HEAD, part * ATTN_W + (h + 1) * HEAD)
                o_ref[:, osl] = _rope_head(t, c, -up, -dn).astype(BF16)
        o_ref[:, 2 * ATTN_W:] = (v_refs[0][...].astype(F32) + v_refs[1][...].astype(F32)
                                 + v_refs[2][...].astype(F32)).astype(BF16)

    blk = pl.BlockSpec((tm, ATTN_W), lambda i: (i, 0))
    tab = pl.BlockSpec((tm, HEAD), lambda i: (i, 0))
    return pl.pallas_call(
        body, name="attn_grad_merge", grid=(S // tm,), in_specs=[blk] * 9 + [tab] * 3,
        out_specs=pl.BlockSpec((tm, 3 * ATTN_W), lambda i: (i, 0)), out_shape=jax.ShapeDtypeStruct((S, 3 * ATTN_W), BF16),
    )(*dqs, *dks, *dvs, *tabs)


def _band_specs(T, L, width, ncol, col):
    ratio = T // N_SIDE
    nhb = L // N_SIDE
    cur = pl.BlockSpec((T, width), lambda r, i: (i, r * ncol + col))
    prev = pl.BlockSpec((N_SIDE, width), lambda r, i: (jnp.maximum(i * ratio - 1, 0), r * ncol + col))
    nxt = pl.BlockSpec((N_SIDE, width), lambda r, i: (jnp.minimum((i + 1) * ratio, nhb - 1), r * ncol + col))
    return cur, prev, nxt


def _band_mask(T, L):
    i = pl.program_id(1)
    row = lax.broadcasted_iota(jnp.int32, (T, T + 2 * N_SIDE), 0)
    col = lax.broadcasted_iota(jnp.int32, (T, T + 2 * N_SIDE), 1)
    pos = i * T - N_SIDE + col
    return (col >= row) & (col <= row + 2 * N_SIDE) & (pos >= 0) & (pos < L)


def _window(p_ref, c_ref, n_ref, sl=None):
    if sl is None:
        return jnp.concatenate([p_ref[...], c_ref[...], n_ref[...]], axis=0)
    return jnp.concatenate([p_ref[:, sl], c_ref[:, sl], n_ref[:, sl]], axis=0)


def _attn_dims(S, d):
    L = S // d
    T = min(128, L)
    return L, T, (d, L // T)


def _v_view(proj, L, d):
    if d == 1:
        return proj, IN_MAIN // ATTN_W, 2
    return proj[:, 2 * ATTN_W:3 * ATTN_W].reshape(L, d * ATTN_W), 1, 0


def _attn_fwd(qk, proj, d):
    S = qk.shape[0]
    L, T, grid = _attn_dims(S, d)
    scale = HEAD ** -0.5
    qk_v = qk.reshape(L, d * 2 * ATTN_W)
    pj_v, ncol_p, vcol = _v_view(proj, L, d)

    def body(q_ref, kp, kc, kn, vp, vc, vn, o_ref, lse_ref):
        mask = _band_mask(T, L)
        lane = lax.broadcasted_iota(jnp.int32, (T, LANES), 1)
        lse_tile = jnp.zeros((T, LANES), F32)
        for h in range(N_HEADS):
            sl = slice(h * HEAD, (h + 1) * HEAD)
            kw, vw = _window(kp, kc, kn, sl), _window(vp, vc, vn, sl)
            s = jnp.where(mask, _dot(q_ref[:, sl], kw, NT) * scale, NEG)
            m = jnp.max(s, axis=1, keepdims=True)
            p = jnp.exp(s - m)
            l = jnp.sum(p, axis=1, keepdims=True)
            o_ref[:, sl] = (_dot(p.astype(BF16), vw) / l).astype(BF16)
            lse_tile = jnp.where(lane == h, m + jnp.log(l), lse_tile)
        lse_ref[...] = lse_tile

    q_cur, _, _ = _band_specs(T, L, ATTN_W, 2, 0)
    k_specs = _band_specs(T, L, ATTN_W, 2, 1)
    v_specs = _band_specs(T, L, ATTN_W, ncol_p, vcol)
    o, lse = pl.pallas_call(
        body, name=f"attn_fwd_d{d}", grid=grid,
        in_specs=[q_cur, k_specs[1], k_specs[0], k_specs[2], v_specs[1], v_specs[0], v_specs[2]],
        out_specs=[pl.BlockSpec((T, ATTN_W), lambda r, i: (i, r)), pl.BlockSpec((T, LANES), lambda r, i: (i, r))],
        out_shape=[jax.ShapeDtypeStruct((L, d * ATTN_W), BF16), jax.ShapeDtypeStruct((L, d * LANES), F32)],
    )(qk_v, qk_v, qk_v, qk_v, pj_v, pj_v, pj_v)
    return o.reshape(S, ATTN_W), lse.reshape(S, LANES)


def _attn_combine(outs, lses, g, tm=256):
    S = outs[0].shape[0]

    def body(o1, o2, o3, l1, l2, l3, g_ref, ao_ref, o_ref, lse_ref):
        lane = lax.broadcasted_iota(jnp.int32, (tm, LANES), 1)
        lse_tile = jnp.zeros((tm, LANES), F32)
        a = [l1[...], l2[...], l3[...]]
        ssq = jnp.zeros((tm, 1), F32)
        for h in range(N_HEADS):
            sl = slice(h * HEAD, (h + 1) * HEAD)
            a1, a2, a3 = (t[:, h:h + 1] for t in a)
            mx = jnp.maximum(jnp.maximum(a1, a2), a3)
            e1, e2, e3 = jnp.exp(a1 - mx), jnp.exp(a2 - mx), jnp.exp(a3 - mx)
            den = e1 + e2 + e3
            oh = (e1 * o1[:, sl].astype(F32) + e2 * o2[:, sl].astype(F32) + e3 * o3[:, sl].astype(F32)) / den
            o_ref[:, sl] = oh
            ssq = ssq + jnp.sum(oh * oh, axis=1, keepdims=True)
            lse_tile = jnp.where(lane == h, mx + jnp.log(den), lse_tile)
        lse_ref[...] = lse_tile
        r = lax.rsqrt(ssq * (1.0 / ATTN_W) + EPS)
        ao_ref[...] = (o_ref[...] * r * g_ref[...]).astype(BF16)

    blk = pl.BlockSpec((tm, ATTN_W), lambda i: (i, 0))
    ls = pl.BlockSpec((tm, LANES), lambda i: (i, 0))
    return pl.pallas_call(
        body, name="attn_combine", grid=(S // tm,),
        in_specs=[blk, blk, blk, ls, ls, ls, pl.BlockSpec((1, ATTN_W), lambda i: (0, 0))], out_specs=[blk, blk, ls],
        out_shape=[jax.ShapeDtypeStruct((S, ATTN_W), BF16), jax.ShapeDtypeStruct((S, ATTN_W), F32),
                   jax.ShapeDtypeStruct((S, LANES), F32)],
    )(*outs, *lses, g)


def _attn_norm_bwd(o, g, dao, tm=256):
    S = o.shape[0]

    def body(o_ref, g_ref, dao_ref, do_ref, dl_ref, dg_ref):
        i = pl.program_id(0)

        @pl.when(i == 0)
        def _():
            dg_ref[...] = jnp.zeros_like(dg_ref)

        ov = o_ref[...]
        r = lax.rsqrt(jnp.mean(ov * ov, axis=-1, keepdims=True) + EPS)
        ohat = ov * r
        dn = dao_ref[...].astype(F32)
        dg_ref[...] += jnp.sum(dn * ohat, axis=0, keepdims=True)
        t = dn * g_ref[...]
        do = r * (t - ohat * jnp.mean(t * ohat, axis=-1, keepdims=True))
        do_ref[...] = do.astype(BF16)
        prod = do * ov
        lane = lax.broadcasted_iota(jnp.int32, (tm, LANES), 1)
        tile = jnp.zeros((tm, LANES), F32)
        for h in range(N_HEADS):
            tile = jnp.where(lane == h, jnp.sum(prod[:, h * HEAD:(h + 1) * HEAD], axis=1, keepdims=True), tile)
        dl_ref[...] = tile

    blk = pl.BlockSpec((tm, ATTN_W), lambda i: (i, 0))
    vec = pl.BlockSpec((1, ATTN_W), lambda i: (0, 0))
    return pl.pallas_call(
        body, name="attn_norm_bwd", grid=(S // tm,),
        in_specs=[blk, vec, pl.BlockSpec((tm, ATTN_W), lambda i: (i, 0))],
        out_specs=[blk, pl.BlockSpec((tm, LANES), lambda i: (i, 0)), vec],
        out_shape=[jax.ShapeDtypeStruct((S, ATTN_W), BF16), jax.ShapeDtypeStruct((S, LANES), F32),
                   jax.ShapeDtypeStruct((1, ATTN_W), F32)],
    )(o, g, dao)


def _attn_bwd_dq(qk, proj, do, lse, delta, d):
    S = qk.shape[0]
    L, T, grid = _attn_dims(S, d)
    scale = HEAD ** -0.5
    qk_v = qk.reshape(L, d * 2 * ATTN_W)
    pj_v, ncol_p, vcol = _v_view(proj, L, d)

    def body(q_ref, kp, kc, kn, vp, vc, vn, do_ref, lse_ref, dl_ref, dq_ref):
        mask = _band_mask(T, L)
        lse_t, dl_t = lse_ref[...], dl_ref[...]
        for h in range(N_HEADS):
            sl = slice(h * HEAD, (h + 1) * HEAD)
            kw, vw = _window(kp, kc, kn, sl), _window(vp, vc, vn, sl)
            s = _dot(q_ref[:, sl], kw, NT) * scale
            p = jnp.where(mask, jnp.exp(s - lse_t[:, h:h + 1]), 0.0)
            dp = _dot(do_ref[:, sl], vw, NT)
            ds = (p * (dp - dl_t[:, h:h + 1]) * scale).astype(BF16)
            dq_ref[:, sl] = _dot(ds, kw).astype(BF16)

    q_cur, _, _ = _band_specs(T, L, ATTN_W, 2, 0)
    k_specs = _band_specs(T, L, ATTN_W, 2, 1)
    v_specs = _band_specs(T, L, ATTN_W, ncol_p, vcol)
    cur = pl.BlockSpec((T, ATTN_W), lambda r, i: (i, r))
    stat = pl.BlockSpec((T, LANES), lambda r, i: (i, r))
    dq = pl.pallas_call(
        body, name=f"attn_bwd_dq_d{d}", grid=grid,
        in_specs=[q_cur, k_specs[1], k_specs[0], k_specs[2], v_specs[1], v_specs[0], v_specs[2], cur, stat, stat],
        out_specs=cur, out_shape=jax.ShapeDtypeStruct((L, d * ATTN_W), BF16),
    )(qk_v, qk_v, qk_v, qk_v, pj_v, pj_v, pj_v, do.reshape(L, d * ATTN_W), lse.reshape(L, d * LANES),
      delta.reshape(L, d * LANES))
    return dq.reshape(S, ATTN_W)


def _attn_bwd_dkv(qk, proj, do, lse, delta, d):
    S = qk.shape[0]
    L, T, grid = _attn_dims(S, d)
    scale = HEAD ** -0.5
    qk_v = qk.reshape(L, d * 2 * ATTN_W)
    pj_v, ncol_p, vcol = _v_view(proj, L, d)

    def body(k_ref, v_ref, qp, qc, qn, dop, doc, don, lp, lc, ln, dlp, dlc, dln, dk_ref, dv_ref):
        mask = _band_mask(T, L)
        lse_t = _window(lp, lc, ln).T
        dl_t = _window(dlp, dlc, dln).T
        for h in range(N_HEADS):
            sl = slice(h * HEAD, (h + 1) * HEAD)
            qw, dow = _window(qp, qc, qn, sl), _window(dop, doc, don, sl)
            st = _dot(k_ref[:, sl], qw, NT) * scale
            pt = jnp.where(mask, jnp.exp(st - lse_t[h:h + 1, :]), 0.0)
            dv_ref[:, sl] = _dot(pt.astype(BF16), dow).astype(BF16)
            dpt = _dot(v_ref[:, sl], dow, NT)
            dst = (pt * (dpt - dl_t[h:h + 1, :]) * scale).astype(BF16)
            dk_ref[:, sl] = _dot(dst, qw).astype(BF16)

    q_specs = _band_specs(T, L, ATTN_W, 2, 0)
    k_cur, _, _ = _band_specs(T, L, ATTN_W, 2, 1)
    v_cur, _, _ = _band_specs(T, L, ATTN_W, ncol_p, vcol)
    do_specs = _band_specs(T, L, ATTN_W, 1, 0)
    st_specs = _band_specs(T, L, LANES, 1, 0)
    cur = pl.BlockSpec((T, ATTN_W), lambda r, i: (i, r))
    do_v, lse_v, dl_v = do.reshape(L, d * ATTN_W), lse.reshape(L, d * LANES), delta.reshape(L, d * LANES)
    dk, dv = pl.pallas_call(
        body, name=f"attn_bwd_dkv_d{d}", grid=grid,
        in_specs=[k_cur, v_cur, q_specs[1], q_specs[0], q_specs[2], do_specs[1], do_specs[0], do_specs[2],
                  st_specs[1], st_specs[0], st_specs[2], st_specs[1], st_specs[0], st_specs[2]],
        out_specs=[cur, cur],
        out_shape=[jax.ShapeDtypeStruct((L, d * ATTN_W), BF16), jax.ShapeDtypeStruct((L, d * ATTN_W), BF16)],
    )(qk_v, pj_v, qk_v, qk_v, qk_v, do_v, do_v, do_v, lse_v, lse_v, lse_v, dl_v, dl_v, dl_v)
    return dk.reshape(S, ATTN_W), dv.reshape(S, ATTN_W)


def _cumsum_rows(x, reverse):
    n = x.shape[0]
    row = lax.broadcasted_iota(jnp.int32, x.shape, 0)
    s = 1
    while s < n:
        if reverse:
            x = x + jnp.where(row < n - s, pltpu.roll(x, n - s, axis=0), 0.0)
        else:
            x = x + jnp.where(row >= s, pltpu.roll(x, s, axis=0), 0.0)
        s *= 2
    return x


def _gla_chunk_terms(q_ref, k_ref, v_ref, g_ref, h, reverse):
    ksl = slice(h * GLA_DK, (h + 1) * GLA_DK)
    q = q_ref[:, ksl].astype(F32) * (GLA_DK ** -0.5)
    k = k_ref[:, ksl].astype(F32)
    v = v_ref[:, h * GLA_DV:(h + 1) * GLA_DV]
    b = _cumsum_rows(g_ref[:, ksl], reverse)
    r_ref = CHUNK // 2 if reverse else CHUNK // 2 - 1
    r_last = 0 if reverse else CHUNK - 1
    b_ref, b_last = b[r_ref:r_ref + 1, :], b[r_last:r_last + 1, :]
    ii = lax.broadcasted_iota(jnp.int32, (CHUNK, CHUNK), 0)
    jj = lax.broadcasted_iota(jnp.int32, (CHUNK, CHUNK), 1)
    causal = (jj >= ii) if reverse else (jj <= ii)
    e_q, e_k = jnp.exp(b - b_ref), jnp.exp(b_ref - b)
    e_in, e_st = jnp.exp(b), jnp.exp(b_last - b)
    return dict(q=q, k=k, v=v, b=b, causal=causal, e_q=e_q, e_k=e_k, e_in=e_in, e_st=e_st, dec=jnp.exp(b_last),
                qe=q * e_q, ke=k * e_k, q_in=q * e_in, k_st=k * e_st, r_ref=r_ref, r_last=r_last)


def _gla_specs(n, order):
    q = pl.BlockSpec((CHUNK, GLA_K), lambda c: (order(c), 3 * ATTN_W // GLA_K))
    k = pl.BlockSpec((CHUNK, GLA_K), lambda c: (order(c), 3 * ATTN_W // GLA_K + 1))
    v = pl.BlockSpec((CHUNK, GLA_V), lambda c: (order(c), (3 * ATTN_W + 2 * GLA_K) // GLA_V))
    return q, k, v


def _gla_fwd(proj, gates, reverse, o_prev=None):
    S = proj.shape[0]
    n = S // CHUNK
    order = (lambda c: n - 1 - c) if reverse else (lambda c: c)
    gcol = 1 if reverse else 0

    def body(*refs):
        if o_prev is None:
            q_ref, k_ref, v_ref, g_ref, o_ref, st_ref, state = refs
        else:
            q_ref, k_ref, v_ref, g_ref, op_ref, o_ref, st_ref, state = refs
        c = pl.program_id(0)

        @pl.when(c == 0)
        def _():
            state[...] = jnp.zeros_like(state)

        for h in range(GLA_HEADS):
            t = _gla_chunk_terms(q_ref, k_ref, v_ref, g_ref, h, reverse)
            a = jnp.where(t["causal"], _dot(t["qe"].astype(BF16), t["ke"].astype(BF16), NT), 0.0)
            o = _dot(a.astype(BF16), t["v"])
            st = state[h]
            st_b = st.astype(BF16)
            st_ref[0, h] = st_b
            o = o + _dot(t["q_in"].astype(BF16), st_b, NT)
            state[h] = st * t["dec"] + _dot(t["v"], t["k_st"].astype(BF16), TN)
            vsl = slice(h * GLA_DV, (h + 1) * GLA_DV)
            if o_prev is not None:
                o = o + op_ref[:, vsl]
            o_ref[:, vsl] = o

    q_spec, k_spec, v_spec = _gla_specs(n, order)
    o_spec = pl.BlockSpec((CHUNK, GLA_V), lambda c: (order(c), 0))
    in_specs = [q_spec, k_spec, v_spec, pl.BlockSpec((CHUNK, GLA_K), lambda c: (order(c), gcol))]
    operands = [proj, proj, proj, gates]
    if o_prev is not None:
        in_specs.append(o_spec)
        operands.append(o_prev)
    return pl.pallas_call(
        body, name="gla_fwd_rev" if reverse else "gla_fwd", grid=(n,), in_specs=in_specs,
        out_specs=[o_spec, pl.BlockSpec((1, GLA_HEADS, GLA_DV, GLA_DK), lambda c: (order(c), 0, 0, 0))],
        out_shape=[jax.ShapeDtypeStruct((S, GLA_V), F32), jax.ShapeDtypeStruct((n, GLA_HEADS, GLA_DV, GLA_DK), BF16)],
        scratch_shapes=[pltpu.VMEM((GLA_HEADS, GLA_DV, GLA_DK), F32)],
    )(*operands)


def _gla_bwd(proj, gates, states, do, reverse, prev=None):
    S = proj.shape[0]
    n = S // CHUNK
    order = (lambda c: c) if reverse else (lambda c: n - 1 - c)
    gcol = 1 if reverse else 0
    out_dt = F32 if prev is None else BF16

    def body(*refs):
        if prev is None:
            q_ref, k_ref, v_ref, g_ref, st_ref, do_ref, dq_ref, dk_ref, dv_ref, dg_ref, dstate = refs
        else:
            q_ref, k_ref, v_ref, g_ref, st_ref, do_ref, pq, pk, pv, dq_ref, dk_ref, dv_ref, dg_ref, dstate = refs
        c = pl.program_id(0)

        @pl.when(c == 0)
        def _():
            dstate[...] = jnp.zeros_like(dstate)

        row = lax.broadcasted_iota(jnp.int32, (CHUNK, GLA_DK), 0)
        for h in range(GLA_HEADS):
            t = _gla_chunk_terms(q_ref, k_ref, v_ref, g_ref, h, reverse)
            ksl = slice(h * GLA_DK, (h + 1) * GLA_DK)
            vsl = slice(h * GLA_DV, (h + 1) * GLA_DV)
            v = t["v"]
            dob = do_ref[:, vsl].astype(BF16)
            st_b = st_ref[0, h]
            dst = dstate[h]
            dst_b = dst.astype(BF16)
            qe_b, ke_b = t["qe"].astype(BF16), t["ke"].astype(BF16)
            q_in_b, k_st_b = t["q_in"].astype(BF16), t["k_st"].astype(BF16)
            a = jnp.where(t["causal"], _dot(qe_b, ke_b, NT), 0.0)
            da = jnp.where(t["causal"], _dot(dob, v, NT), 0.0).astype(BF16)
            dv = _dot(a.astype(BF16), dob, TN) + _dot(k_st_b, dst_b, NT)
            dqe = _dot(da, ke_b)
            dke = _dot(da, qe_b, TN)
            dq_in = _dot(dob, st_b)
            dk_st = _dot(v, dst_b)
            ddec = jnp.sum(dst * st_b.astype(F32), axis=0, keepdims=True)
            dstate[h] = _dot(dob, q_in_b, TN) + dst * t["dec"]
            dq = (dqe * t["e_q"] + dq_in * t["e_in"]) * (GLA_DK ** -0.5)
            dk = dke * t["e_k"] + dk_st * t["e_st"]
            w_q, w_k = dqe * t["qe"], dke * t["ke"]
            w_st = dk_st * t["k_st"]
            db = w_q - w_k + dq_in * t["q_in"] - w_st
            db_ref = jnp.sum(w_k - w_q, axis=0, keepdims=True)
            db_last = jnp.sum(w_st, axis=0, keepdims=True) + ddec * t["dec"]
            db = db + jnp.where(row == t["r_ref"], db_ref, 0.0) + jnp.where(row == t["r_last"], db_last, 0.0)
            dg_ref[:, ksl] = _cumsum_rows(db, not reverse)
            if prev is not None:
                dq, dk, dv = dq + pq[:, ksl], dk + pk[:, ksl], dv + pv[:, vsl]
            dq_ref[:, ksl] = dq.astype(out_dt)
            dk_ref[:, ksl] = dk.astype(out_dt)
            dv_ref[:, vsl] = dv.astype(out_dt)

    q_spec, k_spec, v_spec = _gla_specs(n, order)
    kk = pl.BlockSpec((CHUNK, GLA_K), lambda c: (order(c), 0))
    vv = pl.BlockSpec((CHUNK, GLA_V), lambda c: (order(c), 0))
    in_specs = [q_spec, k_spec, v_spec, pl.BlockSpec((CHUNK, GLA_K), lambda c: (order(c), gcol)),
                pl.BlockSpec((1, GLA_HEADS, GLA_DV, GLA_DK), lambda c: (order(c), 0, 0, 0)), vv]
    operands = [proj, proj, proj, gates, states, do]
    if prev is not None:
        in_specs += [kk, kk, vv]
        operands += list(prev)
    return pl.pallas_call(
        body, name="gla_bwd_rev" if reverse else "gla_bwd", grid=(n,), in_specs=in_specs, out_specs=[kk, kk, vv, kk],
        out_shape=[jax.ShapeDtypeStruct((S, GLA_K), out_dt), jax.ShapeDtypeStruct((S, GLA_K), out_dt),
                   jax.ShapeDtypeStruct((S, GLA_V), out_dt), jax.ShapeDtypeStruct((S, GLA_K), F32)],
        scratch_shapes=[pltpu.VMEM((GLA_HEADS, GLA_DV, GLA_DK), F32)],
    )(*operands)


def _gates_fwd(z, wg, bias, tm=512):
    S = z.shape[0]
    W = 2 * GLA_K

    def body(z_ref, w_ref, b_ref, o_ref):
        zg = _dot(z_ref[...], w_ref[...]) + b_ref[...]
        o_ref[...] = (jnp.minimum(zg, 0.0) - jnp.log(1.0 + jnp.exp(-jnp.abs(zg)))) * (1.0 / GATE_NORM)

    return pl.pallas_call(
        body, name="gates_fwd", grid=(S // tm,),
        in_specs=[pl.BlockSpec((tm, Z_PAD), lambda i: (i, 0)), pl.BlockSpec((Z_PAD, W), lambda i: (0, 0)),
                  pl.BlockSpec((1, W), lambda i: (0, 0))],
        out_specs=pl.BlockSpec((tm, W), lambda i: (i, 0)), out_shape=jax.ShapeDtypeStruct((S, W), F32),
    )(z, wg, bias)


def _gates_bwd(z, wg, bias, dg_f, dg_b, tm=512):
    S = z.shape[0]
    W = 2 * GLA_K

    def body(z_ref, w_ref, b_ref, dgf_ref, dgb_ref, dz_ref, dw_ref, db_ref):
        i = pl.program_id(0)

        @pl.when(i == 0)
        def _():
            dw_ref[...] = jnp.zeros_like(dw_ref)
            db_ref[...] = jnp.zeros_like(db_ref)

        zv = z_ref[...]
        zg = _dot(zv, w_ref[...]) + b_ref[...]
        dg = jnp.concatenate([dgf_ref[...], dgb_ref[...]], axis=1)
        dzg = dg * (1.0 / GATE_NORM) * _sigmoid(-zg)
        db_ref[...] += jnp.sum(dzg, axis=0, keepdims=True)
        dzg_b = dzg.astype(BF16)
        dw_ref[...] += _dot(zv, dzg_b, TN)
        dz_ref[...] = _dot(dzg_b, w_ref[...], NT).astype(BF16)

    half = pl.BlockSpec((tm, GLA_K), lambda i: (i, 0))
    return pl.pallas_call(
        body, name="gates_bwd", grid=(S // tm,),
        in_specs=[pl.BlockSpec((tm, Z_PAD), lambda i: (i, 0)), pl.BlockSpec((Z_PAD, W), lambda i: (0, 0)),
                  pl.BlockSpec((1, W), lambda i: (0, 0)), half, half],
        out_specs=[pl.BlockSpec((tm, Z_PAD), lambda i: (i, 0)), pl.BlockSpec((Z_PAD, W), lambda i: (0, 0)),
                   pl.BlockSpec((1, W), lambda i: (0, 0))],
        out_shape=[jax.ShapeDtypeStruct((S, Z_PAD), BF16), jax.ShapeDtypeStruct((Z_PAD, W), F32),
                   jax.ShapeDtypeStruct((1, W), F32)],
    )(z, wg, bias, dg_f, dg_b)


def _gla_out_fwd(o, proj, g, tm=512):
    S = o.shape[0]

    def body(o_ref, gr_ref, g_ref, out_ref):
        gn = g_ref[...]
        for h in range(GLA_HEADS):
            sl = slice(h * GLA_DV, (h + 1) * GLA_DV)
            ov = o_ref[:, sl]
            r = lax.rsqrt(jnp.mean(ov * ov, axis=-1, keepdims=True) + EPS)
            gr = gr_ref[:, sl].astype(F32)
            out_ref[:, sl] = (ov * r * gn * (gr * _sigmoid(gr))).astype(BF16)

    blk = pl.BlockSpec((tm, GLA_V), lambda i: (i, 0))
    return pl.pallas_call(
        body, name="gla_out_fwd", grid=(S // tm,),
        in_specs=[blk, pl.BlockSpec((tm, GLA_V), lambda i: (i, (3 * ATTN_W + 2 * GLA_K + GLA_V) // GLA_V)),
                  pl.BlockSpec((1, GLA_DV), lambda i: (0, 0))],
        out_specs=blk, out_shape=jax.ShapeDtypeStruct((S, GLA_V), BF16),
    )(o, proj, g)


def _gla_out_bwd(o, proj, g, dcat, tm=512):
    S = o.shape[0]

    def body(o_ref, gr_ref, g_ref, dgo_ref, do_ref, dgr_ref, dg_ref):
        i = pl.program_id(0)

        @pl.when(i == 0)
        def _():
            dg_ref[...] = jnp.zeros_like(dg_ref)

        gn = g_ref[...]
        dg_acc = jnp.zeros((1, GLA_DV), F32)
        for h in range(GLA_HEADS):
            sl = slice(h * GLA_DV, (h + 1) * GLA_DV)
            ov = o_ref[:, sl]
            r = lax.rsqrt(jnp.mean(ov * ov, axis=-1, keepdims=True) + EPS)
            yhat = ov * r
            gr = gr_ref[:, sl].astype(F32)
            sg = _sigmoid(gr)
            dgo = dgo_ref[:, sl].astype(F32)
            dgr_ref[:, sl] = (dgo * (yhat * gn) * (sg * (1.0 + gr * (1.0 - sg)))).astype(BF16)
            dy = dgo * (gr * sg)
            dg_acc = dg_acc + jnp.sum(dy * yhat, axis=0, keepdims=True)
            t = dy * gn
            do_ref[:, sl] = r * (t - yhat * jnp.mean(t * yhat, axis=-1, keepdims=True))
        dg_ref[...] += dg_acc

    blk = pl.BlockSpec((tm, GLA_V), lambda i: (i, 0))
    vec = pl.BlockSpec((1, GLA_DV), lambda i: (0, 0))
    return pl.pallas_call(
        body, name="gla_out_bwd", grid=(S // tm,),
        in_specs=[blk, pl.BlockSpec((tm, GLA_V), lambda i: (i, (3 * ATTN_W + 2 * GLA_K + GLA_V) // GLA_V)), vec,
                  pl.BlockSpec((tm, GLA_V), lambda i: (i, 1))],
        out_specs=[blk, blk, vec],
        out_shape=[jax.ShapeDtypeStruct((S, GLA_V), F32), jax.ShapeDtypeStruct((S, GLA_V), BF16),
                   jax.ShapeDtypeStruct((1, GLA_DV), F32)],
    )(o, proj, g, dcat)


HALO = 16


def _halo_specs(tm, tn, S):
    cur = pl.BlockSpec((tm, tn), lambda j, i: (i, j))
    prev = pl.BlockSpec((HALO, tn), lambda j, i: (jnp.maximum(i * (tm // HALO) - 1, 0), j))
    nxt = pl.BlockSpec((HALO, tn), lambda j, i: (jnp.minimum((i + 1) * (tm // HALO), S // HALO - 1), j))
    return cur, prev, nxt


def _shifted(x, p_ref, n_ref, n_blocks):
    i = pl.program_id(1)
    tm = x.shape[0]
    row = lax.broadcasted_iota(jnp.int32, x.shape, 0)
    before = p_ref[HALO - 1:HALO, :].astype(F32) * (i > 0).astype(F32)
    after = n_ref[0:1, :].astype(F32) * (i < n_blocks - 1).astype(F32)
    x_m1 = jnp.where(row == 0, before, pltpu.roll(x, 1, axis=0))
    x_p1 = jnp.where(row == tm - 1, after, pltpu.roll(x, tm - 1, axis=0))
    return x_m1, x_p1


def _glu_fwd(gp, up, cw, cb, tm=512, tn=1408):
    S = gp.shape[0]
    nb = S // tm

    def body(c_ref, p_ref, n_ref, up_ref, w_ref, b_ref, o_ref):
        x = c_ref[...].astype(F32)
        x_m1, x_p1 = _shifted(x, p_ref, n_ref, nb)
        w = w_ref[...]
        gate = w[0:1, :] * x_m1 + w[1:2, :] * x + w[2:3, :] * x_p1 + b_ref[...]
        o_ref[...] = (gate * _sigmoid(gate) * up_ref[...].astype(F32)).astype(BF16)

    cur, prev, nxt = _halo_specs(tm, tn, S)
    return pl.pallas_call(
        body, name="glu_fwd", grid=(D_FF // tn, nb),
        in_specs=[cur, prev, nxt, cur, pl.BlockSpec((3, tn), lambda j, i: (0, j)), pl.BlockSpec((1, tn), lambda j, i: (0, j))],
        out_specs=cur, out_shape=jax.ShapeDtypeStruct((S, D_FF), BF16),
    )(gp, gp, gp, up, cw, cb)


def _glu_bwd(gp, up, dact, cw, cb, tm=512, tn=1408):
    S = gp.shape[0]
    nb = S // tm

    def body(c_ref, p_ref, n_ref, up_ref, da_ref, w_ref, b_ref, dup_ref, dgate_ref, dw_ref, db_ref):
        @pl.when(pl.program_id(1) == 0)
        def _():
            dw_ref[...] = jnp.zeros_like(dw_ref)
            db_ref[...] = jnp.zeros_like(db_ref)

        x = c_ref[...].astype(F32)
        x_m1, x_p1 = _shifted(x, p_ref, n_ref, nb)
        w = w_ref[...]
        gate = w[0:1, :] * x_m1 + w[1:2, :] * x + w[2:3, :] * x_p1 + b_ref[...]
        sg = _sigmoid(gate)
        da = da_ref[...].astype(F32)
        dup_ref[...] = (da * (gate * sg)).astype(BF16)
        dgate = da * up_ref[...].astype(F32) * (sg * (1.0 + gate * (1.0 - sg)))
        dgate_ref[...] = dgate.astype(BF16)
        db_ref[...] += jnp.sum(dgate, axis=0, keepdims=True)
        dw_ref[...] += jnp.concatenate(
            [jnp.sum(dgate * x_m1, axis=0, keepdims=True), jnp.sum(dgate * x, axis=0, keepdims=True),
             jnp.sum(dgate * x_p1, axis=0, keepdims=True)], axis=0)

    cur, prev, nxt = _halo_specs(tm, tn, S)
    w_spec = pl.BlockSpec((3, tn), lambda j, i: (0, j))
    b_spec = pl.BlockSpec((1, tn), lambda j, i: (0, j))
    return pl.pallas_call(
        body, name="glu_bwd", grid=(D_FF // tn, nb), in_specs=[cur, prev, nxt, cur, cur, w_spec, b_spec],
        out_specs=[cur, cur, w_spec, b_spec],
        out_shape=[jax.ShapeDtypeStruct((S, D_FF), BF16), jax.ShapeDtypeStruct((S, D_FF), BF16),
                   jax.ShapeDtypeStruct((3, D_FF), F32), jax.ShapeDtypeStruct((1, D_FF), F32)],
    )(gp, gp, gp, up, dact, cw, cb)


def _conv_bwd_input(dgate, cw, tm=512, tn=1408):
    S = dgate.shape[0]
    nb = S // tm

    def body(c_ref, p_ref, n_ref, w_ref, o_ref):
        x = c_ref[...].astype(F32)
        x_m1, x_p1 = _shifted(x, p_ref, n_ref, nb)
        w = w_ref[...]
        o_ref[...] = (w[0:1, :] * x_p1 + w[1:2, :] * x + w[2:3, :] * x_m1).astype(BF16)

    cur, prev, nxt = _halo_specs(tm, tn, S)
    return pl.pallas_call(
        body, name="conv_bwd_input", grid=(D_FF // tn, nb),
        in_specs=[cur, prev, nxt, pl.BlockSpec((3, tn), lambda j, i: (0, j))], out_specs=cur,
        out_shape=jax.ShapeDtypeStruct((S, D_FF), BF16),
    )(dgate, dgate, dgate, cw)


def _local_step(x, target, norm1_g, w_in_main, w_in_z, wg, gate_bias, gla_norm_g, attn_norm_g, w_out, norm2_g,
                w_gate4, w_up4, conv_w, conv_b, w_down, final_norm_g):
    S = x.shape[0]
    tabs = _rope_tables(S)

    n1 = _rms_fwd("rms1_fwd", x, norm1_g)
    proj = _mm_nn("in_proj", n1, w_in_main, 1024, 1536, BF16)
    z = _mm_nn("in_proj_z", n1, w_in_z, 1024, Z_PAD, BF16)
    qk = _rope_fwd(proj, tabs)
    branch = [_attn_fwd(qk, proj, d) for d in DILATIONS]
    ao, o_attn, lse = _attn_combine([b[0] for b in branch], [b[1] for b in branch], attn_norm_g)
    gates = _gates_fwd(z, wg, gate_bias)
    o_f, st_f = _gla_fwd(proj, gates, False)
    o_gla, st_b = _gla_fwd(proj, gates, True, o_prev=o_f)
    go = _gla_out_fwd(o_gla, proj, gla_norm_g)
    cat = jnp.concatenate([ao, go], axis=1)
    h1 = _mm_nn("out_proj", cat, w_out, 1024, 1024, F32, res=x)
    n2 = _rms_fwd("rms2_fwd", h1, norm2_g)
    gp = _mm_nn_sharded("ffn_gate", n2, w_gate4, 1024, BF16)
    up = _mm_nn_sharded("ffn_up", n2, w_up4, 1024, BF16)
    act = _glu_fwd(gp, up, conv_w, conv_b)
    tk = D_FF // N_CHIPS
    h2 = _matmul(
        "ffn_down",
        [(act, pl.BlockSpec((1024, tk), lambda i, j, k: (i, k)), w_down, pl.BlockSpec((tk, 1024), lambda i, j, k: (k, j)), NN)],
        (S // 1024, D_MODEL // 1024, N_CHIPS), jax.ShapeDtypeStruct((S, D_MODEL), F32),
        pl.BlockSpec((1024, 1024), lambda i, j, k: (i, j)), N_CHIPS,
        res=(h1, pl.BlockSpec((1024, 1024), lambda i, j, k: (i, j))))
    loss_row, d_final_g, dh2, dh2_b = _final_loss(h2, final_norm_g.reshape(1, D_MODEL), target)

    dact = _mm_nt("ffn_down_bwd", dh2_b, w_down, 1024, tk, BF16)
    dup, dgate, d_conv_w, d_conv_b = _glu_bwd(gp, up, dact, conv_w, conv_b)
    dgp = _conv_bwd_input(dgate, conv_w)
    d_w_down = _mm_tn("ffn_down_wgrad", act, dh2_b, tk, D_MODEL, 1024, BF16)
    d_w_gate4 = _mm_tn("ffn_gate_wgrad", n2, dgp, D_MODEL, tk, 1024, BF16, out3=tk)
    d_w_up4 = _mm_tn("ffn_up_wgrad", n2, dup, D_MODEL, tk, 1024, BF16, out3=tk)
    dn2 = _matmul(
        "ffn_in_bwd",
        [(dgp, pl.BlockSpec((1024, tk), lambda i, j, k: (i, k)), w_gate4, pl.BlockSpec((None, 1024, tk), lambda i, j, k: (k, j, 0)), NT),
         (dup, pl.BlockSpec((1024, tk), lambda i, j, k: (i, k)), w_up4, pl.BlockSpec((None, 1024, tk), lambda i, j, k: (k, j, 0)), NT)],
        (S // 1024, D_MODEL // 1024, N_CHIPS), jax.ShapeDtypeStruct((S, D_MODEL), F32),
        pl.BlockSpec((1024, 1024), lambda i, j, k: (i, j)), N_CHIPS)
    dh1, dh1_b, d_norm2_g = _rms_bwd("rms2_bwd", h1, norm2_g, dn2, dh2)

    d_w_out = _mm_tn("out_proj_wgrad", cat, dh1_b, D_MODEL, 1024, 1024, BF16)
    dcat = _mm_nt("out_proj_bwd", dh1_b, w_out, 1024, 1024, BF16)
    do_attn, delta, d_attn_norm_g = _attn_norm_bwd(o_attn, attn_norm_g, dcat)
    dqs, dks, dvs = [], [], []
    for d in DILATIONS:
        dqs.append(_attn_bwd_dq(qk, proj, do_attn, lse, delta, d))
        dk, dv = _attn_bwd_dkv(qk, proj, do_attn, lse, delta, d)
        dks.append(dk)
        dvs.append(dv)
    d_attn = _attn_grad_merge(dqs, dks, dvs, tabs)
    do_gla, dgr, d_gla_norm_g = _gla_out_bwd(o_gla, proj, gla_norm_g, dcat)
    dq_f, dk_f, dv_f, dg_f = _gla_bwd(proj, gates, st_f, do_gla, False)
    dgq, dgk, dgv, dg_b = _gla_bwd(proj, gates, st_b, do_gla, True, prev=(dq_f, dk_f, dv_f))
    dz, d_wg, d_gate_bias = _gates_bwd(z, wg, gate_bias, dg_f, dg_b)
    dproj = jnp.concatenate([d_attn, dgq, dgk, dgv, dgr], axis=1)
    d_w_in_main = _mm_tn("in_proj_wgrad", n1, dproj, D_MODEL, 1536, 1024, BF16)
    d_w_in_z = _mm_tn("in_proj_z_wgrad", n1, dz, D_MODEL, Z_PAD, 1024, BF16)
    tkm = IN_MAIN // 4
    dn1 = _matmul(
        "in_proj_bwd",
        [(dproj, pl.BlockSpec((1024, tkm), lambda i, j, k: (i, k)), w_in_main, pl.BlockSpec((1024, tkm), lambda i, j, k: (j, k)), NT)],
        (S // 1024, D_MODEL // 1024, 4), jax.ShapeDtypeStruct((S, D_MODEL), F32),
        pl.BlockSpec((1024, 1024), lambda i, j, k: (i, j)), 4)
    dn1 = _mm_nt("in_proj_z_bwd", dz, w_in_z, 1024, 1024, F32, res=dn1)
    grad_x, _, d_norm1_g = _rms_bwd("rms1_bwd", x, norm1_g, dn1, dh1)

    big = dict(w_in_main=d_w_in_main, w_in_z=d_w_in_z, w_out=d_w_out, w_gate4=d_w_gate4, w_up4=d_w_up4, w_down=d_w_down)
    small = dict(loss=loss_row, norm1_g=d_norm1_g, wg=d_wg, gate_bias=d_gate_bias, gla_norm_g=d_gla_norm_g,
                 attn_norm_g=d_attn_norm_g, norm2_g=d_norm2_g, conv_w=d_conv_w, conv_b=d_conv_b, final_norm_g=d_final_g)
    return grad_x, big, small


def _position():
    return lax.axis_index("x"), lax.axis_index("y"), lax.axis_index("c")


def _other_chips(x, y):
    return [(1 - x, y), (x, 1 - y), (1 - x, 1 - y)]


def _gather_chips(name, shards):
    n = len(shards)

    def body(*refs):
        ins, outs = refs[:n], refs[n:2 * n]
        send, recv, loc = refs[2 * n:]
        x, y, c = _position()
        me = 2 * x + y
        chips = _other_chips(x, y)
        started = []
        for w in range(n):
            own = pltpu.make_async_copy(ins[w], outs[w].at[me], loc.at[w])
            own.start()
            started.append(own)
        sends = []
        for w in range(n):
            for j, (px, py) in enumerate(chips):
                cp = pltpu.make_async_remote_copy(ins[w], outs[w].at[me], send.at[3 * w + j], recv.at[3 * w + j],
                                                  device_id=(px, py, c), device_id_type=MESH)
                cp.start()
                sends.append(cp)
        for w in range(n):
            for j, (px, py) in enumerate(chips):
                pltpu.make_async_remote_copy(ins[w], outs[w].at[2 * px + py], send.at[3 * w + j], recv.at[3 * w + j],
                                             device_id=(px, py, c), device_id_type=MESH).wait_recv()
        for cp in sends:
            cp.wait_send()
        for own in started:
            own.wait()

    return pl.pallas_call(
        body, name=name, in_specs=[ANY] * n, out_specs=[ANY] * n,
        out_shape=[jax.ShapeDtypeStruct((N_CHIPS,) + s.shape, s.dtype) for s in shards],
        scratch_shapes=[pltpu.SemaphoreType.DMA((3 * n,)), pltpu.SemaphoreType.DMA((3 * n,)), pltpu.SemaphoreType.DMA((n,))],
    )(*shards)


def _gather_chips_async(name, shards, collective_id):
    n = len(shards)

    def body(*refs):
        ins, outs = refs[:n], refs[n:2 * n]
        send, recv, loc = refs[2 * n:]
        x, y, c = _position()
        me = 2 * x + y
        chips = _other_chips(x, y)
        barrier = pltpu.get_barrier_semaphore()
        for px, py in chips:
            pl.semaphore_signal(barrier, inc=1, device_id=(px, py, c), device_id_type=MESH)
        pl.semaphore_wait(barrier, len(chips))
        started = []
        for w in range(n):
            own = pltpu.make_async_copy(ins[w], outs[w].at[me], loc.at[w])
            own.start()
            started.append(own)
        sends = []
        for w in range(n):
            for j, (px, py) in enumerate(chips):
                cp = pltpu.make_async_remote_copy(ins[w], outs[w].at[me], send.at[3 * w + j], recv.at[3 * w + j],
                                                  device_id=(px, py, c), device_id_type=MESH)
                cp.start()
                sends.append(cp)
        for w in range(n):
            for j, (px, py) in enumerate(chips):
                pltpu.make_async_remote_copy(ins[w], outs[w].at[2 * px + py], send.at[3 * w + j], recv.at[3 * w + j],
                                             device_id=(px, py, c), device_id_type=MESH).wait_recv()
        for cp in sends:
            cp.wait_send()
        for own in started:
            own.wait()

    return pl.kernel(
        body, name=name, mesh=_sequencer(),
        out_type=[jax.ShapeDtypeStruct((N_CHIPS,) + s.shape, s.dtype) for s in shards],
        scratch_types=[pltpu.SemaphoreType.DMA((3 * n,)), pltpu.SemaphoreType.DMA((3 * n,)), pltpu.SemaphoreType.DMA((n,))],
        compiler_params=pltpu.CompilerParams(collective_id=collective_id),
    )(*shards)


def _sibling_exchange(name, arrs):
    n = len(arrs)

    def body(*refs):
        ins, outs = refs[:n], refs[n:2 * n]
        send, recv = refs[2 * n:]
        x, y, c = _position()
        copies = [pltpu.make_async_remote_copy(ins[w], outs[w], send.at[w], recv.at[w], device_id=(x, y, 1 - c),
                                               device_id_type=MESH) for w in range(n)]
        for cp in copies:
            cp.start()
        for cp in copies:
            cp.wait()

    return pl.pallas_call(
        body, name=name, in_specs=[ANY] * n, out_specs=[ANY] * n,
        out_shape=[jax.ShapeDtypeStruct(a.shape, a.dtype) for a in arrs],
        scratch_shapes=[pltpu.SemaphoreType.DMA((n,)), pltpu.SemaphoreType.DMA((n,))],
    )(*arrs)


def _scatter_chips(name, parts):
    n = len(parts)

    def body(*refs):
        ins, outs = refs[:n], refs[n:2 * n]
        send, recv, loc = refs[2 * n:]
        x, y, c = _position()
        me = 2 * x + y
        chips = _other_chips(x, y)
        started = []
        for w in range(n):
            own = pltpu.make_async_copy(ins[w].at[me], outs[w].at[me], loc.at[w])
            own.start()
            started.append(own)
        sends = []
        for w in range(n):
            for j, (px, py) in enumerate(chips):
                cp = pltpu.make_async_remote_copy(ins[w].at[2 * px + py], outs[w].at[me], send.at[3 * w + j],
                                                  recv.at[3 * w + j], device_id=(px, py, c), device_id_type=MESH)
                cp.start()
                sends.append(cp)
        for w in range(n):
            for j, (px, py) in enumerate(chips):
                pltpu.make_async_remote_copy(ins[w].at[me], outs[w].at[2 * px + py], send.at[3 * w + j], recv.at[3 * w + j],
                                             device_id=(px, py, c), device_id_type=MESH).wait_recv()
        for cp in sends:
            cp.wait_send()
        for own in started:
            own.wait()

    return pl.pallas_call(
        body, name=name, in_specs=[ANY] * n, out_specs=[ANY] * n,
        out_shape=[jax.ShapeDtypeStruct(p.shape, p.dtype) for p in parts],
        scratch_shapes=[pltpu.SemaphoreType.DMA((3 * n,)), pltpu.SemaphoreType.DMA((3 * n,)), pltpu.SemaphoreType.DMA((n,))],
    )(*parts)


def _sequencer():
    return plsc.ScalarSubcoreMesh(axis_name="sequencer", num_cores=1)


def _sibling_exchange_async(name, arrs, collective_id):
    n = len(arrs)

    def body(*refs):
        ins, outs = refs[:n], refs[n:2 * n]
        send, recv = refs[2 * n:]
        x, y, c = _position()
        sibling = (x, y, 1 - c)
        barrier = pltpu.get_barrier_semaphore()
        pl.semaphore_signal(barrier, inc=1, device_id=sibling, device_id_type=MESH)
        pl.semaphore_wait(barrier, 1)
        copies = [pltpu.make_async_remote_copy(ins[w], outs[w], send.at[w], recv.at[w], device_id=sibling,
                                               device_id_type=MESH) for w in range(n)]
        for cp in copies:
            cp.start()
        for cp in copies:
            cp.wait()

    return pl.kernel(
        body, name=name, out_type=[jax.ShapeDtypeStruct(a.shape, a.dtype) for a in arrs],
        scratch_types=[pltpu.SemaphoreType.DMA((n,)), pltpu.SemaphoreType.DMA((n,))],
        compiler_params=pltpu.CompilerParams(collective_id=collective_id), mesh=_sequencer(),
    )(*arrs)


def _scatter_chips_async(name, parts, collective_id):
    n = len(parts)

    def body(*refs):
        ins, outs = refs[:n], refs[n:2 * n]
        send, recv, loc = refs[2 * n:]
        x, y, c = _position()
        me = 2 * x + y
        chips = _other_chips(x, y)
        barrier = pltpu.get_barrier_semaphore()
        for px, py in chips:
            pl.semaphore_signal(barrier, inc=1, device_id=(px, py, c), device_id_type=MESH)
        pl.semaphore_wait(barrier, len(chips))
        started = []
        for w in range(n):
            own = pltpu.make_async_copy(ins[w].at[me], outs[w].at[me], loc.at[w])
            own.start()
            started.append(own)
        sends = []
        for w in range(n):
            for j, (px, py) in enumerate(chips):
                cp = pltpu.make_async_remote_copy(ins[w].at[2 * px + py], outs[w].at[me], send.at[3 * w + j],
                                                  recv.at[3 * w + j], device_id=(px, py, c), device_id_type=MESH)
                cp.start()
                sends.append(cp)
        for w in range(n):
            for j, (px, py) in enumerate(chips):
                pltpu.make_async_remote_copy(ins[w].at[me], outs[w].at[2 * px + py], send.at[3 * w + j], recv.at[3 * w + j],
                                             device_id=(px, py, c), device_id_type=MESH).wait_recv()
        for cp in sends:
            cp.wait_send()
        for own in started:
            own.wait()

    return pl.kernel(
        body, name=name, out_type=[jax.ShapeDtypeStruct(p.shape, p.dtype) for p in parts],
        scratch_types=[pltpu.SemaphoreType.DMA((3 * n,)), pltpu.SemaphoreType.DMA((3 * n,)), pltpu.SemaphoreType.DMA((n,))],
        compiler_params=pltpu.CompilerParams(collective_id=collective_id), mesh=_sequencer(),
    )(*parts)


def _allreduce_rows(buf):
    R = buf.shape[0]

    def body(in_ref, out_ref, land, send, recv):
        x, y, c = _position()
        me = 4 * x + 2 * y + c
        land[pl.ds(me, 1)] = in_ref[...][None]
        peers = []
        for mask in range(1, N_DEV):
            px = 1 - x if mask & 4 else x
            py = 1 - y if mask & 2 else y
            pc = 1 - c if mask & 1 else c
            peers.append((px, py, pc))
        sends = []
        for k, peer in enumerate(peers):
            cp = pltpu.make_async_remote_copy(in_ref, land.at[me], send.at[k], recv.at[k], device_id=peer, device_id_type=MESH)
            cp.start()
            sends.append(cp)
        for k, (px, py, pc) in enumerate(peers):
            pltpu.make_async_remote_copy(in_ref, land.at[4 * px + 2 * py + pc], send.at[k], recv.at[k],
                                         device_id=(px, py, pc), device_id_type=MESH).wait_recv()
        for cp in sends:
            cp.wait_send()
        tot = land[0]
        for i in range(1, N_DEV):
            tot = tot + land[i]
        out_ref[...] = tot

    vm = pl.BlockSpec(memory_space=pltpu.VMEM)
    return pl.pallas_call(
        body, name="allreduce_small", in_specs=[vm], out_specs=vm, out_shape=jax.ShapeDtypeStruct((R, LANES), F32),
        scratch_shapes=[pltpu.VMEM((N_DEV, R, LANES), F32), pltpu.SemaphoreType.DMA((N_DEV - 1,)),
                        pltpu.SemaphoreType.DMA((N_DEV - 1,))],
    )(buf)


def _pair_sum(name, a, b, tr=512):
    shape = a.shape
    rows, cols = shape[0] * shape[1], shape[2]

    def body(a_ref, b_ref, o_ref):
        o_ref[...] = (a_ref[...].astype(F32) + b_ref[...].astype(F32)).astype(BF16)

    blk = pl.BlockSpec((tr, cols), lambda i: (i, 0))
    out = pl.pallas_call(
        body, name=name, grid=(rows // tr,), in_specs=[blk, blk], out_specs=blk,
        out_shape=jax.ShapeDtypeStruct((rows, cols), BF16),
    )(a.reshape(rows, cols), b.reshape(rows, cols))
    return out.reshape(shape)


def _adamw_math(w, m, v, g):
    m2 = ADAM_B1 * m + (1.0 - ADAM_B1) * g
    v2 = ADAM_B2 * v + (1.0 - ADAM_B2) * (g * g)
    m_hat = m2 / (1.0 - ADAM_B1 ** ADAM_STEP)
    v_hat = v2 / (1.0 - ADAM_B2 ** ADAM_STEP)
    delta = -ADAM_LR * (m_hat / (jnp.sqrt(v_hat) + ADAM_EPS) + ADAM_WD * w)
    return delta, m2, v2


def _adamw(name, w, m, v, g):
    r, c = w.shape
    stacked = g.ndim == 3
    tr = r if r <= 256 else (256 if r % 256 == 0 else 128)

    def body(w_ref, m_ref, v_ref, g_ref, go_ref, d_ref, m2_ref, v2_ref):
        if stacked:
            gv = g_ref[0].astype(F32)
            for i in range(1, N_CHIPS):
                gv = gv + g_ref[i].astype(F32)
        else:
            gv = g_ref[...]
        delta, m2, v2 = _adamw_math(w_ref[...], m_ref[...], v_ref[...], gv)
        go_ref[...] = gv
        d_ref[...] = delta
        m2_ref[...] = m2
        v2_ref[...] = v2

    blk = pl.BlockSpec((tr, c), lambda i: (i, 0))
    g_spec = pl.BlockSpec((N_CHIPS, tr, c), lambda i: (0, i, 0)) if stacked else blk
    out = jax.ShapeDtypeStruct((r, c), F32)
    return pl.pallas_call(
        body, name=name, grid=(r // tr,), in_specs=[blk, blk, blk, g_spec], out_specs=[blk] * 4, out_shape=[out] * 4,
    )(w, m, v, g)


def _pack_rows(pieces):
    flat = jnp.concatenate([p.reshape(-1) for p in pieces])
    rows = flat.shape[0] // LANES
    pad = (-rows) % 8
    return jnp.pad(flat.reshape(rows, LANES), ((0, pad), (0, 0)))


def _unpack_rows(buf, shapes):
    flat = buf.reshape(-1)
    out, at = [], 0
    for s in shapes:
        size = math.prod(s)
        out.append(flat[at:at + size].reshape(s))
        at += size
    return out


SMALL_NAMES = ("norm1_g", "gf_up", "gf_b", "gb_up", "gb_b", "gla_norm_g", "attn_norm_g", "norm2_g", "conv_w", "conv_b",
               "final_norm_g")
BIG_NAMES = ("w_in", "w_out", "w_gate", "w_up", "w_down")
WEIGHT_ORDER = ("norm1_g", "w_in", "gf_up", "gf_b", "gb_up", "gb_b", "gla_norm_g", "attn_norm_g", "w_out", "norm2_g",
                "w_gate", "w_up", "conv_w", "conv_b", "w_down", "final_norm_g")


def kernel(x, norm1_g, w_in, gf_up, gf_b, gb_up, gb_b, gla_norm_g, attn_norm_g, w_out, norm2_g, w_gate, w_up, conv_w, conv_b, w_down, final_norm_g, loss_target, m_norm1_g, m_w_in, m_gf_up, m_gf_b, m_gb_up, m_gb_b, m_gla_norm_g, m_attn_norm_g, m_w_out, m_norm2_g, m_w_gate, m_w_up, m_conv_w, m_conv_b, m_w_down, m_final_norm_g, v_norm1_g, v_w_in, v_gf_up, v_gf_b, v_gb_up, v_gb_b, v_gla_norm_g, v_attn_norm_g, v_w_out, v_norm2_g, v_w_gate, v_w_up, v_conv_w, v_conv_b, v_w_down, v_final_norm_g):
    w = dict(norm1_g=norm1_g, w_in=w_in, gf_up=gf_up, gf_b=gf_b, gb_up=gb_up, gb_b=gb_b, gla_norm_g=gla_norm_g,
             attn_norm_g=attn_norm_g, w_out=w_out, norm2_g=norm2_g, w_gate=w_gate, w_up=w_up, conv_w=conv_w, conv_b=conv_b,
             w_down=w_down, final_norm_g=final_norm_g)
    m = dict(norm1_g=m_norm1_g, w_in=m_w_in, gf_up=m_gf_up, gf_b=m_gf_b, gb_up=m_gb_up, gb_b=m_gb_b, gla_norm_g=m_gla_norm_g,
             attn_norm_g=m_attn_norm_g, w_out=m_w_out, norm2_g=m_norm2_g, w_gate=m_w_gate, w_up=m_w_up, conv_w=m_conv_w,
             conv_b=m_conv_b, w_down=m_w_down, final_norm_g=m_final_norm_g)
    v = dict(norm1_g=v_norm1_g, w_in=v_w_in, gf_up=v_gf_up, gf_b=v_gf_b, gb_up=v_gb_up, gb_b=v_gb_b, gla_norm_g=v_gla_norm_g,
             attn_norm_g=v_attn_norm_g, w_out=v_w_out, norm2_g=v_norm2_g, w_gate=v_w_gate, w_up=v_w_up, conv_w=v_conv_w,
             conv_b=v_conv_b, w_down=v_w_down, final_norm_g=v_final_norm_g)
    S = x.shape[1]
    chip = 2 * lax.axis_index("x") + lax.axis_index("y")
    n_in = IN_W // N_CHIPS
    n_ff = D_FF // N_CHIPS
    n_gk = GLA_K // N_CHIPS

    shard = {k: w[k][0].astype(BF16) for k in BIG_NAMES}
    (w_in4,) = _gather_chips_async("gather_w_in", [shard["w_in"]], 0)
    w_out4, w_gate4, w_up4 = _gather_chips_async("gather_w_mid", [shard["w_out"], shard["w_gate"], shard["w_up"]], 1)
    (w_down4,) = _gather_chips_async("gather_w_down", [shard["w_down"]], 2)
    w_in_full = jnp.transpose(w_in4, (1, 0, 2)).reshape(D_MODEL, IN_W)
    w_in_main = w_in_full[:, :IN_MAIN]
    w_in_z = jnp.pad(w_in_full[:, IN_MAIN:], ((0, 0), (0, Z_PAD - (IN_W - IN_MAIN))))
    small_shard = _pack_rows([gf_up[0], gb_up[0], conv_w[0]])
    (small4,) = _gather_chips("gather_small", [small_shard])
    rows_up = GATE_RANK * n_gk // LANES
    rows_cw = 3 * n_ff // LANES
    gf_full = jnp.transpose(small4[:, 0:rows_up].reshape(N_CHIPS, GATE_RANK, n_gk), (1, 0, 2)).reshape(GATE_RANK, GLA_K)
    gb_full = jnp.transpose(small4[:, rows_up:2 * rows_up].reshape(N_CHIPS, GATE_RANK, n_gk), (1, 0, 2)).reshape(GATE_RANK, GLA_K)
    cw_full = jnp.transpose(small4[:, 2 * rows_up:2 * rows_up + rows_cw].reshape(N_CHIPS, 3, n_ff), (1, 0, 2)).reshape(3, D_FF)
    wg = jnp.zeros((Z_PAD, 2 * GLA_K), F32)
    wg = wg.at[0:GATE_RANK, 0:GLA_K].set(gf_full).at[GATE_RANK:2 * GATE_RANK, GLA_K:].set(gb_full).astype(BF16)
    gate_bias = jnp.concatenate([gf_b, gb_b], axis=1)

    grad_x, big, small = _local_step(
        x[0], loss_target[0], norm1_g, w_in_main, w_in_z, wg, gate_bias, gla_norm_g, attn_norm_g,
        w_out4.reshape(D_MODEL, D_MODEL), norm2_g, w_gate4, w_up4, cw_full, conv_b, w_down4.reshape(D_FF, D_MODEL), final_norm_g)

    d_w_in = jnp.concatenate([big["w_in_main"], big["w_in_z"][:, :IN_W - IN_MAIN]], axis=1)
    mine = dict(w_in=jnp.transpose(d_w_in.reshape(D_MODEL, N_CHIPS, n_in), (1, 0, 2)),
                w_out=big["w_out"].reshape(N_CHIPS, D_MODEL // N_CHIPS, D_MODEL), w_gate=big["w_gate4"], w_up=big["w_up4"],
                w_down=big["w_down"].reshape(N_CHIPS, n_ff, D_MODEL))
    out = {}
    for gi, group in enumerate((("w_down",), ("w_gate", "w_up"), ("w_out",), ("w_in",))):
        tag = "_".join(group)
        theirs = _sibling_exchange_async(f"sibling_{tag}", [mine[k] for k in group], 3 + 2 * gi)
        chip_sums = [_pair_sum(f"pair_sum_{k}", mine[k], t) for k, t in zip(group, theirs)]
        contributions = _scatter_chips_async(f"scatter_{tag}", chip_sums, 4 + 2 * gi)
        for k, parts in zip(group, contributions):
            res = _adamw(f"adamw_{k}", w[k][0], m[k][0], v[k][0], parts)
            out[k] = [r[None] for r in res]

    d_gf_up = small["wg"][0:GATE_RANK, 0:GLA_K]
    d_gb_up = small["wg"][GATE_RANK:2 * GATE_RANK, GLA_K:]
    pieces = [small["loss"], small["norm1_g"], d_gf_up, small["gate_bias"][:, :GLA_K], d_gb_up, small["gate_bias"][:, GLA_K:],
              small["gla_norm_g"], small["attn_norm_g"], small["norm2_g"], small["conv_w"], small["conv_b"], small["final_norm_g"]]
    total = _allreduce_rows(_pack_rows(pieces))
    summed = _unpack_rows(total, [p.shape for p in pieces])
    loss = summed[0][0, 0]
    g_small = dict(zip(SMALL_NAMES, summed[1:]))
    g_small["gf_up"] = lax.dynamic_slice_in_dim(g_small["gf_up"], chip * n_gk, n_gk, axis=1)
    g_small["gb_up"] = lax.dynamic_slice_in_dim(g_small["gb_up"], chip * n_gk, n_gk, axis=1)
    g_small["conv_w"] = lax.dynamic_slice_in_dim(g_small["conv_w"], chip * n_ff, n_ff, axis=1)
    packed = [_pack_rows([t[k] for k in SMALL_NAMES]) for t in (w, m, v, g_small)]
    res = _adamw("adamw_small", *packed)
    shapes = [w[k].shape for k in SMALL_NAMES]
    for k, vals in zip(SMALL_NAMES, zip(*[_unpack_rows(r, shapes) for r in res])):
        out[k] = list(vals)

    grads, deltas, new_m, new_v = ([out[k][i] for k in WEIGHT_ORDER] for i in range(4))
    return (loss, grad_x[None], *grads, *deltas, *new_m, *new_v)
```

```python
import functools
import math

import jax
import jax.numpy as jnp
from jax import lax
from jax.experimental import pallas as pl
from jax.experimental.pallas import tpu as pltpu
from jax.experimental.pallas import tpu_sc as plsc

F32 = jnp.float32
BF16 = jnp.bfloat16

D_MODEL = 2048
ATTN_W = 1024
HEAD = 128
N_HEADS = 8
N_SIDE = 64
DILATIONS = (1, 4, 16)
ROPE_THETA = 500000.0
ROPE_DIM = 32
GLA_K = 512
GLA_V = 1024
GLA_HEADS = 4
GLA_DK = 128
GLA_DV = 256
GATE_RANK = 16
GATE_NORM = 16.0
CHUNK = 64
IN_MAIN = 6144
IN_W = 6176
Z_PAD = 128
D_FF = 5632
EPS = 1e-6
N_CHIPS = 4
N_DEV = 8
LANES = 128

ADAM_LR = 0.001
ADAM_B1 = 0.9
ADAM_B2 = 0.999
ADAM_EPS = 1e-08
ADAM_WD = 0.01
ADAM_STEP = 10

NEG = -1e30
MESH = pl.DeviceIdType.MESH
ANY = pl.BlockSpec(memory_space=pl.ANY)

NN = ((1,), (0,))
NT = ((1,), (1,))
TN = ((0,), (0,))


def _dot(a, b, dims=NN):
    return lax.dot_general(a, b, (dims, ((), ())), preferred_element_type=F32)


def _sigmoid(x):
    return 1.0 / (1.0 + jnp.exp(-x))


def _after(x, *deps):
    return lax.optimization_barrier((x,) + deps)[0]


def _matmul(name, pairs, grid, out_shape, out_spec, nk, res=None):
    n_in = 2 * len(pairs) + (res is not None)
    dims = [p[4] for p in pairs]

    def body(*refs):
        ins, o_ref = refs[:n_in], refs[n_in]

        def partial_sum():
            tot = None
            for p, dn in enumerate(dims):
                a, b = ins[2 * p][...], ins[2 * p + 1][...]
                t = _dot(a.astype(BF16), b.astype(BF16), dn)
                tot = t if tot is None else tot + t
            return tot

        if nk == 1:
            t = partial_sum()
            if res is not None:
                t = t + ins[-1][...]
            o_ref[...] = t.astype(o_ref.dtype)
        else:
            acc_ref = refs[n_in + 1]
            k = pl.program_id(2)

            @pl.when(k == 0)
            def _():
                if res is not None:
                    acc_ref[...] = ins[-1][...]
                else:
                    acc_ref[...] = jnp.zeros_like(acc_ref)

            acc_ref[...] += partial_sum()

            @pl.when(k == nk - 1)
            def _():
                o_ref[...] = acc_ref[...].astype(o_ref.dtype)

    operands, in_specs = [], []
    for a, a_spec, b, b_spec, _ in pairs:
        operands += [a, b]
        in_specs += [a_spec, b_spec]
    if res is not None:
        operands.append(res[0])
        in_specs.append(res[1])
    acc_shape = tuple(s for s in out_spec.block_shape if s is not None)
    scratch = [pltpu.VMEM(acc_shape, F32)] if nk > 1 else []
    return pl.pallas_call(
        body, name=name, grid=grid, in_specs=in_specs, out_specs=out_spec, out_shape=out_shape, scratch_shapes=scratch,
    )(*operands)


def _mm_nn(name, a, b, tm, tn, out_dtype, res=None):
    M, K = a.shape
    N = b.shape[1]
    pairs = [(a, pl.BlockSpec((tm, K), lambda j, i: (i, 0)), b, pl.BlockSpec((K, tn), lambda j, i: (0, j)), NN)]
    r = None if res is None else (res, pl.BlockSpec((tm, tn), lambda j, i: (i, j)))
    return _matmul(name, pairs, (N // tn, M // tm), jax.ShapeDtypeStruct((M, N), out_dtype),
                   pl.BlockSpec((tm, tn), lambda j, i: (i, j)), 1, r)


def _mm_nn_sharded(name, a, b4, tm, out_dtype):
    M, K = a.shape
    n = b4.shape[2]
    pairs = [(a, pl.BlockSpec((tm, K), lambda j, i: (i, 0)), b4, pl.BlockSpec((None, K, n), lambda j, i: (j, 0, 0)), NN)]
    return _matmul(name, pairs, (N_CHIPS, M // tm), jax.ShapeDtypeStruct((M, N_CHIPS * n), out_dtype),
                   pl.BlockSpec((tm, n), lambda j, i: (i, j)), 1)


def _mm_nt(name, a, b, tm, tn, out_dtype, res=None):
    M, K = a.shape
    N = b.shape[0]
    pairs = [(a, pl.BlockSpec((tm, K), lambda j, i: (i, 0)), b, pl.BlockSpec((tn, K), lambda j, i: (j, 0)), NT)]
    r = None if res is None else (res, pl.BlockSpec((tm, tn), lambda j, i: (i, j)))
    return _matmul(name, pairs, (N // tn, M // tm), jax.ShapeDtypeStruct((M, N), out_dtype),
                   pl.BlockSpec((tm, tn), lambda j, i: (i, j)), 1, r)


def _mm_tn(name, a, g, tka, tn, tmm, out_dtype, out3=None):
    M, Ka = a.shape
    N = g.shape[1]
    pairs = [(a, pl.BlockSpec((tmm, tka), lambda i, j, k: (k, i)), g, pl.BlockSpec((tmm, tn), lambda i, j, k: (k, j)), TN)]
    if out3 is None:
        shape, spec = (Ka, N), pl.BlockSpec((tka, tn), lambda i, j, k: (i, j))
    else:
        shape, spec = (N // out3, Ka, out3), pl.BlockSpec((None, tka, tn), lambda i, j, k: (j, i, 0))
    return _matmul(name, pairs, (Ka // tka, N // tn, M // tmm), jax.ShapeDtypeStruct(shape, out_dtype), spec, M // tmm)


def _rms_fwd(name, x, g, tm=512):
    S, D = x.shape

    def body(x_ref, g_ref, o_ref):
        xv = x_ref[...]
        r = lax.rsqrt(jnp.mean(xv * xv, axis=-1, keepdims=True) + EPS)
        o_ref[...] = (xv * r * g_ref[...]).astype(o_ref.dtype)

    return pl.pallas_call(
        body, name=name, grid=(S // tm,),
        in_specs=[pl.BlockSpec((tm, D), lambda i: (i, 0)), pl.BlockSpec((1, D), lambda i: (0, 0))],
        out_specs=pl.BlockSpec((tm, D), lambda i: (i, 0)), out_shape=jax.ShapeDtypeStruct((S, D), BF16),
    )(x, g)


def _rms_bwd(name, x, g, dn, dres, tm=512):
    S, D = x.shape

    def body(x_ref, g_ref, dn_ref, dres_ref, dx_ref, dxb_ref, dg_ref):
        i = pl.program_id(0)

        @pl.when(i == 0)
        def _():
            dg_ref[...] = jnp.zeros_like(dg_ref)

        xv = x_ref[...]
        r = lax.rsqrt(jnp.mean(xv * xv, axis=-1, keepdims=True) + EPS)
        xhat = xv * r
        dnv = dn_ref[...].astype(F32)
        dg_ref[...] += jnp.sum(dnv * xhat, axis=0, keepdims=True)
        t = dnv * g_ref[...]
        dx = r * (t - xhat * jnp.mean(t * xhat, axis=-1, keepdims=True)) + dres_ref[...]
        dx_ref[...] = dx
        dxb_ref[...] = dx.astype(BF16)

    row = pl.BlockSpec((tm, D), lambda i: (i, 0))
    vec = pl.BlockSpec((1, D), lambda i: (0, 0))
    return pl.pallas_call(
        body, name=name, grid=(S // tm,), in_specs=[row, vec, row, row], out_specs=[row, row, vec],
        out_shape=[jax.ShapeDtypeStruct((S, D), F32), jax.ShapeDtypeStruct((S, D), BF16), jax.ShapeDtypeStruct((1, D), F32)],
    )(x, g, dn, dres)


def _final_loss(h2, g, target, tm=512):
    S, D = h2.shape

    def body(x_ref, g_ref, t_ref, loss_ref, dg_ref, dx_ref, dxb_ref):
        i = pl.program_id(0)

        @pl.when(i == 0)
        def _():
            loss_ref[...] = jnp.zeros_like(loss_ref)
            dg_ref[...] = jnp.zeros_like(dg_ref)

        xv = x_ref[...]
        r = lax.rsqrt(jnp.mean(xv * xv, axis=-1, keepdims=True) + EPS)
        xhat = xv * r
        gv = g_ref[...]
        diff = xhat * gv - t_ref[...]
        per_tok = jnp.mean(diff * diff, axis=-1, keepdims=True)
        loss_ref[...] += 0.5 * jnp.sum(per_tok, axis=0, keepdims=True)
        dy = diff * (1.0 / D)
        dg_ref[...] += jnp.sum(dy * xhat, axis=0, keepdims=True)
        t = dy * gv
        dx = r * (t - xhat * jnp.mean(t * xhat, axis=-1, keepdims=True))
        dx_ref[...] = dx
        dxb_ref[...] = dx.astype(BF16)

    row = pl.BlockSpec((tm, D), lambda i: (i, 0))
    vec = pl.BlockSpec((1, D), lambda i: (0, 0))
    return pl.pallas_call(
        body, name="final_loss", grid=(S // tm,), in_specs=[row, vec, row],
        out_specs=[pl.BlockSpec((1, LANES), lambda i: (0, 0)), vec, row, row],
        out_shape=[jax.ShapeDtypeStruct((1, LANES), F32), jax.ShapeDtypeStruct((1, D), F32),
                   jax.ShapeDtypeStruct((S, D), F32), jax.ShapeDtypeStruct((S, D), BF16)],
    )(h2, g, target)


def _rope_tables(S):
    pos = jnp.arange(S, dtype=F32)
    inv_freq = ROPE_THETA ** (-jnp.arange(0, ROPE_DIM, 2, dtype=F32) / ROPE_DIM)
    ang = pos[:, None] * inv_freq[None, :]
    cos, sin = jnp.cos(ang), jnp.sin(ang)
    half = ROPE_DIM // 2
    rest = HEAD - ROPE_DIM
    z_h, z_r = jnp.zeros((S, half), F32), jnp.zeros((S, rest), F32)
    tab_c = jnp.concatenate([cos, cos, jnp.ones((S, rest), F32)], axis=1)
    tab_up = jnp.concatenate([z_h, sin, z_r], axis=1)
    tab_dn = jnp.concatenate([-sin, z_h, z_r], axis=1)
    return tab_c, tab_up, tab_dn


def _rope_head(t, c, up, dn):
    half = ROPE_DIM // 2
    return t * c + pltpu.roll(t, half, axis=1) * up + pltpu.roll(t, HEAD - half, axis=1) * dn


def _rope_fwd(proj, tabs, tm=512):
    S = proj.shape[0]
    W = 2 * ATTN_W

    def body(p_ref, c_ref, up_ref, dn_ref, o_ref):
        c, up, dn = c_ref[...], up_ref[...], dn_ref[...]
        for h in range(W // HEAD):
            sl = slice(h * HEAD, (h + 1) * HEAD)
            o_ref[:, sl] = _rope_head(p_ref[:, sl].astype(F32), c, up, dn).astype(BF16)

    tab = pl.BlockSpec((tm, HEAD), lambda i: (i, 0))
    return pl.pallas_call(
        body, name="rope_fwd", grid=(S // tm,), in_specs=[pl.BlockSpec((tm, W), lambda i: (i, 0)), tab, tab, tab],
        out_specs=pl.BlockSpec((tm, W), lambda i: (i, 0)), out_shape=jax.ShapeDtypeStruct((S, W), BF16),
    )(proj, *tabs)


def _attn_grad_merge(dqs, dks, dvs, tabs, tm=256):
    S = dqs[0].shape[0]

    def body(*refs):
        q_refs, k_refs, v_refs = refs[0:3], refs[3:6], refs[6:9]
        c, up, dn = refs[9][...], refs[10][...], refs[11][...]
        o_ref = refs[12]
        for h in range(N_HEADS):
            sl = slice(h * HEAD, (h + 1) * HEAD)
            for part, rs in ((0, q_refs), (1, k_refs)):
                t = rs[0][:, sl].astype(F32) + rs[1][:, sl].astype(F32) + rs[2][:, sl].astype(F32)
                osl = slice(part * ATTN_W + h * HEAD, part * ATTN_W + (h + 1) * HEAD)
                o_ref[:, osl] = _rope_head(t, c, -up, -dn).astype(BF16)
        o_ref[:, 2 * ATTN_W:] = (v_refs[0][...].astype(F32) + v_refs[1][...].astype(F32)
                                 + v_refs[2][...].astype(F32)).astype(BF16)

    blk = pl.BlockSpec((tm, ATTN_W), lambda i: (i, 0))
    tab = pl.BlockSpec((tm, HEAD), lambda i: (i, 0))
    return pl.pallas_call(
        body, name="attn_grad_merge", grid=(S // tm,), in_specs=[blk] * 9 + [tab] * 3,
        out_specs=pl.BlockSpec((tm, 3 * ATTN_W), lambda i: (i, 0)), out_shape=jax.ShapeDtypeStruct((S, 3 * ATTN_W), BF16),
    )(*dqs, *dks, *dvs, *tabs)


def _band_specs(T, L, width, ncol, col):
    ratio = T // N_SIDE
    nhb = L // N_SIDE
    cur = pl.BlockSpec((T, width), lambda r, i: (i, r * ncol + col))
    prev = pl.BlockSpec((N_SIDE, width), lambda r, i: (jnp.maximum(i * ratio - 1, 0), r * ncol + col))
    nxt = pl.BlockSpec((N_SIDE, width), lambda r, i: (jnp.minimum((i + 1) * ratio, nhb - 1), r * ncol + col))
    return cur, prev, nxt


def _band_mask(T, L):
    i = pl.program_id(1)
    row = lax.broadcasted_iota(jnp.int32, (T, T + 2 * N_SIDE), 0)
    col = lax.broadcasted_iota(jnp.int32, (T, T + 2 * N_SIDE), 1)
    pos = i * T - N_SIDE + col
    return (col >= row) & (col <= row + 2 * N_SIDE) & (pos >= 0) & (pos < L)


def _window(p_ref, c_ref, n_ref, sl=None):
    if sl is None:
        return jnp.concatenate([p_ref[...], c_ref[...], n_ref[...]], axis=0)
    return jnp.concatenate([p_ref[:, sl], c_ref[:, sl], n_ref[:, sl]], axis=0)


def _attn_dims(S, d):
    L = S // d
    T = min(128, L)
    return L, T, (d, L // T)


def _v_view(proj, L, d):
    if d == 1:
        return proj, IN_MAIN // ATTN_W, 2
    return proj[:, 2 * ATTN_W:3 * ATTN_W].reshape(L, d * ATTN_W), 1, 0


def _attn_fwd(qk, proj, d):
    S = qk.shape[0]
    L, T, grid = _attn_dims(S, d)
    scale = HEAD ** -0.5
    qk_v = qk.reshape(L, d * 2 * ATTN_W)
    pj_v, ncol_p, vcol = _v_view(proj, L, d)

    def body(q_ref, kp, kc, kn, vp, vc, vn, o_ref, lse_ref):
        mask = _band_mask(T, L)
        lane = lax.broadcasted_iota(jnp.int32, (T, LANES), 1)
        lse_tile = jnp.zeros((T, LANES), F32)
        for h in range(N_HEADS):
            sl = slice(h * HEAD, (h + 1) * HEAD)
            kw, vw = _window(kp, kc, kn, sl), _window(vp, vc, vn, sl)
            s = jnp.where(mask, _dot(q_ref[:, sl], kw, NT) * scale, NEG)
            m = jnp.max(s, axis=1, keepdims=True)
            p = jnp.exp(s - m)
            l = jnp.sum(p, axis=1, keepdims=True)
            o_ref[:, sl] = (_dot(p.astype(BF16), vw) / l).astype(BF16)
            lse_tile = jnp.where(lane == h, m + jnp.log(l), lse_tile)
        lse_ref[...] = lse_tile

    q_cur, _, _ = _band_specs(T, L, ATTN_W, 2, 0)
    k_specs = _band_specs(T, L, ATTN_W, 2, 1)
    v_specs = _band_specs(T, L, ATTN_W, ncol_p, vcol)
    o, lse = pl.pallas_call(
        body, name=f"attn_fwd_d{d}", grid=grid,
        in_specs=[q_cur, k_specs[1], k_specs[0], k_specs[2], v_specs[1], v_specs[0], v_specs[2]],
        out_specs=[pl.BlockSpec((T, ATTN_W), lambda r, i: (i, r)), pl.BlockSpec((T, LANES), lambda r, i: (i, r))],
        out_shape=[jax.ShapeDtypeStruct((L, d * ATTN_W), BF16), jax.ShapeDtypeStruct((L, d * LANES), F32)],
    )(qk_v, qk_v, qk_v, qk_v, pj_v, pj_v, pj_v)
    return o.reshape(S, ATTN_W), lse.reshape(S, LANES)


def _attn_combine(outs, lses, g, tm=256):
    S = outs[0].shape[0]

    def body(o1, o2, o3, l1, l2, l3, g_ref, ao_ref, o_ref, lse_ref):
        lane = lax.broadcasted_iota(jnp.int32, (tm, LANES), 1)
        lse_tile = jnp.zeros((tm, LANES), F32)
        a = [l1[...], l2[...], l3[...]]
        ssq = jnp.zeros((tm, 1), F32)
        for h in range(N_HEADS):
            sl = slice(h * HEAD, (h + 1) * HEAD)
            a1, a2, a3 = (t[:, h:h + 1] for t in a)
            mx = jnp.maximum(jnp.maximum(a1, a2), a3)
            e1, e2, e3 = jnp.exp(a1 - mx), jnp.exp(a2 - mx), jnp.exp(a3 - mx)
            den = e1 + e2 + e3
            oh = (e1 * o1[:, sl].astype(F32) + e2 * o2[:, sl].astype(F32) + e3 * o3[:, sl].astype(F32)) / den
            o_ref[:, sl] = oh
            ssq = ssq + jnp.sum(oh * oh, axis=1, keepdims=True)
            lse_tile = jnp.where(lane == h, mx + jnp.log(den), lse_tile)
        lse_ref[...] = lse_tile
        r = lax.rsqrt(ssq * (1.0 / ATTN_W) + EPS)
        ao_ref[...] = (o_ref[...] * r * g_ref[...]).astype(BF16)

    blk = pl.BlockSpec((tm, ATTN_W), lambda i: (i, 0))
    ls = pl.BlockSpec((tm, LANES), lambda i: (i, 0))
    return pl.pallas_call(
        body, name="attn_combine", grid=(S // tm,),
        in_specs=[blk, blk, blk, ls, ls, ls, pl.BlockSpec((1, ATTN_W), lambda i: (0, 0))], out_specs=[blk, blk, ls],
        out_shape=[jax.ShapeDtypeStruct((S, ATTN_W), BF16), jax.ShapeDtypeStruct((S, ATTN_W), F32),
                   jax.ShapeDtypeStruct((S, LANES), F32)],
    )(*outs, *lses, g)


def _attn_norm_bwd(o, g, dao, tm=256):
    S = o.shape[0]

    def body(o_ref, g_ref, dao_ref, do_ref, dl_ref, dg_ref):
        i = pl.program_id(0)

        @pl.when(i == 0)
        def _():
            dg_ref[...] = jnp.zeros_like(dg_ref)

        ov = o_ref[...]
        r = lax.rsqrt(jnp.mean(ov * ov, axis=-1, keepdims=True) + EPS)
        ohat = ov * r
        dn = dao_ref[...].astype(F32)
        dg_ref[...] += jnp.sum(dn * ohat, axis=0, keepdims=True)
        t = dn * g_ref[...]
        do = r * (t - ohat * jnp.mean(t * ohat, axis=-1, keepdims=True))
        do_ref[...] = do.astype(BF16)
        prod = do * ov
        lane = lax.broadcasted_iota(jnp.int32, (tm, LANES), 1)
        tile = jnp.zeros((tm, LANES), F32)
        for h in range(N_HEADS):
            tile = jnp.where(lane == h, jnp.sum(prod[:, h * HEAD:(h + 1) * HEAD], axis=1, keepdims=True), tile)
        dl_ref[...] = tile

    blk = pl.BlockSpec((tm, ATTN_W), lambda i: (i, 0))
    vec = pl.BlockSpec((1, ATTN_W), lambda i: (0, 0))
    return pl.pallas_call(
        body, name="attn_norm_bwd", grid=(S // tm,),
        in_specs=[blk, vec, pl.BlockSpec((tm, ATTN_W), lambda i: (i, 0))],
        out_specs=[blk, pl.BlockSpec((tm, LANES), lambda i: (i, 0)), vec],
        out_shape=[jax.ShapeDtypeStruct((S, ATTN_W), BF16), jax.ShapeDtypeStruct((S, LANES), F32),
                   jax.ShapeDtypeStruct((1, ATTN_W), F32)],
    )(o, g, dao)


def _attn_bwd_dq(qk, proj, do, lse, delta, d):
    S = qk.shape[0]
    L, T, grid = _attn_dims(S, d)
    scale = HEAD ** -0.5
    qk_v = qk.reshape(L, d * 2 * ATTN_W)
    pj_v, ncol_p, vcol = _v_view(proj, L, d)

    def body(q_ref, kp, kc, kn, vp, vc, vn, do_ref, lse_ref, dl_ref, dq_ref):
        mask = _band_mask(T, L)
        lse_t, dl_t = lse_ref[...], dl_ref[...]
        for h in range(N_HEADS):
            sl = slice(h * HEAD, (h + 1) * HEAD)
            kw, vw = _window(kp, kc, kn, sl), _window(vp, vc, vn, sl)
            s = _dot(q_ref[:, sl], kw, NT) * scale
            p = jnp.where(mask, jnp.exp(s - lse_t[:, h:h + 1]), 0.0)
            dp = _dot(do_ref[:, sl], vw, NT)
            ds = (p * (dp - dl_t[:, h:h + 1]) * scale).astype(BF16)
            dq_ref[:, sl] = _dot(ds, kw).astype(BF16)

    q_cur, _, _ = _band_specs(T, L, ATTN_W, 2, 0)
    k_specs = _band_specs(T, L, ATTN_W, 2, 1)
    v_specs = _band_specs(T, L, ATTN_W, ncol_p, vcol)
    cur = pl.BlockSpec((T, ATTN_W), lambda r, i: (i, r))
    stat = pl.BlockSpec((T, LANES), lambda r, i: (i, r))
    dq = pl.pallas_call(
        body, name=f"attn_bwd_dq_d{d}", grid=grid,
        in_specs=[q_cur, k_specs[1], k_specs[0], k_specs[2], v_specs[1], v_specs[0], v_specs[2], cur, stat, stat],
        out_specs=cur, out_shape=jax.ShapeDtypeStruct((L, d * ATTN_W), BF16),
    )(qk_v, qk_v, qk_v, qk_v, pj_v, pj_v, pj_v, do.reshape(L, d * ATTN_W), lse.reshape(L, d * LANES),
      delta.reshape(L, d * LANES))
    return dq.reshape(S, ATTN_W)


def _attn_bwd_dkv(qk, proj, do, lse, delta, d):
    S = qk.shape[0]
    L, T, grid = _attn_dims(S, d)
    scale = HEAD ** -0.5
    qk_v = qk.reshape(L, d * 2 * ATTN_W)
    pj_v, ncol_p, vcol = _v_view(proj, L, d)

    def body(k_ref, v_ref, qp, qc, qn, dop, doc, don, lp, lc, ln, dlp, dlc, dln, dk_ref, dv_ref):
        mask = _band_mask(T, L)
        lse_t = _window(lp, lc, ln).T
        dl_t = _window(dlp, dlc, dln).T
        for h in range(N_HEADS):
            sl = slice(h * HEAD, (h + 1) * HEAD)
            qw, dow = _window(qp, qc, qn, sl), _window(dop, doc, don, sl)
            st = _dot(k_ref[:, sl], qw, NT) * scale
            pt = jnp.where(mask, jnp.exp(st - lse_t[h:h + 1, :]), 0.0)
            dv_ref[:, sl] = _dot(pt.astype(BF16), dow).astype(BF16)
            dpt = _dot(v_ref[:, sl], dow, NT)
            dst = (pt * (dpt - dl_t[h:h + 1, :]) * scale).astype(BF16)
            dk_ref[:, sl] = _dot(dst, qw).astype(BF16)

    q_specs = _band_specs(T, L, ATTN_W, 2, 0)
    k_cur, _, _ = _band_specs(T, L, ATTN_W, 2, 1)
    v_cur, _, _ = _band_specs(T, L, ATTN_W, ncol_p, vcol)
    do_specs = _band_specs(T, L, ATTN_W, 1, 0)
    st_specs = _band_specs(T, L, LANES, 1, 0)
    cur = pl.BlockSpec((T, ATTN_W), lambda r, i: (i, r))
    do_v, lse_v, dl_v = do.reshape(L, d * ATTN_W), lse.reshape(L, d * LANES), delta.reshape(L, d * LANES)
    dk, dv = pl.pallas_call(
        body, name=f"attn_bwd_dkv_d{d}", grid=grid,
        in_specs=[k_cur, v_cur, q_specs[1], q_specs[0], q_specs[2], do_specs[1], do_specs[0], do_specs[2],
                  st_specs[1], st_specs[0], st_specs[2], st_specs[1], st_specs[0], st_specs[2]],
        out_specs=[cur, cur],
        out_shape=[jax.ShapeDtypeStruct((L, d * ATTN_W), BF16), jax.ShapeDtypeStruct((L, d * ATTN_W), BF16)],
    )(qk_v, pj_v, qk_v, qk_v, qk_v, do_v, do_v, do_v, lse_v, lse_v, lse_v, dl_v, dl_v, dl_v)
    return dk.reshape(S, ATTN_W), dv.reshape(S, ATTN_W)


def _cumsum_rows(x, reverse):
    n = x.shape[0]
    row = lax.broadcasted_iota(jnp.int32, x.shape, 0)
    s = 1
    while s < n:
        if reverse:
            x = x + jnp.where(row < n - s, pltpu.roll(x, n - s, axis=0), 0.0)
        else:
            x = x + jnp.where(row >= s, pltpu.roll(x, s, axis=0), 0.0)
        s *= 2
    return x


def _gla_chunk_terms(q_ref, k_ref, v_ref, g_ref, h, reverse):
    ksl = slice(h * GLA_DK, (h + 1) * GLA_DK)
    q = q_ref[:, ksl].astype(F32) * (GLA_DK ** -0.5)
    k = k_ref[:, ksl].astype(F32)
    v = v_ref[:, h * GLA_DV:(h + 1) * GLA_DV]
    b = _cumsum_rows(g_ref[:, ksl], reverse)
    r_ref = CHUNK // 2 if reverse else CHUNK // 2 - 1
    r_last = 0 if reverse else CHUNK - 1
    b_ref, b_last = b[r_ref:r_ref + 1, :], b[r_last:r_last + 1, :]
    ii = lax.broadcasted_iota(jnp.int32, (CHUNK, CHUNK), 0)
    jj = lax.broadcasted_iota(jnp.int32, (CHUNK, CHUNK), 1)
    causal = (jj >= ii) if reverse else (jj <= ii)
    e_q, e_k = jnp.exp(b - b_ref), jnp.exp(b_ref - b)
    e_in, e_st = jnp.exp(b), jnp.exp(b_last - b)
    return dict(q=q, k=k, v=v, b=b, causal=causal, e_q=e_q, e_k=e_k, e_in=e_in, e_st=e_st, dec=jnp.exp(b_last),
                qe=q * e_q, ke=k * e_k, q_in=q * e_in, k_st=k * e_st, r_ref=r_ref, r_last=r_last)


def _gla_specs(n, order):
    q = pl.BlockSpec((CHUNK, GLA_K), lambda c: (order(c), 3 * ATTN_W // GLA_K))
    k = pl.BlockSpec((CHUNK, GLA_K), lambda c: (order(c), 3 * ATTN_W // GLA_K + 1))
    v = pl.BlockSpec((CHUNK, GLA_V), lambda c: (order(c), (3 * ATTN_W + 2 * GLA_K) // GLA_V))
    return q, k, v


def _gla_fwd(proj, gates, reverse, o_prev=None):
    S = proj.shape[0]
    n = S // CHUNK
    order = (lambda c: n - 1 - c) if reverse else (lambda c: c)
    gcol = 1 if reverse else 0

    def body(*refs):
        if o_prev is None:
            q_ref, k_ref, v_ref, g_ref, o_ref, st_ref, state = refs
        else:
            q_ref, k_ref, v_ref, g_ref, op_ref, o_ref, st_ref, state = refs
        c = pl.program_id(0)

        @pl.when(c == 0)
        def _():
            state[...] = jnp.zeros_like(state)

        for h in range(GLA_HEADS):
            t = _gla_chunk_terms(q_ref, k_ref, v_ref, g_ref, h, reverse)
            a = jnp.where(t["causal"], _dot(t["qe"].astype(BF16), t["ke"].astype(BF16), NT), 0.0)
            o = _dot(a.astype(BF16), t["v"])
            st = state[h]
            st_b = st.astype(BF16)
            st_ref[0, h] = st_b
            o = o + _dot(t["q_in"].astype(BF16), st_b, NT)
            state[h] = st * t["dec"] + _dot(t["v"], t["k_st"].astype(BF16), TN)
            vsl = slice(h * GLA_DV, (h + 1) * GLA_DV)
            if o_prev is not None:
                o = o + op_ref[:, vsl]
            o_ref[:, vsl] = o

    q_spec, k_spec, v_spec = _gla_specs(n, order)
    o_spec = pl.BlockSpec((CHUNK, GLA_V), lambda c: (order(c), 0))
    in_specs = [q_spec, k_spec, v_spec, pl.BlockSpec((CHUNK, GLA_K), lambda c: (order(c), gcol))]
    operands = [proj, proj, proj, gates]
    if o_prev is not None:
        in_specs.append(o_spec)
        operands.append(o_prev)
    return pl.pallas_call(
        body, name="gla_fwd_rev" if reverse else "gla_fwd", grid=(n,), in_specs=in_specs,
        out_specs=[o_spec, pl.BlockSpec((1, GLA_HEADS, GLA_DV, GLA_DK), lambda c: (order(c), 0, 0, 0))],
        out_shape=[jax.ShapeDtypeStruct((S, GLA_V), F32), jax.ShapeDtypeStruct((n, GLA_HEADS, GLA_DV, GLA_DK), BF16)],
        scratch_shapes=[pltpu.VMEM((GLA_HEADS, GLA_DV, GLA_DK), F32)],
    )(*operands)


def _gla_bwd(proj, gates, states, do, reverse, prev=None):
    S = proj.shape[0]
    n = S // CHUNK
    order = (lambda c: c) if reverse else (lambda c: n - 1 - c)
    gcol = 1 if reverse else 0
    out_dt = F32 if prev is None else BF16

    def body(*refs):
        if prev is None:
            q_ref, k_ref, v_ref, g_ref, st_ref, do_ref, dq_ref, dk_ref, dv_ref, dg_ref, dstate = refs
        else:
            q_ref, k_ref, v_ref, g_ref, st_ref, do_ref, pq, pk, pv, dq_ref, dk_ref, dv_ref, dg_ref, dstate = refs
        c = pl.program_id(0)

        @pl.when(c == 0)
        def _():
            dstate[...] = jnp.zeros_like(dstate)

        row = lax.broadcasted_iota(jnp.int32, (CHUNK, GLA_DK), 0)
        for h in range(GLA_HEADS):
            t = _gla_chunk_terms(q_ref, k_ref, v_ref, g_ref, h, reverse)
            ksl = slice(h * GLA_DK, (h + 1) * GLA_DK)
            vsl = slice(h * GLA_DV, (h + 1) * GLA_DV)
            v = t["v"]
            dob = do_ref[:, vsl].astype(BF16)
            st_b = st_ref[0, h]
            dst = dstate[h]
            dst_b = dst.astype(BF16)
            qe_b, ke_b = t["qe"].astype(BF16), t["ke"].astype(BF16)
            q_in_b, k_st_b = t["q_in"].astype(BF16), t["k_st"].astype(BF16)
            a = jnp.where(t["causal"], _dot(qe_b, ke_b, NT), 0.0)
            da = jnp.where(t["causal"], _dot(dob, v, NT), 0.0).astype(BF16)
            dv = _dot(a.astype(BF16), dob, TN) + _dot(k_st_b, dst_b, NT)
            dqe = _dot(da, ke_b)
            dke = _dot(da, qe_b, TN)
            dq_in = _dot(dob, st_b)
            dk_st = _dot(v, dst_b)
            ddec = jnp.sum(dst * st_b.astype(F32), axis=0, keepdims=True)
            dstate[h] = _dot(dob, q_in_b, TN) + dst * t["dec"]
            dq = (dqe * t["e_q"] + dq_in * t["e_in"]) * (GLA_DK ** -0.5)
            dk = dke * t["e_k"] + dk_st * t["e_st"]
            w_q, w_k = dqe * t["qe"], dke * t["ke"]
            w_st = dk_st * t["k_st"]
            db = w_q - w_k + dq_in * t["q_in"] - w_st
            db_ref = jnp.sum(w_k - w_q, axis=0, keepdims=True)
            db_last = jnp.sum(w_st, axis=0, keepdims=True) + ddec * t["dec"]
            db = db + jnp.where(row == t["r_ref"], db_ref, 0.0) + jnp.where(row == t["r_last"], db_last, 0.0)
            dg_ref[:, ksl] = _cumsum_rows(db, not reverse)
            if prev is not None:
                dq, dk, dv = dq + pq[:, ksl], dk + pk[:, ksl], dv + pv[:, vsl]
            dq_ref[:, ksl] = dq.astype(out_dt)
            dk_ref[:, ksl] = dk.astype(out_dt)
            dv_ref[:, vsl] = dv.astype(out_dt)

    q_spec, k_spec, v_spec = _gla_specs(n, order)
    kk = pl.BlockSpec((CHUNK, GLA_K), lambda c: (order(c), 0))
    vv = pl.BlockSpec((CHUNK, GLA_V), lambda c: (order(c), 0))
    in_specs = [q_spec, k_spec, v_spec, pl.BlockSpec((CHUNK, GLA_K), lambda c: (order(c), gcol)),
                pl.BlockSpec((1, GLA_HEADS, GLA_DV, GLA_DK), lambda c: (order(c), 0, 0, 0)), vv]
    operands = [proj, proj, proj, gates, states, do]
    if prev is not None:
        in_specs += [kk, kk, vv]
        operands += list(prev)
    return pl.pallas_call(
        body, name="gla_bwd_rev" if reverse else "gla_bwd", grid=(n,), in_specs=in_specs, out_specs=[kk, kk, vv, kk],
        out_shape=[jax.ShapeDtypeStruct((S, GLA_K), out_dt), jax.ShapeDtypeStruct((S, GLA_K), out_dt),
                   jax.ShapeDtypeStruct((S, GLA_V), out_dt), jax.ShapeDtypeStruct((S, GLA_K), F32)],
        scratch_shapes=[pltpu.VMEM((GLA_HEADS, GLA_DV, GLA_DK), F32)],
    )(*operands)


def _gates_fwd(z, wg, bias, tm=512):
    S = z.shape[0]
    W = 2 * GLA_K

    def body(z_ref, w_ref, b_ref, o_ref):
        zg = _dot(z_ref[...], w_ref[...]) + b_ref[...]
        o_ref[...] = (jnp.minimum(zg, 0.0) - jnp.log(1.0 + jnp.exp(-jnp.abs(zg)))) * (1.0 / GATE_NORM)

    return pl.pallas_call(
        body, name="gates_fwd", grid=(S // tm,),
        in_specs=[pl.BlockSpec((tm, Z_PAD), lambda i: (i, 0)), pl.BlockSpec((Z_PAD, W), lambda i: (0, 0)),
                  pl.BlockSpec((1, W), lambda i: (0, 0))],
        out_specs=pl.BlockSpec((tm, W), lambda i: (i, 0)), out_shape=jax.ShapeDtypeStruct((S, W), F32),
    )(z, wg, bias)


def _gates_bwd(z, wg, bias, dg_f, dg_b, tm=512):
    S = z.shape[0]
    W = 2 * GLA_K

    def body(z_ref, w_ref, b_ref, dgf_ref, dgb_ref, dz_ref, dw_ref, db_ref):
        i = pl.program_id(0)

        @pl.when(i == 0)
        def _():
            dw_ref[...] = jnp.zeros_like(dw_ref)
            db_ref[...] = jnp.zeros_like(db_ref)

        zv = z_ref[...]
        zg = _dot(zv, w_ref[...]) + b_ref[...]
        dg = jnp.concatenate([dgf_ref[...], dgb_ref[...]], axis=1)
        dzg = dg * (1.0 / GATE_NORM) * _sigmoid(-zg)
        db_ref[...] += jnp.sum(dzg, axis=0, keepdims=True)
        dzg_b = dzg.astype(BF16)
        dw_ref[...] += _dot(zv, dzg_b, TN)
        dz_ref[...] = _dot(dzg_b, w_ref[...], NT).astype(BF16)

    half = pl.BlockSpec((tm, GLA_K), lambda i: (i, 0))
    return pl.pallas_call(
        body, name="gates_bwd", grid=(S // tm,),
        in_specs=[pl.BlockSpec((tm, Z_PAD), lambda i: (i, 0)), pl.BlockSpec((Z_PAD, W), lambda i: (0, 0)),
                  pl.BlockSpec((1, W), lambda i: (0, 0)), half, half],
        out_specs=[pl.BlockSpec((tm, Z_PAD), lambda i: (i, 0)), pl.BlockSpec((Z_PAD, W), lambda i: (0, 0)),
                   pl.BlockSpec((1, W), lambda i: (0, 0))],
        out_shape=[jax.ShapeDtypeStruct((S, Z_PAD), BF16), jax.ShapeDtypeStruct((Z_PAD, W), F32),
                   jax.ShapeDtypeStruct((1, W), F32)],
    )(z, wg, bias, dg_f, dg_b)


def _gla_out_fwd(o, proj, g, tm=512):
    S = o.shape[0]

    def body(o_ref, gr_ref, g_ref, out_ref):
        gn = g_ref[...]
        for h in range(GLA_HEADS):
            sl = slice(h * GLA_DV, (h + 1) * GLA_DV)
            ov = o_ref[:, sl]
            r = lax.rsqrt(jnp.mean(ov * ov, axis=-1, keepdims=True) + EPS)
            gr = gr_ref[:, sl].astype(F32)
            out_ref[:, sl] = (ov * r * gn * (gr * _sigmoid(gr))).astype(BF16)

    blk = pl.BlockSpec((tm, GLA_V), lambda i: (i, 0))
    return pl.pallas_call(
        body, name="gla_out_fwd", grid=(S // tm,),
        in_specs=[blk, pl.BlockSpec((tm, GLA_V), lambda i: (i, (3 * ATTN_W + 2 * GLA_K + GLA_V) // GLA_V)),
                  pl.BlockSpec((1, GLA_DV), lambda i: (0, 0))],
        out_specs=blk, out_shape=jax.ShapeDtypeStruct((S, GLA_V), BF16),
    )(o, proj, g)


def _gla_out_bwd(o, proj, g, dcat, tm=512):
    S = o.shape[0]

    def body(o_ref, gr_ref, g_ref, dgo_ref, do_ref, dgr_ref, dg_ref):
        i = pl.program_id(0)

        @pl.when(i == 0)
        def _():
            dg_ref[...] = jnp.zeros_like(dg_ref)

        gn = g_ref[...]
        dg_acc = jnp.zeros((1, GLA_DV), F32)
        for h in range(GLA_HEADS):
            sl = slice(h * GLA_DV, (h + 1) * GLA_DV)
            ov = o_ref[:, sl]
            r = lax.rsqrt(jnp.mean(ov * ov, axis=-1, keepdims=True) + EPS)
            yhat = ov * r
            gr = gr_ref[:, sl].astype(F32)
            sg = _sigmoid(gr)
            dgo = dgo_ref[:, sl].astype(F32)
            dgr_ref[:, sl] = (dgo * (yhat * gn) * (sg * (1.0 + gr * (1.0 - sg)))).astype(BF16)
            dy = dgo * (gr * sg)
            dg_acc = dg_acc + jnp.sum(dy * yhat, axis=0, keepdims=True)
            t = dy * gn
            do_ref[:, sl] = r * (t - yhat * jnp.mean(t * yhat, axis=-1, keepdims=True))
        dg_ref[...] += dg_acc

    blk = pl.BlockSpec((tm, GLA_V), lambda i: (i, 0))
    vec = pl.BlockSpec((1, GLA_DV), lambda i: (0, 0))
    return pl.pallas_call(
        body, name="gla_out_bwd", grid=(S // tm,),
        in_specs=[blk, pl.BlockSpec((tm, GLA_V), lambda i: (i, (3 * ATTN_W + 2 * GLA_K + GLA_V) // GLA_V)), vec,
                  pl.BlockSpec((tm, GLA_V), lambda i: (i, 1))],
        out_specs=[blk, blk, vec],
        out_shape=[jax.ShapeDtypeStruct((S, GLA_V), F32), jax.ShapeDtypeStruct((S, GLA_V), BF16),
                   jax.ShapeDtypeStruct((1, GLA_DV), F32)],
    )(o, proj, g, dcat)


HALO = 16


def _halo_specs(tm, tn, S):
    cur = pl.BlockSpec((tm, tn), lambda j, i: (i, j))
    prev = pl.BlockSpec((HALO, tn), lambda j, i: (jnp.maximum(i * (tm // HALO) - 1, 0), j))
    nxt = pl.BlockSpec((HALO, tn), lambda j, i: (jnp.minimum((i + 1) * (tm // HALO), S // HALO - 1), j))
    return cur, prev, nxt


def _shifted(x, p_ref, n_ref, n_blocks):
    i = pl.program_id(1)
    tm = x.shape[0]
    row = lax.broadcasted_iota(jnp.int32, x.shape, 0)
    before = p_ref[HALO - 1:HALO, :].astype(F32) * (i > 0).astype(F32)
    after = n_ref[0:1, :].astype(F32) * (i < n_blocks - 1).astype(F32)
    x_m1 = jnp.where(row == 0, before, pltpu.roll(x, 1, axis=0))
    x_p1 = jnp.where(row == tm - 1, after, pltpu.roll(x, tm - 1, axis=0))
    return x_m1, x_p1


def _glu_fwd(gp, up, cw, cb, tm=512, tn=1408):
    S = gp.shape[0]
    nb = S // tm

    def body(c_ref, p_ref, n_ref, up_ref, w_ref, b_ref, o_ref):
        x = c_ref[...].astype(F32)
        x_m1, x_p1 = _shifted(x, p_ref, n_ref, nb)
        w = w_ref[...]
        gate = w[0:1, :] * x_m1 + w[1:2, :] * x + w[2:3, :] * x_p1 + b_ref[...]
        o_ref[...] = (gate * _sigmoid(gate) * up_ref[...].astype(F32)).astype(BF16)

    cur, prev, nxt = _halo_specs(tm, tn, S)
    return pl.pallas_call(
        body, name="glu_fwd", grid=(D_FF // tn, nb),
        in_specs=[cur, prev, nxt, cur, pl.BlockSpec((3, tn), lambda j, i: (0, j)), pl.BlockSpec((1, tn), lambda j, i: (0, j))],
        out_specs=cur, out_shape=jax.ShapeDtypeStruct((S, D_FF), BF16),
    )(gp, gp, gp, up, cw, cb)


def _glu_bwd(gp, up, dact, cw, cb, tm=512, tn=1408):
    S = gp.shape[0]
    nb = S // tm

    def body(c_ref, p_ref, n_ref, up_ref, da_ref, w_ref, b_ref, dup_ref, dgate_ref, dw_ref, db_ref):
        @pl.when(pl.program_id(1) == 0)
        def _():
            dw_ref[...] = jnp.zeros_like(dw_ref)
            db_ref[...] = jnp.zeros_like(db_ref)

        x = c_ref[...].astype(F32)
        x_m1, x_p1 = _shifted(x, p_ref, n_ref, nb)
        w = w_ref[...]
        gate = w[0:1, :] * x_m1 + w[1:2, :] * x + w[2:3, :] * x_p1 + b_ref[...]
        sg = _sigmoid(gate)
        da = da_ref[...].astype(F32)
        dup_ref[...] = (da * (gate * sg)).astype(BF16)
        dgate = da * up_ref[...].astype(F32) * (sg * (1.0 + gate * (1.0 - sg)))
        dgate_ref[...] = dgate.astype(BF16)
        db_ref[...] += jnp.sum(dgate, axis=0, keepdims=True)
        dw_ref[...] += jnp.concatenate(
            [jnp.sum(dgate * x_m1, axis=0, keepdims=True), jnp.sum(dgate * x, axis=0, keepdims=True),
             jnp.sum(dgate * x_p1, axis=0, keepdims=True)], axis=0)

    cur, prev, nxt = _halo_specs(tm, tn, S)
    w_spec = pl.BlockSpec((3, tn), lambda j, i: (0, j))
    b_spec = pl.BlockSpec((1, tn), lambda j, i: (0, j))
    return pl.pallas_call(
        body, name="glu_bwd", grid=(D_FF // tn, nb), in_specs=[cur, prev, nxt, cur, cur, w_spec, b_spec],
        out_specs=[cur, cur, w_spec, b_spec],
        out_shape=[jax.ShapeDtypeStruct((S, D_FF), BF16), jax.ShapeDtypeStruct((S, D_FF), BF16),
                   jax.ShapeDtypeStruct((3, D_FF), F32), jax.ShapeDtypeStruct((1, D_FF), F32)],
    )(gp, gp, gp, up, dact, cw, cb)


def _conv_bwd_input(dgate, cw, tm=512, tn=1408):
    S = dgate.shape[0]
    nb = S // tm

    def body(c_ref, p_ref, n_ref, w_ref, o_ref):
        x = c_ref[...].astype(F32)
        x_m1, x_p1 = _shifted(x, p_ref, n_ref, nb)
        w = w_ref[...]
        o_ref[...] = (w[0:1, :] * x_p1 + w[1:2, :] * x + w[2:3, :] * x_m1).astype(BF16)

    cur, prev, nxt = _halo_specs(tm, tn, S)
    return pl.pallas_call(
        body, name="conv_bwd_input", grid=(D_FF // tn, nb),
        in_specs=[cur, prev, nxt, pl.BlockSpec((3, tn), lambda j, i: (0, j))], out_specs=cur,
        out_shape=jax.ShapeDtypeStruct((S, D_FF), BF16),
    )(dgate, dgate, dgate, cw)


def _local_step(x, target, norm1_g, w_in_main, w_in_z, wg, gate_bias, gla_norm_g, attn_norm_g, w_out, norm2_g,
                w_gate4, w_up4, conv_w, conv_b, w_down, final_norm_g, on_grad=lambda event, arrays: ()):
    S = x.shape[0]
    tabs = _rope_tables(S)

    n1 = _rms_fwd("rms1_fwd", x, norm1_g)
    proj = _mm_nn("in_proj", n1, w_in_main, 1024, 1536, BF16)
    z = _mm_nn("in_proj_z", n1, w_in_z, 1024, Z_PAD, BF16)
    qk = _rope_fwd(proj, tabs)
    branch = [_attn_fwd(qk, proj, d) for d in DILATIONS]
    ao, o_attn, lse = _attn_combine([b[0] for b in branch], [b[1] for b in branch], attn_norm_g)
    gates = _gates_fwd(z, wg, gate_bias)
    o_f, st_f = _gla_fwd(proj, gates, False)
    o_gla, st_b = _gla_fwd(proj, gates, True, o_prev=o_f)
    go = _gla_out_fwd(o_gla, proj, gla_norm_g)
    cat = jnp.concatenate([ao, go], axis=1)
    h1 = _mm_nn("out_proj", cat, w_out, 1024, 1024, F32, res=x)
    n2 = _rms_fwd("rms2_fwd", h1, norm2_g)
    gp = _mm_nn_sharded("ffn_gate", n2, w_gate4, 1024, BF16)
    up = _mm_nn_sharded("ffn_up", n2, w_up4, 1024, BF16)
    act = _glu_fwd(gp, up, conv_w, conv_b)
    tk = D_FF // N_CHIPS
    h2 = _matmul(
        "ffn_down",
        [(act, pl.BlockSpec((1024, tk), lambda i, j, k: (i, k)), w_down, pl.BlockSpec((tk, 1024), lambda i, j, k: (k, j)), NN)],
        (S // 1024, D_MODEL // 1024, N_CHIPS), jax.ShapeDtypeStruct((S, D_MODEL), F32),
        pl.BlockSpec((1024, 1024), lambda i, j, k: (i, j)), N_CHIPS,
        res=(h1, pl.BlockSpec((1024, 1024), lambda i, j, k: (i, j))))
    loss_row, d_final_g, dh2, dh2_b = _final_loss(h2, final_norm_g.reshape(1, D_MODEL), target)

    dact = _mm_nt("ffn_down_bwd", dh2_b, w_down, 1024, tk, BF16)
    dup, dgate, d_conv_w, d_conv_b = _glu_bwd(gp, up, dact, conv_w, conv_b)
    dgp = _conv_bwd_input(dgate, conv_w)
    d_w_down = _mm_tn("ffn_down_wgrad", act, dh2_b, tk, D_MODEL, 1024, BF16)
    on_grad("w_down", dict(w_down=d_w_down))
    dgp = _after(dgp, d_w_down)
    d_w_gate4 = _mm_tn("ffn_gate_wgrad", n2, dgp, D_MODEL, tk, 1024, BF16, out3=tk)
    dup = _after(dup, d_w_gate4)
    d_w_up4 = _mm_tn("ffn_up_wgrad", n2, dup, D_MODEL, tk, 1024, BF16, out3=tk)
    held = on_grad("w_gate_w_up", dict(w_gate=d_w_gate4, w_up=d_w_up4))
    dgp = _after(dgp, d_w_up4, *held)
    dn2 = _matmul(
        "ffn_in_bwd",
        [(dgp, pl.BlockSpec((1024, tk), lambda i, j, k: (i, k)), w_gate4, pl.BlockSpec((None, 1024, tk), lambda i, j, k: (k, j, 0)), NT),
         (dup, pl.BlockSpec((1024, tk), lambda i, j, k: (i, k)), w_up4, pl.BlockSpec((None, 1024, tk), lambda i, j, k: (k, j, 0)), NT)],
        (S // 1024, D_MODEL // 1024, N_CHIPS), jax.ShapeDtypeStruct((S, D_MODEL), F32),
        pl.BlockSpec((1024, 1024), lambda i, j, k: (i, j)), N_CHIPS)
    dh1, dh1_b, d_norm2_g = _rms_bwd("rms2_bwd", h1, norm2_g, dn2, dh2)

    d_w_out = _mm_tn("out_proj_wgrad", cat, dh1_b, D_MODEL, 1024, 1024, BF16)
    held = on_grad("w_out", dict(w_out=d_w_out))
    dcat = _mm_nt("out_proj_bwd", _after(dh1_b, d_w_out, *held), w_out, 1024, 1024, BF16)
    do_attn, delta, d_attn_norm_g = _attn_norm_bwd(o_attn, attn_norm_g, dcat)
    dqs, dks, dvs = [], [], []
    for d in DILATIONS:
        dqs.append(_attn_bwd_dq(qk, proj, do_attn, lse, delta, d))
        dk, dv = _attn_bwd_dkv(qk, proj, do_attn, lse, delta, d)
        dks.append(dk)
        dvs.append(dv)
    d_attn = _attn_grad_merge(dqs, dks, dvs, tabs)
    held = on_grad("mid", dict(anchor=d_attn))
    do_gla, dgr, d_gla_norm_g = _gla_out_bwd(o_gla, proj, gla_norm_g, _after(dcat, *held))
    dq_f, dk_f, dv_f, dg_f = _gla_bwd(proj, gates, st_f, do_gla, False)
    dgq, dgk, dgv, dg_b = _gla_bwd(proj, gates, st_b, do_gla, True, prev=(dq_f, dk_f, dv_f))
    dz, d_wg, d_gate_bias = _gates_bwd(z, wg, gate_bias, dg_f, dg_b)
    dproj = jnp.concatenate([d_attn, dgq, dgk, dgv, dgr], axis=1)
    d_w_in_main = _mm_tn("in_proj_wgrad", n1, dproj, D_MODEL, 1536, 1024, BF16)
    d_w_in_z = _mm_tn("in_proj_z_wgrad", n1, dz, D_MODEL, Z_PAD, 1024, BF16)
    held = on_grad("w_in", dict(w_in_main=d_w_in_main, w_in_z=d_w_in_z))
    tkm = IN_MAIN // 4
    dn1 = _matmul(
        "in_proj_bwd",
        [(_after(dproj, d_w_in_main, d_w_in_z, *held), pl.BlockSpec((1024, tkm), lambda i, j, k: (i, k)), w_in_main, pl.BlockSpec((1024, tkm), lambda i, j, k: (j, k)), NT)],
        (S // 1024, D_MODEL // 1024, 4), jax.ShapeDtypeStruct((S, D_MODEL), F32),
        pl.BlockSpec((1024, 1024), lambda i, j, k: (i, j)), 4)
    held = on_grad("last", dict(last=dn1))
    dn1 = _mm_nt("in_proj_z_bwd", dz, w_in_z, 1024, 1024, F32, res=_after(dn1, *held))
    grad_x, _, d_norm1_g = _rms_bwd("rms1_bwd", x, norm1_g, dn1, dh1)

    big = dict(w_in_main=d_w_in_main, w_in_z=d_w_in_z, w_out=d_w_out, w_gate4=d_w_gate4, w_up4=d_w_up4, w_down=d_w_down)
    small = dict(loss=loss_row, norm1_g=d_norm1_g, wg=d_wg, gate_bias=d_gate_bias, gla_norm_g=d_gla_norm_g,
                 attn_norm_g=d_attn_norm_g, norm2_g=d_norm2_g, conv_w=d_conv_w, conv_b=d_conv_b, final_norm_g=d_final_g)
    return grad_x, big, small


def _position():
    return lax.axis_index("x"), lax.axis_index("y"), lax.axis_index("c")


def _other_chips(x, y):
    return [(1 - x, y), (x, 1 - y), (1 - x, 1 - y)]


def _gather_chips(name, shards):
    n = len(shards)

    def body(*refs):
        ins, outs = refs[:n], refs[n:2 * n]
        send, recv, loc = refs[2 * n:]
        x, y, c = _position()
        me = 2 * x + y
        chips = _other_chips(x, y)
        started = []
        for w in range(n):
            own = pltpu.make_async_copy(ins[w], outs[w].at[me], loc.at[w])
            own.start()
            started.append(own)
        sends = []
        for w in range(n):
            for j, (px, py) in enumerate(chips):
                cp = pltpu.make_async_remote_copy(ins[w], outs[w].at[me], send.at[3 * w + j], recv.at[3 * w + j],
                                                  device_id=(px, py, c), device_id_type=MESH)
                cp.start()
                sends.append(cp)
        for w in range(n):
            for j, (px, py) in enumerate(chips):
                pltpu.make_async_remote_copy(ins[w], outs[w].at[2 * px + py], send.at[3 * w + j], recv.at[3 * w + j],
                                             device_id=(px, py, c), device_id_type=MESH).wait_recv()
        for cp in sends:
            cp.wait_send()
        for own in started:
            own.wait()

    return pl.pallas_call(
        body, name=name, in_specs=[ANY] * n, out_specs=[ANY] * n,
        out_shape=[jax.ShapeDtypeStruct((N_CHIPS,) + s.shape, s.dtype) for s in shards],
        scratch_shapes=[pltpu.SemaphoreType.DMA((3 * n,)), pltpu.SemaphoreType.DMA((3 * n,)), pltpu.SemaphoreType.DMA((n,))],
    )(*shards)


def _gather_chips_async(name, shards, collective_id):
    n = len(shards)

    def body(*refs):
        ins, outs = refs[:n], refs[n:2 * n]
        send, recv, loc = refs[2 * n:]
        x, y, c = _position()
        me = 2 * x + y
        chips = _other_chips(x, y)
        barrier = pltpu.get_barrier_semaphore()
        for px, py in chips:
            pl.semaphore_signal(barrier, inc=1, device_id=(px, py, c), device_id_type=MESH)
        pl.semaphore_wait(barrier, len(chips))
        started = []
        for w in range(n):
            own = pltpu.make_async_copy(ins[w], outs[w].at[me], loc.at[w])
            own.start()
            started.append(own)
        sends = []
        for w in range(n):
            for j, (px, py) in enumerate(chips):
                cp = pltpu.make_async_remote_copy(ins[w], outs[w].at[me], send.at[3 * w + j], recv.at[3 * w + j],
                                                  device_id=(px, py, c), device_id_type=MESH)
                cp.start()
                sends.append(cp)
        for w in range(n):
            for j, (px, py) in enumerate(chips):
                pltpu.make_async_remote_copy(ins[w], outs[w].at[2 * px + py], send.at[3 * w + j], recv.at[3 * w + j],
                                             device_id=(px, py, c), device_id_type=MESH).wait_recv()
        for cp in sends:
            cp.wait_send()
        for own in started:
            own.wait()

    return pl.kernel(
        body, name=name, mesh=_sequencer(),
        out_type=[jax.ShapeDtypeStruct((N_CHIPS,) + s.shape, s.dtype) for s in shards],
        scratch_types=[pltpu.SemaphoreType.DMA((3 * n,)), pltpu.SemaphoreType.DMA((3 * n,)), pltpu.SemaphoreType.DMA((n,))],
        compiler_params=pltpu.CompilerParams(collective_id=collective_id),
    )(*shards)


def _sibling_exchange(name, arrs):
    n = len(arrs)

    def body(*refs):
        ins, outs = refs[:n], refs[n:2 * n]
        send, recv = refs[2 * n:]
        x, y, c = _position()
        copies = [pltpu.make_async_remote_copy(ins[w], outs[w], send.at[w], recv.at[w], device_id=(x, y, 1 - c),
                                               device_id_type=MESH) for w in range(n)]
        for cp in copies:
            cp.start()
        for cp in copies:
            cp.wait()

    return pl.pallas_call(
        body, name=name, in_specs=[ANY] * n, out_specs=[ANY] * n,
        out_shape=[jax.ShapeDtypeStruct(a.shape, a.dtype) for a in arrs],
        scratch_shapes=[pltpu.SemaphoreType.DMA((n,)), pltpu.SemaphoreType.DMA((n,))],
    )(*arrs)


def _scatter_chips(name, parts):
    n = len(parts)

    def body(*refs):
        ins, outs = refs[:n], refs[n:2 * n]
        send, recv, loc = refs[2 * n:]
        x, y, c = _position()
        me = 2 * x + y
        chips = _other_chips(x, y)
        started = []
        for w in range(n):
            own = pltpu.make_async_copy(ins[w].at[me], outs[w].at[me], loc.at[w])
            own.start()
            started.append(own)
        sends = []
        for w in range(n):
            for j, (px, py) in enumerate(chips):
                cp = pltpu.make_async_remote_copy(ins[w].at[2 * px + py], outs[w].at[me], send.at[3 * w + j],
                                                  recv.at[3 * w + j], device_id=(px, py, c), device_id_type=MESH)
                cp.start()
                sends.append(cp)
        for w in range(n):
            for j, (px, py) in enumerate(chips):
                pltpu.make_async_remote_copy(ins[w].at[me], outs[w].at[2 * px + py], send.at[3 * w + j], recv.at[3 * w + j],
                                             device_id=(px, py, c), device_id_type=MESH).wait_recv()
        for cp in sends:
            cp.wait_send()
        for own in started:
            own.wait()

    return pl.pallas_call(
        body, name=name, in_specs=[ANY] * n, out_specs=[ANY] * n,
        out_shape=[jax.ShapeDtypeStruct(p.shape, p.dtype) for p in parts],
        scratch_shapes=[pltpu.SemaphoreType.DMA((3 * n,)), pltpu.SemaphoreType.DMA((3 * n,)), pltpu.SemaphoreType.DMA((n,))],
    )(*parts)


def _sequencer():
    return plsc.ScalarSubcoreMesh(axis_name="sequencer", num_cores=1)


def _sibling_exchange_async(name, arrs, collective_id):
    n = len(arrs)

    def body(*refs):
        ins, outs = refs[:n], refs[n:2 * n]
        send, recv = refs[2 * n:]
        x, y, c = _position()
        sibling = (x, y, 1 - c)
        barrier = pltpu.get_barrier_semaphore()
        pl.semaphore_signal(barrier, inc=1, device_id=sibling, device_id_type=MESH)
        pl.semaphore_wait(barrier, 1)
        copies = [pltpu.make_async_remote_copy(ins[w], outs[w], send.at[w], recv.at[w], device_id=sibling,
                                               device_id_type=MESH) for w in range(n)]
        for cp in copies:
            cp.start()
        for cp in copies:
            cp.wait()

    return pl.kernel(
        body, name=name, out_type=[jax.ShapeDtypeStruct(a.shape, a.dtype) for a in arrs],
        scratch_types=[pltpu.SemaphoreType.DMA((n,)), pltpu.SemaphoreType.DMA((n,))],
        compiler_params=pltpu.CompilerParams(collective_id=collective_id), mesh=_sequencer(),
    )(*arrs)


def _scatter_chips_async(name, parts, collective_id):
    n = len(parts)

    def body(*refs):
        ins, outs = refs[:n], refs[n:2 * n]
        send, recv, loc = refs[2 * n:]
        x, y, c = _position()
        me = 2 * x + y
        chips = _other_chips(x, y)
        barrier = pltpu.get_barrier_semaphore()
        for px, py in chips:
            pl.semaphore_signal(barrier, inc=1, device_id=(px, py, c), device_id_type=MESH)
        pl.semaphore_wait(barrier, len(chips))
        started = []
        for w in range(n):
            own = pltpu.make_async_copy(ins[w].at[me], outs[w].at[me], loc.at[w])
            own.start()
            started.append(own)
        sends = []
        for w in range(n):
            for j, (px, py) in enumerate(chips):
                cp = pltpu.make_async_remote_copy(ins[w].at[2 * px + py], outs[w].at[me], send.at[3 * w + j],
                                                  recv.at[3 * w + j], device_id=(px, py, c), device_id_type=MESH)
                cp.start()
                sends.append(cp)
        for w in range(n):
            for j, (px, py) in enumerate(chips):
                pltpu.make_async_remote_copy(ins[w].at[me], outs[w].at[2 * px + py], send.at[3 * w + j], recv.at[3 * w + j],
                                             device_id=(px, py, c), device_id_type=MESH).wait_recv()
        for cp in sends:
            cp.wait_send()
        for own in started:
            own.wait()

    return pl.kernel(
        body, name=name, out_type=[jax.ShapeDtypeStruct(p.shape, p.dtype) for p in parts],
        scratch_types=[pltpu.SemaphoreType.DMA((3 * n,)), pltpu.SemaphoreType.DMA((3 * n,)), pltpu.SemaphoreType.DMA((n,))],
        compiler_params=pltpu.CompilerParams(collective_id=collective_id), mesh=_sequencer(),
    )(*parts)


def _allreduce_rows(buf):
    R = buf.shape[0]

    def body(in_ref, out_ref, land, send, recv):
        x, y, c = _position()
        me = 4 * x + 2 * y + c
        land[pl.ds(me, 1)] = in_ref[...][None]
        peers = []
        for mask in range(1, N_DEV):
            px = 1 - x if mask & 4 else x
            py = 1 - y if mask & 2 else y
            pc = 1 - c if mask & 1 else c
            peers.append((px, py, pc))
        sends = []
        for k, peer in enumerate(peers):
            cp = pltpu.make_async_remote_copy(in_ref, land.at[me], send.at[k], recv.at[k], device_id=peer, device_id_type=MESH)
            cp.start()
            sends.append(cp)
        for k, (px, py, pc) in enumerate(peers):
            pltpu.make_async_remote_copy(in_ref, land.at[4 * px + 2 * py + pc], send.at[k], recv.at[k],
                                         device_id=(px, py, pc), device_id_type=MESH).wait_recv()
        for cp in sends:
            cp.wait_send()
        tot = land[0]
        for i in range(1, N_DEV):
            tot = tot + land[i]
        out_ref[...] = tot

    vm = pl.BlockSpec(memory_space=pltpu.VMEM)
    return pl.pallas_call(
        body, name="allreduce_small", in_specs=[vm], out_specs=vm, out_shape=jax.ShapeDtypeStruct((R, LANES), F32),
        scratch_shapes=[pltpu.VMEM((N_DEV, R, LANES), F32), pltpu.SemaphoreType.DMA((N_DEV - 1,)),
                        pltpu.SemaphoreType.DMA((N_DEV - 1,))],
    )(buf)


def _pair_sum(name, a, b, tr=512):
    shape = a.shape
    rows, cols = shape[0] * shape[1], shape[2]

    def body(a_ref, b_ref, o_ref):
        o_ref[...] = (a_ref[...].astype(F32) + b_ref[...].astype(F32)).astype(BF16)

    blk = pl.BlockSpec((tr, cols), lambda i: (i, 0))
    out = pl.pallas_call(
        body, name=name, grid=(rows // tr,), in_specs=[blk, blk], out_specs=blk,
        out_shape=jax.ShapeDtypeStruct((rows, cols), BF16),
    )(a.reshape(rows, cols), b.reshape(rows, cols))
    return out.reshape(shape)


def _adamw_math(w, m, v, g):
    m2 = ADAM_B1 * m + (1.0 - ADAM_B1) * g
    v2 = ADAM_B2 * v + (1.0 - ADAM_B2) * (g * g)
    m_hat = m2 / (1.0 - ADAM_B1 ** ADAM_STEP)
    v_hat = v2 / (1.0 - ADAM_B2 ** ADAM_STEP)
    delta = -ADAM_LR * (m_hat / (jnp.sqrt(v_hat) + ADAM_EPS) + ADAM_WD * w)
    return delta, m2, v2


def _adamw(name, w, m, v, g):
    r, c = w.shape
    stacked = g.ndim == 3
    tr = r if r <= 256 else (256 if r % 256 == 0 else 128)

    def body(w_ref, m_ref, v_ref, g_ref, go_ref, d_ref, m2_ref, v2_ref):
        if stacked:
            gv = g_ref[0].astype(F32)
            for i in range(1, N_CHIPS):
                gv = gv + g_ref[i].astype(F32)
        else:
            gv = g_ref[...]
        delta, m2, v2 = _adamw_math(w_ref[...], m_ref[...], v_ref[...], gv)
        go_ref[...] = gv
        d_ref[...] = delta
        m2_ref[...] = m2
        v2_ref[...] = v2

    blk = pl.BlockSpec((tr, c), lambda i: (i, 0))
    g_spec = pl.BlockSpec((N_CHIPS, tr, c), lambda i: (0, i, 0)) if stacked else blk
    out = jax.ShapeDtypeStruct((r, c), F32)
    return pl.pallas_call(
        body, name=name, grid=(r // tr,), in_specs=[blk, blk, blk, g_spec], out_specs=[blk] * 4, out_shape=[out] * 4,
    )(w, m, v, g)


def _pack_rows(pieces):
    flat = jnp.concatenate([p.reshape(-1) for p in pieces])
    rows = flat.shape[0] // LANES
    pad = (-rows) % 8
    return jnp.pad(flat.reshape(rows, LANES), ((0, pad), (0, 0)))


def _unpack_rows(buf, shapes):
    flat = buf.reshape(-1)
    out, at = [], 0
    for s in shapes:
        size = math.prod(s)
        out.append(flat[at:at + size].reshape(s))
        at += size
    return out


SMALL_NAMES = ("norm1_g", "gf_up", "gf_b", "gb_up", "gb_b", "gla_norm_g", "attn_norm_g", "norm2_g", "conv_w", "conv_b",
               "final_norm_g")
BIG_NAMES = ("w_in", "w_out", "w_gate", "w_up", "w_down")
WEIGHT_ORDER = ("norm1_g", "w_in", "gf_up", "gf_b", "gb_up", "gb_b", "gla_norm_g", "attn_norm_g", "w_out", "norm2_g",
                "w_gate", "w_up", "conv_w", "conv_b", "w_down", "final_norm_g")


def kernel(x, norm1_g, w_in, gf_up, gf_b, gb_up, gb_b, gla_norm_g, attn_norm_g, w_out, norm2_g, w_gate, w_up, conv_w, conv_b, w_down, final_norm_g, loss_target, m_norm1_g, m_w_in, m_gf_up, m_gf_b, m_gb_up, m_gb_b, m_gla_norm_g, m_attn_norm_g, m_w_out, m_norm2_g, m_w_gate, m_w_up, m_conv_w, m_conv_b, m_w_down, m_final_norm_g, v_norm1_g, v_w_in, v_gf_up, v_gf_b, v_gb_up, v_gb_b, v_gla_norm_g, v_attn_norm_g, v_w_out, v_norm2_g, v_w_gate, v_w_up, v_conv_w, v_conv_b, v_w_down, v_final_norm_g):
    w = dict(norm1_g=norm1_g, w_in=w_in, gf_up=gf_up, gf_b=gf_b, gb_up=gb_up, gb_b=gb_b, gla_norm_g=gla_norm_g,
             attn_norm_g=attn_norm_g, w_out=w_out, norm2_g=norm2_g, w_gate=w_gate, w_up=w_up, conv_w=conv_w, conv_b=conv_b,
             w_down=w_down, final_norm_g=final_norm_g)
    m = dict(norm1_g=m_norm1_g, w_in=m_w_in, gf_up=m_gf_up, gf_b=m_gf_b, gb_up=m_gb_up, gb_b=m_gb_b, gla_norm_g=m_gla_norm_g,
             attn_norm_g=m_attn_norm_g, w_out=m_w_out, norm2_g=m_norm2_g, w_gate=m_w_gate, w_up=m_w_up, conv_w=m_conv_w,
             conv_b=m_conv_b, w_down=m_w_down, final_norm_g=m_final_norm_g)
    v = dict(norm1_g=v_norm1_g, w_in=v_w_in, gf_up=v_gf_up, gf_b=v_gf_b, gb_up=v_gb_up, gb_b=v_gb_b, gla_norm_g=v_gla_norm_g,
             attn_norm_g=v_attn_norm_g, w_out=v_w_out, norm2_g=v_norm2_g, w_gate=v_w_gate, w_up=v_w_up, conv_w=v_conv_w,
             conv_b=v_conv_b, w_down=v_w_down, final_norm_g=v_final_norm_g)
    S = x.shape[1]
    chip = 2 * lax.axis_index("x") + lax.axis_index("y")
    n_in = IN_W // N_CHIPS
    n_ff = D_FF // N_CHIPS
    n_gk = GLA_K // N_CHIPS

    shard = {k: w[k][0].astype(BF16) for k in BIG_NAMES}
    small_shard = _pack_rows([gf_up[0], gb_up[0], conv_w[0]])
    small4, w_in4 = _gather_chips_async("gather_w_in", [small_shard, shard["w_in"]], 0)
    w_out4, w_gate4, w_up4 = _gather_chips_async("gather_w_mid", [shard["w_out"], shard["w_gate"], shard["w_up"]], 1)
    (w_down4,) = _gather_chips_async("gather_w_down", [shard["w_down"]], 2)
    w_in_full = jnp.transpose(w_in4, (1, 0, 2)).reshape(D_MODEL, IN_W)
    w_in_main = w_in_full[:, :IN_MAIN]
    w_in_z = jnp.pad(w_in_full[:, IN_MAIN:], ((0, 0), (0, Z_PAD - (IN_W - IN_MAIN))))
    rows_up = GATE_RANK * n_gk // LANES
    rows_cw = 3 * n_ff // LANES
    gf_full = jnp.transpose(small4[:, 0:rows_up].reshape(N_CHIPS, GATE_RANK, n_gk), (1, 0, 2)).reshape(GATE_RANK, GLA_K)
    gb_full = jnp.transpose(small4[:, rows_up:2 * rows_up].reshape(N_CHIPS, GATE_RANK, n_gk), (1, 0, 2)).reshape(GATE_RANK, GLA_K)
    cw_full = jnp.transpose(small4[:, 2 * rows_up:2 * rows_up + rows_cw].reshape(N_CHIPS, 3, n_ff), (1, 0, 2)).reshape(3, D_FF)
    wg = jnp.zeros((Z_PAD, 2 * GLA_K), F32)
    wg = wg.at[0:GATE_RANK, 0:GLA_K].set(gf_full).at[GATE_RANK:2 * GATE_RANK, GLA_K:].set(gb_full).astype(BF16)
    gate_bias = jnp.concatenate([gf_b, gb_b], axis=1)

    pending, contributions, next_id = [], {}, [3]

    def as_shards(group, arrays):
        if group == "w_in":
            d_w_in = jnp.concatenate([arrays["w_in_main"], arrays["w_in_z"][:, :IN_W - IN_MAIN]], axis=1)
            return dict(w_in=jnp.transpose(d_w_in.reshape(D_MODEL, N_CHIPS, n_in), (1, 0, 2)))
        if group == "w_out":
            return dict(w_out=arrays["w_out"].reshape(N_CHIPS, D_MODEL // N_CHIPS, D_MODEL))
        if group == "w_down":
            return dict(w_down=arrays["w_down"].reshape(N_CHIPS, n_ff, D_MODEL))
        return arrays

    out = {}

    def swap(group, arrays):
        mine = as_shards(group, arrays)
        pending.append((group, mine, _sibling_exchange_async(f"sibling_{group}", list(mine.values()), next_id[0])))
        next_id[0] += 1

    def sum_and_send(anchor):
        tag, mine, theirs = pending.pop()
        sums = [_pair_sum(f"pair_sum_{k}", mine[k], _after(t, *anchor)) for k, t in zip(mine, theirs)]
        contributions.update(zip(mine, _scatter_chips_async(f"scatter_{tag}", sums, next_id[0])))
        next_id[0] += 1
        return sums

    def update(names, anchor):
        for k in names:
            res = _adamw(f"adamw_{k}", w[k][0], m[k][0], v[k][0], _after(contributions[k], *anchor))
            out[k] = [r[None] for r in res]
        return [out[k][0] for k in names]

    def on_grad(event, arrays):
        anchor = list(arrays.values())
        held = []
        if event in ("w_gate_w_up", "w_out", "mid", "last"):
            held += sum_and_send(anchor)
        if event == "mid":
            held += update(("w_down", "w_gate", "w_up"), anchor)
        if event == "last":
            held += update(("w_out",), anchor)
        if event in ("w_down", "w_gate_w_up", "w_out", "w_in"):
            swap(event, arrays)
        return held

    grad_x, _, small = _local_step(
        x[0], loss_target[0], norm1_g, w_in_main, w_in_z, wg, gate_bias, gla_norm_g, attn_norm_g,
        w_out4.reshape(D_MODEL, D_MODEL), norm2_g, w_gate4, w_up4, cw_full, conv_b, w_down4.reshape(D_FF, D_MODEL), final_norm_g,
        on_grad=on_grad)
    update(("w_in",), [grad_x])

    d_gf_up = small["wg"][0:GATE_RANK, 0:GLA_K]
    d_gb_up = small["wg"][GATE_RANK:2 * GATE_RANK, GLA_K:]
    pieces = [small["loss"], small["norm1_g"], d_gf_up, small["gate_bias"][:, :GLA_K], d_gb_up, small["gate_bias"][:, GLA_K:],
              small["gla_norm_g"], small["attn_norm_g"], small["norm2_g"], small["conv_w"], small["conv_b"], small["final_norm_g"]]
    total = _allreduce_rows(_pack_rows(pieces))
    summed = _unpack_rows(total, [p.shape for p in pieces])
    loss = summed[0][0, 0]
    g_small = dict(zip(SMALL_NAMES, summed[1:]))
    g_small["gf_up"] = lax.dynamic_slice_in_dim(g_small["gf_up"], chip * n_gk, n_gk, axis=1)
    g_small["gb_up"] = lax.dynamic_slice_in_dim(g_small["gb_up"], chip * n_gk, n_gk, axis=1)
    g_small["conv_w"] = lax.dynamic_slice_in_dim(g_small["conv_w"], chip * n_ff, n_ff, axis=1)
    packed = [_pack_rows([t[k] for k in SMALL_NAMES]) for t in (w, m, v, g_small)]
    res = _adamw("adamw_small", *packed)
    shapes = [w[k].shape for k in SMALL_NAMES]
    for k, vals in zip(SMALL_NAMES, zip(*[_unpack_rows(r, shapes) for r in res])):
        out[k] = list(vals)

    grads, deltas, new_m, new_v = ([out[k][i] for k in WEIGHT_ORDER] for i in range(4))
    return (loss, grad_x[None], *grads, *deltas, *new_m, *new_v)
```

```python
import functools
import math

import jax
import jax.numpy as jnp
from jax import lax
from jax.experimental import pallas as pl
from jax.experimental.pallas import tpu as pltpu
from jax.experimental.pallas import tpu_sc as plsc

F32 = jnp.float32
BF16 = jnp.bfloat16

D_MODEL = 2048
ATTN_W = 1024
HEAD = 128
N_HEADS = 8
N_SIDE = 64
DILATIONS = (1, 4, 16)
ROPE_THETA = 500000.0
ROPE_DIM = 32
GLA_K = 512
GLA_V = 1024
GLA_HEADS = 4
GLA_DK = 128
GLA_DV = 256
GATE_RANK = 16
GATE_NORM = 16.0
CHUNK = 64
IN_MAIN = 6144
IN_W = 6176
Z_W = IN_W - IN_MAIN
D_FF = 5632
EPS = 1e-6
N_CHIPS = 4
N_DEV = 8
LANES = 128

ADAM_LR = 0.001
ADAM_B1 = 0.9
ADAM_B2 = 0.999
ADAM_EPS = 1e-08
ADAM_WD = 0.01
ADAM_STEP = 10

NEG = -1e30
MESH = pl.DeviceIdType.MESH
ANY = pl.BlockSpec(memory_space=pl.ANY)

NN = ((1,), (0,))
NT = ((1,), (1,))
TN = ((0,), (0,))


def _dot(a, b, dims=NN):
    return lax.dot_general(a, b, (dims, ((), ())), preferred_element_type=F32)


def _sigmoid(x):
    return 1.0 / (1.0 + jnp.exp(-x))


def _after(x, *deps):
    return lax.optimization_barrier((x,) + deps)[0]


def _matmul(name, pairs, grid, out_shape, out_spec, nk, res=None, into=None):
    n_in = 2 * len(pairs) + (res is not None)
    dims = [p[4] for p in pairs]

    n_ops = n_in + (into is not None)

    def body(*refs):
        ins, o_ref = refs[:n_in], refs[n_ops]

        def partial_sum():
            tot = None
            for p, dn in enumerate(dims):
                a, b = ins[2 * p][...], ins[2 * p + 1][...]
                t = _dot(a.astype(BF16), b.astype(BF16), dn)
                tot = t if tot is None else tot + t
            return tot

        if nk == 1:
            t = partial_sum()
            if res is not None:
                t = t + ins[-1][...]
            o_ref[...] = t.astype(o_ref.dtype)
        else:
            acc_ref = refs[n_ops + 1]
            k = pl.program_id(2)

            @pl.when(k == 0)
            def _():
                if res is not None:
                    acc_ref[...] = ins[-1][...]
                else:
                    acc_ref[...] = jnp.zeros_like(acc_ref)

            acc_ref[...] += partial_sum()

            @pl.when(k == nk - 1)
            def _():
                o_ref[...] = acc_ref[...].astype(o_ref.dtype)

    operands, in_specs = [], []
    for a, a_spec, b, b_spec, _ in pairs:
        operands += [a, b]
        in_specs += [a_spec, b_spec]
    if res is not None:
        operands.append(res[0])
        in_specs.append(res[1])
    acc_shape = tuple(s for s in out_spec.block_shape if s is not None)
    scratch = [pltpu.VMEM(acc_shape, F32)] if nk > 1 else []
    aliases = {}
    if into is not None:
        aliases = {len(operands): 0}
        operands.append(into)
        in_specs.append(ANY)
    return pl.pallas_call(
        body, name=name, grid=grid, in_specs=in_specs, out_specs=out_spec, out_shape=out_shape, scratch_shapes=scratch,
        input_output_aliases=aliases,
    )(*operands)


def _mm_nn(name, a, b, tm, tn, out_dtype, res=None):
    M, K = a.shape
    N = b.shape[1]
    pairs = [(a, pl.BlockSpec((tm, K), lambda j, i: (i, 0)), b, pl.BlockSpec((K, tn), lambda j, i: (0, j)), NN)]
    r = None if res is None else (res, pl.BlockSpec((tm, tn), lambda j, i: (i, j)))
    return _matmul(name, pairs, (N // tn, M // tm), jax.ShapeDtypeStruct((M, N), out_dtype),
                   pl.BlockSpec((tm, tn), lambda j, i: (i, j)), 1, r)


def _mm_nn_sharded(name, a, b4, tm, out_dtype):
    M, K = a.shape
    n = b4.shape[2]
    pairs = [(a, pl.BlockSpec((tm, K), lambda j, i: (i, 0)), b4, pl.BlockSpec((None, K, n), lambda j, i: (j, 0, 0)), NN)]
    return _matmul(name, pairs, (N_CHIPS, M // tm), jax.ShapeDtypeStruct((M, N_CHIPS * n), out_dtype),
                   pl.BlockSpec((tm, n), lambda j, i: (i, j)), 1)


def _mm_nt(name, a, b, tm, tn, out_dtype, res=None, n_out=None):
    M, K = a.shape
    N = b.shape[0] if n_out is None else n_out
    pairs = [(a, pl.BlockSpec((tm, K), lambda j, i: (i, 0)), b, pl.BlockSpec((tn, K), lambda j, i: (j, 0)), NT)]
    r = None if res is None else (res, pl.BlockSpec((tm, tn), lambda j, i: (i, j)))
    return _matmul(name, pairs, (N // tn, M // tm), jax.ShapeDtypeStruct((M, N), out_dtype),
                   pl.BlockSpec((tm, tn), lambda j, i: (i, j)), 1, r)


def _mm_tn(name, a, g, tka, tn, tmm, out_dtype, out3=None, rows_out=None):
    M, Ka = a.shape
    N = g.shape[1]
    pairs = [(a, pl.BlockSpec((tmm, tka), lambda i, j, k: (k, i)), g, pl.BlockSpec((tmm, tn), lambda i, j, k: (k, j)), TN)]
    if out3 is None:
        shape, spec = (Ka if rows_out is None else rows_out, N), pl.BlockSpec((tka, tn), lambda i, j, k: (i, j))
    else:
        shape, spec = (N // out3, Ka, out3), pl.BlockSpec((None, tka, tn), lambda i, j, k: (j, i, 0))
    return _matmul(name, pairs, (Ka // tka, N // tn, M // tmm), jax.ShapeDtypeStruct(shape, out_dtype), spec, M // tmm)


def _rms_fwd(name, x, g, tm=512):
    S, D = x.shape

    def body(x_ref, g_ref, o_ref):
        xv = x_ref[...]
        r = lax.rsqrt(jnp.mean(xv * xv, axis=-1, keepdims=True) + EPS)
        o_ref[...] = (xv * r * g_ref[...]).astype(o_ref.dtype)

    return pl.pallas_call(
        body, name=name, grid=(S // tm,),
        in_specs=[pl.BlockSpec((tm, D), lambda i: (i, 0)), pl.BlockSpec((1, D), lambda i: (0, 0))],
        out_specs=pl.BlockSpec((tm, D), lambda i: (i, 0)), out_shape=jax.ShapeDtypeStruct((S, D), BF16),
    )(x, g)


def _rms_bwd(name, x, g, dn, dres, tm=512):
    S, D = x.shape

    def body(x_ref, g_ref, dn_ref, dres_ref, dx_ref, dxb_ref, dg_ref):
        i = pl.program_id(0)

        @pl.when(i == 0)
        def _():
            dg_ref[...] = jnp.zeros_like(dg_ref)

        xv = x_ref[...]
        r = lax.rsqrt(jnp.mean(xv * xv, axis=-1, keepdims=True) + EPS)
        xhat = xv * r
        dnv = dn_ref[...].astype(F32)
        dg_ref[...] += jnp.sum(dnv * xhat, axis=0, keepdims=True)
        t = dnv * g_ref[...]
        dx = r * (t - xhat * jnp.mean(t * xhat, axis=-1, keepdims=True)) + dres_ref[...]
        dx_ref[...] = dx
        dxb_ref[...] = dx.astype(BF16)

    row = pl.BlockSpec((tm, D), lambda i: (i, 0))
    vec = pl.BlockSpec((1, D), lambda i: (0, 0))
    return pl.pallas_call(
        body, name=name, grid=(S // tm,), in_specs=[row, vec, row, row], out_specs=[row, row, vec],
        out_shape=[jax.ShapeDtypeStruct((S, D), F32), jax.ShapeDtypeStruct((S, D), BF16), jax.ShapeDtypeStruct((1, D), F32)],
    )(x, g, dn, dres)


def _final_loss(h2, g, target, tm=512):
    S, D = h2.shape

    def body(x_ref, g_ref, t_ref, loss_ref, dg_ref, dx_ref, dxb_ref):
        i = pl.program_id(0)

        @pl.when(i == 0)
        def _():
            loss_ref[...] = jnp.zeros_like(loss_ref)
            dg_ref[...] = jnp.zeros_like(dg_ref)

        xv = x_ref[...]
        r = lax.rsqrt(jnp.mean(xv * xv, axis=-1, keepdims=True) + EPS)
        xhat = xv * r
        gv = g_ref[...]
        diff = xhat * gv - t_ref[...]
        per_tok = jnp.mean(diff * diff, axis=-1, keepdims=True)
        loss_ref[...] += 0.5 * jnp.sum(per_tok, axis=0, keepdims=True)
        dy = diff * (1.0 / D)
        dg_ref[...] += jnp.sum(dy * xhat, axis=0, keepdims=True)
        t = dy * gv
        dx = r * (t - xhat * jnp.mean(t * xhat, axis=-1, keepdims=True))
        dx_ref[...] = dx
        dxb_ref[...] = dx.astype(BF16)

    row = pl.BlockSpec((tm, D), lambda i: (i, 0))
    vec = pl.BlockSpec((1, D), lambda i: (0, 0))
    return pl.pallas_call(
        body, name="final_loss", grid=(S // tm,), in_specs=[row, vec, row],
        out_specs=[pl.BlockSpec((1, LANES), lambda i: (0, 0)), vec, row, row],
        out_shape=[jax.ShapeDtypeStruct((1, LANES), F32), jax.ShapeDtypeStruct((1, D), F32),
                   jax.ShapeDtypeStruct((S, D), F32), jax.ShapeDtypeStruct((S, D), BF16)],
    )(h2, g, target)


def _rope_tables(S):
    pos = jnp.arange(S, dtype=F32)
    inv_freq = ROPE_THETA ** (-jnp.arange(0, ROPE_DIM, 2, dtype=F32) / ROPE_DIM)
    ang = pos[:, None] * inv_freq[None, :]
    cos, sin = jnp.cos(ang), jnp.sin(ang)
    half = ROPE_DIM // 2
    rest = HEAD - ROPE_DIM
    z_h, z_r = jnp.zeros((S, half), F32), jnp.zeros((S, rest), F32)
    tab_c = jnp.concatenate([cos, cos, jnp.ones((S, rest), F32)], axis=1)
    tab_up = jnp.concatenate([z_h, sin, z_r], axis=1)
    tab_dn = jnp.concatenate([-sin, z_h, z_r], axis=1)
    return tab_c, tab_up, tab_dn


def _rope_head(t, c, up, dn):
    half = ROPE_DIM // 2
    return t * c + pltpu.roll(t, half, axis=1) * up + pltpu.roll(t, HEAD - half, axis=1) * dn


def _rope_fwd(proj, tabs, tm=512):
    S = proj.shape[0]
    W = 2 * ATTN_W

    def body(p_ref, c_ref, up_ref, dn_ref, o_ref):
        c, up, dn = c_ref[...], up_ref[...], dn_ref[...]
        for h in range(W // HEAD):
            sl = slice(h * HEAD, (h + 1) * HEAD)
            o_ref[:, sl] = _rope_head(p_ref[:, sl].astype(F32), c, up, dn).astype(BF16)

    tab = pl.BlockSpec((tm, HEAD), lambda i: (i, 0))
    return pl.pallas_call(
        body, name="rope_fwd", grid=(S // tm,), in_specs=[pl.BlockSpec((tm, W), lambda i: (i, 0)), tab, tab, tab],
        out_specs=pl.BlockSpec((tm, W), lambda i: (i, 0)), out_shape=jax.ShapeDtypeStruct((S, W), BF16),
    )(proj, *tabs)


def _attn_grad_merge(dqs, dks, dvs, tabs, tm=256):
    S = dqs[0].shape[0]

    def body(*refs):
        q_refs, k_refs, v_refs = refs[0:3], refs[3:6], refs[6:9]
        c, up, dn = refs[9][...], refs[10][...], refs[11][...]
        o_ref = refs[12]
        for h in range(N_HEADS):
            sl = slice(h * HEAD, (h + 1) * HEAD)
            for part, rs in ((0, q_refs), (1, k_refs)):
                t = rs[0][:, sl].astype(F32) + rs[1][:, sl].astype(F32) + rs[2][:, sl].astype(F32)
                osl = slice(part * ATTN_W + h * HEAD, part * ATTN_W + (h + 1) * HEAD)
                o_ref[:, osl] = _rope_head(t, c, -up, -dn).astype(BF16)
        o_ref[:, 2 * ATTN_W:] = (v_refs[0][...].astype(F32) + v_refs[1][...].astype(F32)
                                 + v_refs[2][...].astype(F32)).astype(BF16)

    blk = pl.BlockSpec((tm, ATTN_W), lambda i: (i, 0))
    tab = pl.BlockSpec((tm, HEAD), lambda i: (i, 0))
    return pl.pallas_call(
        body, name="attn_grad_merge", grid=(S // tm,), in_specs=[blk] * 9 + [tab] * 3,
        out_specs=pl.BlockSpec((tm, 3 * ATTN_W), lambda i: (i, 0)), out_shape=jax.ShapeDtypeStruct((S, 3 * ATTN_W), BF16),
    )(*dqs, *dks, *dvs, *tabs)


def _band_specs(T, L, width, ncol, col):
    ratio = T // N_SIDE
    nhb = L // N_SIDE
    cur = pl.BlockSpec((T, width), lambda r, i: (i, r * ncol + col))
    prev = pl.BlockSpec((N_SIDE, width), lambda r, i: (jnp.maximum(i * ratio - 1, 0), r * ncol + col))
    nxt = pl.BlockSpec((N_SIDE, width), lambda r, i: (jnp.minimum((i + 1) * ratio, nhb - 1), r * ncol + col))
    return cur, prev, nxt


def _band_mask(T, L):
    i = pl.program_id(1)
    row = lax.broadcasted_iota(jnp.int32, (T, T + 2 * N_SIDE), 0)
    col = lax.broadcasted_iota(jnp.int32, (T, T + 2 * N_SIDE), 1)
    pos = i * T - N_SIDE + col
    return (col >= row) & (col <= row + 2 * N_SIDE) & (pos >= 0) & (pos < L)


def _window(p_ref, c_ref, n_ref, sl=None):
    if sl is None:
        return jnp.concatenate([p_ref[...], c_ref[...], n_ref[...]], axis=0)
    return jnp.concatenate([p_ref[:, sl], c_ref[:, sl], n_ref[:, sl]], axis=0)


def _attn_dims(S, d):
    L = S // d
    T = min(128, L)
    return L, T, (d, L // T)


def _v_view(proj, L, d):
    if d == 1:
        return proj, IN_MAIN // ATTN_W, 2
    return proj[:, 2 * ATTN_W:3 * ATTN_W].reshape(L, d * ATTN_W), 1, 0


def _attn_fwd(qk, proj, d):
    S = qk.shape[0]
    L, T, grid = _attn_dims(S, d)
    scale = HEAD ** -0.5
    qk_v = qk.reshape(L, d * 2 * ATTN_W)
    pj_v, ncol_p, vcol = _v_view(proj, L, d)

    def body(q_ref, kp, kc, kn, vp, vc, vn, o_ref, lse_ref):
        mask = _band_mask(T, L)
        lane = lax.broadcasted_iota(jnp.int32, (T, LANES), 1)
        lse_tile = jnp.zeros((T, LANES), F32)
        for h in range(N_HEADS):
            sl = slice(h * HEAD, (h + 1) * HEAD)
            kw, vw = _window(kp, kc, kn, sl), _window(vp, vc, vn, sl)
            s = jnp.where(mask, _dot(q_ref[:, sl], kw, NT) * scale, NEG)
            m = jnp.max(s, axis=1, keepdims=True)
            p = jnp.exp(s - m)
            l = jnp.sum(p, axis=1, keepdims=True)
            o_ref[:, sl] = (_dot(p.astype(BF16), vw) / l).astype(BF16)
            lse_tile = jnp.where(lane == h, m + jnp.log(l), lse_tile)
        lse_ref[...] = lse_tile

    q_cur, _, _ = _band_specs(T, L, ATTN_W, 2, 0)
    k_specs = _band_specs(T, L, ATTN_W, 2, 1)
    v_specs = _band_specs(T, L, ATTN_W, ncol_p, vcol)
    o, lse = pl.pallas_call(
        body, name=f"attn_fwd_d{d}", grid=grid,
        in_specs=[q_cur, k_specs[1], k_specs[0], k_specs[2], v_specs[1], v_specs[0], v_specs[2]],
        out_specs=[pl.BlockSpec((T, ATTN_W), lambda r, i: (i, r)), pl.BlockSpec((T, LANES), lambda r, i: (i, r))],
        out_shape=[jax.ShapeDtypeStruct((L, d * ATTN_W), BF16), jax.ShapeDtypeStruct((L, d * LANES), F32)],
    )(qk_v, qk_v, qk_v, qk_v, pj_v, pj_v, pj_v)
    return o.reshape(S, ATTN_W), lse.reshape(S, LANES)


def _attn_combine(outs, lses, g, tm=256):
    S = outs[0].shape[0]

    def body(o1, o2, o3, l1, l2, l3, g_ref, ao_ref, o_ref, lse_ref):
        lane = lax.broadcasted_iota(jnp.int32, (tm, LANES), 1)
        lse_tile = jnp.zeros((tm, LANES), F32)
        a = [l1[...], l2[...], l3[...]]
        ssq = jnp.zeros((tm, 1), F32)
        for h in range(N_HEADS):
            sl = slice(h * HEAD, (h + 1) * HEAD)
            a1, a2, a3 = (t[:, h:h + 1] for t in a)
            mx = jnp.maximum(jnp.maximum(a1, a2), a3)
            e1, e2, e3 = jnp.exp(a1 - mx), jnp.exp(a2 - mx), jnp.exp(a3 - mx)
            den = e1 + e2 + e3
            oh = (e1 * o1[:, sl].astype(F32) + e2 * o2[:, sl].astype(F32) + e3 * o3[:, sl].astype(F32)) / den
            o_ref[:, sl] = oh
            ssq = ssq + jnp.sum(oh * oh, axis=1, keepdims=True)
            lse_tile = jnp.where(lane == h, mx + jnp.log(den), lse_tile)
        lse_ref[...] = lse_tile
        r = lax.rsqrt(ssq * (1.0 / ATTN_W) + EPS)
        ao_ref[...] = (o_ref[...] * r * g_ref[...]).astype(BF16)

    blk = pl.BlockSpec((tm, ATTN_W), lambda i: (i, 0))
    ls = pl.BlockSpec((tm, LANES), lambda i: (i, 0))
    return pl.pallas_call(
        body, name="attn_combine", grid=(S // tm,),
        in_specs=[blk, blk, blk, ls, ls, ls, pl.BlockSpec((1, ATTN_W), lambda i: (0, 0))], out_specs=[blk, blk, ls],
        out_shape=[jax.ShapeDtypeStruct((S, ATTN_W), BF16), jax.ShapeDtypeStruct((S, ATTN_W), F32),
                   jax.ShapeDtypeStruct((S, LANES), F32)],
    )(*outs, *lses, g)


def _attn_norm_bwd(o, g, dao, tm=256):
    S = o.shape[0]

    def body(o_ref, g_ref, dao_ref, do_ref, dl_ref, dg_ref):
        i = pl.program_id(0)

        @pl.when(i == 0)
        def _():
            dg_ref[...] = jnp.zeros_like(dg_ref)

        ov = o_ref[...]
        r = lax.rsqrt(jnp.mean(ov * ov, axis=-1, keepdims=True) + EPS)
        ohat = ov * r
        dn = dao_ref[...].astype(F32)
        dg_ref[...] += jnp.sum(dn * ohat, axis=0, keepdims=True)
        t = dn * g_ref[...]
        do = r * (t - ohat * jnp.mean(t * ohat, axis=-1, keepdims=True))
        do_ref[...] = do.astype(BF16)
        prod = do * ov
        lane = lax.broadcasted_iota(jnp.int32, (tm, LANES), 1)
        tile = jnp.zeros((tm, LANES), F32)
        for h in range(N_HEADS):
            tile = jnp.where(lane == h, jnp.sum(prod[:, h * HEAD:(h + 1) * HEAD], axis=1, keepdims=True), tile)
        dl_ref[...] = tile

    blk = pl.BlockSpec((tm, ATTN_W), lambda i: (i, 0))
    vec = pl.BlockSpec((1, ATTN_W), lambda i: (0, 0))
    return pl.pallas_call(
        body, name="attn_norm_bwd", grid=(S // tm,),
        in_specs=[blk, vec, pl.BlockSpec((tm, ATTN_W), lambda i: (i, 0))],
        out_specs=[blk, pl.BlockSpec((tm, LANES), lambda i: (i, 0)), vec],
        out_shape=[jax.ShapeDtypeStruct((S, ATTN_W), BF16), jax.ShapeDtypeStruct((S, LANES), F32),
                   jax.ShapeDtypeStruct((1, ATTN_W), F32)],
    )(o, g, dao)


def _attn_bwd_dq(qk, proj, do, lse, delta, d):
    S = qk.shape[0]
    L, T, grid = _attn_dims(S, d)
    scale = HEAD ** -0.5
    qk_v = qk.reshape(L, d * 2 * ATTN_W)
    pj_v, ncol_p, vcol = _v_view(proj, L, d)

    def body(q_ref, kp, kc, kn, vp, vc, vn, do_ref, lse_ref, dl_ref, dq_ref):
        mask = _band_mask(T, L)
        lse_t, dl_t = lse_ref[...], dl_ref[...]
        for h in range(N_HEADS):
            sl = slice(h * HEAD, (h + 1) * HEAD)
            kw, vw = _window(kp, kc, kn, sl), _window(vp, vc, vn, sl)
            s = _dot(q_ref[:, sl], kw, NT) * scale
            p = jnp.where(mask, jnp.exp(s - lse_t[:, h:h + 1]), 0.0)
            dp = _dot(do_ref[:, sl], vw, NT)
            ds = (p * (dp - dl_t[:, h:h + 1]) * scale).astype(BF16)
            dq_ref[:, sl] = _dot(ds, kw).astype(BF16)

    q_cur, _, _ = _band_specs(T, L, ATTN_W, 2, 0)
    k_specs = _band_specs(T, L, ATTN_W, 2, 1)
    v_specs = _band_specs(T, L, ATTN_W, ncol_p, vcol)
    cur = pl.BlockSpec((T, ATTN_W), lambda r, i: (i, r))
    stat = pl.BlockSpec((T, LANES), lambda r, i: (i, r))
    dq = pl.pallas_call(
        body, name=f"attn_bwd_dq_d{d}", grid=grid,
        in_specs=[q_cur, k_specs[1], k_specs[0], k_specs[2], v_specs[1], v_specs[0], v_specs[2], cur, stat, stat],
        out_specs=cur, out_shape=jax.ShapeDtypeStruct((L, d * ATTN_W), BF16),
    )(qk_v, qk_v, qk_v, qk_v, pj_v, pj_v, pj_v, do.reshape(L, d * ATTN_W), lse.reshape(L, d * LANES),
      delta.reshape(L, d * LANES))
    return dq.reshape(S, ATTN_W)


def _attn_bwd_dkv(qk, proj, do, lse, delta, d):
    S = qk.shape[0]
    L, T, grid = _attn_dims(S, d)
    scale = HEAD ** -0.5
    qk_v = qk.reshape(L, d * 2 * ATTN_W)
    pj_v, ncol_p, vcol = _v_view(proj, L, d)

    def body(k_ref, v_ref, qp, qc, qn, dop, doc, don, lp, lc, ln, dlp, dlc, dln, dk_ref, dv_ref):
        mask = _band_mask(T, L)
        lse_t = _window(lp, lc, ln).T
        dl_t = _window(dlp, dlc, dln).T
        for h in range(N_HEADS):
            sl = slice(h * HEAD, (h + 1) * HEAD)
            qw, dow = _window(qp, qc, qn, sl), _window(dop, doc, don, sl)
            st = _dot(k_ref[:, sl], qw, NT) * scale
            pt = jnp.where(mask, jnp.exp(st - lse_t[h:h + 1, :]), 0.0)
            dv_ref[:, sl] = _dot(pt.astype(BF16), dow).astype(BF16)
            dpt = _dot(v_ref[:, sl], dow, NT)
            dst = (pt * (dpt - dl_t[h:h + 1, :]) * scale).astype(BF16)
            dk_ref[:, sl] = _dot(dst, qw).astype(BF16)

    q_specs = _band_specs(T, L, ATTN_W, 2, 0)
    k_cur, _, _ = _band_specs(T, L, ATTN_W, 2, 1)
    v_cur, _, _ = _band_specs(T, L, ATTN_W, ncol_p, vcol)
    do_specs = _band_specs(T, L, ATTN_W, 1, 0)
    st_specs = _band_specs(T, L, LANES, 1, 0)
    cur = pl.BlockSpec((T, ATTN_W), lambda r, i: (i, r))
    do_v, lse_v, dl_v = do.reshape(L, d * ATTN_W), lse.reshape(L, d * LANES), delta.reshape(L, d * LANES)
    dk, dv = pl.pallas_call(
        body, name=f"attn_bwd_dkv_d{d}", grid=grid,
        in_specs=[k_cur, v_cur, q_specs[1], q_specs[0], q_specs[2], do_specs[1], do_specs[0], do_specs[2],
                  st_specs[1], st_specs[0], st_specs[2], st_specs[1], st_specs[0], st_specs[2]],
        out_specs=[cur, cur],
        out_shape=[jax.ShapeDtypeStruct((L, d * ATTN_W), BF16), jax.ShapeDtypeStruct((L, d * ATTN_W), BF16)],
    )(qk_v, pj_v, qk_v, qk_v, qk_v, do_v, do_v, do_v, lse_v, lse_v, lse_v, dl_v, dl_v, dl_v)
    return dk.reshape(S, ATTN_W), dv.reshape(S, ATTN_W)


def _cumsum_rows(x, reverse):
    n = x.shape[0]
    row = lax.broadcasted_iota(jnp.int32, x.shape, 0)
    s = 1
    while s < n:
        if reverse:
            x = x + jnp.where(row < n - s, pltpu.roll(x, n - s, axis=0), 0.0)
        else:
            x = x + jnp.where(row >= s, pltpu.roll(x, s, axis=0), 0.0)
        s *= 2
    return x


def _gla_chunk_terms(q_ref, k_ref, v_ref, g_ref, h, reverse):
    ksl = slice(h * GLA_DK, (h + 1) * GLA_DK)
    q = q_ref[:, ksl].astype(F32) * (GLA_DK ** -0.5)
    k = k_ref[:, ksl].astype(F32)
    v = v_ref[:, h * GLA_DV:(h + 1) * GLA_DV]
    b = _cumsum_rows(g_ref[:, ksl], reverse)
    r_ref = CHUNK // 2 if reverse else CHUNK // 2 - 1
    r_last = 0 if reverse else CHUNK - 1
    b_ref, b_last = b[r_ref:r_ref + 1, :], b[r_last:r_last + 1, :]
    ii = lax.broadcasted_iota(jnp.int32, (CHUNK, CHUNK), 0)
    jj = lax.broadcasted_iota(jnp.int32, (CHUNK, CHUNK), 1)
    causal = (jj >= ii) if reverse else (jj <= ii)
    e_q, e_k = jnp.exp(b - b_ref), jnp.exp(b_ref - b)
    e_in, e_st = jnp.exp(b), jnp.exp(b_last - b)
    return dict(q=q, k=k, v=v, b=b, causal=causal, e_q=e_q, e_k=e_k, e_in=e_in, e_st=e_st, dec=jnp.exp(b_last),
                qe=q * e_q, ke=k * e_k, q_in=q * e_in, k_st=k * e_st, r_ref=r_ref, r_last=r_last)


def _gla_specs(n, order):
    q = pl.BlockSpec((CHUNK, GLA_K), lambda c: (order(c), 3 * ATTN_W // GLA_K))
    k = pl.BlockSpec((CHUNK, GLA_K), lambda c: (order(c), 3 * ATTN_W // GLA_K + 1))
    v = pl.BlockSpec((CHUNK, GLA_V), lambda c: (order(c), (3 * ATTN_W + 2 * GLA_K) // GLA_V))
    return q, k, v


def _gla_fwd(proj, gates, reverse, o_prev=None):
    S = proj.shape[0]
    n = S // CHUNK
    order = (lambda c: n - 1 - c) if reverse else (lambda c: c)
    gcol = 1 if reverse else 0

    def body(*refs):
        if o_prev is None:
            q_ref, k_ref, v_ref, g_ref, o_ref, st_ref, state = refs
        else:
            q_ref, k_ref, v_ref, g_ref, op_ref, o_ref, st_ref, state = refs
        c = pl.program_id(0)

        @pl.when(c == 0)
        def _():
            state[...] = jnp.zeros_like(state)

        for h in range(GLA_HEADS):
            t = _gla_chunk_terms(q_ref, k_ref, v_ref, g_ref, h, reverse)
            a = jnp.where(t["causal"], _dot(t["qe"].astype(BF16), t["ke"].astype(BF16), NT), 0.0)
            o = _dot(a.astype(BF16), t["v"])
            st = state[h]
            st_b = st.astype(BF16)
            st_ref[0, h] = st_b
            o = o + _dot(t["q_in"].astype(BF16), st_b, NT)
            state[h] = st * t["dec"] + _dot(t["v"], t["k_st"].astype(BF16), TN)
            vsl = slice(h * GLA_DV, (h + 1) * GLA_DV)
            if o_prev is not None:
                o = o + op_ref[:, vsl]
            o_ref[:, vsl] = o

    q_spec, k_spec, v_spec = _gla_specs(n, order)
    o_spec = pl.BlockSpec((CHUNK, GLA_V), lambda c: (order(c), 0))
    in_specs = [q_spec, k_spec, v_spec, pl.BlockSpec((CHUNK, GLA_K), lambda c: (order(c), gcol))]
    operands = [proj, proj, proj, gates]
    if o_prev is not None:
        in_specs.append(o_spec)
        operands.append(o_prev)
    return pl.pallas_call(
        body, name="gla_fwd_rev" if reverse else "gla_fwd", grid=(n,), in_specs=in_specs,
        out_specs=[o_spec, pl.BlockSpec((1, GLA_HEADS, GLA_DV, GLA_DK), lambda c: (order(c), 0, 0, 0))],
        out_shape=[jax.ShapeDtypeStruct((S, GLA_V), F32), jax.ShapeDtypeStruct((n, GLA_HEADS, GLA_DV, GLA_DK), BF16)],
        scratch_shapes=[pltpu.VMEM((GLA_HEADS, GLA_DV, GLA_DK), F32)],
    )(*operands)


def _gla_bwd(proj, gates, states, do, reverse, prev=None):
    S = proj.shape[0]
    n = S // CHUNK
    order = (lambda c: c) if reverse else (lambda c: n - 1 - c)
    gcol = 1 if reverse else 0
    out_dt = F32 if prev is None else BF16

    def body(*refs):
        if prev is None:
            q_ref, k_ref, v_ref, g_ref, st_ref, do_ref, dq_ref, dk_ref, dv_ref, dg_ref, dstate = refs
        else:
            q_ref, k_ref, v_ref, g_ref, st_ref, do_ref, pq, pk, pv, dq_ref, dk_ref, dv_ref, dg_ref, dstate = refs
        c = pl.program_id(0)

        @pl.when(c == 0)
        def _():
            dstate[...] = jnp.zeros_like(dstate)

        row = lax.broadcasted_iota(jnp.int32, (CHUNK, GLA_DK), 0)
        for h in range(GLA_HEADS):
            t = _gla_chunk_terms(q_ref, k_ref, v_ref, g_ref, h, reverse)
            ksl = slice(h * GLA_DK, (h + 1) * GLA_DK)
            vsl = slice(h * GLA_DV, (h + 1) * GLA_DV)
            v = t["v"]
            dob = do_ref[:, vsl].astype(BF16)
            st_b = st_ref[0, h]
            dst = dstate[h]
            dst_b = dst.astype(BF16)
            qe_b, ke_b = t["qe"].astype(BF16), t["ke"].astype(BF16)
            q_in_b, k_st_b = t["q_in"].astype(BF16), t["k_st"].astype(BF16)
            a = jnp.where(t["causal"], _dot(qe_b, ke_b, NT), 0.0)
            da = jnp.where(t["causal"], _dot(dob, v, NT), 0.0).astype(BF16)
            dv = _dot(a.astype(BF16), dob, TN) + _dot(k_st_b, dst_b, NT)
            dqe = _dot(da, ke_b)
            dke = _dot(da, qe_b, TN)
            dq_in = _dot(dob, st_b)
            dk_st = _dot(v, dst_b)
            ddec = jnp.sum(dst * st_b.astype(F32), axis=0, keepdims=True)
            dstate[h] = _dot(dob, q_in_b, TN) + dst * t["dec"]
            dq = (dqe * t["e_q"] + dq_in * t["e_in"]) * (GLA_DK ** -0.5)
            dk = dke * t["e_k"] + dk_st * t["e_st"]
            w_q, w_k = dqe * t["qe"], dke * t["ke"]
            w_st = dk_st * t["k_st"]
            db = w_q - w_k + dq_in * t["q_in"] - w_st
            db_ref = jnp.sum(w_k - w_q, axis=0, keepdims=True)
            db_last = jnp.sum(w_st, axis=0, keepdims=True) + ddec * t["dec"]
            db = db + jnp.where(row == t["r_ref"], db_ref, 0.0) + jnp.where(row == t["r_last"], db_last, 0.0)
            dg_ref[:, ksl] = _cumsum_rows(db, not reverse)
            if prev is not None:
                dq, dk, dv = dq + pq[:, ksl], dk + pk[:, ksl], dv + pv[:, vsl]
            dq_ref[:, ksl] = dq.astype(out_dt)
            dk_ref[:, ksl] = dk.astype(out_dt)
            dv_ref[:, vsl] = dv.astype(out_dt)

    q_spec, k_spec, v_spec = _gla_specs(n, order)
    kk = pl.BlockSpec((CHUNK, GLA_K), lambda c: (order(c), 0))
    vv = pl.BlockSpec((CHUNK, GLA_V), lambda c: (order(c), 0))
    in_specs = [q_spec, k_spec, v_spec, pl.BlockSpec((CHUNK, GLA_K), lambda c: (order(c), gcol)),
                pl.BlockSpec((1, GLA_HEADS, GLA_DV, GLA_DK), lambda c: (order(c), 0, 0, 0)), vv]
    operands = [proj, proj, proj, gates, states, do]
    if prev is not None:
        in_specs += [kk, kk, vv]
        operands += list(prev)
    return pl.pallas_call(
        body, name="gla_bwd_rev" if reverse else "gla_bwd", grid=(n,), in_specs=in_specs, out_specs=[kk, kk, vv, kk],
        out_shape=[jax.ShapeDtypeStruct((S, GLA_K), out_dt), jax.ShapeDtypeStruct((S, GLA_K), out_dt),
                   jax.ShapeDtypeStruct((S, GLA_V), out_dt), jax.ShapeDtypeStruct((S, GLA_K), F32)],
        scratch_shapes=[pltpu.VMEM((GLA_HEADS, GLA_DV, GLA_DK), F32)],
    )(*operands)


def _gates_fwd(z, wg, bias, tm=512):
    S = z.shape[0]
    W = 2 * GLA_K

    def body(z_ref, w_ref, b_ref, o_ref):
        zg = _dot(z_ref[...], w_ref[...]) + b_ref[...]
        o_ref[...] = (jnp.minimum(zg, 0.0) - jnp.log(1.0 + jnp.exp(-jnp.abs(zg)))) * (1.0 / GATE_NORM)

    return pl.pallas_call(
        body, name="gates_fwd", grid=(S // tm,),
        in_specs=[pl.BlockSpec((tm, Z_W), lambda i: (i, 0)), pl.BlockSpec((Z_W, W), lambda i: (0, 0)),
                  pl.BlockSpec((1, W), lambda i: (0, 0))],
        out_specs=pl.BlockSpec((tm, W), lambda i: (i, 0)), out_shape=jax.ShapeDtypeStruct((S, W), F32),
    )(z, wg, bias)


def _gates_bwd(z, wg, bias, dg_f, dg_b, tm=512):
    S = z.shape[0]
    W = 2 * GLA_K

    def body(z_ref, w_ref, b_ref, dgf_ref, dgb_ref, dz_ref, dw_ref, db_ref):
        i = pl.program_id(0)

        @pl.when(i == 0)
        def _():
            dw_ref[...] = jnp.zeros_like(dw_ref)
            db_ref[...] = jnp.zeros_like(db_ref)

        zv = z_ref[...]
        zg = _dot(zv, w_ref[...]) + b_ref[...]
        dg = jnp.concatenate([dgf_ref[...], dgb_ref[...]], axis=1)
        dzg = dg * (1.0 / GATE_NORM) * _sigmoid(-zg)
        db_ref[...] += jnp.sum(dzg, axis=0, keepdims=True)
        dzg_b = dzg.astype(BF16)
        dw_ref[...] += _dot(zv, dzg_b, TN)
        dz_ref[...] = _dot(dzg_b, w_ref[...], NT).astype(BF16)

    half = pl.BlockSpec((tm, GLA_K), lambda i: (i, 0))
    return pl.pallas_call(
        body, name="gates_bwd", grid=(S // tm,),
        in_specs=[pl.BlockSpec((tm, Z_W), lambda i: (i, 0)), pl.BlockSpec((Z_W, W), lambda i: (0, 0)),
                  pl.BlockSpec((1, W), lambda i: (0, 0)), half, half],
        out_specs=[pl.BlockSpec((tm, Z_W), lambda i: (i, 0)), pl.BlockSpec((Z_W, W), lambda i: (0, 0)),
                   pl.BlockSpec((1, W), lambda i: (0, 0))],
        out_shape=[jax.ShapeDtypeStruct((S, Z_W), BF16), jax.ShapeDtypeStruct((Z_W, W), F32),
                   jax.ShapeDtypeStruct((1, W), F32)],
    )(z, wg, bias, dg_f, dg_b)


def _gla_out_fwd(o, proj, g, tm=512):
    S = o.shape[0]

    def body(o_ref, gr_ref, g_ref, out_ref):
        gn = g_ref[...]
        for h in range(GLA_HEADS):
            sl = slice(h * GLA_DV, (h + 1) * GLA_DV)
            ov = o_ref[:, sl]
            r = lax.rsqrt(jnp.mean(ov * ov, axis=-1, keepdims=True) + EPS)
            gr = gr_ref[:, sl].astype(F32)
            out_ref[:, sl] = (ov * r * gn * (gr * _sigmoid(gr))).astype(BF16)

    blk = pl.BlockSpec((tm, GLA_V), lambda i: (i, 0))
    return pl.pallas_call(
        body, name="gla_out_fwd", grid=(S // tm,),
        in_specs=[blk, pl.BlockSpec((tm, GLA_V), lambda i: (i, (3 * ATTN_W + 2 * GLA_K + GLA_V) // GLA_V)),
                  pl.BlockSpec((1, GLA_DV), lambda i: (0, 0))],
        out_specs=blk, out_shape=jax.ShapeDtypeStruct((S, GLA_V), BF16),
    )(o, proj, g)


def _gla_out_bwd(o, proj, g, dcat, tm=512):
    S = o.shape[0]

    def body(o_ref, gr_ref, g_ref, dgo_ref, do_ref, dgr_ref, dg_ref):
        i = pl.program_id(0)

        @pl.when(i == 0)
        def _():
            dg_ref[...] = jnp.zeros_like(dg_ref)

        gn = g_ref[...]
        dg_acc = jnp.zeros((1, GLA_DV), F32)
        for h in range(GLA_HEADS):
            sl = slice(h * GLA_DV, (h + 1) * GLA_DV)
            ov = o_ref[:, sl]
            r = lax.rsqrt(jnp.mean(ov * ov, axis=-1, keepdims=True) + EPS)
            yhat = ov * r
            gr = gr_ref[:, sl].astype(F32)
            sg = _sigmoid(gr)
            dgo = dgo_ref[:, sl].astype(F32)
            dgr_ref[:, sl] = (dgo * (yhat * gn) * (sg * (1.0 + gr * (1.0 - sg)))).astype(BF16)
            dy = dgo * (gr * sg)
            dg_acc = dg_acc + jnp.sum(dy * yhat, axis=0, keepdims=True)
            t = dy * gn
            do_ref[:, sl] = r * (t - yhat * jnp.mean(t * yhat, axis=-1, keepdims=True))
        dg_ref[...] += dg_acc

    blk = pl.BlockSpec((tm, GLA_V), lambda i: (i, 0))
    vec = pl.BlockSpec((1, GLA_DV), lambda i: (0, 0))
    return pl.pallas_call(
        body, name="gla_out_bwd", grid=(S // tm,),
        in_specs=[blk, pl.BlockSpec((tm, GLA_V), lambda i: (i, (3 * ATTN_W + 2 * GLA_K + GLA_V) // GLA_V)), vec,
                  pl.BlockSpec((tm, GLA_V), lambda i: (i, 1))],
        out_specs=[blk, blk, vec],
        out_shape=[jax.ShapeDtypeStruct((S, GLA_V), F32), jax.ShapeDtypeStruct((S, GLA_V), BF16),
                   jax.ShapeDtypeStruct((1, GLA_DV), F32)],
    )(o, proj, g, dcat)


HALO = 16


def _halo_specs(tm, tn, S):
    cur = pl.BlockSpec((tm, tn), lambda j, i: (i, j))
    prev = pl.BlockSpec((HALO, tn), lambda j, i: (jnp.maximum(i * (tm // HALO) - 1, 0), j))
    nxt = pl.BlockSpec((HALO, tn), lambda j, i: (jnp.minimum((i + 1) * (tm // HALO), S // HALO - 1), j))
    return cur, prev, nxt


def _shifted(x, p_ref, n_ref, n_blocks):
    i = pl.program_id(1)
    tm = x.shape[0]
    row = lax.broadcasted_iota(jnp.int32, x.shape, 0)
    before = p_ref[HALO - 1:HALO, :].astype(F32) * (i > 0).astype(F32)
    after = n_ref[0:1, :].astype(F32) * (i < n_blocks - 1).astype(F32)
    x_m1 = jnp.where(row == 0, before, pltpu.roll(x, 1, axis=0))
    x_p1 = jnp.where(row == tm - 1, after, pltpu.roll(x, tm - 1, axis=0))
    return x_m1, x_p1


def _glu_fwd(gp, up, cw, cb, tm=512, tn=1408):
    S = gp.shape[0]
    nb = S // tm

    def body(c_ref, p_ref, n_ref, up_ref, w_ref, b_ref, o_ref):
        x = c_ref[...].astype(F32)
        x_m1, x_p1 = _shifted(x, p_ref, n_ref, nb)
        w = w_ref[...]
        gate = w[0:1, :] * x_m1 + w[1:2, :] * x + w[2:3, :] * x_p1 + b_ref[...]
        o_ref[...] = (gate * _sigmoid(gate) * up_ref[...].astype(F32)).astype(BF16)

    cur, prev, nxt = _halo_specs(tm, tn, S)
    return pl.pallas_call(
        body, name="glu_fwd", grid=(D_FF // tn, nb),
        in_specs=[cur, prev, nxt, cur, pl.BlockSpec((3, tn), lambda j, i: (0, j)), pl.BlockSpec((1, tn), lambda j, i: (0, j))],
        out_specs=cur, out_shape=jax.ShapeDtypeStruct((S, D_FF), BF16),
    )(gp, gp, gp, up, cw, cb)


def _glu_bwd(gp, up, dact, cw, cb, tm=512, tn=1408):
    S = gp.shape[0]
    nb = S // tm

    def body(c_ref, p_ref, n_ref, up_ref, da_ref, w_ref, b_ref, dup_ref, dgate_ref, dw_ref, db_ref):
        @pl.when(pl.program_id(1) == 0)
        def _():
            dw_ref[...] = jnp.zeros_like(dw_ref)
            db_ref[...] = jnp.zeros_like(db_ref)

        x = c_ref[...].astype(F32)
        x_m1, x_p1 = _shifted(x, p_ref, n_ref, nb)
        w = w_ref[...]
        gate = w[0:1, :] * x_m1 + w[1:2, :] * x + w[2:3, :] * x_p1 + b_ref[...]
        sg = _sigmoid(gate)
        da = da_ref[...].astype(F32)
        dup_ref[...] = (da * (gate * sg)).astype(BF16)
        dgate = da * up_ref[...].astype(F32) * (sg * (1.0 + gate * (1.0 - sg)))
        dgate_ref[...] = dgate.astype(BF16)
        db_ref[...] += jnp.sum(dgate, axis=0, keepdims=True)
        dw_ref[...] += jnp.concatenate(
            [jnp.sum(dgate * x_m1, axis=0, keepdims=True), jnp.sum(dgate * x, axis=0, keepdims=True),
             jnp.sum(dgate * x_p1, axis=0, keepdims=True)], axis=0)

    cur, prev, nxt = _halo_specs(tm, tn, S)
    w_spec = pl.BlockSpec((3, tn), lambda j, i: (0, j))
    b_spec = pl.BlockSpec((1, tn), lambda j, i: (0, j))
    return pl.pallas_call(
        body, name="glu_bwd", grid=(D_FF // tn, nb), in_specs=[cur, prev, nxt, cur, cur, w_spec, b_spec],
        out_specs=[cur, cur, w_spec, b_spec],
        out_shape=[jax.ShapeDtypeStruct((S, D_FF), BF16), jax.ShapeDtypeStruct((S, D_FF), BF16),
                   jax.ShapeDtypeStruct((3, D_FF), F32), jax.ShapeDtypeStruct((1, D_FF), F32)],
    )(gp, gp, gp, up, dact, cw, cb)


def _conv_bwd_input(dgate, cw, tm=512, tn=1408):
    S = dgate.shape[0]
    nb = S // tm

    def body(c_ref, p_ref, n_ref, w_ref, o_ref):
        x = c_ref[...].astype(F32)
        x_m1, x_p1 = _shifted(x, p_ref, n_ref, nb)
        w = w_ref[...]
        o_ref[...] = (w[0:1, :] * x_p1 + w[1:2, :] * x + w[2:3, :] * x_m1).astype(BF16)

    cur, prev, nxt = _halo_specs(tm, tn, S)
    return pl.pallas_call(
        body, name="conv_bwd_input", grid=(D_FF // tn, nb),
        in_specs=[cur, prev, nxt, pl.BlockSpec((3, tn), lambda j, i: (0, j))], out_specs=cur,
        out_shape=jax.ShapeDtypeStruct((S, D_FF), BF16),
    )(dgate, dgate, dgate, cw)


def _local_step(x, target, norm1_g, w_in_t, wg, gate_bias, gla_norm_g, attn_norm_g, w_out, norm2_g,
                w_gate4, w_up4, conv_w, conv_b, w_down, final_norm_g, on_grad=lambda event, arrays: ()):
    S = x.shape[0]
    tabs = _rope_tables(S)

    n1 = _rms_fwd("rms1_fwd", x, norm1_g)
    z_block = IN_MAIN // Z_W
    proj = _mm_nt("in_proj", n1, w_in_t, 1024, 1536, BF16, n_out=IN_MAIN)
    z = _matmul(
        "in_proj_z",
        [(n1, pl.BlockSpec((1024, D_MODEL), lambda i: (i, 0)), w_in_t, pl.BlockSpec((Z_W, D_MODEL), lambda i: (z_block, 0)), NT)],
        (S // 1024,), jax.ShapeDtypeStruct((S, Z_W), BF16), pl.BlockSpec((1024, Z_W), lambda i: (i, 0)), 1)
    qk = _rope_fwd(proj, tabs)
    branch = [_attn_fwd(qk, proj, d) for d in DILATIONS]
    ao, o_attn, lse = _attn_combine([b[0] for b in branch], [b[1] for b in branch], attn_norm_g)
    gates = _gates_fwd(z, wg, gate_bias)
    o_f, st_f = _gla_fwd(proj, gates, False)
    o_gla, st_b = _gla_fwd(proj, gates, True, o_prev=o_f)
    go = _gla_out_fwd(o_gla, proj, gla_norm_g)
    cat = jnp.concatenate([ao, go], axis=1)
    h1 = _mm_nn("out_proj", cat, w_out, 1024, 1024, F32, res=x)
    n2 = _rms_fwd("rms2_fwd", h1, norm2_g)
    gp = _mm_nn_sharded("ffn_gate", n2, w_gate4, 1024, BF16)
    up = _mm_nn_sharded("ffn_up", n2, w_up4, 1024, BF16)
    act = _glu_fwd(gp, up, conv_w, conv_b)
    tk = D_FF // N_CHIPS
    h2 = _matmul(
        "ffn_down",
        [(act, pl.BlockSpec((1024, tk), lambda i, j, k: (i, k)), w_down, pl.BlockSpec((tk, 1024), lambda i, j, k: (k, j)), NN)],
        (S // 1024, D_MODEL // 1024, N_CHIPS), jax.ShapeDtypeStruct((S, D_MODEL), F32),
        pl.BlockSpec((1024, 1024), lambda i, j, k: (i, j)), N_CHIPS,
        res=(h1, pl.BlockSpec((1024, 1024), lambda i, j, k: (i, j))))
    loss_row, d_final_g, dh2, dh2_b = _final_loss(h2, final_norm_g.reshape(1, D_MODEL), target)

    dact = _mm_nt("ffn_down_bwd", dh2_b, w_down, 1024, tk, BF16)
    dup, dgate, d_conv_w, d_conv_b = _glu_bwd(gp, up, dact, conv_w, conv_b)
    dgp = _conv_bwd_input(dgate, conv_w)
    d_w_down = _mm_tn("ffn_down_wgrad", act, dh2_b, tk, D_MODEL, 1024, BF16)
    on_grad("w_down", dict(w_down=d_w_down))
    dgp = _after(dgp, d_w_down)
    d_w_gate4 = _mm_tn("ffn_gate_wgrad", n2, dgp, D_MODEL, tk, 1024, BF16, out3=tk)
    dup = _after(dup, d_w_gate4)
    d_w_up4 = _mm_tn("ffn_up_wgrad", n2, dup, D_MODEL, tk, 1024, BF16, out3=tk)
    held = on_grad("w_gate_w_up", dict(w_gate=d_w_gate4, w_up=d_w_up4))
    dgp = _after(dgp, d_w_up4, *held)
    dn2 = _matmul(
        "ffn_in_bwd",
        [(dgp, pl.BlockSpec((1024, tk), lambda i, j, k: (i, k)), w_gate4, pl.BlockSpec((None, 1024, tk), lambda i, j, k: (k, j, 0)), NT),
         (dup, pl.BlockSpec((1024, tk), lambda i, j, k: (i, k)), w_up4, pl.BlockSpec((None, 1024, tk), lambda i, j, k: (k, j, 0)), NT)],
        (S // 1024, D_MODEL // 1024, N_CHIPS), jax.ShapeDtypeStruct((S, D_MODEL), F32),
        pl.BlockSpec((1024, 1024), lambda i, j, k: (i, j)), N_CHIPS)
    dh1, dh1_b, d_norm2_g = _rms_bwd("rms2_bwd", h1, norm2_g, dn2, dh2)

    d_w_out = _mm_tn("out_proj_wgrad", cat, dh1_b, D_MODEL, 1024, 1024, BF16)
    held = on_grad("w_out", dict(w_out=d_w_out))
    dcat = _mm_nt("out_proj_bwd", _after(dh1_b, d_w_out, *held), w_out, 1024, 1024, BF16)
    do_attn, delta, d_attn_norm_g = _attn_norm_bwd(o_attn, attn_norm_g, dcat)
    dqs, dks, dvs = [], [], []
    for d in DILATIONS:
        dqs.append(_attn_bwd_dq(qk, proj, do_attn, lse, delta, d))
        dk, dv = _attn_bwd_dkv(qk, proj, do_attn, lse, delta, d)
        dks.append(dk)
        dvs.append(dv)
    d_attn = _attn_grad_merge(dqs, dks, dvs, tabs)
    held = on_grad("mid", dict(anchor=d_attn))
    do_gla, dgr, d_gla_norm_g = _gla_out_bwd(o_gla, proj, gla_norm_g, _after(dcat, *held))
    dq_f, dk_f, dv_f, dg_f = _gla_bwd(proj, gates, st_f, do_gla, False)
    dgq, dgk, dgv, dg_b = _gla_bwd(proj, gates, st_b, do_gla, True, prev=(dq_f, dk_f, dv_f))
    dz, d_wg, d_gate_bias = _gates_bwd(z, wg, gate_bias, dg_f, dg_b)
    dproj = jnp.concatenate([d_attn, dgq, dgk, dgv, dgr], axis=1)
    d_w_in_t = _mm_tn("in_proj_wgrad", dproj, n1, 1536, D_MODEL, 1024, BF16, rows_out=IN_W)
    n_tok = S // 1024
    d_w_in_t = _matmul(
        "in_proj_z_wgrad",
        [(dz, pl.BlockSpec((1024, Z_W), lambda i, j, k: (k, 0)), n1, pl.BlockSpec((1024, D_MODEL), lambda i, j, k: (k, 0)), TN)],
        (1, 1, n_tok), jax.ShapeDtypeStruct((IN_W, D_MODEL), BF16), pl.BlockSpec((Z_W, D_MODEL), lambda i, j, k: (z_block, 0)),
        n_tok, into=d_w_in_t)
    held = on_grad("w_in", dict(w_in_t=d_w_in_t))
    tkm = IN_MAIN // 4
    dn1 = _matmul(
        "in_proj_bwd",
        [(_after(dproj, d_w_in_t, *held), pl.BlockSpec((1024, tkm), lambda i, j, k: (i, k)), w_in_t, pl.BlockSpec((tkm, 1024), lambda i, j, k: (k, j)), NN)],
        (S // 1024, D_MODEL // 1024, 4), jax.ShapeDtypeStruct((S, D_MODEL), F32),
        pl.BlockSpec((1024, 1024), lambda i, j, k: (i, j)), 4)
    held = on_grad("last", dict(last=dn1))
    dn1 = _matmul(
        "in_proj_z_bwd",
        [(dz, pl.BlockSpec((1024, Z_W), lambda j, i: (i, 0)), w_in_t, pl.BlockSpec((Z_W, 1024), lambda j, i: (z_block, j)), NN)],
        (D_MODEL // 1024, S // 1024), jax.ShapeDtypeStruct((S, D_MODEL), F32), pl.BlockSpec((1024, 1024), lambda j, i: (i, j)), 1,
        res=(_after(dn1, *held), pl.BlockSpec((1024, 1024), lambda j, i: (i, j))))
    grad_x, _, d_norm1_g = _rms_bwd("rms1_bwd", x, norm1_g, dn1, dh1)

    big = dict(w_in_t=d_w_in_t, w_out=d_w_out, w_gate4=d_w_gate4, w_up4=d_w_up4, w_down=d_w_down)
    small = dict(loss=loss_row, norm1_g=d_norm1_g, wg=d_wg, gate_bias=d_gate_bias, gla_norm_g=d_gla_norm_g,
                 attn_norm_g=d_attn_norm_g, norm2_g=d_norm2_g, conv_w=d_conv_w, conv_b=d_conv_b, final_norm_g=d_final_g)
    return grad_x, big, small


def _position():
    return lax.axis_index("x"), lax.axis_index("y"), lax.axis_index("c")


def _other_chips(x, y):
    return [(1 - x, y), (x, 1 - y), (1 - x, 1 - y)]


def _gather_chips(name, shards):
    n = len(shards)

    def body(*refs):
        ins, outs = refs[:n], refs[n:2 * n]
        send, recv, loc = refs[2 * n:]
        x, y, c = _position()
        me = 2 * x + y
        chips = _other_chips(x, y)
        started = []
        for w in range(n):
            own = pltpu.make_async_copy(ins[w], outs[w].at[me], loc.at[w])
            own.start()
            started.append(own)
        sends = []
        for w in range(n):
            for j, (px, py) in enumerate(chips):
                cp = pltpu.make_async_remote_copy(ins[w], outs[w].at[me], send.at[3 * w + j], recv.at[3 * w + j],
                                                  device_id=(px, py, c), device_id_type=MESH)
                cp.start()
                sends.append(cp)
        for w in range(n):
            for j, (px, py) in enumerate(chips):
                pltpu.make_async_remote_copy(ins[w], outs[w].at[2 * px + py], send.at[3 * w + j], recv.at[3 * w + j],
                                             device_id=(px, py, c), device_id_type=MESH).wait_recv()
        for cp in sends:
            cp.wait_send()
        for own in started:
            own.wait()

    return pl.pallas_call(
        body, name=name, in_specs=[ANY] * n, out_specs=[ANY] * n,
        out_shape=[jax.ShapeDtypeStruct((N_CHIPS,) + s.shape, s.dtype) for s in shards],
        scratch_shapes=[pltpu.SemaphoreType.DMA((3 * n,)), pltpu.SemaphoreType.DMA((3 * n,)), pltpu.SemaphoreType.DMA((n,))],
    )(*shards)


def _gather_chips_async(name, shards, collective_id):
    n = len(shards)

    def body(*refs):
        ins, outs = refs[:n], refs[n:2 * n]
        send, recv, loc = refs[2 * n:]
        x, y, c = _position()
        me = 2 * x + y
        chips = _other_chips(x, y)
        barrier = pltpu.get_barrier_semaphore()
        for px, py in chips:
            pl.semaphore_signal(barrier, inc=1, device_id=(px, py, c), device_id_type=MESH)
        pl.semaphore_wait(barrier, len(chips))
        started = []
        for w in range(n):
            own = pltpu.make_async_copy(ins[w], outs[w].at[me], loc.at[w])
            own.start()
            started.append(own)
        sends = []
        for w in range(n):
            for j, (px, py) in enumerate(chips):
                cp = pltpu.make_async_remote_copy(ins[w], outs[w].at[me], send.at[3 * w + j], recv.at[3 * w + j],
                                                  device_id=(px, py, c), device_id_type=MESH)
                cp.start()
                sends.append(cp)
        for w in range(n):
            for j, (px, py) in enumerate(chips):
                pltpu.make_async_remote_copy(ins[w], outs[w].at[2 * px + py], send.at[3 * w + j], recv.at[3 * w + j],
                                             device_id=(px, py, c), device_id_type=MESH).wait_recv()
        for cp in sends:
            cp.wait_send()
        for own in started:
            own.wait()

    return pl.kernel(
        body, name=name, mesh=_sequencer(),
        out_type=[jax.ShapeDtypeStruct((N_CHIPS,) + s.shape, s.dtype) for s in shards],
        scratch_types=[pltpu.SemaphoreType.DMA((3 * n,)), pltpu.SemaphoreType.DMA((3 * n,)), pltpu.SemaphoreType.DMA((n,))],
        compiler_params=pltpu.CompilerParams(collective_id=collective_id),
    )(*shards)


def _sibling_exchange(name, arrs):
    n = len(arrs)

    def body(*refs):
        ins, outs = refs[:n], refs[n:2 * n]
        send, recv = refs[2 * n:]
        x, y, c = _position()
        copies = [pltpu.make_async_remote_copy(ins[w], outs[w], send.at[w], recv.at[w], device_id=(x, y, 1 - c),
                                               device_id_type=MESH) for w in range(n)]
        for cp in copies:
            cp.start()
        for cp in copies:
            cp.wait()

    return pl.pallas_call(
        body, name=name, in_specs=[ANY] * n, out_specs=[ANY] * n,
        out_shape=[jax.ShapeDtypeStruct(a.shape, a.dtype) for a in arrs],
        scratch_shapes=[pltpu.SemaphoreType.DMA((n,)), pltpu.SemaphoreType.DMA((n,))],
    )(*arrs)


def _scatter_chips(name, parts):
    n = len(parts)

    def body(*refs):
        ins, outs = refs[:n], refs[n:2 * n]
        send, recv, loc = refs[2 * n:]
        x, y, c = _position()
        me = 2 * x + y
        chips = _other_chips(x, y)
        started = []
        for w in range(n):
            own = pltpu.make_async_copy(ins[w].at[me], outs[w].at[me], loc.at[w])
            own.start()
            started.append(own)
        sends = []
        for w in range(n):
            for j, (px, py) in enumerate(chips):
                cp = pltpu.make_async_remote_copy(ins[w].at[2 * px + py], outs[w].at[me], send.at[3 * w + j],
                                                  recv.at[3 * w + j], device_id=(px, py, c), device_id_type=MESH)
                cp.start()
                sends.append(cp)
        for w in range(n):
            for j, (px, py) in enumerate(chips):
                pltpu.make_async_remote_copy(ins[w].at[me], outs[w].at[2 * px + py], send.at[3 * w + j], recv.at[3 * w + j],
                                             device_id=(px, py, c), device_id_type=MESH).wait_recv()
        for cp in sends:
            cp.wait_send()
        for own in started:
            own.wait()

    return pl.pallas_call(
        body, name=name, in_specs=[ANY] * n, out_specs=[ANY] * n,
        out_shape=[jax.ShapeDtypeStruct(p.shape, p.dtype) for p in parts],
        scratch_shapes=[pltpu.SemaphoreType.DMA((3 * n,)), pltpu.SemaphoreType.DMA((3 * n,)), pltpu.SemaphoreType.DMA((n,))],
    )(*parts)


def _sequencer():
    return plsc.ScalarSubcoreMesh(axis_name="sequencer", num_cores=1)


def _sibling_exchange_async(name, arrs, collective_id):
    n = len(arrs)

    def body(*refs):
        ins, outs = refs[:n], refs[n:2 * n]
        send, recv = refs[2 * n:]
        x, y, c = _position()
        sibling = (x, y, 1 - c)
        barrier = pltpu.get_barrier_semaphore()
        pl.semaphore_signal(barrier, inc=1, device_id=sibling, device_id_type=MESH)
        pl.semaphore_wait(barrier, 1)
        copies = [pltpu.make_async_remote_copy(ins[w], outs[w], send.at[w], recv.at[w], device_id=sibling,
                                               device_id_type=MESH) for w in range(n)]
        for cp in copies:
            cp.start()
        for cp in copies:
            cp.wait()

    return pl.kernel(
        body, name=name, out_type=[jax.ShapeDtypeStruct(a.shape, a.dtype) for a in arrs],
        scratch_types=[pltpu.SemaphoreType.DMA((n,)), pltpu.SemaphoreType.DMA((n,))],
        compiler_params=pltpu.CompilerParams(collective_id=collective_id), mesh=_sequencer(),
    )(*arrs)


def _scatter_chips_async(name, parts, collective_id):
    n = len(parts)

    def body(*refs):
        ins, outs = refs[:n], refs[n:2 * n]
        send, recv, loc = refs[2 * n:]
        x, y, c = _position()
        me = 2 * x + y
        chips = _other_chips(x, y)
        barrier = pltpu.get_barrier_semaphore()
        for px, py in chips:
            pl.semaphore_signal(barrier, inc=1, device_id=(px, py, c), device_id_type=MESH)
        pl.semaphore_wait(barrier, len(chips))
        started = []
        for w in range(n):
            own = pltpu.make_async_copy(ins[w].at[me], outs[w].at[me], loc.at[w])
            own.start()
            started.append(own)
        sends = []
        for w in range(n):
            for j, (px, py) in enumerate(chips):
                cp = pltpu.make_async_remote_copy(ins[w].at[2 * px + py], outs[w].at[me], send.at[3 * w + j],
                                                  recv.at[3 * w + j], device_id=(px, py, c), device_id_type=MESH)
                cp.start()
                sends.append(cp)
        for w in range(n):
            for j, (px, py) in enumerate(chips):
                pltpu.make_async_remote_copy(ins[w].at[me], outs[w].at[2 * px + py], send.at[3 * w + j], recv.at[3 * w + j],
                                             device_id=(px, py, c), device_id_type=MESH).wait_recv()
        for cp in sends:
            cp.wait_send()
        for own in started:
            own.wait()

    return pl.kernel(
        body, name=name, out_type=[jax.ShapeDtypeStruct(p.shape, p.dtype) for p in parts],
        scratch_types=[pltpu.SemaphoreType.DMA((3 * n,)), pltpu.SemaphoreType.DMA((3 * n,)), pltpu.SemaphoreType.DMA((n,))],
        compiler_params=pltpu.CompilerParams(collective_id=collective_id), mesh=_sequencer(),
    )(*parts)


def _allreduce_rows(buf):
    R = buf.shape[0]

    def body(in_ref, out_ref, land, send, recv):
        x, y, c = _position()
        me = 4 * x + 2 * y + c
        land[pl.ds(me, 1)] = in_ref[...][None]
        peers = []
        for mask in range(1, N_DEV):
            px = 1 - x if mask & 4 else x
            py = 1 - y if mask & 2 else y
            pc = 1 - c if mask & 1 else c
            peers.append((px, py, pc))
        sends = []
        for k, peer in enumerate(peers):
            cp = pltpu.make_async_remote_copy(in_ref, land.at[me], send.at[k], recv.at[k], device_id=peer, device_id_type=MESH)
            cp.start()
            sends.append(cp)
        for k, (px, py, pc) in enumerate(peers):
            pltpu.make_async_remote_copy(in_ref, land.at[4 * px + 2 * py + pc], send.at[k], recv.at[k],
                                         device_id=(px, py, pc), device_id_type=MESH).wait_recv()
        for cp in sends:
            cp.wait_send()
        tot = land[0]
        for i in range(1, N_DEV):
            tot = tot + land[i]
        out_ref[...] = tot

    vm = pl.BlockSpec(memory_space=pltpu.VMEM)
    return pl.pallas_call(
        body, name="allreduce_small", in_specs=[vm], out_specs=vm, out_shape=jax.ShapeDtypeStruct((R, LANES), F32),
        scratch_shapes=[pltpu.VMEM((N_DEV, R, LANES), F32), pltpu.SemaphoreType.DMA((N_DEV - 1,)),
                        pltpu.SemaphoreType.DMA((N_DEV - 1,))],
    )(buf)


def _tile2d(r, c):
    for tr in (512, 256, 128):
        if r % tr == 0:
            return tr, c
    return (r, c) if r <= 256 else (r, 256)


def _pair_sum(name, a, b):
    n, r, c = a.shape
    tr, tc = _tile2d(r, c)

    def body(a_ref, b_ref, o_ref):
        o_ref[...] = (a_ref[...].astype(F32) + b_ref[...].astype(F32)).astype(BF16)

    blk = pl.BlockSpec((None, tr, tc), lambda s, i, j: (s, i, j))
    return pl.pallas_call(
        body, name=name, grid=(n, r // tr, c // tc), in_specs=[blk, blk], out_specs=blk,
        out_shape=jax.ShapeDtypeStruct(a.shape, BF16),
    )(a, b)


def _adamw_math(w, m, v, g):
    m2 = ADAM_B1 * m + (1.0 - ADAM_B1) * g
    v2 = ADAM_B2 * v + (1.0 - ADAM_B2) * (g * g)
    m_hat = m2 / (1.0 - ADAM_B1 ** ADAM_STEP)
    v_hat = v2 / (1.0 - ADAM_B2 ** ADAM_STEP)
    delta = -ADAM_LR * (m_hat / (jnp.sqrt(v_hat) + ADAM_EPS) + ADAM_WD * w)
    return delta, m2, v2


def _adamw(name, w, m, v, g):
    r, c = w.shape
    stacked = g.ndim == 3
    tr, tc = _tile2d(r, c)
    if tc == c and r % 256 == 0:
        tr = 256

    def body(w_ref, m_ref, v_ref, g_ref, go_ref, d_ref, m2_ref, v2_ref):
        if stacked:
            gv = g_ref[0].astype(F32)
            for i in range(1, N_CHIPS):
                gv = gv + g_ref[i].astype(F32)
        else:
            gv = g_ref[...]
        delta, m2, v2 = _adamw_math(w_ref[...], m_ref[...], v_ref[...], gv)
        go_ref[...] = gv
        d_ref[...] = delta
        m2_ref[...] = m2
        v2_ref[...] = v2

    blk = pl.BlockSpec((tr, tc), lambda i, j: (i, j))
    g_spec = pl.BlockSpec((N_CHIPS, tr, tc), lambda i, j: (0, i, j)) if stacked else blk
    out = jax.ShapeDtypeStruct((r, c), F32)
    return pl.pallas_call(
        body, name=name, grid=(r // tr, c // tc), in_specs=[blk, blk, blk, g_spec], out_specs=[blk] * 4, out_shape=[out] * 4,
    )(w, m, v, g)


def _pack_rows(pieces):
    flat = jnp.concatenate([p.reshape(-1) for p in pieces])
    rows = flat.shape[0] // LANES
    pad = (-rows) % 8
    return jnp.pad(flat.reshape(rows, LANES), ((0, pad), (0, 0)))


def _unpack_rows(buf, shapes):
    flat = buf.reshape(-1)
    out, at = [], 0
    for s in shapes:
        size = math.prod(s)
        out.append(flat[at:at + size].reshape(s))
        at += size
    return out


SMALL_NAMES = ("norm1_g", "gf_up", "gf_b", "gb_up", "gb_b", "gla_norm_g", "attn_norm_g", "norm2_g", "conv_w", "conv_b",
               "final_norm_g")
BIG_NAMES = ("w_in", "w_out", "w_gate", "w_up", "w_down")
WEIGHT_ORDER = ("norm1_g", "w_in", "gf_up", "gf_b", "gb_up", "gb_b", "gla_norm_g", "attn_norm_g", "w_out", "norm2_g",
                "w_gate", "w_up", "conv_w", "conv_b", "w_down", "final_norm_g")


def kernel(x, norm1_g, w_in, gf_up, gf_b, gb_up, gb_b, gla_norm_g, attn_norm_g, w_out, norm2_g, w_gate, w_up, conv_w, conv_b, w_down, final_norm_g, loss_target, m_norm1_g, m_w_in, m_gf_up, m_gf_b, m_gb_up, m_gb_b, m_gla_norm_g, m_attn_norm_g, m_w_out, m_norm2_g, m_w_gate, m_w_up, m_conv_w, m_conv_b, m_w_down, m_final_norm_g, v_norm1_g, v_w_in, v_gf_up, v_gf_b, v_gb_up, v_gb_b, v_gla_norm_g, v_attn_norm_g, v_w_out, v_norm2_g, v_w_gate, v_w_up, v_conv_w, v_conv_b, v_w_down, v_final_norm_g):
    w = dict(norm1_g=norm1_g, w_in=w_in, gf_up=gf_up, gf_b=gf_b, gb_up=gb_up, gb_b=gb_b, gla_norm_g=gla_norm_g,
             attn_norm_g=attn_norm_g, w_out=w_out, norm2_g=norm2_g, w_gate=w_gate, w_up=w_up, conv_w=conv_w, conv_b=conv_b,
             w_down=w_down, final_norm_g=final_norm_g)
    m = dict(norm1_g=m_norm1_g, w_in=m_w_in, gf_up=m_gf_up, gf_b=m_gf_b, gb_up=m_gb_up, gb_b=m_gb_b, gla_norm_g=m_gla_norm_g,
             attn_norm_g=m_attn_norm_g, w_out=m_w_out, norm2_g=m_norm2_g, w_gate=m_w_gate, w_up=m_w_up, conv_w=m_conv_w,
             conv_b=m_conv_b, w_down=m_w_down, final_norm_g=m_final_norm_g)
    v = dict(norm1_g=v_norm1_g, w_in=v_w_in, gf_up=v_gf_up, gf_b=v_gf_b, gb_up=v_gb_up, gb_b=v_gb_b, gla_norm_g=v_gla_norm_g,
             attn_norm_g=v_attn_norm_g, w_out=v_w_out, norm2_g=v_norm2_g, w_gate=v_w_gate, w_up=v_w_up, conv_w=v_conv_w,
             conv_b=v_conv_b, w_down=v_w_down, final_norm_g=v_final_norm_g)
    S = x.shape[1]
    chip = 2 * lax.axis_index("x") + lax.axis_index("y")
    n_in = IN_W // N_CHIPS
    n_ff = D_FF // N_CHIPS
    n_gk = GLA_K // N_CHIPS

    def owned(t):
        return {k: (jnp.transpose(t[k][0]) if k == "w_in" else t[k][0]) for k in BIG_NAMES}

    own_w, own_m, own_v = owned(w), owned(m), owned(v)
    shard = {k: own_w[k].astype(BF16) for k in BIG_NAMES}
    small_shard = _pack_rows([gf_up[0], gb_up[0], conv_w[0]])
    small4, w_in4 = _gather_chips_async("gather_w_in", [small_shard, shard["w_in"]], 0)
    w_out4, w_gate4, w_up4 = _gather_chips_async("gather_w_mid", [shard["w_out"], shard["w_gate"], shard["w_up"]], 1)
    (w_down4,) = _gather_chips_async("gather_w_down", [shard["w_down"]], 2)
    w_in_t = w_in4.reshape(IN_W, D_MODEL)
    rows_up = GATE_RANK * n_gk // LANES
    rows_cw = 3 * n_ff // LANES
    gf_full = jnp.transpose(small4[:, 0:rows_up].reshape(N_CHIPS, GATE_RANK, n_gk), (1, 0, 2)).reshape(GATE_RANK, GLA_K)
    gb_full = jnp.transpose(small4[:, rows_up:2 * rows_up].reshape(N_CHIPS, GATE_RANK, n_gk), (1, 0, 2)).reshape(GATE_RANK, GLA_K)
    cw_full = jnp.transpose(small4[:, 2 * rows_up:2 * rows_up + rows_cw].reshape(N_CHIPS, 3, n_ff), (1, 0, 2)).reshape(3, D_FF)
    wg = jnp.zeros((Z_W, 2 * GLA_K), F32)
    wg = wg.at[0:GATE_RANK, 0:GLA_K].set(gf_full).at[GATE_RANK:2 * GATE_RANK, GLA_K:].set(gb_full).astype(BF16)
    gate_bias = jnp.concatenate([gf_b, gb_b], axis=1)

    pending, contributions, next_id = [], {}, [3]

    def as_shards(group, arrays):
        if group == "w_in":
            return dict(w_in=arrays["w_in_t"].reshape(N_CHIPS, n_in, D_MODEL))
        if group == "w_out":
            return dict(w_out=arrays["w_out"].reshape(N_CHIPS, D_MODEL // N_CHIPS, D_MODEL))
        if group == "w_down":
            return dict(w_down=arrays["w_down"].reshape(N_CHIPS, n_ff, D_MODEL))
        return arrays

    out = {}

    def swap(group, arrays):
        mine = as_shards(group, arrays)
        pending.append((group, mine, _sibling_exchange_async(f"sibling_{group}", list(mine.values()), next_id[0])))
        next_id[0] += 1

    def sum_and_send(anchor):
        tag, mine, theirs = pending.pop()
        sums = [_pair_sum(f"pair_sum_{k}", mine[k], _after(t, *anchor)) for k, t in zip(mine, theirs)]
        contributions.update(zip(mine, _scatter_chips_async(f"scatter_{tag}", sums, next_id[0])))
        next_id[0] += 1
        return sums

    def update(names, anchor):
        for k in names:
            res = _adamw(f"adamw_{k}", own_w[k], own_m[k], own_v[k], _after(contributions[k], *anchor))
            out[k] = [(jnp.transpose(r) if k == "w_in" else r)[None] for r in res]
        return [out[k][0] for k in names]

    def on_grad(event, arrays):
        anchor = list(arrays.values())
        held = []
        if event in ("w_gate_w_up", "w_out", "mid", "last"):
            held += sum_and_send(anchor)
        if event == "mid":
            held += update(("w_down", "w_gate", "w_up"), anchor)
        if event == "last":
            held += update(("w_out",), anchor)
        if event in ("w_down", "w_gate_w_up", "w_out", "w_in"):
            swap(event, arrays)
        return held

    grad_x, _, small = _local_step(
        x[0], loss_target[0], norm1_g, w_in_t, wg, gate_bias, gla_norm_g, attn_norm_g,
        w_out4.reshape(D_MODEL, D_MODEL), norm2_g, w_gate4, w_up4, cw_full, conv_b, w_down4.reshape(D_FF, D_MODEL), final_norm_g,
        on_grad=on_grad)
    update(("w_in",), [grad_x])

    d_gf_up = small["wg"][0:GATE_RANK, 0:GLA_K]
    d_gb_up = small["wg"][GATE_RANK:2 * GATE_RANK, GLA_K:]
    pieces = [small["loss"], small["norm1_g"], d_gf_up, small["gate_bias"][:, :GLA_K], d_gb_up, small["gate_bias"][:, GLA_K:],
              small["gla_norm_g"], small["attn_norm_g"], small["norm2_g"], small["conv_w"], small["conv_b"], small["final_norm_g"]]
    total = _allreduce_rows(_pack_rows(pieces))
    summed = _unpack_rows(total, [p.shape for p in pieces])
    loss = summed[0][0, 0]
    g_small = dict(zip(SMALL_NAMES, summed[1:]))
    g_small["gf_up"] = lax.dynamic_slice_in_dim(g_small["gf_up"], chip * n_gk, n_gk, axis=1)
    g_small["gb_up"] = lax.dynamic_slice_in_dim(g_small["gb_up"], chip * n_gk, n_gk, axis=1)
    g_small["conv_w"] = lax.dynamic_slice_in_dim(g_small["conv_w"], chip * n_ff, n_ff, axis=1)
    packed = [_pack_rows([t[k] for k in SMALL_NAMES]) for t in (w, m, v, g_small)]
    res = _adamw("adamw_small", *packed)
    shapes = [w[k].shape for k in SMALL_NAMES]
    for k, vals in zip(SMALL_NAMES, zip(*[_unpack_rows(r, shapes) for r in res])):
        out[k] = list(vals)

    grads, deltas, new_m, new_v = ([out[k][i] for k in WEIGHT_ORDER] for i in range(4))
    return (loss, grad_x[None], *grads, *deltas, *new_m, *new_v)
```

```python
import functools
import math

import jax
import jax.numpy as jnp
from jax import lax
from jax.experimental import pallas as pl
from jax.experimental.pallas import tpu as pltpu
from jax.experimental.pallas import tpu_sc as plsc

F32 = jnp.float32
BF16 = jnp.bfloat16

D_MODEL = 2048
ATTN_W = 1024
HEAD = 128
N_HEADS = 8
N_SIDE = 64
DILATIONS = (1, 4, 16)
ROPE_THETA = 500000.0
ROPE_DIM = 32
GLA_K = 512
GLA_V = 1024
GLA_HEADS = 4
GLA_DK = 128
GLA_DV = 256
GATE_RANK = 16
GATE_NORM = 16.0
CHUNK = 64
IN_MAIN = 6144
IN_W = 6176
Z_W = IN_W - IN_MAIN
D_FF = 5632
EPS = 1e-6
N_CHIPS = 4
N_DEV = 8
LANES = 128

ADAM_LR = 0.001
ADAM_B1 = 0.9
ADAM_B2 = 0.999
ADAM_EPS = 1e-08
ADAM_WD = 0.01
ADAM_STEP = 10

NEG = -1e30
MESH = pl.DeviceIdType.MESH
ANY = pl.BlockSpec(memory_space=pl.ANY)

NN = ((1,), (0,))
NT = ((1,), (1,))
TN = ((0,), (0,))


def _dot(a, b, dims=NN):
    return lax.dot_general(a, b, (dims, ((), ())), preferred_element_type=F32)


def _sigmoid(x):
    return 1.0 / (1.0 + jnp.exp(-x))


def _after(x, *deps):
    return lax.optimization_barrier((x,) + deps)[0]


def _matmul(name, pairs, grid, out_shape, out_spec, nk, res=None, into=None):
    n_in = 2 * len(pairs) + (res is not None)
    dims = [p[4] for p in pairs]

    n_ops = n_in + (into is not None)

    def body(*refs):
        ins, o_ref = refs[:n_in], refs[n_ops]

        def partial_sum():
            tot = None
            for p, dn in enumerate(dims):
                a, b = ins[2 * p][...], ins[2 * p + 1][...]
                t = _dot(a.astype(BF16), b.astype(BF16), dn)
                tot = t if tot is None else tot + t
            return tot

        if nk == 1:
            t = partial_sum()
            if res is not None:
                t = t + ins[-1][...]
            o_ref[...] = t.astype(o_ref.dtype)
        else:
            acc_ref = refs[n_ops + 1]
            k = pl.program_id(2)

            @pl.when(k == 0)
            def _():
                if res is not None:
                    acc_ref[...] = ins[-1][...]
                else:
                    acc_ref[...] = jnp.zeros_like(acc_ref)

            acc_ref[...] += partial_sum()

            @pl.when(k == nk - 1)
            def _():
                o_ref[...] = acc_ref[...].astype(o_ref.dtype)

    operands, in_specs = [], []
    for a, a_spec, b, b_spec, _ in pairs:
        operands += [a, b]
        in_specs += [a_spec, b_spec]
    if res is not None:
        operands.append(res[0])
        in_specs.append(res[1])
    acc_shape = tuple(s for s in out_spec.block_shape if s is not None)
    scratch = [pltpu.VMEM(acc_shape, F32)] if nk > 1 else []
    aliases = {}
    if into is not None:
        aliases = {len(operands): 0}
        operands.append(into)
        in_specs.append(ANY)
    return pl.pallas_call(
        body, name=name, grid=grid, in_specs=in_specs, out_specs=out_spec, out_shape=out_shape, scratch_shapes=scratch,
        input_output_aliases=aliases,
    )(*operands)


def _mm_nn(name, a, b, tm, tn, out_dtype, res=None):
    M, K = a.shape
    N = b.shape[1]
    pairs = [(a, pl.BlockSpec((tm, K), lambda j, i: (i, 0)), b, pl.BlockSpec((K, tn), lambda j, i: (0, j)), NN)]
    r = None if res is None else (res, pl.BlockSpec((tm, tn), lambda j, i: (i, j)))
    return _matmul(name, pairs, (N // tn, M // tm), jax.ShapeDtypeStruct((M, N), out_dtype),
                   pl.BlockSpec((tm, tn), lambda j, i: (i, j)), 1, r)


def _mm_nn_sharded(name, a, b4, tm, out_dtype):
    M, K = a.shape
    n = b4.shape[2]
    pairs = [(a, pl.BlockSpec((tm, K), lambda j, i: (i, 0)), b4, pl.BlockSpec((None, K, n), lambda j, i: (j, 0, 0)), NN)]
    return _matmul(name, pairs, (N_CHIPS, M // tm), jax.ShapeDtypeStruct((M, N_CHIPS * n), out_dtype),
                   pl.BlockSpec((tm, n), lambda j, i: (i, j)), 1)


def _mm_nt(name, a, b, tm, tn, out_dtype, res=None, n_out=None):
    M, K = a.shape
    N = b.shape[0] if n_out is None else n_out
    pairs = [(a, pl.BlockSpec((tm, K), lambda j, i: (i, 0)), b, pl.BlockSpec((tn, K), lambda j, i: (j, 0)), NT)]
    r = None if res is None else (res, pl.BlockSpec((tm, tn), lambda j, i: (i, j)))
    return _matmul(name, pairs, (N // tn, M // tm), jax.ShapeDtypeStruct((M, N), out_dtype),
                   pl.BlockSpec((tm, tn), lambda j, i: (i, j)), 1, r)


def _mm_tn(name, a, g, tka, tn, tmm, out_dtype, out3=None, rows_out=None):
    M, Ka = a.shape
    N = g.shape[1]
    pairs = [(a, pl.BlockSpec((tmm, tka), lambda i, j, k: (k, i)), g, pl.BlockSpec((tmm, tn), lambda i, j, k: (k, j)), TN)]
    if out3 is None:
        shape, spec = (Ka if rows_out is None else rows_out, N), pl.BlockSpec((tka, tn), lambda i, j, k: (i, j))
    else:
        shape, spec = (N // out3, Ka, out3), pl.BlockSpec((None, tka, tn), lambda i, j, k: (j, i, 0))
    return _matmul(name, pairs, (Ka // tka, N // tn, M // tmm), jax.ShapeDtypeStruct(shape, out_dtype), spec, M // tmm)


def _rms_fwd(name, x, g, tm=512):
    S, D = x.shape

    def body(x_ref, g_ref, o_ref):
        xv = x_ref[...]
        r = lax.rsqrt(jnp.mean(xv * xv, axis=-1, keepdims=True) + EPS)
        o_ref[...] = (xv * r * g_ref[...]).astype(o_ref.dtype)

    return pl.pallas_call(
        body, name=name, grid=(S // tm,),
        in_specs=[pl.BlockSpec((tm, D), lambda i: (i, 0)), pl.BlockSpec((1, D), lambda i: (0, 0))],
        out_specs=pl.BlockSpec((tm, D), lambda i: (i, 0)), out_shape=jax.ShapeDtypeStruct((S, D), BF16),
    )(x, g)


def _rms_bwd(name, x, g, dn, dres, tm=512):
    S, D = x.shape

    def body(x_ref, g_ref, dn_ref, dres_ref, dx_ref, dxb_ref, dg_ref):
        i = pl.program_id(0)

        @pl.when(i == 0)
        def _():
            dg_ref[...] = jnp.zeros_like(dg_ref)

        xv = x_ref[...]
        r = lax.rsqrt(jnp.mean(xv * xv, axis=-1, keepdims=True) + EPS)
        xhat = xv * r
        dnv = dn_ref[...].astype(F32)
        dg_ref[...] += jnp.sum(dnv * xhat, axis=0, keepdims=True)
        t = dnv * g_ref[...]
        dx = r * (t - xhat * jnp.mean(t * xhat, axis=-1, keepdims=True)) + dres_ref[...]
        dx_ref[...] = dx
        dxb_ref[...] = dx.astype(BF16)

    row = pl.BlockSpec((tm, D), lambda i: (i, 0))
    vec = pl.BlockSpec((1, D), lambda i: (0, 0))
    return pl.pallas_call(
        body, name=name, grid=(S // tm,), in_specs=[row, vec, row, row], out_specs=[row, row, vec],
        out_shape=[jax.ShapeDtypeStruct((S, D), F32), jax.ShapeDtypeStruct((S, D), BF16), jax.ShapeDtypeStruct((1, D), F32)],
    )(x, g, dn, dres)


def _final_loss(h2, g, target, tm=512):
    S, D = h2.shape

    def body(x_ref, g_ref, t_ref, loss_ref, dg_ref, dx_ref, dxb_ref):
        i = pl.program_id(0)

        @pl.when(i == 0)
        def _():
            loss_ref[...] = jnp.zeros_like(loss_ref)
            dg_ref[...] = jnp.zeros_like(dg_ref)

        xv = x_ref[...]
        r = lax.rsqrt(jnp.mean(xv * xv, axis=-1, keepdims=True) + EPS)
        xhat = xv * r
        gv = g_ref[...]
        diff = xhat * gv - t_ref[...]
        per_tok = jnp.mean(diff * diff, axis=-1, keepdims=True)
        loss_ref[...] += 0.5 * jnp.sum(per_tok, axis=0, keepdims=True)
        dy = diff * (1.0 / D)
        dg_ref[...] += jnp.sum(dy * xhat, axis=0, keepdims=True)
        t = dy * gv
        dx = r * (t - xhat * jnp.mean(t * xhat, axis=-1, keepdims=True))
        dx_ref[...] = dx
        dxb_ref[...] = dx.astype(BF16)

    row = pl.BlockSpec((tm, D), lambda i: (i, 0))
    vec = pl.BlockSpec((1, D), lambda i: (0, 0))
    return pl.pallas_call(
        body, name="final_loss", grid=(S // tm,), in_specs=[row, vec, row],
        out_specs=[pl.BlockSpec((1, LANES), lambda i: (0, 0)), vec, row, row],
        out_shape=[jax.ShapeDtypeStruct((1, LANES), F32), jax.ShapeDtypeStruct((1, D), F32),
                   jax.ShapeDtypeStruct((S, D), F32), jax.ShapeDtypeStruct((S, D), BF16)],
    )(h2, g, target)


def _rope_tables(S):
    pos = jnp.arange(S, dtype=F32)
    inv_freq = ROPE_THETA ** (-jnp.arange(0, ROPE_DIM, 2, dtype=F32) / ROPE_DIM)
    ang = pos[:, None] * inv_freq[None, :]
    cos, sin = jnp.cos(ang), jnp.sin(ang)
    half = ROPE_DIM // 2
    rest = HEAD - ROPE_DIM
    z_h, z_r = jnp.zeros((S, half), F32), jnp.zeros((S, rest), F32)
    tab_c = jnp.concatenate([cos, cos, jnp.ones((S, rest), F32)], axis=1)
    tab_up = jnp.concatenate([z_h, sin, z_r], axis=1)
    tab_dn = jnp.concatenate([-sin, z_h, z_r], axis=1)
    return tab_c, tab_up, tab_dn


def _rope_head(t, c, up, dn):
    half = ROPE_DIM // 2
    return t * c + pltpu.roll(t, half, axis=1) * up + pltpu.roll(t, HEAD - half, axis=1) * dn


def _rope_fwd(proj, tabs, tm=512):
    S = proj.shape[0]
    W = 2 * ATTN_W

    def body(p_ref, c_ref, up_ref, dn_ref, o_ref):
        c, up, dn = c_ref[...], up_ref[...], dn_ref[...]
        for h in range(W // HEAD):
            sl = slice(h * HEAD, (h + 1) * HEAD)
            o_ref[:, sl] = _rope_head(p_ref[:, sl].astype(F32), c, up, dn).astype(BF16)

    tab = pl.BlockSpec((tm, HEAD), lambda i: (i, 0))
    return pl.pallas_call(
        body, name="rope_fwd", grid=(S // tm,), in_specs=[pl.BlockSpec((tm, W), lambda i: (i, 0)), tab, tab, tab],
        out_specs=pl.BlockSpec((tm, W), lambda i: (i, 0)), out_shape=jax.ShapeDtypeStruct((S, W), BF16),
    )(proj, *tabs)


def _attn_grad_merge(dqs, dks, dvs, tabs, tm=256):
    S = dqs[0].shape[0]

    def body(*refs):
        q_refs, k_refs, v_refs = refs[0:3], refs[3:6], refs[6:9]
        c, up, dn = refs[9][...], refs[10][...], refs[11][...]
        o_ref = refs[12]
        for h in range(N_HEADS):
            sl = slice(h * HEAD, (h + 1) * HEAD)
            for part, rs in ((0, q_refs), (1, k_refs)):
                t = rs[0][:, sl].astype(F32) + rs[1][:, sl].astype(F32) + rs[2][:, sl].astype(F32)
                osl = slice(part * ATTN_W + h * HEAD, part * ATTN_W + (h + 1) * HEAD)
                o_ref[:, osl] = _rope_head(t, c, -up, -dn).astype(BF16)
        o_ref[:, 2 * ATTN_W:] = (v_refs[0][...].astype(F32) + v_refs[1][...].astype(F32)
                                 + v_refs[2][...].astype(F32)).astype(BF16)

    blk = pl.BlockSpec((tm, ATTN_W), lambda i: (i, 0))
    tab = pl.BlockSpec((tm, HEAD), lambda i: (i, 0))
    return pl.pallas_call(
        body, name="attn_grad_merge", grid=(S // tm,), in_specs=[blk] * 9 + [tab] * 3,
        out_specs=pl.BlockSpec((tm, 3 * ATTN_W), lambda i: (i, 0)), out_shape=jax.ShapeDtypeStruct((S, 3 * ATTN_W), BF16),
    )(*dqs, *dks, *dvs, *tabs)


SUB = 128
Q_COL, K_COL, V_COL = 0, ATTN_W // HEAD, 2 * ATTN_W // HEAD


class _AttnGeo:
    def __init__(self, S, d):
        self.S, self.d, self.L = S, d, S // d
        self.TB = min(2048, S)
        self.halo = N_SIDE * d
        self.W = self.TB + 2 * self.halo
        self.n_sub = self.TB // SUB
        self.grid = (S // self.TB, N_HEADS)
        self.dt = F32 if d > 1 else BF16
        assert self.TB % (SUB * d) == 0 and self.TB % self.halo == 0

    def specs(self, width, col0, per_head=True):
        ratio = self.TB // self.halo
        last = self.S // self.halo - 1
        col = (lambda h: col0 + h) if per_head else (lambda h: col0)
        cur = pl.BlockSpec((self.TB, width), lambda i, h: (i, col(h)))
        prev = pl.BlockSpec((self.halo, width), lambda i, h: (jnp.maximum(i * ratio - 1, 0), col(h)))
        nxt = pl.BlockSpec((self.halo, width), lambda i, h: (jnp.minimum((i + 1) * ratio, last), col(h)))
        return cur, prev, nxt

    def rows(self, sub, n):
        res, blk = sub % self.d, sub // self.d
        start = res + self.d * SUB * blk
        return pl.ds(start, n, stride=self.d) if self.d > 1 else pl.ds(start, n)

    def mask(self, sub):
        base = pl.program_id(0) * (self.TB // self.d) + SUB * (sub // self.d)
        row = lax.broadcasted_iota(jnp.int32, (SUB, 2 * SUB), 0)
        col = lax.broadcasted_iota(jnp.int32, (SUB, 2 * SUB), 1)
        pos = base - N_SIDE + col
        return (col >= row) & (col <= row + 2 * N_SIDE) & (pos >= 0) & (pos < self.L)

    def fill(self, dst, c_ref):
        dst[...] = c_ref[...].astype(dst.dtype)

    def fill_window(self, dst, p_ref, c_ref, n_ref):
        dst[0:self.halo] = p_ref[...].astype(dst.dtype)
        dst[self.halo:self.halo + self.TB] = c_ref[...].astype(dst.dtype)
        dst[self.halo + self.TB:] = n_ref[...].astype(dst.dtype)


def _lane_of(tile, h):
    lane = lax.broadcasted_iota(jnp.int32, tile.shape, 1)
    return jnp.sum(jnp.where(lane == h, tile, 0.0), axis=1, keepdims=True)


def _attn_fwd(qk, proj, d):
    S = qk.shape[0]
    geo = _AttnGeo(S, d)
    scale = HEAD ** -0.5

    def body(q_ref, kp, kc, kn, vp, vc, vn, o_ref, lse_ref, qs, ks, vs, os, ls):
        h = pl.program_id(1)
        geo.fill(qs, q_ref)
        geo.fill_window(ks, kp, kc, kn)
        geo.fill_window(vs, vp, vc, vn)
        for sub in range(geo.n_sub):
            rq, rw = geo.rows(sub, SUB), geo.rows(sub, 2 * SUB)
            q_r, k_r, v_r = qs[rq, :].astype(BF16), ks[rw, :].astype(BF16), vs[rw, :].astype(BF16)
            s = jnp.where(geo.mask(sub), _dot(q_r, k_r, NT) * scale, NEG)
            m = jnp.max(s, axis=1, keepdims=True)
            p = jnp.exp(s - m)
            l = jnp.sum(p, axis=1, keepdims=True)
            os[rq, :] = _dot(p.astype(BF16), v_r) / l
            ls[rq, :] = jnp.broadcast_to(m + jnp.log(l), (SUB, LANES))
        o_ref[...] = os[...].astype(BF16)

        @pl.when(h == 0)
        def _():
            lse_ref[...] = jnp.zeros_like(lse_ref)

        lane = lax.broadcasted_iota(jnp.int32, (geo.TB, LANES), 1)
        lse_ref[...] = jnp.where(lane == h, ls[...], lse_ref[...])

    q_cur, _, _ = geo.specs(HEAD, Q_COL)
    k_specs = geo.specs(HEAD, K_COL)
    v_specs = geo.specs(HEAD, V_COL)
    stat = pl.BlockSpec((geo.TB, LANES), lambda i, h: (i, 0))
    return pl.pallas_call(
        body, name=f"attn_fwd_d{d}", grid=geo.grid,
        in_specs=[q_cur, k_specs[1], k_specs[0], k_specs[2], v_specs[1], v_specs[0], v_specs[2]],
        out_specs=[q_cur, stat],
        out_shape=[jax.ShapeDtypeStruct((S, ATTN_W), BF16), jax.ShapeDtypeStruct((S, LANES), F32)],
        scratch_shapes=[pltpu.VMEM((geo.TB, HEAD), geo.dt), pltpu.VMEM((geo.W, HEAD), geo.dt), pltpu.VMEM((geo.W, HEAD), geo.dt),
                        pltpu.VMEM((geo.TB, HEAD), F32), pltpu.VMEM((geo.TB, LANES), F32)],
    )(qk, qk, qk, qk, proj, proj, proj)


def _attn_combine(outs, lses, g, tm=256):
    S = outs[0].shape[0]

    def body(o1, o2, o3, l1, l2, l3, g_ref, ao_ref, o_ref, lse_ref):
        lane = lax.broadcasted_iota(jnp.int32, (tm, LANES), 1)
        lse_tile = jnp.zeros((tm, LANES), F32)
        a = [l1[...], l2[...], l3[...]]
        ssq = jnp.zeros((tm, 1), F32)
        for h in range(N_HEADS):
            sl = slice(h * HEAD, (h + 1) * HEAD)
            a1, a2, a3 = (t[:, h:h + 1] for t in a)
            mx = jnp.maximum(jnp.maximum(a1, a2), a3)
            e1, e2, e3 = jnp.exp(a1 - mx), jnp.exp(a2 - mx), jnp.exp(a3 - mx)
            den = e1 + e2 + e3
            oh = (e1 * o1[:, sl].astype(F32) + e2 * o2[:, sl].astype(F32) + e3 * o3[:, sl].astype(F32)) / den
            o_ref[:, sl] = oh
            ssq = ssq + jnp.sum(oh * oh, axis=1, keepdims=True)
            lse_tile = jnp.where(lane == h, mx + jnp.log(den), lse_tile)
        lse_ref[...] = lse_tile
        r = lax.rsqrt(ssq * (1.0 / ATTN_W) + EPS)
        ao_ref[...] = (o_ref[...] * r * g_ref[...]).astype(BF16)

    blk = pl.BlockSpec((tm, ATTN_W), lambda i: (i, 0))
    ls = pl.BlockSpec((tm, LANES), lambda i: (i, 0))
    return pl.pallas_call(
        body, name="attn_combine", grid=(S // tm,),
        in_specs=[blk, blk, blk, ls, ls, ls, pl.BlockSpec((1, ATTN_W), lambda i: (0, 0))], out_specs=[blk, blk, ls],
        out_shape=[jax.ShapeDtypeStruct((S, ATTN_W), BF16), jax.ShapeDtypeStruct((S, ATTN_W), F32),
                   jax.ShapeDtypeStruct((S, LANES), F32)],
    )(*outs, *lses, g)


def _attn_norm_bwd(o, g, dao, tm=256):
    S = o.shape[0]

    def body(o_ref, g_ref, dao_ref, do_ref, dl_ref, dg_ref):
        i = pl.program_id(0)

        @pl.when(i == 0)
        def _():
            dg_ref[...] = jnp.zeros_like(dg_ref)

        ov = o_ref[...]
        r = lax.rsqrt(jnp.mean(ov * ov, axis=-1, keepdims=True) + EPS)
        ohat = ov * r
        dn = dao_ref[...].astype(F32)
        dg_ref[...] += jnp.sum(dn * ohat, axis=0, keepdims=True)
        t = dn * g_ref[...]
        do = r * (t - ohat * jnp.mean(t * ohat, axis=-1, keepdims=True))
        do_ref[...] = do.astype(BF16)
        prod = do * ov
        lane = lax.broadcasted_iota(jnp.int32, (tm, LANES), 1)
        tile = jnp.zeros((tm, LANES), F32)
        for h in range(N_HEADS):
            tile = jnp.where(lane == h, jnp.sum(prod[:, h * HEAD:(h + 1) * HEAD], axis=1, keepdims=True), tile)
        dl_ref[...] = tile

    blk = pl.BlockSpec((tm, ATTN_W), lambda i: (i, 0))
    vec = pl.BlockSpec((1, ATTN_W), lambda i: (0, 0))
    return pl.pallas_call(
        body, name="attn_norm_bwd", grid=(S // tm,),
        in_specs=[blk, vec, pl.BlockSpec((tm, ATTN_W), lambda i: (i, 0))],
        out_specs=[blk, pl.BlockSpec((tm, LANES), lambda i: (i, 0)), vec],
        out_shape=[jax.ShapeDtypeStruct((S, ATTN_W), BF16), jax.ShapeDtypeStruct((S, LANES), F32),
                   jax.ShapeDtypeStruct((1, ATTN_W), F32)],
    )(o, g, dao)


def _attn_bwd_dq(qk, proj, do, lse, delta, d):
    S = qk.shape[0]
    geo = _AttnGeo(S, d)
    scale = HEAD ** -0.5

    def body(q_ref, kp, kc, kn, vp, vc, vn, do_ref, lse_ref, dl_ref, dq_ref, qs, ks, vs, dos, dqs):
        h = pl.program_id(1)
        geo.fill(qs, q_ref)
        geo.fill(dos, do_ref)
        geo.fill_window(ks, kp, kc, kn)
        geo.fill_window(vs, vp, vc, vn)
        for sub in range(geo.n_sub):
            rq, rw = geo.rows(sub, SUB), geo.rows(sub, 2 * SUB)
            q_r, k_r, v_r = qs[rq, :].astype(BF16), ks[rw, :].astype(BF16), vs[rw, :].astype(BF16)
            lse_c, dl_c = _lane_of(lse_ref[rq, :], h), _lane_of(dl_ref[rq, :], h)
            s = _dot(q_r, k_r, NT) * scale
            p = jnp.where(geo.mask(sub), jnp.exp(s - lse_c), 0.0)
            dp = _dot(dos[rq, :].astype(BF16), v_r, NT)
            ds = (p * (dp - dl_c) * scale).astype(BF16)
            dqs[rq, :] = _dot(ds, k_r)
        dq_ref[...] = dqs[...].astype(BF16)

    cur, _, _ = geo.specs(HEAD, 0)
    k_specs = geo.specs(HEAD, K_COL)
    v_specs = geo.specs(HEAD, V_COL)
    stat = pl.BlockSpec((geo.TB, LANES), lambda i, h: (i, 0))
    return pl.pallas_call(
        body, name=f"attn_bwd_dq_d{d}", grid=geo.grid,
        in_specs=[cur, k_specs[1], k_specs[0], k_specs[2], v_specs[1], v_specs[0], v_specs[2], cur, stat, stat],
        out_specs=cur, out_shape=jax.ShapeDtypeStruct((S, ATTN_W), BF16),
        scratch_shapes=[pltpu.VMEM((geo.TB, HEAD), geo.dt), pltpu.VMEM((geo.W, HEAD), geo.dt), pltpu.VMEM((geo.W, HEAD), geo.dt),
                        pltpu.VMEM((geo.TB, HEAD), geo.dt), pltpu.VMEM((geo.TB, HEAD), F32)],
    )(qk, qk, qk, qk, proj, proj, proj, do, lse, delta)


def _attn_bwd_dkv(qk, proj, do, lse, delta, d):
    S = qk.shape[0]
    geo = _AttnGeo(S, d)
    scale = HEAD ** -0.5

    def body(k_ref, v_ref, qp, qc, qn, dop, doc, don, lp, lc, ln, dlp, dlc, dln, dk_ref, dv_ref,
             ks, vs, qs, dos, lws, dlws, dks, dvs):
        h = pl.program_id(1)
        geo.fill(ks, k_ref)
        geo.fill(vs, v_ref)
        geo.fill_window(qs, qp, qc, qn)
        geo.fill_window(dos, dop, doc, don)
        geo.fill_window(lws, lp, lc, ln)
        geo.fill_window(dlws, dlp, dlc, dln)
        head = lax.broadcasted_iota(jnp.int32, (LANES, 2 * SUB), 0)
        for sub in range(geo.n_sub):
            rq, rw = geo.rows(sub, SUB), geo.rows(sub, 2 * SUB)
            k_r, v_r = ks[rq, :].astype(BF16), vs[rq, :].astype(BF16)
            q_w, do_w = qs[rw, :].astype(BF16), dos[rw, :].astype(BF16)
            lse_row = jnp.sum(jnp.where(head == h, lws[rw, :].T, 0.0), axis=0, keepdims=True)
            dl_row = jnp.sum(jnp.where(head == h, dlws[rw, :].T, 0.0), axis=0, keepdims=True)
            st = _dot(k_r, q_w, NT) * scale
            pt = jnp.where(geo.mask(sub), jnp.exp(st - lse_row), 0.0)
            dvs[rq, :] = _dot(pt.astype(BF16), do_w)
            dpt = _dot(v_r, do_w, NT)
            dst = (pt * (dpt - dl_row) * scale).astype(BF16)
            dks[rq, :] = _dot(dst, q_w)
        dk_ref[...] = dks[...].astype(BF16)
        dv_ref[...] = dvs[...].astype(BF16)

    q_specs = geo.specs(HEAD, Q_COL)
    k_cur, _, _ = geo.specs(HEAD, K_COL)
    v_cur, _, _ = geo.specs(HEAD, V_COL)
    do_specs = geo.specs(HEAD, 0)
    st_specs = geo.specs(LANES, 0, per_head=False)
    cur = do_specs[0]
    return pl.pallas_call(
        body, name=f"attn_bwd_dkv_d{d}", grid=geo.grid,
        in_specs=[k_cur, v_cur, q_specs[1], q_specs[0], q_specs[2], do_specs[1], do_specs[0], do_specs[2],
                  st_specs[1], st_specs[0], st_specs[2], st_specs[1], st_specs[0], st_specs[2]],
        out_specs=[cur, cur],
        out_shape=[jax.ShapeDtypeStruct((S, ATTN_W), BF16), jax.ShapeDtypeStruct((S, ATTN_W), BF16)],
        scratch_shapes=[pltpu.VMEM((geo.TB, HEAD), geo.dt), pltpu.VMEM((geo.TB, HEAD), geo.dt),
                        pltpu.VMEM((geo.W, HEAD), geo.dt), pltpu.VMEM((geo.W, HEAD), geo.dt),
                        pltpu.VMEM((geo.W, LANES), F32), pltpu.VMEM((geo.W, LANES), F32),
                        pltpu.VMEM((geo.TB, HEAD), F32), pltpu.VMEM((geo.TB, HEAD), F32)],
    )(qk, proj, qk, qk, qk, do, do, do, lse, lse, lse, delta, delta, delta)


def _cumsum_rows(x, reverse):
    n = x.shape[0]
    row = lax.broadcasted_iota(jnp.int32, x.shape, 0)
    s = 1
    while s < n:
        if reverse:
            x = x + jnp.where(row < n - s, pltpu.roll(x, n - s, axis=0), 0.0)
        else:
            x = x + jnp.where(row >= s, pltpu.roll(x, s, axis=0), 0.0)
        s *= 2
    return x


def _gla_chunk_terms(q_ref, k_ref, v_ref, g_ref, h, reverse):
    ksl = slice(h * GLA_DK, (h + 1) * GLA_DK)
    q = q_ref[:, ksl].astype(F32) * (GLA_DK ** -0.5)
    k = k_ref[:, ksl].astype(F32)
    v = v_ref[:, h * GLA_DV:(h + 1) * GLA_DV]
    b = _cumsum_rows(g_ref[:, ksl], reverse)
    r_ref = CHUNK // 2 if reverse else CHUNK // 2 - 1
    r_last = 0 if reverse else CHUNK - 1
    b_ref, b_last = b[r_ref:r_ref + 1, :], b[r_last:r_last + 1, :]
    ii = lax.broadcasted_iota(jnp.int32, (CHUNK, CHUNK), 0)
    jj = lax.broadcasted_iota(jnp.int32, (CHUNK, CHUNK), 1)
    causal = (jj >= ii) if reverse else (jj <= ii)
    e_q, e_k = jnp.exp(b - b_ref), jnp.exp(b_ref - b)
    e_in, e_st = jnp.exp(b), jnp.exp(b_last - b)
    return dict(q=q, k=k, v=v, b=b, causal=causal, e_q=e_q, e_k=e_k, e_in=e_in, e_st=e_st, dec=jnp.exp(b_last),
                qe=q * e_q, ke=k * e_k, q_in=q * e_in, k_st=k * e_st, r_ref=r_ref, r_last=r_last)


def _gla_specs(n, order):
    q = pl.BlockSpec((CHUNK, GLA_K), lambda c: (order(c), 3 * ATTN_W // GLA_K))
    k = pl.BlockSpec((CHUNK, GLA_K), lambda c: (order(c), 3 * ATTN_W // GLA_K + 1))
    v = pl.BlockSpec((CHUNK, GLA_V), lambda c: (order(c), (3 * ATTN_W + 2 * GLA_K) // GLA_V))
    return q, k, v


def _gla_fwd(proj, gates, reverse, o_prev=None):
    S = proj.shape[0]
    n = S // CHUNK
    order = (lambda c: n - 1 - c) if reverse else (lambda c: c)
    gcol = 1 if reverse else 0

    def body(*refs):
        if o_prev is None:
            q_ref, k_ref, v_ref, g_ref, o_ref, st_ref, state = refs
        else:
            q_ref, k_ref, v_ref, g_ref, op_ref, o_ref, st_ref, state = refs
        c = pl.program_id(0)

        @pl.when(c == 0)
        def _():
            state[...] = jnp.zeros_like(state)

        for h in range(GLA_HEADS):
            t = _gla_chunk_terms(q_ref, k_ref, v_ref, g_ref, h, reverse)
            a = jnp.where(t["causal"], _dot(t["qe"].astype(BF16), t["ke"].astype(BF16), NT), 0.0)
            o = _dot(a.astype(BF16), t["v"])
            st = state[h]
            st_b = st.astype(BF16)
            st_ref[0, h] = st_b
            o = o + _dot(t["q_in"].astype(BF16), st_b, NT)
            state[h] = st * t["dec"] + _dot(t["v"], t["k_st"].astype(BF16), TN)
            vsl = slice(h * GLA_DV, (h + 1) * GLA_DV)
            if o_prev is not None:
                o = o + op_ref[:, vsl]
            o_ref[:, vsl] = o

    q_spec, k_spec, v_spec = _gla_specs(n, order)
    o_spec = pl.BlockSpec((CHUNK, GLA_V), lambda c: (order(c), 0))
    in_specs = [q_spec, k_spec, v_spec, pl.BlockSpec((CHUNK, GLA_K), lambda c: (order(c), gcol))]
    operands = [proj, proj, proj, gates]
    if o_prev is not None:
        in_specs.append(o_spec)
        operands.append(o_prev)
    return pl.pallas_call(
        body, name="gla_fwd_rev" if reverse else "gla_fwd", grid=(n,), in_specs=in_specs,
        out_specs=[o_spec, pl.BlockSpec((1, GLA_HEADS, GLA_DV, GLA_DK), lambda c: (order(c), 0, 0, 0))],
        out_shape=[jax.ShapeDtypeStruct((S, GLA_V), F32), jax.ShapeDtypeStruct((n, GLA_HEADS, GLA_DV, GLA_DK), BF16)],
        scratch_shapes=[pltpu.VMEM((GLA_HEADS, GLA_DV, GLA_DK), F32)],
    )(*operands)


def _gla_bwd(proj, gates, states, do, reverse, prev=None):
    S = proj.shape[0]
    n = S // CHUNK
    order = (lambda c: c) if reverse else (lambda c: n - 1 - c)
    gcol = 1 if reverse else 0
    out_dt = F32 if prev is None else BF16

    def body(*refs):
        if prev is None:
            q_ref, k_ref, v_ref, g_ref, st_ref, do_ref, dq_ref, dk_ref, dv_ref, dg_ref, dstate = refs
        else:
            q_ref, k_ref, v_ref, g_ref, st_ref, do_ref, pq, pk, pv, dq_ref, dk_ref, dv_ref, dg_ref, dstate = refs
        c = pl.program_id(0)

        @pl.when(c == 0)
        def _():
            dstate[...] = jnp.zeros_like(dstate)

        row = lax.broadcasted_iota(jnp.int32, (CHUNK, GLA_DK), 0)
        for h in range(GLA_HEADS):
            t = _gla_chunk_terms(q_ref, k_ref, v_ref, g_ref, h, reverse)
            ksl = slice(h * GLA_DK, (h + 1) * GLA_DK)
            vsl = slice(h * GLA_DV, (h + 1) * GLA_DV)
            v = t["v"]
            dob = do_ref[:, vsl].astype(BF16)
            st_b = st_ref[0, h]
            dst = dstate[h]
            dst_b = dst.astype(BF16)
            qe_b, ke_b = t["qe"].astype(BF16), t["ke"].astype(BF16)
            q_in_b, k_st_b = t["q_in"].astype(BF16), t["k_st"].astype(BF16)
            a = jnp.where(t["causal"], _dot(qe_b, ke_b, NT), 0.0)
            da = jnp.where(t["causal"], _dot(dob, v, NT), 0.0).astype(BF16)
            dv = _dot(a.astype(BF16), dob, TN) + _dot(k_st_b, dst_b, NT)
            dqe = _dot(da, ke_b)
            dke = _dot(da, qe_b, TN)
            dq_in = _dot(dob, st_b)
            dk_st = _dot(v, dst_b)
            ddec = jnp.sum(dst * st_b.astype(F32), axis=0, keepdims=True)
            dstate[h] = _dot(dob, q_in_b, TN) + dst * t["dec"]
            dq = (dqe * t["e_q"] + dq_in * t["e_in"]) * (GLA_DK ** -0.5)
            dk = dke * t["e_k"] + dk_st * t["e_st"]
            w_q, w_k = dqe * t["qe"], dke * t["ke"]
            w_st = dk_st * t["k_st"]
            db = w_q - w_k + dq_in * t["q_in"] - w_st
            db_ref = jnp.sum(w_k - w_q, axis=0, keepdims=True)
            db_last = jnp.sum(w_st, axis=0, keepdims=True) + ddec * t["dec"]
            db = db + jnp.where(row == t["r_ref"], db_ref, 0.0) + jnp.where(row == t["r_last"], db_last, 0.0)
            dg_ref[:, ksl] = _cumsum_rows(db, not reverse)
            if prev is not None:
                dq, dk, dv = dq + pq[:, ksl], dk + pk[:, ksl], dv + pv[:, vsl]
            dq_ref[:, ksl] = dq.astype(out_dt)
            dk_ref[:, ksl] = dk.astype(out_dt)
            dv_ref[:, vsl] = dv.astype(out_dt)

    q_spec, k_spec, v_spec = _gla_specs(n, order)
    kk = pl.BlockSpec((CHUNK, GLA_K), lambda c: (order(c), 0))
    vv = pl.BlockSpec((CHUNK, GLA_V), lambda c: (order(c), 0))
    in_specs = [q_spec, k_spec, v_spec, pl.BlockSpec((CHUNK, GLA_K), lambda c: (order(c), gcol)),
                pl.BlockSpec((1, GLA_HEADS, GLA_DV, GLA_DK), lambda c: (order(c), 0, 0, 0)), vv]
    operands = [proj, proj, proj, gates, states, do]
    if prev is not None:
        in_specs += [kk, kk, vv]
        operands += list(prev)
    return pl.pallas_call(
        body, name="gla_bwd_rev" if reverse else "gla_bwd", grid=(n,), in_specs=in_specs, out_specs=[kk, kk, vv, kk],
        out_shape=[jax.ShapeDtypeStruct((S, GLA_K), out_dt), jax.ShapeDtypeStruct((S, GLA_K), out_dt),
                   jax.ShapeDtypeStruct((S, GLA_V), out_dt), jax.ShapeDtypeStruct((S, GLA_K), F32)],
        scratch_shapes=[pltpu.VMEM((GLA_HEADS, GLA_DV, GLA_DK), F32)],
    )(*operands)


def _gates_fwd(z, wg, bias, tm=512):
    S = z.shape[0]
    W = 2 * GLA_K

    def body(z_ref, w_ref, b_ref, o_ref):
        zg = _dot(z_ref[...], w_ref[...]) + b_ref[...]
        o_ref[...] = (jnp.minimum(zg, 0.0) - jnp.log(1.0 + jnp.exp(-jnp.abs(zg)))) * (1.0 / GATE_NORM)

    return pl.pallas_call(
        body, name="gates_fwd", grid=(S // tm,),
        in_specs=[pl.BlockSpec((tm, Z_W), lambda i: (i, 0)), pl.BlockSpec((Z_W, W), lambda i: (0, 0)),
                  pl.BlockSpec((1, W), lambda i: (0, 0))],
        out_specs=pl.BlockSpec((tm, W), lambda i: (i, 0)), out_shape=jax.ShapeDtypeStruct((S, W), F32),
    )(z, wg, bias)


def _gates_bwd(z, wg, bias, dg_f, dg_b, tm=512):
    S = z.shape[0]
    W = 2 * GLA_K

    def body(z_ref, w_ref, b_ref, dgf_ref, dgb_ref, dz_ref, dw_ref, db_ref):
        i = pl.program_id(0)

        @pl.when(i == 0)
        def _():
            dw_ref[...] = jnp.zeros_like(dw_ref)
            db_ref[...] = jnp.zeros_like(db_ref)

        zv = z_ref[...]
        zg = _dot(zv, w_ref[...]) + b_ref[...]
        dg = jnp.concatenate([dgf_ref[...], dgb_ref[...]], axis=1)
        dzg = dg * (1.0 / GATE_NORM) * _sigmoid(-zg)
        db_ref[...] += jnp.sum(dzg, axis=0, keepdims=True)
        dzg_b = dzg.astype(BF16)
        dw_ref[...] += _dot(zv, dzg_b, TN)
        dz_ref[...] = _dot(dzg_b, w_ref[...], NT).astype(BF16)

    half = pl.BlockSpec((tm, GLA_K), lambda i: (i, 0))
    return pl.pallas_call(
        body, name="gates_bwd", grid=(S // tm,),
        in_specs=[pl.BlockSpec((tm, Z_W), lambda i: (i, 0)), pl.BlockSpec((Z_W, W), lambda i: (0, 0)),
                  pl.BlockSpec((1, W), lambda i: (0, 0)), half, half],
        out_specs=[pl.BlockSpec((tm, Z_W), lambda i: (i, 0)), pl.BlockSpec((Z_W, W), lambda i: (0, 0)),
                   pl.BlockSpec((1, W), lambda i: (0, 0))],
        out_shape=[jax.ShapeDtypeStruct((S, Z_W), BF16), jax.ShapeDtypeStruct((Z_W, W), F32),
                   jax.ShapeDtypeStruct((1, W), F32)],
    )(z, wg, bias, dg_f, dg_b)


def _gla_out_fwd(o, proj, g, tm=512):
    S = o.shape[0]

    def body(o_ref, gr_ref, g_ref, out_ref):
        gn = g_ref[...]
        for h in range(GLA_HEADS):
            sl = slice(h * GLA_DV, (h + 1) * GLA_DV)
            ov = o_ref[:, sl]
            r = lax.rsqrt(jnp.mean(ov * ov, axis=-1, keepdims=True) + EPS)
            gr = gr_ref[:, sl].astype(F32)
            out_ref[:, sl] = (ov * r * gn * (gr * _sigmoid(gr))).astype(BF16)

    blk = pl.BlockSpec((tm, GLA_V), lambda i: (i, 0))
    return pl.pallas_call(
        body, name="gla_out_fwd", grid=(S // tm,),
        in_specs=[blk, pl.BlockSpec((tm, GLA_V), lambda i: (i, (3 * ATTN_W + 2 * GLA_K + GLA_V) // GLA_V)),
                  pl.BlockSpec((1, GLA_DV), lambda i: (0, 0))],
        out_specs=blk, out_shape=jax.ShapeDtypeStruct((S, GLA_V), BF16),
    )(o, proj, g)


def _gla_out_bwd(o, proj, g, dcat, tm=512):
    S = o.shape[0]

    def body(o_ref, gr_ref, g_ref, dgo_ref, do_ref, dgr_ref, dg_ref):
        i = pl.program_id(0)

        @pl.when(i == 0)
        def _():
            dg_ref[...] = jnp.zeros_like(dg_ref)

        gn = g_ref[...]
        dg_acc = jnp.zeros((1, GLA_DV), F32)
        for h in range(GLA_HEADS):
            sl = slice(h * GLA_DV, (h + 1) * GLA_DV)
            ov = o_ref[:, sl]
            r = lax.rsqrt(jnp.mean(ov * ov, axis=-1, keepdims=True) + EPS)
            yhat = ov * r
            gr = gr_ref[:, sl].astype(F32)
            sg = _sigmoid(gr)
            dgo = dgo_ref[:, sl].astype(F32)
            dgr_ref[:, sl] = (dgo * (yhat * gn) * (sg * (1.0 + gr * (1.0 - sg)))).astype(BF16)
            dy = dgo * (gr * sg)
            dg_acc = dg_acc + jnp.sum(dy * yhat, axis=0, keepdims=True)
            t = dy * gn
            do_ref[:, sl] = r * (t - yhat * jnp.mean(t * yhat, axis=-1, keepdims=True))
        dg_ref[...] += dg_acc

    blk = pl.BlockSpec((tm, GLA_V), lambda i: (i, 0))
    vec = pl.BlockSpec((1, GLA_DV), lambda i: (0, 0))
    return pl.pallas_call(
        body, name="gla_out_bwd", grid=(S // tm,),
        in_specs=[blk, pl.BlockSpec((tm, GLA_V), lambda i: (i, (3 * ATTN_W + 2 * GLA_K + GLA_V) // GLA_V)), vec,
                  pl.BlockSpec((tm, GLA_V), lambda i: (i, 1))],
        out_specs=[blk, blk, vec],
        out_shape=[jax.ShapeDtypeStruct((S, GLA_V), F32), jax.ShapeDtypeStruct((S, GLA_V), BF16),
                   jax.ShapeDtypeStruct((1, GLA_DV), F32)],
    )(o, proj, g, dcat)


HALO = 16


def _halo_specs(tm, tn, S):
    cur = pl.BlockSpec((tm, tn), lambda j, i: (i, j))
    prev = pl.BlockSpec((HALO, tn), lambda j, i: (jnp.maximum(i * (tm // HALO) - 1, 0), j))
    nxt = pl.BlockSpec((HALO, tn), lambda j, i: (jnp.minimum((i + 1) * (tm // HALO), S // HALO - 1), j))
    return cur, prev, nxt


def _shifted(x, p_ref, n_ref, n_blocks):
    i = pl.program_id(1)
    tm = x.shape[0]
    row = lax.broadcasted_iota(jnp.int32, x.shape, 0)
    before = p_ref[HALO - 1:HALO, :].astype(F32) * (i > 0).astype(F32)
    after = n_ref[0:1, :].astype(F32) * (i < n_blocks - 1).astype(F32)
    x_m1 = jnp.where(row == 0, before, pltpu.roll(x, 1, axis=0))
    x_p1 = jnp.where(row == tm - 1, after, pltpu.roll(x, tm - 1, axis=0))
    return x_m1, x_p1


def _glu_fwd(gp, up, cw, cb, tm=512, tn=1408):
    S = gp.shape[0]
    nb = S // tm

    def body(c_ref, p_ref, n_ref, up_ref, w_ref, b_ref, o_ref):
        x = c_ref[...].astype(F32)
        x_m1, x_p1 = _shifted(x, p_ref, n_ref, nb)
        w = w_ref[...]
        gate = w[0:1, :] * x_m1 + w[1:2, :] * x + w[2:3, :] * x_p1 + b_ref[...]
        o_ref[...] = (gate * _sigmoid(gate) * up_ref[...].astype(F32)).astype(BF16)

    cur, prev, nxt = _halo_specs(tm, tn, S)
    return pl.pallas_call(
        body, name="glu_fwd", grid=(D_FF // tn, nb),
        in_specs=[cur, prev, nxt, cur, pl.BlockSpec((3, tn), lambda j, i: (0, j)), pl.BlockSpec((1, tn), lambda j, i: (0, j))],
        out_specs=cur, out_shape=jax.ShapeDtypeStruct((S, D_FF), BF16),
    )(gp, gp, gp, up, cw, cb)


def _glu_bwd(gp, up, dact, cw, cb, tm=512, tn=1408):
    S = gp.shape[0]
    nb = S // tm

    def body(c_ref, p_ref, n_ref, up_ref, da_ref, w_ref, b_ref, dup_ref, dgate_ref, dw_ref, db_ref):
        @pl.when(pl.program_id(1) == 0)
        def _():
            dw_ref[...] = jnp.zeros_like(dw_ref)
            db_ref[...] = jnp.zeros_like(db_ref)

        x = c_ref[...].astype(F32)
        x_m1, x_p1 = _shifted(x, p_ref, n_ref, nb)
        w = w_ref[...]
        gate = w[0:1, :] * x_m1 + w[1:2, :] * x + w[2:3, :] * x_p1 + b_ref[...]
        sg = _sigmoid(gate)
        da = da_ref[...].astype(F32)
        dup_ref[...] = (da * (gate * sg)).astype(BF16)
        dgate = da * up_ref[...].astype(F32) * (sg * (1.0 + gate * (1.0 - sg)))
        dgate_ref[...] = dgate.astype(BF16)
        db_ref[...] += jnp.sum(dgate, axis=0, keepdims=True)
        dw_ref[...] += jnp.concatenate(
            [jnp.sum(dgate * x_m1, axis=0, keepdims=True), jnp.sum(dgate * x, axis=0, keepdims=True),
             jnp.sum(dgate * x_p1, axis=0, keepdims=True)], axis=0)

    cur, prev, nxt = _halo_specs(tm, tn, S)
    w_spec = pl.BlockSpec((3, tn), lambda j, i: (0, j))
    b_spec = pl.BlockSpec((1, tn), lambda j, i: (0, j))
    return pl.pallas_call(
        body, name="glu_bwd", grid=(D_FF // tn, nb), in_specs=[cur, prev, nxt, cur, cur, w_spec, b_spec],
        out_specs=[cur, cur, w_spec, b_spec],
        out_shape=[jax.ShapeDtypeStruct((S, D_FF), BF16), jax.ShapeDtypeStruct((S, D_FF), BF16),
                   jax.ShapeDtypeStruct((3, D_FF), F32), jax.ShapeDtypeStruct((1, D_FF), F32)],
    )(gp, gp, gp, up, dact, cw, cb)


def _conv_bwd_input(dgate, cw, tm=512, tn=1408):
    S = dgate.shape[0]
    nb = S // tm

    def body(c_ref, p_ref, n_ref, w_ref, o_ref):
        x = c_ref[...].astype(F32)
        x_m1, x_p1 = _shifted(x, p_ref, n_ref, nb)
        w = w_ref[...]
        o_ref[...] = (w[0:1, :] * x_p1 + w[1:2, :] * x + w[2:3, :] * x_m1).astype(BF16)

    cur, prev, nxt = _halo_specs(tm, tn, S)
    return pl.pallas_call(
        body, name="conv_bwd_input", grid=(D_FF // tn, nb),
        in_specs=[cur, prev, nxt, pl.BlockSpec((3, tn), lambda j, i: (0, j))], out_specs=cur,
        out_shape=jax.ShapeDtypeStruct((S, D_FF), BF16),
    )(dgate, dgate, dgate, cw)


def _local_step(x, target, norm1_g, w_in_t, wg, gate_bias, gla_norm_g, attn_norm_g, w_out, norm2_g,
                w_gate4, w_up4, conv_w, conv_b, w_down, final_norm_g, on_grad=lambda event, arrays: ()):
    S = x.shape[0]
    tabs = _rope_tables(S)

    n1 = _rms_fwd("rms1_fwd", x, norm1_g)
    z_block = IN_MAIN // Z_W
    proj = _mm_nt("in_proj", n1, w_in_t, 1024, 1536, BF16, n_out=IN_MAIN)
    z = _matmul(
        "in_proj_z",
        [(n1, pl.BlockSpec((1024, D_MODEL), lambda i: (i, 0)), w_in_t, pl.BlockSpec((Z_W, D_MODEL), lambda i: (z_block, 0)), NT)],
        (S // 1024,), jax.ShapeDtypeStruct((S, Z_W), BF16), pl.BlockSpec((1024, Z_W), lambda i: (i, 0)), 1)
    qk = _rope_fwd(proj, tabs)
    branch = [_attn_fwd(qk, proj, d) for d in DILATIONS]
    ao, o_attn, lse = _attn_combine([b[0] for b in branch], [b[1] for b in branch], attn_norm_g)
    gates = _gates_fwd(z, wg, gate_bias)
    o_f, st_f = _gla_fwd(proj, gates, False)
    o_gla, st_b = _gla_fwd(proj, gates, True, o_prev=o_f)
    go = _gla_out_fwd(o_gla, proj, gla_norm_g)
    cat = jnp.concatenate([ao, go], axis=1)
    h1 = _mm_nn("out_proj", cat, w_out, 1024, 1024, F32, res=x)
    n2 = _rms_fwd("rms2_fwd", h1, norm2_g)
    gp = _mm_nn_sharded("ffn_gate", n2, w_gate4, 1024, BF16)
    up = _mm_nn_sharded("ffn_up", n2, w_up4, 1024, BF16)
    act = _glu_fwd(gp, up, conv_w, conv_b)
    tk = D_FF // N_CHIPS
    h2 = _matmul(
        "ffn_down",
        [(act, pl.BlockSpec((1024, tk), lambda i, j, k: (i, k)), w_down, pl.BlockSpec((tk, 1024), lambda i, j, k: (k, j)), NN)],
        (S // 1024, D_MODEL // 1024, N_CHIPS), jax.ShapeDtypeStruct((S, D_MODEL), F32),
        pl.BlockSpec((1024, 1024), lambda i, j, k: (i, j)), N_CHIPS,
        res=(h1, pl.BlockSpec((1024, 1024), lambda i, j, k: (i, j))))
    loss_row, d_final_g, dh2, dh2_b = _final_loss(h2, final_norm_g.reshape(1, D_MODEL), target)

    dact = _mm_nt("ffn_down_bwd", dh2_b, w_down, 1024, tk, BF16)
    dup, dgate, d_conv_w, d_conv_b = _glu_bwd(gp, up, dact, conv_w, conv_b)
    dgp = _conv_bwd_input(dgate, conv_w)
    d_w_down = _mm_tn("ffn_down_wgrad", act, dh2_b, tk, D_MODEL, 1024, BF16)
    on_grad("w_down", dict(w_down=d_w_down))
    dgp = _after(dgp, d_w_down)
    d_w_gate4 = _mm_tn("ffn_gate_wgrad", n2, dgp, D_MODEL, tk, 1024, BF16, out3=tk)
    dup = _after(dup, d_w_gate4)
    d_w_up4 = _mm_tn("ffn_up_wgrad", n2, dup, D_MODEL, tk, 1024, BF16, out3=tk)
    held = on_grad("w_gate_w_up", dict(w_gate=d_w_gate4, w_up=d_w_up4))
    dgp = _after(dgp, d_w_up4, *held)
    dn2 = _matmul(
        "ffn_in_bwd",
        [(dgp, pl.BlockSpec((1024, tk), lambda i, j, k: (i, k)), w_gate4, pl.BlockSpec((None, 1024, tk), lambda i, j, k: (k, j, 0)), NT),
         (dup, pl.BlockSpec((1024, tk), lambda i, j, k: (i, k)), w_up4, pl.BlockSpec((None, 1024, tk), lambda i, j, k: (k, j, 0)), NT)],
        (S // 1024, D_MODEL // 1024, N_CHIPS), jax.ShapeDtypeStruct((S, D_MODEL), F32),
        pl.BlockSpec((1024, 1024), lambda i, j, k: (i, j)), N_CHIPS)
    dh1, dh1_b, d_norm2_g = _rms_bwd("rms2_bwd", h1, norm2_g, dn2, dh2)

    d_w_out = _mm_tn("out_proj_wgrad", cat, dh1_b, D_MODEL, 1024, 1024, BF16)
    held = on_grad("w_out", dict(w_out=d_w_out))
    dcat = _mm_nt("out_proj_bwd", _after(dh1_b, d_w_out, *held), w_out, 1024, 1024, BF16)
    do_attn, delta, d_attn_norm_g = _attn_norm_bwd(o_attn, attn_norm_g, dcat)
    dqs, dks, dvs = [], [], []
    for d in DILATIONS:
        dqs.append(_attn_bwd_dq(qk, proj, do_attn, lse, delta, d))
        dk, dv = _attn_bwd_dkv(qk, proj, do_attn, lse, delta, d)
        dks.append(dk)
        dvs.append(dv)
    d_attn = _attn_grad_merge(dqs, dks, dvs, tabs)
    held = on_grad("mid", dict(anchor=d_attn))
    do_gla, dgr, d_gla_norm_g = _gla_out_bwd(o_gla, proj, gla_norm_g, _after(dcat, *held))
    dq_f, dk_f, dv_f, dg_f = _gla_bwd(proj, gates, st_f, do_gla, False)
    dgq, dgk, dgv, dg_b = _gla_bwd(proj, gates, st_b, do_gla, True, prev=(dq_f, dk_f, dv_f))
    dz, d_wg, d_gate_bias = _gates_bwd(z, wg, gate_bias, dg_f, dg_b)
    dproj = jnp.concatenate([d_attn, dgq, dgk, dgv, dgr], axis=1)
    d_w_in_t = _mm_tn("in_proj_wgrad", dproj, n1, 1536, D_MODEL, 1024, BF16, rows_out=IN_W)
    n_tok = S // 1024
    d_w_in_t = _matmul(
        "in_proj_z_wgrad",
        [(dz, pl.BlockSpec((1024, Z_W), lambda i, j, k: (k, 0)), n1, pl.BlockSpec((1024, D_MODEL), lambda i, j, k: (k, 0)), TN)],
        (1, 1, n_tok), jax.ShapeDtypeStruct((IN_W, D_MODEL), BF16), pl.BlockSpec((Z_W, D_MODEL), lambda i, j, k: (z_block, 0)),
        n_tok, into=d_w_in_t)
    held = on_grad("w_in", dict(w_in_t=d_w_in_t))
    tkm = IN_MAIN // 4
    dn1 = _matmul(
        "in_proj_bwd",
        [(_after(dproj, d_w_in_t, *held), pl.BlockSpec((1024, tkm), lambda i, j, k: (i, k)), w_in_t, pl.BlockSpec((tkm, 1024), lambda i, j, k: (k, j)), NN)],
        (S // 1024, D_MODEL // 1024, 4), jax.ShapeDtypeStruct((S, D_MODEL), F32),
        pl.BlockSpec((1024, 1024), lambda i, j, k: (i, j)), 4)
    held = on_grad("last", dict(last=dn1))
    dn1 = _matmul(
        "in_proj_z_bwd",
        [(dz, pl.BlockSpec((1024, Z_W), lambda j, i: (i, 0)), w_in_t, pl.BlockSpec((Z_W, 1024), lambda j, i: (z_block, j)), NN)],
        (D_MODEL // 1024, S // 1024), jax.ShapeDtypeStruct((S, D_MODEL), F32), pl.BlockSpec((1024, 1024), lambda j, i: (i, j)), 1,
        res=(_after(dn1, *held), pl.BlockSpec((1024, 1024), lambda j, i: (i, j))))
    grad_x, _, d_norm1_g = _rms_bwd("rms1_bwd", x, norm1_g, dn1, dh1)

    big = dict(w_in_t=d_w_in_t, w_out=d_w_out, w_gate4=d_w_gate4, w_up4=d_w_up4, w_down=d_w_down)
    small = dict(loss=loss_row, norm1_g=d_norm1_g, wg=d_wg, gate_bias=d_gate_bias, gla_norm_g=d_gla_norm_g,
                 attn_norm_g=d_attn_norm_g, norm2_g=d_norm2_g, conv_w=d_conv_w, conv_b=d_conv_b, final_norm_g=d_final_g)
    return grad_x, big, small


def _position():
    return lax.axis_index("x"), lax.axis_index("y"), lax.axis_index("c")


def _other_chips(x, y):
    return [(1 - x, y), (x, 1 - y), (1 - x, 1 - y)]


def _gather_chips(name, shards):
    n = len(shards)

    def body(*refs):
        ins, outs = refs[:n], refs[n:2 * n]
        send, recv, loc = refs[2 * n:]
        x, y, c = _position()
        me = 2 * x + y
        chips = _other_chips(x, y)
        started = []
        for w in range(n):
            own = pltpu.make_async_copy(ins[w], outs[w].at[me], loc.at[w])
            own.start()
            started.append(own)
        sends = []
        for w in range(n):
            for j, (px, py) in enumerate(chips):
                cp = pltpu.make_async_remote_copy(ins[w], outs[w].at[me], send.at[3 * w + j], recv.at[3 * w + j],
                                                  device_id=(px, py, c), device_id_type=MESH)
                cp.start()
                sends.append(cp)
        for w in range(n):
            for j, (px, py) in enumerate(chips):
                pltpu.make_async_remote_copy(ins[w], outs[w].at[2 * px + py], send.at[3 * w + j], recv.at[3 * w + j],
                                             device_id=(px, py, c), device_id_type=MESH).wait_recv()
        for cp in sends:
            cp.wait_send()
        for own in started:
            own.wait()

    return pl.pallas_call(
        body, name=name, in_specs=[ANY] * n, out_specs=[ANY] * n,
        out_shape=[jax.ShapeDtypeStruct((N_CHIPS,) + s.shape, s.dtype) for s in shards],
        scratch_shapes=[pltpu.SemaphoreType.DMA((3 * n,)), pltpu.SemaphoreType.DMA((3 * n,)), pltpu.SemaphoreType.DMA((n,))],
    )(*shards)


def _gather_chips_async(name, shards, collective_id):
    n = len(shards)

    def body(*refs):
        ins, outs = refs[:n], refs[n:2 * n]
        send, recv, loc = refs[2 * n:]
        x, y, c = _position()
        me = 2 * x + y
        chips = _other_chips(x, y)
        barrier = pltpu.get_barrier_semaphore()
        for px, py in chips:
            pl.semaphore_signal(barrier, inc=1, device_id=(px, py, c), device_id_type=MESH)
        pl.semaphore_wait(barrier, len(chips))
        started = []
        for w in range(n):
            own = pltpu.make_async_copy(ins[w], outs[w].at[me], loc.at[w])
            own.start()
            started.append(own)
        sends = []
        for w in range(n):
            for j, (px, py) in enumerate(chips):
                cp = pltpu.make_async_remote_copy(ins[w], outs[w].at[me], send.at[3 * w + j], recv.at[3 * w + j],
                                                  device_id=(px, py, c), device_id_type=MESH)
                cp.start()
                sends.append(cp)
        for w in range(n):
            for j, (px, py) in enumerate(chips):
                pltpu.make_async_remote_copy(ins[w], outs[w].at[2 * px + py], send.at[3 * w + j], recv.at[3 * w + j],
                                             device_id=(px, py, c), device_id_type=MESH).wait_recv()
        for cp in sends:
            cp.wait_send()
        for own in started:
            own.wait()

    return pl.kernel(
        body, name=name, mesh=_sequencer(),
        out_type=[jax.ShapeDtypeStruct((N_CHIPS,) + s.shape, s.dtype) for s in shards],
        scratch_types=[pltpu.SemaphoreType.DMA((3 * n,)), pltpu.SemaphoreType.DMA((3 * n,)), pltpu.SemaphoreType.DMA((n,))],
        compiler_params=pltpu.CompilerParams(collective_id=collective_id),
    )(*shards)


def _sibling_exchange(name, arrs):
    n = len(arrs)

    def body(*refs):
        ins, outs = refs[:n], refs[n:2 * n]
        send, recv = refs[2 * n:]
        x, y, c = _position()
        copies = [pltpu.make_async_remote_copy(ins[w], outs[w], send.at[w], recv.at[w], device_id=(x, y, 1 - c),
                                               device_id_type=MESH) for w in range(n)]
        for cp in copies:
            cp.start()
        for cp in copies:
            cp.wait()

    return pl.pallas_call(
        body, name=name, in_specs=[ANY] * n, out_specs=[ANY] * n,
        out_shape=[jax.ShapeDtypeStruct(a.shape, a.dtype) for a in arrs],
        scratch_shapes=[pltpu.SemaphoreType.DMA((n,)), pltpu.SemaphoreType.DMA((n,))],
    )(*arrs)


def _scatter_chips(name, parts):
    n = len(parts)

    def body(*refs):
        ins, outs = refs[:n], refs[n:2 * n]
        send, recv, loc = refs[2 * n:]
        x, y, c = _position()
        me = 2 * x + y
        chips = _other_chips(x, y)
        started = []
        for w in range(n):
            own = pltpu.make_async_copy(ins[w].at[me], outs[w].at[me], loc.at[w])
            own.start()
            started.append(own)
        sends = []
        for w in range(n):
            for j, (px, py) in enumerate(chips):
                cp = pltpu.make_async_remote_copy(ins[w].at[2 * px + py], outs[w].at[me], send.at[3 * w + j],
                                                  recv.at[3 * w + j], device_id=(px, py, c), device_id_type=MESH)
                cp.start()
                sends.append(cp)
        for w in range(n):
            for j, (px, py) in enumerate(chips):
                pltpu.make_async_remote_copy(ins[w].at[me], outs[w].at[2 * px + py], send.at[3 * w + j], recv.at[3 * w + j],
                                             device_id=(px, py, c), device_id_type=MESH).wait_recv()
        for cp in sends:
            cp.wait_send()
        for own in started:
            own.wait()

    return pl.pallas_call(
        body, name=name, in_specs=[ANY] * n, out_specs=[ANY] * n,
        out_shape=[jax.ShapeDtypeStruct(p.shape, p.dtype) for p in parts],
        scratch_shapes=[pltpu.SemaphoreType.DMA((3 * n,)), pltpu.SemaphoreType.DMA((3 * n,)), pltpu.SemaphoreType.DMA((n,))],
    )(*parts)


def _sequencer():
    return plsc.ScalarSubcoreMesh(axis_name="sequencer", num_cores=1)


def _sibling_exchange_async(name, arrs, collective_id):
    n = len(arrs)

    def body(*refs):
        ins, outs = refs[:n], refs[n:2 * n]
        send, recv = refs[2 * n:]
        x, y, c = _position()
        sibling = (x, y, 1 - c)
        barrier = pltpu.get_barrier_semaphore()
        pl.semaphore_signal(barrier, inc=1, device_id=sibling, device_id_type=MESH)
        pl.semaphore_wait(barrier, 1)
        copies = [pltpu.make_async_remote_copy(ins[w], outs[w], send.at[w], recv.at[w], device_id=sibling,
                                               device_id_type=MESH) for w in range(n)]
        for cp in copies:
            cp.start()
        for cp in copies:
            cp.wait()

    return pl.kernel(
        body, name=name, out_type=[jax.ShapeDtypeStruct(a.shape, a.dtype) for a in arrs],
        scratch_types=[pltpu.SemaphoreType.DMA((n,)), pltpu.SemaphoreType.DMA((n,))],
        compiler_params=pltpu.CompilerParams(collective_id=collective_id), mesh=_sequencer(),
    )(*arrs)


def _scatter_chips_async(name, parts, collective_id):
    n = len(parts)

    def body(*refs):
        ins, outs = refs[:n], refs[n:2 * n]
        send, recv, loc = refs[2 * n:]
        x, y, c = _position()
        me = 2 * x + y
        chips = _other_chips(x, y)
        barrier = pltpu.get_barrier_semaphore()
        for px, py in chips:
            pl.semaphore_signal(barrier, inc=1, device_id=(px, py, c), device_id_type=MESH)
        pl.semaphore_wait(barrier, len(chips))
        started = []
        for w in range(n):
            own = pltpu.make_async_copy(ins[w].at[me], outs[w].at[me], loc.at[w])
            own.start()
            started.append(own)
        sends = []
        for w in range(n):
            for j, (px, py) in enumerate(chips):
                cp = pltpu.make_async_remote_copy(ins[w].at[2 * px + py], outs[w].at[me], send.at[3 * w + j],
                                                  recv.at[3 * w + j], device_id=(px, py, c), device_id_type=MESH)
                cp.start()
                sends.append(cp)
        for w in range(n):
            for j, (px, py) in enumerate(chips):
                pltpu.make_async_remote_copy(ins[w].at[me], outs[w].at[2 * px + py], send.at[3 * w + j], recv.at[3 * w + j],
                                             device_id=(px, py, c), device_id_type=MESH).wait_recv()
        for cp in sends:
            cp.wait_send()
        for own in started:
            own.wait()

    return pl.kernel(
        body, name=name, out_type=[jax.ShapeDtypeStruct(p.shape, p.dtype) for p in parts],
        scratch_types=[pltpu.SemaphoreType.DMA((3 * n,)), pltpu.SemaphoreType.DMA((3 * n,)), pltpu.SemaphoreType.DMA((n,))],
        compiler_params=pltpu.CompilerParams(collective_id=collective_id), mesh=_sequencer(),
    )(*parts)


def _allreduce_rows(buf):
    R = buf.shape[0]

    def body(in_ref, out_ref, land, send, recv):
        x, y, c = _position()
        me = 4 * x + 2 * y + c
        land[pl.ds(me, 1)] = in_ref[...][None]
        peers = []
        for mask in range(1, N_DEV):
            px = 1 - x if mask & 4 else x
            py = 1 - y if mask & 2 else y
            pc = 1 - c if mask & 1 else c
            peers.append((px, py, pc))
        sends = []
        for k, peer in enumerate(peers):
            cp = pltpu.make_async_remote_copy(in_ref, land.at[me], send.at[k], recv.at[k], device_id=peer, device_id_type=MESH)
            cp.start()
            sends.append(cp)
        for k, (px, py, pc) in enumerate(peers):
            pltpu.make_async_remote_copy(in_ref, land.at[4 * px + 2 * py + pc], send.at[k], recv.at[k],
                                         device_id=(px, py, pc), device_id_type=MESH).wait_recv()
        for cp in sends:
            cp.wait_send()
        tot = land[0]
        for i in range(1, N_DEV):
            tot = tot + land[i]
        out_ref[...] = tot

    vm = pl.BlockSpec(memory_space=pltpu.VMEM)
    return pl.pallas_call(
        body, name="allreduce_small", in_specs=[vm], out_specs=vm, out_shape=jax.ShapeDtypeStruct((R, LANES), F32),
        scratch_shapes=[pltpu.VMEM((N_DEV, R, LANES), F32), pltpu.SemaphoreType.DMA((N_DEV - 1,)),
                        pltpu.SemaphoreType.DMA((N_DEV - 1,))],
    )(buf)


def _tile2d(r, c):
    for tr in (512, 256, 128):
        if r % tr == 0:
            return tr, c
    return (r, c) if r <= 256 else (r, 256)


def _pair_sum(name, a, b):
    n, r, c = a.shape
    tr, tc = _tile2d(r, c)

    def body(a_ref, b_ref, o_ref):
        o_ref[...] = (a_ref[...].astype(F32) + b_ref[...].astype(F32)).astype(BF16)

    blk = pl.BlockSpec((None, tr, tc), lambda s, i, j: (s, i, j))
    return pl.pallas_call(
        body, name=name, grid=(n, r // tr, c // tc), in_specs=[blk, blk], out_specs=blk,
        out_shape=jax.ShapeDtypeStruct(a.shape, BF16),
    )(a, b)


def _adamw_math(w, m, v, g):
    m2 = ADAM_B1 * m + (1.0 - ADAM_B1) * g
    v2 = ADAM_B2 * v + (1.0 - ADAM_B2) * (g * g)
    m_hat = m2 / (1.0 - ADAM_B1 ** ADAM_STEP)
    v_hat = v2 / (1.0 - ADAM_B2 ** ADAM_STEP)
    delta = -ADAM_LR * (m_hat / (jnp.sqrt(v_hat) + ADAM_EPS) + ADAM_WD * w)
    return delta, m2, v2


def _adamw(name, w, m, v, g):
    r, c = w.shape
    stacked = g.ndim == 3
    tr, tc = _tile2d(r, c)
    if tc == c and r % 256 == 0:
        tr = 256

    def body(w_ref, m_ref, v_ref, g_ref, go_ref, d_ref, m2_ref, v2_ref):
        if stacked:
            gv = g_ref[0].astype(F32)
            for i in range(1, N_CHIPS):
                gv = gv + g_ref[i].astype(F32)
        else:
            gv = g_ref[...]
        delta, m2, v2 = _adamw_math(w_ref[...], m_ref[...], v_ref[...], gv)
        go_ref[...] = gv
        d_ref[...] = delta
        m2_ref[...] = m2
        v2_ref[...] = v2

    blk = pl.BlockSpec((tr, tc), lambda i, j: (i, j))
    g_spec = pl.BlockSpec((N_CHIPS, tr, tc), lambda i, j: (0, i, j)) if stacked else blk
    out = jax.ShapeDtypeStruct((r, c), F32)
    return pl.pallas_call(
        body, name=name, grid=(r // tr, c // tc), in_specs=[blk, blk, blk, g_spec], out_specs=[blk] * 4, out_shape=[out] * 4,
    )(w, m, v, g)


def _pack_rows(pieces):
    flat = jnp.concatenate([p.reshape(-1) for p in pieces])
    rows = flat.shape[0] // LANES
    pad = (-rows) % 8
    return jnp.pad(flat.reshape(rows, LANES), ((0, pad), (0, 0)))


def _unpack_rows(buf, shapes):
    flat = buf.reshape(-1)
    out, at = [], 0
    for s in shapes:
        size = math.prod(s)
        out.append(flat[at:at + size].reshape(s))
        at += size
    return out


SMALL_NAMES = ("norm1_g", "gf_up", "gf_b", "gb_up", "gb_b", "gla_norm_g", "attn_norm_g", "norm2_g", "conv_w", "conv_b",
               "final_norm_g")
BIG_NAMES = ("w_in", "w_out", "w_gate", "w_up", "w_down")
WEIGHT_ORDER = ("norm1_g", "w_in", "gf_up", "gf_b", "gb_up", "gb_b", "gla_norm_g", "attn_norm_g", "w_out", "norm2_g",
                "w_gate", "w_up", "conv_w", "conv_b", "w_down", "final_norm_g")


def kernel(x, norm1_g, w_in, gf_up, gf_b, gb_up, gb_b, gla_norm_g, attn_norm_g, w_out, norm2_g, w_gate, w_up, conv_w, conv_b, w_down, final_norm_g, loss_target, m_norm1_g, m_w_in, m_gf_up, m_gf_b, m_gb_up, m_gb_b, m_gla_norm_g, m_attn_norm_g, m_w_out, m_norm2_g, m_w_gate, m_w_up, m_conv_w, m_conv_b, m_w_down, m_final_norm_g, v_norm1_g, v_w_in, v_gf_up, v_gf_b, v_gb_up, v_gb_b, v_gla_norm_g, v_attn_norm_g, v_w_out, v_norm2_g, v_w_gate, v_w_up, v_conv_w, v_conv_b, v_w_down, v_final_norm_g):
    w = dict(norm1_g=norm1_g, w_in=w_in, gf_up=gf_up, gf_b=gf_b, gb_up=gb_up, gb_b=gb_b, gla_norm_g=gla_norm_g,
             attn_norm_g=attn_norm_g, w_out=w_out, norm2_g=norm2_g, w_gate=w_gate, w_up=w_up, conv_w=conv_w, conv_b=conv_b,
             w_down=w_down, final_norm_g=final_norm_g)
    m = dict(norm1_g=m_norm1_g, w_in=m_w_in, gf_up=m_gf_up, gf_b=m_gf_b, gb_up=m_gb_up, gb_b=m_gb_b, gla_norm_g=m_gla_norm_g,
             attn_norm_g=m_attn_norm_g, w_out=m_w_out, norm2_g=m_norm2_g, w_gate=m_w_gate, w_up=m_w_up, conv_w=m_conv_w,
             conv_b=m_conv_b, w_down=m_w_down, final_norm_g=m_final_norm_g)
    v = dict(norm1_g=v_norm1_g, w_in=v_w_in, gf_up=v_gf_up, gf_b=v_gf_b, gb_up=v_gb_up, gb_b=v_gb_b, gla_norm_g=v_gla_norm_g,
             attn_norm_g=v_attn_norm_g, w_out=v_w_out, norm2_g=v_norm2_g, w_gate=v_w_gate, w_up=v_w_up, conv_w=v_conv_w,
             conv_b=v_conv_b, w_down=v_w_down, final_norm_g=v_final_norm_g)
    S = x.shape[1]
    chip = 2 * lax.axis_index("x") + lax.axis_index("y")
    n_in = IN_W // N_CHIPS
    n_ff = D_FF // N_CHIPS
    n_gk = GLA_K // N_CHIPS

    def owned(t):
        return {k: (jnp.transpose(t[k][0]) if k == "w_in" else t[k][0]) for k in BIG_NAMES}

    own_w, own_m, own_v = owned(w), owned(m), owned(v)
    shard = {k: own_w[k].astype(BF16) for k in BIG_NAMES}
    small_shard = _pack_rows([gf_up[0], gb_up[0], conv_w[0]])
    small4, w_in4 = _gather_chips_async("gather_w_in", [small_shard, shard["w_in"]], 0)
    w_out4, w_gate4, w_up4 = _gather_chips_async("gather_w_mid", [shard["w_out"], shard["w_gate"], shard["w_up"]], 1)
    (w_down4,) = _gather_chips_async("gather_w_down", [shard["w_down"]], 2)
    w_in_t = w_in4.reshape(IN_W, D_MODEL)
    rows_up = GATE_RANK * n_gk // LANES
    rows_cw = 3 * n_ff // LANES
    gf_full = jnp.transpose(small4[:, 0:rows_up].reshape(N_CHIPS, GATE_RANK, n_gk), (1, 0, 2)).reshape(GATE_RANK, GLA_K)
    gb_full = jnp.transpose(small4[:, rows_up:2 * rows_up].reshape(N_CHIPS, GATE_RANK, n_gk), (1, 0, 2)).reshape(GATE_RANK, GLA_K)
    cw_full = jnp.transpose(small4[:, 2 * rows_up:2 * rows_up + rows_cw].reshape(N_CHIPS, 3, n_ff), (1, 0, 2)).reshape(3, D_FF)
    wg = jnp.zeros((Z_W, 2 * GLA_K), F32)
    wg = wg.at[0:GATE_RANK, 0:GLA_K].set(gf_full).at[GATE_RANK:2 * GATE_RANK, GLA_K:].set(gb_full).astype(BF16)
    gate_bias = jnp.concatenate([gf_b, gb_b], axis=1)

    pending, contributions, next_id = [], {}, [3]

    def as_shards(group, arrays):
        if group == "w_in":
            return dict(w_in=arrays["w_in_t"].reshape(N_CHIPS, n_in, D_MODEL))
        if group == "w_out":
            return dict(w_out=arrays["w_out"].reshape(N_CHIPS, D_MODEL // N_CHIPS, D_MODEL))
        if group == "w_down":
            return dict(w_down=arrays["w_down"].reshape(N_CHIPS, n_ff, D_MODEL))
        return arrays

    out = {}

    def swap(group, arrays):
        mine = as_shards(group, arrays)
        pending.append((group, mine, _sibling_exchange_async(f"sibling_{group}", list(mine.values()), next_id[0])))
        next_id[0] += 1

    def sum_and_send(anchor):
        tag, mine, theirs = pending.pop()
        sums = [_pair_sum(f"pair_sum_{k}", mine[k], _after(t, *anchor)) for k, t in zip(mine, theirs)]
        contributions.update(zip(mine, _scatter_chips_async(f"scatter_{tag}", sums, next_id[0])))
        next_id[0] += 1
        return sums

    def update(names, anchor):
        for k in names:
            res = _adamw(f"adamw_{k}", own_w[k], own_m[k], own_v[k], _after(contributions[k], *anchor))
            out[k] = [(jnp.transpose(r) if k == "w_in" else r)[None] for r in res]
        return [out[k][0] for k in names]

    def on_grad(event, arrays):
        anchor = list(arrays.values())
        held = []
        if event in ("w_gate_w_up", "w_out", "mid", "last"):
            held += sum_and_send(anchor)
        if event == "mid":
            held += update(("w_down", "w_gate", "w_up"), anchor)
        if event == "last":
            held += update(("w_out",), anchor)
        if event in ("w_down", "w_gate_w_up", "w_out", "w_in"):
            swap(event, arrays)
        return held

    grad_x, _, small = _local_step(
        x[0], loss_target[0], norm1_g, w_in_t, wg, gate_bias, gla_norm_g, attn_norm_g,
        w_out4.reshape(D_MODEL, D_MODEL), norm2_g, w_gate4, w_up4, cw_full, conv_b, w_down4.reshape(D_FF, D_MODEL), final_norm_g,
        on_grad=on_grad)
    update(("w_in",), [grad_x])

    d_gf_up = small["wg"][0:GATE_RANK, 0:GLA_K]
    d_gb_up = small["wg"][GATE_RANK:2 * GATE_RANK, GLA_K:]
    pieces = [small["loss"], small["norm1_g"], d_gf_up, small["gate_bias"][:, :GLA_K], d_gb_up, small["gate_bias"][:, GLA_K:],
              small["gla_norm_g"], small["attn_norm_g"], small["norm2_g"], small["conv_w"], small["conv_b"], small["final_norm_g"]]
    total = _allreduce_rows(_pack_rows(pieces))
    summed = _unpack_rows(total, [p.shape for p in pieces])
    loss = summed[0][0, 0]
    g_small = dict(zip(SMALL_NAMES, summed[1:]))
    g_small["gf_up"] = lax.dynamic_slice_in_dim(g_small["gf_up"], chip * n_gk, n_gk, axis=1)
    g_small["gb_up"] = lax.dynamic_slice_in_dim(g_small["gb_up"], chip * n_gk, n_gk, axis=1)
    g_small["conv_w"] = lax.dynamic_slice_in_dim(g_small["conv_w"], chip * n_ff, n_ff, axis=1)
    packed = [_pack_rows([t[k] for k in SMALL_NAMES]) for t in (w, m, v, g_small)]
    res = _adamw("adamw_small", *packed)
    shapes = [w[k].shape for k in SMALL_NAMES]
    for k, vals in zip(SMALL_NAMES, zip(*[_unpack_rows(r, shapes) for r in res])):
        out[k] = list(vals)

    grads, deltas, new_m, new_v = ([out[k][i] for k in WEIGHT_ORDER] for i in range(4))
    return (loss, grad_x[None], *grads, *deltas, *new_m, *new_v)
```

```python
import functools
import math

import jax
import jax.numpy as jnp
from jax import lax
from jax.experimental import pallas as pl
from jax.experimental.pallas import tpu as pltpu
from jax.experimental.pallas import tpu_sc as plsc

F32 = jnp.float32
BF16 = jnp.bfloat16

D_MODEL = 2048
ATTN_W = 1024
HEAD = 128
N_HEADS = 8
N_SIDE = 64
DILATIONS = (1, 4, 16)
ROPE_THETA = 500000.0
ROPE_DIM = 32
GLA_K = 512
GLA_V = 1024
GLA_HEADS = 4
GLA_DK = 128
GLA_DV = 256
GATE_RANK = 16
GATE_NORM = 16.0
CHUNK = 64
IN_MAIN = 6144
IN_W = 6176
Z_W = IN_W - IN_MAIN
D_FF = 5632
EPS = 1e-6
N_CHIPS = 4
N_DEV = 8
LANES = 128

ADAM_LR = 0.001
ADAM_B1 = 0.9
ADAM_B2 = 0.999
ADAM_EPS = 1e-08
ADAM_WD = 0.01
ADAM_STEP = 10

NEG = -1e30
MESH = pl.DeviceIdType.MESH
ANY = pl.BlockSpec(memory_space=pl.ANY)

NN = ((1,), (0,))
NT = ((1,), (1,))
TN = ((0,), (0,))


def _dot(a, b, dims=NN):
    return lax.dot_general(a, b, (dims, ((), ())), preferred_element_type=F32)


def _sigmoid(x):
    return 0.5 * jnp.tanh(0.5 * x) + 0.5


def _after(x, *deps):
    return lax.optimization_barrier((x,) + deps)[0]


def _matmul(name, pairs, grid, out_shape, out_spec, nk, res=None, into=None):
    n_in = 2 * len(pairs) + (res is not None)
    dims = [p[4] for p in pairs]

    n_ops = n_in + (into is not None)

    def body(*refs):
        ins, o_ref = refs[:n_in], refs[n_ops]

        def partial_sum():
            tot = None
            for p, dn in enumerate(dims):
                a, b = ins[2 * p][...], ins[2 * p + 1][...]
                t = _dot(a.astype(BF16), b.astype(BF16), dn)
                tot = t if tot is None else tot + t
            return tot

        if nk == 1:
            t = partial_sum()
            if res is not None:
                t = t + ins[-1][...]
            o_ref[...] = t.astype(o_ref.dtype)
        else:
            acc_ref = refs[n_ops + 1]
            k = pl.program_id(2)

            @pl.when(k == 0)
            def _():
                if res is not None:
                    acc_ref[...] = ins[-1][...]
                else:
                    acc_ref[...] = jnp.zeros_like(acc_ref)

            acc_ref[...] += partial_sum()

            @pl.when(k == nk - 1)
            def _():
                o_ref[...] = acc_ref[...].astype(o_ref.dtype)

    operands, in_specs = [], []
    for a, a_spec, b, b_spec, _ in pairs:
        operands += [a, b]
        in_specs += [a_spec, b_spec]
    if res is not None:
        operands.append(res[0])
        in_specs.append(res[1])
    acc_shape = tuple(s for s in out_spec.block_shape if s is not None)
    scratch = [pltpu.VMEM(acc_shape, F32)] if nk > 1 else []
    aliases = {}
    if into is not None:
        aliases = {len(operands): 0}
        operands.append(into)
        in_specs.append(ANY)
    return pl.pallas_call(
        body, name=name, grid=grid, in_specs=in_specs, out_specs=out_spec, out_shape=out_shape, scratch_shapes=scratch,
        input_output_aliases=aliases,
    )(*operands)


def _mm_nn(name, a, b, tm, tn, out_dtype, res=None):
    M, K = a.shape
    N = b.shape[1]
    pairs = [(a, pl.BlockSpec((tm, K), lambda j, i: (i, 0)), b, pl.BlockSpec((K, tn), lambda j, i: (0, j)), NN)]
    r = None if res is None else (res, pl.BlockSpec((tm, tn), lambda j, i: (i, j)))
    return _matmul(name, pairs, (N // tn, M // tm), jax.ShapeDtypeStruct((M, N), out_dtype),
                   pl.BlockSpec((tm, tn), lambda j, i: (i, j)), 1, r)


def _mm_nn_sharded(name, a, b4, tm, out_dtype):
    M, K = a.shape
    n = b4.shape[2]
    pairs = [(a, pl.BlockSpec((tm, K), lambda j, i: (i, 0)), b4, pl.BlockSpec((None, K, n), lambda j, i: (j, 0, 0)), NN)]
    return _matmul(name, pairs, (N_CHIPS, M // tm), jax.ShapeDtypeStruct((M, N_CHIPS * n), out_dtype),
                   pl.BlockSpec((tm, n), lambda j, i: (i, j)), 1)


def _mm_nt(name, a, b, tm, tn, out_dtype, res=None, n_out=None):
    M, K = a.shape
    N = b.shape[0] if n_out is None else n_out
    pairs = [(a, pl.BlockSpec((tm, K), lambda j, i: (i, 0)), b, pl.BlockSpec((tn, K), lambda j, i: (j, 0)), NT)]
    r = None if res is None else (res, pl.BlockSpec((tm, tn), lambda j, i: (i, j)))
    return _matmul(name, pairs, (N // tn, M // tm), jax.ShapeDtypeStruct((M, N), out_dtype),
                   pl.BlockSpec((tm, tn), lambda j, i: (i, j)), 1, r)


def _mm_tn(name, a, g, tka, tn, tmm, out_dtype, out3=None, rows_out=None):
    M, Ka = a.shape
    N = g.shape[1]
    pairs = [(a, pl.BlockSpec((tmm, tka), lambda i, j, k: (k, i)), g, pl.BlockSpec((tmm, tn), lambda i, j, k: (k, j)), TN)]
    if out3 is None:
        shape, spec = (Ka if rows_out is None else rows_out, N), pl.BlockSpec((tka, tn), lambda i, j, k: (i, j))
    else:
        shape, spec = (N // out3, Ka, out3), pl.BlockSpec((None, tka, tn), lambda i, j, k: (j, i, 0))
    return _matmul(name, pairs, (Ka // tka, N // tn, M // tmm), jax.ShapeDtypeStruct(shape, out_dtype), spec, M // tmm)


def _rms_fwd(name, x, g, tm=512):
    S, D = x.shape

    def body(x_ref, g_ref, o_ref):
        xv = x_ref[...]
        r = lax.rsqrt(jnp.mean(xv * xv, axis=-1, keepdims=True) + EPS)
        o_ref[...] = (xv * r * g_ref[...]).astype(o_ref.dtype)

    return pl.pallas_call(
        body, name=name, grid=(S // tm,),
        in_specs=[pl.BlockSpec((tm, D), lambda i: (i, 0)), pl.BlockSpec((1, D), lambda i: (0, 0))],
        out_specs=pl.BlockSpec((tm, D), lambda i: (i, 0)), out_shape=jax.ShapeDtypeStruct((S, D), BF16),
    )(x, g)


def _rms_bwd(name, x, g, dn, dres, tm=512):
    S, D = x.shape

    def body(x_ref, g_ref, dn_ref, dres_ref, dx_ref, dxb_ref, dg_ref):
        i = pl.program_id(0)

        @pl.when(i == 0)
        def _():
            dg_ref[...] = jnp.zeros_like(dg_ref)

        xv = x_ref[...]
        r = lax.rsqrt(jnp.mean(xv * xv, axis=-1, keepdims=True) + EPS)
        xhat = xv * r
        dnv = dn_ref[...].astype(F32)
        dg_ref[...] += jnp.sum(dnv * xhat, axis=0, keepdims=True)
        t = dnv * g_ref[...]
        dx = r * (t - xhat * jnp.mean(t * xhat, axis=-1, keepdims=True)) + dres_ref[...]
        dx_ref[...] = dx
        dxb_ref[...] = dx.astype(BF16)

    row = pl.BlockSpec((tm, D), lambda i: (i, 0))
    vec = pl.BlockSpec((1, D), lambda i: (0, 0))
    return pl.pallas_call(
        body, name=name, grid=(S // tm,), in_specs=[row, vec, row, row], out_specs=[row, row, vec],
        out_shape=[jax.ShapeDtypeStruct((S, D), F32), jax.ShapeDtypeStruct((S, D), BF16), jax.ShapeDtypeStruct((1, D), F32)],
    )(x, g, dn, dres)


def _final_loss(h2, g, target, tm=512):
    S, D = h2.shape

    def body(x_ref, g_ref, t_ref, loss_ref, dg_ref, dx_ref, dxb_ref):
        i = pl.program_id(0)

        @pl.when(i == 0)
        def _():
            loss_ref[...] = jnp.zeros_like(loss_ref)
            dg_ref[...] = jnp.zeros_like(dg_ref)

        xv = x_ref[...]
        r = lax.rsqrt(jnp.mean(xv * xv, axis=-1, keepdims=True) + EPS)
        xhat = xv * r
        gv = g_ref[...]
        diff = xhat * gv - t_ref[...]
        per_tok = jnp.mean(diff * diff, axis=-1, keepdims=True)
        loss_ref[...] += 0.5 * jnp.sum(per_tok, axis=0, keepdims=True)
        dy = diff * (1.0 / D)
        dg_ref[...] += jnp.sum(dy * xhat, axis=0, keepdims=True)
        t = dy * gv
        dx = r * (t - xhat * jnp.mean(t * xhat, axis=-1, keepdims=True))
        dx_ref[...] = dx
        dxb_ref[...] = dx.astype(BF16)

    row = pl.BlockSpec((tm, D), lambda i: (i, 0))
    vec = pl.BlockSpec((1, D), lambda i: (0, 0))
    return pl.pallas_call(
        body, name="final_loss", grid=(S // tm,), in_specs=[row, vec, row],
        out_specs=[pl.BlockSpec((1, LANES), lambda i: (0, 0)), vec, row, row],
        out_shape=[jax.ShapeDtypeStruct((1, LANES), F32), jax.ShapeDtypeStruct((1, D), F32),
                   jax.ShapeDtypeStruct((S, D), F32), jax.ShapeDtypeStruct((S, D), BF16)],
    )(h2, g, target)


def _rope_tables(S):
    pos = jnp.arange(S, dtype=F32)
    inv_freq = ROPE_THETA ** (-jnp.arange(0, ROPE_DIM, 2, dtype=F32) / ROPE_DIM)
    ang = pos[:, None] * inv_freq[None, :]
    cos, sin = jnp.cos(ang), jnp.sin(ang)
    half = ROPE_DIM // 2
    rest = HEAD - ROPE_DIM
    z_h, z_r = jnp.zeros((S, half), F32), jnp.zeros((S, rest), F32)
    tab_c = jnp.concatenate([cos, cos, jnp.ones((S, rest), F32)], axis=1)
    tab_up = jnp.concatenate([z_h, sin, z_r], axis=1)
    tab_dn = jnp.concatenate([-sin, z_h, z_r], axis=1)
    return tab_c, tab_up, tab_dn


def _rope_head(t, c, up, dn):
    half = ROPE_DIM // 2
    return t * c + pltpu.roll(t, half, axis=1) * up + pltpu.roll(t, HEAD - half, axis=1) * dn


def _rope_fwd(proj, tabs, tm=512):
    S = proj.shape[0]
    W = 2 * ATTN_W

    def body(p_ref, c_ref, up_ref, dn_ref, o_ref):
        c, up, dn = c_ref[...], up_ref[...], dn_ref[...]
        for h in range(W // HEAD):
            sl = slice(h * HEAD, (h + 1) * HEAD)
            o_ref[:, sl] = _rope_head(p_ref[:, sl].astype(F32), c, up, dn).astype(BF16)

    tab = pl.BlockSpec((tm, HEAD), lambda i: (i, 0))
    return pl.pallas_call(
        body, name="rope_fwd", grid=(S // tm,), in_specs=[pl.BlockSpec((tm, W), lambda i: (i, 0)), tab, tab, tab],
        out_specs=pl.BlockSpec((tm, W), lambda i: (i, 0)), out_shape=jax.ShapeDtypeStruct((S, W), BF16),
    )(proj, *tabs)


def _attn_grad_merge(dqs, dks, dvs, tabs, tm=256):
    S = dqs[0].shape[0]

    def body(*refs):
        q_refs, k_refs, v_refs = refs[0:3], refs[3:6], refs[6:9]
        c, up, dn = refs[9][...], refs[10][...], refs[11][...]
        o_ref = refs[12]
        for h in range(N_HEADS):
            sl = slice(h * HEAD, (h + 1) * HEAD)
            for part, rs in ((0, q_refs), (1, k_refs)):
                t = rs[0][:, sl].astype(F32) + rs[1][:, sl].astype(F32) + rs[2][:, sl].astype(F32)
                osl = slice(part * ATTN_W + h * HEAD, part * ATTN_W + (h + 1) * HEAD)
                o_ref[:, osl] = _rope_head(t, c, -up, -dn).astype(BF16)
        o_ref[:, 2 * ATTN_W:] = (v_refs[0][...].astype(F32) + v_refs[1][...].astype(F32)
                                 + v_refs[2][...].astype(F32)).astype(BF16)

    blk = pl.BlockSpec((tm, ATTN_W), lambda i: (i, 0))
    tab = pl.BlockSpec((tm, HEAD), lambda i: (i, 0))
    return pl.pallas_call(
        body, name="attn_grad_merge", grid=(S // tm,), in_specs=[blk] * 9 + [tab] * 3,
        out_specs=pl.BlockSpec((tm, 3 * ATTN_W), lambda i: (i, 0)), out_shape=jax.ShapeDtypeStruct((S, 3 * ATTN_W), BF16),
    )(*dqs, *dks, *dvs, *tabs)


SUB = 128
Q_COL, K_COL, V_COL = 0, ATTN_W // HEAD, 2 * ATTN_W // HEAD


class _AttnGeo:
    def __init__(self, S, d):
        self.S, self.d, self.L = S, d, S // d
        self.halo = N_SIDE * d
        self.TB = min(max(2048, 4 * self.halo), S)
        self.W = self.TB + 2 * self.halo
        self.n_sub = self.TB // SUB
        self.grid = (S // self.TB, N_HEADS)
        self.dt = F32 if d > 1 else BF16
        assert self.TB % (SUB * d) == 0 and self.TB % self.halo == 0

    def specs(self, width, col0, per_head=True):
        ratio = self.TB // self.halo
        last = self.S // self.halo - 1
        col = (lambda h: col0 + h) if per_head else (lambda h: col0)
        cur = pl.BlockSpec((self.TB, width), lambda i, h: (i, col(h)))
        prev = pl.BlockSpec((self.halo, width), lambda i, h: (jnp.maximum(i * ratio - 1, 0), col(h)))
        nxt = pl.BlockSpec((self.halo, width), lambda i, h: (jnp.minimum((i + 1) * ratio, last), col(h)))
        return cur, prev, nxt

    def rows(self, sub, n):
        res, blk = sub % self.d, sub // self.d
        start = res + self.d * SUB * blk
        return pl.ds(start, n, stride=self.d) if self.d > 1 else pl.ds(start, n)

    def mask(self, sub):
        base = pl.program_id(0) * (self.TB // self.d) + SUB * (sub // self.d)
        row = lax.broadcasted_iota(jnp.int32, (SUB, 2 * SUB), 0)
        col = lax.broadcasted_iota(jnp.int32, (SUB, 2 * SUB), 1)
        pos = base - N_SIDE + col
        return (col >= row) & (col <= row + 2 * N_SIDE) & (pos >= 0) & (pos < self.L)

    def fill(self, dst, c_ref):
        dst[...] = c_ref[...].astype(dst.dtype)

    def fill_window(self, dst, p_ref, c_ref, n_ref):
        dst[0:self.halo] = p_ref[...].astype(dst.dtype)
        dst[self.halo:self.halo + self.TB] = c_ref[...].astype(dst.dtype)
        dst[self.halo + self.TB:] = n_ref[...].astype(dst.dtype)


def _lane_of(tile, h):
    lane = lax.broadcasted_iota(jnp.int32, tile.shape, 1)
    return jnp.sum(jnp.where(lane == h, tile, 0.0), axis=1, keepdims=True)


def _attn_fwd(qk, proj, d):
    S = qk.shape[0]
    geo = _AttnGeo(S, d)
    scale = HEAD ** -0.5

    def body(q_ref, kp, kc, kn, vp, vc, vn, o_ref, lse_ref, qs, ks, vs, os, ls):
        h = pl.program_id(1)
        geo.fill(qs, q_ref)
        geo.fill_window(ks, kp, kc, kn)
        geo.fill_window(vs, vp, vc, vn)
        for sub in range(geo.n_sub):
            rq, rw = geo.rows(sub, SUB), geo.rows(sub, 2 * SUB)
            q_r, k_r, v_r = qs[rq, :].astype(BF16), ks[rw, :].astype(BF16), vs[rw, :].astype(BF16)
            s = jnp.where(geo.mask(sub), _dot(q_r, k_r, NT) * scale, NEG)
            m = jnp.max(s, axis=1, keepdims=True)
            p = jnp.exp(s - m)
            l = jnp.sum(p, axis=1, keepdims=True)
            os[rq, :] = _dot(p.astype(BF16), v_r) / l
            ls[rq, :] = jnp.broadcast_to(m + jnp.log(l), (SUB, LANES))
        o_ref[...] = os[...].astype(BF16)

        @pl.when(h == 0)
        def _():
            lse_ref[...] = jnp.zeros_like(lse_ref)

        lane = lax.broadcasted_iota(jnp.int32, (geo.TB, LANES), 1)
        lse_ref[...] = jnp.where(lane == h, ls[...], lse_ref[...])

    q_cur, _, _ = geo.specs(HEAD, Q_COL)
    k_specs = geo.specs(HEAD, K_COL)
    v_specs = geo.specs(HEAD, V_COL)
    stat = pl.BlockSpec((geo.TB, LANES), lambda i, h: (i, 0))
    return pl.pallas_call(
        body, name=f"attn_fwd_d{d}", grid=geo.grid,
        in_specs=[q_cur, k_specs[1], k_specs[0], k_specs[2], v_specs[1], v_specs[0], v_specs[2]],
        out_specs=[q_cur, stat],
        out_shape=[jax.ShapeDtypeStruct((S, ATTN_W), BF16), jax.ShapeDtypeStruct((S, LANES), F32)],
        scratch_shapes=[pltpu.VMEM((geo.TB, HEAD), geo.dt), pltpu.VMEM((geo.W, HEAD), geo.dt), pltpu.VMEM((geo.W, HEAD), geo.dt),
                        pltpu.VMEM((geo.TB, HEAD), F32), pltpu.VMEM((geo.TB, LANES), F32)],
    )(qk, qk, qk, qk, proj, proj, proj)


def _attn_combine(outs, lses, g, tm=256):
    S = outs[0].shape[0]

    def body(o1, o2, o3, l1, l2, l3, g_ref, ao_ref, o_ref, lse_ref):
        lane = lax.broadcasted_iota(jnp.int32, (tm, LANES), 1)
        lse_tile = jnp.zeros((tm, LANES), F32)
        a = [l1[...], l2[...], l3[...]]
        ssq = jnp.zeros((tm, 1), F32)
        for h in range(N_HEADS):
            sl = slice(h * HEAD, (h + 1) * HEAD)
            a1, a2, a3 = (t[:, h:h + 1] for t in a)
            mx = jnp.maximum(jnp.maximum(a1, a2), a3)
            e1, e2, e3 = jnp.exp(a1 - mx), jnp.exp(a2 - mx), jnp.exp(a3 - mx)
            den = e1 + e2 + e3
            oh = (e1 * o1[:, sl].astype(F32) + e2 * o2[:, sl].astype(F32) + e3 * o3[:, sl].astype(F32)) / den
            o_ref[:, sl] = oh
            ssq = ssq + jnp.sum(oh * oh, axis=1, keepdims=True)
            lse_tile = jnp.where(lane == h, mx + jnp.log(den), lse_tile)
        lse_ref[...] = lse_tile
        r = lax.rsqrt(ssq * (1.0 / ATTN_W) + EPS)
        ao_ref[...] = (o_ref[...] * r * g_ref[...]).astype(BF16)

    blk = pl.BlockSpec((tm, ATTN_W), lambda i: (i, 0))
    ls = pl.BlockSpec((tm, LANES), lambda i: (i, 0))
    return pl.pallas_call(
        body, name="attn_combine", grid=(S // tm,),
        in_specs=[blk, blk, blk, ls, ls, ls, pl.BlockSpec((1, ATTN_W), lambda i: (0, 0))], out_specs=[blk, blk, ls],
        out_shape=[jax.ShapeDtypeStruct((S, ATTN_W), BF16), jax.ShapeDtypeStruct((S, ATTN_W), F32),
                   jax.ShapeDtypeStruct((S, LANES), F32)],
    )(*outs, *lses, g)


def _attn_norm_bwd(o, g, dao, tm=256):
    S = o.shape[0]

    def body(o_ref, g_ref, dao_ref, do_ref, dl_ref, dg_ref):
        i = pl.program_id(0)

        @pl.when(i == 0)
        def _():
            dg_ref[...] = jnp.zeros_like(dg_ref)

        ov = o_ref[...]
        r = lax.rsqrt(jnp.mean(ov * ov, axis=-1, keepdims=True) + EPS)
        ohat = ov * r
        dn = dao_ref[...].astype(F32)
        dg_ref[...] += jnp.sum(dn * ohat, axis=0, keepdims=True)
        t = dn * g_ref[...]
        do = r * (t - ohat * jnp.mean(t * ohat, axis=-1, keepdims=True))
        do_ref[...] = do.astype(BF16)
        prod = do * ov
        lane = lax.broadcasted_iota(jnp.int32, (tm, LANES), 1)
        tile = jnp.zeros((tm, LANES), F32)
        for h in range(N_HEADS):
            tile = jnp.where(lane == h, jnp.sum(prod[:, h * HEAD:(h + 1) * HEAD], axis=1, keepdims=True), tile)
        dl_ref[...] = tile

    blk = pl.BlockSpec((tm, ATTN_W), lambda i: (i, 0))
    vec = pl.BlockSpec((1, ATTN_W), lambda i: (0, 0))
    return pl.pallas_call(
        body, name="attn_norm_bwd", grid=(S // tm,),
        in_specs=[blk, vec, pl.BlockSpec((tm, ATTN_W), lambda i: (i, 0))],
        out_specs=[blk, pl.BlockSpec((tm, LANES), lambda i: (i, 0)), vec],
        out_shape=[jax.ShapeDtypeStruct((S, ATTN_W), BF16), jax.ShapeDtypeStruct((S, LANES), F32),
                   jax.ShapeDtypeStruct((1, ATTN_W), F32)],
    )(o, g, dao)


def _attn_bwd_dq(qk, proj, do, lse, delta, d):
    S = qk.shape[0]
    geo = _AttnGeo(S, d)
    scale = HEAD ** -0.5

    def body(q_ref, kp, kc, kn, vp, vc, vn, do_ref, lse_ref, dl_ref, dq_ref, qs, ks, vs, dos, dqs):
        h = pl.program_id(1)
        geo.fill(qs, q_ref)
        geo.fill(dos, do_ref)
        geo.fill_window(ks, kp, kc, kn)
        geo.fill_window(vs, vp, vc, vn)
        for sub in range(geo.n_sub):
            rq, rw = geo.rows(sub, SUB), geo.rows(sub, 2 * SUB)
            q_r, k_r, v_r = qs[rq, :].astype(BF16), ks[rw, :].astype(BF16), vs[rw, :].astype(BF16)
            lse_c, dl_c = _lane_of(lse_ref[rq, :], h), _lane_of(dl_ref[rq, :], h)
            s = _dot(q_r, k_r, NT) * scale
            p = jnp.where(geo.mask(sub), jnp.exp(s - lse_c), 0.0)
            dp = _dot(dos[rq, :].astype(BF16), v_r, NT)
            ds = (p * (dp - dl_c) * scale).astype(BF16)
            dqs[rq, :] = _dot(ds, k_r)
        dq_ref[...] = dqs[...].astype(BF16)

    cur, _, _ = geo.specs(HEAD, 0)
    k_specs = geo.specs(HEAD, K_COL)
    v_specs = geo.specs(HEAD, V_COL)
    stat = pl.BlockSpec((geo.TB, LANES), lambda i, h: (i, 0))
    return pl.pallas_call(
        body, name=f"attn_bwd_dq_d{d}", grid=geo.grid,
        in_specs=[cur, k_specs[1], k_specs[0], k_specs[2], v_specs[1], v_specs[0], v_specs[2], cur, stat, stat],
        out_specs=cur, out_shape=jax.ShapeDtypeStruct((S, ATTN_W), BF16),
        scratch_shapes=[pltpu.VMEM((geo.TB, HEAD), geo.dt), pltpu.VMEM((geo.W, HEAD), geo.dt), pltpu.VMEM((geo.W, HEAD), geo.dt),
                        pltpu.VMEM((geo.TB, HEAD), geo.dt), pltpu.VMEM((geo.TB, HEAD), F32)],
    )(qk, qk, qk, qk, proj, proj, proj, do, lse, delta)


def _attn_bwd_dkv(qk, proj, do, lse, delta, d):
    S = qk.shape[0]
    geo = _AttnGeo(S, d)
    scale = HEAD ** -0.5

    def body(k_ref, v_ref, qp, qc, qn, dop, doc, don, lp, lc, ln, dlp, dlc, dln, dk_ref, dv_ref,
             ks, vs, qs, dos, lws, dlws, dks, dvs):
        h = pl.program_id(1)
        geo.fill(ks, k_ref)
        geo.fill(vs, v_ref)
        geo.fill_window(qs, qp, qc, qn)
        geo.fill_window(dos, dop, doc, don)
        geo.fill_window(lws, lp, lc, ln)
        geo.fill_window(dlws, dlp, dlc, dln)
        head = lax.broadcasted_iota(jnp.int32, (LANES, 2 * SUB), 0)
        for sub in range(geo.n_sub):
            rq, rw = geo.rows(sub, SUB), geo.rows(sub, 2 * SUB)
            k_r, v_r = ks[rq, :].astype(BF16), vs[rq, :].astype(BF16)
            q_w, do_w = qs[rw, :].astype(BF16), dos[rw, :].astype(BF16)
            lse_row = jnp.sum(jnp.where(head == h, lws[rw, :].T, 0.0), axis=0, keepdims=True)
            dl_row = jnp.sum(jnp.where(head == h, dlws[rw, :].T, 0.0), axis=0, keepdims=True)
            st = _dot(k_r, q_w, NT) * scale
            pt = jnp.where(geo.mask(sub), jnp.exp(st - lse_row), 0.0)
            dvs[rq, :] = _dot(pt.astype(BF16), do_w)
            dpt = _dot(v_r, do_w, NT)
            dst = (pt * (dpt - dl_row) * scale).astype(BF16)
            dks[rq, :] = _dot(dst, q_w)
        dk_ref[...] = dks[...].astype(BF16)
        dv_ref[...] = dvs[...].astype(BF16)

    q_specs = geo.specs(HEAD, Q_COL)
    k_cur, _, _ = geo.specs(HEAD, K_COL)
    v_cur, _, _ = geo.specs(HEAD, V_COL)
    do_specs = geo.specs(HEAD, 0)
    st_specs = geo.specs(LANES, 0, per_head=False)
    cur = do_specs[0]
    return pl.pallas_call(
        body, name=f"attn_bwd_dkv_d{d}", grid=geo.grid,
        in_specs=[k_cur, v_cur, q_specs[1], q_specs[0], q_specs[2], do_specs[1], do_specs[0], do_specs[2],
                  st_specs[1], st_specs[0], st_specs[2], st_specs[1], st_specs[0], st_specs[2]],
        out_specs=[cur, cur],
        out_shape=[jax.ShapeDtypeStruct((S, ATTN_W), BF16), jax.ShapeDtypeStruct((S, ATTN_W), BF16)],
        scratch_shapes=[pltpu.VMEM((geo.TB, HEAD), geo.dt), pltpu.VMEM((geo.TB, HEAD), geo.dt),
                        pltpu.VMEM((geo.W, HEAD), geo.dt), pltpu.VMEM((geo.W, HEAD), geo.dt),
                        pltpu.VMEM((geo.W, LANES), F32), pltpu.VMEM((geo.W, LANES), F32),
                        pltpu.VMEM((geo.TB, HEAD), F32), pltpu.VMEM((geo.TB, HEAD), F32)],
    )(qk, proj, qk, qk, qk, do, do, do, lse, lse, lse, delta, delta, delta)


def _cumsum_rows(x, reverse):
    n = x.shape[0]
    row = lax.broadcasted_iota(jnp.int32, x.shape, 0)
    s = 1
    while s < n:
        if reverse:
            x = x + jnp.where(row < n - s, pltpu.roll(x, n - s, axis=0), 0.0)
        else:
            x = x + jnp.where(row >= s, pltpu.roll(x, s, axis=0), 0.0)
        s *= 2
    return x


GLA_GROUP = 2


def _gla_rows(cc):
    return slice(cc * CHUNK, (cc + 1) * CHUNK)


def _gla_chunk_terms(q_ref, k_ref, v_ref, g_ref, h, reverse, rows):
    ksl = slice(h * GLA_DK, (h + 1) * GLA_DK)
    q = q_ref[rows, ksl].astype(F32) * (GLA_DK ** -0.5)
    k = k_ref[rows, ksl].astype(F32)
    v = v_ref[rows, h * GLA_DV:(h + 1) * GLA_DV]
    b = _cumsum_rows(g_ref[rows, ksl], reverse)
    r_ref = CHUNK // 2 if reverse else CHUNK // 2 - 1
    r_last = 0 if reverse else CHUNK - 1
    b_ref, b_last = b[r_ref:r_ref + 1, :], b[r_last:r_last + 1, :]
    ii = lax.broadcasted_iota(jnp.int32, (CHUNK, CHUNK), 0)
    jj = lax.broadcasted_iota(jnp.int32, (CHUNK, CHUNK), 1)
    causal = (jj >= ii) if reverse else (jj <= ii)
    e_q, e_k = jnp.exp(b - b_ref), jnp.exp(b_ref - b)
    e_in, e_st = jnp.exp(b), jnp.exp(b_last - b)
    return dict(q=q, k=k, v=v, b=b, causal=causal, e_q=e_q, e_k=e_k, e_in=e_in, e_st=e_st, dec=jnp.exp(b_last),
                qe=q * e_q, ke=k * e_k, q_in=q * e_in, k_st=k * e_st, r_ref=r_ref, r_last=r_last)


def _gla_specs(order):
    rows = GLA_GROUP * CHUNK
    q = pl.BlockSpec((rows, GLA_K), lambda c: (order(c), 3 * ATTN_W // GLA_K))
    k = pl.BlockSpec((rows, GLA_K), lambda c: (order(c), 3 * ATTN_W // GLA_K + 1))
    v = pl.BlockSpec((rows, GLA_V), lambda c: (order(c), (3 * ATTN_W + 2 * GLA_K) // GLA_V))
    return q, k, v


def _gla_fwd(proj, gates, reverse, o_prev=None):
    S = proj.shape[0]
    n = S // CHUNK
    nb = n // GLA_GROUP
    rows = GLA_GROUP * CHUNK
    order = (lambda c: nb - 1 - c) if reverse else (lambda c: c)
    seq = list(range(GLA_GROUP))[::-1] if reverse else list(range(GLA_GROUP))
    gcol = 1 if reverse else 0

    def body(*refs):
        if o_prev is None:
            q_ref, k_ref, v_ref, g_ref, o_ref, st_ref, state = refs
        else:
            q_ref, k_ref, v_ref, g_ref, op_ref, o_ref, st_ref, state = refs
        c = pl.program_id(0)

        @pl.when(c == 0)
        def _():
            state[...] = jnp.zeros_like(state)

        for h in range(GLA_HEADS):
            vsl = slice(h * GLA_DV, (h + 1) * GLA_DV)
            st = state[h]
            for cc in seq:
                rs = _gla_rows(cc)
                t = _gla_chunk_terms(q_ref, k_ref, v_ref, g_ref, h, reverse, rs)
                a = jnp.where(t["causal"], _dot(t["qe"].astype(BF16), t["ke"].astype(BF16), NT), 0.0)
                o = _dot(a.astype(BF16), t["v"])
                st_b = st.astype(BF16)
                st_ref[cc, h] = st_b
                o = o + _dot(t["q_in"].astype(BF16), st_b, NT)
                st = st * t["dec"] + _dot(t["v"], t["k_st"].astype(BF16), TN)
                if o_prev is not None:
                    o = o + op_ref[rs, vsl]
                o_ref[rs, vsl] = o
            state[h] = st

    q_spec, k_spec, v_spec = _gla_specs(order)
    o_spec = pl.BlockSpec((rows, GLA_V), lambda c: (order(c), 0))
    in_specs = [q_spec, k_spec, v_spec, pl.BlockSpec((rows, GLA_K), lambda c: (order(c), gcol))]
    operands = [proj, proj, proj, gates]
    if o_prev is not None:
        in_specs.append(o_spec)
        operands.append(o_prev)
    return pl.pallas_call(
        body, name="gla_fwd_rev" if reverse else "gla_fwd", grid=(nb,), in_specs=in_specs,
        out_specs=[o_spec, pl.BlockSpec((GLA_GROUP, GLA_HEADS, GLA_DV, GLA_DK), lambda c: (order(c), 0, 0, 0))],
        out_shape=[jax.ShapeDtypeStruct((S, GLA_V), F32), jax.ShapeDtypeStruct((n, GLA_HEADS, GLA_DV, GLA_DK), BF16)],
        scratch_shapes=[pltpu.VMEM((GLA_HEADS, GLA_DV, GLA_DK), F32)],
    )(*operands)


def _gla_bwd(proj, gates, states, do, reverse, prev=None):
    S = proj.shape[0]
    n = S // CHUNK
    nb = n // GLA_GROUP
    rows = GLA_GROUP * CHUNK
    order = (lambda c: c) if reverse else (lambda c: nb - 1 - c)
    seq = list(range(GLA_GROUP)) if reverse else list(range(GLA_GROUP))[::-1]
    gcol = 1 if reverse else 0
    out_dt = F32 if prev is None else BF16

    def body(*refs):
        if prev is None:
            q_ref, k_ref, v_ref, g_ref, st_ref, do_ref, dq_ref, dk_ref, dv_ref, dg_ref, dstate = refs
        else:
            q_ref, k_ref, v_ref, g_ref, st_ref, do_ref, pq, pk, pv, dq_ref, dk_ref, dv_ref, dg_ref, dstate = refs
        c = pl.program_id(0)

        @pl.when(c == 0)
        def _():
            dstate[...] = jnp.zeros_like(dstate)

        row = lax.broadcasted_iota(jnp.int32, (CHUNK, GLA_DK), 0)
        for h in range(GLA_HEADS):
            ksl = slice(h * GLA_DK, (h + 1) * GLA_DK)
            vsl = slice(h * GLA_DV, (h + 1) * GLA_DV)
            dst = dstate[h]
            for cc in seq:
                rs = _gla_rows(cc)
                t = _gla_chunk_terms(q_ref, k_ref, v_ref, g_ref, h, reverse, rs)
                v = t["v"]
                dob = do_ref[rs, vsl].astype(BF16)
                st_b = st_ref[cc, h]
                dst_b = dst.astype(BF16)
                qe_b, ke_b = t["qe"].astype(BF16), t["ke"].astype(BF16)
                q_in_b, k_st_b = t["q_in"].astype(BF16), t["k_st"].astype(BF16)
                a = jnp.where(t["causal"], _dot(qe_b, ke_b, NT), 0.0)
                da = jnp.where(t["causal"], _dot(dob, v, NT), 0.0).astype(BF16)
                dv = _dot(a.astype(BF16), dob, TN) + _dot(k_st_b, dst_b, NT)
                dqe = _dot(da, ke_b)
                dke = _dot(da, qe_b, TN)
                dq_in = _dot(dob, st_b)
                dk_st = _dot(v, dst_b)
                ddec = jnp.sum(dst * st_b.astype(F32), axis=0, keepdims=True)
                dst = _dot(dob, q_in_b, TN) + dst * t["dec"]
                dq = (dqe * t["e_q"] + dq_in * t["e_in"]) * (GLA_DK ** -0.5)
                dk = dke * t["e_k"] + dk_st * t["e_st"]
                w_q, w_k = dqe * t["qe"], dke * t["ke"]
                w_st = dk_st * t["k_st"]
                db = w_q - w_k + dq_in * t["q_in"] - w_st
                db_ref = jnp.sum(w_k - w_q, axis=0, keepdims=True)
                db_last = jnp.sum(w_st, axis=0, keepdims=True) + ddec * t["dec"]
                db = db + jnp.where(row == t["r_ref"], db_ref, 0.0) + jnp.where(row == t["r_last"], db_last, 0.0)
                dg_ref[rs, ksl] = _cumsum_rows(db, not reverse)
                if prev is not None:
                    dq, dk, dv = dq + pq[rs, ksl], dk + pk[rs, ksl], dv + pv[rs, vsl]
                dq_ref[rs, ksl] = dq.astype(out_dt)
                dk_ref[rs, ksl] = dk.astype(out_dt)
                dv_ref[rs, vsl] = dv.astype(out_dt)
            dstate[h] = dst

    q_spec, k_spec, v_spec = _gla_specs(order)
    kk = pl.BlockSpec((rows, GLA_K), lambda c: (order(c), 0))
    vv = pl.BlockSpec((rows, GLA_V), lambda c: (order(c), 0))
    in_specs = [q_spec, k_spec, v_spec, pl.BlockSpec((rows, GLA_K), lambda c: (order(c), gcol)),
                pl.BlockSpec((GLA_GROUP, GLA_HEADS, GLA_DV, GLA_DK), lambda c: (order(c), 0, 0, 0)), vv]
    operands = [proj, proj, proj, gates, states, do]
    if prev is not None:
        in_specs += [kk, kk, vv]
        operands += list(prev)
    return pl.pallas_call(
        body, name="gla_bwd_rev" if reverse else "gla_bwd", grid=(nb,), in_specs=in_specs, out_specs=[kk, kk, vv, kk],
        out_shape=[jax.ShapeDtypeStruct((S, GLA_K), out_dt), jax.ShapeDtypeStruct((S, GLA_K), out_dt),
                   jax.ShapeDtypeStruct((S, GLA_V), out_dt), jax.ShapeDtypeStruct((S, GLA_K), F32)],
        scratch_shapes=[pltpu.VMEM((GLA_HEADS, GLA_DV, GLA_DK), F32)],
    )(*operands)


def _gates_fwd(z, wg, bias, tm=512):
    S = z.shape[0]
    W = 2 * GLA_K

    def body(z_ref, w_ref, b_ref, o_ref):
        zg = _dot(z_ref[...], w_ref[...]) + b_ref[...]
        o_ref[...] = (jnp.minimum(zg, 0.0) - jnp.log(1.0 + jnp.exp(-jnp.abs(zg)))) * (1.0 / GATE_NORM)

    return pl.pallas_call(
        body, name="gates_fwd", grid=(S // tm,),
        in_specs=[pl.BlockSpec((tm, Z_W), lambda i: (i, 0)), pl.BlockSpec((Z_W, W), lambda i: (0, 0)),
                  pl.BlockSpec((1, W), lambda i: (0, 0))],
        out_specs=pl.BlockSpec((tm, W), lambda i: (i, 0)), out_shape=jax.ShapeDtypeStruct((S, W), F32),
    )(z, wg, bias)


def _gates_bwd(z, wg, bias, dg_f, dg_b, tm=512):
    S = z.shape[0]
    W = 2 * GLA_K

    def body(z_ref, w_ref, b_ref, dgf_ref, dgb_ref, dz_ref, dw_ref, db_ref):
        i = pl.program_id(0)

        @pl.when(i == 0)
        def _():
            dw_ref[...] = jnp.zeros_like(dw_ref)
            db_ref[...] = jnp.zeros_like(db_ref)

        zv = z_ref[...]
        zg = _dot(zv, w_ref[...]) + b_ref[...]
        dg = jnp.concatenate([dgf_ref[...], dgb_ref[...]], axis=1)
        dzg = dg * (1.0 / GATE_NORM) * _sigmoid(-zg)
        db_ref[...] += jnp.sum(dzg, axis=0, keepdims=True)
        dzg_b = dzg.astype(BF16)
        dw_ref[...] += _dot(zv, dzg_b, TN)
        dz_ref[...] = _dot(dzg_b, w_ref[...], NT).astype(BF16)

    half = pl.BlockSpec((tm, GLA_K), lambda i: (i, 0))
    return pl.pallas_call(
        body, name="gates_bwd", grid=(S // tm,),
        in_specs=[pl.BlockSpec((tm, Z_W), lambda i: (i, 0)), pl.BlockSpec((Z_W, W), lambda i: (0, 0)),
                  pl.BlockSpec((1, W), lambda i: (0, 0)), half, half],
        out_specs=[pl.BlockSpec((tm, Z_W), lambda i: (i, 0)), pl.BlockSpec((Z_W, W), lambda i: (0, 0)),
                   pl.BlockSpec((1, W), lambda i: (0, 0))],
        out_shape=[jax.ShapeDtypeStruct((S, Z_W), BF16), jax.ShapeDtypeStruct((Z_W, W), F32),
                   jax.ShapeDtypeStruct((1, W), F32)],
    )(z, wg, bias, dg_f, dg_b)


def _gla_out_fwd(o, proj, g, tm=512):
    S = o.shape[0]

    def body(o_ref, gr_ref, g_ref, out_ref):
        gn = g_ref[...]
        for h in range(GLA_HEADS):
            sl = slice(h * GLA_DV, (h + 1) * GLA_DV)
            ov = o_ref[:, sl]
            r = lax.rsqrt(jnp.mean(ov * ov, axis=-1, keepdims=True) + EPS)
            gr = gr_ref[:, sl].astype(F32)
            out_ref[:, sl] = (ov * r * gn * (gr * _sigmoid(gr))).astype(BF16)

    blk = pl.BlockSpec((tm, GLA_V), lambda i: (i, 0))
    return pl.pallas_call(
        body, name="gla_out_fwd", grid=(S // tm,),
        in_specs=[blk, pl.BlockSpec((tm, GLA_V), lambda i: (i, (3 * ATTN_W + 2 * GLA_K + GLA_V) // GLA_V)),
                  pl.BlockSpec((1, GLA_DV), lambda i: (0, 0))],
        out_specs=blk, out_shape=jax.ShapeDtypeStruct((S, GLA_V), BF16),
    )(o, proj, g)


def _gla_out_bwd(o, proj, g, dcat, tm=512):
    S = o.shape[0]

    def body(o_ref, gr_ref, g_ref, dgo_ref, do_ref, dgr_ref, dg_ref):
        i = pl.program_id(0)

        @pl.when(i == 0)
        def _():
            dg_ref[...] = jnp.zeros_like(dg_ref)

        gn = g_ref[...]
        dg_acc = jnp.zeros((1, GLA_DV), F32)
        for h in range(GLA_HEADS):
            sl = slice(h * GLA_DV, (h + 1) * GLA_DV)
            ov = o_ref[:, sl]
            r = lax.rsqrt(jnp.mean(ov * ov, axis=-1, keepdims=True) + EPS)
            yhat = ov * r
            gr = gr_ref[:, sl].astype(F32)
            sg = _sigmoid(gr)
            dgo = dgo_ref[:, sl].astype(F32)
            dgr_ref[:, sl] = (dgo * (yhat * gn) * (sg * (1.0 + gr * (1.0 - sg)))).astype(BF16)
            dy = dgo * (gr * sg)
            dg_acc = dg_acc + jnp.sum(dy * yhat, axis=0, keepdims=True)
            t = dy * gn
            do_ref[:, sl] = r * (t - yhat * jnp.mean(t * yhat, axis=-1, keepdims=True))
        dg_ref[...] += dg_acc

    blk = pl.BlockSpec((tm, GLA_V), lambda i: (i, 0))
    vec = pl.BlockSpec((1, GLA_DV), lambda i: (0, 0))
    return pl.pallas_call(
        body, name="gla_out_bwd", grid=(S // tm,),
        in_specs=[blk, pl.BlockSpec((tm, GLA_V), lambda i: (i, (3 * ATTN_W + 2 * GLA_K + GLA_V) // GLA_V)), vec,
                  pl.BlockSpec((tm, GLA_V), lambda i: (i, 1))],
        out_specs=[blk, blk, vec],
        out_shape=[jax.ShapeDtypeStruct((S, GLA_V), F32), jax.ShapeDtypeStruct((S, GLA_V), BF16),
                   jax.ShapeDtypeStruct((1, GLA_DV), F32)],
    )(o, proj, g, dcat)


HALO = 16


def _halo_specs(tm, tn, S):
    cur = pl.BlockSpec((tm, tn), lambda j, i: (i, j))
    prev = pl.BlockSpec((HALO, tn), lambda j, i: (jnp.maximum(i * (tm // HALO) - 1, 0), j))
    nxt = pl.BlockSpec((HALO, tn), lambda j, i: (jnp.minimum((i + 1) * (tm // HALO), S // HALO - 1), j))
    return cur, prev, nxt


def _shifted(c_ref, p_ref, n_ref, n_blocks):
    i = pl.program_id(1)
    x = c_ref[...].astype(F32)
    tm = x.shape[0]
    row = lax.broadcasted_iota(jnp.int32, x.shape, 0)
    before = p_ref[HALO - 1:HALO, :].astype(F32) * (i > 0).astype(F32)
    after = n_ref[0:1, :].astype(F32) * (i < n_blocks - 1).astype(F32)
    x_m1 = jnp.where(row == 0, before, pltpu.roll(x, 1, axis=0))
    x_p1 = jnp.where(row == tm - 1, after, pltpu.roll(x, tm - 1, axis=0))
    return x, x_m1, x_p1


def _glu_fwd(gp, up, cw, cb, tm=512, tn=1408):
    S = gp.shape[0]
    nb = S // tm

    def body(c_ref, p_ref, n_ref, up_ref, w_ref, b_ref, o_ref):
        x, x_m1, x_p1 = _shifted(c_ref, p_ref, n_ref, nb)
        w = w_ref[...]
        gate = w[0:1, :] * x_m1 + w[1:2, :] * x + w[2:3, :] * x_p1 + b_ref[...]
        o_ref[...] = (gate * _sigmoid(gate) * up_ref[...].astype(F32)).astype(BF16)

    cur, prev, nxt = _halo_specs(tm, tn, S)
    return pl.pallas_call(
        body, name="glu_fwd", grid=(D_FF // tn, nb),
        in_specs=[cur, prev, nxt, cur, pl.BlockSpec((3, tn), lambda j, i: (0, j)), pl.BlockSpec((1, tn), lambda j, i: (0, j))],
        out_specs=cur, out_shape=jax.ShapeDtypeStruct((S, D_FF), BF16),
    )(gp, gp, gp, up, cw, cb)


def _glu_bwd(gp, up, dact, cw, cb, tm=512, tn=1408):
    S = gp.shape[0]
    nb = S // tm

    def body(c_ref, p_ref, n_ref, up_ref, da_ref, w_ref, b_ref, dup_ref, dgate_ref, dw_ref, db_ref):
        @pl.when(pl.program_id(1) == 0)
        def _():
            dw_ref[...] = jnp.zeros_like(dw_ref)
            db_ref[...] = jnp.zeros_like(db_ref)

        x, x_m1, x_p1 = _shifted(c_ref, p_ref, n_ref, nb)
        w = w_ref[...]
        gate = w[0:1, :] * x_m1 + w[1:2, :] * x + w[2:3, :] * x_p1 + b_ref[...]
        sg = _sigmoid(gate)
        da = da_ref[...].astype(F32)
        dup_ref[...] = (da * (gate * sg)).astype(BF16)
        dgate = da * up_ref[...].astype(F32) * (sg * (1.0 + gate * (1.0 - sg)))
        dgate_ref[...] = dgate.astype(BF16)
        db_ref[...] += jnp.sum(dgate, axis=0, keepdims=True)
        dw_ref[...] += jnp.concatenate(
            [jnp.sum(dgate * x_m1, axis=0, keepdims=True), jnp.sum(dgate * x, axis=0, keepdims=True),
             jnp.sum(dgate * x_p1, axis=0, keepdims=True)], axis=0)

    cur, prev, nxt = _halo_specs(tm, tn, S)
    w_spec = pl.BlockSpec((3, tn), lambda j, i: (0, j))
    b_spec = pl.BlockSpec((1, tn), lambda j, i: (0, j))
    return pl.pallas_call(
        body, name="glu_bwd", grid=(D_FF // tn, nb), in_specs=[cur, prev, nxt, cur, cur, w_spec, b_spec],
        out_specs=[cur, cur, w_spec, b_spec],
        out_shape=[jax.ShapeDtypeStruct((S, D_FF), BF16), jax.ShapeDtypeStruct((S, D_FF), BF16),
                   jax.ShapeDtypeStruct((3, D_FF), F32), jax.ShapeDtypeStruct((1, D_FF), F32)],
    )(gp, gp, gp, up, dact, cw, cb)


def _conv_bwd_input(dgate, cw, tm=512, tn=1408):
    S = dgate.shape[0]
    nb = S // tm

    def body(c_ref, p_ref, n_ref, w_ref, o_ref):
        x, x_m1, x_p1 = _shifted(c_ref, p_ref, n_ref, nb)
        w = w_ref[...]
        o_ref[...] = (w[0:1, :] * x_p1 + w[1:2, :] * x + w[2:3, :] * x_m1).astype(BF16)

    cur, prev, nxt = _halo_specs(tm, tn, S)
    return pl.pallas_call(
        body, name="conv_bwd_input", grid=(D_FF // tn, nb),
        in_specs=[cur, prev, nxt, pl.BlockSpec((3, tn), lambda j, i: (0, j))], out_specs=cur,
        out_shape=jax.ShapeDtypeStruct((S, D_FF), BF16),
    )(dgate, dgate, dgate, cw)


def _local_step(x, target, norm1_g, w_in_t, wg, gate_bias, gla_norm_g, attn_norm_g, w_out, norm2_g,
                w_gate4, w_up4, conv_w, conv_b, w_down, final_norm_g, on_grad=lambda event, arrays: ()):
    S = x.shape[0]
    tabs = _rope_tables(S)

    n1 = _rms_fwd("rms1_fwd", x, norm1_g)
    z_block = IN_MAIN // Z_W
    proj = _mm_nt("in_proj", n1, w_in_t, 1024, 1536, BF16, n_out=IN_MAIN)
    z = _matmul(
        "in_proj_z",
        [(n1, pl.BlockSpec((1024, D_MODEL), lambda i: (i, 0)), w_in_t, pl.BlockSpec((Z_W, D_MODEL), lambda i: (z_block, 0)), NT)],
        (S // 1024,), jax.ShapeDtypeStruct((S, Z_W), BF16), pl.BlockSpec((1024, Z_W), lambda i: (i, 0)), 1)
    qk = _rope_fwd(proj, tabs)
    branch = [_attn_fwd(qk, proj, d) for d in DILATIONS]
    ao, o_attn, lse = _attn_combine([b[0] for b in branch], [b[1] for b in branch], attn_norm_g)
    gates = _gates_fwd(z, wg, gate_bias)
    o_f, st_f = _gla_fwd(proj, gates, False)
    o_gla, st_b = _gla_fwd(proj, gates, True, o_prev=o_f)
    go = _gla_out_fwd(o_gla, proj, gla_norm_g)
    cat = jnp.concatenate([ao, go], axis=1)
    h1 = _mm_nn("out_proj", cat, w_out, 1024, 1024, F32, res=x)
    n2 = _rms_fwd("rms2_fwd", h1, norm2_g)
    gp = _mm_nn_sharded("ffn_gate", n2, w_gate4, 1024, BF16)
    up = _mm_nn_sharded("ffn_up", n2, w_up4, 1024, BF16)
    act = _glu_fwd(gp, up, conv_w, conv_b)
    tk = D_FF // N_CHIPS
    h2 = _matmul(
        "ffn_down",
        [(act, pl.BlockSpec((1024, tk), lambda i, j, k: (i, k)), w_down, pl.BlockSpec((tk, 1024), lambda i, j, k: (k, j)), NN)],
        (S // 1024, D_MODEL // 1024, N_CHIPS), jax.ShapeDtypeStruct((S, D_MODEL), F32),
        pl.BlockSpec((1024, 1024), lambda i, j, k: (i, j)), N_CHIPS,
        res=(h1, pl.BlockSpec((1024, 1024), lambda i, j, k: (i, j))))
    loss_row, d_final_g, dh2, dh2_b = _final_loss(h2, final_norm_g.reshape(1, D_MODEL), target)

    dact = _mm_nt("ffn_down_bwd", dh2_b, w_down, 1024, tk, BF16)
    dup, dgate, d_conv_w, d_conv_b = _glu_bwd(gp, up, dact, conv_w, conv_b)
    dgp = _conv_bwd_input(dgate, conv_w)
    d_w_down = _mm_tn("ffn_down_wgrad", act, dh2_b, tk, D_MODEL, 1024, BF16)
    on_grad("w_down", dict(w_down=d_w_down))
    dgp = _after(dgp, d_w_down)
    d_w_gate4 = _mm_tn("ffn_gate_wgrad", n2, dgp, D_MODEL, tk, 1024, BF16, out3=tk)
    dup = _after(dup, d_w_gate4)
    d_w_up4 = _mm_tn("ffn_up_wgrad", n2, dup, D_MODEL, tk, 1024, BF16, out3=tk)
    held = on_grad("w_gate_w_up", dict(w_gate=d_w_gate4, w_up=d_w_up4))
    dgp = _after(dgp, d_w_up4, *held)
    dn2 = _matmul(
        "ffn_in_bwd",
        [(dgp, pl.BlockSpec((1024, tk), lambda i, j, k: (i, k)), w_gate4, pl.BlockSpec((None, 1024, tk), lambda i, j, k: (k, j, 0)), NT),
         (dup, pl.BlockSpec((1024, tk), lambda i, j, k: (i, k)), w_up4, pl.BlockSpec((None, 1024, tk), lambda i, j, k: (k, j, 0)), NT)],
        (S // 1024, D_MODEL // 1024, N_CHIPS), jax.ShapeDtypeStruct((S, D_MODEL), F32),
        pl.BlockSpec((1024, 1024), lambda i, j, k: (i, j)), N_CHIPS)
    dh1, dh1_b, d_norm2_g = _rms_bwd("rms2_bwd", h1, norm2_g, dn2, dh2)

    d_w_out = _mm_tn("out_proj_wgrad", cat, dh1_b, D_MODEL, 1024, 1024, BF16)
    held = on_grad("w_out", dict(w_out=d_w_out))
    dcat = _mm_nt("out_proj_bwd", _after(dh1_b, d_w_out, *held), w_out, 1024, 1024, BF16)
    do_attn, delta, d_attn_norm_g = _attn_norm_bwd(o_attn, attn_norm_g, dcat)
    dqs, dks, dvs = [], [], []
    for d in DILATIONS:
        dqs.append(_attn_bwd_dq(qk, proj, do_attn, lse, delta, d))
        dk, dv = _attn_bwd_dkv(qk, proj, do_attn, lse, delta, d)
        dks.append(dk)
        dvs.append(dv)
    d_attn = _attn_grad_merge(dqs, dks, dvs, tabs)
    held = on_grad("mid", dict(anchor=d_attn))
    do_gla, dgr, d_gla_norm_g = _gla_out_bwd(o_gla, proj, gla_norm_g, _after(dcat, *held))
    dq_f, dk_f, dv_f, dg_f = _gla_bwd(proj, gates, st_f, do_gla, False)
    dgq, dgk, dgv, dg_b = _gla_bwd(proj, gates, st_b, do_gla, True, prev=(dq_f, dk_f, dv_f))
    dz, d_wg, d_gate_bias = _gates_bwd(z, wg, gate_bias, dg_f, dg_b)
    dproj = jnp.concatenate([d_attn, dgq, dgk, dgv, dgr], axis=1)
    d_w_in_t = _mm_tn("in_proj_wgrad", dproj, n1, 1536, D_MODEL, 1024, BF16, rows_out=IN_W)
    n_tok = S // 1024
    d_w_in_t = _matmul(
        "in_proj_z_wgrad",
        [(dz, pl.BlockSpec((1024, Z_W), lambda i, j, k: (k, 0)), n1, pl.BlockSpec((1024, D_MODEL), lambda i, j, k: (k, 0)), TN)],
        (1, 1, n_tok), jax.ShapeDtypeStruct((IN_W, D_MODEL), BF16), pl.BlockSpec((Z_W, D_MODEL), lambda i, j, k: (z_block, 0)),
        n_tok, into=d_w_in_t)
    held = on_grad("w_in", dict(w_in_t=d_w_in_t))
    tkm = IN_MAIN // 4
    dn1 = _matmul(
        "in_proj_bwd",
        [(_after(dproj, d_w_in_t, *held), pl.BlockSpec((1024, tkm), lambda i, j, k: (i, k)), w_in_t, pl.BlockSpec((tkm, 1024), lambda i, j, k: (k, j)), NN)],
        (S // 1024, D_MODEL // 1024, 4), jax.ShapeDtypeStruct((S, D_MODEL), F32),
        pl.BlockSpec((1024, 1024), lambda i, j, k: (i, j)), 4)
    held = on_grad("last", dict(last=dn1))
    dn1 = _matmul(
        "in_proj_z_bwd",
        [(dz, pl.BlockSpec((1024, Z_W), lambda j, i: (i, 0)), w_in_t, pl.BlockSpec((Z_W, 1024), lambda j, i: (z_block, j)), NN)],
        (D_MODEL // 1024, S // 1024), jax.ShapeDtypeStruct((S, D_MODEL), F32), pl.BlockSpec((1024, 1024), lambda j, i: (i, j)), 1,
        res=(_after(dn1, *held), pl.BlockSpec((1024, 1024), lambda j, i: (i, j))))
    grad_x, _, d_norm1_g = _rms_bwd("rms1_bwd", x, norm1_g, dn1, dh1)

    big = dict(w_in_t=d_w_in_t, w_out=d_w_out, w_gate4=d_w_gate4, w_up4=d_w_up4, w_down=d_w_down)
    small = dict(loss=loss_row, norm1_g=d_norm1_g, wg=d_wg, gate_bias=d_gate_bias, gla_norm_g=d_gla_norm_g,
                 attn_norm_g=d_attn_norm_g, norm2_g=d_norm2_g, conv_w=d_conv_w, conv_b=d_conv_b, final_norm_g=d_final_g)
    return grad_x, big, small


def _position():
    return lax.axis_index("x"), lax.axis_index("y"), lax.axis_index("c")


def _other_chips(x, y):
    return [(1 - x, y), (x, 1 - y), (1 - x, 1 - y)]


def _gather_chips(name, shards):
    n = len(shards)

    def body(*refs):
        ins, outs = refs[:n], refs[n:2 * n]
        send, recv, loc = refs[2 * n:]
        x, y, c = _position()
        me = 2 * x + y
        chips = _other_chips(x, y)
        started = []
        for w in range(n):
            own = pltpu.make_async_copy(ins[w], outs[w].at[me], loc.at[w])
            own.start()
            started.append(own)
        sends = []
        for w in range(n):
            for j, (px, py) in enumerate(chips):
                cp = pltpu.make_async_remote_copy(ins[w], outs[w].at[me], send.at[3 * w + j], recv.at[3 * w + j],
                                                  device_id=(px, py, c), device_id_type=MESH)
                cp.start()
                sends.append(cp)
        for w in range(n):
            for j, (px, py) in enumerate(chips):
                pltpu.make_async_remote_copy(ins[w], outs[w].at[2 * px + py], send.at[3 * w + j], recv.at[3 * w + j],
                                             device_id=(px, py, c), device_id_type=MESH).wait_recv()
        for cp in sends:
            cp.wait_send()
        for own in started:
            own.wait()

    return pl.pallas_call(
        body, name=name, in_specs=[ANY] * n, out_specs=[ANY] * n,
        out_shape=[jax.ShapeDtypeStruct((N_CHIPS,) + s.shape, s.dtype) for s in shards],
        scratch_shapes=[pltpu.SemaphoreType.DMA((3 * n,)), pltpu.SemaphoreType.DMA((3 * n,)), pltpu.SemaphoreType.DMA((n,))],
    )(*shards)


def _gather_chips_async(name, shards, collective_id):
    n = len(shards)

    def body(*refs):
        ins, outs = refs[:n], refs[n:2 * n]
        send, recv, loc = refs[2 * n:]
        x, y, c = _position()
        me = 2 * x + y
        chips = _other_chips(x, y)
        barrier = pltpu.get_barrier_semaphore()
        for px, py in chips:
            pl.semaphore_signal(barrier, inc=1, device_id=(px, py, c), device_id_type=MESH)
        pl.semaphore_wait(barrier, len(chips))
        started = []
        for w in range(n):
            own = pltpu.make_async_copy(ins[w], outs[w].at[me], loc.at[w])
            own.start()
            started.append(own)
        sends = []
        for w in range(n):
            for j, (px, py) in enumerate(chips):
                cp = pltpu.make_async_remote_copy(ins[w], outs[w].at[me], send.at[3 * w + j], recv.at[3 * w + j],
                                                  device_id=(px, py, c), device_id_type=MESH)
                cp.start()
                sends.append(cp)
        for w in range(n):
            for j, (px, py) in enumerate(chips):
                pltpu.make_async_remote_copy(ins[w], outs[w].at[2 * px + py], send.at[3 * w + j], recv.at[3 * w + j],
                                             device_id=(px, py, c), device_id_type=MESH).wait_recv()
        for cp in sends:
            cp.wait_send()
        for own in started:
            own.wait()

    return pl.kernel(
        body, name=name, mesh=_sequencer(),
        out_type=[jax.ShapeDtypeStruct((N_CHIPS,) + s.shape, s.dtype) for s in shards],
        scratch_types=[pltpu.SemaphoreType.DMA((3 * n,)), pltpu.SemaphoreType.DMA((3 * n,)), pltpu.SemaphoreType.DMA((n,))],
        compiler_params=pltpu.CompilerParams(collective_id=collective_id),
    )(*shards)


def _sibling_exchange(name, arrs):
    n = len(arrs)

    def body(*refs):
        ins, outs = refs[:n], refs[n:2 * n]
        send, recv = refs[2 * n:]
        x, y, c = _position()
        copies = [pltpu.make_async_remote_copy(ins[w], outs[w], send.at[w], recv.at[w], device_id=(x, y, 1 - c),
                                               device_id_type=MESH) for w in range(n)]
        for cp in copies:
            cp.start()
        for cp in copies:
            cp.wait()

    return pl.pallas_call(
        body, name=name, in_specs=[ANY] * n, out_specs=[ANY] * n,
        out_shape=[jax.ShapeDtypeStruct(a.shape, a.dtype) for a in arrs],
        scratch_shapes=[pltpu.SemaphoreType.DMA((n,)), pltpu.SemaphoreType.DMA((n,))],
    )(*arrs)


def _scatter_chips(name, parts):
    n = len(parts)

    def body(*refs):
        ins, outs = refs[:n], refs[n:2 * n]
        send, recv, loc = refs[2 * n:]
        x, y, c = _position()
        me = 2 * x + y
        chips = _other_chips(x, y)
        started = []
        for w in range(n):
            own = pltpu.make_async_copy(ins[w].at[me], outs[w].at[me], loc.at[w])
            own.start()
            started.append(own)
        sends = []
        for w in range(n):
            for j, (px, py) in enumerate(chips):
                cp = pltpu.make_async_remote_copy(ins[w].at[2 * px + py], outs[w].at[me], send.at[3 * w + j],
                                                  recv.at[3 * w + j], device_id=(px, py, c), device_id_type=MESH)
                cp.start()
                sends.append(cp)
        for w in range(n):
            for j, (px, py) in enumerate(chips):
                pltpu.make_async_remote_copy(ins[w].at[me], outs[w].at[2 * px + py], send.at[3 * w + j], recv.at[3 * w + j],
                                             device_id=(px, py, c), device_id_type=MESH).wait_recv()
        for cp in sends:
            cp.wait_send()
        for own in started:
            own.wait()

    return pl.pallas_call(
        body, name=name, in_specs=[ANY] * n, out_specs=[ANY] * n,
        out_shape=[jax.ShapeDtypeStruct(p.shape, p.dtype) for p in parts],
        scratch_shapes=[pltpu.SemaphoreType.DMA((3 * n,)), pltpu.SemaphoreType.DMA((3 * n,)), pltpu.SemaphoreType.DMA((n,))],
    )(*parts)


def _sequencer():
    return plsc.ScalarSubcoreMesh(axis_name="sequencer", num_cores=1)


def _sibling_exchange_async(name, arrs, collective_id):
    n = len(arrs)

    def body(*refs):
        ins, outs = refs[:n], refs[n:2 * n]
        send, recv = refs[2 * n:]
        x, y, c = _position()
        sibling = (x, y, 1 - c)
        barrier = pltpu.get_barrier_semaphore()
        pl.semaphore_signal(barrier, inc=1, device_id=sibling, device_id_type=MESH)
        pl.semaphore_wait(barrier, 1)
        copies = [pltpu.make_async_remote_copy(ins[w], outs[w], send.at[w], recv.at[w], device_id=sibling,
                                               device_id_type=MESH) for w in range(n)]
        for cp in copies:
            cp.start()
        for cp in copies:
            cp.wait()

    return pl.kernel(
        body, name=name, out_type=[jax.ShapeDtypeStruct(a.shape, a.dtype) for a in arrs],
        scratch_types=[pltpu.SemaphoreType.DMA((n,)), pltpu.SemaphoreType.DMA((n,))],
        compiler_params=pltpu.CompilerParams(collective_id=collective_id), mesh=_sequencer(),
    )(*arrs)


def _scatter_chips_async(name, parts, collective_id):
    n = len(parts)

    def body(*refs):
        ins, outs = refs[:n], refs[n:2 * n]
        send, recv, loc = refs[2 * n:]
        x, y, c = _position()
        me = 2 * x + y
        chips = _other_chips(x, y)
        barrier = pltpu.get_barrier_semaphore()
        for px, py in chips:
            pl.semaphore_signal(barrier, inc=1, device_id=(px, py, c), device_id_type=MESH)
        pl.semaphore_wait(barrier, len(chips))
        started = []
        for w in range(n):
            own = pltpu.make_async_copy(ins[w].at[me], outs[w].at[me], loc.at[w])
            own.start()
            started.append(own)
        sends = []
        for w in range(n):
            for j, (px, py) in enumerate(chips):
                cp = pltpu.make_async_remote_copy(ins[w].at[2 * px + py], outs[w].at[me], send.at[3 * w + j],
                                                  recv.at[3 * w + j], device_id=(px, py, c), device_id_type=MESH)
                cp.start()
                sends.append(cp)
        for w in range(n):
            for j, (px, py) in enumerate(chips):
                pltpu.make_async_remote_copy(ins[w].at[me], outs[w].at[2 * px + py], send.at[3 * w + j], recv.at[3 * w + j],
                                             device_id=(px, py, c), device_id_type=MESH).wait_recv()
        for cp in sends:
            cp.wait_send()
        for own in started:
            own.wait()

    return pl.kernel(
        body, name=name, out_type=[jax.ShapeDtypeStruct(p.shape, p.dtype) for p in parts],
        scratch_types=[pltpu.SemaphoreType.DMA((3 * n,)), pltpu.SemaphoreType.DMA((3 * n,)), pltpu.SemaphoreType.DMA((n,))],
        compiler_params=pltpu.CompilerParams(collective_id=collective_id), mesh=_sequencer(),
    )(*parts)


def _allreduce_rows(buf):
    R = buf.shape[0]

    def body(in_ref, out_ref, land, send, recv):
        x, y, c = _position()
        me = 4 * x + 2 * y + c
        land[pl.ds(me, 1)] = in_ref[...][None]
        peers = []
        for mask in range(1, N_DEV):
            px = 1 - x if mask & 4 else x
            py = 1 - y if mask & 2 else y
            pc = 1 - c if mask & 1 else c
            peers.append((px, py, pc))
        sends = []
        for k, peer in enumerate(peers):
            cp = pltpu.make_async_remote_copy(in_ref, land.at[me], send.at[k], recv.at[k], device_id=peer, device_id_type=MESH)
            cp.start()
            sends.append(cp)
        for k, (px, py, pc) in enumerate(peers):
            pltpu.make_async_remote_copy(in_ref, land.at[4 * px + 2 * py + pc], send.at[k], recv.at[k],
                                         device_id=(px, py, pc), device_id_type=MESH).wait_recv()
        for cp in sends:
            cp.wait_send()
        tot = land[0]
        for i in range(1, N_DEV):
            tot = tot + land[i]
        out_ref[...] = tot

    vm = pl.BlockSpec(memory_space=pltpu.VMEM)
    return pl.pallas_call(
        body, name="allreduce_small", in_specs=[vm], out_specs=vm, out_shape=jax.ShapeDtypeStruct((R, LANES), F32),
        scratch_shapes=[pltpu.VMEM((N_DEV, R, LANES), F32), pltpu.SemaphoreType.DMA((N_DEV - 1,)),
                        pltpu.SemaphoreType.DMA((N_DEV - 1,))],
    )(buf)


def _tile2d(r, c, cap):
    if r <= cap:
        return r, c
    fits = [t for t in range(16, cap + 1, 16) if r % t == 0]
    return (max(fits), c) if fits else (r, 256)


def _pair_sum(name, a, b):
    n, r, c = a.shape
    tr, tc = _tile2d(r, c, 1024)

    def body(a_ref, b_ref, o_ref):
        o_ref[...] = (a_ref[...].astype(F32) + b_ref[...].astype(F32)).astype(BF16)

    blk = pl.BlockSpec((None, tr, tc), lambda s, i, j: (s, i, j))
    return pl.pallas_call(
        body, name=name, grid=(n, r // tr, c // tc), in_specs=[blk, blk], out_specs=blk,
        out_shape=jax.ShapeDtypeStruct(a.shape, BF16),
    )(a, b)


def _adamw_math(w, m, v, g):
    m2 = ADAM_B1 * m + (1.0 - ADAM_B1) * g
    v2 = ADAM_B2 * v + (1.0 - ADAM_B2) * (g * g)
    m_hat = m2 / (1.0 - ADAM_B1 ** ADAM_STEP)
    v_hat = v2 / (1.0 - ADAM_B2 ** ADAM_STEP)
    delta = -ADAM_LR * (m_hat / (jnp.sqrt(v_hat) + ADAM_EPS) + ADAM_WD * w)
    return delta, m2, v2


def _adamw(name, w, m, v, g):
    r, c = w.shape
    stacked = g.ndim == 3
    tr, tc = _tile2d(r, c, 256)

    def body(w_ref, m_ref, v_ref, g_ref, go_ref, d_ref, m2_ref, v2_ref):
        if stacked:
            gv = g_ref[0].astype(F32)
            for i in range(1, N_CHIPS):
                gv = gv + g_ref[i].astype(F32)
        else:
            gv = g_ref[...]
        delta, m2, v2 = _adamw_math(w_ref[...], m_ref[...], v_ref[...], gv)
        go_ref[...] = gv
        d_ref[...] = delta
        m2_ref[...] = m2
        v2_ref[...] = v2

    blk = pl.BlockSpec((tr, tc), lambda i, j: (i, j))
    g_spec = pl.BlockSpec((N_CHIPS, tr, tc), lambda i, j: (0, i, j)) if stacked else blk
    out = jax.ShapeDtypeStruct((r, c), F32)
    return pl.pallas_call(
        body, name=name, grid=(r // tr, c // tc), in_specs=[blk, blk, blk, g_spec], out_specs=[blk] * 4, out_shape=[out] * 4,
    )(w, m, v, g)


def _pack_rows(pieces):
    flat = jnp.concatenate([p.reshape(-1) for p in pieces])
    rows = flat.shape[0] // LANES
    pad = (-rows) % 8
    return jnp.pad(flat.reshape(rows, LANES), ((0, pad), (0, 0)))


def _unpack_rows(buf, shapes):
    flat = buf.reshape(-1)
    out, at = [], 0
    for s in shapes:
        size = math.prod(s)
        out.append(flat[at:at + size].reshape(s))
        at += size
    return out


SMALL_NAMES = ("norm1_g", "gf_up", "gf_b", "gb_up", "gb_b", "gla_norm_g", "attn_norm_g", "norm2_g", "conv_w", "conv_b",
               "final_norm_g")
BIG_NAMES = ("w_in", "w_out", "w_gate", "w_up", "w_down")
WEIGHT_ORDER = ("norm1_g", "w_in", "gf_up", "gf_b", "gb_up", "gb_b", "gla_norm_g", "attn_norm_g", "w_out", "norm2_g",
                "w_gate", "w_up", "conv_w", "conv_b", "w_down", "final_norm_g")


def kernel(x, norm1_g, w_in, gf_up, gf_b, gb_up, gb_b, gla_norm_g, attn_norm_g, w_out, norm2_g, w_gate, w_up, conv_w, conv_b, w_down, final_norm_g, loss_target, m_norm1_g, m_w_in, m_gf_up, m_gf_b, m_gb_up, m_gb_b, m_gla_norm_g, m_attn_norm_g, m_w_out, m_norm2_g, m_w_gate, m_w_up, m_conv_w, m_conv_b, m_w_down, m_final_norm_g, v_norm1_g, v_w_in, v_gf_up, v_gf_b, v_gb_up, v_gb_b, v_gla_norm_g, v_attn_norm_g, v_w_out, v_norm2_g, v_w_gate, v_w_up, v_conv_w, v_conv_b, v_w_down, v_final_norm_g):
    w = dict(norm1_g=norm1_g, w_in=w_in, gf_up=gf_up, gf_b=gf_b, gb_up=gb_up, gb_b=gb_b, gla_norm_g=gla_norm_g,
             attn_norm_g=attn_norm_g, w_out=w_out, norm2_g=norm2_g, w_gate=w_gate, w_up=w_up, conv_w=conv_w, conv_b=conv_b,
             w_down=w_down, final_norm_g=final_norm_g)
    m = dict(norm1_g=m_norm1_g, w_in=m_w_in, gf_up=m_gf_up, gf_b=m_gf_b, gb_up=m_gb_up, gb_b=m_gb_b, gla_norm_g=m_gla_norm_g,
             attn_norm_g=m_attn_norm_g, w_out=m_w_out, norm2_g=m_norm2_g, w_gate=m_w_gate, w_up=m_w_up, conv_w=m_conv_w,
             conv_b=m_conv_b, w_down=m_w_down, final_norm_g=m_final_norm_g)
    v = dict(norm1_g=v_norm1_g, w_in=v_w_in, gf_up=v_gf_up, gf_b=v_gf_b, gb_up=v_gb_up, gb_b=v_gb_b, gla_norm_g=v_gla_norm_g,
             attn_norm_g=v_attn_norm_g, w_out=v_w_out, norm2_g=v_norm2_g, w_gate=v_w_gate, w_up=v_w_up, conv_w=v_conv_w,
             conv_b=v_conv_b, w_down=v_w_down, final_norm_g=v_final_norm_g)
    S = x.shape[1]
    chip = 2 * lax.axis_index("x") + lax.axis_index("y")
    n_in = IN_W // N_CHIPS
    n_ff = D_FF // N_CHIPS
    n_gk = GLA_K // N_CHIPS

    def owned(t):
        return {k: (jnp.transpose(t[k][0]) if k == "w_in" else t[k][0]) for k in BIG_NAMES}

    own_w, own_m, own_v = owned(w), owned(m), owned(v)
    shard = {k: own_w[k].astype(BF16) for k in BIG_NAMES}
    small_shard = _pack_rows([gf_up[0], gb_up[0], conv_w[0]])
    small4, w_in4 = _gather_chips_async("gather_w_in", [small_shard, shard["w_in"]], 0)
    w_out4, w_gate4, w_up4 = _gather_chips_async("gather_w_mid", [shard["w_out"], shard["w_gate"], shard["w_up"]], 1)
    (w_down4,) = _gather_chips_async("gather_w_down", [shard["w_down"]], 2)
    w_in_t = w_in4.reshape(IN_W, D_MODEL)
    rows_up = GATE_RANK * n_gk // LANES
    rows_cw = 3 * n_ff // LANES
    gf_full = jnp.transpose(small4[:, 0:rows_up].reshape(N_CHIPS, GATE_RANK, n_gk), (1, 0, 2)).reshape(GATE_RANK, GLA_K)
    gb_full = jnp.transpose(small4[:, rows_up:2 * rows_up].reshape(N_CHIPS, GATE_RANK, n_gk), (1, 0, 2)).reshape(GATE_RANK, GLA_K)
    cw_full = jnp.transpose(small4[:, 2 * rows_up:2 * rows_up + rows_cw].reshape(N_CHIPS, 3, n_ff), (1, 0, 2)).reshape(3, D_FF)
    wg = jnp.zeros((Z_W, 2 * GLA_K), F32)
    wg = wg.at[0:GATE_RANK, 0:GLA_K].set(gf_full).at[GATE_RANK:2 * GATE_RANK, GLA_K:].set(gb_full).astype(BF16)
    gate_bias = jnp.concatenate([gf_b, gb_b], axis=1)

    pending, contributions, next_id = [], {}, [3]

    def as_shards(group, arrays):
        if group == "w_in":
            return dict(w_in=arrays["w_in_t"].reshape(N_CHIPS, n_in, D_MODEL))
        if group == "w_out":
            return dict(w_out=arrays["w_out"].reshape(N_CHIPS, D_MODEL // N_CHIPS, D_MODEL))
        if group == "w_down":
            return dict(w_down=arrays["w_down"].reshape(N_CHIPS, n_ff, D_MODEL))
        return arrays

    out = {}

    def swap(group, arrays):
        mine = as_shards(group, arrays)
        pending.append((group, mine, _sibling_exchange_async(f"sibling_{group}", list(mine.values()), next_id[0])))
        next_id[0] += 1

    def sum_and_send(anchor):
        tag, mine, theirs = pending.pop()
        sums = [_pair_sum(f"pair_sum_{k}", mine[k], _after(t, *anchor)) for k, t in zip(mine, theirs)]
        contributions.update(zip(mine, _scatter_chips_async(f"scatter_{tag}", sums, next_id[0])))
        next_id[0] += 1
        return sums

    def update(names, anchor):
        for k in names:
            res = _adamw(f"adamw_{k}", own_w[k], own_m[k], own_v[k], _after(contributions[k], *anchor))
            out[k] = [(jnp.transpose(r) if k == "w_in" else r)[None] for r in res]
        return [out[k][0] for k in names]

    def on_grad(event, arrays):
        anchor = list(arrays.values())
        held = []
        if event in ("w_gate_w_up", "w_out", "mid", "last"):
            held += sum_and_send(anchor)
        if event == "mid":
            held += update(("w_down", "w_gate", "w_up"), anchor)
        if event == "last":
            held += update(("w_out",), anchor)
        if event in ("w_down", "w_gate_w_up", "w_out", "w_in"):
            swap(event, arrays)
        return held

    grad_x, _, small = _local_step(
        x[0], loss_target[0], norm1_g, w_in_t, wg, gate_bias, gla_norm_g, attn_norm_g,
        w_out4.reshape(D_MODEL, D_MODEL), norm2_g, w_gate4, w_up4, cw_full, conv_b, w_down4.reshape(D_FF, D_MODEL), final_norm_g,
        on_grad=on_grad)
    update(("w_in",), [grad_x])

    d_gf_up = small["wg"][0:GATE_RANK, 0:GLA_K]
    d_gb_up = small["wg"][GATE_RANK:2 * GATE_RANK, GLA_K:]
    pieces = [small["loss"], small["norm1_g"], d_gf_up, small["gate_bias"][:, :GLA_K], d_gb_up, small["gate_bias"][:, GLA_K:],
              small["gla_norm_g"], small["attn_norm_g"], small["norm2_g"], small["conv_w"], small["conv_b"], small["final_norm_g"]]
    total = _allreduce_rows(_pack_rows(pieces))
    summed = _unpack_rows(total, [p.shape for p in pieces])
    loss = summed[0][0, 0]
    g_small = dict(zip(SMALL_NAMES, summed[1:]))
    g_small["gf_up"] = lax.dynamic_slice_in_dim(g_small["gf_up"], chip * n_gk, n_gk, axis=1)
    g_small["gb_up"] = lax.dynamic_slice_in_dim(g_small["gb_up"], chip * n_gk, n_gk, axis=1)
    g_small["conv_w"] = lax.dynamic_slice_in_dim(g_small["conv_w"], chip * n_ff, n_ff, axis=1)
    packed = [_pack_rows([t[k] for k in SMALL_NAMES]) for t in (w, m, v, g_small)]
    res = _adamw("adamw_small", *packed)
    shapes = [w[k].shape for k in SMALL_NAMES]
    for k, vals in zip(SMALL_NAMES, zip(*[_unpack_rows(r, shapes) for r in res])):
        out[k] = list(vals)

    grads, deltas, new_m, new_v = ([out[k][i] for k in WEIGHT_ORDER] for i in range(4))
    return (loss, grad_x[None], *grads, *deltas, *new_m, *new_v)
```

```python
import functools
import math

import jax
import jax.numpy as jnp
from jax import lax
from jax.experimental import pallas as pl
from jax.experimental.pallas import tpu as pltpu
from jax.experimental.pallas import tpu_sc as plsc

F32 = jnp.float32
BF16 = jnp.bfloat16

D_MODEL = 2048
ATTN_W = 1024
HEAD = 128
N_HEADS = 8
N_SIDE = 64
DILATIONS = (1, 4, 16)
ROPE_THETA = 500000.0
ROPE_DIM = 32
GLA_K = 512
GLA_V = 1024
GLA_HEADS = 4
GLA_DK = 128
GLA_DV = 256
GATE_RANK = 16
GATE_NORM = 16.0
CHUNK = 64
IN_MAIN = 6144
IN_W = 6176
Z_W = IN_W - IN_MAIN
D_FF = 5632
EPS = 1e-6
N_CHIPS = 4
N_DEV = 8
LANES = 128

ADAM_LR = 0.001
ADAM_B1 = 0.9
ADAM_B2 = 0.999
ADAM_EPS = 1e-08
ADAM_WD = 0.01
ADAM_STEP = 10

NEG = -1e30
MESH = pl.DeviceIdType.MESH
ANY = pl.BlockSpec(memory_space=pl.ANY)

NN = ((1,), (0,))
NT = ((1,), (1,))
TN = ((0,), (0,))


def _dot(a, b, dims=NN):
    return lax.dot_general(a, b, (dims, ((), ())), preferred_element_type=F32)


def _sigmoid(x):
    return 0.5 * jnp.tanh(0.5 * x) + 0.5


def _after(x, *deps):
    return lax.optimization_barrier((x,) + deps)[0]


def _matmul(name, pairs, grid, out_shape, out_spec, nk, res=None, into=None):
    n_in = 2 * len(pairs) + (res is not None)
    dims = [p[4] for p in pairs]

    n_ops = n_in + (into is not None)

    def body(*refs):
        ins, o_ref = refs[:n_in], refs[n_ops]

        def partial_sum():
            tot = None
            for p, dn in enumerate(dims):
                a, b = ins[2 * p][...], ins[2 * p + 1][...]
                t = _dot(a.astype(BF16), b.astype(BF16), dn)
                tot = t if tot is None else tot + t
            return tot

        if nk == 1:
            t = partial_sum()
            if res is not None:
                t = t + ins[-1][...]
            o_ref[...] = t.astype(o_ref.dtype)
        else:
            acc_ref = refs[n_ops + 1]
            k = pl.program_id(2)

            @pl.when(k == 0)
            def _():
                if res is not None:
                    acc_ref[...] = ins[-1][...]
                else:
                    acc_ref[...] = jnp.zeros_like(acc_ref)

            acc_ref[...] += partial_sum()

            @pl.when(k == nk - 1)
            def _():
                o_ref[...] = acc_ref[...].astype(o_ref.dtype)

    operands, in_specs = [], []
    for a, a_spec, b, b_spec, _ in pairs:
        operands += [a, b]
        in_specs += [a_spec, b_spec]
    if res is not None:
        operands.append(res[0])
        in_specs.append(res[1])
    acc_shape = tuple(s for s in out_spec.block_shape if s is not None)
    scratch = [pltpu.VMEM(acc_shape, F32)] if nk > 1 else []
    aliases = {}
    if into is not None:
        aliases = {len(operands): 0}
        operands.append(into)
        in_specs.append(ANY)
    return pl.pallas_call(
        body, name=name, grid=grid, in_specs=in_specs, out_specs=out_spec, out_shape=out_shape, scratch_shapes=scratch,
        input_output_aliases=aliases,
    )(*operands)


def _mm_nn(name, a, b, tm, tn, out_dtype, res=None):
    M, K = a.shape
    N = b.shape[1]
    pairs = [(a, pl.BlockSpec((tm, K), lambda j, i: (i, 0)), b, pl.BlockSpec((K, tn), lambda j, i: (0, j)), NN)]
    r = None if res is None else (res, pl.BlockSpec((tm, tn), lambda j, i: (i, j)))
    return _matmul(name, pairs, (N // tn, M // tm), jax.ShapeDtypeStruct((M, N), out_dtype),
                   pl.BlockSpec((tm, tn), lambda j, i: (i, j)), 1, r)


def _mm_nn_sharded(name, a, b4, tm, out_dtype):
    M, K = a.shape
    n = b4.shape[2]
    pairs = [(a, pl.BlockSpec((tm, K), lambda j, i: (i, 0)), b4, pl.BlockSpec((None, K, n), lambda j, i: (j, 0, 0)), NN)]
    return _matmul(name, pairs, (N_CHIPS, M // tm), jax.ShapeDtypeStruct((M, N_CHIPS * n), out_dtype),
                   pl.BlockSpec((tm, n), lambda j, i: (i, j)), 1)


def _mm_nt(name, a, b, tm, tn, out_dtype, res=None, n_out=None):
    M, K = a.shape
    N = b.shape[0] if n_out is None else n_out
    pairs = [(a, pl.BlockSpec((tm, K), lambda j, i: (i, 0)), b, pl.BlockSpec((tn, K), lambda j, i: (j, 0)), NT)]
    r = None if res is None else (res, pl.BlockSpec((tm, tn), lambda j, i: (i, j)))
    return _matmul(name, pairs, (N // tn, M // tm), jax.ShapeDtypeStruct((M, N), out_dtype),
                   pl.BlockSpec((tm, tn), lambda j, i: (i, j)), 1, r)


def _mm_tn(name, a, g, tka, tn, tmm, out_dtype, out3=None, rows_out=None):
    M, Ka = a.shape
    N = g.shape[1]
    pairs = [(a, pl.BlockSpec((tmm, tka), lambda i, j, k: (k, i)), g, pl.BlockSpec((tmm, tn), lambda i, j, k: (k, j)), TN)]
    if out3 is None:
        shape, spec = (Ka if rows_out is None else rows_out, N), pl.BlockSpec((tka, tn), lambda i, j, k: (i, j))
    else:
        shape, spec = (N // out3, Ka, out3), pl.BlockSpec((None, tka, tn), lambda i, j, k: (j, i, 0))
    return _matmul(name, pairs, (Ka // tka, N // tn, M // tmm), jax.ShapeDtypeStruct(shape, out_dtype), spec, M // tmm)


def _rms_fwd(name, x, g, tm=512):
    S, D = x.shape

    def body(x_ref, g_ref, o_ref):
        xv = x_ref[...]
        r = lax.rsqrt(jnp.mean(xv * xv, axis=-1, keepdims=True) + EPS)
        o_ref[...] = (xv * r * g_ref[...]).astype(o_ref.dtype)

    return pl.pallas_call(
        body, name=name, grid=(S // tm,),
        in_specs=[pl.BlockSpec((tm, D), lambda i: (i, 0)), pl.BlockSpec((1, D), lambda i: (0, 0))],
        out_specs=pl.BlockSpec((tm, D), lambda i: (i, 0)), out_shape=jax.ShapeDtypeStruct((S, D), BF16),
    )(x, g)


def _rms_bwd(name, x, g, dn, dres, tm=512):
    S, D = x.shape

    def body(x_ref, g_ref, dn_ref, dres_ref, dx_ref, dxb_ref, dg_ref):
        i = pl.program_id(0)

        @pl.when(i == 0)
        def _():
            dg_ref[...] = jnp.zeros_like(dg_ref)

        xv = x_ref[...]
        r = lax.rsqrt(jnp.mean(xv * xv, axis=-1, keepdims=True) + EPS)
        xhat = xv * r
        dnv = dn_ref[...].astype(F32)
        dg_ref[...] += jnp.sum(dnv * xhat, axis=0, keepdims=True)
        t = dnv * g_ref[...]
        dx = r * (t - xhat * jnp.mean(t * xhat, axis=-1, keepdims=True)) + dres_ref[...]
        dx_ref[...] = dx
        dxb_ref[...] = dx.astype(BF16)

    row = pl.BlockSpec((tm, D), lambda i: (i, 0))
    vec = pl.BlockSpec((1, D), lambda i: (0, 0))
    return pl.pallas_call(
        body, name=name, grid=(S // tm,), in_specs=[row, vec, row, row], out_specs=[row, row, vec],
        out_shape=[jax.ShapeDtypeStruct((S, D), F32), jax.ShapeDtypeStruct((S, D), BF16), jax.ShapeDtypeStruct((1, D), F32)],
    )(x, g, dn, dres)


def _final_loss(h2, g, target, tm=512):
    S, D = h2.shape

    def body(x_ref, g_ref, t_ref, loss_ref, dg_ref, dx_ref, dxb_ref):
        i = pl.program_id(0)

        @pl.when(i == 0)
        def _():
            loss_ref[...] = jnp.zeros_like(loss_ref)
            dg_ref[...] = jnp.zeros_like(dg_ref)

        xv = x_ref[...]
        r = lax.rsqrt(jnp.mean(xv * xv, axis=-1, keepdims=True) + EPS)
        xhat = xv * r
        gv = g_ref[...]
        diff = xhat * gv - t_ref[...]
        per_tok = jnp.mean(diff * diff, axis=-1, keepdims=True)
        loss_ref[...] += 0.5 * jnp.sum(per_tok, axis=0, keepdims=True)
        dy = diff * (1.0 / D)
        dg_ref[...] += jnp.sum(dy * xhat, axis=0, keepdims=True)
        t = dy * gv
        dx = r * (t - xhat * jnp.mean(t * xhat, axis=-1, keepdims=True))
        dx_ref[...] = dx
        dxb_ref[...] = dx.astype(BF16)

    row = pl.BlockSpec((tm, D), lambda i: (i, 0))
    vec = pl.BlockSpec((1, D), lambda i: (0, 0))
    return pl.pallas_call(
        body, name="final_loss", grid=(S // tm,), in_specs=[row, vec, row],
        out_specs=[pl.BlockSpec((1, LANES), lambda i: (0, 0)), vec, row, row],
        out_shape=[jax.ShapeDtypeStruct((1, LANES), F32), jax.ShapeDtypeStruct((1, D), F32),
                   jax.ShapeDtypeStruct((S, D), F32), jax.ShapeDtypeStruct((S, D), BF16)],
    )(h2, g, target)


def _rope_tables(S):
    pos = jnp.arange(S, dtype=F32)
    inv_freq = ROPE_THETA ** (-jnp.arange(0, ROPE_DIM, 2, dtype=F32) / ROPE_DIM)
    ang = pos[:, None] * inv_freq[None, :]
    cos, sin = jnp.cos(ang), jnp.sin(ang)
    half = ROPE_DIM // 2
    rest = HEAD - ROPE_DIM
    z_h, z_r = jnp.zeros((S, half), F32), jnp.zeros((S, rest), F32)
    tab_c = jnp.concatenate([cos, cos, jnp.ones((S, rest), F32)], axis=1)
    tab_up = jnp.concatenate([z_h, sin, z_r], axis=1)
    tab_dn = jnp.concatenate([-sin, z_h, z_r], axis=1)
    return tab_c, tab_up, tab_dn


def _rope_head(t, c, up, dn):
    half = ROPE_DIM // 2
    return t * c + pltpu.roll(t, half, axis=1) * up + pltpu.roll(t, HEAD - half, axis=1) * dn


def _rope_fwd(proj, tabs, tm=512):
    S = proj.shape[0]
    W = 2 * ATTN_W

    def body(p_ref, c_ref, up_ref, dn_ref, o_ref):
        c, up, dn = c_ref[...], up_ref[...], dn_ref[...]
        for h in range(W // HEAD):
            sl = slice(h * HEAD, (h + 1) * HEAD)
            o_ref[:, sl] = _rope_head(p_ref[:, sl].astype(F32), c, up, dn).astype(BF16)

    tab = pl.BlockSpec((tm, HEAD), lambda i: (i, 0))
    return pl.pallas_call(
        body, name="rope_fwd", grid=(S // tm,), in_specs=[pl.BlockSpec((tm, W), lambda i: (i, 0)), tab, tab, tab],
        out_specs=pl.BlockSpec((tm, W), lambda i: (i, 0)), out_shape=jax.ShapeDtypeStruct((S, W), BF16),
    )(proj, *tabs)


def _attn_grad_merge(dqs, dks, dvs, tabs, tm=256):
    S = dqs[0].shape[0]

    def body(*refs):
        q_refs, k_refs, v_refs = refs[0:3], refs[3:6], refs[6:9]
        c, up, dn = refs[9][...], refs[10][...], refs[11][...]
        o_ref = refs[12]
        for h in range(N_HEADS):
            sl = slice(h * HEAD, (h + 1) * HEAD)
            for part, rs in ((0, q_refs), (1, k_refs)):
                t = rs[0][:, sl].astype(F32) + rs[1][:, sl].astype(F32) + rs[2][:, sl].astype(F32)
                osl = slice(part * ATTN_W + h * HEAD, part * ATTN_W + (h + 1) * HEAD)
                o_ref[:, osl] = _rope_head(t, c, -up, -dn).astype(BF16)
        o_ref[:, 2 * ATTN_W:] = (v_refs[0][...].astype(F32) + v_refs[1][...].astype(F32)
                                 + v_refs[2][...].astype(F32)).astype(BF16)

    blk = pl.BlockSpec((tm, ATTN_W), lambda i: (i, 0))
    tab = pl.BlockSpec((tm, HEAD), lambda i: (i, 0))
    return pl.pallas_call(
        body, name="attn_grad_merge", grid=(S // tm,), in_specs=[blk] * 9 + [tab] * 3,
        out_specs=pl.BlockSpec((tm, 3 * ATTN_W), lambda i: (i, 0)), out_shape=jax.ShapeDtypeStruct((S, 3 * ATTN_W), BF16),
    )(*dqs, *dks, *dvs, *tabs)


SUB = 128
Q_COL, K_COL, V_COL = 0, ATTN_W // HEAD, 2 * ATTN_W // HEAD


class _AttnGeo:
    def __init__(self, S, d):
        self.S, self.d, self.L = S, d, S // d
        self.halo = N_SIDE * d
        self.TB = min(2048, S)
        self.W = self.TB + 2 * self.halo
        self.n_sub = self.TB // SUB
        self.grid = (S // self.TB, N_HEADS)
        self.dt = F32 if d > 1 else BF16
        self.su = min(d, 4)
        self.sb = d // self.su
        assert self.TB % (SUB * d) == 0 and self.TB % self.halo == 0

    def specs(self, width, col0, per_head=True):
        ratio = self.TB // self.halo
        last = self.S // self.halo - 1
        col = (lambda h: col0 + h) if per_head else (lambda h: col0)
        cur = pl.BlockSpec((self.TB, width), lambda i, h: (i, col(h)))
        prev = pl.BlockSpec((self.halo, width), lambda i, h: (jnp.maximum(i * ratio - 1, 0), col(h)))
        nxt = pl.BlockSpec((self.halo, width), lambda i, h: (jnp.minimum((i + 1) * ratio, last), col(h)))
        return cur, prev, nxt

    def scratch(self, rows, dtype=None):
        nat = pltpu.VMEM((rows, LANES), self.dt if dtype is None else dtype)
        return [nat] if self.sb == 1 else [nat, pltpu.VMEM((rows, LANES), F32)]

    def bind(self, refs):
        nat = next(refs)
        return (nat, nat) if self.sb == 1 else (nat, next(refs))

    def spread(self, pair):
        nat, streams = pair
        if self.sb > 1:
            n = nat.shape[0] // self.sb
            for a in range(self.sb):
                streams[a * n:(a + 1) * n, :] = nat[pl.ds(a, n, stride=self.sb), :]
        return streams

    def gather(self, pair):
        nat, streams = pair
        if self.sb > 1:
            n = nat.shape[0] // self.sb
            for a in range(self.sb):
                nat[pl.ds(a, n, stride=self.sb), :] = streams[a * n:(a + 1) * n, :]
        return nat

    def rows(self, sub, n, total):
        res, blk = sub % self.d, sub // self.d
        a, b = res % self.sb, res // self.sb
        start = a * (total // self.sb) + b + self.su * SUB * blk
        return pl.ds(start, n, stride=self.su) if self.su > 1 else pl.ds(start, n)

    def mask(self, sub):
        base = pl.program_id(0) * (self.TB // self.d) + SUB * (sub // self.d)
        row = lax.broadcasted_iota(jnp.int32, (SUB, 2 * SUB), 0)
        col = lax.broadcasted_iota(jnp.int32, (SUB, 2 * SUB), 1)
        pos = base - N_SIDE + col
        return (col >= row) & (col <= row + 2 * N_SIDE) & (pos >= 0) & (pos < self.L)

    def fill(self, dst, c_ref):
        dst[...] = c_ref[...].astype(dst.dtype)

    def fill_window(self, dst, p_ref, c_ref, n_ref):
        dst[0:self.halo] = p_ref[...].astype(dst.dtype)
        dst[self.halo:self.halo + self.TB] = c_ref[...].astype(dst.dtype)
        dst[self.halo + self.TB:] = n_ref[...].astype(dst.dtype)


def _lane_of(tile, h):
    lane = lax.broadcasted_iota(jnp.int32, tile.shape, 1)
    return jnp.sum(jnp.where(lane == h, tile, 0.0), axis=1, keepdims=True)


def _attn_fwd(qk, proj, d):
    S = qk.shape[0]
    geo = _AttnGeo(S, d)
    scale = HEAD ** -0.5

    def body(q_ref, kp, kc, kn, vp, vc, vn, o_ref, lse_ref, *scratch):
        h = pl.program_id(1)
        refs = iter(scratch)
        q_p, k_p, v_p, o_p, l_p = (geo.bind(refs) for _ in range(5))
        geo.fill(q_p[0], q_ref)
        geo.fill_window(k_p[0], kp, kc, kn)
        geo.fill_window(v_p[0], vp, vc, vn)
        qs, ks, vs = geo.spread(q_p), geo.spread(k_p), geo.spread(v_p)
        os, ls = o_p[1], l_p[1]
        for sub in range(geo.n_sub):
            rq, rw = geo.rows(sub, SUB, geo.TB), geo.rows(sub, 2 * SUB, geo.W)
            q_r, k_r, v_r = qs[rq, :].astype(BF16), ks[rw, :].astype(BF16), vs[rw, :].astype(BF16)
            s = jnp.where(geo.mask(sub), _dot(q_r, k_r, NT) * scale, NEG)
            m = jnp.max(s, axis=1, keepdims=True)
            p = jnp.exp(s - m)
            l = jnp.sum(p, axis=1, keepdims=True)
            os[rq, :] = _dot(p.astype(BF16), v_r) / l
            ls[rq, :] = jnp.broadcast_to(m + jnp.log(l), (SUB, LANES))
        o_ref[...] = geo.gather(o_p)[...].astype(BF16)

        @pl.when(h == 0)
        def _():
            lse_ref[...] = jnp.zeros_like(lse_ref)

        lane = lax.broadcasted_iota(jnp.int32, (geo.TB, LANES), 1)
        lse_ref[...] = jnp.where(lane == h, geo.gather(l_p)[...], lse_ref[...])

    q_cur, _, _ = geo.specs(HEAD, Q_COL)
    k_specs = geo.specs(HEAD, K_COL)
    v_specs = geo.specs(HEAD, V_COL)
    stat = pl.BlockSpec((geo.TB, LANES), lambda i, h: (i, 0))
    return pl.pallas_call(
        body, name=f"attn_fwd_d{d}", grid=geo.grid,
        in_specs=[q_cur, k_specs[1], k_specs[0], k_specs[2], v_specs[1], v_specs[0], v_specs[2]],
        out_specs=[q_cur, stat],
        out_shape=[jax.ShapeDtypeStruct((S, ATTN_W), BF16), jax.ShapeDtypeStruct((S, LANES), F32)],
        scratch_shapes=(geo.scratch(geo.TB) + geo.scratch(geo.W) + geo.scratch(geo.W) + geo.scratch(geo.TB, F32)
                        + geo.scratch(geo.TB, F32)),
    )(qk, qk, qk, qk, proj, proj, proj)


def _attn_combine(outs, lses, g, tm=256):
    S = outs[0].shape[0]

    def body(o1, o2, o3, l1, l2, l3, g_ref, ao_ref, o_ref, lse_ref):
        lane = lax.broadcasted_iota(jnp.int32, (tm, LANES), 1)
        lse_tile = jnp.zeros((tm, LANES), F32)
        a = [l1[...], l2[...], l3[...]]
        ssq = jnp.zeros((tm, 1), F32)
        for h in range(N_HEADS):
            sl = slice(h * HEAD, (h + 1) * HEAD)
            a1, a2, a3 = (t[:, h:h + 1] for t in a)
            mx = jnp.maximum(jnp.maximum(a1, a2), a3)
            e1, e2, e3 = jnp.exp(a1 - mx), jnp.exp(a2 - mx), jnp.exp(a3 - mx)
            den = e1 + e2 + e3
            oh = (e1 * o1[:, sl].astype(F32) + e2 * o2[:, sl].astype(F32) + e3 * o3[:, sl].astype(F32)) / den
            o_ref[:, sl] = oh
            ssq = ssq + jnp.sum(oh * oh, axis=1, keepdims=True)
            lse_tile = jnp.where(lane == h, mx + jnp.log(den), lse_tile)
        lse_ref[...] = lse_tile
        r = lax.rsqrt(ssq * (1.0 / ATTN_W) + EPS)
        ao_ref[...] = (o_ref[...] * r * g_ref[...]).astype(BF16)

    blk = pl.BlockSpec((tm, ATTN_W), lambda i: (i, 0))
    ls = pl.BlockSpec((tm, LANES), lambda i: (i, 0))
    return pl.pallas_call(
        body, name="attn_combine", grid=(S // tm,),
        in_specs=[blk, blk, blk, ls, ls, ls, pl.BlockSpec((1, ATTN_W), lambda i: (0, 0))], out_specs=[blk, blk, ls],
        out_shape=[jax.ShapeDtypeStruct((S, ATTN_W), BF16), jax.ShapeDtypeStruct((S, ATTN_W), F32),
                   jax.ShapeDtypeStruct((S, LANES), F32)],
    )(*outs, *lses, g)


def _attn_norm_bwd(o, g, dao, tm=256):
    S = o.shape[0]

    def body(o_ref, g_ref, dao_ref, do_ref, dl_ref, dg_ref):
        i = pl.program_id(0)

        @pl.when(i == 0)
        def _():
            dg_ref[...] = jnp.zeros_like(dg_ref)

        ov = o_ref[...]
        r = lax.rsqrt(jnp.mean(ov * ov, axis=-1, keepdims=True) + EPS)
        ohat = ov * r
        dn = dao_ref[...].astype(F32)
        dg_ref[...] += jnp.sum(dn * ohat, axis=0, keepdims=True)
        t = dn * g_ref[...]
        do = r * (t - ohat * jnp.mean(t * ohat, axis=-1, keepdims=True))
        do_ref[...] = do.astype(BF16)
        prod = do * ov
        lane = lax.broadcasted_iota(jnp.int32, (tm, LANES), 1)
        tile = jnp.zeros((tm, LANES), F32)
        for h in range(N_HEADS):
            tile = jnp.where(lane == h, jnp.sum(prod[:, h * HEAD:(h + 1) * HEAD], axis=1, keepdims=True), tile)
        dl_ref[...] = tile

    blk = pl.BlockSpec((tm, ATTN_W), lambda i: (i, 0))
    vec = pl.BlockSpec((1, ATTN_W), lambda i: (0, 0))
    return pl.pallas_call(
        body, name="attn_norm_bwd", grid=(S // tm,),
        in_specs=[blk, vec, pl.BlockSpec((tm, ATTN_W), lambda i: (i, 0))],
        out_specs=[blk, pl.BlockSpec((tm, LANES), lambda i: (i, 0)), vec],
        out_shape=[jax.ShapeDtypeStruct((S, ATTN_W), BF16), jax.ShapeDtypeStruct((S, LANES), F32),
                   jax.ShapeDtypeStruct((1, ATTN_W), F32)],
    )(o, g, dao)


def _attn_bwd_dq(qk, proj, do, lse, delta, d):
    S = qk.shape[0]
    geo = _AttnGeo(S, d)
    scale = HEAD ** -0.5

    def body(q_ref, kp, kc, kn, vp, vc, vn, do_ref, lse_ref, dl_ref, dq_ref, *scratch):
        h = pl.program_id(1)
        refs = iter(scratch)
        q_p, k_p, v_p, do_p, lse_p, dl_p, dq_p = (geo.bind(refs) for _ in range(7))
        geo.fill(q_p[0], q_ref)
        geo.fill(do_p[0], do_ref)
        geo.fill(lse_p[0], lse_ref)
        geo.fill(dl_p[0], dl_ref)
        geo.fill_window(k_p[0], kp, kc, kn)
        geo.fill_window(v_p[0], vp, vc, vn)
        qs, ks, vs, dos = geo.spread(q_p), geo.spread(k_p), geo.spread(v_p), geo.spread(do_p)
        lses, dls = geo.spread(lse_p), geo.spread(dl_p)
        dqs = dq_p[1]
        for sub in range(geo.n_sub):
            rq, rw = geo.rows(sub, SUB, geo.TB), geo.rows(sub, 2 * SUB, geo.W)
            q_r, k_r, v_r = qs[rq, :].astype(BF16), ks[rw, :].astype(BF16), vs[rw, :].astype(BF16)
            lse_c, dl_c = _lane_of(lses[rq, :], h), _lane_of(dls[rq, :], h)
            s = _dot(q_r, k_r, NT) * scale
            p = jnp.where(geo.mask(sub), jnp.exp(s - lse_c), 0.0)
            dp = _dot(dos[rq, :].astype(BF16), v_r, NT)
            ds = (p * (dp - dl_c) * scale).astype(BF16)
            dqs[rq, :] = _dot(ds, k_r)
        dq_ref[...] = geo.gather(dq_p)[...].astype(BF16)

    cur, _, _ = geo.specs(HEAD, 0)
    k_specs = geo.specs(HEAD, K_COL)
    v_specs = geo.specs(HEAD, V_COL)
    stat = pl.BlockSpec((geo.TB, LANES), lambda i, h: (i, 0))
    return pl.pallas_call(
        body, name=f"attn_bwd_dq_d{d}", grid=geo.grid,
        in_specs=[cur, k_specs[1], k_specs[0], k_specs[2], v_specs[1], v_specs[0], v_specs[2], cur, stat, stat],
        out_specs=cur, out_shape=jax.ShapeDtypeStruct((S, ATTN_W), BF16),
        scratch_shapes=(geo.scratch(geo.TB) + geo.scratch(geo.W) + geo.scratch(geo.W) + geo.scratch(geo.TB)
                        + geo.scratch(geo.TB, F32) + geo.scratch(geo.TB, F32) + geo.scratch(geo.TB, F32)),
    )(qk, qk, qk, qk, proj, proj, proj, do, lse, delta)


def _attn_bwd_dkv(qk, proj, do, lse, delta, d):
    S = qk.shape[0]
    geo = _AttnGeo(S, d)
    scale = HEAD ** -0.5

    def body(k_ref, v_ref, qp, qc, qn, dop, doc, don, lp, lc, ln, dlp, dlc, dln, dk_ref, dv_ref, *scratch):
        h = pl.program_id(1)
        refs = iter(scratch)
        k_p, v_p, q_p, do_p, lw_p, dlw_p, dk_p, dv_p = (geo.bind(refs) for _ in range(8))
        geo.fill(k_p[0], k_ref)
        geo.fill(v_p[0], v_ref)
        geo.fill_window(q_p[0], qp, qc, qn)
        geo.fill_window(do_p[0], dop, doc, don)
        geo.fill_window(lw_p[0], lp, lc, ln)
        geo.fill_window(dlw_p[0], dlp, dlc, dln)
        ks, vs, qs, dos = geo.spread(k_p), geo.spread(v_p), geo.spread(q_p), geo.spread(do_p)
        lws, dlws = geo.spread(lw_p), geo.spread(dlw_p)
        dks, dvs = dk_p[1], dv_p[1]
        head = lax.broadcasted_iota(jnp.int32, (LANES, 2 * SUB), 0)
        for sub in range(geo.n_sub):
            rq, rw = geo.rows(sub, SUB, geo.TB), geo.rows(sub, 2 * SUB, geo.W)
            k_r, v_r = ks[rq, :].astype(BF16), vs[rq, :].astype(BF16)
            q_w, do_w = qs[rw, :].astype(BF16), dos[rw, :].astype(BF16)
            lse_row = jnp.sum(jnp.where(head == h, lws[rw, :].T, 0.0), axis=0, keepdims=True)
            dl_row = jnp.sum(jnp.where(head == h, dlws[rw, :].T, 0.0), axis=0, keepdims=True)
            st = _dot(k_r, q_w, NT) * scale
            pt = jnp.where(geo.mask(sub), jnp.exp(st - lse_row), 0.0)
            dvs[rq, :] = _dot(pt.astype(BF16), do_w)
            dpt = _dot(v_r, do_w, NT)
            dst = (pt * (dpt - dl_row) * scale).astype(BF16)
            dks[rq, :] = _dot(dst, q_w)
        dk_ref[...] = geo.gather(dk_p)[...].astype(BF16)
        dv_ref[...] = geo.gather(dv_p)[...].astype(BF16)

    q_specs = geo.specs(HEAD, Q_COL)
    k_cur, _, _ = geo.specs(HEAD, K_COL)
    v_cur, _, _ = geo.specs(HEAD, V_COL)
    do_specs = geo.specs(HEAD, 0)
    st_specs = geo.specs(LANES, 0, per_head=False)
    cur = do_specs[0]
    return pl.pallas_call(
        body, name=f"attn_bwd_dkv_d{d}", grid=geo.grid,
        in_specs=[k_cur, v_cur, q_specs[1], q_specs[0], q_specs[2], do_specs[1], do_specs[0], do_specs[2],
                  st_specs[1], st_specs[0], st_specs[2], st_specs[1], st_specs[0], st_specs[2]],
        out_specs=[cur, cur],
        out_shape=[jax.ShapeDtypeStruct((S, ATTN_W), BF16), jax.ShapeDtypeStruct((S, ATTN_W), BF16)],
        scratch_shapes=(geo.scratch(geo.TB) + geo.scratch(geo.TB) + geo.scratch(geo.W) + geo.scratch(geo.W)
                        + geo.scratch(geo.W, F32) + geo.scratch(geo.W, F32) + geo.scratch(geo.TB, F32)
                        + geo.scratch(geo.TB, F32)),
    )(qk, proj, qk, qk, qk, do, do, do, lse, lse, lse, delta, delta, delta)


def _cumsum_rows(x, reverse):
    n = x.shape[0]
    row = lax.broadcasted_iota(jnp.int32, x.shape, 0)
    s = 1
    while s < n:
        if reverse:
            x = x + jnp.where(row < n - s, pltpu.roll(x, n - s, axis=0), 0.0)
        else:
            x = x + jnp.where(row >= s, pltpu.roll(x, s, axis=0), 0.0)
        s *= 2
    return x


GLA_GROUP = 2


def _gla_rows(cc):
    return slice(cc * CHUNK, (cc + 1) * CHUNK)


def _gla_chunk_terms(q_ref, k_ref, v_ref, g_ref, h, reverse, rows):
    ksl = slice(h * GLA_DK, (h + 1) * GLA_DK)
    q = q_ref[rows, ksl].astype(F32) * (GLA_DK ** -0.5)
    k = k_ref[rows, ksl].astype(F32)
    v = v_ref[rows, h * GLA_DV:(h + 1) * GLA_DV]
    b = _cumsum_rows(g_ref[rows, ksl], reverse)
    r_ref = CHUNK // 2 if reverse else CHUNK // 2 - 1
    r_last = 0 if reverse else CHUNK - 1
    b_ref, b_last = b[r_ref:r_ref + 1, :], b[r_last:r_last + 1, :]
    ii = lax.broadcasted_iota(jnp.int32, (CHUNK, CHUNK), 0)
    jj = lax.broadcasted_iota(jnp.int32, (CHUNK, CHUNK), 1)
    causal = (jj >= ii) if reverse else (jj <= ii)
    e_q, e_k = jnp.exp(b - b_ref), jnp.exp(b_ref - b)
    e_in, e_st = jnp.exp(b), jnp.exp(b_last - b)
    return dict(q=q, k=k, v=v, b=b, causal=causal, e_q=e_q, e_k=e_k, e_in=e_in, e_st=e_st, dec=jnp.exp(b_last),
                qe=q * e_q, ke=k * e_k, q_in=q * e_in, k_st=k * e_st, r_ref=r_ref, r_last=r_last)


def _gla_specs(order):
    rows = GLA_GROUP * CHUNK
    q = pl.BlockSpec((rows, GLA_K), lambda c: (order(c), 3 * ATTN_W // GLA_K))
    k = pl.BlockSpec((rows, GLA_K), lambda c: (order(c), 3 * ATTN_W // GLA_K + 1))
    v = pl.BlockSpec((rows, GLA_V), lambda c: (order(c), (3 * ATTN_W + 2 * GLA_K) // GLA_V))
    return q, k, v


def _gla_fwd(proj, gates, reverse, o_prev=None):
    S = proj.shape[0]
    n = S // CHUNK
    nb = n // GLA_GROUP
    rows = GLA_GROUP * CHUNK
    order = (lambda c: nb - 1 - c) if reverse else (lambda c: c)
    seq = list(range(GLA_GROUP))[::-1] if reverse else list(range(GLA_GROUP))
    gcol = 1 if reverse else 0

    def body(*refs):
        if o_prev is None:
            q_ref, k_ref, v_ref, g_ref, o_ref, st_ref, state = refs
        else:
            q_ref, k_ref, v_ref, g_ref, op_ref, o_ref, st_ref, state = refs
        c = pl.program_id(0)

        @pl.when(c == 0)
        def _():
            state[...] = jnp.zeros_like(state)

        for h in range(GLA_HEADS):
            vsl = slice(h * GLA_DV, (h + 1) * GLA_DV)
            st = state[h]
            for cc in seq:
                rs = _gla_rows(cc)
                t = _gla_chunk_terms(q_ref, k_ref, v_ref, g_ref, h, reverse, rs)
                a = jnp.where(t["causal"], _dot(t["qe"].astype(BF16), t["ke"].astype(BF16), NT), 0.0)
                o = _dot(a.astype(BF16), t["v"])
                st_b = st.astype(BF16)
                st_ref[cc, h] = st_b
                o = o + _dot(t["q_in"].astype(BF16), st_b, NT)
                st = st * t["dec"] + _dot(t["v"], t["k_st"].astype(BF16), TN)
                if o_prev is not None:
                    o = o + op_ref[rs, vsl]
                o_ref[rs, vsl] = o
            state[h] = st

    q_spec, k_spec, v_spec = _gla_specs(order)
    o_spec = pl.BlockSpec((rows, GLA_V), lambda c: (order(c), 0))
    in_specs = [q_spec, k_spec, v_spec, pl.BlockSpec((rows, GLA_K), lambda c: (order(c), gcol))]
    operands = [proj, proj, proj, gates]
    if o_prev is not None:
        in_specs.append(o_spec)
        operands.append(o_prev)
    return pl.pallas_call(
        body, name="gla_fwd_rev" if reverse else "gla_fwd", grid=(nb,), in_specs=in_specs,
        out_specs=[o_spec, pl.BlockSpec((GLA_GROUP, GLA_HEADS, GLA_DV, GLA_DK), lambda c: (order(c), 0, 0, 0))],
        out_shape=[jax.ShapeDtypeStruct((S, GLA_V), F32), jax.ShapeDtypeStruct((n, GLA_HEADS, GLA_DV, GLA_DK), BF16)],
        scratch_shapes=[pltpu.VMEM((GLA_HEADS, GLA_DV, GLA_DK), F32)],
    )(*operands)


def _gla_bwd(proj, gates, states, do, reverse, prev=None):
    S = proj.shape[0]
    n = S // CHUNK
    nb = n // GLA_GROUP
    rows = GLA_GROUP * CHUNK
    order = (lambda c: c) if reverse else (lambda c: nb - 1 - c)
    seq = list(range(GLA_GROUP)) if reverse else list(range(GLA_GROUP))[::-1]
    gcol = 1 if reverse else 0
    out_dt = F32 if prev is None else BF16

    def body(*refs):
        if prev is None:
            q_ref, k_ref, v_ref, g_ref, st_ref, do_ref, dq_ref, dk_ref, dv_ref, dg_ref, dstate = refs
        else:
            q_ref, k_ref, v_ref, g_ref, st_ref, do_ref, pq, pk, pv, dq_ref, dk_ref, dv_ref, dg_ref, dstate = refs
        c = pl.program_id(0)

        @pl.when(c == 0)
        def _():
            dstate[...] = jnp.zeros_like(dstate)

        row = lax.broadcasted_iota(jnp.int32, (CHUNK, GLA_DK), 0)
        for h in range(GLA_HEADS):
            ksl = slice(h * GLA_DK, (h + 1) * GLA_DK)
            vsl = slice(h * GLA_DV, (h + 1) * GLA_DV)
            dst = dstate[h]
            for cc in seq:
                rs = _gla_rows(cc)
                t = _gla_chunk_terms(q_ref, k_ref, v_ref, g_ref, h, reverse, rs)
                v = t["v"]
                dob = do_ref[rs, vsl].astype(BF16)
                st_b = st_ref[cc, h]
                dst_b = dst.astype(BF16)
                qe_b, ke_b = t["qe"].astype(BF16), t["ke"].astype(BF16)
                q_in_b, k_st_b = t["q_in"].astype(BF16), t["k_st"].astype(BF16)
                a = jnp.where(t["causal"], _dot(qe_b, ke_b, NT), 0.0)
                da = jnp.where(t["causal"], _dot(dob, v, NT), 0.0).astype(BF16)
                dv = _dot(a.astype(BF16), dob, TN) + _dot(k_st_b, dst_b, NT)
                dqe = _dot(da, ke_b)
                dke = _dot(da, qe_b, TN)
                dq_in = _dot(dob, st_b)
                dk_st = _dot(v, dst_b)
                ddec = jnp.sum(dst * st_b.astype(F32), axis=0, keepdims=True)
                dst = _dot(dob, q_in_b, TN) + dst * t["dec"]
                dq = (dqe * t["e_q"] + dq_in * t["e_in"]) * (GLA_DK ** -0.5)
                dk = dke * t["e_k"] + dk_st * t["e_st"]
                w_q, w_k = dqe * t["qe"], dke * t["ke"]
                w_st = dk_st * t["k_st"]
                db = w_q - w_k + dq_in * t["q_in"] - w_st
                db_ref = jnp.sum(w_k - w_q, axis=0, keepdims=True)
                db_last = jnp.sum(w_st, axis=0, keepdims=True) + ddec * t["dec"]
                db = db + jnp.where(row == t["r_ref"], db_ref, 0.0) + jnp.where(row == t["r_last"], db_last, 0.0)
                dg_ref[rs, ksl] = _cumsum_rows(db, not reverse)
                if prev is not None:
                    dq, dk, dv = dq + pq[rs, ksl], dk + pk[rs, ksl], dv + pv[rs, vsl]
                dq_ref[rs, ksl] = dq.astype(out_dt)
                dk_ref[rs, ksl] = dk.astype(out_dt)
                dv_ref[rs, vsl] = dv.astype(out_dt)
            dstate[h] = dst

    q_spec, k_spec, v_spec = _gla_specs(order)
    kk = pl.BlockSpec((rows, GLA_K), lambda c: (order(c), 0))
    vv = pl.BlockSpec((rows, GLA_V), lambda c: (order(c), 0))
    in_specs = [q_spec, k_spec, v_spec, pl.BlockSpec((rows, GLA_K), lambda c: (order(c), gcol)),
                pl.BlockSpec((GLA_GROUP, GLA_HEADS, GLA_DV, GLA_DK), lambda c: (order(c), 0, 0, 0)), vv]
    operands = [proj, proj, proj, gates, states, do]
    if prev is not None:
        in_specs += [kk, kk, vv]
        operands += list(prev)
    return pl.pallas_call(
        body, name="gla_bwd_rev" if reverse else "gla_bwd", grid=(nb,), in_specs=in_specs, out_specs=[kk, kk, vv, kk],
        out_shape=[jax.ShapeDtypeStruct((S, GLA_K), out_dt), jax.ShapeDtypeStruct((S, GLA_K), out_dt),
                   jax.ShapeDtypeStruct((S, GLA_V), out_dt), jax.ShapeDtypeStruct((S, GLA_K), F32)],
        scratch_shapes=[pltpu.VMEM((GLA_HEADS, GLA_DV, GLA_DK), F32)],
    )(*operands)


def _gates_fwd(z, wg, bias, tm=512):
    S = z.shape[0]
    W = 2 * GLA_K

    def body(z_ref, w_ref, b_ref, o_ref):
        zg = _dot(z_ref[...], w_ref[...]) + b_ref[...]
        o_ref[...] = (jnp.minimum(zg, 0.0) - jnp.log(1.0 + jnp.exp(-jnp.abs(zg)))) * (1.0 / GATE_NORM)

    return pl.pallas_call(
        body, name="gates_fwd", grid=(S // tm,),
        in_specs=[pl.BlockSpec((tm, Z_W), lambda i: (i, 0)), pl.BlockSpec((Z_W, W), lambda i: (0, 0)),
                  pl.BlockSpec((1, W), lambda i: (0, 0))],
        out_specs=pl.BlockSpec((tm, W), lambda i: (i, 0)), out_shape=jax.ShapeDtypeStruct((S, W), F32),
    )(z, wg, bias)


def _gates_bwd(z, wg, bias, dg_f, dg_b, tm=512):
    S = z.shape[0]
    W = 2 * GLA_K

    def body(z_ref, w_ref, b_ref, dgf_ref, dgb_ref, dz_ref, dw_ref, db_ref):
        i = pl.program_id(0)

        @pl.when(i == 0)
        def _():
            dw_ref[...] = jnp.zeros_like(dw_ref)
            db_ref[...] = jnp.zeros_like(db_ref)

        zv = z_ref[...]
        zg = _dot(zv, w_ref[...]) + b_ref[...]
        dg = jnp.concatenate([dgf_ref[...], dgb_ref[...]], axis=1)
        dzg = dg * (1.0 / GATE_NORM) * _sigmoid(-zg)
        db_ref[...] += jnp.sum(dzg, axis=0, keepdims=True)
        dzg_b = dzg.astype(BF16)
        dw_ref[...] += _dot(zv, dzg_b, TN)
        dz_ref[...] = _dot(dzg_b, w_ref[...], NT).astype(BF16)

    half = pl.BlockSpec((tm, GLA_K), lambda i: (i, 0))
    return pl.pallas_call(
        body, name="gates_bwd", grid=(S // tm,),
        in_specs=[pl.BlockSpec((tm, Z_W), lambda i: (i, 0)), pl.BlockSpec((Z_W, W), lambda i: (0, 0)),
                  pl.BlockSpec((1, W), lambda i: (0, 0)), half, half],
        out_specs=[pl.BlockSpec((tm, Z_W), lambda i: (i, 0)), pl.BlockSpec((Z_W, W), lambda i: (0, 0)),
                   pl.BlockSpec((1, W), lambda i: (0, 0))],
        out_shape=[jax.ShapeDtypeStruct((S, Z_W), BF16), jax.ShapeDtypeStruct((Z_W, W), F32),
                   jax.ShapeDtypeStruct((1, W), F32)],
    )(z, wg, bias, dg_f, dg_b)


def _gla_out_fwd(o, proj, g, tm=512):
    S = o.shape[0]

    def body(o_ref, gr_ref, g_ref, out_ref):
        gn = g_ref[...]
        for h in range(GLA_HEADS):
            sl = slice(h * GLA_DV, (h + 1) * GLA_DV)
            ov = o_ref[:, sl]
            r = lax.rsqrt(jnp.mean(ov * ov, axis=-1, keepdims=True) + EPS)
            gr = gr_ref[:, sl].astype(F32)
            out_ref[:, sl] = (ov * r * gn * (gr * _sigmoid(gr))).astype(BF16)

    blk = pl.BlockSpec((tm, GLA_V), lambda i: (i, 0))
    return pl.pallas_call(
        body, name="gla_out_fwd", grid=(S // tm,),
        in_specs=[blk, pl.BlockSpec((tm, GLA_V), lambda i: (i, (3 * ATTN_W + 2 * GLA_K + GLA_V) // GLA_V)),
                  pl.BlockSpec((1, GLA_DV), lambda i: (0, 0))],
        out_specs=blk, out_shape=jax.ShapeDtypeStruct((S, GLA_V), BF16),
    )(o, proj, g)


def _gla_out_bwd(o, proj, g, dcat, tm=512):
    S = o.shape[0]

    def body(o_ref, gr_ref, g_ref, dgo_ref, do_ref, dgr_ref, dg_ref):
        i = pl.program_id(0)

        @pl.when(i == 0)
        def _():
            dg_ref[...] = jnp.zeros_like(dg_ref)

        gn = g_ref[...]
        dg_acc = jnp.zeros((1, GLA_DV), F32)
        for h in range(GLA_HEADS):
            sl = slice(h * GLA_DV, (h + 1) * GLA_DV)
            ov = o_ref[:, sl]
            r = lax.rsqrt(jnp.mean(ov * ov, axis=-1, keepdims=True) + EPS)
            yhat = ov * r
            gr = gr_ref[:, sl].astype(F32)
            sg = _sigmoid(gr)
            dgo = dgo_ref[:, sl].astype(F32)
            dgr_ref[:, sl] = (dgo * (yhat * gn) * (sg * (1.0 + gr * (1.0 - sg)))).astype(BF16)
            dy = dgo * (gr * sg)
            dg_acc = dg_acc + jnp.sum(dy * yhat, axis=0, keepdims=True)
            t = dy * gn
            do_ref[:, sl] = r * (t - yhat * jnp.mean(t * yhat, axis=-1, keepdims=True))
        dg_ref[...] += dg_acc

    blk = pl.BlockSpec((tm, GLA_V), lambda i: (i, 0))
    vec = pl.BlockSpec((1, GLA_DV), lambda i: (0, 0))
    return pl.pallas_call(
        body, name="gla_out_bwd", grid=(S // tm,),
        in_specs=[blk, pl.BlockSpec((tm, GLA_V), lambda i: (i, (3 * ATTN_W + 2 * GLA_K + GLA_V) // GLA_V)), vec,
                  pl.BlockSpec((tm, GLA_V), lambda i: (i, 1))],
        out_specs=[blk, blk, vec],
        out_shape=[jax.ShapeDtypeStruct((S, GLA_V), F32), jax.ShapeDtypeStruct((S, GLA_V), BF16),
                   jax.ShapeDtypeStruct((1, GLA_DV), F32)],
    )(o, proj, g, dcat)


HALO = 16


def _halo_specs(tm, tn, S):
    cur = pl.BlockSpec((tm, tn), lambda j, i: (i, j))
    prev = pl.BlockSpec((HALO, tn), lambda j, i: (jnp.maximum(i * (tm // HALO) - 1, 0), j))
    nxt = pl.BlockSpec((HALO, tn), lambda j, i: (jnp.minimum((i + 1) * (tm // HALO), S // HALO - 1), j))
    return cur, prev, nxt


def _shifted(c_ref, p_ref, n_ref, n_blocks):
    i = pl.program_id(1)
    x = c_ref[...].astype(F32)
    tm = x.shape[0]
    row = lax.broadcasted_iota(jnp.int32, x.shape, 0)
    before = p_ref[HALO - 1:HALO, :].astype(F32) * (i > 0).astype(F32)
    after = n_ref[0:1, :].astype(F32) * (i < n_blocks - 1).astype(F32)
    x_m1 = jnp.where(row == 0, before, pltpu.roll(x, 1, axis=0))
    x_p1 = jnp.where(row == tm - 1, after, pltpu.roll(x, tm - 1, axis=0))
    return x, x_m1, x_p1


def _glu_fwd(gp, up, cw, cb, tm=512, tn=1408):
    S = gp.shape[0]
    nb = S // tm

    def body(c_ref, p_ref, n_ref, up_ref, w_ref, b_ref, o_ref):
        x, x_m1, x_p1 = _shifted(c_ref, p_ref, n_ref, nb)
        w = w_ref[...]
        gate = w[0:1, :] * x_m1 + w[1:2, :] * x + w[2:3, :] * x_p1 + b_ref[...]
        o_ref[...] = (gate * _sigmoid(gate) * up_ref[...].astype(F32)).astype(BF16)

    cur, prev, nxt = _halo_specs(tm, tn, S)
    return pl.pallas_call(
        body, name="glu_fwd", grid=(D_FF // tn, nb),
        in_specs=[cur, prev, nxt, cur, pl.BlockSpec((3, tn), lambda j, i: (0, j)), pl.BlockSpec((1, tn), lambda j, i: (0, j))],
        out_specs=cur, out_shape=jax.ShapeDtypeStruct((S, D_FF), BF16),
    )(gp, gp, gp, up, cw, cb)


def _glu_bwd(gp, up, dact, cw, cb, tm=512, tn=1408):
    S = gp.shape[0]
    nb = S // tm

    def body(c_ref, p_ref, n_ref, up_ref, da_ref, w_ref, b_ref, dup_ref, dgate_ref, dw_ref, db_ref):
        @pl.when(pl.program_id(1) == 0)
        def _():
            dw_ref[...] = jnp.zeros_like(dw_ref)
            db_ref[...] = jnp.zeros_like(db_ref)

        x, x_m1, x_p1 = _shifted(c_ref, p_ref, n_ref, nb)
        w = w_ref[...]
        gate = w[0:1, :] * x_m1 + w[1:2, :] * x + w[2:3, :] * x_p1 + b_ref[...]
        sg = _sigmoid(gate)
        da = da_ref[...].astype(F32)
        dup_ref[...] = (da * (gate * sg)).astype(BF16)
        dgate = da * up_ref[...].astype(F32) * (sg * (1.0 + gate * (1.0 - sg)))
        dgate_ref[...] = dgate.astype(BF16)
        db_ref[...] += jnp.sum(dgate, axis=0, keepdims=True)
        dw_ref[...] += jnp.concatenate(
            [jnp.sum(dgate * x_m1, axis=0, keepdims=True), jnp.sum(dgate * x, axis=0, keepdims=True),
             jnp.sum(dgate * x_p1, axis=0, keepdims=True)], axis=0)

    cur, prev, nxt = _halo_specs(tm, tn, S)
    w_spec = pl.BlockSpec((3, tn), lambda j, i: (0, j))
    b_spec = pl.BlockSpec((1, tn), lambda j, i: (0, j))
    return pl.pallas_call(
        body, name="glu_bwd", grid=(D_FF // tn, nb), in_specs=[cur, prev, nxt, cur, cur, w_spec, b_spec],
        out_specs=[cur, cur, w_spec, b_spec],
        out_shape=[jax.ShapeDtypeStruct((S, D_FF), BF16), jax.ShapeDtypeStruct((S, D_FF), BF16),
                   jax.ShapeDtypeStruct((3, D_FF), F32), jax.ShapeDtypeStruct((1, D_FF), F32)],
    )(gp, gp, gp, up, dact, cw, cb)


def _conv_bwd_input(dgate, cw, tm=512, tn=1408):
    S = dgate.shape[0]
    nb = S // tm

    def body(c_ref, p_ref, n_ref, w_ref, o_ref):
        x, x_m1, x_p1 = _shifted(c_ref, p_ref, n_ref, nb)
        w = w_ref[...]
        o_ref[...] = (w[0:1, :] * x_p1 + w[1:2, :] * x + w[2:3, :] * x_m1).astype(BF16)

    cur, prev, nxt = _halo_specs(tm, tn, S)
    return pl.pallas_call(
        body, name="conv_bwd_input", grid=(D_FF // tn, nb),
        in_specs=[cur, prev, nxt, pl.BlockSpec((3, tn), lambda j, i: (0, j))], out_specs=cur,
        out_shape=jax.ShapeDtypeStruct((S, D_FF), BF16),
    )(dgate, dgate, dgate, cw)


def _local_step(x, target, norm1_g, w_in_t, wg, gate_bias, gla_norm_g, attn_norm_g, w_out, norm2_g,
                w_gate4, w_up4, conv_w, conv_b, w_down, final_norm_g, on_grad=lambda event, arrays: ()):
    S = x.shape[0]
    tabs = _rope_tables(S)

    n1 = _rms_fwd("rms1_fwd", x, norm1_g)
    z_block = IN_MAIN // Z_W
    proj = _mm_nt("in_proj", n1, w_in_t, 1024, 1536, BF16, n_out=IN_MAIN)
    z = _matmul(
        "in_proj_z",
        [(n1, pl.BlockSpec((1024, D_MODEL), lambda i: (i, 0)), w_in_t, pl.BlockSpec((Z_W, D_MODEL), lambda i: (z_block, 0)), NT)],
        (S // 1024,), jax.ShapeDtypeStruct((S, Z_W), BF16), pl.BlockSpec((1024, Z_W), lambda i: (i, 0)), 1)
    qk = _rope_fwd(proj, tabs)
    branch = [_attn_fwd(qk, proj, d) for d in DILATIONS]
    ao, o_attn, lse = _attn_combine([b[0] for b in branch], [b[1] for b in branch], attn_norm_g)
    gates = _gates_fwd(z, wg, gate_bias)
    o_f, st_f = _gla_fwd(proj, gates, False)
    o_gla, st_b = _gla_fwd(proj, gates, True, o_prev=o_f)
    go = _gla_out_fwd(o_gla, proj, gla_norm_g)
    cat = jnp.concatenate([ao, go], axis=1)
    h1 = _mm_nn("out_proj", cat, w_out, 1024, 1024, F32, res=x)
    n2 = _rms_fwd("rms2_fwd", h1, norm2_g)
    gp = _mm_nn_sharded("ffn_gate", n2, w_gate4, 1024, BF16)
    up = _mm_nn_sharded("ffn_up", n2, w_up4, 1024, BF16)
    act = _glu_fwd(gp, up, conv_w, conv_b)
    tk = D_FF // N_CHIPS
    h2 = _matmul(
        "ffn_down",
        [(act, pl.BlockSpec((1024, tk), lambda i, j, k: (i, k)), w_down, pl.BlockSpec((tk, 1024), lambda i, j, k: (k, j)), NN)],
        (S // 1024, D_MODEL // 1024, N_CHIPS), jax.ShapeDtypeStruct((S, D_MODEL), F32),
        pl.BlockSpec((1024, 1024), lambda i, j, k: (i, j)), N_CHIPS,
        res=(h1, pl.BlockSpec((1024, 1024), lambda i, j, k: (i, j))))
    loss_row, d_final_g, dh2, dh2_b = _final_loss(h2, final_norm_g.reshape(1, D_MODEL), target)

    dact = _mm_nt("ffn_down_bwd", dh2_b, w_down, 1024, tk, BF16)
    dup, dgate, d_conv_w, d_conv_b = _glu_bwd(gp, up, dact, conv_w, conv_b)
    dgp = _conv_bwd_input(dgate, conv_w)
    d_w_down = _mm_tn("ffn_down_wgrad", act, dh2_b, tk, D_MODEL, 1024, BF16)
    on_grad("w_down", dict(w_down=d_w_down))
    dgp = _after(dgp, d_w_down)
    d_w_gate4 = _mm_tn("ffn_gate_wgrad", n2, dgp, D_MODEL, tk, 1024, BF16, out3=tk)
    dup = _after(dup, d_w_gate4)
    d_w_up4 = _mm_tn("ffn_up_wgrad", n2, dup, D_MODEL, tk, 1024, BF16, out3=tk)
    held = on_grad("w_gate_w_up", dict(w_gate=d_w_gate4, w_up=d_w_up4))
    dgp = _after(dgp, d_w_up4, *held)
    dn2 = _matmul(
        "ffn_in_bwd",
        [(dgp, pl.BlockSpec((1024, tk), lambda i, j, k: (i, k)), w_gate4, pl.BlockSpec((None, 1024, tk), lambda i, j, k: (k, j, 0)), NT),
         (dup, pl.BlockSpec((1024, tk), lambda i, j, k: (i, k)), w_up4, pl.BlockSpec((None, 1024, tk), lambda i, j, k: (k, j, 0)), NT)],
        (S // 1024, D_MODEL // 1024, N_CHIPS), jax.ShapeDtypeStruct((S, D_MODEL), F32),
        pl.BlockSpec((1024, 1024), lambda i, j, k: (i, j)), N_CHIPS)
    dh1, dh1_b, d_norm2_g = _rms_bwd("rms2_bwd", h1, norm2_g, dn2, dh2)

    d_w_out = _mm_tn("out_proj_wgrad", cat, dh1_b, D_MODEL, 1024, 1024, BF16)
    held = on_grad("w_out", dict(w_out=d_w_out))
    dcat = _mm_nt("out_proj_bwd", _after(dh1_b, d_w_out, *held), w_out, 1024, 1024, BF16)
    do_attn, delta, d_attn_norm_g = _attn_norm_bwd(o_attn, attn_norm_g, dcat)
    dqs, dks, dvs = [], [], []
    for d in DILATIONS:
        dqs.append(_attn_bwd_dq(qk, proj, do_attn, lse, delta, d))
        dk, dv = _attn_bwd_dkv(qk, proj, do_attn, lse, delta, d)
        dks.append(dk)
        dvs.append(dv)
    d_attn = _attn_grad_merge(dqs, dks, dvs, tabs)
    held = on_grad("mid", dict(anchor=d_attn))
    do_gla, dgr, d_gla_norm_g = _gla_out_bwd(o_gla, proj, gla_norm_g, _after(dcat, *held))
    dq_f, dk_f, dv_f, dg_f = _gla_bwd(proj, gates, st_f, do_gla, False)
    dgq, dgk, dgv, dg_b = _gla_bwd(proj, gates, st_b, do_gla, True, prev=(dq_f, dk_f, dv_f))
    dz, d_wg, d_gate_bias = _gates_bwd(z, wg, gate_bias, dg_f, dg_b)
    dproj = jnp.concatenate([d_attn, dgq, dgk, dgv, dgr], axis=1)
    d_w_in_t = _mm_tn("in_proj_wgrad", dproj, n1, 1536, D_MODEL, 1024, BF16, rows_out=IN_W)
    n_tok = S // 1024
    d_w_in_t = _matmul(
        "in_proj_z_wgrad",
        [(dz, pl.BlockSpec((1024, Z_W), lambda i, j, k: (k, 0)), n1, pl.BlockSpec((1024, D_MODEL), lambda i, j, k: (k, 0)), TN)],
        (1, 1, n_tok), jax.ShapeDtypeStruct((IN_W, D_MODEL), BF16), pl.BlockSpec((Z_W, D_MODEL), lambda i, j, k: (z_block, 0)),
        n_tok, into=d_w_in_t)
    held = on_grad("w_in", dict(w_in_t=d_w_in_t))
    tkm = IN_MAIN // 4
    half = S // 2048

    def in_proj_bwd(name, first, a, into):
        return _matmul(
            name,
            [(a, pl.BlockSpec((1024, tkm), lambda i, j, k: (i + first, k)), w_in_t, pl.BlockSpec((tkm, 1024), lambda i, j, k: (k, j)), NN)],
            (half, D_MODEL // 1024, 4), jax.ShapeDtypeStruct((S, D_MODEL), F32),
            pl.BlockSpec((1024, 1024), lambda i, j, k: (i + first, j)), 4, into=into)

    dproj = _after(dproj, d_w_in_t, *held)
    dn1 = in_proj_bwd("in_proj_bwd_a", 0, dproj, None)
    held = on_grad("last", dict(last=dn1))
    dn1 = in_proj_bwd("in_proj_bwd_b", half, dproj, _after(dn1, *held))
    dn1 = _matmul(
        "in_proj_z_bwd",
        [(dz, pl.BlockSpec((1024, Z_W), lambda j, i: (i, 0)), w_in_t, pl.BlockSpec((Z_W, 1024), lambda j, i: (z_block, j)), NN)],
        (D_MODEL // 1024, S // 1024), jax.ShapeDtypeStruct((S, D_MODEL), F32), pl.BlockSpec((1024, 1024), lambda j, i: (i, j)), 1,
        res=(dn1, pl.BlockSpec((1024, 1024), lambda j, i: (i, j))))
    grad_x, _, d_norm1_g = _rms_bwd("rms1_bwd", x, norm1_g, dn1, dh1)

    big = dict(w_in_t=d_w_in_t, w_out=d_w_out, w_gate4=d_w_gate4, w_up4=d_w_up4, w_down=d_w_down)
    small = dict(loss=loss_row, norm1_g=d_norm1_g, wg=d_wg, gate_bias=d_gate_bias, gla_norm_g=d_gla_norm_g,
                 attn_norm_g=d_attn_norm_g, norm2_g=d_norm2_g, conv_w=d_conv_w, conv_b=d_conv_b, final_norm_g=d_final_g)
    return grad_x, big, small


def _position():
    return lax.axis_index("x"), lax.axis_index("y"), lax.axis_index("c")


def _other_chips(x, y):
    return [(1 - x, y), (x, 1 - y), (1 - x, 1 - y)]


def _gather_chips(name, shards):
    n = len(shards)

    def body(*refs):
        ins, outs = refs[:n], refs[n:2 * n]
        send, recv, loc = refs[2 * n:]
        x, y, c = _position()
        me = 2 * x + y
        chips = _other_chips(x, y)
        started = []
        for w in range(n):
            own = pltpu.make_async_copy(ins[w], outs[w].at[me], loc.at[w])
            own.start()
            started.append(own)
        sends = []
        for w in range(n):
            for j, (px, py) in enumerate(chips):
                cp = pltpu.make_async_remote_copy(ins[w], outs[w].at[me], send.at[3 * w + j], recv.at[3 * w + j],
                                                  device_id=(px, py, c), device_id_type=MESH)
                cp.start()
                sends.append(cp)
        for w in range(n):
            for j, (px, py) in enumerate(chips):
                pltpu.make_async_remote_copy(ins[w], outs[w].at[2 * px + py], send.at[3 * w + j], recv.at[3 * w + j],
                                             device_id=(px, py, c), device_id_type=MESH).wait_recv()
        for cp in sends:
            cp.wait_send()
        for own in started:
            own.wait()

    return pl.pallas_call(
        body, name=name, in_specs=[ANY] * n, out_specs=[ANY] * n,
        out_shape=[jax.ShapeDtypeStruct((N_CHIPS,) + s.shape, s.dtype) for s in shards],
        scratch_shapes=[pltpu.SemaphoreType.DMA((3 * n,)), pltpu.SemaphoreType.DMA((3 * n,)), pltpu.SemaphoreType.DMA((n,))],
    )(*shards)


def _gather_chips_async(name, shards, collective_id):
    n = len(shards)

    def body(*refs):
        ins, outs = refs[:n], refs[n:2 * n]
        send, recv, loc = refs[2 * n:]
        x, y, c = _position()
        me = 2 * x + y
        chips = _other_chips(x, y)
        barrier = pltpu.get_barrier_semaphore()
        for px, py in chips:
            pl.semaphore_signal(barrier, inc=1, device_id=(px, py, c), device_id_type=MESH)
        pl.semaphore_wait(barrier, len(chips))
        started = []
        for w in range(n):
            own = pltpu.make_async_copy(ins[w], outs[w].at[me], loc.at[w])
            own.start()
            started.append(own)
        sends = []
        for w in range(n):
            for j, (px, py) in enumerate(chips):
                cp = pltpu.make_async_remote_copy(ins[w], outs[w].at[me], send.at[3 * w + j], recv.at[3 * w + j],
                                                  device_id=(px, py, c), device_id_type=MESH)
                cp.start()
                sends.append(cp)
        for w in range(n):
            for j, (px, py) in enumerate(chips):
                pltpu.make_async_remote_copy(ins[w], outs[w].at[2 * px + py], send.at[3 * w + j], recv.at[3 * w + j],
                                             device_id=(px, py, c), device_id_type=MESH).wait_recv()
        for cp in sends:
            cp.wait_send()
        for own in started:
            own.wait()

    return pl.kernel(
        body, name=name, mesh=_sequencer(),
        out_type=[jax.ShapeDtypeStruct((N_CHIPS,) + s.shape, s.dtype) for s in shards],
        scratch_types=[pltpu.SemaphoreType.DMA((3 * n,)), pltpu.SemaphoreType.DMA((3 * n,)), pltpu.SemaphoreType.DMA((n,))],
        compiler_params=pltpu.CompilerParams(collective_id=collective_id),
    )(*shards)


def _sibling_exchange(name, arrs):
    n = len(arrs)

    def body(*refs):
        ins, outs = refs[:n], refs[n:2 * n]
        send, recv = refs[2 * n:]
        x, y, c = _position()
        copies = [pltpu.make_async_remote_copy(ins[w], outs[w], send.at[w], recv.at[w], device_id=(x, y, 1 - c),
                                               device_id_type=MESH) for w in range(n)]
        for cp in copies:
            cp.start()
        for cp in copies:
            cp.wait()

    return pl.pallas_call(
        body, name=name, in_specs=[ANY] * n, out_specs=[ANY] * n,
        out_shape=[jax.ShapeDtypeStruct(a.shape, a.dtype) for a in arrs],
        scratch_shapes=[pltpu.SemaphoreType.DMA((n,)), pltpu.SemaphoreType.DMA((n,))],
    )(*arrs)


def _scatter_chips(name, parts):
    n = len(parts)

    def body(*refs):
        ins, outs = refs[:n], refs[n:2 * n]
        send, recv, loc = refs[2 * n:]
        x, y, c = _position()
        me = 2 * x + y
        chips = _other_chips(x, y)
        started = []
        for w in range(n):
            own = pltpu.make_async_copy(ins[w].at[me], outs[w].at[me], loc.at[w])
            own.start()
            started.append(own)
        sends = []
        for w in range(n):
            for j, (px, py) in enumerate(chips):
                cp = pltpu.make_async_remote_copy(ins[w].at[2 * px + py], outs[w].at[me], send.at[3 * w + j],
                                                  recv.at[3 * w + j], device_id=(px, py, c), device_id_type=MESH)
                cp.start()
                sends.append(cp)
        for w in range(n):
            for j, (px, py) in enumerate(chips):
                pltpu.make_async_remote_copy(ins[w].at[me], outs[w].at[2 * px + py], send.at[3 * w + j], recv.at[3 * w + j],
                                             device_id=(px, py, c), device_id_type=MESH).wait_recv()
        for cp in sends:
            cp.wait_send()
        for own in started:
            own.wait()

    return pl.pallas_call(
        body, name=name, in_specs=[ANY] * n, out_specs=[ANY] * n,
        out_shape=[jax.ShapeDtypeStruct(p.shape, p.dtype) for p in parts],
        scratch_shapes=[pltpu.SemaphoreType.DMA((3 * n,)), pltpu.SemaphoreType.DMA((3 * n,)), pltpu.SemaphoreType.DMA((n,))],
    )(*parts)


def _sequencer():
    return plsc.ScalarSubcoreMesh(axis_name="sequencer", num_cores=1)


def _sibling_exchange_async(name, arrs, collective_id):
    n = len(arrs)

    def body(*refs):
        ins, outs = refs[:n], refs[n:2 * n]
        send, recv = refs[2 * n:]
        x, y, c = _position()
        sibling = (x, y, 1 - c)
        barrier = pltpu.get_barrier_semaphore()
        pl.semaphore_signal(barrier, inc=1, device_id=sibling, device_id_type=MESH)
        pl.semaphore_wait(barrier, 1)
        copies = [pltpu.make_async_remote_copy(ins[w], outs[w], send.at[w], recv.at[w], device_id=sibling,
                                               device_id_type=MESH) for w in range(n)]
        for cp in copies:
            cp.start()
        for cp in copies:
            cp.wait()

    return pl.kernel(
        body, name=name, out_type=[jax.ShapeDtypeStruct(a.shape, a.dtype) for a in arrs],
        scratch_types=[pltpu.SemaphoreType.DMA((n,)), pltpu.SemaphoreType.DMA((n,))],
        compiler_params=pltpu.CompilerParams(collective_id=collective_id), mesh=_sequencer(),
    )(*arrs)


def _scatter_chips_async(name, parts, collective_id):
    n = len(parts)

    def body(*refs):
        ins, outs = refs[:n], refs[n:2 * n]
        send, recv, loc = refs[2 * n:]
        x, y, c = _position()
        me = 2 * x + y
        chips = _other_chips(x, y)
        barrier = pltpu.get_barrier_semaphore()
        for px, py in chips:
            pl.semaphore_signal(barrier, inc=1, device_id=(px, py, c), device_id_type=MESH)
        pl.semaphore_wait(barrier, len(chips))
        started = []
        for w in range(n):
            own = pltpu.make_async_copy(ins[w].at[me], outs[w].at[me], loc.at[w])
            own.start()
            started.append(own)
        sends = []
        for w in range(n):
            for j, (px, py) in enumerate(chips):
                cp = pltpu.make_async_remote_copy(ins[w].at[2 * px + py], outs[w].at[me], send.at[3 * w + j],
                                                  recv.at[3 * w + j], device_id=(px, py, c), device_id_type=MESH)
                cp.start()
                sends.append(cp)
        for w in range(n):
            for j, (px, py) in enumerate(chips):
                pltpu.make_async_remote_copy(ins[w].at[me], outs[w].at[2 * px + py], send.at[3 * w + j], recv.at[3 * w + j],
                                             device_id=(px, py, c), device_id_type=MESH).wait_recv()
        for cp in sends:
            cp.wait_send()
        for own in started:
            own.wait()

    return pl.kernel(
        body, name=name, out_type=[jax.ShapeDtypeStruct(p.shape, p.dtype) for p in parts],
        scratch_types=[pltpu.SemaphoreType.DMA((3 * n,)), pltpu.SemaphoreType.DMA((3 * n,)), pltpu.SemaphoreType.DMA((n,))],
        compiler_params=pltpu.CompilerParams(collective_id=collective_id), mesh=_sequencer(),
    )(*parts)


def _allreduce_rows(buf):
    R = buf.shape[0]

    def body(in_ref, out_ref, land, send, recv):
        x, y, c = _position()
        me = 4 * x + 2 * y + c
        land[pl.ds(me, 1)] = in_ref[...][None]
        peers = []
        for mask in range(1, N_DEV):
            px = 1 - x if mask & 4 else x
            py = 1 - y if mask & 2 else y
            pc = 1 - c if mask & 1 else c
            peers.append((px, py, pc))
        sends = []
        for k, peer in enumerate(peers):
            cp = pltpu.make_async_remote_copy(in_ref, land.at[me], send.at[k], recv.at[k], device_id=peer, device_id_type=MESH)
            cp.start()
            sends.append(cp)
        for k, (px, py, pc) in enumerate(peers):
            pltpu.make_async_remote_copy(in_ref, land.at[4 * px + 2 * py + pc], send.at[k], recv.at[k],
                                         device_id=(px, py, pc), device_id_type=MESH).wait_recv()
        for cp in sends:
            cp.wait_send()
        tot = land[0]
        for i in range(1, N_DEV):
            tot = tot + land[i]
        out_ref[...] = tot

    vm = pl.BlockSpec(memory_space=pltpu.VMEM)
    return pl.pallas_call(
        body, name="allreduce_small", in_specs=[vm], out_specs=vm, out_shape=jax.ShapeDtypeStruct((R, LANES), F32),
        scratch_shapes=[pltpu.VMEM((N_DEV, R, LANES), F32), pltpu.SemaphoreType.DMA((N_DEV - 1,)),
                        pltpu.SemaphoreType.DMA((N_DEV - 1,))],
    )(buf)


def _tile2d(r, c, cap):
    if r <= cap:
        return r, c
    fits = [t for t in range(16, cap + 1, 16) if r % t == 0]
    return (max(fits), c) if fits else (r, 256)


def _pair_sum(name, a, b):
    n, r, c = a.shape
    tr, tc = _tile2d(r, c, 1024)

    def body(a_ref, b_ref, o_ref):
        o_ref[...] = (a_ref[...].astype(F32) + b_ref[...].astype(F32)).astype(BF16)

    blk = pl.BlockSpec((None, tr, tc), lambda s, i, j: (s, i, j))
    return pl.pallas_call(
        body, name=name, grid=(n, r // tr, c // tc), in_specs=[blk, blk], out_specs=blk,
        out_shape=jax.ShapeDtypeStruct(a.shape, BF16),
    )(a, b)


def _adamw_math(w, m, v, g):
    m2 = ADAM_B1 * m + (1.0 - ADAM_B1) * g
    v2 = ADAM_B2 * v + (1.0 - ADAM_B2) * (g * g)
    m_hat = m2 / (1.0 - ADAM_B1 ** ADAM_STEP)
    v_hat = v2 / (1.0 - ADAM_B2 ** ADAM_STEP)
    delta = -ADAM_LR * (m_hat / (jnp.sqrt(v_hat) + ADAM_EPS) + ADAM_WD * w)
    return delta, m2, v2


def _adamw(name, w, m, v, g):
    r, c = w.shape
    stacked = g.ndim == 3
    tr, tc = _tile2d(r, c, 256)

    def body(w_ref, m_ref, v_ref, g_ref, go_ref, d_ref, m2_ref, v2_ref):
        if stacked:
            gv = g_ref[0].astype(F32)
            for i in range(1, N_CHIPS):
                gv = gv + g_ref[i].astype(F32)
        else:
            gv = g_ref[...]
        delta, m2, v2 = _adamw_math(w_ref[...], m_ref[...], v_ref[...], gv)
        go_ref[...] = gv
        d_ref[...] = delta
        m2_ref[...] = m2
        v2_ref[...] = v2

    blk = pl.BlockSpec((tr, tc), lambda i, j: (i, j))
    g_spec = pl.BlockSpec((N_CHIPS, tr, tc), lambda i, j: (0, i, j)) if stacked else blk
    out = jax.ShapeDtypeStruct((r, c), F32)
    return pl.pallas_call(
        body, name=name, grid=(r // tr, c // tc), in_specs=[blk, blk, blk, g_spec], out_specs=[blk] * 4, out_shape=[out] * 4,
    )(w, m, v, g)


def _pack_rows(pieces):
    flat = jnp.concatenate([p.reshape(-1) for p in pieces])
    rows = flat.shape[0] // LANES
    pad = (-rows) % 8
    return jnp.pad(flat.reshape(rows, LANES), ((0, pad), (0, 0)))


def _unpack_rows(buf, shapes):
    flat = buf.reshape(-1)
    out, at = [], 0
    for s in shapes:
        size = math.prod(s)
        out.append(flat[at:at + size].reshape(s))
        at += size
    return out


SMALL_NAMES = ("norm1_g", "gf_up", "gf_b", "gb_up", "gb_b", "gla_norm_g", "attn_norm_g", "norm2_g", "conv_w", "conv_b",
               "final_norm_g")
BIG_NAMES = ("w_in", "w_out", "w_gate", "w_up", "w_down")
WEIGHT_ORDER = ("norm1_g", "w_in", "gf_up", "gf_b", "gb_up", "gb_b", "gla_norm_g", "attn_norm_g", "w_out", "norm2_g",
                "w_gate", "w_up", "conv_w", "conv_b", "w_down", "final_norm_g")


def kernel(x, norm1_g, w_in, gf_up, gf_b, gb_up, gb_b, gla_norm_g, attn_norm_g, w_out, norm2_g, w_gate, w_up, conv_w, conv_b, w_down, final_norm_g, loss_target, m_norm1_g, m_w_in, m_gf_up, m_gf_b, m_gb_up, m_gb_b, m_gla_norm_g, m_attn_norm_g, m_w_out, m_norm2_g, m_w_gate, m_w_up, m_conv_w, m_conv_b, m_w_down, m_final_norm_g, v_norm1_g, v_w_in, v_gf_up, v_gf_b, v_gb_up, v_gb_b, v_gla_norm_g, v_attn_norm_g, v_w_out, v_norm2_g, v_w_gate, v_w_up, v_conv_w, v_conv_b, v_w_down, v_final_norm_g):
    w = dict(norm1_g=norm1_g, w_in=w_in, gf_up=gf_up, gf_b=gf_b, gb_up=gb_up, gb_b=gb_b, gla_norm_g=gla_norm_g,
             attn_norm_g=attn_norm_g, w_out=w_out, norm2_g=norm2_g, w_gate=w_gate, w_up=w_up, conv_w=conv_w, conv_b=conv_b,
             w_down=w_down, final_norm_g=final_norm_g)
    m = dict(norm1_g=m_norm1_g, w_in=m_w_in, gf_up=m_gf_up, gf_b=m_gf_b, gb_up=m_gb_up, gb_b=m_gb_b, gla_norm_g=m_gla_norm_g,
             attn_norm_g=m_attn_norm_g, w_out=m_w_out, norm2_g=m_norm2_g, w_gate=m_w_gate, w_up=m_w_up, conv_w=m_conv_w,
             conv_b=m_conv_b, w_down=m_w_down, final_norm_g=m_final_norm_g)
    v = dict(norm1_g=v_norm1_g, w_in=v_w_in, gf_up=v_gf_up, gf_b=v_gf_b, gb_up=v_gb_up, gb_b=v_gb_b, gla_norm_g=v_gla_norm_g,
             attn_norm_g=v_attn_norm_g, w_out=v_w_out, norm2_g=v_norm2_g, w_gate=v_w_gate, w_up=v_w_up, conv_w=v_conv_w,
             conv_b=v_conv_b, w_down=v_w_down, final_norm_g=v_final_norm_g)
    S = x.shape[1]
    chip = 2 * lax.axis_index("x") + lax.axis_index("y")
    n_in = IN_W // N_CHIPS
    n_ff = D_FF // N_CHIPS
    n_gk = GLA_K // N_CHIPS

    def owned(t):
        return {k: (jnp.transpose(t[k][0]) if k == "w_in" else t[k][0]) for k in BIG_NAMES}

    own_w, own_m, own_v = owned(w), owned(m), owned(v)
    shard = {k: own_w[k].astype(BF16) for k in BIG_NAMES}
    small_shard = _pack_rows([gf_up[0], gb_up[0], conv_w[0]])
    small4, w_in4 = _gather_chips_async("gather_w_in", [small_shard, shard["w_in"]], 0)
    w_out4, w_gate4, w_up4 = _gather_chips_async("gather_w_mid", [shard["w_out"], shard["w_gate"], shard["w_up"]], 1)
    (w_down4,) = _gather_chips_async("gather_w_down", [shard["w_down"]], 2)
    w_in_t = w_in4.reshape(IN_W, D_MODEL)
    rows_up = GATE_RANK * n_gk // LANES
    rows_cw = 3 * n_ff // LANES
    gf_full = jnp.transpose(small4[:, 0:rows_up].reshape(N_CHIPS, GATE_RANK, n_gk), (1, 0, 2)).reshape(GATE_RANK, GLA_K)
    gb_full = jnp.transpose(small4[:, rows_up:2 * rows_up].reshape(N_CHIPS, GATE_RANK, n_gk), (1, 0, 2)).reshape(GATE_RANK, GLA_K)
    cw_full = jnp.transpose(small4[:, 2 * rows_up:2 * rows_up + rows_cw].reshape(N_CHIPS, 3, n_ff), (1, 0, 2)).reshape(3, D_FF)
    wg = jnp.zeros((Z_W, 2 * GLA_K), F32)
    wg = wg.at[0:GATE_RANK, 0:GLA_K].set(gf_full).at[GATE_RANK:2 * GATE_RANK, GLA_K:].set(gb_full).astype(BF16)
    gate_bias = jnp.concatenate([gf_b, gb_b], axis=1)

    pending, contributions, next_id = [], {}, [3]

    def as_shards(group, arrays):
        if group == "w_in":
            return dict(w_in=arrays["w_in_t"].reshape(N_CHIPS, n_in, D_MODEL))
        if group == "w_out":
            return dict(w_out=arrays["w_out"].reshape(N_CHIPS, D_MODEL // N_CHIPS, D_MODEL))
        if group == "w_down":
            return dict(w_down=arrays["w_down"].reshape(N_CHIPS, n_ff, D_MODEL))
        return arrays

    out = {}

    def swap(group, arrays):
        mine = as_shards(group, arrays)
        pending.append((group, mine, _sibling_exchange_async(f"sibling_{group}", list(mine.values()), next_id[0])))
        next_id[0] += 1

    def sum_and_send(anchor):
        tag, mine, theirs = pending.pop()
        sums = [_pair_sum(f"pair_sum_{k}", mine[k], _after(t, *anchor)) for k, t in zip(mine, theirs)]
        contributions.update(zip(mine, _scatter_chips_async(f"scatter_{tag}", sums, next_id[0])))
        next_id[0] += 1
        return sums

    def update(names, anchor):
        for k in names:
            res = _adamw(f"adamw_{k}", own_w[k], own_m[k], own_v[k], _after(contributions[k], *anchor))
            out[k] = [(jnp.transpose(r) if k == "w_in" else r)[None] for r in res]
        return [out[k][0] for k in names]

    def on_grad(event, arrays):
        anchor = list(arrays.values())
        held = []
        if event in ("w_gate_w_up", "w_out", "mid", "last"):
            held += sum_and_send(anchor)
        if event == "mid":
            held += update(("w_down", "w_gate", "w_up"), anchor)
        if event == "last":
            held += update(("w_out",), anchor)
        if event in ("w_down", "w_gate_w_up", "w_out", "w_in"):
            swap(event, arrays)
        return held

    grad_x, _, small = _local_step(
        x[0], loss_target[0], norm1_g, w_in_t, wg, gate_bias, gla_norm_g, attn_norm_g,
        w_out4.reshape(D_MODEL, D_MODEL), norm2_g, w_gate4, w_up4, cw_full, conv_b, w_down4.reshape(D_FF, D_MODEL), final_norm_g,
        on_grad=on_grad)
    update(("w_in",), [grad_x])

    d_gf_up = small["wg"][0:GATE_RANK, 0:GLA_K]
    d_gb_up = small["wg"][GATE_RANK:2 * GATE_RANK, GLA_K:]
    pieces = [small["loss"], small["norm1_g"], d_gf_up, small["gate_bias"][:, :GLA_K], d_gb_up, small["gate_bias"][:, GLA_K:],
              small["gla_norm_g"], small["attn_norm_g"], small["norm2_g"], small["conv_w"], small["conv_b"], small["final_norm_g"]]
    total = _allreduce_rows(_pack_rows(pieces))
    summed = _unpack_rows(total, [p.shape for p in pieces])
    loss = summed[0][0, 0]
    g_small = dict(zip(SMALL_NAMES, summed[1:]))
    g_small["gf_up"] = lax.dynamic_slice_in_dim(g_small["gf_up"], chip * n_gk, n_gk, axis=1)
    g_small["gb_up"] = lax.dynamic_slice_in_dim(g_small["gb_up"], chip * n_gk, n_gk, axis=1)
    g_small["conv_w"] = lax.dynamic_slice_in_dim(g_small["conv_w"], chip * n_ff, n_ff, axis=1)
    packed = [_pack_rows([t[k] for k in SMALL_NAMES]) for t in (w, m, v, g_small)]
    res = _adamw("adamw_small", *packed)
    shapes = [w[k].shape for k in SMALL_NAMES]
    for k, vals in zip(SMALL_NAMES, zip(*[_unpack_rows(r, shapes) for r in res])):
        out[k] = list(vals)

    grads, deltas, new_m, new_v = ([out[k][i] for k in WEIGHT_ORDER] for i in range(4))
    return (loss, grad_x[None], *grads, *deltas, *new_m, *new_v)
```

```python
import functools
import math

import jax
import jax.numpy as jnp
from jax import lax
from jax.experimental import pallas as pl
from jax.experimental.pallas import tpu as pltpu
from jax.experimental.pallas import tpu_sc as plsc

F32 = jnp.float32
BF16 = jnp.bfloat16

D_MODEL = 2048
ATTN_W = 1024
HEAD = 128
N_HEADS = 8
N_SIDE = 64
DILATIONS = (1, 4, 16)
ROPE_THETA = 500000.0
ROPE_DIM = 32
GLA_K = 512
GLA_V = 1024
GLA_HEADS = 4
GLA_DK = 128
GLA_DV = 256
GATE_RANK = 16
GATE_NORM = 16.0
CHUNK = 64
IN_MAIN = 6144
IN_W = 6176
Z_W = IN_W - IN_MAIN
D_FF = 5632
EPS = 1e-6
N_CHIPS = 4
N_DEV = 8
LANES = 128

ADAM_LR = 0.001
ADAM_B1 = 0.9
ADAM_B2 = 0.999
ADAM_EPS = 1e-08
ADAM_WD = 0.01
ADAM_STEP = 10

NEG = -1e30
MESH = pl.DeviceIdType.MESH
ANY = pl.BlockSpec(memory_space=pl.ANY)

NN = ((1,), (0,))
NT = ((1,), (1,))
TN = ((0,), (0,))


def _dot(a, b, dims=NN):
    return lax.dot_general(a, b, (dims, ((), ())), preferred_element_type=F32)


def _sigmoid(x):
    return 0.5 * jnp.tanh(0.5 * x) + 0.5


def _after(x, *deps):
    return lax.optimization_barrier((x,) + deps)[0]


def _matmul(name, pairs, grid, out_shape, out_spec, nk, res=None, into=None):
    n_in = 2 * len(pairs) + (res is not None)
    dims = [p[4] for p in pairs]

    n_ops = n_in + (into is not None)

    def body(*refs):
        ins, o_ref = refs[:n_in], refs[n_ops]

        def partial_sum():
            tot = None
            for p, dn in enumerate(dims):
                a, b = ins[2 * p][...], ins[2 * p + 1][...]
                t = _dot(a.astype(BF16), b.astype(BF16), dn)
                tot = t if tot is None else tot + t
            return tot

        if nk == 1:
            t = partial_sum()
            if res is not None:
                t = t + ins[-1][...]
            o_ref[...] = t.astype(o_ref.dtype)
        else:
            acc_ref = refs[n_ops + 1]
            k = pl.program_id(2)

            @pl.when(k == 0)
            def _():
                if res is not None:
                    acc_ref[...] = ins[-1][...]
                else:
                    acc_ref[...] = jnp.zeros_like(acc_ref)

            acc_ref[...] += partial_sum()

            @pl.when(k == nk - 1)
            def _():
                o_ref[...] = acc_ref[...].astype(o_ref.dtype)

    operands, in_specs = [], []
    for a, a_spec, b, b_spec, _ in pairs:
        operands += [a, b]
        in_specs += [a_spec, b_spec]
    if res is not None:
        operands.append(res[0])
        in_specs.append(res[1])
    acc_shape = tuple(s for s in out_spec.block_shape if s is not None)
    scratch = [pltpu.VMEM(acc_shape, F32)] if nk > 1 else []
    aliases = {}
    if into is not None:
        aliases = {len(operands): 0}
        operands.append(into)
        in_specs.append(ANY)
    return pl.pallas_call(
        body, name=name, grid=grid, in_specs=in_specs, out_specs=out_spec, out_shape=out_shape, scratch_shapes=scratch,
        input_output_aliases=aliases,
    )(*operands)


def _mm_nn(name, a, b, tm, tn, out_dtype, res=None):
    M, K = a.shape
    N = b.shape[1]
    pairs = [(a, pl.BlockSpec((tm, K), lambda j, i: (i, 0)), b, pl.BlockSpec((K, tn), lambda j, i: (0, j)), NN)]
    r = None if res is None else (res, pl.BlockSpec((tm, tn), lambda j, i: (i, j)))
    return _matmul(name, pairs, (N // tn, M // tm), jax.ShapeDtypeStruct((M, N), out_dtype),
                   pl.BlockSpec((tm, tn), lambda j, i: (i, j)), 1, r)


def _mm_nn_sharded(name, a, b4, tm, out_dtype):
    M, K = a.shape
    n = b4.shape[2]
    pairs = [(a, pl.BlockSpec((tm, K), lambda j, i: (i, 0)), b4, pl.BlockSpec((None, K, n), lambda j, i: (j, 0, 0)), NN)]
    return _matmul(name, pairs, (N_CHIPS, M // tm), jax.ShapeDtypeStruct((M, N_CHIPS * n), out_dtype),
                   pl.BlockSpec((tm, n), lambda j, i: (i, j)), 1)


def _mm_nt(name, a, b, tm, tn, out_dtype, res=None, n_out=None):
    M, K = a.shape
    N = b.shape[0] if n_out is None else n_out
    pairs = [(a, pl.BlockSpec((tm, K), lambda j, i: (i, 0)), b, pl.BlockSpec((tn, K), lambda j, i: (j, 0)), NT)]
    r = None if res is None else (res, pl.BlockSpec((tm, tn), lambda j, i: (i, j)))
    return _matmul(name, pairs, (N // tn, M // tm), jax.ShapeDtypeStruct((M, N), out_dtype),
                   pl.BlockSpec((tm, tn), lambda j, i: (i, j)), 1, r)


def _mm_tn(name, a, g, tka, tn, tmm, out_dtype, out3=None, rows_out=None):
    M, Ka = a.shape
    N = g.shape[1]
    pairs = [(a, pl.BlockSpec((tmm, tka), lambda i, j, k: (k, i)), g, pl.BlockSpec((tmm, tn), lambda i, j, k: (k, j)), TN)]
    if out3 is None:
        shape, spec = (Ka if rows_out is None else rows_out, N), pl.BlockSpec((tka, tn), lambda i, j, k: (i, j))
    else:
        shape, spec = (N // out3, Ka, out3), pl.BlockSpec((None, tka, tn), lambda i, j, k: (j, i, 0))
    return _matmul(name, pairs, (Ka // tka, N // tn, M // tmm), jax.ShapeDtypeStruct(shape, out_dtype), spec, M // tmm)


def _rms_fwd(name, x, g, tm=512):
    S, D = x.shape

    def body(x_ref, g_ref, o_ref):
        xv = x_ref[...]
        r = lax.rsqrt(jnp.mean(xv * xv, axis=-1, keepdims=True) + EPS)
        o_ref[...] = (xv * r * g_ref[...]).astype(o_ref.dtype)

    return pl.pallas_call(
        body, name=name, grid=(S // tm,),
        in_specs=[pl.BlockSpec((tm, D), lambda i: (i, 0)), pl.BlockSpec((1, D), lambda i: (0, 0))],
        out_specs=pl.BlockSpec((tm, D), lambda i: (i, 0)), out_shape=jax.ShapeDtypeStruct((S, D), BF16),
    )(x, g)


def _rms_bwd(name, x, g, dn, dres, tm=512):
    S, D = x.shape

    def body(x_ref, g_ref, dn_ref, dres_ref, dx_ref, dxb_ref, dg_ref):
        i = pl.program_id(0)

        @pl.when(i == 0)
        def _():
            dg_ref[...] = jnp.zeros_like(dg_ref)

        xv = x_ref[...]
        r = lax.rsqrt(jnp.mean(xv * xv, axis=-1, keepdims=True) + EPS)
        xhat = xv * r
        dnv = dn_ref[...].astype(F32)
        dg_ref[...] += jnp.sum(dnv * xhat, axis=0, keepdims=True)
        t = dnv * g_ref[...]
        dx = r * (t - xhat * jnp.mean(t * xhat, axis=-1, keepdims=True)) + dres_ref[...]
        dx_ref[...] = dx
        dxb_ref[...] = dx.astype(BF16)

    row = pl.BlockSpec((tm, D), lambda i: (i, 0))
    vec = pl.BlockSpec((1, D), lambda i: (0, 0))
    return pl.pallas_call(
        body, name=name, grid=(S // tm,), in_specs=[row, vec, row, row], out_specs=[row, row, vec],
        out_shape=[jax.ShapeDtypeStruct((S, D), F32), jax.ShapeDtypeStruct((S, D), BF16), jax.ShapeDtypeStruct((1, D), F32)],
    )(x, g, dn, dres)


def _final_loss(h2, g, target, tm=512):
    S, D = h2.shape

    def body(x_ref, g_ref, t_ref, loss_ref, dg_ref, dx_ref, dxb_ref):
        i = pl.program_id(0)

        @pl.when(i == 0)
        def _():
            loss_ref[...] = jnp.zeros_like(loss_ref)
            dg_ref[...] = jnp.zeros_like(dg_ref)

        xv = x_ref[...]
        r = lax.rsqrt(jnp.mean(xv * xv, axis=-1, keepdims=True) + EPS)
        xhat = xv * r
        gv = g_ref[...]
        diff = xhat * gv - t_ref[...]
        per_tok = jnp.mean(diff * diff, axis=-1, keepdims=True)
        loss_ref[...] += 0.5 * jnp.sum(per_tok, axis=0, keepdims=True)
        dy = diff * (1.0 / D)
        dg_ref[...] += jnp.sum(dy * xhat, axis=0, keepdims=True)
        t = dy * gv
        dx = r * (t - xhat * jnp.mean(t * xhat, axis=-1, keepdims=True))
        dx_ref[...] = dx
        dxb_ref[...] = dx.astype(BF16)

    row = pl.BlockSpec((tm, D), lambda i: (i, 0))
    vec = pl.BlockSpec((1, D), lambda i: (0, 0))
    return pl.pallas_call(
        body, name="final_loss", grid=(S // tm,), in_specs=[row, vec, row],
        out_specs=[pl.BlockSpec((1, LANES), lambda i: (0, 0)), vec, row, row],
        out_shape=[jax.ShapeDtypeStruct((1, LANES), F32), jax.ShapeDtypeStruct((1, D), F32),
                   jax.ShapeDtypeStruct((S, D), F32), jax.ShapeDtypeStruct((S, D), BF16)],
    )(h2, g, target)


def _rope_tables(S):
    pos = jnp.arange(S, dtype=F32)
    inv_freq = ROPE_THETA ** (-jnp.arange(0, ROPE_DIM, 2, dtype=F32) / ROPE_DIM)
    ang = pos[:, None] * inv_freq[None, :]
    cos, sin = jnp.cos(ang), jnp.sin(ang)
    half = ROPE_DIM // 2
    rest = HEAD - ROPE_DIM
    z_h, z_r = jnp.zeros((S, half), F32), jnp.zeros((S, rest), F32)
    tab_c = jnp.concatenate([cos, cos, jnp.ones((S, rest), F32)], axis=1)
    tab_up = jnp.concatenate([z_h, sin, z_r], axis=1)
    tab_dn = jnp.concatenate([-sin, z_h, z_r], axis=1)
    return tab_c, tab_up, tab_dn


def _rope_head(t, c, up, dn):
    half = ROPE_DIM // 2
    return t * c + pltpu.roll(t, half, axis=1) * up + pltpu.roll(t, HEAD - half, axis=1) * dn


def _rope_fwd(proj, tabs, tm=512):
    S = proj.shape[0]
    W = 2 * ATTN_W

    def body(p_ref, c_ref, up_ref, dn_ref, o_ref):
        c, up, dn = c_ref[...], up_ref[...], dn_ref[...]
        for h in range(W // HEAD):
            sl = slice(h * HEAD, (h + 1) * HEAD)
            o_ref[:, sl] = _rope_head(p_ref[:, sl].astype(F32), c, up, dn).astype(BF16)

    tab = pl.BlockSpec((tm, HEAD), lambda i: (i, 0))
    return pl.pallas_call(
        body, name="rope_fwd", grid=(S // tm,), in_specs=[pl.BlockSpec((tm, W), lambda i: (i, 0)), tab, tab, tab],
        out_specs=pl.BlockSpec((tm, W), lambda i: (i, 0)), out_shape=jax.ShapeDtypeStruct((S, W), BF16),
    )(proj, *tabs)


def _attn_grad_merge(dqs, dks, dvs, tabs, tm=256):
    S = dqs[0].shape[0]

    def body(*refs):
        q_refs, k_refs, v_refs = refs[0:3], refs[3:6], refs[6:9]
        c, up, dn = refs[9][...], refs[10][...], refs[11][...]
        o_ref = refs[12]
        for h in range(N_HEADS):
            sl = slice(h * HEAD, (h + 1) * HEAD)
            for part, rs in ((0, q_refs), (1, k_refs)):
                t = rs[0][:, sl].astype(F32) + rs[1][:, sl].astype(F32) + rs[2][:, sl].astype(F32)
                osl = slice(part * ATTN_W + h * HEAD, part * ATTN_W + (h + 1) * HEAD)
                o_ref[:, osl] = _rope_head(t, c, -up, -dn).astype(BF16)
        o_ref[:, 2 * ATTN_W:] = (v_refs[0][...].astype(F32) + v_refs[1][...].astype(F32)
                                 + v_refs[2][...].astype(F32)).astype(BF16)

    blk = pl.BlockSpec((tm, ATTN_W), lambda i: (i, 0))
    tab = pl.BlockSpec((tm, HEAD), lambda i: (i, 0))
    return pl.pallas_call(
        body, name="attn_grad_merge", grid=(S // tm,), in_specs=[blk] * 9 + [tab] * 3,
        out_specs=pl.BlockSpec((tm, 3 * ATTN_W), lambda i: (i, 0)), out_shape=jax.ShapeDtypeStruct((S, 3 * ATTN_W), BF16),
    )(*dqs, *dks, *dvs, *tabs)


SUB = 128
Q_COL, K_COL, V_COL = 0, ATTN_W // HEAD, 2 * ATTN_W // HEAD


class _AttnGeo:
    def __init__(self, S, d):
        self.S, self.d, self.L = S, d, S // d
        self.halo = N_SIDE * d
        self.TB = min(2048, S)
        self.W = self.TB + 2 * self.halo
        self.n_sub = self.TB // SUB
        self.grid = (S // self.TB, N_HEADS)
        self.dt = F32 if d > 1 else BF16
        self.su = min(d, 4)
        self.sb = d // self.su
        assert self.TB % (SUB * d) == 0 and self.TB % self.halo == 0

    def specs(self, width, col0, per_head=True):
        ratio = self.TB // self.halo
        last = self.S // self.halo - 1
        col = (lambda h: col0 + h) if per_head else (lambda h: col0)
        cur = pl.BlockSpec((self.TB, width), lambda i, h: (i, col(h)))
        prev = pl.BlockSpec((self.halo, width), lambda i, h: (jnp.maximum(i * ratio - 1, 0), col(h)))
        nxt = pl.BlockSpec((self.halo, width), lambda i, h: (jnp.minimum((i + 1) * ratio, last), col(h)))
        return cur, prev, nxt

    def scratch(self, rows, dtype=None):
        nat = pltpu.VMEM((rows, LANES), self.dt if dtype is None else dtype)
        return [nat] if self.sb == 1 else [nat, pltpu.VMEM((rows, LANES), F32)]

    def bind(self, refs):
        nat = next(refs)
        return (nat, nat) if self.sb == 1 else (nat, next(refs))

    def spread(self, pair):
        nat, streams = pair
        if self.sb > 1:
            n = nat.shape[0] // self.sb
            for a in range(self.sb):
                streams[a * n:(a + 1) * n, :] = nat[pl.ds(a, n, stride=self.sb), :]
        return streams

    def gather(self, pair):
        nat, streams = pair
        if self.sb > 1:
            n = nat.shape[0] // self.sb
            for a in range(self.sb):
                nat[pl.ds(a, n, stride=self.sb), :] = streams[a * n:(a + 1) * n, :]
        return nat

    def rows(self, sub, n, total):
        res, blk = sub % self.d, sub // self.d
        a, b = res % self.sb, res // self.sb
        start = a * (total // self.sb) + b + self.su * SUB * blk
        return pl.ds(start, n, stride=self.su) if self.su > 1 else pl.ds(start, n)

    def mask(self, sub):
        base = pl.program_id(0) * (self.TB // self.d) + SUB * (sub // self.d)
        row = lax.broadcasted_iota(jnp.int32, (SUB, 2 * SUB), 0)
        col = lax.broadcasted_iota(jnp.int32, (SUB, 2 * SUB), 1)
        pos = base - N_SIDE + col
        return (col >= row) & (col <= row + 2 * N_SIDE) & (pos >= 0) & (pos < self.L)

    def fill(self, dst, c_ref):
        dst[...] = c_ref[...].astype(dst.dtype)

    def fill_window(self, dst, p_ref, c_ref, n_ref):
        dst[0:self.halo] = p_ref[...].astype(dst.dtype)
        dst[self.halo:self.halo + self.TB] = c_ref[...].astype(dst.dtype)
        dst[self.halo + self.TB:] = n_ref[...].astype(dst.dtype)


def _lane_of(tile, h):
    lane = lax.broadcasted_iota(jnp.int32, tile.shape, 1)
    return jnp.sum(jnp.where(lane == h, tile, 0.0), axis=1, keepdims=True)


def _attn_fwd(qk, proj, d):
    S = qk.shape[0]
    geo = _AttnGeo(S, d)
    scale = HEAD ** -0.5

    def body(q_ref, kp, kc, kn, vp, vc, vn, o_ref, lse_ref, *scratch):
        h = pl.program_id(1)
        refs = iter(scratch)
        q_p, k_p, v_p, o_p, l_p = (geo.bind(refs) for _ in range(5))
        geo.fill(q_p[0], q_ref)
        geo.fill_window(k_p[0], kp, kc, kn)
        geo.fill_window(v_p[0], vp, vc, vn)
        qs, ks, vs = geo.spread(q_p), geo.spread(k_p), geo.spread(v_p)
        os, ls = o_p[1], l_p[1]
        for sub in range(geo.n_sub):
            rq, rw = geo.rows(sub, SUB, geo.TB), geo.rows(sub, 2 * SUB, geo.W)
            q_r, k_r, v_r = qs[rq, :].astype(BF16), ks[rw, :].astype(BF16), vs[rw, :].astype(BF16)
            s = jnp.where(geo.mask(sub), _dot(q_r, k_r, NT) * scale, NEG)
            m = jnp.max(s, axis=1, keepdims=True)
            p = jnp.exp(s - m)
            l = jnp.sum(p, axis=1, keepdims=True)
            os[rq, :] = _dot(p.astype(BF16), v_r) / l
            ls[rq, :] = jnp.broadcast_to(m + jnp.log(l), (SUB, LANES))
        o_ref[...] = geo.gather(o_p)[...].astype(BF16)

        @pl.when(h == 0)
        def _():
            lse_ref[...] = jnp.zeros_like(lse_ref)

        lane = lax.broadcasted_iota(jnp.int32, (geo.TB, LANES), 1)
        lse_ref[...] = jnp.where(lane == h, geo.gather(l_p)[...], lse_ref[...])

    q_cur, _, _ = geo.specs(HEAD, Q_COL)
    k_specs = geo.specs(HEAD, K_COL)
    v_specs = geo.specs(HEAD, V_COL)
    stat = pl.BlockSpec((geo.TB, LANES), lambda i, h: (i, 0))
    return pl.pallas_call(
        body, name=f"attn_fwd_d{d}", grid=geo.grid,
        in_specs=[q_cur, k_specs[1], k_specs[0], k_specs[2], v_specs[1], v_specs[0], v_specs[2]],
        out_specs=[q_cur, stat],
        out_shape=[jax.ShapeDtypeStruct((S, ATTN_W), BF16), jax.ShapeDtypeStruct((S, LANES), F32)],
        scratch_shapes=(geo.scratch(geo.TB) + geo.scratch(geo.W) + geo.scratch(geo.W) + geo.scratch(geo.TB, F32)
                        + geo.scratch(geo.TB, F32)),
    )(qk, qk, qk, qk, proj, proj, proj)


def _attn_combine(outs, lses, g, tm=256):
    S = outs[0].shape[0]

    def body(o1, o2, o3, l1, l2, l3, g_ref, ao_ref, o_ref, lse_ref):
        lane = lax.broadcasted_iota(jnp.int32, (tm, LANES), 1)
        lse_tile = jnp.zeros((tm, LANES), F32)
        a = [l1[...], l2[...], l3[...]]
        ssq = jnp.zeros((tm, 1), F32)
        for h in range(N_HEADS):
            sl = slice(h * HEAD, (h + 1) * HEAD)
            a1, a2, a3 = (t[:, h:h + 1] for t in a)
            mx = jnp.maximum(jnp.maximum(a1, a2), a3)
            e1, e2, e3 = jnp.exp(a1 - mx), jnp.exp(a2 - mx), jnp.exp(a3 - mx)
            den = e1 + e2 + e3
            oh = (e1 * o1[:, sl].astype(F32) + e2 * o2[:, sl].astype(F32) + e3 * o3[:, sl].astype(F32)) / den
            o_ref[:, sl] = oh
            ssq = ssq + jnp.sum(oh * oh, axis=1, keepdims=True)
            lse_tile = jnp.where(lane == h, mx + jnp.log(den), lse_tile)
        lse_ref[...] = lse_tile
        r = lax.rsqrt(ssq * (1.0 / ATTN_W) + EPS)
        ao_ref[...] = (o_ref[...] * r * g_ref[...]).astype(BF16)

    blk = pl.BlockSpec((tm, ATTN_W), lambda i: (i, 0))
    ls = pl.BlockSpec((tm, LANES), lambda i: (i, 0))
    return pl.pallas_call(
        body, name="attn_combine", grid=(S // tm,),
        in_specs=[blk, blk, blk, ls, ls, ls, pl.BlockSpec((1, ATTN_W), lambda i: (0, 0))], out_specs=[blk, blk, ls],
        out_shape=[jax.ShapeDtypeStruct((S, ATTN_W), BF16), jax.ShapeDtypeStruct((S, ATTN_W), F32),
                   jax.ShapeDtypeStruct((S, LANES), F32)],
    )(*outs, *lses, g)


def _attn_norm_bwd(o, g, dao, tm=256):
    S = o.shape[0]

    def body(o_ref, g_ref, dao_ref, do_ref, dl_ref, dg_ref):
        i = pl.program_id(0)

        @pl.when(i == 0)
        def _():
            dg_ref[...] = jnp.zeros_like(dg_ref)

        ov = o_ref[...]
        r = lax.rsqrt(jnp.mean(ov * ov, axis=-1, keepdims=True) + EPS)
        ohat = ov * r
        dn = dao_ref[...].astype(F32)
        dg_ref[...] += jnp.sum(dn * ohat, axis=0, keepdims=True)
        t = dn * g_ref[...]
        do = r * (t - ohat * jnp.mean(t * ohat, axis=-1, keepdims=True))
        do_ref[...] = do.astype(BF16)
        prod = do * ov
        lane = lax.broadcasted_iota(jnp.int32, (tm, LANES), 1)
        tile = jnp.zeros((tm, LANES), F32)
        for h in range(N_HEADS):
            tile = jnp.where(lane == h, jnp.sum(prod[:, h * HEAD:(h + 1) * HEAD], axis=1, keepdims=True), tile)
        dl_ref[...] = tile

    blk = pl.BlockSpec((tm, ATTN_W), lambda i: (i, 0))
    vec = pl.BlockSpec((1, ATTN_W), lambda i: (0, 0))
    return pl.pallas_call(
        body, name="attn_norm_bwd", grid=(S // tm,),
        in_specs=[blk, vec, pl.BlockSpec((tm, ATTN_W), lambda i: (i, 0))],
        out_specs=[blk, pl.BlockSpec((tm, LANES), lambda i: (i, 0)), vec],
        out_shape=[jax.ShapeDtypeStruct((S, ATTN_W), BF16), jax.ShapeDtypeStruct((S, LANES), F32),
                   jax.ShapeDtypeStruct((1, ATTN_W), F32)],
    )(o, g, dao)


def _attn_bwd_dq(qk, proj, do, lse, delta, d):
    S = qk.shape[0]
    geo = _AttnGeo(S, d)
    scale = HEAD ** -0.5

    def body(q_ref, kp, kc, kn, vp, vc, vn, do_ref, lse_ref, dl_ref, dq_ref, *scratch):
        h = pl.program_id(1)
        refs = iter(scratch)
        q_p, k_p, v_p, do_p, lse_p, dl_p, dq_p = (geo.bind(refs) for _ in range(7))
        geo.fill(q_p[0], q_ref)
        geo.fill(do_p[0], do_ref)
        geo.fill(lse_p[0], lse_ref)
        geo.fill(dl_p[0], dl_ref)
        geo.fill_window(k_p[0], kp, kc, kn)
        geo.fill_window(v_p[0], vp, vc, vn)
        qs, ks, vs, dos = geo.spread(q_p), geo.spread(k_p), geo.spread(v_p), geo.spread(do_p)
        lses, dls = geo.spread(lse_p), geo.spread(dl_p)
        dqs = dq_p[1]
        for sub in range(geo.n_sub):
            rq, rw = geo.rows(sub, SUB, geo.TB), geo.rows(sub, 2 * SUB, geo.W)
            q_r, k_r, v_r = qs[rq, :].astype(BF16), ks[rw, :].astype(BF16), vs[rw, :].astype(BF16)
            lse_c, dl_c = _lane_of(lses[rq, :], h), _lane_of(dls[rq, :], h)
            s = _dot(q_r, k_r, NT) * scale
            p = jnp.where(geo.mask(sub), jnp.exp(s - lse_c), 0.0)
            dp = _dot(dos[rq, :].astype(BF16), v_r, NT)
            ds = (p * (dp - dl_c) * scale).astype(BF16)
            dqs[rq, :] = _dot(ds, k_r)
        dq_ref[...] = geo.gather(dq_p)[...].astype(BF16)

    cur, _, _ = geo.specs(HEAD, 0)
    k_specs = geo.specs(HEAD, K_COL)
    v_specs = geo.specs(HEAD, V_COL)
    stat = pl.BlockSpec((geo.TB, LANES), lambda i, h: (i, 0))
    return pl.pallas_call(
        body, name=f"attn_bwd_dq_d{d}", grid=geo.grid,
        in_specs=[cur, k_specs[1], k_specs[0], k_specs[2], v_specs[1], v_specs[0], v_specs[2], cur, stat, stat],
        out_specs=cur, out_shape=jax.ShapeDtypeStruct((S, ATTN_W), BF16),
        scratch_shapes=(geo.scratch(geo.TB) + geo.scratch(geo.W) + geo.scratch(geo.W) + geo.scratch(geo.TB)
                        + geo.scratch(geo.TB, F32) + geo.scratch(geo.TB, F32) + geo.scratch(geo.TB, F32)),
    )(qk, qk, qk, qk, proj, proj, proj, do, lse, delta)


def _attn_bwd_dkv(qk, proj, do, lse, delta, d):
    S = qk.shape[0]
    geo = _AttnGeo(S, d)
    scale = HEAD ** -0.5

    def body(k_ref, v_ref, qp, qc, qn, dop, doc, don, lp, lc, ln, dlp, dlc, dln, dk_ref, dv_ref, *scratch):
        h = pl.program_id(1)
        refs = iter(scratch)
        k_p, v_p, q_p, do_p, lw_p, dlw_p, dk_p, dv_p = (geo.bind(refs) for _ in range(8))
        geo.fill(k_p[0], k_ref)
        geo.fill(v_p[0], v_ref)
        geo.fill_window(q_p[0], qp, qc, qn)
        geo.fill_window(do_p[0], dop, doc, don)
        geo.fill_window(lw_p[0], lp, lc, ln)
        geo.fill_window(dlw_p[0], dlp, dlc, dln)
        ks, vs, qs, dos = geo.spread(k_p), geo.spread(v_p), geo.spread(q_p), geo.spread(do_p)
        lws, dlws = geo.spread(lw_p), geo.spread(dlw_p)
        dks, dvs = dk_p[1], dv_p[1]
        head = lax.broadcasted_iota(jnp.int32, (LANES, 2 * SUB), 0)
        for sub in range(geo.n_sub):
            rq, rw = geo.rows(sub, SUB, geo.TB), geo.rows(sub, 2 * SUB, geo.W)
            k_r, v_r = ks[rq, :].astype(BF16), vs[rq, :].astype(BF16)
            q_w, do_w = qs[rw, :].astype(BF16), dos[rw, :].astype(BF16)
            lse_row = jnp.sum(jnp.where(head == h, lws[rw, :].T, 0.0), axis=0, keepdims=True)
            dl_row = jnp.sum(jnp.where(head == h, dlws[rw, :].T, 0.0), axis=0, keepdims=True)
            st = _dot(k_r, q_w, NT) * scale
            pt = jnp.where(geo.mask(sub), jnp.exp(st - lse_row), 0.0)
            dvs[rq, :] = _dot(pt.astype(BF16), do_w)
            dpt = _dot(v_r, do_w, NT)
            dst = (pt * (dpt - dl_row) * scale).astype(BF16)
            dks[rq, :] = _dot(dst, q_w)
        dk_ref[...] = geo.gather(dk_p)[...].astype(BF16)
        dv_ref[...] = geo.gather(dv_p)[...].astype(BF16)

    q_specs = geo.specs(HEAD, Q_COL)
    k_cur, _, _ = geo.specs(HEAD, K_COL)
    v_cur, _, _ = geo.specs(HEAD, V_COL)
    do_specs = geo.specs(HEAD, 0)
    st_specs = geo.specs(LANES, 0, per_head=False)
    cur = do_specs[0]
    return pl.pallas_call(
        body, name=f"attn_bwd_dkv_d{d}", grid=geo.grid,
        in_specs=[k_cur, v_cur, q_specs[1], q_specs[0], q_specs[2], do_specs[1], do_specs[0], do_specs[2],
                  st_specs[1], st_specs[0], st_specs[2], st_specs[1], st_specs[0], st_specs[2]],
        out_specs=[cur, cur],
        out_shape=[jax.ShapeDtypeStruct((S, ATTN_W), BF16), jax.ShapeDtypeStruct((S, ATTN_W), BF16)],
        scratch_shapes=(geo.scratch(geo.TB) + geo.scratch(geo.TB) + geo.scratch(geo.W) + geo.scratch(geo.W)
                        + geo.scratch(geo.W, F32) + geo.scratch(geo.W, F32) + geo.scratch(geo.TB, F32)
                        + geo.scratch(geo.TB, F32)),
    )(qk, proj, qk, qk, qk, do, do, do, lse, lse, lse, delta, delta, delta)


def _cumsum_rows(x, reverse):
    n = x.shape[0]
    row = lax.broadcasted_iota(jnp.int32, x.shape, 0)
    s = 1
    while s < n:
        if reverse:
            x = x + jnp.where(row < n - s, pltpu.roll(x, n - s, axis=0), 0.0)
        else:
            x = x + jnp.where(row >= s, pltpu.roll(x, s, axis=0), 0.0)
        s *= 2
    return x


GLA_GROUP = 4


def _gla_rows(cc):
    return slice(cc * CHUNK, (cc + 1) * CHUNK)


def _gla_chunk_terms(q_ref, k_ref, v_ref, g_ref, h, reverse, rows):
    ksl = slice(h * GLA_DK, (h + 1) * GLA_DK)
    q = q_ref[rows, ksl].astype(F32) * (GLA_DK ** -0.5)
    k = k_ref[rows, ksl].astype(F32)
    v = v_ref[rows, h * GLA_DV:(h + 1) * GLA_DV]
    b = _cumsum_rows(g_ref[rows, ksl], reverse)
    r_ref = CHUNK // 2 if reverse else CHUNK // 2 - 1
    r_last = 0 if reverse else CHUNK - 1
    b_ref, b_last = b[r_ref:r_ref + 1, :], b[r_last:r_last + 1, :]
    ii = lax.broadcasted_iota(jnp.int32, (CHUNK, CHUNK), 0)
    jj = lax.broadcasted_iota(jnp.int32, (CHUNK, CHUNK), 1)
    causal = (jj >= ii) if reverse else (jj <= ii)
    e_q, e_k = jnp.exp(b - b_ref), jnp.exp(b_ref - b)
    e_in, e_st = jnp.exp(b), jnp.exp(b_last - b)
    return dict(q=q, k=k, v=v, b=b, causal=causal, e_q=e_q, e_k=e_k, e_in=e_in, e_st=e_st, dec=jnp.exp(b_last),
                qe=q * e_q, ke=k * e_k, q_in=q * e_in, k_st=k * e_st, r_ref=r_ref, r_last=r_last)


def _gla_specs(order):
    rows = GLA_GROUP * CHUNK
    q = pl.BlockSpec((rows, GLA_K), lambda c: (order(c), 3 * ATTN_W // GLA_K))
    k = pl.BlockSpec((rows, GLA_K), lambda c: (order(c), 3 * ATTN_W // GLA_K + 1))
    v = pl.BlockSpec((rows, GLA_V), lambda c: (order(c), (3 * ATTN_W + 2 * GLA_K) // GLA_V))
    return q, k, v


def _gla_fwd(proj, gates, reverse, o_prev=None):
    S = proj.shape[0]
    n = S // CHUNK
    nb = n // GLA_GROUP
    rows = GLA_GROUP * CHUNK
    order = (lambda c: nb - 1 - c) if reverse else (lambda c: c)
    seq = list(range(GLA_GROUP))[::-1] if reverse else list(range(GLA_GROUP))
    gcol = 1 if reverse else 0

    def body(*refs):
        if o_prev is None:
            q_ref, k_ref, v_ref, g_ref, o_ref, st_ref, state = refs
        else:
            q_ref, k_ref, v_ref, g_ref, op_ref, o_ref, st_ref, state = refs
        c = pl.program_id(0)

        @pl.when(c == 0)
        def _():
            state[...] = jnp.zeros_like(state)

        for h in range(GLA_HEADS):
            vsl = slice(h * GLA_DV, (h + 1) * GLA_DV)
            st = state[h]
            for cc in seq:
                rs = _gla_rows(cc)
                t = _gla_chunk_terms(q_ref, k_ref, v_ref, g_ref, h, reverse, rs)
                a = jnp.where(t["causal"], _dot(t["qe"].astype(BF16), t["ke"].astype(BF16), NT), 0.0)
                o = _dot(a.astype(BF16), t["v"])
                st_b = st.astype(BF16)
                st_ref[cc, h] = st_b
                o = o + _dot(t["q_in"].astype(BF16), st_b, NT)
                st = st * t["dec"] + _dot(t["v"], t["k_st"].astype(BF16), TN)
                if o_prev is not None:
                    o = o + op_ref[rs, vsl]
                o_ref[rs, vsl] = o
            state[h] = st

    q_spec, k_spec, v_spec = _gla_specs(order)
    o_spec = pl.BlockSpec((rows, GLA_V), lambda c: (order(c), 0))
    in_specs = [q_spec, k_spec, v_spec, pl.BlockSpec((rows, GLA_K), lambda c: (order(c), gcol))]
    operands = [proj, proj, proj, gates]
    if o_prev is not None:
        in_specs.append(o_spec)
        operands.append(o_prev)
    return pl.pallas_call(
        body, name="gla_fwd_rev" if reverse else "gla_fwd", grid=(nb,), in_specs=in_specs,
        out_specs=[o_spec, pl.BlockSpec((GLA_GROUP, GLA_HEADS, GLA_DV, GLA_DK), lambda c: (order(c), 0, 0, 0))],
        out_shape=[jax.ShapeDtypeStruct((S, GLA_V), F32), jax.ShapeDtypeStruct((n, GLA_HEADS, GLA_DV, GLA_DK), BF16)],
        scratch_shapes=[pltpu.VMEM((GLA_HEADS, GLA_DV, GLA_DK), F32)],
    )(*operands)


def _gla_bwd(proj, gates, states, do, reverse, prev=None):
    S = proj.shape[0]
    n = S // CHUNK
    nb = n // GLA_GROUP
    rows = GLA_GROUP * CHUNK
    order = (lambda c: c) if reverse else (lambda c: nb - 1 - c)
    seq = list(range(GLA_GROUP)) if reverse else list(range(GLA_GROUP))[::-1]
    gcol = 1 if reverse else 0
    out_dt = F32 if prev is None else BF16

    def body(*refs):
        if prev is None:
            q_ref, k_ref, v_ref, g_ref, st_ref, do_ref, dq_ref, dk_ref, dv_ref, dg_ref, dstate = refs
        else:
            q_ref, k_ref, v_ref, g_ref, st_ref, do_ref, pq, pk, pv, dq_ref, dk_ref, dv_ref, dg_ref, dstate = refs
        c = pl.program_id(0)

        @pl.when(c == 0)
        def _():
            dstate[...] = jnp.zeros_like(dstate)

        row = lax.broadcasted_iota(jnp.int32, (CHUNK, GLA_DK), 0)
        for h in range(GLA_HEADS):
            ksl = slice(h * GLA_DK, (h + 1) * GLA_DK)
            vsl = slice(h * GLA_DV, (h + 1) * GLA_DV)
            dst = dstate[h]
            for cc in seq:
                rs = _gla_rows(cc)
                t = _gla_chunk_terms(q_ref, k_ref, v_ref, g_ref, h, reverse, rs)
                v = t["v"]
                dob = do_ref[rs, vsl].astype(BF16)
                st_b = st_ref[cc, h]
                dst_b = dst.astype(BF16)
                qe_b, ke_b = t["qe"].astype(BF16), t["ke"].astype(BF16)
                q_in_b, k_st_b = t["q_in"].astype(BF16), t["k_st"].astype(BF16)
                a = jnp.where(t["causal"], _dot(qe_b, ke_b, NT), 0.0)
                da = jnp.where(t["causal"], _dot(dob, v, NT), 0.0).astype(BF16)
                dv = _dot(a.astype(BF16), dob, TN) + _dot(k_st_b, dst_b, NT)
                dqe = _dot(da, ke_b)
                dke = _dot(da, qe_b, TN)
                dq_in = _dot(dob, st_b)
                dk_st = _dot(v, dst_b)
                ddec = jnp.sum(dst * st_b.astype(F32), axis=0, keepdims=True)
                dst = _dot(dob, q_in_b, TN) + dst * t["dec"]
                dq = (dqe * t["e_q"] + dq_in * t["e_in"]) * (GLA_DK ** -0.5)
                dk = dke * t["e_k"] + dk_st * t["e_st"]
                w_q, w_k = dqe * t["qe"], dke * t["ke"]
                w_st = dk_st * t["k_st"]
                db = w_q - w_k + dq_in * t["q_in"] - w_st
                db_ref = jnp.sum(w_k - w_q, axis=0, keepdims=True)
                db_last = jnp.sum(w_st, axis=0, keepdims=True) + ddec * t["dec"]
                db = db + jnp.where(row == t["r_ref"], db_ref, 0.0) + jnp.where(row == t["r_last"], db_last, 0.0)
                dg_ref[rs, ksl] = _cumsum_rows(db, not reverse)
                if prev is not None:
                    dq, dk, dv = dq + pq[rs, ksl], dk + pk[rs, ksl], dv + pv[rs, vsl]
                dq_ref[rs, ksl] = dq.astype(out_dt)
                dk_ref[rs, ksl] = dk.astype(out_dt)
                dv_ref[rs, vsl] = dv.astype(out_dt)
            dstate[h] = dst

    q_spec, k_spec, v_spec = _gla_specs(order)
    kk = pl.BlockSpec((rows, GLA_K), lambda c: (order(c), 0))
    vv = pl.BlockSpec((rows, GLA_V), lambda c: (order(c), 0))
    in_specs = [q_spec, k_spec, v_spec, pl.BlockSpec((rows, GLA_K), lambda c: (order(c), gcol)),
                pl.BlockSpec((GLA_GROUP, GLA_HEADS, GLA_DV, GLA_DK), lambda c: (order(c), 0, 0, 0)), vv]
    operands = [proj, proj, proj, gates, states, do]
    if prev is not None:
        in_specs += [kk, kk, vv]
        operands += list(prev)
    return pl.pallas_call(
        body, name="gla_bwd_rev" if reverse else "gla_bwd", grid=(nb,), in_specs=in_specs, out_specs=[kk, kk, vv, kk],
        out_shape=[jax.ShapeDtypeStruct((S, GLA_K), out_dt), jax.ShapeDtypeStruct((S, GLA_K), out_dt),
                   jax.ShapeDtypeStruct((S, GLA_V), out_dt), jax.ShapeDtypeStruct((S, GLA_K), F32)],
        scratch_shapes=[pltpu.VMEM((GLA_HEADS, GLA_DV, GLA_DK), F32)],
    )(*operands)


def _gates_fwd(z, wg, bias, tm=512):
    S = z.shape[0]
    W = 2 * GLA_K

    def body(z_ref, w_ref, b_ref, o_ref):
        zg = _dot(z_ref[...], w_ref[...]) + b_ref[...]
        o_ref[...] = (jnp.minimum(zg, 0.0) - jnp.log(1.0 + jnp.exp(-jnp.abs(zg)))) * (1.0 / GATE_NORM)

    return pl.pallas_call(
        body, name="gates_fwd", grid=(S // tm,),
        in_specs=[pl.BlockSpec((tm, Z_W), lambda i: (i, 0)), pl.BlockSpec((Z_W, W), lambda i: (0, 0)),
                  pl.BlockSpec((1, W), lambda i: (0, 0))],
        out_specs=pl.BlockSpec((tm, W), lambda i: (i, 0)), out_shape=jax.ShapeDtypeStruct((S, W), F32),
    )(z, wg, bias)


def _gates_bwd(z, wg, bias, dg_f, dg_b, tm=512):
    S = z.shape[0]
    W = 2 * GLA_K

    def body(z_ref, w_ref, b_ref, dgf_ref, dgb_ref, dz_ref, dw_ref, db_ref):
        i = pl.program_id(0)

        @pl.when(i == 0)
        def _():
            dw_ref[...] = jnp.zeros_like(dw_ref)
            db_ref[...] = jnp.zeros_like(db_ref)

        zv = z_ref[...]
        zg = _dot(zv, w_ref[...]) + b_ref[...]
        dg = jnp.concatenate([dgf_ref[...], dgb_ref[...]], axis=1)
        dzg = dg * (1.0 / GATE_NORM) * _sigmoid(-zg)
        db_ref[...] += jnp.sum(dzg, axis=0, keepdims=True)
        dzg_b = dzg.astype(BF16)
        dw_ref[...] += _dot(zv, dzg_b, TN)
        dz_ref[...] = _dot(dzg_b, w_ref[...], NT).astype(BF16)

    half = pl.BlockSpec((tm, GLA_K), lambda i: (i, 0))
    return pl.pallas_call(
        body, name="gates_bwd", grid=(S // tm,),
        in_specs=[pl.BlockSpec((tm, Z_W), lambda i: (i, 0)), pl.BlockSpec((Z_W, W), lambda i: (0, 0)),
                  pl.BlockSpec((1, W), lambda i: (0, 0)), half, half],
        out_specs=[pl.BlockSpec((tm, Z_W), lambda i: (i, 0)), pl.BlockSpec((Z_W, W), lambda i: (0, 0)),
                   pl.BlockSpec((1, W), lambda i: (0, 0))],
        out_shape=[jax.ShapeDtypeStruct((S, Z_W), BF16), jax.ShapeDtypeStruct((Z_W, W), F32),
                   jax.ShapeDtypeStruct((1, W), F32)],
    )(z, wg, bias, dg_f, dg_b)


def _gla_out_fwd(o, proj, g, tm=512):
    S = o.shape[0]

    def body(o_ref, gr_ref, g_ref, out_ref):
        gn = g_ref[...]
        for h in range(GLA_HEADS):
            sl = slice(h * GLA_DV, (h + 1) * GLA_DV)
            ov = o_ref[:, sl]
            r = lax.rsqrt(jnp.mean(ov * ov, axis=-1, keepdims=True) + EPS)
            gr = gr_ref[:, sl].astype(F32)
            out_ref[:, sl] = (ov * r * gn * (gr * _sigmoid(gr))).astype(BF16)

    blk = pl.BlockSpec((tm, GLA_V), lambda i: (i, 0))
    return pl.pallas_call(
        body, name="gla_out_fwd", grid=(S // tm,),
        in_specs=[blk, pl.BlockSpec((tm, GLA_V), lambda i: (i, (3 * ATTN_W + 2 * GLA_K + GLA_V) // GLA_V)),
                  pl.BlockSpec((1, GLA_DV), lambda i: (0, 0))],
        out_specs=blk, out_shape=jax.ShapeDtypeStruct((S, GLA_V), BF16),
    )(o, proj, g)


def _gla_out_bwd(o, proj, g, dcat, tm=512):
    S = o.shape[0]

    def body(o_ref, gr_ref, g_ref, dgo_ref, do_ref, dgr_ref, dg_ref):
        i = pl.program_id(0)

        @pl.when(i == 0)
        def _():
            dg_ref[...] = jnp.zeros_like(dg_ref)

        gn = g_ref[...]
        dg_acc = jnp.zeros((1, GLA_DV), F32)
        for h in range(GLA_HEADS):
            sl = slice(h * GLA_DV, (h + 1) * GLA_DV)
            ov = o_ref[:, sl]
            r = lax.rsqrt(jnp.mean(ov * ov, axis=-1, keepdims=True) + EPS)
            yhat = ov * r
            gr = gr_ref[:, sl].astype(F32)
            sg = _sigmoid(gr)
            dgo = dgo_ref[:, sl].astype(F32)
            dgr_ref[:, sl] = (dgo * (yhat * gn) * (sg * (1.0 + gr * (1.0 - sg)))).astype(BF16)
            dy = dgo * (gr * sg)
            dg_acc = dg_acc + jnp.sum(dy * yhat, axis=0, keepdims=True)
            t = dy * gn
            do_ref[:, sl] = r * (t - yhat * jnp.mean(t * yhat, axis=-1, keepdims=True))
        dg_ref[...] += dg_acc

    blk = pl.BlockSpec((tm, GLA_V), lambda i: (i, 0))
    vec = pl.BlockSpec((1, GLA_DV), lambda i: (0, 0))
    return pl.pallas_call(
        body, name="gla_out_bwd", grid=(S // tm,),
        in_specs=[blk, pl.BlockSpec((tm, GLA_V), lambda i: (i, (3 * ATTN_W + 2 * GLA_K + GLA_V) // GLA_V)), vec,
                  pl.BlockSpec((tm, GLA_V), lambda i: (i, 1))],
        out_specs=[blk, blk, vec],
        out_shape=[jax.ShapeDtypeStruct((S, GLA_V), F32), jax.ShapeDtypeStruct((S, GLA_V), BF16),
                   jax.ShapeDtypeStruct((1, GLA_DV), F32)],
    )(o, proj, g, dcat)


HALO = 16


def _halo_specs(tm, tn, S):
    cur = pl.BlockSpec((tm, tn), lambda j, i: (i, j))
    prev = pl.BlockSpec((HALO, tn), lambda j, i: (jnp.maximum(i * (tm // HALO) - 1, 0), j))
    nxt = pl.BlockSpec((HALO, tn), lambda j, i: (jnp.minimum((i + 1) * (tm // HALO), S // HALO - 1), j))
    return cur, prev, nxt


def _shifted(c_ref, p_ref, n_ref, n_blocks):
    i = pl.program_id(1)
    x = c_ref[...].astype(F32)
    tm = x.shape[0]
    row = lax.broadcasted_iota(jnp.int32, x.shape, 0)
    before = p_ref[HALO - 1:HALO, :].astype(F32) * (i > 0).astype(F32)
    after = n_ref[0:1, :].astype(F32) * (i < n_blocks - 1).astype(F32)
    x_m1 = jnp.where(row == 0, before, pltpu.roll(x, 1, axis=0))
    x_p1 = jnp.where(row == tm - 1, after, pltpu.roll(x, tm - 1, axis=0))
    return x, x_m1, x_p1


def _glu_fwd(gp, up, cw, cb, tm=512, tn=1408):
    S = gp.shape[0]
    nb = S // tm

    def body(c_ref, p_ref, n_ref, up_ref, w_ref, b_ref, o_ref):
        x, x_m1, x_p1 = _shifted(c_ref, p_ref, n_ref, nb)
        w = w_ref[...]
        gate = w[0:1, :] * x_m1 + w[1:2, :] * x + w[2:3, :] * x_p1 + b_ref[...]
        o_ref[...] = (gate * _sigmoid(gate) * up_ref[...].astype(F32)).astype(BF16)

    cur, prev, nxt = _halo_specs(tm, tn, S)
    return pl.pallas_call(
        body, name="glu_fwd", grid=(D_FF // tn, nb),
        in_specs=[cur, prev, nxt, cur, pl.BlockSpec((3, tn), lambda j, i: (0, j)), pl.BlockSpec((1, tn), lambda j, i: (0, j))],
        out_specs=cur, out_shape=jax.ShapeDtypeStruct((S, D_FF), BF16),
    )(gp, gp, gp, up, cw, cb)


def _glu_bwd(gp, up, dact, cw, cb, tm=512, tn=1408):
    S = gp.shape[0]
    nb = S // tm

    def body(c_ref, p_ref, n_ref, up_ref, da_ref, w_ref, b_ref, dup_ref, dgate_ref, dw_ref, db_ref):
        @pl.when(pl.program_id(1) == 0)
        def _():
            dw_ref[...] = jnp.zeros_like(dw_ref)
            db_ref[...] = jnp.zeros_like(db_ref)

        x, x_m1, x_p1 = _shifted(c_ref, p_ref, n_ref, nb)
        w = w_ref[...]
        gate = w[0:1, :] * x_m1 + w[1:2, :] * x + w[2:3, :] * x_p1 + b_ref[...]
        sg = _sigmoid(gate)
        da = da_ref[...].astype(F32)
        dup_ref[...] = (da * (gate * sg)).astype(BF16)
        dgate = da * up_ref[...].astype(F32) * (sg * (1.0 + gate * (1.0 - sg)))
        dgate_ref[...] = dgate.astype(BF16)
        db_ref[...] += jnp.sum(dgate, axis=0, keepdims=True)
        dw_ref[...] += jnp.concatenate(
            [jnp.sum(dgate * x_m1, axis=0, keepdims=True), jnp.sum(dgate * x, axis=0, keepdims=True),
             jnp.sum(dgate * x_p1, axis=0, keepdims=True)], axis=0)

    cur, prev, nxt = _halo_specs(tm, tn, S)
    w_spec = pl.BlockSpec((3, tn), lambda j, i: (0, j))
    b_spec = pl.BlockSpec((1, tn), lambda j, i: (0, j))
    return pl.pallas_call(
        body, name="glu_bwd", grid=(D_FF // tn, nb), in_specs=[cur, prev, nxt, cur, cur, w_spec, b_spec],
        out_specs=[cur, cur, w_spec, b_spec],
        out_shape=[jax.ShapeDtypeStruct((S, D_FF), BF16), jax.ShapeDtypeStruct((S, D_FF), BF16),
                   jax.ShapeDtypeStruct((3, D_FF), F32), jax.ShapeDtypeStruct((1, D_FF), F32)],
    )(gp, gp, gp, up, dact, cw, cb)


def _conv_bwd_input(dgate, cw, tm=512, tn=1408):
    S = dgate.shape[0]
    nb = S // tm

    def body(c_ref, p_ref, n_ref, w_ref, o_ref):
        x, x_m1, x_p1 = _shifted(c_ref, p_ref, n_ref, nb)
        w = w_ref[...]
        o_ref[...] = (w[0:1, :] * x_p1 + w[1:2, :] * x + w[2:3, :] * x_m1).astype(BF16)

    cur, prev, nxt = _halo_specs(tm, tn, S)
    return pl.pallas_call(
        body, name="conv_bwd_input", grid=(D_FF // tn, nb),
        in_specs=[cur, prev, nxt, pl.BlockSpec((3, tn), lambda j, i: (0, j))], out_specs=cur,
        out_shape=jax.ShapeDtypeStruct((S, D_FF), BF16),
    )(dgate, dgate, dgate, cw)


def _local_step(x, target, norm1_g, w_in_t, wg, gate_bias, gla_norm_g, attn_norm_g, w_out, norm2_g,
                w_gate4, w_up4, conv_w, conv_b, w_down, final_norm_g, on_grad=lambda event, arrays: ()):
    S = x.shape[0]
    tabs = _rope_tables(S)

    n1 = _rms_fwd("rms1_fwd", x, norm1_g)
    z_block = IN_MAIN // Z_W
    proj = _mm_nt("in_proj", n1, w_in_t, 1024, 1536, BF16, n_out=IN_MAIN)
    z = _matmul(
        "in_proj_z",
        [(n1, pl.BlockSpec((1024, D_MODEL), lambda i: (i, 0)), w_in_t, pl.BlockSpec((Z_W, D_MODEL), lambda i: (z_block, 0)), NT)],
        (S // 1024,), jax.ShapeDtypeStruct((S, Z_W), BF16), pl.BlockSpec((1024, Z_W), lambda i: (i, 0)), 1)
    qk = _rope_fwd(proj, tabs)
    branch = [_attn_fwd(qk, proj, d) for d in DILATIONS]
    ao, o_attn, lse = _attn_combine([b[0] for b in branch], [b[1] for b in branch], attn_norm_g)
    gates = _gates_fwd(z, wg, gate_bias)
    o_f, st_f = _gla_fwd(proj, gates, False)
    o_gla, st_b = _gla_fwd(proj, gates, True, o_prev=o_f)
    go = _gla_out_fwd(o_gla, proj, gla_norm_g)
    cat = jnp.concatenate([ao, go], axis=1)
    h1 = _mm_nn("out_proj", cat, w_out, 1024, 1024, F32, res=x)
    n2 = _rms_fwd("rms2_fwd", h1, norm2_g)
    gp = _mm_nn_sharded("ffn_gate", n2, w_gate4, 1024, BF16)
    up = _mm_nn_sharded("ffn_up", n2, w_up4, 1024, BF16)
    act = _glu_fwd(gp, up, conv_w, conv_b)
    tk = D_FF // N_CHIPS
    h2 = _matmul(
        "ffn_down",
        [(act, pl.BlockSpec((1024, tk), lambda i, j, k: (i, k)), w_down, pl.BlockSpec((tk, 1024), lambda i, j, k: (k, j)), NN)],
        (S // 1024, D_MODEL // 1024, N_CHIPS), jax.ShapeDtypeStruct((S, D_MODEL), F32),
        pl.BlockSpec((1024, 1024), lambda i, j, k: (i, j)), N_CHIPS,
        res=(h1, pl.BlockSpec((1024, 1024), lambda i, j, k: (i, j))))
    loss_row, d_final_g, dh2, dh2_b = _final_loss(h2, final_norm_g.reshape(1, D_MODEL), target)

    dact = _mm_nt("ffn_down_bwd", dh2_b, w_down, 1024, tk, BF16)
    dup, dgate, d_conv_w, d_conv_b = _glu_bwd(gp, up, dact, conv_w, conv_b)
    dgp = _conv_bwd_input(dgate, conv_w)
    d_w_down = _mm_tn("ffn_down_wgrad", act, dh2_b, tk, D_MODEL, 1024, BF16)
    on_grad("w_down", dict(w_down=d_w_down))
    dgp = _after(dgp, d_w_down)
    d_w_gate4 = _mm_tn("ffn_gate_wgrad", n2, dgp, D_MODEL, tk, 1024, BF16, out3=tk)
    dup = _after(dup, d_w_gate4)
    d_w_up4 = _mm_tn("ffn_up_wgrad", n2, dup, D_MODEL, tk, 1024, BF16, out3=tk)
    held = on_grad("w_gate_w_up", dict(w_gate=d_w_gate4, w_up=d_w_up4))
    dgp = _after(dgp, d_w_up4, *held)
    dn2 = _matmul(
        "ffn_in_bwd",
        [(dgp, pl.BlockSpec((1024, tk), lambda i, j, k: (i, k)), w_gate4, pl.BlockSpec((None, 1024, tk), lambda i, j, k: (k, j, 0)), NT),
         (dup, pl.BlockSpec((1024, tk), lambda i, j, k: (i, k)), w_up4, pl.BlockSpec((None, 1024, tk), lambda i, j, k: (k, j, 0)), NT)],
        (S // 1024, D_MODEL // 1024, N_CHIPS), jax.ShapeDtypeStruct((S, D_MODEL), F32),
        pl.BlockSpec((1024, 1024), lambda i, j, k: (i, j)), N_CHIPS)
    dh1, dh1_b, d_norm2_g = _rms_bwd("rms2_bwd", h1, norm2_g, dn2, dh2)

    d_w_out = _mm_tn("out_proj_wgrad", cat, dh1_b, D_MODEL, 1024, 1024, BF16)
    held = on_grad("w_out", dict(w_out=d_w_out))
    dcat = _mm_nt("out_proj_bwd", _after(dh1_b, d_w_out, *held), w_out, 1024, 1024, BF16)
    do_attn, delta, d_attn_norm_g = _attn_norm_bwd(o_attn, attn_norm_g, dcat)
    dqs, dks, dvs = [], [], []
    for d in DILATIONS:
        dqs.append(_attn_bwd_dq(qk, proj, do_attn, lse, delta, d))
        dk, dv = _attn_bwd_dkv(qk, proj, do_attn, lse, delta, d)
        dks.append(dk)
        dvs.append(dv)
    d_attn = _attn_grad_merge(dqs, dks, dvs, tabs)
    held = on_grad("mid", dict(anchor=d_attn))
    do_gla, dgr, d_gla_norm_g = _gla_out_bwd(o_gla, proj, gla_norm_g, _after(dcat, *held))
    dq_f, dk_f, dv_f, dg_f = _gla_bwd(proj, gates, st_f, do_gla, False)
    dgq, dgk, dgv, dg_b = _gla_bwd(proj, gates, st_b, do_gla, True, prev=(dq_f, dk_f, dv_f))
    dz, d_wg, d_gate_bias = _gates_bwd(z, wg, gate_bias, dg_f, dg_b)
    dproj = jnp.concatenate([d_attn, dgq, dgk, dgv, dgr], axis=1)
    d_w_in_t = _mm_tn("in_proj_wgrad", dproj, n1, 1536, D_MODEL, 1024, BF16, rows_out=IN_W)
    n_tok = S // 1024
    d_w_in_t = _matmul(
        "in_proj_z_wgrad",
        [(dz, pl.BlockSpec((1024, Z_W), lambda i, j, k: (k, 0)), n1, pl.BlockSpec((1024, D_MODEL), lambda i, j, k: (k, 0)), TN)],
        (1, 1, n_tok), jax.ShapeDtypeStruct((IN_W, D_MODEL), BF16), pl.BlockSpec((Z_W, D_MODEL), lambda i, j, k: (z_block, 0)),
        n_tok, into=d_w_in_t)
    held = on_grad("w_in", dict(w_in_t=d_w_in_t))
    tkm = IN_MAIN // 4
    half = S // 2048

    def in_proj_bwd(name, first, a, into):
        return _matmul(
            name,
            [(a, pl.BlockSpec((1024, tkm), lambda i, j, k: (i + first, k)), w_in_t, pl.BlockSpec((tkm, 1024), lambda i, j, k: (k, j)), NN)],
            (half, D_MODEL // 1024, 4), jax.ShapeDtypeStruct((S, D_MODEL), F32),
            pl.BlockSpec((1024, 1024), lambda i, j, k: (i + first, j)), 4, into=into)

    dproj = _after(dproj, d_w_in_t, *held)
    dn1 = in_proj_bwd("in_proj_bwd_a", 0, dproj, None)
    held = on_grad("last", dict(last=dn1))
    dn1 = in_proj_bwd("in_proj_bwd_b", half, dproj, _after(dn1, *held))
    dn1 = _matmul(
        "in_proj_z_bwd",
        [(dz, pl.BlockSpec((1024, Z_W), lambda j, i: (i, 0)), w_in_t, pl.BlockSpec((Z_W, 1024), lambda j, i: (z_block, j)), NN)],
        (D_MODEL // 1024, S // 1024), jax.ShapeDtypeStruct((S, D_MODEL), F32), pl.BlockSpec((1024, 1024), lambda j, i: (i, j)), 1,
        res=(dn1, pl.BlockSpec((1024, 1024), lambda j, i: (i, j))))
    grad_x, _, d_norm1_g = _rms_bwd("rms1_bwd", x, norm1_g, dn1, dh1)

    big = dict(w_in_t=d_w_in_t, w_out=d_w_out, w_gate4=d_w_gate4, w_up4=d_w_up4, w_down=d_w_down)
    small = dict(loss=loss_row, norm1_g=d_norm1_g, wg=d_wg, gate_bias=d_gate_bias, gla_norm_g=d_gla_norm_g,
                 attn_norm_g=d_attn_norm_g, norm2_g=d_norm2_g, conv_w=d_conv_w, conv_b=d_conv_b, final_norm_g=d_final_g)
    return grad_x, big, small


def _position():
    return lax.axis_index("x"), lax.axis_index("y"), lax.axis_index("c")


def _other_chips(x, y):
    return [(1 - x, y), (x, 1 - y), (1 - x, 1 - y)]


def _gather_chips(name, shards):
    n = len(shards)

    def body(*refs):
        ins, outs = refs[:n], refs[n:2 * n]
        send, recv, loc = refs[2 * n:]
        x, y, c = _position()
        me = 2 * x + y
        chips = _other_chips(x, y)
        started = []
        for w in range(n):
            own = pltpu.make_async_copy(ins[w], outs[w].at[me], loc.at[w])
            own.start()
            started.append(own)
        sends = []
        for w in range(n):
            for j, (px, py) in enumerate(chips):
                cp = pltpu.make_async_remote_copy(ins[w], outs[w].at[me], send.at[3 * w + j], recv.at[3 * w + j],
                                                  device_id=(px, py, c), device_id_type=MESH)
                cp.start()
                sends.append(cp)
        for w in range(n):
            for j, (px, py) in enumerate(chips):
                pltpu.make_async_remote_copy(ins[w], outs[w].at[2 * px + py], send.at[3 * w + j], recv.at[3 * w + j],
                                             device_id=(px, py, c), device_id_type=MESH).wait_recv()
        for cp in sends:
            cp.wait_send()
        for own in started:
            own.wait()

    return pl.pallas_call(
        body, name=name, in_specs=[ANY] * n, out_specs=[ANY] * n,
        out_shape=[jax.ShapeDtypeStruct((N_CHIPS,) + s.shape, s.dtype) for s in shards],
        scratch_shapes=[pltpu.SemaphoreType.DMA((3 * n,)), pltpu.SemaphoreType.DMA((3 * n,)), pltpu.SemaphoreType.DMA((n,))],
    )(*shards)


def _gather_chips_async(name, shards, collective_id):
    n = len(shards)

    def body(*refs):
        ins, outs = refs[:n], refs[n:2 * n]
        send, recv, loc = refs[2 * n:]
        x, y, c = _position()
        me = 2 * x + y
        chips = _other_chips(x, y)
        barrier = pltpu.get_barrier_semaphore()
        for px, py in chips:
            pl.semaphore_signal(barrier, inc=1, device_id=(px, py, c), device_id_type=MESH)
        pl.semaphore_wait(barrier, len(chips))
        started = []
        for w in range(n):
            own = pltpu.make_async_copy(ins[w], outs[w].at[me], loc.at[w])
            own.start()
            started.append(own)
        sends = []
        for w in range(n):
            for j, (px, py) in enumerate(chips):
                cp = pltpu.make_async_remote_copy(ins[w], outs[w].at[me], send.at[3 * w + j], recv.at[3 * w + j],
                                                  device_id=(px, py, c), device_id_type=MESH)
                cp.start()
                sends.append(cp)
        for w in range(n):
            for j, (px, py) in enumerate(chips):
                pltpu.make_async_remote_copy(ins[w], outs[w].at[2 * px + py], send.at[3 * w + j], recv.at[3 * w + j],
                                             device_id=(px, py, c), device_id_type=MESH).wait_recv()
        for cp in sends:
            cp.wait_send()
        for own in started:
            own.wait()

    return pl.kernel(
        body, name=name, mesh=_sequencer(),
        out_type=[jax.ShapeDtypeStruct((N_CHIPS,) + s.shape, s.dtype) for s in shards],
        scratch_types=[pltpu.SemaphoreType.DMA((3 * n,)), pltpu.SemaphoreType.DMA((3 * n,)), pltpu.SemaphoreType.DMA((n,))],
        compiler_params=pltpu.CompilerParams(collective_id=collective_id),
    )(*shards)


def _gather_halves_async(name, small, shard, collective_id):
    half = shard.shape[1] // 2

    def body(small_ref, shard_ref, small_out, out, send, recv, loc):
        x, y, c = _position()
        me = 2 * x + y
        sibling = (x, y, 1 - c)
        chips = _other_chips(x, y)
        barrier = pltpu.get_barrier_semaphore()
        for px, py in chips:
            pl.semaphore_signal(barrier, inc=1, device_id=(px, py, c), device_id_type=MESH)
        pl.semaphore_signal(barrier, inc=1, device_id=sibling, device_id_type=MESH)
        pl.semaphore_wait(barrier, len(chips) + 1)
        mine = pl.ds(pl.multiple_of(c * half, LANES), half)
        theirs = pl.ds(pl.multiple_of((1 - c) * half, LANES), half)
        own = [pltpu.make_async_copy(small_ref, small_out.at[me], loc.at[0]),
               pltpu.make_async_copy(shard_ref, out.at[me], loc.at[1])]
        for cp in own:
            cp.start()
        sends = []
        for j, (px, py) in enumerate(chips):
            sends.append(pltpu.make_async_remote_copy(small_ref, small_out.at[me], send.at[j], recv.at[j],
                                                      device_id=(px, py, c), device_id_type=MESH))
            sends.append(pltpu.make_async_remote_copy(shard_ref.at[:, mine], out.at[me, :, mine], send.at[3 + j], recv.at[3 + j],
                                                      device_id=(px, py, c), device_id_type=MESH))
        for cp in sends:
            cp.start()
        passed = []
        for j, (px, py) in enumerate(chips):
            slot = 2 * px + py
            pltpu.make_async_remote_copy(shard_ref.at[:, mine], out.at[slot, :, mine], send.at[3 + j], recv.at[3 + j],
                                         device_id=(px, py, c), device_id_type=MESH).wait_recv()
            cp = pltpu.make_async_remote_copy(out.at[slot, :, mine], out.at[slot, :, mine], send.at[6 + j], recv.at[6 + j],
                                              device_id=sibling, device_id_type=MESH)
            cp.start()
            passed.append(cp)
        for j, (px, py) in enumerate(chips):
            slot = 2 * px + py
            pltpu.make_async_remote_copy(small_ref, small_out.at[slot], send.at[j], recv.at[j],
                                         device_id=(px, py, c), device_id_type=MESH).wait_recv()
            pltpu.make_async_remote_copy(out.at[slot, :, theirs], out.at[slot, :, theirs], send.at[6 + j], recv.at[6 + j],
                                         device_id=sibling, device_id_type=MESH).wait_recv()
        for cp in sends + passed:
            cp.wait_send()
        for cp in own:
            cp.wait()

    return pl.kernel(
        body, name=name, mesh=_sequencer(),
        out_type=[jax.ShapeDtypeStruct((N_CHIPS,) + small.shape, small.dtype),
                  jax.ShapeDtypeStruct((N_CHIPS,) + shard.shape, shard.dtype)],
        scratch_types=[pltpu.SemaphoreType.DMA((9,)), pltpu.SemaphoreType.DMA((9,)), pltpu.SemaphoreType.DMA((2,))],
        compiler_params=pltpu.CompilerParams(collective_id=collective_id),
    )(small, shard)


def _sibling_exchange(name, arrs):
    n = len(arrs)

    def body(*refs):
        ins, outs = refs[:n], refs[n:2 * n]
        send, recv = refs[2 * n:]
        x, y, c = _position()
        copies = [pltpu.make_async_remote_copy(ins[w], outs[w], send.at[w], recv.at[w], device_id=(x, y, 1 - c),
                                               device_id_type=MESH) for w in range(n)]
        for cp in copies:
            cp.start()
        for cp in copies:
            cp.wait()

    return pl.pallas_call(
        body, name=name, in_specs=[ANY] * n, out_specs=[ANY] * n,
        out_shape=[jax.ShapeDtypeStruct(a.shape, a.dtype) for a in arrs],
        scratch_shapes=[pltpu.SemaphoreType.DMA((n,)), pltpu.SemaphoreType.DMA((n,))],
    )(*arrs)


def _scatter_chips(name, parts):
    n = len(parts)

    def body(*refs):
        ins, outs = refs[:n], refs[n:2 * n]
        send, recv, loc = refs[2 * n:]
        x, y, c = _position()
        me = 2 * x + y
        chips = _other_chips(x, y)
        started = []
        for w in range(n):
            own = pltpu.make_async_copy(ins[w].at[me], outs[w].at[me], loc.at[w])
            own.start()
            started.append(own)
        sends = []
        for w in range(n):
            for j, (px, py) in enumerate(chips):
                cp = pltpu.make_async_remote_copy(ins[w].at[2 * px + py], outs[w].at[me], send.at[3 * w + j],
                                                  recv.at[3 * w + j], device_id=(px, py, c), device_id_type=MESH)
                cp.start()
                sends.append(cp)
        for w in range(n):
            for j, (px, py) in enumerate(chips):
                pltpu.make_async_remote_copy(ins[w].at[me], outs[w].at[2 * px + py], send.at[3 * w + j], recv.at[3 * w + j],
                                             device_id=(px, py, c), device_id_type=MESH).wait_recv()
        for cp in sends:
            cp.wait_send()
        for own in started:
            own.wait()

    return pl.pallas_call(
        body, name=name, in_specs=[ANY] * n, out_specs=[ANY] * n,
        out_shape=[jax.ShapeDtypeStruct(p.shape, p.dtype) for p in parts],
        scratch_shapes=[pltpu.SemaphoreType.DMA((3 * n,)), pltpu.SemaphoreType.DMA((3 * n,)), pltpu.SemaphoreType.DMA((n,))],
    )(*parts)


def _sequencer():
    return plsc.ScalarSubcoreMesh(axis_name="sequencer", num_cores=1)


def _sibling_exchange_async(name, arrs, collective_id):
    n = len(arrs)

    def body(*refs):
        ins, outs = refs[:n], refs[n:2 * n]
        send, recv = refs[2 * n:]
        x, y, c = _position()
        sibling = (x, y, 1 - c)
        barrier = pltpu.get_barrier_semaphore()
        pl.semaphore_signal(barrier, inc=1, device_id=sibling, device_id_type=MESH)
        pl.semaphore_wait(barrier, 1)
        copies = [pltpu.make_async_remote_copy(ins[w], outs[w], send.at[w], recv.at[w], device_id=sibling,
                                               device_id_type=MESH) for w in range(n)]
        for cp in copies:
            cp.start()
        for cp in copies:
            cp.wait()

    return pl.kernel(
        body, name=name, out_type=[jax.ShapeDtypeStruct(a.shape, a.dtype) for a in arrs],
        scratch_types=[pltpu.SemaphoreType.DMA((n,)), pltpu.SemaphoreType.DMA((n,))],
        compiler_params=pltpu.CompilerParams(collective_id=collective_id), mesh=_sequencer(),
    )(*arrs)


def _scatter_chips_async(name, parts, collective_id):
    n = len(parts)

    def body(*refs):
        ins, outs = refs[:n], refs[n:2 * n]
        send, recv, loc = refs[2 * n:]
        x, y, c = _position()
        me = 2 * x + y
        chips = _other_chips(x, y)
        barrier = pltpu.get_barrier_semaphore()
        for px, py in chips:
            pl.semaphore_signal(barrier, inc=1, device_id=(px, py, c), device_id_type=MESH)
        pl.semaphore_wait(barrier, len(chips))
        started = []
        for w in range(n):
            own = pltpu.make_async_copy(ins[w].at[me], outs[w].at[me], loc.at[w])
            own.start()
            started.append(own)
        sends = []
        for w in range(n):
            for j, (px, py) in enumerate(chips):
                cp = pltpu.make_async_remote_copy(ins[w].at[2 * px + py], outs[w].at[me], send.at[3 * w + j],
                                                  recv.at[3 * w + j], device_id=(px, py, c), device_id_type=MESH)
                cp.start()
                sends.append(cp)
        for w in range(n):
            for j, (px, py) in enumerate(chips):
                pltpu.make_async_remote_copy(ins[w].at[me], outs[w].at[2 * px + py], send.at[3 * w + j], recv.at[3 * w + j],
                                             device_id=(px, py, c), device_id_type=MESH).wait_recv()
        for cp in sends:
            cp.wait_send()
        for own in started:
            own.wait()

    return pl.kernel(
        body, name=name, out_type=[jax.ShapeDtypeStruct(p.shape, p.dtype) for p in parts],
        scratch_types=[pltpu.SemaphoreType.DMA((3 * n,)), pltpu.SemaphoreType.DMA((3 * n,)), pltpu.SemaphoreType.DMA((n,))],
        compiler_params=pltpu.CompilerParams(collective_id=collective_id), mesh=_sequencer(),
    )(*parts)


def _allreduce_rows(buf):
    R = buf.shape[0]

    def body(in_ref, out_ref, land, send, recv):
        x, y, c = _position()
        me = 4 * x + 2 * y + c
        land[pl.ds(me, 1)] = in_ref[...][None]
        peers = []
        for mask in range(1, N_DEV):
            px = 1 - x if mask & 4 else x
            py = 1 - y if mask & 2 else y
            pc = 1 - c if mask & 1 else c
            peers.append((px, py, pc))
        sends = []
        for k, peer in enumerate(peers):
            cp = pltpu.make_async_remote_copy(in_ref, land.at[me], send.at[k], recv.at[k], device_id=peer, device_id_type=MESH)
            cp.start()
            sends.append(cp)
        for k, (px, py, pc) in enumerate(peers):
            pltpu.make_async_remote_copy(in_ref, land.at[4 * px + 2 * py + pc], send.at[k], recv.at[k],
                                         device_id=(px, py, pc), device_id_type=MESH).wait_recv()
        for cp in sends:
            cp.wait_send()
        tot = land[0]
        for i in range(1, N_DEV):
            tot = tot + land[i]
        out_ref[...] = tot

    vm = pl.BlockSpec(memory_space=pltpu.VMEM)
    return pl.pallas_call(
        body, name="allreduce_small", in_specs=[vm], out_specs=vm, out_shape=jax.ShapeDtypeStruct((R, LANES), F32),
        scratch_shapes=[pltpu.VMEM((N_DEV, R, LANES), F32), pltpu.SemaphoreType.DMA((N_DEV - 1,)),
                        pltpu.SemaphoreType.DMA((N_DEV - 1,))],
    )(buf)


def _tile2d(r, c, cap):
    if r <= cap:
        return r, c
    fits = [t for t in range(16, cap + 1, 16) if r % t == 0]
    return (max(fits), c) if fits else (r, 256)


def _pair_sum(name, a, b):
    n, r, c = a.shape
    tr, tc = _tile2d(r, c, 1024)

    def body(a_ref, b_ref, o_ref):
        o_ref[...] = (a_ref[...].astype(F32) + b_ref[...].astype(F32)).astype(BF16)

    blk = pl.BlockSpec((None, tr, tc), lambda s, i, j: (s, i, j))
    return pl.pallas_call(
        body, name=name, grid=(n, r // tr, c // tc), in_specs=[blk, blk], out_specs=blk,
        out_shape=jax.ShapeDtypeStruct(a.shape, BF16),
    )(a, b)


def _adamw_math(w, m, v, g):
    m2 = ADAM_B1 * m + (1.0 - ADAM_B1) * g
    v2 = ADAM_B2 * v + (1.0 - ADAM_B2) * (g * g)
    m_hat = m2 / (1.0 - ADAM_B1 ** ADAM_STEP)
    v_hat = v2 / (1.0 - ADAM_B2 ** ADAM_STEP)
    delta = -ADAM_LR * (m_hat / (jnp.sqrt(v_hat) + ADAM_EPS) + ADAM_WD * w)
    return delta, m2, v2


def _adamw(name, w, m, v, g):
    r, c = w.shape
    stacked = g.ndim == 3
    tr, tc = _tile2d(r, c, 256)

    def body(w_ref, m_ref, v_ref, g_ref, go_ref, d_ref, m2_ref, v2_ref):
        if stacked:
            gv = g_ref[0].astype(F32)
            for i in range(1, N_CHIPS):
                gv = gv + g_ref[i].astype(F32)
        else:
            gv = g_ref[...]
        delta, m2, v2 = _adamw_math(w_ref[...], m_ref[...], v_ref[...], gv)
        go_ref[...] = gv
        d_ref[...] = delta
        m2_ref[...] = m2
        v2_ref[...] = v2

    blk = pl.BlockSpec((tr, tc), lambda i, j: (i, j))
    g_spec = pl.BlockSpec((N_CHIPS, tr, tc), lambda i, j: (0, i, j)) if stacked else blk
    out = jax.ShapeDtypeStruct((r, c), F32)
    return pl.pallas_call(
        body, name=name, grid=(r // tr, c // tc), in_specs=[blk, blk, blk, g_spec], out_specs=[blk] * 4, out_shape=[out] * 4,
    )(w, m, v, g)


def _pack_rows(pieces):
    flat = jnp.concatenate([p.reshape(-1) for p in pieces])
    rows = flat.shape[0] // LANES
    pad = (-rows) % 8
    return jnp.pad(flat.reshape(rows, LANES), ((0, pad), (0, 0)))


def _unpack_rows(buf, shapes):
    flat = buf.reshape(-1)
    out, at = [], 0
    for s in shapes:
        size = math.prod(s)
        out.append(flat[at:at + size].reshape(s))
        at += size
    return out


SMALL_NAMES = ("norm1_g", "gf_up", "gf_b", "gb_up", "gb_b", "gla_norm_g", "attn_norm_g", "norm2_g", "conv_w", "conv_b",
               "final_norm_g")
BIG_NAMES = ("w_in", "w_out", "w_gate", "w_up", "w_down")
WEIGHT_ORDER = ("norm1_g", "w_in", "gf_up", "gf_b", "gb_up", "gb_b", "gla_norm_g", "attn_norm_g", "w_out", "norm2_g",
                "w_gate", "w_up", "conv_w", "conv_b", "w_down", "final_norm_g")


def kernel(x, norm1_g, w_in, gf_up, gf_b, gb_up, gb_b, gla_norm_g, attn_norm_g, w_out, norm2_g, w_gate, w_up, conv_w, conv_b, w_down, final_norm_g, loss_target, m_norm1_g, m_w_in, m_gf_up, m_gf_b, m_gb_up, m_gb_b, m_gla_norm_g, m_attn_norm_g, m_w_out, m_norm2_g, m_w_gate, m_w_up, m_conv_w, m_conv_b, m_w_down, m_final_norm_g, v_norm1_g, v_w_in, v_gf_up, v_gf_b, v_gb_up, v_gb_b, v_gla_norm_g, v_attn_norm_g, v_w_out, v_norm2_g, v_w_gate, v_w_up, v_conv_w, v_conv_b, v_w_down, v_final_norm_g):
    w = dict(norm1_g=norm1_g, w_in=w_in, gf_up=gf_up, gf_b=gf_b, gb_up=gb_up, gb_b=gb_b, gla_norm_g=gla_norm_g,
             attn_norm_g=attn_norm_g, w_out=w_out, norm2_g=norm2_g, w_gate=w_gate, w_up=w_up, conv_w=conv_w, conv_b=conv_b,
             w_down=w_down, final_norm_g=final_norm_g)
    m = dict(norm1_g=m_norm1_g, w_in=m_w_in, gf_up=m_gf_up, gf_b=m_gf_b, gb_up=m_gb_up, gb_b=m_gb_b, gla_norm_g=m_gla_norm_g,
             attn_norm_g=m_attn_norm_g, w_out=m_w_out, norm2_g=m_norm2_g, w_gate=m_w_gate, w_up=m_w_up, conv_w=m_conv_w,
             conv_b=m_conv_b, w_down=m_w_down, final_norm_g=m_final_norm_g)
    v = dict(norm1_g=v_norm1_g, w_in=v_w_in, gf_up=v_gf_up, gf_b=v_gf_b, gb_up=v_gb_up, gb_b=v_gb_b, gla_norm_g=v_gla_norm_g,
             attn_norm_g=v_attn_norm_g, w_out=v_w_out, norm2_g=v_norm2_g, w_gate=v_w_gate, w_up=v_w_up, conv_w=v_conv_w,
             conv_b=v_conv_b, w_down=v_w_down, final_norm_g=v_final_norm_g)
    S = x.shape[1]
    chip = 2 * lax.axis_index("x") + lax.axis_index("y")
    n_in = IN_W // N_CHIPS
    n_ff = D_FF // N_CHIPS
    n_gk = GLA_K // N_CHIPS

    def owned(t):
        return {k: (jnp.transpose(t[k][0]) if k == "w_in" else t[k][0]) for k in BIG_NAMES}

    own_w, own_m, own_v = owned(w), owned(m), owned(v)
    shard = {k: own_w[k].astype(BF16) for k in BIG_NAMES}
    small_shard = _pack_rows([gf_up[0], gb_up[0], conv_w[0]])
    small4, w_in4 = _gather_halves_async("gather_w_in", small_shard, shard["w_in"], 0)
    w_out4, w_gate4, w_up4 = _gather_chips_async("gather_w_mid", [shard["w_out"], shard["w_gate"], shard["w_up"]], 1)
    (w_down4,) = _gather_chips_async("gather_w_down", [shard["w_down"]], 2)
    w_in_t = w_in4.reshape(IN_W, D_MODEL)
    rows_up = GATE_RANK * n_gk // LANES
    rows_cw = 3 * n_ff // LANES
    gf_full = jnp.transpose(small4[:, 0:rows_up].reshape(N_CHIPS, GATE_RANK, n_gk), (1, 0, 2)).reshape(GATE_RANK, GLA_K)
    gb_full = jnp.transpose(small4[:, rows_up:2 * rows_up].reshape(N_CHIPS, GATE_RANK, n_gk), (1, 0, 2)).reshape(GATE_RANK, GLA_K)
    cw_full = jnp.transpose(small4[:, 2 * rows_up:2 * rows_up + rows_cw].reshape(N_CHIPS, 3, n_ff), (1, 0, 2)).reshape(3, D_FF)
    wg = jnp.zeros((Z_W, 2 * GLA_K), F32)
    wg = wg.at[0:GATE_RANK, 0:GLA_K].set(gf_full).at[GATE_RANK:2 * GATE_RANK, GLA_K:].set(gb_full).astype(BF16)
    gate_bias = jnp.concatenate([gf_b, gb_b], axis=1)

    pending, contributions, next_id = [], {}, [3]

    def as_shards(group, arrays):
        if group == "w_in":
            return dict(w_in=arrays["w_in_t"].reshape(N_CHIPS, n_in, D_MODEL))
        if group == "w_out":
            return dict(w_out=arrays["w_out"].reshape(N_CHIPS, D_MODEL // N_CHIPS, D_MODEL))
        if group == "w_down":
            return dict(w_down=arrays["w_down"].reshape(N_CHIPS, n_ff, D_MODEL))
        return arrays

    out = {}

    def swap(group, arrays):
        mine = as_shards(group, arrays)
        pending.append((group, mine, _sibling_exchange_async(f"sibling_{group}", list(mine.values()), next_id[0])))
        next_id[0] += 1

    def sum_and_send(anchor):
        tag, mine, theirs = pending.pop()
        sums = [_pair_sum(f"pair_sum_{k}", mine[k], _after(t, *anchor)) for k, t in zip(mine, theirs)]
        contributions.update(zip(mine, _scatter_chips_async(f"scatter_{tag}", sums, next_id[0])))
        next_id[0] += 1
        return sums

    def update(names, anchor):
        for k in names:
            res = _adamw(f"adamw_{k}", own_w[k], own_m[k], own_v[k], _after(contributions[k], *anchor))
            out[k] = [(jnp.transpose(r) if k == "w_in" else r)[None] for r in res]
        return [out[k][0] for k in names]

    def on_grad(event, arrays):
        anchor = list(arrays.values())
        held = []
        if event in ("w_gate_w_up", "w_out", "mid", "last"):
            held += sum_and_send(anchor)
        if event == "mid":
            held += update(("w_down", "w_gate", "w_up"), anchor)
        if event == "last":
            held += update(("w_out",), anchor)
        if event in ("w_down", "w_gate_w_up", "w_out", "w_in"):
            swap(event, arrays)
        return held

    grad_x, _, small = _local_step(
        x[0], loss_target[0], norm1_g, w_in_t, wg, gate_bias, gla_norm_g, attn_norm_g,
        w_out4.reshape(D_MODEL, D_MODEL), norm2_g, w_gate4, w_up4, cw_full, conv_b, w_down4.reshape(D_FF, D_MODEL), final_norm_g,
        on_grad=on_grad)
    update(("w_in",), [grad_x])

    d_gf_up = small["wg"][0:GATE_RANK, 0:GLA_K]
    d_gb_up = small["wg"][GATE_RANK:2 * GATE_RANK, GLA_K:]
    pieces = [small["loss"], small["norm1_g"], d_gf_up, small["gate_bias"][:, :GLA_K], d_gb_up, small["gate_bias"][:, GLA_K:],
              small["gla_norm_g"], small["attn_norm_g"], small["norm2_g"], small["conv_w"], small["conv_b"], small["final_norm_g"]]
    total = _allreduce_rows(_pack_rows(pieces))
    summed = _unpack_rows(total, [p.shape for p in pieces])
    loss = summed[0][0, 0]
    g_small = dict(zip(SMALL_NAMES, summed[1:]))
    g_small["gf_up"] = lax.dynamic_slice_in_dim(g_small["gf_up"], chip * n_gk, n_gk, axis=1)
    g_small["gb_up"] = lax.dynamic_slice_in_dim(g_small["gb_up"], chip * n_gk, n_gk, axis=1)
    g_small["conv_w"] = lax.dynamic_slice_in_dim(g_small["conv_w"], chip * n_ff, n_ff, axis=1)
    packed = [_pack_rows([t[k] for k in SMALL_NAMES]) for t in (w, m, v, g_small)]
    res = _adamw("adamw_small", *packed)
    shapes = [w[k].shape for k in SMALL_NAMES]
    for k, vals in zip(SMALL_NAMES, zip(*[_unpack_rows(r, shapes) for r in res])):
        out[k] = list(vals)

    grads, deltas, new_m, new_v = ([out[k][i] for k in WEIGHT_ORDER] for i in range(4))
    return (loss, grad_x[None], *grads, *deltas, *new_m, *new_v)
```

```python
import functools
import math

import jax
import jax.numpy as jnp
from jax import lax
from jax.experimental import pallas as pl
from jax.experimental.pallas import tpu as pltpu
from jax.experimental.pallas import tpu_sc as plsc

F32 = jnp.float32
BF16 = jnp.bfloat16

D_MODEL = 2048
ATTN_W = 1024
HEAD = 128
N_HEADS = 8
N_SIDE = 64
DILATIONS = (1, 4, 16)
ROPE_THETA = 500000.0
ROPE_DIM = 32
GLA_K = 512
GLA_V = 1024
GLA_HEADS = 4
GLA_DK = 128
GLA_DV = 256
GATE_RANK = 16
GATE_NORM = 16.0
CHUNK = 64
IN_MAIN = 6144
IN_W = 6176
Z_W = IN_W - IN_MAIN
D_FF = 5632
EPS = 1e-6
N_CHIPS = 4
N_DEV = 8
LANES = 128

ADAM_LR = 0.001
ADAM_B1 = 0.9
ADAM_B2 = 0.999
ADAM_EPS = 1e-08
ADAM_WD = 0.01
ADAM_STEP = 10

NEG = -1e30
MESH = pl.DeviceIdType.MESH
ANY = pl.BlockSpec(memory_space=pl.ANY)

NN = ((1,), (0,))
NT = ((1,), (1,))
TN = ((0,), (0,))


def _dot(a, b, dims=NN):
    return lax.dot_general(a, b, (dims, ((), ())), preferred_element_type=F32)


def _sigmoid(x):
    return 0.5 * jnp.tanh(0.5 * x) + 0.5


def _after(x, *deps):
    return lax.optimization_barrier((x,) + deps)[0]


def _matmul(name, pairs, grid, out_shape, out_spec, nk, res=None, into=None, first=None):
    n_in = 2 * len(pairs) + (res is not None)
    dims = [p[4] for p in pairs]

    n_ops = n_in + (into is not None) + 2 * (first is not None)

    def body(*refs):
        ins, o_ref = refs[:n_in], refs[n_ops]

        def partial_sum():
            tot = None
            for p, dn in enumerate(dims):
                a, b = ins[2 * p][...], ins[2 * p + 1][...]
                t = _dot(a.astype(BF16), b.astype(BF16), dn)
                tot = t if tot is None else tot + t
            return tot

        if nk == 1:
            t = partial_sum()
            if res is not None:
                t = t + ins[-1][...]
            o_ref[...] = t.astype(o_ref.dtype)
        else:
            acc_ref = refs[n_ops + 1]
            k = pl.program_id(2)

            @pl.when(k == 0)
            def _():
                if first is not None:
                    start = _dot(refs[n_in][...].astype(BF16), refs[n_in + 1][...].astype(BF16), first[4])
                    acc_ref[...] = start + ins[-1][...] if res is not None else start
                elif res is not None:
                    acc_ref[...] = ins[-1][...]
                else:
                    acc_ref[...] = jnp.zeros_like(acc_ref)

            acc_ref[...] += partial_sum()

            @pl.when(k == nk - 1)
            def _():
                o_ref[...] = acc_ref[...].astype(o_ref.dtype)

    operands, in_specs = [], []
    for a, a_spec, b, b_spec, _ in pairs:
        operands += [a, b]
        in_specs += [a_spec, b_spec]
    if res is not None:
        operands.append(res[0])
        in_specs.append(res[1])
    if first is not None:
        assert nk > 1
        operands += [first[0], first[2]]
        in_specs += [first[1], first[3]]
    acc_shape = tuple(s for s in out_spec.block_shape if s is not None)
    scratch = [pltpu.VMEM(acc_shape, F32)] if nk > 1 else []
    aliases = {}
    if into is not None:
        aliases = {len(operands): 0}
        operands.append(into)
        in_specs.append(ANY)
    return pl.pallas_call(
        body, name=name, grid=grid, in_specs=in_specs, out_specs=out_spec, out_shape=out_shape, scratch_shapes=scratch,
        input_output_aliases=aliases,
    )(*operands)


def _mm_nn(name, a, b, tm, tn, out_dtype, res=None):
    M, K = a.shape
    N = b.shape[1]
    pairs = [(a, pl.BlockSpec((tm, K), lambda j, i: (i, 0)), b, pl.BlockSpec((K, tn), lambda j, i: (0, j)), NN)]
    r = None if res is None else (res, pl.BlockSpec((tm, tn), lambda j, i: (i, j)))
    return _matmul(name, pairs, (N // tn, M // tm), jax.ShapeDtypeStruct((M, N), out_dtype),
                   pl.BlockSpec((tm, tn), lambda j, i: (i, j)), 1, r)


def _mm_nn_sharded(name, a, b4, tm, out_dtype):
    M, K = a.shape
    n = b4.shape[2]
    pairs = [(a, pl.BlockSpec((tm, K), lambda j, i: (i, 0)), b4, pl.BlockSpec((None, K, n), lambda j, i: (j, 0, 0)), NN)]
    return _matmul(name, pairs, (N_CHIPS, M // tm), jax.ShapeDtypeStruct((M, N_CHIPS * n), out_dtype),
                   pl.BlockSpec((tm, n), lambda j, i: (i, j)), 1)


def _mm_nt(name, a, b, tm, tn, out_dtype, res=None, n_out=None):
    M, K = a.shape
    N = b.shape[0] if n_out is None else n_out
    pairs = [(a, pl.BlockSpec((tm, K), lambda j, i: (i, 0)), b, pl.BlockSpec((tn, K), lambda j, i: (j, 0)), NT)]
    r = None if res is None else (res, pl.BlockSpec((tm, tn), lambda j, i: (i, j)))
    return _matmul(name, pairs, (N // tn, M // tm), jax.ShapeDtypeStruct((M, N), out_dtype),
                   pl.BlockSpec((tm, tn), lambda j, i: (i, j)), 1, r)


def _mm_tn(name, a, g, tka, tn, tmm, out_dtype, out3=None, rows_out=None):
    M, Ka = a.shape
    N = g.shape[1]
    pairs = [(a, pl.BlockSpec((tmm, tka), lambda i, j, k: (k, i)), g, pl.BlockSpec((tmm, tn), lambda i, j, k: (k, j)), TN)]
    if out3 is None:
        shape, spec = (Ka if rows_out is None else rows_out, N), pl.BlockSpec((tka, tn), lambda i, j, k: (i, j))
    else:
        shape, spec = (N // out3, Ka, out3), pl.BlockSpec((None, tka, tn), lambda i, j, k: (j, i, 0))
    return _matmul(name, pairs, (Ka // tka, N // tn, M // tmm), jax.ShapeDtypeStruct(shape, out_dtype), spec, M // tmm)


def _rms_fwd(name, x, g, tm=512):
    S, D = x.shape

    def body(x_ref, g_ref, o_ref):
        xv = x_ref[...]
        r = lax.rsqrt(jnp.mean(xv * xv, axis=-1, keepdims=True) + EPS)
        o_ref[...] = (xv * r * g_ref[...]).astype(o_ref.dtype)

    return pl.pallas_call(
        body, name=name, grid=(S // tm,),
        in_specs=[pl.BlockSpec((tm, D), lambda i: (i, 0)), pl.BlockSpec((1, D), lambda i: (0, 0))],
        out_specs=pl.BlockSpec((tm, D), lambda i: (i, 0)), out_shape=jax.ShapeDtypeStruct((S, D), BF16),
    )(x, g)


def _rms_bwd(name, x, g, dn, dres, tm=512):
    S, D = x.shape

    def body(x_ref, g_ref, dn_ref, dres_ref, dx_ref, dxb_ref, dg_ref):
        i = pl.program_id(0)

        @pl.when(i == 0)
        def _():
            dg_ref[...] = jnp.zeros_like(dg_ref)

        xv = x_ref[...]
        r = lax.rsqrt(jnp.mean(xv * xv, axis=-1, keepdims=True) + EPS)
        xhat = xv * r
        dnv = dn_ref[...].astype(F32)
        dg_ref[...] += jnp.sum(dnv * xhat, axis=0, keepdims=True)
        t = dnv * g_ref[...]
        dx = r * (t - xhat * jnp.mean(t * xhat, axis=-1, keepdims=True)) + dres_ref[...]
        dx_ref[...] = dx
        dxb_ref[...] = dx.astype(BF16)

    row = pl.BlockSpec((tm, D), lambda i: (i, 0))
    vec = pl.BlockSpec((1, D), lambda i: (0, 0))
    return pl.pallas_call(
        body, name=name, grid=(S // tm,), in_specs=[row, vec, row, row], out_specs=[row, row, vec],
        out_shape=[jax.ShapeDtypeStruct((S, D), F32), jax.ShapeDtypeStruct((S, D), BF16), jax.ShapeDtypeStruct((1, D), F32)],
    )(x, g, dn, dres)


def _final_loss(h2, g, target, tm=512):
    S, D = h2.shape

    def body(x_ref, g_ref, t_ref, loss_ref, dg_ref, dx_ref, dxb_ref):
        i = pl.program_id(0)

        @pl.when(i == 0)
        def _():
            loss_ref[...] = jnp.zeros_like(loss_ref)
            dg_ref[...] = jnp.zeros_like(dg_ref)

        xv = x_ref[...]
        r = lax.rsqrt(jnp.mean(xv * xv, axis=-1, keepdims=True) + EPS)
        xhat = xv * r
        gv = g_ref[...]
        diff = xhat * gv - t_ref[...]
        per_tok = jnp.mean(diff * diff, axis=-1, keepdims=True)
        loss_ref[...] += 0.5 * jnp.sum(per_tok, axis=0, keepdims=True)
        dy = diff * (1.0 / D)
        dg_ref[...] += jnp.sum(dy * xhat, axis=0, keepdims=True)
        t = dy * gv
        dx = r * (t - xhat * jnp.mean(t * xhat, axis=-1, keepdims=True))
        dx_ref[...] = dx
        dxb_ref[...] = dx.astype(BF16)

    row = pl.BlockSpec((tm, D), lambda i: (i, 0))
    vec = pl.BlockSpec((1, D), lambda i: (0, 0))
    return pl.pallas_call(
        body, name="final_loss", grid=(S // tm,), in_specs=[row, vec, row],
        out_specs=[pl.BlockSpec((1, LANES), lambda i: (0, 0)), vec, row, row],
        out_shape=[jax.ShapeDtypeStruct((1, LANES), F32), jax.ShapeDtypeStruct((1, D), F32),
                   jax.ShapeDtypeStruct((S, D), F32), jax.ShapeDtypeStruct((S, D), BF16)],
    )(h2, g, target)


def _rope_tables(S):
    pos = jnp.arange(S, dtype=F32)
    inv_freq = ROPE_THETA ** (-jnp.arange(0, ROPE_DIM, 2, dtype=F32) / ROPE_DIM)
    ang = pos[:, None] * inv_freq[None, :]
    cos, sin = jnp.cos(ang), jnp.sin(ang)
    half = ROPE_DIM // 2
    rest = HEAD - ROPE_DIM
    z_h, z_r = jnp.zeros((S, half), F32), jnp.zeros((S, rest), F32)
    tab_c = jnp.concatenate([cos, cos, jnp.ones((S, rest), F32)], axis=1)
    tab_up = jnp.concatenate([z_h, sin, z_r], axis=1)
    tab_dn = jnp.concatenate([-sin, z_h, z_r], axis=1)
    return tab_c, tab_up, tab_dn


def _rope_head(t, c, up, dn):
    half = ROPE_DIM // 2
    return t * c + pltpu.roll(t, half, axis=1) * up + pltpu.roll(t, HEAD - half, axis=1) * dn


def _rope_fwd(proj, tabs, tm=512):
    S = proj.shape[0]
    W = 2 * ATTN_W

    def body(p_ref, c_ref, up_ref, dn_ref, o_ref):
        c, up, dn = c_ref[...], up_ref[...], dn_ref[...]
        for h in range(W // HEAD):
            sl = slice(h * HEAD, (h + 1) * HEAD)
            o_ref[:, sl] = _rope_head(p_ref[:, sl].astype(F32), c, up, dn).astype(BF16)

    tab = pl.BlockSpec((tm, HEAD), lambda i: (i, 0))
    return pl.pallas_call(
        body, name="rope_fwd", grid=(S // tm,), in_specs=[pl.BlockSpec((tm, W), lambda i: (i, 0)), tab, tab, tab],
        out_specs=pl.BlockSpec((tm, W), lambda i: (i, 0)), out_shape=jax.ShapeDtypeStruct((S, W), BF16),
    )(proj, *tabs)


def _attn_grad_merge(dqs, dks, dvs, tabs, tm=256):
    S = dqs[0].shape[0]

    def body(*refs):
        q_refs, k_refs, v_refs = refs[0:3], refs[3:6], refs[6:9]
        c, up, dn = refs[9][...], refs[10][...], refs[11][...]
        o_ref = refs[12]
        for h in range(N_HEADS):
            sl = slice(h * HEAD, (h + 1) * HEAD)
            for part, rs in ((0, q_refs), (1, k_refs)):
                t = rs[0][:, sl].astype(F32) + rs[1][:, sl].astype(F32) + rs[2][:, sl].astype(F32)
                osl = slice(part * ATTN_W + h * HEAD, part * ATTN_W + (h + 1) * HEAD)
                o_ref[:, osl] = _rope_head(t, c, -up, -dn).astype(BF16)
        o_ref[:, 2 * ATTN_W:] = (v_refs[0][...].astype(F32) + v_refs[1][...].astype(F32)
                                 + v_refs[2][...].astype(F32)).astype(BF16)

    blk = pl.BlockSpec((tm, ATTN_W), lambda i: (i, 0))
    tab = pl.BlockSpec((tm, HEAD), lambda i: (i, 0))
    return pl.pallas_call(
        body, name="attn_grad_merge", grid=(S // tm,), in_specs=[blk] * 9 + [tab] * 3,
        out_specs=pl.BlockSpec((tm, 3 * ATTN_W), lambda i: (i, 0)), out_shape=jax.ShapeDtypeStruct((S, IN_MAIN), BF16),
    )(*dqs, *dks, *dvs, *tabs)


SUB = 128
Q_COL, K_COL, V_COL = 0, ATTN_W // HEAD, 2 * ATTN_W // HEAD


class _AttnGeo:
    def __init__(self, S, d):
        self.S, self.d, self.L = S, d, S // d
        self.halo = N_SIDE * d
        self.TB = min(2048, S)
        self.W = self.TB + 2 * self.halo
        self.n_sub = self.TB // SUB
        self.grid = (S // self.TB, N_HEADS)
        self.dt = F32 if d > 1 else BF16
        self.su = min(d, 4)
        self.sb = d // self.su
        assert self.TB % (SUB * d) == 0 and self.TB % self.halo == 0

    def specs(self, width, col0, per_head=True):
        ratio = self.TB // self.halo
        last = self.S // self.halo - 1
        col = (lambda h: col0 + h) if per_head else (lambda h: col0)
        cur = pl.BlockSpec((self.TB, width), lambda i, h: (i, col(h)))
        prev = pl.BlockSpec((self.halo, width), lambda i, h: (jnp.maximum(i * ratio - 1, 0), col(h)))
        nxt = pl.BlockSpec((self.halo, width), lambda i, h: (jnp.minimum((i + 1) * ratio, last), col(h)))
        return cur, prev, nxt

    def scratch(self, rows, dtype=None):
        nat = pltpu.VMEM((rows, LANES), self.dt if dtype is None else dtype)
        return [nat] if self.sb == 1 else [nat, pltpu.VMEM((rows, LANES), F32)]

    def bind(self, refs):
        nat = next(refs)
        return (nat, nat) if self.sb == 1 else (nat, next(refs))

    def spread(self, pair):
        nat, streams = pair
        if self.sb > 1:
            n = nat.shape[0] // self.sb
            for a in range(self.sb):
                streams[a * n:(a + 1) * n, :] = nat[pl.ds(a, n, stride=self.sb), :]
        return streams

    def gather(self, pair):
        nat, streams = pair
        if self.sb > 1:
            n = nat.shape[0] // self.sb
            for a in range(self.sb):
                nat[pl.ds(a, n, stride=self.sb), :] = streams[a * n:(a + 1) * n, :]
        return nat

    def rows(self, sub, n, total):
        res, blk = sub % self.d, sub // self.d
        a, b = res % self.sb, res // self.sb
        start = a * (total // self.sb) + b + self.su * SUB * blk
        return pl.ds(start, n, stride=self.su) if self.su > 1 else pl.ds(start, n)

    def mask(self, sub):
        base = pl.program_id(0) * (self.TB // self.d) + SUB * (sub // self.d)
        row = lax.broadcasted_iota(jnp.int32, (SUB, 2 * SUB), 0)
        col = lax.broadcasted_iota(jnp.int32, (SUB, 2 * SUB), 1)
        pos = base - N_SIDE + col
        return (col >= row) & (col <= row + 2 * N_SIDE) & (pos >= 0) & (pos < self.L)

    def fill(self, dst, c_ref):
        dst[...] = c_ref[...].astype(dst.dtype)

    def fill_window(self, dst, p_ref, c_ref, n_ref):
        dst[0:self.halo] = p_ref[...].astype(dst.dtype)
        dst[self.halo:self.halo + self.TB] = c_ref[...].astype(dst.dtype)
        dst[self.halo + self.TB:] = n_ref[...].astype(dst.dtype)


def _lane_of(tile, h):
    lane = lax.broadcasted_iota(jnp.int32, tile.shape, 1)
    return jnp.sum(jnp.where(lane == h, tile, 0.0), axis=1, keepdims=True)


def _attn_fwd(qk, proj, d):
    S = qk.shape[0]
    geo = _AttnGeo(S, d)
    scale = HEAD ** -0.5

    def body(q_ref, kp, kc, kn, vp, vc, vn, o_ref, lse_ref, *scratch):
        h = pl.program_id(1)
        refs = iter(scratch)
        q_p, k_p, v_p, o_p, l_p = (geo.bind(refs) for _ in range(5))
        geo.fill(q_p[0], q_ref)
        geo.fill_window(k_p[0], kp, kc, kn)
        geo.fill_window(v_p[0], vp, vc, vn)
        qs, ks, vs = geo.spread(q_p), geo.spread(k_p), geo.spread(v_p)
        os, ls = o_p[1], l_p[1]
        for sub in range(geo.n_sub):
            rq, rw = geo.rows(sub, SUB, geo.TB), geo.rows(sub, 2 * SUB, geo.W)
            q_r, k_r, v_r = qs[rq, :].astype(BF16), ks[rw, :].astype(BF16), vs[rw, :].astype(BF16)
            s = jnp.where(geo.mask(sub), _dot(q_r, k_r, NT) * scale, NEG)
            m = jnp.max(s, axis=1, keepdims=True)
            p = jnp.exp(s - m)
            l = jnp.sum(p, axis=1, keepdims=True)
            os[rq, :] = _dot(p.astype(BF16), v_r) / l
            ls[rq, :] = jnp.broadcast_to(m + jnp.log(l), (SUB, LANES))
        o_ref[...] = geo.gather(o_p)[...].astype(BF16)

        @pl.when(h == 0)
        def _():
            lse_ref[...] = jnp.zeros_like(lse_ref)

        lane = lax.broadcasted_iota(jnp.int32, (geo.TB, LANES), 1)
        lse_ref[...] = jnp.where(lane == h, geo.gather(l_p)[...], lse_ref[...])

    q_cur, _, _ = geo.specs(HEAD, Q_COL)
    k_specs = geo.specs(HEAD, K_COL)
    v_specs = geo.specs(HEAD, V_COL)
    stat = pl.BlockSpec((geo.TB, LANES), lambda i, h: (i, 0))
    return pl.pallas_call(
        body, name=f"attn_fwd_d{d}", grid=geo.grid,
        in_specs=[q_cur, k_specs[1], k_specs[0], k_specs[2], v_specs[1], v_specs[0], v_specs[2]],
        out_specs=[q_cur, stat],
        out_shape=[jax.ShapeDtypeStruct((S, ATTN_W), BF16), jax.ShapeDtypeStruct((S, LANES), F32)],
        scratch_shapes=(geo.scratch(geo.TB) + geo.scratch(geo.W) + geo.scratch(geo.W) + geo.scratch(geo.TB, F32)
                        + geo.scratch(geo.TB, F32)),
    )(qk, qk, qk, qk, proj, proj, proj)


def _attn_combine(outs, lses, g, tm=256):
    S = outs[0].shape[0]

    def body(o1, o2, o3, l1, l2, l3, g_ref, ao_ref, o_ref, lse_ref):
        lane = lax.broadcasted_iota(jnp.int32, (tm, LANES), 1)
        lse_tile = jnp.zeros((tm, LANES), F32)
        a = [l1[...], l2[...], l3[...]]
        ssq = jnp.zeros((tm, 1), F32)
        for h in range(N_HEADS):
            sl = slice(h * HEAD, (h + 1) * HEAD)
            a1, a2, a3 = (t[:, h:h + 1] for t in a)
            mx = jnp.maximum(jnp.maximum(a1, a2), a3)
            e1, e2, e3 = jnp.exp(a1 - mx), jnp.exp(a2 - mx), jnp.exp(a3 - mx)
            den = e1 + e2 + e3
            oh = (e1 * o1[:, sl].astype(F32) + e2 * o2[:, sl].astype(F32) + e3 * o3[:, sl].astype(F32)) / den
            o_ref[:, sl] = oh
            ssq = ssq + jnp.sum(oh * oh, axis=1, keepdims=True)
            lse_tile = jnp.where(lane == h, mx + jnp.log(den), lse_tile)
        lse_ref[...] = lse_tile
        r = lax.rsqrt(ssq * (1.0 / ATTN_W) + EPS)
        ao_ref[...] = (o_ref[...] * r * g_ref[...]).astype(BF16)

    blk = pl.BlockSpec((tm, ATTN_W), lambda i: (i, 0))
    ls = pl.BlockSpec((tm, LANES), lambda i: (i, 0))
    return pl.pallas_call(
        body, name="attn_combine", grid=(S // tm,),
        in_specs=[blk, blk, blk, ls, ls, ls, pl.BlockSpec((1, ATTN_W), lambda i: (0, 0))], out_specs=[blk, blk, ls],
        out_shape=[jax.ShapeDtypeStruct((S, D_MODEL), BF16), jax.ShapeDtypeStruct((S, ATTN_W), F32),
                   jax.ShapeDtypeStruct((S, LANES), F32)],
    )(*outs, *lses, g)


def _attn_norm_bwd(o, g, dao, tm=256):
    S = o.shape[0]

    def body(o_ref, g_ref, dao_ref, do_ref, dl_ref, dg_ref):
        i = pl.program_id(0)

        @pl.when(i == 0)
        def _():
            dg_ref[...] = jnp.zeros_like(dg_ref)

        ov = o_ref[...]
        r = lax.rsqrt(jnp.mean(ov * ov, axis=-1, keepdims=True) + EPS)
        ohat = ov * r
        dn = dao_ref[...].astype(F32)
        dg_ref[...] += jnp.sum(dn * ohat, axis=0, keepdims=True)
        t = dn * g_ref[...]
        do = r * (t - ohat * jnp.mean(t * ohat, axis=-1, keepdims=True))
        do_ref[...] = do.astype(BF16)
        prod = do * ov
        lane = lax.broadcasted_iota(jnp.int32, (tm, LANES), 1)
        tile = jnp.zeros((tm, LANES), F32)
        for h in range(N_HEADS):
            tile = jnp.where(lane == h, jnp.sum(prod[:, h * HEAD:(h + 1) * HEAD], axis=1, keepdims=True), tile)
        dl_ref[...] = tile

    blk = pl.BlockSpec((tm, ATTN_W), lambda i: (i, 0))
    vec = pl.BlockSpec((1, ATTN_W), lambda i: (0, 0))
    return pl.pallas_call(
        body, name="attn_norm_bwd", grid=(S // tm,),
        in_specs=[blk, vec, pl.BlockSpec((tm, ATTN_W), lambda i: (i, 0))],
        out_specs=[blk, pl.BlockSpec((tm, LANES), lambda i: (i, 0)), vec],
        out_shape=[jax.ShapeDtypeStruct((S, ATTN_W), BF16), jax.ShapeDtypeStruct((S, LANES), F32),
                   jax.ShapeDtypeStruct((1, ATTN_W), F32)],
    )(o, g, dao)


def _attn_bwd_dq(qk, proj, do, lse, delta, d):
    S = qk.shape[0]
    geo = _AttnGeo(S, d)
    scale = HEAD ** -0.5

    def body(q_ref, kp, kc, kn, vp, vc, vn, do_ref, lse_ref, dl_ref, dq_ref, *scratch):
        h = pl.program_id(1)
        refs = iter(scratch)
        q_p, k_p, v_p, do_p, lse_p, dl_p, dq_p = (geo.bind(refs) for _ in range(7))
        geo.fill(q_p[0], q_ref)
        geo.fill(do_p[0], do_ref)
        geo.fill(lse_p[0], lse_ref)
        geo.fill(dl_p[0], dl_ref)
        geo.fill_window(k_p[0], kp, kc, kn)
        geo.fill_window(v_p[0], vp, vc, vn)
        qs, ks, vs, dos = geo.spread(q_p), geo.spread(k_p), geo.spread(v_p), geo.spread(do_p)
        lses, dls = geo.spread(lse_p), geo.spread(dl_p)
        dqs = dq_p[1]
        for sub in range(geo.n_sub):
            rq, rw = geo.rows(sub, SUB, geo.TB), geo.rows(sub, 2 * SUB, geo.W)
            q_r, k_r, v_r = qs[rq, :].astype(BF16), ks[rw, :].astype(BF16), vs[rw, :].astype(BF16)
            lse_c, dl_c = _lane_of(lses[rq, :], h), _lane_of(dls[rq, :], h)
            s = _dot(q_r, k_r, NT) * scale
            p = jnp.where(geo.mask(sub), jnp.exp(s - lse_c), 0.0)
            dp = _dot(dos[rq, :].astype(BF16), v_r, NT)
            ds = (p * (dp - dl_c) * scale).astype(BF16)
            dqs[rq, :] = _dot(ds, k_r)
        dq_ref[...] = geo.gather(dq_p)[...].astype(BF16)

    cur, _, _ = geo.specs(HEAD, 0)
    k_specs = geo.specs(HEAD, K_COL)
    v_specs = geo.specs(HEAD, V_COL)
    stat = pl.BlockSpec((geo.TB, LANES), lambda i, h: (i, 0))
    return pl.pallas_call(
        body, name=f"attn_bwd_dq_d{d}", grid=geo.grid,
        in_specs=[cur, k_specs[1], k_specs[0], k_specs[2], v_specs[1], v_specs[0], v_specs[2], cur, stat, stat],
        out_specs=cur, out_shape=jax.ShapeDtypeStruct((S, ATTN_W), BF16),
        scratch_shapes=(geo.scratch(geo.TB) + geo.scratch(geo.W) + geo.scratch(geo.W) + geo.scratch(geo.TB)
                        + geo.scratch(geo.TB, F32) + geo.scratch(geo.TB, F32) + geo.scratch(geo.TB, F32)),
    )(qk, qk, qk, qk, proj, proj, proj, do, lse, delta)


def _attn_bwd_dkv(qk, proj, do, lse, delta, d):
    S = qk.shape[0]
    geo = _AttnGeo(S, d)
    scale = HEAD ** -0.5

    def body(k_ref, v_ref, qp, qc, qn, dop, doc, don, lp, lc, ln, dlp, dlc, dln, dk_ref, dv_ref, *scratch):
        h = pl.program_id(1)
        refs = iter(scratch)
        k_p, v_p, q_p, do_p, lw_p, dlw_p, dk_p, dv_p = (geo.bind(refs) for _ in range(8))
        geo.fill(k_p[0], k_ref)
        geo.fill(v_p[0], v_ref)
        geo.fill_window(q_p[0], qp, qc, qn)
        geo.fill_window(do_p[0], dop, doc, don)
        geo.fill_window(lw_p[0], lp, lc, ln)
        geo.fill_window(dlw_p[0], dlp, dlc, dln)
        ks, vs, qs, dos = geo.spread(k_p), geo.spread(v_p), geo.spread(q_p), geo.spread(do_p)
        lws, dlws = geo.spread(lw_p), geo.spread(dlw_p)
        dks, dvs = dk_p[1], dv_p[1]
        head = lax.broadcasted_iota(jnp.int32, (LANES, 2 * SUB), 0)
        for sub in range(geo.n_sub):
            rq, rw = geo.rows(sub, SUB, geo.TB), geo.rows(sub, 2 * SUB, geo.W)
            k_r, v_r = ks[rq, :].astype(BF16), vs[rq, :].astype(BF16)
            q_w, do_w = qs[rw, :].astype(BF16), dos[rw, :].astype(BF16)
            lse_row = jnp.sum(jnp.where(head == h, lws[rw, :].T, 0.0), axis=0, keepdims=True)
            dl_row = jnp.sum(jnp.where(head == h, dlws[rw, :].T, 0.0), axis=0, keepdims=True)
            st = _dot(k_r, q_w, NT) * scale
            pt = jnp.where(geo.mask(sub), jnp.exp(st - lse_row), 0.0)
            dvs[rq, :] = _dot(pt.astype(BF16), do_w)
            dpt = _dot(v_r, do_w, NT)
            dst = (pt * (dpt - dl_row) * scale).astype(BF16)
            dks[rq, :] = _dot(dst, q_w)
        dk_ref[...] = geo.gather(dk_p)[...].astype(BF16)
        dv_ref[...] = geo.gather(dv_p)[...].astype(BF16)

    q_specs = geo.specs(HEAD, Q_COL)
    k_cur, _, _ = geo.specs(HEAD, K_COL)
    v_cur, _, _ = geo.specs(HEAD, V_COL)
    do_specs = geo.specs(HEAD, 0)
    st_specs = geo.specs(LANES, 0, per_head=False)
    cur = do_specs[0]
    return pl.pallas_call(
        body, name=f"attn_bwd_dkv_d{d}", grid=geo.grid,
        in_specs=[k_cur, v_cur, q_specs[1], q_specs[0], q_specs[2], do_specs[1], do_specs[0], do_specs[2],
                  st_specs[1], st_specs[0], st_specs[2], st_specs[1], st_specs[0], st_specs[2]],
        out_specs=[cur, cur],
        out_shape=[jax.ShapeDtypeStruct((S, ATTN_W), BF16), jax.ShapeDtypeStruct((S, ATTN_W), BF16)],
        scratch_shapes=(geo.scratch(geo.TB) + geo.scratch(geo.TB) + geo.scratch(geo.W) + geo.scratch(geo.W)
                        + geo.scratch(geo.W, F32) + geo.scratch(geo.W, F32) + geo.scratch(geo.TB, F32)
                        + geo.scratch(geo.TB, F32)),
    )(qk, proj, qk, qk, qk, do, do, do, lse, lse, lse, delta, delta, delta)


def _cumsum_rows(x, reverse):
    n = x.shape[0]
    row = lax.broadcasted_iota(jnp.int32, x.shape, 0)
    s = 1
    while s < n:
        if reverse:
            x = x + jnp.where(row < n - s, pltpu.roll(x, n - s, axis=0), 0.0)
        else:
            x = x + jnp.where(row >= s, pltpu.roll(x, s, axis=0), 0.0)
        s *= 2
    return x


GLA_GROUP = 4


def _gla_rows(cc):
    return slice(cc * CHUNK, (cc + 1) * CHUNK)


def _gla_chunk_terms(q_ref, k_ref, v_ref, g_ref, h, reverse, rows):
    ksl = slice(h * GLA_DK, (h + 1) * GLA_DK)
    q = q_ref[rows, ksl].astype(F32) * (GLA_DK ** -0.5)
    k = k_ref[rows, ksl].astype(F32)
    v = v_ref[rows, h * GLA_DV:(h + 1) * GLA_DV]
    b = _cumsum_rows(g_ref[rows, ksl], reverse)
    r_ref = CHUNK // 2 if reverse else CHUNK // 2 - 1
    r_last = 0 if reverse else CHUNK - 1
    b_ref, b_last = b[r_ref:r_ref + 1, :], b[r_last:r_last + 1, :]
    ii = lax.broadcasted_iota(jnp.int32, (CHUNK, CHUNK), 0)
    jj = lax.broadcasted_iota(jnp.int32, (CHUNK, CHUNK), 1)
    causal = (jj >= ii) if reverse else (jj <= ii)
    e_q, e_k = jnp.exp(b - b_ref), jnp.exp(b_ref - b)
    e_in, e_st = jnp.exp(b), jnp.exp(b_last - b)
    return dict(q=q, k=k, v=v, b=b, causal=causal, e_q=e_q, e_k=e_k, e_in=e_in, e_st=e_st, dec=jnp.exp(b_last),
                qe=q * e_q, ke=k * e_k, q_in=q * e_in, k_st=k * e_st, r_ref=r_ref, r_last=r_last)


def _gla_specs(order):
    rows = GLA_GROUP * CHUNK
    q = pl.BlockSpec((rows, GLA_K), lambda c: (order(c), 3 * ATTN_W // GLA_K))
    k = pl.BlockSpec((rows, GLA_K), lambda c: (order(c), 3 * ATTN_W // GLA_K + 1))
    v = pl.BlockSpec((rows, GLA_V), lambda c: (order(c), (3 * ATTN_W + 2 * GLA_K) // GLA_V))
    return q, k, v


def _gla_fwd(proj, gates, reverse, o_prev=None):
    S = proj.shape[0]
    n = S // CHUNK
    nb = n // GLA_GROUP
    rows = GLA_GROUP * CHUNK
    order = (lambda c: nb - 1 - c) if reverse else (lambda c: c)
    seq = list(range(GLA_GROUP))[::-1] if reverse else list(range(GLA_GROUP))
    gcol = 1 if reverse else 0

    def body(*refs):
        if o_prev is None:
            q_ref, k_ref, v_ref, g_ref, o_ref, st_ref, state = refs
        else:
            q_ref, k_ref, v_ref, g_ref, op_ref, o_ref, st_ref, state = refs
        c = pl.program_id(0)

        @pl.when(c == 0)
        def _():
            state[...] = jnp.zeros_like(state)

        for h in range(GLA_HEADS):
            vsl = slice(h * GLA_DV, (h + 1) * GLA_DV)
            st = state[h]
            for cc in seq:
                rs = _gla_rows(cc)
                t = _gla_chunk_terms(q_ref, k_ref, v_ref, g_ref, h, reverse, rs)
                a = jnp.where(t["causal"], _dot(t["qe"].astype(BF16), t["ke"].astype(BF16), NT), 0.0)
                o = _dot(a.astype(BF16), t["v"])
                st_b = st.astype(BF16)
                st_ref[cc, h] = st_b
                o = o + _dot(t["q_in"].astype(BF16), st_b, NT)
                st = st * t["dec"] + _dot(t["v"], t["k_st"].astype(BF16), TN)
                if o_prev is not None:
                    o = o + op_ref[rs, vsl]
                o_ref[rs, vsl] = o
            state[h] = st

    q_spec, k_spec, v_spec = _gla_specs(order)
    o_spec = pl.BlockSpec((rows, GLA_V), lambda c: (order(c), 0))
    in_specs = [q_spec, k_spec, v_spec, pl.BlockSpec((rows, GLA_K), lambda c: (order(c), gcol))]
    operands = [proj, proj, proj, gates]
    if o_prev is not None:
        in_specs.append(o_spec)
        operands.append(o_prev)
    return pl.pallas_call(
        body, name="gla_fwd_rev" if reverse else "gla_fwd", grid=(nb,), in_specs=in_specs,
        out_specs=[o_spec, pl.BlockSpec((GLA_GROUP, GLA_HEADS, GLA_DV, GLA_DK), lambda c: (order(c), 0, 0, 0))],
        out_shape=[jax.ShapeDtypeStruct((S, GLA_V), F32), jax.ShapeDtypeStruct((n, GLA_HEADS, GLA_DV, GLA_DK), BF16)],
        scratch_shapes=[pltpu.VMEM((GLA_HEADS, GLA_DV, GLA_DK), F32)],
    )(*operands)


def _gla_bwd(proj, gates, states, do, reverse, prev=None):
    S = proj.shape[0]
    n = S // CHUNK
    nb = n // GLA_GROUP
    rows = GLA_GROUP * CHUNK
    order = (lambda c: c) if reverse else (lambda c: nb - 1 - c)
    seq = list(range(GLA_GROUP)) if reverse else list(range(GLA_GROUP))[::-1]
    gcol = 1 if reverse else 0
    out_dt = F32 if prev is None else BF16

    def body(*refs):
        if prev is None:
            q_ref, k_ref, v_ref, g_ref, st_ref, do_ref, dq_ref, dk_ref, dv_ref, dg_ref, dstate = refs
        else:
            q_ref, k_ref, v_ref, g_ref, st_ref, do_ref, pq, pk, pv, dq_ref, dk_ref, dv_ref, dg_ref, dstate = refs
        c = pl.program_id(0)

        @pl.when(c == 0)
        def _():
            dstate[...] = jnp.zeros_like(dstate)

        row = lax.broadcasted_iota(jnp.int32, (CHUNK, GLA_DK), 0)
        for h in range(GLA_HEADS):
            ksl = slice(h * GLA_DK, (h + 1) * GLA_DK)
            vsl = slice(h * GLA_DV, (h + 1) * GLA_DV)
            dst = dstate[h]
            for cc in seq:
                rs = _gla_rows(cc)
                t = _gla_chunk_terms(q_ref, k_ref, v_ref, g_ref, h, reverse, rs)
                v = t["v"]
                dob = do_ref[rs, vsl].astype(BF16)
                st_b = st_ref[cc, h]
                dst_b = dst.astype(BF16)
                qe_b, ke_b = t["qe"].astype(BF16), t["ke"].astype(BF16)
                q_in_b, k_st_b = t["q_in"].astype(BF16), t["k_st"].astype(BF16)
                a = jnp.where(t["causal"], _dot(qe_b, ke_b, NT), 0.0)
                da = jnp.where(t["causal"], _dot(dob, v, NT), 0.0).astype(BF16)
                dv = _dot(a.astype(BF16), dob, TN) + _dot(k_st_b, dst_b, NT)
                dqe = _dot(da, ke_b)
                dke = _dot(da, qe_b, TN)
                dq_in = _dot(dob, st_b)
                dk_st = _dot(v, dst_b)
                ddec = jnp.sum(dst * st_b.astype(F32), axis=0, keepdims=True)
                dst = _dot(dob, q_in_b, TN) + dst * t["dec"]
                dq = (dqe * t["e_q"] + dq_in * t["e_in"]) * (GLA_DK ** -0.5)
                dk = dke * t["e_k"] + dk_st * t["e_st"]
                w_q, w_k = dqe * t["qe"], dke * t["ke"]
                w_st = dk_st * t["k_st"]
                db = w_q - w_k + dq_in * t["q_in"] - w_st
                db_ref = jnp.sum(w_k - w_q, axis=0, keepdims=True)
                db_last = jnp.sum(w_st, axis=0, keepdims=True) + ddec * t["dec"]
                db = db + jnp.where(row == t["r_ref"], db_ref, 0.0) + jnp.where(row == t["r_last"], db_last, 0.0)
                dg_ref[rs, ksl] = _cumsum_rows(db, not reverse)
                if prev is not None:
                    dq, dk, dv = dq + pq[rs, ksl], dk + pk[rs, ksl], dv + pv[rs, vsl]
                dq_ref[rs, ksl] = dq.astype(out_dt)
                dk_ref[rs, ksl] = dk.astype(out_dt)
                dv_ref[rs, vsl] = dv.astype(out_dt)
            dstate[h] = dst

    q_spec, k_spec, v_spec = _gla_specs(order)
    kk = pl.BlockSpec((rows, GLA_K), lambda c: (order(c), 0))
    vv = pl.BlockSpec((rows, GLA_V), lambda c: (order(c), 0))
    in_specs = [q_spec, k_spec, v_spec, pl.BlockSpec((rows, GLA_K), lambda c: (order(c), gcol)),
                pl.BlockSpec((GLA_GROUP, GLA_HEADS, GLA_DV, GLA_DK), lambda c: (order(c), 0, 0, 0)), vv]
    operands = [proj, proj, proj, gates, states, do]
    if prev is not None:
        in_specs += [kk, kk, vv]
        operands += list(prev)
    return pl.pallas_call(
        body, name="gla_bwd_rev" if reverse else "gla_bwd", grid=(nb,), in_specs=in_specs, out_specs=[kk, kk, vv, kk],
        out_shape=[jax.ShapeDtypeStruct((S, GLA_K), out_dt), jax.ShapeDtypeStruct((S, GLA_K), out_dt),
                   jax.ShapeDtypeStruct((S, GLA_V), out_dt), jax.ShapeDtypeStruct((S, GLA_K), F32)],
        scratch_shapes=[pltpu.VMEM((GLA_HEADS, GLA_DV, GLA_DK), F32)],
    )(*operands)


def _gates_fwd(z, wg, bias, tm=512):
    S = z.shape[0]
    W = 2 * GLA_K

    def body(z_ref, w_ref, b_ref, o_ref):
        zg = _dot(z_ref[...], w_ref[...]) + b_ref[...]
        o_ref[...] = (jnp.minimum(zg, 0.0) - jnp.log(1.0 + jnp.exp(-jnp.abs(zg)))) * (1.0 / GATE_NORM)

    return pl.pallas_call(
        body, name="gates_fwd", grid=(S // tm,),
        in_specs=[pl.BlockSpec((tm, Z_W), lambda i: (i, 0)), pl.BlockSpec((Z_W, W), lambda i: (0, 0)),
                  pl.BlockSpec((1, W), lambda i: (0, 0))],
        out_specs=pl.BlockSpec((tm, W), lambda i: (i, 0)), out_shape=jax.ShapeDtypeStruct((S, W), F32),
    )(z, wg, bias)


def _gates_bwd(z, wg, bias, dg_f, dg_b, tm=512):
    S = z.shape[0]
    W = 2 * GLA_K

    def body(z_ref, w_ref, b_ref, dgf_ref, dgb_ref, dz_ref, dw_ref, db_ref):
        i = pl.program_id(0)

        @pl.when(i == 0)
        def _():
            dw_ref[...] = jnp.zeros_like(dw_ref)
            db_ref[...] = jnp.zeros_like(db_ref)

        zv = z_ref[...]
        zg = _dot(zv, w_ref[...]) + b_ref[...]
        dg = jnp.concatenate([dgf_ref[...], dgb_ref[...]], axis=1)
        dzg = dg * (1.0 / GATE_NORM) * _sigmoid(-zg)
        db_ref[...] += jnp.sum(dzg, axis=0, keepdims=True)
        dzg_b = dzg.astype(BF16)
        dw_ref[...] += _dot(zv, dzg_b, TN)
        dz_ref[...] = _dot(dzg_b, w_ref[...], NT).astype(BF16)

    half = pl.BlockSpec((tm, GLA_K), lambda i: (i, 0))
    return pl.pallas_call(
        body, name="gates_bwd", grid=(S // tm,),
        in_specs=[pl.BlockSpec((tm, Z_W), lambda i: (i, 0)), pl.BlockSpec((Z_W, W), lambda i: (0, 0)),
                  pl.BlockSpec((1, W), lambda i: (0, 0)), half, half],
        out_specs=[pl.BlockSpec((tm, Z_W), lambda i: (i, 0)), pl.BlockSpec((Z_W, W), lambda i: (0, 0)),
                   pl.BlockSpec((1, W), lambda i: (0, 0))],
        out_shape=[jax.ShapeDtypeStruct((S, Z_W), BF16), jax.ShapeDtypeStruct((Z_W, W), F32),
                   jax.ShapeDtypeStruct((1, W), F32)],
    )(z, wg, bias, dg_f, dg_b)


def _gla_out_fwd(o, proj, g, cat, tm=512):
    S = o.shape[0]

    def body(o_ref, gr_ref, g_ref, cat_ref, out_ref):
        gn = g_ref[...]
        for h in range(GLA_HEADS):
            sl = slice(h * GLA_DV, (h + 1) * GLA_DV)
            ov = o_ref[:, sl]
            r = lax.rsqrt(jnp.mean(ov * ov, axis=-1, keepdims=True) + EPS)
            gr = gr_ref[:, sl].astype(F32)
            out_ref[:, sl] = (ov * r * gn * (gr * _sigmoid(gr))).astype(BF16)

    blk = pl.BlockSpec((tm, GLA_V), lambda i: (i, 0))
    return pl.pallas_call(
        body, name="gla_out_fwd", grid=(S // tm,),
        in_specs=[blk, pl.BlockSpec((tm, GLA_V), lambda i: (i, (3 * ATTN_W + 2 * GLA_K + GLA_V) // GLA_V)),
                  pl.BlockSpec((1, GLA_DV), lambda i: (0, 0)), ANY],
        out_specs=pl.BlockSpec((tm, GLA_V), lambda i: (i, 1)), out_shape=jax.ShapeDtypeStruct((S, D_MODEL), BF16),
        input_output_aliases={3: 0},
    )(o, proj, g, cat)


def _gla_out_bwd(o, proj, g, dcat, dproj, tm=512):
    S = o.shape[0]

    def body(o_ref, gr_ref, g_ref, dgo_ref, dproj_ref, do_ref, dgr_ref, dg_ref):
        i = pl.program_id(0)

        @pl.when(i == 0)
        def _():
            dg_ref[...] = jnp.zeros_like(dg_ref)

        gn = g_ref[...]
        dg_acc = jnp.zeros((1, GLA_DV), F32)
        for h in range(GLA_HEADS):
            sl = slice(h * GLA_DV, (h + 1) * GLA_DV)
            ov = o_ref[:, sl]
            r = lax.rsqrt(jnp.mean(ov * ov, axis=-1, keepdims=True) + EPS)
            yhat = ov * r
            gr = gr_ref[:, sl].astype(F32)
            sg = _sigmoid(gr)
            dgo = dgo_ref[:, sl].astype(F32)
            dgr_ref[:, sl] = (dgo * (yhat * gn) * (sg * (1.0 + gr * (1.0 - sg)))).astype(BF16)
            dy = dgo * (gr * sg)
            dg_acc = dg_acc + jnp.sum(dy * yhat, axis=0, keepdims=True)
            t = dy * gn
            do_ref[:, sl] = r * (t - yhat * jnp.mean(t * yhat, axis=-1, keepdims=True))
        dg_ref[...] += dg_acc

    blk = pl.BlockSpec((tm, GLA_V), lambda i: (i, 0))
    vec = pl.BlockSpec((1, GLA_DV), lambda i: (0, 0))
    return pl.pallas_call(
        body, name="gla_out_bwd", grid=(S // tm,),
        in_specs=[blk, pl.BlockSpec((tm, GLA_V), lambda i: (i, (3 * ATTN_W + 2 * GLA_K + GLA_V) // GLA_V)), vec,
                  pl.BlockSpec((tm, GLA_V), lambda i: (i, 1)), ANY],
        out_specs=[blk, pl.BlockSpec((tm, GLA_V), lambda i: (i, (3 * ATTN_W + 2 * GLA_K + GLA_V) // GLA_V)), vec],
        out_shape=[jax.ShapeDtypeStruct((S, GLA_V), F32), jax.ShapeDtypeStruct((S, IN_MAIN), BF16),
                   jax.ShapeDtypeStruct((1, GLA_DV), F32)],
        input_output_aliases={4: 1},
    )(o, proj, g, dcat, dproj)


HALO = 16


def _halo_specs(tm, tn, S):
    cur = pl.BlockSpec((tm, tn), lambda j, i: (i, j))
    prev = pl.BlockSpec((HALO, tn), lambda j, i: (jnp.maximum(i * (tm // HALO) - 1, 0), j))
    nxt = pl.BlockSpec((HALO, tn), lambda j, i: (jnp.minimum((i + 1) * (tm // HALO), S // HALO - 1), j))
    return cur, prev, nxt


def _shifted(c_ref, p_ref, n_ref, n_blocks):
    i = pl.program_id(1)
    x = c_ref[...].astype(F32)
    tm = x.shape[0]
    row = lax.broadcasted_iota(jnp.int32, x.shape, 0)
    before = p_ref[HALO - 1:HALO, :].astype(F32) * (i > 0).astype(F32)
    after = n_ref[0:1, :].astype(F32) * (i < n_blocks - 1).astype(F32)
    x_m1 = jnp.where(row == 0, before, pltpu.roll(x, 1, axis=0))
    x_p1 = jnp.where(row == tm - 1, after, pltpu.roll(x, tm - 1, axis=0))
    return x, x_m1, x_p1


def _glu_fwd(gp, up, cw, cb, tm=512, tn=1408):
    S = gp.shape[0]
    nb = S // tm

    def body(c_ref, p_ref, n_ref, up_ref, w_ref, b_ref, o_ref):
        x, x_m1, x_p1 = _shifted(c_ref, p_ref, n_ref, nb)
        w = w_ref[...]
        gate = w[0:1, :] * x_m1 + w[1:2, :] * x + w[2:3, :] * x_p1 + b_ref[...]
        o_ref[...] = (gate * _sigmoid(gate) * up_ref[...].astype(F32)).astype(BF16)

    cur, prev, nxt = _halo_specs(tm, tn, S)
    return pl.pallas_call(
        body, name="glu_fwd", grid=(D_FF // tn, nb),
        in_specs=[cur, prev, nxt, cur, pl.BlockSpec((3, tn), lambda j, i: (0, j)), pl.BlockSpec((1, tn), lambda j, i: (0, j))],
        out_specs=cur, out_shape=jax.ShapeDtypeStruct((S, D_FF), BF16),
    )(gp, gp, gp, up, cw, cb)


def _glu_bwd(gp, up, dact, cw, cb, tm=512, tn=1408):
    S = gp.shape[0]
    nb = S // tm

    def body(c_ref, p_ref, n_ref, up_ref, upp_ref, upn_ref, da_ref, dap_ref, dan_ref, w_ref, b_ref,
             dup_ref, dgp_ref, dw_ref, db_ref):
        i = pl.program_id(1)

        @pl.when(i == 0)
        def _():
            dw_ref[...] = jnp.zeros_like(dw_ref)
            db_ref[...] = jnp.zeros_like(db_ref)

        x, x_m1, x_p1 = _shifted(c_ref, p_ref, n_ref, nb)
        w = w_ref[...]
        w0, w1, w2, b = w[0:1, :], w[1:2, :], w[2:3, :], b_ref[...]

        def d_gate(gate, da, upv):
            sg = _sigmoid(gate)
            return sg, da * upv * (sg * (1.0 + gate * (1.0 - sg)))

        gate = w0 * x_m1 + w1 * x + w2 * x_p1 + b
        da = da_ref[...].astype(F32)
        sg, dgate = d_gate(gate, da, up_ref[...].astype(F32))
        dup_ref[...] = (da * (gate * sg)).astype(BF16)
        db_ref[...] += jnp.sum(dgate, axis=0, keepdims=True)
        dw_ref[...] += jnp.concatenate(
            [jnp.sum(dgate * x_m1, axis=0, keepdims=True), jnp.sum(dgate * x, axis=0, keepdims=True),
             jnp.sum(dgate * x_p1, axis=0, keepdims=True)], axis=0)

        pv, nv = p_ref[...].astype(F32), n_ref[...].astype(F32)
        gate_before = w0 * pv[HALO - 2:HALO - 1, :] + w1 * pv[HALO - 1:HALO, :] + w2 * x[0:1, :] + b
        _, dgate_before = d_gate(gate_before, dap_ref[...].astype(F32)[HALO - 1:HALO, :], upp_ref[...].astype(F32)[HALO - 1:HALO, :])
        gate_after = w0 * x[tm - 1:tm, :] + w1 * nv[0:1, :] + w2 * nv[1:2, :] + b
        _, dgate_after = d_gate(gate_after, dan_ref[...].astype(F32)[0:1, :], upn_ref[...].astype(F32)[0:1, :])
        dgate_before = dgate_before * (i > 0).astype(F32)
        dgate_after = dgate_after * (i < nb - 1).astype(F32)
        row = lax.broadcasted_iota(jnp.int32, dgate.shape, 0)
        dg_m1 = jnp.where(row == 0, dgate_before, pltpu.roll(dgate, 1, axis=0))
        dg_p1 = jnp.where(row == tm - 1, dgate_after, pltpu.roll(dgate, tm - 1, axis=0))
        dgp_ref[...] = (w0 * dg_p1 + w1 * dgate + w2 * dg_m1).astype(BF16)

    cur, prev, nxt = _halo_specs(tm, tn, S)
    w_spec = pl.BlockSpec((3, tn), lambda j, i: (0, j))
    b_spec = pl.BlockSpec((1, tn), lambda j, i: (0, j))
    return pl.pallas_call(
        body, name="glu_bwd", grid=(D_FF // tn, nb),
        in_specs=[cur, prev, nxt, cur, prev, nxt, cur, prev, nxt, w_spec, b_spec],
        out_specs=[cur, cur, w_spec, b_spec],
        out_shape=[jax.ShapeDtypeStruct((S, D_FF), BF16), jax.ShapeDtypeStruct((S, D_FF), BF16),
                   jax.ShapeDtypeStruct((3, D_FF), F32), jax.ShapeDtypeStruct((1, D_FF), F32)],
    )(gp, gp, gp, up, up, up, dact, dact, dact, cw, cb)


def _local_step(x, target, norm1_g, w_in_t, wg, gate_bias, gla_norm_g, attn_norm_g, w_out, norm2_g,
                w_gate4, w_up4, conv_w, conv_b, w_down, final_norm_g, on_grad=lambda event, arrays: ()):
    S = x.shape[0]
    tabs = _rope_tables(S)

    n1 = _rms_fwd("rms1_fwd", x, norm1_g)
    z_block = IN_MAIN // Z_W
    proj = _mm_nt("in_proj", n1, w_in_t, 1024, 1536, BF16, n_out=IN_MAIN)
    z = _matmul(
        "in_proj_z",
        [(n1, pl.BlockSpec((1024, D_MODEL), lambda i: (i, 0)), w_in_t, pl.BlockSpec((Z_W, D_MODEL), lambda i: (z_block, 0)), NT)],
        (S // 1024,), jax.ShapeDtypeStruct((S, Z_W), BF16), pl.BlockSpec((1024, Z_W), lambda i: (i, 0)), 1)
    qk = _rope_fwd(proj, tabs)
    branch = [_attn_fwd(qk, proj, d) for d in DILATIONS]
    ao, o_attn, lse = _attn_combine([b[0] for b in branch], [b[1] for b in branch], attn_norm_g)
    gates = _gates_fwd(z, wg, gate_bias)
    o_f, st_f = _gla_fwd(proj, gates, False)
    o_gla, st_b = _gla_fwd(proj, gates, True, o_prev=o_f)
    cat = _gla_out_fwd(o_gla, proj, gla_norm_g, ao)
    h1 = _mm_nn("out_proj", cat, w_out, 1024, 1024, F32, res=x)
    n2 = _rms_fwd("rms2_fwd", h1, norm2_g)
    gp = _mm_nn_sharded("ffn_gate", n2, w_gate4, 1024, BF16)
    up = _mm_nn_sharded("ffn_up", n2, w_up4, 1024, BF16)
    act = _glu_fwd(gp, up, conv_w, conv_b)
    tk = D_FF // N_CHIPS
    h2 = _matmul(
        "ffn_down",
        [(act, pl.BlockSpec((1024, tk), lambda i, j, k: (i, k)), w_down, pl.BlockSpec((tk, 1024), lambda i, j, k: (k, j)), NN)],
        (S // 1024, D_MODEL // 1024, N_CHIPS), jax.ShapeDtypeStruct((S, D_MODEL), F32),
        pl.BlockSpec((1024, 1024), lambda i, j, k: (i, j)), N_CHIPS,
        res=(h1, pl.BlockSpec((1024, 1024), lambda i, j, k: (i, j))))
    loss_row, d_final_g, dh2, dh2_b = _final_loss(h2, final_norm_g.reshape(1, D_MODEL), target)

    dact = _mm_nt("ffn_down_bwd", dh2_b, w_down, 1024, tk, BF16)
    dup, dgp, d_conv_w, d_conv_b = _glu_bwd(gp, up, dact, conv_w, conv_b)
    d_w_down = _mm_tn("ffn_down_wgrad", act, dh2_b, tk, D_MODEL, 1024, BF16)
    on_grad("w_down", dict(w_down=d_w_down))
    dgp = _after(dgp, d_w_down)
    d_w_gate4 = _mm_tn("ffn_gate_wgrad", n2, dgp, D_MODEL, tk, 1024, BF16, out3=tk)
    dup = _after(dup, d_w_gate4)
    d_w_up4 = _mm_tn("ffn_up_wgrad", n2, dup, D_MODEL, tk, 1024, BF16, out3=tk)
    held = on_grad("w_gate_w_up", dict(w_gate=d_w_gate4, w_up=d_w_up4))
    dgp = _after(dgp, d_w_up4, *held)
    dn2 = _matmul(
        "ffn_in_bwd",
        [(dgp, pl.BlockSpec((1024, tk), lambda i, j, k: (i, k)), w_gate4, pl.BlockSpec((None, 1024, tk), lambda i, j, k: (k, j, 0)), NT),
         (dup, pl.BlockSpec((1024, tk), lambda i, j, k: (i, k)), w_up4, pl.BlockSpec((None, 1024, tk), lambda i, j, k: (k, j, 0)), NT)],
        (S // 1024, D_MODEL // 1024, N_CHIPS), jax.ShapeDtypeStruct((S, D_MODEL), F32),
        pl.BlockSpec((1024, 1024), lambda i, j, k: (i, j)), N_CHIPS)
    dh1, dh1_b, d_norm2_g = _rms_bwd("rms2_bwd", h1, norm2_g, dn2, dh2)

    d_w_out = _mm_tn("out_proj_wgrad", cat, dh1_b, D_MODEL, 1024, 1024, BF16)
    held = on_grad("w_out", dict(w_out=d_w_out))
    dcat = _mm_nt("out_proj_bwd", _after(dh1_b, d_w_out, *held), w_out, 1024, 1024, BF16)
    do_attn, delta, d_attn_norm_g = _attn_norm_bwd(o_attn, attn_norm_g, dcat)
    dqs, dks, dvs = [], [], []
    for d in DILATIONS:
        dqs.append(_attn_bwd_dq(qk, proj, do_attn, lse, delta, d))
        dk, dv = _attn_bwd_dkv(qk, proj, do_attn, lse, delta, d)
        dks.append(dk)
        dvs.append(dv)
    dproj = _attn_grad_merge(dqs, dks, dvs, tabs)
    held = on_grad("mid", dict(anchor=dproj))
    do_gla, dproj, d_gla_norm_g = _gla_out_bwd(o_gla, proj, gla_norm_g, _after(dcat, *held), dproj)
    dq_f, dk_f, dv_f, dg_f = _gla_bwd(proj, gates, st_f, do_gla, False)
    dgq, dgk, dgv, dg_b = _gla_bwd(proj, gates, st_b, do_gla, True, prev=(dq_f, dk_f, dv_f))
    dz, d_wg, d_gate_bias = _gates_bwd(z, wg, gate_bias, dg_f, dg_b)
    dproj = lax.dynamic_update_slice(dproj, jnp.concatenate([dgq, dgk, dgv], axis=1), (0, 3 * ATTN_W))
    d_w_in_t = _mm_tn("in_proj_wgrad", dproj, n1, 1536, D_MODEL, 1024, BF16, rows_out=IN_W)
    n_tok = S // 1024
    d_w_in_t = _matmul(
        "in_proj_z_wgrad",
        [(dz, pl.BlockSpec((1024, Z_W), lambda i, j, k: (k, 0)), n1, pl.BlockSpec((1024, D_MODEL), lambda i, j, k: (k, 0)), TN)],
        (1, 1, n_tok), jax.ShapeDtypeStruct((IN_W, D_MODEL), BF16), pl.BlockSpec((Z_W, D_MODEL), lambda i, j, k: (z_block, 0)),
        n_tok, into=d_w_in_t)
    held = on_grad("w_in", dict(w_in_t=d_w_in_t))
    tkm = IN_MAIN // 4
    half = S // 2048

    def in_proj_bwd(name, first, a, into):
        return _matmul(
            name,
            [(a, pl.BlockSpec((1024, tkm), lambda i, j, k: (i + first, k)), w_in_t, pl.BlockSpec((tkm, 1024), lambda i, j, k: (k, j)), NN)],
            (half, D_MODEL // 1024, 4), jax.ShapeDtypeStruct((S, D_MODEL), F32),
            pl.BlockSpec((1024, 1024), lambda i, j, k: (i + first, j)), 4, into=into,
            first=(dz, pl.BlockSpec((1024, Z_W), lambda i, j, k: (i + first, 0)),
                   w_in_t, pl.BlockSpec((Z_W, 1024), lambda i, j, k: (z_block, j)), NN))

    dproj = _after(dproj, d_w_in_t, *held)
    dn1 = in_proj_bwd("in_proj_bwd_a", 0, dproj, None)
    held = on_grad("last", dict(last=dn1))
    dn1 = in_proj_bwd("in_proj_bwd_b", half, dproj, _after(dn1, *held))
    grad_x, _, d_norm1_g = _rms_bwd("rms1_bwd", x, norm1_g, dn1, dh1)

    big = dict(w_in_t=d_w_in_t, w_out=d_w_out, w_gate4=d_w_gate4, w_up4=d_w_up4, w_down=d_w_down)
    small = dict(loss=loss_row, norm1_g=d_norm1_g, wg=d_wg, gate_bias=d_gate_bias, gla_norm_g=d_gla_norm_g,
                 attn_norm_g=d_attn_norm_g, norm2_g=d_norm2_g, conv_w=d_conv_w, conv_b=d_conv_b, final_norm_g=d_final_g)
    return grad_x, big, small


def _position():
    return lax.axis_index("x"), lax.axis_index("y"), lax.axis_index("c")


def _other_chips(x, y):
    return [(1 - x, y), (x, 1 - y), (1 - x, 1 - y)]


def _gather_chips(name, shards):
    n = len(shards)

    def body(*refs):
        ins, outs = refs[:n], refs[n:2 * n]
        send, recv, loc = refs[2 * n:]
        x, y, c = _position()
        me = 2 * x + y
        chips = _other_chips(x, y)
        started = []
        for w in range(n):
            own = pltpu.make_async_copy(ins[w], outs[w].at[me], loc.at[w])
            own.start()
            started.append(own)
        sends = []
        for w in range(n):
            for j, (px, py) in enumerate(chips):
                cp = pltpu.make_async_remote_copy(ins[w], outs[w].at[me], send.at[3 * w + j], recv.at[3 * w + j],
                                                  device_id=(px, py, c), device_id_type=MESH)
                cp.start()
                sends.append(cp)
        for w in range(n):
            for j, (px, py) in enumerate(chips):
                pltpu.make_async_remote_copy(ins[w], outs[w].at[2 * px + py], send.at[3 * w + j], recv.at[3 * w + j],
                                             device_id=(px, py, c), device_id_type=MESH).wait_recv()
        for cp in sends:
            cp.wait_send()
        for own in started:
            own.wait()

    return pl.pallas_call(
        body, name=name, in_specs=[ANY] * n, out_specs=[ANY] * n,
        out_shape=[jax.ShapeDtypeStruct((N_CHIPS,) + s.shape, s.dtype) for s in shards],
        scratch_shapes=[pltpu.SemaphoreType.DMA((3 * n,)), pltpu.SemaphoreType.DMA((3 * n,)), pltpu.SemaphoreType.DMA((n,))],
    )(*shards)


def _gather_chips_async(name, shards, collective_id):
    n = len(shards)

    def body(*refs):
        ins, outs = refs[:n], refs[n:2 * n]
        send, recv, loc = refs[2 * n:]
        x, y, c = _position()
        me = 2 * x + y
        chips = _other_chips(x, y)
        barrier = pltpu.get_barrier_semaphore()
        for px, py in chips:
            pl.semaphore_signal(barrier, inc=1, device_id=(px, py, c), device_id_type=MESH)
        pl.semaphore_wait(barrier, len(chips))
        started = []
        for w in range(n):
            own = pltpu.make_async_copy(ins[w], outs[w].at[me], loc.at[w])
            own.start()
            started.append(own)
        sends = []
        for w in range(n):
            for j, (px, py) in enumerate(chips):
                cp = pltpu.make_async_remote_copy(ins[w], outs[w].at[me], send.at[3 * w + j], recv.at[3 * w + j],
                                                  device_id=(px, py, c), device_id_type=MESH)
                cp.start()
                sends.append(cp)
        for w in range(n):
            for j, (px, py) in enumerate(chips):
                pltpu.make_async_remote_copy(ins[w], outs[w].at[2 * px + py], send.at[3 * w + j], recv.at[3 * w + j],
                                             device_id=(px, py, c), device_id_type=MESH).wait_recv()
        for cp in sends:
            cp.wait_send()
        for own in started:
            own.wait()

    return pl.kernel(
        body, name=name, mesh=_sequencer(),
        out_type=[jax.ShapeDtypeStruct((N_CHIPS,) + s.shape, s.dtype) for s in shards],
        scratch_types=[pltpu.SemaphoreType.DMA((3 * n,)), pltpu.SemaphoreType.DMA((3 * n,)), pltpu.SemaphoreType.DMA((n,))],
        compiler_params=pltpu.CompilerParams(collective_id=collective_id),
    )(*shards)


def _gather_halves_async(name, small, shard, collective_id):
    half = shard.shape[1] // 2

    def body(small_ref, shard_ref, small_out, out, send, recv, loc):
        x, y, c = _position()
        me = 2 * x + y
        sibling = (x, y, 1 - c)
        chips = _other_chips(x, y)
        barrier = pltpu.get_barrier_semaphore()
        for px, py in chips:
            pl.semaphore_signal(barrier, inc=1, device_id=(px, py, c), device_id_type=MESH)
        pl.semaphore_signal(barrier, inc=1, device_id=sibling, device_id_type=MESH)
        pl.semaphore_wait(barrier, len(chips) + 1)
        mine = pl.ds(pl.multiple_of(c * half, LANES), half)
        theirs = pl.ds(pl.multiple_of((1 - c) * half, LANES), half)
        own = [pltpu.make_async_copy(small_ref, small_out.at[me], loc.at[0]),
               pltpu.make_async_copy(shard_ref, out.at[me], loc.at[1])]
        for cp in own:
            cp.start()
        sends = []
        for j, (px, py) in enumerate(chips):
            sends.append(pltpu.make_async_remote_copy(small_ref, small_out.at[me], send.at[j], recv.at[j],
                                                      device_id=(px, py, c), device_id_type=MESH))
            sends.append(pltpu.make_async_remote_copy(shard_ref.at[:, mine], out.at[me, :, mine], send.at[3 + j], recv.at[3 + j],
                                                      device_id=(px, py, c), device_id_type=MESH))
        for cp in sends:
            cp.start()
        passed = []
        for j, (px, py) in enumerate(chips):
            slot = 2 * px + py
            pltpu.make_async_remote_copy(shard_ref.at[:, mine], out.at[slot, :, mine], send.at[3 + j], recv.at[3 + j],
                                         device_id=(px, py, c), device_id_type=MESH).wait_recv()
            cp = pltpu.make_async_remote_copy(out.at[slot, :, mine], out.at[slot, :, mine], send.at[6 + j], recv.at[6 + j],
                                              device_id=sibling, device_id_type=MESH)
            cp.start()
            passed.append(cp)
        for j, (px, py) in enumerate(chips):
            slot = 2 * px + py
            pltpu.make_async_remote_copy(small_ref, small_out.at[slot], send.at[j], recv.at[j],
                                         device_id=(px, py, c), device_id_type=MESH).wait_recv()
            pltpu.make_async_remote_copy(out.at[slot, :, theirs], out.at[slot, :, theirs], send.at[6 + j], recv.at[6 + j],
                                         device_id=sibling, device_id_type=MESH).wait_recv()
        for cp in sends + passed:
            cp.wait_send()
        for cp in own:
            cp.wait()

    return pl.kernel(
        body, name=name, mesh=_sequencer(),
        out_type=[jax.ShapeDtypeStruct((N_CHIPS,) + small.shape, small.dtype),
                  jax.ShapeDtypeStruct((N_CHIPS,) + shard.shape, shard.dtype)],
        scratch_types=[pltpu.SemaphoreType.DMA((9,)), pltpu.SemaphoreType.DMA((9,)), pltpu.SemaphoreType.DMA((2,))],
        compiler_params=pltpu.CompilerParams(collective_id=collective_id),
    )(small, shard)


def _sibling_exchange(name, arrs):
    n = len(arrs)

    def body(*refs):
        ins, outs = refs[:n], refs[n:2 * n]
        send, recv = refs[2 * n:]
        x, y, c = _position()
        copies = [pltpu.make_async_remote_copy(ins[w], outs[w], send.at[w], recv.at[w], device_id=(x, y, 1 - c),
                                               device_id_type=MESH) for w in range(n)]
        for cp in copies:
            cp.start()
        for cp in copies:
            cp.wait()

    return pl.pallas_call(
        body, name=name, in_specs=[ANY] * n, out_specs=[ANY] * n,
        out_shape=[jax.ShapeDtypeStruct(a.shape, a.dtype) for a in arrs],
        scratch_shapes=[pltpu.SemaphoreType.DMA((n,)), pltpu.SemaphoreType.DMA((n,))],
    )(*arrs)


def _scatter_chips(name, parts):
    n = len(parts)

    def body(*refs):
        ins, outs = refs[:n], refs[n:2 * n]
        send, recv, loc = refs[2 * n:]
        x, y, c = _position()
        me = 2 * x + y
        chips = _other_chips(x, y)
        started = []
        for w in range(n):
            own = pltpu.make_async_copy(ins[w].at[me], outs[w].at[me], loc.at[w])
            own.start()
            started.append(own)
        sends = []
        for w in range(n):
            for j, (px, py) in enumerate(chips):
                cp = pltpu.make_async_remote_copy(ins[w].at[2 * px + py], outs[w].at[me], send.at[3 * w + j],
                                                  recv.at[3 * w + j], device_id=(px, py, c), device_id_type=MESH)
                cp.start()
                sends.append(cp)
        for w in range(n):
            for j, (px, py) in enumerate(chips):
                pltpu.make_async_remote_copy(ins[w].at[me], outs[w].at[2 * px + py], send.at[3 * w + j], recv.at[3 * w + j],
                                             device_id=(px, py, c), device_id_type=MESH).wait_recv()
        for cp in sends:
            cp.wait_send()
        for own in started:
            own.wait()

    return pl.pallas_call(
        body, name=name, in_specs=[ANY] * n, out_specs=[ANY] * n,
        out_shape=[jax.ShapeDtypeStruct(p.shape, p.dtype) for p in parts],
        scratch_shapes=[pltpu.SemaphoreType.DMA((3 * n,)), pltpu.SemaphoreType.DMA((3 * n,)), pltpu.SemaphoreType.DMA((n,))],
    )(*parts)


def _sequencer():
    return plsc.ScalarSubcoreMesh(axis_name="sequencer", num_cores=1)


def _sibling_exchange_async(name, arrs, collective_id):
    n = len(arrs)

    def body(*refs):
        ins, outs = refs[:n], refs[n:2 * n]
        send, recv = refs[2 * n:]
        x, y, c = _position()
        sibling = (x, y, 1 - c)
        barrier = pltpu.get_barrier_semaphore()
        pl.semaphore_signal(barrier, inc=1, device_id=sibling, device_id_type=MESH)
        pl.semaphore_wait(barrier, 1)
        copies = [pltpu.make_async_remote_copy(ins[w], outs[w], send.at[w], recv.at[w], device_id=sibling,
                                               device_id_type=MESH) for w in range(n)]
        for cp in copies:
            cp.start()
        for cp in copies:
            cp.wait()

    return pl.kernel(
        body, name=name, out_type=[jax.ShapeDtypeStruct(a.shape, a.dtype) for a in arrs],
        scratch_types=[pltpu.SemaphoreType.DMA((n,)), pltpu.SemaphoreType.DMA((n,))],
        compiler_params=pltpu.CompilerParams(collective_id=collective_id), mesh=_sequencer(),
    )(*arrs)


def _scatter_chips_async(name, parts, collective_id):
    n = len(parts)

    def body(*refs):
        ins, outs = refs[:n], refs[n:2 * n]
        send, recv, loc = refs[2 * n:]
        x, y, c = _position()
        me = 2 * x + y
        chips = _other_chips(x, y)
        barrier = pltpu.get_barrier_semaphore()
        for px, py in chips:
            pl.semaphore_signal(barrier, inc=1, device_id=(px, py, c), device_id_type=MESH)
        pl.semaphore_wait(barrier, len(chips))
        started = []
        for w in range(n):
            own = pltpu.make_async_copy(ins[w].at[me], outs[w].at[me], loc.at[w])
            own.start()
            started.append(own)
        sends = []
        for w in range(n):
            for j, (px, py) in enumerate(chips):
                cp = pltpu.make_async_remote_copy(ins[w].at[2 * px + py], outs[w].at[me], send.at[3 * w + j],
                                                  recv.at[3 * w + j], device_id=(px, py, c), device_id_type=MESH)
                cp.start()
                sends.append(cp)
        for w in range(n):
            for j, (px, py) in enumerate(chips):
                pltpu.make_async_remote_copy(ins[w].at[me], outs[w].at[2 * px + py], send.at[3 * w + j], recv.at[3 * w + j],
                                             device_id=(px, py, c), device_id_type=MESH).wait_recv()
        for cp in sends:
            cp.wait_send()
        for own in started:
            own.wait()

    return pl.kernel(
        body, name=name, out_type=[jax.ShapeDtypeStruct(p.shape, p.dtype) for p in parts],
        scratch_types=[pltpu.SemaphoreType.DMA((3 * n,)), pltpu.SemaphoreType.DMA((3 * n,)), pltpu.SemaphoreType.DMA((n,))],
        compiler_params=pltpu.CompilerParams(collective_id=collective_id), mesh=_sequencer(),
    )(*parts)


def _allreduce_rows(buf):
    R = buf.shape[0]

    def body(in_ref, out_ref, land, send, recv):
        x, y, c = _position()
        me = 4 * x + 2 * y + c
        land[pl.ds(me, 1)] = in_ref[...][None]
        peers = []
        for mask in range(1, N_DEV):
            px = 1 - x if mask & 4 else x
            py = 1 - y if mask & 2 else y
            pc = 1 - c if mask & 1 else c
            peers.append((px, py, pc))
        sends = []
        for k, peer in enumerate(peers):
            cp = pltpu.make_async_remote_copy(in_ref, land.at[me], send.at[k], recv.at[k], device_id=peer, device_id_type=MESH)
            cp.start()
            sends.append(cp)
        for k, (px, py, pc) in enumerate(peers):
            pltpu.make_async_remote_copy(in_ref, land.at[4 * px + 2 * py + pc], send.at[k], recv.at[k],
                                         device_id=(px, py, pc), device_id_type=MESH).wait_recv()
        for cp in sends:
            cp.wait_send()
        tot = land[0]
        for i in range(1, N_DEV):
            tot = tot + land[i]
        out_ref[...] = tot

    vm = pl.BlockSpec(memory_space=pltpu.VMEM)
    return pl.pallas_call(
        body, name="allreduce_small", in_specs=[vm], out_specs=vm, out_shape=jax.ShapeDtypeStruct((R, LANES), F32),
        scratch_shapes=[pltpu.VMEM((N_DEV, R, LANES), F32), pltpu.SemaphoreType.DMA((N_DEV - 1,)),
                        pltpu.SemaphoreType.DMA((N_DEV - 1,))],
    )(buf)


def _tile2d(r, c, cap):
    if r <= cap:
        return r, c
    fits = [t for t in range(16, cap + 1, 16) if r % t == 0]
    return (max(fits), c) if fits else (r, 256)


def _pair_sum(name, a, b):
    n, r, c = a.shape
    tr, tc = _tile2d(r, c, 1024)

    def body(a_ref, b_ref, o_ref):
        o_ref[...] = (a_ref[...].astype(F32) + b_ref[...].astype(F32)).astype(BF16)

    blk = pl.BlockSpec((None, tr, tc), lambda s, i, j: (s, i, j))
    return pl.pallas_call(
        body, name=name, grid=(n, r // tr, c // tc), in_specs=[blk, blk], out_specs=blk,
        out_shape=jax.ShapeDtypeStruct(a.shape, BF16),
    )(a, b)


def _adamw_math(w, m, v, g):
    m2 = ADAM_B1 * m + (1.0 - ADAM_B1) * g
    v2 = ADAM_B2 * v + (1.0 - ADAM_B2) * (g * g)
    m_hat = m2 / (1.0 - ADAM_B1 ** ADAM_STEP)
    v_hat = v2 / (1.0 - ADAM_B2 ** ADAM_STEP)
    delta = -ADAM_LR * (m_hat / (jnp.sqrt(v_hat) + ADAM_EPS) + ADAM_WD * w)
    return delta, m2, v2


def _adamw(name, w, m, v, g):
    r, c = w.shape
    stacked = g.ndim == 3
    tr, tc = _tile2d(r, c, 256)

    def body(w_ref, m_ref, v_ref, g_ref, go_ref, d_ref, m2_ref, v2_ref):
        if stacked:
            gv = g_ref[0].astype(F32)
            for i in range(1, N_CHIPS):
                gv = gv + g_ref[i].astype(F32)
        else:
            gv = g_ref[...]
        delta, m2, v2 = _adamw_math(w_ref[...], m_ref[...], v_ref[...], gv)
        go_ref[...] = gv
        d_ref[...] = delta
        m2_ref[...] = m2
        v2_ref[...] = v2

    blk = pl.BlockSpec((tr, tc), lambda i, j: (i, j))
    g_spec = pl.BlockSpec((N_CHIPS, tr, tc), lambda i, j: (0, i, j)) if stacked else blk
    out = jax.ShapeDtypeStruct((r, c), F32)
    return pl.pallas_call(
        body, name=name, grid=(r // tr, c // tc), in_specs=[blk, blk, blk, g_spec], out_specs=[blk] * 4, out_shape=[out] * 4,
    )(w, m, v, g)


def _pack_rows(pieces):
    flat = jnp.concatenate([p.reshape(-1) for p in pieces])
    rows = flat.shape[0] // LANES
    pad = (-rows) % 8
    return jnp.pad(flat.reshape(rows, LANES), ((0, pad), (0, 0)))


def _unpack_rows(buf, shapes):
    flat = buf.reshape(-1)
    out, at = [], 0
    for s in shapes:
        size = math.prod(s)
        out.append(flat[at:at + size].reshape(s))
        at += size
    return out


SMALL_NAMES = ("norm1_g", "gf_up", "gf_b", "gb_up", "gb_b", "gla_norm_g", "attn_norm_g", "norm2_g", "conv_w", "conv_b",
               "final_norm_g")
BIG_NAMES = ("w_in", "w_out", "w_gate", "w_up", "w_down")
WEIGHT_ORDER = ("norm1_g", "w_in", "gf_up", "gf_b", "gb_up", "gb_b", "gla_norm_g", "attn_norm_g", "w_out", "norm2_g",
                "w_gate", "w_up", "conv_w", "conv_b", "w_down", "final_norm_g")


def kernel(x, norm1_g, w_in, gf_up, gf_b, gb_up, gb_b, gla_norm_g, attn_norm_g, w_out, norm2_g, w_gate, w_up, conv_w, conv_b, w_down, final_norm_g, loss_target, m_norm1_g, m_w_in, m_gf_up, m_gf_b, m_gb_up, m_gb_b, m_gla_norm_g, m_attn_norm_g, m_w_out, m_norm2_g, m_w_gate, m_w_up, m_conv_w, m_conv_b, m_w_down, m_final_norm_g, v_norm1_g, v_w_in, v_gf_up, v_gf_b, v_gb_up, v_gb_b, v_gla_norm_g, v_attn_norm_g, v_w_out, v_norm2_g, v_w_gate, v_w_up, v_conv_w, v_conv_b, v_w_down, v_final_norm_g):
    w = dict(norm1_g=norm1_g, w_in=w_in, gf_up=gf_up, gf_b=gf_b, gb_up=gb_up, gb_b=gb_b, gla_norm_g=gla_norm_g,
             attn_norm_g=attn_norm_g, w_out=w_out, norm2_g=norm2_g, w_gate=w_gate, w_up=w_up, conv_w=conv_w, conv_b=conv_b,
             w_down=w_down, final_norm_g=final_norm_g)
    m = dict(norm1_g=m_norm1_g, w_in=m_w_in, gf_up=m_gf_up, gf_b=m_gf_b, gb_up=m_gb_up, gb_b=m_gb_b, gla_norm_g=m_gla_norm_g,
             attn_norm_g=m_attn_norm_g, w_out=m_w_out, norm2_g=m_norm2_g, w_gate=m_w_gate, w_up=m_w_up, conv_w=m_conv_w,
             conv_b=m_conv_b, w_down=m_w_down, final_norm_g=m_final_norm_g)
    v = dict(norm1_g=v_norm1_g, w_in=v_w_in, gf_up=v_gf_up, gf_b=v_gf_b, gb_up=v_gb_up, gb_b=v_gb_b, gla_norm_g=v_gla_norm_g,
             attn_norm_g=v_attn_norm_g, w_out=v_w_out, norm2_g=v_norm2_g, w_gate=v_w_gate, w_up=v_w_up, conv_w=v_conv_w,
             conv_b=v_conv_b, w_down=v_w_down, final_norm_g=v_final_norm_g)
    S = x.shape[1]
    chip = 2 * lax.axis_index("x") + lax.axis_index("y")
    n_in = IN_W // N_CHIPS
    n_ff = D_FF // N_CHIPS
    n_gk = GLA_K // N_CHIPS

    def owned(t):
        return {k: (jnp.transpose(t[k][0]) if k == "w_in" else t[k][0]) for k in BIG_NAMES}

    own_w, own_m, own_v = owned(w), owned(m), owned(v)
    shard = {k: own_w[k].astype(BF16) for k in BIG_NAMES}
    small_shard = _pack_rows([gf_up[0], gb_up[0], conv_w[0]])
    small4, w_in4 = _gather_halves_async("gather_w_in", small_shard, shard["w_in"], 0)
    w_out4, w_gate4, w_up4 = _gather_chips_async("gather_w_mid", [shard["w_out"], shard["w_gate"], shard["w_up"]], 1)
    (w_down4,) = _gather_chips_async("gather_w_down", [shard["w_down"]], 2)
    w_in_t = w_in4.reshape(IN_W, D_MODEL)
    rows_up = GATE_RANK * n_gk // LANES
    rows_cw = 3 * n_ff // LANES
    gf_full = jnp.transpose(small4[:, 0:rows_up].reshape(N_CHIPS, GATE_RANK, n_gk), (1, 0, 2)).reshape(GATE_RANK, GLA_K)
    gb_full = jnp.transpose(small4[:, rows_up:2 * rows_up].reshape(N_CHIPS, GATE_RANK, n_gk), (1, 0, 2)).reshape(GATE_RANK, GLA_K)
    cw_full = jnp.transpose(small4[:, 2 * rows_up:2 * rows_up + rows_cw].reshape(N_CHIPS, 3, n_ff), (1, 0, 2)).reshape(3, D_FF)
    wg = jnp.zeros((Z_W, 2 * GLA_K), F32)
    wg = wg.at[0:GATE_RANK, 0:GLA_K].set(gf_full).at[GATE_RANK:2 * GATE_RANK, GLA_K:].set(gb_full).astype(BF16)
    gate_bias = jnp.concatenate([gf_b, gb_b], axis=1)

    pending, contributions, next_id = [], {}, [3]

    def as_shards(group, arrays):
        if group == "w_in":
            return dict(w_in=arrays["w_in_t"].reshape(N_CHIPS, n_in, D_MODEL))
        if group == "w_out":
            return dict(w_out=arrays["w_out"].reshape(N_CHIPS, D_MODEL // N_CHIPS, D_MODEL))
        if group == "w_down":
            return dict(w_down=arrays["w_down"].reshape(N_CHIPS, n_ff, D_MODEL))
        return arrays

    out = {}

    def swap(group, arrays):
        mine = as_shards(group, arrays)
        pending.append((group, mine, _sibling_exchange_async(f"sibling_{group}", list(mine.values()), next_id[0])))
        next_id[0] += 1

    def sum_and_send(anchor):
        tag, mine, theirs = pending.pop()
        sums = [_pair_sum(f"pair_sum_{k}", mine[k], _after(t, *anchor)) for k, t in zip(mine, theirs)]
        contributions.update(zip(mine, _scatter_chips_async(f"scatter_{tag}", sums, next_id[0])))
        next_id[0] += 1
        return sums

    def update(names, anchor):
        for k in names:
            res = _adamw(f"adamw_{k}", own_w[k], own_m[k], own_v[k], _after(contributions[k], *anchor))
            out[k] = [(jnp.transpose(r) if k == "w_in" else r)[None] for r in res]
        return [out[k][0] for k in names]

    def on_grad(event, arrays):
        anchor = list(arrays.values())
        held = []
        if event in ("w_gate_w_up", "w_out", "mid", "last"):
            held += sum_and_send(anchor)
        if event == "mid":
            held += update(("w_down", "w_gate", "w_up"), anchor)
        if event == "last":
            held += update(("w_out",), anchor)
        if event in ("w_down", "w_gate_w_up", "w_out", "w_in"):
            swap(event, arrays)
        return held

    grad_x, _, small = _local_step(
        x[0], loss_target[0], norm1_g, w_in_t, wg, gate_bias, gla_norm_g, attn_norm_g,
        w_out4.reshape(D_MODEL, D_MODEL), norm2_g, w_gate4, w_up4, cw_full, conv_b, w_down4.reshape(D_FF, D_MODEL), final_norm_g,
        on_grad=on_grad)
    update(("w_in",), [grad_x])

    d_gf_up = small["wg"][0:GATE_RANK, 0:GLA_K]
    d_gb_up = small["wg"][GATE_RANK:2 * GATE_RANK, GLA_K:]
    pieces = [small["loss"], small["norm1_g"], d_gf_up, small["gate_bias"][:, :GLA_K], d_gb_up, small["gate_bias"][:, GLA_K:],
              small["gla_norm_g"], small["attn_norm_g"], small["norm2_g"], small["conv_w"], small["conv_b"], small["final_norm_g"]]
    total = _allreduce_rows(_pack_rows(pieces))
    summed = _unpack_rows(total, [p.shape for p in pieces])
    loss = summed[0][0, 0]
    g_small = dict(zip(SMALL_NAMES, summed[1:]))
    g_small["gf_up"] = lax.dynamic_slice_in_dim(g_small["gf_up"], chip * n_gk, n_gk, axis=1)
    g_small["gb_up"] = lax.dynamic_slice_in_dim(g_small["gb_up"], chip * n_gk, n_gk, axis=1)
    g_small["conv_w"] = lax.dynamic_slice_in_dim(g_small["conv_w"], chip * n_ff, n_ff, axis=1)
    packed = [_pack_rows([t[k] for k in SMALL_NAMES]) for t in (w, m, v, g_small)]
    res = _adamw("adamw_small", *packed)
    shapes = [w[k].shape for k in SMALL_NAMES]
    for k, vals in zip(SMALL_NAMES, zip(*[_unpack_rows(r, shapes) for r in res])):
        out[k] = list(vals)

    grads, deltas, new_m, new_v = ([out[k][i] for k in WEIGHT_ORDER] for i in range(4))
    return (loss, grad_x[None], *grads, *deltas, *new_m, *new_v)
```

```python
import functools
import math

import jax
import jax.numpy as jnp
from jax import lax
from jax.experimental import pallas as pl
from jax.experimental.pallas import tpu as pltpu
from jax.experimental.pallas import tpu_sc as plsc

F32 = jnp.float32
BF16 = jnp.bfloat16

D_MODEL = 2048
ATTN_W = 1024
HEAD = 128
N_HEADS = 8
N_SIDE = 64
DILATIONS = (1, 4, 16)
ROPE_THETA = 500000.0
ROPE_DIM = 32
GLA_K = 512
GLA_V = 1024
GLA_HEADS = 4
GLA_DK = 128
GLA_DV = 256
GATE_RANK = 16
GATE_NORM = 16.0
CHUNK = 64
IN_MAIN = 6144
IN_W = 6176
Z_W = IN_W - IN_MAIN
D_FF = 5632
EPS = 1e-6
N_CHIPS = 4
N_DEV = 8
LANES = 128

ADAM_LR = 0.001
ADAM_B1 = 0.9
ADAM_B2 = 0.999
ADAM_EPS = 1e-08
ADAM_WD = 0.01
ADAM_STEP = 10

NEG = -1e30
MESH = pl.DeviceIdType.MESH
ANY = pl.BlockSpec(memory_space=pl.ANY)

NN = ((1,), (0,))
NT = ((1,), (1,))
TN = ((0,), (0,))


def _dot(a, b, dims=NN):
    return lax.dot_general(a, b, (dims, ((), ())), preferred_element_type=F32)


def _sigmoid(x):
    return 0.5 * jnp.tanh(0.5 * x) + 0.5


def _after(x, *deps):
    return lax.optimization_barrier((x,) + deps)[0]


def _matmul(name, pairs, grid, out_shape, out_spec, nk, res=None, into=None, first=None):
    n_in = 2 * len(pairs) + (res is not None)
    dims = [p[4] for p in pairs]

    n_ops = n_in + (into is not None) + 2 * (first is not None)

    def body(*refs):
        ins, o_ref = refs[:n_in], refs[n_ops]

        def partial_sum():
            tot = None
            for p, dn in enumerate(dims):
                a, b = ins[2 * p][...], ins[2 * p + 1][...]
                t = _dot(a.astype(BF16), b.astype(BF16), dn)
                tot = t if tot is None else tot + t
            return tot

        if nk == 1:
            t = partial_sum()
            if res is not None:
                t = t + ins[-1][...]
            o_ref[...] = t.astype(o_ref.dtype)
        else:
            acc_ref = refs[n_ops + 1]
            k = pl.program_id(2)

            @pl.when(k == 0)
            def _():
                if first is not None:
                    start = _dot(refs[n_in][...].astype(BF16), refs[n_in + 1][...].astype(BF16), first[4])
                    acc_ref[...] = start + ins[-1][...] if res is not None else start
                elif res is not None:
                    acc_ref[...] = ins[-1][...]
                else:
                    acc_ref[...] = jnp.zeros_like(acc_ref)

            acc_ref[...] += partial_sum()

            @pl.when(k == nk - 1)
            def _():
                o_ref[...] = acc_ref[...].astype(o_ref.dtype)

    operands, in_specs = [], []
    for a, a_spec, b, b_spec, _ in pairs:
        operands += [a, b]
        in_specs += [a_spec, b_spec]
    if res is not None:
        operands.append(res[0])
        in_specs.append(res[1])
    if first is not None:
        assert nk > 1
        operands += [first[0], first[2]]
        in_specs += [first[1], first[3]]
    acc_shape = tuple(s for s in out_spec.block_shape if s is not None)
    scratch = [pltpu.VMEM(acc_shape, F32)] if nk > 1 else []
    aliases = {}
    if into is not None:
        aliases = {len(operands): 0}
        operands.append(into)
        in_specs.append(ANY)
    return pl.pallas_call(
        body, name=name, grid=grid, in_specs=in_specs, out_specs=out_spec, out_shape=out_shape, scratch_shapes=scratch,
        input_output_aliases=aliases,
    )(*operands)


def _mm_nn(name, a, b, tm, tn, out_dtype, res=None):
    M, K = a.shape
    N = b.shape[1]
    pairs = [(a, pl.BlockSpec((tm, K), lambda j, i: (i, 0)), b, pl.BlockSpec((K, tn), lambda j, i: (0, j)), NN)]
    r = None if res is None else (res, pl.BlockSpec((tm, tn), lambda j, i: (i, j)))
    return _matmul(name, pairs, (N // tn, M // tm), jax.ShapeDtypeStruct((M, N), out_dtype),
                   pl.BlockSpec((tm, tn), lambda j, i: (i, j)), 1, r)


def _mm_nn_sharded(name, a, b4, tm, out_dtype):
    M, K = a.shape
    n = b4.shape[2]
    pairs = [(a, pl.BlockSpec((tm, K), lambda j, i: (i, 0)), b4, pl.BlockSpec((None, K, n), lambda j, i: (j, 0, 0)), NN)]
    return _matmul(name, pairs, (N_CHIPS, M // tm), jax.ShapeDtypeStruct((M, N_CHIPS * n), out_dtype),
                   pl.BlockSpec((tm, n), lambda j, i: (i, j)), 1)


def _mm_nt(name, a, b, tm, tn, out_dtype, res=None, n_out=None):
    M, K = a.shape
    N = b.shape[0] if n_out is None else n_out
    pairs = [(a, pl.BlockSpec((tm, K), lambda j, i: (i, 0)), b, pl.BlockSpec((tn, K), lambda j, i: (j, 0)), NT)]
    r = None if res is None else (res, pl.BlockSpec((tm, tn), lambda j, i: (i, j)))
    return _matmul(name, pairs, (N // tn, M // tm), jax.ShapeDtypeStruct((M, N), out_dtype),
                   pl.BlockSpec((tm, tn), lambda j, i: (i, j)), 1, r)


def _mm_tn(name, a, g, tka, tn, tmm, out_dtype, out3=None, rows_out=None):
    M, Ka = a.shape
    N = g.shape[1]
    pairs = [(a, pl.BlockSpec((tmm, tka), lambda i, j, k: (k, i)), g, pl.BlockSpec((tmm, tn), lambda i, j, k: (k, j)), TN)]
    if out3 is None:
        shape, spec = (Ka if rows_out is None else rows_out, N), pl.BlockSpec((tka, tn), lambda i, j, k: (i, j))
    else:
        shape, spec = (N // out3, Ka, out3), pl.BlockSpec((None, tka, tn), lambda i, j, k: (j, i, 0))
    return _matmul(name, pairs, (Ka // tka, N // tn, M // tmm), jax.ShapeDtypeStruct(shape, out_dtype), spec, M // tmm)


def _rms_fwd(name, x, g, tm=512):
    S, D = x.shape

    def body(x_ref, g_ref, o_ref):
        xv = x_ref[...]
        r = lax.rsqrt(jnp.mean(xv * xv, axis=-1, keepdims=True) + EPS)
        o_ref[...] = (xv * r * g_ref[...]).astype(o_ref.dtype)

    return pl.pallas_call(
        body, name=name, grid=(S // tm,),
        in_specs=[pl.BlockSpec((tm, D), lambda i: (i, 0)), pl.BlockSpec((1, D), lambda i: (0, 0))],
        out_specs=pl.BlockSpec((tm, D), lambda i: (i, 0)), out_shape=jax.ShapeDtypeStruct((S, D), BF16),
    )(x, g)


def _rms_bwd(name, x, g, dn, dres, tm=512):
    S, D = x.shape

    def body(x_ref, g_ref, dn_ref, dres_ref, dx_ref, dxb_ref, dg_ref):
        i = pl.program_id(0)

        @pl.when(i == 0)
        def _():
            dg_ref[...] = jnp.zeros_like(dg_ref)

        xv = x_ref[...]
        r = lax.rsqrt(jnp.mean(xv * xv, axis=-1, keepdims=True) + EPS)
        xhat = xv * r
        dnv = dn_ref[...].astype(F32)
        dg_ref[...] += jnp.sum(dnv * xhat, axis=0, keepdims=True)
        t = dnv * g_ref[...]
        dx = r * (t - xhat * jnp.mean(t * xhat, axis=-1, keepdims=True)) + dres_ref[...]
        dx_ref[...] = dx
        dxb_ref[...] = dx.astype(BF16)

    row = pl.BlockSpec((tm, D), lambda i: (i, 0))
    vec = pl.BlockSpec((1, D), lambda i: (0, 0))
    return pl.pallas_call(
        body, name=name, grid=(S // tm,), in_specs=[row, vec, row, row], out_specs=[row, row, vec],
        out_shape=[jax.ShapeDtypeStruct((S, D), F32), jax.ShapeDtypeStruct((S, D), BF16), jax.ShapeDtypeStruct((1, D), F32)],
    )(x, g, dn, dres)


def _final_loss(h2, g, target, tm=512):
    S, D = h2.shape

    def body(x_ref, g_ref, t_ref, loss_ref, dg_ref, dx_ref, dxb_ref):
        i = pl.program_id(0)

        @pl.when(i == 0)
        def _():
            loss_ref[...] = jnp.zeros_like(loss_ref)
            dg_ref[...] = jnp.zeros_like(dg_ref)

        xv = x_ref[...]
        r = lax.rsqrt(jnp.mean(xv * xv, axis=-1, keepdims=True) + EPS)
        xhat = xv * r
        gv = g_ref[...]
        diff = xhat * gv - t_ref[...]
        per_tok = jnp.mean(diff * diff, axis=-1, keepdims=True)
        loss_ref[...] += 0.5 * jnp.sum(per_tok, axis=0, keepdims=True)
        dy = diff * (1.0 / D)
        dg_ref[...] += jnp.sum(dy * xhat, axis=0, keepdims=True)
        t = dy * gv
        dx = r * (t - xhat * jnp.mean(t * xhat, axis=-1, keepdims=True))
        dx_ref[...] = dx
        dxb_ref[...] = dx.astype(BF16)

    row = pl.BlockSpec((tm, D), lambda i: (i, 0))
    vec = pl.BlockSpec((1, D), lambda i: (0, 0))
    return pl.pallas_call(
        body, name="final_loss", grid=(S // tm,), in_specs=[row, vec, row],
        out_specs=[pl.BlockSpec((1, LANES), lambda i: (0, 0)), vec, row, row],
        out_shape=[jax.ShapeDtypeStruct((1, LANES), F32), jax.ShapeDtypeStruct((1, D), F32),
                   jax.ShapeDtypeStruct((S, D), F32), jax.ShapeDtypeStruct((S, D), BF16)],
    )(h2, g, target)


def _rope_tables(S):
    pos = jnp.arange(S, dtype=F32)
    inv_freq = ROPE_THETA ** (-jnp.arange(0, ROPE_DIM, 2, dtype=F32) / ROPE_DIM)
    ang = pos[:, None] * inv_freq[None, :]
    cos, sin = jnp.cos(ang), jnp.sin(ang)
    half = ROPE_DIM // 2
    rest = HEAD - ROPE_DIM
    z_h, z_r = jnp.zeros((S, half), F32), jnp.zeros((S, rest), F32)
    tab_c = jnp.concatenate([cos, cos, jnp.ones((S, rest), F32)], axis=1)
    tab_up = jnp.concatenate([z_h, sin, z_r], axis=1)
    tab_dn = jnp.concatenate([-sin, z_h, z_r], axis=1)
    return tab_c, tab_up, tab_dn


def _rope_head(t, c, up, dn):
    half = ROPE_DIM // 2
    return t * c + pltpu.roll(t, half, axis=1) * up + pltpu.roll(t, HEAD - half, axis=1) * dn


def _rope_fwd(proj, tabs, tm=512):
    S = proj.shape[0]
    W = 2 * ATTN_W

    def body(p_ref, c_ref, up_ref, dn_ref, o_ref):
        c, up, dn = c_ref[...], up_ref[...], dn_ref[...]
        for h in range(W // HEAD):
            sl = slice(h * HEAD, (h + 1) * HEAD)
            o_ref[:, sl] = _rope_head(p_ref[:, sl].astype(F32), c, up, dn).astype(BF16)

    tab = pl.BlockSpec((tm, HEAD), lambda i: (i, 0))
    return pl.pallas_call(
        body, name="rope_fwd", grid=(S // tm,), in_specs=[pl.BlockSpec((tm, W), lambda i: (i, 0)), tab, tab, tab],
        out_specs=pl.BlockSpec((tm, W), lambda i: (i, 0)), out_shape=jax.ShapeDtypeStruct((S, W), BF16),
    )(proj, *tabs)


def _attn_grad_merge(dqs, dks, dvs, tabs, tm=256):
    S = dqs[0].shape[0]

    def body(*refs):
        q_refs, k_refs, v_refs = refs[0:3], refs[3:6], refs[6:9]
        c, up, dn = refs[9][...], refs[10][...], refs[11][...]
        o_ref = refs[12]
        for h in range(N_HEADS):
            sl = slice(h * HEAD, (h + 1) * HEAD)
            for part, rs in ((0, q_refs), (1, k_refs)):
                t = rs[0][:, sl].astype(F32) + rs[1][:, sl].astype(F32) + rs[2][:, sl].astype(F32)
                osl = slice(part * ATTN_W + h * HEAD, part * ATTN_W + (h + 1) * HEAD)
                o_ref[:, osl] = _rope_head(t, c, -up, -dn).astype(BF16)
        o_ref[:, 2 * ATTN_W:] = (v_refs[0][...].astype(F32) + v_refs[1][...].astype(F32)
                                 + v_refs[2][...].astype(F32)).astype(BF16)

    blk = pl.BlockSpec((tm, ATTN_W), lambda i: (i, 0))
    tab = pl.BlockSpec((tm, HEAD), lambda i: (i, 0))
    return pl.pallas_call(
        body, name="attn_grad_merge", grid=(S // tm,), in_specs=[blk] * 9 + [tab] * 3,
        out_specs=pl.BlockSpec((tm, 3 * ATTN_W), lambda i: (i, 0)), out_shape=jax.ShapeDtypeStruct((S, IN_MAIN), BF16),
    )(*dqs, *dks, *dvs, *tabs)


SUB = 128
Q_COL, K_COL, V_COL = 0, ATTN_W // HEAD, 2 * ATTN_W // HEAD


class _AttnGeo:
    def __init__(self, S, d):
        self.S, self.d, self.L = S, d, S // d
        self.halo = N_SIDE * d
        self.TB = min(2048, S)
        self.W = self.TB + 2 * self.halo
        self.n_sub = self.TB // SUB
        self.grid = (S // self.TB, N_HEADS)
        self.dt = F32 if d > 1 else BF16
        self.su = min(d, 4)
        self.sb = d // self.su
        assert self.TB % (SUB * d) == 0 and self.TB % self.halo == 0

    def specs(self, width, col0, per_head=True):
        ratio = self.TB // self.halo
        last = self.S // self.halo - 1
        col = (lambda h: col0 + h) if per_head else (lambda h: col0)
        cur = pl.BlockSpec((self.TB, width), lambda i, h: (i, col(h)))
        prev = pl.BlockSpec((self.halo, width), lambda i, h: (jnp.maximum(i * ratio - 1, 0), col(h)))
        nxt = pl.BlockSpec((self.halo, width), lambda i, h: (jnp.minimum((i + 1) * ratio, last), col(h)))
        return cur, prev, nxt

    def scratch(self, rows, dtype=None):
        nat = pltpu.VMEM((rows, LANES), self.dt if dtype is None else dtype)
        return [nat] if self.sb == 1 else [nat, pltpu.VMEM((rows, LANES), F32)]

    def bind(self, refs):
        nat = next(refs)
        return (nat, nat) if self.sb == 1 else (nat, next(refs))

    def spread(self, pair):
        nat, streams = pair
        if self.sb > 1:
            n = nat.shape[0] // self.sb
            for a in range(self.sb):
                streams[a * n:(a + 1) * n, :] = nat[pl.ds(a, n, stride=self.sb), :]
        return streams

    def gather(self, pair):
        nat, streams = pair
        if self.sb > 1:
            n = nat.shape[0] // self.sb
            for a in range(self.sb):
                nat[pl.ds(a, n, stride=self.sb), :] = streams[a * n:(a + 1) * n, :]
        return nat

    def rows(self, sub, n, total):
        res, blk = sub % self.d, sub // self.d
        a, b = res % self.sb, res // self.sb
        start = a * (total // self.sb) + b + self.su * SUB * blk
        return pl.ds(start, n, stride=self.su) if self.su > 1 else pl.ds(start, n)

    def band(self):
        row = lax.broadcasted_iota(jnp.int32, (SUB, 2 * SUB), 0)
        col = lax.broadcasted_iota(jnp.int32, (SUB, 2 * SUB), 1)
        return (col >= row) & (col <= row + 2 * N_SIDE), col

    def mask(self, sub, band):
        inside, col = band
        blk, n_blk = sub // self.d, self.TB // (SUB * self.d)
        base = pl.program_id(0) * (self.TB // self.d) + SUB * blk
        if blk == 0:
            inside = inside & (col >= N_SIDE - base)
        if blk == n_blk - 1:
            inside = inside & (col < self.L + N_SIDE - base)
        return inside

    def fill(self, dst, c_ref):
        dst[...] = c_ref[...].astype(dst.dtype)

    def fill_window(self, dst, p_ref, c_ref, n_ref):
        dst[0:self.halo] = p_ref[...].astype(dst.dtype)
        dst[self.halo:self.halo + self.TB] = c_ref[...].astype(dst.dtype)
        dst[self.halo + self.TB:] = n_ref[...].astype(dst.dtype)


def _lane_of(tile, h):
    lane = lax.broadcasted_iota(jnp.int32, tile.shape, 1)
    return jnp.sum(jnp.where(lane == h, tile, 0.0), axis=1, keepdims=True)


def _attn_fwd(qk, proj, d):
    S = qk.shape[0]
    geo = _AttnGeo(S, d)
    scale = HEAD ** -0.5

    def body(q_ref, kp, kc, kn, vp, vc, vn, o_ref, lse_ref, *scratch):
        h = pl.program_id(1)
        refs = iter(scratch)
        q_p, k_p, v_p, o_p, l_p = (geo.bind(refs) for _ in range(5))
        geo.fill(q_p[0], q_ref)
        geo.fill_window(k_p[0], kp, kc, kn)
        geo.fill_window(v_p[0], vp, vc, vn)
        qs, ks, vs = geo.spread(q_p), geo.spread(k_p), geo.spread(v_p)
        os, ls = o_p[1], l_p[1]
        band = geo.band()
        for sub in range(geo.n_sub):
            rq, rw = geo.rows(sub, SUB, geo.TB), geo.rows(sub, 2 * SUB, geo.W)
            q_r, k_r, v_r = qs[rq, :].astype(BF16), ks[rw, :].astype(BF16), vs[rw, :].astype(BF16)
            s = jnp.where(geo.mask(sub, band), _dot(q_r, k_r, NT) * scale, NEG)
            m = jnp.max(s, axis=1, keepdims=True)
            p = jnp.exp(s - m)
            l = jnp.sum(p, axis=1, keepdims=True)
            os[rq, :] = _dot(p.astype(BF16), v_r) / l
            ls[rq, :] = jnp.broadcast_to(m + jnp.log(l), (SUB, LANES))
        o_ref[...] = geo.gather(o_p)[...].astype(BF16)

        @pl.when(h == 0)
        def _():
            lse_ref[...] = jnp.zeros_like(lse_ref)

        lane = lax.broadcasted_iota(jnp.int32, (geo.TB, LANES), 1)
        lse_ref[...] = jnp.where(lane == h, geo.gather(l_p)[...], lse_ref[...])

    q_cur, _, _ = geo.specs(HEAD, Q_COL)
    k_specs = geo.specs(HEAD, K_COL)
    v_specs = geo.specs(HEAD, V_COL)
    stat = pl.BlockSpec((geo.TB, LANES), lambda i, h: (i, 0))
    return pl.pallas_call(
        body, name=f"attn_fwd_d{d}", grid=geo.grid,
        in_specs=[q_cur, k_specs[1], k_specs[0], k_specs[2], v_specs[1], v_specs[0], v_specs[2]],
        out_specs=[q_cur, stat],
        out_shape=[jax.ShapeDtypeStruct((S, ATTN_W), BF16), jax.ShapeDtypeStruct((S, LANES), F32)],
        scratch_shapes=(geo.scratch(geo.TB) + geo.scratch(geo.W) + geo.scratch(geo.W) + geo.scratch(geo.TB, F32)
                        + geo.scratch(geo.TB, F32)),
    )(qk, qk, qk, qk, proj, proj, proj)


def _attn_combine(outs, lses, g, tm=256):
    S = outs[0].shape[0]

    def body(o1, o2, o3, l1, l2, l3, g_ref, ao_ref, o_ref, lse_ref):
        a1, a2, a3 = l1[...], l2[...], l3[...]
        mx = jnp.maximum(jnp.maximum(a1, a2), a3)
        e1, e2, e3 = jnp.exp(a1 - mx), jnp.exp(a2 - mx), jnp.exp(a3 - mx)
        den = e1 + e2 + e3
        lse_ref[...] = mx + jnp.log(den)
        head_of_col = lax.broadcasted_iota(jnp.int32, (LANES, ATTN_W), 1) // HEAD
        spread = (lax.broadcasted_iota(jnp.int32, (LANES, ATTN_W), 0) == head_of_col).astype(F32)

        def wide(e):
            return lax.dot_general(e / den, spread, (NN, ((), ())), precision=lax.Precision.HIGHEST,
                                   preferred_element_type=F32)

        ov = wide(e1) * o1[...].astype(F32) + wide(e2) * o2[...].astype(F32) + wide(e3) * o3[...].astype(F32)
        o_ref[...] = ov
        r = lax.rsqrt(jnp.mean(ov * ov, axis=-1, keepdims=True) + EPS)
        ao_ref[...] = (ov * r * g_ref[...]).astype(BF16)

    blk = pl.BlockSpec((tm, ATTN_W), lambda i: (i, 0))
    ls = pl.BlockSpec((tm, LANES), lambda i: (i, 0))
    return pl.pallas_call(
        body, name="attn_combine", grid=(S // tm,),
        in_specs=[blk, blk, blk, ls, ls, ls, pl.BlockSpec((1, ATTN_W), lambda i: (0, 0))], out_specs=[blk, blk, ls],
        out_shape=[jax.ShapeDtypeStruct((S, D_MODEL), BF16), jax.ShapeDtypeStruct((S, ATTN_W), F32),
                   jax.ShapeDtypeStruct((S, LANES), F32)],
    )(*outs, *lses, g)


def _attn_norm_bwd(o, g, dao, tm=256):
    S = o.shape[0]

    def body(o_ref, g_ref, dao_ref, do_ref, dl_ref, dg_ref):
        i = pl.program_id(0)

        @pl.when(i == 0)
        def _():
            dg_ref[...] = jnp.zeros_like(dg_ref)

        ov = o_ref[...]
        r = lax.rsqrt(jnp.mean(ov * ov, axis=-1, keepdims=True) + EPS)
        ohat = ov * r
        dn = dao_ref[...].astype(F32)
        dg_ref[...] += jnp.sum(dn * ohat, axis=0, keepdims=True)
        t = dn * g_ref[...]
        do = r * (t - ohat * jnp.mean(t * ohat, axis=-1, keepdims=True))
        do_ref[...] = do.astype(BF16)
        prod = do * ov
        lane = lax.broadcasted_iota(jnp.int32, (tm, LANES), 1)
        tile = jnp.zeros((tm, LANES), F32)
        for h in range(N_HEADS):
            tile = jnp.where(lane == h, jnp.sum(prod[:, h * HEAD:(h + 1) * HEAD], axis=1, keepdims=True), tile)
        dl_ref[...] = tile

    blk = pl.BlockSpec((tm, ATTN_W), lambda i: (i, 0))
    vec = pl.BlockSpec((1, ATTN_W), lambda i: (0, 0))
    return pl.pallas_call(
        body, name="attn_norm_bwd", grid=(S // tm,),
        in_specs=[blk, vec, pl.BlockSpec((tm, ATTN_W), lambda i: (i, 0))],
        out_specs=[blk, pl.BlockSpec((tm, LANES), lambda i: (i, 0)), vec],
        out_shape=[jax.ShapeDtypeStruct((S, ATTN_W), BF16), jax.ShapeDtypeStruct((S, LANES), F32),
                   jax.ShapeDtypeStruct((1, ATTN_W), F32)],
    )(o, g, dao)


def _attn_bwd_dq(qk, proj, do, lse, delta, d):
    S = qk.shape[0]
    geo = _AttnGeo(S, d)
    scale = HEAD ** -0.5

    def body(q_ref, kp, kc, kn, vp, vc, vn, do_ref, lse_ref, dl_ref, dq_ref, *scratch):
        h = pl.program_id(1)
        refs = iter(scratch)
        q_p, k_p, v_p, do_p, lse_p, dl_p, dq_p = (geo.bind(refs) for _ in range(7))
        geo.fill(q_p[0], q_ref)
        geo.fill(do_p[0], do_ref)
        geo.fill(lse_p[0], lse_ref)
        geo.fill(dl_p[0], dl_ref)
        geo.fill_window(k_p[0], kp, kc, kn)
        geo.fill_window(v_p[0], vp, vc, vn)
        qs, ks, vs, dos = geo.spread(q_p), geo.spread(k_p), geo.spread(v_p), geo.spread(do_p)
        lses, dls = geo.spread(lse_p), geo.spread(dl_p)
        dqs = dq_p[1]
        band = geo.band()
        for sub in range(geo.n_sub):
            rq, rw = geo.rows(sub, SUB, geo.TB), geo.rows(sub, 2 * SUB, geo.W)
            q_r, k_r, v_r = qs[rq, :].astype(BF16), ks[rw, :].astype(BF16), vs[rw, :].astype(BF16)
            lse_c, dl_c = _lane_of(lses[rq, :], h), _lane_of(dls[rq, :], h)
            s = _dot(q_r, k_r, NT) * scale
            p = jnp.where(geo.mask(sub, band), jnp.exp(s - lse_c), 0.0)
            dp = _dot(dos[rq, :].astype(BF16), v_r, NT)
            ds = (p * (dp - dl_c) * scale).astype(BF16)
            dqs[rq, :] = _dot(ds, k_r)
        dq_ref[...] = geo.gather(dq_p)[...].astype(BF16)

    cur, _, _ = geo.specs(HEAD, 0)
    k_specs = geo.specs(HEAD, K_COL)
    v_specs = geo.specs(HEAD, V_COL)
    stat = pl.BlockSpec((geo.TB, LANES), lambda i, h: (i, 0))
    return pl.pallas_call(
        body, name=f"attn_bwd_dq_d{d}", grid=geo.grid,
        in_specs=[cur, k_specs[1], k_specs[0], k_specs[2], v_specs[1], v_specs[0], v_specs[2], cur, stat, stat],
        out_specs=cur, out_shape=jax.ShapeDtypeStruct((S, ATTN_W), BF16),
        scratch_shapes=(geo.scratch(geo.TB) + geo.scratch(geo.W) + geo.scratch(geo.W) + geo.scratch(geo.TB)
                        + geo.scratch(geo.TB, F32) + geo.scratch(geo.TB, F32) + geo.scratch(geo.TB, F32)),
    )(qk, qk, qk, qk, proj, proj, proj, do, lse, delta)


def _attn_bwd_dkv(qk, proj, do, lse, delta, d):
    S = qk.shape[0]
    geo = _AttnGeo(S, d)
    scale = HEAD ** -0.5

    def body(k_ref, v_ref, qp, qc, qn, dop, doc, don, lp, lc, ln, dlp, dlc, dln, dk_ref, dv_ref, *scratch):
        h = pl.program_id(1)
        refs = iter(scratch)
        k_p, v_p, q_p, do_p, lw_p, dlw_p, dk_p, dv_p = (geo.bind(refs) for _ in range(8))
        geo.fill(k_p[0], k_ref)
        geo.fill(v_p[0], v_ref)
        geo.fill_window(q_p[0], qp, qc, qn)
        geo.fill_window(do_p[0], dop, doc, don)
        geo.fill_window(lw_p[0], lp, lc, ln)
        geo.fill_window(dlw_p[0], dlp, dlc, dln)
        ks, vs, qs, dos = geo.spread(k_p), geo.spread(v_p), geo.spread(q_p), geo.spread(do_p)
        lws, dlws = geo.spread(lw_p), geo.spread(dlw_p)
        dks, dvs = dk_p[1], dv_p[1]
        head = lax.broadcasted_iota(jnp.int32, (LANES, 2 * SUB), 0)
        band = geo.band()
        for sub in range(geo.n_sub):
            rq, rw = geo.rows(sub, SUB, geo.TB), geo.rows(sub, 2 * SUB, geo.W)
            k_r, v_r = ks[rq, :].astype(BF16), vs[rq, :].astype(BF16)
            q_w, do_w = qs[rw, :].astype(BF16), dos[rw, :].astype(BF16)
            lse_row = jnp.sum(jnp.where(head == h, lws[rw, :].T, 0.0), axis=0, keepdims=True)
            dl_row = jnp.sum(jnp.where(head == h, dlws[rw, :].T, 0.0), axis=0, keepdims=True)
            st = _dot(k_r, q_w, NT) * scale
            pt = jnp.where(geo.mask(sub, band), jnp.exp(st - lse_row), 0.0)
            dvs[rq, :] = _dot(pt.astype(BF16), do_w)
            dpt = _dot(v_r, do_w, NT)
            dst = (pt * (dpt - dl_row) * scale).astype(BF16)
            dks[rq, :] = _dot(dst, q_w)
        dk_ref[...] = geo.gather(dk_p)[...].astype(BF16)
        dv_ref[...] = geo.gather(dv_p)[...].astype(BF16)

    q_specs = geo.specs(HEAD, Q_COL)
    k_cur, _, _ = geo.specs(HEAD, K_COL)
    v_cur, _, _ = geo.specs(HEAD, V_COL)
    do_specs = geo.specs(HEAD, 0)
    st_specs = geo.specs(LANES, 0, per_head=False)
    cur = do_specs[0]
    return pl.pallas_call(
        body, name=f"attn_bwd_dkv_d{d}", grid=geo.grid,
        in_specs=[k_cur, v_cur, q_specs[1], q_specs[0], q_specs[2], do_specs[1], do_specs[0], do_specs[2],
                  st_specs[1], st_specs[0], st_specs[2], st_specs[1], st_specs[0], st_specs[2]],
        out_specs=[cur, cur],
        out_shape=[jax.ShapeDtypeStruct((S, ATTN_W), BF16), jax.ShapeDtypeStruct((S, ATTN_W), BF16)],
        scratch_shapes=(geo.scratch(geo.TB) + geo.scratch(geo.TB) + geo.scratch(geo.W) + geo.scratch(geo.W)
                        + geo.scratch(geo.W, F32) + geo.scratch(geo.W, F32) + geo.scratch(geo.TB, F32)
                        + geo.scratch(geo.TB, F32)),
    )(qk, proj, qk, qk, qk, do, do, do, lse, lse, lse, delta, delta, delta)


def _cumsum_rows(x, reverse):
    n = x.shape[0]
    row = lax.broadcasted_iota(jnp.int32, x.shape, 0)
    s = 1
    while s < n:
        if reverse:
            x = x + jnp.where(row < n - s, pltpu.roll(x, n - s, axis=0), 0.0)
        else:
            x = x + jnp.where(row >= s, pltpu.roll(x, s, axis=0), 0.0)
        s *= 2
    return x


GLA_GROUP = 4


def _gla_rows(cc):
    return slice(cc * CHUNK, (cc + 1) * CHUNK)


def _gla_chunk_terms(q_ref, k_ref, v_ref, g_ref, h, reverse, rows):
    ksl = slice(h * GLA_DK, (h + 1) * GLA_DK)
    q = q_ref[rows, ksl].astype(F32) * (GLA_DK ** -0.5)
    k = k_ref[rows, ksl].astype(F32)
    v = v_ref[rows, h * GLA_DV:(h + 1) * GLA_DV]
    b = _cumsum_rows(g_ref[rows, ksl], reverse)
    r_ref = CHUNK // 2 if reverse else CHUNK // 2 - 1
    r_last = 0 if reverse else CHUNK - 1
    b_ref, b_last = b[r_ref:r_ref + 1, :], b[r_last:r_last + 1, :]
    ii = lax.broadcasted_iota(jnp.int32, (CHUNK, CHUNK), 0)
    jj = lax.broadcasted_iota(jnp.int32, (CHUNK, CHUNK), 1)
    causal = (jj >= ii) if reverse else (jj <= ii)
    e_q, e_k = jnp.exp(b - b_ref), jnp.exp(b_ref - b)
    e_in, e_st = jnp.exp(b), jnp.exp(b_last - b)
    return dict(q=q, k=k, v=v, b=b, causal=causal, e_q=e_q, e_k=e_k, e_in=e_in, e_st=e_st, dec=jnp.exp(b_last),
                qe=q * e_q, ke=k * e_k, q_in=q * e_in, k_st=k * e_st, r_ref=r_ref, r_last=r_last)


def _gla_specs(order):
    rows = GLA_GROUP * CHUNK
    q = pl.BlockSpec((rows, GLA_K), lambda c: (order(c), 3 * ATTN_W // GLA_K))
    k = pl.BlockSpec((rows, GLA_K), lambda c: (order(c), 3 * ATTN_W // GLA_K + 1))
    v = pl.BlockSpec((rows, GLA_V), lambda c: (order(c), (3 * ATTN_W + 2 * GLA_K) // GLA_V))
    return q, k, v


def _gla_fwd(proj, gates, reverse, o_prev=None):
    S = proj.shape[0]
    n = S // CHUNK
    nb = n // GLA_GROUP
    rows = GLA_GROUP * CHUNK
    order = (lambda c: nb - 1 - c) if reverse else (lambda c: c)
    seq = list(range(GLA_GROUP))[::-1] if reverse else list(range(GLA_GROUP))
    gcol = 1 if reverse else 0

    def body(*refs):
        if o_prev is None:
            q_ref, k_ref, v_ref, g_ref, o_ref, st_ref, state = refs
        else:
            q_ref, k_ref, v_ref, g_ref, op_ref, o_ref, st_ref, state = refs
        c = pl.program_id(0)

        @pl.when(c == 0)
        def _():
            state[...] = jnp.zeros_like(state)

        for h in range(GLA_HEADS):
            vsl = slice(h * GLA_DV, (h + 1) * GLA_DV)
            st = state[h]
            for cc in seq:
                rs = _gla_rows(cc)
                t = _gla_chunk_terms(q_ref, k_ref, v_ref, g_ref, h, reverse, rs)
                a = jnp.where(t["causal"], _dot(t["qe"].astype(BF16), t["ke"].astype(BF16), NT), 0.0)
                o = _dot(a.astype(BF16), t["v"])
                st_b = st.astype(BF16)
                st_ref[cc, h] = st_b
                o = o + _dot(t["q_in"].astype(BF16), st_b, NT)
                st = st * t["dec"] + _dot(t["v"], t["k_st"].astype(BF16), TN)
                if o_prev is not None:
                    o = o + op_ref[rs, vsl]
                o_ref[rs, vsl] = o
            state[h] = st

    q_spec, k_spec, v_spec = _gla_specs(order)
    o_spec = pl.BlockSpec((rows, GLA_V), lambda c: (order(c), 0))
    in_specs = [q_spec, k_spec, v_spec, pl.BlockSpec((rows, GLA_K), lambda c: (order(c), gcol))]
    operands = [proj, proj, proj, gates]
    if o_prev is not None:
        in_specs.append(o_spec)
        operands.append(o_prev)
    return pl.pallas_call(
        body, name="gla_fwd_rev" if reverse else "gla_fwd", grid=(nb,), in_specs=in_specs,
        out_specs=[o_spec, pl.BlockSpec((GLA_GROUP, GLA_HEADS, GLA_DV, GLA_DK), lambda c: (order(c), 0, 0, 0))],
        out_shape=[jax.ShapeDtypeStruct((S, GLA_V), F32), jax.ShapeDtypeStruct((n, GLA_HEADS, GLA_DV, GLA_DK), BF16)],
        scratch_shapes=[pltpu.VMEM((GLA_HEADS, GLA_DV, GLA_DK), F32)],
    )(*operands)


def _gla_bwd(proj, gates, states, do, reverse, prev=None):
    S = proj.shape[0]
    n = S // CHUNK
    nb = n // GLA_GROUP
    rows = GLA_GROUP * CHUNK
    order = (lambda c: c) if reverse else (lambda c: nb - 1 - c)
    seq = list(range(GLA_GROUP)) if reverse else list(range(GLA_GROUP))[::-1]
    gcol = 1 if reverse else 0
    out_dt = F32 if prev is None else BF16

    def body(*refs):
        if prev is None:
            q_ref, k_ref, v_ref, g_ref, st_ref, do_ref, dq_ref, dk_ref, dv_ref, dg_ref, dstate = refs
        else:
            q_ref, k_ref, v_ref, g_ref, st_ref, do_ref, pq, pk, pv, dq_ref, dk_ref, dv_ref, dg_ref, dstate = refs
        c = pl.program_id(0)

        @pl.when(c == 0)
        def _():
            dstate[...] = jnp.zeros_like(dstate)

        row = lax.broadcasted_iota(jnp.int32, (CHUNK, GLA_DK), 0)
        for h in range(GLA_HEADS):
            ksl = slice(h * GLA_DK, (h + 1) * GLA_DK)
            vsl = slice(h * GLA_DV, (h + 1) * GLA_DV)
            dst = dstate[h]
            for cc in seq:
                rs = _gla_rows(cc)
                t = _gla_chunk_terms(q_ref, k_ref, v_ref, g_ref, h, reverse, rs)
                v = t["v"]
                dob = do_ref[rs, vsl].astype(BF16)
                st_b = st_ref[cc, h]
                dst_b = dst.astype(BF16)
                qe_b, ke_b = t["qe"].astype(BF16), t["ke"].astype(BF16)
                q_in_b, k_st_b = t["q_in"].astype(BF16), t["k_st"].astype(BF16)
                a = jnp.where(t["causal"], _dot(qe_b, ke_b, NT), 0.0)
                da = jnp.where(t["causal"], _dot(dob, v, NT), 0.0).astype(BF16)
                dv = _dot(a.astype(BF16), dob, TN) + _dot(k_st_b, dst_b, NT)
                dqe = _dot(da, ke_b)
                dke = _dot(da, qe_b, TN)
                dq_in = _dot(dob, st_b)
                dk_st = _dot(v, dst_b)
                ddec = jnp.sum(dst * st_b.astype(F32), axis=0, keepdims=True)
                dst = _dot(dob, q_in_b, TN) + dst * t["dec"]
                dq = (dqe * t["e_q"] + dq_in * t["e_in"]) * (GLA_DK ** -0.5)
                dk = dke * t["e_k"] + dk_st * t["e_st"]
                w_q, w_k = dqe * t["qe"], dke * t["ke"]
                w_st = dk_st * t["k_st"]
                db = w_q - w_k + dq_in * t["q_in"] - w_st
                db_ref = jnp.sum(w_k - w_q, axis=0, keepdims=True)
                db_last = jnp.sum(w_st, axis=0, keepdims=True) + ddec * t["dec"]
                db = db + jnp.where(row == t["r_ref"], db_ref, 0.0) + jnp.where(row == t["r_last"], db_last, 0.0)
                dg_ref[rs, ksl] = _cumsum_rows(db, not reverse)
                if prev is not None:
                    dq, dk, dv = dq + pq[rs, ksl], dk + pk[rs, ksl], dv + pv[rs, vsl]
                dq_ref[rs, ksl] = dq.astype(out_dt)
                dk_ref[rs, ksl] = dk.astype(out_dt)
                dv_ref[rs, vsl] = dv.astype(out_dt)
            dstate[h] = dst

    q_spec, k_spec, v_spec = _gla_specs(order)
    kk = pl.BlockSpec((rows, GLA_K), lambda c: (order(c), 0))
    vv = pl.BlockSpec((rows, GLA_V), lambda c: (order(c), 0))
    in_specs = [q_spec, k_spec, v_spec, pl.BlockSpec((rows, GLA_K), lambda c: (order(c), gcol)),
                pl.BlockSpec((GLA_GROUP, GLA_HEADS, GLA_DV, GLA_DK), lambda c: (order(c), 0, 0, 0)), vv]
    operands = [proj, proj, proj, gates, states, do]
    if prev is not None:
        in_specs += [kk, kk, vv]
        operands += list(prev)
    return pl.pallas_call(
        body, name="gla_bwd_rev" if reverse else "gla_bwd", grid=(nb,), in_specs=in_specs, out_specs=[kk, kk, vv, kk],
        out_shape=[jax.ShapeDtypeStruct((S, GLA_K), out_dt), jax.ShapeDtypeStruct((S, GLA_K), out_dt),
                   jax.ShapeDtypeStruct((S, GLA_V), out_dt), jax.ShapeDtypeStruct((S, GLA_K), F32)],
        scratch_shapes=[pltpu.VMEM((GLA_HEADS, GLA_DV, GLA_DK), F32)],
    )(*operands)


def _gates_fwd(z, wg, bias, tm=512):
    S = z.shape[0]
    W = 2 * GLA_K

    def body(z_ref, w_ref, b_ref, o_ref):
        zg = _dot(z_ref[...], w_ref[...]) + b_ref[...]
        o_ref[...] = (jnp.minimum(zg, 0.0) - jnp.log(1.0 + jnp.exp(-jnp.abs(zg)))) * (1.0 / GATE_NORM)

    return pl.pallas_call(
        body, name="gates_fwd", grid=(S // tm,),
        in_specs=[pl.BlockSpec((tm, Z_W), lambda i: (i, 0)), pl.BlockSpec((Z_W, W), lambda i: (0, 0)),
                  pl.BlockSpec((1, W), lambda i: (0, 0))],
        out_specs=pl.BlockSpec((tm, W), lambda i: (i, 0)), out_shape=jax.ShapeDtypeStruct((S, W), F32),
    )(z, wg, bias)


def _gates_bwd(z, wg, bias, dg_f, dg_b, tm=512):
    S = z.shape[0]
    W = 2 * GLA_K

    def body(z_ref, w_ref, b_ref, dgf_ref, dgb_ref, dz_ref, dw_ref, db_ref):
        i = pl.program_id(0)

        @pl.when(i == 0)
        def _():
            dw_ref[...] = jnp.zeros_like(dw_ref)
            db_ref[...] = jnp.zeros_like(db_ref)

        zv = z_ref[...]
        zg = _dot(zv, w_ref[...]) + b_ref[...]
        dg = jnp.concatenate([dgf_ref[...], dgb_ref[...]], axis=1)
        dzg = dg * (1.0 / GATE_NORM) * _sigmoid(-zg)
        db_ref[...] += jnp.sum(dzg, axis=0, keepdims=True)
        dzg_b = dzg.astype(BF16)
        dw_ref[...] += _dot(zv, dzg_b, TN)
        dz_ref[...] = _dot(dzg_b, w_ref[...], NT).astype(BF16)

    half = pl.BlockSpec((tm, GLA_K), lambda i: (i, 0))
    return pl.pallas_call(
        body, name="gates_bwd", grid=(S // tm,),
        in_specs=[pl.BlockSpec((tm, Z_W), lambda i: (i, 0)), pl.BlockSpec((Z_W, W), lambda i: (0, 0)),
                  pl.BlockSpec((1, W), lambda i: (0, 0)), half, half],
        out_specs=[pl.BlockSpec((tm, Z_W), lambda i: (i, 0)), pl.BlockSpec((Z_W, W), lambda i: (0, 0)),
                   pl.BlockSpec((1, W), lambda i: (0, 0))],
        out_shape=[jax.ShapeDtypeStruct((S, Z_W), BF16), jax.ShapeDtypeStruct((Z_W, W), F32),
                   jax.ShapeDtypeStruct((1, W), F32)],
    )(z, wg, bias, dg_f, dg_b)


def _gla_out_fwd(o, proj, g, cat, tm=512):
    S = o.shape[0]

    def body(o_ref, gr_ref, g_ref, cat_ref, out_ref):
        gn = g_ref[...]
        for h in range(GLA_HEADS):
            sl = slice(h * GLA_DV, (h + 1) * GLA_DV)
            ov = o_ref[:, sl]
            r = lax.rsqrt(jnp.mean(ov * ov, axis=-1, keepdims=True) + EPS)
            gr = gr_ref[:, sl].astype(F32)
            out_ref[:, sl] = (ov * r * gn * (gr * _sigmoid(gr))).astype(BF16)

    blk = pl.BlockSpec((tm, GLA_V), lambda i: (i, 0))
    return pl.pallas_call(
        body, name="gla_out_fwd", grid=(S // tm,),
        in_specs=[blk, pl.BlockSpec((tm, GLA_V), lambda i: (i, (3 * ATTN_W + 2 * GLA_K + GLA_V) // GLA_V)),
                  pl.BlockSpec((1, GLA_DV), lambda i: (0, 0)), ANY],
        out_specs=pl.BlockSpec((tm, GLA_V), lambda i: (i, 1)), out_shape=jax.ShapeDtypeStruct((S, D_MODEL), BF16),
        input_output_aliases={3: 0},
    )(o, proj, g, cat)


def _gla_out_bwd(o, proj, g, dcat, dproj, tm=512):
    S = o.shape[0]

    def body(o_ref, gr_ref, g_ref, dgo_ref, dproj_ref, do_ref, dgr_ref, dg_ref):
        i = pl.program_id(0)

        @pl.when(i == 0)
        def _():
            dg_ref[...] = jnp.zeros_like(dg_ref)

        gn = g_ref[...]
        dg_acc = jnp.zeros((1, GLA_DV), F32)
        for h in range(GLA_HEADS):
            sl = slice(h * GLA_DV, (h + 1) * GLA_DV)
            ov = o_ref[:, sl]
            r = lax.rsqrt(jnp.mean(ov * ov, axis=-1, keepdims=True) + EPS)
            yhat = ov * r
            gr = gr_ref[:, sl].astype(F32)
            sg = _sigmoid(gr)
            dgo = dgo_ref[:, sl].astype(F32)
            dgr_ref[:, sl] = (dgo * (yhat * gn) * (sg * (1.0 + gr * (1.0 - sg)))).astype(BF16)
            dy = dgo * (gr * sg)
            dg_acc = dg_acc + jnp.sum(dy * yhat, axis=0, keepdims=True)
            t = dy * gn
            do_ref[:, sl] = r * (t - yhat * jnp.mean(t * yhat, axis=-1, keepdims=True))
        dg_ref[...] += dg_acc

    blk = pl.BlockSpec((tm, GLA_V), lambda i: (i, 0))
    vec = pl.BlockSpec((1, GLA_DV), lambda i: (0, 0))
    return pl.pallas_call(
        body, name="gla_out_bwd", grid=(S // tm,),
        in_specs=[blk, pl.BlockSpec((tm, GLA_V), lambda i: (i, (3 * ATTN_W + 2 * GLA_K + GLA_V) // GLA_V)), vec,
                  pl.BlockSpec((tm, GLA_V), lambda i: (i, 1)), ANY],
        out_specs=[blk, pl.BlockSpec((tm, GLA_V), lambda i: (i, (3 * ATTN_W + 2 * GLA_K + GLA_V) // GLA_V)), vec],
        out_shape=[jax.ShapeDtypeStruct((S, GLA_V), F32), jax.ShapeDtypeStruct((S, IN_MAIN), BF16),
                   jax.ShapeDtypeStruct((1, GLA_DV), F32)],
        input_output_aliases={4: 1},
    )(o, proj, g, dcat, dproj)


HALO = 16


def _halo_specs(tm, tn, S):
    cur = pl.BlockSpec((tm, tn), lambda j, i: (i, j))
    prev = pl.BlockSpec((HALO, tn), lambda j, i: (jnp.maximum(i * (tm // HALO) - 1, 0), j))
    nxt = pl.BlockSpec((HALO, tn), lambda j, i: (jnp.minimum((i + 1) * (tm // HALO), S // HALO - 1), j))
    return cur, prev, nxt


def _shifted(c_ref, p_ref, n_ref, n_blocks):
    i = pl.program_id(1)
    x = c_ref[...].astype(F32)
    tm = x.shape[0]
    row = lax.broadcasted_iota(jnp.int32, x.shape, 0)
    before = p_ref[HALO - 1:HALO, :].astype(F32) * (i > 0).astype(F32)
    after = n_ref[0:1, :].astype(F32) * (i < n_blocks - 1).astype(F32)
    x_m1 = jnp.where(row == 0, before, pltpu.roll(x, 1, axis=0))
    x_p1 = jnp.where(row == tm - 1, after, pltpu.roll(x, tm - 1, axis=0))
    return x, x_m1, x_p1


def _glu_fwd(gp, up, cw, cb, tm=512, tn=1408):
    S = gp.shape[0]
    nb = S // tm

    def body(c_ref, p_ref, n_ref, up_ref, w_ref, b_ref, o_ref):
        x, x_m1, x_p1 = _shifted(c_ref, p_ref, n_ref, nb)
        w = w_ref[...]
        gate = w[0:1, :] * x_m1 + w[1:2, :] * x + w[2:3, :] * x_p1 + b_ref[...]
        o_ref[...] = (gate * _sigmoid(gate) * up_ref[...].astype(F32)).astype(BF16)

    cur, prev, nxt = _halo_specs(tm, tn, S)
    return pl.pallas_call(
        body, name="glu_fwd", grid=(D_FF // tn, nb),
        in_specs=[cur, prev, nxt, cur, pl.BlockSpec((3, tn), lambda j, i: (0, j)), pl.BlockSpec((1, tn), lambda j, i: (0, j))],
        out_specs=cur, out_shape=jax.ShapeDtypeStruct((S, D_FF), BF16),
    )(gp, gp, gp, up, cw, cb)


def _glu_bwd(gp, up, dact, cw, cb, tm=512, tn=1408):
    S = gp.shape[0]
    nb = S // tm

    def body(c_ref, p_ref, n_ref, up_ref, upp_ref, upn_ref, da_ref, dap_ref, dan_ref, w_ref, b_ref,
             dup_ref, dgp_ref, dw_ref, db_ref):
        i = pl.program_id(1)

        @pl.when(i == 0)
        def _():
            dw_ref[...] = jnp.zeros_like(dw_ref)
            db_ref[...] = jnp.zeros_like(db_ref)

        x, x_m1, x_p1 = _shifted(c_ref, p_ref, n_ref, nb)
        w = w_ref[...]
        w0, w1, w2, b = w[0:1, :], w[1:2, :], w[2:3, :], b_ref[...]

        def d_gate(gate, da, upv):
            sg = _sigmoid(gate)
            return sg, da * upv * (sg * (1.0 + gate * (1.0 - sg)))

        gate = w0 * x_m1 + w1 * x + w2 * x_p1 + b
        da = da_ref[...].astype(F32)
        sg, dgate = d_gate(gate, da, up_ref[...].astype(F32))
        dup_ref[...] = (da * (gate * sg)).astype(BF16)
        db_ref[...] += jnp.sum(dgate, axis=0, keepdims=True)
        dw_ref[...] += jnp.concatenate(
            [jnp.sum(dgate * x_m1, axis=0, keepdims=True), jnp.sum(dgate * x, axis=0, keepdims=True),
             jnp.sum(dgate * x_p1, axis=0, keepdims=True)], axis=0)

        pv, nv = p_ref[...].astype(F32), n_ref[...].astype(F32)
        gate_before = w0 * pv[HALO - 2:HALO - 1, :] + w1 * pv[HALO - 1:HALO, :] + w2 * x[0:1, :] + b
        _, dgate_before = d_gate(gate_before, dap_ref[...].astype(F32)[HALO - 1:HALO, :], upp_ref[...].astype(F32)[HALO - 1:HALO, :])
        gate_after = w0 * x[tm - 1:tm, :] + w1 * nv[0:1, :] + w2 * nv[1:2, :] + b
        _, dgate_after = d_gate(gate_after, dan_ref[...].astype(F32)[0:1, :], upn_ref[...].astype(F32)[0:1, :])
        dgate_before = dgate_before * (i > 0).astype(F32)
        dgate_after = dgate_after * (i < nb - 1).astype(F32)
        row = lax.broadcasted_iota(jnp.int32, dgate.shape, 0)
        dg_m1 = jnp.where(row == 0, dgate_before, pltpu.roll(dgate, 1, axis=0))
        dg_p1 = jnp.where(row == tm - 1, dgate_after, pltpu.roll(dgate, tm - 1, axis=0))
        dgp_ref[...] = (w0 * dg_p1 + w1 * dgate + w2 * dg_m1).astype(BF16)

    cur, prev, nxt = _halo_specs(tm, tn, S)
    w_spec = pl.BlockSpec((3, tn), lambda j, i: (0, j))
    b_spec = pl.BlockSpec((1, tn), lambda j, i: (0, j))
    return pl.pallas_call(
        body, name="glu_bwd", grid=(D_FF // tn, nb),
        in_specs=[cur, prev, nxt, cur, prev, nxt, cur, prev, nxt, w_spec, b_spec],
        out_specs=[cur, cur, w_spec, b_spec],
        out_shape=[jax.ShapeDtypeStruct((S, D_FF), BF16), jax.ShapeDtypeStruct((S, D_FF), BF16),
                   jax.ShapeDtypeStruct((3, D_FF), F32), jax.ShapeDtypeStruct((1, D_FF), F32)],
    )(gp, gp, gp, up, up, up, dact, dact, dact, cw, cb)


def _local_step(x, target, norm1_g, w_in_t, wg, gate_bias, gla_norm_g, attn_norm_g, w_out, norm2_g,
                w_gate4, w_up4, conv_w, conv_b, w_down, final_norm_g, on_grad=lambda event, arrays: ()):
    S = x.shape[0]
    tabs = _rope_tables(S)

    n1 = _rms_fwd("rms1_fwd", x, norm1_g)
    z_block = IN_MAIN // Z_W
    proj = _mm_nt("in_proj", n1, w_in_t, 1024, 1536, BF16, n_out=IN_MAIN)
    z = _matmul(
        "in_proj_z",
        [(n1, pl.BlockSpec((1024, D_MODEL), lambda i: (i, 0)), w_in_t, pl.BlockSpec((Z_W, D_MODEL), lambda i: (z_block, 0)), NT)],
        (S // 1024,), jax.ShapeDtypeStruct((S, Z_W), BF16), pl.BlockSpec((1024, Z_W), lambda i: (i, 0)), 1)
    qk = _rope_fwd(proj, tabs)
    branch = [_attn_fwd(qk, proj, d) for d in DILATIONS]
    ao, o_attn, lse = _attn_combine([b[0] for b in branch], [b[1] for b in branch], attn_norm_g)
    gates = _gates_fwd(z, wg, gate_bias)
    o_f, st_f = _gla_fwd(proj, gates, False)
    o_gla, st_b = _gla_fwd(proj, gates, True, o_prev=o_f)
    cat = _gla_out_fwd(o_gla, proj, gla_norm_g, ao)
    h1 = _mm_nn("out_proj", cat, w_out, 1024, 1024, F32, res=x)
    n2 = _rms_fwd("rms2_fwd", h1, norm2_g)
    gp = _mm_nn_sharded("ffn_gate", n2, w_gate4, 1024, BF16)
    up = _mm_nn_sharded("ffn_up", n2, w_up4, 1024, BF16)
    act = _glu_fwd(gp, up, conv_w, conv_b)
    tk = D_FF // N_CHIPS
    h2 = _matmul(
        "ffn_down",
        [(act, pl.BlockSpec((1024, tk), lambda i, j, k: (i, k)), w_down, pl.BlockSpec((tk, 1024), lambda i, j, k: (k, j)), NN)],
        (S // 1024, D_MODEL // 1024, N_CHIPS), jax.ShapeDtypeStruct((S, D_MODEL), F32),
        pl.BlockSpec((1024, 1024), lambda i, j, k: (i, j)), N_CHIPS,
        res=(h1, pl.BlockSpec((1024, 1024), lambda i, j, k: (i, j))))
    loss_row, d_final_g, dh2, dh2_b = _final_loss(h2, final_norm_g.reshape(1, D_MODEL), target)

    dact = _mm_nt("ffn_down_bwd", dh2_b, w_down, 1024, tk, BF16)
    dup, dgp, d_conv_w, d_conv_b = _glu_bwd(gp, up, dact, conv_w, conv_b)
    d_w_down = _mm_tn("ffn_down_wgrad", act, dh2_b, tk, D_MODEL, 1024, BF16)
    on_grad("w_down", dict(w_down=d_w_down))
    dgp = _after(dgp, d_w_down)
    d_w_gate4 = _mm_tn("ffn_gate_wgrad", n2, dgp, D_MODEL, tk, 1024, BF16, out3=tk)
    dup = _after(dup, d_w_gate4)
    d_w_up4 = _mm_tn("ffn_up_wgrad", n2, dup, D_MODEL, tk, 1024, BF16, out3=tk)
    held = on_grad("w_gate_w_up", dict(w_gate=d_w_gate4, w_up=d_w_up4))
    dgp = _after(dgp, d_w_up4, *held)
    dn2 = _matmul(
        "ffn_in_bwd",
        [(dgp, pl.BlockSpec((1024, tk), lambda i, j, k: (i, k)), w_gate4, pl.BlockSpec((None, 1024, tk), lambda i, j, k: (k, j, 0)), NT),
         (dup, pl.BlockSpec((1024, tk), lambda i, j, k: (i, k)), w_up4, pl.BlockSpec((None, 1024, tk), lambda i, j, k: (k, j, 0)), NT)],
        (S // 1024, D_MODEL // 1024, N_CHIPS), jax.ShapeDtypeStruct((S, D_MODEL), F32),
        pl.BlockSpec((1024, 1024), lambda i, j, k: (i, j)), N_CHIPS)
    dh1, dh1_b, d_norm2_g = _rms_bwd("rms2_bwd", h1, norm2_g, dn2, dh2)

    d_w_out = _mm_tn("out_proj_wgrad", cat, dh1_b, D_MODEL, 1024, 1024, BF16)
    held = on_grad("w_out", dict(w_out=d_w_out))
    dcat = _mm_nt("out_proj_bwd", _after(dh1_b, d_w_out, *held), w_out, 1024, 1024, BF16)
    do_attn, delta, d_attn_norm_g = _attn_norm_bwd(o_attn, attn_norm_g, dcat)
    dqs, dks, dvs = [], [], []
    for d in DILATIONS:
        dqs.append(_attn_bwd_dq(qk, proj, do_attn, lse, delta, d))
        dk, dv = _attn_bwd_dkv(qk, proj, do_attn, lse, delta, d)
        dks.append(dk)
        dvs.append(dv)
    dproj = _attn_grad_merge(dqs, dks, dvs, tabs)
    held = on_grad("mid", dict(anchor=dproj))
    do_gla, dproj, d_gla_norm_g = _gla_out_bwd(o_gla, proj, gla_norm_g, _after(dcat, *held), dproj)
    dq_f, dk_f, dv_f, dg_f = _gla_bwd(proj, gates, st_f, do_gla, False)
    dgq, dgk, dgv, dg_b = _gla_bwd(proj, gates, st_b, do_gla, True, prev=(dq_f, dk_f, dv_f))
    dz, d_wg, d_gate_bias = _gates_bwd(z, wg, gate_bias, dg_f, dg_b)
    dproj = lax.dynamic_update_slice(dproj, jnp.concatenate([dgq, dgk, dgv], axis=1), (0, 3 * ATTN_W))
    d_w_in_t = _mm_tn("in_proj_wgrad", dproj, n1, 1536, D_MODEL, 1024, BF16, rows_out=IN_W)
    n_tok = S // 1024
    d_w_in_t = _matmul(
        "in_proj_z_wgrad",
        [(dz, pl.BlockSpec((1024, Z_W), lambda i, j, k: (k, 0)), n1, pl.BlockSpec((1024, D_MODEL), lambda i, j, k: (k, 0)), TN)],
        (1, 1, n_tok), jax.ShapeDtypeStruct((IN_W, D_MODEL), BF16), pl.BlockSpec((Z_W, D_MODEL), lambda i, j, k: (z_block, 0)),
        n_tok, into=d_w_in_t)
    held = on_grad("w_in", dict(w_in_t=d_w_in_t))
    tkm = IN_MAIN // 4
    half = S // 2048

    def in_proj_bwd(name, first, a, into):
        return _matmul(
            name,
            [(a, pl.BlockSpec((1024, tkm), lambda i, j, k: (i + first, k)), w_in_t, pl.BlockSpec((tkm, 1024), lambda i, j, k: (k, j)), NN)],
            (half, D_MODEL // 1024, 4), jax.ShapeDtypeStruct((S, D_MODEL), F32),
            pl.BlockSpec((1024, 1024), lambda i, j, k: (i + first, j)), 4, into=into,
            first=(dz, pl.BlockSpec((1024, Z_W), lambda i, j, k: (i + first, 0)),
                   w_in_t, pl.BlockSpec((Z_W, 1024), lambda i, j, k: (z_block, j)), NN))

    dproj = _after(dproj, d_w_in_t, *held)
    dn1 = in_proj_bwd("in_proj_bwd_a", 0, dproj, None)
    held = on_grad("last", dict(last=dn1))
    dn1 = in_proj_bwd("in_proj_bwd_b", half, dproj, _after(dn1, *held))
    grad_x, _, d_norm1_g = _rms_bwd("rms1_bwd", x, norm1_g, dn1, dh1)

    big = dict(w_in_t=d_w_in_t, w_out=d_w_out, w_gate4=d_w_gate4, w_up4=d_w_up4, w_down=d_w_down)
    small = dict(loss=loss_row, norm1_g=d_norm1_g, wg=d_wg, gate_bias=d_gate_bias, gla_norm_g=d_gla_norm_g,
                 attn_norm_g=d_attn_norm_g, norm2_g=d_norm2_g, conv_w=d_conv_w, conv_b=d_conv_b, final_norm_g=d_final_g)
    return grad_x, big, small


def _position():
    return lax.axis_index("x"), lax.axis_index("y"), lax.axis_index("c")


def _other_chips(x, y):
    return [(1 - x, y), (x, 1 - y), (1 - x, 1 - y)]


def _gather_chips(name, shards):
    n = len(shards)

    def body(*refs):
        ins, outs = refs[:n], refs[n:2 * n]
        send, recv, loc = refs[2 * n:]
        x, y, c = _position()
        me = 2 * x + y
        chips = _other_chips(x, y)
        started = []
        for w in range(n):
            own = pltpu.make_async_copy(ins[w], outs[w].at[me], loc.at[w])
            own.start()
            started.append(own)
        sends = []
        for w in range(n):
            for j, (px, py) in enumerate(chips):
                cp = pltpu.make_async_remote_copy(ins[w], outs[w].at[me], send.at[3 * w + j], recv.at[3 * w + j],
                                                  device_id=(px, py, c), device_id_type=MESH)
                cp.start()
                sends.append(cp)
        for w in range(n):
            for j, (px, py) in enumerate(chips):
                pltpu.make_async_remote_copy(ins[w], outs[w].at[2 * px + py], send.at[3 * w + j], recv.at[3 * w + j],
                                             device_id=(px, py, c), device_id_type=MESH).wait_recv()
        for cp in sends:
            cp.wait_send()
        for own in started:
            own.wait()

    return pl.pallas_call(
        body, name=name, in_specs=[ANY] * n, out_specs=[ANY] * n,
        out_shape=[jax.ShapeDtypeStruct((N_CHIPS,) + s.shape, s.dtype) for s in shards],
        scratch_shapes=[pltpu.SemaphoreType.DMA((3 * n,)), pltpu.SemaphoreType.DMA((3 * n,)), pltpu.SemaphoreType.DMA((n,))],
    )(*shards)


def _gather_chips_async(name, shards, collective_id):
    n = len(shards)

    def body(*refs):
        ins, outs = refs[:n], refs[n:2 * n]
        send, recv, loc = refs[2 * n:]
        x, y, c = _position()
        me = 2 * x + y
        chips = _other_chips(x, y)
        barrier = pltpu.get_barrier_semaphore()
        for px, py in chips:
            pl.semaphore_signal(barrier, inc=1, device_id=(px, py, c), device_id_type=MESH)
        pl.semaphore_wait(barrier, len(chips))
        started = []
        for w in range(n):
            own = pltpu.make_async_copy(ins[w], outs[w].at[me], loc.at[w])
            own.start()
            started.append(own)
        sends = []
        for w in range(n):
            for j, (px, py) in enumerate(chips):
                cp = pltpu.make_async_remote_copy(ins[w], outs[w].at[me], send.at[3 * w + j], recv.at[3 * w + j],
                                                  device_id=(px, py, c), device_id_type=MESH)
                cp.start()
                sends.append(cp)
        for w in range(n):
            for j, (px, py) in enumerate(chips):
                pltpu.make_async_remote_copy(ins[w], outs[w].at[2 * px + py], send.at[3 * w + j], recv.at[3 * w + j],
                                             device_id=(px, py, c), device_id_type=MESH).wait_recv()
        for cp in sends:
            cp.wait_send()
        for own in started:
            own.wait()

    return pl.kernel(
        body, name=name, mesh=_sequencer(),
        out_type=[jax.ShapeDtypeStruct((N_CHIPS,) + s.shape, s.dtype) for s in shards],
        scratch_types=[pltpu.SemaphoreType.DMA((3 * n,)), pltpu.SemaphoreType.DMA((3 * n,)), pltpu.SemaphoreType.DMA((n,))],
        compiler_params=pltpu.CompilerParams(collective_id=collective_id),
    )(*shards)


def _gather_halves_async(name, small, shard, collective_id):
    half = shard.shape[1] // 2

    def body(small_ref, shard_ref, small_out, out, send, recv, loc):
        x, y, c = _position()
        me = 2 * x + y
        sibling = (x, y, 1 - c)
        chips = _other_chips(x, y)
        barrier = pltpu.get_barrier_semaphore()
        for px, py in chips:
            pl.semaphore_signal(barrier, inc=1, device_id=(px, py, c), device_id_type=MESH)
        pl.semaphore_signal(barrier, inc=1, device_id=sibling, device_id_type=MESH)
        pl.semaphore_wait(barrier, len(chips) + 1)
        mine = pl.ds(pl.multiple_of(c * half, LANES), half)
        theirs = pl.ds(pl.multiple_of((1 - c) * half, LANES), half)
        own = [pltpu.make_async_copy(small_ref, small_out.at[me], loc.at[0]),
               pltpu.make_async_copy(shard_ref, out.at[me], loc.at[1])]
        for cp in own:
            cp.start()
        sends = []
        for j, (px, py) in enumerate(chips):
            sends.append(pltpu.make_async_remote_copy(small_ref, small_out.at[me], send.at[j], recv.at[j],
                                                      device_id=(px, py, c), device_id_type=MESH))
            sends.append(pltpu.make_async_remote_copy(shard_ref.at[:, mine], out.at[me, :, mine], send.at[3 + j], recv.at[3 + j],
                                                      device_id=(px, py, c), device_id_type=MESH))
        for cp in sends:
            cp.start()
        passed = []
        for j, (px, py) in enumerate(chips):
            slot = 2 * px + py
            pltpu.make_async_remote_copy(shard_ref.at[:, mine], out.at[slot, :, mine], send.at[3 + j], recv.at[3 + j],
                                         device_id=(px, py, c), device_id_type=MESH).wait_recv()
            cp = pltpu.make_async_remote_copy(out.at[slot, :, mine], out.at[slot, :, mine], send.at[6 + j], recv.at[6 + j],
                                              device_id=sibling, device_id_type=MESH)
            cp.start()
            passed.append(cp)
        for j, (px, py) in enumerate(chips):
            slot = 2 * px + py
            pltpu.make_async_remote_copy(small_ref, small_out.at[slot], send.at[j], recv.at[j],
                                         device_id=(px, py, c), device_id_type=MESH).wait_recv()
            pltpu.make_async_remote_copy(out.at[slot, :, theirs], out.at[slot, :, theirs], send.at[6 + j], recv.at[6 + j],
                                         device_id=sibling, device_id_type=MESH).wait_recv()
        for cp in sends + passed:
            cp.wait_send()
        for cp in own:
            cp.wait()

    return pl.kernel(
        body, name=name, mesh=_sequencer(),
        out_type=[jax.ShapeDtypeStruct((N_CHIPS,) + small.shape, small.dtype),
                  jax.ShapeDtypeStruct((N_CHIPS,) + shard.shape, shard.dtype)],
        scratch_types=[pltpu.SemaphoreType.DMA((9,)), pltpu.SemaphoreType.DMA((9,)), pltpu.SemaphoreType.DMA((2,))],
        compiler_params=pltpu.CompilerParams(collective_id=collective_id),
    )(small, shard)


def _sibling_exchange(name, arrs):
    n = len(arrs)

    def body(*refs):
        ins, outs = refs[:n], refs[n:2 * n]
        send, recv = refs[2 * n:]
        x, y, c = _position()
        copies = [pltpu.make_async_remote_copy(ins[w], outs[w], send.at[w], recv.at[w], device_id=(x, y, 1 - c),
                                               device_id_type=MESH) for w in range(n)]
        for cp in copies:
            cp.start()
        for cp in copies:
            cp.wait()

    return pl.pallas_call(
        body, name=name, in_specs=[ANY] * n, out_specs=[ANY] * n,
        out_shape=[jax.ShapeDtypeStruct(a.shape, a.dtype) for a in arrs],
        scratch_shapes=[pltpu.SemaphoreType.DMA((n,)), pltpu.SemaphoreType.DMA((n,))],
    )(*arrs)


def _scatter_chips(name, parts):
    n = len(parts)

    def body(*refs):
        ins, outs = refs[:n], refs[n:2 * n]
        send, recv, loc = refs[2 * n:]
        x, y, c = _position()
        me = 2 * x + y
        chips = _other_chips(x, y)
        started = []
        for w in range(n):
            own = pltpu.make_async_copy(ins[w].at[me], outs[w].at[me], loc.at[w])
            own.start()
            started.append(own)
        sends = []
        for w in range(n):
            for j, (px, py) in enumerate(chips):
                cp = pltpu.make_async_remote_copy(ins[w].at[2 * px + py], outs[w].at[me], send.at[3 * w + j],
                                                  recv.at[3 * w + j], device_id=(px, py, c), device_id_type=MESH)
                cp.start()
                sends.append(cp)
        for w in range(n):
            for j, (px, py) in enumerate(chips):
                pltpu.make_async_remote_copy(ins[w].at[me], outs[w].at[2 * px + py], send.at[3 * w + j], recv.at[3 * w + j],
                                             device_id=(px, py, c), device_id_type=MESH).wait_recv()
        for cp in sends:
            cp.wait_send()
        for own in started:
            own.wait()

    return pl.pallas_call(
        body, name=name, in_specs=[ANY] * n, out_specs=[ANY] * n,
        out_shape=[jax.ShapeDtypeStruct(p.shape, p.dtype) for p in parts],
        scratch_shapes=[pltpu.SemaphoreType.DMA((3 * n,)), pltpu.SemaphoreType.DMA((3 * n,)), pltpu.SemaphoreType.DMA((n,))],
    )(*parts)


def _sequencer():
    return plsc.ScalarSubcoreMesh(axis_name="sequencer", num_cores=1)


def _sibling_exchange_async(name, arrs, collective_id):
    n = len(arrs)

    def body(*refs):
        ins, outs = refs[:n], refs[n:2 * n]
        send, recv = refs[2 * n:]
        x, y, c = _position()
        sibling = (x, y, 1 - c)
        barrier = pltpu.get_barrier_semaphore()
        pl.semaphore_signal(barrier, inc=1, device_id=sibling, device_id_type=MESH)
        pl.semaphore_wait(barrier, 1)
        copies = [pltpu.make_async_remote_copy(ins[w], outs[w], send.at[w], recv.at[w], device_id=sibling,
                                               device_id_type=MESH) for w in range(n)]
        for cp in copies:
            cp.start()
        for cp in copies:
            cp.wait()

    return pl.kernel(
        body, name=name, out_type=[jax.ShapeDtypeStruct(a.shape, a.dtype) for a in arrs],
        scratch_types=[pltpu.SemaphoreType.DMA((n,)), pltpu.SemaphoreType.DMA((n,))],
        compiler_params=pltpu.CompilerParams(collective_id=collective_id), mesh=_sequencer(),
    )(*arrs)


def _scatter_chips_async(name, parts, collective_id):
    n = len(parts)

    def body(*refs):
        ins, outs = refs[:n], refs[n:2 * n]
        send, recv, loc = refs[2 * n:]
        x, y, c = _position()
        me = 2 * x + y
        chips = _other_chips(x, y)
        barrier = pltpu.get_barrier_semaphore()
        for px, py in chips:
            pl.semaphore_signal(barrier, inc=1, device_id=(px, py, c), device_id_type=MESH)
        pl.semaphore_wait(barrier, len(chips))
        started = []
        for w in range(n):
            own = pltpu.make_async_copy(ins[w].at[me], outs[w].at[me], loc.at[w])
            own.start()
            started.append(own)
        sends = []
        for w in range(n):
            for j, (px, py) in enumerate(chips):
                cp = pltpu.make_async_remote_copy(ins[w].at[2 * px + py], outs[w].at[me], send.at[3 * w + j],
                                                  recv.at[3 * w + j], device_id=(px, py, c), device_id_type=MESH)
                cp.start()
                sends.append(cp)
        for w in range(n):
            for j, (px, py) in enumerate(chips):
                pltpu.make_async_remote_copy(ins[w].at[me], outs[w].at[2 * px + py], send.at[3 * w + j], recv.at[3 * w + j],
                                             device_id=(px, py, c), device_id_type=MESH).wait_recv()
        for cp in sends:
            cp.wait_send()
        for own in started:
            own.wait()

    return pl.kernel(
        body, name=name, out_type=[jax.ShapeDtypeStruct(p.shape, p.dtype) for p in parts],
        scratch_types=[pltpu.SemaphoreType.DMA((3 * n,)), pltpu.SemaphoreType.DMA((3 * n,)), pltpu.SemaphoreType.DMA((n,))],
        compiler_params=pltpu.CompilerParams(collective_id=collective_id), mesh=_sequencer(),
    )(*parts)


def _allreduce_rows(buf):
    R = buf.shape[0]

    def body(in_ref, out_ref, land, send, recv):
        x, y, c = _position()
        me = 4 * x + 2 * y + c
        land[pl.ds(me, 1)] = in_ref[...][None]
        peers = []
        for mask in range(1, N_DEV):
            px = 1 - x if mask & 4 else x
            py = 1 - y if mask & 2 else y
            pc = 1 - c if mask & 1 else c
            peers.append((px, py, pc))
        sends = []
        for k, peer in enumerate(peers):
            cp = pltpu.make_async_remote_copy(in_ref, land.at[me], send.at[k], recv.at[k], device_id=peer, device_id_type=MESH)
            cp.start()
            sends.append(cp)
        for k, (px, py, pc) in enumerate(peers):
            pltpu.make_async_remote_copy(in_ref, land.at[4 * px + 2 * py + pc], send.at[k], recv.at[k],
                                         device_id=(px, py, pc), device_id_type=MESH).wait_recv()
        for cp in sends:
            cp.wait_send()
        tot = land[0]
        for i in range(1, N_DEV):
            tot = tot + land[i]
        out_ref[...] = tot

    vm = pl.BlockSpec(memory_space=pltpu.VMEM)
    return pl.pallas_call(
        body, name="allreduce_small", in_specs=[vm], out_specs=vm, out_shape=jax.ShapeDtypeStruct((R, LANES), F32),
        scratch_shapes=[pltpu.VMEM((N_DEV, R, LANES), F32), pltpu.SemaphoreType.DMA((N_DEV - 1,)),
                        pltpu.SemaphoreType.DMA((N_DEV - 1,))],
    )(buf)


def _tile2d(r, c, cap):
    if r <= cap:
        return r, c
    fits = [t for t in range(16, cap + 1, 16) if r % t == 0]
    return (max(fits), c) if fits else (r, 256)


def _pair_sum(name, a, b):
    n, r, c = a.shape
    tr, tc = _tile2d(r, c, 1024)

    def body(a_ref, b_ref, o_ref):
        o_ref[...] = (a_ref[...].astype(F32) + b_ref[...].astype(F32)).astype(BF16)

    blk = pl.BlockSpec((None, tr, tc), lambda s, i, j: (s, i, j))
    return pl.pallas_call(
        body, name=name, grid=(n, r // tr, c // tc), in_specs=[blk, blk], out_specs=blk,
        out_shape=jax.ShapeDtypeStruct(a.shape, BF16),
    )(a, b)


def _adamw_math(w, m, v, g):
    m2 = ADAM_B1 * m + (1.0 - ADAM_B1) * g
    v2 = ADAM_B2 * v + (1.0 - ADAM_B2) * (g * g)
    m_hat = m2 / (1.0 - ADAM_B1 ** ADAM_STEP)
    v_hat = v2 / (1.0 - ADAM_B2 ** ADAM_STEP)
    delta = -ADAM_LR * (m_hat / (jnp.sqrt(v_hat) + ADAM_EPS) + ADAM_WD * w)
    return delta, m2, v2


def _adamw(name, w, m, v, g):
    r, c = w.shape
    stacked = g.ndim == 3
    tr, tc = _tile2d(r, c, 256)

    def body(w_ref, m_ref, v_ref, g_ref, go_ref, d_ref, m2_ref, v2_ref):
        if stacked:
            gv = g_ref[0].astype(F32)
            for i in range(1, N_CHIPS):
                gv = gv + g_ref[i].astype(F32)
        else:
            gv = g_ref[...]
        delta, m2, v2 = _adamw_math(w_ref[...], m_ref[...], v_ref[...], gv)
        go_ref[...] = gv
        d_ref[...] = delta
        m2_ref[...] = m2
        v2_ref[...] = v2

    blk = pl.BlockSpec((tr, tc), lambda i, j: (i, j))
    g_spec = pl.BlockSpec((N_CHIPS, tr, tc), lambda i, j: (0, i, j)) if stacked else blk
    out = jax.ShapeDtypeStruct((r, c), F32)
    return pl.pallas_call(
        body, name=name, grid=(r // tr, c // tc), in_specs=[blk, blk, blk, g_spec], out_specs=[blk] * 4, out_shape=[out] * 4,
    )(w, m, v, g)


def _pack_rows(pieces):
    flat = jnp.concatenate([p.reshape(-1) for p in pieces])
    rows = flat.shape[0] // LANES
    pad = (-rows) % 8
    return jnp.pad(flat.reshape(rows, LANES), ((0, pad), (0, 0)))


def _unpack_rows(buf, shapes):
    flat = buf.reshape(-1)
    out, at = [], 0
    for s in shapes:
        size = math.prod(s)
        out.append(flat[at:at + size].reshape(s))
        at += size
    return out


SMALL_NAMES = ("norm1_g", "gf_up", "gf_b", "gb_up", "gb_b", "gla_norm_g", "attn_norm_g", "norm2_g", "conv_w", "conv_b",
               "final_norm_g")
BIG_NAMES = ("w_in", "w_out", "w_gate", "w_up", "w_down")
WEIGHT_ORDER = ("norm1_g", "w_in", "gf_up", "gf_b", "gb_up", "gb_b", "gla_norm_g", "attn_norm_g", "w_out", "norm2_g",
                "w_gate", "w_up", "conv_w", "conv_b", "w_down", "final_norm_g")


def kernel(x, norm1_g, w_in, gf_up, gf_b, gb_up, gb_b, gla_norm_g, attn_norm_g, w_out, norm2_g, w_gate, w_up, conv_w, conv_b, w_down, final_norm_g, loss_target, m_norm1_g, m_w_in, m_gf_up, m_gf_b, m_gb_up, m_gb_b, m_gla_norm_g, m_attn_norm_g, m_w_out, m_norm2_g, m_w_gate, m_w_up, m_conv_w, m_conv_b, m_w_down, m_final_norm_g, v_norm1_g, v_w_in, v_gf_up, v_gf_b, v_gb_up, v_gb_b, v_gla_norm_g, v_attn_norm_g, v_w_out, v_norm2_g, v_w_gate, v_w_up, v_conv_w, v_conv_b, v_w_down, v_final_norm_g):
    w = dict(norm1_g=norm1_g, w_in=w_in, gf_up=gf_up, gf_b=gf_b, gb_up=gb_up, gb_b=gb_b, gla_norm_g=gla_norm_g,
             attn_norm_g=attn_norm_g, w_out=w_out, norm2_g=norm2_g, w_gate=w_gate, w_up=w_up, conv_w=conv_w, conv_b=conv_b,
             w_down=w_down, final_norm_g=final_norm_g)
    m = dict(norm1_g=m_norm1_g, w_in=m_w_in, gf_up=m_gf_up, gf_b=m_gf_b, gb_up=m_gb_up, gb_b=m_gb_b, gla_norm_g=m_gla_norm_g,
             attn_norm_g=m_attn_norm_g, w_out=m_w_out, norm2_g=m_norm2_g, w_gate=m_w_gate, w_up=m_w_up, conv_w=m_conv_w,
             conv_b=m_conv_b, w_down=m_w_down, final_norm_g=m_final_norm_g)
    v = dict(norm1_g=v_norm1_g, w_in=v_w_in, gf_up=v_gf_up, gf_b=v_gf_b, gb_up=v_gb_up, gb_b=v_gb_b, gla_norm_g=v_gla_norm_g,
             attn_norm_g=v_attn_norm_g, w_out=v_w_out, norm2_g=v_norm2_g, w_gate=v_w_gate, w_up=v_w_up, conv_w=v_conv_w,
             conv_b=v_conv_b, w_down=v_w_down, final_norm_g=v_final_norm_g)
    S = x.shape[1]
    chip = 2 * lax.axis_index("x") + lax.axis_index("y")
    n_in = IN_W // N_CHIPS
    n_ff = D_FF // N_CHIPS
    n_gk = GLA_K // N_CHIPS

    def owned(t):
        return {k: (jnp.transpose(t[k][0]) if k == "w_in" else t[k][0]) for k in BIG_NAMES}

    own_w, own_m, own_v = owned(w), owned(m), owned(v)
    shard = {k: own_w[k].astype(BF16) for k in BIG_NAMES}
    small_shard = _pack_rows([gf_up[0], gb_up[0], conv_w[0]])
    small4, w_in4 = _gather_halves_async("gather_w_in", small_shard, shard["w_in"], 0)
    w_out4, w_gate4, w_up4 = _gather_chips_async("gather_w_mid", [shard["w_out"], shard["w_gate"], shard["w_up"]], 1)
    (w_down4,) = _gather_chips_async("gather_w_down", [shard["w_down"]], 2)
    w_in_t = w_in4.reshape(IN_W, D_MODEL)
    rows_up = GATE_RANK * n_gk // LANES
    rows_cw = 3 * n_ff // LANES
    gf_full = jnp.transpose(small4[:, 0:rows_up].reshape(N_CHIPS, GATE_RANK, n_gk), (1, 0, 2)).reshape(GATE_RANK, GLA_K)
    gb_full = jnp.transpose(small4[:, rows_up:2 * rows_up].reshape(N_CHIPS, GATE_RANK, n_gk), (1, 0, 2)).reshape(GATE_RANK, GLA_K)
    cw_full = jnp.transpose(small4[:, 2 * rows_up:2 * rows_up + rows_cw].reshape(N_CHIPS, 3, n_ff), (1, 0, 2)).reshape(3, D_FF)
    wg = jnp.zeros((Z_W, 2 * GLA_K), F32)
    wg = wg.at[0:GATE_RANK, 0:GLA_K].set(gf_full).at[GATE_RANK:2 * GATE_RANK, GLA_K:].set(gb_full).astype(BF16)
    gate_bias = jnp.concatenate([gf_b, gb_b], axis=1)

    pending, contributions, next_id = [], {}, [3]

    def as_shards(group, arrays):
        if group == "w_in":
            return dict(w_in=arrays["w_in_t"].reshape(N_CHIPS, n_in, D_MODEL))
        if group == "w_out":
            return dict(w_out=arrays["w_out"].reshape(N_CHIPS, D_MODEL // N_CHIPS, D_MODEL))
        if group == "w_down":
            return dict(w_down=arrays["w_down"].reshape(N_CHIPS, n_ff, D_MODEL))
        return arrays

    out = {}

    def swap(group, arrays):
        mine = as_shards(group, arrays)
        pending.append((group, mine, _sibling_exchange_async(f"sibling_{group}", list(mine.values()), next_id[0])))
        next_id[0] += 1

    def sum_and_send(anchor):
        tag, mine, theirs = pending.pop()
        sums = [_pair_sum(f"pair_sum_{k}", mine[k], _after(t, *anchor)) for k, t in zip(mine, theirs)]
        contributions.update(zip(mine, _scatter_chips_async(f"scatter_{tag}", sums, next_id[0])))
        next_id[0] += 1
        return sums

    def update(names, anchor):
        for k in names:
            res = _adamw(f"adamw_{k}", own_w[k], own_m[k], own_v[k], _after(contributions[k], *anchor))
            out[k] = [(jnp.transpose(r) if k == "w_in" else r)[None] for r in res]
        return [out[k][0] for k in names]

    def on_grad(event, arrays):
        anchor = list(arrays.values())
        held = []
        if event in ("w_gate_w_up", "w_out", "mid", "last"):
            held += sum_and_send(anchor)
        if event == "mid":
            held += update(("w_down", "w_gate", "w_up"), anchor)
        if event == "last":
            held += update(("w_out",), anchor)
        if event in ("w_down", "w_gate_w_up", "w_out", "w_in"):
            swap(event, arrays)
        return held

    grad_x, _, small = _local_step(
        x[0], loss_target[0], norm1_g, w_in_t, wg, gate_bias, gla_norm_g, attn_norm_g,
        w_out4.reshape(D_MODEL, D_MODEL), norm2_g, w_gate4, w_up4, cw_full, conv_b, w_down4.reshape(D_FF, D_MODEL), final_norm_g,
        on_grad=on_grad)
    update(("w_in",), [grad_x])

    d_gf_up = small["wg"][0:GATE_RANK, 0:GLA_K]
    d_gb_up = small["wg"][GATE_RANK:2 * GATE_RANK, GLA_K:]
    pieces = [small["loss"], small["norm1_g"], d_gf_up, small["gate_bias"][:, :GLA_K], d_gb_up, small["gate_bias"][:, GLA_K:],
              small["gla_norm_g"], small["attn_norm_g"], small["norm2_g"], small["conv_w"], small["conv_b"], small["final_norm_g"]]
    total = _allreduce_rows(_pack_rows(pieces))
    summed = _unpack_rows(total, [p.shape for p in pieces])
    loss = summed[0][0, 0]
    g_small = dict(zip(SMALL_NAMES, summed[1:]))
    g_small["gf_up"] = lax.dynamic_slice_in_dim(g_small["gf_up"], chip * n_gk, n_gk, axis=1)
    g_small["gb_up"] = lax.dynamic_slice_in_dim(g_small["gb_up"], chip * n_gk, n_gk, axis=1)
    g_small["conv_w"] = lax.dynamic_slice_in_dim(g_small["conv_w"], chip * n_ff, n_ff, axis=1)
    packed = [_pack_rows([t[k] for k in SMALL_NAMES]) for t in (w, m, v, g_small)]
    res = _adamw("adamw_small", *packed)
    shapes = [w[k].shape for k in SMALL_NAMES]
    for k, vals in zip(SMALL_NAMES, zip(*[_unpack_rows(r, shapes) for r in res])):
        out[k] = list(vals)

    grads, deltas, new_m, new_v = ([out[k][i] for k in WEIGHT_ORDER] for i in range(4))
    return (loss, grad_x[None], *grads, *deltas, *new_m, *new_v)
```

```python
import functools
import math

import jax
import jax.numpy as jnp
from jax import lax
from jax.experimental import pallas as pl
from jax.experimental.pallas import tpu as pltpu
from jax.experimental.pallas import tpu_sc as plsc

F32 = jnp.float32
BF16 = jnp.bfloat16

D_MODEL = 2048
ATTN_W = 1024
HEAD = 128
N_HEADS = 8
N_SIDE = 64
DILATIONS = (1, 4, 16)
ROPE_THETA = 500000.0
ROPE_DIM = 32
GLA_K = 512
GLA_V = 1024
GLA_HEADS = 4
GLA_DK = 128
GLA_DV = 256
GATE_RANK = 16
GATE_NORM = 16.0
CHUNK = 64
IN_MAIN = 6144
IN_W = 6176
Z_W = IN_W - IN_MAIN
D_FF = 5632
EPS = 1e-6
N_CHIPS = 4
N_DEV = 8
LANES = 128

ADAM_LR = 0.001
ADAM_B1 = 0.9
ADAM_B2 = 0.999
ADAM_EPS = 1e-08
ADAM_WD = 0.01
ADAM_STEP = 10

NEG = -1e30
MESH = pl.DeviceIdType.MESH
ANY = pl.BlockSpec(memory_space=pl.ANY)

NN = ((1,), (0,))
NT = ((1,), (1,))
TN = ((0,), (0,))


def _dot(a, b, dims=NN):
    return lax.dot_general(a, b, (dims, ((), ())), preferred_element_type=F32)


def _sigmoid(x):
    return 0.5 * jnp.tanh(0.5 * x) + 0.5


def _after(x, *deps):
    return lax.optimization_barrier((x,) + deps)[0]


def _matmul(name, pairs, grid, out_shape, out_spec, nk, res=None, into=None, first=None):
    n_in = 2 * len(pairs) + (res is not None)
    dims = [p[4] for p in pairs]

    n_ops = n_in + (into is not None) + 2 * (first is not None)

    def body(*refs):
        ins, o_ref = refs[:n_in], refs[n_ops]

        def partial_sum():
            tot = None
            for p, dn in enumerate(dims):
                a, b = ins[2 * p][...], ins[2 * p + 1][...]
                t = _dot(a.astype(BF16), b.astype(BF16), dn)
                tot = t if tot is None else tot + t
            return tot

        if nk == 1:
            t = partial_sum()
            if res is not None:
                t = t + ins[-1][...]
            o_ref[...] = t.astype(o_ref.dtype)
        else:
            acc_ref = refs[n_ops + 1]
            k = pl.program_id(2)

            @pl.when(k == 0)
            def _():
                if first is not None:
                    start = _dot(refs[n_in][...].astype(BF16), refs[n_in + 1][...].astype(BF16), first[4])
                    acc_ref[...] = start + ins[-1][...] if res is not None else start
                elif res is not None:
                    acc_ref[...] = ins[-1][...]
                else:
                    acc_ref[...] = jnp.zeros_like(acc_ref)

            acc_ref[...] += partial_sum()

            @pl.when(k == nk - 1)
            def _():
                o_ref[...] = acc_ref[...].astype(o_ref.dtype)

    operands, in_specs = [], []
    for a, a_spec, b, b_spec, _ in pairs:
        operands += [a, b]
        in_specs += [a_spec, b_spec]
    if res is not None:
        operands.append(res[0])
        in_specs.append(res[1])
    if first is not None:
        assert nk > 1
        operands += [first[0], first[2]]
        in_specs += [first[1], first[3]]
    acc_shape = tuple(s for s in out_spec.block_shape if s is not None)
    scratch = [pltpu.VMEM(acc_shape, F32)] if nk > 1 else []
    aliases = {}
    if into is not None:
        aliases = {len(operands): 0}
        operands.append(into)
        in_specs.append(ANY)
    return pl.pallas_call(
        body, name=name, grid=grid, in_specs=in_specs, out_specs=out_spec, out_shape=out_shape, scratch_shapes=scratch,
        input_output_aliases=aliases,
    )(*operands)


def _mm_nn(name, a, b, tm, tn, out_dtype, res=None):
    M, K = a.shape
    N = b.shape[1]
    pairs = [(a, pl.BlockSpec((tm, K), lambda j, i: (i, 0)), b, pl.BlockSpec((K, tn), lambda j, i: (0, j)), NN)]
    r = None if res is None else (res, pl.BlockSpec((tm, tn), lambda j, i: (i, j)))
    return _matmul(name, pairs, (N // tn, M // tm), jax.ShapeDtypeStruct((M, N), out_dtype),
                   pl.BlockSpec((tm, tn), lambda j, i: (i, j)), 1, r)


def _mm_nn_sharded(name, a, b4, tm, out_dtype):
    M, K = a.shape
    n = b4.shape[2]
    pairs = [(a, pl.BlockSpec((tm, K), lambda j, i: (i, 0)), b4, pl.BlockSpec((None, K, n), lambda j, i: (j, 0, 0)), NN)]
    return _matmul(name, pairs, (N_CHIPS, M // tm), jax.ShapeDtypeStruct((M, N_CHIPS * n), out_dtype),
                   pl.BlockSpec((tm, n), lambda j, i: (i, j)), 1)


def _mm_nt(name, a, b, tm, tn, out_dtype, res=None, n_out=None):
    M, K = a.shape
    N = b.shape[0] if n_out is None else n_out
    pairs = [(a, pl.BlockSpec((tm, K), lambda j, i: (i, 0)), b, pl.BlockSpec((tn, K), lambda j, i: (j, 0)), NT)]
    r = None if res is None else (res, pl.BlockSpec((tm, tn), lambda j, i: (i, j)))
    return _matmul(name, pairs, (N // tn, M // tm), jax.ShapeDtypeStruct((M, N), out_dtype),
                   pl.BlockSpec((tm, tn), lambda j, i: (i, j)), 1, r)


def _mm_tn(name, a, g, tka, tn, tmm, out_dtype, out3=None, rows_out=None):
    M, Ka = a.shape
    N = g.shape[1]
    pairs = [(a, pl.BlockSpec((tmm, tka), lambda i, j, k: (k, i)), g, pl.BlockSpec((tmm, tn), lambda i, j, k: (k, j)), TN)]
    if out3 is None:
        shape, spec = (Ka if rows_out is None else rows_out, N), pl.BlockSpec((tka, tn), lambda i, j, k: (i, j))
    else:
        shape, spec = (N // out3, Ka, out3), pl.BlockSpec((None, tka, tn), lambda i, j, k: (j, i, 0))
    return _matmul(name, pairs, (Ka // tka, N // tn, M // tmm), jax.ShapeDtypeStruct(shape, out_dtype), spec, M // tmm)


def _rms_fwd(name, x, g, tm=512):
    S, D = x.shape

    def body(x_ref, g_ref, o_ref):
        xv = x_ref[...]
        r = lax.rsqrt(jnp.mean(xv * xv, axis=-1, keepdims=True) + EPS)
        o_ref[...] = (xv * r * g_ref[...]).astype(o_ref.dtype)

    return pl.pallas_call(
        body, name=name, grid=(S // tm,),
        in_specs=[pl.BlockSpec((tm, D), lambda i: (i, 0)), pl.BlockSpec((1, D), lambda i: (0, 0))],
        out_specs=pl.BlockSpec((tm, D), lambda i: (i, 0)), out_shape=jax.ShapeDtypeStruct((S, D), BF16),
    )(x, g)


def _rms_bwd(name, x, g, dn, dres, tm=512):
    S, D = x.shape

    def body(x_ref, g_ref, dn_ref, dres_ref, dx_ref, dxb_ref, dg_ref):
        i = pl.program_id(0)

        @pl.when(i == 0)
        def _():
            dg_ref[...] = jnp.zeros_like(dg_ref)

        xv = x_ref[...]
        r = lax.rsqrt(jnp.mean(xv * xv, axis=-1, keepdims=True) + EPS)
        xhat = xv * r
        dnv = dn_ref[...].astype(F32)
        dg_ref[...] += jnp.sum(dnv * xhat, axis=0, keepdims=True)
        t = dnv * g_ref[...]
        dx = r * (t - xhat * jnp.mean(t * xhat, axis=-1, keepdims=True)) + dres_ref[...]
        dx_ref[...] = dx
        dxb_ref[...] = dx.astype(BF16)

    row = pl.BlockSpec((tm, D), lambda i: (i, 0))
    vec = pl.BlockSpec((1, D), lambda i: (0, 0))
    return pl.pallas_call(
        body, name=name, grid=(S // tm,), in_specs=[row, vec, row, row], out_specs=[row, row, vec],
        out_shape=[jax.ShapeDtypeStruct((S, D), F32), jax.ShapeDtypeStruct((S, D), BF16), jax.ShapeDtypeStruct((1, D), F32)],
    )(x, g, dn, dres)


def _final_loss(h2, g, target, tm=512):
    S, D = h2.shape

    def body(x_ref, g_ref, t_ref, loss_ref, dg_ref, dx_ref, dxb_ref):
        i = pl.program_id(0)

        @pl.when(i == 0)
        def _():
            loss_ref[...] = jnp.zeros_like(loss_ref)
            dg_ref[...] = jnp.zeros_like(dg_ref)

        xv = x_ref[...]
        r = lax.rsqrt(jnp.mean(xv * xv, axis=-1, keepdims=True) + EPS)
        xhat = xv * r
        gv = g_ref[...]
        diff = xhat * gv - t_ref[...]
        per_tok = jnp.mean(diff * diff, axis=-1, keepdims=True)
        loss_ref[...] += 0.5 * jnp.sum(per_tok, axis=0, keepdims=True)
        dy = diff * (1.0 / D)
        dg_ref[...] += jnp.sum(dy * xhat, axis=0, keepdims=True)
        t = dy * gv
        dx = r * (t - xhat * jnp.mean(t * xhat, axis=-1, keepdims=True))
        dx_ref[...] = dx
        dxb_ref[...] = dx.astype(BF16)

    row = pl.BlockSpec((tm, D), lambda i: (i, 0))
    vec = pl.BlockSpec((1, D), lambda i: (0, 0))
    return pl.pallas_call(
        body, name="final_loss", grid=(S // tm,), in_specs=[row, vec, row],
        out_specs=[pl.BlockSpec((1, LANES), lambda i: (0, 0)), vec, row, row],
        out_shape=[jax.ShapeDtypeStruct((1, LANES), F32), jax.ShapeDtypeStruct((1, D), F32),
                   jax.ShapeDtypeStruct((S, D), F32), jax.ShapeDtypeStruct((S, D), BF16)],
    )(h2, g, target)


def _rope_tables(S):
    pos = jnp.arange(S, dtype=F32)
    inv_freq = ROPE_THETA ** (-jnp.arange(0, ROPE_DIM, 2, dtype=F32) / ROPE_DIM)
    ang = pos[:, None] * inv_freq[None, :]
    cos, sin = jnp.cos(ang), jnp.sin(ang)
    half = ROPE_DIM // 2
    rest = HEAD - ROPE_DIM
    z_h, z_r = jnp.zeros((S, half), F32), jnp.zeros((S, rest), F32)
    tab_c = jnp.concatenate([cos, cos, jnp.ones((S, rest), F32)], axis=1)
    tab_up = jnp.concatenate([z_h, sin, z_r], axis=1)
    tab_dn = jnp.concatenate([-sin, z_h, z_r], axis=1)
    return tab_c, tab_up, tab_dn


def _rope_head(t, c, up, dn):
    half = ROPE_DIM // 2
    return t * c + pltpu.roll(t, half, axis=1) * up + pltpu.roll(t, HEAD - half, axis=1) * dn


def _rope_fwd(proj, tabs, tm=512):
    S = proj.shape[0]
    W = 2 * ATTN_W

    def body(p_ref, c_ref, up_ref, dn_ref, o_ref):
        c, up, dn = c_ref[...], up_ref[...], dn_ref[...]
        for h in range(W // HEAD):
            sl = slice(h * HEAD, (h + 1) * HEAD)
            o_ref[:, sl] = _rope_head(p_ref[:, sl].astype(F32), c, up, dn).astype(BF16)

    tab = pl.BlockSpec((tm, HEAD), lambda i: (i, 0))
    return pl.pallas_call(
        body, name="rope_fwd", grid=(S // tm,), in_specs=[pl.BlockSpec((tm, W), lambda i: (i, 0)), tab, tab, tab],
        out_specs=pl.BlockSpec((tm, W), lambda i: (i, 0)), out_shape=jax.ShapeDtypeStruct((S, W), BF16),
    )(proj, *tabs)


def _attn_grad_merge(dqs, dks, dvs, tabs, tm=256):
    S = dqs[0].shape[0]

    def body(*refs):
        q_refs, k_refs, v_refs = refs[0:3], refs[3:6], refs[6:9]
        c, up, dn = refs[9][...], refs[10][...], refs[11][...]
        o_ref = refs[12]
        for h in range(N_HEADS):
            sl = slice(h * HEAD, (h + 1) * HEAD)
            for part, rs in ((0, q_refs), (1, k_refs)):
                t = rs[0][:, sl].astype(F32) + rs[1][:, sl].astype(F32) + rs[2][:, sl].astype(F32)
                osl = slice(part * ATTN_W + h * HEAD, part * ATTN_W + (h + 1) * HEAD)
                o_ref[:, osl] = _rope_head(t, c, -up, -dn).astype(BF16)
        o_ref[:, 2 * ATTN_W:] = (v_refs[0][...].astype(F32) + v_refs[1][...].astype(F32)
                                 + v_refs[2][...].astype(F32)).astype(BF16)

    blk = pl.BlockSpec((tm, ATTN_W), lambda i: (i, 0))
    tab = pl.BlockSpec((tm, HEAD), lambda i: (i, 0))
    return pl.pallas_call(
        body, name="attn_grad_merge", grid=(S // tm,), in_specs=[blk] * 9 + [tab] * 3,
        out_specs=pl.BlockSpec((tm, 3 * ATTN_W), lambda i: (i, 0)), out_shape=jax.ShapeDtypeStruct((S, IN_MAIN), BF16),
    )(*dqs, *dks, *dvs, *tabs)


SUB = 128
Q_COL, K_COL, V_COL = 0, ATTN_W // HEAD, 2 * ATTN_W // HEAD


class _AttnGeo:
    def __init__(self, S, d):
        self.S, self.d, self.L = S, d, S // d
        self.halo = N_SIDE * d
        self.TB = min(2048, S)
        self.W = self.TB + 2 * self.halo
        self.n_sub = self.TB // SUB
        self.grid = (S // self.TB, N_HEADS)
        self.dt = F32 if d > 1 else BF16
        self.su = min(d, 4)
        self.sb = d // self.su
        assert self.TB % (SUB * d) == 0 and self.TB % self.halo == 0

    def specs(self, width, col0, per_head=True):
        ratio = self.TB // self.halo
        last = self.S // self.halo - 1
        col = (lambda h: col0 + h) if per_head else (lambda h: col0)
        cur = pl.BlockSpec((self.TB, width), lambda i, h: (i, col(h)))
        prev = pl.BlockSpec((self.halo, width), lambda i, h: (jnp.maximum(i * ratio - 1, 0), col(h)))
        nxt = pl.BlockSpec((self.halo, width), lambda i, h: (jnp.minimum((i + 1) * ratio, last), col(h)))
        return cur, prev, nxt

    def scratch(self, rows, dtype=None):
        nat = pltpu.VMEM((rows, LANES), self.dt if dtype is None else dtype)
        return [nat] if self.sb == 1 else [nat, pltpu.VMEM((rows, LANES), F32)]

    def bind(self, refs):
        nat = next(refs)
        return (nat, nat) if self.sb == 1 else (nat, next(refs))

    def spread(self, pair):
        nat, streams = pair
        if self.sb > 1:
            n = nat.shape[0] // self.sb
            for a in range(self.sb):
                streams[a * n:(a + 1) * n, :] = nat[pl.ds(a, n, stride=self.sb), :]
        return streams

    def gather(self, pair):
        nat, streams = pair
        if self.sb > 1:
            n = nat.shape[0] // self.sb
            for a in range(self.sb):
                nat[pl.ds(a, n, stride=self.sb), :] = streams[a * n:(a + 1) * n, :]
        return nat

    def rows(self, sub, n, total):
        res, blk = sub % self.d, sub // self.d
        a, b = res % self.sb, res // self.sb
        start = a * (total // self.sb) + b + self.su * SUB * blk
        return pl.ds(start, n, stride=self.su) if self.su > 1 else pl.ds(start, n)

    def band(self):
        row = lax.broadcasted_iota(jnp.int32, (SUB, 2 * SUB), 0)
        col = lax.broadcasted_iota(jnp.int32, (SUB, 2 * SUB), 1)
        return (col >= row) & (col <= row + 2 * N_SIDE), col

    def mask(self, sub, band):
        inside, col = band
        blk, n_blk = sub // self.d, self.TB // (SUB * self.d)
        base = pl.program_id(0) * (self.TB // self.d) + SUB * blk
        if blk == 0:
            inside = inside & (col >= N_SIDE - base)
        if blk == n_blk - 1:
            inside = inside & (col < self.L + N_SIDE - base)
        return inside

    def fill(self, dst, c_ref):
        dst[...] = c_ref[...].astype(dst.dtype)

    def fill_window(self, dst, p_ref, c_ref, n_ref):
        dst[0:self.halo] = p_ref[...].astype(dst.dtype)
        dst[self.halo:self.halo + self.TB] = c_ref[...].astype(dst.dtype)
        dst[self.halo + self.TB:] = n_ref[...].astype(dst.dtype)


def _lane_of(tile, h):
    lane = lax.broadcasted_iota(jnp.int32, tile.shape, 1)
    return jnp.sum(jnp.where(lane == h, tile, 0.0), axis=1, keepdims=True)


def _attn_fwd(qk, proj, d):
    S = qk.shape[0]
    geo = _AttnGeo(S, d)
    scale = HEAD ** -0.5

    def body(q_ref, kp, kc, kn, vp, vc, vn, o_ref, lse_ref, *scratch):
        h = pl.program_id(1)
        refs = iter(scratch)
        q_p, k_p, v_p, o_p, l_p = (geo.bind(refs) for _ in range(5))
        geo.fill(q_p[0], q_ref)
        geo.fill_window(k_p[0], kp, kc, kn)
        geo.fill_window(v_p[0], vp, vc, vn)
        qs, ks, vs = geo.spread(q_p), geo.spread(k_p), geo.spread(v_p)
        os, ls = o_p[1], l_p[1]
        band = geo.band()
        for sub in range(geo.n_sub):
            rq, rw = geo.rows(sub, SUB, geo.TB), geo.rows(sub, 2 * SUB, geo.W)
            q_r, k_r, v_r = qs[rq, :].astype(BF16), ks[rw, :].astype(BF16), vs[rw, :].astype(BF16)
            s = jnp.where(geo.mask(sub, band), _dot(q_r, k_r, NT) * scale, NEG)
            m = jnp.max(s, axis=1, keepdims=True)
            p = jnp.exp(s - m)
            l = jnp.sum(p, axis=1, keepdims=True)
            os[rq, :] = _dot(p.astype(BF16), v_r) / l
            ls[rq, :] = jnp.broadcast_to(m + jnp.log(l), (SUB, LANES))
        o_ref[...] = geo.gather(o_p)[...].astype(BF16)

        @pl.when(h == 0)
        def _():
            lse_ref[...] = jnp.zeros_like(lse_ref)

        lane = lax.broadcasted_iota(jnp.int32, (geo.TB, LANES), 1)
        lse_ref[...] = jnp.where(lane == h, geo.gather(l_p)[...], lse_ref[...])

    q_cur, _, _ = geo.specs(HEAD, Q_COL)
    k_specs = geo.specs(HEAD, K_COL)
    v_specs = geo.specs(HEAD, V_COL)
    stat = pl.BlockSpec((geo.TB, LANES), lambda i, h: (i, 0))
    return pl.pallas_call(
        body, name=f"attn_fwd_d{d}", grid=geo.grid,
        in_specs=[q_cur, k_specs[1], k_specs[0], k_specs[2], v_specs[1], v_specs[0], v_specs[2]],
        out_specs=[q_cur, stat],
        out_shape=[jax.ShapeDtypeStruct((S, ATTN_W), BF16), jax.ShapeDtypeStruct((S, LANES), F32)],
        scratch_shapes=(geo.scratch(geo.TB) + geo.scratch(geo.W) + geo.scratch(geo.W) + geo.scratch(geo.TB, F32)
                        + geo.scratch(geo.TB, F32)),
    )(qk, qk, qk, qk, proj, proj, proj)


def _attn_combine(outs, lses, g, tm=256):
    S = outs[0].shape[0]

    def body(o1, o2, o3, l1, l2, l3, g_ref, ao_ref, o_ref, lse_ref):
        a1, a2, a3 = l1[...], l2[...], l3[...]
        mx = jnp.maximum(jnp.maximum(a1, a2), a3)
        e1, e2, e3 = jnp.exp(a1 - mx), jnp.exp(a2 - mx), jnp.exp(a3 - mx)
        den = e1 + e2 + e3
        lse_ref[...] = mx + jnp.log(den)
        head_of_col = lax.broadcasted_iota(jnp.int32, (LANES, ATTN_W), 1) // HEAD
        spread = (lax.broadcasted_iota(jnp.int32, (LANES, ATTN_W), 0) == head_of_col).astype(F32)

        def wide(e):
            return lax.dot_general(e / den, spread, (NN, ((), ())), precision=lax.Precision.HIGHEST,
                                   preferred_element_type=F32)

        ov = wide(e1) * o1[...].astype(F32) + wide(e2) * o2[...].astype(F32) + wide(e3) * o3[...].astype(F32)
        o_ref[...] = ov
        r = lax.rsqrt(jnp.mean(ov * ov, axis=-1, keepdims=True) + EPS)
        ao_ref[...] = (ov * r * g_ref[...]).astype(BF16)

    blk = pl.BlockSpec((tm, ATTN_W), lambda i: (i, 0))
    ls = pl.BlockSpec((tm, LANES), lambda i: (i, 0))
    return pl.pallas_call(
        body, name="attn_combine", grid=(S // tm,),
        in_specs=[blk, blk, blk, ls, ls, ls, pl.BlockSpec((1, ATTN_W), lambda i: (0, 0))], out_specs=[blk, blk, ls],
        out_shape=[jax.ShapeDtypeStruct((S, D_MODEL), BF16), jax.ShapeDtypeStruct((S, ATTN_W), F32),
                   jax.ShapeDtypeStruct((S, LANES), F32)],
    )(*outs, *lses, g)


def _attn_norm_bwd(o, g, dao, tm=256):
    S = o.shape[0]

    def body(o_ref, g_ref, dao_ref, do_ref, dl_ref, dg_ref):
        i = pl.program_id(0)

        @pl.when(i == 0)
        def _():
            dg_ref[...] = jnp.zeros_like(dg_ref)

        ov = o_ref[...]
        r = lax.rsqrt(jnp.mean(ov * ov, axis=-1, keepdims=True) + EPS)
        ohat = ov * r
        dn = dao_ref[...].astype(F32)
        dg_ref[...] += jnp.sum(dn * ohat, axis=0, keepdims=True)
        t = dn * g_ref[...]
        do = r * (t - ohat * jnp.mean(t * ohat, axis=-1, keepdims=True))
        do_ref[...] = do.astype(BF16)
        prod = do * ov
        lane = lax.broadcasted_iota(jnp.int32, (tm, LANES), 1)
        tile = jnp.zeros((tm, LANES), F32)
        for h in range(N_HEADS):
            tile = jnp.where(lane == h, jnp.sum(prod[:, h * HEAD:(h + 1) * HEAD], axis=1, keepdims=True), tile)
        dl_ref[...] = tile

    blk = pl.BlockSpec((tm, ATTN_W), lambda i: (i, 0))
    vec = pl.BlockSpec((1, ATTN_W), lambda i: (0, 0))
    return pl.pallas_call(
        body, name="attn_norm_bwd", grid=(S // tm,),
        in_specs=[blk, vec, pl.BlockSpec((tm, ATTN_W), lambda i: (i, 0))],
        out_specs=[blk, pl.BlockSpec((tm, LANES), lambda i: (i, 0)), vec],
        out_shape=[jax.ShapeDtypeStruct((S, ATTN_W), BF16), jax.ShapeDtypeStruct((S, LANES), F32),
                   jax.ShapeDtypeStruct((1, ATTN_W), F32)],
    )(o, g, dao)


def _attn_bwd_dq(qk, proj, do, lse, delta, d):
    S = qk.shape[0]
    geo = _AttnGeo(S, d)
    scale = HEAD ** -0.5

    def body(q_ref, kp, kc, kn, vp, vc, vn, do_ref, lse_ref, dl_ref, dq_ref, *scratch):
        h = pl.program_id(1)
        refs = iter(scratch)
        q_p, k_p, v_p, do_p, lse_p, dl_p, dq_p = (geo.bind(refs) for _ in range(7))
        geo.fill(q_p[0], q_ref)
        geo.fill(do_p[0], do_ref)
        geo.fill(lse_p[0], lse_ref)
        geo.fill(dl_p[0], dl_ref)
        geo.fill_window(k_p[0], kp, kc, kn)
        geo.fill_window(v_p[0], vp, vc, vn)
        qs, ks, vs, dos = geo.spread(q_p), geo.spread(k_p), geo.spread(v_p), geo.spread(do_p)
        lses, dls = geo.spread(lse_p), geo.spread(dl_p)
        dqs = dq_p[1]
        band = geo.band()
        for sub in range(geo.n_sub):
            rq, rw = geo.rows(sub, SUB, geo.TB), geo.rows(sub, 2 * SUB, geo.W)
            q_r, k_r, v_r = qs[rq, :].astype(BF16), ks[rw, :].astype(BF16), vs[rw, :].astype(BF16)
            lse_c, dl_c = _lane_of(lses[rq, :], h), _lane_of(dls[rq, :], h)
            s = _dot(q_r, k_r, NT) * scale
            p = jnp.where(geo.mask(sub, band), jnp.exp(s - lse_c), 0.0)
            dp = _dot(dos[rq, :].astype(BF16), v_r, NT)
            ds = (p * (dp - dl_c) * scale).astype(BF16)
            dqs[rq, :] = _dot(ds, k_r)
        dq_ref[...] = geo.gather(dq_p)[...].astype(BF16)

    cur, _, _ = geo.specs(HEAD, 0)
    k_specs = geo.specs(HEAD, K_COL)
    v_specs = geo.specs(HEAD, V_COL)
    stat = pl.BlockSpec((geo.TB, LANES), lambda i, h: (i, 0))
    return pl.pallas_call(
        body, name=f"attn_bwd_dq_d{d}", grid=geo.grid,
        in_specs=[cur, k_specs[1], k_specs[0], k_specs[2], v_specs[1], v_specs[0], v_specs[2], cur, stat, stat],
        out_specs=cur, out_shape=jax.ShapeDtypeStruct((S, ATTN_W), BF16),
        scratch_shapes=(geo.scratch(geo.TB) + geo.scratch(geo.W) + geo.scratch(geo.W) + geo.scratch(geo.TB)
                        + geo.scratch(geo.TB, F32) + geo.scratch(geo.TB, F32) + geo.scratch(geo.TB, F32)),
    )(qk, qk, qk, qk, proj, proj, proj, do, lse, delta)


def _attn_bwd_dkv(qk, proj, do, lse, delta, d):
    S = qk.shape[0]
    geo = _AttnGeo(S, d)
    scale = HEAD ** -0.5

    def body(k_ref, v_ref, qp, qc, qn, dop, doc, don, lp, lc, ln, dlp, dlc, dln, dk_ref, dv_ref, *scratch):
        h = pl.program_id(1)
        refs = iter(scratch)
        k_p, v_p, q_p, do_p, lw_p, dlw_p, dk_p, dv_p = (geo.bind(refs) for _ in range(8))
        geo.fill(k_p[0], k_ref)
        geo.fill(v_p[0], v_ref)
        geo.fill_window(q_p[0], qp, qc, qn)
        geo.fill_window(do_p[0], dop, doc, don)
        geo.fill_window(lw_p[0], lp, lc, ln)
        geo.fill_window(dlw_p[0], dlp, dlc, dln)
        ks, vs, qs, dos = geo.spread(k_p), geo.spread(v_p), geo.spread(q_p), geo.spread(do_p)
        lws, dlws = geo.spread(lw_p), geo.spread(dlw_p)
        dks, dvs = dk_p[1], dv_p[1]
        head = lax.broadcasted_iota(jnp.int32, (LANES, 2 * SUB), 0)
        band = geo.band()
        for sub in range(geo.n_sub):
            rq, rw = geo.rows(sub, SUB, geo.TB), geo.rows(sub, 2 * SUB, geo.W)
            k_r, v_r = ks[rq, :].astype(BF16), vs[rq, :].astype(BF16)
            q_w, do_w = qs[rw, :].astype(BF16), dos[rw, :].astype(BF16)
            lse_row = jnp.sum(jnp.where(head == h, lws[rw, :].T, 0.0), axis=0, keepdims=True)
            dl_row = jnp.sum(jnp.where(head == h, dlws[rw, :].T, 0.0), axis=0, keepdims=True)
            st = _dot(k_r, q_w, NT) * scale
            pt = jnp.where(geo.mask(sub, band), jnp.exp(st - lse_row), 0.0)
            dvs[rq, :] = _dot(pt.astype(BF16), do_w)
            dpt = _dot(v_r, do_w, NT)
            dst = (pt * (dpt - dl_row) * scale).astype(BF16)
            dks[rq, :] = _dot(dst, q_w)
        dk_ref[...] = geo.gather(dk_p)[...].astype(BF16)
        dv_ref[...] = geo.gather(dv_p)[...].astype(BF16)

    q_specs = geo.specs(HEAD, Q_COL)
    k_cur, _, _ = geo.specs(HEAD, K_COL)
    v_cur, _, _ = geo.specs(HEAD, V_COL)
    do_specs = geo.specs(HEAD, 0)
    st_specs = geo.specs(LANES, 0, per_head=False)
    cur = do_specs[0]
    return pl.pallas_call(
        body, name=f"attn_bwd_dkv_d{d}", grid=geo.grid,
        in_specs=[k_cur, v_cur, q_specs[1], q_specs[0], q_specs[2], do_specs[1], do_specs[0], do_specs[2],
                  st_specs[1], st_specs[0], st_specs[2], st_specs[1], st_specs[0], st_specs[2]],
        out_specs=[cur, cur],
        out_shape=[jax.ShapeDtypeStruct((S, ATTN_W), BF16), jax.ShapeDtypeStruct((S, ATTN_W), BF16)],
        scratch_shapes=(geo.scratch(geo.TB) + geo.scratch(geo.TB) + geo.scratch(geo.W) + geo.scratch(geo.W)
                        + geo.scratch(geo.W, F32) + geo.scratch(geo.W, F32) + geo.scratch(geo.TB, F32)
                        + geo.scratch(geo.TB, F32)),
    )(qk, proj, qk, qk, qk, do, do, do, lse, lse, lse, delta, delta, delta)


def _cumsum_rows(x, reverse):
    n = x.shape[0]
    row = lax.broadcasted_iota(jnp.int32, x.shape, 0)
    s = 1
    while s < n:
        if reverse:
            x = x + jnp.where(row < n - s, pltpu.roll(x, n - s, axis=0), 0.0)
        else:
            x = x + jnp.where(row >= s, pltpu.roll(x, s, axis=0), 0.0)
        s *= 2
    return x


GLA_GROUP = 8


def _gla_rows(cc):
    return slice(cc * CHUNK, (cc + 1) * CHUNK)


def _gla_chunk_terms(q_ref, k_ref, v_ref, g_ref, h, reverse, rows):
    ksl = slice(h * GLA_DK, (h + 1) * GLA_DK)
    q = q_ref[rows, ksl].astype(F32) * (GLA_DK ** -0.5)
    k = k_ref[rows, ksl].astype(F32)
    v = v_ref[rows, h * GLA_DV:(h + 1) * GLA_DV]
    b = _cumsum_rows(g_ref[rows, ksl], reverse)
    r_ref = CHUNK // 2 if reverse else CHUNK // 2 - 1
    r_last = 0 if reverse else CHUNK - 1
    b_ref, b_last = b[r_ref:r_ref + 1, :], b[r_last:r_last + 1, :]
    ii = lax.broadcasted_iota(jnp.int32, (CHUNK, CHUNK), 0)
    jj = lax.broadcasted_iota(jnp.int32, (CHUNK, CHUNK), 1)
    causal = (jj >= ii) if reverse else (jj <= ii)
    e_q, e_k = jnp.exp(b - b_ref), jnp.exp(b_ref - b)
    e_in, e_st = jnp.exp(b), jnp.exp(b_last - b)
    return dict(q=q, k=k, v=v, b=b, causal=causal, e_q=e_q, e_k=e_k, e_in=e_in, e_st=e_st, dec=jnp.exp(b_last),
                qe=q * e_q, ke=k * e_k, q_in=q * e_in, k_st=k * e_st, r_ref=r_ref, r_last=r_last)


def _gla_specs(order):
    rows = GLA_GROUP * CHUNK
    q = pl.BlockSpec((rows, GLA_K), lambda c: (order(c), 3 * ATTN_W // GLA_K))
    k = pl.BlockSpec((rows, GLA_K), lambda c: (order(c), 3 * ATTN_W // GLA_K + 1))
    v = pl.BlockSpec((rows, GLA_V), lambda c: (order(c), (3 * ATTN_W + 2 * GLA_K) // GLA_V))
    return q, k, v


def _gla_fwd(proj, gates, reverse, o_prev=None):
    S = proj.shape[0]
    n = S // CHUNK
    nb = n // GLA_GROUP
    rows = GLA_GROUP * CHUNK
    order = (lambda c: nb - 1 - c) if reverse else (lambda c: c)
    seq = list(range(GLA_GROUP))[::-1] if reverse else list(range(GLA_GROUP))
    gcol = 1 if reverse else 0

    def body(*refs):
        if o_prev is None:
            q_ref, k_ref, v_ref, g_ref, o_ref, st_ref, state = refs
        else:
            q_ref, k_ref, v_ref, g_ref, op_ref, o_ref, st_ref, state = refs
        c = pl.program_id(0)

        @pl.when(c == 0)
        def _():
            state[...] = jnp.zeros_like(state)

        for h in range(GLA_HEADS):
            vsl = slice(h * GLA_DV, (h + 1) * GLA_DV)
            st = state[h]
            for cc in seq:
                rs = _gla_rows(cc)
                t = _gla_chunk_terms(q_ref, k_ref, v_ref, g_ref, h, reverse, rs)
                a = jnp.where(t["causal"], _dot(t["qe"].astype(BF16), t["ke"].astype(BF16), NT), 0.0)
                o = _dot(a.astype(BF16), t["v"])
                st_b = st.astype(BF16)
                st_ref[cc, h] = st_b
                o = o + _dot(t["q_in"].astype(BF16), st_b, NT)
                st = st * t["dec"] + _dot(t["v"], t["k_st"].astype(BF16), TN)
                if o_prev is not None:
                    o = o + op_ref[rs, vsl]
                o_ref[rs, vsl] = o
            state[h] = st

    q_spec, k_spec, v_spec = _gla_specs(order)
    o_spec = pl.BlockSpec((rows, GLA_V), lambda c: (order(c), 0))
    in_specs = [q_spec, k_spec, v_spec, pl.BlockSpec((rows, GLA_K), lambda c: (order(c), gcol))]
    operands = [proj, proj, proj, gates]
    if o_prev is not None:
        in_specs.append(o_spec)
        operands.append(o_prev)
    return pl.pallas_call(
        body, name="gla_fwd_rev" if reverse else "gla_fwd", grid=(nb,), in_specs=in_specs,
        out_specs=[o_spec, pl.BlockSpec((GLA_GROUP, GLA_HEADS, GLA_DV, GLA_DK), lambda c: (order(c), 0, 0, 0))],
        out_shape=[jax.ShapeDtypeStruct((S, GLA_V), F32), jax.ShapeDtypeStruct((n, GLA_HEADS, GLA_DV, GLA_DK), BF16)],
        scratch_shapes=[pltpu.VMEM((GLA_HEADS, GLA_DV, GLA_DK), F32)],
    )(*operands)


def _gla_bwd(proj, gates, states, do, reverse, prev=None):
    S = proj.shape[0]
    n = S // CHUNK
    nb = n // GLA_GROUP
    rows = GLA_GROUP * CHUNK
    order = (lambda c: c) if reverse else (lambda c: nb - 1 - c)
    seq = list(range(GLA_GROUP)) if reverse else list(range(GLA_GROUP))[::-1]
    gcol = 1 if reverse else 0
    out_dt = F32 if prev is None else BF16

    def body(*refs):
        if prev is None:
            q_ref, k_ref, v_ref, g_ref, st_ref, do_ref, dq_ref, dk_ref, dv_ref, dg_ref, dstate = refs
        else:
            q_ref, k_ref, v_ref, g_ref, st_ref, do_ref, pq, pk, pv, dq_ref, dk_ref, dv_ref, dg_ref, dstate = refs
        c = pl.program_id(0)

        @pl.when(c == 0)
        def _():
            dstate[...] = jnp.zeros_like(dstate)

        row = lax.broadcasted_iota(jnp.int32, (CHUNK, GLA_DK), 0)
        for h in range(GLA_HEADS):
            ksl = slice(h * GLA_DK, (h + 1) * GLA_DK)
            vsl = slice(h * GLA_DV, (h + 1) * GLA_DV)
            dst = dstate[h]
            for cc in seq:
                rs = _gla_rows(cc)
                t = _gla_chunk_terms(q_ref, k_ref, v_ref, g_ref, h, reverse, rs)
                v = t["v"]
                dob = do_ref[rs, vsl].astype(BF16)
                st_b = st_ref[cc, h]
                dst_b = dst.astype(BF16)
                qe_b, ke_b = t["qe"].astype(BF16), t["ke"].astype(BF16)
                q_in_b, k_st_b = t["q_in"].astype(BF16), t["k_st"].astype(BF16)
                a = jnp.where(t["causal"], _dot(qe_b, ke_b, NT), 0.0)
                da = jnp.where(t["causal"], _dot(dob, v, NT), 0.0).astype(BF16)
                dv = _dot(a.astype(BF16), dob, TN) + _dot(k_st_b, dst_b, NT)
                dqe = _dot(da, ke_b)
                dke = _dot(da, qe_b, TN)
                dq_in = _dot(dob, st_b)
                dk_st = _dot(v, dst_b)
                ddec = jnp.sum(dst * st_b.astype(F32), axis=0, keepdims=True)
                dst = _dot(dob, q_in_b, TN) + dst * t["dec"]
                dq = (dqe * t["e_q"] + dq_in * t["e_in"]) * (GLA_DK ** -0.5)
                dk = dke * t["e_k"] + dk_st * t["e_st"]
                w_q, w_k = dqe * t["qe"], dke * t["ke"]
                w_st = dk_st * t["k_st"]
                db = w_q - w_k + dq_in * t["q_in"] - w_st
                db_ref = jnp.sum(w_k - w_q, axis=0, keepdims=True)
                db_last = jnp.sum(w_st, axis=0, keepdims=True) + ddec * t["dec"]
                db = db + jnp.where(row == t["r_ref"], db_ref, 0.0) + jnp.where(row == t["r_last"], db_last, 0.0)
                dg_ref[rs, ksl] = _cumsum_rows(db, not reverse)
                if prev is not None:
                    dq, dk, dv = dq + pq[rs, ksl], dk + pk[rs, ksl], dv + pv[rs, vsl]
                dq_ref[rs, ksl] = dq.astype(out_dt)
                dk_ref[rs, ksl] = dk.astype(out_dt)
                dv_ref[rs, vsl] = dv.astype(out_dt)
            dstate[h] = dst

    q_spec, k_spec, v_spec = _gla_specs(order)
    kk = pl.BlockSpec((rows, GLA_K), lambda c: (order(c), 0))
    vv = pl.BlockSpec((rows, GLA_V), lambda c: (order(c), 0))
    in_specs = [q_spec, k_spec, v_spec, pl.BlockSpec((rows, GLA_K), lambda c: (order(c), gcol)),
                pl.BlockSpec((GLA_GROUP, GLA_HEADS, GLA_DV, GLA_DK), lambda c: (order(c), 0, 0, 0)), vv]
    operands = [proj, proj, proj, gates, states, do]
    if prev is not None:
        in_specs += [kk, kk, vv]
        operands += list(prev)
    return pl.pallas_call(
        body, name="gla_bwd_rev" if reverse else "gla_bwd", grid=(nb,), in_specs=in_specs, out_specs=[kk, kk, vv, kk],
        out_shape=[jax.ShapeDtypeStruct((S, GLA_K), out_dt), jax.ShapeDtypeStruct((S, GLA_K), out_dt),
                   jax.ShapeDtypeStruct((S, GLA_V), out_dt), jax.ShapeDtypeStruct((S, GLA_K), F32)],
        scratch_shapes=[pltpu.VMEM((GLA_HEADS, GLA_DV, GLA_DK), F32)],
    )(*operands)


def _gates_fwd(z, wg, bias, tm=512):
    S = z.shape[0]
    W = 2 * GLA_K

    def body(z_ref, w_ref, b_ref, o_ref):
        zg = _dot(z_ref[...], w_ref[...]) + b_ref[...]
        o_ref[...] = (jnp.minimum(zg, 0.0) - jnp.log(1.0 + jnp.exp(-jnp.abs(zg)))) * (1.0 / GATE_NORM)

    return pl.pallas_call(
        body, name="gates_fwd", grid=(S // tm,),
        in_specs=[pl.BlockSpec((tm, Z_W), lambda i: (i, 0)), pl.BlockSpec((Z_W, W), lambda i: (0, 0)),
                  pl.BlockSpec((1, W), lambda i: (0, 0))],
        out_specs=pl.BlockSpec((tm, W), lambda i: (i, 0)), out_shape=jax.ShapeDtypeStruct((S, W), F32),
    )(z, wg, bias)


def _gates_bwd(z, wg, bias, dg_f, dg_b, tm=512):
    S = z.shape[0]
    W = 2 * GLA_K

    def body(z_ref, w_ref, b_ref, dgf_ref, dgb_ref, dz_ref, dw_ref, db_ref):
        i = pl.program_id(0)

        @pl.when(i == 0)
        def _():
            dw_ref[...] = jnp.zeros_like(dw_ref)
            db_ref[...] = jnp.zeros_like(db_ref)

        zv = z_ref[...]
        zg = _dot(zv, w_ref[...]) + b_ref[...]
        dg = jnp.concatenate([dgf_ref[...], dgb_ref[...]], axis=1)
        dzg = dg * (1.0 / GATE_NORM) * _sigmoid(-zg)
        db_ref[...] += jnp.sum(dzg, axis=0, keepdims=True)
        dzg_b = dzg.astype(BF16)
        dw_ref[...] += _dot(zv, dzg_b, TN)
        dz_ref[...] = _dot(dzg_b, w_ref[...], NT).astype(BF16)

    half = pl.BlockSpec((tm, GLA_K), lambda i: (i, 0))
    return pl.pallas_call(
        body, name="gates_bwd", grid=(S // tm,),
        in_specs=[pl.BlockSpec((tm, Z_W), lambda i: (i, 0)), pl.BlockSpec((Z_W, W), lambda i: (0, 0)),
                  pl.BlockSpec((1, W), lambda i: (0, 0)), half, half],
        out_specs=[pl.BlockSpec((tm, Z_W), lambda i: (i, 0)), pl.BlockSpec((Z_W, W), lambda i: (0, 0)),
                   pl.BlockSpec((1, W), lambda i: (0, 0))],
        out_shape=[jax.ShapeDtypeStruct((S, Z_W), BF16), jax.ShapeDtypeStruct((Z_W, W), F32),
                   jax.ShapeDtypeStruct((1, W), F32)],
    )(z, wg, bias, dg_f, dg_b)


def _gla_out_fwd(o, proj, g, cat, tm=512):
    S = o.shape[0]

    def body(o_ref, gr_ref, g_ref, cat_ref, out_ref):
        gn = g_ref[...]
        for h in range(GLA_HEADS):
            sl = slice(h * GLA_DV, (h + 1) * GLA_DV)
            ov = o_ref[:, sl]
            r = lax.rsqrt(jnp.mean(ov * ov, axis=-1, keepdims=True) + EPS)
            gr = gr_ref[:, sl].astype(F32)
            out_ref[:, sl] = (ov * r * gn * (gr * _sigmoid(gr))).astype(BF16)

    blk = pl.BlockSpec((tm, GLA_V), lambda i: (i, 0))
    return pl.pallas_call(
        body, name="gla_out_fwd", grid=(S // tm,),
        in_specs=[blk, pl.BlockSpec((tm, GLA_V), lambda i: (i, (3 * ATTN_W + 2 * GLA_K + GLA_V) // GLA_V)),
                  pl.BlockSpec((1, GLA_DV), lambda i: (0, 0)), ANY],
        out_specs=pl.BlockSpec((tm, GLA_V), lambda i: (i, 1)), out_shape=jax.ShapeDtypeStruct((S, D_MODEL), BF16),
        input_output_aliases={3: 0},
    )(o, proj, g, cat)


def _gla_out_bwd(o, proj, g, dcat, dproj, tm=512):
    S = o.shape[0]

    def body(o_ref, gr_ref, g_ref, dgo_ref, dproj_ref, do_ref, dgr_ref, dg_ref):
        i = pl.program_id(0)

        @pl.when(i == 0)
        def _():
            dg_ref[...] = jnp.zeros_like(dg_ref)

        gn = g_ref[...]
        dg_acc = jnp.zeros((1, GLA_DV), F32)
        for h in range(GLA_HEADS):
            sl = slice(h * GLA_DV, (h + 1) * GLA_DV)
            ov = o_ref[:, sl]
            r = lax.rsqrt(jnp.mean(ov * ov, axis=-1, keepdims=True) + EPS)
            yhat = ov * r
            gr = gr_ref[:, sl].astype(F32)
            sg = _sigmoid(gr)
            dgo = dgo_ref[:, sl].astype(F32)
            dgr_ref[:, sl] = (dgo * (yhat * gn) * (sg * (1.0 + gr * (1.0 - sg)))).astype(BF16)
            dy = dgo * (gr * sg)
            dg_acc = dg_acc + jnp.sum(dy * yhat, axis=0, keepdims=True)
            t = dy * gn
            do_ref[:, sl] = r * (t - yhat * jnp.mean(t * yhat, axis=-1, keepdims=True))
        dg_ref[...] += dg_acc

    blk = pl.BlockSpec((tm, GLA_V), lambda i: (i, 0))
    vec = pl.BlockSpec((1, GLA_DV), lambda i: (0, 0))
    return pl.pallas_call(
        body, name="gla_out_bwd", grid=(S // tm,),
        in_specs=[blk, pl.BlockSpec((tm, GLA_V), lambda i: (i, (3 * ATTN_W + 2 * GLA_K + GLA_V) // GLA_V)), vec,
                  pl.BlockSpec((tm, GLA_V), lambda i: (i, 1)), ANY],
        out_specs=[blk, pl.BlockSpec((tm, GLA_V), lambda i: (i, (3 * ATTN_W + 2 * GLA_K + GLA_V) // GLA_V)), vec],
        out_shape=[jax.ShapeDtypeStruct((S, GLA_V), F32), jax.ShapeDtypeStruct((S, IN_MAIN), BF16),
                   jax.ShapeDtypeStruct((1, GLA_DV), F32)],
        input_output_aliases={4: 1},
    )(o, proj, g, dcat, dproj)


HALO = 16


def _halo_specs(tm, tn, S):
    cur = pl.BlockSpec((tm, tn), lambda j, i: (i, j))
    prev = pl.BlockSpec((HALO, tn), lambda j, i: (jnp.maximum(i * (tm // HALO) - 1, 0), j))
    nxt = pl.BlockSpec((HALO, tn), lambda j, i: (jnp.minimum((i + 1) * (tm // HALO), S // HALO - 1), j))
    return cur, prev, nxt


def _shifted(c_ref, p_ref, n_ref, n_blocks, i=None):
    if i is None:
        i = pl.program_id(1)
    x = c_ref[...].astype(F32)
    tm = x.shape[0]
    row = lax.broadcasted_iota(jnp.int32, x.shape, 0)
    before = p_ref[HALO - 1:HALO, :].astype(F32) * (i > 0).astype(F32)
    after = n_ref[0:1, :].astype(F32) * (i < n_blocks - 1).astype(F32)
    x_m1 = jnp.where(row == 0, before, pltpu.roll(x, 1, axis=0))
    x_p1 = jnp.where(row == tm - 1, after, pltpu.roll(x, tm - 1, axis=0))
    return x, x_m1, x_p1


def _glu_fwd(gp, up, cw, cb, tm=512, tn=1408):
    S = gp.shape[0]
    nb = S // tm

    def body(c_ref, p_ref, n_ref, up_ref, w_ref, b_ref, o_ref):
        x, x_m1, x_p1 = _shifted(c_ref, p_ref, n_ref, nb)
        w = w_ref[...]
        gate = w[0:1, :] * x_m1 + w[1:2, :] * x + w[2:3, :] * x_p1 + b_ref[...]
        o_ref[...] = (gate * _sigmoid(gate) * up_ref[...].astype(F32)).astype(BF16)

    cur, prev, nxt = _halo_specs(tm, tn, S)
    return pl.pallas_call(
        body, name="glu_fwd", grid=(D_FF // tn, nb),
        in_specs=[cur, prev, nxt, cur, pl.BlockSpec((3, tn), lambda j, i: (0, j)), pl.BlockSpec((1, tn), lambda j, i: (0, j))],
        out_specs=cur, out_shape=jax.ShapeDtypeStruct((S, D_FF), BF16),
    )(gp, gp, gp, up, cw, cb)


def _glu_bwd(gp, up, dact, cw, cb, tm=512, tn=1408):
    S = gp.shape[0]
    nb = S // tm

    def body(c_ref, p_ref, n_ref, up_ref, upp_ref, upn_ref, da_ref, dap_ref, dan_ref, w_ref, b_ref,
             dup_ref, dgp_ref, dw_ref, db_ref):
        i = pl.program_id(1)

        @pl.when(i == 0)
        def _():
            dw_ref[...] = jnp.zeros_like(dw_ref)
            db_ref[...] = jnp.zeros_like(db_ref)

        x, x_m1, x_p1 = _shifted(c_ref, p_ref, n_ref, nb)
        w = w_ref[...]
        w0, w1, w2, b = w[0:1, :], w[1:2, :], w[2:3, :], b_ref[...]

        def d_gate(gate, da, upv):
            sg = _sigmoid(gate)
            return sg, da * upv * (sg * (1.0 + gate * (1.0 - sg)))

        gate = w0 * x_m1 + w1 * x + w2 * x_p1 + b
        da = da_ref[...].astype(F32)
        sg, dgate = d_gate(gate, da, up_ref[...].astype(F32))
        dup_ref[...] = (da * (gate * sg)).astype(BF16)
        db_ref[...] += jnp.sum(dgate, axis=0, keepdims=True)
        dw_ref[...] += jnp.concatenate(
            [jnp.sum(dgate * x_m1, axis=0, keepdims=True), jnp.sum(dgate * x, axis=0, keepdims=True),
             jnp.sum(dgate * x_p1, axis=0, keepdims=True)], axis=0)

        pv, nv = p_ref[...].astype(F32), n_ref[...].astype(F32)
        gate_before = w0 * pv[HALO - 2:HALO - 1, :] + w1 * pv[HALO - 1:HALO, :] + w2 * x[0:1, :] + b
        _, dgate_before = d_gate(gate_before, dap_ref[...].astype(F32)[HALO - 1:HALO, :], upp_ref[...].astype(F32)[HALO - 1:HALO, :])
        gate_after = w0 * x[tm - 1:tm, :] + w1 * nv[0:1, :] + w2 * nv[1:2, :] + b
        _, dgate_after = d_gate(gate_after, dan_ref[...].astype(F32)[0:1, :], upn_ref[...].astype(F32)[0:1, :])
        dgate_before = dgate_before * (i > 0).astype(F32)
        dgate_after = dgate_after * (i < nb - 1).astype(F32)
        row = lax.broadcasted_iota(jnp.int32, dgate.shape, 0)
        dg_m1 = jnp.where(row == 0, dgate_before, pltpu.roll(dgate, 1, axis=0))
        dg_p1 = jnp.where(row == tm - 1, dgate_after, pltpu.roll(dgate, tm - 1, axis=0))
        dgp_ref[...] = (w0 * dg_p1 + w1 * dgate + w2 * dg_m1).astype(BF16)

    cur, prev, nxt = _halo_specs(tm, tn, S)
    w_spec = pl.BlockSpec((3, tn), lambda j, i: (0, j))
    b_spec = pl.BlockSpec((1, tn), lambda j, i: (0, j))
    return pl.pallas_call(
        body, name="glu_bwd", grid=(D_FF // tn, nb),
        in_specs=[cur, prev, nxt, cur, prev, nxt, cur, prev, nxt, w_spec, b_spec],
        out_specs=[cur, cur, w_spec, b_spec],
        out_shape=[jax.ShapeDtypeStruct((S, D_FF), BF16), jax.ShapeDtypeStruct((S, D_FF), BF16),
                   jax.ShapeDtypeStruct((3, D_FF), F32), jax.ShapeDtypeStruct((1, D_FF), F32)],
    )(gp, gp, gp, up, up, up, dact, dact, dact, cw, cb)


def _local_step(x, target, norm1_g, w_in_t, wg, gate_bias, gla_norm_g, attn_norm_g, w_out, norm2_g,
                w_gate4, w_up4, conv_w, conv_b, w_down, final_norm_g, on_grad=lambda event, arrays: ()):
    S = x.shape[0]
    tabs = _rope_tables(S)

    n1 = _rms_fwd("rms1_fwd", x, norm1_g)
    z_block = IN_MAIN // Z_W
    proj = _mm_nt("in_proj", n1, w_in_t, 1024, 1536, BF16, n_out=IN_MAIN)
    z = _matmul(
        "in_proj_z",
        [(n1, pl.BlockSpec((1024, D_MODEL), lambda i: (i, 0)), w_in_t, pl.BlockSpec((Z_W, D_MODEL), lambda i: (z_block, 0)), NT)],
        (S // 1024,), jax.ShapeDtypeStruct((S, Z_W), BF16), pl.BlockSpec((1024, Z_W), lambda i: (i, 0)), 1)
    qk = _rope_fwd(proj, tabs)
    branch = [_attn_fwd(qk, proj, d) for d in DILATIONS]
    ao, o_attn, lse = _attn_combine([b[0] for b in branch], [b[1] for b in branch], attn_norm_g)
    gates = _gates_fwd(z, wg, gate_bias)
    o_f, st_f = _gla_fwd(proj, gates, False)
    o_gla, st_b = _gla_fwd(proj, gates, True, o_prev=o_f)
    cat = _gla_out_fwd(o_gla, proj, gla_norm_g, ao)
    h1 = _mm_nn("out_proj", cat, w_out, 1024, 1024, F32, res=x)
    n2 = _rms_fwd("rms2_fwd", h1, norm2_g)
    gp = _mm_nn_sharded("ffn_gate", n2, w_gate4, 1024, BF16)
    up = _mm_nn_sharded("ffn_up", n2, w_up4, 1024, BF16)
    act = _glu_fwd(gp, up, conv_w, conv_b)
    tk = D_FF // N_CHIPS
    h2 = _mm_nn("ffn_down", act, w_down, 1024, 512, F32, res=h1)
    loss_row, d_final_g, dh2, dh2_b = _final_loss(h2, final_norm_g.reshape(1, D_MODEL), target)

    dact = _mm_nt("ffn_down_bwd", dh2_b, w_down, 1024, tk, BF16)
    dup, dgp, d_conv_w, d_conv_b = _glu_bwd(gp, up, dact, conv_w, conv_b)
    d_w_down = _mm_tn("ffn_down_wgrad", act, dh2_b, tk, D_MODEL, 1024, BF16)
    on_grad("w_down", dict(w_down=d_w_down))
    dgp = _after(dgp, d_w_down)
    d_w_gate4 = _mm_tn("ffn_gate_wgrad", n2, dgp, D_MODEL, tk, 1024, BF16, out3=tk)
    dup = _after(dup, d_w_gate4)
    d_w_up4 = _mm_tn("ffn_up_wgrad", n2, dup, D_MODEL, tk, 1024, BF16, out3=tk)
    held = on_grad("w_gate_w_up", dict(w_gate=d_w_gate4, w_up=d_w_up4))
    dgp = _after(dgp, d_w_up4, *held)
    shard_pairs = [
        (g, pl.BlockSpec((512, tk), functools.partial(lambda s, j, i: (i, s), s)),
         w4, pl.BlockSpec((None, 512, tk), functools.partial(lambda s, j, i: (s, j, 0), s)), NT)
        for g, w4 in ((dgp, w_gate4), (dup, w_up4)) for s in range(N_CHIPS)]
    dn2 = _matmul("ffn_in_bwd", shard_pairs, (D_MODEL // 512, S // 512), jax.ShapeDtypeStruct((S, D_MODEL), F32),
                  pl.BlockSpec((512, 512), lambda j, i: (i, j)), 1)
    dh1, dh1_b, d_norm2_g = _rms_bwd("rms2_bwd", h1, norm2_g, dn2, dh2)

    d_w_out = _mm_tn("out_proj_wgrad", cat, dh1_b, D_MODEL, 1024, 1024, BF16)
    held = on_grad("w_out", dict(w_out=d_w_out))
    dcat = _mm_nt("out_proj_bwd", _after(dh1_b, d_w_out, *held), w_out, 1024, 1024, BF16)
    do_attn, delta, d_attn_norm_g = _attn_norm_bwd(o_attn, attn_norm_g, dcat)
    dqs, dks, dvs = [], [], []
    for d in DILATIONS:
        dqs.append(_attn_bwd_dq(qk, proj, do_attn, lse, delta, d))
        dk, dv = _attn_bwd_dkv(qk, proj, do_attn, lse, delta, d)
        dks.append(dk)
        dvs.append(dv)
    dproj = _attn_grad_merge(dqs, dks, dvs, tabs)
    held = on_grad("mid", dict(anchor=dproj))
    do_gla, dproj, d_gla_norm_g = _gla_out_bwd(o_gla, proj, gla_norm_g, _after(dcat, *held), dproj)
    dq_f, dk_f, dv_f, dg_f = _gla_bwd(proj, gates, st_f, do_gla, False)
    dgq, dgk, dgv, dg_b = _gla_bwd(proj, gates, st_b, do_gla, True, prev=(dq_f, dk_f, dv_f))
    dz, d_wg, d_gate_bias = _gates_bwd(z, wg, gate_bias, dg_f, dg_b)
    dproj = lax.dynamic_update_slice(dproj, jnp.concatenate([dgq, dgk, dgv], axis=1), (0, 3 * ATTN_W))
    d_w_in_t = _mm_tn("in_proj_wgrad", dproj, n1, 1536, D_MODEL, 1024, BF16, rows_out=IN_W)
    n_tok = S // 1024
    d_w_in_t = _matmul(
        "in_proj_z_wgrad",
        [(dz, pl.BlockSpec((1024, Z_W), lambda i, j, k: (k, 0)), n1, pl.BlockSpec((1024, D_MODEL), lambda i, j, k: (k, 0)), TN)],
        (1, 1, n_tok), jax.ShapeDtypeStruct((IN_W, D_MODEL), BF16), pl.BlockSpec((Z_W, D_MODEL), lambda i, j, k: (z_block, 0)),
        n_tok, into=d_w_in_t)
    held = on_grad("w_in", dict(w_in_t=d_w_in_t))
    tkm = IN_MAIN // 4
    half = S // 2048

    def in_proj_bwd(name, first, a, into):
        return _matmul(
            name,
            [(a, pl.BlockSpec((1024, IN_MAIN), lambda j, i: (i + first, 0)), w_in_t, pl.BlockSpec((IN_MAIN, 512), lambda j, i: (0, j)), NN),
             (dz, pl.BlockSpec((1024, Z_W), lambda j, i: (i + first, 0)), w_in_t, pl.BlockSpec((Z_W, 512), lambda j, i: (z_block, j)), NN)],
            (D_MODEL // 512, half), jax.ShapeDtypeStruct((S, D_MODEL), F32),
            pl.BlockSpec((1024, 512), lambda j, i: (i + first, j)), 1, into=into)

    dproj = _after(dproj, d_w_in_t, *held)
    dn1 = in_proj_bwd("in_proj_bwd_a", 0, dproj, None)
    held = on_grad("last", dict(last=dn1))
    dn1 = in_proj_bwd("in_proj_bwd_b", half, dproj, _after(dn1, *held))
    grad_x, _, d_norm1_g = _rms_bwd("rms1_bwd", x, norm1_g, dn1, dh1)

    big = dict(w_in_t=d_w_in_t, w_out=d_w_out, w_gate4=d_w_gate4, w_up4=d_w_up4, w_down=d_w_down)
    small = dict(loss=loss_row, norm1_g=d_norm1_g, wg=d_wg, gate_bias=d_gate_bias, gla_norm_g=d_gla_norm_g,
                 attn_norm_g=d_attn_norm_g, norm2_g=d_norm2_g, conv_w=d_conv_w, conv_b=d_conv_b, final_norm_g=d_final_g)
    return grad_x, big, small


def _position():
    return lax.axis_index("x"), lax.axis_index("y"), lax.axis_index("c")


def _other_chips(x, y):
    return [(1 - x, y), (x, 1 - y), (1 - x, 1 - y)]


def _gather_chips(name, shards):
    n = len(shards)

    def body(*refs):
        ins, outs = refs[:n], refs[n:2 * n]
        send, recv, loc = refs[2 * n:]
        x, y, c = _position()
        me = 2 * x + y
        chips = _other_chips(x, y)
        started = []
        for w in range(n):
            own = pltpu.make_async_copy(ins[w], outs[w].at[me], loc.at[w])
            own.start()
            started.append(own)
        sends = []
        for w in range(n):
            for j, (px, py) in enumerate(chips):
                cp = pltpu.make_async_remote_copy(ins[w], outs[w].at[me], send.at[3 * w + j], recv.at[3 * w + j],
                                                  device_id=(px, py, c), device_id_type=MESH)
                cp.start()
                sends.append(cp)
        for w in range(n):
            for j, (px, py) in enumerate(chips):
                pltpu.make_async_remote_copy(ins[w], outs[w].at[2 * px + py], send.at[3 * w + j], recv.at[3 * w + j],
                                             device_id=(px, py, c), device_id_type=MESH).wait_recv()
        for cp in sends:
            cp.wait_send()
        for own in started:
            own.wait()

    return pl.pallas_call(
        body, name=name, in_specs=[ANY] * n, out_specs=[ANY] * n,
        out_shape=[jax.ShapeDtypeStruct((N_CHIPS,) + s.shape, s.dtype) for s in shards],
        scratch_shapes=[pltpu.SemaphoreType.DMA((3 * n,)), pltpu.SemaphoreType.DMA((3 * n,)), pltpu.SemaphoreType.DMA((n,))],
    )(*shards)


def _gather_chips_async(name, shards, collective_id):
    n = len(shards)

    def body(*refs):
        ins, outs = refs[:n], refs[n:2 * n]
        send, recv, loc = refs[2 * n:]
        x, y, c = _position()
        me = 2 * x + y
        chips = _other_chips(x, y)
        barrier = pltpu.get_barrier_semaphore()
        for px, py in chips:
            pl.semaphore_signal(barrier, inc=1, device_id=(px, py, c), device_id_type=MESH)
        pl.semaphore_wait(barrier, len(chips))
        started = []
        for w in range(n):
            own = pltpu.make_async_copy(ins[w], outs[w].at[me], loc.at[w])
            own.start()
            started.append(own)
        sends = []
        for w in range(n):
            for j, (px, py) in enumerate(chips):
                cp = pltpu.make_async_remote_copy(ins[w], outs[w].at[me], send.at[3 * w + j], recv.at[3 * w + j],
                                                  device_id=(px, py, c), device_id_type=MESH)
                cp.start()
                sends.append(cp)
        for w in range(n):
            for j, (px, py) in enumerate(chips):
                pltpu.make_async_remote_copy(ins[w], outs[w].at[2 * px + py], send.at[3 * w + j], recv.at[3 * w + j],
                                             device_id=(px, py, c), device_id_type=MESH).wait_recv()
        for cp in sends:
            cp.wait_send()
        for own in started:
            own.wait()

    return pl.kernel(
        body, name=name, mesh=_sequencer(),
        out_type=[jax.ShapeDtypeStruct((N_CHIPS,) + s.shape, s.dtype) for s in shards],
        scratch_types=[pltpu.SemaphoreType.DMA((3 * n,)), pltpu.SemaphoreType.DMA((3 * n,)), pltpu.SemaphoreType.DMA((n,))],
        compiler_params=pltpu.CompilerParams(collective_id=collective_id),
    )(*shards)


def _gather_halves_async(name, small, shard, collective_id):
    half = shard.shape[1] // 2

    def body(small_ref, shard_ref, small_out, out, send, recv, loc):
        x, y, c = _position()
        me = 2 * x + y
        sibling = (x, y, 1 - c)
        chips = _other_chips(x, y)
        barrier = pltpu.get_barrier_semaphore()
        for px, py in chips:
            pl.semaphore_signal(barrier, inc=1, device_id=(px, py, c), device_id_type=MESH)
        pl.semaphore_signal(barrier, inc=1, device_id=sibling, device_id_type=MESH)
        pl.semaphore_wait(barrier, len(chips) + 1)
        mine = pl.ds(pl.multiple_of(c * half, LANES), half)
        theirs = pl.ds(pl.multiple_of((1 - c) * half, LANES), half)
        own = [pltpu.make_async_copy(small_ref, small_out.at[me], loc.at[0]),
               pltpu.make_async_copy(shard_ref, out.at[me], loc.at[1])]
        for cp in own:
            cp.start()
        sends = []
        for j, (px, py) in enumerate(chips):
            sends.append(pltpu.make_async_remote_copy(small_ref, small_out.at[me], send.at[j], recv.at[j],
                                                      device_id=(px, py, c), device_id_type=MESH))
            sends.append(pltpu.make_async_remote_copy(shard_ref.at[:, mine], out.at[me, :, mine], send.at[3 + j], recv.at[3 + j],
                                                      device_id=(px, py, c), device_id_type=MESH))
        for cp in sends:
            cp.start()
        passed = []
        for j, (px, py) in enumerate(chips):
            slot = 2 * px + py
            pltpu.make_async_remote_copy(shard_ref.at[:, mine], out.at[slot, :, mine], send.at[3 + j], recv.at[3 + j],
                                         device_id=(px, py, c), device_id_type=MESH).wait_recv()
            cp = pltpu.make_async_remote_copy(out.at[slot, :, mine], out.at[slot, :, mine], send.at[6 + j], recv.at[6 + j],
                                              device_id=sibling, device_id_type=MESH)
            cp.start()
            passed.append(cp)
        for j, (px, py) in enumerate(chips):
            slot = 2 * px + py
            pltpu.make_async_remote_copy(small_ref, small_out.at[slot], send.at[j], recv.at[j],
                                         device_id=(px, py, c), device_id_type=MESH).wait_recv()
            pltpu.make_async_remote_copy(out.at[slot, :, theirs], out.at[slot, :, theirs], send.at[6 + j], recv.at[6 + j],
                                         device_id=sibling, device_id_type=MESH).wait_recv()
        for cp in sends + passed:
            cp.wait_send()
        for cp in own:
            cp.wait()

    return pl.kernel(
        body, name=name, mesh=_sequencer(),
        out_type=[jax.ShapeDtypeStruct((N_CHIPS,) + small.shape, small.dtype),
                  jax.ShapeDtypeStruct((N_CHIPS,) + shard.shape, shard.dtype)],
        scratch_types=[pltpu.SemaphoreType.DMA((9,)), pltpu.SemaphoreType.DMA((9,)), pltpu.SemaphoreType.DMA((2,))],
        compiler_params=pltpu.CompilerParams(collective_id=collective_id),
    )(small, shard)


def _sibling_exchange(name, arrs):
    n = len(arrs)

    def body(*refs):
        ins, outs = refs[:n], refs[n:2 * n]
        send, recv = refs[2 * n:]
        x, y, c = _position()
        copies = [pltpu.make_async_remote_copy(ins[w], outs[w], send.at[w], recv.at[w], device_id=(x, y, 1 - c),
                                               device_id_type=MESH) for w in range(n)]
        for cp in copies:
            cp.start()
        for cp in copies:
            cp.wait()

    return pl.pallas_call(
        body, name=name, in_specs=[ANY] * n, out_specs=[ANY] * n,
        out_shape=[jax.ShapeDtypeStruct(a.shape, a.dtype) for a in arrs],
        scratch_shapes=[pltpu.SemaphoreType.DMA((n,)), pltpu.SemaphoreType.DMA((n,))],
    )(*arrs)


def _scatter_chips(name, parts):
    n = len(parts)

    def body(*refs):
        ins, outs = refs[:n], refs[n:2 * n]
        send, recv, loc = refs[2 * n:]
        x, y, c = _position()
        me = 2 * x + y
        chips = _other_chips(x, y)
        started = []
        for w in range(n):
            own = pltpu.make_async_copy(ins[w].at[me], outs[w].at[me], loc.at[w])
            own.start()
            started.append(own)
        sends = []
        for w in range(n):
            for j, (px, py) in enumerate(chips):
                cp = pltpu.make_async_remote_copy(ins[w].at[2 * px + py], outs[w].at[me], send.at[3 * w + j],
                                                  recv.at[3 * w + j], device_id=(px, py, c), device_id_type=MESH)
                cp.start()
                sends.append(cp)
        for w in range(n):
            for j, (px, py) in enumerate(chips):
                pltpu.make_async_remote_copy(ins[w].at[me], outs[w].at[2 * px + py], send.at[3 * w + j], recv.at[3 * w + j],
                                             device_id=(px, py, c), device_id_type=MESH).wait_recv()
        for cp in sends:
            cp.wait_send()
        for own in started:
            own.wait()

    return pl.pallas_call(
        body, name=name, in_specs=[ANY] * n, out_specs=[ANY] * n,
        out_shape=[jax.ShapeDtypeStruct(p.shape, p.dtype) for p in parts],
        scratch_shapes=[pltpu.SemaphoreType.DMA((3 * n,)), pltpu.SemaphoreType.DMA((3 * n,)), pltpu.SemaphoreType.DMA((n,))],
    )(*parts)


def _sequencer():
    return plsc.ScalarSubcoreMesh(axis_name="sequencer", num_cores=1)


def _sibling_exchange_async(name, arrs, collective_id):
    n = len(arrs)

    def body(*refs):
        ins, outs = refs[:n], refs[n:2 * n]
        send, recv = refs[2 * n:]
        x, y, c = _position()
        sibling = (x, y, 1 - c)
        barrier = pltpu.get_barrier_semaphore()
        pl.semaphore_signal(barrier, inc=1, device_id=sibling, device_id_type=MESH)
        pl.semaphore_wait(barrier, 1)
        copies = [pltpu.make_async_remote_copy(ins[w], outs[w], send.at[w], recv.at[w], device_id=sibling,
                                               device_id_type=MESH) for w in range(n)]
        for cp in copies:
            cp.start()
        for cp in copies:
            cp.wait()

    return pl.kernel(
        body, name=name, out_type=[jax.ShapeDtypeStruct(a.shape, a.dtype) for a in arrs],
        scratch_types=[pltpu.SemaphoreType.DMA((n,)), pltpu.SemaphoreType.DMA((n,))],
        compiler_params=pltpu.CompilerParams(collective_id=collective_id), mesh=_sequencer(),
    )(*arrs)


def _scatter_chips_async(name, parts, collective_id):
    n = len(parts)

    def body(*refs):
        ins, outs = refs[:n], refs[n:2 * n]
        send, recv, loc = refs[2 * n:]
        x, y, c = _position()
        me = 2 * x + y
        chips = _other_chips(x, y)
        barrier = pltpu.get_barrier_semaphore()
        for px, py in chips:
            pl.semaphore_signal(barrier, inc=1, device_id=(px, py, c), device_id_type=MESH)
        pl.semaphore_wait(barrier, len(chips))
        started = []
        for w in range(n):
            own = pltpu.make_async_copy(ins[w].at[me], outs[w].at[me], loc.at[w])
            own.start()
            started.append(own)
        sends = []
        for w in range(n):
            for j, (px, py) in enumerate(chips):
                cp = pltpu.make_async_remote_copy(ins[w].at[2 * px + py], outs[w].at[me], send.at[3 * w + j],
                                                  recv.at[3 * w + j], device_id=(px, py, c), device_id_type=MESH)
                cp.start()
                sends.append(cp)
        for w in range(n):
            for j, (px, py) in enumerate(chips):
                pltpu.make_async_remote_copy(ins[w].at[me], outs[w].at[2 * px + py], send.at[3 * w + j], recv.at[3 * w + j],
                                             device_id=(px, py, c), device_id_type=MESH).wait_recv()
        for cp in sends:
            cp.wait_send()
        for own in started:
            own.wait()

    return pl.kernel(
        body, name=name, out_type=[jax.ShapeDtypeStruct(p.shape, p.dtype) for p in parts],
        scratch_types=[pltpu.SemaphoreType.DMA((3 * n,)), pltpu.SemaphoreType.DMA((3 * n,)), pltpu.SemaphoreType.DMA((n,))],
        compiler_params=pltpu.CompilerParams(collective_id=collective_id), mesh=_sequencer(),
    )(*parts)


def _allreduce_rows(buf):
    R = buf.shape[0]

    def body(in_ref, out_ref, land, send, recv):
        x, y, c = _position()
        me = 4 * x + 2 * y + c
        land[pl.ds(me, 1)] = in_ref[...][None]
        peers = []
        for mask in range(1, N_DEV):
            px = 1 - x if mask & 4 else x
            py = 1 - y if mask & 2 else y
            pc = 1 - c if mask & 1 else c
            peers.append((px, py, pc))
        sends = []
        for k, peer in enumerate(peers):
            cp = pltpu.make_async_remote_copy(in_ref, land.at[me], send.at[k], recv.at[k], device_id=peer, device_id_type=MESH)
            cp.start()
            sends.append(cp)
        for k, (px, py, pc) in enumerate(peers):
            pltpu.make_async_remote_copy(in_ref, land.at[4 * px + 2 * py + pc], send.at[k], recv.at[k],
                                         device_id=(px, py, pc), device_id_type=MESH).wait_recv()
        for cp in sends:
            cp.wait_send()
        tot = land[0]
        for i in range(1, N_DEV):
            tot = tot + land[i]
        out_ref[...] = tot

    vm = pl.BlockSpec(memory_space=pltpu.VMEM)
    return pl.pallas_call(
        body, name="allreduce_small", in_specs=[vm], out_specs=vm, out_shape=jax.ShapeDtypeStruct((R, LANES), F32),
        scratch_shapes=[pltpu.VMEM((N_DEV, R, LANES), F32), pltpu.SemaphoreType.DMA((N_DEV - 1,)),
                        pltpu.SemaphoreType.DMA((N_DEV - 1,))],
    )(buf)


def _tile2d(r, c, cap):
    if r <= cap:
        return r, c
    fits = [t for t in range(16, cap + 1, 16) if r % t == 0]
    return (max(fits), c) if fits else (r, 256)


def _pair_sum(name, a, b):
    n, r, c = a.shape
    tr, tc = _tile2d(r, c, 1024)

    def body(a_ref, b_ref, o_ref):
        o_ref[...] = (a_ref[...].astype(F32) + b_ref[...].astype(F32)).astype(BF16)

    blk = pl.BlockSpec((None, tr, tc), lambda s, i, j: (s, i, j))
    return pl.pallas_call(
        body, name=name, grid=(n, r // tr, c // tc), in_specs=[blk, blk], out_specs=blk,
        out_shape=jax.ShapeDtypeStruct(a.shape, BF16),
    )(a, b)


def _adamw_math(w, m, v, g):
    m2 = ADAM_B1 * m + (1.0 - ADAM_B1) * g
    v2 = ADAM_B2 * v + (1.0 - ADAM_B2) * (g * g)
    m_hat = m2 / (1.0 - ADAM_B1 ** ADAM_STEP)
    v_hat = v2 / (1.0 - ADAM_B2 ** ADAM_STEP)
    delta = -ADAM_LR * (m_hat / (jnp.sqrt(v_hat) + ADAM_EPS) + ADAM_WD * w)
    return delta, m2, v2


def _adamw(name, w, m, v, g):
    r, c = w.shape
    stacked = g.ndim == 3
    tr, tc = _tile2d(r, c, 256)

    def body(w_ref, m_ref, v_ref, g_ref, go_ref, d_ref, m2_ref, v2_ref):
        if stacked:
            gv = g_ref[0].astype(F32)
            for i in range(1, N_CHIPS):
                gv = gv + g_ref[i].astype(F32)
        else:
            gv = g_ref[...]
        delta, m2, v2 = _adamw_math(w_ref[...], m_ref[...], v_ref[...], gv)
        go_ref[...] = gv
        d_ref[...] = delta
        m2_ref[...] = m2
        v2_ref[...] = v2

    blk = pl.BlockSpec((tr, tc), lambda i, j: (i, j))
    g_spec = pl.BlockSpec((N_CHIPS, tr, tc), lambda i, j: (0, i, j)) if stacked else blk
    out = jax.ShapeDtypeStruct((r, c), F32)
    return pl.pallas_call(
        body, name=name, grid=(r // tr, c // tc), in_specs=[blk, blk, blk, g_spec], out_specs=[blk] * 4, out_shape=[out] * 4,
    )(w, m, v, g)


def _pack_rows(pieces):
    flat = jnp.concatenate([p.reshape(-1) for p in pieces])
    rows = flat.shape[0] // LANES
    pad = (-rows) % 8
    return jnp.pad(flat.reshape(rows, LANES), ((0, pad), (0, 0)))


def _unpack_rows(buf, shapes):
    flat = buf.reshape(-1)
    out, at = [], 0
    for s in shapes:
        size = math.prod(s)
        out.append(flat[at:at + size].reshape(s))
        at += size
    return out


SMALL_NAMES = ("norm1_g", "gf_up", "gf_b", "gb_up", "gb_b", "gla_norm_g", "attn_norm_g", "norm2_g", "conv_w", "conv_b",
               "final_norm_g")
BIG_NAMES = ("w_in", "w_out", "w_gate", "w_up", "w_down")
WEIGHT_ORDER = ("norm1_g", "w_in", "gf_up", "gf_b", "gb_up", "gb_b", "gla_norm_g", "attn_norm_g", "w_out", "norm2_g",
                "w_gate", "w_up", "conv_w", "conv_b", "w_down", "final_norm_g")


def kernel(x, norm1_g, w_in, gf_up, gf_b, gb_up, gb_b, gla_norm_g, attn_norm_g, w_out, norm2_g, w_gate, w_up, conv_w, conv_b, w_down, final_norm_g, loss_target, m_norm1_g, m_w_in, m_gf_up, m_gf_b, m_gb_up, m_gb_b, m_gla_norm_g, m_attn_norm_g, m_w_out, m_norm2_g, m_w_gate, m_w_up, m_conv_w, m_conv_b, m_w_down, m_final_norm_g, v_norm1_g, v_w_in, v_gf_up, v_gf_b, v_gb_up, v_gb_b, v_gla_norm_g, v_attn_norm_g, v_w_out, v_norm2_g, v_w_gate, v_w_up, v_conv_w, v_conv_b, v_w_down, v_final_norm_g):
    w = dict(norm1_g=norm1_g, w_in=w_in, gf_up=gf_up, gf_b=gf_b, gb_up=gb_up, gb_b=gb_b, gla_norm_g=gla_norm_g,
             attn_norm_g=attn_norm_g, w_out=w_out, norm2_g=norm2_g, w_gate=w_gate, w_up=w_up, conv_w=conv_w, conv_b=conv_b,
             w_down=w_down, final_norm_g=final_norm_g)
    m = dict(norm1_g=m_norm1_g, w_in=m_w_in, gf_up=m_gf_up, gf_b=m_gf_b, gb_up=m_gb_up, gb_b=m_gb_b, gla_norm_g=m_gla_norm_g,
             attn_norm_g=m_attn_norm_g, w_out=m_w_out, norm2_g=m_norm2_g, w_gate=m_w_gate, w_up=m_w_up, conv_w=m_conv_w,
             conv_b=m_conv_b, w_down=m_w_down, final_norm_g=m_final_norm_g)
    v = dict(norm1_g=v_norm1_g, w_in=v_w_in, gf_up=v_gf_up, gf_b=v_gf_b, gb_up=v_gb_up, gb_b=v_gb_b, gla_norm_g=v_gla_norm_g,
             attn_norm_g=v_attn_norm_g, w_out=v_w_out, norm2_g=v_norm2_g, w_gate=v_w_gate, w_up=v_w_up, conv_w=v_conv_w,
             conv_b=v_conv_b, w_down=v_w_down, final_norm_g=v_final_norm_g)
    S = x.shape[1]
    chip = 2 * lax.axis_index("x") + lax.axis_index("y")
    n_in = IN_W // N_CHIPS
    n_ff = D_FF // N_CHIPS
    n_gk = GLA_K // N_CHIPS

    def owned(t):
        return {k: (jnp.transpose(t[k][0]) if k == "w_in" else t[k][0]) for k in BIG_NAMES}

    own_w, own_m, own_v = owned(w), owned(m), owned(v)
    shard = {k: own_w[k].astype(BF16) for k in BIG_NAMES}
    small_shard = _pack_rows([gf_up[0], gb_up[0], conv_w[0]])
    small4, w_in4 = _gather_halves_async("gather_w_in", small_shard, shard["w_in"], 0)
    w_out4, w_gate4, w_up4 = _gather_chips_async("gather_w_mid", [shard["w_out"], shard["w_gate"], shard["w_up"]], 1)
    (w_down4,) = _gather_chips_async("gather_w_down", [shard["w_down"]], 2)
    w_in_t = w_in4.reshape(IN_W, D_MODEL)
    rows_up = GATE_RANK * n_gk // LANES
    rows_cw = 3 * n_ff // LANES
    gf_full = jnp.transpose(small4[:, 0:rows_up].reshape(N_CHIPS, GATE_RANK, n_gk), (1, 0, 2)).reshape(GATE_RANK, GLA_K)
    gb_full = jnp.transpose(small4[:, rows_up:2 * rows_up].reshape(N_CHIPS, GATE_RANK, n_gk), (1, 0, 2)).reshape(GATE_RANK, GLA_K)
    cw_full = jnp.transpose(small4[:, 2 * rows_up:2 * rows_up + rows_cw].reshape(N_CHIPS, 3, n_ff), (1, 0, 2)).reshape(3, D_FF)
    wg = jnp.zeros((Z_W, 2 * GLA_K), F32)
    wg = wg.at[0:GATE_RANK, 0:GLA_K].set(gf_full).at[GATE_RANK:2 * GATE_RANK, GLA_K:].set(gb_full).astype(BF16)
    gate_bias = jnp.concatenate([gf_b, gb_b], axis=1)

    pending, contributions, next_id = [], {}, [3]

    def as_shards(group, arrays):
        if group == "w_in":
            return dict(w_in=arrays["w_in_t"].reshape(N_CHIPS, n_in, D_MODEL))
        if group == "w_out":
            return dict(w_out=arrays["w_out"].reshape(N_CHIPS, D_MODEL // N_CHIPS, D_MODEL))
        if group == "w_down":
            return dict(w_down=arrays["w_down"].reshape(N_CHIPS, n_ff, D_MODEL))
        return arrays

    out = {}

    def swap(group, arrays):
        mine = as_shards(group, arrays)
        pending.append((group, mine, _sibling_exchange_async(f"sibling_{group}", list(mine.values()), next_id[0])))
        next_id[0] += 1

    def sum_and_send(anchor):
        tag, mine, theirs = pending.pop()
        sums = [_pair_sum(f"pair_sum_{k}", mine[k], _after(t, *anchor)) for k, t in zip(mine, theirs)]
        contributions.update(zip(mine, _scatter_chips_async(f"scatter_{tag}", sums, next_id[0])))
        next_id[0] += 1
        return sums

    def update(names, anchor):
        for k in names:
            res = _adamw(f"adamw_{k}", own_w[k], own_m[k], own_v[k], _after(contributions[k], *anchor))
            out[k] = [(jnp.transpose(r) if k == "w_in" else r)[None] for r in res]
        return [out[k][0] for k in names]

    def on_grad(event, arrays):
        anchor = list(arrays.values())
        held = []
        if event in ("w_gate_w_up", "w_out", "mid", "last"):
            held += sum_and_send(anchor)
        if event == "mid":
            held += update(("w_down", "w_gate", "w_up"), anchor)
        if event == "last":
            held += update(("w_out",), anchor)
        if event in ("w_down", "w_gate_w_up", "w_out", "w_in"):
            swap(event, arrays)
        return held

    grad_x, _, small = _local_step(
        x[0], loss_target[0], norm1_g, w_in_t, wg, gate_bias, gla_norm_g, attn_norm_g,
        w_out4.reshape(D_MODEL, D_MODEL), norm2_g, w_gate4, w_up4, cw_full, conv_b, w_down4.reshape(D_FF, D_MODEL), final_norm_g,
        on_grad=on_grad)
    update(("w_in",), [grad_x])

    d_gf_up = small["wg"][0:GATE_RANK, 0:GLA_K]
    d_gb_up = small["wg"][GATE_RANK:2 * GATE_RANK, GLA_K:]
    pieces = [small["loss"], small["norm1_g"], d_gf_up, small["gate_bias"][:, :GLA_K], d_gb_up, small["gate_bias"][:, GLA_K:],
              small["gla_norm_g"], small["attn_norm_g"], small["norm2_g"], small["conv_w"], small["conv_b"], small["final_norm_g"]]
    total = _allreduce_rows(_pack_rows(pieces))
    summed = _unpack_rows(total, [p.shape for p in pieces])
    loss = summed[0][0, 0]
    g_small = dict(zip(SMALL_NAMES, summed[1:]))
    g_small["gf_up"] = lax.dynamic_slice_in_dim(g_small["gf_up"], chip * n_gk, n_gk, axis=1)
    g_small["gb_up"] = lax.dynamic_slice_in_dim(g_small["gb_up"], chip * n_gk, n_gk, axis=1)
    g_small["conv_w"] = lax.dynamic_slice_in_dim(g_small["conv_w"], chip * n_ff, n_ff, axis=1)
    packed = [_pack_rows([t[k] for k in SMALL_NAMES]) for t in (w, m, v, g_small)]
    res = _adamw("adamw_small", *packed)
    shapes = [w[k].shape for k in SMALL_NAMES]
    for k, vals in zip(SMALL_NAMES, zip(*[_unpack_rows(r, shapes) for r in res])):
        out[k] = list(vals)

    grads, deltas, new_m, new_v = ([out[k][i] for k in WEIGHT_ORDER] for i in range(4))
    return (loss, grad_x[None], *grads, *deltas, *new_m, *new_v)
```

```python
import functools
import math

import jax
import jax.numpy as jnp
from jax import lax
from jax.experimental import pallas as pl
from jax.experimental.pallas import tpu as pltpu
from jax.experimental.pallas import tpu_sc as plsc

F32 = jnp.float32
BF16 = jnp.bfloat16

D_MODEL = 2048
ATTN_W = 1024
HEAD = 128
N_HEADS = 8
N_SIDE = 64
DILATIONS = (1, 4, 16)
ROPE_THETA = 500000.0
ROPE_DIM = 32
GLA_K = 512
GLA_V = 1024
GLA_HEADS = 4
GLA_DK = 128
GLA_DV = 256
GATE_RANK = 16
GATE_NORM = 16.0
CHUNK = 64
IN_MAIN = 6144
IN_W = 6176
Z_W = IN_W - IN_MAIN
D_FF = 5632
EPS = 1e-6
N_CHIPS = 4
N_DEV = 8
LANES = 128

ADAM_LR = 0.001
ADAM_B1 = 0.9
ADAM_B2 = 0.999
ADAM_EPS = 1e-08
ADAM_WD = 0.01
ADAM_STEP = 10

NEG = -1e30
MESH = pl.DeviceIdType.MESH
ANY = pl.BlockSpec(memory_space=pl.ANY)

NN = ((1,), (0,))
NT = ((1,), (1,))
TN = ((0,), (0,))


def _dot(a, b, dims=NN):
    return lax.dot_general(a, b, (dims, ((), ())), preferred_element_type=F32)


def _sigmoid(x):
    return 0.5 * jnp.tanh(0.5 * x) + 0.5


def _after(x, *deps):
    return lax.optimization_barrier((x,) + deps)[0]


def _matmul(name, pairs, grid, out_shape, out_spec, nk, res=None, into=None, first=None):
    n_in = 2 * len(pairs) + (res is not None)
    dims = [p[4] for p in pairs]

    n_ops = n_in + (into is not None) + 2 * (first is not None)

    def body(*refs):
        ins, o_ref = refs[:n_in], refs[n_ops]

        def partial_sum():
            tot = None
            for p, dn in enumerate(dims):
                a, b = ins[2 * p][...], ins[2 * p + 1][...]
                t = _dot(a.astype(BF16), b.astype(BF16), dn)
                tot = t if tot is None else tot + t
            return tot

        if nk == 1:
            t = partial_sum()
            if res is not None:
                t = t + ins[-1][...]
            o_ref[...] = t.astype(o_ref.dtype)
        else:
            acc_ref = refs[n_ops + 1]
            k = pl.program_id(2)

            @pl.when(k == 0)
            def _():
                if first is not None:
                    start = _dot(refs[n_in][...].astype(BF16), refs[n_in + 1][...].astype(BF16), first[4])
                    acc_ref[...] = start + ins[-1][...] if res is not None else start
                elif res is not None:
                    acc_ref[...] = ins[-1][...]
                else:
                    acc_ref[...] = jnp.zeros_like(acc_ref)

            acc_ref[...] += partial_sum()

            @pl.when(k == nk - 1)
            def _():
                o_ref[...] = acc_ref[...].astype(o_ref.dtype)

    operands, in_specs = [], []
    for a, a_spec, b, b_spec, _ in pairs:
        operands += [a, b]
        in_specs += [a_spec, b_spec]
    if res is not None:
        operands.append(res[0])
        in_specs.append(res[1])
    if first is not None:
        assert nk > 1
        operands += [first[0], first[2]]
        in_specs += [first[1], first[3]]
    acc_shape = tuple(s for s in out_spec.block_shape if s is not None)
    scratch = [pltpu.VMEM(acc_shape, F32)] if nk > 1 else []
    aliases = {}
    if into is not None:
        aliases = {len(operands): 0}
        operands.append(into)
        in_specs.append(ANY)
    return pl.pallas_call(
        body, name=name, grid=grid, in_specs=in_specs, out_specs=out_spec, out_shape=out_shape, scratch_shapes=scratch,
        input_output_aliases=aliases,
    )(*operands)


def _mm_nn(name, a, b, tm, tn, out_dtype, res=None):
    M, K = a.shape
    N = b.shape[1]
    pairs = [(a, pl.BlockSpec((tm, K), lambda j, i: (i, 0)), b, pl.BlockSpec((K, tn), lambda j, i: (0, j)), NN)]
    r = None if res is None else (res, pl.BlockSpec((tm, tn), lambda j, i: (i, j)))
    return _matmul(name, pairs, (N // tn, M // tm), jax.ShapeDtypeStruct((M, N), out_dtype),
                   pl.BlockSpec((tm, tn), lambda j, i: (i, j)), 1, r)


def _mm_nn_sharded(name, a, b4, tm, out_dtype):
    M, K = a.shape
    n = b4.shape[2]
    pairs = [(a, pl.BlockSpec((tm, K), lambda j, i: (i, 0)), b4, pl.BlockSpec((None, K, n), lambda j, i: (j, 0, 0)), NN)]
    return _matmul(name, pairs, (N_CHIPS, M // tm), jax.ShapeDtypeStruct((M, N_CHIPS * n), out_dtype),
                   pl.BlockSpec((tm, n), lambda j, i: (i, j)), 1)


def _mm_nt(name, a, b, tm, tn, out_dtype, res=None, n_out=None):
    M, K = a.shape
    N = b.shape[0] if n_out is None else n_out
    pairs = [(a, pl.BlockSpec((tm, K), lambda j, i: (i, 0)), b, pl.BlockSpec((tn, K), lambda j, i: (j, 0)), NT)]
    r = None if res is None else (res, pl.BlockSpec((tm, tn), lambda j, i: (i, j)))
    return _matmul(name, pairs, (N // tn, M // tm), jax.ShapeDtypeStruct((M, N), out_dtype),
                   pl.BlockSpec((tm, tn), lambda j, i: (i, j)), 1, r)


def _mm_tn(name, a, g, tka, tn, tmm, out_dtype, out3=None, rows_out=None):
    M, Ka = a.shape
    N = g.shape[1]
    pairs = [(a, pl.BlockSpec((tmm, tka), lambda i, j, k: (k, i)), g, pl.BlockSpec((tmm, tn), lambda i, j, k: (k, j)), TN)]
    if out3 is None:
        shape, spec = (Ka if rows_out is None else rows_out, N), pl.BlockSpec((tka, tn), lambda i, j, k: (i, j))
    else:
        shape, spec = (N // out3, Ka, out3), pl.BlockSpec((None, tka, tn), lambda i, j, k: (j, i, 0))
    return _matmul(name, pairs, (Ka // tka, N // tn, M // tmm), jax.ShapeDtypeStruct(shape, out_dtype), spec, M // tmm)


def _rms_fwd(name, x, g, tm=512):
    S, D = x.shape

    def body(x_ref, g_ref, o_ref):
        xv = x_ref[...]
        r = lax.rsqrt(jnp.mean(xv * xv, axis=-1, keepdims=True) + EPS)
        o_ref[...] = (xv * r * g_ref[...]).astype(o_ref.dtype)

    return pl.pallas_call(
        body, name=name, grid=(S // tm,),
        in_specs=[pl.BlockSpec((tm, D), lambda i: (i, 0)), pl.BlockSpec((1, D), lambda i: (0, 0))],
        out_specs=pl.BlockSpec((tm, D), lambda i: (i, 0)), out_shape=jax.ShapeDtypeStruct((S, D), BF16),
    )(x, g)


def _rms_bwd(name, x, g, dn, dres, tm=512):
    S, D = x.shape

    def body(x_ref, g_ref, dn_ref, dres_ref, dx_ref, dxb_ref, dg_ref):
        i = pl.program_id(0)

        @pl.when(i == 0)
        def _():
            dg_ref[...] = jnp.zeros_like(dg_ref)

        xv = x_ref[...]
        r = lax.rsqrt(jnp.mean(xv * xv, axis=-1, keepdims=True) + EPS)
        xhat = xv * r
        dnv = dn_ref[...].astype(F32)
        dg_ref[...] += jnp.sum(dnv * xhat, axis=0, keepdims=True)
        t = dnv * g_ref[...]
        dx = r * (t - xhat * jnp.mean(t * xhat, axis=-1, keepdims=True)) + dres_ref[...]
        dx_ref[...] = dx
        dxb_ref[...] = dx.astype(BF16)

    row = pl.BlockSpec((tm, D), lambda i: (i, 0))
    vec = pl.BlockSpec((1, D), lambda i: (0, 0))
    return pl.pallas_call(
        body, name=name, grid=(S // tm,), in_specs=[row, vec, row, row], out_specs=[row, row, vec],
        out_shape=[jax.ShapeDtypeStruct((S, D), F32), jax.ShapeDtypeStruct((S, D), BF16), jax.ShapeDtypeStruct((1, D), F32)],
    )(x, g, dn, dres)


def _final_loss(h2, g, target, tm=512):
    S, D = h2.shape

    def body(x_ref, g_ref, t_ref, loss_ref, dg_ref, dx_ref, dxb_ref):
        i = pl.program_id(0)

        @pl.when(i == 0)
        def _():
            loss_ref[...] = jnp.zeros_like(loss_ref)
            dg_ref[...] = jnp.zeros_like(dg_ref)

        xv = x_ref[...]
        r = lax.rsqrt(jnp.mean(xv * xv, axis=-1, keepdims=True) + EPS)
        xhat = xv * r
        gv = g_ref[...]
        diff = xhat * gv - t_ref[...]
        per_tok = jnp.mean(diff * diff, axis=-1, keepdims=True)
        loss_ref[...] += 0.5 * jnp.sum(per_tok, axis=0, keepdims=True)
        dy = diff * (1.0 / D)
        dg_ref[...] += jnp.sum(dy * xhat, axis=0, keepdims=True)
        t = dy * gv
        dx = r * (t - xhat * jnp.mean(t * xhat, axis=-1, keepdims=True))
        dx_ref[...] = dx
        dxb_ref[...] = dx.astype(BF16)

    row = pl.BlockSpec((tm, D), lambda i: (i, 0))
    vec = pl.BlockSpec((1, D), lambda i: (0, 0))
    return pl.pallas_call(
        body, name="final_loss", grid=(S // tm,), in_specs=[row, vec, row],
        out_specs=[pl.BlockSpec((1, LANES), lambda i: (0, 0)), vec, row, row],
        out_shape=[jax.ShapeDtypeStruct((1, LANES), F32), jax.ShapeDtypeStruct((1, D), F32),
                   jax.ShapeDtypeStruct((S, D), F32), jax.ShapeDtypeStruct((S, D), BF16)],
    )(h2, g, target)


def _rope_tables(S):
    pos = jnp.arange(S, dtype=F32)
    inv_freq = ROPE_THETA ** (-jnp.arange(0, ROPE_DIM, 2, dtype=F32) / ROPE_DIM)
    ang = pos[:, None] * inv_freq[None, :]
    cos, sin = jnp.cos(ang), jnp.sin(ang)
    half = ROPE_DIM // 2
    rest = HEAD - ROPE_DIM
    z_h, z_r = jnp.zeros((S, half), F32), jnp.zeros((S, rest), F32)
    tab_c = jnp.concatenate([cos, cos, jnp.ones((S, rest), F32)], axis=1)
    tab_up = jnp.concatenate([z_h, sin, z_r], axis=1)
    tab_dn = jnp.concatenate([-sin, z_h, z_r], axis=1)
    return tab_c, tab_up, tab_dn


def _rope_head(t, c, up, dn):
    half = ROPE_DIM // 2
    return t * c + pltpu.roll(t, half, axis=1) * up + pltpu.roll(t, HEAD - half, axis=1) * dn


def _rope_fwd(proj, tabs, tm=512):
    S = proj.shape[0]
    W = 2 * ATTN_W

    def body(p_ref, c_ref, up_ref, dn_ref, o_ref):
        c, up, dn = c_ref[...], up_ref[...], dn_ref[...]
        for h in range(W // HEAD):
            sl = slice(h * HEAD, (h + 1) * HEAD)
            o_ref[:, sl] = _rope_head(p_ref[:, sl].astype(F32), c, up, dn).astype(BF16)

    tab = pl.BlockSpec((tm, HEAD), lambda i: (i, 0))
    return pl.pallas_call(
        body, name="rope_fwd", grid=(S // tm,), in_specs=[pl.BlockSpec((tm, W), lambda i: (i, 0)), tab, tab, tab],
        out_specs=pl.BlockSpec((tm, W), lambda i: (i, 0)), out_shape=jax.ShapeDtypeStruct((S, W), BF16),
    )(proj, *tabs)


def _attn_grad_merge(dqs, dks, dvs, tabs, tm=256):
    S = dqs[0].shape[0]

    def body(*refs):
        q_refs, k_refs, v_refs = refs[0:3], refs[3:6], refs[6:9]
        c, up, dn = refs[9][...], refs[10][...], refs[11][...]
        o_ref = refs[12]
        for h in range(N_HEADS):
            sl = slice(h * HEAD, (h + 1) * HEAD)
            for part, rs in ((0, q_refs), (1, k_refs)):
                t = rs[0][:, sl].astype(F32) + rs[1][:, sl].astype(F32) + rs[2][:, sl].astype(F32)
                osl = slice(part * ATTN_W + h * HEAD, part * ATTN_W + (h + 1) * HEAD)
                o_ref[:, osl] = _rope_head(t, c, -up, -dn).astype(BF16)
        o_ref[:, 2 * ATTN_W:] = (v_refs[0][...].astype(F32) + v_refs[1][...].astype(F32)
                                 + v_refs[2][...].astype(F32)).astype(BF16)

    blk = pl.BlockSpec((tm, ATTN_W), lambda i: (i, 0))
    tab = pl.BlockSpec((tm, HEAD), lambda i: (i, 0))
    return pl.pallas_call(
        body, name="attn_grad_merge", grid=(S // tm,), in_specs=[blk] * 9 + [tab] * 3,
        out_specs=pl.BlockSpec((tm, 3 * ATTN_W), lambda i: (i, 0)), out_shape=jax.ShapeDtypeStruct((S, IN_MAIN), BF16),
    )(*dqs, *dks, *dvs, *tabs)


SUB = 128
Q_COL, K_COL, V_COL = 0, ATTN_W // HEAD, 2 * ATTN_W // HEAD


class _AttnGeo:
    def __init__(self, S, d):
        self.S, self.d, self.L = S, d, S // d
        self.halo = N_SIDE * d
        self.TB = min(2048, S)
        self.W = self.TB + 2 * self.halo
        self.n_sub = self.TB // SUB
        self.grid = (S // self.TB, N_HEADS)
        self.dt = F32 if d > 1 else BF16
        self.su = min(d, 4)
        self.sb = d // self.su
        assert self.TB % (SUB * d) == 0 and self.TB % self.halo == 0

    def specs(self, width, col0, per_head=True):
        ratio = self.TB // self.halo
        last = self.S // self.halo - 1
        col = (lambda h: col0 + h) if per_head else (lambda h: col0)
        cur = pl.BlockSpec((self.TB, width), lambda i, h: (i, col(h)))
        prev = pl.BlockSpec((self.halo, width), lambda i, h: (jnp.maximum(i * ratio - 1, 0), col(h)))
        nxt = pl.BlockSpec((self.halo, width), lambda i, h: (jnp.minimum((i + 1) * ratio, last), col(h)))
        return cur, prev, nxt

    def scratch(self, rows, dtype=None):
        nat = pltpu.VMEM((rows, LANES), self.dt if dtype is None else dtype)
        return [nat] if self.sb == 1 else [nat, pltpu.VMEM((rows, LANES), F32)]

    def bind(self, refs):
        nat = next(refs)
        return (nat, nat) if self.sb == 1 else (nat, next(refs))

    def spread(self, pair):
        nat, streams = pair
        if self.sb > 1:
            n = nat.shape[0] // self.sb
            for a in range(self.sb):
                streams[a * n:(a + 1) * n, :] = nat[pl.ds(a, n, stride=self.sb), :]
        return streams

    def gather(self, pair):
        nat, streams = pair
        if self.sb > 1:
            n = nat.shape[0] // self.sb
            for a in range(self.sb):
                nat[pl.ds(a, n, stride=self.sb), :] = streams[a * n:(a + 1) * n, :]
        return nat

    def rows(self, sub, n, total):
        res, blk = sub % self.d, sub // self.d
        a, b = res % self.sb, res // self.sb
        start = a * (total // self.sb) + b + self.su * SUB * blk
        return pl.ds(start, n, stride=self.su) if self.su > 1 else pl.ds(start, n)

    def band(self):
        row = lax.broadcasted_iota(jnp.int32, (SUB, 2 * SUB), 0)
        col = lax.broadcasted_iota(jnp.int32, (SUB, 2 * SUB), 1)
        return (col >= row) & (col <= row + 2 * N_SIDE), col

    def mask(self, sub, band):
        inside, col = band
        blk, n_blk = sub // self.d, self.TB // (SUB * self.d)
        base = pl.program_id(0) * (self.TB // self.d) + SUB * blk
        if blk == 0:
            inside = inside & (col >= N_SIDE - base)
        if blk == n_blk - 1:
            inside = inside & (col < self.L + N_SIDE - base)
        return inside

    def fill(self, dst, c_ref):
        dst[...] = c_ref[...].astype(dst.dtype)

    def fill_window(self, dst, p_ref, c_ref, n_ref):
        dst[0:self.halo] = p_ref[...].astype(dst.dtype)
        dst[self.halo:self.halo + self.TB] = c_ref[...].astype(dst.dtype)
        dst[self.halo + self.TB:] = n_ref[...].astype(dst.dtype)


def _lane_of(tile, h):
    lane = lax.broadcasted_iota(jnp.int32, tile.shape, 1)
    return jnp.sum(jnp.where(lane == h, tile, 0.0), axis=1, keepdims=True)


def _attn_fwd(qk, proj, d):
    S = qk.shape[0]
    geo = _AttnGeo(S, d)
    scale = HEAD ** -0.5

    def body(q_ref, kp, kc, kn, vp, vc, vn, o_ref, lse_ref, *scratch):
        h = pl.program_id(1)
        refs = iter(scratch)
        q_p, k_p, v_p, o_p, l_p = (geo.bind(refs) for _ in range(5))
        geo.fill(q_p[0], q_ref)
        geo.fill_window(k_p[0], kp, kc, kn)
        geo.fill_window(v_p[0], vp, vc, vn)
        qs, ks, vs = geo.spread(q_p), geo.spread(k_p), geo.spread(v_p)
        os, ls = o_p[1], l_p[1]
        band = geo.band()
        for sub in range(geo.n_sub):
            rq, rw = geo.rows(sub, SUB, geo.TB), geo.rows(sub, 2 * SUB, geo.W)
            q_r, k_r, v_r = qs[rq, :].astype(BF16), ks[rw, :].astype(BF16), vs[rw, :].astype(BF16)
            s = jnp.where(geo.mask(sub, band), _dot(q_r, k_r, NT) * scale, NEG)
            m = jnp.max(s, axis=1, keepdims=True)
            p = jnp.exp(s - m)
            l = jnp.sum(p, axis=1, keepdims=True)
            os[rq, :] = _dot(p.astype(BF16), v_r) / l
            ls[rq, :] = jnp.broadcast_to(m + jnp.log(l), (SUB, LANES))
        o_ref[...] = geo.gather(o_p)[...].astype(BF16)

        @pl.when(h == 0)
        def _():
            lse_ref[...] = jnp.zeros_like(lse_ref)

        lane = lax.broadcasted_iota(jnp.int32, (geo.TB, LANES), 1)
        lse_ref[...] = jnp.where(lane == h, geo.gather(l_p)[...], lse_ref[...])

    q_cur, _, _ = geo.specs(HEAD, Q_COL)
    k_specs = geo.specs(HEAD, K_COL)
    v_specs = geo.specs(HEAD, V_COL)
    stat = pl.BlockSpec((geo.TB, LANES), lambda i, h: (i, 0))
    return pl.pallas_call(
        body, name=f"attn_fwd_d{d}", grid=geo.grid,
        in_specs=[q_cur, k_specs[1], k_specs[0], k_specs[2], v_specs[1], v_specs[0], v_specs[2]],
        out_specs=[q_cur, stat],
        out_shape=[jax.ShapeDtypeStruct((S, ATTN_W), BF16), jax.ShapeDtypeStruct((S, LANES), F32)],
        scratch_shapes=(geo.scratch(geo.TB) + geo.scratch(geo.W) + geo.scratch(geo.W) + geo.scratch(geo.TB, F32)
                        + geo.scratch(geo.TB, F32)),
    )(qk, qk, qk, qk, proj, proj, proj)


def _attn_combine(outs, lses, g, tm=256):
    S = outs[0].shape[0]

    def body(o1, o2, o3, l1, l2, l3, g_ref, ao_ref, o_ref, lse_ref):
        a1, a2, a3 = l1[...], l2[...], l3[...]
        mx = jnp.maximum(jnp.maximum(a1, a2), a3)
        e1, e2, e3 = jnp.exp(a1 - mx), jnp.exp(a2 - mx), jnp.exp(a3 - mx)
        den = e1 + e2 + e3
        lse_ref[...] = mx + jnp.log(den)
        head_of_col = lax.broadcasted_iota(jnp.int32, (LANES, ATTN_W), 1) // HEAD
        spread = (lax.broadcasted_iota(jnp.int32, (LANES, ATTN_W), 0) == head_of_col).astype(F32)

        def wide(e):
            return lax.dot_general(e / den, spread, (NN, ((), ())), precision=lax.Precision.HIGHEST,
                                   preferred_element_type=F32)

        ov = wide(e1) * o1[...].astype(F32) + wide(e2) * o2[...].astype(F32) + wide(e3) * o3[...].astype(F32)
        o_ref[...] = ov
        r = lax.rsqrt(jnp.mean(ov * ov, axis=-1, keepdims=True) + EPS)
        ao_ref[...] = (ov * r * g_ref[...]).astype(BF16)

    blk = pl.BlockSpec((tm, ATTN_W), lambda i: (i, 0))
    ls = pl.BlockSpec((tm, LANES), lambda i: (i, 0))
    return pl.pallas_call(
        body, name="attn_combine", grid=(S // tm,),
        in_specs=[blk, blk, blk, ls, ls, ls, pl.BlockSpec((1, ATTN_W), lambda i: (0, 0))], out_specs=[blk, blk, ls],
        out_shape=[jax.ShapeDtypeStruct((S, D_MODEL), BF16), jax.ShapeDtypeStruct((S, ATTN_W), F32),
                   jax.ShapeDtypeStruct((S, LANES), F32)],
    )(*outs, *lses, g)


def _attn_norm_bwd(o, g, dao, tm=256):
    S = o.shape[0]

    def body(o_ref, g_ref, dao_ref, do_ref, dl_ref, dg_ref):
        i = pl.program_id(0)

        @pl.when(i == 0)
        def _():
            dg_ref[...] = jnp.zeros_like(dg_ref)

        ov = o_ref[...]
        r = lax.rsqrt(jnp.mean(ov * ov, axis=-1, keepdims=True) + EPS)
        ohat = ov * r
        dn = dao_ref[...].astype(F32)
        dg_ref[...] += jnp.sum(dn * ohat, axis=0, keepdims=True)
        t = dn * g_ref[...]
        do = r * (t - ohat * jnp.mean(t * ohat, axis=-1, keepdims=True))
        do_ref[...] = do.astype(BF16)
        prod = do * ov
        lane = lax.broadcasted_iota(jnp.int32, (tm, LANES), 1)
        tile = jnp.zeros((tm, LANES), F32)
        for h in range(N_HEADS):
            tile = jnp.where(lane == h, jnp.sum(prod[:, h * HEAD:(h + 1) * HEAD], axis=1, keepdims=True), tile)
        dl_ref[...] = tile

    blk = pl.BlockSpec((tm, ATTN_W), lambda i: (i, 0))
    vec = pl.BlockSpec((1, ATTN_W), lambda i: (0, 0))
    return pl.pallas_call(
        body, name="attn_norm_bwd", grid=(S // tm,),
        in_specs=[blk, vec, pl.BlockSpec((tm, ATTN_W), lambda i: (i, 0))],
        out_specs=[blk, pl.BlockSpec((tm, LANES), lambda i: (i, 0)), vec],
        out_shape=[jax.ShapeDtypeStruct((S, ATTN_W), BF16), jax.ShapeDtypeStruct((S, LANES), F32),
                   jax.ShapeDtypeStruct((1, ATTN_W), F32)],
    )(o, g, dao)


def _attn_bwd_dq(qk, proj, do, lse, delta, d):
    S = qk.shape[0]
    geo = _AttnGeo(S, d)
    scale = HEAD ** -0.5

    def body(q_ref, kp, kc, kn, vp, vc, vn, do_ref, lse_ref, dl_ref, dq_ref, *scratch):
        h = pl.program_id(1)
        refs = iter(scratch)
        q_p, k_p, v_p, do_p, lse_p, dl_p, dq_p = (geo.bind(refs) for _ in range(7))
        geo.fill(q_p[0], q_ref)
        geo.fill(do_p[0], do_ref)
        geo.fill(lse_p[0], lse_ref)
        geo.fill(dl_p[0], dl_ref)
        geo.fill_window(k_p[0], kp, kc, kn)
        geo.fill_window(v_p[0], vp, vc, vn)
        qs, ks, vs, dos = geo.spread(q_p), geo.spread(k_p), geo.spread(v_p), geo.spread(do_p)
        lses, dls = geo.spread(lse_p), geo.spread(dl_p)
        dqs = dq_p[1]
        band = geo.band()
        for sub in range(geo.n_sub):
            rq, rw = geo.rows(sub, SUB, geo.TB), geo.rows(sub, 2 * SUB, geo.W)
            q_r, k_r, v_r = qs[rq, :].astype(BF16), ks[rw, :].astype(BF16), vs[rw, :].astype(BF16)
            lse_c, dl_c = _lane_of(lses[rq, :], h), _lane_of(dls[rq, :], h)
            s = _dot(q_r, k_r, NT) * scale
            p = jnp.where(geo.mask(sub, band), jnp.exp(s - lse_c), 0.0)
            dp = _dot(dos[rq, :].astype(BF16), v_r, NT)
            ds = (p * (dp - dl_c) * scale).astype(BF16)
            dqs[rq, :] = _dot(ds, k_r)
        dq_ref[...] = geo.gather(dq_p)[...].astype(BF16)

    cur, _, _ = geo.specs(HEAD, 0)
    k_specs = geo.specs(HEAD, K_COL)
    v_specs = geo.specs(HEAD, V_COL)
    stat = pl.BlockSpec((geo.TB, LANES), lambda i, h: (i, 0))
    return pl.pallas_call(
        body, name=f"attn_bwd_dq_d{d}", grid=geo.grid,
        in_specs=[cur, k_specs[1], k_specs[0], k_specs[2], v_specs[1], v_specs[0], v_specs[2], cur, stat, stat],
        out_specs=cur, out_shape=jax.ShapeDtypeStruct((S, ATTN_W), BF16),
        scratch_shapes=(geo.scratch(geo.TB) + geo.scratch(geo.W) + geo.scratch(geo.W) + geo.scratch(geo.TB)
                        + geo.scratch(geo.TB, F32) + geo.scratch(geo.TB, F32) + geo.scratch(geo.TB, F32)),
    )(qk, qk, qk, qk, proj, proj, proj, do, lse, delta)


def _attn_bwd_dkv(qk, proj, do, lse, delta, d):
    S = qk.shape[0]
    geo = _AttnGeo(S, d)
    scale = HEAD ** -0.5

    def body(k_ref, v_ref, qp, qc, qn, dop, doc, don, lp, lc, ln, dlp, dlc, dln, dk_ref, dv_ref, *scratch):
        h = pl.program_id(1)
        refs = iter(scratch)
        k_p, v_p, q_p, do_p, lw_p, dlw_p, dk_p, dv_p = (geo.bind(refs) for _ in range(8))
        geo.fill(k_p[0], k_ref)
        geo.fill(v_p[0], v_ref)
        geo.fill_window(q_p[0], qp, qc, qn)
        geo.fill_window(do_p[0], dop, doc, don)
        geo.fill_window(lw_p[0], lp, lc, ln)
        geo.fill_window(dlw_p[0], dlp, dlc, dln)
        ks, vs, qs, dos = geo.spread(k_p), geo.spread(v_p), geo.spread(q_p), geo.spread(do_p)
        lws, dlws = geo.spread(lw_p), geo.spread(dlw_p)
        dks, dvs = dk_p[1], dv_p[1]
        head = lax.broadcasted_iota(jnp.int32, (LANES, 2 * SUB), 0)
        band = geo.band()
        for sub in range(geo.n_sub):
            rq, rw = geo.rows(sub, SUB, geo.TB), geo.rows(sub, 2 * SUB, geo.W)
            k_r, v_r = ks[rq, :].astype(BF16), vs[rq, :].astype(BF16)
            q_w, do_w = qs[rw, :].astype(BF16), dos[rw, :].astype(BF16)
            lse_row = jnp.sum(jnp.where(head == h, lws[rw, :].T, 0.0), axis=0, keepdims=True)
            dl_row = jnp.sum(jnp.where(head == h, dlws[rw, :].T, 0.0), axis=0, keepdims=True)
            st = _dot(k_r, q_w, NT) * scale
            pt = jnp.where(geo.mask(sub, band), jnp.exp(st - lse_row), 0.0)
            dvs[rq, :] = _dot(pt.astype(BF16), do_w)
            dpt = _dot(v_r, do_w, NT)
            dst = (pt * (dpt - dl_row) * scale).astype(BF16)
            dks[rq, :] = _dot(dst, q_w)
        dk_ref[...] = geo.gather(dk_p)[...].astype(BF16)
        dv_ref[...] = geo.gather(dv_p)[...].astype(BF16)

    q_specs = geo.specs(HEAD, Q_COL)
    k_cur, _, _ = geo.specs(HEAD, K_COL)
    v_cur, _, _ = geo.specs(HEAD, V_COL)
    do_specs = geo.specs(HEAD, 0)
    st_specs = geo.specs(LANES, 0, per_head=False)
    cur = do_specs[0]
    return pl.pallas_call(
        body, name=f"attn_bwd_dkv_d{d}", grid=geo.grid,
        in_specs=[k_cur, v_cur, q_specs[1], q_specs[0], q_specs[2], do_specs[1], do_specs[0], do_specs[2],
                  st_specs[1], st_specs[0], st_specs[2], st_specs[1], st_specs[0], st_specs[2]],
        out_specs=[cur, cur],
        out_shape=[jax.ShapeDtypeStruct((S, ATTN_W), BF16), jax.ShapeDtypeStruct((S, ATTN_W), BF16)],
        scratch_shapes=(geo.scratch(geo.TB) + geo.scratch(geo.TB) + geo.scratch(geo.W) + geo.scratch(geo.W)
                        + geo.scratch(geo.W, F32) + geo.scratch(geo.W, F32) + geo.scratch(geo.TB, F32)
                        + geo.scratch(geo.TB, F32)),
    )(qk, proj, qk, qk, qk, do, do, do, lse, lse, lse, delta, delta, delta)


def _cumsum_rows(x, reverse):
    n = x.shape[0]
    row = lax.broadcasted_iota(jnp.int32, x.shape, 0)
    s = 1
    while s < n:
        if reverse:
            x = x + jnp.where(row < n - s, pltpu.roll(x, n - s, axis=0), 0.0)
        else:
            x = x + jnp.where(row >= s, pltpu.roll(x, s, axis=0), 0.0)
        s *= 2
    return x


GLA_GROUP = 8


def _gla_rows(cc):
    return slice(cc * CHUNK, (cc + 1) * CHUNK)


def _gla_chunk_terms(q_ref, k_ref, v_ref, g_ref, h, reverse, rows):
    ksl = slice(h * GLA_DK, (h + 1) * GLA_DK)
    q = q_ref[rows, ksl].astype(F32) * (GLA_DK ** -0.5)
    k = k_ref[rows, ksl].astype(F32)
    v = v_ref[rows, h * GLA_DV:(h + 1) * GLA_DV]
    b = _cumsum_rows(g_ref[rows, ksl], reverse)
    r_ref = CHUNK // 2 if reverse else CHUNK // 2 - 1
    r_last = 0 if reverse else CHUNK - 1
    b_ref, b_last = b[r_ref:r_ref + 1, :], b[r_last:r_last + 1, :]
    ii = lax.broadcasted_iota(jnp.int32, (CHUNK, CHUNK), 0)
    jj = lax.broadcasted_iota(jnp.int32, (CHUNK, CHUNK), 1)
    causal = (jj >= ii) if reverse else (jj <= ii)
    e_q, e_k = jnp.exp(b - b_ref), jnp.exp(b_ref - b)
    e_in, e_st = jnp.exp(b), jnp.exp(b_last - b)
    return dict(q=q, k=k, v=v, b=b, causal=causal, e_q=e_q, e_k=e_k, e_in=e_in, e_st=e_st, dec=jnp.exp(b_last),
                qe=q * e_q, ke=k * e_k, q_in=q * e_in, k_st=k * e_st, r_ref=r_ref, r_last=r_last)


def _gla_specs(order):
    rows = GLA_GROUP * CHUNK
    q = pl.BlockSpec((rows, GLA_K), lambda c: (order(c), 3 * ATTN_W // GLA_K))
    k = pl.BlockSpec((rows, GLA_K), lambda c: (order(c), 3 * ATTN_W // GLA_K + 1))
    v = pl.BlockSpec((rows, GLA_V), lambda c: (order(c), (3 * ATTN_W + 2 * GLA_K) // GLA_V))
    return q, k, v


def _gla_fwd(proj, gates, reverse, o_prev=None):
    S = proj.shape[0]
    n = S // CHUNK
    nb = n // GLA_GROUP
    rows = GLA_GROUP * CHUNK
    order = (lambda c: nb - 1 - c) if reverse else (lambda c: c)
    seq = list(range(GLA_GROUP))[::-1] if reverse else list(range(GLA_GROUP))
    gcol = 1 if reverse else 0

    def body(*refs):
        if o_prev is None:
            q_ref, k_ref, v_ref, g_ref, o_ref, st_ref, state = refs
        else:
            q_ref, k_ref, v_ref, g_ref, op_ref, o_ref, st_ref, state = refs
        c = pl.program_id(0)

        @pl.when(c == 0)
        def _():
            state[...] = jnp.zeros_like(state)

        for h in range(GLA_HEADS):
            vsl = slice(h * GLA_DV, (h + 1) * GLA_DV)
            st = state[h]
            for cc in seq:
                rs = _gla_rows(cc)
                t = _gla_chunk_terms(q_ref, k_ref, v_ref, g_ref, h, reverse, rs)
                a = jnp.where(t["causal"], _dot(t["qe"].astype(BF16), t["ke"].astype(BF16), NT), 0.0)
                o = _dot(a.astype(BF16), t["v"])
                st_b = st.astype(BF16)
                st_ref[cc, h] = st_b
                o = o + _dot(t["q_in"].astype(BF16), st_b, NT)
                st = st * t["dec"] + _dot(t["v"], t["k_st"].astype(BF16), TN)
                if o_prev is not None:
                    o = o + op_ref[rs, vsl]
                o_ref[rs, vsl] = o
            state[h] = st

    q_spec, k_spec, v_spec = _gla_specs(order)
    o_spec = pl.BlockSpec((rows, GLA_V), lambda c: (order(c), 0))
    in_specs = [q_spec, k_spec, v_spec, pl.BlockSpec((rows, GLA_K), lambda c: (order(c), gcol))]
    operands = [proj, proj, proj, gates]
    if o_prev is not None:
        in_specs.append(o_spec)
        operands.append(o_prev)
    return pl.pallas_call(
        body, name="gla_fwd_rev" if reverse else "gla_fwd", grid=(nb,), in_specs=in_specs,
        out_specs=[o_spec, pl.BlockSpec((GLA_GROUP, GLA_HEADS, GLA_DV, GLA_DK), lambda c: (order(c), 0, 0, 0))],
        out_shape=[jax.ShapeDtypeStruct((S, GLA_V), F32), jax.ShapeDtypeStruct((n, GLA_HEADS, GLA_DV, GLA_DK), BF16)],
        scratch_shapes=[pltpu.VMEM((GLA_HEADS, GLA_DV, GLA_DK), F32)],
    )(*operands)


def _gla_bwd(proj, gates, states, do, reverse, prev=None):
    S = proj.shape[0]
    n = S // CHUNK
    nb = n // GLA_GROUP
    rows = GLA_GROUP * CHUNK
    order = (lambda c: c) if reverse else (lambda c: nb - 1 - c)
    seq = list(range(GLA_GROUP)) if reverse else list(range(GLA_GROUP))[::-1]
    gcol = 1 if reverse else 0
    out_dt = F32 if prev is None else BF16

    def body(*refs):
        if prev is None:
            q_ref, k_ref, v_ref, g_ref, st_ref, do_ref, dq_ref, dk_ref, dv_ref, dg_ref, dstate = refs
        else:
            q_ref, k_ref, v_ref, g_ref, st_ref, do_ref, pq, pk, pv, dq_ref, dk_ref, dv_ref, dg_ref, dstate = refs
        c = pl.program_id(0)

        @pl.when(c == 0)
        def _():
            dstate[...] = jnp.zeros_like(dstate)

        row = lax.broadcasted_iota(jnp.int32, (CHUNK, GLA_DK), 0)
        for h in range(GLA_HEADS):
            ksl = slice(h * GLA_DK, (h + 1) * GLA_DK)
            vsl = slice(h * GLA_DV, (h + 1) * GLA_DV)
            dst = dstate[h]
            for cc in seq:
                rs = _gla_rows(cc)
                t = _gla_chunk_terms(q_ref, k_ref, v_ref, g_ref, h, reverse, rs)
                v = t["v"]
                dob = do_ref[rs, vsl].astype(BF16)
                st_b = st_ref[cc, h]
                dst_b = dst.astype(BF16)
                qe_b, ke_b = t["qe"].astype(BF16), t["ke"].astype(BF16)
                q_in_b, k_st_b = t["q_in"].astype(BF16), t["k_st"].astype(BF16)
                a = jnp.where(t["causal"], _dot(qe_b, ke_b, NT), 0.0)
                da = jnp.where(t["causal"], _dot(dob, v, NT), 0.0).astype(BF16)
                dv = _dot(a.astype(BF16), dob, TN) + _dot(k_st_b, dst_b, NT)
                dqe = _dot(da, ke_b)
                dke = _dot(da, qe_b, TN)
                dq_in = _dot(dob, st_b)
                dk_st = _dot(v, dst_b)
                ddec = jnp.sum(dst * st_b.astype(F32), axis=0, keepdims=True)
                dst = _dot(dob, q_in_b, TN) + dst * t["dec"]
                dq = (dqe * t["e_q"] + dq_in * t["e_in"]) * (GLA_DK ** -0.5)
                dk = dke * t["e_k"] + dk_st * t["e_st"]
                w_q, w_k = dqe * t["qe"], dke * t["ke"]
                w_st = dk_st * t["k_st"]
                db = w_q - w_k + dq_in * t["q_in"] - w_st
                db_ref = jnp.sum(w_k - w_q, axis=0, keepdims=True)
                db_last = jnp.sum(w_st, axis=0, keepdims=True) + ddec * t["dec"]
                db = db + jnp.where(row == t["r_ref"], db_ref, 0.0) + jnp.where(row == t["r_last"], db_last, 0.0)
                dg_ref[rs, ksl] = _cumsum_rows(db, not reverse)
                if prev is not None:
                    dq, dk, dv = dq + pq[rs, ksl], dk + pk[rs, ksl], dv + pv[rs, vsl]
                dq_ref[rs, ksl] = dq.astype(out_dt)
                dk_ref[rs, ksl] = dk.astype(out_dt)
                dv_ref[rs, vsl] = dv.astype(out_dt)
            dstate[h] = dst

    q_spec, k_spec, v_spec = _gla_specs(order)
    kk = pl.BlockSpec((rows, GLA_K), lambda c: (order(c), 0))
    vv = pl.BlockSpec((rows, GLA_V), lambda c: (order(c), 0))
    in_specs = [q_spec, k_spec, v_spec, pl.BlockSpec((rows, GLA_K), lambda c: (order(c), gcol)),
                pl.BlockSpec((GLA_GROUP, GLA_HEADS, GLA_DV, GLA_DK), lambda c: (order(c), 0, 0, 0)), vv]
    operands = [proj, proj, proj, gates, states, do]
    if prev is not None:
        in_specs += [kk, kk, vv]
        operands += list(prev)
    return pl.pallas_call(
        body, name="gla_bwd_rev" if reverse else "gla_bwd", grid=(nb,), in_specs=in_specs, out_specs=[kk, kk, vv, kk],
        out_shape=[jax.ShapeDtypeStruct((S, GLA_K), out_dt), jax.ShapeDtypeStruct((S, GLA_K), out_dt),
                   jax.ShapeDtypeStruct((S, GLA_V), out_dt), jax.ShapeDtypeStruct((S, GLA_K), F32)],
        scratch_shapes=[pltpu.VMEM((GLA_HEADS, GLA_DV, GLA_DK), F32)],
    )(*operands)


def _gates_fwd(z, wg, bias, tm=512):
    S = z.shape[0]
    W = 2 * GLA_K

    def body(z_ref, w_ref, b_ref, o_ref):
        zg = _dot(z_ref[...], w_ref[...]) + b_ref[...]
        o_ref[...] = (jnp.minimum(zg, 0.0) - jnp.log(1.0 + jnp.exp(-jnp.abs(zg)))) * (1.0 / GATE_NORM)

    return pl.pallas_call(
        body, name="gates_fwd", grid=(S // tm,),
        in_specs=[pl.BlockSpec((tm, Z_W), lambda i: (i, 0)), pl.BlockSpec((Z_W, W), lambda i: (0, 0)),
                  pl.BlockSpec((1, W), lambda i: (0, 0))],
        out_specs=pl.BlockSpec((tm, W), lambda i: (i, 0)), out_shape=jax.ShapeDtypeStruct((S, W), F32),
    )(z, wg, bias)


def _gates_bwd(z, wg, bias, dg_f, dg_b, tm=512):
    S = z.shape[0]
    W = 2 * GLA_K

    def body(z_ref, w_ref, b_ref, dgf_ref, dgb_ref, dz_ref, dw_ref, db_ref):
        i = pl.program_id(0)

        @pl.when(i == 0)
        def _():
            dw_ref[...] = jnp.zeros_like(dw_ref)
            db_ref[...] = jnp.zeros_like(db_ref)

        zv = z_ref[...]
        zg = _dot(zv, w_ref[...]) + b_ref[...]
        dg = jnp.concatenate([dgf_ref[...], dgb_ref[...]], axis=1)
        dzg = dg * (1.0 / GATE_NORM) * _sigmoid(-zg)
        db_ref[...] += jnp.sum(dzg, axis=0, keepdims=True)
        dzg_b = dzg.astype(BF16)
        dw_ref[...] += _dot(zv, dzg_b, TN)
        dz_ref[...] = _dot(dzg_b, w_ref[...], NT).astype(BF16)

    half = pl.BlockSpec((tm, GLA_K), lambda i: (i, 0))
    return pl.pallas_call(
        body, name="gates_bwd", grid=(S // tm,),
        in_specs=[pl.BlockSpec((tm, Z_W), lambda i: (i, 0)), pl.BlockSpec((Z_W, W), lambda i: (0, 0)),
                  pl.BlockSpec((1, W), lambda i: (0, 0)), half, half],
        out_specs=[pl.BlockSpec((tm, Z_W), lambda i: (i, 0)), pl.BlockSpec((Z_W, W), lambda i: (0, 0)),
                   pl.BlockSpec((1, W), lambda i: (0, 0))],
        out_shape=[jax.ShapeDtypeStruct((S, Z_W), BF16), jax.ShapeDtypeStruct((Z_W, W), F32),
                   jax.ShapeDtypeStruct((1, W), F32)],
    )(z, wg, bias, dg_f, dg_b)


def _gla_out_fwd(o, proj, g, cat, tm=512):
    S = o.shape[0]

    def body(o_ref, gr_ref, g_ref, cat_ref, out_ref):
        gn = g_ref[...]
        for h in range(GLA_HEADS):
            sl = slice(h * GLA_DV, (h + 1) * GLA_DV)
            ov = o_ref[:, sl]
            r = lax.rsqrt(jnp.mean(ov * ov, axis=-1, keepdims=True) + EPS)
            gr = gr_ref[:, sl].astype(F32)
            out_ref[:, sl] = (ov * r * gn * (gr * _sigmoid(gr))).astype(BF16)

    blk = pl.BlockSpec((tm, GLA_V), lambda i: (i, 0))
    return pl.pallas_call(
        body, name="gla_out_fwd", grid=(S // tm,),
        in_specs=[blk, pl.BlockSpec((tm, GLA_V), lambda i: (i, (3 * ATTN_W + 2 * GLA_K + GLA_V) // GLA_V)),
                  pl.BlockSpec((1, GLA_DV), lambda i: (0, 0)), ANY],
        out_specs=pl.BlockSpec((tm, GLA_V), lambda i: (i, 1)), out_shape=jax.ShapeDtypeStruct((S, D_MODEL), BF16),
        input_output_aliases={3: 0},
    )(o, proj, g, cat)


def _gla_out_bwd(o, proj, g, dcat, dproj, tm=512):
    S = o.shape[0]

    def body(o_ref, gr_ref, g_ref, dgo_ref, dproj_ref, do_ref, dgr_ref, dg_ref):
        i = pl.program_id(0)

        @pl.when(i == 0)
        def _():
            dg_ref[...] = jnp.zeros_like(dg_ref)

        gn = g_ref[...]
        dg_acc = jnp.zeros((1, GLA_DV), F32)
        for h in range(GLA_HEADS):
            sl = slice(h * GLA_DV, (h + 1) * GLA_DV)
            ov = o_ref[:, sl]
            r = lax.rsqrt(jnp.mean(ov * ov, axis=-1, keepdims=True) + EPS)
            yhat = ov * r
            gr = gr_ref[:, sl].astype(F32)
            sg = _sigmoid(gr)
            dgo = dgo_ref[:, sl].astype(F32)
            dgr_ref[:, sl] = (dgo * (yhat * gn) * (sg * (1.0 + gr * (1.0 - sg)))).astype(BF16)
            dy = dgo * (gr * sg)
            dg_acc = dg_acc + jnp.sum(dy * yhat, axis=0, keepdims=True)
            t = dy * gn
            do_ref[:, sl] = r * (t - yhat * jnp.mean(t * yhat, axis=-1, keepdims=True))
        dg_ref[...] += dg_acc

    blk = pl.BlockSpec((tm, GLA_V), lambda i: (i, 0))
    vec = pl.BlockSpec((1, GLA_DV), lambda i: (0, 0))
    return pl.pallas_call(
        body, name="gla_out_bwd", grid=(S // tm,),
        in_specs=[blk, pl.BlockSpec((tm, GLA_V), lambda i: (i, (3 * ATTN_W + 2 * GLA_K + GLA_V) // GLA_V)), vec,
                  pl.BlockSpec((tm, GLA_V), lambda i: (i, 1)), ANY],
        out_specs=[blk, pl.BlockSpec((tm, GLA_V), lambda i: (i, (3 * ATTN_W + 2 * GLA_K + GLA_V) // GLA_V)), vec],
        out_shape=[jax.ShapeDtypeStruct((S, GLA_V), F32), jax.ShapeDtypeStruct((S, IN_MAIN), BF16),
                   jax.ShapeDtypeStruct((1, GLA_DV), F32)],
        input_output_aliases={4: 1},
    )(o, proj, g, dcat, dproj)


HALO = 16


def _halo_specs(tm, tn, S):
    cur = pl.BlockSpec((tm, tn), lambda j, i: (i, j))
    prev = pl.BlockSpec((HALO, tn), lambda j, i: (jnp.maximum(i * (tm // HALO) - 1, 0), j))
    nxt = pl.BlockSpec((HALO, tn), lambda j, i: (jnp.minimum((i + 1) * (tm // HALO), S // HALO - 1), j))
    return cur, prev, nxt


def _shifted(c_ref, p_ref, n_ref, n_blocks, i=None):
    if i is None:
        i = pl.program_id(1)
    x = c_ref[...].astype(F32)
    tm = x.shape[0]
    row = lax.broadcasted_iota(jnp.int32, x.shape, 0)
    before = p_ref[HALO - 1:HALO, :].astype(F32) * (i > 0).astype(F32)
    after = n_ref[0:1, :].astype(F32) * (i < n_blocks - 1).astype(F32)
    x_m1 = jnp.where(row == 0, before, pltpu.roll(x, 1, axis=0))
    x_p1 = jnp.where(row == tm - 1, after, pltpu.roll(x, tm - 1, axis=0))
    return x, x_m1, x_p1


def _glu_fwd(gp, up, cw, cb, tm=512, tn=1408):
    S = gp.shape[0]
    nb = S // tm

    def body(c_ref, p_ref, n_ref, up_ref, w_ref, b_ref, o_ref):
        x, x_m1, x_p1 = _shifted(c_ref, p_ref, n_ref, nb)
        w = w_ref[...]
        gate = w[0:1, :] * x_m1 + w[1:2, :] * x + w[2:3, :] * x_p1 + b_ref[...]
        o_ref[...] = (gate * _sigmoid(gate) * up_ref[...].astype(F32)).astype(BF16)

    cur, prev, nxt = _halo_specs(tm, tn, S)
    return pl.pallas_call(
        body, name="glu_fwd", grid=(D_FF // tn, nb),
        in_specs=[cur, prev, nxt, cur, pl.BlockSpec((3, tn), lambda j, i: (0, j)), pl.BlockSpec((1, tn), lambda j, i: (0, j))],
        out_specs=cur, out_shape=jax.ShapeDtypeStruct((S, D_FF), BF16),
    )(gp, gp, gp, up, cw, cb)


def _glu_bwd(gp, up, dact, cw, cb, tm=512, tn=1408):
    S = gp.shape[0]
    nb = S // tm

    def body(c_ref, p_ref, n_ref, up_ref, upp_ref, upn_ref, da_ref, dap_ref, dan_ref, w_ref, b_ref,
             dup_ref, dgp_ref, dw_ref, db_ref):
        i = pl.program_id(1)

        @pl.when(i == 0)
        def _():
            dw_ref[...] = jnp.zeros_like(dw_ref)
            db_ref[...] = jnp.zeros_like(db_ref)

        x, x_m1, x_p1 = _shifted(c_ref, p_ref, n_ref, nb)
        w = w_ref[...]
        w0, w1, w2, b = w[0:1, :], w[1:2, :], w[2:3, :], b_ref[...]

        def d_gate(gate, da, upv):
            sg = _sigmoid(gate)
            return sg, da * upv * (sg * (1.0 + gate * (1.0 - sg)))

        gate = w0 * x_m1 + w1 * x + w2 * x_p1 + b
        da = da_ref[...].astype(F32)
        sg, dgate = d_gate(gate, da, up_ref[...].astype(F32))
        dup_ref[...] = (da * (gate * sg)).astype(BF16)
        db_ref[...] += jnp.sum(dgate, axis=0, keepdims=True)
        dw_ref[...] += jnp.concatenate(
            [jnp.sum(dgate * x_m1, axis=0, keepdims=True), jnp.sum(dgate * x, axis=0, keepdims=True),
             jnp.sum(dgate * x_p1, axis=0, keepdims=True)], axis=0)

        pv, nv = p_ref[...].astype(F32), n_ref[...].astype(F32)
        gate_before = w0 * pv[HALO - 2:HALO - 1, :] + w1 * pv[HALO - 1:HALO, :] + w2 * x[0:1, :] + b
        _, dgate_before = d_gate(gate_before, dap_ref[...].astype(F32)[HALO - 1:HALO, :], upp_ref[...].astype(F32)[HALO - 1:HALO, :])
        gate_after = w0 * x[tm - 1:tm, :] + w1 * nv[0:1, :] + w2 * nv[1:2, :] + b
        _, dgate_after = d_gate(gate_after, dan_ref[...].astype(F32)[0:1, :], upn_ref[...].astype(F32)[0:1, :])
        dgate_before = dgate_before * (i > 0).astype(F32)
        dgate_after = dgate_after * (i < nb - 1).astype(F32)
        row = lax.broadcasted_iota(jnp.int32, dgate.shape, 0)
        dg_m1 = jnp.where(row == 0, dgate_before, pltpu.roll(dgate, 1, axis=0))
        dg_p1 = jnp.where(row == tm - 1, dgate_after, pltpu.roll(dgate, tm - 1, axis=0))
        dgp_ref[...] = (w0 * dg_p1 + w1 * dgate + w2 * dg_m1).astype(BF16)

    cur, prev, nxt = _halo_specs(tm, tn, S)
    w_spec = pl.BlockSpec((3, tn), lambda j, i: (0, j))
    b_spec = pl.BlockSpec((1, tn), lambda j, i: (0, j))
    return pl.pallas_call(
        body, name="glu_bwd", grid=(D_FF // tn, nb),
        in_specs=[cur, prev, nxt, cur, prev, nxt, cur, prev, nxt, w_spec, b_spec],
        out_specs=[cur, cur, w_spec, b_spec],
        out_shape=[jax.ShapeDtypeStruct((S, D_FF), BF16), jax.ShapeDtypeStruct((S, D_FF), BF16),
                   jax.ShapeDtypeStruct((3, D_FF), F32), jax.ShapeDtypeStruct((1, D_FF), F32)],
    )(gp, gp, gp, up, up, up, dact, dact, dact, cw, cb)


def _local_step(x, target, norm1_g, w_in_t, wg, gate_bias, gla_norm_g, attn_norm_g, w_out, norm2_g,
                w_gate4, w_up4, conv_w, conv_b, w_down, final_norm_g, on_grad=lambda event, arrays: ()):
    S = x.shape[0]
    tabs = _rope_tables(S)

    n1 = _rms_fwd("rms1_fwd", x, norm1_g)
    z_block = IN_MAIN // Z_W
    proj = _mm_nt("in_proj", n1, w_in_t, 1024, 1536, BF16, n_out=IN_MAIN)
    z = _matmul(
        "in_proj_z",
        [(n1, pl.BlockSpec((1024, D_MODEL), lambda i: (i, 0)), w_in_t, pl.BlockSpec((Z_W, D_MODEL), lambda i: (z_block, 0)), NT)],
        (S // 1024,), jax.ShapeDtypeStruct((S, Z_W), BF16), pl.BlockSpec((1024, Z_W), lambda i: (i, 0)), 1)
    qk = _rope_fwd(proj, tabs)
    branch = [_attn_fwd(qk, proj, d) for d in DILATIONS]
    ao, o_attn, lse = _attn_combine([b[0] for b in branch], [b[1] for b in branch], attn_norm_g)
    gates = _gates_fwd(z, wg, gate_bias)
    o_f, st_f = _gla_fwd(proj, gates, False)
    o_gla, st_b = _gla_fwd(proj, gates, True, o_prev=o_f)
    cat = _gla_out_fwd(o_gla, proj, gla_norm_g, ao)
    h1 = _mm_nn("out_proj", cat, w_out, 1024, 1024, F32, res=x)
    n2 = _rms_fwd("rms2_fwd", h1, norm2_g)
    gp = _mm_nn_sharded("ffn_gate", n2, w_gate4, 1024, BF16)
    up = _mm_nn_sharded("ffn_up", n2, w_up4, 1024, BF16)
    act = _glu_fwd(gp, up, conv_w, conv_b)
    tk = D_FF // N_CHIPS
    h2 = _mm_nn("ffn_down", act, w_down, 1024, 512, F32, res=h1)
    loss_row, d_final_g, dh2, dh2_b = _final_loss(h2, final_norm_g.reshape(1, D_MODEL), target)

    dact = _mm_nt("ffn_down_bwd", dh2_b, w_down, 1024, tk, BF16)
    dup, dgp, d_conv_w, d_conv_b = _glu_bwd(gp, up, dact, conv_w, conv_b)
    d_w_down = _mm_tn("ffn_down_wgrad", act, dh2_b, 512, D_MODEL, 2048, BF16)
    on_grad("w_down", dict(w_down=d_w_down))
    dgp = _after(dgp, d_w_down)
    d_w_gate4 = _mm_tn("ffn_gate_wgrad", n2, dgp, 1024, tk, 2048, BF16, out3=tk)
    dup = _after(dup, d_w_gate4)
    d_w_up4 = _mm_tn("ffn_up_wgrad", n2, dup, 1024, tk, 2048, BF16, out3=tk)
    held = on_grad("w_gate_w_up", dict(w_gate=d_w_gate4, w_up=d_w_up4))
    dgp = _after(dgp, d_w_up4, *held)
    shard_pairs = [
        (g, pl.BlockSpec((512, tk), functools.partial(lambda s, j, i: (i, s), s)),
         w4, pl.BlockSpec((None, 512, tk), functools.partial(lambda s, j, i: (s, j, 0), s)), NT)
        for g, w4 in ((dgp, w_gate4), (dup, w_up4)) for s in range(N_CHIPS)]
    dn2 = _matmul("ffn_in_bwd", shard_pairs, (D_MODEL // 512, S // 512), jax.ShapeDtypeStruct((S, D_MODEL), F32),
                  pl.BlockSpec((512, 512), lambda j, i: (i, j)), 1)
    dh1, dh1_b, d_norm2_g = _rms_bwd("rms2_bwd", h1, norm2_g, dn2, dh2)

    d_w_out = _mm_tn("out_proj_wgrad", cat, dh1_b, D_MODEL, 1024, 1024, BF16)
    held = on_grad("w_out", dict(w_out=d_w_out))
    dcat = _mm_nt("out_proj_bwd", _after(dh1_b, d_w_out, *held), w_out, 1024, 1024, BF16)
    do_attn, delta, d_attn_norm_g = _attn_norm_bwd(o_attn, attn_norm_g, dcat)
    dqs, dks, dvs = [], [], []
    for d in DILATIONS:
        dqs.append(_attn_bwd_dq(qk, proj, do_attn, lse, delta, d))
        dk, dv = _attn_bwd_dkv(qk, proj, do_attn, lse, delta, d)
        dks.append(dk)
        dvs.append(dv)
    dproj = _attn_grad_merge(dqs, dks, dvs, tabs)
    held = on_grad("mid", dict(anchor=dproj))
    do_gla, dproj, d_gla_norm_g = _gla_out_bwd(o_gla, proj, gla_norm_g, _after(dcat, *held), dproj)
    dq_f, dk_f, dv_f, dg_f = _gla_bwd(proj, gates, st_f, do_gla, False)
    dgq, dgk, dgv, dg_b = _gla_bwd(proj, gates, st_b, do_gla, True, prev=(dq_f, dk_f, dv_f))
    dz, d_wg, d_gate_bias = _gates_bwd(z, wg, gate_bias, dg_f, dg_b)
    dproj = lax.dynamic_update_slice(dproj, jnp.concatenate([dgq, dgk, dgv], axis=1), (0, 3 * ATTN_W))
    d_w_in_t = _mm_tn("in_proj_wgrad", dproj, n1, 768, D_MODEL, 2048, BF16, rows_out=IN_W)
    n_tok = S // 1024
    d_w_in_t = _matmul(
        "in_proj_z_wgrad",
        [(dz, pl.BlockSpec((1024, Z_W), lambda i, j, k: (k, 0)), n1, pl.BlockSpec((1024, D_MODEL), lambda i, j, k: (k, 0)), TN)],
        (1, 1, n_tok), jax.ShapeDtypeStruct((IN_W, D_MODEL), BF16), pl.BlockSpec((Z_W, D_MODEL), lambda i, j, k: (z_block, 0)),
        n_tok, into=d_w_in_t)
    held = on_grad("w_in", dict(w_in_t=d_w_in_t))
    half = S // 2048

    def in_proj_bwd(name, first, a, into):
        return _matmul(
            name,
            [(a, pl.BlockSpec((1024, IN_MAIN), lambda j, i: (i + first, 0)), w_in_t, pl.BlockSpec((IN_MAIN, 512), lambda j, i: (0, j)), NN),
             (dz, pl.BlockSpec((1024, Z_W), lambda j, i: (i + first, 0)), w_in_t, pl.BlockSpec((Z_W, 512), lambda j, i: (z_block, j)), NN)],
            (D_MODEL // 512, half), jax.ShapeDtypeStruct((S, D_MODEL), F32),
            pl.BlockSpec((1024, 512), lambda j, i: (i + first, j)), 1, into=into)

    dproj = _after(dproj, d_w_in_t, *held)
    dn1 = in_proj_bwd("in_proj_bwd_a", 0, dproj, None)
    held = on_grad("last", dict(last=dn1))
    dn1 = in_proj_bwd("in_proj_bwd_b", half, dproj, _after(dn1, *held))
    grad_x, _, d_norm1_g = _rms_bwd("rms1_bwd", x, norm1_g, dn1, dh1)

    big = dict(w_in_t=d_w_in_t, w_out=d_w_out, w_gate4=d_w_gate4, w_up4=d_w_up4, w_down=d_w_down)
    small = dict(loss=loss_row, norm1_g=d_norm1_g, wg=d_wg, gate_bias=d_gate_bias, gla_norm_g=d_gla_norm_g,
                 attn_norm_g=d_attn_norm_g, norm2_g=d_norm2_g, conv_w=d_conv_w, conv_b=d_conv_b, final_norm_g=d_final_g)
    return grad_x, big, small


def _position():
    return lax.axis_index("x"), lax.axis_index("y"), lax.axis_index("c")


def _other_chips(x, y):
    return [(1 - x, y), (x, 1 - y), (1 - x, 1 - y)]


def _gather_chips_async(name, shards, collective_id):
    n = len(shards)

    def body(*refs):
        ins, outs = refs[:n], refs[n:2 * n]
        send, recv, loc = refs[2 * n:]
        x, y, c = _position()
        me = 2 * x + y
        chips = _other_chips(x, y)
        barrier = pltpu.get_barrier_semaphore()
        for px, py in chips:
            pl.semaphore_signal(barrier, inc=1, device_id=(px, py, c), device_id_type=MESH)
        pl.semaphore_wait(barrier, len(chips))
        started = []
        for w in range(n):
            own = pltpu.make_async_copy(ins[w], outs[w].at[me], loc.at[w])
            own.start()
            started.append(own)
        sends = []
        for w in range(n):
            for j, (px, py) in enumerate(chips):
                cp = pltpu.make_async_remote_copy(ins[w], outs[w].at[me], send.at[3 * w + j], recv.at[3 * w + j],
                                                  device_id=(px, py, c), device_id_type=MESH)
                cp.start()
                sends.append(cp)
        for w in range(n):
            for j, (px, py) in enumerate(chips):
                pltpu.make_async_remote_copy(ins[w], outs[w].at[2 * px + py], send.at[3 * w + j], recv.at[3 * w + j],
                                             device_id=(px, py, c), device_id_type=MESH).wait_recv()
        for cp in sends:
            cp.wait_send()
        for own in started:
            own.wait()

    return pl.kernel(
        body, name=name, mesh=_sequencer(),
        out_type=[jax.ShapeDtypeStruct((N_CHIPS,) + s.shape, s.dtype) for s in shards],
        scratch_types=[pltpu.SemaphoreType.DMA((3 * n,)), pltpu.SemaphoreType.DMA((3 * n,)), pltpu.SemaphoreType.DMA((n,))],
        compiler_params=pltpu.CompilerParams(collective_id=collective_id),
    )(*shards)


def _gather_halves_async(name, small, shard, collective_id):
    half = shard.shape[1] // 2

    def body(small_ref, shard_ref, small_out, out, send, recv, loc):
        x, y, c = _position()
        me = 2 * x + y
        sibling = (x, y, 1 - c)
        chips = _other_chips(x, y)
        barrier = pltpu.get_barrier_semaphore()
        for px, py in chips:
            pl.semaphore_signal(barrier, inc=1, device_id=(px, py, c), device_id_type=MESH)
        pl.semaphore_signal(barrier, inc=1, device_id=sibling, device_id_type=MESH)
        pl.semaphore_wait(barrier, len(chips) + 1)
        mine = pl.ds(pl.multiple_of(c * half, LANES), half)
        theirs = pl.ds(pl.multiple_of((1 - c) * half, LANES), half)
        own = [pltpu.make_async_copy(small_ref, small_out.at[me], loc.at[0]),
               pltpu.make_async_copy(shard_ref, out.at[me], loc.at[1])]
        for cp in own:
            cp.start()
        sends = []
        for j, (px, py) in enumerate(chips):
            sends.append(pltpu.make_async_remote_copy(small_ref, small_out.at[me], send.at[j], recv.at[j],
                                                      device_id=(px, py, c), device_id_type=MESH))
            sends.append(pltpu.make_async_remote_copy(shard_ref.at[:, mine], out.at[me, :, mine], send.at[3 + j], recv.at[3 + j],
                                                      device_id=(px, py, c), device_id_type=MESH))
        for cp in sends:
            cp.start()
        passed = []
        for j, (px, py) in enumerate(chips):
            slot = 2 * px + py
            pltpu.make_async_remote_copy(shard_ref.at[:, mine], out.at[slot, :, mine], send.at[3 + j], recv.at[3 + j],
                                         device_id=(px, py, c), device_id_type=MESH).wait_recv()
            cp = pltpu.make_async_remote_copy(out.at[slot, :, mine], out.at[slot, :, mine], send.at[6 + j], recv.at[6 + j],
                                              device_id=sibling, device_id_type=MESH)
            cp.start()
            passed.append(cp)
        for j, (px, py) in enumerate(chips):
            slot = 2 * px + py
            pltpu.make_async_remote_copy(small_ref, small_out.at[slot], send.at[j], recv.at[j],
                                         device_id=(px, py, c), device_id_type=MESH).wait_recv()
            pltpu.make_async_remote_copy(out.at[slot, :, theirs], out.at[slot, :, theirs], send.at[6 + j], recv.at[6 + j],
                                         device_id=sibling, device_id_type=MESH).wait_recv()
        for cp in sends + passed:
            cp.wait_send()
        for cp in own:
            cp.wait()

    return pl.kernel(
        body, name=name, mesh=_sequencer(),
        out_type=[jax.ShapeDtypeStruct((N_CHIPS,) + small.shape, small.dtype),
                  jax.ShapeDtypeStruct((N_CHIPS,) + shard.shape, shard.dtype)],
        scratch_types=[pltpu.SemaphoreType.DMA((9,)), pltpu.SemaphoreType.DMA((9,)), pltpu.SemaphoreType.DMA((2,))],
        compiler_params=pltpu.CompilerParams(collective_id=collective_id),
    )(small, shard)


def _sequencer():
    return plsc.ScalarSubcoreMesh(axis_name="sequencer", num_cores=1)


def _sibling_exchange_async(name, arrs, collective_id):
    n = len(arrs)

    def body(*refs):
        ins, outs = refs[:n], refs[n:2 * n]
        send, recv = refs[2 * n:]
        x, y, c = _position()
        sibling = (x, y, 1 - c)
        barrier = pltpu.get_barrier_semaphore()
        pl.semaphore_signal(barrier, inc=1, device_id=sibling, device_id_type=MESH)
        pl.semaphore_wait(barrier, 1)
        copies = [pltpu.make_async_remote_copy(ins[w], outs[w], send.at[w], recv.at[w], device_id=sibling,
                                               device_id_type=MESH) for w in range(n)]
        for cp in copies:
            cp.start()
        for cp in copies:
            cp.wait()

    return pl.kernel(
        body, name=name, out_type=[jax.ShapeDtypeStruct(a.shape, a.dtype) for a in arrs],
        scratch_types=[pltpu.SemaphoreType.DMA((n,)), pltpu.SemaphoreType.DMA((n,))],
        compiler_params=pltpu.CompilerParams(collective_id=collective_id), mesh=_sequencer(),
    )(*arrs)


def _scatter_chips_async(name, parts, collective_id):
    n = len(parts)

    def body(*refs):
        ins, outs = refs[:n], refs[n:2 * n]
        send, recv, loc = refs[2 * n:]
        x, y, c = _position()
        me = 2 * x + y
        chips = _other_chips(x, y)
        barrier = pltpu.get_barrier_semaphore()
        for px, py in chips:
            pl.semaphore_signal(barrier, inc=1, device_id=(px, py, c), device_id_type=MESH)
        pl.semaphore_wait(barrier, len(chips))
        started = []
        for w in range(n):
            own = pltpu.make_async_copy(ins[w].at[me], outs[w].at[me], loc.at[w])
            own.start()
            started.append(own)
        sends = []
        for w in range(n):
            for j, (px, py) in enumerate(chips):
                cp = pltpu.make_async_remote_copy(ins[w].at[2 * px + py], outs[w].at[me], send.at[3 * w + j],
                                                  recv.at[3 * w + j], device_id=(px, py, c), device_id_type=MESH)
                cp.start()
                sends.append(cp)
        for w in range(n):
            for j, (px, py) in enumerate(chips):
                pltpu.make_async_remote_copy(ins[w].at[me], outs[w].at[2 * px + py], send.at[3 * w + j], recv.at[3 * w + j],
                                             device_id=(px, py, c), device_id_type=MESH).wait_recv()
        for cp in sends:
            cp.wait_send()
        for own in started:
            own.wait()

    return pl.kernel(
        body, name=name, out_type=[jax.ShapeDtypeStruct(p.shape, p.dtype) for p in parts],
        scratch_types=[pltpu.SemaphoreType.DMA((3 * n,)), pltpu.SemaphoreType.DMA((3 * n,)), pltpu.SemaphoreType.DMA((n,))],
        compiler_params=pltpu.CompilerParams(collective_id=collective_id), mesh=_sequencer(),
    )(*parts)


def _allreduce_rows(buf):
    R = buf.shape[0]

    def body(in_ref, out_ref, land, send, recv):
        x, y, c = _position()
        me = 4 * x + 2 * y + c
        land[pl.ds(me, 1)] = in_ref[...][None]
        peers = []
        for mask in range(1, N_DEV):
            px = 1 - x if mask & 4 else x
            py = 1 - y if mask & 2 else y
            pc = 1 - c if mask & 1 else c
            peers.append((px, py, pc))
        sends = []
        for k, peer in enumerate(peers):
            cp = pltpu.make_async_remote_copy(in_ref, land.at[me], send.at[k], recv.at[k], device_id=peer, device_id_type=MESH)
            cp.start()
            sends.append(cp)
        for k, (px, py, pc) in enumerate(peers):
            pltpu.make_async_remote_copy(in_ref, land.at[4 * px + 2 * py + pc], send.at[k], recv.at[k],
                                         device_id=(px, py, pc), device_id_type=MESH).wait_recv()
        for cp in sends:
            cp.wait_send()
        tot = land[0]
        for i in range(1, N_DEV):
            tot = tot + land[i]
        out_ref[...] = tot

    vm = pl.BlockSpec(memory_space=pltpu.VMEM)
    return pl.pallas_call(
        body, name="allreduce_small", in_specs=[vm], out_specs=vm, out_shape=jax.ShapeDtypeStruct((R, LANES), F32),
        scratch_shapes=[pltpu.VMEM((N_DEV, R, LANES), F32), pltpu.SemaphoreType.DMA((N_DEV - 1,)),
                        pltpu.SemaphoreType.DMA((N_DEV - 1,))],
    )(buf)


def _tile2d(r, c, cap):
    if r <= cap:
        return r, c
    fits = [t for t in range(16, cap + 1, 16) if r % t == 0]
    return (max(fits), c) if fits else (r, 256)


def _pair_sum(name, a, b):
    n, r, c = a.shape
    tr, tc = _tile2d(r, c, 1024)

    def body(a_ref, b_ref, o_ref):
        o_ref[...] = (a_ref[...].astype(F32) + b_ref[...].astype(F32)).astype(BF16)

    blk = pl.BlockSpec((None, tr, tc), lambda s, i, j: (s, i, j))
    return pl.pallas_call(
        body, name=name, grid=(n, r // tr, c // tc), in_specs=[blk, blk], out_specs=blk,
        out_shape=jax.ShapeDtypeStruct(a.shape, BF16),
    )(a, b)


def _adamw_math(w, m, v, g):
    m2 = ADAM_B1 * m + (1.0 - ADAM_B1) * g
    v2 = ADAM_B2 * v + (1.0 - ADAM_B2) * (g * g)
    m_hat = m2 / (1.0 - ADAM_B1 ** ADAM_STEP)
    v_hat = v2 / (1.0 - ADAM_B2 ** ADAM_STEP)
    delta = -ADAM_LR * (m_hat / (jnp.sqrt(v_hat) + ADAM_EPS) + ADAM_WD * w)
    return delta, m2, v2


def _adamw(name, w, m, v, g):
    r, c = w.shape
    stacked = g.ndim == 3
    tr, tc = _tile2d(r, c, 256)

    def body(w_ref, m_ref, v_ref, g_ref, go_ref, d_ref, m2_ref, v2_ref):
        if stacked:
            gv = g_ref[0].astype(F32)
            for i in range(1, N_CHIPS):
                gv = gv + g_ref[i].astype(F32)
        else:
            gv = g_ref[...]
        delta, m2, v2 = _adamw_math(w_ref[...], m_ref[...], v_ref[...], gv)
        go_ref[...] = gv
        d_ref[...] = delta
        m2_ref[...] = m2
        v2_ref[...] = v2

    blk = pl.BlockSpec((tr, tc), lambda i, j: (i, j))
    g_spec = pl.BlockSpec((N_CHIPS, tr, tc), lambda i, j: (0, i, j)) if stacked else blk
    out = jax.ShapeDtypeStruct((r, c), F32)
    return pl.pallas_call(
        body, name=name, grid=(r // tr, c // tc), in_specs=[blk, blk, blk, g_spec], out_specs=[blk] * 4, out_shape=[out] * 4,
    )(w, m, v, g)


def _pack_rows(pieces):
    flat = jnp.concatenate([p.reshape(-1) for p in pieces])
    rows = flat.shape[0] // LANES
    pad = (-rows) % 8
    return jnp.pad(flat.reshape(rows, LANES), ((0, pad), (0, 0)))


def _unpack_rows(buf, shapes):
    flat = buf.reshape(-1)
    out, at = [], 0
    for s in shapes:
        size = math.prod(s)
        out.append(flat[at:at + size].reshape(s))
        at += size
    return out


SMALL_NAMES = ("norm1_g", "gf_up", "gf_b", "gb_up", "gb_b", "gla_norm_g", "attn_norm_g", "norm2_g", "conv_w", "conv_b",
               "final_norm_g")
BIG_NAMES = ("w_in", "w_out", "w_gate", "w_up", "w_down")
WEIGHT_ORDER = ("norm1_g", "w_in", "gf_up", "gf_b", "gb_up", "gb_b", "gla_norm_g", "attn_norm_g", "w_out", "norm2_g",
                "w_gate", "w_up", "conv_w", "conv_b", "w_down", "final_norm_g")


def kernel(x, norm1_g, w_in, gf_up, gf_b, gb_up, gb_b, gla_norm_g, attn_norm_g, w_out, norm2_g, w_gate, w_up, conv_w, conv_b, w_down, final_norm_g, loss_target, m_norm1_g, m_w_in, m_gf_up, m_gf_b, m_gb_up, m_gb_b, m_gla_norm_g, m_attn_norm_g, m_w_out, m_norm2_g, m_w_gate, m_w_up, m_conv_w, m_conv_b, m_w_down, m_final_norm_g, v_norm1_g, v_w_in, v_gf_up, v_gf_b, v_gb_up, v_gb_b, v_gla_norm_g, v_attn_norm_g, v_w_out, v_norm2_g, v_w_gate, v_w_up, v_conv_w, v_conv_b, v_w_down, v_final_norm_g):
    w = dict(norm1_g=norm1_g, w_in=w_in, gf_up=gf_up, gf_b=gf_b, gb_up=gb_up, gb_b=gb_b, gla_norm_g=gla_norm_g,
             attn_norm_g=attn_norm_g, w_out=w_out, norm2_g=norm2_g, w_gate=w_gate, w_up=w_up, conv_w=conv_w, conv_b=conv_b,
             w_down=w_down, final_norm_g=final_norm_g)
    m = dict(norm1_g=m_norm1_g, w_in=m_w_in, gf_up=m_gf_up, gf_b=m_gf_b, gb_up=m_gb_up, gb_b=m_gb_b, gla_norm_g=m_gla_norm_g,
             attn_norm_g=m_attn_norm_g, w_out=m_w_out, norm2_g=m_norm2_g, w_gate=m_w_gate, w_up=m_w_up, conv_w=m_conv_w,
             conv_b=m_conv_b, w_down=m_w_down, final_norm_g=m_final_norm_g)
    v = dict(norm1_g=v_norm1_g, w_in=v_w_in, gf_up=v_gf_up, gf_b=v_gf_b, gb_up=v_gb_up, gb_b=v_gb_b, gla_norm_g=v_gla_norm_g,
             attn_norm_g=v_attn_norm_g, w_out=v_w_out, norm2_g=v_norm2_g, w_gate=v_w_gate, w_up=v_w_up, conv_w=v_conv_w,
             conv_b=v_conv_b, w_down=v_w_down, final_norm_g=v_final_norm_g)
    S = x.shape[1]
    chip = 2 * lax.axis_index("x") + lax.axis_index("y")
    n_in = IN_W // N_CHIPS
    n_ff = D_FF // N_CHIPS
    n_gk = GLA_K // N_CHIPS

    def owned(t):
        return {k: (jnp.transpose(t[k][0]) if k == "w_in" else t[k][0]) for k in BIG_NAMES}

    own_w, own_m, own_v = owned(w), owned(m), owned(v)
    shard = {k: own_w[k].astype(BF16) for k in BIG_NAMES}
    small_shard = _pack_rows([gf_up[0], gb_up[0], conv_w[0]])
    small4, w_in4 = _gather_halves_async("gather_w_in", small_shard, shard["w_in"], 0)
    w_out4, w_gate4, w_up4 = _gather_chips_async("gather_w_mid", [shard["w_out"], shard["w_gate"], shard["w_up"]], 1)
    (w_down4,) = _gather_chips_async("gather_w_down", [shard["w_down"]], 2)
    w_in_t = w_in4.reshape(IN_W, D_MODEL)
    rows_up = GATE_RANK * n_gk // LANES
    rows_cw = 3 * n_ff // LANES
    gf_full = jnp.transpose(small4[:, 0:rows_up].reshape(N_CHIPS, GATE_RANK, n_gk), (1, 0, 2)).reshape(GATE_RANK, GLA_K)
    gb_full = jnp.transpose(small4[:, rows_up:2 * rows_up].reshape(N_CHIPS, GATE_RANK, n_gk), (1, 0, 2)).reshape(GATE_RANK, GLA_K)
    cw_full = jnp.transpose(small4[:, 2 * rows_up:2 * rows_up + rows_cw].reshape(N_CHIPS, 3, n_ff), (1, 0, 2)).reshape(3, D_FF)
    wg = jnp.zeros((Z_W, 2 * GLA_K), F32)
    wg = wg.at[0:GATE_RANK, 0:GLA_K].set(gf_full).at[GATE_RANK:2 * GATE_RANK, GLA_K:].set(gb_full).astype(BF16)
    gate_bias = jnp.concatenate([gf_b, gb_b], axis=1)

    pending, contributions, next_id = [], {}, [3]

    def as_shards(group, arrays):
        if group == "w_in":
            return dict(w_in=arrays["w_in_t"].reshape(N_CHIPS, n_in, D_MODEL))
        if group == "w_out":
            return dict(w_out=arrays["w_out"].reshape(N_CHIPS, D_MODEL // N_CHIPS, D_MODEL))
        if group == "w_down":
            return dict(w_down=arrays["w_down"].reshape(N_CHIPS, n_ff, D_MODEL))
        return arrays

    out = {}

    def swap(group, arrays):
        mine = as_shards(group, arrays)
        pending.append((group, mine, _sibling_exchange_async(f"sibling_{group}", list(mine.values()), next_id[0])))
        next_id[0] += 1

    def sum_and_send(anchor):
        tag, mine, theirs = pending.pop()
        sums = [_pair_sum(f"pair_sum_{k}", mine[k], _after(t, *anchor)) for k, t in zip(mine, theirs)]
        contributions.update(zip(mine, _scatter_chips_async(f"scatter_{tag}", sums, next_id[0])))
        next_id[0] += 1
        return sums

    def update(names, anchor):
        for k in names:
            res = _adamw(f"adamw_{k}", own_w[k], own_m[k], own_v[k], _after(contributions[k], *anchor))
            out[k] = [(jnp.transpose(r) if k == "w_in" else r)[None] for r in res]
        return [out[k][0] for k in names]

    def on_grad(event, arrays):
        anchor = list(arrays.values())
        held = []
        if event in ("w_gate_w_up", "w_out", "mid", "last"):
            held += sum_and_send(anchor)
        if event == "mid":
            held += update(("w_down", "w_gate", "w_up"), anchor)
        if event == "last":
            held += update(("w_out",), anchor)
        if event in ("w_down", "w_gate_w_up", "w_out", "w_in"):
            swap(event, arrays)
        return held

    grad_x, _, small = _local_step(
        x[0], loss_target[0], norm1_g, w_in_t, wg, gate_bias, gla_norm_g, attn_norm_g,
        w_out4.reshape(D_MODEL, D_MODEL), norm2_g, w_gate4, w_up4, cw_full, conv_b, w_down4.reshape(D_FF, D_MODEL), final_norm_g,
        on_grad=on_grad)
    update(("w_in",), [grad_x])

    d_gf_up = small["wg"][0:GATE_RANK, 0:GLA_K]
    d_gb_up = small["wg"][GATE_RANK:2 * GATE_RANK, GLA_K:]
    pieces = [small["loss"], small["norm1_g"], d_gf_up, small["gate_bias"][:, :GLA_K], d_gb_up, small["gate_bias"][:, GLA_K:],
              small["gla_norm_g"], small["attn_norm_g"], small["norm2_g"], small["conv_w"], small["conv_b"], small["final_norm_g"]]
    total = _allreduce_rows(_pack_rows(pieces))
    summed = _unpack_rows(total, [p.shape for p in pieces])
    loss = summed[0][0, 0]
    g_small = dict(zip(SMALL_NAMES, summed[1:]))
    g_small["gf_up"] = lax.dynamic_slice_in_dim(g_small["gf_up"], chip * n_gk, n_gk, axis=1)
    g_small["gb_up"] = lax.dynamic_slice_in_dim(g_small["gb_up"], chip * n_gk, n_gk, axis=1)
    g_small["conv_w"] = lax.dynamic_slice_in_dim(g_small["conv_w"], chip * n_ff, n_ff, axis=1)
    packed = [_pack_rows([t[k] for k in SMALL_NAMES]) for t in (w, m, v, g_small)]
    res = _adamw("adamw_small", *packed)
    shapes = [w[k].shape for k in SMALL_NAMES]
    for k, vals in zip(SMALL_NAMES, zip(*[_unpack_rows(r, shapes) for r in res])):
        out[k] = list(vals)

    grads, deltas, new_m, new_v = ([out[k][i] for k in WEIGHT_ORDER] for i in range(4))
    return (loss, grad_x[None], *grads, *deltas, *new_m, *new_v)
```

```python
import functools
import math

import jax
import jax.numpy as jnp
from jax import lax
from jax.experimental import pallas as pl
from jax.experimental.pallas import tpu as pltpu
from jax.experimental.pallas import tpu_sc as plsc

F32 = jnp.float32
BF16 = jnp.bfloat16

D_MODEL = 2048
ATTN_W = 1024
HEAD = 128
N_HEADS = 8
N_SIDE = 64
DILATIONS = (1, 4, 16)
ROPE_THETA = 500000.0
ROPE_DIM = 32
GLA_K = 512
GLA_V = 1024
GLA_HEADS = 4
GLA_DK = 128
GLA_DV = 256
GATE_RANK = 16
GATE_NORM = 16.0
CHUNK = 64
IN_MAIN = 6144
IN_W = 6176
Z_W = IN_W - IN_MAIN
D_FF = 5632
EPS = 1e-6
N_CHIPS = 4
N_DEV = 8
LANES = 128

ADAM_LR = 0.001
ADAM_B1 = 0.9
ADAM_B2 = 0.999
ADAM_EPS = 1e-08
ADAM_WD = 0.01
ADAM_STEP = 10

NEG = -1e30
MESH = pl.DeviceIdType.MESH
ANY = pl.BlockSpec(memory_space=pl.ANY)

NN = ((1,), (0,))
NT = ((1,), (1,))
TN = ((0,), (0,))


def _dot(a, b, dims=NN):
    return lax.dot_general(a, b, (dims, ((), ())), preferred_element_type=F32)


def _sigmoid(x):
    return 0.5 * jnp.tanh(0.5 * x) + 0.5


def _after(x, *deps):
    return lax.optimization_barrier((x,) + deps)[0]


def _matmul(name, pairs, grid, out_shape, out_spec, nk, res=None, into=None, first=None):
    n_in = 2 * len(pairs) + (res is not None)
    dims = [p[4] for p in pairs]

    n_ops = n_in + (into is not None) + 2 * (first is not None)

    def body(*refs):
        ins, o_ref = refs[:n_in], refs[n_ops]

        def partial_sum():
            tot = None
            for p, dn in enumerate(dims):
                a, b = ins[2 * p][...], ins[2 * p + 1][...]
                t = _dot(a.astype(BF16), b.astype(BF16), dn)
                tot = t if tot is None else tot + t
            return tot

        if nk == 1:
            t = partial_sum()
            if res is not None:
                t = t + ins[-1][...]
            o_ref[...] = t.astype(o_ref.dtype)
        else:
            acc_ref = refs[n_ops + 1]
            k = pl.program_id(2)

            @pl.when(k == 0)
            def _():
                if first is not None:
                    start = _dot(refs[n_in][...].astype(BF16), refs[n_in + 1][...].astype(BF16), first[4])
                    acc_ref[...] = start + ins[-1][...] if res is not None else start
                elif res is not None:
                    acc_ref[...] = ins[-1][...]
                else:
                    acc_ref[...] = jnp.zeros_like(acc_ref)

            acc_ref[...] += partial_sum()

            @pl.when(k == nk - 1)
            def _():
                o_ref[...] = acc_ref[...].astype(o_ref.dtype)

    operands, in_specs = [], []
    for a, a_spec, b, b_spec, _ in pairs:
        operands += [a, b]
        in_specs += [a_spec, b_spec]
    if res is not None:
        operands.append(res[0])
        in_specs.append(res[1])
    if first is not None:
        assert nk > 1
        operands += [first[0], first[2]]
        in_specs += [first[1], first[3]]
    acc_shape = tuple(s for s in out_spec.block_shape if s is not None)
    scratch = [pltpu.VMEM(acc_shape, F32)] if nk > 1 else []
    aliases = {}
    if into is not None:
        aliases = {len(operands): 0}
        operands.append(into)
        in_specs.append(ANY)
    return pl.pallas_call(
        body, name=name, grid=grid, in_specs=in_specs, out_specs=out_spec, out_shape=out_shape, scratch_shapes=scratch,
        input_output_aliases=aliases,
    )(*operands)


def _mm_nn(name, a, b, tm, tn, out_dtype, res=None):
    M, K = a.shape
    N = b.shape[1]
    pairs = [(a, pl.BlockSpec((tm, K), lambda j, i: (i, 0)), b, pl.BlockSpec((K, tn), lambda j, i: (0, j)), NN)]
    r = None if res is None else (res, pl.BlockSpec((tm, tn), lambda j, i: (i, j)))
    return _matmul(name, pairs, (N // tn, M // tm), jax.ShapeDtypeStruct((M, N), out_dtype),
                   pl.BlockSpec((tm, tn), lambda j, i: (i, j)), 1, r)


def _mm_nn_sharded(name, a, b4, tm, out_dtype):
    M, K = a.shape
    n = b4.shape[2]
    pairs = [(a, pl.BlockSpec((tm, K), lambda j, i: (i, 0)), b4, pl.BlockSpec((None, K, n), lambda j, i: (j, 0, 0)), NN)]
    return _matmul(name, pairs, (N_CHIPS, M // tm), jax.ShapeDtypeStruct((M, N_CHIPS * n), out_dtype),
                   pl.BlockSpec((tm, n), lambda j, i: (i, j)), 1)


def _mm_nt(name, a, b, tm, tn, out_dtype, res=None, n_out=None):
    M, K = a.shape
    N = b.shape[0] if n_out is None else n_out
    pairs = [(a, pl.BlockSpec((tm, K), lambda j, i: (i, 0)), b, pl.BlockSpec((tn, K), lambda j, i: (j, 0)), NT)]
    r = None if res is None else (res, pl.BlockSpec((tm, tn), lambda j, i: (i, j)))
    return _matmul(name, pairs, (N // tn, M // tm), jax.ShapeDtypeStruct((M, N), out_dtype),
                   pl.BlockSpec((tm, tn), lambda j, i: (i, j)), 1, r)


def _mm_tn(name, a, g, tka, tn, tmm, out_dtype, out3=None, rows_out=None):
    M, Ka = a.shape
    N = g.shape[1]
    pairs = [(a, pl.BlockSpec((tmm, tka), lambda i, j, k: (k, i)), g, pl.BlockSpec((tmm, tn), lambda i, j, k: (k, j)), TN)]
    if out3 is None:
        shape, spec = (Ka if rows_out is None else rows_out, N), pl.BlockSpec((tka, tn), lambda i, j, k: (i, j))
    else:
        shape, spec = (N // out3, Ka, out3), pl.BlockSpec((None, tka, tn), lambda i, j, k: (j, i, 0))
    return _matmul(name, pairs, (Ka // tka, N // tn, M // tmm), jax.ShapeDtypeStruct(shape, out_dtype), spec, M // tmm)


def _rms_fwd(name, x, g, tm=512):
    S, D = x.shape

    def body(x_ref, g_ref, o_ref):
        xv = x_ref[...]
        r = lax.rsqrt(jnp.mean(xv * xv, axis=-1, keepdims=True) + EPS)
        o_ref[...] = (xv * r * g_ref[...]).astype(o_ref.dtype)

    return pl.pallas_call(
        body, name=name, grid=(S // tm,),
        in_specs=[pl.BlockSpec((tm, D), lambda i: (i, 0)), pl.BlockSpec((1, D), lambda i: (0, 0))],
        out_specs=pl.BlockSpec((tm, D), lambda i: (i, 0)), out_shape=jax.ShapeDtypeStruct((S, D), BF16),
    )(x, g)


def _rms_bwd(name, x, g, dn, dres, tm=512):
    S, D = x.shape

    def body(x_ref, g_ref, dn_ref, dres_ref, dx_ref, dxb_ref, dg_ref):
        i = pl.program_id(0)

        @pl.when(i == 0)
        def _():
            dg_ref[...] = jnp.zeros_like(dg_ref)

        xv = x_ref[...]
        r = lax.rsqrt(jnp.mean(xv * xv, axis=-1, keepdims=True) + EPS)
        xhat = xv * r
        dnv = dn_ref[...].astype(F32)
        dg_ref[...] += jnp.sum(dnv * xhat, axis=0, keepdims=True)
        t = dnv * g_ref[...]
        dx = r * (t - xhat * jnp.mean(t * xhat, axis=-1, keepdims=True)) + dres_ref[...]
        dx_ref[...] = dx
        dxb_ref[...] = dx.astype(BF16)

    row = pl.BlockSpec((tm, D), lambda i: (i, 0))
    vec = pl.BlockSpec((1, D), lambda i: (0, 0))
    return pl.pallas_call(
        body, name=name, grid=(S // tm,), in_specs=[row, vec, row, row], out_specs=[row, row, vec],
        out_shape=[jax.ShapeDtypeStruct((S, D), F32), jax.ShapeDtypeStruct((S, D), BF16), jax.ShapeDtypeStruct((1, D), F32)],
    )(x, g, dn, dres)


def _final_loss(h2, g, target, tm=512):
    S, D = h2.shape

    def body(x_ref, g_ref, t_ref, loss_ref, dg_ref, dx_ref, dxb_ref):
        i = pl.program_id(0)

        @pl.when(i == 0)
        def _():
            loss_ref[...] = jnp.zeros_like(loss_ref)
            dg_ref[...] = jnp.zeros_like(dg_ref)

        xv = x_ref[...]
        r = lax.rsqrt(jnp.mean(xv * xv, axis=-1, keepdims=True) + EPS)
        xhat = xv * r
        gv = g_ref[...]
        diff = xhat * gv - t_ref[...]
        per_tok = jnp.mean(diff * diff, axis=-1, keepdims=True)
        loss_ref[...] += 0.5 * jnp.sum(per_tok, axis=0, keepdims=True)
        dy = diff * (1.0 / D)
        dg_ref[...] += jnp.sum(dy * xhat, axis=0, keepdims=True)
        t = dy * gv
        dx = r * (t - xhat * jnp.mean(t * xhat, axis=-1, keepdims=True))
        dx_ref[...] = dx
        dxb_ref[...] = dx.astype(BF16)

    row = pl.BlockSpec((tm, D), lambda i: (i, 0))
    vec = pl.BlockSpec((1, D), lambda i: (0, 0))
    return pl.pallas_call(
        body, name="final_loss", grid=(S // tm,), in_specs=[row, vec, row],
        out_specs=[pl.BlockSpec((1, LANES), lambda i: (0, 0)), vec, row, row],
        out_shape=[jax.ShapeDtypeStruct((1, LANES), F32), jax.ShapeDtypeStruct((1, D), F32),
                   jax.ShapeDtypeStruct((S, D), F32), jax.ShapeDtypeStruct((S, D), BF16)],
    )(h2, g, target)


def _rope_tables(S):
    pos = jnp.arange(S, dtype=F32)
    inv_freq = ROPE_THETA ** (-jnp.arange(0, ROPE_DIM, 2, dtype=F32) / ROPE_DIM)
    ang = pos[:, None] * inv_freq[None, :]
    cos, sin = jnp.cos(ang), jnp.sin(ang)
    half = ROPE_DIM // 2
    rest = HEAD - ROPE_DIM
    z_h, z_r = jnp.zeros((S, half), F32), jnp.zeros((S, rest), F32)
    tab_c = jnp.concatenate([cos, cos, jnp.ones((S, rest), F32)], axis=1)
    tab_up = jnp.concatenate([z_h, sin, z_r], axis=1)
    tab_dn = jnp.concatenate([-sin, z_h, z_r], axis=1)
    return tab_c, tab_up, tab_dn


def _rope_head(t, c, up, dn):
    half = ROPE_DIM // 2
    return t * c + pltpu.roll(t, half, axis=1) * up + pltpu.roll(t, HEAD - half, axis=1) * dn


def _rope_fwd(proj, tabs, tm=512):
    S = proj.shape[0]
    W = 2 * ATTN_W

    def body(p_ref, c_ref, up_ref, dn_ref, o_ref):
        c, up, dn = c_ref[...], up_ref[...], dn_ref[...]
        for h in range(W // HEAD):
            sl = slice(h * HEAD, (h + 1) * HEAD)
            o_ref[:, sl] = _rope_head(p_ref[:, sl].astype(F32), c, up, dn).astype(BF16)

    tab = pl.BlockSpec((tm, HEAD), lambda i: (i, 0))
    return pl.pallas_call(
        body, name="rope_fwd", grid=(S // tm,), in_specs=[pl.BlockSpec((tm, W), lambda i: (i, 0)), tab, tab, tab],
        out_specs=pl.BlockSpec((tm, W), lambda i: (i, 0)), out_shape=jax.ShapeDtypeStruct((S, W), BF16),
    )(proj, *tabs)


def _attn_grad_merge(dqs, dks, dvs, tabs, tm=256):
    S = dqs[0].shape[0]

    def body(*refs):
        q_refs, k_refs, v_refs = refs[0:3], refs[3:6], refs[6:9]
        c, up, dn = refs[9][...], refs[10][...], refs[11][...]
        o_ref = refs[12]
        for h in range(N_HEADS):
            sl = slice(h * HEAD, (h + 1) * HEAD)
            for part, rs in ((0, q_refs), (1, k_refs)):
                t = rs[0][:, sl].astype(F32) + rs[1][:, sl].astype(F32) + rs[2][:, sl].astype(F32)
                osl = slice(part * ATTN_W + h * HEAD, part * ATTN_W + (h + 1) * HEAD)
                o_ref[:, osl] = _rope_head(t, c, -up, -dn).astype(BF16)
        o_ref[:, 2 * ATTN_W:] = (v_refs[0][...].astype(F32) + v_refs[1][...].astype(F32)
                                 + v_refs[2][...].astype(F32)).astype(BF16)

    blk = pl.BlockSpec((tm, ATTN_W), lambda i: (i, 0))
    tab = pl.BlockSpec((tm, HEAD), lambda i: (i, 0))
    return pl.pallas_call(
        body, name="attn_grad_merge", grid=(S // tm,), in_specs=[blk] * 9 + [tab] * 3,
        out_specs=pl.BlockSpec((tm, 3 * ATTN_W), lambda i: (i, 0)), out_shape=jax.ShapeDtypeStruct((S, IN_MAIN), BF16),
    )(*dqs, *dks, *dvs, *tabs)


SUB = 128
Q_COL, K_COL, V_COL = 0, ATTN_W // HEAD, 2 * ATTN_W // HEAD


class _AttnGeo:
    def __init__(self, S, d):
        self.S, self.d, self.L = S, d, S // d
        self.halo = N_SIDE * d
        self.TB = min(2048, S)
        self.W = self.TB + 2 * self.halo
        self.n_sub = self.TB // SUB
        self.grid = (S // self.TB, N_HEADS)
        self.dt = F32 if d > 1 else BF16
        self.su = min(d, 4)
        self.sb = d // self.su
        assert self.TB % (SUB * d) == 0 and self.TB % self.halo == 0

    def specs(self, width, col0, per_head=True):
        ratio = self.TB // self.halo
        last = self.S // self.halo - 1
        col = (lambda h: col0 + h) if per_head else (lambda h: col0)
        cur = pl.BlockSpec((self.TB, width), lambda i, h: (i, col(h)))
        prev = pl.BlockSpec((self.halo, width), lambda i, h: (jnp.maximum(i * ratio - 1, 0), col(h)))
        nxt = pl.BlockSpec((self.halo, width), lambda i, h: (jnp.minimum((i + 1) * ratio, last), col(h)))
        return cur, prev, nxt

    def scratch(self, rows, dtype=None):
        nat = pltpu.VMEM((rows, LANES), self.dt if dtype is None else dtype)
        return [nat] if self.sb == 1 else [nat, pltpu.VMEM((rows, LANES), F32)]

    def bind(self, refs):
        nat = next(refs)
        return (nat, nat) if self.sb == 1 else (nat, next(refs))

    def spread(self, pair):
        nat, streams = pair
        if self.sb > 1:
            n = nat.shape[0] // self.sb
            for a in range(self.sb):
                streams[a * n:(a + 1) * n, :] = nat[pl.ds(a, n, stride=self.sb), :]
        return streams

    def gather(self, pair):
        nat, streams = pair
        if self.sb > 1:
            n = nat.shape[0] // self.sb
            for a in range(self.sb):
                nat[pl.ds(a, n, stride=self.sb), :] = streams[a * n:(a + 1) * n, :]
        return nat

    def rows(self, sub, n, total):
        res, blk = sub % self.d, sub // self.d
        a, b = res % self.sb, res // self.sb
        start = a * (total // self.sb) + b + self.su * SUB * blk
        return pl.ds(start, n, stride=self.su) if self.su > 1 else pl.ds(start, n)

    def band(self):
        row = lax.broadcasted_iota(jnp.int32, (SUB, 2 * SUB), 0)
        col = lax.broadcasted_iota(jnp.int32, (SUB, 2 * SUB), 1)
        return (col >= row) & (col <= row + 2 * N_SIDE), col

    def mask(self, sub, band):
        inside, col = band
        blk, n_blk = sub // self.d, self.TB // (SUB * self.d)
        base = pl.program_id(0) * (self.TB // self.d) + SUB * blk
        if blk == 0:
            inside = inside & (col >= N_SIDE - base)
        if blk == n_blk - 1:
            inside = inside & (col < self.L + N_SIDE - base)
        return inside

    def fill(self, dst, c_ref):
        dst[...] = c_ref[...].astype(dst.dtype)

    def fill_window(self, dst, p_ref, c_ref, n_ref):
        dst[0:self.halo] = p_ref[...].astype(dst.dtype)
        dst[self.halo:self.halo + self.TB] = c_ref[...].astype(dst.dtype)
        dst[self.halo + self.TB:] = n_ref[...].astype(dst.dtype)


def _lane_of(tile, h):
    lane = lax.broadcasted_iota(jnp.int32, tile.shape, 1)
    return jnp.sum(jnp.where(lane == h, tile, 0.0), axis=1, keepdims=True)


def _attn_fwd(qk, proj, d):
    S = qk.shape[0]
    geo = _AttnGeo(S, d)
    scale = HEAD ** -0.5

    def body(q_ref, kp, kc, kn, vp, vc, vn, o_ref, lse_ref, *scratch):
        h = pl.program_id(1)
        refs = iter(scratch)
        q_p, k_p, v_p, o_p, l_p = (geo.bind(refs) for _ in range(5))
        geo.fill(q_p[0], q_ref)
        geo.fill_window(k_p[0], kp, kc, kn)
        geo.fill_window(v_p[0], vp, vc, vn)
        qs, ks, vs = geo.spread(q_p), geo.spread(k_p), geo.spread(v_p)
        os, ls = o_p[1], l_p[1]
        band = geo.band()
        for sub in range(geo.n_sub):
            rq, rw = geo.rows(sub, SUB, geo.TB), geo.rows(sub, 2 * SUB, geo.W)
            q_r, k_r, v_r = qs[rq, :].astype(BF16), ks[rw, :].astype(BF16), vs[rw, :].astype(BF16)
            s = jnp.where(geo.mask(sub, band), _dot(q_r, k_r, NT) * scale, NEG)
            m = jnp.max(s, axis=1, keepdims=True)
            p = jnp.exp(s - m)
            l = jnp.sum(p, axis=1, keepdims=True)
            os[rq, :] = _dot(p.astype(BF16), v_r) / l
            ls[rq, :] = jnp.broadcast_to(m + jnp.log(l), (SUB, LANES))
        o_ref[...] = geo.gather(o_p)[...].astype(BF16)

        @pl.when(h == 0)
        def _():
            lse_ref[...] = jnp.zeros_like(lse_ref)

        lane = lax.broadcasted_iota(jnp.int32, (geo.TB, LANES), 1)
        lse_ref[...] = jnp.where(lane == h, geo.gather(l_p)[...], lse_ref[...])

    q_cur, _, _ = geo.specs(HEAD, Q_COL)
    k_specs = geo.specs(HEAD, K_COL)
    v_specs = geo.specs(HEAD, V_COL)
    stat = pl.BlockSpec((geo.TB, LANES), lambda i, h: (i, 0))
    return pl.pallas_call(
        body, name=f"attn_fwd_d{d}", grid=geo.grid,
        in_specs=[q_cur, k_specs[1], k_specs[0], k_specs[2], v_specs[1], v_specs[0], v_specs[2]],
        out_specs=[q_cur, stat],
        out_shape=[jax.ShapeDtypeStruct((S, ATTN_W), BF16), jax.ShapeDtypeStruct((S, LANES), F32)],
        scratch_shapes=(geo.scratch(geo.TB) + geo.scratch(geo.W) + geo.scratch(geo.W) + geo.scratch(geo.TB, F32)
                        + geo.scratch(geo.TB, F32)),
    )(qk, qk, qk, qk, proj, proj, proj)


def _attn_combine(outs, lses, g, tm=256):
    S = outs[0].shape[0]

    def body(o1, o2, o3, l1, l2, l3, g_ref, ao_ref, o_ref, lse_ref):
        a1, a2, a3 = l1[...], l2[...], l3[...]
        mx = jnp.maximum(jnp.maximum(a1, a2), a3)
        e1, e2, e3 = jnp.exp(a1 - mx), jnp.exp(a2 - mx), jnp.exp(a3 - mx)
        den = e1 + e2 + e3
        lse_ref[...] = mx + jnp.log(den)
        head_of_col = lax.broadcasted_iota(jnp.int32, (LANES, ATTN_W), 1) // HEAD
        spread = (lax.broadcasted_iota(jnp.int32, (LANES, ATTN_W), 0) == head_of_col).astype(BF16)

        def wide(e):
            wgt = e / den
            hi = wgt.astype(BF16)
            lo = (wgt - hi.astype(F32)).astype(BF16)
            return _dot(hi, spread) + _dot(lo, spread)

        ov = wide(e1) * o1[...].astype(F32) + wide(e2) * o2[...].astype(F32) + wide(e3) * o3[...].astype(F32)
        o_ref[...] = ov
        r = lax.rsqrt(jnp.mean(ov * ov, axis=-1, keepdims=True) + EPS)
        ao_ref[...] = (ov * r * g_ref[...]).astype(BF16)

    blk = pl.BlockSpec((tm, ATTN_W), lambda i: (i, 0))
    ls = pl.BlockSpec((tm, LANES), lambda i: (i, 0))
    return pl.pallas_call(
        body, name="attn_combine", grid=(S // tm,),
        in_specs=[blk, blk, blk, ls, ls, ls, pl.BlockSpec((1, ATTN_W), lambda i: (0, 0))], out_specs=[blk, blk, ls],
        out_shape=[jax.ShapeDtypeStruct((S, D_MODEL), BF16), jax.ShapeDtypeStruct((S, ATTN_W), F32),
                   jax.ShapeDtypeStruct((S, LANES), F32)],
    )(*outs, *lses, g)


def _attn_norm_bwd(o, g, dao, tm=256):
    S = o.shape[0]

    def body(o_ref, g_ref, dao_ref, do_ref, dl_ref, dg_ref):
        i = pl.program_id(0)

        @pl.when(i == 0)
        def _():
            dg_ref[...] = jnp.zeros_like(dg_ref)

        ov = o_ref[...]
        r = lax.rsqrt(jnp.mean(ov * ov, axis=-1, keepdims=True) + EPS)
        ohat = ov * r
        dn = dao_ref[...].astype(F32)
        dg_ref[...] += jnp.sum(dn * ohat, axis=0, keepdims=True)
        t = dn * g_ref[...]
        do = r * (t - ohat * jnp.mean(t * ohat, axis=-1, keepdims=True))
        do_ref[...] = do.astype(BF16)
        prod = do * ov
        lane = lax.broadcasted_iota(jnp.int32, (tm, LANES), 1)
        tile = jnp.zeros((tm, LANES), F32)
        for h in range(N_HEADS):
            tile = jnp.where(lane == h, jnp.sum(prod[:, h * HEAD:(h + 1) * HEAD], axis=1, keepdims=True), tile)
        dl_ref[...] = tile

    blk = pl.BlockSpec((tm, ATTN_W), lambda i: (i, 0))
    vec = pl.BlockSpec((1, ATTN_W), lambda i: (0, 0))
    return pl.pallas_call(
        body, name="attn_norm_bwd", grid=(S // tm,),
        in_specs=[blk, vec, pl.BlockSpec((tm, ATTN_W), lambda i: (i, 0))],
        out_specs=[blk, pl.BlockSpec((tm, LANES), lambda i: (i, 0)), vec],
        out_shape=[jax.ShapeDtypeStruct((S, ATTN_W), BF16), jax.ShapeDtypeStruct((S, LANES), F32),
                   jax.ShapeDtypeStruct((1, ATTN_W), F32)],
    )(o, g, dao)


def _attn_bwd_dq(qk, proj, do, lse, delta, d):
    S = qk.shape[0]
    geo = _AttnGeo(S, d)
    scale = HEAD ** -0.5

    def body(q_ref, kp, kc, kn, vp, vc, vn, do_ref, lse_ref, dl_ref, dq_ref, *scratch):
        h = pl.program_id(1)
        refs = iter(scratch)
        q_p, k_p, v_p, do_p, lse_p, dl_p, dq_p = (geo.bind(refs) for _ in range(7))
        geo.fill(q_p[0], q_ref)
        geo.fill(do_p[0], do_ref)
        geo.fill(lse_p[0], lse_ref)
        geo.fill(dl_p[0], dl_ref)
        geo.fill_window(k_p[0], kp, kc, kn)
        geo.fill_window(v_p[0], vp, vc, vn)
        qs, ks, vs, dos = geo.spread(q_p), geo.spread(k_p), geo.spread(v_p), geo.spread(do_p)
        lses, dls = geo.spread(lse_p), geo.spread(dl_p)
        dqs = dq_p[1]
        band = geo.band()
        for sub in range(geo.n_sub):
            rq, rw = geo.rows(sub, SUB, geo.TB), geo.rows(sub, 2 * SUB, geo.W)
            q_r, k_r, v_r = qs[rq, :].astype(BF16), ks[rw, :].astype(BF16), vs[rw, :].astype(BF16)
            lse_c, dl_c = _lane_of(lses[rq, :], h), _lane_of(dls[rq, :], h)
            s = _dot(q_r, k_r, NT) * scale
            p = jnp.where(geo.mask(sub, band), jnp.exp(s - lse_c), 0.0)
            dp = _dot(dos[rq, :].astype(BF16), v_r, NT)
            ds = (p * (dp - dl_c) * scale).astype(BF16)
            dqs[rq, :] = _dot(ds, k_r)
        dq_ref[...] = geo.gather(dq_p)[...].astype(BF16)

    cur, _, _ = geo.specs(HEAD, 0)
    k_specs = geo.specs(HEAD, K_COL)
    v_specs = geo.specs(HEAD, V_COL)
    stat = pl.BlockSpec((geo.TB, LANES), lambda i, h: (i, 0))
    return pl.pallas_call(
        body, name=f"attn_bwd_dq_d{d}", grid=geo.grid,
        in_specs=[cur, k_specs[1], k_specs[0], k_specs[2], v_specs[1], v_specs[0], v_specs[2], cur, stat, stat],
        out_specs=cur, out_shape=jax.ShapeDtypeStruct((S, ATTN_W), BF16),
        scratch_shapes=(geo.scratch(geo.TB) + geo.scratch(geo.W) + geo.scratch(geo.W) + geo.scratch(geo.TB)
                        + geo.scratch(geo.TB, F32) + geo.scratch(geo.TB, F32) + geo.scratch(geo.TB, F32)),
    )(qk, qk, qk, qk, proj, proj, proj, do, lse, delta)


def _attn_bwd_dkv(qk, proj, do, lse, delta, d):
    S = qk.shape[0]
    geo = _AttnGeo(S, d)
    scale = HEAD ** -0.5

    def body(k_ref, v_ref, qp, qc, qn, dop, doc, don, lp, lc, ln, dlp, dlc, dln, dk_ref, dv_ref, *scratch):
        h = pl.program_id(1)
        refs = iter(scratch)
        k_p, v_p, q_p, do_p, lw_p, dlw_p, dk_p, dv_p = (geo.bind(refs) for _ in range(8))
        geo.fill(k_p[0], k_ref)
        geo.fill(v_p[0], v_ref)
        geo.fill_window(q_p[0], qp, qc, qn)
        geo.fill_window(do_p[0], dop, doc, don)
        geo.fill_window(lw_p[0], lp, lc, ln)
        geo.fill_window(dlw_p[0], dlp, dlc, dln)
        ks, vs, qs, dos = geo.spread(k_p), geo.spread(v_p), geo.spread(q_p), geo.spread(do_p)
        lws, dlws = geo.spread(lw_p), geo.spread(dlw_p)
        dks, dvs = dk_p[1], dv_p[1]
        head = lax.broadcasted_iota(jnp.int32, (LANES, 2 * SUB), 0)
        band = geo.band()
        for sub in range(geo.n_sub):
            rq, rw = geo.rows(sub, SUB, geo.TB), geo.rows(sub, 2 * SUB, geo.W)
            k_r, v_r = ks[rq, :].astype(BF16), vs[rq, :].astype(BF16)
            q_w, do_w = qs[rw, :].astype(BF16), dos[rw, :].astype(BF16)
            lse_row = jnp.sum(jnp.where(head == h, lws[rw, :].T, 0.0), axis=0, keepdims=True)
            dl_row = jnp.sum(jnp.where(head == h, dlws[rw, :].T, 0.0), axis=0, keepdims=True)
            st = _dot(k_r, q_w, NT) * scale
            pt = jnp.where(geo.mask(sub, band), jnp.exp(st - lse_row), 0.0)
            dvs[rq, :] = _dot(pt.astype(BF16), do_w)
            dpt = _dot(v_r, do_w, NT)
            dst = (pt * (dpt - dl_row) * scale).astype(BF16)
            dks[rq, :] = _dot(dst, q_w)
        dk_ref[...] = geo.gather(dk_p)[...].astype(BF16)
        dv_ref[...] = geo.gather(dv_p)[...].astype(BF16)

    q_specs = geo.specs(HEAD, Q_COL)
    k_cur, _, _ = geo.specs(HEAD, K_COL)
    v_cur, _, _ = geo.specs(HEAD, V_COL)
    do_specs = geo.specs(HEAD, 0)
    st_specs = geo.specs(LANES, 0, per_head=False)
    cur = do_specs[0]
    return pl.pallas_call(
        body, name=f"attn_bwd_dkv_d{d}", grid=geo.grid,
        in_specs=[k_cur, v_cur, q_specs[1], q_specs[0], q_specs[2], do_specs[1], do_specs[0], do_specs[2],
                  st_specs[1], st_specs[0], st_specs[2], st_specs[1], st_specs[0], st_specs[2]],
        out_specs=[cur, cur],
        out_shape=[jax.ShapeDtypeStruct((S, ATTN_W), BF16), jax.ShapeDtypeStruct((S, ATTN_W), BF16)],
        scratch_shapes=(geo.scratch(geo.TB) + geo.scratch(geo.TB) + geo.scratch(geo.W) + geo.scratch(geo.W)
                        + geo.scratch(geo.W, F32) + geo.scratch(geo.W, F32) + geo.scratch(geo.TB, F32)
                        + geo.scratch(geo.TB, F32)),
    )(qk, proj, qk, qk, qk, do, do, do, lse, lse, lse, delta, delta, delta)


def _cumsum_rows(x, reverse):
    n = x.shape[0]
    row = lax.broadcasted_iota(jnp.int32, x.shape, 0)
    s = 1
    while s < n:
        if reverse:
            x = x + jnp.where(row < n - s, pltpu.roll(x, n - s, axis=0), 0.0)
        else:
            x = x + jnp.where(row >= s, pltpu.roll(x, s, axis=0), 0.0)
        s *= 2
    return x


GLA_GROUP = 8


def _gla_rows(cc):
    return slice(cc * CHUNK, (cc + 1) * CHUNK)


def _gla_chunk_terms(q_ref, k_ref, v_ref, g_ref, h, reverse, rows):
    ksl = slice(h * GLA_DK, (h + 1) * GLA_DK)
    q = q_ref[rows, ksl].astype(F32) * (GLA_DK ** -0.5)
    k = k_ref[rows, ksl].astype(F32)
    v = v_ref[rows, h * GLA_DV:(h + 1) * GLA_DV]
    b = _cumsum_rows(g_ref[rows, ksl], reverse)
    r_ref = CHUNK // 2 if reverse else CHUNK // 2 - 1
    r_last = 0 if reverse else CHUNK - 1
    b_ref, b_last = b[r_ref:r_ref + 1, :], b[r_last:r_last + 1, :]
    ii = lax.broadcasted_iota(jnp.int32, (CHUNK, CHUNK), 0)
    jj = lax.broadcasted_iota(jnp.int32, (CHUNK, CHUNK), 1)
    causal = (jj >= ii) if reverse else (jj <= ii)
    e_q, e_k = jnp.exp(b - b_ref), jnp.exp(b_ref - b)
    e_in, e_st = jnp.exp(b), jnp.exp(b_last - b)
    return dict(q=q, k=k, v=v, b=b, causal=causal, e_q=e_q, e_k=e_k, e_in=e_in, e_st=e_st, dec=jnp.exp(b_last),
                qe=q * e_q, ke=k * e_k, q_in=q * e_in, k_st=k * e_st, r_ref=r_ref, r_last=r_last)


def _gla_specs(order):
    rows = GLA_GROUP * CHUNK
    q = pl.BlockSpec((rows, GLA_K), lambda c: (order(c), 3 * ATTN_W // GLA_K))
    k = pl.BlockSpec((rows, GLA_K), lambda c: (order(c), 3 * ATTN_W // GLA_K + 1))
    v = pl.BlockSpec((rows, GLA_V), lambda c: (order(c), (3 * ATTN_W + 2 * GLA_K) // GLA_V))
    return q, k, v


def _gla_fwd(proj, gates, reverse, o_prev=None):
    S = proj.shape[0]
    n = S // CHUNK
    nb = n // GLA_GROUP
    rows = GLA_GROUP * CHUNK
    order = (lambda c: nb - 1 - c) if reverse else (lambda c: c)
    seq = list(range(GLA_GROUP))[::-1] if reverse else list(range(GLA_GROUP))
    gcol = 1 if reverse else 0

    def body(*refs):
        if o_prev is None:
            q_ref, k_ref, v_ref, g_ref, o_ref, st_ref, state = refs
        else:
            q_ref, k_ref, v_ref, g_ref, op_ref, o_ref, st_ref, state = refs
        c = pl.program_id(0)

        @pl.when(c == 0)
        def _():
            state[...] = jnp.zeros_like(state)

        for h in range(GLA_HEADS):
            vsl = slice(h * GLA_DV, (h + 1) * GLA_DV)
            st = state[h]
            for cc in seq:
                rs = _gla_rows(cc)
                t = _gla_chunk_terms(q_ref, k_ref, v_ref, g_ref, h, reverse, rs)
                a = jnp.where(t["causal"], _dot(t["qe"].astype(BF16), t["ke"].astype(BF16), NT), 0.0)
                o = _dot(a.astype(BF16), t["v"])
                st_b = st.astype(BF16)
                st_ref[cc, h] = st_b
                o = o + _dot(t["q_in"].astype(BF16), st_b, NT)
                st = st * t["dec"] + _dot(t["v"], t["k_st"].astype(BF16), TN)
                if o_prev is not None:
                    o = o + op_ref[rs, vsl]
                o_ref[rs, vsl] = o
            state[h] = st

    q_spec, k_spec, v_spec = _gla_specs(order)
    o_spec = pl.BlockSpec((rows, GLA_V), lambda c: (order(c), 0))
    in_specs = [q_spec, k_spec, v_spec, pl.BlockSpec((rows, GLA_K), lambda c: (order(c), gcol))]
    operands = [proj, proj, proj, gates]
    if o_prev is not None:
        in_specs.append(o_spec)
        operands.append(o_prev)
    return pl.pallas_call(
        body, name="gla_fwd_rev" if reverse else "gla_fwd", grid=(nb,), in_specs=in_specs,
        out_specs=[o_spec, pl.BlockSpec((GLA_GROUP, GLA_HEADS, GLA_DV, GLA_DK), lambda c: (order(c), 0, 0, 0))],
        out_shape=[jax.ShapeDtypeStruct((S, GLA_V), F32), jax.ShapeDtypeStruct((n, GLA_HEADS, GLA_DV, GLA_DK), BF16)],
        scratch_shapes=[pltpu.VMEM((GLA_HEADS, GLA_DV, GLA_DK), F32)],
    )(*operands)


def _gla_bwd(proj, gates, states, do, reverse, prev=None):
    S = proj.shape[0]
    n = S // CHUNK
    nb = n // GLA_GROUP
    rows = GLA_GROUP * CHUNK
    order = (lambda c: c) if reverse else (lambda c: nb - 1 - c)
    seq = list(range(GLA_GROUP)) if reverse else list(range(GLA_GROUP))[::-1]
    gcol = 1 if reverse else 0
    out_dt = F32 if prev is None else BF16

    def body(*refs):
        if prev is None:
            q_ref, k_ref, v_ref, g_ref, st_ref, do_ref, dq_ref, dk_ref, dv_ref, dg_ref, dstate = refs
        else:
            q_ref, k_ref, v_ref, g_ref, st_ref, do_ref, pq, pk, pv, dq_ref, dk_ref, dv_ref, dg_ref, dstate = refs
        c = pl.program_id(0)

        @pl.when(c == 0)
        def _():
            dstate[...] = jnp.zeros_like(dstate)

        row = lax.broadcasted_iota(jnp.int32, (CHUNK, GLA_DK), 0)
        for h in range(GLA_HEADS):
            ksl = slice(h * GLA_DK, (h + 1) * GLA_DK)
            vsl = slice(h * GLA_DV, (h + 1) * GLA_DV)
            dst = dstate[h]
            for cc in seq:
                rs = _gla_rows(cc)
                t = _gla_chunk_terms(q_ref, k_ref, v_ref, g_ref, h, reverse, rs)
                v = t["v"]
                dob = do_ref[rs, vsl].astype(BF16)
                st_b = st_ref[cc, h]
                dst_b = dst.astype(BF16)
                qe_b, ke_b = t["qe"].astype(BF16), t["ke"].astype(BF16)
                q_in_b, k_st_b = t["q_in"].astype(BF16), t["k_st"].astype(BF16)
                a = jnp.where(t["causal"], _dot(qe_b, ke_b, NT), 0.0)
                da = jnp.where(t["causal"], _dot(dob, v, NT), 0.0).astype(BF16)
                dv = _dot(a.astype(BF16), dob, TN) + _dot(k_st_b, dst_b, NT)
                dqe = _dot(da, ke_b)
                dke = _dot(da, qe_b, TN)
                dq_in = _dot(dob, st_b)
                dk_st = _dot(v, dst_b)
                ddec = jnp.sum(dst * st_b.astype(F32), axis=0, keepdims=True)
                dst = _dot(dob, q_in_b, TN) + dst * t["dec"]
                dq = (dqe * t["e_q"] + dq_in * t["e_in"]) * (GLA_DK ** -0.5)
                dk = dke * t["e_k"] + dk_st * t["e_st"]
                w_q, w_k = dqe * t["qe"], dke * t["ke"]
                w_st = dk_st * t["k_st"]
                db = w_q - w_k + dq_in * t["q_in"] - w_st
                db_ref = jnp.sum(w_k - w_q, axis=0, keepdims=True)
                db_last = jnp.sum(w_st, axis=0, keepdims=True) + ddec * t["dec"]
                db = db + jnp.where(row == t["r_ref"], db_ref, 0.0) + jnp.where(row == t["r_last"], db_last, 0.0)
                dg_ref[rs, ksl] = _cumsum_rows(db, not reverse)
                if prev is not None:
                    dq, dk, dv = dq + pq[rs, ksl], dk + pk[rs, ksl], dv + pv[rs, vsl]
                dq_ref[rs, ksl] = dq.astype(out_dt)
                dk_ref[rs, ksl] = dk.astype(out_dt)
                dv_ref[rs, vsl] = dv.astype(out_dt)
            dstate[h] = dst

    q_spec, k_spec, v_spec = _gla_specs(order)
    kk = pl.BlockSpec((rows, GLA_K), lambda c: (order(c), 0))
    vv = pl.BlockSpec((rows, GLA_V), lambda c: (order(c), 0))
    in_specs = [q_spec, k_spec, v_spec, pl.BlockSpec((rows, GLA_K), lambda c: (order(c), gcol)),
                pl.BlockSpec((GLA_GROUP, GLA_HEADS, GLA_DV, GLA_DK), lambda c: (order(c), 0, 0, 0)), vv]
    operands = [proj, proj, proj, gates, states, do]
    if prev is not None:
        in_specs += [kk, kk, vv]
        operands += list(prev)
    return pl.pallas_call(
        body, name="gla_bwd_rev" if reverse else "gla_bwd", grid=(nb,), in_specs=in_specs, out_specs=[kk, kk, vv, kk],
        out_shape=[jax.ShapeDtypeStruct((S, GLA_K), out_dt), jax.ShapeDtypeStruct((S, GLA_K), out_dt),
                   jax.ShapeDtypeStruct((S, GLA_V), out_dt), jax.ShapeDtypeStruct((S, GLA_K), F32)],
        scratch_shapes=[pltpu.VMEM((GLA_HEADS, GLA_DV, GLA_DK), F32)],
    )(*operands)


def _gates_fwd(z, wg, bias, tm=512):
    S = z.shape[0]
    W = 2 * GLA_K

    def body(z_ref, w_ref, b_ref, o_ref):
        zg = _dot(z_ref[...], w_ref[...]) + b_ref[...]
        o_ref[...] = (jnp.minimum(zg, 0.0) - jnp.log(1.0 + jnp.exp(-jnp.abs(zg)))) * (1.0 / GATE_NORM)

    return pl.pallas_call(
        body, name="gates_fwd", grid=(S // tm,),
        in_specs=[pl.BlockSpec((tm, Z_W), lambda i: (i, 0)), pl.BlockSpec((Z_W, W), lambda i: (0, 0)),
                  pl.BlockSpec((1, W), lambda i: (0, 0))],
        out_specs=pl.BlockSpec((tm, W), lambda i: (i, 0)), out_shape=jax.ShapeDtypeStruct((S, W), F32),
    )(z, wg, bias)


def _gates_bwd(z, wg, bias, dg_f, dg_b, tm=512):
    S = z.shape[0]
    W = 2 * GLA_K

    def body(z_ref, w_ref, b_ref, dgf_ref, dgb_ref, dz_ref, dw_ref, db_ref):
        i = pl.program_id(0)

        @pl.when(i == 0)
        def _():
            dw_ref[...] = jnp.zeros_like(dw_ref)
            db_ref[...] = jnp.zeros_like(db_ref)

        zv = z_ref[...]
        zg = _dot(zv, w_ref[...]) + b_ref[...]
        dg = jnp.concatenate([dgf_ref[...], dgb_ref[...]], axis=1)
        dzg = dg * (1.0 / GATE_NORM) * _sigmoid(-zg)
        db_ref[...] += jnp.sum(dzg, axis=0, keepdims=True)
        dzg_b = dzg.astype(BF16)
        dw_ref[...] += _dot(zv, dzg_b, TN)
        dz_ref[...] = _dot(dzg_b, w_ref[...], NT).astype(BF16)

    half = pl.BlockSpec((tm, GLA_K), lambda i: (i, 0))
    return pl.pallas_call(
        body, name="gates_bwd", grid=(S // tm,),
        in_specs=[pl.BlockSpec((tm, Z_W), lambda i: (i, 0)), pl.BlockSpec((Z_W, W), lambda i: (0, 0)),
                  pl.BlockSpec((1, W), lambda i: (0, 0)), half, half],
        out_specs=[pl.BlockSpec((tm, Z_W), lambda i: (i, 0)), pl.BlockSpec((Z_W, W), lambda i: (0, 0)),
                   pl.BlockSpec((1, W), lambda i: (0, 0))],
        out_shape=[jax.ShapeDtypeStruct((S, Z_W), BF16), jax.ShapeDtypeStruct((Z_W, W), F32),
                   jax.ShapeDtypeStruct((1, W), F32)],
    )(z, wg, bias, dg_f, dg_b)


def _gla_out_fwd(o, proj, g, cat, tm=512):
    S = o.shape[0]

    def body(o_ref, gr_ref, g_ref, cat_ref, out_ref):
        gn = g_ref[...]
        for h in range(GLA_HEADS):
            sl = slice(h * GLA_DV, (h + 1) * GLA_DV)
            ov = o_ref[:, sl]
            r = lax.rsqrt(jnp.mean(ov * ov, axis=-1, keepdims=True) + EPS)
            gr = gr_ref[:, sl].astype(F32)
            out_ref[:, sl] = (ov * r * gn * (gr * _sigmoid(gr))).astype(BF16)

    blk = pl.BlockSpec((tm, GLA_V), lambda i: (i, 0))
    return pl.pallas_call(
        body, name="gla_out_fwd", grid=(S // tm,),
        in_specs=[blk, pl.BlockSpec((tm, GLA_V), lambda i: (i, (3 * ATTN_W + 2 * GLA_K + GLA_V) // GLA_V)),
                  pl.BlockSpec((1, GLA_DV), lambda i: (0, 0)), ANY],
        out_specs=pl.BlockSpec((tm, GLA_V), lambda i: (i, 1)), out_shape=jax.ShapeDtypeStruct((S, D_MODEL), BF16),
        input_output_aliases={3: 0},
    )(o, proj, g, cat)


def _gla_out_bwd(o, proj, g, dcat, dproj, tm=512):
    S = o.shape[0]

    def body(o_ref, gr_ref, g_ref, dgo_ref, dproj_ref, do_ref, dgr_ref, dg_ref):
        i = pl.program_id(0)

        @pl.when(i == 0)
        def _():
            dg_ref[...] = jnp.zeros_like(dg_ref)

        gn = g_ref[...]
        dg_acc = jnp.zeros((1, GLA_DV), F32)
        for h in range(GLA_HEADS):
            sl = slice(h * GLA_DV, (h + 1) * GLA_DV)
            ov = o_ref[:, sl]
            r = lax.rsqrt(jnp.mean(ov * ov, axis=-1, keepdims=True) + EPS)
            yhat = ov * r
            gr = gr_ref[:, sl].astype(F32)
            sg = _sigmoid(gr)
            dgo = dgo_ref[:, sl].astype(F32)
            dgr_ref[:, sl] = (dgo * (yhat * gn) * (sg * (1.0 + gr * (1.0 - sg)))).astype(BF16)
            dy = dgo * (gr * sg)
            dg_acc = dg_acc + jnp.sum(dy * yhat, axis=0, keepdims=True)
            t = dy * gn
            do_ref[:, sl] = r * (t - yhat * jnp.mean(t * yhat, axis=-1, keepdims=True))
        dg_ref[...] += dg_acc

    blk = pl.BlockSpec((tm, GLA_V), lambda i: (i, 0))
    vec = pl.BlockSpec((1, GLA_DV), lambda i: (0, 0))
    return pl.pallas_call(
        body, name="gla_out_bwd", grid=(S // tm,),
        in_specs=[blk, pl.BlockSpec((tm, GLA_V), lambda i: (i, (3 * ATTN_W + 2 * GLA_K + GLA_V) // GLA_V)), vec,
                  pl.BlockSpec((tm, GLA_V), lambda i: (i, 1)), ANY],
        out_specs=[blk, pl.BlockSpec((tm, GLA_V), lambda i: (i, (3 * ATTN_W + 2 * GLA_K + GLA_V) // GLA_V)), vec],
        out_shape=[jax.ShapeDtypeStruct((S, GLA_V), F32), jax.ShapeDtypeStruct((S, IN_MAIN), BF16),
                   jax.ShapeDtypeStruct((1, GLA_DV), F32)],
        input_output_aliases={4: 1},
    )(o, proj, g, dcat, dproj)


HALO = 16


def _halo_specs(tm, tn, S):
    cur = pl.BlockSpec((tm, tn), lambda j, i: (i, j))
    prev = pl.BlockSpec((HALO, tn), lambda j, i: (jnp.maximum(i * (tm // HALO) - 1, 0), j))
    nxt = pl.BlockSpec((HALO, tn), lambda j, i: (jnp.minimum((i + 1) * (tm // HALO), S // HALO - 1), j))
    return cur, prev, nxt


def _shifted(c_ref, p_ref, n_ref, n_blocks, i=None):
    if i is None:
        i = pl.program_id(1)
    x = c_ref[...].astype(F32)
    tm = x.shape[0]
    row = lax.broadcasted_iota(jnp.int32, x.shape, 0)
    before = p_ref[HALO - 1:HALO, :].astype(F32) * (i > 0).astype(F32)
    after = n_ref[0:1, :].astype(F32) * (i < n_blocks - 1).astype(F32)
    x_m1 = jnp.where(row == 0, before, pltpu.roll(x, 1, axis=0))
    x_p1 = jnp.where(row == tm - 1, after, pltpu.roll(x, tm - 1, axis=0))
    return x, x_m1, x_p1


def _glu_fwd(gp, up, cw, cb, tm=512, tn=1408):
    S = gp.shape[0]
    nb = S // tm

    def body(c_ref, p_ref, n_ref, up_ref, w_ref, b_ref, o_ref):
        x, x_m1, x_p1 = _shifted(c_ref, p_ref, n_ref, nb)
        w = w_ref[...]
        gate = w[0:1, :] * x_m1 + w[1:2, :] * x + w[2:3, :] * x_p1 + b_ref[...]
        o_ref[...] = (gate * _sigmoid(gate) * up_ref[...].astype(F32)).astype(BF16)

    cur, prev, nxt = _halo_specs(tm, tn, S)
    return pl.pallas_call(
        body, name="glu_fwd", grid=(D_FF // tn, nb),
        in_specs=[cur, prev, nxt, cur, pl.BlockSpec((3, tn), lambda j, i: (0, j)), pl.BlockSpec((1, tn), lambda j, i: (0, j))],
        out_specs=cur, out_shape=jax.ShapeDtypeStruct((S, D_FF), BF16),
    )(gp, gp, gp, up, cw, cb)


def _glu_bwd(gp, up, dact, cw, cb, tm=512, tn=1408):
    S = gp.shape[0]
    nb = S // tm

    def body(c_ref, p_ref, n_ref, up_ref, upp_ref, upn_ref, da_ref, dap_ref, dan_ref, w_ref, b_ref,
             dup_ref, dgp_ref, dw_ref, db_ref):
        i = pl.program_id(1)

        @pl.when(i == 0)
        def _():
            dw_ref[...] = jnp.zeros_like(dw_ref)
            db_ref[...] = jnp.zeros_like(db_ref)

        x, x_m1, x_p1 = _shifted(c_ref, p_ref, n_ref, nb)
        w = w_ref[...]
        w0, w1, w2, b = w[0:1, :], w[1:2, :], w[2:3, :], b_ref[...]

        def d_gate(gate, da, upv):
            sg = _sigmoid(gate)
            return sg, da * upv * (sg * (1.0 + gate * (1.0 - sg)))

        gate = w0 * x_m1 + w1 * x + w2 * x_p1 + b
        da = da_ref[...].astype(F32)
        sg, dgate = d_gate(gate, da, up_ref[...].astype(F32))
        dup_ref[...] = (da * (gate * sg)).astype(BF16)
        db_ref[...] += jnp.sum(dgate, axis=0, keepdims=True)
        dw_ref[...] += jnp.concatenate(
            [jnp.sum(dgate * x_m1, axis=0, keepdims=True), jnp.sum(dgate * x, axis=0, keepdims=True),
             jnp.sum(dgate * x_p1, axis=0, keepdims=True)], axis=0)

        pv, nv = p_ref[...].astype(F32), n_ref[...].astype(F32)
        gate_before = w0 * pv[HALO - 2:HALO - 1, :] + w1 * pv[HALO - 1:HALO, :] + w2 * x[0:1, :] + b
        _, dgate_before = d_gate(gate_before, dap_ref[...].astype(F32)[HALO - 1:HALO, :], upp_ref[...].astype(F32)[HALO - 1:HALO, :])
        gate_after = w0 * x[tm - 1:tm, :] + w1 * nv[0:1, :] + w2 * nv[1:2, :] + b
        _, dgate_after = d_gate(gate_after, dan_ref[...].astype(F32)[0:1, :], upn_ref[...].astype(F32)[0:1, :])
        dgate_before = dgate_before * (i > 0).astype(F32)
        dgate_after = dgate_after * (i < nb - 1).astype(F32)
        row = lax.broadcasted_iota(jnp.int32, dgate.shape, 0)
        dg_m1 = jnp.where(row == 0, dgate_before, pltpu.roll(dgate, 1, axis=0))
        dg_p1 = jnp.where(row == tm - 1, dgate_after, pltpu.roll(dgate, tm - 1, axis=0))
        dgp_ref[...] = (w0 * dg_p1 + w1 * dgate + w2 * dg_m1).astype(BF16)

    cur, prev, nxt = _halo_specs(tm, tn, S)
    w_spec = pl.BlockSpec((3, tn), lambda j, i: (0, j))
    b_spec = pl.BlockSpec((1, tn), lambda j, i: (0, j))
    return pl.pallas_call(
        body, name="glu_bwd", grid=(D_FF // tn, nb),
        in_specs=[cur, prev, nxt, cur, prev, nxt, cur, prev, nxt, w_spec, b_spec],
        out_specs=[cur, cur, w_spec, b_spec],
        out_shape=[jax.ShapeDtypeStruct((S, D_FF), BF16), jax.ShapeDtypeStruct((S, D_FF), BF16),
                   jax.ShapeDtypeStruct((3, D_FF), F32), jax.ShapeDtypeStruct((1, D_FF), F32)],
    )(gp, gp, gp, up, up, up, dact, dact, dact, cw, cb)


def _local_step(x, target, norm1_g, w_in_t, wg, gate_bias, gla_norm_g, attn_norm_g, w_out, norm2_g,
                w_gate4, w_up4, conv_w, conv_b, w_down, final_norm_g, on_grad=lambda event, arrays: ()):
    S = x.shape[0]
    tabs = _rope_tables(S)

    n1 = _rms_fwd("rms1_fwd", x, norm1_g)
    z_block = IN_MAIN // Z_W
    proj = _mm_nt("in_proj", n1, w_in_t, 1024, 1536, BF16, n_out=IN_MAIN)
    z = _matmul(
        "in_proj_z",
        [(n1, pl.BlockSpec((1024, D_MODEL), lambda i: (i, 0)), w_in_t, pl.BlockSpec((Z_W, D_MODEL), lambda i: (z_block, 0)), NT)],
        (S // 1024,), jax.ShapeDtypeStruct((S, Z_W), BF16), pl.BlockSpec((1024, Z_W), lambda i: (i, 0)), 1)
    qk = _rope_fwd(proj, tabs)
    branch = [_attn_fwd(qk, proj, d) for d in DILATIONS]
    ao, o_attn, lse = _attn_combine([b[0] for b in branch], [b[1] for b in branch], attn_norm_g)
    gates = _gates_fwd(z, wg, gate_bias)
    o_f, st_f = _gla_fwd(proj, gates, False)
    o_gla, st_b = _gla_fwd(proj, gates, True, o_prev=o_f)
    cat = _gla_out_fwd(o_gla, proj, gla_norm_g, ao)
    h1 = _mm_nn("out_proj", cat, w_out, 1024, 1024, F32, res=x)
    n2 = _rms_fwd("rms2_fwd", h1, norm2_g)
    gp = _mm_nn_sharded("ffn_gate", n2, w_gate4, 1024, BF16)
    up = _mm_nn_sharded("ffn_up", n2, w_up4, 1024, BF16)
    act = _glu_fwd(gp, up, conv_w, conv_b)
    tk = D_FF // N_CHIPS
    h2 = _mm_nn("ffn_down", act, w_down, 1024, 512, F32, res=h1)
    loss_row, d_final_g, dh2, dh2_b = _final_loss(h2, final_norm_g.reshape(1, D_MODEL), target)

    dact = _mm_nt("ffn_down_bwd", dh2_b, w_down, 1024, tk, BF16)
    dup, dgp, d_conv_w, d_conv_b = _glu_bwd(gp, up, dact, conv_w, conv_b)
    d_w_down = _mm_tn("ffn_down_wgrad", act, dh2_b, 512, D_MODEL, 2048, BF16)
    on_grad("w_down", dict(w_down=d_w_down))
    dgp = _after(dgp, d_w_down)
    d_w_gate4 = _mm_tn("ffn_gate_wgrad", n2, dgp, 1024, tk, 2048, BF16, out3=tk)
    dup = _after(dup, d_w_gate4)
    d_w_up4 = _mm_tn("ffn_up_wgrad", n2, dup, 1024, tk, 2048, BF16, out3=tk)
    held = on_grad("w_gate_w_up", dict(w_gate=d_w_gate4, w_up=d_w_up4))
    dgp = _after(dgp, d_w_up4, *held)
    shard_pairs = [
        (g, pl.BlockSpec((512, tk), functools.partial(lambda s, j, i: (i, s), s)),
         w4, pl.BlockSpec((None, 512, tk), functools.partial(lambda s, j, i: (s, j, 0), s)), NT)
        for g, w4 in ((dgp, w_gate4), (dup, w_up4)) for s in range(N_CHIPS)]
    dn2 = _matmul("ffn_in_bwd", shard_pairs, (D_MODEL // 512, S // 512), jax.ShapeDtypeStruct((S, D_MODEL), BF16),
                  pl.BlockSpec((512, 512), lambda j, i: (i, j)), 1)
    dh1, dh1_b, d_norm2_g = _rms_bwd("rms2_bwd", h1, norm2_g, dn2, dh2)

    d_w_out = _mm_tn("out_proj_wgrad", cat, dh1_b, D_MODEL, 1024, 1024, BF16)
    held = on_grad("w_out", dict(w_out=d_w_out))
    dcat = _mm_nt("out_proj_bwd", _after(dh1_b, d_w_out, *held), w_out, 1024, 1024, BF16)
    do_attn, delta, d_attn_norm_g = _attn_norm_bwd(o_attn, attn_norm_g, dcat)
    dqs, dks, dvs = [], [], []
    for d in DILATIONS:
        dqs.append(_attn_bwd_dq(qk, proj, do_attn, lse, delta, d))
        dk, dv = _attn_bwd_dkv(qk, proj, do_attn, lse, delta, d)
        dks.append(dk)
        dvs.append(dv)
    dproj = _attn_grad_merge(dqs, dks, dvs, tabs)
    held = on_grad("mid", dict(anchor=dproj))
    do_gla, dproj, d_gla_norm_g = _gla_out_bwd(o_gla, proj, gla_norm_g, _after(dcat, *held), dproj)
    dq_f, dk_f, dv_f, dg_f = _gla_bwd(proj, gates, st_f, do_gla, False)
    dgq, dgk, dgv, dg_b = _gla_bwd(proj, gates, st_b, do_gla, True, prev=(dq_f, dk_f, dv_f))
    dz, d_wg, d_gate_bias = _gates_bwd(z, wg, gate_bias, dg_f, dg_b)
    dproj = lax.dynamic_update_slice(dproj, jnp.concatenate([dgq, dgk, dgv], axis=1), (0, 3 * ATTN_W))
    d_w_in_t = _mm_tn("in_proj_wgrad", dproj, n1, 768, D_MODEL, 2048, BF16, rows_out=IN_W)
    n_tok = S // 1024
    d_w_in_t = _matmul(
        "in_proj_z_wgrad",
        [(dz, pl.BlockSpec((1024, Z_W), lambda i, j, k: (k, 0)), n1, pl.BlockSpec((1024, D_MODEL), lambda i, j, k: (k, 0)), TN)],
        (1, 1, n_tok), jax.ShapeDtypeStruct((IN_W, D_MODEL), BF16), pl.BlockSpec((Z_W, D_MODEL), lambda i, j, k: (z_block, 0)),
        n_tok, into=d_w_in_t)
    held = on_grad("w_in", dict(w_in_t=d_w_in_t))
    half = S // 2048

    def in_proj_bwd(name, first, a, into):
        return _matmul(
            name,
            [(a, pl.BlockSpec((1024, IN_MAIN), lambda j, i: (i + first, 0)), w_in_t, pl.BlockSpec((IN_MAIN, 512), lambda j, i: (0, j)), NN),
             (dz, pl.BlockSpec((1024, Z_W), lambda j, i: (i + first, 0)), w_in_t, pl.BlockSpec((Z_W, 512), lambda j, i: (z_block, j)), NN)],
            (D_MODEL // 512, half), jax.ShapeDtypeStruct((S, D_MODEL), BF16),
            pl.BlockSpec((1024, 512), lambda j, i: (i + first, j)), 1, into=into)

    dproj = _after(dproj, d_w_in_t, *held)
    dn1 = in_proj_bwd("in_proj_bwd_a", 0, dproj, None)
    held = on_grad("last", dict(last=dn1))
    dn1 = in_proj_bwd("in_proj_bwd_b", half, dproj, _after(dn1, *held))
    grad_x, _, d_norm1_g = _rms_bwd("rms1_bwd", x, norm1_g, dn1, dh1)

    big = dict(w_in_t=d_w_in_t, w_out=d_w_out, w_gate4=d_w_gate4, w_up4=d_w_up4, w_down=d_w_down)
    small = dict(loss=loss_row, norm1_g=d_norm1_g, wg=d_wg, gate_bias=d_gate_bias, gla_norm_g=d_gla_norm_g,
                 attn_norm_g=d_attn_norm_g, norm2_g=d_norm2_g, conv_w=d_conv_w, conv_b=d_conv_b, final_norm_g=d_final_g)
    return grad_x, big, small


def _position():
    return lax.axis_index("x"), lax.axis_index("y"), lax.axis_index("c")


def _other_chips(x, y):
    return [(1 - x, y), (x, 1 - y), (1 - x, 1 - y)]


def _gather_chips_async(name, shards, collective_id):
    n = len(shards)

    def body(*refs):
        ins, outs = refs[:n], refs[n:2 * n]
        send, recv, loc = refs[2 * n:]
        x, y, c = _position()
        me = 2 * x + y
        chips = _other_chips(x, y)
        barrier = pltpu.get_barrier_semaphore()
        for px, py in chips:
            pl.semaphore_signal(barrier, inc=1, device_id=(px, py, c), device_id_type=MESH)
        pl.semaphore_wait(barrier, len(chips))
        started = []
        for w in range(n):
            own = pltpu.make_async_copy(ins[w], outs[w].at[me], loc.at[w])
            own.start()
            started.append(own)
        sends = []
        for w in range(n):
            for j, (px, py) in enumerate(chips):
                cp = pltpu.make_async_remote_copy(ins[w], outs[w].at[me], send.at[3 * w + j], recv.at[3 * w + j],
                                                  device_id=(px, py, c), device_id_type=MESH)
                cp.start()
                sends.append(cp)
        for w in range(n):
            for j, (px, py) in enumerate(chips):
                pltpu.make_async_remote_copy(ins[w], outs[w].at[2 * px + py], send.at[3 * w + j], recv.at[3 * w + j],
                                             device_id=(px, py, c), device_id_type=MESH).wait_recv()
        for cp in sends:
            cp.wait_send()
        for own in started:
            own.wait()

    return pl.kernel(
        body, name=name, mesh=_sequencer(),
        out_type=[jax.ShapeDtypeStruct((N_CHIPS,) + s.shape, s.dtype) for s in shards],
        scratch_types=[pltpu.SemaphoreType.DMA((3 * n,)), pltpu.SemaphoreType.DMA((3 * n,)), pltpu.SemaphoreType.DMA((n,))],
        compiler_params=pltpu.CompilerParams(collective_id=collective_id),
    )(*shards)


def _gather_halves_async(name, small, shard, collective_id):
    half = shard.shape[1] // 2

    def body(small_ref, shard_ref, small_out, out, send, recv, loc):
        x, y, c = _position()
        me = 2 * x + y
        sibling = (x, y, 1 - c)
        chips = _other_chips(x, y)
        barrier = pltpu.get_barrier_semaphore()
        for px, py in chips:
            pl.semaphore_signal(barrier, inc=1, device_id=(px, py, c), device_id_type=MESH)
        pl.semaphore_signal(barrier, inc=1, device_id=sibling, device_id_type=MESH)
        pl.semaphore_wait(barrier, len(chips) + 1)
        mine = pl.ds(pl.multiple_of(c * half, LANES), half)
        theirs = pl.ds(pl.multiple_of((1 - c) * half, LANES), half)
        own = [pltpu.make_async_copy(small_ref, small_out.at[me], loc.at[0]),
               pltpu.make_async_copy(shard_ref, out.at[me], loc.at[1])]
        for cp in own:
            cp.start()
        sends = []
        for j, (px, py) in enumerate(chips):
            sends.append(pltpu.make_async_remote_copy(small_ref, small_out.at[me], send.at[j], recv.at[j],
                                                      device_id=(px, py, c), device_id_type=MESH))
            sends.append(pltpu.make_async_remote_copy(shard_ref.at[:, mine], out.at[me, :, mine], send.at[3 + j], recv.at[3 + j],
                                                      device_id=(px, py, c), device_id_type=MESH))
        for cp in sends:
            cp.start()
        passed = []
        for j, (px, py) in enumerate(chips):
            slot = 2 * px + py
            pltpu.make_async_remote_copy(shard_ref.at[:, mine], out.at[slot, :, mine], send.at[3 + j], recv.at[3 + j],
                                         device_id=(px, py, c), device_id_type=MESH).wait_recv()
            cp = pltpu.make_async_remote_copy(out.at[slot, :, mine], out.at[slot, :, mine], send.at[6 + j], recv.at[6 + j],
                                              device_id=sibling, device_id_type=MESH)
            cp.start()
            passed.append(cp)
        for j, (px, py) in enumerate(chips):
            slot = 2 * px + py
            pltpu.make_async_remote_copy(small_ref, small_out.at[slot], send.at[j], recv.at[j],
                                         device_id=(px, py, c), device_id_type=MESH).wait_recv()
            pltpu.make_async_remote_copy(out.at[slot, :, theirs], out.at[slot, :, theirs], send.at[6 + j], recv.at[6 + j],
                                         device_id=sibling, device_id_type=MESH).wait_recv()
        for cp in sends + passed:
            cp.wait_send()
        for cp in own:
            cp.wait()

    return pl.kernel(
        body, name=name, mesh=_sequencer(),
        out_type=[jax.ShapeDtypeStruct((N_CHIPS,) + small.shape, small.dtype),
                  jax.ShapeDtypeStruct((N_CHIPS,) + shard.shape, shard.dtype)],
        scratch_types=[pltpu.SemaphoreType.DMA((9,)), pltpu.SemaphoreType.DMA((9,)), pltpu.SemaphoreType.DMA((2,))],
        compiler_params=pltpu.CompilerParams(collective_id=collective_id),
    )(small, shard)


def _sequencer():
    return plsc.ScalarSubcoreMesh(axis_name="sequencer", num_cores=1)


def _sibling_exchange_async(name, arrs, collective_id):
    n = len(arrs)

    def body(*refs):
        ins, outs = refs[:n], refs[n:2 * n]
        send, recv = refs[2 * n:]
        x, y, c = _position()
        sibling = (x, y, 1 - c)
        barrier = pltpu.get_barrier_semaphore()
        pl.semaphore_signal(barrier, inc=1, device_id=sibling, device_id_type=MESH)
        pl.semaphore_wait(barrier, 1)
        copies = [pltpu.make_async_remote_copy(ins[w], outs[w], send.at[w], recv.at[w], device_id=sibling,
                                               device_id_type=MESH) for w in range(n)]
        for cp in copies:
            cp.start()
        for cp in copies:
            cp.wait()

    return pl.kernel(
        body, name=name, out_type=[jax.ShapeDtypeStruct(a.shape, a.dtype) for a in arrs],
        scratch_types=[pltpu.SemaphoreType.DMA((n,)), pltpu.SemaphoreType.DMA((n,))],
        compiler_params=pltpu.CompilerParams(collective_id=collective_id), mesh=_sequencer(),
    )(*arrs)


def _scatter_chips_async(name, parts, collective_id):
    n = len(parts)

    def body(*refs):
        ins, outs = refs[:n], refs[n:2 * n]
        send, recv, loc = refs[2 * n:]
        x, y, c = _position()
        me = 2 * x + y
        chips = _other_chips(x, y)
        barrier = pltpu.get_barrier_semaphore()
        for px, py in chips:
            pl.semaphore_signal(barrier, inc=1, device_id=(px, py, c), device_id_type=MESH)
        pl.semaphore_wait(barrier, len(chips))
        started = []
        for w in range(n):
            own = pltpu.make_async_copy(ins[w].at[me], outs[w].at[me], loc.at[w])
            own.start()
            started.append(own)
        sends = []
        for w in range(n):
            for j, (px, py) in enumerate(chips):
                cp = pltpu.make_async_remote_copy(ins[w].at[2 * px + py], outs[w].at[me], send.at[3 * w + j],
                                                  recv.at[3 * w + j], device_id=(px, py, c), device_id_type=MESH)
                cp.start()
                sends.append(cp)
        for w in range(n):
            for j, (px, py) in enumerate(chips):
                pltpu.make_async_remote_copy(ins[w].at[me], outs[w].at[2 * px + py], send.at[3 * w + j], recv.at[3 * w + j],
                                             device_id=(px, py, c), device_id_type=MESH).wait_recv()
        for cp in sends:
            cp.wait_send()
        for own in started:
            own.wait()

    return pl.kernel(
        body, name=name, out_type=[jax.ShapeDtypeStruct(p.shape, p.dtype) for p in parts],
        scratch_types=[pltpu.SemaphoreType.DMA((3 * n,)), pltpu.SemaphoreType.DMA((3 * n,)), pltpu.SemaphoreType.DMA((n,))],
        compiler_params=pltpu.CompilerParams(collective_id=collective_id), mesh=_sequencer(),
    )(*parts)


def _allreduce_rows(buf):
    R = buf.shape[0]

    def body(in_ref, out_ref, land, send, recv):
        x, y, c = _position()
        me = 4 * x + 2 * y + c
        land[pl.ds(me, 1)] = in_ref[...][None]
        peers = []
        for mask in range(1, N_DEV):
            px = 1 - x if mask & 4 else x
            py = 1 - y if mask & 2 else y
            pc = 1 - c if mask & 1 else c
            peers.append((px, py, pc))
        sends = []
        for k, peer in enumerate(peers):
            cp = pltpu.make_async_remote_copy(in_ref, land.at[me], send.at[k], recv.at[k], device_id=peer, device_id_type=MESH)
            cp.start()
            sends.append(cp)
        for k, (px, py, pc) in enumerate(peers):
            pltpu.make_async_remote_copy(in_ref, land.at[4 * px + 2 * py + pc], send.at[k], recv.at[k],
                                         device_id=(px, py, pc), device_id_type=MESH).wait_recv()
        for cp in sends:
            cp.wait_send()
        tot = land[0]
        for i in range(1, N_DEV):
            tot = tot + land[i]
        out_ref[...] = tot

    vm = pl.BlockSpec(memory_space=pltpu.VMEM)
    return pl.pallas_call(
        body, name="allreduce_small", in_specs=[vm], out_specs=vm, out_shape=jax.ShapeDtypeStruct((R, LANES), F32),
        scratch_shapes=[pltpu.VMEM((N_DEV, R, LANES), F32), pltpu.SemaphoreType.DMA((N_DEV - 1,)),
                        pltpu.SemaphoreType.DMA((N_DEV - 1,))],
    )(buf)


def _tile2d(r, c, cap):
    if r <= cap:
        return r, c
    fits = [t for t in range(16, cap + 1, 16) if r % t == 0]
    return (max(fits), c) if fits else (r, 256)


def _pair_sum(name, a, b):
    n, r, c = a.shape
    tr, tc = _tile2d(r, c, 1024)

    def body(a_ref, b_ref, o_ref):
        o_ref[...] = (a_ref[...].astype(F32) + b_ref[...].astype(F32)).astype(BF16)

    blk = pl.BlockSpec((None, tr, tc), lambda s, i, j: (s, i, j))
    return pl.pallas_call(
        body, name=name, grid=(n, r // tr, c // tc), in_specs=[blk, blk], out_specs=blk,
        out_shape=jax.ShapeDtypeStruct(a.shape, BF16),
    )(a, b)


def _adamw_math(w, m, v, g):
    m2 = ADAM_B1 * m + (1.0 - ADAM_B1) * g
    v2 = ADAM_B2 * v + (1.0 - ADAM_B2) * (g * g)
    m_hat = m2 / (1.0 - ADAM_B1 ** ADAM_STEP)
    v_hat = v2 / (1.0 - ADAM_B2 ** ADAM_STEP)
    delta = -ADAM_LR * (m_hat / (jnp.sqrt(v_hat) + ADAM_EPS) + ADAM_WD * w)
    return delta, m2, v2


def _adamw(name, w, m, v, g):
    r, c = w.shape
    stacked = g.ndim == 3
    tr, tc = _tile2d(r, c, 256)

    def body(w_ref, m_ref, v_ref, g_ref, go_ref, d_ref, m2_ref, v2_ref):
        if stacked:
            gv = g_ref[0].astype(F32)
            for i in range(1, N_CHIPS):
                gv = gv + g_ref[i].astype(F32)
        else:
            gv = g_ref[...]
        delta, m2, v2 = _adamw_math(w_ref[...], m_ref[...], v_ref[...], gv)
        go_ref[...] = gv
        d_ref[...] = delta
        m2_ref[...] = m2
        v2_ref[...] = v2

    blk = pl.BlockSpec((tr, tc), lambda i, j: (i, j))
    g_spec = pl.BlockSpec((N_CHIPS, tr, tc), lambda i, j: (0, i, j)) if stacked else blk
    out = jax.ShapeDtypeStruct((r, c), F32)
    return pl.pallas_call(
        body, name=name, grid=(r // tr, c // tc), in_specs=[blk, blk, blk, g_spec], out_specs=[blk] * 4, out_shape=[out] * 4,
    )(w, m, v, g)


def _pack_rows(pieces):
    flat = jnp.concatenate([p.reshape(-1) for p in pieces])
    rows = flat.shape[0] // LANES
    pad = (-rows) % 8
    return jnp.pad(flat.reshape(rows, LANES), ((0, pad), (0, 0)))


def _unpack_rows(buf, shapes):
    flat = buf.reshape(-1)
    out, at = [], 0
    for s in shapes:
        size = math.prod(s)
        out.append(flat[at:at + size].reshape(s))
        at += size
    return out


SMALL_NAMES = ("norm1_g", "gf_up", "gf_b", "gb_up", "gb_b", "gla_norm_g", "attn_norm_g", "norm2_g", "conv_w", "conv_b",
               "final_norm_g")
BIG_NAMES = ("w_in", "w_out", "w_gate", "w_up", "w_down")
WEIGHT_ORDER = ("norm1_g", "w_in", "gf_up", "gf_b", "gb_up", "gb_b", "gla_norm_g", "attn_norm_g", "w_out", "norm2_g",
                "w_gate", "w_up", "conv_w", "conv_b", "w_down", "final_norm_g")


def kernel(x, norm1_g, w_in, gf_up, gf_b, gb_up, gb_b, gla_norm_g, attn_norm_g, w_out, norm2_g, w_gate, w_up, conv_w, conv_b, w_down, final_norm_g, loss_target, m_norm1_g, m_w_in, m_gf_up, m_gf_b, m_gb_up, m_gb_b, m_gla_norm_g, m_attn_norm_g, m_w_out, m_norm2_g, m_w_gate, m_w_up, m_conv_w, m_conv_b, m_w_down, m_final_norm_g, v_norm1_g, v_w_in, v_gf_up, v_gf_b, v_gb_up, v_gb_b, v_gla_norm_g, v_attn_norm_g, v_w_out, v_norm2_g, v_w_gate, v_w_up, v_conv_w, v_conv_b, v_w_down, v_final_norm_g):
    w = dict(norm1_g=norm1_g, w_in=w_in, gf_up=gf_up, gf_b=gf_b, gb_up=gb_up, gb_b=gb_b, gla_norm_g=gla_norm_g,
             attn_norm_g=attn_norm_g, w_out=w_out, norm2_g=norm2_g, w_gate=w_gate, w_up=w_up, conv_w=conv_w, conv_b=conv_b,
             w_down=w_down, final_norm_g=final_norm_g)
    m = dict(norm1_g=m_norm1_g, w_in=m_w_in, gf_up=m_gf_up, gf_b=m_gf_b, gb_up=m_gb_up, gb_b=m_gb_b, gla_norm_g=m_gla_norm_g,
             attn_norm_g=m_attn_norm_g, w_out=m_w_out, norm2_g=m_norm2_g, w_gate=m_w_gate, w_up=m_w_up, conv_w=m_conv_w,
             conv_b=m_conv_b, w_down=m_w_down, final_norm_g=m_final_norm_g)
    v = dict(norm1_g=v_norm1_g, w_in=v_w_in, gf_up=v_gf_up, gf_b=v_gf_b, gb_up=v_gb_up, gb_b=v_gb_b, gla_norm_g=v_gla_norm_g,
             attn_norm_g=v_attn_norm_g, w_out=v_w_out, norm2_g=v_norm2_g, w_gate=v_w_gate, w_up=v_w_up, conv_w=v_conv_w,
             conv_b=v_conv_b, w_down=v_w_down, final_norm_g=v_final_norm_g)
    S = x.shape[1]
    chip = 2 * lax.axis_index("x") + lax.axis_index("y")
    n_in = IN_W // N_CHIPS
    n_ff = D_FF // N_CHIPS
    n_gk = GLA_K // N_CHIPS

    def owned(t):
        return {k: (jnp.transpose(t[k][0]) if k == "w_in" else t[k][0]) for k in BIG_NAMES}

    own_w, own_m, own_v = owned(w), owned(m), owned(v)
    shard = {k: own_w[k].astype(BF16) for k in BIG_NAMES}
    small_shard = _pack_rows([gf_up[0], gb_up[0], conv_w[0]])
    small4, w_in4 = _gather_halves_async("gather_w_in", small_shard, shard["w_in"], 0)
    w_out4, w_gate4, w_up4 = _gather_chips_async("gather_w_mid", [shard["w_out"], shard["w_gate"], shard["w_up"]], 1)
    (w_down4,) = _gather_chips_async("gather_w_down", [shard["w_down"]], 2)
    w_in_t = w_in4.reshape(IN_W, D_MODEL)
    rows_up = GATE_RANK * n_gk // LANES
    rows_cw = 3 * n_ff // LANES
    gf_full = jnp.transpose(small4[:, 0:rows_up].reshape(N_CHIPS, GATE_RANK, n_gk), (1, 0, 2)).reshape(GATE_RANK, GLA_K)
    gb_full = jnp.transpose(small4[:, rows_up:2 * rows_up].reshape(N_CHIPS, GATE_RANK, n_gk), (1, 0, 2)).reshape(GATE_RANK, GLA_K)
    cw_full = jnp.transpose(small4[:, 2 * rows_up:2 * rows_up + rows_cw].reshape(N_CHIPS, 3, n_ff), (1, 0, 2)).reshape(3, D_FF)
    wg = jnp.zeros((Z_W, 2 * GLA_K), F32)
    wg = wg.at[0:GATE_RANK, 0:GLA_K].set(gf_full).at[GATE_RANK:2 * GATE_RANK, GLA_K:].set(gb_full).astype(BF16)
    gate_bias = jnp.concatenate([gf_b, gb_b], axis=1)

    pending, contributions, next_id = [], {}, [3]

    def as_shards(group, arrays):
        if group == "w_in":
            return dict(w_in=arrays["w_in_t"].reshape(N_CHIPS, n_in, D_MODEL))
        if group == "w_out":
            return dict(w_out=arrays["w_out"].reshape(N_CHIPS, D_MODEL // N_CHIPS, D_MODEL))
        if group == "w_down":
            return dict(w_down=arrays["w_down"].reshape(N_CHIPS, n_ff, D_MODEL))
        return arrays

    out = {}

    def swap(group, arrays):
        mine = as_shards(group, arrays)
        pending.append((group, mine, _sibling_exchange_async(f"sibling_{group}", list(mine.values()), next_id[0])))
        next_id[0] += 1

    def sum_and_send(anchor):
        tag, mine, theirs = pending.pop()
        sums = [_pair_sum(f"pair_sum_{k}", mine[k], _after(t, *anchor)) for k, t in zip(mine, theirs)]
        contributions.update(zip(mine, _scatter_chips_async(f"scatter_{tag}", sums, next_id[0])))
        next_id[0] += 1
        return sums

    def update(names, anchor):
        for k in names:
            res = _adamw(f"adamw_{k}", own_w[k], own_m[k], own_v[k], _after(contributions[k], *anchor))
            out[k] = [(jnp.transpose(r) if k == "w_in" else r)[None] for r in res]
        return [out[k][0] for k in names]

    def on_grad(event, arrays):
        anchor = list(arrays.values())
        held = []
        if event in ("w_gate_w_up", "w_out", "mid", "last"):
            held += sum_and_send(anchor)
        if event == "mid":
            held += update(("w_down", "w_gate", "w_up"), anchor)
        if event == "last":
            held += update(("w_out",), anchor)
        if event in ("w_down", "w_gate_w_up", "w_out", "w_in"):
            swap(event, arrays)
        return held

    grad_x, _, small = _local_step(
        x[0], loss_target[0], norm1_g, w_in_t, wg, gate_bias, gla_norm_g, attn_norm_g,
        w_out4.reshape(D_MODEL, D_MODEL), norm2_g, w_gate4, w_up4, cw_full, conv_b, w_down4.reshape(D_FF, D_MODEL), final_norm_g,
        on_grad=on_grad)
    update(("w_in",), [grad_x])

    d_gf_up = small["wg"][0:GATE_RANK, 0:GLA_K]
    d_gb_up = small["wg"][GATE_RANK:2 * GATE_RANK, GLA_K:]
    pieces = [small["loss"], small["norm1_g"], d_gf_up, small["gate_bias"][:, :GLA_K], d_gb_up, small["gate_bias"][:, GLA_K:],
              small["gla_norm_g"], small["attn_norm_g"], small["norm2_g"], small["conv_w"], small["conv_b"], small["final_norm_g"]]
    total = _allreduce_rows(_pack_rows(pieces))
    summed = _unpack_rows(total, [p.shape for p in pieces])
    loss = summed[0][0, 0]
    g_small = dict(zip(SMALL_NAMES, summed[1:]))
    g_small["gf_up"] = lax.dynamic_slice_in_dim(g_small["gf_up"], chip * n_gk, n_gk, axis=1)
    g_small["gb_up"] = lax.dynamic_slice_in_dim(g_small["gb_up"], chip * n_gk, n_gk, axis=1)
    g_small["conv_w"] = lax.dynamic_slice_in_dim(g_small["conv_w"], chip * n_ff, n_ff, axis=1)
    packed = [_pack_rows([t[k] for k in SMALL_NAMES]) for t in (w, m, v, g_small)]
    res = _adamw("adamw_small", *packed)
    shapes = [w[k].shape for k in SMALL_NAMES]
    for k, vals in zip(SMALL_NAMES, zip(*[_unpack_rows(r, shapes) for r in res])):
        out[k] = list(vals)

    grads, deltas, new_m, new_v = ([out[k][i] for k in WEIGHT_ORDER] for i in range(4))
    return (loss, grad_x[None], *grads, *deltas, *new_m, *new_v)
```

```python
import functools
import math

import jax
import jax.numpy as jnp
from jax import lax
from jax.experimental import pallas as pl
from jax.experimental.pallas import tpu as pltpu
from jax.experimental.pallas import tpu_sc as plsc

F32 = jnp.float32
BF16 = jnp.bfloat16

D_MODEL = 2048
ATTN_W = 1024
HEAD = 128
N_HEADS = 8
N_SIDE = 64
DILATIONS = (1, 4, 16)
ROPE_THETA = 500000.0
ROPE_DIM = 32
GLA_K = 512
GLA_V = 1024
GLA_HEADS = 4
GLA_DK = 128
GLA_DV = 256
GATE_RANK = 16
GATE_NORM = 16.0
CHUNK = 64
IN_MAIN = 6144
IN_W = 6176
Z_W = IN_W - IN_MAIN
D_FF = 5632
EPS = 1e-6
N_CHIPS = 4
N_DEV = 8
LANES = 128

ADAM_LR = 0.001
ADAM_B1 = 0.9
ADAM_B2 = 0.999
ADAM_EPS = 1e-08
ADAM_WD = 0.01
ADAM_STEP = 10

NEG = -1e30
MESH = pl.DeviceIdType.MESH
ANY = pl.BlockSpec(memory_space=pl.ANY)

NN = ((1,), (0,))
NT = ((1,), (1,))
TN = ((0,), (0,))


def _dot(a, b, dims=NN):
    return lax.dot_general(a, b, (dims, ((), ())), preferred_element_type=F32)


def _sigmoid(x):
    return 0.5 * jnp.tanh(0.5 * x) + 0.5


def _after(x, *deps):
    return lax.optimization_barrier((x,) + deps)[0]


def _matmul(name, pairs, grid, out_shape, out_spec, nk, res=None, into=None, first=None):
    n_in = 2 * len(pairs) + (res is not None)
    dims = [p[4] for p in pairs]

    n_ops = n_in + (into is not None) + 2 * (first is not None)

    def body(*refs):
        ins, o_ref = refs[:n_in], refs[n_ops]

        def partial_sum():
            tot = None
            for p, dn in enumerate(dims):
                a, b = ins[2 * p][...], ins[2 * p + 1][...]
                t = _dot(a.astype(BF16), b.astype(BF16), dn)
                tot = t if tot is None else tot + t
            return tot

        if nk == 1:
            t = partial_sum()
            if res is not None:
                t = t + ins[-1][...]
            o_ref[...] = t.astype(o_ref.dtype)
        else:
            acc_ref = refs[n_ops + 1]
            k = pl.program_id(2)

            @pl.when(k == 0)
            def _():
                if first is not None:
                    start = _dot(refs[n_in][...].astype(BF16), refs[n_in + 1][...].astype(BF16), first[4])
                    acc_ref[...] = start + ins[-1][...] if res is not None else start
                elif res is not None:
                    acc_ref[...] = ins[-1][...]
                else:
                    acc_ref[...] = jnp.zeros_like(acc_ref)

            acc_ref[...] += partial_sum()

            @pl.when(k == nk - 1)
            def _():
                o_ref[...] = acc_ref[...].astype(o_ref.dtype)

    operands, in_specs = [], []
    for a, a_spec, b, b_spec, _ in pairs:
        operands += [a, b]
        in_specs += [a_spec, b_spec]
    if res is not None:
        operands.append(res[0])
        in_specs.append(res[1])
    if first is not None:
        assert nk > 1
        operands += [first[0], first[2]]
        in_specs += [first[1], first[3]]
    acc_shape = tuple(s for s in out_spec.block_shape if s is not None)
    scratch = [pltpu.VMEM(acc_shape, F32)] if nk > 1 else []
    aliases = {}
    if into is not None:
        aliases = {len(operands): 0}
        operands.append(into)
        in_specs.append(ANY)
    return pl.pallas_call(
        body, name=name, grid=grid, in_specs=in_specs, out_specs=out_spec, out_shape=out_shape, scratch_shapes=scratch,
        input_output_aliases=aliases,
    )(*operands)


def _mm_nn(name, a, b, tm, tn, out_dtype, res=None):
    M, K = a.shape
    N = b.shape[1]
    pairs = [(a, pl.BlockSpec((tm, K), lambda j, i: (i, 0)), b, pl.BlockSpec((K, tn), lambda j, i: (0, j)), NN)]
    r = None if res is None else (res, pl.BlockSpec((tm, tn), lambda j, i: (i, j)))
    return _matmul(name, pairs, (N // tn, M // tm), jax.ShapeDtypeStruct((M, N), out_dtype),
                   pl.BlockSpec((tm, tn), lambda j, i: (i, j)), 1, r)


def _mm_nn_sharded(name, a, b4, tm, out_dtype):
    M, K = a.shape
    n = b4.shape[2]
    pairs = [(a, pl.BlockSpec((tm, K), lambda j, i: (i, 0)), b4, pl.BlockSpec((None, K, n), lambda j, i: (j, 0, 0)), NN)]
    return _matmul(name, pairs, (N_CHIPS, M // tm), jax.ShapeDtypeStruct((M, N_CHIPS * n), out_dtype),
                   pl.BlockSpec((tm, n), lambda j, i: (i, j)), 1)


def _mm_nt(name, a, b, tm, tn, out_dtype, res=None, n_out=None):
    M, K = a.shape
    N = b.shape[0] if n_out is None else n_out
    pairs = [(a, pl.BlockSpec((tm, K), lambda j, i: (i, 0)), b, pl.BlockSpec((tn, K), lambda j, i: (j, 0)), NT)]
    r = None if res is None else (res, pl.BlockSpec((tm, tn), lambda j, i: (i, j)))
    return _matmul(name, pairs, (N // tn, M // tm), jax.ShapeDtypeStruct((M, N), out_dtype),
                   pl.BlockSpec((tm, tn), lambda j, i: (i, j)), 1, r)


def _mm_tn(name, a, g, tka, tn, tmm, out_dtype, out3=None, rows_out=None):
    M, Ka = a.shape
    N = g.shape[1]
    pairs = [(a, pl.BlockSpec((tmm, tka), lambda i, j, k: (k, i)), g, pl.BlockSpec((tmm, tn), lambda i, j, k: (k, j)), TN)]
    if out3 is None:
        shape, spec = (Ka if rows_out is None else rows_out, N), pl.BlockSpec((tka, tn), lambda i, j, k: (i, j))
    else:
        shape, spec = (N // out3, Ka, out3), pl.BlockSpec((None, tka, tn), lambda i, j, k: (j, i, 0))
    return _matmul(name, pairs, (Ka // tka, N // tn, M // tmm), jax.ShapeDtypeStruct(shape, out_dtype), spec, M // tmm)


def _rms_fwd(name, x, g, tm=512):
    S, D = x.shape

    def body(x_ref, g_ref, o_ref):
        xv = x_ref[...]
        r = lax.rsqrt(jnp.mean(xv * xv, axis=-1, keepdims=True) + EPS)
        o_ref[...] = (xv * r * g_ref[...]).astype(o_ref.dtype)

    return pl.pallas_call(
        body, name=name, grid=(S // tm,),
        in_specs=[pl.BlockSpec((tm, D), lambda i: (i, 0)), pl.BlockSpec((1, D), lambda i: (0, 0))],
        out_specs=pl.BlockSpec((tm, D), lambda i: (i, 0)), out_shape=jax.ShapeDtypeStruct((S, D), BF16),
    )(x, g)


def _rms_bwd(name, x, g, dn, dres, tm=512):
    S, D = x.shape

    def body(x_ref, g_ref, dn_ref, dres_ref, dx_ref, dxb_ref, dg_ref):
        i = pl.program_id(0)

        @pl.when(i == 0)
        def _():
            dg_ref[...] = jnp.zeros_like(dg_ref)

        xv = x_ref[...]
        r = lax.rsqrt(jnp.mean(xv * xv, axis=-1, keepdims=True) + EPS)
        xhat = xv * r
        dnv = dn_ref[...].astype(F32)
        dg_ref[...] += jnp.sum(dnv * xhat, axis=0, keepdims=True)
        t = dnv * g_ref[...]
        dx = r * (t - xhat * jnp.mean(t * xhat, axis=-1, keepdims=True)) + dres_ref[...]
        dx_ref[...] = dx
        dxb_ref[...] = dx.astype(BF16)

    row = pl.BlockSpec((tm, D), lambda i: (i, 0))
    vec = pl.BlockSpec((1, D), lambda i: (0, 0))
    return pl.pallas_call(
        body, name=name, grid=(S // tm,), in_specs=[row, vec, row, row], out_specs=[row, row, vec],
        out_shape=[jax.ShapeDtypeStruct((S, D), F32), jax.ShapeDtypeStruct((S, D), BF16), jax.ShapeDtypeStruct((1, D), F32)],
    )(x, g, dn, dres)


def _final_loss(h2, g, target, tm=512):
    S, D = h2.shape

    def body(x_ref, g_ref, t_ref, loss_ref, dg_ref, dx_ref, dxb_ref):
        i = pl.program_id(0)

        @pl.when(i == 0)
        def _():
            loss_ref[...] = jnp.zeros_like(loss_ref)
            dg_ref[...] = jnp.zeros_like(dg_ref)

        xv = x_ref[...]
        r = lax.rsqrt(jnp.mean(xv * xv, axis=-1, keepdims=True) + EPS)
        xhat = xv * r
        gv = g_ref[...]
        diff = xhat * gv - t_ref[...]
        per_tok = jnp.mean(diff * diff, axis=-1, keepdims=True)
        loss_ref[...] += 0.5 * jnp.sum(per_tok, axis=0, keepdims=True)
        dy = diff * (1.0 / D)
        dg_ref[...] += jnp.sum(dy * xhat, axis=0, keepdims=True)
        t = dy * gv
        dx = r * (t - xhat * jnp.mean(t * xhat, axis=-1, keepdims=True))
        dx_ref[...] = dx
        dxb_ref[...] = dx.astype(BF16)

    row = pl.BlockSpec((tm, D), lambda i: (i, 0))
    vec = pl.BlockSpec((1, D), lambda i: (0, 0))
    return pl.pallas_call(
        body, name="final_loss", grid=(S // tm,), in_specs=[row, vec, row],
        out_specs=[pl.BlockSpec((1, LANES), lambda i: (0, 0)), vec, row, row],
        out_shape=[jax.ShapeDtypeStruct((1, LANES), F32), jax.ShapeDtypeStruct((1, D), F32),
                   jax.ShapeDtypeStruct((S, D), F32), jax.ShapeDtypeStruct((S, D), BF16)],
    )(h2, g, target)


def _rope_tables(S):
    pos = jnp.arange(S, dtype=F32)
    inv_freq = ROPE_THETA ** (-jnp.arange(0, ROPE_DIM, 2, dtype=F32) / ROPE_DIM)
    ang = pos[:, None] * inv_freq[None, :]
    cos, sin = jnp.cos(ang), jnp.sin(ang)
    half = ROPE_DIM // 2
    rest = HEAD - ROPE_DIM
    z_h, z_r = jnp.zeros((S, half), F32), jnp.zeros((S, rest), F32)
    tab_c = jnp.concatenate([cos, cos, jnp.ones((S, rest), F32)], axis=1)
    tab_up = jnp.concatenate([z_h, sin, z_r], axis=1)
    tab_dn = jnp.concatenate([-sin, z_h, z_r], axis=1)
    return tab_c, tab_up, tab_dn


def _rope_head(t, c, up, dn):
    half = ROPE_DIM // 2
    return t * c + pltpu.roll(t, half, axis=1) * up + pltpu.roll(t, HEAD - half, axis=1) * dn


def _rope_fwd(proj, tabs, tm=512):
    S = proj.shape[0]
    W = 2 * ATTN_W

    def body(p_ref, c_ref, up_ref, dn_ref, o_ref):
        c, up, dn = c_ref[...], up_ref[...], dn_ref[...]
        for h in range(W // HEAD):
            sl = slice(h * HEAD, (h + 1) * HEAD)
            o_ref[:, sl] = _rope_head(p_ref[:, sl].astype(F32), c, up, dn).astype(BF16)

    tab = pl.BlockSpec((tm, HEAD), lambda i: (i, 0))
    return pl.pallas_call(
        body, name="rope_fwd", grid=(S // tm,), in_specs=[pl.BlockSpec((tm, W), lambda i: (i, 0)), tab, tab, tab],
        out_specs=pl.BlockSpec((tm, W), lambda i: (i, 0)), out_shape=jax.ShapeDtypeStruct((S, W), BF16),
    )(proj, *tabs)


def _attn_grad_merge(dqs, dks, dvs, tabs, tm=256):
    S = dqs[0].shape[0]

    def body(*refs):
        q_refs, k_refs, v_refs = refs[0:3], refs[3:6], refs[6:9]
        c, up, dn = refs[9][...], refs[10][...], refs[11][...]
        o_ref = refs[12]
        for h in range(N_HEADS):
            sl = slice(h * HEAD, (h + 1) * HEAD)
            for part, rs in ((0, q_refs), (1, k_refs)):
                t = rs[0][:, sl].astype(F32) + rs[1][:, sl].astype(F32) + rs[2][:, sl].astype(F32)
                osl = slice(part * ATTN_W + h * HEAD, part * ATTN_W + (h + 1) * HEAD)
                o_ref[:, osl] = _rope_head(t, c, -up, -dn).astype(BF16)
        o_ref[:, 2 * ATTN_W:] = (v_refs[0][...].astype(F32) + v_refs[1][...].astype(F32)
                                 + v_refs[2][...].astype(F32)).astype(BF16)

    blk = pl.BlockSpec((tm, ATTN_W), lambda i: (i, 0))
    tab = pl.BlockSpec((tm, HEAD), lambda i: (i, 0))
    return pl.pallas_call(
        body, name="attn_grad_merge", grid=(S // tm,), in_specs=[blk] * 9 + [tab] * 3,
        out_specs=pl.BlockSpec((tm, 3 * ATTN_W), lambda i: (i, 0)), out_shape=jax.ShapeDtypeStruct((S, IN_MAIN), BF16),
    )(*dqs, *dks, *dvs, *tabs)


SUB = 128
WIN = SUB + 2 * N_SIDE
Q_COL, K_COL, V_COL = 0, ATTN_W // HEAD, 2 * ATTN_W // HEAD


class _AttnGeo:
    def __init__(self, S, d):
        self.S, self.d, self.L = S, d, S // d
        self.halo = N_SIDE * d
        self.TB = min(2048, S)
        self.W = self.TB + 2 * self.halo
        self.n_sub = self.TB // SUB
        self.grid = (S // self.TB, N_HEADS)
        self.dt = F32 if d > 1 else BF16
        self.su = min(d, 4)
        self.sb = d // self.su
        assert self.TB % (SUB * d) == 0 and self.TB % self.halo == 0

    def specs(self, width, col0, per_head=True):
        ratio = self.TB // self.halo
        last = self.S // self.halo - 1
        col = (lambda h: col0 + h) if per_head else (lambda h: col0)
        cur = pl.BlockSpec((self.TB, width), lambda i, h: (i, col(h)))
        prev = pl.BlockSpec((self.halo, width), lambda i, h: (jnp.maximum(i * ratio - 1, 0), col(h)))
        nxt = pl.BlockSpec((self.halo, width), lambda i, h: (jnp.minimum((i + 1) * ratio, last), col(h)))
        return cur, prev, nxt

    def scratch(self, rows, dtype=None):
        nat = pltpu.VMEM((rows, LANES), self.dt if dtype is None else dtype)
        return [nat] if self.sb == 1 else [nat, pltpu.VMEM((rows, LANES), F32)]

    def bind(self, refs):
        nat = next(refs)
        return (nat, nat) if self.sb == 1 else (nat, next(refs))

    def spread(self, pair):
        nat, streams = pair
        if self.sb > 1:
            n = nat.shape[0] // self.sb
            for a in range(self.sb):
                streams[a * n:(a + 1) * n, :] = nat[pl.ds(a, n, stride=self.sb), :]
        return streams

    def gather(self, pair):
        nat, streams = pair
        if self.sb > 1:
            n = nat.shape[0] // self.sb
            for a in range(self.sb):
                nat[pl.ds(a, n, stride=self.sb), :] = streams[a * n:(a + 1) * n, :]
        return nat

    def rows(self, sub, n, total):
        res, blk = sub % self.d, sub // self.d
        a, b = res % self.sb, res // self.sb
        start = a * (total // self.sb) + b + self.su * SUB * blk
        return pl.ds(start, n, stride=self.su) if self.su > 1 else pl.ds(start, n)

    def band(self):
        row = lax.broadcasted_iota(jnp.int32, (SUB, WIN), 0)
        col = lax.broadcasted_iota(jnp.int32, (SUB, WIN), 1)
        return (col >= row) & (col <= row + 2 * N_SIDE), col

    def mask(self, sub, band):
        inside, col = band
        blk, n_blk = sub // self.d, self.TB // (SUB * self.d)
        base = pl.program_id(0) * (self.TB // self.d) + SUB * blk
        if blk == 0:
            inside = inside & (col >= N_SIDE - base)
        if blk == n_blk - 1:
            inside = inside & (col < self.L + N_SIDE - base)
        return inside

    def fill(self, dst, c_ref):
        dst[...] = c_ref[...].astype(dst.dtype)

    def fill_window(self, dst, p_ref, c_ref, n_ref):
        dst[0:self.halo] = p_ref[...].astype(dst.dtype)
        dst[self.halo:self.halo + self.TB] = c_ref[...].astype(dst.dtype)
        dst[self.halo + self.TB:] = n_ref[...].astype(dst.dtype)


def _lane_of(tile, h):
    lane = lax.broadcasted_iota(jnp.int32, tile.shape, 1)
    return jnp.sum(jnp.where(lane == h, tile, 0.0), axis=1, keepdims=True)


def _attn_fwd(qk, proj, d):
    S = qk.shape[0]
    geo = _AttnGeo(S, d)
    scale = HEAD ** -0.5

    def body(q_ref, kp, kc, kn, vp, vc, vn, o_ref, lse_ref, *scratch):
        h = pl.program_id(1)
        refs = iter(scratch)
        q_p, k_p, v_p, o_p, l_p = (geo.bind(refs) for _ in range(5))
        geo.fill(q_p[0], q_ref)
        geo.fill_window(k_p[0], kp, kc, kn)
        geo.fill_window(v_p[0], vp, vc, vn)
        qs, ks, vs = geo.spread(q_p), geo.spread(k_p), geo.spread(v_p)
        os, ls = o_p[1], l_p[1]
        band = geo.band()
        for sub in range(geo.n_sub):
            rq, rw = geo.rows(sub, SUB, geo.TB), geo.rows(sub, WIN, geo.W)
            q_r, k_r, v_r = qs[rq, :].astype(BF16), ks[rw, :].astype(BF16), vs[rw, :].astype(BF16)
            s = jnp.where(geo.mask(sub, band), _dot(q_r, k_r, NT) * scale, NEG)
            m = jnp.max(s, axis=1, keepdims=True)
            p = jnp.exp(s - m)
            l = jnp.sum(p, axis=1, keepdims=True)
            os[rq, :] = _dot(p.astype(BF16), v_r) / l
            ls[rq, :] = jnp.broadcast_to(m + jnp.log(l), (SUB, LANES))
        o_ref[...] = geo.gather(o_p)[...].astype(BF16)

        @pl.when(h == 0)
        def _():
            lse_ref[...] = jnp.zeros_like(lse_ref)

        lane = lax.broadcasted_iota(jnp.int32, (geo.TB, LANES), 1)
        lse_ref[...] = jnp.where(lane == h, geo.gather(l_p)[...], lse_ref[...])

    q_cur, _, _ = geo.specs(HEAD, Q_COL)
    k_specs = geo.specs(HEAD, K_COL)
    v_specs = geo.specs(HEAD, V_COL)
    stat = pl.BlockSpec((geo.TB, LANES), lambda i, h: (i, 0))
    return pl.pallas_call(
        body, name=f"attn_fwd_d{d}", grid=geo.grid,
        in_specs=[q_cur, k_specs[1], k_specs[0], k_specs[2], v_specs[1], v_specs[0], v_specs[2]],
        out_specs=[q_cur, stat],
        out_shape=[jax.ShapeDtypeStruct((S, ATTN_W), BF16), jax.ShapeDtypeStruct((S, LANES), F32)],
        scratch_shapes=(geo.scratch(geo.TB) + geo.scratch(geo.W) + geo.scratch(geo.W) + geo.scratch(geo.TB, F32)
                        + geo.scratch(geo.TB, F32)),
    )(qk, qk, qk, qk, proj, proj, proj)


def _attn_combine(outs, lses, g, tm=256):
    S = outs[0].shape[0]

    def body(o1, o2, o3, l1, l2, l3, g_ref, ao_ref, o_ref, lse_ref):
        a1, a2, a3 = l1[...], l2[...], l3[...]
        mx = jnp.maximum(jnp.maximum(a1, a2), a3)
        e1, e2, e3 = jnp.exp(a1 - mx), jnp.exp(a2 - mx), jnp.exp(a3 - mx)
        den = e1 + e2 + e3
        lse_ref[...] = mx + jnp.log(den)
        head_of_col = lax.broadcasted_iota(jnp.int32, (LANES, ATTN_W), 1) // HEAD
        spread = (lax.broadcasted_iota(jnp.int32, (LANES, ATTN_W), 0) == head_of_col).astype(BF16)

        def wide(e):
            wgt = e / den
            hi = wgt.astype(BF16)
            lo = (wgt - hi.astype(F32)).astype(BF16)
            return _dot(hi, spread) + _dot(lo, spread)

        ov = wide(e1) * o1[...].astype(F32) + wide(e2) * o2[...].astype(F32) + wide(e3) * o3[...].astype(F32)
        o_ref[...] = ov
        r = lax.rsqrt(jnp.mean(ov * ov, axis=-1, keepdims=True) + EPS)
        ao_ref[...] = (ov * r * g_ref[...]).astype(BF16)

    blk = pl.BlockSpec((tm, ATTN_W), lambda i: (i, 0))
    ls = pl.BlockSpec((tm, LANES), lambda i: (i, 0))
    return pl.pallas_call(
        body, name="attn_combine", grid=(S // tm,),
        in_specs=[blk, blk, blk, ls, ls, ls, pl.BlockSpec((1, ATTN_W), lambda i: (0, 0))], out_specs=[blk, blk, ls],
        out_shape=[jax.ShapeDtypeStruct((S, D_MODEL), BF16), jax.ShapeDtypeStruct((S, ATTN_W), F32),
                   jax.ShapeDtypeStruct((S, LANES), F32)],
    )(*outs, *lses, g)


def _attn_norm_bwd(o, g, dao, tm=256):
    S = o.shape[0]

    def body(o_ref, g_ref, dao_ref, do_ref, dl_ref, dg_ref):
        i = pl.program_id(0)

        @pl.when(i == 0)
        def _():
            dg_ref[...] = jnp.zeros_like(dg_ref)

        ov = o_ref[...]
        r = lax.rsqrt(jnp.mean(ov * ov, axis=-1, keepdims=True) + EPS)
        ohat = ov * r
        dn = dao_ref[...].astype(F32)
        dg_ref[...] += jnp.sum(dn * ohat, axis=0, keepdims=True)
        t = dn * g_ref[...]
        do = r * (t - ohat * jnp.mean(t * ohat, axis=-1, keepdims=True))
        do_ref[...] = do.astype(BF16)
        prod = do * ov
        lane = lax.broadcasted_iota(jnp.int32, (tm, LANES), 1)
        tile = jnp.zeros((tm, LANES), F32)
        for h in range(N_HEADS):
            tile = jnp.where(lane == h, jnp.sum(prod[:, h * HEAD:(h + 1) * HEAD], axis=1, keepdims=True), tile)
        dl_ref[...] = tile

    blk = pl.BlockSpec((tm, ATTN_W), lambda i: (i, 0))
    vec = pl.BlockSpec((1, ATTN_W), lambda i: (0, 0))
    return pl.pallas_call(
        body, name="attn_norm_bwd", grid=(S // tm,),
        in_specs=[blk, vec, pl.BlockSpec((tm, ATTN_W), lambda i: (i, 0))],
        out_specs=[blk, pl.BlockSpec((tm, LANES), lambda i: (i, 0)), vec],
        out_shape=[jax.ShapeDtypeStruct((S, ATTN_W), BF16), jax.ShapeDtypeStruct((S, LANES), F32),
                   jax.ShapeDtypeStruct((1, ATTN_W), F32)],
    )(o, g, dao)


def _attn_bwd_dq(qk, proj, do, lse, delta, d):
    S = qk.shape[0]
    geo = _AttnGeo(S, d)
    scale = HEAD ** -0.5

    def body(q_ref, kp, kc, kn, vp, vc, vn, do_ref, lse_ref, dl_ref, dq_ref, *scratch):
        h = pl.program_id(1)
        refs = iter(scratch)
        q_p, k_p, v_p, do_p, lse_p, dl_p, dq_p = (geo.bind(refs) for _ in range(7))
        geo.fill(q_p[0], q_ref)
        geo.fill(do_p[0], do_ref)
        geo.fill(lse_p[0], lse_ref)
        geo.fill(dl_p[0], dl_ref)
        geo.fill_window(k_p[0], kp, kc, kn)
        geo.fill_window(v_p[0], vp, vc, vn)
        qs, ks, vs, dos = geo.spread(q_p), geo.spread(k_p), geo.spread(v_p), geo.spread(do_p)
        lses, dls = geo.spread(lse_p), geo.spread(dl_p)
        dqs = dq_p[1]
        band = geo.band()
        for sub in range(geo.n_sub):
            rq, rw = geo.rows(sub, SUB, geo.TB), geo.rows(sub, WIN, geo.W)
            q_r, k_r, v_r = qs[rq, :].astype(BF16), ks[rw, :].astype(BF16), vs[rw, :].astype(BF16)
            lse_c, dl_c = _lane_of(lses[rq, :], h), _lane_of(dls[rq, :], h)
            s = _dot(q_r, k_r, NT) * scale
            p = jnp.where(geo.mask(sub, band), jnp.exp(s - lse_c), 0.0)
            dp = _dot(dos[rq, :].astype(BF16), v_r, NT)
            ds = (p * (dp - dl_c) * scale).astype(BF16)
            dqs[rq, :] = _dot(ds, k_r)
        dq_ref[...] = geo.gather(dq_p)[...].astype(BF16)

    cur, _, _ = geo.specs(HEAD, 0)
    k_specs = geo.specs(HEAD, K_COL)
    v_specs = geo.specs(HEAD, V_COL)
    stat = pl.BlockSpec((geo.TB, LANES), lambda i, h: (i, 0))
    return pl.pallas_call(
        body, name=f"attn_bwd_dq_d{d}", grid=geo.grid,
        in_specs=[cur, k_specs[1], k_specs[0], k_specs[2], v_specs[1], v_specs[0], v_specs[2], cur, stat, stat],
        out_specs=cur, out_shape=jax.ShapeDtypeStruct((S, ATTN_W), BF16),
        scratch_shapes=(geo.scratch(geo.TB) + geo.scratch(geo.W) + geo.scratch(geo.W) + geo.scratch(geo.TB)
                        + geo.scratch(geo.TB, F32) + geo.scratch(geo.TB, F32) + geo.scratch(geo.TB, F32)),
    )(qk, qk, qk, qk, proj, proj, proj, do, lse, delta)


def _attn_bwd_dkv(qk, proj, do, lse, delta, d):
    S = qk.shape[0]
    geo = _AttnGeo(S, d)
    scale = HEAD ** -0.5

    def body(k_ref, v_ref, qp, qc, qn, dop, doc, don, lp, lc, ln, dlp, dlc, dln, dk_ref, dv_ref, *scratch):
        h = pl.program_id(1)
        refs = iter(scratch)
        k_p, v_p, q_p, do_p, lw_p, dlw_p, dk_p, dv_p = (geo.bind(refs) for _ in range(8))
        geo.fill(k_p[0], k_ref)
        geo.fill(v_p[0], v_ref)
        geo.fill_window(q_p[0], qp, qc, qn)
        geo.fill_window(do_p[0], dop, doc, don)
        geo.fill_window(lw_p[0], lp, lc, ln)
        geo.fill_window(dlw_p[0], dlp, dlc, dln)
        ks, vs, qs, dos = geo.spread(k_p), geo.spread(v_p), geo.spread(q_p), geo.spread(do_p)
        lws, dlws = geo.spread(lw_p), geo.spread(dlw_p)
        dks, dvs = dk_p[1], dv_p[1]
        head = lax.broadcasted_iota(jnp.int32, (LANES, WIN), 0)
        band = geo.band()
        for sub in range(geo.n_sub):
            rq, rw = geo.rows(sub, SUB, geo.TB), geo.rows(sub, WIN, geo.W)
            k_r, v_r = ks[rq, :].astype(BF16), vs[rq, :].astype(BF16)
            q_w, do_w = qs[rw, :].astype(BF16), dos[rw, :].astype(BF16)
            lse_row = jnp.sum(jnp.where(head == h, lws[rw, :].T, 0.0), axis=0, keepdims=True)
            dl_row = jnp.sum(jnp.where(head == h, dlws[rw, :].T, 0.0), axis=0, keepdims=True)
            st = _dot(k_r, q_w, NT) * scale
            pt = jnp.where(geo.mask(sub, band), jnp.exp(st - lse_row), 0.0)
            dvs[rq, :] = _dot(pt.astype(BF16), do_w)
            dpt = _dot(v_r, do_w, NT)
            dst = (pt * (dpt - dl_row) * scale).astype(BF16)
            dks[rq, :] = _dot(dst, q_w)
        dk_ref[...] = geo.gather(dk_p)[...].astype(BF16)
        dv_ref[...] = geo.gather(dv_p)[...].astype(BF16)

    q_specs = geo.specs(HEAD, Q_COL)
    k_cur, _, _ = geo.specs(HEAD, K_COL)
    v_cur, _, _ = geo.specs(HEAD, V_COL)
    do_specs = geo.specs(HEAD, 0)
    st_specs = geo.specs(LANES, 0, per_head=False)
    cur = do_specs[0]
    return pl.pallas_call(
        body, name=f"attn_bwd_dkv_d{d}", grid=geo.grid,
        in_specs=[k_cur, v_cur, q_specs[1], q_specs[0], q_specs[2], do_specs[1], do_specs[0], do_specs[2],
                  st_specs[1], st_specs[0], st_specs[2], st_specs[1], st_specs[0], st_specs[2]],
        out_specs=[cur, cur],
        out_shape=[jax.ShapeDtypeStruct((S, ATTN_W), BF16), jax.ShapeDtypeStruct((S, ATTN_W), BF16)],
        scratch_shapes=(geo.scratch(geo.TB) + geo.scratch(geo.TB) + geo.scratch(geo.W) + geo.scratch(geo.W)
                        + geo.scratch(geo.W, F32) + geo.scratch(geo.W, F32) + geo.scratch(geo.TB, F32)
                        + geo.scratch(geo.TB, F32)),
    )(qk, proj, qk, qk, qk, do, do, do, lse, lse, lse, delta, delta, delta)


def _cumsum_rows(x, reverse):
    n = x.shape[0]
    row = lax.broadcasted_iota(jnp.int32, x.shape, 0)
    s = 1
    while s < n:
        if reverse:
            x = x + jnp.where(row < n - s, pltpu.roll(x, n - s, axis=0), 0.0)
        else:
            x = x + jnp.where(row >= s, pltpu.roll(x, s, axis=0), 0.0)
        s *= 2
    return x


GLA_GROUP = 8


def _gla_rows(cc):
    return slice(cc * CHUNK, (cc + 1) * CHUNK)


def _gla_chunk_terms(q_ref, k_ref, v_ref, g_ref, h, reverse, rows, b_ref=None):
    ksl = slice(h * GLA_DK, (h + 1) * GLA_DK)
    q = q_ref[rows, ksl].astype(F32) * (GLA_DK ** -0.5)
    k = k_ref[rows, ksl].astype(F32)
    v = v_ref[rows, h * GLA_DV:(h + 1) * GLA_DV]
    b = _cumsum_rows(g_ref[rows, ksl], reverse) if b_ref is None else b_ref[rows, ksl]
    r_ref = CHUNK // 2 if reverse else CHUNK // 2 - 1
    r_last = 0 if reverse else CHUNK - 1
    b_ref, b_last = b[r_ref:r_ref + 1, :], b[r_last:r_last + 1, :]
    ii = lax.broadcasted_iota(jnp.int32, (CHUNK, CHUNK), 0)
    jj = lax.broadcasted_iota(jnp.int32, (CHUNK, CHUNK), 1)
    causal = (jj >= ii) if reverse else (jj <= ii)
    e_q, e_k = jnp.exp(b - b_ref), jnp.exp(b_ref - b)
    e_in, e_st = jnp.exp(b), jnp.exp(b_last - b)
    return dict(q=q, k=k, v=v, b=b, causal=causal, e_q=e_q, e_k=e_k, e_in=e_in, e_st=e_st, dec=jnp.exp(b_last),
                qe=q * e_q, ke=k * e_k, q_in=q * e_in, k_st=k * e_st, r_ref=r_ref, r_last=r_last)


def _gla_specs(order):
    rows = GLA_GROUP * CHUNK
    q = pl.BlockSpec((rows, GLA_K), lambda c: (order(c), 3 * ATTN_W // GLA_K))
    k = pl.BlockSpec((rows, GLA_K), lambda c: (order(c), 3 * ATTN_W // GLA_K + 1))
    v = pl.BlockSpec((rows, GLA_V), lambda c: (order(c), (3 * ATTN_W + 2 * GLA_K) // GLA_V))
    return q, k, v


def _gla_fwd(proj, gates, reverse, o_prev=None):
    S = proj.shape[0]
    n = S // CHUNK
    nb = n // GLA_GROUP
    rows = GLA_GROUP * CHUNK
    order = (lambda c: nb - 1 - c) if reverse else (lambda c: c)
    seq = list(range(GLA_GROUP))[::-1] if reverse else list(range(GLA_GROUP))
    gcol = 1 if reverse else 0

    def body(*refs):
        if o_prev is None:
            q_ref, k_ref, v_ref, g_ref, o_ref, st_ref, a_ref, b_ref, state = refs
        else:
            q_ref, k_ref, v_ref, g_ref, op_ref, o_ref, st_ref, a_ref, b_ref, state = refs
        c = pl.program_id(0)

        @pl.when(c == 0)
        def _():
            state[...] = jnp.zeros_like(state)

        for h in range(GLA_HEADS):
            vsl = slice(h * GLA_DV, (h + 1) * GLA_DV)
            st = state[h]
            for cc in seq:
                rs = _gla_rows(cc)
                t = _gla_chunk_terms(q_ref, k_ref, v_ref, g_ref, h, reverse, rs)
                b_ref[rs, h * GLA_DK:(h + 1) * GLA_DK] = t["b"]
                a = jnp.where(t["causal"], _dot(t["qe"].astype(BF16), t["ke"].astype(BF16), NT), 0.0).astype(BF16)
                a_ref[cc, h] = a
                o = _dot(a, t["v"])
                st_b = st.astype(BF16)
                st_ref[cc, h] = st_b
                o = o + _dot(t["q_in"].astype(BF16), st_b, NT)
                st = st * t["dec"] + _dot(t["v"], t["k_st"].astype(BF16), TN)
                if o_prev is not None:
                    o = o + op_ref[rs, vsl]
                o_ref[rs, vsl] = o
            state[h] = st

    q_spec, k_spec, v_spec = _gla_specs(order)
    o_spec = pl.BlockSpec((rows, GLA_V), lambda c: (order(c), 0))
    in_specs = [q_spec, k_spec, v_spec, pl.BlockSpec((rows, GLA_K), lambda c: (order(c), gcol))]
    operands = [proj, proj, proj, gates]
    if o_prev is not None:
        in_specs.append(o_spec)
        operands.append(o_prev)
    return pl.pallas_call(
        body, name="gla_fwd_rev" if reverse else "gla_fwd", grid=(nb,), in_specs=in_specs,
        out_specs=[o_spec, pl.BlockSpec((GLA_GROUP, GLA_HEADS, GLA_DV, GLA_DK), lambda c: (order(c), 0, 0, 0)),
                   pl.BlockSpec((GLA_GROUP, GLA_HEADS, CHUNK, CHUNK), lambda c: (order(c), 0, 0, 0)),
                   pl.BlockSpec((rows, GLA_K), lambda c: (order(c), 0))],
        out_shape=[jax.ShapeDtypeStruct((S, GLA_V), F32), jax.ShapeDtypeStruct((n, GLA_HEADS, GLA_DV, GLA_DK), BF16),
                   jax.ShapeDtypeStruct((n, GLA_HEADS, CHUNK, CHUNK), BF16), jax.ShapeDtypeStruct((S, GLA_K), F32)],
        scratch_shapes=[pltpu.VMEM((GLA_HEADS, GLA_DV, GLA_DK), F32)],
    )(*operands)


def _gla_bwd(proj, kept, do, reverse, prev=None):
    S = proj.shape[0]
    n = S // CHUNK
    nb = n // GLA_GROUP
    rows = GLA_GROUP * CHUNK
    order = (lambda c: c) if reverse else (lambda c: nb - 1 - c)
    seq = list(range(GLA_GROUP)) if reverse else list(range(GLA_GROUP))[::-1]
    out_dt = F32 if prev is None else BF16

    def body(*refs):
        if prev is None:
            q_ref, k_ref, v_ref, b_ref, st_ref, a_ref, do_ref, dq_ref, dk_ref, dv_ref, dg_ref, dstate = refs
        else:
            q_ref, k_ref, v_ref, b_ref, st_ref, a_ref, do_ref, pq, pk, pv, dq_ref, dk_ref, dv_ref, dg_ref, dstate = refs
        c = pl.program_id(0)

        @pl.when(c == 0)
        def _():
            dstate[...] = jnp.zeros_like(dstate)

        row = lax.broadcasted_iota(jnp.int32, (CHUNK, GLA_DK), 0)
        for h in range(GLA_HEADS):
            ksl = slice(h * GLA_DK, (h + 1) * GLA_DK)
            vsl = slice(h * GLA_DV, (h + 1) * GLA_DV)
            dst = dstate[h]
            for cc in seq:
                rs = _gla_rows(cc)
                t = _gla_chunk_terms(q_ref, k_ref, v_ref, None, h, reverse, rs, b_ref)
                v = t["v"]
                dob = do_ref[rs, vsl].astype(BF16)
                st_b = st_ref[cc, h]
                dst_b = dst.astype(BF16)
                qe_b, ke_b = t["qe"].astype(BF16), t["ke"].astype(BF16)
                q_in_b, k_st_b = t["q_in"].astype(BF16), t["k_st"].astype(BF16)
                da = jnp.where(t["causal"], _dot(dob, v, NT), 0.0).astype(BF16)
                dv = _dot(a_ref[cc, h], dob, TN) + _dot(k_st_b, dst_b, NT)
                dqe = _dot(da, ke_b)
                dke = _dot(da, qe_b, TN)
                dq_in = _dot(dob, st_b)
                dk_st = _dot(v, dst_b)
                ddec = jnp.sum(dst * st_b.astype(F32), axis=0, keepdims=True)
                dst = _dot(dob, q_in_b, TN) + dst * t["dec"]
                dq = (dqe * t["e_q"] + dq_in * t["e_in"]) * (GLA_DK ** -0.5)
                dk = dke * t["e_k"] + dk_st * t["e_st"]
                w_q, w_k = dqe * t["qe"], dke * t["ke"]
                w_st = dk_st * t["k_st"]
                db = w_q - w_k + dq_in * t["q_in"] - w_st
                db_ref = jnp.sum(w_k - w_q, axis=0, keepdims=True)
                db_last = jnp.sum(w_st, axis=0, keepdims=True) + ddec * t["dec"]
                db = db + jnp.where(row == t["r_ref"], db_ref, 0.0) + jnp.where(row == t["r_last"], db_last, 0.0)
                dg_ref[rs, ksl] = _cumsum_rows(db, not reverse)
                if prev is not None:
                    dq, dk, dv = dq + pq[rs, ksl], dk + pk[rs, ksl], dv + pv[rs, vsl]
                dq_ref[rs, ksl] = dq.astype(out_dt)
                dk_ref[rs, ksl] = dk.astype(out_dt)
                dv_ref[rs, vsl] = dv.astype(out_dt)
            dstate[h] = dst

    q_spec, k_spec, v_spec = _gla_specs(order)
    kk = pl.BlockSpec((rows, GLA_K), lambda c: (order(c), 0))
    vv = pl.BlockSpec((rows, GLA_V), lambda c: (order(c), 0))
    states, scores, sums = kept
    in_specs = [q_spec, k_spec, v_spec, kk,
                pl.BlockSpec((GLA_GROUP, GLA_HEADS, GLA_DV, GLA_DK), lambda c: (order(c), 0, 0, 0)),
                pl.BlockSpec((GLA_GROUP, GLA_HEADS, CHUNK, CHUNK), lambda c: (order(c), 0, 0, 0)), vv]
    operands = [proj, proj, proj, sums, states, scores, do]
    if prev is not None:
        in_specs += [kk, kk, vv]
        operands += list(prev)
    return pl.pallas_call(
        body, name="gla_bwd_rev" if reverse else "gla_bwd", grid=(nb,), in_specs=in_specs, out_specs=[kk, kk, vv, kk],
        out_shape=[jax.ShapeDtypeStruct((S, GLA_K), out_dt), jax.ShapeDtypeStruct((S, GLA_K), out_dt),
                   jax.ShapeDtypeStruct((S, GLA_V), out_dt), jax.ShapeDtypeStruct((S, GLA_K), F32)],
        scratch_shapes=[pltpu.VMEM((GLA_HEADS, GLA_DV, GLA_DK), F32)],
    )(*operands)


def _gates_fwd(z, wg, bias, tm=512):
    S = z.shape[0]
    W = 2 * GLA_K

    def body(z_ref, w_ref, b_ref, o_ref):
        zg = _dot(z_ref[...], w_ref[...]) + b_ref[...]
        o_ref[...] = (jnp.minimum(zg, 0.0) - jnp.log(1.0 + jnp.exp(-jnp.abs(zg)))) * (1.0 / GATE_NORM)

    return pl.pallas_call(
        body, name="gates_fwd", grid=(S // tm,),
        in_specs=[pl.BlockSpec((tm, Z_W), lambda i: (i, 0)), pl.BlockSpec((Z_W, W), lambda i: (0, 0)),
                  pl.BlockSpec((1, W), lambda i: (0, 0))],
        out_specs=pl.BlockSpec((tm, W), lambda i: (i, 0)), out_shape=jax.ShapeDtypeStruct((S, W), F32),
    )(z, wg, bias)


def _gates_bwd(z, wg, bias, dg_f, dg_b, tm=512):
    S = z.shape[0]
    W = 2 * GLA_K

    def body(z_ref, w_ref, b_ref, dgf_ref, dgb_ref, dz_ref, dw_ref, db_ref):
        i = pl.program_id(0)

        @pl.when(i == 0)
        def _():
            dw_ref[...] = jnp.zeros_like(dw_ref)
            db_ref[...] = jnp.zeros_like(db_ref)

        zv = z_ref[...]
        zg = _dot(zv, w_ref[...]) + b_ref[...]
        dg = jnp.concatenate([dgf_ref[...], dgb_ref[...]], axis=1)
        dzg = dg * (1.0 / GATE_NORM) * _sigmoid(-zg)
        db_ref[...] += jnp.sum(dzg, axis=0, keepdims=True)
        dzg_b = dzg.astype(BF16)
        dw_ref[...] += _dot(zv, dzg_b, TN)
        dz_ref[...] = _dot(dzg_b, w_ref[...], NT).astype(BF16)

    half = pl.BlockSpec((tm, GLA_K), lambda i: (i, 0))
    return pl.pallas_call(
        body, name="gates_bwd", grid=(S // tm,),
        in_specs=[pl.BlockSpec((tm, Z_W), lambda i: (i, 0)), pl.BlockSpec((Z_W, W), lambda i: (0, 0)),
                  pl.BlockSpec((1, W), lambda i: (0, 0)), half, half],
        out_specs=[pl.BlockSpec((tm, Z_W), lambda i: (i, 0)), pl.BlockSpec((Z_W, W), lambda i: (0, 0)),
                   pl.BlockSpec((1, W), lambda i: (0, 0))],
        out_shape=[jax.ShapeDtypeStruct((S, Z_W), BF16), jax.ShapeDtypeStruct((Z_W, W), F32),
                   jax.ShapeDtypeStruct((1, W), F32)],
    )(z, wg, bias, dg_f, dg_b)


def _gla_out_fwd(o, proj, g, cat, tm=512):
    S = o.shape[0]

    def body(o_ref, gr_ref, g_ref, cat_ref, out_ref):
        gn = g_ref[...]
        for h in range(GLA_HEADS):
            sl = slice(h * GLA_DV, (h + 1) * GLA_DV)
            ov = o_ref[:, sl]
            r = lax.rsqrt(jnp.mean(ov * ov, axis=-1, keepdims=True) + EPS)
            gr = gr_ref[:, sl].astype(F32)
            out_ref[:, sl] = (ov * r * gn * (gr * _sigmoid(gr))).astype(BF16)

    blk = pl.BlockSpec((tm, GLA_V), lambda i: (i, 0))
    return pl.pallas_call(
        body, name="gla_out_fwd", grid=(S // tm,),
        in_specs=[blk, pl.BlockSpec((tm, GLA_V), lambda i: (i, (3 * ATTN_W + 2 * GLA_K + GLA_V) // GLA_V)),
                  pl.BlockSpec((1, GLA_DV), lambda i: (0, 0)), ANY],
        out_specs=pl.BlockSpec((tm, GLA_V), lambda i: (i, 1)), out_shape=jax.ShapeDtypeStruct((S, D_MODEL), BF16),
        input_output_aliases={3: 0},
    )(o, proj, g, cat)


def _gla_out_bwd(o, proj, g, dcat, dproj, tm=512):
    S = o.shape[0]

    def body(o_ref, gr_ref, g_ref, dgo_ref, dproj_ref, do_ref, dgr_ref, dg_ref):
        i = pl.program_id(0)

        @pl.when(i == 0)
        def _():
            dg_ref[...] = jnp.zeros_like(dg_ref)

        gn = g_ref[...]
        dg_acc = jnp.zeros((1, GLA_DV), F32)
        for h in range(GLA_HEADS):
            sl = slice(h * GLA_DV, (h + 1) * GLA_DV)
            ov = o_ref[:, sl]
            r = lax.rsqrt(jnp.mean(ov * ov, axis=-1, keepdims=True) + EPS)
            yhat = ov * r
            gr = gr_ref[:, sl].astype(F32)
            sg = _sigmoid(gr)
            dgo = dgo_ref[:, sl].astype(F32)
            dgr_ref[:, sl] = (dgo * (yhat * gn) * (sg * (1.0 + gr * (1.0 - sg)))).astype(BF16)
            dy = dgo * (gr * sg)
            dg_acc = dg_acc + jnp.sum(dy * yhat, axis=0, keepdims=True)
            t = dy * gn
            do_ref[:, sl] = r * (t - yhat * jnp.mean(t * yhat, axis=-1, keepdims=True))
        dg_ref[...] += dg_acc

    blk = pl.BlockSpec((tm, GLA_V), lambda i: (i, 0))
    vec = pl.BlockSpec((1, GLA_DV), lambda i: (0, 0))
    return pl.pallas_call(
        body, name="gla_out_bwd", grid=(S // tm,),
        in_specs=[blk, pl.BlockSpec((tm, GLA_V), lambda i: (i, (3 * ATTN_W + 2 * GLA_K + GLA_V) // GLA_V)), vec,
                  pl.BlockSpec((tm, GLA_V), lambda i: (i, 1)), ANY],
        out_specs=[blk, pl.BlockSpec((tm, GLA_V), lambda i: (i, (3 * ATTN_W + 2 * GLA_K + GLA_V) // GLA_V)), vec],
        out_shape=[jax.ShapeDtypeStruct((S, GLA_V), F32), jax.ShapeDtypeStruct((S, IN_MAIN), BF16),
                   jax.ShapeDtypeStruct((1, GLA_DV), F32)],
        input_output_aliases={4: 1},
    )(o, proj, g, dcat, dproj)


HALO = 16


def _halo_specs(tm, tn, S):
    cur = pl.BlockSpec((tm, tn), lambda j, i: (i, j))
    prev = pl.BlockSpec((HALO, tn), lambda j, i: (jnp.maximum(i * (tm // HALO) - 1, 0), j))
    nxt = pl.BlockSpec((HALO, tn), lambda j, i: (jnp.minimum((i + 1) * (tm // HALO), S // HALO - 1), j))
    return cur, prev, nxt


def _shifted(c_ref, p_ref, n_ref, n_blocks, i=None):
    if i is None:
        i = pl.program_id(1)
    x = c_ref[...].astype(F32)
    tm = x.shape[0]
    row = lax.broadcasted_iota(jnp.int32, x.shape, 0)
    before = p_ref[HALO - 1:HALO, :].astype(F32) * (i > 0).astype(F32)
    after = n_ref[0:1, :].astype(F32) * (i < n_blocks - 1).astype(F32)
    x_m1 = jnp.where(row == 0, before, pltpu.roll(x, 1, axis=0))
    x_p1 = jnp.where(row == tm - 1, after, pltpu.roll(x, tm - 1, axis=0))
    return x, x_m1, x_p1


def _glu_fwd(gp, up, cw, cb, tm=512, tn=1408):
    S = gp.shape[0]
    nb = S // tm

    def body(c_ref, p_ref, n_ref, up_ref, w_ref, b_ref, o_ref):
        x, x_m1, x_p1 = _shifted(c_ref, p_ref, n_ref, nb)
        w = w_ref[...]
        gate = w[0:1, :] * x_m1 + w[1:2, :] * x + w[2:3, :] * x_p1 + b_ref[...]
        o_ref[...] = (gate * _sigmoid(gate) * up_ref[...].astype(F32)).astype(BF16)

    cur, prev, nxt = _halo_specs(tm, tn, S)
    return pl.pallas_call(
        body, name="glu_fwd", grid=(D_FF // tn, nb),
        in_specs=[cur, prev, nxt, cur, pl.BlockSpec((3, tn), lambda j, i: (0, j)), pl.BlockSpec((1, tn), lambda j, i: (0, j))],
        out_specs=cur, out_shape=jax.ShapeDtypeStruct((S, D_FF), BF16),
    )(gp, gp, gp, up, cw, cb)


def _glu_bwd(gp, up, dact, cw, cb, tm=512, tn=1408):
    S = gp.shape[0]
    nb = S // tm

    def body(c_ref, p_ref, n_ref, up_ref, upp_ref, upn_ref, da_ref, dap_ref, dan_ref, w_ref, b_ref,
             dup_ref, dgp_ref, dw_ref, db_ref):
        i = pl.program_id(1)

        @pl.when(i == 0)
        def _():
            dw_ref[...] = jnp.zeros_like(dw_ref)
            db_ref[...] = jnp.zeros_like(db_ref)

        x, x_m1, x_p1 = _shifted(c_ref, p_ref, n_ref, nb)
        w = w_ref[...]
        w0, w1, w2, b = w[0:1, :], w[1:2, :], w[2:3, :], b_ref[...]

        def d_gate(gate, da, upv):
            sg = _sigmoid(gate)
            return sg, da * upv * (sg * (1.0 + gate * (1.0 - sg)))

        gate = w0 * x_m1 + w1 * x + w2 * x_p1 + b
        da = da_ref[...].astype(F32)
        sg, dgate = d_gate(gate, da, up_ref[...].astype(F32))
        dup_ref[...] = (da * (gate * sg)).astype(BF16)
        db_ref[...] += jnp.sum(dgate, axis=0, keepdims=True)
        dw_ref[...] += jnp.concatenate(
            [jnp.sum(dgate * x_m1, axis=0, keepdims=True), jnp.sum(dgate * x, axis=0, keepdims=True),
             jnp.sum(dgate * x_p1, axis=0, keepdims=True)], axis=0)

        pv, nv = p_ref[...].astype(F32), n_ref[...].astype(F32)
        gate_before = w0 * pv[HALO - 2:HALO - 1, :] + w1 * pv[HALO - 1:HALO, :] + w2 * x[0:1, :] + b
        _, dgate_before = d_gate(gate_before, dap_ref[...].astype(F32)[HALO - 1:HALO, :], upp_ref[...].astype(F32)[HALO - 1:HALO, :])
        gate_after = w0 * x[tm - 1:tm, :] + w1 * nv[0:1, :] + w2 * nv[1:2, :] + b
        _, dgate_after = d_gate(gate_after, dan_ref[...].astype(F32)[0:1, :], upn_ref[...].astype(F32)[0:1, :])
        dgate_before = dgate_before * (i > 0).astype(F32)
        dgate_after = dgate_after * (i < nb - 1).astype(F32)
        row = lax.broadcasted_iota(jnp.int32, dgate.shape, 0)
        dg_m1 = jnp.where(row == 0, dgate_before, pltpu.roll(dgate, 1, axis=0))
        dg_p1 = jnp.where(row == tm - 1, dgate_after, pltpu.roll(dgate, tm - 1, axis=0))
        dgp_ref[...] = (w0 * dg_p1 + w1 * dgate + w2 * dg_m1).astype(BF16)

    cur, prev, nxt = _halo_specs(tm, tn, S)
    w_spec = pl.BlockSpec((3, tn), lambda j, i: (0, j))
    b_spec = pl.BlockSpec((1, tn), lambda j, i: (0, j))
    return pl.pallas_call(
        body, name="glu_bwd", grid=(D_FF // tn, nb),
        in_specs=[cur, prev, nxt, cur, prev, nxt, cur, prev, nxt, w_spec, b_spec],
        out_specs=[cur, cur, w_spec, b_spec],
        out_shape=[jax.ShapeDtypeStruct((S, D_FF), BF16), jax.ShapeDtypeStruct((S, D_FF), BF16),
                   jax.ShapeDtypeStruct((3, D_FF), F32), jax.ShapeDtypeStruct((1, D_FF), F32)],
    )(gp, gp, gp, up, up, up, dact, dact, dact, cw, cb)


def _local_step(x, target, norm1_g, w_in_t, wg, gate_bias, gla_norm_g, attn_norm_g, w_out, norm2_g,
                w_gate4, w_up4, conv_w, conv_b, w_down, final_norm_g, on_grad=lambda event, arrays: ()):
    S = x.shape[0]
    tabs = _rope_tables(S)

    n1 = _rms_fwd("rms1_fwd", x, norm1_g)
    z_block = IN_MAIN // Z_W
    proj = _mm_nt("in_proj", n1, w_in_t, 1024, 1536, BF16, n_out=IN_MAIN)
    z = _matmul(
        "in_proj_z",
        [(n1, pl.BlockSpec((1024, D_MODEL), lambda i: (i, 0)), w_in_t, pl.BlockSpec((Z_W, D_MODEL), lambda i: (z_block, 0)), NT)],
        (S // 1024,), jax.ShapeDtypeStruct((S, Z_W), BF16), pl.BlockSpec((1024, Z_W), lambda i: (i, 0)), 1)
    qk = _rope_fwd(proj, tabs)
    branch = [_attn_fwd(qk, proj, d) for d in DILATIONS]
    ao, o_attn, lse = _attn_combine([b[0] for b in branch], [b[1] for b in branch], attn_norm_g)
    gates = _gates_fwd(z, wg, gate_bias)
    o_f, *kept_f = _gla_fwd(proj, gates, False)
    o_gla, *kept_b = _gla_fwd(proj, gates, True, o_prev=o_f)
    cat = _gla_out_fwd(o_gla, proj, gla_norm_g, ao)
    h1 = _mm_nn("out_proj", cat, w_out, 1024, 1024, F32, res=x)
    n2 = _rms_fwd("rms2_fwd", h1, norm2_g)
    gp = _mm_nn_sharded("ffn_gate", n2, w_gate4, 1024, BF16)
    up = _mm_nn_sharded("ffn_up", n2, w_up4, 1024, BF16)
    act = _glu_fwd(gp, up, conv_w, conv_b)
    tk = D_FF // N_CHIPS
    h2 = _mm_nn("ffn_down", act, w_down, 1024, 512, F32, res=h1)
    loss_row, d_final_g, dh2, dh2_b = _final_loss(h2, final_norm_g.reshape(1, D_MODEL), target)

    dact = _mm_nt("ffn_down_bwd", dh2_b, w_down, 1024, tk, BF16)
    dup, dgp, d_conv_w, d_conv_b = _glu_bwd(gp, up, dact, conv_w, conv_b)
    d_w_down = _mm_tn("ffn_down_wgrad", act, dh2_b, 512, D_MODEL, 2048, BF16)
    on_grad("w_down", dict(w_down=d_w_down))
    dgp = _after(dgp, d_w_down)
    d_w_gate4 = _mm_tn("ffn_gate_wgrad", n2, dgp, 1024, tk, 2048, BF16, out3=tk)
    dup = _after(dup, d_w_gate4)
    d_w_up4 = _mm_tn("ffn_up_wgrad", n2, dup, 1024, tk, 2048, BF16, out3=tk)
    held = on_grad("w_gate_w_up", dict(w_gate=d_w_gate4, w_up=d_w_up4))
    dgp = _after(dgp, d_w_up4, *held)
    shard_pairs = [
        (g, pl.BlockSpec((512, tk), functools.partial(lambda s, j, i: (i, s), s)),
         w4, pl.BlockSpec((None, 512, tk), functools.partial(lambda s, j, i: (s, j, 0), s)), NT)
        for g, w4 in ((dgp, w_gate4), (dup, w_up4)) for s in range(N_CHIPS)]
    dn2 = _matmul("ffn_in_bwd", shard_pairs, (D_MODEL // 512, S // 512), jax.ShapeDtypeStruct((S, D_MODEL), BF16),
                  pl.BlockSpec((512, 512), lambda j, i: (i, j)), 1)
    dh1, dh1_b, d_norm2_g = _rms_bwd("rms2_bwd", h1, norm2_g, dn2, dh2)

    d_w_out = _mm_tn("out_proj_wgrad", cat, dh1_b, D_MODEL, 1024, 1024, BF16)
    held = on_grad("w_out", dict(w_out=d_w_out))
    dcat = _mm_nt("out_proj_bwd", _after(dh1_b, d_w_out, *held), w_out, 1024, 1024, BF16)
    do_attn, delta, d_attn_norm_g = _attn_norm_bwd(o_attn, attn_norm_g, dcat)
    dqs, dks, dvs = [], [], []
    for d in DILATIONS:
        dqs.append(_attn_bwd_dq(qk, proj, do_attn, lse, delta, d))
        dk, dv = _attn_bwd_dkv(qk, proj, do_attn, lse, delta, d)
        dks.append(dk)
        dvs.append(dv)
    dproj = _attn_grad_merge(dqs, dks, dvs, tabs)
    held = on_grad("mid", dict(anchor=dproj))
    do_gla, dproj, d_gla_norm_g = _gla_out_bwd(o_gla, proj, gla_norm_g, _after(dcat, *held), dproj)
    dq_f, dk_f, dv_f, dg_f = _gla_bwd(proj, kept_f, do_gla, False)
    dgq, dgk, dgv, dg_b = _gla_bwd(proj, kept_b, do_gla, True, prev=(dq_f, dk_f, dv_f))
    dz, d_wg, d_gate_bias = _gates_bwd(z, wg, gate_bias, dg_f, dg_b)
    dproj = lax.dynamic_update_slice(dproj, jnp.concatenate([dgq, dgk, dgv], axis=1), (0, 3 * ATTN_W))
    d_w_in_t = _mm_tn("in_proj_wgrad", dproj, n1, 768, D_MODEL, 2048, BF16, rows_out=IN_W)
    n_tok = S // 1024
    d_w_in_t = _matmul(
        "in_proj_z_wgrad",
        [(dz, pl.BlockSpec((1024, Z_W), lambda i, j, k: (k, 0)), n1, pl.BlockSpec((1024, D_MODEL), lambda i, j, k: (k, 0)), TN)],
        (1, 1, n_tok), jax.ShapeDtypeStruct((IN_W, D_MODEL), BF16), pl.BlockSpec((Z_W, D_MODEL), lambda i, j, k: (z_block, 0)),
        n_tok, into=d_w_in_t)
    held = on_grad("w_in", dict(w_in_t=d_w_in_t))
    half = S // 2048

    def in_proj_bwd(name, first, a, into):
        return _matmul(
            name,
            [(a, pl.BlockSpec((1024, IN_MAIN), lambda j, i: (i + first, 0)), w_in_t, pl.BlockSpec((IN_MAIN, 512), lambda j, i: (0, j)), NN),
             (dz, pl.BlockSpec((1024, Z_W), lambda j, i: (i + first, 0)), w_in_t, pl.BlockSpec((Z_W, 512), lambda j, i: (z_block, j)), NN)],
            (D_MODEL // 512, half), jax.ShapeDtypeStruct((S, D_MODEL), BF16),
            pl.BlockSpec((1024, 512), lambda j, i: (i + first, j)), 1, into=into)

    dproj = _after(dproj, d_w_in_t, *held)
    dn1 = in_proj_bwd("in_proj_bwd_a", 0, dproj, None)
    held = on_grad("last", dict(last=dn1))
    dn1 = in_proj_bwd("in_proj_bwd_b", half, dproj, _after(dn1, *held))
    grad_x, _, d_norm1_g = _rms_bwd("rms1_bwd", x, norm1_g, dn1, dh1)

    big = dict(w_in_t=d_w_in_t, w_out=d_w_out, w_gate4=d_w_gate4, w_up4=d_w_up4, w_down=d_w_down)
    small = dict(loss=loss_row, norm1_g=d_norm1_g, wg=d_wg, gate_bias=d_gate_bias, gla_norm_g=d_gla_norm_g,
                 attn_norm_g=d_attn_norm_g, norm2_g=d_norm2_g, conv_w=d_conv_w, conv_b=d_conv_b, final_norm_g=d_final_g)
    return grad_x, big, small


def _position():
    return lax.axis_index("x"), lax.axis_index("y"), lax.axis_index("c")


def _other_chips(x, y):
    return [(1 - x, y), (x, 1 - y), (1 - x, 1 - y)]


def _gather_chips_async(name, shards, collective_id):
    n = len(shards)

    def body(*refs):
        ins, outs = refs[:n], refs[n:2 * n]
        send, recv, loc = refs[2 * n:]
        x, y, c = _position()
        me = 2 * x + y
        chips = _other_chips(x, y)
        barrier = pltpu.get_barrier_semaphore()
        for px, py in chips:
            pl.semaphore_signal(barrier, inc=1, device_id=(px, py, c), device_id_type=MESH)
        pl.semaphore_wait(barrier, len(chips))
        started = []
        for w in range(n):
            own = pltpu.make_async_copy(ins[w], outs[w].at[me], loc.at[w])
            own.start()
            started.append(own)
        sends = []
        for w in range(n):
            for j, (px, py) in enumerate(chips):
                cp = pltpu.make_async_remote_copy(ins[w], outs[w].at[me], send.at[3 * w + j], recv.at[3 * w + j],
                                                  device_id=(px, py, c), device_id_type=MESH)
                cp.start()
                sends.append(cp)
        for w in range(n):
            for j, (px, py) in enumerate(chips):
                pltpu.make_async_remote_copy(ins[w], outs[w].at[2 * px + py], send.at[3 * w + j], recv.at[3 * w + j],
                                             device_id=(px, py, c), device_id_type=MESH).wait_recv()
        for cp in sends:
            cp.wait_send()
        for own in started:
            own.wait()

    return pl.kernel(
        body, name=name, mesh=_sequencer(),
        out_type=[jax.ShapeDtypeStruct((N_CHIPS,) + s.shape, s.dtype) for s in shards],
        scratch_types=[pltpu.SemaphoreType.DMA((3 * n,)), pltpu.SemaphoreType.DMA((3 * n,)), pltpu.SemaphoreType.DMA((n,))],
        compiler_params=pltpu.CompilerParams(collective_id=collective_id),
    )(*shards)


def _gather_halves_async(name, small, shard, collective_id):
    half = shard.shape[1] // 2

    def body(small_ref, shard_ref, small_out, out, send, recv, loc):
        x, y, c = _position()
        me = 2 * x + y
        sibling = (x, y, 1 - c)
        chips = _other_chips(x, y)
        barrier = pltpu.get_barrier_semaphore()
        for px, py in chips:
            pl.semaphore_signal(barrier, inc=1, device_id=(px, py, c), device_id_type=MESH)
        pl.semaphore_signal(barrier, inc=1, device_id=sibling, device_id_type=MESH)
        pl.semaphore_wait(barrier, len(chips) + 1)
        mine = pl.ds(pl.multiple_of(c * half, LANES), half)
        theirs = pl.ds(pl.multiple_of((1 - c) * half, LANES), half)
        own = [pltpu.make_async_copy(small_ref, small_out.at[me], loc.at[0]),
               pltpu.make_async_copy(shard_ref, out.at[me], loc.at[1])]
        for cp in own:
            cp.start()
        sends = []
        for j, (px, py) in enumerate(chips):
            sends.append(pltpu.make_async_remote_copy(small_ref, small_out.at[me], send.at[j], recv.at[j],
                                                      device_id=(px, py, c), device_id_type=MESH))
            sends.append(pltpu.make_async_remote_copy(shard_ref.at[:, mine], out.at[me, :, mine], send.at[3 + j], recv.at[3 + j],
                                                      device_id=(px, py, c), device_id_type=MESH))
        for cp in sends:
            cp.start()
        passed = []
        for j, (px, py) in enumerate(chips):
            slot = 2 * px + py
            pltpu.make_async_remote_copy(shard_ref.at[:, mine], out.at[slot, :, mine], send.at[3 + j], recv.at[3 + j],
                                         device_id=(px, py, c), device_id_type=MESH).wait_recv()
            cp = pltpu.make_async_remote_copy(out.at[slot, :, mine], out.at[slot, :, mine], send.at[6 + j], recv.at[6 + j],
                                              device_id=sibling, device_id_type=MESH)
            cp.start()
            passed.append(cp)
        for j, (px, py) in enumerate(chips):
            slot = 2 * px + py
            pltpu.make_async_remote_copy(small_ref, small_out.at[slot], send.at[j], recv.at[j],
                                         device_id=(px, py, c), device_id_type=MESH).wait_recv()
            pltpu.make_async_remote_copy(out.at[slot, :, theirs], out.at[slot, :, theirs], send.at[6 + j], recv.at[6 + j],
                                         device_id=sibling, device_id_type=MESH).wait_recv()
        for cp in sends + passed:
            cp.wait_send()
        for cp in own:
            cp.wait()

    return pl.kernel(
        body, name=name, mesh=_sequencer(),
        out_type=[jax.ShapeDtypeStruct((N_CHIPS,) + small.shape, small.dtype),
                  jax.ShapeDtypeStruct((N_CHIPS,) + shard.shape, shard.dtype)],
        scratch_types=[pltpu.SemaphoreType.DMA((9,)), pltpu.SemaphoreType.DMA((9,)), pltpu.SemaphoreType.DMA((2,))],
        compiler_params=pltpu.CompilerParams(collective_id=collective_id),
    )(small, shard)


def _sequencer():
    return plsc.ScalarSubcoreMesh(axis_name="sequencer", num_cores=1)


def _sibling_exchange_async(name, arrs, collective_id):
    n = len(arrs)

    def body(*refs):
        ins, outs = refs[:n], refs[n:2 * n]
        send, recv = refs[2 * n:]
        x, y, c = _position()
        sibling = (x, y, 1 - c)
        barrier = pltpu.get_barrier_semaphore()
        pl.semaphore_signal(barrier, inc=1, device_id=sibling, device_id_type=MESH)
        pl.semaphore_wait(barrier, 1)
        copies = [pltpu.make_async_remote_copy(ins[w], outs[w], send.at[w], recv.at[w], device_id=sibling,
                                               device_id_type=MESH) for w in range(n)]
        for cp in copies:
            cp.start()
        for cp in copies:
            cp.wait()

    return pl.kernel(
        body, name=name, out_type=[jax.ShapeDtypeStruct(a.shape, a.dtype) for a in arrs],
        scratch_types=[pltpu.SemaphoreType.DMA((n,)), pltpu.SemaphoreType.DMA((n,))],
        compiler_params=pltpu.CompilerParams(collective_id=collective_id), mesh=_sequencer(),
    )(*arrs)


def _scatter_chips_async(name, parts, collective_id):
    n = len(parts)

    def body(*refs):
        ins, outs = refs[:n], refs[n:2 * n]
        send, recv, loc = refs[2 * n:]
        x, y, c = _position()
        me = 2 * x + y
        chips = _other_chips(x, y)
        barrier = pltpu.get_barrier_semaphore()
        for px, py in chips:
            pl.semaphore_signal(barrier, inc=1, device_id=(px, py, c), device_id_type=MESH)
        pl.semaphore_wait(barrier, len(chips))
        started = []
        for w in range(n):
            own = pltpu.make_async_copy(ins[w].at[me], outs[w].at[me], loc.at[w])
            own.start()
            started.append(own)
        sends = []
        for w in range(n):
            for j, (px, py) in enumerate(chips):
                cp = pltpu.make_async_remote_copy(ins[w].at[2 * px + py], outs[w].at[me], send.at[3 * w + j],
                                                  recv.at[3 * w + j], device_id=(px, py, c), device_id_type=MESH)
                cp.start()
                sends.append(cp)
        for w in range(n):
            for j, (px, py) in enumerate(chips):
                pltpu.make_async_remote_copy(ins[w].at[me], outs[w].at[2 * px + py], send.at[3 * w + j], recv.at[3 * w + j],
                                             device_id=(px, py, c), device_id_type=MESH).wait_recv()
        for cp in sends:
            cp.wait_send()
        for own in started:
            own.wait()

    return pl.kernel(
        body, name=name, out_type=[jax.ShapeDtypeStruct(p.shape, p.dtype) for p in parts],
        scratch_types=[pltpu.SemaphoreType.DMA((3 * n,)), pltpu.SemaphoreType.DMA((3 * n,)), pltpu.SemaphoreType.DMA((n,))],
        compiler_params=pltpu.CompilerParams(collective_id=collective_id), mesh=_sequencer(),
    )(*parts)


def _allreduce_rows(buf):
    R = buf.shape[0]

    def body(in_ref, out_ref, land, send, recv):
        x, y, c = _position()
        me = 4 * x + 2 * y + c
        land[pl.ds(me, 1)] = in_ref[...][None]
        peers = []
        for mask in range(1, N_DEV):
            px = 1 - x if mask & 4 else x
            py = 1 - y if mask & 2 else y
            pc = 1 - c if mask & 1 else c
            peers.append((px, py, pc))
        sends = []
        for k, peer in enumerate(peers):
            cp = pltpu.make_async_remote_copy(in_ref, land.at[me], send.at[k], recv.at[k], device_id=peer, device_id_type=MESH)
            cp.start()
            sends.append(cp)
        for k, (px, py, pc) in enumerate(peers):
            pltpu.make_async_remote_copy(in_ref, land.at[4 * px + 2 * py + pc], send.at[k], recv.at[k],
                                         device_id=(px, py, pc), device_id_type=MESH).wait_recv()
        for cp in sends:
            cp.wait_send()
        tot = land[0]
        for i in range(1, N_DEV):
            tot = tot + land[i]
        out_ref[...] = tot

    vm = pl.BlockSpec(memory_space=pltpu.VMEM)
    return pl.pallas_call(
        body, name="allreduce_small", in_specs=[vm], out_specs=vm, out_shape=jax.ShapeDtypeStruct((R, LANES), F32),
        scratch_shapes=[pltpu.VMEM((N_DEV, R, LANES), F32), pltpu.SemaphoreType.DMA((N_DEV - 1,)),
                        pltpu.SemaphoreType.DMA((N_DEV - 1,))],
    )(buf)


def _tile2d(r, c, cap):
    if r <= cap:
        return r, c
    fits = [t for t in range(16, cap + 1, 16) if r % t == 0]
    return (max(fits), c) if fits else (r, 256)


def _pair_sum(name, a, b):
    n, r, c = a.shape
    tr, tc = _tile2d(r, c, 1024)

    def body(a_ref, b_ref, o_ref):
        o_ref[...] = (a_ref[...].astype(F32) + b_ref[...].astype(F32)).astype(BF16)

    blk = pl.BlockSpec((None, tr, tc), lambda s, i, j: (s, i, j))
    return pl.pallas_call(
        body, name=name, grid=(n, r // tr, c // tc), in_specs=[blk, blk], out_specs=blk,
        out_shape=jax.ShapeDtypeStruct(a.shape, BF16),
    )(a, b)


def _adamw_math(w, m, v, g):
    m2 = ADAM_B1 * m + (1.0 - ADAM_B1) * g
    v2 = ADAM_B2 * v + (1.0 - ADAM_B2) * (g * g)
    m_hat = m2 / (1.0 - ADAM_B1 ** ADAM_STEP)
    v_hat = v2 / (1.0 - ADAM_B2 ** ADAM_STEP)
    delta = -ADAM_LR * (m_hat / (jnp.sqrt(v_hat) + ADAM_EPS) + ADAM_WD * w)
    return delta, m2, v2


def _adamw(name, w, m, v, g):
    r, c = w.shape
    stacked = g.ndim == 3
    tr, tc = _tile2d(r, c, 256)

    def body(w_ref, m_ref, v_ref, g_ref, go_ref, d_ref, m2_ref, v2_ref):
        if stacked:
            gv = g_ref[0].astype(F32)
            for i in range(1, N_CHIPS):
                gv = gv + g_ref[i].astype(F32)
        else:
            gv = g_ref[...]
        delta, m2, v2 = _adamw_math(w_ref[...], m_ref[...], v_ref[...], gv)
        go_ref[...] = gv
        d_ref[...] = delta
        m2_ref[...] = m2
        v2_ref[...] = v2

    blk = pl.BlockSpec((tr, tc), lambda i, j: (i, j))
    g_spec = pl.BlockSpec((N_CHIPS, tr, tc), lambda i, j: (0, i, j)) if stacked else blk
    out = jax.ShapeDtypeStruct((r, c), F32)
    return pl.pallas_call(
        body, name=name, grid=(r // tr, c // tc), in_specs=[blk, blk, blk, g_spec], out_specs=[blk] * 4, out_shape=[out] * 4,
    )(w, m, v, g)


def _pack_rows(pieces):
    flat = jnp.concatenate([p.reshape(-1) for p in pieces])
    rows = flat.shape[0] // LANES
    pad = (-rows) % 8
    return jnp.pad(flat.reshape(rows, LANES), ((0, pad), (0, 0)))


def _unpack_rows(buf, shapes):
    flat = buf.reshape(-1)
    out, at = [], 0
    for s in shapes:
        size = math.prod(s)
        out.append(flat[at:at + size].reshape(s))
        at += size
    return out


SMALL_NAMES = ("norm1_g", "gf_up", "gf_b", "gb_up", "gb_b", "gla_norm_g", "attn_norm_g", "norm2_g", "conv_w", "conv_b",
               "final_norm_g")
BIG_NAMES = ("w_in", "w_out", "w_gate", "w_up", "w_down")
WEIGHT_ORDER = ("norm1_g", "w_in", "gf_up", "gf_b", "gb_up", "gb_b", "gla_norm_g", "attn_norm_g", "w_out", "norm2_g",
                "w_gate", "w_up", "conv_w", "conv_b", "w_down", "final_norm_g")


def kernel(x, norm1_g, w_in, gf_up, gf_b, gb_up, gb_b, gla_norm_g, attn_norm_g, w_out, norm2_g, w_gate, w_up, conv_w, conv_b, w_down, final_norm_g, loss_target, m_norm1_g, m_w_in, m_gf_up, m_gf_b, m_gb_up, m_gb_b, m_gla_norm_g, m_attn_norm_g, m_w_out, m_norm2_g, m_w_gate, m_w_up, m_conv_w, m_conv_b, m_w_down, m_final_norm_g, v_norm1_g, v_w_in, v_gf_up, v_gf_b, v_gb_up, v_gb_b, v_gla_norm_g, v_attn_norm_g, v_w_out, v_norm2_g, v_w_gate, v_w_up, v_conv_w, v_conv_b, v_w_down, v_final_norm_g):
    w = dict(norm1_g=norm1_g, w_in=w_in, gf_up=gf_up, gf_b=gf_b, gb_up=gb_up, gb_b=gb_b, gla_norm_g=gla_norm_g,
             attn_norm_g=attn_norm_g, w_out=w_out, norm2_g=norm2_g, w_gate=w_gate, w_up=w_up, conv_w=conv_w, conv_b=conv_b,
             w_down=w_down, final_norm_g=final_norm_g)
    m = dict(norm1_g=m_norm1_g, w_in=m_w_in, gf_up=m_gf_up, gf_b=m_gf_b, gb_up=m_gb_up, gb_b=m_gb_b, gla_norm_g=m_gla_norm_g,
             attn_norm_g=m_attn_norm_g, w_out=m_w_out, norm2_g=m_norm2_g, w_gate=m_w_gate, w_up=m_w_up, conv_w=m_conv_w,
             conv_b=m_conv_b, w_down=m_w_down, final_norm_g=m_final_norm_g)
    v = dict(norm1_g=v_norm1_g, w_in=v_w_in, gf_up=v_gf_up, gf_b=v_gf_b, gb_up=v_gb_up, gb_b=v_gb_b, gla_norm_g=v_gla_norm_g,
             attn_norm_g=v_attn_norm_g, w_out=v_w_out, norm2_g=v_norm2_g, w_gate=v_w_gate, w_up=v_w_up, conv_w=v_conv_w,
             conv_b=v_conv_b, w_down=v_w_down, final_norm_g=v_final_norm_g)
    S = x.shape[1]
    chip = 2 * lax.axis_index("x") + lax.axis_index("y")
    n_in = IN_W // N_CHIPS
    n_ff = D_FF // N_CHIPS
    n_gk = GLA_K // N_CHIPS

    def owned(t):
        return {k: (jnp.transpose(t[k][0]) if k == "w_in" else t[k][0]) for k in BIG_NAMES}

    own_w, own_m, own_v = owned(w), owned(m), owned(v)
    shard = {k: own_w[k].astype(BF16) for k in BIG_NAMES}
    small_shard = _pack_rows([gf_up[0], gb_up[0], conv_w[0]])
    small4, w_in4 = _gather_halves_async("gather_w_in", small_shard, shard["w_in"], 0)
    w_out4, w_gate4, w_up4 = _gather_chips_async("gather_w_mid", [shard["w_out"], shard["w_gate"], shard["w_up"]], 1)
    (w_down4,) = _gather_chips_async("gather_w_down", [shard["w_down"]], 2)
    w_in_t = w_in4.reshape(IN_W, D_MODEL)
    rows_up = GATE_RANK * n_gk // LANES
    rows_cw = 3 * n_ff // LANES
    gf_full = jnp.transpose(small4[:, 0:rows_up].reshape(N_CHIPS, GATE_RANK, n_gk), (1, 0, 2)).reshape(GATE_RANK, GLA_K)
    gb_full = jnp.transpose(small4[:, rows_up:2 * rows_up].reshape(N_CHIPS, GATE_RANK, n_gk), (1, 0, 2)).reshape(GATE_RANK, GLA_K)
    cw_full = jnp.transpose(small4[:, 2 * rows_up:2 * rows_up + rows_cw].reshape(N_CHIPS, 3, n_ff), (1, 0, 2)).reshape(3, D_FF)
    wg = jnp.zeros((Z_W, 2 * GLA_K), F32)
    wg = wg.at[0:GATE_RANK, 0:GLA_K].set(gf_full).at[GATE_RANK:2 * GATE_RANK, GLA_K:].set(gb_full).astype(BF16)
    gate_bias = jnp.concatenate([gf_b, gb_b], axis=1)

    pending, contributions, next_id = [], {}, [3]

    def as_shards(group, arrays):
        if group == "w_in":
            return dict(w_in=arrays["w_in_t"].reshape(N_CHIPS, n_in, D_MODEL))
        if group == "w_out":
            return dict(w_out=arrays["w_out"].reshape(N_CHIPS, D_MODEL // N_CHIPS, D_MODEL))
        if group == "w_down":
            return dict(w_down=arrays["w_down"].reshape(N_CHIPS, n_ff, D_MODEL))
        return arrays

    out = {}

    def swap(group, arrays):
        mine = as_shards(group, arrays)
        pending.append((group, mine, _sibling_exchange_async(f"sibling_{group}", list(mine.values()), next_id[0])))
        next_id[0] += 1

    def sum_and_send(anchor):
        tag, mine, theirs = pending.pop()
        sums = [_pair_sum(f"pair_sum_{k}", mine[k], _after(t, *anchor)) for k, t in zip(mine, theirs)]
        contributions.update(zip(mine, _scatter_chips_async(f"scatter_{tag}", sums, next_id[0])))
        next_id[0] += 1
        return sums

    def update(names, anchor):
        for k in names:
            res = _adamw(f"adamw_{k}", own_w[k], own_m[k], own_v[k], _after(contributions[k], *anchor))
            out[k] = [(jnp.transpose(r) if k == "w_in" else r)[None] for r in res]
        return [out[k][0] for k in names]

    def on_grad(event, arrays):
        anchor = list(arrays.values())
        held = []
        if event in ("w_gate_w_up", "w_out", "mid", "last"):
            held += sum_and_send(anchor)
        if event == "mid":
            held += update(("w_down", "w_gate", "w_up"), anchor)
        if event == "last":
            held += update(("w_out",), anchor)
        if event in ("w_down", "w_gate_w_up", "w_out", "w_in"):
            swap(event, arrays)
        return held

    grad_x, _, small = _local_step(
        x[0], loss_target[0], norm1_g, w_in_t, wg, gate_bias, gla_norm_g, attn_norm_g,
        w_out4.reshape(D_MODEL, D_MODEL), norm2_g, w_gate4, w_up4, cw_full, conv_b, w_down4.reshape(D_FF, D_MODEL), final_norm_g,
        on_grad=on_grad)
    update(("w_in",), [grad_x])

    d_gf_up = small["wg"][0:GATE_RANK, 0:GLA_K]
    d_gb_up = small["wg"][GATE_RANK:2 * GATE_RANK, GLA_K:]
    pieces = [small["loss"], small["norm1_g"], d_gf_up, small["gate_bias"][:, :GLA_K], d_gb_up, small["gate_bias"][:, GLA_K:],
              small["gla_norm_g"], small["attn_norm_g"], small["norm2_g"], small["conv_w"], small["conv_b"], small["final_norm_g"]]
    total = _allreduce_rows(_pack_rows(pieces))
    summed = _unpack_rows(total, [p.shape for p in pieces])
    loss = summed[0][0, 0]
    g_small = dict(zip(SMALL_NAMES, summed[1:]))
    g_small["gf_up"] = lax.dynamic_slice_in_dim(g_small["gf_up"], chip * n_gk, n_gk, axis=1)
    g_small["gb_up"] = lax.dynamic_slice_in_dim(g_small["gb_up"], chip * n_gk, n_gk, axis=1)
    g_small["conv_w"] = lax.dynamic_slice_in_dim(g_small["conv_w"], chip * n_ff, n_ff, axis=1)
    packed = [_pack_rows([t[k] for k in SMALL_NAMES]) for t in (w, m, v, g_small)]
    res = _adamw("adamw_small", *packed)
    shapes = [w[k].shape for k in SMALL_NAMES]
    for k, vals in zip(SMALL_NAMES, zip(*[_unpack_rows(r, shapes) for r in res])):
        out[k] = list(vals)

    grads, deltas, new_m, new_v = ([out[k][i] for k in WEIGHT_ORDER] for i in range(4))
    return (loss, grad_x[None], *grads, *deltas, *new_m, *new_v)
```

```python
import functools
import math

import jax
import jax.numpy as jnp
from jax import lax
from jax.experimental import pallas as pl
from jax.experimental.pallas import tpu as pltpu
from jax.experimental.pallas import tpu_sc as plsc

F32 = jnp.float32
BF16 = jnp.bfloat16

D_MODEL = 2048
ATTN_W = 1024
HEAD = 128
N_HEADS = 8
N_SIDE = 64
DILATIONS = (1, 4, 16)
ROPE_THETA = 500000.0
ROPE_DIM = 32
GLA_K = 512
GLA_V = 1024
GLA_HEADS = 4
GLA_DK = 128
GLA_DV = 256
GATE_RANK = 16
GATE_NORM = 16.0
CHUNK = 64
IN_MAIN = 6144
IN_W = 6176
Z_W = IN_W - IN_MAIN
D_FF = 5632
EPS = 1e-6
N_CHIPS = 4
N_DEV = 8
LANES = 128

ADAM_LR = 0.001
ADAM_B1 = 0.9
ADAM_B2 = 0.999
ADAM_EPS = 1e-08
ADAM_WD = 0.01
ADAM_STEP = 10

NEG = -1e30
MESH = pl.DeviceIdType.MESH
ANY = pl.BlockSpec(memory_space=pl.ANY)

NN = ((1,), (0,))
NT = ((1,), (1,))
TN = ((0,), (0,))


def _dot(a, b, dims=NN):
    return lax.dot_general(a, b, (dims, ((), ())), preferred_element_type=F32)


def _sigmoid(x):
    return 0.5 * jnp.tanh(0.5 * x) + 0.5


def _after(x, *deps):
    return lax.optimization_barrier((x,) + deps)[0]


def _matmul(name, pairs, grid, out_shape, out_spec, nk, res=None, into=None, first=None):
    n_in = 2 * len(pairs) + (res is not None)
    dims = [p[4] for p in pairs]

    n_ops = n_in + (into is not None) + 2 * (first is not None)

    def body(*refs):
        ins, o_ref = refs[:n_in], refs[n_ops]

        def partial_sum():
            tot = None
            for p, dn in enumerate(dims):
                a, b = ins[2 * p][...], ins[2 * p + 1][...]
                t = _dot(a.astype(BF16), b.astype(BF16), dn)
                tot = t if tot is None else tot + t
            return tot

        if nk == 1:
            t = partial_sum()
            if res is not None:
                t = t + ins[-1][...]
            o_ref[...] = t.astype(o_ref.dtype)
        else:
            acc_ref = refs[n_ops + 1]
            k = pl.program_id(2)

            @pl.when(k == 0)
            def _():
                if first is not None:
                    start = _dot(refs[n_in][...].astype(BF16), refs[n_in + 1][...].astype(BF16), first[4])
                    acc_ref[...] = start + ins[-1][...] if res is not None else start
                elif res is not None:
                    acc_ref[...] = ins[-1][...]
                else:
                    acc_ref[...] = jnp.zeros_like(acc_ref)

            acc_ref[...] += partial_sum()

            @pl.when(k == nk - 1)
            def _():
                o_ref[...] = acc_ref[...].astype(o_ref.dtype)

    operands, in_specs = [], []
    for a, a_spec, b, b_spec, _ in pairs:
        operands += [a, b]
        in_specs += [a_spec, b_spec]
    if res is not None:
        operands.append(res[0])
        in_specs.append(res[1])
    if first is not None:
        assert nk > 1
        operands += [first[0], first[2]]
        in_specs += [first[1], first[3]]
    acc_shape = tuple(s for s in out_spec.block_shape if s is not None)
    scratch = [pltpu.VMEM(acc_shape, F32)] if nk > 1 else []
    aliases = {}
    if into is not None:
        aliases = {len(operands): 0}
        operands.append(into)
        in_specs.append(ANY)
    return pl.pallas_call(
        body, name=name, grid=grid, in_specs=in_specs, out_specs=out_spec, out_shape=out_shape, scratch_shapes=scratch,
        input_output_aliases=aliases,
    )(*operands)


def _mm_nn(name, a, b, tm, tn, out_dtype, res=None):
    M, K = a.shape
    N = b.shape[1]
    pairs = [(a, pl.BlockSpec((tm, K), lambda j, i: (i, 0)), b, pl.BlockSpec((K, tn), lambda j, i: (0, j)), NN)]
    r = None if res is None else (res, pl.BlockSpec((tm, tn), lambda j, i: (i, j)))
    return _matmul(name, pairs, (N // tn, M // tm), jax.ShapeDtypeStruct((M, N), out_dtype),
                   pl.BlockSpec((tm, tn), lambda j, i: (i, j)), 1, r)


def _mm_nn_sharded(name, a, b4, tm, out_dtype):
    M, K = a.shape
    n = b4.shape[2]
    pairs = [(a, pl.BlockSpec((tm, K), lambda j, i: (i, 0)), b4, pl.BlockSpec((None, K, n), lambda j, i: (j, 0, 0)), NN)]
    return _matmul(name, pairs, (N_CHIPS, M // tm), jax.ShapeDtypeStruct((M, N_CHIPS * n), out_dtype),
                   pl.BlockSpec((tm, n), lambda j, i: (i, j)), 1)


def _mm_nt(name, a, b, tm, tn, out_dtype, res=None, n_out=None):
    M, K = a.shape
    N = b.shape[0] if n_out is None else n_out
    pairs = [(a, pl.BlockSpec((tm, K), lambda j, i: (i, 0)), b, pl.BlockSpec((tn, K), lambda j, i: (j, 0)), NT)]
    r = None if res is None else (res, pl.BlockSpec((tm, tn), lambda j, i: (i, j)))
    return _matmul(name, pairs, (N // tn, M // tm), jax.ShapeDtypeStruct((M, N), out_dtype),
                   pl.BlockSpec((tm, tn), lambda j, i: (i, j)), 1, r)


def _mm_tn(name, a, g, tka, tn, tmm, out_dtype, out3=None, rows_out=None):
    M, Ka = a.shape
    N = g.shape[1]
    pairs = [(a, pl.BlockSpec((tmm, tka), lambda i, j, k: (k, i)), g, pl.BlockSpec((tmm, tn), lambda i, j, k: (k, j)), TN)]
    if out3 is None:
        shape, spec = (Ka if rows_out is None else rows_out, N), pl.BlockSpec((tka, tn), lambda i, j, k: (i, j))
    else:
        shape, spec = (N // out3, Ka, out3), pl.BlockSpec((None, tka, tn), lambda i, j, k: (j, i, 0))
    return _matmul(name, pairs, (Ka // tka, N // tn, M // tmm), jax.ShapeDtypeStruct(shape, out_dtype), spec, M // tmm)


def _rms_fwd(name, x, g, tm=512):
    S, D = x.shape

    def body(x_ref, g_ref, o_ref):
        xv = x_ref[...]
        r = lax.rsqrt(jnp.mean(xv * xv, axis=-1, keepdims=True) + EPS)
        o_ref[...] = (xv * r * g_ref[...]).astype(o_ref.dtype)

    return pl.pallas_call(
        body, name=name, grid=(S // tm,),
        in_specs=[pl.BlockSpec((tm, D), lambda i: (i, 0)), pl.BlockSpec((1, D), lambda i: (0, 0))],
        out_specs=pl.BlockSpec((tm, D), lambda i: (i, 0)), out_shape=jax.ShapeDtypeStruct((S, D), BF16),
    )(x, g)


def _rms_bwd(name, x, g, dn, dres, tm=512):
    S, D = x.shape

    def body(x_ref, g_ref, dn_ref, dres_ref, dx_ref, dxb_ref, dg_ref):
        i = pl.program_id(0)

        @pl.when(i == 0)
        def _():
            dg_ref[...] = jnp.zeros_like(dg_ref)

        xv = x_ref[...]
        r = lax.rsqrt(jnp.mean(xv * xv, axis=-1, keepdims=True) + EPS)
        xhat = xv * r
        dnv = dn_ref[...].astype(F32)
        dg_ref[...] += jnp.sum(dnv * xhat, axis=0, keepdims=True)
        t = dnv * g_ref[...]
        dx = r * (t - xhat * jnp.mean(t * xhat, axis=-1, keepdims=True)) + dres_ref[...]
        dx_ref[...] = dx
        dxb_ref[...] = dx.astype(BF16)

    row = pl.BlockSpec((tm, D), lambda i: (i, 0))
    vec = pl.BlockSpec((1, D), lambda i: (0, 0))
    return pl.pallas_call(
        body, name=name, grid=(S // tm,), in_specs=[row, vec, row, row], out_specs=[row, row, vec],
        out_shape=[jax.ShapeDtypeStruct((S, D), F32), jax.ShapeDtypeStruct((S, D), BF16), jax.ShapeDtypeStruct((1, D), F32)],
    )(x, g, dn, dres)


def _final_loss(h2, g, target, tm=512):
    S, D = h2.shape

    def body(x_ref, g_ref, t_ref, loss_ref, dg_ref, dx_ref, dxb_ref):
        i = pl.program_id(0)

        @pl.when(i == 0)
        def _():
            loss_ref[...] = jnp.zeros_like(loss_ref)
            dg_ref[...] = jnp.zeros_like(dg_ref)

        xv = x_ref[...]
        r = lax.rsqrt(jnp.mean(xv * xv, axis=-1, keepdims=True) + EPS)
        xhat = xv * r
        gv = g_ref[...]
        diff = xhat * gv - t_ref[...]
        per_tok = jnp.mean(diff * diff, axis=-1, keepdims=True)
        loss_ref[...] += 0.5 * jnp.sum(per_tok, axis=0, keepdims=True)
        dy = diff * (1.0 / D)
        dg_ref[...] += jnp.sum(dy * xhat, axis=0, keepdims=True)
        t = dy * gv
        dx = r * (t - xhat * jnp.mean(t * xhat, axis=-1, keepdims=True))
        dx_ref[...] = dx
        dxb_ref[...] = dx.astype(BF16)

    row = pl.BlockSpec((tm, D), lambda i: (i, 0))
    vec = pl.BlockSpec((1, D), lambda i: (0, 0))
    return pl.pallas_call(
        body, name="final_loss", grid=(S // tm,), in_specs=[row, vec, row],
        out_specs=[pl.BlockSpec((1, LANES), lambda i: (0, 0)), vec, row, row],
        out_shape=[jax.ShapeDtypeStruct((1, LANES), F32), jax.ShapeDtypeStruct((1, D), F32),
                   jax.ShapeDtypeStruct((S, D), F32), jax.ShapeDtypeStruct((S, D), BF16)],
    )(h2, g, target)


def _rope_tables(S):
    pos = jnp.arange(S, dtype=F32)
    inv_freq = ROPE_THETA ** (-jnp.arange(0, ROPE_DIM, 2, dtype=F32) / ROPE_DIM)
    ang = pos[:, None] * inv_freq[None, :]
    cos, sin = jnp.cos(ang), jnp.sin(ang)
    half = ROPE_DIM // 2
    rest = HEAD - ROPE_DIM
    z_h, z_r = jnp.zeros((S, half), F32), jnp.zeros((S, rest), F32)
    tab_c = jnp.concatenate([cos, cos, jnp.ones((S, rest), F32)], axis=1)
    tab_up = jnp.concatenate([z_h, sin, z_r], axis=1)
    tab_dn = jnp.concatenate([-sin, z_h, z_r], axis=1)
    return tab_c, tab_up, tab_dn


def _rope_head(t, c, up, dn):
    half = ROPE_DIM // 2
    return t * c + pltpu.roll(t, half, axis=1) * up + pltpu.roll(t, HEAD - half, axis=1) * dn


def _rope_fwd(proj, tabs, tm=512):
    S = proj.shape[0]
    W = 2 * ATTN_W

    def body(p_ref, c_ref, up_ref, dn_ref, o_ref):
        c, up, dn = c_ref[...], up_ref[...], dn_ref[...]
        for h in range(W // HEAD):
            sl = slice(h * HEAD, (h + 1) * HEAD)
            o_ref[:, sl] = _rope_head(p_ref[:, sl].astype(F32), c, up, dn).astype(BF16)

    tab = pl.BlockSpec((tm, HEAD), lambda i: (i, 0))
    return pl.pallas_call(
        body, name="rope_fwd", grid=(S // tm,), in_specs=[pl.BlockSpec((tm, W), lambda i: (i, 0)), tab, tab, tab],
        out_specs=pl.BlockSpec((tm, W), lambda i: (i, 0)), out_shape=jax.ShapeDtypeStruct((S, W), BF16),
    )(proj, *tabs)


def _attn_grad_merge(dqs, dks, dvs, tabs, tm=256):
    S = dqs[0].shape[0]

    def body(*refs):
        q_refs, k_refs, v_refs = refs[0:3], refs[3:6], refs[6:9]
        c, up, dn = refs[9][...], refs[10][...], refs[11][...]
        o_ref = refs[12]
        for h in range(N_HEADS):
            sl = slice(h * HEAD, (h + 1) * HEAD)
            for part, rs in ((0, q_refs), (1, k_refs)):
                t = rs[0][:, sl].astype(F32) + rs[1][:, sl].astype(F32) + rs[2][:, sl].astype(F32)
                osl = slice(part * ATTN_W + h * HEAD, part * ATTN_W + (h + 1) * HEAD)
                o_ref[:, osl] = _rope_head(t, c, -up, -dn).astype(BF16)
        o_ref[:, 2 * ATTN_W:] = (v_refs[0][...].astype(F32) + v_refs[1][...].astype(F32)
                                 + v_refs[2][...].astype(F32)).astype(BF16)

    blk = pl.BlockSpec((tm, ATTN_W), lambda i: (i, 0))
    tab = pl.BlockSpec((tm, HEAD), lambda i: (i, 0))
    return pl.pallas_call(
        body, name="attn_grad_merge", grid=(S // tm,), in_specs=[blk] * 9 + [tab] * 3,
        out_specs=pl.BlockSpec((tm, 3 * ATTN_W), lambda i: (i, 0)), out_shape=jax.ShapeDtypeStruct((S, IN_MAIN), BF16),
    )(*dqs, *dks, *dvs, *tabs)


SUB = 128
WIN = SUB + 2 * N_SIDE
Q_COL, K_COL, V_COL = 0, ATTN_W // HEAD, 2 * ATTN_W // HEAD


class _AttnGeo:
    def __init__(self, S, d):
        self.S, self.d, self.L = S, d, S // d
        self.halo = N_SIDE * d
        self.TB = min(2048, S)
        self.W = self.TB + 2 * self.halo
        self.n_sub = self.TB // SUB
        self.grid = (S // self.TB, N_HEADS)
        self.dt = F32 if d > 1 else BF16
        self.su = min(d, 4)
        self.sb = d // self.su
        assert self.TB % (SUB * d) == 0 and self.TB % self.halo == 0

    def specs(self, width, col0, per_head=True):
        ratio = self.TB // self.halo
        last = self.S // self.halo - 1
        col = (lambda h: col0 + h) if per_head else (lambda h: col0)
        cur = pl.BlockSpec((self.TB, width), lambda i, h: (i, col(h)))
        prev = pl.BlockSpec((self.halo, width), lambda i, h: (jnp.maximum(i * ratio - 1, 0), col(h)))
        nxt = pl.BlockSpec((self.halo, width), lambda i, h: (jnp.minimum((i + 1) * ratio, last), col(h)))
        return cur, prev, nxt

    def scratch(self, rows, dtype=None):
        nat = pltpu.VMEM((rows, LANES), self.dt if dtype is None else dtype)
        return [nat] if self.sb == 1 else [nat, pltpu.VMEM((rows, LANES), F32)]

    def bind(self, refs):
        nat = next(refs)
        return (nat, nat) if self.sb == 1 else (nat, next(refs))

    def spread(self, pair):
        nat, streams = pair
        if self.sb > 1:
            n = nat.shape[0] // self.sb
            for a in range(self.sb):
                streams[a * n:(a + 1) * n, :] = nat[pl.ds(a, n, stride=self.sb), :]
        return streams

    def gather(self, pair):
        nat, streams = pair
        if self.sb > 1:
            n = nat.shape[0] // self.sb
            for a in range(self.sb):
                nat[pl.ds(a, n, stride=self.sb), :] = streams[a * n:(a + 1) * n, :]
        return nat

    def rows(self, sub, n, total):
        res, blk = sub % self.d, sub // self.d
        a, b = res % self.sb, res // self.sb
        start = a * (total // self.sb) + b + self.su * SUB * blk
        return pl.ds(start, n, stride=self.su) if self.su > 1 else pl.ds(start, n)

    def band(self):
        row = lax.broadcasted_iota(jnp.int32, (SUB, WIN), 0)
        col = lax.broadcasted_iota(jnp.int32, (SUB, WIN), 1)
        return (col >= row) & (col <= row + 2 * N_SIDE), col

    def mask(self, sub, band):
        inside, col = band
        blk, n_blk = sub // self.d, self.TB // (SUB * self.d)
        base = pl.program_id(0) * (self.TB // self.d) + SUB * blk
        if blk == 0:
            inside = inside & (col >= N_SIDE - base)
        if blk == n_blk - 1:
            inside = inside & (col < self.L + N_SIDE - base)
        return inside

    def fill(self, dst, c_ref):
        dst[...] = c_ref[...].astype(dst.dtype)

    def fill_window(self, dst, p_ref, c_ref, n_ref):
        dst[0:self.halo] = p_ref[...].astype(dst.dtype)
        dst[self.halo:self.halo + self.TB] = c_ref[...].astype(dst.dtype)
        dst[self.halo + self.TB:] = n_ref[...].astype(dst.dtype)


def _lane_of(tile, h):
    lane = lax.broadcasted_iota(jnp.int32, tile.shape, 1)
    return jnp.sum(jnp.where(lane == h, tile, 0.0), axis=1, keepdims=True)


def _attn_fwd(qk, proj, d):
    S = qk.shape[0]
    geo = _AttnGeo(S, d)
    scale = HEAD ** -0.5

    def body(q_ref, kp, kc, kn, vp, vc, vn, o_ref, lse_ref, *scratch):
        h = pl.program_id(1)
        refs = iter(scratch)
        q_p, k_p, v_p, o_p, l_p = (geo.bind(refs) for _ in range(5))
        geo.fill(q_p[0], q_ref)
        geo.fill_window(k_p[0], kp, kc, kn)
        geo.fill_window(v_p[0], vp, vc, vn)
        qs, ks, vs = geo.spread(q_p), geo.spread(k_p), geo.spread(v_p)
        os, ls = o_p[1], l_p[1]
        band = geo.band()
        for sub in range(geo.n_sub):
            rq, rw = geo.rows(sub, SUB, geo.TB), geo.rows(sub, WIN, geo.W)
            q_r, k_r, v_r = qs[rq, :].astype(BF16), ks[rw, :].astype(BF16), vs[rw, :].astype(BF16)
            s = jnp.where(geo.mask(sub, band), _dot(q_r, k_r, NT) * scale, NEG)
            m = jnp.max(s, axis=1, keepdims=True)
            p = jnp.exp(s - m)
            l = jnp.sum(p, axis=1, keepdims=True)
            os[rq, :] = _dot(p.astype(BF16), v_r) / l
            ls[rq, :] = jnp.broadcast_to(m + jnp.log(l), (SUB, LANES))
        o_ref[...] = geo.gather(o_p)[...].astype(BF16)

        @pl.when(h == 0)
        def _():
            lse_ref[...] = jnp.zeros_like(lse_ref)

        lane = lax.broadcasted_iota(jnp.int32, (geo.TB, LANES), 1)
        lse_ref[...] = jnp.where(lane == h, geo.gather(l_p)[...], lse_ref[...])

    q_cur, _, _ = geo.specs(HEAD, Q_COL)
    k_specs = geo.specs(HEAD, K_COL)
    v_specs = geo.specs(HEAD, V_COL)
    stat = pl.BlockSpec((geo.TB, LANES), lambda i, h: (i, 0))
    return pl.pallas_call(
        body, name=f"attn_fwd_d{d}", grid=geo.grid,
        in_specs=[q_cur, k_specs[1], k_specs[0], k_specs[2], v_specs[1], v_specs[0], v_specs[2]],
        out_specs=[q_cur, stat],
        out_shape=[jax.ShapeDtypeStruct((S, ATTN_W), BF16), jax.ShapeDtypeStruct((S, LANES), F32)],
        scratch_shapes=(geo.scratch(geo.TB) + geo.scratch(geo.W) + geo.scratch(geo.W) + geo.scratch(geo.TB, F32)
                        + geo.scratch(geo.TB, F32)),
    )(qk, qk, qk, qk, proj, proj, proj)


def _attn_combine(outs, lses, g, tm=256):
    S = outs[0].shape[0]

    def body(o1, o2, o3, l1, l2, l3, g_ref, ao_ref, o_ref, lse_ref):
        a1, a2, a3 = l1[...], l2[...], l3[...]
        mx = jnp.maximum(jnp.maximum(a1, a2), a3)
        e1, e2, e3 = jnp.exp(a1 - mx), jnp.exp(a2 - mx), jnp.exp(a3 - mx)
        den = e1 + e2 + e3
        lse_ref[...] = mx + jnp.log(den)
        head_of_col = lax.broadcasted_iota(jnp.int32, (LANES, ATTN_W), 1) // HEAD
        spread = (lax.broadcasted_iota(jnp.int32, (LANES, ATTN_W), 0) == head_of_col).astype(BF16)

        def wide(e):
            wgt = e / den
            hi = wgt.astype(BF16)
            lo = (wgt - hi.astype(F32)).astype(BF16)
            return _dot(hi, spread) + _dot(lo, spread)

        ov = wide(e1) * o1[...].astype(F32) + wide(e2) * o2[...].astype(F32) + wide(e3) * o3[...].astype(F32)
        o_ref[...] = ov
        r = lax.rsqrt(jnp.mean(ov * ov, axis=-1, keepdims=True) + EPS)
        ao_ref[...] = (ov * r * g_ref[...]).astype(BF16)

    blk = pl.BlockSpec((tm, ATTN_W), lambda i: (i, 0))
    ls = pl.BlockSpec((tm, LANES), lambda i: (i, 0))
    return pl.pallas_call(
        body, name="attn_combine", grid=(S // tm,),
        in_specs=[blk, blk, blk, ls, ls, ls, pl.BlockSpec((1, ATTN_W), lambda i: (0, 0))], out_specs=[blk, blk, ls],
        out_shape=[jax.ShapeDtypeStruct((S, D_MODEL), BF16), jax.ShapeDtypeStruct((S, ATTN_W), F32),
                   jax.ShapeDtypeStruct((S, LANES), F32)],
    )(*outs, *lses, g)


def _attn_norm_bwd(o, g, dao, tm=256):
    S = o.shape[0]

    def body(o_ref, g_ref, dao_ref, do_ref, dl_ref, dg_ref):
        i = pl.program_id(0)

        @pl.when(i == 0)
        def _():
            dg_ref[...] = jnp.zeros_like(dg_ref)

        ov = o_ref[...]
        r = lax.rsqrt(jnp.mean(ov * ov, axis=-1, keepdims=True) + EPS)
        ohat = ov * r
        dn = dao_ref[...].astype(F32)
        dg_ref[...] += jnp.sum(dn * ohat, axis=0, keepdims=True)
        t = dn * g_ref[...]
        do = r * (t - ohat * jnp.mean(t * ohat, axis=-1, keepdims=True))
        do_ref[...] = do.astype(BF16)
        prod = do * ov
        lane = lax.broadcasted_iota(jnp.int32, (tm, LANES), 1)
        tile = jnp.zeros((tm, LANES), F32)
        for h in range(N_HEADS):
            tile = jnp.where(lane == h, jnp.sum(prod[:, h * HEAD:(h + 1) * HEAD], axis=1, keepdims=True), tile)
        dl_ref[...] = tile

    blk = pl.BlockSpec((tm, ATTN_W), lambda i: (i, 0))
    vec = pl.BlockSpec((1, ATTN_W), lambda i: (0, 0))
    return pl.pallas_call(
        body, name="attn_norm_bwd", grid=(S // tm,),
        in_specs=[blk, vec, pl.BlockSpec((tm, ATTN_W), lambda i: (i, 0))],
        out_specs=[blk, pl.BlockSpec((tm, LANES), lambda i: (i, 0)), vec],
        out_shape=[jax.ShapeDtypeStruct((S, ATTN_W), BF16), jax.ShapeDtypeStruct((S, LANES), F32),
                   jax.ShapeDtypeStruct((1, ATTN_W), F32)],
    )(o, g, dao)


def _attn_bwd_dq(qk, proj, do, lse, delta, d):
    S = qk.shape[0]
    geo = _AttnGeo(S, d)
    scale = HEAD ** -0.5

    def body(q_ref, kp, kc, kn, vp, vc, vn, do_ref, lse_ref, dl_ref, dq_ref, *scratch):
        h = pl.program_id(1)
        refs = iter(scratch)
        q_p, k_p, v_p, do_p, lse_p, dl_p, dq_p = (geo.bind(refs) for _ in range(7))
        geo.fill(q_p[0], q_ref)
        geo.fill(do_p[0], do_ref)
        geo.fill(lse_p[0], lse_ref)
        geo.fill(dl_p[0], dl_ref)
        geo.fill_window(k_p[0], kp, kc, kn)
        geo.fill_window(v_p[0], vp, vc, vn)
        qs, ks, vs, dos = geo.spread(q_p), geo.spread(k_p), geo.spread(v_p), geo.spread(do_p)
        lses, dls = geo.spread(lse_p), geo.spread(dl_p)
        dqs = dq_p[1]
        band = geo.band()
        for sub in range(geo.n_sub):
            rq, rw = geo.rows(sub, SUB, geo.TB), geo.rows(sub, WIN, geo.W)
            q_r, k_r, v_r = qs[rq, :].astype(BF16), ks[rw, :].astype(BF16), vs[rw, :].astype(BF16)
            lse_c, dl_c = _lane_of(lses[rq, :], h), _lane_of(dls[rq, :], h)
            s = _dot(q_r, k_r, NT) * scale
            p = jnp.where(geo.mask(sub, band), jnp.exp(s - lse_c), 0.0)
            dp = _dot(dos[rq, :].astype(BF16), v_r, NT)
            ds = (p * (dp - dl_c) * scale).astype(BF16)
            dqs[rq, :] = _dot(ds, k_r)
        dq_ref[...] = geo.gather(dq_p)[...].astype(BF16)

    cur, _, _ = geo.specs(HEAD, 0)
    k_specs = geo.specs(HEAD, K_COL)
    v_specs = geo.specs(HEAD, V_COL)
    stat = pl.BlockSpec((geo.TB, LANES), lambda i, h: (i, 0))
    return pl.pallas_call(
        body, name=f"attn_bwd_dq_d{d}", grid=geo.grid,
        in_specs=[cur, k_specs[1], k_specs[0], k_specs[2], v_specs[1], v_specs[0], v_specs[2], cur, stat, stat],
        out_specs=cur, out_shape=jax.ShapeDtypeStruct((S, ATTN_W), BF16),
        scratch_shapes=(geo.scratch(geo.TB) + geo.scratch(geo.W) + geo.scratch(geo.W) + geo.scratch(geo.TB)
                        + geo.scratch(geo.TB, F32) + geo.scratch(geo.TB, F32) + geo.scratch(geo.TB, F32)),
    )(qk, qk, qk, qk, proj, proj, proj, do, lse, delta)


def _attn_bwd_dkv(qk, proj, do, lse, delta, d):
    S = qk.shape[0]
    geo = _AttnGeo(S, d)
    scale = HEAD ** -0.5

    def body(k_ref, v_ref, qp, qc, qn, dop, doc, don, lp, lc, ln, dlp, dlc, dln, dk_ref, dv_ref, *scratch):
        h = pl.program_id(1)
        refs = iter(scratch)
        k_p, v_p, q_p, do_p, lw_p, dlw_p, dk_p, dv_p = (geo.bind(refs) for _ in range(8))
        geo.fill(k_p[0], k_ref)
        geo.fill(v_p[0], v_ref)
        geo.fill_window(q_p[0], qp, qc, qn)
        geo.fill_window(do_p[0], dop, doc, don)
        geo.fill_window(lw_p[0], lp, lc, ln)
        geo.fill_window(dlw_p[0], dlp, dlc, dln)
        ks, vs, qs, dos = geo.spread(k_p), geo.spread(v_p), geo.spread(q_p), geo.spread(do_p)
        lws, dlws = geo.spread(lw_p), geo.spread(dlw_p)
        dks, dvs = dk_p[1], dv_p[1]
        head = lax.broadcasted_iota(jnp.int32, (LANES, WIN), 0)
        band = geo.band()
        for sub in range(geo.n_sub):
            rq, rw = geo.rows(sub, SUB, geo.TB), geo.rows(sub, WIN, geo.W)
            k_r, v_r = ks[rq, :].astype(BF16), vs[rq, :].astype(BF16)
            q_w, do_w = qs[rw, :].astype(BF16), dos[rw, :].astype(BF16)
            lse_row = jnp.sum(jnp.where(head == h, lws[rw, :].T, 0.0), axis=0, keepdims=True)
            dl_row = jnp.sum(jnp.where(head == h, dlws[rw, :].T, 0.0), axis=0, keepdims=True)
            st = _dot(k_r, q_w, NT) * scale
            pt = jnp.where(geo.mask(sub, band), jnp.exp(st - lse_row), 0.0)
            dvs[rq, :] = _dot(pt.astype(BF16), do_w)
            dpt = _dot(v_r, do_w, NT)
            dst = (pt * (dpt - dl_row) * scale).astype(BF16)
            dks[rq, :] = _dot(dst, q_w)
        dk_ref[...] = geo.gather(dk_p)[...].astype(BF16)
        dv_ref[...] = geo.gather(dv_p)[...].astype(BF16)

    q_specs = geo.specs(HEAD, Q_COL)
    k_cur, _, _ = geo.specs(HEAD, K_COL)
    v_cur, _, _ = geo.specs(HEAD, V_COL)
    do_specs = geo.specs(HEAD, 0)
    st_specs = geo.specs(LANES, 0, per_head=False)
    cur = do_specs[0]
    return pl.pallas_call(
        body, name=f"attn_bwd_dkv_d{d}", grid=geo.grid,
        in_specs=[k_cur, v_cur, q_specs[1], q_specs[0], q_specs[2], do_specs[1], do_specs[0], do_specs[2],
                  st_specs[1], st_specs[0], st_specs[2], st_specs[1], st_specs[0], st_specs[2]],
        out_specs=[cur, cur],
        out_shape=[jax.ShapeDtypeStruct((S, ATTN_W), BF16), jax.ShapeDtypeStruct((S, ATTN_W), BF16)],
        scratch_shapes=(geo.scratch(geo.TB) + geo.scratch(geo.TB) + geo.scratch(geo.W) + geo.scratch(geo.W)
                        + geo.scratch(geo.W, F32) + geo.scratch(geo.W, F32) + geo.scratch(geo.TB, F32)
                        + geo.scratch(geo.TB, F32)),
    )(qk, proj, qk, qk, qk, do, do, do, lse, lse, lse, delta, delta, delta)


def _cumsum_rows(x, reverse):
    n = x.shape[0]
    row = lax.broadcasted_iota(jnp.int32, x.shape, 0)
    s = 1
    while s < n:
        if reverse:
            x = x + jnp.where(row < n - s, pltpu.roll(x, n - s, axis=0), 0.0)
        else:
            x = x + jnp.where(row >= s, pltpu.roll(x, s, axis=0), 0.0)
        s *= 2
    return x


GLA_GROUP = 8


def _gla_rows(cc):
    return slice(cc * CHUNK, (cc + 1) * CHUNK)


def _gla_chunk_terms(q_ref, k_ref, v_ref, g_ref, h, reverse, rows, b_ref=None):
    ksl = slice(h * GLA_DK, (h + 1) * GLA_DK)
    q = q_ref[rows, ksl].astype(F32) * (GLA_DK ** -0.5)
    k = k_ref[rows, ksl].astype(F32)
    v = v_ref[rows, h * GLA_DV:(h + 1) * GLA_DV]
    b = _cumsum_rows(g_ref[rows, ksl], reverse) if b_ref is None else b_ref[rows, ksl]
    r_ref = CHUNK // 2 if reverse else CHUNK // 2 - 1
    r_last = 0 if reverse else CHUNK - 1
    b_ref, b_last = b[r_ref:r_ref + 1, :], b[r_last:r_last + 1, :]
    ii = lax.broadcasted_iota(jnp.int32, (CHUNK, CHUNK), 0)
    jj = lax.broadcasted_iota(jnp.int32, (CHUNK, CHUNK), 1)
    causal = (jj >= ii) if reverse else (jj <= ii)
    e_q, e_k = jnp.exp(b - b_ref), jnp.exp(b_ref - b)
    e_in, e_st = jnp.exp(b), jnp.exp(b_last - b)
    return dict(q=q, k=k, v=v, b=b, causal=causal, e_q=e_q, e_k=e_k, e_in=e_in, e_st=e_st, dec=jnp.exp(b_last),
                qe=q * e_q, ke=k * e_k, q_in=q * e_in, k_st=k * e_st, r_ref=r_ref, r_last=r_last)


def _gla_specs(order):
    rows = GLA_GROUP * CHUNK
    q = pl.BlockSpec((rows, GLA_K), lambda c: (order(c), 3 * ATTN_W // GLA_K))
    k = pl.BlockSpec((rows, GLA_K), lambda c: (order(c), 3 * ATTN_W // GLA_K + 1))
    v = pl.BlockSpec((rows, GLA_V), lambda c: (order(c), (3 * ATTN_W + 2 * GLA_K) // GLA_V))
    return q, k, v


def _gla_fwd(proj, gates, reverse, o_prev=None):
    S = proj.shape[0]
    n = S // CHUNK
    nb = n // GLA_GROUP
    rows = GLA_GROUP * CHUNK
    order = (lambda c: nb - 1 - c) if reverse else (lambda c: c)
    seq = list(range(GLA_GROUP))[::-1] if reverse else list(range(GLA_GROUP))
    gcol = 1 if reverse else 0

    def body(*refs):
        if o_prev is None:
            q_ref, k_ref, v_ref, g_ref, o_ref, st_ref, a_ref, b_ref, state = refs
        else:
            q_ref, k_ref, v_ref, g_ref, op_ref, o_ref, st_ref, a_ref, b_ref, state = refs
        c = pl.program_id(0)

        @pl.when(c == 0)
        def _():
            state[...] = jnp.zeros_like(state)

        for h in range(GLA_HEADS):
            vsl = slice(h * GLA_DV, (h + 1) * GLA_DV)
            st = state[h]
            for cc in seq:
                rs = _gla_rows(cc)
                t = _gla_chunk_terms(q_ref, k_ref, v_ref, g_ref, h, reverse, rs)
                b_ref[rs, h * GLA_DK:(h + 1) * GLA_DK] = t["b"]
                a = jnp.where(t["causal"], _dot(t["qe"].astype(BF16), t["ke"].astype(BF16), NT), 0.0).astype(BF16)
                a_ref[cc, h] = a
                o = _dot(a, t["v"])
                st_b = st.astype(BF16)
                st_ref[cc, h] = st_b
                o = o + _dot(t["q_in"].astype(BF16), st_b, NT)
                st = st * t["dec"] + _dot(t["v"], t["k_st"].astype(BF16), TN)
                if o_prev is not None:
                    o = o + op_ref[rs, vsl]
                o_ref[rs, vsl] = o
            state[h] = st

    q_spec, k_spec, v_spec = _gla_specs(order)
    o_spec = pl.BlockSpec((rows, GLA_V), lambda c: (order(c), 0))
    in_specs = [q_spec, k_spec, v_spec, pl.BlockSpec((rows, GLA_K), lambda c: (order(c), gcol))]
    operands = [proj, proj, proj, gates]
    if o_prev is not None:
        in_specs.append(o_spec)
        operands.append(o_prev)
    return pl.pallas_call(
        body, name="gla_fwd_rev" if reverse else "gla_fwd", grid=(nb,), in_specs=in_specs,
        out_specs=[o_spec, pl.BlockSpec((GLA_GROUP, GLA_HEADS, GLA_DV, GLA_DK), lambda c: (order(c), 0, 0, 0)),
                   pl.BlockSpec((GLA_GROUP, GLA_HEADS, CHUNK, CHUNK), lambda c: (order(c), 0, 0, 0)),
                   pl.BlockSpec((rows, GLA_K), lambda c: (order(c), 0))],
        out_shape=[jax.ShapeDtypeStruct((S, GLA_V), F32), jax.ShapeDtypeStruct((n, GLA_HEADS, GLA_DV, GLA_DK), BF16),
                   jax.ShapeDtypeStruct((n, GLA_HEADS, CHUNK, CHUNK), BF16), jax.ShapeDtypeStruct((S, GLA_K), F32)],
        scratch_shapes=[pltpu.VMEM((GLA_HEADS, GLA_DV, GLA_DK), F32)],
    )(*operands)


def _gla_bwd(proj, kept, do, reverse, prev=None):
    S = proj.shape[0]
    n = S // CHUNK
    nb = n // GLA_GROUP
    rows = GLA_GROUP * CHUNK
    order = (lambda c: c) if reverse else (lambda c: nb - 1 - c)
    seq = list(range(GLA_GROUP)) if reverse else list(range(GLA_GROUP))[::-1]
    out_dt = F32 if prev is None else BF16

    def body(*refs):
        if prev is None:
            q_ref, k_ref, v_ref, b_ref, st_ref, a_ref, do_ref, dq_ref, dk_ref, dv_ref, dg_ref, dstate = refs
        else:
            q_ref, k_ref, v_ref, b_ref, st_ref, a_ref, do_ref, pq, pk, pv, dq_ref, dk_ref, dv_ref, dg_ref, dstate = refs
        c = pl.program_id(0)

        @pl.when(c == 0)
        def _():
            dstate[...] = jnp.zeros_like(dstate)

        row = lax.broadcasted_iota(jnp.int32, (CHUNK, GLA_DK), 0)
        for h in range(GLA_HEADS):
            ksl = slice(h * GLA_DK, (h + 1) * GLA_DK)
            vsl = slice(h * GLA_DV, (h + 1) * GLA_DV)
            dst = dstate[h]
            for cc in seq:
                rs = _gla_rows(cc)
                t = _gla_chunk_terms(q_ref, k_ref, v_ref, None, h, reverse, rs, b_ref)
                v = t["v"]
                dob = do_ref[rs, vsl].astype(BF16)
                st_b = st_ref[cc, h]
                dst_b = dst.astype(BF16)
                qe_b, ke_b = t["qe"].astype(BF16), t["ke"].astype(BF16)
                q_in_b, k_st_b = t["q_in"].astype(BF16), t["k_st"].astype(BF16)
                da = jnp.where(t["causal"], _dot(dob, v, NT), 0.0).astype(BF16)
                dv = _dot(a_ref[cc, h], dob, TN) + _dot(k_st_b, dst_b, NT)
                dqe = _dot(da, ke_b)
                dke = _dot(da, qe_b, TN)
                dq_in = _dot(dob, st_b)
                dk_st = _dot(v, dst_b)
                ddec = jnp.sum(dst * st_b.astype(F32), axis=0, keepdims=True)
                dst = _dot(dob, q_in_b, TN) + dst * t["dec"]
                dq = (dqe * t["e_q"] + dq_in * t["e_in"]) * (GLA_DK ** -0.5)
                dk = dke * t["e_k"] + dk_st * t["e_st"]
                w_q, w_k = dqe * t["qe"], dke * t["ke"]
                w_st = dk_st * t["k_st"]
                db = w_q - w_k + dq_in * t["q_in"] - w_st
                db_ref = jnp.sum(w_k - w_q, axis=0, keepdims=True)
                db_last = jnp.sum(w_st, axis=0, keepdims=True) + ddec * t["dec"]
                db = db + jnp.where(row == t["r_ref"], db_ref, 0.0) + jnp.where(row == t["r_last"], db_last, 0.0)
                dg_ref[rs, ksl] = _cumsum_rows(db, not reverse)
                if prev is not None:
                    dq, dk, dv = dq + pq[rs, ksl], dk + pk[rs, ksl], dv + pv[rs, vsl]
                dq_ref[rs, ksl] = dq.astype(out_dt)
                dk_ref[rs, ksl] = dk.astype(out_dt)
                dv_ref[rs, vsl] = dv.astype(out_dt)
            dstate[h] = dst

    q_spec, k_spec, v_spec = _gla_specs(order)
    kk = pl.BlockSpec((rows, GLA_K), lambda c: (order(c), 0))
    vv = pl.BlockSpec((rows, GLA_V), lambda c: (order(c), 0))
    states, scores, sums = kept
    in_specs = [q_spec, k_spec, v_spec, kk,
                pl.BlockSpec((GLA_GROUP, GLA_HEADS, GLA_DV, GLA_DK), lambda c: (order(c), 0, 0, 0)),
                pl.BlockSpec((GLA_GROUP, GLA_HEADS, CHUNK, CHUNK), lambda c: (order(c), 0, 0, 0)), vv]
    operands = [proj, proj, proj, sums, states, scores, do]
    if prev is not None:
        in_specs += [kk, kk, vv]
        operands += list(prev)
    return pl.pallas_call(
        body, name="gla_bwd_rev" if reverse else "gla_bwd", grid=(nb,), in_specs=in_specs, out_specs=[kk, kk, vv, kk],
        out_shape=[jax.ShapeDtypeStruct((S, GLA_K), out_dt), jax.ShapeDtypeStruct((S, GLA_K), out_dt),
                   jax.ShapeDtypeStruct((S, GLA_V), out_dt), jax.ShapeDtypeStruct((S, GLA_K), F32)],
        scratch_shapes=[pltpu.VMEM((GLA_HEADS, GLA_DV, GLA_DK), F32)],
    )(*operands)


def _gates_fwd(z, wg, bias, tm=512):
    S = z.shape[0]
    W = 2 * GLA_K

    def body(z_ref, w_ref, b_ref, o_ref):
        zg = _dot(z_ref[...], w_ref[...]) + b_ref[...]
        o_ref[...] = (jnp.minimum(zg, 0.0) - jnp.log(1.0 + jnp.exp(-jnp.abs(zg)))) * (1.0 / GATE_NORM)

    return pl.pallas_call(
        body, name="gates_fwd", grid=(S // tm,),
        in_specs=[pl.BlockSpec((tm, Z_W), lambda i: (i, 0)), pl.BlockSpec((Z_W, W), lambda i: (0, 0)),
                  pl.BlockSpec((1, W), lambda i: (0, 0))],
        out_specs=pl.BlockSpec((tm, W), lambda i: (i, 0)), out_shape=jax.ShapeDtypeStruct((S, W), F32),
    )(z, wg, bias)


def _gates_bwd(z, wg, bias, dg_f, dg_b, tm=512):
    S = z.shape[0]
    W = 2 * GLA_K

    def body(z_ref, w_ref, b_ref, dgf_ref, dgb_ref, dz_ref, dw_ref, db_ref):
        i = pl.program_id(0)

        @pl.when(i == 0)
        def _():
            dw_ref[...] = jnp.zeros_like(dw_ref)
            db_ref[...] = jnp.zeros_like(db_ref)

        zv = z_ref[...]
        zg = _dot(zv, w_ref[...]) + b_ref[...]
        dg = jnp.concatenate([dgf_ref[...], dgb_ref[...]], axis=1)
        dzg = dg * (1.0 / GATE_NORM) * _sigmoid(-zg)
        db_ref[...] += jnp.sum(dzg, axis=0, keepdims=True)
        dzg_b = dzg.astype(BF16)
        dw_ref[...] += _dot(zv, dzg_b, TN)
        dz_ref[...] = _dot(dzg_b, w_ref[...], NT).astype(BF16)

    half = pl.BlockSpec((tm, GLA_K), lambda i: (i, 0))
    return pl.pallas_call(
        body, name="gates_bwd", grid=(S // tm,),
        in_specs=[pl.BlockSpec((tm, Z_W), lambda i: (i, 0)), pl.BlockSpec((Z_W, W), lambda i: (0, 0)),
                  pl.BlockSpec((1, W), lambda i: (0, 0)), half, half],
        out_specs=[pl.BlockSpec((tm, Z_W), lambda i: (i, 0)), pl.BlockSpec((Z_W, W), lambda i: (0, 0)),
                   pl.BlockSpec((1, W), lambda i: (0, 0))],
        out_shape=[jax.ShapeDtypeStruct((S, Z_W), BF16), jax.ShapeDtypeStruct((Z_W, W), F32),
                   jax.ShapeDtypeStruct((1, W), F32)],
    )(z, wg, bias, dg_f, dg_b)


def _gla_out_fwd(o, proj, g, cat, tm=512):
    S = o.shape[0]

    def body(o_ref, gr_ref, g_ref, cat_ref, out_ref):
        gn = g_ref[...]
        for h in range(GLA_HEADS):
            sl = slice(h * GLA_DV, (h + 1) * GLA_DV)
            ov = o_ref[:, sl]
            r = lax.rsqrt(jnp.mean(ov * ov, axis=-1, keepdims=True) + EPS)
            gr = gr_ref[:, sl].astype(F32)
            out_ref[:, sl] = (ov * r * gn * (gr * _sigmoid(gr))).astype(BF16)

    blk = pl.BlockSpec((tm, GLA_V), lambda i: (i, 0))
    return pl.pallas_call(
        body, name="gla_out_fwd", grid=(S // tm,),
        in_specs=[blk, pl.BlockSpec((tm, GLA_V), lambda i: (i, (3 * ATTN_W + 2 * GLA_K + GLA_V) // GLA_V)),
                  pl.BlockSpec((1, GLA_DV), lambda i: (0, 0)), ANY],
        out_specs=pl.BlockSpec((tm, GLA_V), lambda i: (i, 1)), out_shape=jax.ShapeDtypeStruct((S, D_MODEL), BF16),
        input_output_aliases={3: 0},
    )(o, proj, g, cat)


def _gla_out_bwd(o, proj, g, dcat, dproj, tm=512):
    S = o.shape[0]

    def body(o_ref, gr_ref, g_ref, dgo_ref, dproj_ref, do_ref, dgr_ref, dg_ref):
        i = pl.program_id(0)

        @pl.when(i == 0)
        def _():
            dg_ref[...] = jnp.zeros_like(dg_ref)

        gn = g_ref[...]
        dg_acc = jnp.zeros((1, GLA_DV), F32)
        for h in range(GLA_HEADS):
            sl = slice(h * GLA_DV, (h + 1) * GLA_DV)
            ov = o_ref[:, sl]
            r = lax.rsqrt(jnp.mean(ov * ov, axis=-1, keepdims=True) + EPS)
            yhat = ov * r
            gr = gr_ref[:, sl].astype(F32)
            sg = _sigmoid(gr)
            dgo = dgo_ref[:, sl].astype(F32)
            dgr_ref[:, sl] = (dgo * (yhat * gn) * (sg * (1.0 + gr * (1.0 - sg)))).astype(BF16)
            dy = dgo * (gr * sg)
            dg_acc = dg_acc + jnp.sum(dy * yhat, axis=0, keepdims=True)
            t = dy * gn
            do_ref[:, sl] = r * (t - yhat * jnp.mean(t * yhat, axis=-1, keepdims=True))
        dg_ref[...] += dg_acc

    blk = pl.BlockSpec((tm, GLA_V), lambda i: (i, 0))
    vec = pl.BlockSpec((1, GLA_DV), lambda i: (0, 0))
    return pl.pallas_call(
        body, name="gla_out_bwd", grid=(S // tm,),
        in_specs=[blk, pl.BlockSpec((tm, GLA_V), lambda i: (i, (3 * ATTN_W + 2 * GLA_K + GLA_V) // GLA_V)), vec,
                  pl.BlockSpec((tm, GLA_V), lambda i: (i, 1)), ANY],
        out_specs=[blk, pl.BlockSpec((tm, GLA_V), lambda i: (i, (3 * ATTN_W + 2 * GLA_K + GLA_V) // GLA_V)), vec],
        out_shape=[jax.ShapeDtypeStruct((S, GLA_V), F32), jax.ShapeDtypeStruct((S, IN_MAIN), BF16),
                   jax.ShapeDtypeStruct((1, GLA_DV), F32)],
        input_output_aliases={4: 1},
    )(o, proj, g, dcat, dproj)


HALO = 16


def _halo_specs(tm, tn, S):
    cur = pl.BlockSpec((tm, tn), lambda j, i: (i, j))
    prev = pl.BlockSpec((HALO, tn), lambda j, i: (jnp.maximum(i * (tm // HALO) - 1, 0), j))
    nxt = pl.BlockSpec((HALO, tn), lambda j, i: (jnp.minimum((i + 1) * (tm // HALO), S // HALO - 1), j))
    return cur, prev, nxt


def _shifted(c_ref, p_ref, n_ref, n_blocks, i=None):
    if i is None:
        i = pl.program_id(1)
    x = c_ref[...].astype(F32)
    tm = x.shape[0]
    row = lax.broadcasted_iota(jnp.int32, x.shape, 0)
    before = p_ref[HALO - 1:HALO, :].astype(F32) * (i > 0).astype(F32)
    after = n_ref[0:1, :].astype(F32) * (i < n_blocks - 1).astype(F32)
    x_m1 = jnp.where(row == 0, before, pltpu.roll(x, 1, axis=0))
    x_p1 = jnp.where(row == tm - 1, after, pltpu.roll(x, tm - 1, axis=0))
    return x, x_m1, x_p1


def _glu_fwd(gp, up, cw, cb, tm=512, tn=1408):
    S = gp.shape[0]
    nb = S // tm

    def body(c_ref, p_ref, n_ref, up_ref, w_ref, b_ref, o_ref, gate_ref):
        x, x_m1, x_p1 = _shifted(c_ref, p_ref, n_ref, nb)
        w = w_ref[...]
        gate = w[0:1, :] * x_m1 + w[1:2, :] * x + w[2:3, :] * x_p1 + b_ref[...]
        gate_ref[...] = gate.astype(BF16)
        o_ref[...] = (gate * _sigmoid(gate) * up_ref[...].astype(F32)).astype(BF16)

    cur, prev, nxt = _halo_specs(tm, tn, S)
    out = jax.ShapeDtypeStruct((S, D_FF), BF16)
    return pl.pallas_call(
        body, name="glu_fwd", grid=(D_FF // tn, nb),
        in_specs=[cur, prev, nxt, cur, pl.BlockSpec((3, tn), lambda j, i: (0, j)), pl.BlockSpec((1, tn), lambda j, i: (0, j))],
        out_specs=[cur, cur], out_shape=[out, out],
    )(gp, gp, gp, up, cw, cb)


def _glu_bwd(gate, gp, up, dact, cw, tm=512, tn=1408):
    S = gp.shape[0]
    nb = S // tm

    def body(g_ref, gb_ref, ga_ref, x_ref, up_ref, upp_ref, upn_ref, da_ref, dap_ref, dan_ref, w_ref,
             dup_ref, dgp_ref, dw_ref, db_ref):
        i = pl.program_id(1)

        @pl.when(i == 0)
        def _():
            dw_ref[...] = jnp.zeros_like(dw_ref)
            db_ref[...] = jnp.zeros_like(db_ref)

        w = w_ref[...]
        w0, w1, w2 = w[0:1, :], w[1:2, :], w[2:3, :]

        def d_gate(g, da, upv):
            sg = _sigmoid(g)
            return sg, da * upv * (sg * (1.0 + g * (1.0 - sg)))

        g = g_ref[...].astype(F32)
        da = da_ref[...].astype(F32)
        sg, dgate = d_gate(g, da, up_ref[...].astype(F32))
        dup_ref[...] = (da * (g * sg)).astype(BF16)

        last = slice(HALO - 1, HALO)
        _, dgate_before = d_gate(gb_ref[...].astype(F32)[last, :], dap_ref[...].astype(F32)[last, :], upp_ref[...].astype(F32)[last, :])
        _, dgate_after = d_gate(ga_ref[...].astype(F32)[0:1, :], dan_ref[...].astype(F32)[0:1, :], upn_ref[...].astype(F32)[0:1, :])
        dgate_before = dgate_before * (i > 0).astype(F32)
        dgate_after = dgate_after * (i < nb - 1).astype(F32)
        row = lax.broadcasted_iota(jnp.int32, dgate.shape, 0)
        dg_m1 = jnp.where(row == 0, dgate_before, pltpu.roll(dgate, 1, axis=0))
        dg_p1 = jnp.where(row == tm - 1, dgate_after, pltpu.roll(dgate, tm - 1, axis=0))
        dgp_ref[...] = (w0 * dg_p1 + w1 * dgate + w2 * dg_m1).astype(BF16)
        x = x_ref[...].astype(F32)
        db_ref[...] += jnp.sum(dgate, axis=0, keepdims=True)
        dw_ref[...] += jnp.concatenate(
            [jnp.sum(dg_p1 * x, axis=0, keepdims=True), jnp.sum(dgate * x, axis=0, keepdims=True),
             jnp.sum(dg_m1 * x, axis=0, keepdims=True)], axis=0)

    cur, prev, nxt = _halo_specs(tm, tn, S)
    w_spec = pl.BlockSpec((3, tn), lambda j, i: (0, j))
    b_spec = pl.BlockSpec((1, tn), lambda j, i: (0, j))
    return pl.pallas_call(
        body, name="glu_bwd", grid=(D_FF // tn, nb),
        in_specs=[cur, prev, nxt, cur, cur, prev, nxt, cur, prev, nxt, w_spec],
        out_specs=[cur, cur, w_spec, b_spec],
        out_shape=[jax.ShapeDtypeStruct((S, D_FF), BF16), jax.ShapeDtypeStruct((S, D_FF), BF16),
                   jax.ShapeDtypeStruct((3, D_FF), F32), jax.ShapeDtypeStruct((1, D_FF), F32)],
    )(gate, gate, gate, gp, up, up, up, dact, dact, dact, cw)


def _local_step(x, target, norm1_g, w_in_t, wg, gate_bias, gla_norm_g, attn_norm_g, w_out, norm2_g,
                w_gate4, w_up4, conv_w, conv_b, w_down, final_norm_g, on_grad=lambda event, arrays: ()):
    S = x.shape[0]
    tabs = _rope_tables(S)

    n1 = _rms_fwd("rms1_fwd", x, norm1_g)
    z_block = IN_MAIN // Z_W
    proj = _mm_nt("in_proj", n1, w_in_t, 1024, 1536, BF16, n_out=IN_MAIN)
    z = _matmul(
        "in_proj_z",
        [(n1, pl.BlockSpec((1024, D_MODEL), lambda i: (i, 0)), w_in_t, pl.BlockSpec((Z_W, D_MODEL), lambda i: (z_block, 0)), NT)],
        (S // 1024,), jax.ShapeDtypeStruct((S, Z_W), BF16), pl.BlockSpec((1024, Z_W), lambda i: (i, 0)), 1)
    qk = _rope_fwd(proj, tabs)
    branch = [_attn_fwd(qk, proj, d) for d in DILATIONS]
    ao, o_attn, lse = _attn_combine([b[0] for b in branch], [b[1] for b in branch], attn_norm_g)
    gates = _gates_fwd(z, wg, gate_bias)
    o_f, *kept_f = _gla_fwd(proj, gates, False)
    o_gla, *kept_b = _gla_fwd(proj, gates, True, o_prev=o_f)
    cat = _gla_out_fwd(o_gla, proj, gla_norm_g, ao)
    h1 = _mm_nn("out_proj", cat, w_out, 1024, 1024, F32, res=x)
    n2 = _rms_fwd("rms2_fwd", h1, norm2_g)
    gp = _mm_nn_sharded("ffn_gate", n2, w_gate4, 1024, BF16)
    up = _mm_nn_sharded("ffn_up", n2, w_up4, 1024, BF16)
    act, gate = _glu_fwd(gp, up, conv_w, conv_b)
    tk = D_FF // N_CHIPS
    h2 = _mm_nn("ffn_down", act, w_down, 1024, 512, F32, res=h1)
    loss_row, d_final_g, dh2, dh2_b = _final_loss(h2, final_norm_g.reshape(1, D_MODEL), target)

    dact = _mm_nt("ffn_down_bwd", dh2_b, w_down, 1024, tk, BF16)
    dup, dgp, d_conv_w, d_conv_b = _glu_bwd(gate, gp, up, dact, conv_w)
    d_w_down = _mm_tn("ffn_down_wgrad", act, dh2_b, 512, D_MODEL, 2048, BF16)
    on_grad("w_down", dict(w_down=d_w_down))
    dgp = _after(dgp, d_w_down)
    d_w_gate4 = _mm_tn("ffn_gate_wgrad", n2, dgp, 1024, tk, 2048, BF16, out3=tk)
    dup = _after(dup, d_w_gate4)
    d_w_up4 = _mm_tn("ffn_up_wgrad", n2, dup, 1024, tk, 2048, BF16, out3=tk)
    held = on_grad("w_gate_w_up", dict(w_gate=d_w_gate4, w_up=d_w_up4))
    dgp = _after(dgp, d_w_up4, *held)
    shard_pairs = [
        (g, pl.BlockSpec((512, tk), functools.partial(lambda s, j, i: (i, s), s)),
         w4, pl.BlockSpec((None, 512, tk), functools.partial(lambda s, j, i: (s, j, 0), s)), NT)
        for g, w4 in ((dgp, w_gate4), (dup, w_up4)) for s in range(N_CHIPS)]
    dn2 = _matmul("ffn_in_bwd", shard_pairs, (D_MODEL // 512, S // 512), jax.ShapeDtypeStruct((S, D_MODEL), BF16),
                  pl.BlockSpec((512, 512), lambda j, i: (i, j)), 1)
    dh1, dh1_b, d_norm2_g = _rms_bwd("rms2_bwd", h1, norm2_g, dn2, dh2)

    d_w_out = _mm_tn("out_proj_wgrad", cat, dh1_b, D_MODEL, 1024, 1024, BF16)
    held = on_grad("w_out", dict(w_out=d_w_out))
    dcat = _mm_nt("out_proj_bwd", _after(dh1_b, d_w_out, *held), w_out, 1024, 1024, BF16)
    do_attn, delta, d_attn_norm_g = _attn_norm_bwd(o_attn, attn_norm_g, dcat)
    dqs, dks, dvs = [], [], []
    for d in DILATIONS:
        dqs.append(_attn_bwd_dq(qk, proj, do_attn, lse, delta, d))
        dk, dv = _attn_bwd_dkv(qk, proj, do_attn, lse, delta, d)
        dks.append(dk)
        dvs.append(dv)
    dproj = _attn_grad_merge(dqs, dks, dvs, tabs)
    held = on_grad("mid", dict(anchor=dproj))
    do_gla, dproj, d_gla_norm_g = _gla_out_bwd(o_gla, proj, gla_norm_g, _after(dcat, *held), dproj)
    dq_f, dk_f, dv_f, dg_f = _gla_bwd(proj, kept_f, do_gla, False)
    dgq, dgk, dgv, dg_b = _gla_bwd(proj, kept_b, do_gla, True, prev=(dq_f, dk_f, dv_f))
    dz, d_wg, d_gate_bias = _gates_bwd(z, wg, gate_bias, dg_f, dg_b)
    dproj = lax.dynamic_update_slice(dproj, jnp.concatenate([dgq, dgk, dgv], axis=1), (0, 3 * ATTN_W))
    d_w_in_t = _mm_tn("in_proj_wgrad", dproj, n1, 768, D_MODEL, 2048, BF16, rows_out=IN_W)
    n_tok = S // 1024
    d_w_in_t = _matmul(
        "in_proj_z_wgrad",
        [(dz, pl.BlockSpec((1024, Z_W), lambda i, j, k: (k, 0)), n1, pl.BlockSpec((1024, D_MODEL), lambda i, j, k: (k, 0)), TN)],
        (1, 1, n_tok), jax.ShapeDtypeStruct((IN_W, D_MODEL), BF16), pl.BlockSpec((Z_W, D_MODEL), lambda i, j, k: (z_block, 0)),
        n_tok, into=d_w_in_t)
    held = on_grad("w_in", dict(w_in_t=d_w_in_t))
    half = S // 2048

    def in_proj_bwd(name, first, a, into):
        return _matmul(
            name,
            [(a, pl.BlockSpec((1024, IN_MAIN), lambda j, i: (i + first, 0)), w_in_t, pl.BlockSpec((IN_MAIN, 512), lambda j, i: (0, j)), NN),
             (dz, pl.BlockSpec((1024, Z_W), lambda j, i: (i + first, 0)), w_in_t, pl.BlockSpec((Z_W, 512), lambda j, i: (z_block, j)), NN)],
            (D_MODEL // 512, half), jax.ShapeDtypeStruct((S, D_MODEL), BF16),
            pl.BlockSpec((1024, 512), lambda j, i: (i + first, j)), 1, into=into)

    dproj = _after(dproj, d_w_in_t, *held)
    dn1 = in_proj_bwd("in_proj_bwd_a", 0, dproj, None)
    held = on_grad("last", dict(last=dn1))
    dn1 = in_proj_bwd("in_proj_bwd_b", half, dproj, _after(dn1, *held))
    grad_x, _, d_norm1_g = _rms_bwd("rms1_bwd", x, norm1_g, dn1, dh1)

    big = dict(w_in_t=d_w_in_t, w_out=d_w_out, w_gate4=d_w_gate4, w_up4=d_w_up4, w_down=d_w_down)
    small = dict(loss=loss_row, norm1_g=d_norm1_g, wg=d_wg, gate_bias=d_gate_bias, gla_norm_g=d_gla_norm_g,
                 attn_norm_g=d_attn_norm_g, norm2_g=d_norm2_g, conv_w=d_conv_w, conv_b=d_conv_b, final_norm_g=d_final_g)
    return grad_x, big, small


def _position():
    return lax.axis_index("x"), lax.axis_index("y"), lax.axis_index("c")


def _other_chips(x, y):
    return [(1 - x, y), (x, 1 - y), (1 - x, 1 - y)]


def _gather_chips_async(name, shards, collective_id):
    n = len(shards)

    def body(*refs):
        ins, outs = refs[:n], refs[n:2 * n]
        send, recv, loc = refs[2 * n:]
        x, y, c = _position()
        me = 2 * x + y
        chips = _other_chips(x, y)
        barrier = pltpu.get_barrier_semaphore()
        for px, py in chips:
            pl.semaphore_signal(barrier, inc=1, device_id=(px, py, c), device_id_type=MESH)
        pl.semaphore_wait(barrier, len(chips))
        started = []
        for w in range(n):
            own = pltpu.make_async_copy(ins[w], outs[w].at[me], loc.at[w])
            own.start()
            started.append(own)
        sends = []
        for w in range(n):
            for j, (px, py) in enumerate(chips):
                cp = pltpu.make_async_remote_copy(ins[w], outs[w].at[me], send.at[3 * w + j], recv.at[3 * w + j],
                                                  device_id=(px, py, c), device_id_type=MESH)
                cp.start()
                sends.append(cp)
        for w in range(n):
            for j, (px, py) in enumerate(chips):
                pltpu.make_async_remote_copy(ins[w], outs[w].at[2 * px + py], send.at[3 * w + j], recv.at[3 * w + j],
                                             device_id=(px, py, c), device_id_type=MESH).wait_recv()
        for cp in sends:
            cp.wait_send()
        for own in started:
            own.wait()

    return pl.kernel(
        body, name=name, mesh=_sequencer(),
        out_type=[jax.ShapeDtypeStruct((N_CHIPS,) + s.shape, s.dtype) for s in shards],
        scratch_types=[pltpu.SemaphoreType.DMA((3 * n,)), pltpu.SemaphoreType.DMA((3 * n,)), pltpu.SemaphoreType.DMA((n,))],
        compiler_params=pltpu.CompilerParams(collective_id=collective_id),
    )(*shards)


def _gather_halves_async(name, small, shard, collective_id):
    half = shard.shape[1] // 2

    def body(small_ref, shard_ref, small_out, out, send, recv, loc):
        x, y, c = _position()
        me = 2 * x + y
        sibling = (x, y, 1 - c)
        chips = _other_chips(x, y)
        barrier = pltpu.get_barrier_semaphore()
        for px, py in chips:
            pl.semaphore_signal(barrier, inc=1, device_id=(px, py, c), device_id_type=MESH)
        pl.semaphore_signal(barrier, inc=1, device_id=sibling, device_id_type=MESH)
        pl.semaphore_wait(barrier, len(chips) + 1)
        mine = pl.ds(pl.multiple_of(c * half, LANES), half)
        theirs = pl.ds(pl.multiple_of((1 - c) * half, LANES), half)
        own = [pltpu.make_async_copy(small_ref, small_out.at[me], loc.at[0]),
               pltpu.make_async_copy(shard_ref, out.at[me], loc.at[1])]
        for cp in own:
            cp.start()
        sends = []
        for j, (px, py) in enumerate(chips):
            sends.append(pltpu.make_async_remote_copy(small_ref, small_out.at[me], send.at[j], recv.at[j],
                                                      device_id=(px, py, c), device_id_type=MESH))
            sends.append(pltpu.make_async_remote_copy(shard_ref.at[:, mine], out.at[me, :, mine], send.at[3 + j], recv.at[3 + j],
                                                      device_id=(px, py, c), device_id_type=MESH))
        for cp in sends:
            cp.start()
        passed = []
        for j, (px, py) in enumerate(chips):
            slot = 2 * px + py
            pltpu.make_async_remote_copy(shard_ref.at[:, mine], out.at[slot, :, mine], send.at[3 + j], recv.at[3 + j],
                                         device_id=(px, py, c), device_id_type=MESH).wait_recv()
            cp = pltpu.make_async_remote_copy(out.at[slot, :, mine], out.at[slot, :, mine], send.at[6 + j], recv.at[6 + j],
                                              device_id=sibling, device_id_type=MESH)
            cp.start()
            passed.append(cp)
        for j, (px, py) in enumerate(chips):
            slot = 2 * px + py
            pltpu.make_async_remote_copy(small_ref, small_out.at[slot], send.at[j], recv.at[j],
                                         device_id=(px, py, c), device_id_type=MESH).wait_recv()
            pltpu.make_async_remote_copy(out.at[slot, :, theirs], out.at[slot, :, theirs], send.at[6 + j], recv.at[6 + j],
                                         device_id=sibling, device_id_type=MESH).wait_recv()
        for cp in sends + passed:
            cp.wait_send()
        for cp in own:
            cp.wait()

    return pl.kernel(
        body, name=name, mesh=_sequencer(),
        out_type=[jax.ShapeDtypeStruct((N_CHIPS,) + small.shape, small.dtype),
                  jax.ShapeDtypeStruct((N_CHIPS,) + shard.shape, shard.dtype)],
        scratch_types=[pltpu.SemaphoreType.DMA((9,)), pltpu.SemaphoreType.DMA((9,)), pltpu.SemaphoreType.DMA((2,))],
        compiler_params=pltpu.CompilerParams(collective_id=collective_id),
    )(small, shard)


def _sequencer():
    return plsc.ScalarSubcoreMesh(axis_name="sequencer", num_cores=1)


def _sibling_exchange_async(name, arrs, collective_id):
    n = len(arrs)

    def body(*refs):
        ins, outs = refs[:n], refs[n:2 * n]
        send, recv = refs[2 * n:]
        x, y, c = _position()
        sibling = (x, y, 1 - c)
        barrier = pltpu.get_barrier_semaphore()
        pl.semaphore_signal(barrier, inc=1, device_id=sibling, device_id_type=MESH)
        pl.semaphore_wait(barrier, 1)
        copies = [pltpu.make_async_remote_copy(ins[w], outs[w], send.at[w], recv.at[w], device_id=sibling,
                                               device_id_type=MESH) for w in range(n)]
        for cp in copies:
            cp.start()
        for cp in copies:
            cp.wait()

    return pl.kernel(
        body, name=name, out_type=[jax.ShapeDtypeStruct(a.shape, a.dtype) for a in arrs],
        scratch_types=[pltpu.SemaphoreType.DMA((n,)), pltpu.SemaphoreType.DMA((n,))],
        compiler_params=pltpu.CompilerParams(collective_id=collective_id), mesh=_sequencer(),
    )(*arrs)


def _scatter_chips_async(name, parts, collective_id):
    n = len(parts)

    def body(*refs):
        ins, outs = refs[:n], refs[n:2 * n]
        send, recv, loc = refs[2 * n:]
        x, y, c = _position()
        me = 2 * x + y
        chips = _other_chips(x, y)
        barrier = pltpu.get_barrier_semaphore()
        for px, py in chips:
            pl.semaphore_signal(barrier, inc=1, device_id=(px, py, c), device_id_type=MESH)
        pl.semaphore_wait(barrier, len(chips))
        started = []
        for w in range(n):
            own = pltpu.make_async_copy(ins[w].at[me], outs[w].at[me], loc.at[w])
            own.start()
            started.append(own)
        sends = []
        for w in range(n):
            for j, (px, py) in enumerate(chips):
                cp = pltpu.make_async_remote_copy(ins[w].at[2 * px + py], outs[w].at[me], send.at[3 * w + j],
                                                  recv.at[3 * w + j], device_id=(px, py, c), device_id_type=MESH)
                cp.start()
                sends.append(cp)
        for w in range(n):
            for j, (px, py) in enumerate(chips):
                pltpu.make_async_remote_copy(ins[w].at[me], outs[w].at[2 * px + py], send.at[3 * w + j], recv.at[3 * w + j],
                                             device_id=(px, py, c), device_id_type=MESH).wait_recv()
        for cp in sends:
            cp.wait_send()
        for own in started:
            own.wait()

    return pl.kernel(
        body, name=name, out_type=[jax.ShapeDtypeStruct(p.shape, p.dtype) for p in parts],
        scratch_types=[pltpu.SemaphoreType.DMA((3 * n,)), pltpu.SemaphoreType.DMA((3 * n,)), pltpu.SemaphoreType.DMA((n,))],
        compiler_params=pltpu.CompilerParams(collective_id=collective_id), mesh=_sequencer(),
    )(*parts)


def _allreduce_rows(buf):
    R = buf.shape[0]

    def body(in_ref, out_ref, land, send, recv):
        x, y, c = _position()
        me = 4 * x + 2 * y + c
        land[pl.ds(me, 1)] = in_ref[...][None]
        peers = []
        for mask in range(1, N_DEV):
            px = 1 - x if mask & 4 else x
            py = 1 - y if mask & 2 else y
            pc = 1 - c if mask & 1 else c
            peers.append((px, py, pc))
        sends = []
        for k, peer in enumerate(peers):
            cp = pltpu.make_async_remote_copy(in_ref, land.at[me], send.at[k], recv.at[k], device_id=peer, device_id_type=MESH)
            cp.start()
            sends.append(cp)
        for k, (px, py, pc) in enumerate(peers):
            pltpu.make_async_remote_copy(in_ref, land.at[4 * px + 2 * py + pc], send.at[k], recv.at[k],
                                         device_id=(px, py, pc), device_id_type=MESH).wait_recv()
        for cp in sends:
            cp.wait_send()
        tot = land[0]
        for i in range(1, N_DEV):
            tot = tot + land[i]
        out_ref[...] = tot

    vm = pl.BlockSpec(memory_space=pltpu.VMEM)
    return pl.pallas_call(
        body, name="allreduce_small", in_specs=[vm], out_specs=vm, out_shape=jax.ShapeDtypeStruct((R, LANES), F32),
        scratch_shapes=[pltpu.VMEM((N_DEV, R, LANES), F32), pltpu.SemaphoreType.DMA((N_DEV - 1,)),
                        pltpu.SemaphoreType.DMA((N_DEV - 1,))],
    )(buf)


def _tile2d(r, c, cap):
    if r <= cap:
        return r, c
    fits = [t for t in range(16, cap + 1, 16) if r % t == 0]
    return (max(fits), c) if fits else (r, 256)


def _pair_sum(name, a, b):
    n, r, c = a.shape
    tr, tc = _tile2d(r, c, 1024)

    def body(a_ref, b_ref, o_ref):
        o_ref[...] = (a_ref[...].astype(F32) + b_ref[...].astype(F32)).astype(BF16)

    blk = pl.BlockSpec((None, tr, tc), lambda s, i, j: (s, i, j))
    return pl.pallas_call(
        body, name=name, grid=(n, r // tr, c // tc), in_specs=[blk, blk], out_specs=blk,
        out_shape=jax.ShapeDtypeStruct(a.shape, BF16),
    )(a, b)


def _adamw_math(w, m, v, g):
    m2 = ADAM_B1 * m + (1.0 - ADAM_B1) * g
    v2 = ADAM_B2 * v + (1.0 - ADAM_B2) * (g * g)
    m_hat = m2 / (1.0 - ADAM_B1 ** ADAM_STEP)
    v_hat = v2 / (1.0 - ADAM_B2 ** ADAM_STEP)
    delta = -ADAM_LR * (m_hat / (jnp.sqrt(v_hat) + ADAM_EPS) + ADAM_WD * w)
    return delta, m2, v2


def _adamw(name, w, m, v, g):
    r, c = w.shape
    stacked = g.ndim == 3
    tr, tc = _tile2d(r, c, 256)

    def body(w_ref, m_ref, v_ref, g_ref, go_ref, d_ref, m2_ref, v2_ref):
        if stacked:
            gv = g_ref[0].astype(F32)
            for i in range(1, N_CHIPS):
                gv = gv + g_ref[i].astype(F32)
        else:
            gv = g_ref[...]
        delta, m2, v2 = _adamw_math(w_ref[...], m_ref[...], v_ref[...], gv)
        go_ref[...] = gv
        d_ref[...] = delta
        m2_ref[...] = m2
        v2_ref[...] = v2

    blk = pl.BlockSpec((tr, tc), lambda i, j: (i, j))
    g_spec = pl.BlockSpec((N_CHIPS, tr, tc), lambda i, j: (0, i, j)) if stacked else blk
    out = jax.ShapeDtypeStruct((r, c), F32)
    return pl.pallas_call(
        body, name=name, grid=(r // tr, c // tc), in_specs=[blk, blk, blk, g_spec], out_specs=[blk] * 4, out_shape=[out] * 4,
    )(w, m, v, g)


def _pack_rows(pieces):
    flat = jnp.concatenate([p.reshape(-1) for p in pieces])
    rows = flat.shape[0] // LANES
    pad = (-rows) % 8
    return jnp.pad(flat.reshape(rows, LANES), ((0, pad), (0, 0)))


def _unpack_rows(buf, shapes):
    flat = buf.reshape(-1)
    out, at = [], 0
    for s in shapes:
        size = math.prod(s)
        out.append(flat[at:at + size].reshape(s))
        at += size
    return out


SMALL_NAMES = ("norm1_g", "gf_up", "gf_b", "gb_up", "gb_b", "gla_norm_g", "attn_norm_g", "norm2_g", "conv_w", "conv_b",
               "final_norm_g")
BIG_NAMES = ("w_in", "w_out", "w_gate", "w_up", "w_down")
WEIGHT_ORDER = ("norm1_g", "w_in", "gf_up", "gf_b", "gb_up", "gb_b", "gla_norm_g", "attn_norm_g", "w_out", "norm2_g",
                "w_gate", "w_up", "conv_w", "conv_b", "w_down", "final_norm_g")


def kernel(x, norm1_g, w_in, gf_up, gf_b, gb_up, gb_b, gla_norm_g, attn_norm_g, w_out, norm2_g, w_gate, w_up, conv_w, conv_b, w_down, final_norm_g, loss_target, m_norm1_g, m_w_in, m_gf_up, m_gf_b, m_gb_up, m_gb_b, m_gla_norm_g, m_attn_norm_g, m_w_out, m_norm2_g, m_w_gate, m_w_up, m_conv_w, m_conv_b, m_w_down, m_final_norm_g, v_norm1_g, v_w_in, v_gf_up, v_gf_b, v_gb_up, v_gb_b, v_gla_norm_g, v_attn_norm_g, v_w_out, v_norm2_g, v_w_gate, v_w_up, v_conv_w, v_conv_b, v_w_down, v_final_norm_g):
    w = dict(norm1_g=norm1_g, w_in=w_in, gf_up=gf_up, gf_b=gf_b, gb_up=gb_up, gb_b=gb_b, gla_norm_g=gla_norm_g,
             attn_norm_g=attn_norm_g, w_out=w_out, norm2_g=norm2_g, w_gate=w_gate, w_up=w_up, conv_w=conv_w, conv_b=conv_b,
             w_down=w_down, final_norm_g=final_norm_g)
    m = dict(norm1_g=m_norm1_g, w_in=m_w_in, gf_up=m_gf_up, gf_b=m_gf_b, gb_up=m_gb_up, gb_b=m_gb_b, gla_norm_g=m_gla_norm_g,
             attn_norm_g=m_attn_norm_g, w_out=m_w_out, norm2_g=m_norm2_g, w_gate=m_w_gate, w_up=m_w_up, conv_w=m_conv_w,
             conv_b=m_conv_b, w_down=m_w_down, final_norm_g=m_final_norm_g)
    v = dict(norm1_g=v_norm1_g, w_in=v_w_in, gf_up=v_gf_up, gf_b=v_gf_b, gb_up=v_gb_up, gb_b=v_gb_b, gla_norm_g=v_gla_norm_g,
             attn_norm_g=v_attn_norm_g, w_out=v_w_out, norm2_g=v_norm2_g, w_gate=v_w_gate, w_up=v_w_up, conv_w=v_conv_w,
             conv_b=v_conv_b, w_down=v_w_down, final_norm_g=v_final_norm_g)
    S = x.shape[1]
    chip = 2 * lax.axis_index("x") + lax.axis_index("y")
    n_in = IN_W // N_CHIPS
    n_ff = D_FF // N_CHIPS
    n_gk = GLA_K // N_CHIPS

    def owned(t):
        return {k: (jnp.transpose(t[k][0]) if k == "w_in" else t[k][0]) for k in BIG_NAMES}

    own_w, own_m, own_v = owned(w), owned(m), owned(v)
    shard = {k: own_w[k].astype(BF16) for k in BIG_NAMES}
    small_shard = _pack_rows([gf_up[0], gb_up[0], conv_w[0]])
    small4, w_in4 = _gather_halves_async("gather_w_in", small_shard, shard["w_in"], 0)
    w_out4, w_gate4, w_up4 = _gather_chips_async("gather_w_mid", [shard["w_out"], shard["w_gate"], shard["w_up"]], 1)
    (w_down4,) = _gather_chips_async("gather_w_down", [shard["w_down"]], 2)
    w_in_t = w_in4.reshape(IN_W, D_MODEL)
    rows_up = GATE_RANK * n_gk // LANES
    rows_cw = 3 * n_ff // LANES
    gf_full = jnp.transpose(small4[:, 0:rows_up].reshape(N_CHIPS, GATE_RANK, n_gk), (1, 0, 2)).reshape(GATE_RANK, GLA_K)
    gb_full = jnp.transpose(small4[:, rows_up:2 * rows_up].reshape(N_CHIPS, GATE_RANK, n_gk), (1, 0, 2)).reshape(GATE_RANK, GLA_K)
    cw_full = jnp.transpose(small4[:, 2 * rows_up:2 * rows_up + rows_cw].reshape(N_CHIPS, 3, n_ff), (1, 0, 2)).reshape(3, D_FF)
    wg = jnp.zeros((Z_W, 2 * GLA_K), F32)
    wg = wg.at[0:GATE_RANK, 0:GLA_K].set(gf_full).at[GATE_RANK:2 * GATE_RANK, GLA_K:].set(gb_full).astype(BF16)
    gate_bias = jnp.concatenate([gf_b, gb_b], axis=1)

    pending, contributions, next_id = [], {}, [3]

    def as_shards(group, arrays):
        if group == "w_in":
            return dict(w_in=arrays["w_in_t"].reshape(N_CHIPS, n_in, D_MODEL))
        if group == "w_out":
            return dict(w_out=arrays["w_out"].reshape(N_CHIPS, D_MODEL // N_CHIPS, D_MODEL))
        if group == "w_down":
            return dict(w_down=arrays["w_down"].reshape(N_CHIPS, n_ff, D_MODEL))
        return arrays

    out = {}

    def swap(group, arrays):
        mine = as_shards(group, arrays)
        pending.append((group, mine, _sibling_exchange_async(f"sibling_{group}", list(mine.values()), next_id[0])))
        next_id[0] += 1

    def sum_and_send(anchor):
        tag, mine, theirs = pending.pop()
        sums = [_pair_sum(f"pair_sum_{k}", mine[k], _after(t, *anchor)) for k, t in zip(mine, theirs)]
        contributions.update(zip(mine, _scatter_chips_async(f"scatter_{tag}", sums, next_id[0])))
        next_id[0] += 1
        return sums

    def update(names, anchor):
        for k in names:
            res = _adamw(f"adamw_{k}", own_w[k], own_m[k], own_v[k], _after(contributions[k], *anchor))
            out[k] = [(jnp.transpose(r) if k == "w_in" else r)[None] for r in res]
        return [out[k][0] for k in names]

    def on_grad(event, arrays):
        anchor = list(arrays.values())
        held = []
        if event in ("w_gate_w_up", "w_out", "mid", "last"):
            held += sum_and_send(anchor)
        if event == "mid":
            held += update(("w_down", "w_gate", "w_up"), anchor)
        if event == "last":
            held += update(("w_out",), anchor)
        if event in ("w_down", "w_gate_w_up", "w_out", "w_in"):
            swap(event, arrays)
        return held

    grad_x, _, small = _local_step(
        x[0], loss_target[0], norm1_g, w_in_t, wg, gate_bias, gla_norm_g, attn_norm_g,
        w_out4.reshape(D_MODEL, D_MODEL), norm2_g, w_gate4, w_up4, cw_full, conv_b, w_down4.reshape(D_FF, D_MODEL), final_norm_g,
        on_grad=on_grad)
    update(("w_in",), [grad_x])

    d_gf_up = small["wg"][0:GATE_RANK, 0:GLA_K]
    d_gb_up = small["wg"][GATE_RANK:2 * GATE_RANK, GLA_K:]
    pieces = [small["loss"], small["norm1_g"], d_gf_up, small["gate_bias"][:, :GLA_K], d_gb_up, small["gate_bias"][:, GLA_K:],
              small["gla_norm_g"], small["attn_norm_g"], small["norm2_g"], small["conv_w"], small["conv_b"], small["final_norm_g"]]
    total = _allreduce_rows(_pack_rows(pieces))
    summed = _unpack_rows(total, [p.shape for p in pieces])
    loss = summed[0][0, 0]
    g_small = dict(zip(SMALL_NAMES, summed[1:]))
    g_small["gf_up"] = lax.dynamic_slice_in_dim(g_small["gf_up"], chip * n_gk, n_gk, axis=1)
    g_small["gb_up"] = lax.dynamic_slice_in_dim(g_small["gb_up"], chip * n_gk, n_gk, axis=1)
    g_small["conv_w"] = lax.dynamic_slice_in_dim(g_small["conv_w"], chip * n_ff, n_ff, axis=1)
    packed = [_pack_rows([t[k] for k in SMALL_NAMES]) for t in (w, m, v, g_small)]
    res = _adamw("adamw_small", *packed)
    shapes = [w[k].shape for k in SMALL_NAMES]
    for k, vals in zip(SMALL_NAMES, zip(*[_unpack_rows(r, shapes) for r in res])):
        out[k] = list(vals)

    grads, deltas, new_m, new_v = ([out[k][i] for k in WEIGHT_ORDER] for i in range(4))
    return (loss, grad_x[None], *grads, *deltas, *new_m, *new_v)
```

```python
import functools
import math

import jax
import jax.numpy as jnp
from jax import lax
from jax.experimental import pallas as pl
from jax.experimental.pallas import tpu as pltpu
from jax.experimental.pallas import tpu_sc as plsc

F32 = jnp.float32
BF16 = jnp.bfloat16

D_MODEL = 2048
ATTN_W = 1024
HEAD = 128
N_HEADS = 8
N_SIDE = 64
DILATIONS = (1, 4, 16)
ROPE_THETA = 500000.0
ROPE_DIM = 32
GLA_K = 512
GLA_V = 1024
GLA_HEADS = 4
GLA_DK = 128
GLA_DV = 256
GATE_RANK = 16
GATE_NORM = 16.0
CHUNK = 64
IN_MAIN = 6144
IN_W = 6176
Z_W = IN_W - IN_MAIN
D_FF = 5632
EPS = 1e-6
N_CHIPS = 4
N_DEV = 8
LANES = 128

ADAM_LR = 0.001
ADAM_B1 = 0.9
ADAM_B2 = 0.999
ADAM_EPS = 1e-08
ADAM_WD = 0.01
ADAM_STEP = 10

NEG = -1e30
MESH = pl.DeviceIdType.MESH
ANY = pl.BlockSpec(memory_space=pl.ANY)

NN = ((1,), (0,))
NT = ((1,), (1,))
TN = ((0,), (0,))


def _dot(a, b, dims=NN):
    return lax.dot_general(a, b, (dims, ((), ())), preferred_element_type=F32)


def _sigmoid(x):
    return 0.5 * jnp.tanh(0.5 * x) + 0.5


def _silu(x):
    h = 0.5 * x
    return h * jnp.tanh(h) + h


def _after(x, *deps):
    return lax.optimization_barrier((x,) + deps)[0]


def _matmul(name, pairs, grid, out_shape, out_spec, nk, res=None, into=None, first=None):
    n_in = 2 * len(pairs) + (res is not None)
    dims = [p[4] for p in pairs]

    n_ops = n_in + (into is not None) + 2 * (first is not None)

    def body(*refs):
        ins, o_ref = refs[:n_in], refs[n_ops]

        def partial_sum():
            tot = None
            for p, dn in enumerate(dims):
                a, b = ins[2 * p][...], ins[2 * p + 1][...]
                t = _dot(a.astype(BF16), b.astype(BF16), dn)
                tot = t if tot is None else tot + t
            return tot

        if nk == 1:
            t = partial_sum()
            if res is not None:
                t = t + ins[-1][...]
            o_ref[...] = t.astype(o_ref.dtype)
        else:
            acc_ref = refs[n_ops + 1]
            k = pl.program_id(2)

            @pl.when(k == 0)
            def _():
                if first is not None:
                    start = _dot(refs[n_in][...].astype(BF16), refs[n_in + 1][...].astype(BF16), first[4])
                    acc_ref[...] = start + ins[-1][...] if res is not None else start
                elif res is not None:
                    acc_ref[...] = ins[-1][...]
                else:
                    acc_ref[...] = jnp.zeros_like(acc_ref)

            acc_ref[...] += partial_sum()

            @pl.when(k == nk - 1)
            def _():
                o_ref[...] = acc_ref[...].astype(o_ref.dtype)

    operands, in_specs = [], []
    for a, a_spec, b, b_spec, _ in pairs:
        operands += [a, b]
        in_specs += [a_spec, b_spec]
    if res is not None:
        operands.append(res[0])
        in_specs.append(res[1])
    if first is not None:
        assert nk > 1
        operands += [first[0], first[2]]
        in_specs += [first[1], first[3]]
    acc_shape = tuple(s for s in out_spec.block_shape if s is not None)
    scratch = [pltpu.VMEM(acc_shape, F32)] if nk > 1 else []
    aliases = {}
    if into is not None:
        aliases = {len(operands): 0}
        operands.append(into)
        in_specs.append(ANY)
    return pl.pallas_call(
        body, name=name, grid=grid, in_specs=in_specs, out_specs=out_spec, out_shape=out_shape, scratch_shapes=scratch,
        input_output_aliases=aliases,
    )(*operands)


def _mm_nn(name, a, b, tm, tn, out_dtype, res=None):
    M, K = a.shape
    N = b.shape[1]
    pairs = [(a, pl.BlockSpec((tm, K), lambda j, i: (i, 0)), b, pl.BlockSpec((K, tn), lambda j, i: (0, j)), NN)]
    r = None if res is None else (res, pl.BlockSpec((tm, tn), lambda j, i: (i, j)))
    return _matmul(name, pairs, (N // tn, M // tm), jax.ShapeDtypeStruct((M, N), out_dtype),
                   pl.BlockSpec((tm, tn), lambda j, i: (i, j)), 1, r)


def _mm_nn_sharded(name, a, b4, tm, out_dtype):
    M, K = a.shape
    n = b4.shape[2]
    pairs = [(a, pl.BlockSpec((tm, K), lambda j, i: (i, 0)), b4, pl.BlockSpec((None, K, n), lambda j, i: (j, 0, 0)), NN)]
    return _matmul(name, pairs, (N_CHIPS, M // tm), jax.ShapeDtypeStruct((M, N_CHIPS * n), out_dtype),
                   pl.BlockSpec((tm, n), lambda j, i: (i, j)), 1)


def _mm_nt(name, a, b, tm, tn, out_dtype, res=None, n_out=None):
    M, K = a.shape
    N = b.shape[0] if n_out is None else n_out
    pairs = [(a, pl.BlockSpec((tm, K), lambda j, i: (i, 0)), b, pl.BlockSpec((tn, K), lambda j, i: (j, 0)), NT)]
    r = None if res is None else (res, pl.BlockSpec((tm, tn), lambda j, i: (i, j)))
    return _matmul(name, pairs, (N // tn, M // tm), jax.ShapeDtypeStruct((M, N), out_dtype),
                   pl.BlockSpec((tm, tn), lambda j, i: (i, j)), 1, r)


def _mm_tn(name, a, g, tka, tn, tmm, out_dtype, out3=None, rows_out=None):
    M, Ka = a.shape
    N = g.shape[1]
    pairs = [(a, pl.BlockSpec((tmm, tka), lambda i, j, k: (k, i)), g, pl.BlockSpec((tmm, tn), lambda i, j, k: (k, j)), TN)]
    if out3 is None:
        shape, spec = (Ka if rows_out is None else rows_out, N), pl.BlockSpec((tka, tn), lambda i, j, k: (i, j))
    else:
        shape, spec = (N // out3, Ka, out3), pl.BlockSpec((None, tka, tn), lambda i, j, k: (j, i, 0))
    return _matmul(name, pairs, (Ka // tka, N // tn, M // tmm), jax.ShapeDtypeStruct(shape, out_dtype), spec, M // tmm)


def _rms_fwd(name, x, g, tm=512):
    S, D = x.shape

    def body(x_ref, g_ref, o_ref):
        xv = x_ref[...]
        r = lax.rsqrt(jnp.mean(xv * xv, axis=-1, keepdims=True) + EPS)
        o_ref[...] = (xv * r * g_ref[...]).astype(o_ref.dtype)

    return pl.pallas_call(
        body, name=name, grid=(S // tm,),
        in_specs=[pl.BlockSpec((tm, D), lambda i: (i, 0)), pl.BlockSpec((1, D), lambda i: (0, 0))],
        out_specs=pl.BlockSpec((tm, D), lambda i: (i, 0)), out_shape=jax.ShapeDtypeStruct((S, D), BF16),
    )(x, g)


def _rms_bwd(name, x, g, dn, dres, tm=512):
    S, D = x.shape

    def body(x_ref, g_ref, dn_ref, dres_ref, dx_ref, dxb_ref, dg_ref):
        i = pl.program_id(0)

        @pl.when(i == 0)
        def _():
            dg_ref[...] = jnp.zeros_like(dg_ref)

        xv = x_ref[...]
        r = lax.rsqrt(jnp.mean(xv * xv, axis=-1, keepdims=True) + EPS)
        xhat = xv * r
        dnv = dn_ref[...].astype(F32)
        dg_ref[...] += jnp.sum(dnv * xhat, axis=0, keepdims=True)
        t = dnv * g_ref[...]
        dx = r * (t - xhat * jnp.mean(t * xhat, axis=-1, keepdims=True)) + dres_ref[...]
        dx_ref[...] = dx
        dxb_ref[...] = dx.astype(BF16)

    row = pl.BlockSpec((tm, D), lambda i: (i, 0))
    vec = pl.BlockSpec((1, D), lambda i: (0, 0))
    return pl.pallas_call(
        body, name=name, grid=(S // tm,), in_specs=[row, vec, row, row], out_specs=[row, row, vec],
        out_shape=[jax.ShapeDtypeStruct((S, D), F32), jax.ShapeDtypeStruct((S, D), BF16), jax.ShapeDtypeStruct((1, D), F32)],
    )(x, g, dn, dres)


def _final_loss(h2, g, target, tm=512):
    S, D = h2.shape

    def body(x_ref, g_ref, t_ref, loss_ref, dg_ref, dx_ref, dxb_ref):
        i = pl.program_id(0)

        @pl.when(i == 0)
        def _():
            loss_ref[...] = jnp.zeros_like(loss_ref)
            dg_ref[...] = jnp.zeros_like(dg_ref)

        xv = x_ref[...]
        r = lax.rsqrt(jnp.mean(xv * xv, axis=-1, keepdims=True) + EPS)
        xhat = xv * r
        gv = g_ref[...]
        diff = xhat * gv - t_ref[...]
        per_tok = jnp.mean(diff * diff, axis=-1, keepdims=True)
        loss_ref[...] += 0.5 * jnp.sum(per_tok, axis=0, keepdims=True)
        dy = diff * (1.0 / D)
        dg_ref[...] += jnp.sum(dy * xhat, axis=0, keepdims=True)
        t = dy * gv
        dx = r * (t - xhat * jnp.mean(t * xhat, axis=-1, keepdims=True))
        dx_ref[...] = dx
        dxb_ref[...] = dx.astype(BF16)

    row = pl.BlockSpec((tm, D), lambda i: (i, 0))
    vec = pl.BlockSpec((1, D), lambda i: (0, 0))
    return pl.pallas_call(
        body, name="final_loss", grid=(S // tm,), in_specs=[row, vec, row],
        out_specs=[pl.BlockSpec((1, LANES), lambda i: (0, 0)), vec, row, row],
        out_shape=[jax.ShapeDtypeStruct((1, LANES), F32), jax.ShapeDtypeStruct((1, D), F32),
                   jax.ShapeDtypeStruct((S, D), F32), jax.ShapeDtypeStruct((S, D), BF16)],
    )(h2, g, target)


def _rope_tables(S):
    pos = jnp.arange(S, dtype=F32)
    inv_freq = ROPE_THETA ** (-jnp.arange(0, ROPE_DIM, 2, dtype=F32) / ROPE_DIM)
    ang = pos[:, None] * inv_freq[None, :]
    cos, sin = jnp.cos(ang), jnp.sin(ang)
    half = ROPE_DIM // 2
    rest = HEAD - ROPE_DIM
    z_h, z_r = jnp.zeros((S, half), F32), jnp.zeros((S, rest), F32)
    tab_c = jnp.concatenate([cos, cos, jnp.ones((S, rest), F32)], axis=1)
    tab_up = jnp.concatenate([z_h, sin, z_r], axis=1)
    tab_dn = jnp.concatenate([-sin, z_h, z_r], axis=1)
    return tab_c, tab_up, tab_dn


def _rope_head(t, c, up, dn):
    half = ROPE_DIM // 2
    return t * c + pltpu.roll(t, half, axis=1) * up + pltpu.roll(t, HEAD - half, axis=1) * dn


def _rope_fwd(proj, tabs, tm=512):
    S = proj.shape[0]
    W = 2 * ATTN_W

    def body(p_ref, c_ref, up_ref, dn_ref, o_ref):
        c, up, dn = c_ref[...], up_ref[...], dn_ref[...]
        for h in range(W // HEAD):
            sl = slice(h * HEAD, (h + 1) * HEAD)
            o_ref[:, sl] = _rope_head(p_ref[:, sl].astype(F32), c, up, dn).astype(BF16)

    tab = pl.BlockSpec((tm, HEAD), lambda i: (i, 0))
    return pl.pallas_call(
        body, name="rope_fwd", grid=(S // tm,), in_specs=[pl.BlockSpec((tm, W), lambda i: (i, 0)), tab, tab, tab],
        out_specs=pl.BlockSpec((tm, W), lambda i: (i, 0)), out_shape=jax.ShapeDtypeStruct((S, W), BF16),
    )(proj, *tabs)


def _attn_grad_merge(dq, dk, dv, tabs, tm=256):
    S = dq.shape[0]

    def body(q_ref, k_ref, v_ref, c_ref, up_ref, dn_ref, o_ref):
        c, up, dn = c_ref[...], up_ref[...], dn_ref[...]
        for h in range(N_HEADS):
            sl = slice(h * HEAD, (h + 1) * HEAD)
            for part, r in ((0, q_ref), (1, k_ref)):
                osl = slice(part * ATTN_W + h * HEAD, part * ATTN_W + (h + 1) * HEAD)
                o_ref[:, osl] = _rope_head(r[:, sl].astype(F32), c, -up, -dn).astype(BF16)
        o_ref[:, 2 * ATTN_W:] = v_ref[...]

    blk = pl.BlockSpec((tm, ATTN_W), lambda i: (i, 0))
    tab = pl.BlockSpec((tm, HEAD), lambda i: (i, 0))
    return pl.pallas_call(
        body, name="attn_grad_merge", grid=(S // tm,), in_specs=[blk] * 3 + [tab] * 3,
        out_specs=pl.BlockSpec((tm, 3 * ATTN_W), lambda i: (i, 0)), out_shape=jax.ShapeDtypeStruct((S, IN_MAIN), BF16),
    )(dq, dk, dv, *tabs)


SUB = 128
WIN = SUB + 2 * N_SIDE
Q_COL, K_COL, V_COL = 0, ATTN_W // HEAD, 2 * ATTN_W // HEAD


class _AttnGeo:
    def __init__(self, S, d):
        self.S, self.d, self.L = S, d, S // d
        self.halo = N_SIDE * d
        self.TB = min(2048, S)
        self.W = self.TB + 2 * self.halo
        self.n_sub = self.TB // SUB
        self.grid = (S // self.TB, N_HEADS)
        self.dt = F32 if d > 1 else BF16
        self.su = min(d, 4)
        self.sb = d // self.su
        assert self.TB % (SUB * d) == 0 and self.TB % self.halo == 0

    def specs(self, width, col0, per_head=True):
        ratio = self.TB // self.halo
        last = self.S // self.halo - 1
        col = (lambda h: col0 + h) if per_head else (lambda h: col0)
        cur = pl.BlockSpec((self.TB, width), lambda i, h: (i, col(h)))
        prev = pl.BlockSpec((self.halo, width), lambda i, h: (jnp.maximum(i * ratio - 1, 0), col(h)))
        nxt = pl.BlockSpec((self.halo, width), lambda i, h: (jnp.minimum((i + 1) * ratio, last), col(h)))
        return cur, prev, nxt

    def scratch(self, rows, dtype=None):
        nat = pltpu.VMEM((rows, LANES), self.dt if dtype is None else dtype)
        return [nat] if self.sb == 1 else [nat, pltpu.VMEM((rows, LANES), F32)]

    def bind(self, refs):
        nat = next(refs)
        return (nat, nat) if self.sb == 1 else (nat, next(refs))

    def spread(self, pair):
        nat, streams = pair
        if self.sb > 1:
            n = nat.shape[0] // self.sb
            for a in range(self.sb):
                streams[a * n:(a + 1) * n, :] = nat[pl.ds(a, n, stride=self.sb), :]
        return streams

    def gather(self, pair):
        nat, streams = pair
        if self.sb > 1:
            n = nat.shape[0] // self.sb
            for a in range(self.sb):
                nat[pl.ds(a, n, stride=self.sb), :] = streams[a * n:(a + 1) * n, :]
        return nat

    def rows(self, sub, n, total):
        res, blk = sub % self.d, sub // self.d
        a, b = res % self.sb, res // self.sb
        start = a * (total // self.sb) + b + self.su * SUB * blk
        return pl.ds(start, n, stride=self.su) if self.su > 1 else pl.ds(start, n)

    def band(self):
        row = lax.broadcasted_iota(jnp.int32, (SUB, WIN), 0)
        col = lax.broadcasted_iota(jnp.int32, (SUB, WIN), 1)
        return (col >= row) & (col <= row + 2 * N_SIDE), col

    def mask(self, sub, band):
        inside, col = band
        blk, n_blk = sub // self.d, self.TB // (SUB * self.d)
        base = pl.program_id(0) * (self.TB // self.d) + SUB * blk
        if blk == 0:
            inside = inside & (col >= N_SIDE - base)
        if blk == n_blk - 1:
            inside = inside & (col < self.L + N_SIDE - base)
        return inside

    def fill(self, dst, c_ref):
        dst[...] = c_ref[...].astype(dst.dtype)

    def fill_window(self, dst, p_ref, c_ref, n_ref):
        dst[0:self.halo] = p_ref[...].astype(dst.dtype)
        dst[self.halo:self.halo + self.TB] = c_ref[...].astype(dst.dtype)
        dst[self.halo + self.TB:] = n_ref[...].astype(dst.dtype)


def _lane_of(tile, h):
    lane = lax.broadcasted_iota(jnp.int32, tile.shape, 1)
    return jnp.sum(jnp.where(lane == h, tile, 0.0), axis=1, keepdims=True)


def _attn_fwd(qk, proj, d):
    S = qk.shape[0]
    geo = _AttnGeo(S, d)
    scale = HEAD ** -0.5

    def body(q_ref, kp, kc, kn, vp, vc, vn, o_ref, lse_ref, *scratch):
        h = pl.program_id(1)
        refs = iter(scratch)
        q_p, k_p, v_p, o_p, l_p = (geo.bind(refs) for _ in range(5))
        geo.fill(q_p[0], q_ref)
        geo.fill_window(k_p[0], kp, kc, kn)
        geo.fill_window(v_p[0], vp, vc, vn)
        qs, ks, vs = geo.spread(q_p), geo.spread(k_p), geo.spread(v_p)
        os, ls = o_p[1], l_p[1]
        band = geo.band()
        for sub in range(geo.n_sub):
            rq, rw = geo.rows(sub, SUB, geo.TB), geo.rows(sub, WIN, geo.W)
            q_r, k_r, v_r = qs[rq, :].astype(BF16), ks[rw, :].astype(BF16), vs[rw, :].astype(BF16)
            s = jnp.where(geo.mask(sub, band), _dot(q_r, k_r, NT) * scale, NEG)
            m = jnp.max(s, axis=1, keepdims=True)
            p = jnp.exp(s - m)
            l = jnp.sum(p, axis=1, keepdims=True)
            os[rq, :] = _dot(p.astype(BF16), v_r) / l
            ls[rq, :] = jnp.broadcast_to(m + jnp.log(l), (SUB, LANES))
        o_ref[...] = geo.gather(o_p)[...].astype(BF16)

        @pl.when(h == 0)
        def _():
            lse_ref[...] = jnp.zeros_like(lse_ref)

        lane = lax.broadcasted_iota(jnp.int32, (geo.TB, LANES), 1)
        lse_ref[...] = jnp.where(lane == h, geo.gather(l_p)[...], lse_ref[...])

    q_cur, _, _ = geo.specs(HEAD, Q_COL)
    k_specs = geo.specs(HEAD, K_COL)
    v_specs = geo.specs(HEAD, V_COL)
    stat = pl.BlockSpec((geo.TB, LANES), lambda i, h: (i, 0))
    return pl.pallas_call(
        body, name=f"attn_fwd_d{d}", grid=geo.grid,
        in_specs=[q_cur, k_specs[1], k_specs[0], k_specs[2], v_specs[1], v_specs[0], v_specs[2]],
        out_specs=[q_cur, stat],
        out_shape=[jax.ShapeDtypeStruct((S, ATTN_W), BF16), jax.ShapeDtypeStruct((S, LANES), F32)],
        scratch_shapes=(geo.scratch(geo.TB) + geo.scratch(geo.W) + geo.scratch(geo.W) + geo.scratch(geo.TB, F32)
                        + geo.scratch(geo.TB, F32)),
    )(qk, qk, qk, qk, proj, proj, proj)


def _attn_combine(outs, lses, g, tm=256):
    S = outs[0].shape[0]

    def body(o1, o2, o3, l1, l2, l3, g_ref, ao_ref, o_ref, lse_ref):
        a1, a2, a3 = l1[...], l2[...], l3[...]
        mx = jnp.maximum(jnp.maximum(a1, a2), a3)
        e1, e2, e3 = jnp.exp(a1 - mx), jnp.exp(a2 - mx), jnp.exp(a3 - mx)
        den = e1 + e2 + e3
        lse_ref[...] = mx + jnp.log(den)
        head_of_col = lax.broadcasted_iota(jnp.int32, (LANES, ATTN_W), 1) // HEAD
        spread = (lax.broadcasted_iota(jnp.int32, (LANES, ATTN_W), 0) == head_of_col).astype(BF16)

        def wide(e):
            wgt = e / den
            hi = wgt.astype(BF16)
            lo = (wgt - hi.astype(F32)).astype(BF16)
            return _dot(hi, spread) + _dot(lo, spread)

        ov = wide(e1) * o1[...].astype(F32) + wide(e2) * o2[...].astype(F32) + wide(e3) * o3[...].astype(F32)
        o_ref[...] = ov
        r = lax.rsqrt(jnp.mean(ov * ov, axis=-1, keepdims=True) + EPS)
        ao_ref[...] = (ov * r * g_ref[...]).astype(BF16)

    blk = pl.BlockSpec((tm, ATTN_W), lambda i: (i, 0))
    ls = pl.BlockSpec((tm, LANES), lambda i: (i, 0))
    return pl.pallas_call(
        body, name="attn_combine", grid=(S // tm,),
        in_specs=[blk, blk, blk, ls, ls, ls, pl.BlockSpec((1, ATTN_W), lambda i: (0, 0))], out_specs=[blk, blk, ls],
        out_shape=[jax.ShapeDtypeStruct((S, D_MODEL), BF16), jax.ShapeDtypeStruct((S, ATTN_W), F32),
                   jax.ShapeDtypeStruct((S, LANES), F32)],
    )(*outs, *lses, g)


def _attn_norm_bwd(o, g, dao, tm=256):
    S = o.shape[0]

    def body(o_ref, g_ref, dao_ref, do_ref, dl_ref, dg_ref):
        i = pl.program_id(0)

        @pl.when(i == 0)
        def _():
            dg_ref[...] = jnp.zeros_like(dg_ref)

        ov = o_ref[...]
        r = lax.rsqrt(jnp.mean(ov * ov, axis=-1, keepdims=True) + EPS)
        ohat = ov * r
        dn = dao_ref[...].astype(F32)
        dg_ref[...] += jnp.sum(dn * ohat, axis=0, keepdims=True)
        t = dn * g_ref[...]
        do = r * (t - ohat * jnp.mean(t * ohat, axis=-1, keepdims=True))
        do_ref[...] = do.astype(BF16)
        prod = do * ov
        lane = lax.broadcasted_iota(jnp.int32, (tm, LANES), 1)
        tile = jnp.zeros((tm, LANES), F32)
        for h in range(N_HEADS):
            tile = jnp.where(lane == h, jnp.sum(prod[:, h * HEAD:(h + 1) * HEAD], axis=1, keepdims=True), tile)
        dl_ref[...] = tile

    blk = pl.BlockSpec((tm, ATTN_W), lambda i: (i, 0))
    vec = pl.BlockSpec((1, ATTN_W), lambda i: (0, 0))
    return pl.pallas_call(
        body, name="attn_norm_bwd", grid=(S // tm,),
        in_specs=[blk, vec, pl.BlockSpec((tm, ATTN_W), lambda i: (i, 0))],
        out_specs=[blk, pl.BlockSpec((tm, LANES), lambda i: (i, 0)), vec],
        out_shape=[jax.ShapeDtypeStruct((S, ATTN_W), BF16), jax.ShapeDtypeStruct((S, LANES), F32),
                   jax.ShapeDtypeStruct((1, ATTN_W), F32)],
    )(o, g, dao)


def _attn_bwd_dq(qk, proj, do, lse, delta, d, prev):
    S = qk.shape[0]
    geo = _AttnGeo(S, d)
    scale = HEAD ** -0.5
    before = [] if prev is None else [prev]

    def body(q_ref, kp, kc, kn, vp, vc, vn, do_ref, lse_ref, dl_ref, *rest):
        prev_refs, (dq_ref, *scratch) = rest[:len(before)], rest[len(before):]
        h = pl.program_id(1)
        refs = iter(scratch)
        q_p, k_p, v_p, do_p, lse_p, dl_p, dq_p = (geo.bind(refs) for _ in range(7))
        geo.fill(q_p[0], q_ref)
        geo.fill(do_p[0], do_ref)
        geo.fill(lse_p[0], lse_ref)
        geo.fill(dl_p[0], dl_ref)
        geo.fill_window(k_p[0], kp, kc, kn)
        geo.fill_window(v_p[0], vp, vc, vn)
        qs, ks, vs, dos = geo.spread(q_p), geo.spread(k_p), geo.spread(v_p), geo.spread(do_p)
        lses, dls = geo.spread(lse_p), geo.spread(dl_p)
        dqs = dq_p[1]
        band = geo.band()
        for sub in range(geo.n_sub):
            rq, rw = geo.rows(sub, SUB, geo.TB), geo.rows(sub, WIN, geo.W)
            q_r, k_r, v_r = qs[rq, :].astype(BF16), ks[rw, :].astype(BF16), vs[rw, :].astype(BF16)
            lse_c, dl_c = _lane_of(lses[rq, :], h), _lane_of(dls[rq, :], h)
            s = _dot(q_r, k_r, NT) * scale
            p = jnp.where(geo.mask(sub, band), jnp.exp(s - lse_c), 0.0)
            dp = _dot(dos[rq, :].astype(BF16), v_r, NT)
            ds = (p * (dp - dl_c) * scale).astype(BF16)
            dqs[rq, :] = _dot(ds, k_r)
        total = geo.gather(dq_p)[...]
        for p_ref in prev_refs:
            total = total + p_ref[...].astype(F32)
        dq_ref[...] = total.astype(BF16)

    cur, _, _ = geo.specs(HEAD, 0)
    k_specs = geo.specs(HEAD, K_COL)
    v_specs = geo.specs(HEAD, V_COL)
    stat = pl.BlockSpec((geo.TB, LANES), lambda i, h: (i, 0))
    return pl.pallas_call(
        body, name=f"attn_bwd_dq_d{d}", grid=geo.grid,
        in_specs=[cur, k_specs[1], k_specs[0], k_specs[2], v_specs[1], v_specs[0], v_specs[2], cur, stat, stat]
        + [cur] * len(before),
        out_specs=cur, out_shape=jax.ShapeDtypeStruct((S, ATTN_W), BF16),
        scratch_shapes=(geo.scratch(geo.TB) + geo.scratch(geo.W) + geo.scratch(geo.W) + geo.scratch(geo.TB)
                        + geo.scratch(geo.TB, F32) + geo.scratch(geo.TB, F32) + geo.scratch(geo.TB, F32)),
    )(qk, qk, qk, qk, proj, proj, proj, do, lse, delta, *before)


def _attn_bwd_dkv(qk, proj, do, lse, delta, d, prev):
    S = qk.shape[0]
    geo = _AttnGeo(S, d)
    scale = HEAD ** -0.5
    before = [] if prev is None else list(prev)

    def body(k_ref, v_ref, qp, qc, qn, dop, doc, don, lp, lc, ln, dlp, dlc, dln, *rest):
        prev_refs, (dk_ref, dv_ref, *scratch) = rest[:len(before)], rest[len(before):]
        h = pl.program_id(1)
        refs = iter(scratch)
        k_p, v_p, q_p, do_p, lw_p, dlw_p, dk_p, dv_p = (geo.bind(refs) for _ in range(8))
        geo.fill(k_p[0], k_ref)
        geo.fill(v_p[0], v_ref)
        geo.fill_window(q_p[0], qp, qc, qn)
        geo.fill_window(do_p[0], dop, doc, don)
        geo.fill_window(lw_p[0], lp, lc, ln)
        geo.fill_window(dlw_p[0], dlp, dlc, dln)
        ks, vs, qs, dos = geo.spread(k_p), geo.spread(v_p), geo.spread(q_p), geo.spread(do_p)
        lws, dlws = geo.spread(lw_p), geo.spread(dlw_p)
        dks, dvs = dk_p[1], dv_p[1]
        head = lax.broadcasted_iota(jnp.int32, (LANES, WIN), 0)
        band = geo.band()
        for sub in range(geo.n_sub):
            rq, rw = geo.rows(sub, SUB, geo.TB), geo.rows(sub, WIN, geo.W)
            k_r, v_r = ks[rq, :].astype(BF16), vs[rq, :].astype(BF16)
            q_w, do_w = qs[rw, :].astype(BF16), dos[rw, :].astype(BF16)
            lse_row = jnp.sum(jnp.where(head == h, lws[rw, :].T, 0.0), axis=0, keepdims=True)
            dl_row = jnp.sum(jnp.where(head == h, dlws[rw, :].T, 0.0), axis=0, keepdims=True)
            st = _dot(k_r, q_w, NT) * scale
            pt = jnp.where(geo.mask(sub, band), jnp.exp(st - lse_row), 0.0)
            dvs[rq, :] = _dot(pt.astype(BF16), do_w)
            dpt = _dot(v_r, do_w, NT)
            dst = (pt * (dpt - dl_row) * scale).astype(BF16)
            dks[rq, :] = _dot(dst, q_w)
        dk_tot, dv_tot = geo.gather(dk_p)[...], geo.gather(dv_p)[...]
        if prev_refs:
            dk_tot, dv_tot = dk_tot + prev_refs[0][...].astype(F32), dv_tot + prev_refs[1][...].astype(F32)
        dk_ref[...] = dk_tot.astype(BF16)
        dv_ref[...] = dv_tot.astype(BF16)

    q_specs = geo.specs(HEAD, Q_COL)
    k_cur, _, _ = geo.specs(HEAD, K_COL)
    v_cur, _, _ = geo.specs(HEAD, V_COL)
    do_specs = geo.specs(HEAD, 0)
    st_specs = geo.specs(LANES, 0, per_head=False)
    cur = do_specs[0]
    return pl.pallas_call(
        body, name=f"attn_bwd_dkv_d{d}", grid=geo.grid,
        in_specs=[k_cur, v_cur, q_specs[1], q_specs[0], q_specs[2], do_specs[1], do_specs[0], do_specs[2],
                  st_specs[1], st_specs[0], st_specs[2], st_specs[1], st_specs[0], st_specs[2]] + [cur] * len(before),
        out_specs=[cur, cur],
        out_shape=[jax.ShapeDtypeStruct((S, ATTN_W), BF16), jax.ShapeDtypeStruct((S, ATTN_W), BF16)],
        scratch_shapes=(geo.scratch(geo.TB) + geo.scratch(geo.TB) + geo.scratch(geo.W) + geo.scratch(geo.W)
                        + geo.scratch(geo.W, F32) + geo.scratch(geo.W, F32) + geo.scratch(geo.TB, F32)
                        + geo.scratch(geo.TB, F32)),
    )(qk, proj, qk, qk, qk, do, do, do, lse, lse, lse, delta, delta, delta, *before)


def _cumsum_rows(x, reverse):
    n = x.shape[0]
    row = lax.broadcasted_iota(jnp.int32, x.shape, 0)
    s = 1
    while s < n:
        if reverse:
            x = x + jnp.where(row < n - s, pltpu.roll(x, n - s, axis=0), 0.0)
        else:
            x = x + jnp.where(row >= s, pltpu.roll(x, s, axis=0), 0.0)
        s *= 2
    return x


GLA_GROUP = 8


def _gla_rows(cc):
    return slice(cc * CHUNK, (cc + 1) * CHUNK)


def _gla_chunk_terms(q_ref, k_ref, v_ref, g_ref, h, reverse, rows, b_ref=None):
    ksl = slice(h * GLA_DK, (h + 1) * GLA_DK)
    q = q_ref[rows, ksl].astype(F32) * (GLA_DK ** -0.5)
    k = k_ref[rows, ksl].astype(F32)
    v = v_ref[rows, h * GLA_DV:(h + 1) * GLA_DV]
    b = _cumsum_rows(g_ref[rows, ksl], reverse) if b_ref is None else b_ref[rows, ksl]
    r_ref = CHUNK // 2 if reverse else CHUNK // 2 - 1
    r_last = 0 if reverse else CHUNK - 1
    b_ref, b_last = b[r_ref:r_ref + 1, :], b[r_last:r_last + 1, :]
    ii = lax.broadcasted_iota(jnp.int32, (CHUNK, CHUNK), 0)
    jj = lax.broadcasted_iota(jnp.int32, (CHUNK, CHUNK), 1)
    causal = (jj >= ii) if reverse else (jj <= ii)
    e_q, e_k = jnp.exp(b - b_ref), jnp.exp(b_ref - b)
    e_in, e_st = jnp.exp(b), jnp.exp(b_last - b)
    return dict(q=q, k=k, v=v, b=b, causal=causal, e_q=e_q, e_k=e_k, e_in=e_in, e_st=e_st, dec=jnp.exp(b_last),
                qe=q * e_q, ke=k * e_k, q_in=q * e_in, k_st=k * e_st, r_ref=r_ref, r_last=r_last)


def _gla_specs(order):
    rows = GLA_GROUP * CHUNK
    q = pl.BlockSpec((rows, GLA_K), lambda c: (order(c), 3 * ATTN_W // GLA_K))
    k = pl.BlockSpec((rows, GLA_K), lambda c: (order(c), 3 * ATTN_W // GLA_K + 1))
    v = pl.BlockSpec((rows, GLA_V), lambda c: (order(c), (3 * ATTN_W + 2 * GLA_K) // GLA_V))
    return q, k, v


def _gla_fwd(proj, gates, reverse, o_prev=None):
    S = proj.shape[0]
    n = S // CHUNK
    nb = n // GLA_GROUP
    rows = GLA_GROUP * CHUNK
    order = (lambda c: nb - 1 - c) if reverse else (lambda c: c)
    seq = list(range(GLA_GROUP))[::-1] if reverse else list(range(GLA_GROUP))
    gcol = 1 if reverse else 0

    def body(*refs):
        if o_prev is None:
            q_ref, k_ref, v_ref, g_ref, o_ref, st_ref, a_ref, b_ref, state = refs
        else:
            q_ref, k_ref, v_ref, g_ref, op_ref, o_ref, st_ref, a_ref, b_ref, state = refs
        c = pl.program_id(0)

        @pl.when(c == 0)
        def _():
            state[...] = jnp.zeros_like(state)

        for h in range(GLA_HEADS):
            vsl = slice(h * GLA_DV, (h + 1) * GLA_DV)
            st = state[h]
            for cc in seq:
                rs = _gla_rows(cc)
                t = _gla_chunk_terms(q_ref, k_ref, v_ref, g_ref, h, reverse, rs)
                b_ref[rs, h * GLA_DK:(h + 1) * GLA_DK] = t["b"]
                a = jnp.where(t["causal"], _dot(t["qe"].astype(BF16), t["ke"].astype(BF16), NT), 0.0).astype(BF16)
                a_ref[cc, h] = a
                o = _dot(a, t["v"])
                st_b = st.astype(BF16)
                st_ref[cc, h] = st_b
                o = o + _dot(t["q_in"].astype(BF16), st_b, NT)
                st = st * t["dec"] + _dot(t["v"], t["k_st"].astype(BF16), TN)
                if o_prev is not None:
                    o = o + op_ref[rs, vsl]
                o_ref[rs, vsl] = o
            state[h] = st

    q_spec, k_spec, v_spec = _gla_specs(order)
    o_spec = pl.BlockSpec((rows, GLA_V), lambda c: (order(c), 0))
    in_specs = [q_spec, k_spec, v_spec, pl.BlockSpec((rows, GLA_K), lambda c: (order(c), gcol))]
    operands = [proj, proj, proj, gates]
    if o_prev is not None:
        in_specs.append(o_spec)
        operands.append(o_prev)
    return pl.pallas_call(
        body, name="gla_fwd_rev" if reverse else "gla_fwd", grid=(nb,), in_specs=in_specs,
        out_specs=[o_spec, pl.BlockSpec((GLA_GROUP, GLA_HEADS, GLA_DV, GLA_DK), lambda c: (order(c), 0, 0, 0)),
                   pl.BlockSpec((GLA_GROUP, GLA_HEADS, CHUNK, CHUNK), lambda c: (order(c), 0, 0, 0)),
                   pl.BlockSpec((rows, GLA_K), lambda c: (order(c), 0))],
        out_shape=[jax.ShapeDtypeStruct((S, GLA_V), F32), jax.ShapeDtypeStruct((n, GLA_HEADS, GLA_DV, GLA_DK), BF16),
                   jax.ShapeDtypeStruct((n, GLA_HEADS, CHUNK, CHUNK), BF16), jax.ShapeDtypeStruct((S, GLA_K), F32)],
        scratch_shapes=[pltpu.VMEM((GLA_HEADS, GLA_DV, GLA_DK), F32)],
    )(*operands)


def _gla_bwd(proj, kept, do, reverse, prev=None):
    S = proj.shape[0]
    n = S // CHUNK
    nb = n // GLA_GROUP
    rows = GLA_GROUP * CHUNK
    order = (lambda c: c) if reverse else (lambda c: nb - 1 - c)
    seq = list(range(GLA_GROUP)) if reverse else list(range(GLA_GROUP))[::-1]
    out_dt = F32 if prev is None else BF16

    def body(*refs):
        if prev is None:
            q_ref, k_ref, v_ref, b_ref, st_ref, a_ref, do_ref, dq_ref, dk_ref, dv_ref, dg_ref, dstate = refs
        else:
            q_ref, k_ref, v_ref, b_ref, st_ref, a_ref, do_ref, pq, pk, pv, dq_ref, dk_ref, dv_ref, dg_ref, dstate = refs
        c = pl.program_id(0)

        @pl.when(c == 0)
        def _():
            dstate[...] = jnp.zeros_like(dstate)

        row = lax.broadcasted_iota(jnp.int32, (CHUNK, GLA_DK), 0)
        for h in range(GLA_HEADS):
            ksl = slice(h * GLA_DK, (h + 1) * GLA_DK)
            vsl = slice(h * GLA_DV, (h + 1) * GLA_DV)
            dst = dstate[h]
            for cc in seq:
                rs = _gla_rows(cc)
                t = _gla_chunk_terms(q_ref, k_ref, v_ref, None, h, reverse, rs, b_ref)
                v = t["v"]
                dob = do_ref[rs, vsl].astype(BF16)
                st_b = st_ref[cc, h]
                dst_b = dst.astype(BF16)
                qe_b, ke_b = t["qe"].astype(BF16), t["ke"].astype(BF16)
                q_in_b, k_st_b = t["q_in"].astype(BF16), t["k_st"].astype(BF16)
                da = jnp.where(t["causal"], _dot(dob, v, NT), 0.0).astype(BF16)
                dv = _dot(a_ref[cc, h], dob, TN) + _dot(k_st_b, dst_b, NT)
                dqe = _dot(da, ke_b)
                dke = _dot(da, qe_b, TN)
                dq_in = _dot(dob, st_b)
                dk_st = _dot(v, dst_b)
                ddec = jnp.sum(dst * st_b.astype(F32), axis=0, keepdims=True)
                dst = _dot(dob, q_in_b, TN) + dst * t["dec"]
                dq = (dqe * t["e_q"] + dq_in * t["e_in"]) * (GLA_DK ** -0.5)
                dk = dke * t["e_k"] + dk_st * t["e_st"]
                w_q, w_k = dqe * t["qe"], dke * t["ke"]
                w_st = dk_st * t["k_st"]
                db = w_q - w_k + dq_in * t["q_in"] - w_st
                db_ref = jnp.sum(w_k - w_q, axis=0, keepdims=True)
                db_last = jnp.sum(w_st, axis=0, keepdims=True) + ddec * t["dec"]
                db = db + jnp.where(row == t["r_ref"], db_ref, 0.0) + jnp.where(row == t["r_last"], db_last, 0.0)
                dg_ref[rs, ksl] = _cumsum_rows(db, not reverse)
                if prev is not None:
                    dq, dk, dv = dq + pq[rs, ksl], dk + pk[rs, ksl], dv + pv[rs, vsl]
                dq_ref[rs, ksl] = dq.astype(out_dt)
                dk_ref[rs, ksl] = dk.astype(out_dt)
                dv_ref[rs, vsl] = dv.astype(out_dt)
            dstate[h] = dst

    q_spec, k_spec, v_spec = _gla_specs(order)
    kk = pl.BlockSpec((rows, GLA_K), lambda c: (order(c), 0))
    vv = pl.BlockSpec((rows, GLA_V), lambda c: (order(c), 0))
    states, scores, sums = kept
    in_specs = [q_spec, k_spec, v_spec, kk,
                pl.BlockSpec((GLA_GROUP, GLA_HEADS, GLA_DV, GLA_DK), lambda c: (order(c), 0, 0, 0)),
                pl.BlockSpec((GLA_GROUP, GLA_HEADS, CHUNK, CHUNK), lambda c: (order(c), 0, 0, 0)), vv]
    operands = [proj, proj, proj, sums, states, scores, do]
    if prev is not None:
        in_specs += [kk, kk, vv]
        operands += list(prev)
    return pl.pallas_call(
        body, name="gla_bwd_rev" if reverse else "gla_bwd", grid=(nb,), in_specs=in_specs, out_specs=[kk, kk, vv, kk],
        out_shape=[jax.ShapeDtypeStruct((S, GLA_K), out_dt), jax.ShapeDtypeStruct((S, GLA_K), out_dt),
                   jax.ShapeDtypeStruct((S, GLA_V), out_dt), jax.ShapeDtypeStruct((S, GLA_K), F32)],
        scratch_shapes=[pltpu.VMEM((GLA_HEADS, GLA_DV, GLA_DK), F32)],
    )(*operands)


def _gates_fwd(z, wg, bias, tm=512):
    S = z.shape[0]
    W = 2 * GLA_K

    def body(z_ref, w_ref, b_ref, o_ref):
        zg = _dot(z_ref[...], w_ref[...]) + b_ref[...]
        o_ref[...] = (jnp.minimum(zg, 0.0) - jnp.log(1.0 + jnp.exp(-jnp.abs(zg)))) * (1.0 / GATE_NORM)

    return pl.pallas_call(
        body, name="gates_fwd", grid=(S // tm,),
        in_specs=[pl.BlockSpec((tm, Z_W), lambda i: (i, 0)), pl.BlockSpec((Z_W, W), lambda i: (0, 0)),
                  pl.BlockSpec((1, W), lambda i: (0, 0))],
        out_specs=pl.BlockSpec((tm, W), lambda i: (i, 0)), out_shape=jax.ShapeDtypeStruct((S, W), F32),
    )(z, wg, bias)


def _gates_bwd(z, wg, bias, dg_f, dg_b, tm=512):
    S = z.shape[0]
    W = 2 * GLA_K

    def body(z_ref, w_ref, b_ref, dgf_ref, dgb_ref, dz_ref, dw_ref, db_ref):
        i = pl.program_id(0)

        @pl.when(i == 0)
        def _():
            dw_ref[...] = jnp.zeros_like(dw_ref)
            db_ref[...] = jnp.zeros_like(db_ref)

        zv = z_ref[...]
        zg = _dot(zv, w_ref[...]) + b_ref[...]
        dg = jnp.concatenate([dgf_ref[...], dgb_ref[...]], axis=1)
        dzg = dg * (1.0 / GATE_NORM) * _sigmoid(-zg)
        db_ref[...] += jnp.sum(dzg, axis=0, keepdims=True)
        dzg_b = dzg.astype(BF16)
        dw_ref[...] += _dot(zv, dzg_b, TN)
        dz_ref[...] = _dot(dzg_b, w_ref[...], NT).astype(BF16)

    half = pl.BlockSpec((tm, GLA_K), lambda i: (i, 0))
    return pl.pallas_call(
        body, name="gates_bwd", grid=(S // tm,),
        in_specs=[pl.BlockSpec((tm, Z_W), lambda i: (i, 0)), pl.BlockSpec((Z_W, W), lambda i: (0, 0)),
                  pl.BlockSpec((1, W), lambda i: (0, 0)), half, half],
        out_specs=[pl.BlockSpec((tm, Z_W), lambda i: (i, 0)), pl.BlockSpec((Z_W, W), lambda i: (0, 0)),
                   pl.BlockSpec((1, W), lambda i: (0, 0))],
        out_shape=[jax.ShapeDtypeStruct((S, Z_W), BF16), jax.ShapeDtypeStruct((Z_W, W), F32),
                   jax.ShapeDtypeStruct((1, W), F32)],
    )(z, wg, bias, dg_f, dg_b)


def _gla_out_fwd(o, proj, g, cat, tm=512):
    S = o.shape[0]

    def body(o_ref, gr_ref, g_ref, cat_ref, out_ref):
        gn = g_ref[...]
        for h in range(GLA_HEADS):
            sl = slice(h * GLA_DV, (h + 1) * GLA_DV)
            ov = o_ref[:, sl]
            r = lax.rsqrt(jnp.mean(ov * ov, axis=-1, keepdims=True) + EPS)
            gr = gr_ref[:, sl].astype(F32)
            out_ref[:, sl] = (ov * r * gn * _silu(gr)).astype(BF16)

    blk = pl.BlockSpec((tm, GLA_V), lambda i: (i, 0))
    return pl.pallas_call(
        body, name="gla_out_fwd", grid=(S // tm,),
        in_specs=[blk, pl.BlockSpec((tm, GLA_V), lambda i: (i, (3 * ATTN_W + 2 * GLA_K + GLA_V) // GLA_V)),
                  pl.BlockSpec((1, GLA_DV), lambda i: (0, 0)), ANY],
        out_specs=pl.BlockSpec((tm, GLA_V), lambda i: (i, 1)), out_shape=jax.ShapeDtypeStruct((S, D_MODEL), BF16),
        input_output_aliases={3: 0},
    )(o, proj, g, cat)


def _gla_out_bwd(o, proj, g, dcat, dproj, tm=512):
    S = o.shape[0]

    def body(o_ref, gr_ref, g_ref, dgo_ref, dproj_ref, do_ref, dgr_ref, dg_ref):
        i = pl.program_id(0)

        @pl.when(i == 0)
        def _():
            dg_ref[...] = jnp.zeros_like(dg_ref)

        gn = g_ref[...]
        dg_acc = jnp.zeros((1, GLA_DV), F32)
        for h in range(GLA_HEADS):
            sl = slice(h * GLA_DV, (h + 1) * GLA_DV)
            ov = o_ref[:, sl]
            r = lax.rsqrt(jnp.mean(ov * ov, axis=-1, keepdims=True) + EPS)
            yhat = ov * r
            gr = gr_ref[:, sl].astype(F32)
            sg = _sigmoid(gr)
            dgo = dgo_ref[:, sl].astype(F32)
            dgr_ref[:, sl] = (dgo * (yhat * gn) * (sg * (1.0 + gr * (1.0 - sg)))).astype(BF16)
            dy = dgo * (gr * sg)
            dg_acc = dg_acc + jnp.sum(dy * yhat, axis=0, keepdims=True)
            t = dy * gn
            do_ref[:, sl] = r * (t - yhat * jnp.mean(t * yhat, axis=-1, keepdims=True))
        dg_ref[...] += dg_acc

    blk = pl.BlockSpec((tm, GLA_V), lambda i: (i, 0))
    vec = pl.BlockSpec((1, GLA_DV), lambda i: (0, 0))
    return pl.pallas_call(
        body, name="gla_out_bwd", grid=(S // tm,),
        in_specs=[blk, pl.BlockSpec((tm, GLA_V), lambda i: (i, (3 * ATTN_W + 2 * GLA_K + GLA_V) // GLA_V)), vec,
                  pl.BlockSpec((tm, GLA_V), lambda i: (i, 1)), ANY],
        out_specs=[blk, pl.BlockSpec((tm, GLA_V), lambda i: (i, (3 * ATTN_W + 2 * GLA_K + GLA_V) // GLA_V)), vec],
        out_shape=[jax.ShapeDtypeStruct((S, GLA_V), F32), jax.ShapeDtypeStruct((S, IN_MAIN), BF16),
                   jax.ShapeDtypeStruct((1, GLA_DV), F32)],
        input_output_aliases={4: 1},
    )(o, proj, g, dcat, dproj)


HALO = 16


def _halo_specs(tm, tn, S):
    cur = pl.BlockSpec((tm, tn), lambda j, i: (i, j))
    prev = pl.BlockSpec((HALO, tn), lambda j, i: (jnp.maximum(i * (tm // HALO) - 1, 0), j))
    nxt = pl.BlockSpec((HALO, tn), lambda j, i: (jnp.minimum((i + 1) * (tm // HALO), S // HALO - 1), j))
    return cur, prev, nxt


def _shifted(c_ref, p_ref, n_ref, n_blocks, i=None):
    if i is None:
        i = pl.program_id(1)
    x = c_ref[...].astype(F32)
    tm = x.shape[0]
    row = lax.broadcasted_iota(jnp.int32, x.shape, 0)
    before = p_ref[HALO - 1:HALO, :].astype(F32) * (i > 0).astype(F32)
    after = n_ref[0:1, :].astype(F32) * (i < n_blocks - 1).astype(F32)
    x_m1 = jnp.where(row == 0, before, pltpu.roll(x, 1, axis=0))
    x_p1 = jnp.where(row == tm - 1, after, pltpu.roll(x, tm - 1, axis=0))
    return x, x_m1, x_p1


def _glu_fwd(gp, up, cw, cb, tm=512, tn=1408):
    S = gp.shape[0]
    nb = S // tm

    def body(c_ref, p_ref, n_ref, up_ref, w_ref, b_ref, o_ref, gate_ref):
        x, x_m1, x_p1 = _shifted(c_ref, p_ref, n_ref, nb)
        w = w_ref[...]
        gate = w[0:1, :] * x_m1 + w[1:2, :] * x + w[2:3, :] * x_p1 + b_ref[...]
        gate_ref[...] = gate.astype(BF16)
        o_ref[...] = (_silu(gate) * up_ref[...].astype(F32)).astype(BF16)

    cur, prev, nxt = _halo_specs(tm, tn, S)
    out = jax.ShapeDtypeStruct((S, D_FF), BF16)
    return pl.pallas_call(
        body, name="glu_fwd", grid=(D_FF // tn, nb),
        in_specs=[cur, prev, nxt, cur, pl.BlockSpec((3, tn), lambda j, i: (0, j)), pl.BlockSpec((1, tn), lambda j, i: (0, j))],
        out_specs=[cur, cur], out_shape=[out, out],
    )(gp, gp, gp, up, cw, cb)


def _glu_bwd(gate, gp, up, dact, cw, tm=512, tn=1408):
    S = gp.shape[0]
    nb = S // tm

    def body(g_ref, gb_ref, ga_ref, x_ref, up_ref, upp_ref, upn_ref, da_ref, dap_ref, dan_ref, w_ref,
             dup_ref, dgp_ref, dw_ref, db_ref):
        i = pl.program_id(1)

        @pl.when(i == 0)
        def _():
            dw_ref[...] = jnp.zeros_like(dw_ref)
            db_ref[...] = jnp.zeros_like(db_ref)

        w = w_ref[...]
        w0, w1, w2 = w[0:1, :], w[1:2, :], w[2:3, :]

        def d_gate(g, da, upv):
            sg = _sigmoid(g)
            return sg, da * upv * (sg * (1.0 + g * (1.0 - sg)))

        g = g_ref[...].astype(F32)
        da = da_ref[...].astype(F32)
        sg, dgate = d_gate(g, da, up_ref[...].astype(F32))
        dup_ref[...] = (da * (g * sg)).astype(BF16)

        last = slice(HALO - 1, HALO)
        _, dgate_before = d_gate(gb_ref[...].astype(F32)[last, :], dap_ref[...].astype(F32)[last, :], upp_ref[...].astype(F32)[last, :])
        _, dgate_after = d_gate(ga_ref[...].astype(F32)[0:1, :], dan_ref[...].astype(F32)[0:1, :], upn_ref[...].astype(F32)[0:1, :])
        dgate_before = dgate_before * (i > 0).astype(F32)
        dgate_after = dgate_after * (i < nb - 1).astype(F32)
        row = lax.broadcasted_iota(jnp.int32, dgate.shape, 0)
        dg_m1 = jnp.where(row == 0, dgate_before, pltpu.roll(dgate, 1, axis=0))
        dg_p1 = jnp.where(row == tm - 1, dgate_after, pltpu.roll(dgate, tm - 1, axis=0))
        dgp_ref[...] = (w0 * dg_p1 + w1 * dgate + w2 * dg_m1).astype(BF16)
        x = x_ref[...].astype(F32)
        db_ref[...] += jnp.sum(dgate, axis=0, keepdims=True)
        dw_ref[...] += jnp.concatenate(
            [jnp.sum(dg_p1 * x, axis=0, keepdims=True), jnp.sum(dgate * x, axis=0, keepdims=True),
             jnp.sum(dg_m1 * x, axis=0, keepdims=True)], axis=0)

    cur, prev, nxt = _halo_specs(tm, tn, S)
    w_spec = pl.BlockSpec((3, tn), lambda j, i: (0, j))
    b_spec = pl.BlockSpec((1, tn), lambda j, i: (0, j))
    return pl.pallas_call(
        body, name="glu_bwd", grid=(D_FF // tn, nb),
        in_specs=[cur, prev, nxt, cur, cur, prev, nxt, cur, prev, nxt, w_spec],
        out_specs=[cur, cur, w_spec, b_spec],
        out_shape=[jax.ShapeDtypeStruct((S, D_FF), BF16), jax.ShapeDtypeStruct((S, D_FF), BF16),
                   jax.ShapeDtypeStruct((3, D_FF), F32), jax.ShapeDtypeStruct((1, D_FF), F32)],
    )(gate, gate, gate, gp, up, up, up, dact, dact, dact, cw)


def _local_step(x, target, norm1_g, w_in_t, wg, gate_bias, gla_norm_g, attn_norm_g, w_out, norm2_g,
                w_gate4, w_up4, conv_w, conv_b, w_down, final_norm_g, on_grad=lambda event, arrays: ()):
    S = x.shape[0]
    tabs = _rope_tables(S)

    n1 = _rms_fwd("rms1_fwd", x, norm1_g)
    z_block = IN_MAIN // Z_W
    proj = _mm_nt("in_proj", n1, w_in_t, 1024, 1536, BF16, n_out=IN_MAIN)
    z = _matmul(
        "in_proj_z",
        [(n1, pl.BlockSpec((1024, D_MODEL), lambda i: (i, 0)), w_in_t, pl.BlockSpec((Z_W, D_MODEL), lambda i: (z_block, 0)), NT)],
        (S // 1024,), jax.ShapeDtypeStruct((S, Z_W), BF16), pl.BlockSpec((1024, Z_W), lambda i: (i, 0)), 1)
    qk = _rope_fwd(proj, tabs)
    branch = [_attn_fwd(qk, proj, d) for d in DILATIONS]
    ao, o_attn, lse = _attn_combine([b[0] for b in branch], [b[1] for b in branch], attn_norm_g)
    gates = _gates_fwd(z, wg, gate_bias)
    o_f, *kept_f = _gla_fwd(proj, gates, False)
    o_gla, *kept_b = _gla_fwd(proj, gates, True, o_prev=o_f)
    cat = _gla_out_fwd(o_gla, proj, gla_norm_g, ao)
    h1 = _mm_nn("out_proj", cat, w_out, 1024, 1024, F32, res=x)
    n2 = _rms_fwd("rms2_fwd", h1, norm2_g)
    gp = _mm_nn_sharded("ffn_gate", n2, w_gate4, 1024, BF16)
    up = _mm_nn_sharded("ffn_up", n2, w_up4, 1024, BF16)
    act, gate = _glu_fwd(gp, up, conv_w, conv_b)
    tk = D_FF // N_CHIPS
    h2 = _mm_nn("ffn_down", act, w_down, 1024, 512, F32, res=h1)
    loss_row, d_final_g, dh2, dh2_b = _final_loss(h2, final_norm_g.reshape(1, D_MODEL), target)

    dact = _mm_nt("ffn_down_bwd", dh2_b, w_down, 1024, tk, BF16)
    dup, dgp, d_conv_w, d_conv_b = _glu_bwd(gate, gp, up, dact, conv_w)
    d_w_down = _mm_tn("ffn_down_wgrad", act, dh2_b, 512, D_MODEL, 2048, BF16)
    on_grad("w_down", dict(w_down=d_w_down))
    dgp = _after(dgp, d_w_down)
    d_w_gate4 = _mm_tn("ffn_gate_wgrad", n2, dgp, 1024, tk, 2048, BF16, out3=tk)
    dup = _after(dup, d_w_gate4)
    d_w_up4 = _mm_tn("ffn_up_wgrad", n2, dup, 1024, tk, 2048, BF16, out3=tk)
    held = on_grad("w_gate_w_up", dict(w_gate=d_w_gate4, w_up=d_w_up4))
    dgp = _after(dgp, d_w_up4, *held)
    shard_pairs = [
        (g, pl.BlockSpec((512, tk), functools.partial(lambda s, j, i: (i, s), s)),
         w4, pl.BlockSpec((None, 512, tk), functools.partial(lambda s, j, i: (s, j, 0), s)), NT)
        for g, w4 in ((dgp, w_gate4), (dup, w_up4)) for s in range(N_CHIPS)]
    dn2 = _matmul("ffn_in_bwd", shard_pairs, (D_MODEL // 512, S // 512), jax.ShapeDtypeStruct((S, D_MODEL), BF16),
                  pl.BlockSpec((512, 512), lambda j, i: (i, j)), 1)
    dh1, dh1_b, d_norm2_g = _rms_bwd("rms2_bwd", h1, norm2_g, dn2, dh2)

    d_w_out = _mm_tn("out_proj_wgrad", cat, dh1_b, D_MODEL, 1024, 1024, BF16)
    held = on_grad("w_out", dict(w_out=d_w_out))
    dcat = _mm_nt("out_proj_bwd", _after(dh1_b, d_w_out, *held), w_out, 1024, 1024, BF16)
    do_attn, delta, d_attn_norm_g = _attn_norm_bwd(o_attn, attn_norm_g, dcat)
    dq_a, dkv_a = None, None
    for d in DILATIONS:
        dq_a = _attn_bwd_dq(qk, proj, do_attn, lse, delta, d, dq_a)
        dkv_a = _attn_bwd_dkv(qk, proj, do_attn, lse, delta, d, dkv_a)
    dproj = _attn_grad_merge(dq_a, dkv_a[0], dkv_a[1], tabs)
    held = on_grad("mid", dict(anchor=dproj))
    do_gla, dproj, d_gla_norm_g = _gla_out_bwd(o_gla, proj, gla_norm_g, _after(dcat, *held), dproj)
    dq_f, dk_f, dv_f, dg_f = _gla_bwd(proj, kept_f, do_gla, False)
    dgq, dgk, dgv, dg_b = _gla_bwd(proj, kept_b, do_gla, True, prev=(dq_f, dk_f, dv_f))
    dz, d_wg, d_gate_bias = _gates_bwd(z, wg, gate_bias, dg_f, dg_b)
    dproj = lax.dynamic_update_slice(dproj, jnp.concatenate([dgq, dgk, dgv], axis=1), (0, 3 * ATTN_W))
    d_w_in_t = _mm_tn("in_proj_wgrad", dproj, n1, 768, D_MODEL, 2048, BF16, rows_out=IN_W)
    n_tok = S // 1024
    d_w_in_t = _matmul(
        "in_proj_z_wgrad",
        [(dz, pl.BlockSpec((1024, Z_W), lambda i, j, k: (k, 0)), n1, pl.BlockSpec((1024, D_MODEL), lambda i, j, k: (k, 0)), TN)],
        (1, 1, n_tok), jax.ShapeDtypeStruct((IN_W, D_MODEL), BF16), pl.BlockSpec((Z_W, D_MODEL), lambda i, j, k: (z_block, 0)),
        n_tok, into=d_w_in_t)
    held = on_grad("w_in", dict(w_in_t=d_w_in_t))
    half = S // 2048

    def in_proj_bwd(name, first, a, into):
        return _matmul(
            name,
            [(a, pl.BlockSpec((1024, IN_MAIN), lambda j, i: (i + first, 0)), w_in_t, pl.BlockSpec((IN_MAIN, 512), lambda j, i: (0, j)), NN),
             (dz, pl.BlockSpec((1024, Z_W), lambda j, i: (i + first, 0)), w_in_t, pl.BlockSpec((Z_W, 512), lambda j, i: (z_block, j)), NN)],
            (D_MODEL // 512, half), jax.ShapeDtypeStruct((S, D_MODEL), BF16),
            pl.BlockSpec((1024, 512), lambda j, i: (i + first, j)), 1, into=into)

    dproj = _after(dproj, d_w_in_t, *held)
    dn1 = in_proj_bwd("in_proj_bwd_a", 0, dproj, None)
    held = on_grad("last", dict(last=dn1))
    dn1 = in_proj_bwd("in_proj_bwd_b", half, dproj, _after(dn1, *held))
    grad_x, _, d_norm1_g = _rms_bwd("rms1_bwd", x, norm1_g, dn1, dh1)

    big = dict(w_in_t=d_w_in_t, w_out=d_w_out, w_gate4=d_w_gate4, w_up4=d_w_up4, w_down=d_w_down)
    small = dict(loss=loss_row, norm1_g=d_norm1_g, wg=d_wg, gate_bias=d_gate_bias, gla_norm_g=d_gla_norm_g,
                 attn_norm_g=d_attn_norm_g, norm2_g=d_norm2_g, conv_w=d_conv_w, conv_b=d_conv_b, final_norm_g=d_final_g)
    return grad_x, big, small


def _position():
    return lax.axis_index("x"), lax.axis_index("y"), lax.axis_index("c")


def _other_chips(x, y):
    return [(1 - x, y), (x, 1 - y), (1 - x, 1 - y)]


def _gather_chips_async(name, shards, collective_id):
    n = len(shards)

    def body(*refs):
        ins, outs = refs[:n], refs[n:2 * n]
        send, recv, loc = refs[2 * n:]
        x, y, c = _position()
        me = 2 * x + y
        chips = _other_chips(x, y)
        barrier = pltpu.get_barrier_semaphore()
        for px, py in chips:
            pl.semaphore_signal(barrier, inc=1, device_id=(px, py, c), device_id_type=MESH)
        pl.semaphore_wait(barrier, len(chips))
        started = []
        for w in range(n):
            own = pltpu.make_async_copy(ins[w], outs[w].at[me], loc.at[w])
            own.start()
            started.append(own)
        sends = []
        for w in range(n):
            for j, (px, py) in enumerate(chips):
                cp = pltpu.make_async_remote_copy(ins[w], outs[w].at[me], send.at[3 * w + j], recv.at[3 * w + j],
                                                  device_id=(px, py, c), device_id_type=MESH)
                cp.start()
                sends.append(cp)
        for w in range(n):
            for j, (px, py) in enumerate(chips):
                pltpu.make_async_remote_copy(ins[w], outs[w].at[2 * px + py], send.at[3 * w + j], recv.at[3 * w + j],
                                             device_id=(px, py, c), device_id_type=MESH).wait_recv()
        for cp in sends:
            cp.wait_send()
        for own in started:
            own.wait()

    return pl.kernel(
        body, name=name, mesh=_sequencer(),
        out_type=[jax.ShapeDtypeStruct((N_CHIPS,) + s.shape, s.dtype) for s in shards],
        scratch_types=[pltpu.SemaphoreType.DMA((3 * n,)), pltpu.SemaphoreType.DMA((3 * n,)), pltpu.SemaphoreType.DMA((n,))],
        compiler_params=pltpu.CompilerParams(collective_id=collective_id),
    )(*shards)


def _gather_halves_async(name, small, shard, collective_id):
    half = shard.shape[1] // 2

    def body(small_ref, shard_ref, small_out, out, send, recv, loc):
        x, y, c = _position()
        me = 2 * x + y
        sibling = (x, y, 1 - c)
        chips = _other_chips(x, y)
        barrier = pltpu.get_barrier_semaphore()
        for px, py in chips:
            pl.semaphore_signal(barrier, inc=1, device_id=(px, py, c), device_id_type=MESH)
        pl.semaphore_signal(barrier, inc=1, device_id=sibling, device_id_type=MESH)
        pl.semaphore_wait(barrier, len(chips) + 1)
        mine = pl.ds(pl.multiple_of(c * half, LANES), half)
        theirs = pl.ds(pl.multiple_of((1 - c) * half, LANES), half)
        own = [pltpu.make_async_copy(small_ref, small_out.at[me], loc.at[0]),
               pltpu.make_async_copy(shard_ref, out.at[me], loc.at[1])]
        for cp in own:
            cp.start()
        sends = []
        for j, (px, py) in enumerate(chips):
            sends.append(pltpu.make_async_remote_copy(small_ref, small_out.at[me], send.at[j], recv.at[j],
                                                      device_id=(px, py, c), device_id_type=MESH))
            sends.append(pltpu.make_async_remote_copy(shard_ref.at[:, mine], out.at[me, :, mine], send.at[3 + j], recv.at[3 + j],
                                                      device_id=(px, py, c), device_id_type=MESH))
        for cp in sends:
            cp.start()
        passed = []
        for j, (px, py) in enumerate(chips):
            slot = 2 * px + py
            pltpu.make_async_remote_copy(shard_ref.at[:, mine], out.at[slot, :, mine], send.at[3 + j], recv.at[3 + j],
                                         device_id=(px, py, c), device_id_type=MESH).wait_recv()
            cp = pltpu.make_async_remote_copy(out.at[slot, :, mine], out.at[slot, :, mine], send.at[6 + j], recv.at[6 + j],
                                              device_id=sibling, device_id_type=MESH)
            cp.start()
            passed.append(cp)
        for j, (px, py) in enumerate(chips):
            slot = 2 * px + py
            pltpu.make_async_remote_copy(small_ref, small_out.at[slot], send.at[j], recv.at[j],
                                         device_id=(px, py, c), device_id_type=MESH).wait_recv()
            pltpu.make_async_remote_copy(out.at[slot, :, theirs], out.at[slot, :, theirs], send.at[6 + j], recv.at[6 + j],
                                         device_id=sibling, device_id_type=MESH).wait_recv()
        for cp in sends + passed:
            cp.wait_send()
        for cp in own:
            cp.wait()

    return pl.kernel(
        body, name=name, mesh=_sequencer(),
        out_type=[jax.ShapeDtypeStruct((N_CHIPS,) + small.shape, small.dtype),
                  jax.ShapeDtypeStruct((N_CHIPS,) + shard.shape, shard.dtype)],
        scratch_types=[pltpu.SemaphoreType.DMA((9,)), pltpu.SemaphoreType.DMA((9,)), pltpu.SemaphoreType.DMA((2,))],
        compiler_params=pltpu.CompilerParams(collective_id=collective_id),
    )(small, shard)


def _sequencer():
    return plsc.ScalarSubcoreMesh(axis_name="sequencer", num_cores=1)


def _sibling_exchange_async(name, arrs, collective_id):
    n = len(arrs)

    def body(*refs):
        ins, outs = refs[:n], refs[n:2 * n]
        send, recv = refs[2 * n:]
        x, y, c = _position()
        sibling = (x, y, 1 - c)
        barrier = pltpu.get_barrier_semaphore()
        pl.semaphore_signal(barrier, inc=1, device_id=sibling, device_id_type=MESH)
        pl.semaphore_wait(barrier, 1)
        copies = [pltpu.make_async_remote_copy(ins[w], outs[w], send.at[w], recv.at[w], device_id=sibling,
                                               device_id_type=MESH) for w in range(n)]
        for cp in copies:
            cp.start()
        for cp in copies:
            cp.wait()

    return pl.kernel(
        body, name=name, out_type=[jax.ShapeDtypeStruct(a.shape, a.dtype) for a in arrs],
        scratch_types=[pltpu.SemaphoreType.DMA((n,)), pltpu.SemaphoreType.DMA((n,))],
        compiler_params=pltpu.CompilerParams(collective_id=collective_id), mesh=_sequencer(),
    )(*arrs)


def _scatter_chips_async(name, parts, collective_id):
    n = len(parts)

    def body(*refs):
        ins, outs = refs[:n], refs[n:2 * n]
        send, recv, loc = refs[2 * n:]
        x, y, c = _position()
        me = 2 * x + y
        chips = _other_chips(x, y)
        barrier = pltpu.get_barrier_semaphore()
        for px, py in chips:
            pl.semaphore_signal(barrier, inc=1, device_id=(px, py, c), device_id_type=MESH)
        pl.semaphore_wait(barrier, len(chips))
        started = []
        for w in range(n):
            own = pltpu.make_async_copy(ins[w].at[me], outs[w].at[me], loc.at[w])
            own.start()
            started.append(own)
        sends = []
        for w in range(n):
            for j, (px, py) in enumerate(chips):
                cp = pltpu.make_async_remote_copy(ins[w].at[2 * px + py], outs[w].at[me], send.at[3 * w + j],
                                                  recv.at[3 * w + j], device_id=(px, py, c), device_id_type=MESH)
                cp.start()
                sends.append(cp)
        for w in range(n):
            for j, (px, py) in enumerate(chips):
                pltpu.make_async_remote_copy(ins[w].at[me], outs[w].at[2 * px + py], send.at[3 * w + j], recv.at[3 * w + j],
                                             device_id=(px, py, c), device_id_type=MESH).wait_recv()
        for cp in sends:
            cp.wait_send()
        for own in started:
            own.wait()

    return pl.kernel(
        body, name=name, out_type=[jax.ShapeDtypeStruct(p.shape, p.dtype) for p in parts],
        scratch_types=[pltpu.SemaphoreType.DMA((3 * n,)), pltpu.SemaphoreType.DMA((3 * n,)), pltpu.SemaphoreType.DMA((n,))],
        compiler_params=pltpu.CompilerParams(collective_id=collective_id), mesh=_sequencer(),
    )(*parts)


def _allreduce_rows(buf):
    R = buf.shape[0]

    def body(in_ref, out_ref, land, send, recv):
        x, y, c = _position()
        me = 4 * x + 2 * y + c
        land[pl.ds(me, 1)] = in_ref[...][None]
        peers = []
        for mask in range(1, N_DEV):
            px = 1 - x if mask & 4 else x
            py = 1 - y if mask & 2 else y
            pc = 1 - c if mask & 1 else c
            peers.append((px, py, pc))
        sends = []
        for k, peer in enumerate(peers):
            cp = pltpu.make_async_remote_copy(in_ref, land.at[me], send.at[k], recv.at[k], device_id=peer, device_id_type=MESH)
            cp.start()
            sends.append(cp)
        for k, (px, py, pc) in enumerate(peers):
            pltpu.make_async_remote_copy(in_ref, land.at[4 * px + 2 * py + pc], send.at[k], recv.at[k],
                                         device_id=(px, py, pc), device_id_type=MESH).wait_recv()
        for cp in sends:
            cp.wait_send()
        tot = land[0]
        for i in range(1, N_DEV):
            tot = tot + land[i]
        out_ref[...] = tot

    vm = pl.BlockSpec(memory_space=pltpu.VMEM)
    return pl.pallas_call(
        body, name="allreduce_small", in_specs=[vm], out_specs=vm, out_shape=jax.ShapeDtypeStruct((R, LANES), F32),
        scratch_shapes=[pltpu.VMEM((N_DEV, R, LANES), F32), pltpu.SemaphoreType.DMA((N_DEV - 1,)),
                        pltpu.SemaphoreType.DMA((N_DEV - 1,))],
    )(buf)


def _tile2d(r, c, cap):
    if r <= cap:
        return r, c
    fits = [t for t in range(16, cap + 1, 16) if r % t == 0]
    return (max(fits), c) if fits else (r, 256)


def _pair_sum(name, a, b):
    n, r, c = a.shape
    tr, tc = _tile2d(r, c, 1024)

    def body(a_ref, b_ref, o_ref):
        o_ref[...] = (a_ref[...].astype(F32) + b_ref[...].astype(F32)).astype(BF16)

    blk = pl.BlockSpec((None, tr, tc), lambda s, i, j: (s, i, j))
    return pl.pallas_call(
        body, name=name, grid=(n, r // tr, c // tc), in_specs=[blk, blk], out_specs=blk,
        out_shape=jax.ShapeDtypeStruct(a.shape, BF16),
    )(a, b)


def _adamw_math(w, m, v, g):
    m2 = ADAM_B1 * m + (1.0 - ADAM_B1) * g
    v2 = ADAM_B2 * v + (1.0 - ADAM_B2) * (g * g)
    m_hat = m2 / (1.0 - ADAM_B1 ** ADAM_STEP)
    v_hat = v2 / (1.0 - ADAM_B2 ** ADAM_STEP)
    delta = -ADAM_LR * (m_hat / (jnp.sqrt(v_hat) + ADAM_EPS) + ADAM_WD * w)
    return delta, m2, v2


def _adamw(name, w, m, v, g):
    r, c = w.shape
    stacked = g.ndim == 3
    tr, tc = _tile2d(r, c, 256)

    def body(w_ref, m_ref, v_ref, g_ref, go_ref, d_ref, m2_ref, v2_ref):
        if stacked:
            gv = g_ref[0].astype(F32)
            for i in range(1, N_CHIPS):
                gv = gv + g_ref[i].astype(F32)
        else:
            gv = g_ref[...]
        delta, m2, v2 = _adamw_math(w_ref[...], m_ref[...], v_ref[...], gv)
        go_ref[...] = gv
        d_ref[...] = delta
        m2_ref[...] = m2
        v2_ref[...] = v2

    blk = pl.BlockSpec((tr, tc), lambda i, j: (i, j))
    g_spec = pl.BlockSpec((N_CHIPS, tr, tc), lambda i, j: (0, i, j)) if stacked else blk
    out = jax.ShapeDtypeStruct((r, c), F32)
    return pl.pallas_call(
        body, name=name, grid=(r // tr, c // tc), in_specs=[blk, blk, blk, g_spec], out_specs=[blk] * 4, out_shape=[out] * 4,
    )(w, m, v, g)


def _pack_rows(pieces):
    flat = jnp.concatenate([p.reshape(-1) for p in pieces])
    rows = flat.shape[0] // LANES
    pad = (-rows) % 8
    return jnp.pad(flat.reshape(rows, LANES), ((0, pad), (0, 0)))


def _unpack_rows(buf, shapes):
    flat = buf.reshape(-1)
    out, at = [], 0
    for s in shapes:
        size = math.prod(s)
        out.append(flat[at:at + size].reshape(s))
        at += size
    return out


SMALL_NAMES = ("norm1_g", "gf_up", "gf_b", "gb_up", "gb_b", "gla_norm_g", "attn_norm_g", "norm2_g", "conv_w", "conv_b",
               "final_norm_g")
BIG_NAMES = ("w_in", "w_out", "w_gate", "w_up", "w_down")
WEIGHT_ORDER = ("norm1_g", "w_in", "gf_up", "gf_b", "gb_up", "gb_b", "gla_norm_g", "attn_norm_g", "w_out", "norm2_g",
                "w_gate", "w_up", "conv_w", "conv_b", "w_down", "final_norm_g")


def kernel(x, norm1_g, w_in, gf_up, gf_b, gb_up, gb_b, gla_norm_g, attn_norm_g, w_out, norm2_g, w_gate, w_up, conv_w, conv_b, w_down, final_norm_g, loss_target, m_norm1_g, m_w_in, m_gf_up, m_gf_b, m_gb_up, m_gb_b, m_gla_norm_g, m_attn_norm_g, m_w_out, m_norm2_g, m_w_gate, m_w_up, m_conv_w, m_conv_b, m_w_down, m_final_norm_g, v_norm1_g, v_w_in, v_gf_up, v_gf_b, v_gb_up, v_gb_b, v_gla_norm_g, v_attn_norm_g, v_w_out, v_norm2_g, v_w_gate, v_w_up, v_conv_w, v_conv_b, v_w_down, v_final_norm_g):
    w = dict(norm1_g=norm1_g, w_in=w_in, gf_up=gf_up, gf_b=gf_b, gb_up=gb_up, gb_b=gb_b, gla_norm_g=gla_norm_g,
             attn_norm_g=attn_norm_g, w_out=w_out, norm2_g=norm2_g, w_gate=w_gate, w_up=w_up, conv_w=conv_w, conv_b=conv_b,
             w_down=w_down, final_norm_g=final_norm_g)
    m = dict(norm1_g=m_norm1_g, w_in=m_w_in, gf_up=m_gf_up, gf_b=m_gf_b, gb_up=m_gb_up, gb_b=m_gb_b, gla_norm_g=m_gla_norm_g,
             attn_norm_g=m_attn_norm_g, w_out=m_w_out, norm2_g=m_norm2_g, w_gate=m_w_gate, w_up=m_w_up, conv_w=m_conv_w,
             conv_b=m_conv_b, w_down=m_w_down, final_norm_g=m_final_norm_g)
    v = dict(norm1_g=v_norm1_g, w_in=v_w_in, gf_up=v_gf_up, gf_b=v_gf_b, gb_up=v_gb_up, gb_b=v_gb_b, gla_norm_g=v_gla_norm_g,
             attn_norm_g=v_attn_norm_g, w_out=v_w_out, norm2_g=v_norm2_g, w_gate=v_w_gate, w_up=v_w_up, conv_w=v_conv_w,
             conv_b=v_conv_b, w_down=v_w_down, final_norm_g=v_final_norm_g)
    S = x.shape[1]
    chip = 2 * lax.axis_index("x") + lax.axis_index("y")
    n_in = IN_W // N_CHIPS
    n_ff = D_FF // N_CHIPS
    n_gk = GLA_K // N_CHIPS

    def owned(t):
        return {k: (jnp.transpose(t[k][0]) if k == "w_in" else t[k][0]) for k in BIG_NAMES}

    own_w, own_m, own_v = owned(w), owned(m), owned(v)
    shard = {k: own_w[k].astype(BF16) for k in BIG_NAMES}
    small_shard = _pack_rows([gf_up[0], gb_up[0], conv_w[0]])
    small4, w_in4 = _gather_halves_async("gather_w_in", small_shard, shard["w_in"], 0)
    w_out4, w_gate4, w_up4 = _gather_chips_async("gather_w_mid", [shard["w_out"], shard["w_gate"], shard["w_up"]], 1)
    (w_down4,) = _gather_chips_async("gather_w_down", [shard["w_down"]], 2)
    w_in_t = w_in4.reshape(IN_W, D_MODEL)
    rows_up = GATE_RANK * n_gk // LANES
    rows_cw = 3 * n_ff // LANES
    gf_full = jnp.transpose(small4[:, 0:rows_up].reshape(N_CHIPS, GATE_RANK, n_gk), (1, 0, 2)).reshape(GATE_RANK, GLA_K)
    gb_full = jnp.transpose(small4[:, rows_up:2 * rows_up].reshape(N_CHIPS, GATE_RANK, n_gk), (1, 0, 2)).reshape(GATE_RANK, GLA_K)
    cw_full = jnp.transpose(small4[:, 2 * rows_up:2 * rows_up + rows_cw].reshape(N_CHIPS, 3, n_ff), (1, 0, 2)).reshape(3, D_FF)
    wg = jnp.zeros((Z_W, 2 * GLA_K), F32)
    wg = wg.at[0:GATE_RANK, 0:GLA_K].set(gf_full).at[GATE_RANK:2 * GATE_RANK, GLA_K:].set(gb_full).astype(BF16)
    gate_bias = jnp.concatenate([gf_b, gb_b], axis=1)

    pending, contributions, next_id = [], {}, [3]

    def as_shards(group, arrays):
        if group == "w_in":
            return dict(w_in=arrays["w_in_t"].reshape(N_CHIPS, n_in, D_MODEL))
        if group == "w_out":
            return dict(w_out=arrays["w_out"].reshape(N_CHIPS, D_MODEL // N_CHIPS, D_MODEL))
        if group == "w_down":
            return dict(w_down=arrays["w_down"].reshape(N_CHIPS, n_ff, D_MODEL))
        return arrays

    out = {}

    def swap(group, arrays):
        mine = as_shards(group, arrays)
        pending.append((group, mine, _sibling_exchange_async(f"sibling_{group}", list(mine.values()), next_id[0])))
        next_id[0] += 1

    def sum_and_send(anchor):
        tag, mine, theirs = pending.pop()
        sums = [_pair_sum(f"pair_sum_{k}", mine[k], _after(t, *anchor)) for k, t in zip(mine, theirs)]
        contributions.update(zip(mine, _scatter_chips_async(f"scatter_{tag}", sums, next_id[0])))
        next_id[0] += 1
        return sums

    def update(names, anchor):
        for k in names:
            res = _adamw(f"adamw_{k}", own_w[k], own_m[k], own_v[k], _after(contributions[k], *anchor))
            out[k] = [(jnp.transpose(r) if k == "w_in" else r)[None] for r in res]
        return [out[k][0] for k in names]

    def on_grad(event, arrays):
        anchor = list(arrays.values())
        held = []
        if event in ("w_gate_w_up", "w_out", "mid", "last"):
            held += sum_and_send(anchor)
        if event == "mid":
            held += update(("w_down", "w_gate", "w_up"), anchor)
        if event == "last":
            held += update(("w_out",), anchor)
        if event in ("w_down", "w_gate_w_up", "w_out", "w_in"):
            swap(event, arrays)
        return held

    grad_x, _, small = _local_step(
        x[0], loss_target[0], norm1_g, w_in_t, wg, gate_bias, gla_norm_g, attn_norm_g,
        w_out4.reshape(D_MODEL, D_MODEL), norm2_g, w_gate4, w_up4, cw_full, conv_b, w_down4.reshape(D_FF, D_MODEL), final_norm_g,
        on_grad=on_grad)
    update(("w_in",), [grad_x])

    d_gf_up = small["wg"][0:GATE_RANK, 0:GLA_K]
    d_gb_up = small["wg"][GATE_RANK:2 * GATE_RANK, GLA_K:]
    pieces = [small["loss"], small["norm1_g"], d_gf_up, small["gate_bias"][:, :GLA_K], d_gb_up, small["gate_bias"][:, GLA_K:],
              small["gla_norm_g"], small["attn_norm_g"], small["norm2_g"], small["conv_w"], small["conv_b"], small["final_norm_g"]]
    total = _allreduce_rows(_pack_rows(pieces))
    summed = _unpack_rows(total, [p.shape for p in pieces])
    loss = summed[0][0, 0]
    g_small = dict(zip(SMALL_NAMES, summed[1:]))
    g_small["gf_up"] = lax.dynamic_slice_in_dim(g_small["gf_up"], chip * n_gk, n_gk, axis=1)
    g_small["gb_up"] = lax.dynamic_slice_in_dim(g_small["gb_up"], chip * n_gk, n_gk, axis=1)
    g_small["conv_w"] = lax.dynamic_slice_in_dim(g_small["conv_w"], chip * n_ff, n_ff, axis=1)
    packed = [_pack_rows([t[k] for k in SMALL_NAMES]) for t in (w, m, v, g_small)]
    res = _adamw("adamw_small", *packed)
    shapes = [w[k].shape for k in SMALL_NAMES]
    for k, vals in zip(SMALL_NAMES, zip(*[_unpack_rows(r, shapes) for r in res])):
        out[k] = list(vals)

    grads, deltas, new_m, new_v = ([out[k][i] for k in WEIGHT_ORDER] for i in range(4))
    return (loss, grad_x[None], *grads, *deltas, *new_m, *new_v)
```

```python
import functools
import math

import jax
import jax.numpy as jnp
from jax import lax
from jax.experimental import pallas as pl
from jax.experimental.pallas import tpu as pltpu
from jax.experimental.pallas import tpu_sc as plsc

F32 = jnp.float32
BF16 = jnp.bfloat16

D_MODEL = 2048
ATTN_W = 1024
HEAD = 128
N_HEADS = 8
N_SIDE = 64
DILATIONS = (1, 4, 16)
ROPE_THETA = 500000.0
ROPE_DIM = 32
GLA_K = 512
GLA_V = 1024
GLA_HEADS = 4
GLA_DK = 128
GLA_DV = 256
GATE_RANK = 16
GATE_NORM = 16.0
CHUNK = 64
IN_MAIN = 6144
IN_W = 6176
Z_W = IN_W - IN_MAIN
D_FF = 5632
EPS = 1e-6
N_CHIPS = 4
N_DEV = 8
LANES = 128

ADAM_LR = 0.001
ADAM_B1 = 0.9
ADAM_B2 = 0.999
ADAM_EPS = 1e-08
ADAM_WD = 0.01
ADAM_STEP = 10

NEG = -1e30
MESH = pl.DeviceIdType.MESH
ANY = pl.BlockSpec(memory_space=pl.ANY)

NN = ((1,), (0,))
NT = ((1,), (1,))
TN = ((0,), (0,))


def _dot(a, b, dims=NN):
    return lax.dot_general(a, b, (dims, ((), ())), preferred_element_type=F32)


def _sigmoid(x):
    return 0.5 * jnp.tanh(0.5 * x) + 0.5


def _silu(x):
    h = 0.5 * x
    return h * jnp.tanh(h) + h


def _after(x, *deps):
    return lax.optimization_barrier((x,) + deps)[0]


def _matmul(name, pairs, grid, out_shape, out_spec, nk, res=None, into=None, first=None):
    n_in = 2 * len(pairs) + (res is not None)
    dims = [p[4] for p in pairs]

    n_ops = n_in + (into is not None) + 2 * (first is not None)

    def body(*refs):
        ins, o_ref = refs[:n_in], refs[n_ops]

        def partial_sum():
            tot = None
            for p, dn in enumerate(dims):
                a, b = ins[2 * p][...], ins[2 * p + 1][...]
                t = _dot(a.astype(BF16), b.astype(BF16), dn)
                tot = t if tot is None else tot + t
            return tot

        if nk == 1:
            t = partial_sum()
            if res is not None:
                t = t + ins[-1][...]
            o_ref[...] = t.astype(o_ref.dtype)
        else:
            acc_ref = refs[n_ops + 1]
            k = pl.program_id(2)

            @pl.when(k == 0)
            def _():
                if first is not None:
                    start = _dot(refs[n_in][...].astype(BF16), refs[n_in + 1][...].astype(BF16), first[4])
                    acc_ref[...] = start + ins[-1][...] if res is not None else start
                elif res is not None:
                    acc_ref[...] = ins[-1][...]
                else:
                    acc_ref[...] = jnp.zeros_like(acc_ref)

            acc_ref[...] += partial_sum()

            @pl.when(k == nk - 1)
            def _():
                o_ref[...] = acc_ref[...].astype(o_ref.dtype)

    operands, in_specs = [], []
    for a, a_spec, b, b_spec, _ in pairs:
        operands += [a, b]
        in_specs += [a_spec, b_spec]
    if res is not None:
        operands.append(res[0])
        in_specs.append(res[1])
    if first is not None:
        assert nk > 1
        operands += [first[0], first[2]]
        in_specs += [first[1], first[3]]
    acc_shape = tuple(s for s in out_spec.block_shape if s is not None)
    scratch = [pltpu.VMEM(acc_shape, F32)] if nk > 1 else []
    aliases = {}
    if into is not None:
        aliases = {len(operands): 0}
        operands.append(into)
        in_specs.append(ANY)
    return pl.pallas_call(
        body, name=name, grid=grid, in_specs=in_specs, out_specs=out_spec, out_shape=out_shape, scratch_shapes=scratch,
        input_output_aliases=aliases,
    )(*operands)


def _mm_nn(name, a, b, tm, tn, out_dtype, res=None):
    M, K = a.shape
    N = b.shape[1]
    pairs = [(a, pl.BlockSpec((tm, K), lambda j, i: (i, 0)), b, pl.BlockSpec((K, tn), lambda j, i: (0, j)), NN)]
    r = None if res is None else (res, pl.BlockSpec((tm, tn), lambda j, i: (i, j)))
    return _matmul(name, pairs, (N // tn, M // tm), jax.ShapeDtypeStruct((M, N), out_dtype),
                   pl.BlockSpec((tm, tn), lambda j, i: (i, j)), 1, r)


def _mm_nn_sharded(name, a, b4, tm, out_dtype):
    M, K = a.shape
    n = b4.shape[2]
    pairs = [(a, pl.BlockSpec((tm, K), lambda j, i: (i, 0)), b4, pl.BlockSpec((None, K, n), lambda j, i: (j, 0, 0)), NN)]
    return _matmul(name, pairs, (N_CHIPS, M // tm), jax.ShapeDtypeStruct((M, N_CHIPS * n), out_dtype),
                   pl.BlockSpec((tm, n), lambda j, i: (i, j)), 1)


def _mm_nt(name, a, b, tm, tn, out_dtype, res=None, n_out=None):
    M, K = a.shape
    N = b.shape[0] if n_out is None else n_out
    pairs = [(a, pl.BlockSpec((tm, K), lambda j, i: (i, 0)), b, pl.BlockSpec((tn, K), lambda j, i: (j, 0)), NT)]
    r = None if res is None else (res, pl.BlockSpec((tm, tn), lambda j, i: (i, j)))
    return _matmul(name, pairs, (N // tn, M // tm), jax.ShapeDtypeStruct((M, N), out_dtype),
                   pl.BlockSpec((tm, tn), lambda j, i: (i, j)), 1, r)


def _mm_tn(name, a, g, tka, tn, tmm, out_dtype, out3=None, rows_out=None):
    M, Ka = a.shape
    N = g.shape[1]
    pairs = [(a, pl.BlockSpec((tmm, tka), lambda i, j, k: (k, i)), g, pl.BlockSpec((tmm, tn), lambda i, j, k: (k, j)), TN)]
    if out3 is None:
        shape, spec = (Ka if rows_out is None else rows_out, N), pl.BlockSpec((tka, tn), lambda i, j, k: (i, j))
    else:
        shape, spec = (N // out3, Ka, out3), pl.BlockSpec((None, tka, tn), lambda i, j, k: (j, i, 0))
    return _matmul(name, pairs, (Ka // tka, N // tn, M // tmm), jax.ShapeDtypeStruct(shape, out_dtype), spec, M // tmm)


def _rms_fwd(name, x, g, tm=512):
    S, D = x.shape

    def body(x_ref, g_ref, o_ref):
        xv = x_ref[...]
        r = lax.rsqrt(jnp.mean(xv * xv, axis=-1, keepdims=True) + EPS)
        o_ref[...] = (xv * r * g_ref[...]).astype(o_ref.dtype)

    return pl.pallas_call(
        body, name=name, grid=(S // tm,),
        in_specs=[pl.BlockSpec((tm, D), lambda i: (i, 0)), pl.BlockSpec((1, D), lambda i: (0, 0))],
        out_specs=pl.BlockSpec((tm, D), lambda i: (i, 0)), out_shape=jax.ShapeDtypeStruct((S, D), BF16),
    )(x, g)


def _rms_bwd(name, x, g, dn, dres, tm=512):
    S, D = x.shape

    def body(x_ref, g_ref, dn_ref, dres_ref, dx_ref, dxb_ref, dg_ref):
        i = pl.program_id(0)

        @pl.when(i == 0)
        def _():
            dg_ref[...] = jnp.zeros_like(dg_ref)

        xv = x_ref[...]
        r = lax.rsqrt(jnp.mean(xv * xv, axis=-1, keepdims=True) + EPS)
        xhat = xv * r
        dnv = dn_ref[...].astype(F32)
        dg_ref[...] += jnp.sum(dnv * xhat, axis=0, keepdims=True)
        t = dnv * g_ref[...]
        dx = r * (t - xhat * jnp.mean(t * xhat, axis=-1, keepdims=True)) + dres_ref[...]
        dx_ref[...] = dx
        dxb_ref[...] = dx.astype(BF16)

    row = pl.BlockSpec((tm, D), lambda i: (i, 0))
    vec = pl.BlockSpec((1, D), lambda i: (0, 0))
    return pl.pallas_call(
        body, name=name, grid=(S // tm,), in_specs=[row, vec, row, row], out_specs=[row, row, vec],
        out_shape=[jax.ShapeDtypeStruct((S, D), F32), jax.ShapeDtypeStruct((S, D), BF16), jax.ShapeDtypeStruct((1, D), F32)],
    )(x, g, dn, dres)


def _final_loss(h2, g, target, tm=512):
    S, D = h2.shape

    def body(x_ref, g_ref, t_ref, loss_ref, dg_ref, dx_ref, dxb_ref):
        i = pl.program_id(0)

        @pl.when(i == 0)
        def _():
            loss_ref[...] = jnp.zeros_like(loss_ref)
            dg_ref[...] = jnp.zeros_like(dg_ref)

        xv = x_ref[...]
        r = lax.rsqrt(jnp.mean(xv * xv, axis=-1, keepdims=True) + EPS)
        xhat = xv * r
        gv = g_ref[...]
        diff = xhat * gv - t_ref[...]
        per_tok = jnp.mean(diff * diff, axis=-1, keepdims=True)
        loss_ref[...] += 0.5 * jnp.sum(per_tok, axis=0, keepdims=True)
        dy = diff * (1.0 / D)
        dg_ref[...] += jnp.sum(dy * xhat, axis=0, keepdims=True)
        t = dy * gv
        dx = r * (t - xhat * jnp.mean(t * xhat, axis=-1, keepdims=True))
        dx_ref[...] = dx
        dxb_ref[...] = dx.astype(BF16)

    row = pl.BlockSpec((tm, D), lambda i: (i, 0))
    vec = pl.BlockSpec((1, D), lambda i: (0, 0))
    return pl.pallas_call(
        body, name="final_loss", grid=(S // tm,), in_specs=[row, vec, row],
        out_specs=[pl.BlockSpec((1, LANES), lambda i: (0, 0)), vec, row, row],
        out_shape=[jax.ShapeDtypeStruct((1, LANES), F32), jax.ShapeDtypeStruct((1, D), F32),
                   jax.ShapeDtypeStruct((S, D), F32), jax.ShapeDtypeStruct((S, D), BF16)],
    )(h2, g, target)


def _rope_tables(S):
    pos = jnp.arange(S, dtype=F32)
    inv_freq = ROPE_THETA ** (-jnp.arange(0, ROPE_DIM, 2, dtype=F32) / ROPE_DIM)
    ang = pos[:, None] * inv_freq[None, :]
    cos, sin = jnp.cos(ang), jnp.sin(ang)
    half = ROPE_DIM // 2
    rest = HEAD - ROPE_DIM
    z_h, z_r = jnp.zeros((S, half), F32), jnp.zeros((S, rest), F32)
    tab_c = jnp.concatenate([cos, cos, jnp.ones((S, rest), F32)], axis=1)
    tab_up = jnp.concatenate([z_h, sin, z_r], axis=1)
    tab_dn = jnp.concatenate([-sin, z_h, z_r], axis=1)
    return tab_c, tab_up, tab_dn


def _rope_head(t, c, up, dn):
    half = ROPE_DIM // 2
    return t * c + pltpu.roll(t, half, axis=1) * up + pltpu.roll(t, HEAD - half, axis=1) * dn


def _rope_fwd(proj, tabs, tm=512):
    S = proj.shape[0]
    W = 2 * ATTN_W

    def body(p_ref, c_ref, up_ref, dn_ref, o_ref):
        c, up, dn = c_ref[...], up_ref[...], dn_ref[...]
        for h in range(W // HEAD):
            sl = slice(h * HEAD, (h + 1) * HEAD)
            o_ref[:, sl] = _rope_head(p_ref[:, sl].astype(F32), c, up, dn).astype(BF16)

    tab = pl.BlockSpec((tm, HEAD), lambda i: (i, 0))
    return pl.pallas_call(
        body, name="rope_fwd", grid=(S // tm,), in_specs=[pl.BlockSpec((tm, W), lambda i: (i, 0)), tab, tab, tab],
        out_specs=pl.BlockSpec((tm, W), lambda i: (i, 0)), out_shape=jax.ShapeDtypeStruct((S, W), BF16),
    )(proj, *tabs)


def _attn_grad_merge(dq, dk, dv, tabs, tm=256):
    S = dq.shape[0]

    def body(q_ref, k_ref, v_ref, c_ref, up_ref, dn_ref, o_ref):
        c, up, dn = c_ref[...], up_ref[...], dn_ref[...]
        for h in range(N_HEADS):
            sl = slice(h * HEAD, (h + 1) * HEAD)
            for part, r in ((0, q_ref), (1, k_ref)):
                osl = slice(part * ATTN_W + h * HEAD, part * ATTN_W + (h + 1) * HEAD)
                o_ref[:, osl] = _rope_head(r[:, sl].astype(F32), c, -up, -dn).astype(BF16)
        o_ref[:, 2 * ATTN_W:] = v_ref[...]

    blk = pl.BlockSpec((tm, ATTN_W), lambda i: (i, 0))
    tab = pl.BlockSpec((tm, HEAD), lambda i: (i, 0))
    return pl.pallas_call(
        body, name="attn_grad_merge", grid=(S // tm,), in_specs=[blk] * 3 + [tab] * 3,
        out_specs=pl.BlockSpec((tm, 3 * ATTN_W), lambda i: (i, 0)), out_shape=jax.ShapeDtypeStruct((S, IN_MAIN), BF16),
    )(dq, dk, dv, *tabs)


SUB = 128
WIN = SUB + 2 * N_SIDE
Q_COL, K_COL, V_COL = 0, ATTN_W // HEAD, 2 * ATTN_W // HEAD


class _AttnGeo:
    def __init__(self, S, d):
        self.S, self.d, self.L = S, d, S // d
        self.halo = N_SIDE * d
        self.TB = min(2048, S)
        self.W = self.TB + 2 * self.halo
        self.n_sub = self.TB // SUB
        self.grid = (S // self.TB, N_HEADS)
        self.dt = F32 if d > 1 else BF16
        self.su = min(d, 4)
        self.sb = d // self.su
        assert self.TB % (SUB * d) == 0 and self.TB % self.halo == 0

    def specs(self, width, col0, per_head=True):
        ratio = self.TB // self.halo
        last = self.S // self.halo - 1
        col = (lambda h: col0 + h) if per_head else (lambda h: col0)
        cur = pl.BlockSpec((self.TB, width), lambda i, h: (i, col(h)))
        prev = pl.BlockSpec((self.halo, width), lambda i, h: (jnp.maximum(i * ratio - 1, 0), col(h)))
        nxt = pl.BlockSpec((self.halo, width), lambda i, h: (jnp.minimum((i + 1) * ratio, last), col(h)))
        return cur, prev, nxt

    def scratch(self, rows, dtype=None):
        nat = pltpu.VMEM((rows, LANES), self.dt if dtype is None else dtype)
        return [nat] if self.sb == 1 else [nat, pltpu.VMEM((rows, LANES), F32)]

    def bind(self, refs):
        nat = next(refs)
        return (nat, nat) if self.sb == 1 else (nat, next(refs))

    def spread(self, pair):
        nat, streams = pair
        if self.sb > 1:
            n = nat.shape[0] // self.sb
            for a in range(self.sb):
                streams[a * n:(a + 1) * n, :] = nat[pl.ds(a, n, stride=self.sb), :]
        return streams

    def gather(self, pair):
        nat, streams = pair
        if self.sb > 1:
            n = nat.shape[0] // self.sb
            for a in range(self.sb):
                nat[pl.ds(a, n, stride=self.sb), :] = streams[a * n:(a + 1) * n, :]
        return nat

    def rows(self, sub, n, total):
        res, blk = sub % self.d, sub // self.d
        a, b = res % self.sb, res // self.sb
        start = a * (total // self.sb) + b + self.su * SUB * blk
        return pl.ds(start, n, stride=self.su) if self.su > 1 else pl.ds(start, n)

    def band(self):
        row = lax.broadcasted_iota(jnp.int32, (SUB, WIN), 0)
        col = lax.broadcasted_iota(jnp.int32, (SUB, WIN), 1)
        return (col >= row) & (col <= row + 2 * N_SIDE), col

    def mask(self, sub, band):
        inside, col = band
        blk, n_blk = sub // self.d, self.TB // (SUB * self.d)
        base = pl.program_id(0) * (self.TB // self.d) + SUB * blk
        if blk == 0:
            inside = inside & (col >= N_SIDE - base)
        if blk == n_blk - 1:
            inside = inside & (col < self.L + N_SIDE - base)
        return inside

    def fill(self, dst, c_ref):
        dst[...] = c_ref[...].astype(dst.dtype)

    def fill_window(self, dst, p_ref, c_ref, n_ref):
        dst[0:self.halo] = p_ref[...].astype(dst.dtype)
        dst[self.halo:self.halo + self.TB] = c_ref[...].astype(dst.dtype)
        dst[self.halo + self.TB:] = n_ref[...].astype(dst.dtype)


def _lane_of(tile, h):
    lane = lax.broadcasted_iota(jnp.int32, tile.shape, 1)
    return jnp.sum(jnp.where(lane == h, tile, 0.0), axis=1, keepdims=True)


def _attn_fwd(qk, proj, d):
    S = qk.shape[0]
    geo = _AttnGeo(S, d)
    scale = HEAD ** -0.5

    def body(q_ref, kp, kc, kn, vp, vc, vn, o_ref, lse_ref, *scratch):
        h = pl.program_id(1)
        refs = iter(scratch)
        q_p, k_p, v_p, o_p, l_p = (geo.bind(refs) for _ in range(5))
        geo.fill(q_p[0], q_ref)
        geo.fill_window(k_p[0], kp, kc, kn)
        geo.fill_window(v_p[0], vp, vc, vn)
        qs, ks, vs = geo.spread(q_p), geo.spread(k_p), geo.spread(v_p)
        os, ls = o_p[1], l_p[1]
        band = geo.band()
        for sub in range(geo.n_sub):
            rq, rw = geo.rows(sub, SUB, geo.TB), geo.rows(sub, WIN, geo.W)
            q_r, k_r, v_r = qs[rq, :].astype(BF16), ks[rw, :].astype(BF16), vs[rw, :].astype(BF16)
            s = jnp.where(geo.mask(sub, band), _dot(q_r, k_r, NT) * scale, NEG)
            m = jnp.max(s, axis=1, keepdims=True)
            p = jnp.exp(s - m)
            l = jnp.sum(p, axis=1, keepdims=True)
            os[rq, :] = _dot(p.astype(BF16), v_r) / l
            ls[rq, :] = jnp.broadcast_to(m + jnp.log(l), (SUB, LANES))
        o_ref[...] = geo.gather(o_p)[...].astype(BF16)

        @pl.when(h == 0)
        def _():
            lse_ref[...] = jnp.zeros_like(lse_ref)

        lane = lax.broadcasted_iota(jnp.int32, (geo.TB, LANES), 1)
        lse_ref[...] = jnp.where(lane == h, geo.gather(l_p)[...], lse_ref[...])

    q_cur, _, _ = geo.specs(HEAD, Q_COL)
    k_specs = geo.specs(HEAD, K_COL)
    v_specs = geo.specs(HEAD, V_COL)
    stat = pl.BlockSpec((geo.TB, LANES), lambda i, h: (i, 0))
    return pl.pallas_call(
        body, name=f"attn_fwd_d{d}", grid=geo.grid,
        in_specs=[q_cur, k_specs[1], k_specs[0], k_specs[2], v_specs[1], v_specs[0], v_specs[2]],
        out_specs=[q_cur, stat],
        out_shape=[jax.ShapeDtypeStruct((S, ATTN_W), BF16), jax.ShapeDtypeStruct((S, LANES), F32)],
        scratch_shapes=(geo.scratch(geo.TB) + geo.scratch(geo.W) + geo.scratch(geo.W) + geo.scratch(geo.TB, F32)
                        + geo.scratch(geo.TB, F32)),
    )(qk, qk, qk, qk, proj, proj, proj)


def _attn_combine(outs, lses, g, tm=256):
    S = outs[0].shape[0]

    def body(o1, o2, o3, l1, l2, l3, g_ref, ao_ref, o_ref, lse_ref):
        a1, a2, a3 = l1[...], l2[...], l3[...]
        mx = jnp.maximum(jnp.maximum(a1, a2), a3)
        e1, e2, e3 = jnp.exp(a1 - mx), jnp.exp(a2 - mx), jnp.exp(a3 - mx)
        den = e1 + e2 + e3
        lse_ref[...] = mx + jnp.log(den)
        head_of_col = lax.broadcasted_iota(jnp.int32, (LANES, ATTN_W), 1) // HEAD
        spread = (lax.broadcasted_iota(jnp.int32, (LANES, ATTN_W), 0) == head_of_col).astype(BF16)

        def wide(e):
            wgt = e / den
            hi = wgt.astype(BF16)
            lo = (wgt - hi.astype(F32)).astype(BF16)
            return _dot(hi, spread) + _dot(lo, spread)

        ov = wide(e1) * o1[...].astype(F32) + wide(e2) * o2[...].astype(F32) + wide(e3) * o3[...].astype(F32)
        o_ref[...] = ov
        r = lax.rsqrt(jnp.mean(ov * ov, axis=-1, keepdims=True) + EPS)
        ao_ref[...] = (ov * r * g_ref[...]).astype(BF16)

    blk = pl.BlockSpec((tm, ATTN_W), lambda i: (i, 0))
    ls = pl.BlockSpec((tm, LANES), lambda i: (i, 0))
    return pl.pallas_call(
        body, name="attn_combine", grid=(S // tm,),
        in_specs=[blk, blk, blk, ls, ls, ls, pl.BlockSpec((1, ATTN_W), lambda i: (0, 0))], out_specs=[blk, blk, ls],
        out_shape=[jax.ShapeDtypeStruct((S, D_MODEL), BF16), jax.ShapeDtypeStruct((S, ATTN_W), F32),
                   jax.ShapeDtypeStruct((S, LANES), F32)],
    )(*outs, *lses, g)


def _attn_norm_bwd(o, g, dao, tm=256):
    S = o.shape[0]

    def body(o_ref, g_ref, dao_ref, do_ref, dl_ref, dg_ref):
        i = pl.program_id(0)

        @pl.when(i == 0)
        def _():
            dg_ref[...] = jnp.zeros_like(dg_ref)

        ov = o_ref[...]
        r = lax.rsqrt(jnp.mean(ov * ov, axis=-1, keepdims=True) + EPS)
        ohat = ov * r
        dn = dao_ref[...].astype(F32)
        dg_ref[...] += jnp.sum(dn * ohat, axis=0, keepdims=True)
        t = dn * g_ref[...]
        do = r * (t - ohat * jnp.mean(t * ohat, axis=-1, keepdims=True))
        do_ref[...] = do.astype(BF16)
        prod = do * ov
        lane = lax.broadcasted_iota(jnp.int32, (tm, LANES), 1)
        tile = jnp.zeros((tm, LANES), F32)
        for h in range(N_HEADS):
            tile = jnp.where(lane == h, jnp.sum(prod[:, h * HEAD:(h + 1) * HEAD], axis=1, keepdims=True), tile)
        dl_ref[...] = tile

    blk = pl.BlockSpec((tm, ATTN_W), lambda i: (i, 0))
    vec = pl.BlockSpec((1, ATTN_W), lambda i: (0, 0))
    return pl.pallas_call(
        body, name="attn_norm_bwd", grid=(S // tm,),
        in_specs=[blk, vec, pl.BlockSpec((tm, ATTN_W), lambda i: (i, 0))],
        out_specs=[blk, pl.BlockSpec((tm, LANES), lambda i: (i, 0)), vec],
        out_shape=[jax.ShapeDtypeStruct((S, ATTN_W), BF16), jax.ShapeDtypeStruct((S, LANES), F32),
                   jax.ShapeDtypeStruct((1, ATTN_W), F32)],
    )(o, g, dao)


def _attn_bwd_dq(qk, proj, do, lse, delta, d, prev):
    S = qk.shape[0]
    geo = _AttnGeo(S, d)
    scale = HEAD ** -0.5
    before = [] if prev is None else [prev]

    def body(q_ref, kp, kc, kn, vp, vc, vn, do_ref, lse_ref, dl_ref, *rest):
        prev_refs, (dq_ref, *scratch) = rest[:len(before)], rest[len(before):]
        h = pl.program_id(1)
        refs = iter(scratch)
        q_p, k_p, v_p, do_p, lse_p, dl_p, dq_p = (geo.bind(refs) for _ in range(7))
        geo.fill(q_p[0], q_ref)
        geo.fill(do_p[0], do_ref)
        geo.fill(lse_p[0], lse_ref)
        geo.fill(dl_p[0], dl_ref)
        geo.fill_window(k_p[0], kp, kc, kn)
        geo.fill_window(v_p[0], vp, vc, vn)
        qs, ks, vs, dos = geo.spread(q_p), geo.spread(k_p), geo.spread(v_p), geo.spread(do_p)
        lses, dls = geo.spread(lse_p), geo.spread(dl_p)
        dqs = dq_p[1]
        band = geo.band()
        for sub in range(geo.n_sub):
            rq, rw = geo.rows(sub, SUB, geo.TB), geo.rows(sub, WIN, geo.W)
            q_r, k_r, v_r = qs[rq, :].astype(BF16), ks[rw, :].astype(BF16), vs[rw, :].astype(BF16)
            lse_c, dl_c = _lane_of(lses[rq, :], h), _lane_of(dls[rq, :], h)
            s = _dot(q_r, k_r, NT) * scale
            p = jnp.where(geo.mask(sub, band), jnp.exp(s - lse_c), 0.0)
            dp = _dot(dos[rq, :].astype(BF16), v_r, NT)
            ds = (p * (dp - dl_c) * scale).astype(BF16)
            dqs[rq, :] = _dot(ds, k_r)
        total = geo.gather(dq_p)[...]
        for p_ref in prev_refs:
            total = total + p_ref[...].astype(F32)
        dq_ref[...] = total.astype(BF16)

    cur, _, _ = geo.specs(HEAD, 0)
    k_specs = geo.specs(HEAD, K_COL)
    v_specs = geo.specs(HEAD, V_COL)
    stat = pl.BlockSpec((geo.TB, LANES), lambda i, h: (i, 0))
    return pl.pallas_call(
        body, name=f"attn_bwd_dq_d{d}", grid=geo.grid,
        in_specs=[cur, k_specs[1], k_specs[0], k_specs[2], v_specs[1], v_specs[0], v_specs[2], cur, stat, stat]
        + [cur] * len(before),
        out_specs=cur, out_shape=jax.ShapeDtypeStruct((S, ATTN_W), BF16),
        scratch_shapes=(geo.scratch(geo.TB) + geo.scratch(geo.W) + geo.scratch(geo.W) + geo.scratch(geo.TB)
                        + geo.scratch(geo.TB, F32) + geo.scratch(geo.TB, F32) + geo.scratch(geo.TB, F32)),
    )(qk, qk, qk, qk, proj, proj, proj, do, lse, delta, *before)


def _attn_bwd_dkv(qk, proj, do, lse, delta, d, prev):
    S = qk.shape[0]
    geo = _AttnGeo(S, d)
    scale = HEAD ** -0.5
    before = [] if prev is None else list(prev)

    def body(k_ref, v_ref, qp, qc, qn, dop, doc, don, lp, lc, ln, dlp, dlc, dln, *rest):
        prev_refs, (dk_ref, dv_ref, *scratch) = rest[:len(before)], rest[len(before):]
        h = pl.program_id(1)
        refs = iter(scratch)
        k_p, v_p, q_p, do_p, lw_p, dlw_p, dk_p, dv_p = (geo.bind(refs) for _ in range(8))
        geo.fill(k_p[0], k_ref)
        geo.fill(v_p[0], v_ref)
        geo.fill_window(q_p[0], qp, qc, qn)
        geo.fill_window(do_p[0], dop, doc, don)
        geo.fill_window(lw_p[0], lp, lc, ln)
        geo.fill_window(dlw_p[0], dlp, dlc, dln)
        ks, vs, qs, dos = geo.spread(k_p), geo.spread(v_p), geo.spread(q_p), geo.spread(do_p)
        lws, dlws = geo.spread(lw_p), geo.spread(dlw_p)
        dks, dvs = dk_p[1], dv_p[1]
        head = lax.broadcasted_iota(jnp.int32, (LANES, WIN), 0)
        band = geo.band()
        for sub in range(geo.n_sub):
            rq, rw = geo.rows(sub, SUB, geo.TB), geo.rows(sub, WIN, geo.W)
            k_r, v_r = ks[rq, :].astype(BF16), vs[rq, :].astype(BF16)
            q_w, do_w = qs[rw, :].astype(BF16), dos[rw, :].astype(BF16)
            lse_row = jnp.sum(jnp.where(head == h, lws[rw, :].T, 0.0), axis=0, keepdims=True)
            dl_row = jnp.sum(jnp.where(head == h, dlws[rw, :].T, 0.0), axis=0, keepdims=True)
            st = _dot(k_r, q_w, NT) * scale
            pt = jnp.where(geo.mask(sub, band), jnp.exp(st - lse_row), 0.0)
            dvs[rq, :] = _dot(pt.astype(BF16), do_w)
            dpt = _dot(v_r, do_w, NT)
            dst = (pt * (dpt - dl_row) * scale).astype(BF16)
            dks[rq, :] = _dot(dst, q_w)
        dk_tot, dv_tot = geo.gather(dk_p)[...], geo.gather(dv_p)[...]
        if prev_refs:
            dk_tot, dv_tot = dk_tot + prev_refs[0][...].astype(F32), dv_tot + prev_refs[1][...].astype(F32)
        dk_ref[...] = dk_tot.astype(BF16)
        dv_ref[...] = dv_tot.astype(BF16)

    q_specs = geo.specs(HEAD, Q_COL)
    k_cur, _, _ = geo.specs(HEAD, K_COL)
    v_cur, _, _ = geo.specs(HEAD, V_COL)
    do_specs = geo.specs(HEAD, 0)
    st_specs = geo.specs(LANES, 0, per_head=False)
    cur = do_specs[0]
    return pl.pallas_call(
        body, name=f"attn_bwd_dkv_d{d}", grid=geo.grid,
        in_specs=[k_cur, v_cur, q_specs[1], q_specs[0], q_specs[2], do_specs[1], do_specs[0], do_specs[2],
                  st_specs[1], st_specs[0], st_specs[2], st_specs[1], st_specs[0], st_specs[2]] + [cur] * len(before),
        out_specs=[cur, cur],
        out_shape=[jax.ShapeDtypeStruct((S, ATTN_W), BF16), jax.ShapeDtypeStruct((S, ATTN_W), BF16)],
        scratch_shapes=(geo.scratch(geo.TB) + geo.scratch(geo.TB) + geo.scratch(geo.W) + geo.scratch(geo.W)
                        + geo.scratch(geo.W, F32) + geo.scratch(geo.W, F32) + geo.scratch(geo.TB, F32)
                        + geo.scratch(geo.TB, F32)),
    )(qk, proj, qk, qk, qk, do, do, do, lse, lse, lse, delta, delta, delta, *before)


def _cumsum_rows(x, reverse):
    n = x.shape[0]
    row = lax.broadcasted_iota(jnp.int32, x.shape, 0)
    s = 1
    while s < n:
        if reverse:
            x = x + jnp.where(row < n - s, pltpu.roll(x, n - s, axis=0), 0.0)
        else:
            x = x + jnp.where(row >= s, pltpu.roll(x, s, axis=0), 0.0)
        s *= 2
    return x


GLA_GROUP = 8


def _gla_rows(cc):
    return slice(cc * CHUNK, (cc + 1) * CHUNK)


def _gla_chunk_terms(q_ref, k_ref, v_ref, g_ref, h, reverse, rows, b_ref=None):
    ksl = slice(h * GLA_DK, (h + 1) * GLA_DK)
    q = q_ref[rows, ksl].astype(F32) * (GLA_DK ** -0.5)
    k = k_ref[rows, ksl].astype(F32)
    v = v_ref[rows, h * GLA_DV:(h + 1) * GLA_DV]
    b = _cumsum_rows(g_ref[rows, ksl], reverse) if b_ref is None else b_ref[rows, ksl]
    r_ref = CHUNK // 2 if reverse else CHUNK // 2 - 1
    r_last = 0 if reverse else CHUNK - 1
    b_ref, b_last = b[r_ref:r_ref + 1, :], b[r_last:r_last + 1, :]
    ii = lax.broadcasted_iota(jnp.int32, (CHUNK, CHUNK), 0)
    jj = lax.broadcasted_iota(jnp.int32, (CHUNK, CHUNK), 1)
    causal = (jj >= ii) if reverse else (jj <= ii)
    e_q, e_k = jnp.exp(b - b_ref), jnp.exp(b_ref - b)
    e_in, e_st = jnp.exp(b), jnp.exp(b_last - b)
    return dict(q=q, k=k, v=v, b=b, causal=causal, e_q=e_q, e_k=e_k, e_in=e_in, e_st=e_st, dec=jnp.exp(b_last),
                qe=q * e_q, ke=k * e_k, q_in=q * e_in, k_st=k * e_st, r_ref=r_ref, r_last=r_last)


def _gla_specs(order):
    rows = GLA_GROUP * CHUNK
    q = pl.BlockSpec((rows, GLA_K), lambda c: (order(c), 3 * ATTN_W // GLA_K))
    k = pl.BlockSpec((rows, GLA_K), lambda c: (order(c), 3 * ATTN_W // GLA_K + 1))
    v = pl.BlockSpec((rows, GLA_V), lambda c: (order(c), (3 * ATTN_W + 2 * GLA_K) // GLA_V))
    return q, k, v


def _gla_fwd(proj, gates, reverse, o_prev=None):
    S = proj.shape[0]
    n = S // CHUNK
    nb = n // GLA_GROUP
    rows = GLA_GROUP * CHUNK
    order = (lambda c: nb - 1 - c) if reverse else (lambda c: c)
    seq = list(range(GLA_GROUP))[::-1] if reverse else list(range(GLA_GROUP))
    gcol = 1 if reverse else 0

    def body(*refs):
        if o_prev is None:
            q_ref, k_ref, v_ref, g_ref, o_ref, st_ref, a_ref, b_ref, state = refs
        else:
            q_ref, k_ref, v_ref, g_ref, op_ref, o_ref, st_ref, a_ref, b_ref, state = refs
        c = pl.program_id(0)

        @pl.when(c == 0)
        def _():
            state[...] = jnp.zeros_like(state)

        for h in range(GLA_HEADS):
            vsl = slice(h * GLA_DV, (h + 1) * GLA_DV)
            st = state[h]
            for cc in seq:
                rs = _gla_rows(cc)
                t = _gla_chunk_terms(q_ref, k_ref, v_ref, g_ref, h, reverse, rs)
                b_ref[rs, h * GLA_DK:(h + 1) * GLA_DK] = t["b"]
                a = jnp.where(t["causal"], _dot(t["qe"].astype(BF16), t["ke"].astype(BF16), NT), 0.0).astype(BF16)
                a_ref[cc, h] = a
                o = _dot(a, t["v"])
                st_b = st.astype(BF16)
                st_ref[cc, h] = st_b
                o = o + _dot(t["q_in"].astype(BF16), st_b, NT)
                st = st * t["dec"] + _dot(t["v"], t["k_st"].astype(BF16), TN)
                if o_prev is not None:
                    o = o + op_ref[rs, vsl]
                o_ref[rs, vsl] = o
            state[h] = st

    q_spec, k_spec, v_spec = _gla_specs(order)
    o_spec = pl.BlockSpec((rows, GLA_V), lambda c: (order(c), 0))
    in_specs = [q_spec, k_spec, v_spec, pl.BlockSpec((rows, GLA_K), lambda c: (order(c), gcol))]
    operands = [proj, proj, proj, gates]
    if o_prev is not None:
        in_specs.append(o_spec)
        operands.append(o_prev)
    return pl.pallas_call(
        body, name="gla_fwd_rev" if reverse else "gla_fwd", grid=(nb,), in_specs=in_specs,
        out_specs=[o_spec, pl.BlockSpec((GLA_GROUP, GLA_HEADS, GLA_DV, GLA_DK), lambda c: (order(c), 0, 0, 0)),
                   pl.BlockSpec((GLA_GROUP, GLA_HEADS, CHUNK, CHUNK), lambda c: (order(c), 0, 0, 0)),
                   pl.BlockSpec((rows, GLA_K), lambda c: (order(c), 0))],
        out_shape=[jax.ShapeDtypeStruct((S, GLA_V), F32), jax.ShapeDtypeStruct((n, GLA_HEADS, GLA_DV, GLA_DK), BF16),
                   jax.ShapeDtypeStruct((n, GLA_HEADS, CHUNK, CHUNK), BF16), jax.ShapeDtypeStruct((S, GLA_K), F32)],
        scratch_shapes=[pltpu.VMEM((GLA_HEADS, GLA_DV, GLA_DK), F32)],
    )(*operands)


def _gla_bwd(proj, kept, do, reverse, prev=None):
    S = proj.shape[0]
    n = S // CHUNK
    nb = n // GLA_GROUP
    rows = GLA_GROUP * CHUNK
    order = (lambda c: c) if reverse else (lambda c: nb - 1 - c)
    seq = list(range(GLA_GROUP)) if reverse else list(range(GLA_GROUP))[::-1]
    out_dt = F32 if prev is None else BF16

    def body(*refs):
        if prev is None:
            q_ref, k_ref, v_ref, b_ref, st_ref, a_ref, do_ref, dq_ref, dk_ref, dv_ref, dg_ref, dstate = refs
        else:
            q_ref, k_ref, v_ref, b_ref, st_ref, a_ref, do_ref, pq, pk, pv, dq_ref, dk_ref, dv_ref, dg_ref, dstate = refs
        c = pl.program_id(0)

        @pl.when(c == 0)
        def _():
            dstate[...] = jnp.zeros_like(dstate)

        row = lax.broadcasted_iota(jnp.int32, (CHUNK, GLA_DK), 0)
        for h in range(GLA_HEADS):
            ksl = slice(h * GLA_DK, (h + 1) * GLA_DK)
            vsl = slice(h * GLA_DV, (h + 1) * GLA_DV)
            dst = dstate[h]
            for cc in seq:
                rs = _gla_rows(cc)
                t = _gla_chunk_terms(q_ref, k_ref, v_ref, None, h, reverse, rs, b_ref)
                v = t["v"]
                dob = do_ref[rs, vsl].astype(BF16)
                st_b = st_ref[cc, h]
                dst_b = dst.astype(BF16)
                qe_b, ke_b = t["qe"].astype(BF16), t["ke"].astype(BF16)
                q_in_b, k_st_b = t["q_in"].astype(BF16), t["k_st"].astype(BF16)
                da = jnp.where(t["causal"], _dot(dob, v, NT), 0.0).astype(BF16)
                dv = _dot(a_ref[cc, h], dob, TN) + _dot(k_st_b, dst_b, NT)
                dqe = _dot(da, ke_b)
                dke = _dot(da, qe_b, TN)
                dq_in = _dot(dob, st_b)
                dk_st = _dot(v, dst_b)
                ddec = jnp.sum(dst * st_b.astype(F32), axis=0, keepdims=True)
                dst = _dot(dob, q_in_b, TN) + dst * t["dec"]
                dq = (dqe * t["e_q"] + dq_in * t["e_in"]) * (GLA_DK ** -0.5)
                dk = dke * t["e_k"] + dk_st * t["e_st"]
                w_q, w_k = dqe * t["qe"], dke * t["ke"]
                w_st = dk_st * t["k_st"]
                db = w_q - w_k + dq_in * t["q_in"] - w_st
                db_ref = jnp.sum(w_k - w_q, axis=0, keepdims=True)
                db_last = jnp.sum(w_st, axis=0, keepdims=True) + ddec * t["dec"]
                db = db + jnp.where(row == t["r_ref"], db_ref, 0.0) + jnp.where(row == t["r_last"], db_last, 0.0)
                dg_ref[rs, ksl] = _cumsum_rows(db, not reverse)
                if prev is not None:
                    dq, dk, dv = dq + pq[rs, ksl], dk + pk[rs, ksl], dv + pv[rs, vsl]
                dq_ref[rs, ksl] = dq.astype(out_dt)
                dk_ref[rs, ksl] = dk.astype(out_dt)
                dv_ref[rs, vsl] = dv.astype(out_dt)
            dstate[h] = dst

    q_spec, k_spec, v_spec = _gla_specs(order)
    kk = pl.BlockSpec((rows, GLA_K), lambda c: (order(c), 0))
    vv = pl.BlockSpec((rows, GLA_V), lambda c: (order(c), 0))
    states, scores, sums = kept
    in_specs = [q_spec, k_spec, v_spec, kk,
                pl.BlockSpec((GLA_GROUP, GLA_HEADS, GLA_DV, GLA_DK), lambda c: (order(c), 0, 0, 0)),
                pl.BlockSpec((GLA_GROUP, GLA_HEADS, CHUNK, CHUNK), lambda c: (order(c), 0, 0, 0)), vv]
    operands = [proj, proj, proj, sums, states, scores, do]
    if prev is not None:
        in_specs += [kk, kk, vv]
        operands += list(prev)
    return pl.pallas_call(
        body, name="gla_bwd_rev" if reverse else "gla_bwd", grid=(nb,), in_specs=in_specs, out_specs=[kk, kk, vv, kk],
        out_shape=[jax.ShapeDtypeStruct((S, GLA_K), out_dt), jax.ShapeDtypeStruct((S, GLA_K), out_dt),
                   jax.ShapeDtypeStruct((S, GLA_V), out_dt), jax.ShapeDtypeStruct((S, GLA_K), F32)],
        scratch_shapes=[pltpu.VMEM((GLA_HEADS, GLA_DV, GLA_DK), F32)],
    )(*operands)


def _gates_fwd(z, wg, bias, tm=512):
    S = z.shape[0]
    W = 2 * GLA_K

    def body(z_ref, w_ref, b_ref, o_ref):
        zg = _dot(z_ref[...], w_ref[...]) + b_ref[...]
        o_ref[...] = (jnp.minimum(zg, 0.0) - jnp.log(1.0 + jnp.exp(-jnp.abs(zg)))) * (1.0 / GATE_NORM)

    return pl.pallas_call(
        body, name="gates_fwd", grid=(S // tm,),
        in_specs=[pl.BlockSpec((tm, Z_W), lambda i: (i, 0)), pl.BlockSpec((Z_W, W), lambda i: (0, 0)),
                  pl.BlockSpec((1, W), lambda i: (0, 0))],
        out_specs=pl.BlockSpec((tm, W), lambda i: (i, 0)), out_shape=jax.ShapeDtypeStruct((S, W), F32),
    )(z, wg, bias)


def _gates_bwd(z, wg, bias, dg_f, dg_b, tm=512):
    S = z.shape[0]
    W = 2 * GLA_K

    def body(z_ref, w_ref, b_ref, dgf_ref, dgb_ref, dz_ref, dw_ref, db_ref):
        i = pl.program_id(0)

        @pl.when(i == 0)
        def _():
            dw_ref[...] = jnp.zeros_like(dw_ref)
            db_ref[...] = jnp.zeros_like(db_ref)

        zv = z_ref[...]
        zg = _dot(zv, w_ref[...]) + b_ref[...]
        dg = jnp.concatenate([dgf_ref[...], dgb_ref[...]], axis=1)
        dzg = dg * (1.0 / GATE_NORM) * _sigmoid(-zg)
        db_ref[...] += jnp.sum(dzg, axis=0, keepdims=True)
        dzg_b = dzg.astype(BF16)
        dw_ref[...] += _dot(zv, dzg_b, TN)
        dz_ref[...] = _dot(dzg_b, w_ref[...], NT).astype(BF16)

    half = pl.BlockSpec((tm, GLA_K), lambda i: (i, 0))
    return pl.pallas_call(
        body, name="gates_bwd", grid=(S // tm,),
        in_specs=[pl.BlockSpec((tm, Z_W), lambda i: (i, 0)), pl.BlockSpec((Z_W, W), lambda i: (0, 0)),
                  pl.BlockSpec((1, W), lambda i: (0, 0)), half, half],
        out_specs=[pl.BlockSpec((tm, Z_W), lambda i: (i, 0)), pl.BlockSpec((Z_W, W), lambda i: (0, 0)),
                   pl.BlockSpec((1, W), lambda i: (0, 0))],
        out_shape=[jax.ShapeDtypeStruct((S, Z_W), BF16), jax.ShapeDtypeStruct((Z_W, W), F32),
                   jax.ShapeDtypeStruct((1, W), F32)],
    )(z, wg, bias, dg_f, dg_b)


def _gla_out_fwd(o, proj, g, cat, tm=512):
    S = o.shape[0]

    def body(o_ref, gr_ref, g_ref, cat_ref, out_ref):
        gn = g_ref[...]
        for h in range(GLA_HEADS):
            sl = slice(h * GLA_DV, (h + 1) * GLA_DV)
            ov = o_ref[:, sl]
            r = lax.rsqrt(jnp.mean(ov * ov, axis=-1, keepdims=True) + EPS)
            gr = gr_ref[:, sl].astype(F32)
            out_ref[:, sl] = (ov * r * gn * _silu(gr)).astype(BF16)

    blk = pl.BlockSpec((tm, GLA_V), lambda i: (i, 0))
    return pl.pallas_call(
        body, name="gla_out_fwd", grid=(S // tm,),
        in_specs=[blk, pl.BlockSpec((tm, GLA_V), lambda i: (i, (3 * ATTN_W + 2 * GLA_K + GLA_V) // GLA_V)),
                  pl.BlockSpec((1, GLA_DV), lambda i: (0, 0)), ANY],
        out_specs=pl.BlockSpec((tm, GLA_V), lambda i: (i, 1)), out_shape=jax.ShapeDtypeStruct((S, D_MODEL), BF16),
        input_output_aliases={3: 0},
    )(o, proj, g, cat)


def _gla_out_bwd(o, proj, g, dcat, dproj, tm=512):
    S = o.shape[0]

    def body(o_ref, gr_ref, g_ref, dgo_ref, dproj_ref, do_ref, dgr_ref, dg_ref):
        i = pl.program_id(0)

        @pl.when(i == 0)
        def _():
            dg_ref[...] = jnp.zeros_like(dg_ref)

        gn = g_ref[...]
        dg_acc = jnp.zeros((1, GLA_DV), F32)
        for h in range(GLA_HEADS):
            sl = slice(h * GLA_DV, (h + 1) * GLA_DV)
            ov = o_ref[:, sl]
            r = lax.rsqrt(jnp.mean(ov * ov, axis=-1, keepdims=True) + EPS)
            yhat = ov * r
            gr = gr_ref[:, sl].astype(F32)
            sg = _sigmoid(gr)
            dgo = dgo_ref[:, sl].astype(F32)
            dgr_ref[:, sl] = (dgo * (yhat * gn) * (sg * (1.0 + gr * (1.0 - sg)))).astype(BF16)
            dy = dgo * (gr * sg)
            dg_acc = dg_acc + jnp.sum(dy * yhat, axis=0, keepdims=True)
            t = dy * gn
            do_ref[:, sl] = r * (t - yhat * jnp.mean(t * yhat, axis=-1, keepdims=True))
        dg_ref[...] += dg_acc

    blk = pl.BlockSpec((tm, GLA_V), lambda i: (i, 0))
    vec = pl.BlockSpec((1, GLA_DV), lambda i: (0, 0))
    return pl.pallas_call(
        body, name="gla_out_bwd", grid=(S // tm,),
        in_specs=[blk, pl.BlockSpec((tm, GLA_V), lambda i: (i, (3 * ATTN_W + 2 * GLA_K + GLA_V) // GLA_V)), vec,
                  pl.BlockSpec((tm, GLA_V), lambda i: (i, 1)), ANY],
        out_specs=[blk, pl.BlockSpec((tm, GLA_V), lambda i: (i, (3 * ATTN_W + 2 * GLA_K + GLA_V) // GLA_V)), vec],
        out_shape=[jax.ShapeDtypeStruct((S, GLA_V), F32), jax.ShapeDtypeStruct((S, IN_MAIN), BF16),
                   jax.ShapeDtypeStruct((1, GLA_DV), F32)],
        input_output_aliases={4: 1},
    )(o, proj, g, dcat, dproj)


HALO = 16


def _halo_specs(tm, tn, S):
    cur = pl.BlockSpec((tm, tn), lambda j, i: (i, j))
    prev = pl.BlockSpec((HALO, tn), lambda j, i: (jnp.maximum(i * (tm // HALO) - 1, 0), j))
    nxt = pl.BlockSpec((HALO, tn), lambda j, i: (jnp.minimum((i + 1) * (tm // HALO), S // HALO - 1), j))
    return cur, prev, nxt


def _shifted(c_ref, p_ref, n_ref, n_blocks, i=None):
    if i is None:
        i = pl.program_id(1)
    x = c_ref[...].astype(F32)
    tm = x.shape[0]
    row = lax.broadcasted_iota(jnp.int32, x.shape, 0)
    before = p_ref[HALO - 1:HALO, :].astype(F32) * (i > 0).astype(F32)
    after = n_ref[0:1, :].astype(F32) * (i < n_blocks - 1).astype(F32)
    x_m1 = jnp.where(row == 0, before, pltpu.roll(x, 1, axis=0))
    x_p1 = jnp.where(row == tm - 1, after, pltpu.roll(x, tm - 1, axis=0))
    return x, x_m1, x_p1


def _glu_fwd(gp, up, cw, cb, tm=512, tn=1408):
    S = gp.shape[0]
    nb = S // tm

    def body(c_ref, p_ref, n_ref, up_ref, w_ref, b_ref, o_ref, gate_ref):
        x, x_m1, x_p1 = _shifted(c_ref, p_ref, n_ref, nb)
        w = w_ref[...]
        gate = w[0:1, :] * x_m1 + w[1:2, :] * x + w[2:3, :] * x_p1 + b_ref[...]
        gate_ref[...] = gate.astype(BF16)
        o_ref[...] = (_silu(gate) * up_ref[...].astype(F32)).astype(BF16)

    cur, prev, nxt = _halo_specs(tm, tn, S)
    out = jax.ShapeDtypeStruct((S, D_FF), BF16)
    return pl.pallas_call(
        body, name="glu_fwd", grid=(D_FF // tn, nb),
        in_specs=[cur, prev, nxt, cur, pl.BlockSpec((3, tn), lambda j, i: (0, j)), pl.BlockSpec((1, tn), lambda j, i: (0, j))],
        out_specs=[cur, cur], out_shape=[out, out],
    )(gp, gp, gp, up, cw, cb)


def _glu_bwd(gate, gp, up, dact, cw, tm=512, tn=1408):
    S = gp.shape[0]
    nb = S // tm

    def body(g_ref, gb_ref, ga_ref, x_ref, up_ref, upp_ref, upn_ref, da_ref, dap_ref, dan_ref, w_ref,
             dup_ref, dgp_ref, dw_ref, db_ref):
        i = pl.program_id(1)

        @pl.when(i == 0)
        def _():
            dw_ref[...] = jnp.zeros_like(dw_ref)
            db_ref[...] = jnp.zeros_like(db_ref)

        w = w_ref[...]
        w0, w1, w2 = w[0:1, :], w[1:2, :], w[2:3, :]

        def d_gate(g, da, upv):
            sg = _sigmoid(g)
            return sg, da * upv * (sg * (1.0 + g * (1.0 - sg)))

        g = g_ref[...].astype(F32)
        da = da_ref[...].astype(F32)
        sg, dgate = d_gate(g, da, up_ref[...].astype(F32))
        dup_ref[...] = (da * (g * sg)).astype(BF16)

        last = slice(HALO - 1, HALO)
        _, dgate_before = d_gate(gb_ref[...].astype(F32)[last, :], dap_ref[...].astype(F32)[last, :], upp_ref[...].astype(F32)[last, :])
        _, dgate_after = d_gate(ga_ref[...].astype(F32)[0:1, :], dan_ref[...].astype(F32)[0:1, :], upn_ref[...].astype(F32)[0:1, :])
        dgate_before = dgate_before * (i > 0).astype(F32)
        dgate_after = dgate_after * (i < nb - 1).astype(F32)
        row = lax.broadcasted_iota(jnp.int32, dgate.shape, 0)
        dg_m1 = jnp.where(row == 0, dgate_before, pltpu.roll(dgate, 1, axis=0))
        dg_p1 = jnp.where(row == tm - 1, dgate_after, pltpu.roll(dgate, tm - 1, axis=0))
        dgp_ref[...] = (w0 * dg_p1 + w1 * dgate + w2 * dg_m1).astype(BF16)
        x = x_ref[...].astype(F32)
        db_ref[...] += jnp.sum(dgate, axis=0, keepdims=True)
        dw_ref[...] += jnp.concatenate(
            [jnp.sum(dg_p1 * x, axis=0, keepdims=True), jnp.sum(dgate * x, axis=0, keepdims=True),
             jnp.sum(dg_m1 * x, axis=0, keepdims=True)], axis=0)

    cur, prev, nxt = _halo_specs(tm, tn, S)
    w_spec = pl.BlockSpec((3, tn), lambda j, i: (0, j))
    b_spec = pl.BlockSpec((1, tn), lambda j, i: (0, j))
    return pl.pallas_call(
        body, name="glu_bwd", grid=(D_FF // tn, nb),
        in_specs=[cur, prev, nxt, cur, cur, prev, nxt, cur, prev, nxt, w_spec],
        out_specs=[cur, cur, w_spec, b_spec],
        out_shape=[jax.ShapeDtypeStruct((S, D_FF), BF16), jax.ShapeDtypeStruct((S, D_FF), BF16),
                   jax.ShapeDtypeStruct((3, D_FF), F32), jax.ShapeDtypeStruct((1, D_FF), F32)],
    )(gate, gate, gate, gp, up, up, up, dact, dact, dact, cw)


def _local_step(x, target, norm1_g, w_in_t, wg, gate_bias, gla_norm_g, attn_norm_g, w_out, norm2_g,
                w_gate4, w_up4, conv_w, conv_b, w_down, final_norm_g, on_grad=lambda event, arrays: ()):
    S = x.shape[0]
    tabs = _rope_tables(S)

    n1 = _rms_fwd("rms1_fwd", x, norm1_g)
    z_block = IN_MAIN // Z_W
    proj = _mm_nt("in_proj", n1, w_in_t, 1024, 1536, BF16, n_out=IN_MAIN)
    z = _matmul(
        "in_proj_z",
        [(n1, pl.BlockSpec((1024, D_MODEL), lambda i: (i, 0)), w_in_t, pl.BlockSpec((Z_W, D_MODEL), lambda i: (z_block, 0)), NT)],
        (S // 1024,), jax.ShapeDtypeStruct((S, Z_W), BF16), pl.BlockSpec((1024, Z_W), lambda i: (i, 0)), 1)
    qk = _rope_fwd(proj, tabs)
    branch = [_attn_fwd(qk, proj, d) for d in DILATIONS]
    ao, o_attn, lse = _attn_combine([b[0] for b in branch], [b[1] for b in branch], attn_norm_g)
    gates = _gates_fwd(z, wg, gate_bias)
    o_f, *kept_f = _gla_fwd(proj, gates, False)
    o_gla, *kept_b = _gla_fwd(proj, gates, True, o_prev=o_f)
    cat = _gla_out_fwd(o_gla, proj, gla_norm_g, ao)
    h1 = _mm_nn("out_proj", cat, w_out, 1024, 1024, F32, res=x)
    n2 = _rms_fwd("rms2_fwd", h1, norm2_g)
    gp = _mm_nn_sharded("ffn_gate", n2, w_gate4, 1024, BF16)
    up = _mm_nn_sharded("ffn_up", n2, w_up4, 1024, BF16)
    act, gate = _glu_fwd(gp, up, conv_w, conv_b)
    tk = D_FF // N_CHIPS
    h2 = _mm_nn("ffn_down", act, w_down, 1024, 512, F32, res=h1)
    loss_row, d_final_g, dh2, dh2_b = _final_loss(h2, final_norm_g.reshape(1, D_MODEL), target)

    dact = _mm_nt("ffn_down_bwd", dh2_b, w_down, 1024, tk, BF16)
    dup, dgp, d_conv_w, d_conv_b = _glu_bwd(gate, gp, up, dact, conv_w)
    d_w_down = _mm_tn("ffn_down_wgrad", act, dh2_b, 512, D_MODEL, 2048, BF16)
    on_grad("w_down", dict(w_down=d_w_down))
    dgp = _after(dgp, d_w_down)
    d_w_gate4 = _mm_tn("ffn_gate_wgrad", n2, dgp, 1024, tk, 2048, BF16, out3=tk)
    dup = _after(dup, d_w_gate4)
    d_w_up4 = _mm_tn("ffn_up_wgrad", n2, dup, 1024, tk, 2048, BF16, out3=tk)
    held = on_grad("w_gate_w_up", dict(w_gate=d_w_gate4, w_up=d_w_up4))
    dgp = _after(dgp, d_w_up4, *held)
    shard_pairs = [
        (g, pl.BlockSpec((512, tk), functools.partial(lambda s, j, i: (i, s), s)),
         w4, pl.BlockSpec((None, 512, tk), functools.partial(lambda s, j, i: (s, j, 0), s)), NT)
        for g, w4 in ((dgp, w_gate4), (dup, w_up4)) for s in range(N_CHIPS)]
    dn2 = _matmul("ffn_in_bwd", shard_pairs, (D_MODEL // 512, S // 512), jax.ShapeDtypeStruct((S, D_MODEL), BF16),
                  pl.BlockSpec((512, 512), lambda j, i: (i, j)), 1)
    dh1, dh1_b, d_norm2_g = _rms_bwd("rms2_bwd", h1, norm2_g, dn2, dh2)

    d_w_out = _mm_tn("out_proj_wgrad", cat, dh1_b, D_MODEL, 1024, 1024, BF16)
    held = on_grad("w_out", dict(w_out=d_w_out))
    dcat = _mm_nt("out_proj_bwd", _after(dh1_b, d_w_out, *held), w_out, 1024, 1024, BF16)
    do_attn, delta, d_attn_norm_g = _attn_norm_bwd(o_attn, attn_norm_g, dcat)
    dq_a, dkv_a = None, None
    for d in DILATIONS:
        dq_a = _attn_bwd_dq(qk, proj, do_attn, lse, delta, d, dq_a)
        dkv_a = _attn_bwd_dkv(qk, proj, do_attn, lse, delta, d, dkv_a)
    dproj = _attn_grad_merge(dq_a, dkv_a[0], dkv_a[1], tabs)
    held = on_grad("mid", dict(anchor=dproj))
    do_gla, dproj, d_gla_norm_g = _gla_out_bwd(o_gla, proj, gla_norm_g, _after(dcat, *held), dproj)
    dq_f, dk_f, dv_f, dg_f = _gla_bwd(proj, kept_f, do_gla, False)
    dgq, dgk, dgv, dg_b = _gla_bwd(proj, kept_b, do_gla, True, prev=(dq_f, dk_f, dv_f))
    dz, d_wg, d_gate_bias = _gates_bwd(z, wg, gate_bias, dg_f, dg_b)
    dproj = lax.dynamic_update_slice(dproj, jnp.concatenate([dgq, dgk, dgv], axis=1), (0, 3 * ATTN_W))
    d_w_in_t = _mm_tn("in_proj_wgrad", dproj, n1, 768, D_MODEL, 2048, BF16, rows_out=IN_W)
    n_tok = S // 1024
    d_w_in_t = _matmul(
        "in_proj_z_wgrad",
        [(dz, pl.BlockSpec((1024, Z_W), lambda i, j, k: (k, 0)), n1, pl.BlockSpec((1024, D_MODEL), lambda i, j, k: (k, 0)), TN)],
        (1, 1, n_tok), jax.ShapeDtypeStruct((IN_W, D_MODEL), BF16), pl.BlockSpec((Z_W, D_MODEL), lambda i, j, k: (z_block, 0)),
        n_tok, into=d_w_in_t)
    held = on_grad("w_in", dict(w_in_t=d_w_in_t))
    n_rows = S // 1024
    head_rows = max(1, n_rows // 4)

    def in_proj_bwd(name, first, count, a, into):
        return _matmul(
            name,
            [(a, pl.BlockSpec((1024, IN_MAIN), lambda j, i: (i + first, 0)), w_in_t, pl.BlockSpec((IN_MAIN, 512), lambda j, i: (0, j)), NN),
             (dz, pl.BlockSpec((1024, Z_W), lambda j, i: (i + first, 0)), w_in_t, pl.BlockSpec((Z_W, 512), lambda j, i: (z_block, j)), NN)],
            (D_MODEL // 512, count), jax.ShapeDtypeStruct((S, D_MODEL), BF16),
            pl.BlockSpec((1024, 512), lambda j, i: (i + first, j)), 1, into=into)

    dproj = _after(dproj, d_w_in_t, *held)
    dn1 = in_proj_bwd("in_proj_bwd_a", 0, head_rows, dproj, None)
    held = on_grad("last", dict(last=dn1))
    dn1 = in_proj_bwd("in_proj_bwd_b", head_rows, n_rows - head_rows, dproj, _after(dn1, *held))
    grad_x, _, d_norm1_g = _rms_bwd("rms1_bwd", x, norm1_g, dn1, dh1)

    big = dict(w_in_t=d_w_in_t, w_out=d_w_out, w_gate4=d_w_gate4, w_up4=d_w_up4, w_down=d_w_down)
    small = dict(loss=loss_row, norm1_g=d_norm1_g, wg=d_wg, gate_bias=d_gate_bias, gla_norm_g=d_gla_norm_g,
                 attn_norm_g=d_attn_norm_g, norm2_g=d_norm2_g, conv_w=d_conv_w, conv_b=d_conv_b, final_norm_g=d_final_g)
    return grad_x, big, small


def _position():
    return lax.axis_index("x"), lax.axis_index("y"), lax.axis_index("c")


def _other_chips(x, y):
    return [(1 - x, y), (x, 1 - y), (1 - x, 1 - y)]


def _gather_chips_async(name, shards, collective_id):
    n = len(shards)

    def body(*refs):
        ins, outs = refs[:n], refs[n:2 * n]
        send, recv, loc = refs[2 * n:]
        x, y, c = _position()
        me = 2 * x + y
        chips = _other_chips(x, y)
        barrier = pltpu.get_barrier_semaphore()
        for px, py in chips:
            pl.semaphore_signal(barrier, inc=1, device_id=(px, py, c), device_id_type=MESH)
        pl.semaphore_wait(barrier, len(chips))
        started = []
        for w in range(n):
            own = pltpu.make_async_copy(ins[w], outs[w].at[me], loc.at[w])
            own.start()
            started.append(own)
        sends = []
        for w in range(n):
            for j, (px, py) in enumerate(chips):
                cp = pltpu.make_async_remote_copy(ins[w], outs[w].at[me], send.at[3 * w + j], recv.at[3 * w + j],
                                                  device_id=(px, py, c), device_id_type=MESH)
                cp.start()
                sends.append(cp)
        for w in range(n):
            for j, (px, py) in enumerate(chips):
                pltpu.make_async_remote_copy(ins[w], outs[w].at[2 * px + py], send.at[3 * w + j], recv.at[3 * w + j],
                                             device_id=(px, py, c), device_id_type=MESH).wait_recv()
        for cp in sends:
            cp.wait_send()
        for own in started:
            own.wait()

    return pl.kernel(
        body, name=name, mesh=_sequencer(),
        out_type=[jax.ShapeDtypeStruct((N_CHIPS,) + s.shape, s.dtype) for s in shards],
        scratch_types=[pltpu.SemaphoreType.DMA((3 * n,)), pltpu.SemaphoreType.DMA((3 * n,)), pltpu.SemaphoreType.DMA((n,))],
        compiler_params=pltpu.CompilerParams(collective_id=collective_id),
    )(*shards)


def _gather_halves_async(name, small, shard, collective_id):
    half = shard.shape[1] // 2

    def body(small_ref, shard_ref, small_out, out, send, recv, loc):
        x, y, c = _position()
        me = 2 * x + y
        sibling = (x, y, 1 - c)
        chips = _other_chips(x, y)
        barrier = pltpu.get_barrier_semaphore()
        for px, py in chips:
            pl.semaphore_signal(barrier, inc=1, device_id=(px, py, c), device_id_type=MESH)
        pl.semaphore_signal(barrier, inc=1, device_id=sibling, device_id_type=MESH)
        pl.semaphore_wait(barrier, len(chips) + 1)
        mine = pl.ds(pl.multiple_of(c * half, LANES), half)
        theirs = pl.ds(pl.multiple_of((1 - c) * half, LANES), half)
        own = [pltpu.make_async_copy(small_ref, small_out.at[me], loc.at[0]),
               pltpu.make_async_copy(shard_ref, out.at[me], loc.at[1])]
        for cp in own:
            cp.start()
        sends = []
        for j, (px, py) in enumerate(chips):
            sends.append(pltpu.make_async_remote_copy(small_ref, small_out.at[me], send.at[j], recv.at[j],
                                                      device_id=(px, py, c), device_id_type=MESH))
            sends.append(pltpu.make_async_remote_copy(shard_ref.at[:, mine], out.at[me, :, mine], send.at[3 + j], recv.at[3 + j],
                                                      device_id=(px, py, c), device_id_type=MESH))
        for cp in sends:
            cp.start()
        passed = []
        for j, (px, py) in enumerate(chips):
            slot = 2 * px + py
            pltpu.make_async_remote_copy(shard_ref.at[:, mine], out.at[slot, :, mine], send.at[3 + j], recv.at[3 + j],
                                         device_id=(px, py, c), device_id_type=MESH).wait_recv()
            cp = pltpu.make_async_remote_copy(out.at[slot, :, mine], out.at[slot, :, mine], send.at[6 + j], recv.at[6 + j],
                                              device_id=sibling, device_id_type=MESH)
            cp.start()
            passed.append(cp)
        for j, (px, py) in enumerate(chips):
            slot = 2 * px + py
            pltpu.make_async_remote_copy(small_ref, small_out.at[slot], send.at[j], recv.at[j],
                                         device_id=(px, py, c), device_id_type=MESH).wait_recv()
            pltpu.make_async_remote_copy(out.at[slot, :, theirs], out.at[slot, :, theirs], send.at[6 + j], recv.at[6 + j],
                                         device_id=sibling, device_id_type=MESH).wait_recv()
        for cp in sends + passed:
            cp.wait_send()
        for cp in own:
            cp.wait()

    return pl.kernel(
        body, name=name, mesh=_sequencer(),
        out_type=[jax.ShapeDtypeStruct((N_CHIPS,) + small.shape, small.dtype),
                  jax.ShapeDtypeStruct((N_CHIPS,) + shard.shape, shard.dtype)],
        scratch_types=[pltpu.SemaphoreType.DMA((9,)), pltpu.SemaphoreType.DMA((9,)), pltpu.SemaphoreType.DMA((2,))],
        compiler_params=pltpu.CompilerParams(collective_id=collective_id),
    )(small, shard)


def _sequencer():
    return plsc.ScalarSubcoreMesh(axis_name="sequencer", num_cores=1)


def _sibling_exchange_async(name, arrs, collective_id):
    n = len(arrs)

    def body(*refs):
        ins, outs = refs[:n], refs[n:2 * n]
        send, recv = refs[2 * n:]
        x, y, c = _position()
        sibling = (x, y, 1 - c)
        barrier = pltpu.get_barrier_semaphore()
        pl.semaphore_signal(barrier, inc=1, device_id=sibling, device_id_type=MESH)
        pl.semaphore_wait(barrier, 1)
        copies = [pltpu.make_async_remote_copy(ins[w], outs[w], send.at[w], recv.at[w], device_id=sibling,
                                               device_id_type=MESH) for w in range(n)]
        for cp in copies:
            cp.start()
        for cp in copies:
            cp.wait()

    return pl.kernel(
        body, name=name, out_type=[jax.ShapeDtypeStruct(a.shape, a.dtype) for a in arrs],
        scratch_types=[pltpu.SemaphoreType.DMA((n,)), pltpu.SemaphoreType.DMA((n,))],
        compiler_params=pltpu.CompilerParams(collective_id=collective_id), mesh=_sequencer(),
    )(*arrs)


def _scatter_chips_async(name, parts, collective_id):
    n = len(parts)

    def body(*refs):
        ins, outs = refs[:n], refs[n:2 * n]
        send, recv, loc = refs[2 * n:]
        x, y, c = _position()
        me = 2 * x + y
        chips = _other_chips(x, y)
        barrier = pltpu.get_barrier_semaphore()
        for px, py in chips:
            pl.semaphore_signal(barrier, inc=1, device_id=(px, py, c), device_id_type=MESH)
        pl.semaphore_wait(barrier, len(chips))
        started = []
        for w in range(n):
            own = pltpu.make_async_copy(ins[w].at[me], outs[w].at[me], loc.at[w])
            own.start()
            started.append(own)
        sends = []
        for w in range(n):
            for j, (px, py) in enumerate(chips):
                cp = pltpu.make_async_remote_copy(ins[w].at[2 * px + py], outs[w].at[me], send.at[3 * w + j],
                                                  recv.at[3 * w + j], device_id=(px, py, c), device_id_type=MESH)
                cp.start()
                sends.append(cp)
        for w in range(n):
            for j, (px, py) in enumerate(chips):
                pltpu.make_async_remote_copy(ins[w].at[me], outs[w].at[2 * px + py], send.at[3 * w + j], recv.at[3 * w + j],
                                             device_id=(px, py, c), device_id_type=MESH).wait_recv()
        for cp in sends:
            cp.wait_send()
        for own in started:
            own.wait()

    return pl.kernel(
        body, name=name, out_type=[jax.ShapeDtypeStruct(p.shape, p.dtype) for p in parts],
        scratch_types=[pltpu.SemaphoreType.DMA((3 * n,)), pltpu.SemaphoreType.DMA((3 * n,)), pltpu.SemaphoreType.DMA((n,))],
        compiler_params=pltpu.CompilerParams(collective_id=collective_id), mesh=_sequencer(),
    )(*parts)


def _allreduce_rows(buf):
    R = buf.shape[0]

    def body(in_ref, out_ref, land, send, recv):
        x, y, c = _position()
        me = 4 * x + 2 * y + c
        land[pl.ds(me, 1)] = in_ref[...][None]
        peers = []
        for mask in range(1, N_DEV):
            px = 1 - x if mask & 4 else x
            py = 1 - y if mask & 2 else y
            pc = 1 - c if mask & 1 else c
            peers.append((px, py, pc))
        sends = []
        for k, peer in enumerate(peers):
            cp = pltpu.make_async_remote_copy(in_ref, land.at[me], send.at[k], recv.at[k], device_id=peer, device_id_type=MESH)
            cp.start()
            sends.append(cp)
        for k, (px, py, pc) in enumerate(peers):
            pltpu.make_async_remote_copy(in_ref, land.at[4 * px + 2 * py + pc], send.at[k], recv.at[k],
                                         device_id=(px, py, pc), device_id_type=MESH).wait_recv()
        for cp in sends:
            cp.wait_send()
        tot = land[0]
        for i in range(1, N_DEV):
            tot = tot + land[i]
        out_ref[...] = tot

    vm = pl.BlockSpec(memory_space=pltpu.VMEM)
    return pl.pallas_call(
        body, name="allreduce_small", in_specs=[vm], out_specs=vm, out_shape=jax.ShapeDtypeStruct((R, LANES), F32),
        scratch_shapes=[pltpu.VMEM((N_DEV, R, LANES), F32), pltpu.SemaphoreType.DMA((N_DEV - 1,)),
                        pltpu.SemaphoreType.DMA((N_DEV - 1,))],
    )(buf)


def _tile2d(r, c, cap):
    if r <= cap:
        return r, c
    fits = [t for t in range(16, cap + 1, 16) if r % t == 0]
    return (max(fits), c) if fits else (r, 256)


def _pair_sum(name, a, b):
    n, r, c = a.shape
    tr, tc = _tile2d(r, c, 1024)

    def body(a_ref, b_ref, o_ref):
        o_ref[...] = (a_ref[...].astype(F32) + b_ref[...].astype(F32)).astype(BF16)

    blk = pl.BlockSpec((None, tr, tc), lambda s, i, j: (s, i, j))
    return pl.pallas_call(
        body, name=name, grid=(n, r // tr, c // tc), in_specs=[blk, blk], out_specs=blk,
        out_shape=jax.ShapeDtypeStruct(a.shape, BF16),
    )(a, b)


def _adamw_math(w, m, v, g):
    m2 = ADAM_B1 * m + (1.0 - ADAM_B1) * g
    v2 = ADAM_B2 * v + (1.0 - ADAM_B2) * (g * g)
    m_hat = m2 / (1.0 - ADAM_B1 ** ADAM_STEP)
    v_hat = v2 / (1.0 - ADAM_B2 ** ADAM_STEP)
    delta = -ADAM_LR * (m_hat / (jnp.sqrt(v_hat) + ADAM_EPS) + ADAM_WD * w)
    return delta, m2, v2


def _adamw(name, w, m, v, g):
    r, c = w.shape
    stacked = g.ndim == 3
    tr, tc = _tile2d(r, c, 256)

    def body(w_ref, m_ref, v_ref, g_ref, go_ref, d_ref, m2_ref, v2_ref):
        if stacked:
            gv = g_ref[0].astype(F32)
            for i in range(1, N_CHIPS):
                gv = gv + g_ref[i].astype(F32)
        else:
            gv = g_ref[...]
        delta, m2, v2 = _adamw_math(w_ref[...], m_ref[...], v_ref[...], gv)
        go_ref[...] = gv
        d_ref[...] = delta
        m2_ref[...] = m2
        v2_ref[...] = v2

    blk = pl.BlockSpec((tr, tc), lambda i, j: (i, j))
    g_spec = pl.BlockSpec((N_CHIPS, tr, tc), lambda i, j: (0, i, j)) if stacked else blk
    out = jax.ShapeDtypeStruct((r, c), F32)
    return pl.pallas_call(
        body, name=name, grid=(r // tr, c // tc), in_specs=[blk, blk, blk, g_spec], out_specs=[blk] * 4, out_shape=[out] * 4,
    )(w, m, v, g)


def _pack_rows(pieces):
    flat = jnp.concatenate([p.reshape(-1) for p in pieces])
    rows = flat.shape[0] // LANES
    pad = (-rows) % 8
    return jnp.pad(flat.reshape(rows, LANES), ((0, pad), (0, 0)))


def _unpack_rows(buf, shapes):
    flat = buf.reshape(-1)
    out, at = [], 0
    for s in shapes:
        size = math.prod(s)
        out.append(flat[at:at + size].reshape(s))
        at += size
    return out


SMALL_NAMES = ("norm1_g", "gf_up", "gf_b", "gb_up", "gb_b", "gla_norm_g", "attn_norm_g", "norm2_g", "conv_w", "conv_b",
               "final_norm_g")
BIG_NAMES = ("w_in", "w_out", "w_gate", "w_up", "w_down")
WEIGHT_ORDER = ("norm1_g", "w_in", "gf_up", "gf_b", "gb_up", "gb_b", "gla_norm_g", "attn_norm_g", "w_out", "norm2_g",
                "w_gate", "w_up", "conv_w", "conv_b", "w_down", "final_norm_g")


def kernel(x, norm1_g, w_in, gf_up, gf_b, gb_up, gb_b, gla_norm_g, attn_norm_g, w_out, norm2_g, w_gate, w_up, conv_w, conv_b, w_down, final_norm_g, loss_target, m_norm1_g, m_w_in, m_gf_up, m_gf_b, m_gb_up, m_gb_b, m_gla_norm_g, m_attn_norm_g, m_w_out, m_norm2_g, m_w_gate, m_w_up, m_conv_w, m_conv_b, m_w_down, m_final_norm_g, v_norm1_g, v_w_in, v_gf_up, v_gf_b, v_gb_up, v_gb_b, v_gla_norm_g, v_attn_norm_g, v_w_out, v_norm2_g, v_w_gate, v_w_up, v_conv_w, v_conv_b, v_w_down, v_final_norm_g):
    w = dict(norm1_g=norm1_g, w_in=w_in, gf_up=gf_up, gf_b=gf_b, gb_up=gb_up, gb_b=gb_b, gla_norm_g=gla_norm_g,
             attn_norm_g=attn_norm_g, w_out=w_out, norm2_g=norm2_g, w_gate=w_gate, w_up=w_up, conv_w=conv_w, conv_b=conv_b,
             w_down=w_down, final_norm_g=final_norm_g)
    m = dict(norm1_g=m_norm1_g, w_in=m_w_in, gf_up=m_gf_up, gf_b=m_gf_b, gb_up=m_gb_up, gb_b=m_gb_b, gla_norm_g=m_gla_norm_g,
             attn_norm_g=m_attn_norm_g, w_out=m_w_out, norm2_g=m_norm2_g, w_gate=m_w_gate, w_up=m_w_up, conv_w=m_conv_w,
             conv_b=m_conv_b, w_down=m_w_down, final_norm_g=m_final_norm_g)
    v = dict(norm1_g=v_norm1_g, w_in=v_w_in, gf_up=v_gf_up, gf_b=v_gf_b, gb_up=v_gb_up, gb_b=v_gb_b, gla_norm_g=v_gla_norm_g,
             attn_norm_g=v_attn_norm_g, w_out=v_w_out, norm2_g=v_norm2_g, w_gate=v_w_gate, w_up=v_w_up, conv_w=v_conv_w,
             conv_b=v_conv_b, w_down=v_w_down, final_norm_g=v_final_norm_g)
    S = x.shape[1]
    chip = 2 * lax.axis_index("x") + lax.axis_index("y")
    n_in = IN_W // N_CHIPS
    n_ff = D_FF // N_CHIPS
    n_gk = GLA_K // N_CHIPS

    def owned(t):
        return {k: (jnp.transpose(t[k][0]) if k == "w_in" else t[k][0]) for k in BIG_NAMES}

    own_w, own_m, own_v = owned(w), owned(m), owned(v)
    shard = {k: own_w[k].astype(BF16) for k in BIG_NAMES}
    small_shard = _pack_rows([gf_up[0], gb_up[0], conv_w[0]])
    small4, w_in4 = _gather_halves_async("gather_w_in", small_shard, shard["w_in"], 0)
    w_out4, w_gate4, w_up4 = _gather_chips_async("gather_w_mid", [shard["w_out"], shard["w_gate"], shard["w_up"]], 1)
    (w_down4,) = _gather_chips_async("gather_w_down", [shard["w_down"]], 2)
    w_in_t = w_in4.reshape(IN_W, D_MODEL)
    rows_up = GATE_RANK * n_gk // LANES
    rows_cw = 3 * n_ff // LANES
    gf_full = jnp.transpose(small4[:, 0:rows_up].reshape(N_CHIPS, GATE_RANK, n_gk), (1, 0, 2)).reshape(GATE_RANK, GLA_K)
    gb_full = jnp.transpose(small4[:, rows_up:2 * rows_up].reshape(N_CHIPS, GATE_RANK, n_gk), (1, 0, 2)).reshape(GATE_RANK, GLA_K)
    cw_full = jnp.transpose(small4[:, 2 * rows_up:2 * rows_up + rows_cw].reshape(N_CHIPS, 3, n_ff), (1, 0, 2)).reshape(3, D_FF)
    wg = jnp.zeros((Z_W, 2 * GLA_K), F32)
    wg = wg.at[0:GATE_RANK, 0:GLA_K].set(gf_full).at[GATE_RANK:2 * GATE_RANK, GLA_K:].set(gb_full).astype(BF16)
    gate_bias = jnp.concatenate([gf_b, gb_b], axis=1)

    pending, contributions, next_id = [], {}, [3]

    def as_shards(group, arrays):
        if group == "w_in":
            return dict(w_in=arrays["w_in_t"].reshape(N_CHIPS, n_in, D_MODEL))
        if group == "w_out":
            return dict(w_out=arrays["w_out"].reshape(N_CHIPS, D_MODEL // N_CHIPS, D_MODEL))
        if group == "w_down":
            return dict(w_down=arrays["w_down"].reshape(N_CHIPS, n_ff, D_MODEL))
        return arrays

    out = {}

    def swap(group, arrays):
        mine = as_shards(group, arrays)
        pending.append((group, mine, _sibling_exchange_async(f"sibling_{group}", list(mine.values()), next_id[0])))
        next_id[0] += 1

    def sum_and_send(anchor):
        tag, mine, theirs = pending.pop()
        sums = [_pair_sum(f"pair_sum_{k}", mine[k], _after(t, *anchor)) for k, t in zip(mine, theirs)]
        contributions.update(zip(mine, _scatter_chips_async(f"scatter_{tag}", sums, next_id[0])))
        next_id[0] += 1
        return sums

    def update(names, anchor):
        for k in names:
            res = _adamw(f"adamw_{k}", own_w[k], own_m[k], own_v[k], _after(contributions[k], *anchor))
            out[k] = [(jnp.transpose(r) if k == "w_in" else r)[None] for r in res]
        return [out[k][0] for k in names]

    def on_grad(event, arrays):
        anchor = list(arrays.values())
        held = []
        if event in ("w_gate_w_up", "w_out", "mid", "last"):
            held += sum_and_send(anchor)
        if event == "mid":
            held += update(("w_down", "w_gate", "w_up"), anchor)
        if event == "last":
            held += update(("w_out",), anchor)
        if event in ("w_down", "w_gate_w_up", "w_out", "w_in"):
            swap(event, arrays)
        return held

    grad_x, _, small = _local_step(
        x[0], loss_target[0], norm1_g, w_in_t, wg, gate_bias, gla_norm_g, attn_norm_g,
        w_out4.reshape(D_MODEL, D_MODEL), norm2_g, w_gate4, w_up4, cw_full, conv_b, w_down4.reshape(D_FF, D_MODEL), final_norm_g,
        on_grad=on_grad)
    update(("w_in",), [grad_x])

    d_gf_up = small["wg"][0:GATE_RANK, 0:GLA_K]
    d_gb_up = small["wg"][GATE_RANK:2 * GATE_RANK, GLA_K:]
    pieces = [small["loss"], small["norm1_g"], d_gf_up, small["gate_bias"][:, :GLA_K], d_gb_up, small["gate_bias"][:, GLA_K:],
              small["gla_norm_g"], small["attn_norm_g"], small["norm2_g"], small["conv_w"], small["conv_b"], small["final_norm_g"]]
    total = _allreduce_rows(_pack_rows(pieces))
    summed = _unpack_rows(total, [p.shape for p in pieces])
    loss = summed[0][0, 0]
    g_small = dict(zip(SMALL_NAMES, summed[1:]))
    g_small["gf_up"] = lax.dynamic_slice_in_dim(g_small["gf_up"], chip * n_gk, n_gk, axis=1)
    g_small["gb_up"] = lax.dynamic_slice_in_dim(g_small["gb_up"], chip * n_gk, n_gk, axis=1)
    g_small["conv_w"] = lax.dynamic_slice_in_dim(g_small["conv_w"], chip * n_ff, n_ff, axis=1)
    packed = [_pack_rows([t[k] for k in SMALL_NAMES]) for t in (w, m, v, g_small)]
    res = _adamw("adamw_small", *packed)
    shapes = [w[k].shape for k in SMALL_NAMES]
    for k, vals in zip(SMALL_NAMES, zip(*[_unpack_rows(r, shapes) for r in res])):
        out[k] = list(vals)

    grads, deltas, new_m, new_v = ([out[k][i] for k in WEIGHT_ORDER] for i in range(4))
    return (loss, grad_x[None], *grads, *deltas, *new_m, *new_v)
```

```python
import functools
import math

import jax
import jax.numpy as jnp
from jax import lax
from jax.experimental import pallas as pl
from jax.experimental.pallas import tpu as pltpu
from jax.experimental.pallas import tpu_sc as plsc

F32 = jnp.float32
BF16 = jnp.bfloat16

D_MODEL = 2048
ATTN_W = 1024
HEAD = 128
N_HEADS = 8
N_SIDE = 64
DILATIONS = (1, 4, 16)
ROPE_THETA = 500000.0
ROPE_DIM = 32
GLA_K = 512
GLA_V = 1024
GLA_HEADS = 4
GLA_DK = 128
GLA_DV = 256
GATE_RANK = 16
GATE_NORM = 16.0
CHUNK = 64
IN_MAIN = 6144
IN_W = 6176
Z_W = IN_W - IN_MAIN
D_FF = 5632
EPS = 1e-6
N_CHIPS = 4
N_DEV = 8
LANES = 128

ADAM_LR = 0.001
ADAM_B1 = 0.9
ADAM_B2 = 0.999
ADAM_EPS = 1e-08
ADAM_WD = 0.01
ADAM_STEP = 10

NEG = -1e30
MESH = pl.DeviceIdType.MESH
ANY = pl.BlockSpec(memory_space=pl.ANY)

NN = ((1,), (0,))
NT = ((1,), (1,))
TN = ((0,), (0,))


def _dot(a, b, dims=NN):
    return lax.dot_general(a, b, (dims, ((), ())), preferred_element_type=F32)


def _sigmoid(x):
    return 0.5 * jnp.tanh(0.5 * x) + 0.5


def _silu(x):
    h = 0.5 * x
    return h * jnp.tanh(h) + h


def _after(x, *deps):
    return lax.optimization_barrier((x,) + deps)[0]


def _matmul(name, pairs, grid, out_shape, out_spec, nk, res=None, into=None, first=None):
    n_in = 2 * len(pairs) + (res is not None)
    dims = [p[4] for p in pairs]

    n_ops = n_in + (into is not None) + 2 * (first is not None)

    def body(*refs):
        ins, o_ref = refs[:n_in], refs[n_ops]

        def partial_sum():
            tot = None
            for p, dn in enumerate(dims):
                a, b = ins[2 * p][...], ins[2 * p + 1][...]
                t = _dot(a.astype(BF16), b.astype(BF16), dn)
                tot = t if tot is None else tot + t
            return tot

        if nk == 1:
            t = partial_sum()
            if res is not None:
                t = t + ins[-1][...]
            o_ref[...] = t.astype(o_ref.dtype)
        else:
            acc_ref = refs[n_ops + 1]
            k = pl.program_id(2)

            @pl.when(k == 0)
            def _():
                if first is not None:
                    start = _dot(refs[n_in][...].astype(BF16), refs[n_in + 1][...].astype(BF16), first[4])
                    acc_ref[...] = start + ins[-1][...] if res is not None else start
                elif res is not None:
                    acc_ref[...] = ins[-1][...]
                else:
                    acc_ref[...] = jnp.zeros_like(acc_ref)

            acc_ref[...] += partial_sum()

            @pl.when(k == nk - 1)
            def _():
                o_ref[...] = acc_ref[...].astype(o_ref.dtype)

    operands, in_specs = [], []
    for a, a_spec, b, b_spec, _ in pairs:
        operands += [a, b]
        in_specs += [a_spec, b_spec]
    if res is not None:
        operands.append(res[0])
        in_specs.append(res[1])
    if first is not None:
        assert nk > 1
        operands += [first[0], first[2]]
        in_specs += [first[1], first[3]]
    acc_shape = tuple(s for s in out_spec.block_shape if s is not None)
    scratch = [pltpu.VMEM(acc_shape, F32)] if nk > 1 else []
    aliases = {}
    if into is not None:
        aliases = {len(operands): 0}
        operands.append(into)
        in_specs.append(ANY)
    return pl.pallas_call(
        body, name=name, grid=grid, in_specs=in_specs, out_specs=out_spec, out_shape=out_shape, scratch_shapes=scratch,
        input_output_aliases=aliases,
    )(*operands)


def _mm_nn(name, a, b, tm, tn, out_dtype, res=None):
    M, K = a.shape
    N = b.shape[1]
    pairs = [(a, pl.BlockSpec((tm, K), lambda j, i: (i, 0)), b, pl.BlockSpec((K, tn), lambda j, i: (0, j)), NN)]
    r = None if res is None else (res, pl.BlockSpec((tm, tn), lambda j, i: (i, j)))
    return _matmul(name, pairs, (N // tn, M // tm), jax.ShapeDtypeStruct((M, N), out_dtype),
                   pl.BlockSpec((tm, tn), lambda j, i: (i, j)), 1, r)


def _mm_nn_sharded(name, a, b4, tm, out_dtype):
    M, K = a.shape
    n = b4.shape[2]
    pairs = [(a, pl.BlockSpec((tm, K), lambda j, i: (i, 0)), b4, pl.BlockSpec((None, K, n), lambda j, i: (j, 0, 0)), NN)]
    return _matmul(name, pairs, (N_CHIPS, M // tm), jax.ShapeDtypeStruct((M, N_CHIPS * n), out_dtype),
                   pl.BlockSpec((tm, n), lambda j, i: (i, j)), 1)


def _mm_nt(name, a, b, tm, tn, out_dtype, res=None, n_out=None):
    M, K = a.shape
    N = b.shape[0] if n_out is None else n_out
    pairs = [(a, pl.BlockSpec((tm, K), lambda j, i: (i, 0)), b, pl.BlockSpec((tn, K), lambda j, i: (j, 0)), NT)]
    r = None if res is None else (res, pl.BlockSpec((tm, tn), lambda j, i: (i, j)))
    return _matmul(name, pairs, (N // tn, M // tm), jax.ShapeDtypeStruct((M, N), out_dtype),
                   pl.BlockSpec((tm, tn), lambda j, i: (i, j)), 1, r)


def _mm_tn(name, a, g, tka, tn, tmm, out_dtype, out3=None, rows_out=None):
    M, Ka = a.shape
    N = g.shape[1]
    pairs = [(a, pl.BlockSpec((tmm, tka), lambda i, j, k: (k, i)), g, pl.BlockSpec((tmm, tn), lambda i, j, k: (k, j)), TN)]
    if out3 is None:
        shape, spec = (Ka if rows_out is None else rows_out, N), pl.BlockSpec((tka, tn), lambda i, j, k: (i, j))
    else:
        shape, spec = (N // out3, Ka, out3), pl.BlockSpec((None, tka, tn), lambda i, j, k: (j, i, 0))
    return _matmul(name, pairs, (Ka // tka, N // tn, M // tmm), jax.ShapeDtypeStruct(shape, out_dtype), spec, M // tmm)


def _rms_fwd(name, x, g, tm=512):
    S, D = x.shape

    def body(x_ref, g_ref, o_ref):
        xv = x_ref[...]
        r = lax.rsqrt(jnp.mean(xv * xv, axis=-1, keepdims=True) + EPS)
        o_ref[...] = (xv * r * g_ref[...]).astype(o_ref.dtype)

    return pl.pallas_call(
        body, name=name, grid=(S // tm,),
        in_specs=[pl.BlockSpec((tm, D), lambda i: (i, 0)), pl.BlockSpec((1, D), lambda i: (0, 0))],
        out_specs=pl.BlockSpec((tm, D), lambda i: (i, 0)), out_shape=jax.ShapeDtypeStruct((S, D), BF16),
    )(x, g)


def _rms_bwd(name, x, g, dn, dres, tm=512):
    S, D = x.shape

    def body(x_ref, g_ref, dn_ref, dres_ref, dx_ref, dxb_ref, dg_ref):
        i = pl.program_id(0)

        @pl.when(i == 0)
        def _():
            dg_ref[...] = jnp.zeros_like(dg_ref)

        xv = x_ref[...]
        r = lax.rsqrt(jnp.mean(xv * xv, axis=-1, keepdims=True) + EPS)
        xhat = xv * r
        dnv = dn_ref[...].astype(F32)
        dg_ref[...] += jnp.sum(dnv * xhat, axis=0, keepdims=True)
        t = dnv * g_ref[...]
        dx = r * (t - xhat * jnp.mean(t * xhat, axis=-1, keepdims=True)) + dres_ref[...]
        dx_ref[...] = dx
        dxb_ref[...] = dx.astype(BF16)

    row = pl.BlockSpec((tm, D), lambda i: (i, 0))
    vec = pl.BlockSpec((1, D), lambda i: (0, 0))
    return pl.pallas_call(
        body, name=name, grid=(S // tm,), in_specs=[row, vec, row, row], out_specs=[row, row, vec],
        out_shape=[jax.ShapeDtypeStruct((S, D), F32), jax.ShapeDtypeStruct((S, D), BF16), jax.ShapeDtypeStruct((1, D), F32)],
    )(x, g, dn, dres)


def _final_loss(h2, g, target, tm=512):
    S, D = h2.shape

    def body(x_ref, g_ref, t_ref, loss_ref, dg_ref, dx_ref, dxb_ref):
        i = pl.program_id(0)

        @pl.when(i == 0)
        def _():
            loss_ref[...] = jnp.zeros_like(loss_ref)
            dg_ref[...] = jnp.zeros_like(dg_ref)

        xv = x_ref[...]
        r = lax.rsqrt(jnp.mean(xv * xv, axis=-1, keepdims=True) + EPS)
        xhat = xv * r
        gv = g_ref[...]
        diff = xhat * gv - t_ref[...]
        per_tok = jnp.mean(diff * diff, axis=-1, keepdims=True)
        loss_ref[...] += 0.5 * jnp.sum(per_tok, axis=0, keepdims=True)
        dy = diff * (1.0 / D)
        dg_ref[...] += jnp.sum(dy * xhat, axis=0, keepdims=True)
        t = dy * gv
        dx = r * (t - xhat * jnp.mean(t * xhat, axis=-1, keepdims=True))
        dx_ref[...] = dx
        dxb_ref[...] = dx.astype(BF16)

    row = pl.BlockSpec((tm, D), lambda i: (i, 0))
    vec = pl.BlockSpec((1, D), lambda i: (0, 0))
    return pl.pallas_call(
        body, name="final_loss", grid=(S // tm,), in_specs=[row, vec, row],
        out_specs=[pl.BlockSpec((1, LANES), lambda i: (0, 0)), vec, row, row],
        out_shape=[jax.ShapeDtypeStruct((1, LANES), F32), jax.ShapeDtypeStruct((1, D), F32),
                   jax.ShapeDtypeStruct((S, D), F32), jax.ShapeDtypeStruct((S, D), BF16)],
    )(h2, g, target)


def _rope_tables(S):
    pos = jnp.arange(S, dtype=F32)
    inv_freq = ROPE_THETA ** (-jnp.arange(0, ROPE_DIM, 2, dtype=F32) / ROPE_DIM)
    ang = pos[:, None] * inv_freq[None, :]
    cos, sin = jnp.cos(ang), jnp.sin(ang)
    half = ROPE_DIM // 2
    rest = HEAD - ROPE_DIM
    z_h, z_r = jnp.zeros((S, half), F32), jnp.zeros((S, rest), F32)
    tab_c = jnp.concatenate([cos, cos, jnp.ones((S, rest), F32)], axis=1)
    tab_up = jnp.concatenate([z_h, sin, z_r], axis=1)
    tab_dn = jnp.concatenate([-sin, z_h, z_r], axis=1)
    return tab_c, tab_up, tab_dn


def _rope_head(t, c, up, dn):
    half = ROPE_DIM // 2
    return t * c + pltpu.roll(t, half, axis=1) * up + pltpu.roll(t, HEAD - half, axis=1) * dn


def _rope_fwd(proj, tabs, tm=512):
    S = proj.shape[0]
    W = 2 * ATTN_W

    def body(p_ref, c_ref, up_ref, dn_ref, o_ref):
        c, up, dn = c_ref[...], up_ref[...], dn_ref[...]
        for h in range(W // HEAD):
            sl = slice(h * HEAD, (h + 1) * HEAD)
            o_ref[:, sl] = _rope_head(p_ref[:, sl].astype(F32), c, up, dn).astype(BF16)

    tab = pl.BlockSpec((tm, HEAD), lambda i: (i, 0))
    return pl.pallas_call(
        body, name="rope_fwd", grid=(S // tm,), in_specs=[pl.BlockSpec((tm, W), lambda i: (i, 0)), tab, tab, tab],
        out_specs=pl.BlockSpec((tm, W), lambda i: (i, 0)), out_shape=jax.ShapeDtypeStruct((S, W), BF16),
    )(proj, *tabs)


def _attn_grad_merge(dq, dk, dv, tabs, tm=256):
    S = dq.shape[0]

    def body(q_ref, k_ref, v_ref, c_ref, up_ref, dn_ref, o_ref):
        c, up, dn = c_ref[...], up_ref[...], dn_ref[...]
        for h in range(N_HEADS):
            sl = slice(h * HEAD, (h + 1) * HEAD)
            for part, r in ((0, q_ref), (1, k_ref)):
                osl = slice(part * ATTN_W + h * HEAD, part * ATTN_W + (h + 1) * HEAD)
                o_ref[:, osl] = _rope_head(r[:, sl].astype(F32), c, -up, -dn).astype(BF16)
        o_ref[:, 2 * ATTN_W:] = v_ref[...]

    blk = pl.BlockSpec((tm, ATTN_W), lambda i: (i, 0))
    tab = pl.BlockSpec((tm, HEAD), lambda i: (i, 0))
    return pl.pallas_call(
        body, name="attn_grad_merge", grid=(S // tm,), in_specs=[blk] * 3 + [tab] * 3,
        out_specs=pl.BlockSpec((tm, 3 * ATTN_W), lambda i: (i, 0)), out_shape=jax.ShapeDtypeStruct((S, IN_MAIN), BF16),
    )(dq, dk, dv, *tabs)


SUB = 128
WIN = SUB + 2 * N_SIDE
Q_COL, K_COL, V_COL = 0, ATTN_W // HEAD, 2 * ATTN_W // HEAD


class _AttnGeo:
    def __init__(self, S, d):
        self.S, self.d, self.L = S, d, S // d
        self.halo = N_SIDE * d
        self.TB = min(2048, S)
        self.W = self.TB + 2 * self.halo
        self.n_sub = self.TB // SUB
        self.grid = (S // self.TB, N_HEADS)
        self.dt = F32 if d > 1 else BF16
        self.su = min(d, 4)
        self.sb = d // self.su
        assert self.TB % (SUB * d) == 0 and self.TB % self.halo == 0

    def specs(self, width, col0, per_head=True):
        ratio = self.TB // self.halo
        last = self.S // self.halo - 1
        col = (lambda h: col0 + h) if per_head else (lambda h: col0)
        cur = pl.BlockSpec((self.TB, width), lambda i, h: (i, col(h)))
        prev = pl.BlockSpec((self.halo, width), lambda i, h: (jnp.maximum(i * ratio - 1, 0), col(h)))
        nxt = pl.BlockSpec((self.halo, width), lambda i, h: (jnp.minimum((i + 1) * ratio, last), col(h)))
        return cur, prev, nxt

    def scratch(self, rows, dtype=None):
        nat = pltpu.VMEM((rows, LANES), self.dt if dtype is None else dtype)
        return [nat] if self.sb == 1 else [nat, pltpu.VMEM((rows, LANES), F32)]

    def bind(self, refs):
        nat = next(refs)
        return (nat, nat) if self.sb == 1 else (nat, next(refs))

    def spread(self, pair):
        nat, streams = pair
        if self.sb > 1:
            n = nat.shape[0] // self.sb
            for a in range(self.sb):
                streams[a * n:(a + 1) * n, :] = nat[pl.ds(a, n, stride=self.sb), :]
        return streams

    def gather(self, pair):
        nat, streams = pair
        if self.sb > 1:
            n = nat.shape[0] // self.sb
            for a in range(self.sb):
                nat[pl.ds(a, n, stride=self.sb), :] = streams[a * n:(a + 1) * n, :]
        return nat

    def rows(self, sub, n, total):
        res, blk = sub % self.d, sub // self.d
        a, b = res % self.sb, res // self.sb
        start = a * (total // self.sb) + b + self.su * SUB * blk
        return pl.ds(start, n, stride=self.su) if self.su > 1 else pl.ds(start, n)

    def band(self):
        row = lax.broadcasted_iota(jnp.int32, (SUB, WIN), 0)
        col = lax.broadcasted_iota(jnp.int32, (SUB, WIN), 1)
        return (col >= row) & (col <= row + 2 * N_SIDE), col

    def mask(self, sub, band):
        inside, col = band
        blk, n_blk = sub // self.d, self.TB // (SUB * self.d)
        base = pl.program_id(0) * (self.TB // self.d) + SUB * blk
        if blk == 0:
            inside = inside & (col >= N_SIDE - base)
        if blk == n_blk - 1:
            inside = inside & (col < self.L + N_SIDE - base)
        return inside

    def fill(self, dst, c_ref):
        dst[...] = c_ref[...].astype(dst.dtype)

    def fill_window(self, dst, p_ref, c_ref, n_ref):
        dst[0:self.halo] = p_ref[...].astype(dst.dtype)
        dst[self.halo:self.halo + self.TB] = c_ref[...].astype(dst.dtype)
        dst[self.halo + self.TB:] = n_ref[...].astype(dst.dtype)


def _lane_of(tile, h):
    lane = lax.broadcasted_iota(jnp.int32, tile.shape, 1)
    return jnp.sum(jnp.where(lane == h, tile, 0.0), axis=1, keepdims=True)


def _attn_fwd(qk, proj, d):
    S = qk.shape[0]
    geo = _AttnGeo(S, d)
    scale = HEAD ** -0.5

    def body(q_ref, kp, kc, kn, vp, vc, vn, o_ref, lse_ref, *scratch):
        h = pl.program_id(1)
        refs = iter(scratch)
        q_p, k_p, v_p, o_p, l_p = (geo.bind(refs) for _ in range(5))
        geo.fill(q_p[0], q_ref)
        geo.fill_window(k_p[0], kp, kc, kn)
        geo.fill_window(v_p[0], vp, vc, vn)
        qs, ks, vs = geo.spread(q_p), geo.spread(k_p), geo.spread(v_p)
        os, ls = o_p[1], l_p[1]
        band = geo.band()
        for sub in range(geo.n_sub):
            rq, rw = geo.rows(sub, SUB, geo.TB), geo.rows(sub, WIN, geo.W)
            q_r, k_r, v_r = qs[rq, :].astype(BF16), ks[rw, :].astype(BF16), vs[rw, :].astype(BF16)
            s = jnp.where(geo.mask(sub, band), _dot(q_r, k_r, NT) * scale, NEG)
            m = jnp.max(s, axis=1, keepdims=True)
            p = jnp.exp(s - m)
            l = jnp.sum(p, axis=1, keepdims=True)
            os[rq, :] = _dot(p.astype(BF16), v_r) / l
            ls[rq, :] = jnp.broadcast_to(m + jnp.log(l), (SUB, LANES))
        o_ref[...] = geo.gather(o_p)[...].astype(BF16)

        @pl.when(h == 0)
        def _():
            lse_ref[...] = jnp.zeros_like(lse_ref)

        lane = lax.broadcasted_iota(jnp.int32, (geo.TB, LANES), 1)
        lse_ref[...] = jnp.where(lane == h, geo.gather(l_p)[...], lse_ref[...])

    q_cur, _, _ = geo.specs(HEAD, Q_COL)
    k_specs = geo.specs(HEAD, K_COL)
    v_specs = geo.specs(HEAD, V_COL)
    stat = pl.BlockSpec((geo.TB, LANES), lambda i, h: (i, 0))
    return pl.pallas_call(
        body, name=f"attn_fwd_d{d}", grid=geo.grid,
        in_specs=[q_cur, k_specs[1], k_specs[0], k_specs[2], v_specs[1], v_specs[0], v_specs[2]],
        out_specs=[q_cur, stat],
        out_shape=[jax.ShapeDtypeStruct((S, ATTN_W), BF16), jax.ShapeDtypeStruct((S, LANES), F32)],
        scratch_shapes=(geo.scratch(geo.TB) + geo.scratch(geo.W) + geo.scratch(geo.W) + geo.scratch(geo.TB, F32)
                        + geo.scratch(geo.TB, F32)),
    )(qk, qk, qk, qk, proj, proj, proj)


def _attn_combine(outs, lses, g, tm=256):
    S = outs[0].shape[0]

    def body(o1, o2, o3, l1, l2, l3, g_ref, ao_ref, o_ref, lse_ref):
        a1, a2, a3 = l1[...], l2[...], l3[...]
        mx = jnp.maximum(jnp.maximum(a1, a2), a3)
        e1, e2, e3 = jnp.exp(a1 - mx), jnp.exp(a2 - mx), jnp.exp(a3 - mx)
        den = e1 + e2 + e3
        lse_ref[...] = mx + jnp.log(den)
        head_of_col = lax.broadcasted_iota(jnp.int32, (LANES, ATTN_W), 1) // HEAD
        spread = (lax.broadcasted_iota(jnp.int32, (LANES, ATTN_W), 0) == head_of_col).astype(BF16)

        def wide(e):
            wgt = e / den
            hi = wgt.astype(BF16)
            lo = (wgt - hi.astype(F32)).astype(BF16)
            return _dot(hi, spread) + _dot(lo, spread)

        ov = wide(e1) * o1[...].astype(F32) + wide(e2) * o2[...].astype(F32) + wide(e3) * o3[...].astype(F32)
        o_ref[...] = ov
        r = lax.rsqrt(jnp.mean(ov * ov, axis=-1, keepdims=True) + EPS)
        ao_ref[...] = (ov * r * g_ref[...]).astype(BF16)

    blk = pl.BlockSpec((tm, ATTN_W), lambda i: (i, 0))
    ls = pl.BlockSpec((tm, LANES), lambda i: (i, 0))
    return pl.pallas_call(
        body, name="attn_combine", grid=(S // tm,),
        in_specs=[blk, blk, blk, ls, ls, ls, pl.BlockSpec((1, ATTN_W), lambda i: (0, 0))], out_specs=[blk, blk, ls],
        out_shape=[jax.ShapeDtypeStruct((S, D_MODEL), BF16), jax.ShapeDtypeStruct((S, ATTN_W), F32),
                   jax.ShapeDtypeStruct((S, LANES), F32)],
    )(*outs, *lses, g)


def _attn_norm_bwd(o, g, dao, tm=256):
    S = o.shape[0]

    def body(o_ref, g_ref, dao_ref, do_ref, dl_ref, dg_ref):
        i = pl.program_id(0)

        @pl.when(i == 0)
        def _():
            dg_ref[...] = jnp.zeros_like(dg_ref)

        ov = o_ref[...]
        r = lax.rsqrt(jnp.mean(ov * ov, axis=-1, keepdims=True) + EPS)
        ohat = ov * r
        dn = dao_ref[...].astype(F32)
        dg_ref[...] += jnp.sum(dn * ohat, axis=0, keepdims=True)
        t = dn * g_ref[...]
        do = r * (t - ohat * jnp.mean(t * ohat, axis=-1, keepdims=True))
        do_ref[...] = do.astype(BF16)
        prod = do * ov
        lane = lax.broadcasted_iota(jnp.int32, (tm, LANES), 1)
        tile = jnp.zeros((tm, LANES), F32)
        for h in range(N_HEADS):
            tile = jnp.where(lane == h, jnp.sum(prod[:, h * HEAD:(h + 1) * HEAD], axis=1, keepdims=True), tile)
        dl_ref[...] = tile

    blk = pl.BlockSpec((tm, ATTN_W), lambda i: (i, 0))
    vec = pl.BlockSpec((1, ATTN_W), lambda i: (0, 0))
    return pl.pallas_call(
        body, name="attn_norm_bwd", grid=(S // tm,),
        in_specs=[blk, vec, pl.BlockSpec((tm, ATTN_W), lambda i: (i, 0))],
        out_specs=[blk, pl.BlockSpec((tm, LANES), lambda i: (i, 0)), vec],
        out_shape=[jax.ShapeDtypeStruct((S, ATTN_W), BF16), jax.ShapeDtypeStruct((S, LANES), F32),
                   jax.ShapeDtypeStruct((1, ATTN_W), F32)],
    )(o, g, dao)


def _attn_bwd_dq(qk, proj, do, lse, delta, d, prev):
    S = qk.shape[0]
    geo = _AttnGeo(S, d)
    scale = HEAD ** -0.5
    before = [] if prev is None else [prev]

    def body(q_ref, kp, kc, kn, vp, vc, vn, do_ref, lse_ref, dl_ref, *rest):
        prev_refs, (dq_ref, *scratch) = rest[:len(before)], rest[len(before):]
        h = pl.program_id(1)
        refs = iter(scratch)
        q_p, k_p, v_p, do_p, lse_p, dl_p, dq_p = (geo.bind(refs) for _ in range(7))
        geo.fill(q_p[0], q_ref)
        geo.fill(do_p[0], do_ref)
        geo.fill(lse_p[0], lse_ref)
        geo.fill(dl_p[0], dl_ref)
        geo.fill_window(k_p[0], kp, kc, kn)
        geo.fill_window(v_p[0], vp, vc, vn)
        qs, ks, vs, dos = geo.spread(q_p), geo.spread(k_p), geo.spread(v_p), geo.spread(do_p)
        lses, dls = geo.spread(lse_p), geo.spread(dl_p)
        dqs = dq_p[1]
        band = geo.band()
        for sub in range(geo.n_sub):
            rq, rw = geo.rows(sub, SUB, geo.TB), geo.rows(sub, WIN, geo.W)
            q_r, k_r, v_r = qs[rq, :].astype(BF16), ks[rw, :].astype(BF16), vs[rw, :].astype(BF16)
            lse_c, dl_c = _lane_of(lses[rq, :], h), _lane_of(dls[rq, :], h)
            s = _dot(q_r, k_r, NT) * scale
            p = jnp.where(geo.mask(sub, band), jnp.exp(s - lse_c), 0.0)
            dp = _dot(dos[rq, :].astype(BF16), v_r, NT)
            ds = (p * (dp - dl_c) * scale).astype(BF16)
            dqs[rq, :] = _dot(ds, k_r)
        total = geo.gather(dq_p)[...]
        for p_ref in prev_refs:
            total = total + p_ref[...].astype(F32)
        dq_ref[...] = total.astype(BF16)

    cur, _, _ = geo.specs(HEAD, 0)
    k_specs = geo.specs(HEAD, K_COL)
    v_specs = geo.specs(HEAD, V_COL)
    stat = pl.BlockSpec((geo.TB, LANES), lambda i, h: (i, 0))
    return pl.pallas_call(
        body, name=f"attn_bwd_dq_d{d}", grid=geo.grid,
        in_specs=[cur, k_specs[1], k_specs[0], k_specs[2], v_specs[1], v_specs[0], v_specs[2], cur, stat, stat]
        + [cur] * len(before),
        out_specs=cur, out_shape=jax.ShapeDtypeStruct((S, ATTN_W), BF16),
        scratch_shapes=(geo.scratch(geo.TB) + geo.scratch(geo.W) + geo.scratch(geo.W) + geo.scratch(geo.TB)
                        + geo.scratch(geo.TB, F32) + geo.scratch(geo.TB, F32) + geo.scratch(geo.TB, F32)),
    )(qk, qk, qk, qk, proj, proj, proj, do, lse, delta, *before)


def _attn_bwd_dkv(qk, proj, do, lse, delta, d, prev):
    S = qk.shape[0]
    geo = _AttnGeo(S, d)
    scale = HEAD ** -0.5
    before = [] if prev is None else list(prev)

    def body(k_ref, v_ref, qp, qc, qn, dop, doc, don, lp, lc, ln, dlp, dlc, dln, *rest):
        prev_refs, (dk_ref, dv_ref, *scratch) = rest[:len(before)], rest[len(before):]
        h = pl.program_id(1)
        refs = iter(scratch)
        k_p, v_p, q_p, do_p, lw_p, dlw_p, dk_p, dv_p = (geo.bind(refs) for _ in range(8))
        geo.fill(k_p[0], k_ref)
        geo.fill(v_p[0], v_ref)
        geo.fill_window(q_p[0], qp, qc, qn)
        geo.fill_window(do_p[0], dop, doc, don)
        geo.fill_window(lw_p[0], lp, lc, ln)
        geo.fill_window(dlw_p[0], dlp, dlc, dln)
        ks, vs, qs, dos = geo.spread(k_p), geo.spread(v_p), geo.spread(q_p), geo.spread(do_p)
        lws, dlws = geo.spread(lw_p), geo.spread(dlw_p)
        dks, dvs = dk_p[1], dv_p[1]
        head = lax.broadcasted_iota(jnp.int32, (LANES, WIN), 0)
        band = geo.band()
        for sub in range(geo.n_sub):
            rq, rw = geo.rows(sub, SUB, geo.TB), geo.rows(sub, WIN, geo.W)
            k_r, v_r = ks[rq, :].astype(BF16), vs[rq, :].astype(BF16)
            q_w, do_w = qs[rw, :].astype(BF16), dos[rw, :].astype(BF16)
            lse_row = jnp.sum(jnp.where(head == h, lws[rw, :].T, 0.0), axis=0, keepdims=True)
            dl_row = jnp.sum(jnp.where(head == h, dlws[rw, :].T, 0.0), axis=0, keepdims=True)
            st = _dot(k_r, q_w, NT) * scale
            pt = jnp.where(geo.mask(sub, band), jnp.exp(st - lse_row), 0.0)
            dvs[rq, :] = _dot(pt.astype(BF16), do_w)
            dpt = _dot(v_r, do_w, NT)
            dst = (pt * (dpt - dl_row) * scale).astype(BF16)
            dks[rq, :] = _dot(dst, q_w)
        dk_tot, dv_tot = geo.gather(dk_p)[...], geo.gather(dv_p)[...]
        if prev_refs:
            dk_tot, dv_tot = dk_tot + prev_refs[0][...].astype(F32), dv_tot + prev_refs[1][...].astype(F32)
        dk_ref[...] = dk_tot.astype(BF16)
        dv_ref[...] = dv_tot.astype(BF16)

    q_specs = geo.specs(HEAD, Q_COL)
    k_cur, _, _ = geo.specs(HEAD, K_COL)
    v_cur, _, _ = geo.specs(HEAD, V_COL)
    do_specs = geo.specs(HEAD, 0)
    st_specs = geo.specs(LANES, 0, per_head=False)
    cur = do_specs[0]
    return pl.pallas_call(
        body, name=f"attn_bwd_dkv_d{d}", grid=geo.grid,
        in_specs=[k_cur, v_cur, q_specs[1], q_specs[0], q_specs[2], do_specs[1], do_specs[0], do_specs[2],
                  st_specs[1], st_specs[0], st_specs[2], st_specs[1], st_specs[0], st_specs[2]] + [cur] * len(before),
        out_specs=[cur, cur],
        out_shape=[jax.ShapeDtypeStruct((S, ATTN_W), BF16), jax.ShapeDtypeStruct((S, ATTN_W), BF16)],
        scratch_shapes=(geo.scratch(geo.TB) + geo.scratch(geo.TB) + geo.scratch(geo.W) + geo.scratch(geo.W)
                        + geo.scratch(geo.W, F32) + geo.scratch(geo.W, F32) + geo.scratch(geo.TB, F32)
                        + geo.scratch(geo.TB, F32)),
    )(qk, proj, qk, qk, qk, do, do, do, lse, lse, lse, delta, delta, delta, *before)


def _cumsum_rows(x, reverse):
    n = x.shape[0]
    row = lax.broadcasted_iota(jnp.int32, x.shape, 0)
    s = 1
    while s < n:
        if reverse:
            x = x + jnp.where(row < n - s, pltpu.roll(x, n - s, axis=0), 0.0)
        else:
            x = x + jnp.where(row >= s, pltpu.roll(x, s, axis=0), 0.0)
        s *= 2
    return x


GLA_GROUP = 8


def _gla_rows(cc):
    return slice(cc * CHUNK, (cc + 1) * CHUNK)


def _gla_chunk_terms(q_ref, k_ref, v_ref, g_ref, h, reverse, rows, b_ref=None):
    ksl = slice(h * GLA_DK, (h + 1) * GLA_DK)
    q = q_ref[rows, ksl].astype(F32) * (GLA_DK ** -0.5)
    k = k_ref[rows, ksl].astype(F32)
    v = v_ref[rows, h * GLA_DV:(h + 1) * GLA_DV]
    b = _cumsum_rows(g_ref[rows, ksl], reverse) if b_ref is None else b_ref[rows, ksl]
    r_ref = CHUNK // 2 if reverse else CHUNK // 2 - 1
    r_last = 0 if reverse else CHUNK - 1
    b_ref, b_last = b[r_ref:r_ref + 1, :], b[r_last:r_last + 1, :]
    ii = lax.broadcasted_iota(jnp.int32, (CHUNK, CHUNK), 0)
    jj = lax.broadcasted_iota(jnp.int32, (CHUNK, CHUNK), 1)
    causal = (jj >= ii) if reverse else (jj <= ii)
    e_q, e_k = jnp.exp(b - b_ref), jnp.exp(b_ref - b)
    e_in, e_st = jnp.exp(b), jnp.exp(b_last - b)
    return dict(q=q, k=k, v=v, b=b, causal=causal, e_q=e_q, e_k=e_k, e_in=e_in, e_st=e_st, dec=jnp.exp(b_last),
                qe=q * e_q, ke=k * e_k, q_in=q * e_in, k_st=k * e_st, r_ref=r_ref, r_last=r_last)


def _gla_specs(order):
    rows = GLA_GROUP * CHUNK
    q = pl.BlockSpec((rows, GLA_K), lambda c: (order(c), 3 * ATTN_W // GLA_K))
    k = pl.BlockSpec((rows, GLA_K), lambda c: (order(c), 3 * ATTN_W // GLA_K + 1))
    v = pl.BlockSpec((rows, GLA_V), lambda c: (order(c), (3 * ATTN_W + 2 * GLA_K) // GLA_V))
    return q, k, v


def _gla_fwd(proj, gates, reverse, o_prev=None):
    S = proj.shape[0]
    n = S // CHUNK
    nb = n // GLA_GROUP
    rows = GLA_GROUP * CHUNK
    order = (lambda c: nb - 1 - c) if reverse else (lambda c: c)
    seq = list(range(GLA_GROUP))[::-1] if reverse else list(range(GLA_GROUP))
    gcol = 1 if reverse else 0

    def body(*refs):
        if o_prev is None:
            q_ref, k_ref, v_ref, g_ref, o_ref, st_ref, a_ref, b_ref, state = refs
        else:
            q_ref, k_ref, v_ref, g_ref, op_ref, o_ref, st_ref, a_ref, b_ref, state = refs
        c = pl.program_id(0)

        @pl.when(c == 0)
        def _():
            state[...] = jnp.zeros_like(state)

        for h in range(GLA_HEADS):
            vsl = slice(h * GLA_DV, (h + 1) * GLA_DV)
            st = state[h]
            for cc in seq:
                rs = _gla_rows(cc)
                t = _gla_chunk_terms(q_ref, k_ref, v_ref, g_ref, h, reverse, rs)
                b_ref[rs, h * GLA_DK:(h + 1) * GLA_DK] = t["b"]
                a = jnp.where(t["causal"], _dot(t["qe"].astype(BF16), t["ke"].astype(BF16), NT), 0.0).astype(BF16)
                a_ref[cc, h] = a
                o = _dot(a, t["v"])
                st_b = st.astype(BF16)
                st_ref[cc, h] = st_b
                o = o + _dot(t["q_in"].astype(BF16), st_b, NT)
                st = st * t["dec"] + _dot(t["v"], t["k_st"].astype(BF16), TN)
                if o_prev is not None:
                    o = o + op_ref[rs, vsl]
                o_ref[rs, vsl] = o
            state[h] = st

    q_spec, k_spec, v_spec = _gla_specs(order)
    o_spec = pl.BlockSpec((rows, GLA_V), lambda c: (order(c), 0))
    in_specs = [q_spec, k_spec, v_spec, pl.BlockSpec((rows, GLA_K), lambda c: (order(c), gcol))]
    operands = [proj, proj, proj, gates]
    if o_prev is not None:
        in_specs.append(o_spec)
        operands.append(o_prev)
    return pl.pallas_call(
        body, name="gla_fwd_rev" if reverse else "gla_fwd", grid=(nb,), in_specs=in_specs,
        out_specs=[o_spec, pl.BlockSpec((GLA_GROUP, GLA_HEADS, GLA_DV, GLA_DK), lambda c: (order(c), 0, 0, 0)),
                   pl.BlockSpec((GLA_GROUP, GLA_HEADS, CHUNK, CHUNK), lambda c: (order(c), 0, 0, 0)),
                   pl.BlockSpec((rows, GLA_K), lambda c: (order(c), 0))],
        out_shape=[jax.ShapeDtypeStruct((S, GLA_V), F32), jax.ShapeDtypeStruct((n, GLA_HEADS, GLA_DV, GLA_DK), BF16),
                   jax.ShapeDtypeStruct((n, GLA_HEADS, CHUNK, CHUNK), BF16), jax.ShapeDtypeStruct((S, GLA_K), F32)],
        scratch_shapes=[pltpu.VMEM((GLA_HEADS, GLA_DV, GLA_DK), F32)],
    )(*operands)


def _gla_bwd(proj, kept, do, reverse, prev=None):
    S = proj.shape[0]
    n = S // CHUNK
    nb = n // GLA_GROUP
    rows = GLA_GROUP * CHUNK
    order = (lambda c: c) if reverse else (lambda c: nb - 1 - c)
    seq = list(range(GLA_GROUP)) if reverse else list(range(GLA_GROUP))[::-1]
    out_dt = F32 if prev is None else BF16

    def body(*refs):
        if prev is None:
            q_ref, k_ref, v_ref, b_ref, st_ref, a_ref, do_ref, dq_ref, dk_ref, dv_ref, dg_ref, dstate = refs
        else:
            q_ref, k_ref, v_ref, b_ref, st_ref, a_ref, do_ref, pq, pk, pv, dq_ref, dk_ref, dv_ref, dg_ref, dstate = refs
        c = pl.program_id(0)

        @pl.when(c == 0)
        def _():
            dstate[...] = jnp.zeros_like(dstate)

        row = lax.broadcasted_iota(jnp.int32, (CHUNK, GLA_DK), 0)
        for h in range(GLA_HEADS):
            ksl = slice(h * GLA_DK, (h + 1) * GLA_DK)
            vsl = slice(h * GLA_DV, (h + 1) * GLA_DV)
            dst = dstate[h]
            for cc in seq:
                rs = _gla_rows(cc)
                t = _gla_chunk_terms(q_ref, k_ref, v_ref, None, h, reverse, rs, b_ref)
                v = t["v"]
                dob = do_ref[rs, vsl].astype(BF16)
                st_b = st_ref[cc, h]
                dst_b = dst.astype(BF16)
                qe_b, ke_b = t["qe"].astype(BF16), t["ke"].astype(BF16)
                q_in_b, k_st_b = t["q_in"].astype(BF16), t["k_st"].astype(BF16)
                da = jnp.where(t["causal"], _dot(dob, v, NT), 0.0).astype(BF16)
                dv = _dot(a_ref[cc, h], dob, TN) + _dot(k_st_b, dst_b, NT)
                dqe = _dot(da, ke_b)
                dke = _dot(da, qe_b, TN)
                dq_in = _dot(dob, st_b)
                dk_st = _dot(v, dst_b)
                ddec = jnp.sum(dst * st_b.astype(F32), axis=0, keepdims=True)
                dst = _dot(dob, q_in_b, TN) + dst * t["dec"]
                dq = (dqe * t["e_q"] + dq_in * t["e_in"]) * (GLA_DK ** -0.5)
                dk = dke * t["e_k"] + dk_st * t["e_st"]
                w_q, w_k = dqe * t["qe"], dke * t["ke"]
                w_st = dk_st * t["k_st"]
                db = w_q - w_k + dq_in * t["q_in"] - w_st
                db_ref = jnp.sum(w_k - w_q, axis=0, keepdims=True)
                db_last = jnp.sum(w_st, axis=0, keepdims=True) + ddec * t["dec"]
                db = db + jnp.where(row == t["r_ref"], db_ref, 0.0) + jnp.where(row == t["r_last"], db_last, 0.0)
                dg_ref[rs, ksl] = _cumsum_rows(db, not reverse)
                if prev is not None:
                    dq, dk, dv = dq + pq[rs, ksl], dk + pk[rs, ksl], dv + pv[rs, vsl]
                dq_ref[rs, ksl] = dq.astype(out_dt)
                dk_ref[rs, ksl] = dk.astype(out_dt)
                dv_ref[rs, vsl] = dv.astype(out_dt)
            dstate[h] = dst

    q_spec, k_spec, v_spec = _gla_specs(order)
    kk = pl.BlockSpec((rows, GLA_K), lambda c: (order(c), 0))
    vv = pl.BlockSpec((rows, GLA_V), lambda c: (order(c), 0))
    states, scores, sums = kept
    in_specs = [q_spec, k_spec, v_spec, kk,
                pl.BlockSpec((GLA_GROUP, GLA_HEADS, GLA_DV, GLA_DK), lambda c: (order(c), 0, 0, 0)),
                pl.BlockSpec((GLA_GROUP, GLA_HEADS, CHUNK, CHUNK), lambda c: (order(c), 0, 0, 0)), vv]
    operands = [proj, proj, proj, sums, states, scores, do]
    if prev is not None:
        in_specs += [kk, kk, vv]
        operands += list(prev)
    return pl.pallas_call(
        body, name="gla_bwd_rev" if reverse else "gla_bwd", grid=(nb,), in_specs=in_specs, out_specs=[kk, kk, vv, kk],
        out_shape=[jax.ShapeDtypeStruct((S, GLA_K), out_dt), jax.ShapeDtypeStruct((S, GLA_K), out_dt),
                   jax.ShapeDtypeStruct((S, GLA_V), out_dt), jax.ShapeDtypeStruct((S, GLA_K), F32)],
        scratch_shapes=[pltpu.VMEM((GLA_HEADS, GLA_DV, GLA_DK), F32)],
    )(*operands)


def _gates_fwd(z, wg, bias, tm=512):
    S = z.shape[0]
    W = 2 * GLA_K

    def body(z_ref, w_ref, b_ref, o_ref):
        zg = _dot(z_ref[...], w_ref[...]) + b_ref[...]
        o_ref[...] = (jnp.minimum(zg, 0.0) - jnp.log(1.0 + jnp.exp(-jnp.abs(zg)))) * (1.0 / GATE_NORM)

    return pl.pallas_call(
        body, name="gates_fwd", grid=(S // tm,),
        in_specs=[pl.BlockSpec((tm, Z_W), lambda i: (i, 0)), pl.BlockSpec((Z_W, W), lambda i: (0, 0)),
                  pl.BlockSpec((1, W), lambda i: (0, 0))],
        out_specs=pl.BlockSpec((tm, W), lambda i: (i, 0)), out_shape=jax.ShapeDtypeStruct((S, W), F32),
    )(z, wg, bias)


def _gates_bwd(z, wg, bias, dg_f, dg_b, tm=512):
    S = z.shape[0]
    W = 2 * GLA_K

    def body(z_ref, w_ref, b_ref, dgf_ref, dgb_ref, dz_ref, dw_ref, db_ref):
        i = pl.program_id(0)

        @pl.when(i == 0)
        def _():
            dw_ref[...] = jnp.zeros_like(dw_ref)
            db_ref[...] = jnp.zeros_like(db_ref)

        zv = z_ref[...]
        zg = _dot(zv, w_ref[...]) + b_ref[...]
        dg = jnp.concatenate([dgf_ref[...], dgb_ref[...]], axis=1)
        dzg = dg * (1.0 / GATE_NORM) * _sigmoid(-zg)
        db_ref[...] += jnp.sum(dzg, axis=0, keepdims=True)
        dzg_b = dzg.astype(BF16)
        dw_ref[...] += _dot(zv, dzg_b, TN)
        dz_ref[...] = _dot(dzg_b, w_ref[...], NT).astype(BF16)

    half = pl.BlockSpec((tm, GLA_K), lambda i: (i, 0))
    return pl.pallas_call(
        body, name="gates_bwd", grid=(S // tm,),
        in_specs=[pl.BlockSpec((tm, Z_W), lambda i: (i, 0)), pl.BlockSpec((Z_W, W), lambda i: (0, 0)),
                  pl.BlockSpec((1, W), lambda i: (0, 0)), half, half],
        out_specs=[pl.BlockSpec((tm, Z_W), lambda i: (i, 0)), pl.BlockSpec((Z_W, W), lambda i: (0, 0)),
                   pl.BlockSpec((1, W), lambda i: (0, 0))],
        out_shape=[jax.ShapeDtypeStruct((S, Z_W), BF16), jax.ShapeDtypeStruct((Z_W, W), F32),
                   jax.ShapeDtypeStruct((1, W), F32)],
    )(z, wg, bias, dg_f, dg_b)


def _gla_out_fwd(o, proj, g, cat, tm=512):
    S = o.shape[0]

    def body(o_ref, gr_ref, g_ref, cat_ref, out_ref):
        gn = g_ref[...]
        for h in range(GLA_HEADS):
            sl = slice(h * GLA_DV, (h + 1) * GLA_DV)
            ov = o_ref[:, sl]
            r = lax.rsqrt(jnp.mean(ov * ov, axis=-1, keepdims=True) + EPS)
            gr = gr_ref[:, sl].astype(F32)
            out_ref[:, sl] = (ov * r * gn * _silu(gr)).astype(BF16)

    blk = pl.BlockSpec((tm, GLA_V), lambda i: (i, 0))
    return pl.pallas_call(
        body, name="gla_out_fwd", grid=(S // tm,),
        in_specs=[blk, pl.BlockSpec((tm, GLA_V), lambda i: (i, (3 * ATTN_W + 2 * GLA_K + GLA_V) // GLA_V)),
                  pl.BlockSpec((1, GLA_DV), lambda i: (0, 0)), ANY],
        out_specs=pl.BlockSpec((tm, GLA_V), lambda i: (i, 1)), out_shape=jax.ShapeDtypeStruct((S, D_MODEL), BF16),
        input_output_aliases={3: 0},
    )(o, proj, g, cat)


def _gla_out_bwd(o, proj, g, dcat, dproj, tm=512):
    S = o.shape[0]

    def body(o_ref, gr_ref, g_ref, dgo_ref, dproj_ref, do_ref, dgr_ref, dg_ref):
        i = pl.program_id(0)

        @pl.when(i == 0)
        def _():
            dg_ref[...] = jnp.zeros_like(dg_ref)

        gn = g_ref[...]
        dg_acc = jnp.zeros((1, GLA_DV), F32)
        for h in range(GLA_HEADS):
            sl = slice(h * GLA_DV, (h + 1) * GLA_DV)
            ov = o_ref[:, sl]
            r = lax.rsqrt(jnp.mean(ov * ov, axis=-1, keepdims=True) + EPS)
            yhat = ov * r
            gr = gr_ref[:, sl].astype(F32)
            sg = _sigmoid(gr)
            dgo = dgo_ref[:, sl].astype(F32)
            dgr_ref[:, sl] = (dgo * (yhat * gn) * (sg * (1.0 + gr * (1.0 - sg)))).astype(BF16)
            dy = dgo * (gr * sg)
            dg_acc = dg_acc + jnp.sum(dy * yhat, axis=0, keepdims=True)
            t = dy * gn
            do_ref[:, sl] = r * (t - yhat * jnp.mean(t * yhat, axis=-1, keepdims=True))
        dg_ref[...] += dg_acc

    blk = pl.BlockSpec((tm, GLA_V), lambda i: (i, 0))
    vec = pl.BlockSpec((1, GLA_DV), lambda i: (0, 0))
    return pl.pallas_call(
        body, name="gla_out_bwd", grid=(S // tm,),
        in_specs=[blk, pl.BlockSpec((tm, GLA_V), lambda i: (i, (3 * ATTN_W + 2 * GLA_K + GLA_V) // GLA_V)), vec,
                  pl.BlockSpec((tm, GLA_V), lambda i: (i, 1)), ANY],
        out_specs=[blk, pl.BlockSpec((tm, GLA_V), lambda i: (i, (3 * ATTN_W + 2 * GLA_K + GLA_V) // GLA_V)), vec],
        out_shape=[jax.ShapeDtypeStruct((S, GLA_V), F32), jax.ShapeDtypeStruct((S, IN_MAIN), BF16),
                   jax.ShapeDtypeStruct((1, GLA_DV), F32)],
        input_output_aliases={4: 1},
    )(o, proj, g, dcat, dproj)


HALO = 16


def _halo_specs(tm, tn, S):
    cur = pl.BlockSpec((tm, tn), lambda j, i: (i, j))
    prev = pl.BlockSpec((HALO, tn), lambda j, i: (jnp.maximum(i * (tm // HALO) - 1, 0), j))
    nxt = pl.BlockSpec((HALO, tn), lambda j, i: (jnp.minimum((i + 1) * (tm // HALO), S // HALO - 1), j))
    return cur, prev, nxt


def _shifted(c_ref, p_ref, n_ref, n_blocks, i=None):
    if i is None:
        i = pl.program_id(1)
    x = c_ref[...].astype(F32)
    tm = x.shape[0]
    row = lax.broadcasted_iota(jnp.int32, x.shape, 0)
    before = p_ref[HALO - 1:HALO, :].astype(F32) * (i > 0).astype(F32)
    after = n_ref[0:1, :].astype(F32) * (i < n_blocks - 1).astype(F32)
    x_m1 = jnp.where(row == 0, before, pltpu.roll(x, 1, axis=0))
    x_p1 = jnp.where(row == tm - 1, after, pltpu.roll(x, tm - 1, axis=0))
    return x, x_m1, x_p1


def _glu_fwd(gp, up, cw, cb, tm=512, tn=1408):
    S = gp.shape[0]
    nb = S // tm

    def body(c_ref, p_ref, n_ref, up_ref, w_ref, b_ref, o_ref, gate_ref):
        x, x_m1, x_p1 = _shifted(c_ref, p_ref, n_ref, nb)
        w = w_ref[...]
        gate = w[0:1, :] * x_m1 + w[1:2, :] * x + w[2:3, :] * x_p1 + b_ref[...]
        gate_ref[...] = gate.astype(BF16)
        o_ref[...] = (_silu(gate) * up_ref[...].astype(F32)).astype(BF16)

    cur, prev, nxt = _halo_specs(tm, tn, S)
    out = jax.ShapeDtypeStruct((S, D_FF), BF16)
    return pl.pallas_call(
        body, name="glu_fwd", grid=(D_FF // tn, nb),
        in_specs=[cur, prev, nxt, cur, pl.BlockSpec((3, tn), lambda j, i: (0, j)), pl.BlockSpec((1, tn), lambda j, i: (0, j))],
        out_specs=[cur, cur], out_shape=[out, out],
    )(gp, gp, gp, up, cw, cb)


def _glu_bwd(gate, gp, up, dact, cw, tm=512, tn=1408):
    S = gp.shape[0]
    nb = S // tm

    def body(g_ref, gb_ref, ga_ref, x_ref, up_ref, upp_ref, upn_ref, da_ref, dap_ref, dan_ref, w_ref,
             dup_ref, dgp_ref, dw_ref, db_ref):
        i = pl.program_id(1)

        @pl.when(i == 0)
        def _():
            dw_ref[...] = jnp.zeros_like(dw_ref)
            db_ref[...] = jnp.zeros_like(db_ref)

        w = w_ref[...]
        w0, w1, w2 = w[0:1, :], w[1:2, :], w[2:3, :]

        def d_gate(g, da, upv):
            sg = _sigmoid(g)
            return sg, da * upv * (sg * (1.0 + g * (1.0 - sg)))

        g = g_ref[...].astype(F32)
        da = da_ref[...].astype(F32)
        sg, dgate = d_gate(g, da, up_ref[...].astype(F32))
        dup_ref[...] = (da * (g * sg)).astype(BF16)

        last = slice(HALO - 1, HALO)
        _, dgate_before = d_gate(gb_ref[...].astype(F32)[last, :], dap_ref[...].astype(F32)[last, :], upp_ref[...].astype(F32)[last, :])
        _, dgate_after = d_gate(ga_ref[...].astype(F32)[0:1, :], dan_ref[...].astype(F32)[0:1, :], upn_ref[...].astype(F32)[0:1, :])
        dgate_before = dgate_before * (i > 0).astype(F32)
        dgate_after = dgate_after * (i < nb - 1).astype(F32)
        row = lax.broadcasted_iota(jnp.int32, dgate.shape, 0)
        dg_m1 = jnp.where(row == 0, dgate_before, pltpu.roll(dgate, 1, axis=0))
        dg_p1 = jnp.where(row == tm - 1, dgate_after, pltpu.roll(dgate, tm - 1, axis=0))
        dgp_ref[...] = (w0 * dg_p1 + w1 * dgate + w2 * dg_m1).astype(BF16)
        x = x_ref[...].astype(F32)
        db_ref[...] += jnp.sum(dgate, axis=0, keepdims=True)
        dw_ref[...] += jnp.concatenate(
            [jnp.sum(dg_p1 * x, axis=0, keepdims=True), jnp.sum(dgate * x, axis=0, keepdims=True),
             jnp.sum(dg_m1 * x, axis=0, keepdims=True)], axis=0)

    cur, prev, nxt = _halo_specs(tm, tn, S)
    w_spec = pl.BlockSpec((3, tn), lambda j, i: (0, j))
    b_spec = pl.BlockSpec((1, tn), lambda j, i: (0, j))
    return pl.pallas_call(
        body, name="glu_bwd", grid=(D_FF // tn, nb),
        in_specs=[cur, prev, nxt, cur, cur, prev, nxt, cur, prev, nxt, w_spec],
        out_specs=[cur, cur, w_spec, b_spec],
        out_shape=[jax.ShapeDtypeStruct((S, D_FF), BF16), jax.ShapeDtypeStruct((S, D_FF), BF16),
                   jax.ShapeDtypeStruct((3, D_FF), F32), jax.ShapeDtypeStruct((1, D_FF), F32)],
    )(gate, gate, gate, gp, up, up, up, dact, dact, dact, cw)


def _local_step(x, target, norm1_g, w_in_t, wg, gate_bias, gla_norm_g, attn_norm_g, w_out, norm2_g,
                w_gate4, w_up4, conv_w, conv_b, w_down, final_norm_g, on_grad=lambda event, arrays: ()):
    S = x.shape[0]
    tabs = _rope_tables(S)

    n1 = _rms_fwd("rms1_fwd", x, norm1_g)
    z_block = IN_MAIN // Z_W
    proj = _mm_nt("in_proj", n1, w_in_t, 1024, 1536, BF16, n_out=IN_MAIN)
    z = _matmul(
        "in_proj_z",
        [(n1, pl.BlockSpec((1024, D_MODEL), lambda i: (i, 0)), w_in_t, pl.BlockSpec((Z_W, D_MODEL), lambda i: (z_block, 0)), NT)],
        (S // 1024,), jax.ShapeDtypeStruct((S, Z_W), BF16), pl.BlockSpec((1024, Z_W), lambda i: (i, 0)), 1)
    qk = _rope_fwd(proj, tabs)
    branch = [_attn_fwd(qk, proj, d) for d in DILATIONS]
    ao, o_attn, lse = _attn_combine([b[0] for b in branch], [b[1] for b in branch], attn_norm_g)
    gates = _gates_fwd(z, wg, gate_bias)
    o_f, *kept_f = _gla_fwd(proj, gates, False)
    o_gla, *kept_b = _gla_fwd(proj, gates, True, o_prev=o_f)
    cat = _gla_out_fwd(o_gla, proj, gla_norm_g, ao)
    h1 = _mm_nn("out_proj", cat, w_out, 1024, 1024, F32, res=x)
    n2 = _rms_fwd("rms2_fwd", h1, norm2_g)
    gp = _mm_nn_sharded("ffn_gate", n2, w_gate4, 1024, BF16)
    up = _mm_nn_sharded("ffn_up", n2, w_up4, 1024, BF16)
    act, gate = _glu_fwd(gp, up, conv_w, conv_b)
    tk = D_FF // N_CHIPS
    h2 = _mm_nn("ffn_down", act, w_down, 1024, 512, F32, res=h1)
    loss_row, d_final_g, dh2, dh2_b = _final_loss(h2, final_norm_g.reshape(1, D_MODEL), target)

    dact = _mm_nt("ffn_down_bwd", dh2_b, w_down, 1024, tk, BF16)
    dup, dgp, d_conv_w, d_conv_b = _glu_bwd(gate, gp, up, dact, conv_w)
    d_w_down = _mm_tn("ffn_down_wgrad", act, dh2_b, 512, D_MODEL, 2048, BF16)
    on_grad("w_down", dict(w_down=d_w_down))
    dgp = _after(dgp, d_w_down)
    d_w_gate4 = _mm_tn("ffn_gate_wgrad", n2, dgp, 1024, tk, 2048, BF16, out3=tk)
    dup = _after(dup, d_w_gate4)
    d_w_up4 = _mm_tn("ffn_up_wgrad", n2, dup, 1024, tk, 2048, BF16, out3=tk)
    held = on_grad("w_gate_w_up", dict(w_gate=d_w_gate4, w_up=d_w_up4))
    dgp = _after(dgp, d_w_up4, *held)
    shard_pairs = [
        (g, pl.BlockSpec((512, tk), functools.partial(lambda s, j, i: (i, s), s)),
         w4, pl.BlockSpec((None, 512, tk), functools.partial(lambda s, j, i: (s, j, 0), s)), NT)
        for g, w4 in ((dgp, w_gate4), (dup, w_up4)) for s in range(N_CHIPS)]
    dn2 = _matmul("ffn_in_bwd", shard_pairs, (D_MODEL // 512, S // 512), jax.ShapeDtypeStruct((S, D_MODEL), BF16),
                  pl.BlockSpec((512, 512), lambda j, i: (i, j)), 1)
    dh1, dh1_b, d_norm2_g = _rms_bwd("rms2_bwd", h1, norm2_g, dn2, dh2)

    d_w_out = _mm_tn("out_proj_wgrad", cat, dh1_b, D_MODEL, 1024, 1024, BF16)
    held = on_grad("w_out", dict(w_out=d_w_out))
    dcat = _mm_nt("out_proj_bwd", _after(dh1_b, d_w_out, *held), w_out, 1024, 1024, BF16)
    do_attn, delta, d_attn_norm_g = _attn_norm_bwd(o_attn, attn_norm_g, dcat)
    dq_a, dkv_a = None, None
    for d in DILATIONS:
        dq_a = _attn_bwd_dq(qk, proj, do_attn, lse, delta, d, dq_a)
        dkv_a = _attn_bwd_dkv(qk, proj, do_attn, lse, delta, d, dkv_a)
    dproj = _attn_grad_merge(dq_a, dkv_a[0], dkv_a[1], tabs)
    held = on_grad("mid", dict(anchor=dproj))
    do_gla, dproj, d_gla_norm_g = _gla_out_bwd(o_gla, proj, gla_norm_g, _after(dcat, *held), dproj)
    dq_f, dk_f, dv_f, dg_f = _gla_bwd(proj, kept_f, do_gla, False)
    dgq, dgk, dgv, dg_b = _gla_bwd(proj, kept_b, do_gla, True, prev=(dq_f, dk_f, dv_f))
    dz, d_wg, d_gate_bias = _gates_bwd(z, wg, gate_bias, dg_f, dg_b)
    dproj = lax.dynamic_update_slice(dproj, jnp.concatenate([dgq, dgk, dgv], axis=1), (0, 3 * ATTN_W))
    d_w_in_t = _mm_tn("in_proj_wgrad", dproj, n1, 768, D_MODEL, 2048, BF16, rows_out=IN_W)
    n_tok = S // 1024
    d_w_in_t = _matmul(
        "in_proj_z_wgrad",
        [(dz, pl.BlockSpec((1024, Z_W), lambda i, j, k: (k, 0)), n1, pl.BlockSpec((1024, D_MODEL), lambda i, j, k: (k, 0)), TN)],
        (1, 1, n_tok), jax.ShapeDtypeStruct((IN_W, D_MODEL), BF16), pl.BlockSpec((Z_W, D_MODEL), lambda i, j, k: (z_block, 0)),
        n_tok, into=d_w_in_t)
    held = on_grad("w_in", dict(w_in_t=d_w_in_t))
    n_rows = S // 1024
    head_rows = max(1, n_rows // 4)

    def in_proj_bwd(name, first, count, a, into):
        return _matmul(
            name,
            [(a, pl.BlockSpec((1024, IN_MAIN), lambda j, i: (i + first, 0)), w_in_t, pl.BlockSpec((IN_MAIN, 512), lambda j, i: (0, j)), NN),
             (dz, pl.BlockSpec((1024, Z_W), lambda j, i: (i + first, 0)), w_in_t, pl.BlockSpec((Z_W, 512), lambda j, i: (z_block, j)), NN)],
            (D_MODEL // 512, count), jax.ShapeDtypeStruct((S, D_MODEL), BF16),
            pl.BlockSpec((1024, 512), lambda j, i: (i + first, j)), 1, into=into)

    dproj = _after(dproj, d_w_in_t, *held)
    dn1 = in_proj_bwd("in_proj_bwd_a", 0, head_rows, dproj, None)
    held = on_grad("last", dict(last=dn1))
    dn1 = in_proj_bwd("in_proj_bwd_b", head_rows, n_rows - head_rows, dproj, _after(dn1, *held))
    grad_x, _, d_norm1_g = _rms_bwd("rms1_bwd", x, norm1_g, dn1, dh1)

    big = dict(w_in_t=d_w_in_t, w_out=d_w_out, w_gate4=d_w_gate4, w_up4=d_w_up4, w_down=d_w_down)
    small = dict(loss=loss_row, norm1_g=d_norm1_g, wg=d_wg, gate_bias=d_gate_bias, gla_norm_g=d_gla_norm_g,
                 attn_norm_g=d_attn_norm_g, norm2_g=d_norm2_g, conv_w=d_conv_w, conv_b=d_conv_b, final_norm_g=d_final_g)
    return grad_x, big, small


def _position():
    return lax.axis_index("x"), lax.axis_index("y"), lax.axis_index("c")


def _other_chips(x, y):
    return [(1 - x, y), (x, 1 - y), (1 - x, 1 - y)]


def _gather_chips_async(name, shards, collective_id):
    n = len(shards)

    def body(*refs):
        ins, outs = refs[:n], refs[n:2 * n]
        send, recv, loc = refs[2 * n:]
        x, y, c = _position()
        me = 2 * x + y
        chips = _other_chips(x, y)
        barrier = pltpu.get_barrier_semaphore()
        for px, py in chips:
            pl.semaphore_signal(barrier, inc=1, device_id=(px, py, c), device_id_type=MESH)
        pl.semaphore_wait(barrier, len(chips))
        started = []
        for w in range(n):
            own = pltpu.make_async_copy(ins[w], outs[w].at[me], loc.at[w])
            own.start()
            started.append(own)
        sends = []
        for w in range(n):
            for j, (px, py) in enumerate(chips):
                cp = pltpu.make_async_remote_copy(ins[w], outs[w].at[me], send.at[3 * w + j], recv.at[3 * w + j],
                                                  device_id=(px, py, c), device_id_type=MESH)
                cp.start()
                sends.append(cp)
        for w in range(n):
            for j, (px, py) in enumerate(chips):
                pltpu.make_async_remote_copy(ins[w], outs[w].at[2 * px + py], send.at[3 * w + j], recv.at[3 * w + j],
                                             device_id=(px, py, c), device_id_type=MESH).wait_recv()
        for cp in sends:
            cp.wait_send()
        for own in started:
            own.wait()

    return pl.kernel(
        body, name=name, mesh=_sequencer(),
        out_type=[jax.ShapeDtypeStruct((N_CHIPS,) + s.shape, s.dtype) for s in shards],
        scratch_types=[pltpu.SemaphoreType.DMA((3 * n,)), pltpu.SemaphoreType.DMA((3 * n,)), pltpu.SemaphoreType.DMA((n,))],
        compiler_params=pltpu.CompilerParams(collective_id=collective_id),
    )(*shards)


def _gather_halves_async(name, small, shard, collective_id):
    half = shard.shape[1] // 2

    def body(small_ref, shard_ref, small_out, out, send, recv, loc):
        x, y, c = _position()
        me = 2 * x + y
        sibling = (x, y, 1 - c)
        chips = _other_chips(x, y)
        barrier = pltpu.get_barrier_semaphore()
        for px, py in chips:
            pl.semaphore_signal(barrier, inc=1, device_id=(px, py, c), device_id_type=MESH)
        pl.semaphore_signal(barrier, inc=1, device_id=sibling, device_id_type=MESH)
        pl.semaphore_wait(barrier, len(chips) + 1)
        mine = pl.ds(pl.multiple_of(c * half, LANES), half)
        theirs = pl.ds(pl.multiple_of((1 - c) * half, LANES), half)
        own = [pltpu.make_async_copy(small_ref, small_out.at[me], loc.at[0]),
               pltpu.make_async_copy(shard_ref, out.at[me], loc.at[1])]
        for cp in own:
            cp.start()
        sends = []
        for j, (px, py) in enumerate(chips):
            sends.append(pltpu.make_async_remote_copy(small_ref, small_out.at[me], send.at[j], recv.at[j],
                                                      device_id=(px, py, c), device_id_type=MESH))
            sends.append(pltpu.make_async_remote_copy(shard_ref.at[:, mine], out.at[me, :, mine], send.at[3 + j], recv.at[3 + j],
                                                      device_id=(px, py, c), device_id_type=MESH))
        for cp in sends:
            cp.start()
        passed = []
        for j, (px, py) in enumerate(chips):
            slot = 2 * px + py
            pltpu.make_async_remote_copy(shard_ref.at[:, mine], out.at[slot, :, mine], send.at[3 + j], recv.at[3 + j],
                                         device_id=(px, py, c), device_id_type=MESH).wait_recv()
            cp = pltpu.make_async_remote_copy(out.at[slot, :, mine], out.at[slot, :, mine], send.at[6 + j], recv.at[6 + j],
                                              device_id=sibling, device_id_type=MESH)
            cp.start()
            passed.append(cp)
        for j, (px, py) in enumerate(chips):
            slot = 2 * px + py
            pltpu.make_async_remote_copy(small_ref, small_out.at[slot], send.at[j], recv.at[j],
                                         device_id=(px, py, c), device_id_type=MESH).wait_recv()
            pltpu.make_async_remote_copy(out.at[slot, :, theirs], out.at[slot, :, theirs], send.at[6 + j], recv.at[6 + j],
                                         device_id=sibling, device_id_type=MESH).wait_recv()
        for cp in sends + passed:
            cp.wait_send()
        for cp in own:
            cp.wait()

    return pl.kernel(
        body, name=name, mesh=_sequencer(),
        out_type=[jax.ShapeDtypeStruct((N_CHIPS,) + small.shape, small.dtype),
                  jax.ShapeDtypeStruct((N_CHIPS,) + shard.shape, shard.dtype)],
        scratch_types=[pltpu.SemaphoreType.DMA((9,)), pltpu.SemaphoreType.DMA((9,)), pltpu.SemaphoreType.DMA((2,))],
        compiler_params=pltpu.CompilerParams(collective_id=collective_id),
    )(small, shard)


def _sequencer():
    return plsc.ScalarSubcoreMesh(axis_name="sequencer", num_cores=1)


def _sibling_exchange_async(name, arrs, collective_id):
    n = len(arrs)

    def body(*refs):
        ins, outs = refs[:n], refs[n:2 * n]
        send, recv = refs[2 * n:]
        x, y, c = _position()
        sibling = (x, y, 1 - c)
        barrier = pltpu.get_barrier_semaphore()
        pl.semaphore_signal(barrier, inc=1, device_id=sibling, device_id_type=MESH)
        pl.semaphore_wait(barrier, 1)
        copies = [pltpu.make_async_remote_copy(ins[w], outs[w], send.at[w], recv.at[w], device_id=sibling,
                                               device_id_type=MESH) for w in range(n)]
        for cp in copies:
            cp.start()
        for cp in copies:
            cp.wait()

    return pl.kernel(
        body, name=name, out_type=[jax.ShapeDtypeStruct(a.shape, a.dtype) for a in arrs],
        scratch_types=[pltpu.SemaphoreType.DMA((n,)), pltpu.SemaphoreType.DMA((n,))],
        compiler_params=pltpu.CompilerParams(collective_id=collective_id), mesh=_sequencer(),
    )(*arrs)


def _scatter_chips_async(name, parts, collective_id):
    n = len(parts)

    def body(*refs):
        ins, outs = refs[:n], refs[n:2 * n]
        send, recv, loc = refs[2 * n:]
        x, y, c = _position()
        me = 2 * x + y
        chips = _other_chips(x, y)
        barrier = pltpu.get_barrier_semaphore()
        for px, py in chips:
            pl.semaphore_signal(barrier, inc=1, device_id=(px, py, c), device_id_type=MESH)
        pl.semaphore_wait(barrier, len(chips))
        started = []
        for w in range(n):
            own = pltpu.make_async_copy(ins[w].at[me], outs[w].at[me], loc.at[w])
            own.start()
            started.append(own)
        sends = []
        for w in range(n):
            for j, (px, py) in enumerate(chips):
                cp = pltpu.make_async_remote_copy(ins[w].at[2 * px + py], outs[w].at[me], send.at[3 * w + j],
                                                  recv.at[3 * w + j], device_id=(px, py, c), device_id_type=MESH)
                cp.start()
                sends.append(cp)
        for w in range(n):
            for j, (px, py) in enumerate(chips):
                pltpu.make_async_remote_copy(ins[w].at[me], outs[w].at[2 * px + py], send.at[3 * w + j], recv.at[3 * w + j],
                                             device_id=(px, py, c), device_id_type=MESH).wait_recv()
        for cp in sends:
            cp.wait_send()
        for own in started:
            own.wait()

    return pl.kernel(
        body, name=name, out_type=[jax.ShapeDtypeStruct(p.shape, p.dtype) for p in parts],
        scratch_types=[pltpu.SemaphoreType.DMA((3 * n,)), pltpu.SemaphoreType.DMA((3 * n,)), pltpu.SemaphoreType.DMA((n,))],
        compiler_params=pltpu.CompilerParams(collective_id=collective_id), mesh=_sequencer(),
    )(*parts)


def _allreduce_rows(buf):
    R = buf.shape[0]

    def body(in_ref, out_ref, land, send, recv):
        x, y, c = _position()
        me = 4 * x + 2 * y + c
        land[pl.ds(me, 1)] = in_ref[...][None]
        peers = []
        for mask in range(1, N_DEV):
            px = 1 - x if mask & 4 else x
            py = 1 - y if mask & 2 else y
            pc = 1 - c if mask & 1 else c
            peers.append((px, py, pc))
        sends = []
        for k, peer in enumerate(peers):
            cp = pltpu.make_async_remote_copy(in_ref, land.at[me], send.at[k], recv.at[k], device_id=peer, device_id_type=MESH)
            cp.start()
            sends.append(cp)
        for k, (px, py, pc) in enumerate(peers):
            pltpu.make_async_remote_copy(in_ref, land.at[4 * px + 2 * py + pc], send.at[k], recv.at[k],
                                         device_id=(px, py, pc), device_id_type=MESH).wait_recv()
        for cp in sends:
            cp.wait_send()
        tot = land[0]
        for i in range(1, N_DEV):
            tot = tot + land[i]
        out_ref[...] = tot

    vm = pl.BlockSpec(memory_space=pltpu.VMEM)
    return pl.pallas_call(
        body, name="allreduce_small", in_specs=[vm], out_specs=vm, out_shape=jax.ShapeDtypeStruct((R, LANES), F32),
        scratch_shapes=[pltpu.VMEM((N_DEV, R, LANES), F32), pltpu.SemaphoreType.DMA((N_DEV - 1,)),
                        pltpu.SemaphoreType.DMA((N_DEV - 1,))],
    )(buf)


def _tile2d(r, c, cap):
    if r <= cap:
        return r, c
    fits = [t for t in range(16, cap + 1, 16) if r % t == 0]
    return (max(fits), c) if fits else (r, 256)


def _pair_sum(name, a, b):
    n, r, c = a.shape
    tr, tc = _tile2d(r, c, 1024)

    def body(a_ref, b_ref, o_ref):
        o_ref[...] = (a_ref[...].astype(F32) + b_ref[...].astype(F32)).astype(BF16)

    blk = pl.BlockSpec((None, tr, tc), lambda s, i, j: (s, i, j))
    return pl.pallas_call(
        body, name=name, grid=(n, r // tr, c // tc), in_specs=[blk, blk], out_specs=blk,
        out_shape=jax.ShapeDtypeStruct(a.shape, BF16),
    )(a, b)


def _adamw_math(w, m, v, g):
    m2 = ADAM_B1 * m + (1.0 - ADAM_B1) * g
    v2 = ADAM_B2 * v + (1.0 - ADAM_B2) * (g * g)
    m_hat = m2 / (1.0 - ADAM_B1 ** ADAM_STEP)
    v_hat = v2 / (1.0 - ADAM_B2 ** ADAM_STEP)
    delta = -ADAM_LR * (m_hat / (jnp.sqrt(v_hat) + ADAM_EPS) + ADAM_WD * w)
    return delta, m2, v2


def _adamw(name, w, m, v, g):
    r, c = w.shape
    stacked = g.ndim == 3
    tr, tc = _tile2d(r, c, 256)

    def body(w_ref, m_ref, v_ref, g_ref, go_ref, d_ref, m2_ref, v2_ref):
        if stacked:
            gv = g_ref[0].astype(F32)
            for i in range(1, N_CHIPS):
                gv = gv + g_ref[i].astype(F32)
        else:
            gv = g_ref[...]
        delta, m2, v2 = _adamw_math(w_ref[...], m_ref[...], v_ref[...], gv)
        go_ref[...] = gv
        d_ref[...] = delta
        m2_ref[...] = m2
        v2_ref[...] = v2

    blk = pl.BlockSpec((tr, tc), lambda i, j: (i, j))
    g_spec = pl.BlockSpec((N_CHIPS, tr, tc), lambda i, j: (0, i, j)) if stacked else blk
    out = jax.ShapeDtypeStruct((r, c), F32)
    return pl.pallas_call(
        body, name=name, grid=(r // tr, c // tc), in_specs=[blk, blk, blk, g_spec], out_specs=[blk] * 4, out_shape=[out] * 4,
    )(w, m, v, g)


def _pack_rows(pieces):
    flat = jnp.concatenate([p.reshape(-1) for p in pieces])
    rows = flat.shape[0] // LANES
    pad = (-rows) % 8
    return jnp.pad(flat.reshape(rows, LANES), ((0, pad), (0, 0)))


def _unpack_rows(buf, shapes):
    flat = buf.reshape(-1)
    out, at = [], 0
    for s in shapes:
        size = math.prod(s)
        out.append(flat[at:at + size].reshape(s))
        at += size
    return out


SMALL_NAMES = ("norm1_g", "gf_up", "gf_b", "gb_up", "gb_b", "gla_norm_g", "attn_norm_g", "norm2_g", "conv_w", "conv_b",
               "final_norm_g")
BIG_NAMES = ("w_in", "w_out", "w_gate", "w_up", "w_down")
WEIGHT_ORDER = ("norm1_g", "w_in", "gf_up", "gf_b", "gb_up", "gb_b", "gla_norm_g", "attn_norm_g", "w_out", "norm2_g",
                "w_gate", "w_up", "conv_w", "conv_b", "w_down", "final_norm_g")


def kernel(x, norm1_g, w_in, gf_up, gf_b, gb_up, gb_b, gla_norm_g, attn_norm_g, w_out, norm2_g, w_gate, w_up, conv_w, conv_b, w_down, final_norm_g, loss_target, m_norm1_g, m_w_in, m_gf_up, m_gf_b, m_gb_up, m_gb_b, m_gla_norm_g, m_attn_norm_g, m_w_out, m_norm2_g, m_w_gate, m_w_up, m_conv_w, m_conv_b, m_w_down, m_final_norm_g, v_norm1_g, v_w_in, v_gf_up, v_gf_b, v_gb_up, v_gb_b, v_gla_norm_g, v_attn_norm_g, v_w_out, v_norm2_g, v_w_gate, v_w_up, v_conv_w, v_conv_b, v_w_down, v_final_norm_g):
    w = dict(norm1_g=norm1_g, w_in=w_in, gf_up=gf_up, gf_b=gf_b, gb_up=gb_up, gb_b=gb_b, gla_norm_g=gla_norm_g,
             attn_norm_g=attn_norm_g, w_out=w_out, norm2_g=norm2_g, w_gate=w_gate, w_up=w_up, conv_w=conv_w, conv_b=conv_b,
             w_down=w_down, final_norm_g=final_norm_g)
    m = dict(norm1_g=m_norm1_g, w_in=m_w_in, gf_up=m_gf_up, gf_b=m_gf_b, gb_up=m_gb_up, gb_b=m_gb_b, gla_norm_g=m_gla_norm_g,
             attn_norm_g=m_attn_norm_g, w_out=m_w_out, norm2_g=m_norm2_g, w_gate=m_w_gate, w_up=m_w_up, conv_w=m_conv_w,
             conv_b=m_conv_b, w_down=m_w_down, final_norm_g=m_final_norm_g)
    v = dict(norm1_g=v_norm1_g, w_in=v_w_in, gf_up=v_gf_up, gf_b=v_gf_b, gb_up=v_gb_up, gb_b=v_gb_b, gla_norm_g=v_gla_norm_g,
             attn_norm_g=v_attn_norm_g, w_out=v_w_out, norm2_g=v_norm2_g, w_gate=v_w_gate, w_up=v_w_up, conv_w=v_conv_w,
             conv_b=v_conv_b, w_down=v_w_down, final_norm_g=v_final_norm_g)
    S = x.shape[1]
    chip = 2 * lax.axis_index("x") + lax.axis_index("y")
    n_in = IN_W // N_CHIPS
    n_ff = D_FF // N_CHIPS
    n_gk = GLA_K // N_CHIPS

    def owned(t):
        return {k: (jnp.transpose(t[k][0]) if k == "w_in" else t[k][0]) for k in BIG_NAMES}

    own_w, own_m, own_v = owned(w), owned(m), owned(v)
    shard = {k: own_w[k].astype(BF16) for k in BIG_NAMES}
    small_shard = _pack_rows([gf_up[0], gb_up[0], conv_w[0]])
    small4, w_in4 = _gather_halves_async("gather_w_in", small_shard, shard["w_in"], 0)
    w_out4, w_gate4, w_up4 = _gather_chips_async("gather_w_mid", [shard["w_out"], shard["w_gate"], shard["w_up"]], 1)
    (w_down4,) = _gather_chips_async("gather_w_down", [shard["w_down"]], 2)
    w_in_t = w_in4.reshape(IN_W, D_MODEL)
    rows_up = GATE_RANK * n_gk // LANES
    rows_cw = 3 * n_ff // LANES
    gf_full = jnp.transpose(small4[:, 0:rows_up].reshape(N_CHIPS, GATE_RANK, n_gk), (1, 0, 2)).reshape(GATE_RANK, GLA_K)
    gb_full = jnp.transpose(small4[:, rows_up:2 * rows_up].reshape(N_CHIPS, GATE_RANK, n_gk), (1, 0, 2)).reshape(GATE_RANK, GLA_K)
    cw_full = jnp.transpose(small4[:, 2 * rows_up:2 * rows_up + rows_cw].reshape(N_CHIPS, 3, n_ff), (1, 0, 2)).reshape(3, D_FF)
    wg = jnp.zeros((Z_W, 2 * GLA_K), F32)
    wg = wg.at[0:GATE_RANK, 0:GLA_K].set(gf_full).at[GATE_RANK:2 * GATE_RANK, GLA_K:].set(gb_full).astype(BF16)
    gate_bias = jnp.concatenate([gf_b, gb_b], axis=1)

    pending, contributions, next_id = [], {}, [3]

    def as_shards(group, arrays):
        if group == "w_in":
            return dict(w_in=arrays["w_in_t"].reshape(N_CHIPS, n_in, D_MODEL))
        if group == "w_out":
            return dict(w_out=arrays["w_out"].reshape(N_CHIPS, D_MODEL // N_CHIPS, D_MODEL))
        if group == "w_down":
            return dict(w_down=arrays["w_down"].reshape(N_CHIPS, n_ff, D_MODEL))
        return arrays

    out = {}

    def swap(group, arrays):
        mine = as_shards(group, arrays)
        pending.append((group, mine, _sibling_exchange_async(f"sibling_{group}", list(mine.values()), next_id[0])))
        next_id[0] += 1

    def sum_and_send(anchor):
        tag, mine, theirs = pending.pop()
        sums = [_pair_sum(f"pair_sum_{k}", mine[k], _after(t, *anchor)) for k, t in zip(mine, theirs)]
        contributions.update(zip(mine, _scatter_chips_async(f"scatter_{tag}", sums, next_id[0])))
        next_id[0] += 1
        return sums

    def update(names, anchor):
        for k in names:
            res = _adamw(f"adamw_{k}", own_w[k], own_m[k], own_v[k], _after(contributions[k], *anchor))
            out[k] = [(jnp.transpose(r) if k == "w_in" else r)[None] for r in res]
        return [out[k][0] for k in names]

    def on_grad(event, arrays):
        anchor = list(arrays.values())
        held = []
        if event in ("w_gate_w_up", "w_out", "mid", "last"):
            held += sum_and_send(anchor)
        if event == "mid":
            held += update(("w_down", "w_gate", "w_up"), anchor)
        if event in ("w_down", "w_gate_w_up", "w_out", "w_in"):
            swap(event, arrays)
        return held

    grad_x, _, small = _local_step(
        x[0], loss_target[0], norm1_g, w_in_t, wg, gate_bias, gla_norm_g, attn_norm_g,
        w_out4.reshape(D_MODEL, D_MODEL), norm2_g, w_gate4, w_up4, cw_full, conv_b, w_down4.reshape(D_FF, D_MODEL), final_norm_g,
        on_grad=on_grad)
    update(("w_out", "w_in"), [grad_x])

    d_gf_up = small["wg"][0:GATE_RANK, 0:GLA_K]
    d_gb_up = small["wg"][GATE_RANK:2 * GATE_RANK, GLA_K:]
    pieces = [small["loss"], small["norm1_g"], d_gf_up, small["gate_bias"][:, :GLA_K], d_gb_up, small["gate_bias"][:, GLA_K:],
              small["gla_norm_g"], small["attn_norm_g"], small["norm2_g"], small["conv_w"], small["conv_b"], small["final_norm_g"]]
    total = _allreduce_rows(_pack_rows(pieces))
    summed = _unpack_rows(total, [p.shape for p in pieces])
    loss = summed[0][0, 0]
    g_small = dict(zip(SMALL_NAMES, summed[1:]))
    g_small["gf_up"] = lax.dynamic_slice_in_dim(g_small["gf_up"], chip * n_gk, n_gk, axis=1)
    g_small["gb_up"] = lax.dynamic_slice_in_dim(g_small["gb_up"], chip * n_gk, n_gk, axis=1)
    g_small["conv_w"] = lax.dynamic_slice_in_dim(g_small["conv_w"], chip * n_ff, n_ff, axis=1)
    packed = [_pack_rows([t[k] for k in SMALL_NAMES]) for t in (w, m, v, g_small)]
    res = _adamw("adamw_small", *packed)
    shapes = [w[k].shape for k in SMALL_NAMES]
    for k, vals in zip(SMALL_NAMES, zip(*[_unpack_rows(r, shapes) for r in res])):
        out[k] = list(vals)

    grads, deltas, new_m, new_v = ([out[k][i] for k in WEIGHT_ORDER] for i in range(4))
    return (loss, grad_x[None], *grads, *deltas, *new_m, *new_v)
```

```python
import functools
import math

import jax
import jax.numpy as jnp
from jax import lax
from jax.experimental import pallas as pl
from jax.experimental.pallas import tpu as pltpu
from jax.experimental.pallas import tpu_sc as plsc

F32 = jnp.float32
BF16 = jnp.bfloat16

D_MODEL = 2048
ATTN_W = 1024
HEAD = 128
N_HEADS = 8
N_SIDE = 64
DILATIONS = (1, 4, 16)
ROPE_THETA = 500000.0
ROPE_DIM = 32
GLA_K = 512
GLA_V = 1024
GLA_HEADS = 4
GLA_DK = 128
GLA_DV = 256
GATE_RANK = 16
GATE_NORM = 16.0
CHUNK = 64
IN_MAIN = 6144
IN_W = 6176
Z_W = IN_W - IN_MAIN
D_FF = 5632
EPS = 1e-6
N_CHIPS = 4
N_DEV = 8
LANES = 128

ADAM_LR = 0.001
ADAM_B1 = 0.9
ADAM_B2 = 0.999
ADAM_EPS = 1e-08
ADAM_WD = 0.01
ADAM_STEP = 10

NEG = -1e30
MESH = pl.DeviceIdType.MESH
ANY = pl.BlockSpec(memory_space=pl.ANY)

NN = ((1,), (0,))
NT = ((1,), (1,))
TN = ((0,), (0,))


def _dot(a, b, dims=NN):
    return lax.dot_general(a, b, (dims, ((), ())), preferred_element_type=F32)


def _sigmoid(x):
    return 0.5 * jnp.tanh(0.5 * x) + 0.5


def _silu(x):
    h = 0.5 * x
    return h * jnp.tanh(h) + h


def _after(x, *deps):
    return lax.optimization_barrier((x,) + deps)[0]


def _matmul(name, pairs, grid, out_shape, out_spec, nk, res=None, into=None, first=None):
    n_in = 2 * len(pairs) + (res is not None)
    dims = [p[4] for p in pairs]

    n_ops = n_in + (into is not None) + 2 * (first is not None)

    def body(*refs):
        ins, o_ref = refs[:n_in], refs[n_ops]

        def partial_sum():
            tot = None
            for p, dn in enumerate(dims):
                a, b = ins[2 * p][...], ins[2 * p + 1][...]
                t = _dot(a.astype(BF16), b.astype(BF16), dn)
                tot = t if tot is None else tot + t
            return tot

        if nk == 1:
            t = partial_sum()
            if res is not None:
                t = t + ins[-1][...]
            o_ref[...] = t.astype(o_ref.dtype)
        else:
            acc_ref = refs[n_ops + 1]
            k = pl.program_id(2)

            @pl.when(k == 0)
            def _():
                if first is not None:
                    start = _dot(refs[n_in][...].astype(BF16), refs[n_in + 1][...].astype(BF16), first[4])
                    acc_ref[...] = start + ins[-1][...] if res is not None else start
                elif res is not None:
                    acc_ref[...] = ins[-1][...]
                else:
                    acc_ref[...] = jnp.zeros_like(acc_ref)

            acc_ref[...] += partial_sum()

            @pl.when(k == nk - 1)
            def _():
                o_ref[...] = acc_ref[...].astype(o_ref.dtype)

    operands, in_specs = [], []
    for a, a_spec, b, b_spec, _ in pairs:
        operands += [a, b]
        in_specs += [a_spec, b_spec]
    if res is not None:
        operands.append(res[0])
        in_specs.append(res[1])
    if first is not None:
        assert nk > 1
        operands += [first[0], first[2]]
        in_specs += [first[1], first[3]]
    acc_shape = tuple(s for s in out_spec.block_shape if s is not None)
    scratch = [pltpu.VMEM(acc_shape, F32)] if nk > 1 else []
    aliases = {}
    if into is not None:
        aliases = {len(operands): 0}
        operands.append(into)
        in_specs.append(ANY)
    return pl.pallas_call(
        body, name=name, grid=grid, in_specs=in_specs, out_specs=out_spec, out_shape=out_shape, scratch_shapes=scratch,
        input_output_aliases=aliases,
    )(*operands)


def _mm_nn(name, a, b, tm, tn, out_dtype, res=None):
    M, K = a.shape
    N = b.shape[1]
    pairs = [(a, pl.BlockSpec((tm, K), lambda j, i: (i, 0)), b, pl.BlockSpec((K, tn), lambda j, i: (0, j)), NN)]
    r = None if res is None else (res, pl.BlockSpec((tm, tn), lambda j, i: (i, j)))
    return _matmul(name, pairs, (N // tn, M // tm), jax.ShapeDtypeStruct((M, N), out_dtype),
                   pl.BlockSpec((tm, tn), lambda j, i: (i, j)), 1, r)


def _mm_nn_sharded(name, a, b4, tm, out_dtype):
    M, K = a.shape
    n = b4.shape[2]
    pairs = [(a, pl.BlockSpec((tm, K), lambda j, i: (i, 0)), b4, pl.BlockSpec((None, K, n), lambda j, i: (j, 0, 0)), NN)]
    return _matmul(name, pairs, (N_CHIPS, M // tm), jax.ShapeDtypeStruct((M, N_CHIPS * n), out_dtype),
                   pl.BlockSpec((tm, n), lambda j, i: (i, j)), 1)


def _mm_nt(name, a, b, tm, tn, out_dtype, res=None, n_out=None):
    M, K = a.shape
    N = b.shape[0] if n_out is None else n_out
    pairs = [(a, pl.BlockSpec((tm, K), lambda j, i: (i, 0)), b, pl.BlockSpec((tn, K), lambda j, i: (j, 0)), NT)]
    r = None if res is None else (res, pl.BlockSpec((tm, tn), lambda j, i: (i, j)))
    return _matmul(name, pairs, (N // tn, M // tm), jax.ShapeDtypeStruct((M, N), out_dtype),
                   pl.BlockSpec((tm, tn), lambda j, i: (i, j)), 1, r)


def _mm_tn(name, a, g, tka, tn, tmm, out_dtype, out3=None, rows_out=None):
    M, Ka = a.shape
    N = g.shape[1]
    pairs = [(a, pl.BlockSpec((tmm, tka), lambda i, j, k: (k, i)), g, pl.BlockSpec((tmm, tn), lambda i, j, k: (k, j)), TN)]
    if out3 is None:
        shape, spec = (Ka if rows_out is None else rows_out, N), pl.BlockSpec((tka, tn), lambda i, j, k: (i, j))
    else:
        shape, spec = (N // out3, Ka, out3), pl.BlockSpec((None, tka, tn), lambda i, j, k: (j, i, 0))
    return _matmul(name, pairs, (Ka // tka, N // tn, M // tmm), jax.ShapeDtypeStruct(shape, out_dtype), spec, M // tmm)


def _rms_fwd(name, x, g, tm=512):
    S, D = x.shape

    def body(x_ref, g_ref, o_ref):
        xv = x_ref[...]
        r = lax.rsqrt(jnp.mean(xv * xv, axis=-1, keepdims=True) + EPS)
        o_ref[...] = (xv * r * g_ref[...]).astype(o_ref.dtype)

    return pl.pallas_call(
        body, name=name, grid=(S // tm,),
        in_specs=[pl.BlockSpec((tm, D), lambda i: (i, 0)), pl.BlockSpec((1, D), lambda i: (0, 0))],
        out_specs=pl.BlockSpec((tm, D), lambda i: (i, 0)), out_shape=jax.ShapeDtypeStruct((S, D), BF16),
    )(x, g)


def _rms_bwd(name, x, g, dn, dres, tm=512):
    S, D = x.shape

    def body(x_ref, g_ref, dn_ref, dres_ref, dx_ref, dxb_ref, dg_ref):
        i = pl.program_id(0)

        @pl.when(i == 0)
        def _():
            dg_ref[...] = jnp.zeros_like(dg_ref)

        xv = x_ref[...]
        r = lax.rsqrt(jnp.mean(xv * xv, axis=-1, keepdims=True) + EPS)
        xhat = xv * r
        dnv = dn_ref[...].astype(F32)
        dg_ref[...] += jnp.sum(dnv * xhat, axis=0, keepdims=True)
        t = dnv * g_ref[...]
        dx = r * (t - xhat * jnp.mean(t * xhat, axis=-1, keepdims=True)) + dres_ref[...]
        dx_ref[...] = dx
        dxb_ref[...] = dx.astype(BF16)

    row = pl.BlockSpec((tm, D), lambda i: (i, 0))
    vec = pl.BlockSpec((1, D), lambda i: (0, 0))
    return pl.pallas_call(
        body, name=name, grid=(S // tm,), in_specs=[row, vec, row, row], out_specs=[row, row, vec],
        out_shape=[jax.ShapeDtypeStruct((S, D), F32), jax.ShapeDtypeStruct((S, D), BF16), jax.ShapeDtypeStruct((1, D), F32)],
    )(x, g, dn, dres)


def _final_loss(h2, g, target, tm=512):
    S, D = h2.shape

    def body(x_ref, g_ref, t_ref, loss_ref, dg_ref, dx_ref, dxb_ref):
        i = pl.program_id(0)

        @pl.when(i == 0)
        def _():
            loss_ref[...] = jnp.zeros_like(loss_ref)
            dg_ref[...] = jnp.zeros_like(dg_ref)

        xv = x_ref[...]
        r = lax.rsqrt(jnp.mean(xv * xv, axis=-1, keepdims=True) + EPS)
        xhat = xv * r
        gv = g_ref[...]
        diff = xhat * gv - t_ref[...]
        per_tok = jnp.mean(diff * diff, axis=-1, keepdims=True)
        loss_ref[...] += 0.5 * jnp.sum(per_tok, axis=0, keepdims=True)
        dy = diff * (1.0 / D)
        dg_ref[...] += jnp.sum(dy * xhat, axis=0, keepdims=True)
        t = dy * gv
        dx = r * (t - xhat * jnp.mean(t * xhat, axis=-1, keepdims=True))
        dx_ref[...] = dx
        dxb_ref[...] = dx.astype(BF16)

    row = pl.BlockSpec((tm, D), lambda i: (i, 0))
    vec = pl.BlockSpec((1, D), lambda i: (0, 0))
    return pl.pallas_call(
        body, name="final_loss", grid=(S // tm,), in_specs=[row, vec, row],
        out_specs=[pl.BlockSpec((1, LANES), lambda i: (0, 0)), vec, row, row],
        out_shape=[jax.ShapeDtypeStruct((1, LANES), F32), jax.ShapeDtypeStruct((1, D), F32),
                   jax.ShapeDtypeStruct((S, D), F32), jax.ShapeDtypeStruct((S, D), BF16)],
    )(h2, g, target)


def _rope_tables(S):
    pos = jnp.arange(S, dtype=F32)
    inv_freq = ROPE_THETA ** (-jnp.arange(0, ROPE_DIM, 2, dtype=F32) / ROPE_DIM)
    ang = pos[:, None] * inv_freq[None, :]
    cos, sin = jnp.cos(ang), jnp.sin(ang)
    half = ROPE_DIM // 2
    rest = HEAD - ROPE_DIM
    z_h, z_r = jnp.zeros((S, half), F32), jnp.zeros((S, rest), F32)
    tab_c = jnp.concatenate([cos, cos, jnp.ones((S, rest), F32)], axis=1)
    tab_up = jnp.concatenate([z_h, sin, z_r], axis=1)
    tab_dn = jnp.concatenate([-sin, z_h, z_r], axis=1)
    return tab_c, tab_up, tab_dn


def _rope_head(t, c, up, dn):
    half = ROPE_DIM // 2
    return t * c + pltpu.roll(t, half, axis=1) * up + pltpu.roll(t, HEAD - half, axis=1) * dn


def _rope_fwd(proj, tabs, tm=512):
    S = proj.shape[0]
    W = 2 * ATTN_W

    def body(p_ref, c_ref, up_ref, dn_ref, o_ref):
        c, up, dn = c_ref[...], up_ref[...], dn_ref[...]
        for h in range(W // HEAD):
            sl = slice(h * HEAD, (h + 1) * HEAD)
            o_ref[:, sl] = _rope_head(p_ref[:, sl].astype(F32), c, up, dn).astype(BF16)

    tab = pl.BlockSpec((tm, HEAD), lambda i: (i, 0))
    return pl.pallas_call(
        body, name="rope_fwd", grid=(S // tm,), in_specs=[pl.BlockSpec((tm, W), lambda i: (i, 0)), tab, tab, tab],
        out_specs=pl.BlockSpec((tm, W), lambda i: (i, 0)), out_shape=jax.ShapeDtypeStruct((S, W), BF16),
    )(proj, *tabs)


def _attn_grad_merge(dq, dk, dv, tabs, tm=256):
    S = dq.shape[0]

    def body(q_ref, k_ref, v_ref, c_ref, up_ref, dn_ref, o_ref):
        c, up, dn = c_ref[...], up_ref[...], dn_ref[...]
        for h in range(N_HEADS):
            sl = slice(h * HEAD, (h + 1) * HEAD)
            for part, r in ((0, q_ref), (1, k_ref)):
                osl = slice(part * ATTN_W + h * HEAD, part * ATTN_W + (h + 1) * HEAD)
                o_ref[:, osl] = _rope_head(r[:, sl].astype(F32), c, -up, -dn).astype(BF16)
        o_ref[:, 2 * ATTN_W:] = v_ref[...]

    blk = pl.BlockSpec((tm, ATTN_W), lambda i: (i, 0))
    tab = pl.BlockSpec((tm, HEAD), lambda i: (i, 0))
    return pl.pallas_call(
        body, name="attn_grad_merge", grid=(S // tm,), in_specs=[blk] * 3 + [tab] * 3,
        out_specs=pl.BlockSpec((tm, 3 * ATTN_W), lambda i: (i, 0)), out_shape=jax.ShapeDtypeStruct((S, IN_MAIN), BF16),
    )(dq, dk, dv, *tabs)


SUB = 128
WIN = SUB + 2 * N_SIDE
Q_COL, K_COL, V_COL = 0, ATTN_W // HEAD, 2 * ATTN_W // HEAD


class _AttnGeo:
    def __init__(self, S, d):
        self.S, self.d, self.L = S, d, S // d
        self.halo = N_SIDE * d
        self.TB = min(2048, S)
        self.W = self.TB + 2 * self.halo
        self.n_sub = self.TB // SUB
        self.grid = (S // self.TB, N_HEADS)
        self.dt = F32 if d > 1 else BF16
        self.su = min(d, 4)
        self.sb = d // self.su
        assert self.TB % (SUB * d) == 0 and self.TB % self.halo == 0

    def specs(self, width, col0, per_head=True):
        ratio = self.TB // self.halo
        last = self.S // self.halo - 1
        col = (lambda h: col0 + h) if per_head else (lambda h: col0)
        cur = pl.BlockSpec((self.TB, width), lambda i, h: (i, col(h)))
        prev = pl.BlockSpec((self.halo, width), lambda i, h: (jnp.maximum(i * ratio - 1, 0), col(h)))
        nxt = pl.BlockSpec((self.halo, width), lambda i, h: (jnp.minimum((i + 1) * ratio, last), col(h)))
        return cur, prev, nxt

    def scratch(self, rows, dtype=None):
        nat = pltpu.VMEM((rows, LANES), self.dt if dtype is None else dtype)
        return [nat] if self.sb == 1 else [nat, pltpu.VMEM((rows, LANES), F32)]

    def bind(self, refs):
        nat = next(refs)
        return (nat, nat) if self.sb == 1 else (nat, next(refs))

    def spread(self, pair):
        nat, streams = pair
        if self.sb > 1:
            n = nat.shape[0] // self.sb
            for a in range(self.sb):
                streams[a * n:(a + 1) * n, :] = nat[pl.ds(a, n, stride=self.sb), :]
        return streams

    def gather(self, pair):
        nat, streams = pair
        if self.sb > 1:
            n = nat.shape[0] // self.sb
            for a in range(self.sb):
                nat[pl.ds(a, n, stride=self.sb), :] = streams[a * n:(a + 1) * n, :]
        return nat

    def rows(self, sub, n, total):
        res, blk = sub % self.d, sub // self.d
        a, b = res % self.sb, res // self.sb
        start = a * (total // self.sb) + b + self.su * SUB * blk
        return pl.ds(start, n, stride=self.su) if self.su > 1 else pl.ds(start, n)

    def band(self):
        row = lax.broadcasted_iota(jnp.int32, (SUB, WIN), 0)
        col = lax.broadcasted_iota(jnp.int32, (SUB, WIN), 1)
        return (col >= row) & (col <= row + 2 * N_SIDE), col

    def mask(self, sub, band):
        inside, col = band
        blk, n_blk = sub // self.d, self.TB // (SUB * self.d)
        base = pl.program_id(0) * (self.TB // self.d) + SUB * blk
        if blk == 0:
            inside = inside & (col >= N_SIDE - base)
        if blk == n_blk - 1:
            inside = inside & (col < self.L + N_SIDE - base)
        return inside

    def fill(self, dst, c_ref):
        dst[...] = c_ref[...].astype(dst.dtype)

    def fill_window(self, dst, p_ref, c_ref, n_ref):
        dst[0:self.halo] = p_ref[...].astype(dst.dtype)
        dst[self.halo:self.halo + self.TB] = c_ref[...].astype(dst.dtype)
        dst[self.halo + self.TB:] = n_ref[...].astype(dst.dtype)


def _lane_of(tile, h):
    lane = lax.broadcasted_iota(jnp.int32, tile.shape, 1)
    return jnp.sum(jnp.where(lane == h, tile, 0.0), axis=1, keepdims=True)


def _attn_fwd(qk, proj, d):
    S = qk.shape[0]
    geo = _AttnGeo(S, d)
    scale = HEAD ** -0.5

    def body(q_ref, kp, kc, kn, vp, vc, vn, o_ref, lse_ref, *scratch):
        h = pl.program_id(1)
        refs = iter(scratch)
        q_p, k_p, v_p, o_p, l_p = (geo.bind(refs) for _ in range(5))
        geo.fill(q_p[0], q_ref)
        geo.fill_window(k_p[0], kp, kc, kn)
        geo.fill_window(v_p[0], vp, vc, vn)
        qs, ks, vs = geo.spread(q_p), geo.spread(k_p), geo.spread(v_p)
        os, ls = o_p[1], l_p[1]
        band = geo.band()
        for sub in range(geo.n_sub):
            rq, rw = geo.rows(sub, SUB, geo.TB), geo.rows(sub, WIN, geo.W)
            q_r, k_r, v_r = qs[rq, :].astype(BF16), ks[rw, :].astype(BF16), vs[rw, :].astype(BF16)
            s = jnp.where(geo.mask(sub, band), _dot(q_r, k_r, NT) * scale, NEG)
            m = jnp.max(s, axis=1, keepdims=True)
            p = jnp.exp(s - m)
            l = jnp.sum(p, axis=1, keepdims=True)
            os[rq, :] = _dot(p.astype(BF16), v_r) / l
            ls[rq, :] = jnp.broadcast_to(m + jnp.log(l), (SUB, LANES))
        o_ref[...] = geo.gather(o_p)[...].astype(BF16)

        @pl.when(h == 0)
        def _():
            lse_ref[...] = jnp.zeros_like(lse_ref)

        lane = lax.broadcasted_iota(jnp.int32, (geo.TB, LANES), 1)
        lse_ref[...] = jnp.where(lane == h, geo.gather(l_p)[...], lse_ref[...])

    q_cur, _, _ = geo.specs(HEAD, Q_COL)
    k_specs = geo.specs(HEAD, K_COL)
    v_specs = geo.specs(HEAD, V_COL)
    stat = pl.BlockSpec((geo.TB, LANES), lambda i, h: (i, 0))
    return pl.pallas_call(
        body, name=f"attn_fwd_d{d}", grid=geo.grid,
        in_specs=[q_cur, k_specs[1], k_specs[0], k_specs[2], v_specs[1], v_specs[0], v_specs[2]],
        out_specs=[q_cur, stat],
        out_shape=[jax.ShapeDtypeStruct((S, ATTN_W), BF16), jax.ShapeDtypeStruct((S, LANES), F32)],
        scratch_shapes=(geo.scratch(geo.TB) + geo.scratch(geo.W) + geo.scratch(geo.W) + geo.scratch(geo.TB, F32)
                        + geo.scratch(geo.TB, F32)),
    )(qk, qk, qk, qk, proj, proj, proj)


def _attn_combine(outs, lses, g, tm=256):
    S = outs[0].shape[0]

    def body(o1, o2, o3, l1, l2, l3, g_ref, ao_ref, o_ref, lse_ref):
        a1, a2, a3 = l1[...], l2[...], l3[...]
        mx = jnp.maximum(jnp.maximum(a1, a2), a3)
        e1, e2, e3 = jnp.exp(a1 - mx), jnp.exp(a2 - mx), jnp.exp(a3 - mx)
        den = e1 + e2 + e3
        lse_ref[...] = mx + jnp.log(den)
        head_of_col = lax.broadcasted_iota(jnp.int32, (LANES, ATTN_W), 1) // HEAD
        spread = (lax.broadcasted_iota(jnp.int32, (LANES, ATTN_W), 0) == head_of_col).astype(BF16)

        def wide(e):
            wgt = e / den
            hi = wgt.astype(BF16)
            lo = (wgt - hi.astype(F32)).astype(BF16)
            return _dot(hi, spread) + _dot(lo, spread)

        ov = wide(e1) * o1[...].astype(F32) + wide(e2) * o2[...].astype(F32) + wide(e3) * o3[...].astype(F32)
        o_ref[...] = ov
        r = lax.rsqrt(jnp.mean(ov * ov, axis=-1, keepdims=True) + EPS)
        ao_ref[...] = (ov * r * g_ref[...]).astype(BF16)

    blk = pl.BlockSpec((tm, ATTN_W), lambda i: (i, 0))
    ls = pl.BlockSpec((tm, LANES), lambda i: (i, 0))
    return pl.pallas_call(
        body, name="attn_combine", grid=(S // tm,),
        in_specs=[blk, blk, blk, ls, ls, ls, pl.BlockSpec((1, ATTN_W), lambda i: (0, 0))], out_specs=[blk, blk, ls],
        out_shape=[jax.ShapeDtypeStruct((S, D_MODEL), BF16), jax.ShapeDtypeStruct((S, ATTN_W), F32),
                   jax.ShapeDtypeStruct((S, LANES), F32)],
    )(*outs, *lses, g)


def _attn_norm_bwd(o, g, dao, tm=256):
    S = o.shape[0]

    def body(o_ref, g_ref, dao_ref, do_ref, dl_ref, dg_ref):
        i = pl.program_id(0)

        @pl.when(i == 0)
        def _():
            dg_ref[...] = jnp.zeros_like(dg_ref)

        ov = o_ref[...]
        r = lax.rsqrt(jnp.mean(ov * ov, axis=-1, keepdims=True) + EPS)
        ohat = ov * r
        dn = dao_ref[...].astype(F32)
        dg_ref[...] += jnp.sum(dn * ohat, axis=0, keepdims=True)
        t = dn * g_ref[...]
        do = r * (t - ohat * jnp.mean(t * ohat, axis=-1, keepdims=True))
        do_ref[...] = do.astype(BF16)
        prod = do * ov
        lane = lax.broadcasted_iota(jnp.int32, (tm, LANES), 1)
        tile = jnp.zeros((tm, LANES), F32)
        for h in range(N_HEADS):
            tile = jnp.where(lane == h, jnp.sum(prod[:, h * HEAD:(h + 1) * HEAD], axis=1, keepdims=True), tile)
        dl_ref[...] = tile

    blk = pl.BlockSpec((tm, ATTN_W), lambda i: (i, 0))
    vec = pl.BlockSpec((1, ATTN_W), lambda i: (0, 0))
    return pl.pallas_call(
        body, name="attn_norm_bwd", grid=(S // tm,),
        in_specs=[blk, vec, pl.BlockSpec((tm, ATTN_W), lambda i: (i, 0))],
        out_specs=[blk, pl.BlockSpec((tm, LANES), lambda i: (i, 0)), vec],
        out_shape=[jax.ShapeDtypeStruct((S, ATTN_W), BF16), jax.ShapeDtypeStruct((S, LANES), F32),
                   jax.ShapeDtypeStruct((1, ATTN_W), F32)],
    )(o, g, dao)


def _attn_bwd_dq(qk, proj, do, lse, delta, d, prev):
    S = qk.shape[0]
    geo = _AttnGeo(S, d)
    scale = HEAD ** -0.5
    before = [] if prev is None else [prev]

    def body(q_ref, kp, kc, kn, vp, vc, vn, do_ref, lse_ref, dl_ref, *rest):
        prev_refs, (dq_ref, *scratch) = rest[:len(before)], rest[len(before):]
        h = pl.program_id(1)
        refs = iter(scratch)
        q_p, k_p, v_p, do_p, lse_p, dl_p, dq_p = (geo.bind(refs) for _ in range(7))
        geo.fill(q_p[0], q_ref)
        geo.fill(do_p[0], do_ref)
        geo.fill(lse_p[0], lse_ref)
        geo.fill(dl_p[0], dl_ref)
        geo.fill_window(k_p[0], kp, kc, kn)
        geo.fill_window(v_p[0], vp, vc, vn)
        qs, ks, vs, dos = geo.spread(q_p), geo.spread(k_p), geo.spread(v_p), geo.spread(do_p)
        lses, dls = geo.spread(lse_p), geo.spread(dl_p)
        dqs = dq_p[1]
        band = geo.band()
        for sub in range(geo.n_sub):
            rq, rw = geo.rows(sub, SUB, geo.TB), geo.rows(sub, WIN, geo.W)
            q_r, k_r, v_r = qs[rq, :].astype(BF16), ks[rw, :].astype(BF16), vs[rw, :].astype(BF16)
            lse_c, dl_c = _lane_of(lses[rq, :], h), _lane_of(dls[rq, :], h)
            s = _dot(q_r, k_r, NT) * scale
            p = jnp.where(geo.mask(sub, band), jnp.exp(s - lse_c), 0.0)
            dp = _dot(dos[rq, :].astype(BF16), v_r, NT)
            ds = (p * (dp - dl_c) * scale).astype(BF16)
            dqs[rq, :] = _dot(ds, k_r)
        total = geo.gather(dq_p)[...]
        for p_ref in prev_refs:
            total = total + p_ref[...].astype(F32)
        dq_ref[...] = total.astype(BF16)

    cur, _, _ = geo.specs(HEAD, 0)
    k_specs = geo.specs(HEAD, K_COL)
    v_specs = geo.specs(HEAD, V_COL)
    stat = pl.BlockSpec((geo.TB, LANES), lambda i, h: (i, 0))
    return pl.pallas_call(
        body, name=f"attn_bwd_dq_d{d}", grid=geo.grid,
        in_specs=[cur, k_specs[1], k_specs[0], k_specs[2], v_specs[1], v_specs[0], v_specs[2], cur, stat, stat]
        + [cur] * len(before),
        out_specs=cur, out_shape=jax.ShapeDtypeStruct((S, ATTN_W), BF16),
        scratch_shapes=(geo.scratch(geo.TB) + geo.scratch(geo.W) + geo.scratch(geo.W) + geo.scratch(geo.TB)
                        + geo.scratch(geo.TB, F32) + geo.scratch(geo.TB, F32) + geo.scratch(geo.TB, F32)),
    )(qk, qk, qk, qk, proj, proj, proj, do, lse, delta, *before)


def _attn_bwd_dkv(qk, proj, do, lse, delta, d, prev):
    S = qk.shape[0]
    geo = _AttnGeo(S, d)
    scale = HEAD ** -0.5
    before = [] if prev is None else list(prev)

    def body(k_ref, v_ref, qp, qc, qn, dop, doc, don, lp, lc, ln, dlp, dlc, dln, *rest):
        prev_refs, (dk_ref, dv_ref, *scratch) = rest[:len(before)], rest[len(before):]
        h = pl.program_id(1)
        refs = iter(scratch)
        k_p, v_p, q_p, do_p, lw_p, dlw_p, dk_p, dv_p = (geo.bind(refs) for _ in range(8))
        geo.fill(k_p[0], k_ref)
        geo.fill(v_p[0], v_ref)
        geo.fill_window(q_p[0], qp, qc, qn)
        geo.fill_window(do_p[0], dop, doc, don)
        geo.fill_window(lw_p[0], lp, lc, ln)
        geo.fill_window(dlw_p[0], dlp, dlc, dln)
        ks, vs, qs, dos = geo.spread(k_p), geo.spread(v_p), geo.spread(q_p), geo.spread(do_p)
        lws, dlws = geo.spread(lw_p), geo.spread(dlw_p)
        dks, dvs = dk_p[1], dv_p[1]
        head = lax.broadcasted_iota(jnp.int32, (LANES, WIN), 0)
        band = geo.band()
        for sub in range(geo.n_sub):
            rq, rw = geo.rows(sub, SUB, geo.TB), geo.rows(sub, WIN, geo.W)
            k_r, v_r = ks[rq, :].astype(BF16), vs[rq, :].astype(BF16)
            q_w, do_w = qs[rw, :].astype(BF16), dos[rw, :].astype(BF16)
            lse_row = jnp.sum(jnp.where(head == h, lws[rw, :].T, 0.0), axis=0, keepdims=True)
            dl_row = jnp.sum(jnp.where(head == h, dlws[rw, :].T, 0.0), axis=0, keepdims=True)
            st = _dot(k_r, q_w, NT) * scale
            pt = jnp.where(geo.mask(sub, band), jnp.exp(st - lse_row), 0.0)
            dvs[rq, :] = _dot(pt.astype(BF16), do_w)
            dpt = _dot(v_r, do_w, NT)
            dst = (pt * (dpt - dl_row) * scale).astype(BF16)
            dks[rq, :] = _dot(dst, q_w)
        dk_tot, dv_tot = geo.gather(dk_p)[...], geo.gather(dv_p)[...]
        if prev_refs:
            dk_tot, dv_tot = dk_tot + prev_refs[0][...].astype(F32), dv_tot + prev_refs[1][...].astype(F32)
        dk_ref[...] = dk_tot.astype(BF16)
        dv_ref[...] = dv_tot.astype(BF16)

    q_specs = geo.specs(HEAD, Q_COL)
    k_cur, _, _ = geo.specs(HEAD, K_COL)
    v_cur, _, _ = geo.specs(HEAD, V_COL)
    do_specs = geo.specs(HEAD, 0)
    st_specs = geo.specs(LANES, 0, per_head=False)
    cur = do_specs[0]
    return pl.pallas_call(
        body, name=f"attn_bwd_dkv_d{d}", grid=geo.grid,
        in_specs=[k_cur, v_cur, q_specs[1], q_specs[0], q_specs[2], do_specs[1], do_specs[0], do_specs[2],
                  st_specs[1], st_specs[0], st_specs[2], st_specs[1], st_specs[0], st_specs[2]] + [cur] * len(before),
        out_specs=[cur, cur],
        out_shape=[jax.ShapeDtypeStruct((S, ATTN_W), BF16), jax.ShapeDtypeStruct((S, ATTN_W), BF16)],
        scratch_shapes=(geo.scratch(geo.TB) + geo.scratch(geo.TB) + geo.scratch(geo.W) + geo.scratch(geo.W)
                        + geo.scratch(geo.W, F32) + geo.scratch(geo.W, F32) + geo.scratch(geo.TB, F32)
                        + geo.scratch(geo.TB, F32)),
    )(qk, proj, qk, qk, qk, do, do, do, lse, lse, lse, delta, delta, delta, *before)


def _cumsum_rows(x, reverse):
    n = x.shape[0]
    row = lax.broadcasted_iota(jnp.int32, x.shape, 0)
    s = 1
    while s < n:
        if reverse:
            x = x + jnp.where(row < n - s, pltpu.roll(x, n - s, axis=0), 0.0)
        else:
            x = x + jnp.where(row >= s, pltpu.roll(x, s, axis=0), 0.0)
        s *= 2
    return x


GLA_GROUP = 8


def _gla_rows(cc):
    return slice(cc * CHUNK, (cc + 1) * CHUNK)


def _gla_chunk_terms(q_ref, k_ref, v_ref, g_ref, h, reverse, rows, b_ref=None):
    ksl = slice(h * GLA_DK, (h + 1) * GLA_DK)
    q = q_ref[rows, ksl].astype(F32) * (GLA_DK ** -0.5)
    k = k_ref[rows, ksl].astype(F32)
    v = v_ref[rows, h * GLA_DV:(h + 1) * GLA_DV]
    b = _cumsum_rows(g_ref[rows, ksl], reverse) if b_ref is None else b_ref[rows, ksl]
    r_ref = CHUNK // 2 if reverse else CHUNK // 2 - 1
    r_last = 0 if reverse else CHUNK - 1
    b_ref, b_last = b[r_ref:r_ref + 1, :], b[r_last:r_last + 1, :]
    ii = lax.broadcasted_iota(jnp.int32, (CHUNK, CHUNK), 0)
    jj = lax.broadcasted_iota(jnp.int32, (CHUNK, CHUNK), 1)
    causal = (jj >= ii) if reverse else (jj <= ii)
    e_q, e_k = jnp.exp(b - b_ref), jnp.exp(b_ref - b)
    e_in, e_st = jnp.exp(b), jnp.exp(b_last - b)
    return dict(q=q, k=k, v=v, b=b, causal=causal, e_q=e_q, e_k=e_k, e_in=e_in, e_st=e_st, dec=jnp.exp(b_last),
                qe=q * e_q, ke=k * e_k, q_in=q * e_in, k_st=k * e_st, r_ref=r_ref, r_last=r_last)


def _gla_specs(order):
    rows = GLA_GROUP * CHUNK
    q = pl.BlockSpec((rows, GLA_K), lambda c: (order(c), 3 * ATTN_W // GLA_K))
    k = pl.BlockSpec((rows, GLA_K), lambda c: (order(c), 3 * ATTN_W // GLA_K + 1))
    v = pl.BlockSpec((rows, GLA_V), lambda c: (order(c), (3 * ATTN_W + 2 * GLA_K) // GLA_V))
    return q, k, v


def _gla_fwd(proj, gates, reverse, o_prev=None):
    S = proj.shape[0]
    n = S // CHUNK
    nb = n // GLA_GROUP
    rows = GLA_GROUP * CHUNK
    order = (lambda c: nb - 1 - c) if reverse else (lambda c: c)
    seq = list(range(GLA_GROUP))[::-1] if reverse else list(range(GLA_GROUP))
    gcol = 1 if reverse else 0

    def body(*refs):
        if o_prev is None:
            q_ref, k_ref, v_ref, g_ref, o_ref, st_ref, a_ref, b_ref, state = refs
        else:
            q_ref, k_ref, v_ref, g_ref, op_ref, o_ref, st_ref, a_ref, b_ref, state = refs
        c = pl.program_id(0)

        @pl.when(c == 0)
        def _():
            state[...] = jnp.zeros_like(state)

        for h in range(GLA_HEADS):
            vsl = slice(h * GLA_DV, (h + 1) * GLA_DV)
            st = state[h]
            for cc in seq:
                rs = _gla_rows(cc)
                t = _gla_chunk_terms(q_ref, k_ref, v_ref, g_ref, h, reverse, rs)
                b_ref[rs, h * GLA_DK:(h + 1) * GLA_DK] = t["b"]
                a = jnp.where(t["causal"], _dot(t["qe"].astype(BF16), t["ke"].astype(BF16), NT), 0.0).astype(BF16)
                a_ref[cc, h] = a
                o = _dot(a, t["v"])
                st_b = st.astype(BF16)
                st_ref[cc, h] = st_b
                o = o + _dot(t["q_in"].astype(BF16), st_b, NT)
                st = st * t["dec"] + _dot(t["v"], t["k_st"].astype(BF16), TN)
                if o_prev is not None:
                    o = o + op_ref[rs, vsl]
                o_ref[rs, vsl] = o
            state[h] = st

    q_spec, k_spec, v_spec = _gla_specs(order)
    o_spec = pl.BlockSpec((rows, GLA_V), lambda c: (order(c), 0))
    in_specs = [q_spec, k_spec, v_spec, pl.BlockSpec((rows, GLA_K), lambda c: (order(c), gcol))]
    operands = [proj, proj, proj, gates]
    if o_prev is not None:
        in_specs.append(o_spec)
        operands.append(o_prev)
    return pl.pallas_call(
        body, name="gla_fwd_rev" if reverse else "gla_fwd", grid=(nb,), in_specs=in_specs,
        out_specs=[o_spec, pl.BlockSpec((GLA_GROUP, GLA_HEADS, GLA_DV, GLA_DK), lambda c: (order(c), 0, 0, 0)),
                   pl.BlockSpec((GLA_GROUP, GLA_HEADS, CHUNK, CHUNK), lambda c: (order(c), 0, 0, 0)),
                   pl.BlockSpec((rows, GLA_K), lambda c: (order(c), 0))],
        out_shape=[jax.ShapeDtypeStruct((S, GLA_V), F32), jax.ShapeDtypeStruct((n, GLA_HEADS, GLA_DV, GLA_DK), BF16),
                   jax.ShapeDtypeStruct((n, GLA_HEADS, CHUNK, CHUNK), BF16), jax.ShapeDtypeStruct((S, GLA_K), F32)],
        scratch_shapes=[pltpu.VMEM((GLA_HEADS, GLA_DV, GLA_DK), F32)],
    )(*operands)


def _gla_bwd(proj, kept, do, reverse, prev=None):
    S = proj.shape[0]
    n = S // CHUNK
    nb = n // GLA_GROUP
    rows = GLA_GROUP * CHUNK
    order = (lambda c: c) if reverse else (lambda c: nb - 1 - c)
    seq = list(range(GLA_GROUP)) if reverse else list(range(GLA_GROUP))[::-1]
    out_dt = F32 if prev is None else BF16

    def body(*refs):
        if prev is None:
            q_ref, k_ref, v_ref, b_ref, st_ref, a_ref, do_ref, dq_ref, dk_ref, dv_ref, dg_ref, dstate = refs
        else:
            q_ref, k_ref, v_ref, b_ref, st_ref, a_ref, do_ref, pq, pk, pv, dq_ref, dk_ref, dv_ref, dg_ref, dstate = refs
        c = pl.program_id(0)

        @pl.when(c == 0)
        def _():
            dstate[...] = jnp.zeros_like(dstate)

        row = lax.broadcasted_iota(jnp.int32, (CHUNK, GLA_DK), 0)
        for h in range(GLA_HEADS):
            ksl = slice(h * GLA_DK, (h + 1) * GLA_DK)
            vsl = slice(h * GLA_DV, (h + 1) * GLA_DV)
            dst = dstate[h]
            for cc in seq:
                rs = _gla_rows(cc)
                t = _gla_chunk_terms(q_ref, k_ref, v_ref, None, h, reverse, rs, b_ref)
                v = t["v"]
                dob = do_ref[rs, vsl].astype(BF16)
                st_b = st_ref[cc, h]
                dst_b = dst.astype(BF16)
                qe_b, ke_b = t["qe"].astype(BF16), t["ke"].astype(BF16)
                q_in_b, k_st_b = t["q_in"].astype(BF16), t["k_st"].astype(BF16)
                da = jnp.where(t["causal"], _dot(dob, v, NT), 0.0).astype(BF16)
                dv = _dot(a_ref[cc, h], dob, TN) + _dot(k_st_b, dst_b, NT)
                dqe = _dot(da, ke_b)
                dke = _dot(da, qe_b, TN)
                dq_in = _dot(dob, st_b)
                dk_st = _dot(v, dst_b)
                ddec = jnp.sum(dst * st_b.astype(F32), axis=0, keepdims=True)
                dst = _dot(dob, q_in_b, TN) + dst * t["dec"]
                dq = (dqe * t["e_q"] + dq_in * t["e_in"]) * (GLA_DK ** -0.5)
                dk = dke * t["e_k"] + dk_st * t["e_st"]
                w_q, w_k = dqe * t["qe"], dke * t["ke"]
                w_st = dk_st * t["k_st"]
                db = w_q - w_k + dq_in * t["q_in"] - w_st
                db_ref = jnp.sum(w_k - w_q, axis=0, keepdims=True)
                db_last = jnp.sum(w_st, axis=0, keepdims=True) + ddec * t["dec"]
                db = db + jnp.where(row == t["r_ref"], db_ref, 0.0) + jnp.where(row == t["r_last"], db_last, 0.0)
                dg_ref[rs, ksl] = _cumsum_rows(db, not reverse)
                if prev is not None:
                    dq, dk, dv = dq + pq[rs, ksl], dk + pk[rs, ksl], dv + pv[rs, vsl]
                dq_ref[rs, ksl] = dq.astype(out_dt)
                dk_ref[rs, ksl] = dk.astype(out_dt)
                dv_ref[rs, vsl] = dv.astype(out_dt)
            dstate[h] = dst

    q_spec, k_spec, v_spec = _gla_specs(order)
    kk = pl.BlockSpec((rows, GLA_K), lambda c: (order(c), 0))
    vv = pl.BlockSpec((rows, GLA_V), lambda c: (order(c), 0))
    states, scores, sums = kept
    in_specs = [q_spec, k_spec, v_spec, kk,
                pl.BlockSpec((GLA_GROUP, GLA_HEADS, GLA_DV, GLA_DK), lambda c: (order(c), 0, 0, 0)),
                pl.BlockSpec((GLA_GROUP, GLA_HEADS, CHUNK, CHUNK), lambda c: (order(c), 0, 0, 0)), vv]
    operands = [proj, proj, proj, sums, states, scores, do]
    if prev is not None:
        in_specs += [kk, kk, vv]
        operands += list(prev)
    return pl.pallas_call(
        body, name="gla_bwd_rev" if reverse else "gla_bwd", grid=(nb,), in_specs=in_specs, out_specs=[kk, kk, vv, kk],
        out_shape=[jax.ShapeDtypeStruct((S, GLA_K), out_dt), jax.ShapeDtypeStruct((S, GLA_K), out_dt),
                   jax.ShapeDtypeStruct((S, GLA_V), out_dt), jax.ShapeDtypeStruct((S, GLA_K), F32)],
        scratch_shapes=[pltpu.VMEM((GLA_HEADS, GLA_DV, GLA_DK), F32)],
    )(*operands)


def _gates_fwd(z, wg, bias, tm=512):
    S = z.shape[0]
    W = 2 * GLA_K

    def body(z_ref, w_ref, b_ref, o_ref):
        zg = _dot(z_ref[...], w_ref[...]) + b_ref[...]
        o_ref[...] = (jnp.minimum(zg, 0.0) - jnp.log(1.0 + jnp.exp(-jnp.abs(zg)))) * (1.0 / GATE_NORM)

    return pl.pallas_call(
        body, name="gates_fwd", grid=(S // tm,),
        in_specs=[pl.BlockSpec((tm, Z_W), lambda i: (i, 0)), pl.BlockSpec((Z_W, W), lambda i: (0, 0)),
                  pl.BlockSpec((1, W), lambda i: (0, 0))],
        out_specs=pl.BlockSpec((tm, W), lambda i: (i, 0)), out_shape=jax.ShapeDtypeStruct((S, W), F32),
    )(z, wg, bias)


def _gates_bwd(z, wg, bias, dg_f, dg_b, tm=512):
    S = z.shape[0]
    W = 2 * GLA_K

    def body(z_ref, w_ref, b_ref, dgf_ref, dgb_ref, dz_ref, dw_ref, db_ref):
        i = pl.program_id(0)

        @pl.when(i == 0)
        def _():
            dw_ref[...] = jnp.zeros_like(dw_ref)
            db_ref[...] = jnp.zeros_like(db_ref)

        zv = z_ref[...]
        zg = _dot(zv, w_ref[...]) + b_ref[...]
        dg = jnp.concatenate([dgf_ref[...], dgb_ref[...]], axis=1)
        dzg = dg * (1.0 / GATE_NORM) * _sigmoid(-zg)
        db_ref[...] += jnp.sum(dzg, axis=0, keepdims=True)
        dzg_b = dzg.astype(BF16)
        dw_ref[...] += _dot(zv, dzg_b, TN)
        dz_ref[...] = _dot(dzg_b, w_ref[...], NT).astype(BF16)

    half = pl.BlockSpec((tm, GLA_K), lambda i: (i, 0))
    return pl.pallas_call(
        body, name="gates_bwd", grid=(S // tm,),
        in_specs=[pl.BlockSpec((tm, Z_W), lambda i: (i, 0)), pl.BlockSpec((Z_W, W), lambda i: (0, 0)),
                  pl.BlockSpec((1, W), lambda i: (0, 0)), half, half],
        out_specs=[pl.BlockSpec((tm, Z_W), lambda i: (i, 0)), pl.BlockSpec((Z_W, W), lambda i: (0, 0)),
                   pl.BlockSpec((1, W), lambda i: (0, 0))],
        out_shape=[jax.ShapeDtypeStruct((S, Z_W), BF16), jax.ShapeDtypeStruct((Z_W, W), F32),
                   jax.ShapeDtypeStruct((1, W), F32)],
    )(z, wg, bias, dg_f, dg_b)


def _gla_out_fwd(o, proj, g, cat, tm=512):
    S = o.shape[0]

    def body(o_ref, gr_ref, g_ref, cat_ref, out_ref):
        gn = g_ref[...]
        for h in range(GLA_HEADS):
            sl = slice(h * GLA_DV, (h + 1) * GLA_DV)
            ov = o_ref[:, sl]
            r = lax.rsqrt(jnp.mean(ov * ov, axis=-1, keepdims=True) + EPS)
            gr = gr_ref[:, sl].astype(F32)
            out_ref[:, sl] = (ov * r * gn * _silu(gr)).astype(BF16)

    blk = pl.BlockSpec((tm, GLA_V), lambda i: (i, 0))
    return pl.pallas_call(
        body, name="gla_out_fwd", grid=(S // tm,),
        in_specs=[blk, pl.BlockSpec((tm, GLA_V), lambda i: (i, (3 * ATTN_W + 2 * GLA_K + GLA_V) // GLA_V)),
                  pl.BlockSpec((1, GLA_DV), lambda i: (0, 0)), ANY],
        out_specs=pl.BlockSpec((tm, GLA_V), lambda i: (i, 1)), out_shape=jax.ShapeDtypeStruct((S, D_MODEL), BF16),
        input_output_aliases={3: 0},
    )(o, proj, g, cat)


def _gla_out_bwd(o, proj, g, dcat, dproj, tm=512):
    S = o.shape[0]

    def body(o_ref, gr_ref, g_ref, dgo_ref, dproj_ref, do_ref, dgr_ref, dg_ref):
        i = pl.program_id(0)

        @pl.when(i == 0)
        def _():
            dg_ref[...] = jnp.zeros_like(dg_ref)

        gn = g_ref[...]
        dg_acc = jnp.zeros((1, GLA_DV), F32)
        for h in range(GLA_HEADS):
            sl = slice(h * GLA_DV, (h + 1) * GLA_DV)
            ov = o_ref[:, sl]
            r = lax.rsqrt(jnp.mean(ov * ov, axis=-1, keepdims=True) + EPS)
            yhat = ov * r
            gr = gr_ref[:, sl].astype(F32)
            sg = _sigmoid(gr)
            dgo = dgo_ref[:, sl].astype(F32)
            dgr_ref[:, sl] = (dgo * (yhat * gn) * (sg * (1.0 + gr * (1.0 - sg)))).astype(BF16)
            dy = dgo * (gr * sg)
            dg_acc = dg_acc + jnp.sum(dy * yhat, axis=0, keepdims=True)
            t = dy * gn
            do_ref[:, sl] = r * (t - yhat * jnp.mean(t * yhat, axis=-1, keepdims=True))
        dg_ref[...] += dg_acc

    blk = pl.BlockSpec((tm, GLA_V), lambda i: (i, 0))
    vec = pl.BlockSpec((1, GLA_DV), lambda i: (0, 0))
    return pl.pallas_call(
        body, name="gla_out_bwd", grid=(S // tm,),
        in_specs=[blk, pl.BlockSpec((tm, GLA_V), lambda i: (i, (3 * ATTN_W + 2 * GLA_K + GLA_V) // GLA_V)), vec,
                  pl.BlockSpec((tm, GLA_V), lambda i: (i, 1)), ANY],
        out_specs=[blk, pl.BlockSpec((tm, GLA_V), lambda i: (i, (3 * ATTN_W + 2 * GLA_K + GLA_V) // GLA_V)), vec],
        out_shape=[jax.ShapeDtypeStruct((S, GLA_V), F32), jax.ShapeDtypeStruct((S, IN_MAIN), BF16),
                   jax.ShapeDtypeStruct((1, GLA_DV), F32)],
        input_output_aliases={4: 1},
    )(o, proj, g, dcat, dproj)


HALO = 16


def _halo_specs(tm, tn, S):
    cur = pl.BlockSpec((tm, tn), lambda j, i: (i, j))
    prev = pl.BlockSpec((HALO, tn), lambda j, i: (jnp.maximum(i * (tm // HALO) - 1, 0), j))
    nxt = pl.BlockSpec((HALO, tn), lambda j, i: (jnp.minimum((i + 1) * (tm // HALO), S // HALO - 1), j))
    return cur, prev, nxt


def _shifted(c_ref, p_ref, n_ref, n_blocks, i=None):
    if i is None:
        i = pl.program_id(1)
    x = c_ref[...].astype(F32)
    tm = x.shape[0]
    row = lax.broadcasted_iota(jnp.int32, x.shape, 0)
    before = p_ref[HALO - 1:HALO, :].astype(F32) * (i > 0).astype(F32)
    after = n_ref[0:1, :].astype(F32) * (i < n_blocks - 1).astype(F32)
    x_m1 = jnp.where(row == 0, before, pltpu.roll(x, 1, axis=0))
    x_p1 = jnp.where(row == tm - 1, after, pltpu.roll(x, tm - 1, axis=0))
    return x, x_m1, x_p1


def _glu_fwd(gp, up, cw, cb, tm=512, tn=1408):
    S = gp.shape[0]
    nb = S // tm

    def body(c_ref, p_ref, n_ref, up_ref, w_ref, b_ref, o_ref, gate_ref):
        x, x_m1, x_p1 = _shifted(c_ref, p_ref, n_ref, nb)
        w = w_ref[...]
        gate = w[0:1, :] * x_m1 + w[1:2, :] * x + w[2:3, :] * x_p1 + b_ref[...]
        gate_ref[...] = gate.astype(BF16)
        o_ref[...] = (_silu(gate) * up_ref[...].astype(F32)).astype(BF16)

    cur, prev, nxt = _halo_specs(tm, tn, S)
    out = jax.ShapeDtypeStruct((S, D_FF), BF16)
    return pl.pallas_call(
        body, name="glu_fwd", grid=(D_FF // tn, nb),
        in_specs=[cur, prev, nxt, cur, pl.BlockSpec((3, tn), lambda j, i: (0, j)), pl.BlockSpec((1, tn), lambda j, i: (0, j))],
        out_specs=[cur, cur], out_shape=[out, out],
    )(gp, gp, gp, up, cw, cb)


def _glu_bwd(gate, gp, up, dact, cw, tm=512, tn=1408):
    S = gp.shape[0]
    nb = S // tm

    def body(g_ref, gb_ref, ga_ref, x_ref, up_ref, upp_ref, upn_ref, da_ref, dap_ref, dan_ref, w_ref,
             dup_ref, dgp_ref, dw_ref, db_ref):
        i = pl.program_id(1)

        @pl.when(i == 0)
        def _():
            dw_ref[...] = jnp.zeros_like(dw_ref)
            db_ref[...] = jnp.zeros_like(db_ref)

        w = w_ref[...]
        w0, w1, w2 = w[0:1, :], w[1:2, :], w[2:3, :]

        def d_gate(g, da, upv):
            sg = _sigmoid(g)
            return sg, da * upv * (sg * (1.0 + g * (1.0 - sg)))

        g = g_ref[...].astype(F32)
        da = da_ref[...].astype(F32)
        sg, dgate = d_gate(g, da, up_ref[...].astype(F32))
        dup_ref[...] = (da * (g * sg)).astype(BF16)

        last = slice(HALO - 1, HALO)
        _, dgate_before = d_gate(gb_ref[...].astype(F32)[last, :], dap_ref[...].astype(F32)[last, :], upp_ref[...].astype(F32)[last, :])
        _, dgate_after = d_gate(ga_ref[...].astype(F32)[0:1, :], dan_ref[...].astype(F32)[0:1, :], upn_ref[...].astype(F32)[0:1, :])
        dgate_before = dgate_before * (i > 0).astype(F32)
        dgate_after = dgate_after * (i < nb - 1).astype(F32)
        row = lax.broadcasted_iota(jnp.int32, dgate.shape, 0)
        dg_m1 = jnp.where(row == 0, dgate_before, pltpu.roll(dgate, 1, axis=0))
        dg_p1 = jnp.where(row == tm - 1, dgate_after, pltpu.roll(dgate, tm - 1, axis=0))
        dgp_ref[...] = (w0 * dg_p1 + w1 * dgate + w2 * dg_m1).astype(BF16)
        x = x_ref[...].astype(F32)
        db_ref[...] += jnp.sum(dgate, axis=0, keepdims=True)
        dw_ref[...] += jnp.concatenate(
            [jnp.sum(dg_p1 * x, axis=0, keepdims=True), jnp.sum(dgate * x, axis=0, keepdims=True),
             jnp.sum(dg_m1 * x, axis=0, keepdims=True)], axis=0)

    cur, prev, nxt = _halo_specs(tm, tn, S)
    w_spec = pl.BlockSpec((3, tn), lambda j, i: (0, j))
    b_spec = pl.BlockSpec((1, tn), lambda j, i: (0, j))
    return pl.pallas_call(
        body, name="glu_bwd", grid=(D_FF // tn, nb),
        in_specs=[cur, prev, nxt, cur, cur, prev, nxt, cur, prev, nxt, w_spec],
        out_specs=[cur, cur, w_spec, b_spec],
        out_shape=[jax.ShapeDtypeStruct((S, D_FF), BF16), jax.ShapeDtypeStruct((S, D_FF), BF16),
                   jax.ShapeDtypeStruct((3, D_FF), F32), jax.ShapeDtypeStruct((1, D_FF), F32)],
    )(gate, gate, gate, gp, up, up, up, dact, dact, dact, cw)


def _local_step(x, target, norm1_g, w_in_t, wg, gate_bias, gla_norm_g, attn_norm_g, w_out, norm2_g,
                w_gate4, w_up4, conv_w, conv_b, w_down, final_norm_g, on_grad=lambda event, arrays: ()):
    S = x.shape[0]
    tabs = _rope_tables(S)

    n1 = _rms_fwd("rms1_fwd", x, norm1_g)
    z_block = IN_MAIN // Z_W
    proj = _mm_nt("in_proj", n1, w_in_t, 1024, 1536, BF16, n_out=IN_MAIN)
    z = _matmul(
        "in_proj_z",
        [(n1, pl.BlockSpec((1024, D_MODEL), lambda i: (i, 0)), w_in_t, pl.BlockSpec((Z_W, D_MODEL), lambda i: (z_block, 0)), NT)],
        (S // 1024,), jax.ShapeDtypeStruct((S, Z_W), BF16), pl.BlockSpec((1024, Z_W), lambda i: (i, 0)), 1)
    qk = _rope_fwd(proj, tabs)
    branch = [_attn_fwd(qk, proj, d) for d in DILATIONS]
    ao, o_attn, lse = _attn_combine([b[0] for b in branch], [b[1] for b in branch], attn_norm_g)
    gates = _gates_fwd(z, wg, gate_bias)
    o_f, *kept_f = _gla_fwd(proj, gates, False)
    o_gla, *kept_b = _gla_fwd(proj, gates, True, o_prev=o_f)
    cat = _gla_out_fwd(o_gla, proj, gla_norm_g, ao)
    h1 = _mm_nn("out_proj", cat, w_out, 1024, 1024, F32, res=x)
    n2 = _rms_fwd("rms2_fwd", h1, norm2_g)
    gp = _mm_nn_sharded("ffn_gate", n2, w_gate4, 1024, BF16)
    up = _mm_nn_sharded("ffn_up", n2, w_up4, 1024, BF16)
    act, gate = _glu_fwd(gp, up, conv_w, conv_b)
    tk = D_FF // N_CHIPS
    h2 = _mm_nn("ffn_down", act, w_down, 1024, 512, F32, res=h1)
    loss_row, d_final_g, dh2, dh2_b = _final_loss(h2, final_norm_g.reshape(1, D_MODEL), target)

    dact = _mm_nt("ffn_down_bwd", dh2_b, w_down, 1024, tk, BF16)
    dup, dgp, d_conv_w, d_conv_b = _glu_bwd(gate, gp, up, dact, conv_w)
    d_w_down = _mm_tn("ffn_down_wgrad", act, dh2_b, 512, D_MODEL, 2048, BF16)
    on_grad("w_down", dict(w_down=d_w_down))
    dgp = _after(dgp, d_w_down)
    d_w_gate4 = _mm_tn("ffn_gate_wgrad", n2, dgp, 1024, tk, 2048, BF16, out3=tk)
    dup = _after(dup, d_w_gate4)
    d_w_up4 = _mm_tn("ffn_up_wgrad", n2, dup, 1024, tk, 2048, BF16, out3=tk)
    held = on_grad("w_gate_w_up", dict(w_gate=d_w_gate4, w_up=d_w_up4))
    dgp = _after(dgp, d_w_up4, *held)
    shard_pairs = [
        (g, pl.BlockSpec((512, tk), functools.partial(lambda s, j, i: (i, s), s)),
         w4, pl.BlockSpec((None, 512, tk), functools.partial(lambda s, j, i: (s, j, 0), s)), NT)
        for g, w4 in ((dgp, w_gate4), (dup, w_up4)) for s in range(N_CHIPS)]
    dn2 = _matmul("ffn_in_bwd", shard_pairs, (D_MODEL // 512, S // 512), jax.ShapeDtypeStruct((S, D_MODEL), BF16),
                  pl.BlockSpec((512, 512), lambda j, i: (i, j)), 1)
    dh1, dh1_b, d_norm2_g = _rms_bwd("rms2_bwd", h1, norm2_g, dn2, dh2)

    d_w_out = _mm_tn("out_proj_wgrad", cat, dh1_b, D_MODEL, 1024, 1024, BF16)
    held = on_grad("w_out", dict(w_out=d_w_out))
    dcat = _mm_nt("out_proj_bwd", _after(dh1_b, d_w_out, *held), w_out, 1024, 1024, BF16)
    do_attn, delta, d_attn_norm_g = _attn_norm_bwd(o_attn, attn_norm_g, dcat)
    dq_a, dkv_a = None, None
    for d in DILATIONS:
        dq_a = _attn_bwd_dq(qk, proj, do_attn, lse, delta, d, dq_a)
        dkv_a = _attn_bwd_dkv(qk, proj, do_attn, lse, delta, d, dkv_a)
    dproj = _attn_grad_merge(dq_a, dkv_a[0], dkv_a[1], tabs)
    held = on_grad("mid", dict(anchor=dproj))
    do_gla, dproj, d_gla_norm_g = _gla_out_bwd(o_gla, proj, gla_norm_g, _after(dcat, *held), dproj)
    dq_f, dk_f, dv_f, dg_f = _gla_bwd(proj, kept_f, do_gla, False)
    dgq, dgk, dgv, dg_b = _gla_bwd(proj, kept_b, do_gla, True, prev=(dq_f, dk_f, dv_f))
    dz, d_wg, d_gate_bias = _gates_bwd(z, wg, gate_bias, dg_f, dg_b)
    dproj = lax.dynamic_update_slice(dproj, jnp.concatenate([dgq, dgk, dgv], axis=1), (0, 3 * ATTN_W))
    d_w_in_t = _mm_tn("in_proj_wgrad", dproj, n1, 768, D_MODEL, 2048, BF16, rows_out=IN_W)
    n_tok = S // 1024
    d_w_in_t = _matmul(
        "in_proj_z_wgrad",
        [(dz, pl.BlockSpec((1024, Z_W), lambda i, j, k: (k, 0)), n1, pl.BlockSpec((1024, D_MODEL), lambda i, j, k: (k, 0)), TN)],
        (1, 1, n_tok), jax.ShapeDtypeStruct((IN_W, D_MODEL), BF16), pl.BlockSpec((Z_W, D_MODEL), lambda i, j, k: (z_block, 0)),
        n_tok, into=d_w_in_t)
    held = on_grad("w_in", dict(w_in_t=d_w_in_t))
    n_rows = S // 1024
    head_rows = max(1, n_rows // 8)

    def in_proj_bwd(name, first, count, a, into):
        return _matmul(
            name,
            [(a, pl.BlockSpec((1024, IN_MAIN), lambda j, i: (i + first, 0)), w_in_t, pl.BlockSpec((IN_MAIN, 512), lambda j, i: (0, j)), NN),
             (dz, pl.BlockSpec((1024, Z_W), lambda j, i: (i + first, 0)), w_in_t, pl.BlockSpec((Z_W, 512), lambda j, i: (z_block, j)), NN)],
            (D_MODEL // 512, count), jax.ShapeDtypeStruct((S, D_MODEL), BF16),
            pl.BlockSpec((1024, 512), lambda j, i: (i + first, j)), 1, into=into)

    dproj = _after(dproj, d_w_in_t, *held)
    dn1 = in_proj_bwd("in_proj_bwd_a", 0, head_rows, dproj, None)
    held = on_grad("last", dict(last=dn1))
    dn1 = in_proj_bwd("in_proj_bwd_b", head_rows, n_rows - head_rows, dproj, _after(dn1, *held))
    grad_x, _, d_norm1_g = _rms_bwd("rms1_bwd", x, norm1_g, dn1, dh1)

    big = dict(w_in_t=d_w_in_t, w_out=d_w_out, w_gate4=d_w_gate4, w_up4=d_w_up4, w_down=d_w_down)
    small = dict(loss=loss_row, norm1_g=d_norm1_g, wg=d_wg, gate_bias=d_gate_bias, gla_norm_g=d_gla_norm_g,
                 attn_norm_g=d_attn_norm_g, norm2_g=d_norm2_g, conv_w=d_conv_w, conv_b=d_conv_b, final_norm_g=d_final_g)
    return grad_x, big, small


def _position():
    return lax.axis_index("x"), lax.axis_index("y"), lax.axis_index("c")


def _other_chips(x, y):
    return [(1 - x, y), (x, 1 - y), (1 - x, 1 - y)]


def _gather_chips_async(name, shards, collective_id):
    n = len(shards)

    def body(*refs):
        ins, outs = refs[:n], refs[n:2 * n]
        send, recv, loc = refs[2 * n:]
        x, y, c = _position()
        me = 2 * x + y
        chips = _other_chips(x, y)
        barrier = pltpu.get_barrier_semaphore()
        for px, py in chips:
            pl.semaphore_signal(barrier, inc=1, device_id=(px, py, c), device_id_type=MESH)
        pl.semaphore_wait(barrier, len(chips))
        started = []
        for w in range(n):
            own = pltpu.make_async_copy(ins[w], outs[w].at[me], loc.at[w])
            own.start()
            started.append(own)
        sends = []
        for w in range(n):
            for j, (px, py) in enumerate(chips):
                cp = pltpu.make_async_remote_copy(ins[w], outs[w].at[me], send.at[3 * w + j], recv.at[3 * w + j],
                                                  device_id=(px, py, c), device_id_type=MESH)
                cp.start()
                sends.append(cp)
        for w in range(n):
            for j, (px, py) in enumerate(chips):
                pltpu.make_async_remote_copy(ins[w], outs[w].at[2 * px + py], send.at[3 * w + j], recv.at[3 * w + j],
                                             device_id=(px, py, c), device_id_type=MESH).wait_recv()
        for cp in sends:
            cp.wait_send()
        for own in started:
            own.wait()

    return pl.kernel(
        body, name=name, mesh=_sequencer(),
        out_type=[jax.ShapeDtypeStruct((N_CHIPS,) + s.shape, s.dtype) for s in shards],
        scratch_types=[pltpu.SemaphoreType.DMA((3 * n,)), pltpu.SemaphoreType.DMA((3 * n,)), pltpu.SemaphoreType.DMA((n,))],
        compiler_params=pltpu.CompilerParams(collective_id=collective_id),
    )(*shards)


def _gather_halves_async(name, small, shard, collective_id):
    half = shard.shape[1] // 2

    def body(small_ref, shard_ref, small_out, out, send, recv, loc):
        x, y, c = _position()
        me = 2 * x + y
        sibling = (x, y, 1 - c)
        chips = _other_chips(x, y)
        barrier = pltpu.get_barrier_semaphore()
        for px, py in chips:
            pl.semaphore_signal(barrier, inc=1, device_id=(px, py, c), device_id_type=MESH)
        pl.semaphore_signal(barrier, inc=1, device_id=sibling, device_id_type=MESH)
        pl.semaphore_wait(barrier, len(chips) + 1)
        mine = pl.ds(pl.multiple_of(c * half, LANES), half)
        theirs = pl.ds(pl.multiple_of((1 - c) * half, LANES), half)
        own = [pltpu.make_async_copy(small_ref, small_out.at[me], loc.at[0]),
               pltpu.make_async_copy(shard_ref, out.at[me], loc.at[1])]
        for cp in own:
            cp.start()
        sends = []
        for j, (px, py) in enumerate(chips):
            sends.append(pltpu.make_async_remote_copy(small_ref, small_out.at[me], send.at[j], recv.at[j],
                                                      device_id=(px, py, c), device_id_type=MESH))
            sends.append(pltpu.make_async_remote_copy(shard_ref.at[:, mine], out.at[me, :, mine], send.at[3 + j], recv.at[3 + j],
                                                      device_id=(px, py, c), device_id_type=MESH))
        for cp in sends:
            cp.start()
        passed = []
        for j, (px, py) in enumerate(chips):
            slot = 2 * px + py
            pltpu.make_async_remote_copy(shard_ref.at[:, mine], out.at[slot, :, mine], send.at[3 + j], recv.at[3 + j],
                                         device_id=(px, py, c), device_id_type=MESH).wait_recv()
            cp = pltpu.make_async_remote_copy(out.at[slot, :, mine], out.at[slot, :, mine], send.at[6 + j], recv.at[6 + j],
                                              device_id=sibling, device_id_type=MESH)
            cp.start()
            passed.append(cp)
        for j, (px, py) in enumerate(chips):
            slot = 2 * px + py
            pltpu.make_async_remote_copy(small_ref, small_out.at[slot], send.at[j], recv.at[j],
                                         device_id=(px, py, c), device_id_type=MESH).wait_recv()
            pltpu.make_async_remote_copy(out.at[slot, :, theirs], out.at[slot, :, theirs], send.at[6 + j], recv.at[6 + j],
                                         device_id=sibling, device_id_type=MESH).wait_recv()
        for cp in sends + passed:
            cp.wait_send()
        for cp in own:
            cp.wait()

    return pl.kernel(
        body, name=name, mesh=_sequencer(),
        out_type=[jax.ShapeDtypeStruct((N_CHIPS,) + small.shape, small.dtype),
                  jax.ShapeDtypeStruct((N_CHIPS,) + shard.shape, shard.dtype)],
        scratch_types=[pltpu.SemaphoreType.DMA((9,)), pltpu.SemaphoreType.DMA((9,)), pltpu.SemaphoreType.DMA((2,))],
        compiler_params=pltpu.CompilerParams(collective_id=collective_id),
    )(small, shard)


def _sequencer():
    return plsc.ScalarSubcoreMesh(axis_name="sequencer", num_cores=1)


def _sibling_exchange_async(name, arrs, collective_id):
    n = len(arrs)

    def body(*refs):
        ins, outs = refs[:n], refs[n:2 * n]
        send, recv = refs[2 * n:]
        x, y, c = _position()
        sibling = (x, y, 1 - c)
        barrier = pltpu.get_barrier_semaphore()
        pl.semaphore_signal(barrier, inc=1, device_id=sibling, device_id_type=MESH)
        pl.semaphore_wait(barrier, 1)
        copies = [pltpu.make_async_remote_copy(ins[w], outs[w], send.at[w], recv.at[w], device_id=sibling,
                                               device_id_type=MESH) for w in range(n)]
        for cp in copies:
            cp.start()
        for cp in copies:
            cp.wait()

    return pl.kernel(
        body, name=name, out_type=[jax.ShapeDtypeStruct(a.shape, a.dtype) for a in arrs],
        scratch_types=[pltpu.SemaphoreType.DMA((n,)), pltpu.SemaphoreType.DMA((n,))],
        compiler_params=pltpu.CompilerParams(collective_id=collective_id), mesh=_sequencer(),
    )(*arrs)


def _scatter_chips_async(name, parts, collective_id):
    n = len(parts)

    def body(*refs):
        ins, outs = refs[:n], refs[n:2 * n]
        send, recv, loc = refs[2 * n:]
        x, y, c = _position()
        me = 2 * x + y
        chips = _other_chips(x, y)
        barrier = pltpu.get_barrier_semaphore()
        for px, py in chips:
            pl.semaphore_signal(barrier, inc=1, device_id=(px, py, c), device_id_type=MESH)
        pl.semaphore_wait(barrier, len(chips))
        started = []
        for w in range(n):
            own = pltpu.make_async_copy(ins[w].at[me], outs[w].at[me], loc.at[w])
            own.start()
            started.append(own)
        sends = []
        for w in range(n):
            for j, (px, py) in enumerate(chips):
                cp = pltpu.make_async_remote_copy(ins[w].at[2 * px + py], outs[w].at[me], send.at[3 * w + j],
                                                  recv.at[3 * w + j], device_id=(px, py, c), device_id_type=MESH)
                cp.start()
                sends.append(cp)
        for w in range(n):
            for j, (px, py) in enumerate(chips):
                pltpu.make_async_remote_copy(ins[w].at[me], outs[w].at[2 * px + py], send.at[3 * w + j], recv.at[3 * w + j],
                                             device_id=(px, py, c), device_id_type=MESH).wait_recv()
        for cp in sends:
            cp.wait_send()
        for own in started:
            own.wait()

    return pl.kernel(
        body, name=name, out_type=[jax.ShapeDtypeStruct(p.shape, p.dtype) for p in parts],
        scratch_types=[pltpu.SemaphoreType.DMA((3 * n,)), pltpu.SemaphoreType.DMA((3 * n,)), pltpu.SemaphoreType.DMA((n,))],
        compiler_params=pltpu.CompilerParams(collective_id=collective_id), mesh=_sequencer(),
    )(*parts)


def _allreduce_rows(buf):
    R = buf.shape[0]

    def body(in_ref, out_ref, land, send, recv):
        x, y, c = _position()
        me = 4 * x + 2 * y + c
        land[pl.ds(me, 1)] = in_ref[...][None]
        peers = []
        for mask in range(1, N_DEV):
            px = 1 - x if mask & 4 else x
            py = 1 - y if mask & 2 else y
            pc = 1 - c if mask & 1 else c
            peers.append((px, py, pc))
        sends = []
        for k, peer in enumerate(peers):
            cp = pltpu.make_async_remote_copy(in_ref, land.at[me], send.at[k], recv.at[k], device_id=peer, device_id_type=MESH)
            cp.start()
            sends.append(cp)
        for k, (px, py, pc) in enumerate(peers):
            pltpu.make_async_remote_copy(in_ref, land.at[4 * px + 2 * py + pc], send.at[k], recv.at[k],
                                         device_id=(px, py, pc), device_id_type=MESH).wait_recv()
        for cp in sends:
            cp.wait_send()
        tot = land[0]
        for i in range(1, N_DEV):
            tot = tot + land[i]
        out_ref[...] = tot

    vm = pl.BlockSpec(memory_space=pltpu.VMEM)
    return pl.pallas_call(
        body, name="allreduce_small", in_specs=[vm], out_specs=vm, out_shape=jax.ShapeDtypeStruct((R, LANES), F32),
        scratch_shapes=[pltpu.VMEM((N_DEV, R, LANES), F32), pltpu.SemaphoreType.DMA((N_DEV - 1,)),
                        pltpu.SemaphoreType.DMA((N_DEV - 1,))],
    )(buf)


def _tile2d(r, c, cap):
    if r <= cap:
        return r, c
    fits = [t for t in range(16, cap + 1, 16) if r % t == 0]
    return (max(fits), c) if fits else (r, 256)


def _pair_sum(name, a, b):
    n, r, c = a.shape
    tr, tc = _tile2d(r, c, 1024)

    def body(a_ref, b_ref, o_ref):
        o_ref[...] = (a_ref[...].astype(F32) + b_ref[...].astype(F32)).astype(BF16)

    blk = pl.BlockSpec((None, tr, tc), lambda s, i, j: (s, i, j))
    return pl.pallas_call(
        body, name=name, grid=(n, r // tr, c // tc), in_specs=[blk, blk], out_specs=blk,
        out_shape=jax.ShapeDtypeStruct(a.shape, BF16),
    )(a, b)


def _adamw_math(w, m, v, g):
    m2 = ADAM_B1 * m + (1.0 - ADAM_B1) * g
    v2 = ADAM_B2 * v + (1.0 - ADAM_B2) * (g * g)
    m_hat = m2 / (1.0 - ADAM_B1 ** ADAM_STEP)
    v_hat = v2 / (1.0 - ADAM_B2 ** ADAM_STEP)
    delta = -ADAM_LR * (m_hat / (jnp.sqrt(v_hat) + ADAM_EPS) + ADAM_WD * w)
    return delta, m2, v2


def _adamw(name, w, m, v, g):
    r, c = w.shape
    stacked = g.ndim == 3
    tr, tc = _tile2d(r, c, 256)

    def body(w_ref, m_ref, v_ref, g_ref, go_ref, d_ref, m2_ref, v2_ref):
        if stacked:
            gv = g_ref[0].astype(F32)
            for i in range(1, N_CHIPS):
                gv = gv + g_ref[i].astype(F32)
        else:
            gv = g_ref[...]
        delta, m2, v2 = _adamw_math(w_ref[...], m_ref[...], v_ref[...], gv)
        go_ref[...] = gv
        d_ref[...] = delta
        m2_ref[...] = m2
        v2_ref[...] = v2

    blk = pl.BlockSpec((tr, tc), lambda i, j: (i, j))
    g_spec = pl.BlockSpec((N_CHIPS, tr, tc), lambda i, j: (0, i, j)) if stacked else blk
    out = jax.ShapeDtypeStruct((r, c), F32)
    return pl.pallas_call(
        body, name=name, grid=(r // tr, c // tc), in_specs=[blk, blk, blk, g_spec], out_specs=[blk] * 4, out_shape=[out] * 4,
    )(w, m, v, g)


def _pack_rows(pieces):
    flat = jnp.concatenate([p.reshape(-1) for p in pieces])
    rows = flat.shape[0] // LANES
    pad = (-rows) % 8
    return jnp.pad(flat.reshape(rows, LANES), ((0, pad), (0, 0)))


def _unpack_rows(buf, shapes):
    flat = buf.reshape(-1)
    out, at = [], 0
    for s in shapes:
        size = math.prod(s)
        out.append(flat[at:at + size].reshape(s))
        at += size
    return out


SMALL_NAMES = ("norm1_g", "gf_up", "gf_b", "gb_up", "gb_b", "gla_norm_g", "attn_norm_g", "norm2_g", "conv_w", "conv_b",
               "final_norm_g")
BIG_NAMES = ("w_in", "w_out", "w_gate", "w_up", "w_down")
WEIGHT_ORDER = ("norm1_g", "w_in", "gf_up", "gf_b", "gb_up", "gb_b", "gla_norm_g", "attn_norm_g", "w_out", "norm2_g",
                "w_gate", "w_up", "conv_w", "conv_b", "w_down", "final_norm_g")


def kernel(x, norm1_g, w_in, gf_up, gf_b, gb_up, gb_b, gla_norm_g, attn_norm_g, w_out, norm2_g, w_gate, w_up, conv_w, conv_b, w_down, final_norm_g, loss_target, m_norm1_g, m_w_in, m_gf_up, m_gf_b, m_gb_up, m_gb_b, m_gla_norm_g, m_attn_norm_g, m_w_out, m_norm2_g, m_w_gate, m_w_up, m_conv_w, m_conv_b, m_w_down, m_final_norm_g, v_norm1_g, v_w_in, v_gf_up, v_gf_b, v_gb_up, v_gb_b, v_gla_norm_g, v_attn_norm_g, v_w_out, v_norm2_g, v_w_gate, v_w_up, v_conv_w, v_conv_b, v_w_down, v_final_norm_g):
    w = dict(norm1_g=norm1_g, w_in=w_in, gf_up=gf_up, gf_b=gf_b, gb_up=gb_up, gb_b=gb_b, gla_norm_g=gla_norm_g,
             attn_norm_g=attn_norm_g, w_out=w_out, norm2_g=norm2_g, w_gate=w_gate, w_up=w_up, conv_w=conv_w, conv_b=conv_b,
             w_down=w_down, final_norm_g=final_norm_g)
    m = dict(norm1_g=m_norm1_g, w_in=m_w_in, gf_up=m_gf_up, gf_b=m_gf_b, gb_up=m_gb_up, gb_b=m_gb_b, gla_norm_g=m_gla_norm_g,
             attn_norm_g=m_attn_norm_g, w_out=m_w_out, norm2_g=m_norm2_g, w_gate=m_w_gate, w_up=m_w_up, conv_w=m_conv_w,
             conv_b=m_conv_b, w_down=m_w_down, final_norm_g=m_final_norm_g)
    v = dict(norm1_g=v_norm1_g, w_in=v_w_in, gf_up=v_gf_up, gf_b=v_gf_b, gb_up=v_gb_up, gb_b=v_gb_b, gla_norm_g=v_gla_norm_g,
             attn_norm_g=v_attn_norm_g, w_out=v_w_out, norm2_g=v_norm2_g, w_gate=v_w_gate, w_up=v_w_up, conv_w=v_conv_w,
             conv_b=v_conv_b, w_down=v_w_down, final_norm_g=v_final_norm_g)
    S = x.shape[1]
    chip = 2 * lax.axis_index("x") + lax.axis_index("y")
    n_in = IN_W // N_CHIPS
    n_ff = D_FF // N_CHIPS
    n_gk = GLA_K // N_CHIPS

    def owned(t):
        return {k: (jnp.transpose(t[k][0]) if k == "w_in" else t[k][0]) for k in BIG_NAMES}

    own_w, own_m, own_v = owned(w), owned(m), owned(v)
    shard = {k: own_w[k].astype(BF16) for k in BIG_NAMES}
    small_shard = _pack_rows([gf_up[0], gb_up[0], conv_w[0]])
    small4, w_in4 = _gather_halves_async("gather_w_in", small_shard, shard["w_in"], 0)
    w_out4, w_gate4, w_up4 = _gather_chips_async("gather_w_mid", [shard["w_out"], shard["w_gate"], shard["w_up"]], 1)
    (w_down4,) = _gather_chips_async("gather_w_down", [shard["w_down"]], 2)
    w_in_t = w_in4.reshape(IN_W, D_MODEL)
    rows_up = GATE_RANK * n_gk // LANES
    rows_cw = 3 * n_ff // LANES
    gf_full = jnp.transpose(small4[:, 0:rows_up].reshape(N_CHIPS, GATE_RANK, n_gk), (1, 0, 2)).reshape(GATE_RANK, GLA_K)
    gb_full = jnp.transpose(small4[:, rows_up:2 * rows_up].reshape(N_CHIPS, GATE_RANK, n_gk), (1, 0, 2)).reshape(GATE_RANK, GLA_K)
    cw_full = jnp.transpose(small4[:, 2 * rows_up:2 * rows_up + rows_cw].reshape(N_CHIPS, 3, n_ff), (1, 0, 2)).reshape(3, D_FF)
    wg = jnp.zeros((Z_W, 2 * GLA_K), F32)
    wg = wg.at[0:GATE_RANK, 0:GLA_K].set(gf_full).at[GATE_RANK:2 * GATE_RANK, GLA_K:].set(gb_full).astype(BF16)
    gate_bias = jnp.concatenate([gf_b, gb_b], axis=1)

    pending, contributions, next_id = [], {}, [3]

    def as_shards(group, arrays):
        if group == "w_in":
            return dict(w_in=arrays["w_in_t"].reshape(N_CHIPS, n_in, D_MODEL))
        if group == "w_out":
            return dict(w_out=arrays["w_out"].reshape(N_CHIPS, D_MODEL // N_CHIPS, D_MODEL))
        if group == "w_down":
            return dict(w_down=arrays["w_down"].reshape(N_CHIPS, n_ff, D_MODEL))
        return arrays

    out = {}

    def swap(group, arrays):
        mine = as_shards(group, arrays)
        pending.append((group, mine, _sibling_exchange_async(f"sibling_{group}", list(mine.values()), next_id[0])))
        next_id[0] += 1

    def sum_and_send(anchor):
        tag, mine, theirs = pending.pop()
        sums = [_pair_sum(f"pair_sum_{k}", mine[k], _after(t, *anchor)) for k, t in zip(mine, theirs)]
        contributions.update(zip(mine, _scatter_chips_async(f"scatter_{tag}", sums, next_id[0])))
        next_id[0] += 1
        return sums

    def update(names, anchor):
        for k in names:
            res = _adamw(f"adamw_{k}", own_w[k], own_m[k], own_v[k], _after(contributions[k], *anchor))
            out[k] = [(jnp.transpose(r) if k == "w_in" else r)[None] for r in res]
        return [out[k][0] for k in names]

    def on_grad(event, arrays):
        anchor = list(arrays.values())
        held = []
        if event in ("w_gate_w_up", "w_out", "mid", "last"):
            held += sum_and_send(anchor)
        if event == "mid":
            held += update(("w_down", "w_gate", "w_up"), anchor)
        if event in ("w_down", "w_gate_w_up", "w_out", "w_in"):
            swap(event, arrays)
        return held

    grad_x, _, small = _local_step(
        x[0], loss_target[0], norm1_g, w_in_t, wg, gate_bias, gla_norm_g, attn_norm_g,
        w_out4.reshape(D_MODEL, D_MODEL), norm2_g, w_gate4, w_up4, cw_full, conv_b, w_down4.reshape(D_FF, D_MODEL), final_norm_g,
        on_grad=on_grad)
    update(("w_out", "w_in"), [grad_x])

    d_gf_up = small["wg"][0:GATE_RANK, 0:GLA_K]
    d_gb_up = small["wg"][GATE_RANK:2 * GATE_RANK, GLA_K:]
    pieces = [small["loss"], small["norm1_g"], d_gf_up, small["gate_bias"][:, :GLA_K], d_gb_up, small["gate_bias"][:, GLA_K:],
              small["gla_norm_g"], small["attn_norm_g"], small["norm2_g"], small["conv_w"], small["conv_b"], small["final_norm_g"]]
    total = _allreduce_rows(_pack_rows(pieces))
    summed = _unpack_rows(total, [p.shape for p in pieces])
    loss = summed[0][0, 0]
    g_small = dict(zip(SMALL_NAMES, summed[1:]))
    g_small["gf_up"] = lax.dynamic_slice_in_dim(g_small["gf_up"], chip * n_gk, n_gk, axis=1)
    g_small["gb_up"] = lax.dynamic_slice_in_dim(g_small["gb_up"], chip * n_gk, n_gk, axis=1)
    g_small["conv_w"] = lax.dynamic_slice_in_dim(g_small["conv_w"], chip * n_ff, n_ff, axis=1)
    packed = [_pack_rows([t[k] for k in SMALL_NAMES]) for t in (w, m, v, g_small)]
    res = _adamw("adamw_small", *packed)
    shapes = [w[k].shape for k in SMALL_NAMES]
    for k, vals in zip(SMALL_NAMES, zip(*[_unpack_rows(r, shapes) for r in res])):
        out[k] = list(vals)

    grads, deltas, new_m, new_v = ([out[k][i] for k in WEIGHT_ORDER] for i in range(4))
    return (loss, grad_x[None], *grads, *deltas, *new_m, *new_v)
```

```python
import functools
import math

import jax
import jax.numpy as jnp
from jax import lax
from jax.experimental import pallas as pl
from jax.experimental.pallas import tpu as pltpu
from jax.experimental.pallas import tpu_sc as plsc

F32 = jnp.float32
BF16 = jnp.bfloat16

D_MODEL = 2048
ATTN_W = 1024
HEAD = 128
N_HEADS = 8
N_SIDE = 64
DILATIONS = (1, 4, 16)
ROPE_THETA = 500000.0
ROPE_DIM = 32
GLA_K = 512
GLA_V = 1024
GLA_HEADS = 4
GLA_DK = 128
GLA_DV = 256
GATE_RANK = 16
GATE_NORM = 16.0
CHUNK = 64
IN_MAIN = 6144
IN_W = 6176
Z_W = IN_W - IN_MAIN
D_FF = 5632
EPS = 1e-6
N_CHIPS = 4
N_DEV = 8
LANES = 128

ADAM_LR = 0.001
ADAM_B1 = 0.9
ADAM_B2 = 0.999
ADAM_EPS = 1e-08
ADAM_WD = 0.01
ADAM_STEP = 10

NEG = -1e30
MESH = pl.DeviceIdType.MESH
ANY = pl.BlockSpec(memory_space=pl.ANY)

NN = ((1,), (0,))
NT = ((1,), (1,))
TN = ((0,), (0,))


def _dot(a, b, dims=NN):
    return lax.dot_general(a, b, (dims, ((), ())), preferred_element_type=F32)


def _sigmoid(x):
    return 0.5 * jnp.tanh(0.5 * x) + 0.5


def _silu(x):
    h = 0.5 * x
    return h * jnp.tanh(h) + h


def _after(x, *deps):
    return lax.optimization_barrier((x,) + deps)[0]


def _matmul(name, pairs, grid, out_shape, out_spec, nk, res=None, into=None, first=None):
    n_in = 2 * len(pairs) + (res is not None)
    dims = [p[4] for p in pairs]

    n_ops = n_in + (into is not None) + 2 * (first is not None)

    def body(*refs):
        ins, o_ref = refs[:n_in], refs[n_ops]

        def partial_sum():
            tot = None
            for p, dn in enumerate(dims):
                a, b = ins[2 * p][...], ins[2 * p + 1][...]
                t = _dot(a.astype(BF16), b.astype(BF16), dn)
                tot = t if tot is None else tot + t
            return tot

        if nk == 1:
            t = partial_sum()
            if res is not None:
                t = t + ins[-1][...]
            o_ref[...] = t.astype(o_ref.dtype)
        else:
            acc_ref = refs[n_ops + 1]
            k = pl.program_id(2)

            @pl.when(k == 0)
            def _():
                if first is not None:
                    start = _dot(refs[n_in][...].astype(BF16), refs[n_in + 1][...].astype(BF16), first[4])
                    acc_ref[...] = start + ins[-1][...] if res is not None else start
                elif res is not None:
                    acc_ref[...] = ins[-1][...]
                else:
                    acc_ref[...] = jnp.zeros_like(acc_ref)

            acc_ref[...] += partial_sum()

            @pl.when(k == nk - 1)
            def _():
                o_ref[...] = acc_ref[...].astype(o_ref.dtype)

    operands, in_specs = [], []
    for a, a_spec, b, b_spec, _ in pairs:
        operands += [a, b]
        in_specs += [a_spec, b_spec]
    if res is not None:
        operands.append(res[0])
        in_specs.append(res[1])
    if first is not None:
        assert nk > 1
        operands += [first[0], first[2]]
        in_specs += [first[1], first[3]]
    acc_shape = tuple(s for s in out_spec.block_shape if s is not None)
    scratch = [pltpu.VMEM(acc_shape, F32)] if nk > 1 else []
    aliases = {}
    if into is not None:
        aliases = {len(operands): 0}
        operands.append(into)
        in_specs.append(ANY)
    return pl.pallas_call(
        body, name=name, grid=grid, in_specs=in_specs, out_specs=out_spec, out_shape=out_shape, scratch_shapes=scratch,
        input_output_aliases=aliases,
    )(*operands)


def _mm_nn(name, a, b, tm, tn, out_dtype, res=None):
    M, K = a.shape
    N = b.shape[1]
    pairs = [(a, pl.BlockSpec((tm, K), lambda j, i: (i, 0)), b, pl.BlockSpec((K, tn), lambda j, i: (0, j)), NN)]
    r = None if res is None else (res, pl.BlockSpec((tm, tn), lambda j, i: (i, j)))
    return _matmul(name, pairs, (N // tn, M // tm), jax.ShapeDtypeStruct((M, N), out_dtype),
                   pl.BlockSpec((tm, tn), lambda j, i: (i, j)), 1, r)


def _mm_nn_sharded(name, a, b4, tm, out_dtype):
    M, K = a.shape
    n = b4.shape[2]
    pairs = [(a, pl.BlockSpec((tm, K), lambda j, i: (i, 0)), b4, pl.BlockSpec((None, K, n), lambda j, i: (j, 0, 0)), NN)]
    return _matmul(name, pairs, (N_CHIPS, M // tm), jax.ShapeDtypeStruct((M, N_CHIPS * n), out_dtype),
                   pl.BlockSpec((tm, n), lambda j, i: (i, j)), 1)


def _mm_nt(name, a, b, tm, tn, out_dtype, res=None, n_out=None):
    M, K = a.shape
    N = b.shape[0] if n_out is None else n_out
    pairs = [(a, pl.BlockSpec((tm, K), lambda j, i: (i, 0)), b, pl.BlockSpec((tn, K), lambda j, i: (j, 0)), NT)]
    r = None if res is None else (res, pl.BlockSpec((tm, tn), lambda j, i: (i, j)))
    return _matmul(name, pairs, (N // tn, M // tm), jax.ShapeDtypeStruct((M, N), out_dtype),
                   pl.BlockSpec((tm, tn), lambda j, i: (i, j)), 1, r)


def _mm_tn(name, a, g, tka, tn, tmm, out_dtype, out3=None, rows_out=None):
    M, Ka = a.shape
    N = g.shape[1]
    pairs = [(a, pl.BlockSpec((tmm, tka), lambda i, j, k: (k, i)), g, pl.BlockSpec((tmm, tn), lambda i, j, k: (k, j)), TN)]
    if out3 is None:
        shape, spec = (Ka if rows_out is None else rows_out, N), pl.BlockSpec((tka, tn), lambda i, j, k: (i, j))
    else:
        shape, spec = (N // out3, Ka, out3), pl.BlockSpec((None, tka, tn), lambda i, j, k: (j, i, 0))
    return _matmul(name, pairs, (Ka // tka, N // tn, M // tmm), jax.ShapeDtypeStruct(shape, out_dtype), spec, M // tmm)


def _rms_fwd(name, x, g, tm=512):
    S, D = x.shape

    def body(x_ref, g_ref, o_ref):
        xv = x_ref[...]
        r = lax.rsqrt(jnp.mean(xv * xv, axis=-1, keepdims=True) + EPS)
        o_ref[...] = (xv * r * g_ref[...]).astype(o_ref.dtype)

    return pl.pallas_call(
        body, name=name, grid=(S // tm,),
        in_specs=[pl.BlockSpec((tm, D), lambda i: (i, 0)), pl.BlockSpec((1, D), lambda i: (0, 0))],
        out_specs=pl.BlockSpec((tm, D), lambda i: (i, 0)), out_shape=jax.ShapeDtypeStruct((S, D), BF16),
    )(x, g)


def _rms_bwd(name, x, g, dn, dres, tm=512):
    S, D = x.shape

    def body(x_ref, g_ref, dn_ref, dres_ref, dx_ref, dxb_ref, dg_ref):
        i = pl.program_id(0)

        @pl.when(i == 0)
        def _():
            dg_ref[...] = jnp.zeros_like(dg_ref)

        xv = x_ref[...]
        r = lax.rsqrt(jnp.mean(xv * xv, axis=-1, keepdims=True) + EPS)
        xhat = xv * r
        dnv = dn_ref[...].astype(F32)
        dg_ref[...] += jnp.sum(dnv * xhat, axis=0, keepdims=True)
        t = dnv * g_ref[...]
        dx = r * (t - xhat * jnp.mean(t * xhat, axis=-1, keepdims=True)) + dres_ref[...]
        dx_ref[...] = dx
        dxb_ref[...] = dx.astype(BF16)

    row = pl.BlockSpec((tm, D), lambda i: (i, 0))
    vec = pl.BlockSpec((1, D), lambda i: (0, 0))
    return pl.pallas_call(
        body, name=name, grid=(S // tm,), in_specs=[row, vec, row, row], out_specs=[row, row, vec],
        out_shape=[jax.ShapeDtypeStruct((S, D), F32), jax.ShapeDtypeStruct((S, D), BF16), jax.ShapeDtypeStruct((1, D), F32)],
    )(x, g, dn, dres)


def _final_loss(h2, g, target, tm=512):
    S, D = h2.shape

    def body(x_ref, g_ref, t_ref, loss_ref, dg_ref, dx_ref, dxb_ref):
        i = pl.program_id(0)

        @pl.when(i == 0)
        def _():
            loss_ref[...] = jnp.zeros_like(loss_ref)
            dg_ref[...] = jnp.zeros_like(dg_ref)

        xv = x_ref[...]
        r = lax.rsqrt(jnp.mean(xv * xv, axis=-1, keepdims=True) + EPS)
        xhat = xv * r
        gv = g_ref[...]
        diff = xhat * gv - t_ref[...]
        per_tok = jnp.mean(diff * diff, axis=-1, keepdims=True)
        loss_ref[...] += 0.5 * jnp.sum(per_tok, axis=0, keepdims=True)
        dy = diff * (1.0 / D)
        dg_ref[...] += jnp.sum(dy * xhat, axis=0, keepdims=True)
        t = dy * gv
        dx = r * (t - xhat * jnp.mean(t * xhat, axis=-1, keepdims=True))
        dx_ref[...] = dx
        dxb_ref[...] = dx.astype(BF16)

    row = pl.BlockSpec((tm, D), lambda i: (i, 0))
    vec = pl.BlockSpec((1, D), lambda i: (0, 0))
    return pl.pallas_call(
        body, name="final_loss", grid=(S // tm,), in_specs=[row, vec, row],
        out_specs=[pl.BlockSpec((1, LANES), lambda i: (0, 0)), vec, row, row],
        out_shape=[jax.ShapeDtypeStruct((1, LANES), F32), jax.ShapeDtypeStruct((1, D), F32),
                   jax.ShapeDtypeStruct((S, D), F32), jax.ShapeDtypeStruct((S, D), BF16)],
    )(h2, g, target)


def _rope_tables(S):
    pos = jnp.arange(S, dtype=F32)
    inv_freq = ROPE_THETA ** (-jnp.arange(0, ROPE_DIM, 2, dtype=F32) / ROPE_DIM)
    ang = pos[:, None] * inv_freq[None, :]
    cos, sin = jnp.cos(ang), jnp.sin(ang)
    half = ROPE_DIM // 2
    rest = HEAD - ROPE_DIM
    z_h, z_r = jnp.zeros((S, half), F32), jnp.zeros((S, rest), F32)
    tab_c = jnp.concatenate([cos, cos, jnp.ones((S, rest), F32)], axis=1)
    tab_up = jnp.concatenate([z_h, sin, z_r], axis=1)
    tab_dn = jnp.concatenate([-sin, z_h, z_r], axis=1)
    return tab_c, tab_up, tab_dn


def _rope_head(t, c, up, dn):
    half = ROPE_DIM // 2
    return t * c + pltpu.roll(t, half, axis=1) * up + pltpu.roll(t, HEAD - half, axis=1) * dn


def _rope_fwd(proj, tabs, tm=512):
    S = proj.shape[0]
    W = 2 * ATTN_W

    def body(p_ref, c_ref, up_ref, dn_ref, o_ref):
        c, up, dn = c_ref[...], up_ref[...], dn_ref[...]
        for h in range(W // HEAD):
            sl = slice(h * HEAD, (h + 1) * HEAD)
            o_ref[:, sl] = _rope_head(p_ref[:, sl].astype(F32), c, up, dn).astype(BF16)

    tab = pl.BlockSpec((tm, HEAD), lambda i: (i, 0))
    return pl.pallas_call(
        body, name="rope_fwd", grid=(S // tm,), in_specs=[pl.BlockSpec((tm, W), lambda i: (i, 0)), tab, tab, tab],
        out_specs=pl.BlockSpec((tm, W), lambda i: (i, 0)), out_shape=jax.ShapeDtypeStruct((S, W), BF16),
    )(proj, *tabs)


def _attn_grad_merge(dq, dk, dv, tabs, tm=256):
    S = dq.shape[0]

    def body(q_ref, k_ref, v_ref, c_ref, up_ref, dn_ref, o_ref):
        c, up, dn = c_ref[...], up_ref[...], dn_ref[...]
        for h in range(N_HEADS):
            sl = slice(h * HEAD, (h + 1) * HEAD)
            for part, r in ((0, q_ref), (1, k_ref)):
                osl = slice(part * ATTN_W + h * HEAD, part * ATTN_W + (h + 1) * HEAD)
                o_ref[:, osl] = _rope_head(r[:, sl].astype(F32), c, -up, -dn).astype(BF16)
        o_ref[:, 2 * ATTN_W:] = v_ref[...]

    blk = pl.BlockSpec((tm, ATTN_W), lambda i: (i, 0))
    tab = pl.BlockSpec((tm, HEAD), lambda i: (i, 0))
    return pl.pallas_call(
        body, name="attn_grad_merge", grid=(S // tm,), in_specs=[blk] * 3 + [tab] * 3,
        out_specs=pl.BlockSpec((tm, 3 * ATTN_W), lambda i: (i, 0)), out_shape=jax.ShapeDtypeStruct((S, IN_MAIN), BF16),
    )(dq, dk, dv, *tabs)


SUB = 128
WIN = SUB + 2 * N_SIDE
Q_COL, K_COL, V_COL = 0, ATTN_W // HEAD, 2 * ATTN_W // HEAD


class _AttnGeo:
    def __init__(self, S, d):
        self.S, self.d, self.L = S, d, S // d
        self.halo = N_SIDE * d
        self.TB = min(2048, S)
        self.W = self.TB + 2 * self.halo
        self.n_sub = self.TB // SUB
        self.grid = (S // self.TB, N_HEADS)
        self.dt = F32 if d > 1 else BF16
        self.su = min(d, 4)
        self.sb = d // self.su
        assert self.TB % (SUB * d) == 0 and self.TB % self.halo == 0

    def specs(self, width, col0, per_head=True):
        ratio = self.TB // self.halo
        last = self.S // self.halo - 1
        col = (lambda h: col0 + h) if per_head else (lambda h: col0)
        cur = pl.BlockSpec((self.TB, width), lambda i, h: (i, col(h)))
        prev = pl.BlockSpec((self.halo, width), lambda i, h: (jnp.maximum(i * ratio - 1, 0), col(h)))
        nxt = pl.BlockSpec((self.halo, width), lambda i, h: (jnp.minimum((i + 1) * ratio, last), col(h)))
        return cur, prev, nxt

    def scratch(self, rows, dtype=None):
        nat = pltpu.VMEM((rows, LANES), self.dt if dtype is None else dtype)
        return [nat] if self.sb == 1 else [nat, pltpu.VMEM((rows, LANES), F32)]

    def bind(self, refs):
        nat = next(refs)
        return (nat, nat) if self.sb == 1 else (nat, next(refs))

    def spread(self, pair):
        nat, streams = pair
        if self.sb > 1:
            n = nat.shape[0] // self.sb
            for a in range(self.sb):
                streams[a * n:(a + 1) * n, :] = nat[pl.ds(a, n, stride=self.sb), :]
        return streams

    def gather(self, pair):
        nat, streams = pair
        if self.sb > 1:
            n = nat.shape[0] // self.sb
            for a in range(self.sb):
                nat[pl.ds(a, n, stride=self.sb), :] = streams[a * n:(a + 1) * n, :]
        return nat

    def rows(self, sub, n, total):
        res, blk = sub % self.d, sub // self.d
        a, b = res % self.sb, res // self.sb
        start = a * (total // self.sb) + b + self.su * SUB * blk
        return pl.ds(start, n, stride=self.su) if self.su > 1 else pl.ds(start, n)

    def band(self):
        row = lax.broadcasted_iota(jnp.int32, (SUB, WIN), 0)
        col = lax.broadcasted_iota(jnp.int32, (SUB, WIN), 1)
        return (col >= row) & (col <= row + 2 * N_SIDE), col

    def mask(self, sub, band):
        inside, col = band
        blk, n_blk = sub // self.d, self.TB // (SUB * self.d)
        base = pl.program_id(0) * (self.TB // self.d) + SUB * blk
        if blk == 0:
            inside = inside & (col >= N_SIDE - base)
        if blk == n_blk - 1:
            inside = inside & (col < self.L + N_SIDE - base)
        return inside

    def fill(self, dst, c_ref):
        dst[...] = c_ref[...].astype(dst.dtype)

    def fill_window(self, dst, p_ref, c_ref, n_ref):
        dst[0:self.halo] = p_ref[...].astype(dst.dtype)
        dst[self.halo:self.halo + self.TB] = c_ref[...].astype(dst.dtype)
        dst[self.halo + self.TB:] = n_ref[...].astype(dst.dtype)


def _lane_of(tile, h):
    lane = lax.broadcasted_iota(jnp.int32, tile.shape, 1)
    return jnp.sum(jnp.where(lane == h, tile, 0.0), axis=1, keepdims=True)


def _attn_fwd(qk, proj, d):
    S = qk.shape[0]
    geo = _AttnGeo(S, d)
    scale = HEAD ** -0.5

    def body(q_ref, kp, kc, kn, vp, vc, vn, o_ref, lse_ref, *scratch):
        h = pl.program_id(1)
        refs = iter(scratch)
        q_p, k_p, v_p, o_p, l_p = (geo.bind(refs) for _ in range(5))
        geo.fill(q_p[0], q_ref)
        geo.fill_window(k_p[0], kp, kc, kn)
        geo.fill_window(v_p[0], vp, vc, vn)
        qs, ks, vs = geo.spread(q_p), geo.spread(k_p), geo.spread(v_p)
        os, ls = o_p[1], l_p[1]
        band = geo.band()
        for sub in range(geo.n_sub):
            rq, rw = geo.rows(sub, SUB, geo.TB), geo.rows(sub, WIN, geo.W)
            q_r, k_r, v_r = qs[rq, :].astype(BF16), ks[rw, :].astype(BF16), vs[rw, :].astype(BF16)
            s = jnp.where(geo.mask(sub, band), _dot(q_r, k_r, NT) * scale, NEG)
            m = jnp.max(s, axis=1, keepdims=True)
            p = jnp.exp(s - m)
            l = jnp.sum(p, axis=1, keepdims=True)
            os[rq, :] = _dot(p.astype(BF16), v_r) / l
            ls[rq, :] = jnp.broadcast_to(m + jnp.log(l), (SUB, LANES))
        o_ref[...] = geo.gather(o_p)[...].astype(BF16)

        @pl.when(h == 0)
        def _():
            lse_ref[...] = jnp.zeros_like(lse_ref)

        lane = lax.broadcasted_iota(jnp.int32, (geo.TB, LANES), 1)
        lse_ref[...] = jnp.where(lane == h, geo.gather(l_p)[...], lse_ref[...])

    q_cur, _, _ = geo.specs(HEAD, Q_COL)
    k_specs = geo.specs(HEAD, K_COL)
    v_specs = geo.specs(HEAD, V_COL)
    stat = pl.BlockSpec((geo.TB, LANES), lambda i, h: (i, 0))
    return pl.pallas_call(
        body, name=f"attn_fwd_d{d}", grid=geo.grid,
        in_specs=[q_cur, k_specs[1], k_specs[0], k_specs[2], v_specs[1], v_specs[0], v_specs[2]],
        out_specs=[q_cur, stat],
        out_shape=[jax.ShapeDtypeStruct((S, ATTN_W), BF16), jax.ShapeDtypeStruct((S, LANES), F32)],
        scratch_shapes=(geo.scratch(geo.TB) + geo.scratch(geo.W) + geo.scratch(geo.W) + geo.scratch(geo.TB, F32)
                        + geo.scratch(geo.TB, F32)),
    )(qk, qk, qk, qk, proj, proj, proj)


def _attn_combine(outs, lses, g, tm=256):
    S = outs[0].shape[0]

    def body(o1, o2, o3, l1, l2, l3, g_ref, ao_ref, o_ref, lse_ref):
        a1, a2, a3 = l1[...], l2[...], l3[...]
        mx = jnp.maximum(jnp.maximum(a1, a2), a3)
        e1, e2, e3 = jnp.exp(a1 - mx), jnp.exp(a2 - mx), jnp.exp(a3 - mx)
        den = e1 + e2 + e3
        lse_ref[...] = mx + jnp.log(den)
        head_of_col = lax.broadcasted_iota(jnp.int32, (LANES, ATTN_W), 1) // HEAD
        spread = (lax.broadcasted_iota(jnp.int32, (LANES, ATTN_W), 0) == head_of_col).astype(BF16)

        def wide(e):
            wgt = e / den
            hi = wgt.astype(BF16)
            lo = (wgt - hi.astype(F32)).astype(BF16)
            return _dot(hi, spread) + _dot(lo, spread)

        ov = wide(e1) * o1[...].astype(F32) + wide(e2) * o2[...].astype(F32) + wide(e3) * o3[...].astype(F32)
        o_ref[...] = ov
        r = lax.rsqrt(jnp.mean(ov * ov, axis=-1, keepdims=True) + EPS)
        ao_ref[...] = (ov * r * g_ref[...]).astype(BF16)

    blk = pl.BlockSpec((tm, ATTN_W), lambda i: (i, 0))
    ls = pl.BlockSpec((tm, LANES), lambda i: (i, 0))
    return pl.pallas_call(
        body, name="attn_combine", grid=(S // tm,),
        in_specs=[blk, blk, blk, ls, ls, ls, pl.BlockSpec((1, ATTN_W), lambda i: (0, 0))], out_specs=[blk, blk, ls],
        out_shape=[jax.ShapeDtypeStruct((S, D_MODEL), BF16), jax.ShapeDtypeStruct((S, ATTN_W), F32),
                   jax.ShapeDtypeStruct((S, LANES), F32)],
    )(*outs, *lses, g)


def _attn_norm_bwd(o, g, dao, tm=256):
    S = o.shape[0]

    def body(o_ref, g_ref, dao_ref, do_ref, dl_ref, dg_ref):
        i = pl.program_id(0)

        @pl.when(i == 0)
        def _():
            dg_ref[...] = jnp.zeros_like(dg_ref)

        ov = o_ref[...]
        r = lax.rsqrt(jnp.mean(ov * ov, axis=-1, keepdims=True) + EPS)
        ohat = ov * r
        dn = dao_ref[...].astype(F32)
        dg_ref[...] += jnp.sum(dn * ohat, axis=0, keepdims=True)
        t = dn * g_ref[...]
        do = r * (t - ohat * jnp.mean(t * ohat, axis=-1, keepdims=True))
        do_ref[...] = do.astype(BF16)
        prod = do * ov
        lane = lax.broadcasted_iota(jnp.int32, (tm, LANES), 1)
        tile = jnp.zeros((tm, LANES), F32)
        for h in range(N_HEADS):
            tile = jnp.where(lane == h, jnp.sum(prod[:, h * HEAD:(h + 1) * HEAD], axis=1, keepdims=True), tile)
        dl_ref[...] = tile

    blk = pl.BlockSpec((tm, ATTN_W), lambda i: (i, 0))
    vec = pl.BlockSpec((1, ATTN_W), lambda i: (0, 0))
    return pl.pallas_call(
        body, name="attn_norm_bwd", grid=(S // tm,),
        in_specs=[blk, vec, pl.BlockSpec((tm, ATTN_W), lambda i: (i, 0))],
        out_specs=[blk, pl.BlockSpec((tm, LANES), lambda i: (i, 0)), vec],
        out_shape=[jax.ShapeDtypeStruct((S, ATTN_W), BF16), jax.ShapeDtypeStruct((S, LANES), F32),
                   jax.ShapeDtypeStruct((1, ATTN_W), F32)],
    )(o, g, dao)


def _attn_bwd_dq(qk, proj, do, lse, delta, d, prev):
    S = qk.shape[0]
    geo = _AttnGeo(S, d)
    scale = HEAD ** -0.5
    before = [] if prev is None else [prev]

    def body(q_ref, kp, kc, kn, vp, vc, vn, do_ref, lse_ref, dl_ref, *rest):
        prev_refs, (dq_ref, *scratch) = rest[:len(before)], rest[len(before):]
        h = pl.program_id(1)
        refs = iter(scratch)
        q_p, k_p, v_p, do_p, lse_p, dl_p, dq_p = (geo.bind(refs) for _ in range(7))
        geo.fill(q_p[0], q_ref)
        geo.fill(do_p[0], do_ref)
        geo.fill(lse_p[0], lse_ref)
        geo.fill(dl_p[0], dl_ref)
        geo.fill_window(k_p[0], kp, kc, kn)
        geo.fill_window(v_p[0], vp, vc, vn)
        qs, ks, vs, dos = geo.spread(q_p), geo.spread(k_p), geo.spread(v_p), geo.spread(do_p)
        lses, dls = geo.spread(lse_p), geo.spread(dl_p)
        dqs = dq_p[1]
        band = geo.band()
        for pair in range(0, geo.n_sub, 2):
            stage = []
            for sub in (pair, pair + 1):
                rq, rw = geo.rows(sub, SUB, geo.TB), geo.rows(sub, WIN, geo.W)
                q_r, k_r, v_r = qs[rq, :].astype(BF16), ks[rw, :].astype(BF16), vs[rw, :].astype(BF16)
                s = _dot(q_r, k_r, NT)
                dp = _dot(dos[rq, :].astype(BF16), v_r, NT)
                stage.append((sub, rq, k_r, s, dp))
            for sub, rq, k_r, s, dp in stage:
                lse_c, dl_c = _lane_of(lses[rq, :], h), _lane_of(dls[rq, :], h)
                p = jnp.where(geo.mask(sub, band), jnp.exp(s * scale - lse_c), 0.0)
                ds = (p * (dp - dl_c) * scale).astype(BF16)
                dqs[rq, :] = _dot(ds, k_r)
        total = geo.gather(dq_p)[...]
        for p_ref in prev_refs:
            total = total + p_ref[...].astype(F32)
        dq_ref[...] = total.astype(BF16)

    cur, _, _ = geo.specs(HEAD, 0)
    k_specs = geo.specs(HEAD, K_COL)
    v_specs = geo.specs(HEAD, V_COL)
    stat = pl.BlockSpec((geo.TB, LANES), lambda i, h: (i, 0))
    return pl.pallas_call(
        body, name=f"attn_bwd_dq_d{d}", grid=geo.grid,
        in_specs=[cur, k_specs[1], k_specs[0], k_specs[2], v_specs[1], v_specs[0], v_specs[2], cur, stat, stat]
        + [cur] * len(before),
        out_specs=cur, out_shape=jax.ShapeDtypeStruct((S, ATTN_W), BF16),
        scratch_shapes=(geo.scratch(geo.TB) + geo.scratch(geo.W) + geo.scratch(geo.W) + geo.scratch(geo.TB)
                        + geo.scratch(geo.TB, F32) + geo.scratch(geo.TB, F32) + geo.scratch(geo.TB, F32)),
    )(qk, qk, qk, qk, proj, proj, proj, do, lse, delta, *before)


def _attn_bwd_dkv(qk, proj, do, lse, delta, d, prev):
    S = qk.shape[0]
    geo = _AttnGeo(S, d)
    scale = HEAD ** -0.5
    before = [] if prev is None else list(prev)

    def body(k_ref, v_ref, qp, qc, qn, dop, doc, don, lp, lc, ln, dlp, dlc, dln, *rest):
        prev_refs, (dk_ref, dv_ref, *scratch) = rest[:len(before)], rest[len(before):]
        h = pl.program_id(1)
        refs = iter(scratch)
        k_p, v_p, q_p, do_p, lw_p, dlw_p, dk_p, dv_p = (geo.bind(refs) for _ in range(8))
        geo.fill(k_p[0], k_ref)
        geo.fill(v_p[0], v_ref)
        geo.fill_window(q_p[0], qp, qc, qn)
        geo.fill_window(do_p[0], dop, doc, don)
        geo.fill_window(lw_p[0], lp, lc, ln)
        geo.fill_window(dlw_p[0], dlp, dlc, dln)
        ks, vs, qs, dos = geo.spread(k_p), geo.spread(v_p), geo.spread(q_p), geo.spread(do_p)
        lws, dlws = geo.spread(lw_p), geo.spread(dlw_p)
        dks, dvs = dk_p[1], dv_p[1]
        head = lax.broadcasted_iota(jnp.int32, (LANES, WIN), 0)
        band = geo.band()
        for sub in range(geo.n_sub):
            rq, rw = geo.rows(sub, SUB, geo.TB), geo.rows(sub, WIN, geo.W)
            k_r, v_r = ks[rq, :].astype(BF16), vs[rq, :].astype(BF16)
            q_w, do_w = qs[rw, :].astype(BF16), dos[rw, :].astype(BF16)
            lse_row = jnp.sum(jnp.where(head == h, lws[rw, :].T, 0.0), axis=0, keepdims=True)
            dl_row = jnp.sum(jnp.where(head == h, dlws[rw, :].T, 0.0), axis=0, keepdims=True)
            st = _dot(k_r, q_w, NT) * scale
            pt = jnp.where(geo.mask(sub, band), jnp.exp(st - lse_row), 0.0)
            dvs[rq, :] = _dot(pt.astype(BF16), do_w)
            dpt = _dot(v_r, do_w, NT)
            dst = (pt * (dpt - dl_row) * scale).astype(BF16)
            dks[rq, :] = _dot(dst, q_w)
        dk_tot, dv_tot = geo.gather(dk_p)[...], geo.gather(dv_p)[...]
        if prev_refs:
            dk_tot, dv_tot = dk_tot + prev_refs[0][...].astype(F32), dv_tot + prev_refs[1][...].astype(F32)
        dk_ref[...] = dk_tot.astype(BF16)
        dv_ref[...] = dv_tot.astype(BF16)

    q_specs = geo.specs(HEAD, Q_COL)
    k_cur, _, _ = geo.specs(HEAD, K_COL)
    v_cur, _, _ = geo.specs(HEAD, V_COL)
    do_specs = geo.specs(HEAD, 0)
    st_specs = geo.specs(LANES, 0, per_head=False)
    cur = do_specs[0]
    return pl.pallas_call(
        body, name=f"attn_bwd_dkv_d{d}", grid=geo.grid,
        in_specs=[k_cur, v_cur, q_specs[1], q_specs[0], q_specs[2], do_specs[1], do_specs[0], do_specs[2],
                  st_specs[1], st_specs[0], st_specs[2], st_specs[1], st_specs[0], st_specs[2]] + [cur] * len(before),
        out_specs=[cur, cur],
        out_shape=[jax.ShapeDtypeStruct((S, ATTN_W), BF16), jax.ShapeDtypeStruct((S, ATTN_W), BF16)],
        scratch_shapes=(geo.scratch(geo.TB) + geo.scratch(geo.TB) + geo.scratch(geo.W) + geo.scratch(geo.W)
                        + geo.scratch(geo.W, F32) + geo.scratch(geo.W, F32) + geo.scratch(geo.TB, F32)
                        + geo.scratch(geo.TB, F32)),
    )(qk, proj, qk, qk, qk, do, do, do, lse, lse, lse, delta, delta, delta, *before)


def _cumsum_rows(x, reverse):
    n = x.shape[0]
    row = lax.broadcasted_iota(jnp.int32, x.shape, 0)
    s = 1
    while s < n:
        if reverse:
            x = x + jnp.where(row < n - s, pltpu.roll(x, n - s, axis=0), 0.0)
        else:
            x = x + jnp.where(row >= s, pltpu.roll(x, s, axis=0), 0.0)
        s *= 2
    return x


GLA_GROUP = 8


def _gla_rows(cc):
    return slice(cc * CHUNK, (cc + 1) * CHUNK)


def _gla_chunk_terms(q_ref, k_ref, v_ref, g_ref, h, reverse, rows, b_ref=None):
    ksl = slice(h * GLA_DK, (h + 1) * GLA_DK)
    q = q_ref[rows, ksl].astype(F32) * (GLA_DK ** -0.5)
    k = k_ref[rows, ksl].astype(F32)
    v = v_ref[rows, h * GLA_DV:(h + 1) * GLA_DV]
    b = _cumsum_rows(g_ref[rows, ksl], reverse) if b_ref is None else b_ref[rows, ksl]
    r_ref = CHUNK // 2 if reverse else CHUNK // 2 - 1
    r_last = 0 if reverse else CHUNK - 1
    b_ref, b_last = b[r_ref:r_ref + 1, :], b[r_last:r_last + 1, :]
    ii = lax.broadcasted_iota(jnp.int32, (CHUNK, CHUNK), 0)
    jj = lax.broadcasted_iota(jnp.int32, (CHUNK, CHUNK), 1)
    causal = (jj >= ii) if reverse else (jj <= ii)
    e_q, e_k = jnp.exp(b - b_ref), jnp.exp(b_ref - b)
    e_in, e_st = jnp.exp(b), jnp.exp(b_last - b)
    return dict(q=q, k=k, v=v, b=b, causal=causal, e_q=e_q, e_k=e_k, e_in=e_in, e_st=e_st, dec=jnp.exp(b_last),
                qe=q * e_q, ke=k * e_k, q_in=q * e_in, k_st=k * e_st, r_ref=r_ref, r_last=r_last)


def _gla_specs(order):
    rows = GLA_GROUP * CHUNK
    q = pl.BlockSpec((rows, GLA_K), lambda c: (order(c), 3 * ATTN_W // GLA_K))
    k = pl.BlockSpec((rows, GLA_K), lambda c: (order(c), 3 * ATTN_W // GLA_K + 1))
    v = pl.BlockSpec((rows, GLA_V), lambda c: (order(c), (3 * ATTN_W + 2 * GLA_K) // GLA_V))
    return q, k, v


def _gla_fwd(proj, gates, reverse, o_prev=None):
    S = proj.shape[0]
    n = S // CHUNK
    nb = n // GLA_GROUP
    rows = GLA_GROUP * CHUNK
    order = (lambda c: nb - 1 - c) if reverse else (lambda c: c)
    seq = list(range(GLA_GROUP))[::-1] if reverse else list(range(GLA_GROUP))
    gcol = 1 if reverse else 0

    def body(*refs):
        if o_prev is None:
            q_ref, k_ref, v_ref, g_ref, o_ref, st_ref, a_ref, b_ref, state = refs
        else:
            q_ref, k_ref, v_ref, g_ref, op_ref, o_ref, st_ref, a_ref, b_ref, state = refs
        c = pl.program_id(0)

        @pl.when(c == 0)
        def _():
            state[...] = jnp.zeros_like(state)

        for h in range(GLA_HEADS):
            vsl = slice(h * GLA_DV, (h + 1) * GLA_DV)
            st = state[h]
            for cc in seq:
                rs = _gla_rows(cc)
                t = _gla_chunk_terms(q_ref, k_ref, v_ref, g_ref, h, reverse, rs)
                b_ref[rs, h * GLA_DK:(h + 1) * GLA_DK] = t["b"]
                a = jnp.where(t["causal"], _dot(t["qe"].astype(BF16), t["ke"].astype(BF16), NT), 0.0).astype(BF16)
                a_ref[cc, h] = a
                o = _dot(a, t["v"])
                st_b = st.astype(BF16)
                st_ref[cc, h] = st_b
                o = o + _dot(t["q_in"].astype(BF16), st_b, NT)
                st = st * t["dec"] + _dot(t["v"], t["k_st"].astype(BF16), TN)
                if o_prev is not None:
                    o = o + op_ref[rs, vsl]
                o_ref[rs, vsl] = o
            state[h] = st

    q_spec, k_spec, v_spec = _gla_specs(order)
    o_spec = pl.BlockSpec((rows, GLA_V), lambda c: (order(c), 0))
    in_specs = [q_spec, k_spec, v_spec, pl.BlockSpec((rows, GLA_K), lambda c: (order(c), gcol))]
    operands = [proj, proj, proj, gates]
    if o_prev is not None:
        in_specs.append(o_spec)
        operands.append(o_prev)
    return pl.pallas_call(
        body, name="gla_fwd_rev" if reverse else "gla_fwd", grid=(nb,), in_specs=in_specs,
        out_specs=[o_spec, pl.BlockSpec((GLA_GROUP, GLA_HEADS, GLA_DV, GLA_DK), lambda c: (order(c), 0, 0, 0)),
                   pl.BlockSpec((GLA_GROUP, GLA_HEADS, CHUNK, CHUNK), lambda c: (order(c), 0, 0, 0)),
                   pl.BlockSpec((rows, GLA_K), lambda c: (order(c), 0))],
        out_shape=[jax.ShapeDtypeStruct((S, GLA_V), F32), jax.ShapeDtypeStruct((n, GLA_HEADS, GLA_DV, GLA_DK), BF16),
                   jax.ShapeDtypeStruct((n, GLA_HEADS, CHUNK, CHUNK), BF16), jax.ShapeDtypeStruct((S, GLA_K), F32)],
        scratch_shapes=[pltpu.VMEM((GLA_HEADS, GLA_DV, GLA_DK), F32)],
    )(*operands)


def _gla_bwd(proj, kept, do, reverse, prev=None):
    S = proj.shape[0]
    n = S // CHUNK
    nb = n // GLA_GROUP
    rows = GLA_GROUP * CHUNK
    order = (lambda c: c) if reverse else (lambda c: nb - 1 - c)
    seq = list(range(GLA_GROUP)) if reverse else list(range(GLA_GROUP))[::-1]
    out_dt = F32 if prev is None else BF16

    def body(*refs):
        if prev is None:
            q_ref, k_ref, v_ref, b_ref, st_ref, a_ref, do_ref, dq_ref, dk_ref, dv_ref, dg_ref, dstate = refs
        else:
            q_ref, k_ref, v_ref, b_ref, st_ref, a_ref, do_ref, pq, pk, pv, dq_ref, dk_ref, dv_ref, dg_ref, dstate = refs
        c = pl.program_id(0)

        @pl.when(c == 0)
        def _():
            dstate[...] = jnp.zeros_like(dstate)

        row = lax.broadcasted_iota(jnp.int32, (CHUNK, GLA_DK), 0)
        for h in range(GLA_HEADS):
            ksl = slice(h * GLA_DK, (h + 1) * GLA_DK)
            vsl = slice(h * GLA_DV, (h + 1) * GLA_DV)
            dst = dstate[h]
            for cc in seq:
                rs = _gla_rows(cc)
                t = _gla_chunk_terms(q_ref, k_ref, v_ref, None, h, reverse, rs, b_ref)
                v = t["v"]
                dob = do_ref[rs, vsl].astype(BF16)
                st_b = st_ref[cc, h]
                dst_b = dst.astype(BF16)
                qe_b, ke_b = t["qe"].astype(BF16), t["ke"].astype(BF16)
                q_in_b, k_st_b = t["q_in"].astype(BF16), t["k_st"].astype(BF16)
                da = jnp.where(t["causal"], _dot(dob, v, NT), 0.0).astype(BF16)
                dv = _dot(a_ref[cc, h], dob, TN) + _dot(k_st_b, dst_b, NT)
                dqe = _dot(da, ke_b)
                dke = _dot(da, qe_b, TN)
                dq_in = _dot(dob, st_b)
                dk_st = _dot(v, dst_b)
                ddec = jnp.sum(dst * st_b.astype(F32), axis=0, keepdims=True)
                dst = _dot(dob, q_in_b, TN) + dst * t["dec"]
                dq = (dqe * t["e_q"] + dq_in * t["e_in"]) * (GLA_DK ** -0.5)
                dk = dke * t["e_k"] + dk_st * t["e_st"]
                w_q, w_k = dqe * t["qe"], dke * t["ke"]
                w_st = dk_st * t["k_st"]
                db = w_q - w_k + dq_in * t["q_in"] - w_st
                db_ref = jnp.sum(w_k - w_q, axis=0, keepdims=True)
                db_last = jnp.sum(w_st, axis=0, keepdims=True) + ddec * t["dec"]
                db = db + jnp.where(row == t["r_ref"], db_ref, 0.0) + jnp.where(row == t["r_last"], db_last, 0.0)
                dg_ref[rs, ksl] = _cumsum_rows(db, not reverse)
                if prev is not None:
                    dq, dk, dv = dq + pq[rs, ksl], dk + pk[rs, ksl], dv + pv[rs, vsl]
                dq_ref[rs, ksl] = dq.astype(out_dt)
                dk_ref[rs, ksl] = dk.astype(out_dt)
                dv_ref[rs, vsl] = dv.astype(out_dt)
            dstate[h] = dst

    q_spec, k_spec, v_spec = _gla_specs(order)
    kk = pl.BlockSpec((rows, GLA_K), lambda c: (order(c), 0))
    vv = pl.BlockSpec((rows, GLA_V), lambda c: (order(c), 0))
    states, scores, sums = kept
    in_specs = [q_spec, k_spec, v_spec, kk,
                pl.BlockSpec((GLA_GROUP, GLA_HEADS, GLA_DV, GLA_DK), lambda c: (order(c), 0, 0, 0)),
                pl.BlockSpec((GLA_GROUP, GLA_HEADS, CHUNK, CHUNK), lambda c: (order(c), 0, 0, 0)), vv]
    operands = [proj, proj, proj, sums, states, scores, do]
    if prev is not None:
        in_specs += [kk, kk, vv]
        operands += list(prev)
    return pl.pallas_call(
        body, name="gla_bwd_rev" if reverse else "gla_bwd", grid=(nb,), in_specs=in_specs, out_specs=[kk, kk, vv, kk],
        out_shape=[jax.ShapeDtypeStruct((S, GLA_K), out_dt), jax.ShapeDtypeStruct((S, GLA_K), out_dt),
                   jax.ShapeDtypeStruct((S, GLA_V), out_dt), jax.ShapeDtypeStruct((S, GLA_K), F32)],
        scratch_shapes=[pltpu.VMEM((GLA_HEADS, GLA_DV, GLA_DK), F32)],
    )(*operands)


def _gates_fwd(z, wg, bias, tm=512):
    S = z.shape[0]
    W = 2 * GLA_K

    def body(z_ref, w_ref, b_ref, o_ref):
        zg = _dot(z_ref[...], w_ref[...]) + b_ref[...]
        o_ref[...] = (jnp.minimum(zg, 0.0) - jnp.log(1.0 + jnp.exp(-jnp.abs(zg)))) * (1.0 / GATE_NORM)

    return pl.pallas_call(
        body, name="gates_fwd", grid=(S // tm,),
        in_specs=[pl.BlockSpec((tm, Z_W), lambda i: (i, 0)), pl.BlockSpec((Z_W, W), lambda i: (0, 0)),
                  pl.BlockSpec((1, W), lambda i: (0, 0))],
        out_specs=pl.BlockSpec((tm, W), lambda i: (i, 0)), out_shape=jax.ShapeDtypeStruct((S, W), F32),
    )(z, wg, bias)


def _gates_bwd(z, wg, bias, dg_f, dg_b, tm=512):
    S = z.shape[0]
    W = 2 * GLA_K

    def body(z_ref, w_ref, b_ref, dgf_ref, dgb_ref, dz_ref, dw_ref, db_ref):
        i = pl.program_id(0)

        @pl.when(i == 0)
        def _():
            dw_ref[...] = jnp.zeros_like(dw_ref)
            db_ref[...] = jnp.zeros_like(db_ref)

        zv = z_ref[...]
        zg = _dot(zv, w_ref[...]) + b_ref[...]
        dg = jnp.concatenate([dgf_ref[...], dgb_ref[...]], axis=1)
        dzg = dg * (1.0 / GATE_NORM) * _sigmoid(-zg)
        db_ref[...] += jnp.sum(dzg, axis=0, keepdims=True)
        dzg_b = dzg.astype(BF16)
        dw_ref[...] += _dot(zv, dzg_b, TN)
        dz_ref[...] = _dot(dzg_b, w_ref[...], NT).astype(BF16)

    half = pl.BlockSpec((tm, GLA_K), lambda i: (i, 0))
    return pl.pallas_call(
        body, name="gates_bwd", grid=(S // tm,),
        in_specs=[pl.BlockSpec((tm, Z_W), lambda i: (i, 0)), pl.BlockSpec((Z_W, W), lambda i: (0, 0)),
                  pl.BlockSpec((1, W), lambda i: (0, 0)), half, half],
        out_specs=[pl.BlockSpec((tm, Z_W), lambda i: (i, 0)), pl.BlockSpec((Z_W, W), lambda i: (0, 0)),
                   pl.BlockSpec((1, W), lambda i: (0, 0))],
        out_shape=[jax.ShapeDtypeStruct((S, Z_W), BF16), jax.ShapeDtypeStruct((Z_W, W), F32),
                   jax.ShapeDtypeStruct((1, W), F32)],
    )(z, wg, bias, dg_f, dg_b)


def _gla_out_fwd(o, proj, g, cat, tm=512):
    S = o.shape[0]

    def body(o_ref, gr_ref, g_ref, cat_ref, out_ref):
        gn = g_ref[...]
        for h in range(GLA_HEADS):
            sl = slice(h * GLA_DV, (h + 1) * GLA_DV)
            ov = o_ref[:, sl]
            r = lax.rsqrt(jnp.mean(ov * ov, axis=-1, keepdims=True) + EPS)
            gr = gr_ref[:, sl].astype(F32)
            out_ref[:, sl] = (ov * r * gn * _silu(gr)).astype(BF16)

    blk = pl.BlockSpec((tm, GLA_V), lambda i: (i, 0))
    return pl.pallas_call(
        body, name="gla_out_fwd", grid=(S // tm,),
        in_specs=[blk, pl.BlockSpec((tm, GLA_V), lambda i: (i, (3 * ATTN_W + 2 * GLA_K + GLA_V) // GLA_V)),
                  pl.BlockSpec((1, GLA_DV), lambda i: (0, 0)), ANY],
        out_specs=pl.BlockSpec((tm, GLA_V), lambda i: (i, 1)), out_shape=jax.ShapeDtypeStruct((S, D_MODEL), BF16),
        input_output_aliases={3: 0},
    )(o, proj, g, cat)


def _gla_out_bwd(o, proj, g, dcat, dproj, tm=512):
    S = o.shape[0]

    def body(o_ref, gr_ref, g_ref, dgo_ref, dproj_ref, do_ref, dgr_ref, dg_ref):
        i = pl.program_id(0)

        @pl.when(i == 0)
        def _():
            dg_ref[...] = jnp.zeros_like(dg_ref)

        gn = g_ref[...]
        dg_acc = jnp.zeros((1, GLA_DV), F32)
        for h in range(GLA_HEADS):
            sl = slice(h * GLA_DV, (h + 1) * GLA_DV)
            ov = o_ref[:, sl]
            r = lax.rsqrt(jnp.mean(ov * ov, axis=-1, keepdims=True) + EPS)
            yhat = ov * r
            gr = gr_ref[:, sl].astype(F32)
            sg = _sigmoid(gr)
            dgo = dgo_ref[:, sl].astype(F32)
            dgr_ref[:, sl] = (dgo * (yhat * gn) * (sg * (1.0 + gr * (1.0 - sg)))).astype(BF16)
            dy = dgo * (gr * sg)
            dg_acc = dg_acc + jnp.sum(dy * yhat, axis=0, keepdims=True)
            t = dy * gn
            do_ref[:, sl] = r * (t - yhat * jnp.mean(t * yhat, axis=-1, keepdims=True))
        dg_ref[...] += dg_acc

    blk = pl.BlockSpec((tm, GLA_V), lambda i: (i, 0))
    vec = pl.BlockSpec((1, GLA_DV), lambda i: (0, 0))
    return pl.pallas_call(
        body, name="gla_out_bwd", grid=(S // tm,),
        in_specs=[blk, pl.BlockSpec((tm, GLA_V), lambda i: (i, (3 * ATTN_W + 2 * GLA_K + GLA_V) // GLA_V)), vec,
                  pl.BlockSpec((tm, GLA_V), lambda i: (i, 1)), ANY],
        out_specs=[blk, pl.BlockSpec((tm, GLA_V), lambda i: (i, (3 * ATTN_W + 2 * GLA_K + GLA_V) // GLA_V)), vec],
        out_shape=[jax.ShapeDtypeStruct((S, GLA_V), F32), jax.ShapeDtypeStruct((S, IN_MAIN), BF16),
                   jax.ShapeDtypeStruct((1, GLA_DV), F32)],
        input_output_aliases={4: 1},
    )(o, proj, g, dcat, dproj)


HALO = 16


def _halo_specs(tm, tn, S):
    cur = pl.BlockSpec((tm, tn), lambda j, i: (i, j))
    prev = pl.BlockSpec((HALO, tn), lambda j, i: (jnp.maximum(i * (tm // HALO) - 1, 0), j))
    nxt = pl.BlockSpec((HALO, tn), lambda j, i: (jnp.minimum((i + 1) * (tm // HALO), S // HALO - 1), j))
    return cur, prev, nxt


def _shifted(c_ref, p_ref, n_ref, n_blocks, i=None):
    if i is None:
        i = pl.program_id(1)
    x = c_ref[...].astype(F32)
    tm = x.shape[0]
    row = lax.broadcasted_iota(jnp.int32, x.shape, 0)
    before = p_ref[HALO - 1:HALO, :].astype(F32) * (i > 0).astype(F32)
    after = n_ref[0:1, :].astype(F32) * (i < n_blocks - 1).astype(F32)
    x_m1 = jnp.where(row == 0, before, pltpu.roll(x, 1, axis=0))
    x_p1 = jnp.where(row == tm - 1, after, pltpu.roll(x, tm - 1, axis=0))
    return x, x_m1, x_p1


def _glu_fwd(gp, up, cw, cb, tm=512, tn=1408):
    S = gp.shape[0]
    nb = S // tm

    def body(c_ref, p_ref, n_ref, up_ref, w_ref, b_ref, o_ref, gate_ref):
        x, x_m1, x_p1 = _shifted(c_ref, p_ref, n_ref, nb)
        w = w_ref[...]
        gate = w[0:1, :] * x_m1 + w[1:2, :] * x + w[2:3, :] * x_p1 + b_ref[...]
        gate_ref[...] = gate.astype(BF16)
        o_ref[...] = (_silu(gate) * up_ref[...].astype(F32)).astype(BF16)

    cur, prev, nxt = _halo_specs(tm, tn, S)
    out = jax.ShapeDtypeStruct((S, D_FF), BF16)
    return pl.pallas_call(
        body, name="glu_fwd", grid=(D_FF // tn, nb),
        in_specs=[cur, prev, nxt, cur, pl.BlockSpec((3, tn), lambda j, i: (0, j)), pl.BlockSpec((1, tn), lambda j, i: (0, j))],
        out_specs=[cur, cur], out_shape=[out, out],
    )(gp, gp, gp, up, cw, cb)


def _glu_bwd(gate, gp, up, dact, cw, tm=512, tn=1408):
    S = gp.shape[0]
    nb = S // tm

    def body(g_ref, gb_ref, ga_ref, x_ref, up_ref, upp_ref, upn_ref, da_ref, dap_ref, dan_ref, w_ref,
             dup_ref, dgp_ref, dw_ref, db_ref):
        i = pl.program_id(1)

        @pl.when(i == 0)
        def _():
            dw_ref[...] = jnp.zeros_like(dw_ref)
            db_ref[...] = jnp.zeros_like(db_ref)

        w = w_ref[...]
        w0, w1, w2 = w[0:1, :], w[1:2, :], w[2:3, :]

        def d_gate(g, da, upv):
            sg = _sigmoid(g)
            return sg, da * upv * (sg * (1.0 + g * (1.0 - sg)))

        g = g_ref[...].astype(F32)
        da = da_ref[...].astype(F32)
        sg, dgate = d_gate(g, da, up_ref[...].astype(F32))
        dup_ref[...] = (da * (g * sg)).astype(BF16)

        last = slice(HALO - 1, HALO)
        _, dgate_before = d_gate(gb_ref[...].astype(F32)[last, :], dap_ref[...].astype(F32)[last, :], upp_ref[...].astype(F32)[last, :])
        _, dgate_after = d_gate(ga_ref[...].astype(F32)[0:1, :], dan_ref[...].astype(F32)[0:1, :], upn_ref[...].astype(F32)[0:1, :])
        dgate_before = dgate_before * (i > 0).astype(F32)
        dgate_after = dgate_after * (i < nb - 1).astype(F32)
        row = lax.broadcasted_iota(jnp.int32, dgate.shape, 0)
        dg_m1 = jnp.where(row == 0, dgate_before, pltpu.roll(dgate, 1, axis=0))
        dg_p1 = jnp.where(row == tm - 1, dgate_after, pltpu.roll(dgate, tm - 1, axis=0))
        dgp_ref[...] = (w0 * dg_p1 + w1 * dgate + w2 * dg_m1).astype(BF16)
        x = x_ref[...].astype(F32)
        db_ref[...] += jnp.sum(dgate, axis=0, keepdims=True)
        dw_ref[...] += jnp.concatenate(
            [jnp.sum(dg_p1 * x, axis=0, keepdims=True), jnp.sum(dgate * x, axis=0, keepdims=True),
             jnp.sum(dg_m1 * x, axis=0, keepdims=True)], axis=0)

    cur, prev, nxt = _halo_specs(tm, tn, S)
    w_spec = pl.BlockSpec((3, tn), lambda j, i: (0, j))
    b_spec = pl.BlockSpec((1, tn), lambda j, i: (0, j))
    return pl.pallas_call(
        body, name="glu_bwd", grid=(D_FF // tn, nb),
        in_specs=[cur, prev, nxt, cur, cur, prev, nxt, cur, prev, nxt, w_spec],
        out_specs=[cur, cur, w_spec, b_spec],
        out_shape=[jax.ShapeDtypeStruct((S, D_FF), BF16), jax.ShapeDtypeStruct((S, D_FF), BF16),
                   jax.ShapeDtypeStruct((3, D_FF), F32), jax.ShapeDtypeStruct((1, D_FF), F32)],
    )(gate, gate, gate, gp, up, up, up, dact, dact, dact, cw)


def _local_step(x, target, norm1_g, w_in_t, wg, gate_bias, gla_norm_g, attn_norm_g, w_out, norm2_g,
                w_gate4, w_up4, conv_w, conv_b, w_down, final_norm_g, on_grad=lambda event, arrays: ()):
    S = x.shape[0]
    tabs = _rope_tables(S)

    n1 = _rms_fwd("rms1_fwd", x, norm1_g)
    z_block = IN_MAIN // Z_W
    proj = _mm_nt("in_proj", n1, w_in_t, 1024, 1536, BF16, n_out=IN_MAIN)
    z = _matmul(
        "in_proj_z",
        [(n1, pl.BlockSpec((1024, D_MODEL), lambda i: (i, 0)), w_in_t, pl.BlockSpec((Z_W, D_MODEL), lambda i: (z_block, 0)), NT)],
        (S // 1024,), jax.ShapeDtypeStruct((S, Z_W), BF16), pl.BlockSpec((1024, Z_W), lambda i: (i, 0)), 1)
    qk = _rope_fwd(proj, tabs)
    branch = [_attn_fwd(qk, proj, d) for d in DILATIONS]
    ao, o_attn, lse = _attn_combine([b[0] for b in branch], [b[1] for b in branch], attn_norm_g)
    gates = _gates_fwd(z, wg, gate_bias)
    o_f, *kept_f = _gla_fwd(proj, gates, False)
    o_gla, *kept_b = _gla_fwd(proj, gates, True, o_prev=o_f)
    cat = _gla_out_fwd(o_gla, proj, gla_norm_g, ao)
    h1 = _mm_nn("out_proj", cat, w_out, 1024, 1024, F32, res=x)
    n2 = _rms_fwd("rms2_fwd", h1, norm2_g)
    gp = _mm_nn_sharded("ffn_gate", n2, w_gate4, 1024, BF16)
    up = _mm_nn_sharded("ffn_up", n2, w_up4, 1024, BF16)
    act, gate = _glu_fwd(gp, up, conv_w, conv_b)
    tk = D_FF // N_CHIPS
    h2 = _mm_nn("ffn_down", act, w_down, 1024, 512, F32, res=h1)
    loss_row, d_final_g, dh2, dh2_b = _final_loss(h2, final_norm_g.reshape(1, D_MODEL), target)

    dact = _mm_nt("ffn_down_bwd", dh2_b, w_down, 1024, tk, BF16)
    dup, dgp, d_conv_w, d_conv_b = _glu_bwd(gate, gp, up, dact, conv_w)
    d_w_down = _mm_tn("ffn_down_wgrad", act, dh2_b, 512, D_MODEL, 2048, BF16)
    on_grad("w_down", dict(w_down=d_w_down))
    dgp = _after(dgp, d_w_down)
    d_w_gate4 = _mm_tn("ffn_gate_wgrad", n2, dgp, 1024, tk, 2048, BF16, out3=tk)
    dup = _after(dup, d_w_gate4)
    d_w_up4 = _mm_tn("ffn_up_wgrad", n2, dup, 1024, tk, 2048, BF16, out3=tk)
    held = on_grad("w_gate_w_up", dict(w_gate=d_w_gate4, w_up=d_w_up4))
    dgp = _after(dgp, d_w_up4, *held)
    shard_pairs = [
        (g, pl.BlockSpec((512, tk), functools.partial(lambda s, j, i: (i, s), s)),
         w4, pl.BlockSpec((None, 512, tk), functools.partial(lambda s, j, i: (s, j, 0), s)), NT)
        for g, w4 in ((dgp, w_gate4), (dup, w_up4)) for s in range(N_CHIPS)]
    dn2 = _matmul("ffn_in_bwd", shard_pairs, (D_MODEL // 512, S // 512), jax.ShapeDtypeStruct((S, D_MODEL), BF16),
                  pl.BlockSpec((512, 512), lambda j, i: (i, j)), 1)
    dh1, dh1_b, d_norm2_g = _rms_bwd("rms2_bwd", h1, norm2_g, dn2, dh2)

    d_w_out = _mm_tn("out_proj_wgrad", cat, dh1_b, D_MODEL, 1024, 1024, BF16)
    held = on_grad("w_out", dict(w_out=d_w_out))
    dcat = _mm_nt("out_proj_bwd", _after(dh1_b, d_w_out, *held), w_out, 1024, 1024, BF16)
    do_attn, delta, d_attn_norm_g = _attn_norm_bwd(o_attn, attn_norm_g, dcat)
    dq_a, dkv_a = None, None
    for d in DILATIONS:
        dq_a = _attn_bwd_dq(qk, proj, do_attn, lse, delta, d, dq_a)
        dkv_a = _attn_bwd_dkv(qk, proj, do_attn, lse, delta, d, dkv_a)
    dproj = _attn_grad_merge(dq_a, dkv_a[0], dkv_a[1], tabs)
    held = on_grad("mid", dict(anchor=dproj))
    do_gla, dproj, d_gla_norm_g = _gla_out_bwd(o_gla, proj, gla_norm_g, _after(dcat, *held), dproj)
    dq_f, dk_f, dv_f, dg_f = _gla_bwd(proj, kept_f, do_gla, False)
    dgq, dgk, dgv, dg_b = _gla_bwd(proj, kept_b, do_gla, True, prev=(dq_f, dk_f, dv_f))
    dz, d_wg, d_gate_bias = _gates_bwd(z, wg, gate_bias, dg_f, dg_b)
    dproj = lax.dynamic_update_slice(dproj, jnp.concatenate([dgq, dgk, dgv], axis=1), (0, 3 * ATTN_W))
    d_w_in_t = _mm_tn("in_proj_wgrad", dproj, n1, 768, D_MODEL, 2048, BF16, rows_out=IN_W)
    n_tok = S // 1024
    d_w_in_t = _matmul(
        "in_proj_z_wgrad",
        [(dz, pl.BlockSpec((1024, Z_W), lambda i, j, k: (k, 0)), n1, pl.BlockSpec((1024, D_MODEL), lambda i, j, k: (k, 0)), TN)],
        (1, 1, n_tok), jax.ShapeDtypeStruct((IN_W, D_MODEL), BF16), pl.BlockSpec((Z_W, D_MODEL), lambda i, j, k: (z_block, 0)),
        n_tok, into=d_w_in_t)
    held = on_grad("w_in", dict(w_in_t=d_w_in_t))
    n_rows = S // 1024
    head_rows = max(1, n_rows // 8)

    def in_proj_bwd(name, first, count, a, into):
        return _matmul(
            name,
            [(a, pl.BlockSpec((1024, IN_MAIN), lambda j, i: (i + first, 0)), w_in_t, pl.BlockSpec((IN_MAIN, 512), lambda j, i: (0, j)), NN),
             (dz, pl.BlockSpec((1024, Z_W), lambda j, i: (i + first, 0)), w_in_t, pl.BlockSpec((Z_W, 512), lambda j, i: (z_block, j)), NN)],
            (D_MODEL // 512, count), jax.ShapeDtypeStruct((S, D_MODEL), BF16),
            pl.BlockSpec((1024, 512), lambda j, i: (i + first, j)), 1, into=into)

    dproj = _after(dproj, d_w_in_t, *held)
    dn1 = in_proj_bwd("in_proj_bwd_a", 0, head_rows, dproj, None)
    held = on_grad("last", dict(last=dn1))
    dn1 = in_proj_bwd("in_proj_bwd_b", head_rows, n_rows - head_rows, dproj, _after(dn1, *held))
    grad_x, _, d_norm1_g = _rms_bwd("rms1_bwd", x, norm1_g, dn1, dh1)

    big = dict(w_in_t=d_w_in_t, w_out=d_w_out, w_gate4=d_w_gate4, w_up4=d_w_up4, w_down=d_w_down)
    small = dict(loss=loss_row, norm1_g=d_norm1_g, wg=d_wg, gate_bias=d_gate_bias, gla_norm_g=d_gla_norm_g,
                 attn_norm_g=d_attn_norm_g, norm2_g=d_norm2_g, conv_w=d_conv_w, conv_b=d_conv_b, final_norm_g=d_final_g)
    return grad_x, big, small


def _position():
    return lax.axis_index("x"), lax.axis_index("y"), lax.axis_index("c")


def _other_chips(x, y):
    return [(1 - x, y), (x, 1 - y), (1 - x, 1 - y)]


def _gather_chips_async(name, shards, collective_id):
    n = len(shards)

    def body(*refs):
        ins, outs = refs[:n], refs[n:2 * n]
        send, recv, loc = refs[2 * n:]
        x, y, c = _position()
        me = 2 * x + y
        chips = _other_chips(x, y)
        barrier = pltpu.get_barrier_semaphore()
        for px, py in chips:
            pl.semaphore_signal(barrier, inc=1, device_id=(px, py, c), device_id_type=MESH)
        pl.semaphore_wait(barrier, len(chips))
        started = []
        for w in range(n):
            own = pltpu.make_async_copy(ins[w], outs[w].at[me], loc.at[w])
            own.start()
            started.append(own)
        sends = []
        for w in range(n):
            for j, (px, py) in enumerate(chips):
                cp = pltpu.make_async_remote_copy(ins[w], outs[w].at[me], send.at[3 * w + j], recv.at[3 * w + j],
                                                  device_id=(px, py, c), device_id_type=MESH)
                cp.start()
                sends.append(cp)
        for w in range(n):
            for j, (px, py) in enumerate(chips):
                pltpu.make_async_remote_copy(ins[w], outs[w].at[2 * px + py], send.at[3 * w + j], recv.at[3 * w + j],
                                             device_id=(px, py, c), device_id_type=MESH).wait_recv()
        for cp in sends:
            cp.wait_send()
        for own in started:
            own.wait()

    return pl.kernel(
        body, name=name, mesh=_sequencer(),
        out_type=[jax.ShapeDtypeStruct((N_CHIPS,) + s.shape, s.dtype) for s in shards],
        scratch_types=[pltpu.SemaphoreType.DMA((3 * n,)), pltpu.SemaphoreType.DMA((3 * n,)), pltpu.SemaphoreType.DMA((n,))],
        compiler_params=pltpu.CompilerParams(collective_id=collective_id),
    )(*shards)


def _gather_halves_async(name, small, shard, collective_id):
    half = shard.shape[1] // 2

    def body(small_ref, shard_ref, small_out, out, send, recv, loc):
        x, y, c = _position()
        me = 2 * x + y
        sibling = (x, y, 1 - c)
        chips = _other_chips(x, y)
        barrier = pltpu.get_barrier_semaphore()
        for px, py in chips:
            pl.semaphore_signal(barrier, inc=1, device_id=(px, py, c), device_id_type=MESH)
        pl.semaphore_signal(barrier, inc=1, device_id=sibling, device_id_type=MESH)
        pl.semaphore_wait(barrier, len(chips) + 1)
        mine = pl.ds(pl.multiple_of(c * half, LANES), half)
        theirs = pl.ds(pl.multiple_of((1 - c) * half, LANES), half)
        own = [pltpu.make_async_copy(small_ref, small_out.at[me], loc.at[0]),
               pltpu.make_async_copy(shard_ref, out.at[me], loc.at[1])]
        for cp in own:
            cp.start()
        sends = []
        for j, (px, py) in enumerate(chips):
            sends.append(pltpu.make_async_remote_copy(small_ref, small_out.at[me], send.at[j], recv.at[j],
                                                      device_id=(px, py, c), device_id_type=MESH))
            sends.append(pltpu.make_async_remote_copy(shard_ref.at[:, mine], out.at[me, :, mine], send.at[3 + j], recv.at[3 + j],
                                                      device_id=(px, py, c), device_id_type=MESH))
        for cp in sends:
            cp.start()
        passed = []
        for j, (px, py) in enumerate(chips):
            slot = 2 * px + py
            pltpu.make_async_remote_copy(shard_ref.at[:, mine], out.at[slot, :, mine], send.at[3 + j], recv.at[3 + j],
                                         device_id=(px, py, c), device_id_type=MESH).wait_recv()
            cp = pltpu.make_async_remote_copy(out.at[slot, :, mine], out.at[slot, :, mine], send.at[6 + j], recv.at[6 + j],
                                              device_id=sibling, device_id_type=MESH)
            cp.start()
            passed.append(cp)
        for j, (px, py) in enumerate(chips):
            slot = 2 * px + py
            pltpu.make_async_remote_copy(small_ref, small_out.at[slot], send.at[j], recv.at[j],
                                         device_id=(px, py, c), device_id_type=MESH).wait_recv()
            pltpu.make_async_remote_copy(out.at[slot, :, theirs], out.at[slot, :, theirs], send.at[6 + j], recv.at[6 + j],
                                         device_id=sibling, device_id_type=MESH).wait_recv()
        for cp in sends + passed:
            cp.wait_send()
        for cp in own:
            cp.wait()

    return pl.kernel(
        body, name=name, mesh=_sequencer(),
        out_type=[jax.ShapeDtypeStruct((N_CHIPS,) + small.shape, small.dtype),
                  jax.ShapeDtypeStruct((N_CHIPS,) + shard.shape, shard.dtype)],
        scratch_types=[pltpu.SemaphoreType.DMA((9,)), pltpu.SemaphoreType.DMA((9,)), pltpu.SemaphoreType.DMA((2,))],
        compiler_params=pltpu.CompilerParams(collective_id=collective_id),
    )(small, shard)


def _sequencer():
    return plsc.ScalarSubcoreMesh(axis_name="sequencer", num_cores=1)


def _sibling_exchange_async(name, arrs, collective_id):
    n = len(arrs)

    def body(*refs):
        ins, outs = refs[:n], refs[n:2 * n]
        send, recv = refs[2 * n:]
        x, y, c = _position()
        sibling = (x, y, 1 - c)
        barrier = pltpu.get_barrier_semaphore()
        pl.semaphore_signal(barrier, inc=1, device_id=sibling, device_id_type=MESH)
        pl.semaphore_wait(barrier, 1)
        copies = [pltpu.make_async_remote_copy(ins[w], outs[w], send.at[w], recv.at[w], device_id=sibling,
                                               device_id_type=MESH) for w in range(n)]
        for cp in copies:
            cp.start()
        for cp in copies:
            cp.wait()

    return pl.kernel(
        body, name=name, out_type=[jax.ShapeDtypeStruct(a.shape, a.dtype) for a in arrs],
        scratch_types=[pltpu.SemaphoreType.DMA((n,)), pltpu.SemaphoreType.DMA((n,))],
        compiler_params=pltpu.CompilerParams(collective_id=collective_id), mesh=_sequencer(),
    )(*arrs)


def _scatter_chips_async(name, parts, collective_id):
    n = len(parts)

    def body(*refs):
        ins, outs = refs[:n], refs[n:2 * n]
        send, recv, loc = refs[2 * n:]
        x, y, c = _position()
        me = 2 * x + y
        chips = _other_chips(x, y)
        barrier = pltpu.get_barrier_semaphore()
        for px, py in chips:
            pl.semaphore_signal(barrier, inc=1, device_id=(px, py, c), device_id_type=MESH)
        pl.semaphore_wait(barrier, len(chips))
        started = []
        for w in range(n):
            own = pltpu.make_async_copy(ins[w].at[me], outs[w].at[me], loc.at[w])
            own.start()
            started.append(own)
        sends = []
        for w in range(n):
            for j, (px, py) in enumerate(chips):
                cp = pltpu.make_async_remote_copy(ins[w].at[2 * px + py], outs[w].at[me], send.at[3 * w + j],
                                                  recv.at[3 * w + j], device_id=(px, py, c), device_id_type=MESH)
                cp.start()
                sends.append(cp)
        for w in range(n):
            for j, (px, py) in enumerate(chips):
                pltpu.make_async_remote_copy(ins[w].at[me], outs[w].at[2 * px + py], send.at[3 * w + j], recv.at[3 * w + j],
                                             device_id=(px, py, c), device_id_type=MESH).wait_recv()
        for cp in sends:
            cp.wait_send()
        for own in started:
            own.wait()

    return pl.kernel(
        body, name=name, out_type=[jax.ShapeDtypeStruct(p.shape, p.dtype) for p in parts],
        scratch_types=[pltpu.SemaphoreType.DMA((3 * n,)), pltpu.SemaphoreType.DMA((3 * n,)), pltpu.SemaphoreType.DMA((n,))],
        compiler_params=pltpu.CompilerParams(collective_id=collective_id), mesh=_sequencer(),
    )(*parts)


def _allreduce_rows(buf):
    R = buf.shape[0]

    def body(in_ref, out_ref, land, send, recv):
        x, y, c = _position()
        me = 4 * x + 2 * y + c
        land[pl.ds(me, 1)] = in_ref[...][None]
        peers = []
        for mask in range(1, N_DEV):
            px = 1 - x if mask & 4 else x
            py = 1 - y if mask & 2 else y
            pc = 1 - c if mask & 1 else c
            peers.append((px, py, pc))
        sends = []
        for k, peer in enumerate(peers):
            cp = pltpu.make_async_remote_copy(in_ref, land.at[me], send.at[k], recv.at[k], device_id=peer, device_id_type=MESH)
            cp.start()
            sends.append(cp)
        for k, (px, py, pc) in enumerate(peers):
            pltpu.make_async_remote_copy(in_ref, land.at[4 * px + 2 * py + pc], send.at[k], recv.at[k],
                                         device_id=(px, py, pc), device_id_type=MESH).wait_recv()
        for cp in sends:
            cp.wait_send()
        tot = land[0]
        for i in range(1, N_DEV):
            tot = tot + land[i]
        out_ref[...] = tot

    vm = pl.BlockSpec(memory_space=pltpu.VMEM)
    return pl.pallas_call(
        body, name="allreduce_small", in_specs=[vm], out_specs=vm, out_shape=jax.ShapeDtypeStruct((R, LANES), F32),
        scratch_shapes=[pltpu.VMEM((N_DEV, R, LANES), F32), pltpu.SemaphoreType.DMA((N_DEV - 1,)),
                        pltpu.SemaphoreType.DMA((N_DEV - 1,))],
    )(buf)


def _tile2d(r, c, cap):
    if r <= cap:
        return r, c
    fits = [t for t in range(16, cap + 1, 16) if r % t == 0]
    return (max(fits), c) if fits else (r, 256)


def _pair_sum(name, a, b):
    n, r, c = a.shape
    tr, tc = _tile2d(r, c, 1024)

    def body(a_ref, b_ref, o_ref):
        o_ref[...] = (a_ref[...].astype(F32) + b_ref[...].astype(F32)).astype(BF16)

    blk = pl.BlockSpec((None, tr, tc), lambda s, i, j: (s, i, j))
    return pl.pallas_call(
        body, name=name, grid=(n, r // tr, c // tc), in_specs=[blk, blk], out_specs=blk,
        out_shape=jax.ShapeDtypeStruct(a.shape, BF16),
    )(a, b)


def _adamw_math(w, m, v, g):
    m2 = ADAM_B1 * m + (1.0 - ADAM_B1) * g
    v2 = ADAM_B2 * v + (1.0 - ADAM_B2) * (g * g)
    m_hat = m2 / (1.0 - ADAM_B1 ** ADAM_STEP)
    v_hat = v2 / (1.0 - ADAM_B2 ** ADAM_STEP)
    delta = -ADAM_LR * (m_hat / (jnp.sqrt(v_hat) + ADAM_EPS) + ADAM_WD * w)
    return delta, m2, v2


def _adamw(name, w, m, v, g):
    r, c = w.shape
    stacked = g.ndim == 3
    tr, tc = _tile2d(r, c, 256)

    def body(w_ref, m_ref, v_ref, g_ref, go_ref, d_ref, m2_ref, v2_ref):
        if stacked:
            gv = g_ref[0].astype(F32)
            for i in range(1, N_CHIPS):
                gv = gv + g_ref[i].astype(F32)
        else:
            gv = g_ref[...]
        delta, m2, v2 = _adamw_math(w_ref[...], m_ref[...], v_ref[...], gv)
        go_ref[...] = gv
        d_ref[...] = delta
        m2_ref[...] = m2
        v2_ref[...] = v2

    blk = pl.BlockSpec((tr, tc), lambda i, j: (i, j))
    g_spec = pl.BlockSpec((N_CHIPS, tr, tc), lambda i, j: (0, i, j)) if stacked else blk
    out = jax.ShapeDtypeStruct((r, c), F32)
    return pl.pallas_call(
        body, name=name, grid=(r // tr, c // tc), in_specs=[blk, blk, blk, g_spec], out_specs=[blk] * 4, out_shape=[out] * 4,
    )(w, m, v, g)


def _pack_rows(pieces):
    flat = jnp.concatenate([p.reshape(-1) for p in pieces])
    rows = flat.shape[0] // LANES
    pad = (-rows) % 8
    return jnp.pad(flat.reshape(rows, LANES), ((0, pad), (0, 0)))


def _unpack_rows(buf, shapes):
    flat = buf.reshape(-1)
    out, at = [], 0
    for s in shapes:
        size = math.prod(s)
        out.append(flat[at:at + size].reshape(s))
        at += size
    return out


SMALL_NAMES = ("norm1_g", "gf_up", "gf_b", "gb_up", "gb_b", "gla_norm_g", "attn_norm_g", "norm2_g", "conv_w", "conv_b",
               "final_norm_g")
BIG_NAMES = ("w_in", "w_out", "w_gate", "w_up", "w_down")
WEIGHT_ORDER = ("norm1_g", "w_in", "gf_up", "gf_b", "gb_up", "gb_b", "gla_norm_g", "attn_norm_g", "w_out", "norm2_g",
                "w_gate", "w_up", "conv_w", "conv_b", "w_down", "final_norm_g")


def kernel(x, norm1_g, w_in, gf_up, gf_b, gb_up, gb_b, gla_norm_g, attn_norm_g, w_out, norm2_g, w_gate, w_up, conv_w, conv_b, w_down, final_norm_g, loss_target, m_norm1_g, m_w_in, m_gf_up, m_gf_b, m_gb_up, m_gb_b, m_gla_norm_g, m_attn_norm_g, m_w_out, m_norm2_g, m_w_gate, m_w_up, m_conv_w, m_conv_b, m_w_down, m_final_norm_g, v_norm1_g, v_w_in, v_gf_up, v_gf_b, v_gb_up, v_gb_b, v_gla_norm_g, v_attn_norm_g, v_w_out, v_norm2_g, v_w_gate, v_w_up, v_conv_w, v_conv_b, v_w_down, v_final_norm_g):
    w = dict(norm1_g=norm1_g, w_in=w_in, gf_up=gf_up, gf_b=gf_b, gb_up=gb_up, gb_b=gb_b, gla_norm_g=gla_norm_g,
             attn_norm_g=attn_norm_g, w_out=w_out, norm2_g=norm2_g, w_gate=w_gate, w_up=w_up, conv_w=conv_w, conv_b=conv_b,
             w_down=w_down, final_norm_g=final_norm_g)
    m = dict(norm1_g=m_norm1_g, w_in=m_w_in, gf_up=m_gf_up, gf_b=m_gf_b, gb_up=m_gb_up, gb_b=m_gb_b, gla_norm_g=m_gla_norm_g,
             attn_norm_g=m_attn_norm_g, w_out=m_w_out, norm2_g=m_norm2_g, w_gate=m_w_gate, w_up=m_w_up, conv_w=m_conv_w,
             conv_b=m_conv_b, w_down=m_w_down, final_norm_g=m_final_norm_g)
    v = dict(norm1_g=v_norm1_g, w_in=v_w_in, gf_up=v_gf_up, gf_b=v_gf_b, gb_up=v_gb_up, gb_b=v_gb_b, gla_norm_g=v_gla_norm_g,
             attn_norm_g=v_attn_norm_g, w_out=v_w_out, norm2_g=v_norm2_g, w_gate=v_w_gate, w_up=v_w_up, conv_w=v_conv_w,
             conv_b=v_conv_b, w_down=v_w_down, final_norm_g=v_final_norm_g)
    S = x.shape[1]
    chip = 2 * lax.axis_index("x") + lax.axis_index("y")
    n_in = IN_W // N_CHIPS
    n_ff = D_FF // N_CHIPS
    n_gk = GLA_K // N_CHIPS

    def owned(t):
        return {k: (jnp.transpose(t[k][0]) if k == "w_in" else t[k][0]) for k in BIG_NAMES}

    own_w, own_m, own_v = owned(w), owned(m), owned(v)
    shard = {k: own_w[k].astype(BF16) for k in BIG_NAMES}
    small_shard = _pack_rows([gf_up[0], gb_up[0], conv_w[0]])
    small4, w_in4 = _gather_halves_async("gather_w_in", small_shard, shard["w_in"], 0)
    w_out4, w_gate4, w_up4 = _gather_chips_async("gather_w_mid", [shard["w_out"], shard["w_gate"], shard["w_up"]], 1)
    (w_down4,) = _gather_chips_async("gather_w_down", [shard["w_down"]], 2)
    w_in_t = w_in4.reshape(IN_W, D_MODEL)
    rows_up = GATE_RANK * n_gk // LANES
    rows_cw = 3 * n_ff // LANES
    gf_full = jnp.transpose(small4[:, 0:rows_up].reshape(N_CHIPS, GATE_RANK, n_gk), (1, 0, 2)).reshape(GATE_RANK, GLA_K)
    gb_full = jnp.transpose(small4[:, rows_up:2 * rows_up].reshape(N_CHIPS, GATE_RANK, n_gk), (1, 0, 2)).reshape(GATE_RANK, GLA_K)
    cw_full = jnp.transpose(small4[:, 2 * rows_up:2 * rows_up + rows_cw].reshape(N_CHIPS, 3, n_ff), (1, 0, 2)).reshape(3, D_FF)
    wg = jnp.zeros((Z_W, 2 * GLA_K), F32)
    wg = wg.at[0:GATE_RANK, 0:GLA_K].set(gf_full).at[GATE_RANK:2 * GATE_RANK, GLA_K:].set(gb_full).astype(BF16)
    gate_bias = jnp.concatenate([gf_b, gb_b], axis=1)

    pending, contributions, next_id = [], {}, [3]

    def as_shards(group, arrays):
        if group == "w_in":
            return dict(w_in=arrays["w_in_t"].reshape(N_CHIPS, n_in, D_MODEL))
        if group == "w_out":
            return dict(w_out=arrays["w_out"].reshape(N_CHIPS, D_MODEL // N_CHIPS, D_MODEL))
        if group == "w_down":
            return dict(w_down=arrays["w_down"].reshape(N_CHIPS, n_ff, D_MODEL))
        return arrays

    out = {}

    def swap(group, arrays):
        mine = as_shards(group, arrays)
        pending.append((group, mine, _sibling_exchange_async(f"sibling_{group}", list(mine.values()), next_id[0])))
        next_id[0] += 1

    def sum_and_send(anchor):
        tag, mine, theirs = pending.pop()
        sums = [_pair_sum(f"pair_sum_{k}", mine[k], _after(t, *anchor)) for k, t in zip(mine, theirs)]
        contributions.update(zip(mine, _scatter_chips_async(f"scatter_{tag}", sums, next_id[0])))
        next_id[0] += 1
        return sums

    def update(names, anchor):
        for k in names:
            res = _adamw(f"adamw_{k}", own_w[k], own_m[k], own_v[k], _after(contributions[k], *anchor))
            out[k] = [(jnp.transpose(r) if k == "w_in" else r)[None] for r in res]
        return [out[k][0] for k in names]

    def on_grad(event, arrays):
        anchor = list(arrays.values())
        held = []
        if event in ("w_gate_w_up", "w_out", "mid", "last"):
            held += sum_and_send(anchor)
        if event == "mid":
            held += update(("w_down", "w_gate", "w_up"), anchor)
        if event in ("w_down", "w_gate_w_up", "w_out", "w_in"):
            swap(event, arrays)
        return held

    grad_x, _, small = _local_step(
        x[0], loss_target[0], norm1_g, w_in_t, wg, gate_bias, gla_norm_g, attn_norm_g,
        w_out4.reshape(D_MODEL, D_MODEL), norm2_g, w_gate4, w_up4, cw_full, conv_b, w_down4.reshape(D_FF, D_MODEL), final_norm_g,
        on_grad=on_grad)
    update(("w_out", "w_in"), [grad_x])

    d_gf_up = small["wg"][0:GATE_RANK, 0:GLA_K]
    d_gb_up = small["wg"][GATE_RANK:2 * GATE_RANK, GLA_K:]
    pieces = [small["loss"], small["norm1_g"], d_gf_up, small["gate_bias"][:, :GLA_K], d_gb_up, small["gate_bias"][:, GLA_K:],
              small["gla_norm_g"], small["attn_norm_g"], small["norm2_g"], small["conv_w"], small["conv_b"], small["final_norm_g"]]
    total = _allreduce_rows(_pack_rows(pieces))
    summed = _unpack_rows(total, [p.shape for p in pieces])
    loss = summed[0][0, 0]
    g_small = dict(zip(SMALL_NAMES, summed[1:]))
    g_small["gf_up"] = lax.dynamic_slice_in_dim(g_small["gf_up"], chip * n_gk, n_gk, axis=1)
    g_small["gb_up"] = lax.dynamic_slice_in_dim(g_small["gb_up"], chip * n_gk, n_gk, axis=1)
    g_small["conv_w"] = lax.dynamic_slice_in_dim(g_small["conv_w"], chip * n_ff, n_ff, axis=1)
    packed = [_pack_rows([t[k] for k in SMALL_NAMES]) for t in (w, m, v, g_small)]
    res = _adamw("adamw_small", *packed)
    shapes = [w[k].shape for k in SMALL_NAMES]
    for k, vals in zip(SMALL_NAMES, zip(*[_unpack_rows(r, shapes) for r in res])):
        out[k] = list(vals)

    grads, deltas, new_m, new_v = ([out[k][i] for k in WEIGHT_ORDER] for i in range(4))
    return (loss, grad_x[None], *grads, *deltas, *new_m, *new_v)
```
